```python
import math
import jax, jax.numpy as jnp
from jax import lax
import numpy as np

D_MODEL = 1024
BATCH = 8
SEQ = 2048
DEPTH = 2

N_MIXERS = 2
CHUNK = 128
N_MEM = 256
D_INNER = 2 * D_MODEL
A_GROUPS = 8
SSM_HEAD_DIM = 64
SSM_HEADS = D_INNER // SSM_HEAD_DIM
SSM_GROUPS = 4
SSM_HPG = SSM_HEADS // SSM_GROUPS
SSM_STATE = 128
CONV_K = 4
CONV_DIM = D_INNER + 2 * SSM_GROUPS * SSM_STATE
X_HEADS = 4
X_HEAD_DIM = 256
X_WIDTH = X_HEADS * X_HEAD_DIM
MIX_OUT = D_INNER + X_WIDTH
D_FF = 4 * D_MODEL
A_IN = 2 * D_INNER + X_WIDTH
B_IN = D_INNER + CONV_DIM + SSM_HEADS + X_WIDTH
EPS = 1e-6

kernel_name = "hybrid_gmlp_ssd_memxattn"


def rms_norm(x, g):
    xf = x.astype(jnp.float32)
    y = xf * lax.rsqrt(jnp.mean(xf * xf, axis=-1, keepdims=True) + EPS)
    return (y * g.astype(jnp.float32)).astype(x.dtype)


def layer_norm(x, g, b):
    xf = x.astype(jnp.float32)
    mu = jnp.mean(xf, axis=-1, keepdims=True)
    xc = xf - mu
    y = xc * lax.rsqrt(jnp.mean(xc * xc, axis=-1, keepdims=True) + EPS)
    return (y * g.astype(jnp.float32) + b.astype(jnp.float32)).astype(x.dtype)


def gmlp_spatial_gating(u, v, ln_g, ln_b, ws, bs):
    bn, s, _ = u.shape
    v = layer_norm(v, ln_g, ln_b)
    v = v.reshape(bn, s // CHUNK, CHUNK, A_GROUPS, D_INNER // A_GROUPS)
    causal = jnp.tril(jnp.ones((CHUNK, CHUNK), dtype=bool))
    w = jnp.where(causal[None], ws, jnp.zeros_like(ws))
    sv = jnp.einsum('gts,bcsgd->bctgd', w, v) + bs.T[:, :, None]
    return u * sv.reshape(bn, s, D_INNER)


def causal_dwconv(x, w, b):
    y = lax.conv_general_dilated(
        x, w[:, None, :], window_strides=(1,), padding=[(CONV_K - 1, 0)],
        dimension_numbers=('NWC', 'WIO', 'NWC'), feature_group_count=x.shape[-1])
    return y + b


def ssd_mixer(zxbcdt, conv_w, conv_b, dt_bias, a_log, d_skip, gnorm):
    bn, s, _ = zxbcdt.shape
    nc = s // CHUNK
    z = zxbcdt[..., :D_INNER]
    xbc = zxbcdt[..., D_INNER:D_INNER + CONV_DIM]
    dt = zxbcdt[..., D_INNER + CONV_DIM:]
    xbc = jax.nn.silu(causal_dwconv(xbc, conv_w, conv_b))
    gn = SSM_GROUPS * SSM_STATE
    xs = xbc[..., :D_INNER].astype(jnp.float32)
    bm = xbc[..., D_INNER:D_INNER + gn].astype(jnp.float32)
    cm = xbc[..., D_INNER + gn:].astype(jnp.float32)

    x = xs.reshape(bn, nc, CHUNK, SSM_GROUPS, SSM_HPG, SSM_HEAD_DIM)
    bm = bm.reshape(bn, nc, CHUNK, SSM_GROUPS, SSM_STATE)
    cm = cm.reshape(bn, nc, CHUNK, SSM_GROUPS, SSM_STATE)
    dt = jax.nn.softplus(dt.astype(jnp.float32) + dt_bias.astype(jnp.float32))
    dt = dt.reshape(bn, nc, CHUNK, SSM_GROUPS, SSM_HPG)
    a = -jnp.exp(a_log.astype(jnp.float32)).reshape(SSM_GROUPS, SSM_HPG)
    da = jnp.transpose(dt * a, (0, 3, 4, 1, 2))
    xdt = x * dt[..., None]

    cs = jnp.cumsum(da, axis=-1)
    causal = jnp.tril(jnp.ones((CHUNK, CHUNK), dtype=bool))
    seg = cs[..., :, None] - cs[..., None, :]
    lmat = jnp.exp(jnp.where(causal, seg, -jnp.inf))

    cb = jnp.einsum('bclgn,bcsgn->bcgls', cm, bm)
    y_diag = jnp.einsum('bcgls,bgrcls,bcsgrp->bclgrp', cb, lmat, xdt)

    decay_states = jnp.exp(cs[..., -1:] - cs)
    states = jnp.einsum('bclgn,bgrcl,bclgrp->bcgrpn', bm, decay_states, xdt)
    chunk_decay = jnp.exp(cs[..., -1])

    def step(h, inp):
        st, dec = inp
        return h * dec[..., None, None] + st, h

    h0 = jnp.zeros((bn, SSM_GROUPS, SSM_HPG, SSM_HEAD_DIM, SSM_STATE), jnp.float32)
    _, prev = lax.scan(step, h0, (jnp.moveaxis(states, 1, 0), jnp.moveaxis(chunk_decay, 3, 0)))
    prev = jnp.moveaxis(prev, 0, 1)

    y_off = jnp.einsum('bclgn,bcgrpn,bgrcl->bclgrp', cm, prev, jnp.exp(cs))
    y = y_diag + y_off + x * d_skip.astype(jnp.float32).reshape(SSM_GROUPS, SSM_HPG)[..., None]
    y = y.reshape(bn, s, D_INNER)

    yg = (y * jax.nn.silu(z.astype(jnp.float32))).reshape(bn, s, SSM_GROUPS, D_INNER // SSM_GROUPS)
    yg = yg * lax.rsqrt(jnp.mean(yg * yg, axis=-1, keepdims=True) + EPS)
    y = yg.reshape(bn, s, D_INNER) * gnorm.astype(jnp.float32)
    return y.astype(zxbcdt.dtype)


def memory_attention(q, mem, mem_g, w_kv):
    bn, s, _ = q.shape
    m = rms_norm(mem, mem_g)
    kv = m @ w_kv
    k = kv[..., :X_WIDTH].reshape(bn, N_MEM, X_HEADS, X_HEAD_DIM)
    v = kv[..., X_WIDTH:].reshape(bn, N_MEM, X_HEADS, X_HEAD_DIM)
    qh = q.reshape(bn, s, X_HEADS, X_HEAD_DIM)
    sc = jnp.einsum('bshd,bmhd->bhsm', qh, k).astype(jnp.float32) * (1.0 / math.sqrt(X_HEAD_DIM))
    p = jax.nn.softmax(sc, axis=-1).astype(v.dtype)
    o = jnp.einsum('bhsm,bmhd->bshd', p, v)
    return o.reshape(bn, s, X_WIDTH)


def _fwd_setup_inputs(seed: int = 0) -> dict:
    key = jax.random.key(seed)
    ks = jax.random.split(key, 32)
    na = (DEPTH + 1) // 2
    nb = DEPTH // 2
    f32 = jnp.float32

    def nrm(k, shape, scale):
        return jax.random.normal(k, shape, f32) * scale

    def gain(k, shape):
        return 1.0 + 0.02 * jax.random.normal(k, shape, f32)

    dt0 = jnp.exp(jax.random.uniform(ks[20], (nb, SSM_HEADS), f32, math.log(1e-3), math.log(1e-1)))
    return {
        "x": jax.random.normal(ks[0], (BATCH, SEQ, D_MODEL), f32),
        "mem": jax.random.normal(ks[1], (BATCH, N_MEM, D_MODEL), f32),
        "norm_mix": gain(ks[2], (DEPTH, D_MODEL)),
        "norm_ffn": gain(ks[3], (DEPTH, D_MODEL)),
        "mem_norm": gain(ks[4], (DEPTH, D_MODEL)),
        "w_kv": nrm(ks[5], (DEPTH, D_MODEL, 2 * X_WIDTH), D_MODEL ** -0.5),
        "w_out": nrm(ks[6], (DEPTH, MIX_OUT, D_MODEL), MIX_OUT ** -0.5),
        "w_ffn1": nrm(ks[7], (DEPTH, D_MODEL, D_FF), D_MODEL ** -0.5),
        "w_ffn2": nrm(ks[8], (DEPTH, D_FF, D_MODEL), D_FF ** -0.5),
        "a_in": nrm(ks[9], (na, D_MODEL, A_IN), D_MODEL ** -0.5),
        "a_ln_g": gain(ks[10], (na, D_INNER)),
        "a_ln_b": nrm(ks[11], (na, D_INNER), 0.02),
        "a_ws": nrm(ks[12], (na, A_GROUPS, CHUNK, CHUNK), 0.5 * CHUNK ** -0.5),
        "a_bs": gain(ks[13], (na, A_GROUPS, CHUNK)),
        "b_in": nrm(ks[14], (nb, D_MODEL, B_IN), D_MODEL ** -0.5),
        "b_conv_w": nrm(ks[15], (nb, CONV_K, CONV_DIM), CONV_K ** -0.5),
        "b_conv_b": nrm(ks[16], (nb, CONV_DIM), 0.02),
        "b_dt_bias": dt0 + jnp.log(-jnp.expm1(-dt0)),
        "b_a_log": jnp.log(jax.random.uniform(ks[17], (nb, SSM_HEADS), f32, 1.0, 16.0)),
        "b_d": gain(ks[18], (nb, SSM_HEADS)),
        "b_gnorm": gain(ks[19], (nb, D_INNER)),
        "final_norm": gain(ks[21], (D_MODEL,)),
    }


def _fwd_reference(x, mem, norm_mix, norm_ffn, mem_norm, w_kv, w_out, w_ffn1, w_ffn2,
              a_in, a_ln_g, a_ln_b, a_ws, a_bs,
              b_in, b_conv_w, b_conv_b, b_dt_bias, b_a_log, b_d, b_gnorm,
              final_norm):
    h = x
    for i in range(DEPTH):
        j = i // N_MIXERS
        a = rms_norm(h, norm_mix[i])
        if i % N_MIXERS == 0:
            proj = a @ a_in[j]
            u = jax.nn.gelu(proj[..., :D_INNER], approximate=False)
            v = jax.nn.gelu(proj[..., D_INNER:2 * D_INNER], approximate=False)
            q = proj[..., 2 * D_INNER:]
            mix = gmlp_spatial_gating(u, v, a_ln_g[j], a_ln_b[j], a_ws[j], a_bs[j])
        else:
            proj = a @ b_in[j]
            ssm_cols = D_INNER + CONV_DIM + SSM_HEADS
            mix = ssd_mixer(proj[..., :ssm_cols], b_conv_w[j], b_conv_b[j], b_dt_bias[j],
                            b_a_log[j], b_d[j], b_gnorm[j])
            q = proj[..., ssm_cols:]
        mo = memory_attention(q, mem, mem_norm[i], w_kv[i])
        h = h + jnp.concatenate([mix, mo], axis=-1) @ w_out[i]
        f = rms_norm(h, norm_ffn[i])
        h = h + jnp.square(jax.nn.relu(f @ w_ffn1[i])) @ w_ffn2[i]
    return rms_norm(h, final_norm)


import jax as _jax
import jax.numpy as _jnp

TWIN_FORMAT = 'train_step'
FWD_PARAMS = ['x', 'mem', 'norm_mix', 'norm_ffn', 'mem_norm', 'w_kv', 'w_out', 'w_ffn1', 'w_ffn2', 'a_in', 'a_ln_g', 'a_ln_b', 'a_ws', 'a_bs', 'b_in', 'b_conv_w', 'b_conv_b', 'b_dt_bias', 'b_a_log', 'b_d', 'b_gnorm', 'final_norm']
TWIN_WEIGHTS = ['norm_mix', 'norm_ffn', 'mem_norm', 'w_kv', 'w_out', 'w_ffn1', 'w_ffn2', 'a_in', 'a_ln_g', 'a_ln_b', 'a_ws', 'a_bs', 'b_in', 'b_conv_w', 'b_conv_b', 'b_dt_bias', 'b_a_log', 'b_d', 'b_gnorm', 'final_norm']
TWIN_DIFF_INPUT = 'x'
TWIN_INPUTS = ['x', 'mem', 'norm_mix', 'norm_ffn', 'mem_norm', 'w_kv', 'w_out', 'w_ffn1', 'w_ffn2', 'a_in', 'a_ln_g', 'a_ln_b', 'a_ws', 'a_bs', 'b_in', 'b_conv_w', 'b_conv_b', 'b_dt_bias', 'b_a_log', 'b_d', 'b_gnorm', 'final_norm', 'loss_target', 'm_norm_mix', 'm_norm_ffn', 'm_mem_norm', 'm_w_kv', 'm_w_out', 'm_w_ffn1', 'm_w_ffn2', 'm_a_in', 'm_a_ln_g', 'm_a_ln_b', 'm_a_ws', 'm_a_bs', 'm_b_in', 'm_b_conv_w', 'm_b_conv_b', 'm_b_dt_bias', 'm_b_a_log', 'm_b_d', 'm_b_gnorm', 'm_final_norm', 'v_norm_mix', 'v_norm_ffn', 'v_mem_norm', 'v_w_kv', 'v_w_out', 'v_w_ffn1', 'v_w_ffn2', 'v_a_in', 'v_a_ln_g', 'v_a_ln_b', 'v_a_ws', 'v_a_bs', 'v_b_in', 'v_b_conv_w', 'v_b_conv_b', 'v_b_dt_bias', 'v_b_a_log', 'v_b_d', 'v_b_gnorm', 'v_final_norm']
TWIN_OUTPUTS = ['loss', 'grad_x', 'grad_norm_mix', 'grad_norm_ffn', 'grad_mem_norm', 'grad_w_kv', 'grad_w_out', 'grad_w_ffn1', 'grad_w_ffn2', 'grad_a_in', 'grad_a_ln_g', 'grad_a_ln_b', 'grad_a_ws', 'grad_a_bs', 'grad_b_in', 'grad_b_conv_w', 'grad_b_conv_b', 'grad_b_dt_bias', 'grad_b_a_log', 'grad_b_d', 'grad_b_gnorm', 'grad_final_norm', 'delta_norm_mix', 'delta_norm_ffn', 'delta_mem_norm', 'delta_w_kv', 'delta_w_out', 'delta_w_ffn1', 'delta_w_ffn2', 'delta_a_in', 'delta_a_ln_g', 'delta_a_ln_b', 'delta_a_ws', 'delta_a_bs', 'delta_b_in', 'delta_b_conv_w', 'delta_b_conv_b', 'delta_b_dt_bias', 'delta_b_a_log', 'delta_b_d', 'delta_b_gnorm', 'delta_final_norm', 'new_m_norm_mix', 'new_m_norm_ffn', 'new_m_mem_norm', 'new_m_w_kv', 'new_m_w_out', 'new_m_w_ffn1', 'new_m_w_ffn2', 'new_m_a_in', 'new_m_a_ln_g', 'new_m_a_ln_b', 'new_m_a_ws', 'new_m_a_bs', 'new_m_b_in', 'new_m_b_conv_w', 'new_m_b_conv_b', 'new_m_b_dt_bias', 'new_m_b_a_log', 'new_m_b_d', 'new_m_b_gnorm', 'new_m_final_norm', 'new_v_norm_mix', 'new_v_norm_ffn', 'new_v_mem_norm', 'new_v_w_kv', 'new_v_w_out', 'new_v_w_ffn1', 'new_v_w_ffn2', 'new_v_a_in', 'new_v_a_ln_g', 'new_v_a_ln_b', 'new_v_a_ws', 'new_v_a_bs', 'new_v_b_in', 'new_v_b_conv_w', 'new_v_b_conv_b', 'new_v_b_dt_bias', 'new_v_b_a_log', 'new_v_b_d', 'new_v_b_gnorm', 'new_v_final_norm']
TWIN_LEAF_KINDS = {'loss': 'loss', 'grad_x': 'grad_x', 'grad_norm_mix': 'grad_w', 'grad_norm_ffn': 'grad_w', 'grad_mem_norm': 'grad_w', 'grad_w_kv': 'grad_w', 'grad_w_out': 'grad_w', 'grad_w_ffn1': 'grad_w', 'grad_w_ffn2': 'grad_w', 'grad_a_in': 'grad_w', 'grad_a_ln_g': 'grad_w', 'grad_a_ln_b': 'grad_w', 'grad_a_ws': 'grad_w', 'grad_a_bs': 'grad_w', 'grad_b_in': 'grad_w', 'grad_b_conv_w': 'grad_w', 'grad_b_conv_b': 'grad_w', 'grad_b_dt_bias': 'grad_w', 'grad_b_a_log': 'grad_w', 'grad_b_d': 'grad_w', 'grad_b_gnorm': 'grad_w', 'grad_final_norm': 'grad_w', 'delta_norm_mix': 'delta_w', 'delta_norm_ffn': 'delta_w', 'delta_mem_norm': 'delta_w', 'delta_w_kv': 'delta_w', 'delta_w_out': 'delta_w', 'delta_w_ffn1': 'delta_w', 'delta_w_ffn2': 'delta_w', 'delta_a_in': 'delta_w', 'delta_a_ln_g': 'delta_w', 'delta_a_ln_b': 'delta_w', 'delta_a_ws': 'delta_w', 'delta_a_bs': 'delta_w', 'delta_b_in': 'delta_w', 'delta_b_conv_w': 'delta_w', 'delta_b_conv_b': 'delta_w', 'delta_b_dt_bias': 'delta_w', 'delta_b_a_log': 'delta_w', 'delta_b_d': 'delta_w', 'delta_b_gnorm': 'delta_w', 'delta_final_norm': 'delta_w', 'new_m_norm_mix': 'new_m', 'new_m_norm_ffn': 'new_m', 'new_m_mem_norm': 'new_m', 'new_m_w_kv': 'new_m', 'new_m_w_out': 'new_m', 'new_m_w_ffn1': 'new_m', 'new_m_w_ffn2': 'new_m', 'new_m_a_in': 'new_m', 'new_m_a_ln_g': 'new_m', 'new_m_a_ln_b': 'new_m', 'new_m_a_ws': 'new_m', 'new_m_a_bs': 'new_m', 'new_m_b_in': 'new_m', 'new_m_b_conv_w': 'new_m', 'new_m_b_conv_b': 'new_m', 'new_m_b_dt_bias': 'new_m', 'new_m_b_a_log': 'new_m', 'new_m_b_d': 'new_m', 'new_m_b_gnorm': 'new_m', 'new_m_final_norm': 'new_m', 'new_v_norm_mix': 'new_v', 'new_v_norm_ffn': 'new_v', 'new_v_mem_norm': 'new_v', 'new_v_w_kv': 'new_v', 'new_v_w_out': 'new_v', 'new_v_w_ffn1': 'new_v', 'new_v_w_ffn2': 'new_v', 'new_v_a_in': 'new_v', 'new_v_a_ln_g': 'new_v', 'new_v_a_ln_b': 'new_v', 'new_v_a_ws': 'new_v', 'new_v_a_bs': 'new_v', 'new_v_b_in': 'new_v', 'new_v_b_conv_w': 'new_v', 'new_v_b_conv_b': 'new_v', 'new_v_b_dt_bias': 'new_v', 'new_v_b_a_log': 'new_v', 'new_v_b_d': 'new_v', 'new_v_b_gnorm': 'new_v', 'new_v_final_norm': 'new_v'}


def _forward(args):
    return _fwd_reference(*[args[k] for k in FWD_PARAMS])


def _output_shape():
    out = _jax.eval_shape(lambda: _forward(_fwd_setup_inputs(0)))
    return out.shape, out.dtype

N_MICROBATCH = 1
ADAM_LR = 0.001
ADAM_B1 = 0.9
ADAM_B2 = 0.999
ADAM_EPS = 1e-08
ADAM_WD = 0.01
ADAM_STEP = 10
PER_EXAMPLE_BATCH_AXIS = {'x': 0, 'mem': 0, 'loss_target': 0}
SHARED_INPUTS = []
_WEIGHT_DTYPES = {'norm_mix': _jnp.float32, 'norm_ffn': _jnp.float32, 'mem_norm': _jnp.float32, 'w_kv': _jnp.float32, 'w_out': _jnp.float32, 'w_ffn1': _jnp.float32, 'w_ffn2': _jnp.float32, 'a_in': _jnp.float32, 'a_ln_g': _jnp.float32, 'a_ln_b': _jnp.float32, 'a_ws': _jnp.float32, 'a_bs': _jnp.float32, 'b_in': _jnp.float32, 'b_conv_w': _jnp.float32, 'b_conv_b': _jnp.float32, 'b_dt_bias': _jnp.float32, 'b_a_log': _jnp.float32, 'b_d': _jnp.float32, 'b_gnorm': _jnp.float32, 'final_norm': _jnp.float32}
MOMENT_SCALE = {'norm_mix': 8.646236e-02, 'norm_ffn': 1.079170e-01, 'mem_norm': 9.964151e-03, 'w_kv': 6.804433e-03, 'w_out': 7.238538e-02, 'w_ffn1': 5.228283e-02, 'w_ffn2': 1.000377e-01, 'a_in': 3.862253e-02, 'a_ln_g': 1.910331e-02, 'a_ln_b': 1.896489e-02, 'a_ws': 5.188797e-02, 'a_bs': 7.183682e-02, 'b_in': 3.547144e-02, 'b_conv_w': 3.584637e-02, 'b_conv_b': 4.765793e-02, 'b_dt_bias': 8.043986e-02, 'b_a_log': 1.304628e-01, 'b_d': 2.079064e-01, 'b_gnorm': 4.080385e-02, 'final_norm': 1.631370e+01}


def _to_microbatches(a, axis):
    t = _jnp.moveaxis(a, axis, 0)
    t = t.reshape((N_MICROBATCH, t.shape[0] // N_MICROBATCH) + t.shape[1:])
    return _jnp.moveaxis(t, 1, axis + 1)


def setup_inputs(seed: int = 0) -> dict:
    inp = _fwd_setup_inputs(seed)
    key = _jax.random.fold_in(_jax.random.key(seed), 7919)
    shape, _ = _output_shape()
    out = dict(inp)
    out["loss_target"] = _jax.random.normal(_jax.random.fold_in(key, 0), shape, _jnp.float32)
    for i, name in enumerate(TWIN_WEIGHTS):
        w = inp[name].astype(_jnp.float32)
        if MOMENT_SCALE is None:
            s = _jnp.sqrt(_jnp.mean(_jnp.square(w)) + 1e-30)
        else:
            s = MOMENT_SCALE[name]
        km, kv = _jax.random.split(_jax.random.fold_in(key, i + 1))
        out[name] = w
        out["m_" + name] = s * _jax.random.normal(km, w.shape, _jnp.float32)
        out["v_" + name] = (s * s) * _jax.random.uniform(kv, w.shape, _jnp.float32, 0.5, 1.5)
    if N_MICROBATCH > 1:
        for name, axis in PER_EXAMPLE_BATCH_AXIS.items():
            out[name] = _to_microbatches(out[name], axis)
    return {'x': out['x'], 'mem': out['mem'], 'norm_mix': out['norm_mix'], 'norm_ffn': out['norm_ffn'], 'mem_norm': out['mem_norm'], 'w_kv': out['w_kv'], 'w_out': out['w_out'], 'w_ffn1': out['w_ffn1'], 'w_ffn2': out['w_ffn2'], 'a_in': out['a_in'], 'a_ln_g': out['a_ln_g'], 'a_ln_b': out['a_ln_b'], 'a_ws': out['a_ws'], 'a_bs': out['a_bs'], 'b_in': out['b_in'], 'b_conv_w': out['b_conv_w'], 'b_conv_b': out['b_conv_b'], 'b_dt_bias': out['b_dt_bias'], 'b_a_log': out['b_a_log'], 'b_d': out['b_d'], 'b_gnorm': out['b_gnorm'], 'final_norm': out['final_norm'], 'loss_target': out['loss_target'], 'm_norm_mix': out['m_norm_mix'], 'm_norm_ffn': out['m_norm_ffn'], 'm_mem_norm': out['m_mem_norm'], 'm_w_kv': out['m_w_kv'], 'm_w_out': out['m_w_out'], 'm_w_ffn1': out['m_w_ffn1'], 'm_w_ffn2': out['m_w_ffn2'], 'm_a_in': out['m_a_in'], 'm_a_ln_g': out['m_a_ln_g'], 'm_a_ln_b': out['m_a_ln_b'], 'm_a_ws': out['m_a_ws'], 'm_a_bs': out['m_a_bs'], 'm_b_in': out['m_b_in'], 'm_b_conv_w': out['m_b_conv_w'], 'm_b_conv_b': out['m_b_conv_b'], 'm_b_dt_bias': out['m_b_dt_bias'], 'm_b_a_log': out['m_b_a_log'], 'm_b_d': out['m_b_d'], 'm_b_gnorm': out['m_b_gnorm'], 'm_final_norm': out['m_final_norm'], 'v_norm_mix': out['v_norm_mix'], 'v_norm_ffn': out['v_norm_ffn'], 'v_mem_norm': out['v_mem_norm'], 'v_w_kv': out['v_w_kv'], 'v_w_out': out['v_w_out'], 'v_w_ffn1': out['v_w_ffn1'], 'v_w_ffn2': out['v_w_ffn2'], 'v_a_in': out['v_a_in'], 'v_a_ln_g': out['v_a_ln_g'], 'v_a_ln_b': out['v_a_ln_b'], 'v_a_ws': out['v_a_ws'], 'v_a_bs': out['v_a_bs'], 'v_b_in': out['v_b_in'], 'v_b_conv_w': out['v_b_conv_w'], 'v_b_conv_b': out['v_b_conv_b'], 'v_b_dt_bias': out['v_b_dt_bias'], 'v_b_a_log': out['v_b_a_log'], 'v_b_d': out['v_b_d'], 'v_b_gnorm': out['v_b_gnorm'], 'v_final_norm': out['v_final_norm']}


def _loss(weights, diff, rest, loss_target):
    with _jax.named_scope("forward"):
        args = {**rest, TWIN_DIFF_INPUT: diff, **{k: w.astype(_WEIGHT_DTYPES[k]) for k, w in weights.items()}}
        y = _forward(args)
    with _jax.named_scope("loss_head"):
        err = _jnp.square(y.astype(_jnp.float32) - loss_target)
        return 0.5 * _jnp.sum(_jnp.mean(err, axis=-1)) if err.ndim else 0.5 * err


def _adamw(w, g, m, v):
    m = ADAM_B1 * m + (1.0 - ADAM_B1) * g
    v = ADAM_B2 * v + (1.0 - ADAM_B2) * _jnp.square(g)
    m_hat = m / (1.0 - ADAM_B1 ** ADAM_STEP)
    v_hat = v / (1.0 - ADAM_B2 ** ADAM_STEP)
    delta = -ADAM_LR * (m_hat / (_jnp.sqrt(v_hat) + ADAM_EPS) + ADAM_WD * w)
    return delta, m, v


def reference(x, mem, norm_mix, norm_ffn, mem_norm, w_kv, w_out, w_ffn1, w_ffn2, a_in, a_ln_g, a_ln_b, a_ws, a_bs, b_in, b_conv_w, b_conv_b, b_dt_bias, b_a_log, b_d, b_gnorm, final_norm, loss_target, m_norm_mix, m_norm_ffn, m_mem_norm, m_w_kv, m_w_out, m_w_ffn1, m_w_ffn2, m_a_in, m_a_ln_g, m_a_ln_b, m_a_ws, m_a_bs, m_b_in, m_b_conv_w, m_b_conv_b, m_b_dt_bias, m_b_a_log, m_b_d, m_b_gnorm, m_final_norm, v_norm_mix, v_norm_ffn, v_mem_norm, v_w_kv, v_w_out, v_w_ffn1, v_w_ffn2, v_a_in, v_a_ln_g, v_a_ln_b, v_a_ws, v_a_bs, v_b_in, v_b_conv_w, v_b_conv_b, v_b_dt_bias, v_b_a_log, v_b_d, v_b_gnorm, v_final_norm):
    given = dict(x=x, mem=mem, norm_mix=norm_mix, norm_ffn=norm_ffn, mem_norm=mem_norm, w_kv=w_kv, w_out=w_out, w_ffn1=w_ffn1, w_ffn2=w_ffn2, a_in=a_in, a_ln_g=a_ln_g, a_ln_b=a_ln_b, a_ws=a_ws, a_bs=a_bs, b_in=b_in, b_conv_w=b_conv_w, b_conv_b=b_conv_b, b_dt_bias=b_dt_bias, b_a_log=b_a_log, b_d=b_d, b_gnorm=b_gnorm, final_norm=final_norm, loss_target=loss_target, m_norm_mix=m_norm_mix, m_norm_ffn=m_norm_ffn, m_mem_norm=m_mem_norm, m_w_kv=m_w_kv, m_w_out=m_w_out, m_w_ffn1=m_w_ffn1, m_w_ffn2=m_w_ffn2, m_a_in=m_a_in, m_a_ln_g=m_a_ln_g, m_a_ln_b=m_a_ln_b, m_a_ws=m_a_ws, m_a_bs=m_a_bs, m_b_in=m_b_in, m_b_conv_w=m_b_conv_w, m_b_conv_b=m_b_conv_b, m_b_dt_bias=m_b_dt_bias, m_b_a_log=m_b_a_log, m_b_d=m_b_d, m_b_gnorm=m_b_gnorm, m_final_norm=m_final_norm, v_norm_mix=v_norm_mix, v_norm_ffn=v_norm_ffn, v_mem_norm=v_mem_norm, v_w_kv=v_w_kv, v_w_out=v_w_out, v_w_ffn1=v_w_ffn1, v_w_ffn2=v_w_ffn2, v_a_in=v_a_in, v_a_ln_g=v_a_ln_g, v_a_ln_b=v_a_ln_b, v_a_ws=v_a_ws, v_a_bs=v_a_bs, v_b_in=v_b_in, v_b_conv_w=v_b_conv_w, v_b_conv_b=v_b_conv_b, v_b_dt_bias=v_b_dt_bias, v_b_a_log=v_b_a_log, v_b_d=v_b_d, v_b_gnorm=v_b_gnorm, v_final_norm=v_final_norm)
    weights = {n: given[n] for n in TWIN_WEIGHTS}
    shared = {n: given[n] for n in SHARED_INPUTS}
    per_example = {n: given[n] for n in ['x', 'mem']}
    grad_fn = _jax.value_and_grad(_loss, argnums=(0, 1))

    def one_microbatch(ex, loss_target):
        ex = dict(ex)
        diff = ex.pop(TWIN_DIFF_INPUT)
        return grad_fn(weights, diff, {**shared, **ex}, loss_target)

    if N_MICROBATCH == 1:
        loss, (grad_w, grad_x) = one_microbatch(per_example, given["loss_target"])
    else:
        def body(carry, xs):
            loss_sum, grad_sum = carry
            l_k, (gw_k, gx_k) = one_microbatch(xs[0], xs[1])
            with _jax.named_scope("update"):
                return (loss_sum + l_k, _jax.tree.map(_jnp.add, grad_sum, gw_k)), gx_k

        init = (_jnp.zeros((), _jnp.float32), _jax.tree.map(_jnp.zeros_like, weights))
        (loss, grad_w), grad_x = _jax.lax.scan(body, init, (per_example, given["loss_target"]))
    with _jax.named_scope("update"):
        delta_w, new_m, new_v = {}, {}, {}
        for n in TWIN_WEIGHTS:
            delta_w[n], new_m[n], new_v[n] = _adamw(weights[n], grad_w[n], given["m_" + n], given["v_" + n])
    return (loss, grad_x, *[grad_w[n] for n in TWIN_WEIGHTS], *[delta_w[n] for n in TWIN_WEIGHTS],
            *[new_m[n] for n in TWIN_WEIGHTS], *[new_v[n] for n in TWIN_WEIGHTS])
```

```python
import math

import jax
import jax.numpy as jnp
from jax import lax
from jax.experimental import pallas as pl
from jax.experimental.pallas import tpu as pltpu

F32 = jnp.float32
BF16 = jnp.bfloat16
SDS = jax.ShapeDtypeStruct

D_MODEL = 1024
SEQ = 2048
CHUNK = 128
N_MEM = 256
D_INNER = 2048
A_GROUPS = 8
A_GROUP_W = D_INNER // A_GROUPS
SSM_HEADS = 32
SSM_HEAD_DIM = 64
SSM_GROUPS = 4
SSM_HPG = 8
SSM_STATE = 128
SSM_GROUP_W = SSM_HPG * SSM_HEAD_DIM
CONV_K = 4
CONV_DIM = 3072
X_HEADS = 4
X_HEAD_DIM = 256
X_WIDTH = 1024
MIX_OUT = 3072
D_FF = 4096
A_IN = 5120
B_IN = 6176
B_IN_PAD = 6272
B_Q_OFF = 5120
B_DT_OFF = 6144
N_CHUNKS = SEQ // CHUNK
EPS = 1e-6
N_CHIPS = 4

ADAM_LR = 0.001
ADAM_B1 = 0.9
ADAM_B2 = 0.999
ADAM_EPS = 1e-08
ADAM_WD = 0.01
ADAM_STEP = 10

VMEM_LIMIT = 48 * 1024 * 1024
MESH = pl.DeviceIdType.MESH


def _cparams(sem):
    return pltpu.CompilerParams(dimension_semantics=sem, vmem_limit_bytes=VMEM_LIMIT)


def _dot(a, b, dims=(((1,), (0,)), ((), ()))):
    return lax.dot_general(a.astype(BF16), b.astype(BF16), dims, preferred_element_type=F32)


def _dot_nt(a, b):
    return _dot(a, b, (((1,), (1,)), ((), ())))


def _dot_tn(a, b):
    return _dot(a, b, (((0,), (0,)), ((), ())))


def _pick(n, cands):
    for c in cands:
        if n % c == 0:
            return c
    raise ValueError(f"no tile for {n}")


def _mm_call(a, b, *, dims, grid, a_spec, b_spec, acc_shape, out_shapes, out_specs, name,
             extras=(), extra_specs=(), epilogue=None):
    n_k = grid[2]
    n_extra = len(extras)
    n_out = len(out_shapes)

    def body(*refs):
        a_ref, b_ref = refs[0], refs[1]
        extra_refs = refs[2:2 + n_extra]
        out_refs = refs[2 + n_extra:2 + n_extra + n_out]
        acc = refs[-1]
        k = pl.program_id(2)

        @pl.when(k == 0)
        def _():
            acc[...] = jnp.zeros_like(acc)

        acc[...] += _dot(a_ref[...], b_ref[...], dims)

        @pl.when(k == n_k - 1)
        def _():
            vals = (acc[...],) if epilogue is None else epilogue(acc[...], *[e[...] for e in extra_refs])
            for o_ref, v in zip(out_refs, vals):
                o_ref[...] = v.astype(o_ref.dtype)

    return pl.pallas_call(
        body, grid=grid, in_specs=[a_spec, b_spec, *extra_specs], out_specs=list(out_specs),
        out_shape=list(out_shapes), scratch_shapes=[pltpu.VMEM(acc_shape, F32)],
        compiler_params=_cparams(("parallel", "parallel", "arbitrary")), name=name,
    )(a, b, *extras)


def _w_dims(w):
    if w.ndim == 2:
        return w.shape[0], w.shape[1], 1, w.shape[1]
    return w.shape[1], w.shape[0] * w.shape[2], w.shape[0], w.shape[2]


def _mm_nn(a, w, *, name, out_dtype=F32, a_cols=None, extras=(), epilogue=None, n_out_dtypes=None):
    m = a.shape[0]
    k_dim, n_dim, _, n_slot = _w_dims(w)
    a_off, a_w = (0, a.shape[1]) if a_cols is None else a_cols
    assert a_w == k_dim
    tm = _pick(m, (1024, 512, 256))
    tn = _pick(n_slot, (512, 896, 640, 256, 128))
    tk = _pick(k_dim, (512, 384, 256, 128))
    assert a_off % tk == 0
    nb = n_slot // tn
    a_spec = pl.BlockSpec((tm, tk), lambda i, j, k: (i, a_off // tk + k))
    if w.ndim == 2:
        b_spec = pl.BlockSpec((tk, tn), lambda i, j, k: (k, j))
    else:
        b_spec = pl.BlockSpec((None, tk, tn), lambda i, j, k: (j // nb, k, j % nb))
    o_spec = pl.BlockSpec((tm, tn), lambda i, j, k: (i, j))
    dts = n_out_dtypes or (out_dtype,)
    outs = _mm_call(a, w, dims=(((1,), (0,)), ((), ())), grid=(m // tm, n_dim // tn, k_dim // tk),
                    a_spec=a_spec, b_spec=b_spec, acc_shape=(tm, tn),
                    out_shapes=[SDS((m, n_dim), dt) for dt in dts], out_specs=[o_spec] * len(dts), name=name,
                    extras=extras, extra_specs=[o_spec] * len(extras), epilogue=epilogue)
    return outs if n_out_dtypes else outs[0]


def _mm_nt(a, w, *, name, out_dtype=F32, extras=(), epilogue=None):
    m = a.shape[0]
    k_dim, n_dim, _, n_slot = _w_dims(w)
    assert a.shape[1] == n_dim
    tm = _pick(m, (1024, 512, 256))
    to = _pick(k_dim, (512, 384, 256, 128))
    tc = _pick(n_slot, (512, 896, 640, 256, 128))
    nb = n_slot // tc
    a_spec = pl.BlockSpec((tm, tc), lambda i, j, k: (i, k))
    if w.ndim == 2:
        b_spec = pl.BlockSpec((to, tc), lambda i, j, k: (j, k))
    else:
        b_spec = pl.BlockSpec((None, to, tc), lambda i, j, k: (k // nb, j, k % nb))
    o_spec = pl.BlockSpec((tm, to), lambda i, j, k: (i, j))
    return _mm_call(a, w, dims=(((1,), (1,)), ((), ())), grid=(m // tm, k_dim // to, n_dim // tc),
                    a_spec=a_spec, b_spec=b_spec, acc_shape=(tm, to),
                    out_shapes=[SDS((m, k_dim), out_dtype)], out_specs=[o_spec], name=name,
                    extras=extras, extra_specs=[o_spec] * len(extras), epilogue=epilogue)[0]


def _mm_tn(x, dy, *, name, slots=1, x_cols=None):
    s = x.shape[0]
    x_off, k_dim = (0, x.shape[1]) if x_cols is None else x_cols
    n_dim = dy.shape[1]
    n_slot = n_dim // slots
    tm = _pick(k_dim, (512, 384, 256, 128))
    tn = _pick(n_slot, (512, 896, 640, 256, 128))
    tk = _pick(s, (1024, 512, 256))
    assert x_off % tm == 0
    nb = n_slot // tn
    a_spec = pl.BlockSpec((tk, tm), lambda i, j, k: (k, x_off // tm + i))
    b_spec = pl.BlockSpec((tk, tn), lambda i, j, k: (k, j))
    if slots == 1:
        o_shape, o_spec = SDS((k_dim, n_dim), F32), pl.BlockSpec((tm, tn), lambda i, j, k: (i, j))
    else:
        o_shape = SDS((slots, k_dim, n_slot), F32)
        o_spec = pl.BlockSpec((None, tm, tn), lambda i, j, k: (j // nb, i, j % nb))
    return _mm_call(x, dy, dims=(((0,), (0,)), ((), ())), grid=(k_dim // tm, n_dim // tn, s // tk),
                    a_spec=a_spec, b_spec=b_spec, acc_shape=(tm, tn),
                    out_shapes=[o_shape], out_specs=[o_spec], name=name)[0]


def _rms(x, g):
    return x * lax.rsqrt(jnp.mean(x * x, axis=-1, keepdims=True) + EPS) * g


def _rms_fwd(h, g, *, name):
    rows, d = h.shape
    tr = _pick(rows, (512, 256))

    def body(h_ref, g_ref, o_ref):
        o_ref[...] = _rms(h_ref[...], g_ref[...]).astype(o_ref.dtype)

    return pl.pallas_call(
        body, grid=(rows // tr,),
        in_specs=[pl.BlockSpec((tr, d), lambda i: (i, 0)), pl.BlockSpec((1, d), lambda i: (0, 0))],
        out_specs=pl.BlockSpec((tr, d), lambda i: (i, 0)), out_shape=SDS((rows, d), BF16),
        compiler_params=_cparams(("parallel",)), name=name)(h, g)


def _rms_bwd(h, g, da, dres, *, name):
    rows, d = h.shape
    tr = _pick(rows, (512, 256))

    def body(h_ref, g_ref, da_ref, dres_ref, dh_ref, dg_ref):
        _, vjp = jax.vjp(_rms, h_ref[...], g_ref[...])
        dh, dg = vjp(da_ref[...].astype(F32))
        dh_ref[...] = dres_ref[...] + dh

        @pl.when(pl.program_id(0) == 0)
        def _():
            dg_ref[...] = jnp.zeros_like(dg_ref)

        dg_ref[...] += dg

    row_spec = pl.BlockSpec((tr, d), lambda i: (i, 0))
    vec_spec = pl.BlockSpec((1, d), lambda i: (0, 0))
    return pl.pallas_call(
        body, grid=(rows // tr,), in_specs=[row_spec, vec_spec, row_spec, row_spec],
        out_specs=[row_spec, vec_spec], out_shape=[SDS((rows, d), F32), SDS((1, d), F32)],
        compiler_params=_cparams(("arbitrary",)), name=name)(h, g, da, dres)


def _loss_head(h, g, target, *, name):
    rows, d = h.shape
    tr = _pick(rows, (512, 256))

    def body(h_ref, g_ref, t_ref, loss_ref, dh_ref, dg_ref):
        y, vjp = jax.vjp(_rms, h_ref[...], g_ref[...])
        err = y - t_ref[...]
        dh, dg = vjp(err * (1.0 / d))
        dh_ref[...] = dh

        @pl.when(pl.program_id(0) == 0)
        def _():
            dg_ref[...] = jnp.zeros_like(dg_ref)
            loss_ref[...] = jnp.zeros_like(loss_ref)

        dg_ref[...] += dg
        part = jnp.sum(jnp.sum(err * err, axis=-1, keepdims=True), axis=0, keepdims=True) * (0.5 / d)
        loss_ref[...] += jnp.broadcast_to(part, loss_ref.shape)

    row_spec = pl.BlockSpec((tr, d), lambda i: (i, 0))
    vec_spec = pl.BlockSpec((1, d), lambda i: (0, 0))
    loss_spec = pl.BlockSpec((8, 128), lambda i: (0, 0))
    return pl.pallas_call(
        body, grid=(rows // tr,), in_specs=[row_spec, vec_spec, row_spec],
        out_specs=[loss_spec, row_spec, vec_spec],
        out_shape=[SDS((8, 128), F32), SDS((rows, d), F32), SDS((1, d), F32)],
        compiler_params=_cparams(("arbitrary",)), name=name)(h, g, target)


def _gelu(x):
    return 0.5 * x * (1.0 + lax.erf(x * (1.0 / math.sqrt(2.0))))


def _gate_tile(pu, pv, ln_g, ln_b, ws, bs_t):
    u = [_gelu(p) for p in pu]
    v = [_gelu(p) for p in pv]
    mu = sum(jnp.sum(t, axis=-1, keepdims=True) for t in v) * (1.0 / D_INNER)
    vc = [t - mu for t in v]
    var = sum(jnp.sum(t * t, axis=-1, keepdims=True) for t in vc) * (1.0 / D_INNER)
    rstd = lax.rsqrt(var + EPS)
    row = lax.broadcasted_iota(jnp.int32, (CHUNK, CHUNK), 0)
    col = lax.broadcasted_iota(jnp.int32, (CHUNK, CHUNK), 1)
    out = []
    for gi in range(A_GROUPS):
        vn = vc[gi] * rstd * ln_g[gi] + ln_b[gi]
        w = jnp.where(row >= col, ws[gi], 0.0)
        sv = _dot(w, vn) + bs_t[gi]
        out.append(u[gi] * sv)
    return out


def _split(ref, n, width):
    return [ref[:, i * width:(i + 1) * width] for i in range(n)]


def _gate_in_specs():
    return [
        pl.BlockSpec((CHUNK, D_INNER), lambda c: (c, 0)),
        pl.BlockSpec((CHUNK, D_INNER), lambda c: (c, 1)),
        pl.BlockSpec((1, D_INNER), lambda c: (0, 0)),
        pl.BlockSpec((1, D_INNER), lambda c: (0, 0)),
        pl.BlockSpec((A_GROUPS, CHUNK, CHUNK), lambda c: (0, 0, 0)),
        pl.BlockSpec((A_GROUPS, CHUNK, 1), lambda c: (0, 0, 0)),
    ]


def _gate_args(u_ref, v_ref, g_ref, b_ref, ws_ref, bs_ref):
    ng, gw = A_GROUPS, A_GROUP_W
    return (_split(u_ref, ng, gw), _split(v_ref, ng, gw), _split(g_ref, ng, gw), _split(b_ref, ng, gw),
            [ws_ref[i] for i in range(ng)], [bs_ref[i] for i in range(ng)])


def _gate_fwd(proj, ln_g, ln_b, ws, bs_col, mixcat, *, name):
    def body(u_ref, v_ref, g_ref, b_ref, ws_ref, bs_ref, cat_in, cat_ref):
        del cat_in
        out = _gate_tile(*_gate_args(u_ref, v_ref, g_ref, b_ref, ws_ref, bs_ref))
        for gi, o in enumerate(out):
            cat_ref[:, gi * A_GROUP_W:(gi + 1) * A_GROUP_W] = o.astype(cat_ref.dtype)

    return pl.pallas_call(
        body, grid=(N_CHUNKS,), in_specs=[*_gate_in_specs(), pl.BlockSpec(memory_space=pl.ANY)],
        out_specs=pl.BlockSpec((CHUNK, D_INNER), lambda c: (c, 0)), out_shape=SDS(mixcat.shape, mixcat.dtype),
        input_output_aliases={6: 0}, compiler_params=_cparams(("parallel",)), name=name,
    )(proj, proj, ln_g, ln_b, ws, bs_col, mixcat)


def _gate_bwd(proj, ln_g, ln_b, ws, bs_col, dcat, dproj, *, name):
    ng, gw = A_GROUPS, A_GROUP_W

    def body(u_ref, v_ref, g_ref, b_ref, ws_ref, bs_ref, d_ref, dproj_in, dproj_ref, dg_ref, db_ref, dws_ref, dbs_ref):
        del dproj_in
        args = _gate_args(u_ref, v_ref, g_ref, b_ref, ws_ref, bs_ref)
        _, vjp = jax.vjp(_gate_tile, *args)
        dpu, dpv, dg, db, dws, dbs = vjp(_split(d_ref, ng, gw))
        for gi in range(ng):
            dproj_ref[:, gi * gw:(gi + 1) * gw] = dpu[gi].astype(dproj_ref.dtype)
            dproj_ref[:, D_INNER + gi * gw:D_INNER + (gi + 1) * gw] = dpv[gi].astype(dproj_ref.dtype)

        @pl.when(pl.program_id(0) == 0)
        def _():
            for r in (dg_ref, db_ref, dws_ref, dbs_ref):
                r[...] = jnp.zeros_like(r)

        for gi in range(ng):
            dg_ref[:, gi * gw:(gi + 1) * gw] += dg[gi]
            db_ref[:, gi * gw:(gi + 1) * gw] += db[gi]
            dws_ref[gi] += dws[gi]
            dbs_ref[gi] += dbs[gi]

    in_specs = _gate_in_specs()
    return pl.pallas_call(
        body, grid=(N_CHUNKS,),
        in_specs=[*in_specs, pl.BlockSpec((CHUNK, D_INNER), lambda c: (c, 0)), pl.BlockSpec(memory_space=pl.ANY)],
        out_specs=[pl.BlockSpec((CHUNK, 2 * D_INNER), lambda c: (c, 0)), *in_specs[2:]],
        out_shape=[SDS(dproj.shape, dproj.dtype), SDS((1, D_INNER), F32), SDS((1, D_INNER), F32),
                   SDS((ng, CHUNK, CHUNK), F32), SDS((ng, CHUNK, 1), F32)],
        input_output_aliases={7: 0}, compiler_params=_cparams(("arbitrary",)), name=name,
    )(proj, proj, ln_g, ln_b, ws, bs_col, dcat, dproj)


ATT_TQ = 512


def _attn_tile(q, k, v):
    s = _dot_nt(q, k) * (1.0 / math.sqrt(X_HEAD_DIM))
    s = s - jnp.max(s, axis=-1, keepdims=True)
    e = jnp.exp(s)
    p = e / jnp.sum(e, axis=-1, keepdims=True)
    return _dot(p, v)


def _attn_in_specs(q_blk, order):
    hd = X_HEAD_DIM
    return [
        pl.BlockSpec((ATT_TQ, hd), lambda a, b: (order(a, b)[0], q_blk + order(a, b)[1])),
        pl.BlockSpec((N_MEM, hd), lambda a, b: (0, order(a, b)[1])),
        pl.BlockSpec((N_MEM, hd), lambda a, b: (0, X_HEADS + order(a, b)[1])),
    ]


def _attn_fwd(proj, q_off, kv, *, name):
    order = lambda i, h: (i, h)
    cat_blk = D_INNER // X_HEAD_DIM

    def body(q_ref, k_ref, v_ref, o_ref):
        o_ref[...] = _attn_tile(q_ref[...], k_ref[...], v_ref[...]).astype(o_ref.dtype)

    return pl.pallas_call(
        body, grid=(SEQ // ATT_TQ, X_HEADS), in_specs=_attn_in_specs(q_off // X_HEAD_DIM, order),
        out_specs=pl.BlockSpec((ATT_TQ, X_HEAD_DIM), lambda i, h: (i, cat_blk + h)),
        out_shape=SDS((SEQ, MIX_OUT), BF16), compiler_params=_cparams(("parallel", "parallel")), name=name,
    )(proj, kv, kv)


def _attn_bwd(proj, q_off, kv, dcat, dproj_width, dq_off, *, name):
    order = lambda h, i: (i, h)
    cat_blk = D_INNER // X_HEAD_DIM
    dq_blk = dq_off // X_HEAD_DIM

    def body(q_ref, k_ref, v_ref, do_ref, dq_ref, dk_ref, dv_ref):
        _, vjp = jax.vjp(_attn_tile, q_ref[...], k_ref[...], v_ref[...])
        dq, dk, dv = vjp(do_ref[...])
        dq_ref[...] = dq.astype(dq_ref.dtype)

        @pl.when(pl.program_id(1) == 0)
        def _():
            dk_ref[...] = jnp.zeros_like(dk_ref)
            dv_ref[...] = jnp.zeros_like(dv_ref)

        dk_ref[...] += dk
        dv_ref[...] += dv

    kv_spec = pl.BlockSpec((N_MEM, X_HEAD_DIM), lambda h, i: (0, h))
    return pl.pallas_call(
        body, grid=(X_HEADS, SEQ // ATT_TQ),
        in_specs=[*_attn_in_specs(q_off // X_HEAD_DIM, order),
                  pl.BlockSpec((ATT_TQ, X_HEAD_DIM), lambda h, i: (i, cat_blk + h))],
        out_specs=[pl.BlockSpec((ATT_TQ, X_HEAD_DIM), lambda h, i: (i, dq_blk + h)), kv_spec, kv_spec],
        out_shape=[SDS((SEQ, dproj_width), BF16), SDS((N_MEM, X_WIDTH), F32), SDS((N_MEM, X_WIDTH), F32)],
        compiler_params=_cparams(("parallel", "arbitrary")), name=name,
    )(proj, kv, kv, dcat)


CONV_TC = 512


def _shift_down(x, s):
    if s == 0:
        return x
    row = lax.broadcasted_iota(jnp.int32, x.shape, 0)
    return jnp.where(row >= s, pltpu.roll(x, s, 0), 0.0)


def _shift_up(x, s):
    if s == 0:
        return x
    n = x.shape[0]
    row = lax.broadcasted_iota(jnp.int32, x.shape, 0)
    return jnp.where(row < n - s, pltpu.roll(x, n - s, 0), 0.0)


def _conv_pre(x, w_ref, b_ref):
    pre = b_ref[...] + jnp.zeros_like(x)
    for k in range(CONV_K):
        pre = pre + w_ref[k:k + 1, :] * _shift_down(x, CONV_K - 1 - k)
    return pre


def _conv_fwd(proj, w, b, *, name):
    blk0 = D_INNER // CONV_TC

    def body(x_ref, w_ref, b_ref, o_ref):
        pre = _conv_pre(x_ref[...], w_ref, b_ref)
        o_ref[...] = pre * jax.nn.sigmoid(pre)

    return pl.pallas_call(
        body, grid=(CONV_DIM // CONV_TC,),
        in_specs=[pl.BlockSpec((SEQ, CONV_TC), lambda j: (0, blk0 + j)), pl.BlockSpec((CONV_K, CONV_TC), lambda j: (0, j)),
                  pl.BlockSpec((1, CONV_TC), lambda j: (0, j))],
        out_specs=pl.BlockSpec((SEQ, CONV_TC), lambda j: (0, j)), out_shape=SDS((SEQ, CONV_DIM), F32),
        compiler_params=_cparams(("parallel",)), name=name)(proj, w, b)


def _conv_bwd(proj, w, b, dxs, dbm, dcm, dproj, *, name):
    tc = CONV_TC // 2
    blk0 = D_INNER // tc
    n_x = D_INNER // tc
    n_b = SSM_GROUPS * SSM_STATE // tc

    def body(x_ref, w_ref, b_ref, dxs_ref, dbm_ref, dcm_ref, dproj_in, dproj_ref, dw_ref, db_ref):
        del dproj_in
        j = pl.program_id(0)
        x = x_ref[...]
        pre = _conv_pre(x, w_ref, b_ref)
        sg = jax.nn.sigmoid(pre)
        dact = jnp.where(j < n_x, dxs_ref[...], jnp.where(j < n_x + n_b, dbm_ref[...], dcm_ref[...]))
        dpre = dact * (sg * (1.0 + pre * (1.0 - sg)))
        dx = jnp.zeros_like(x)
        for k in range(CONV_K):
            s = CONV_K - 1 - k
            dx = dx + w_ref[k:k + 1, :] * _shift_up(dpre, s)
            dw_ref[k:k + 1, :] = jnp.sum(dpre * _shift_down(x, s), axis=0, keepdims=True)
        dproj_ref[...] = dx.astype(dproj_ref.dtype)
        db_ref[...] = jnp.sum(dpre, axis=0, keepdims=True)

    clip = lambda v, hi: jnp.minimum(jnp.maximum(v, 0), hi)
    return pl.pallas_call(
        body, grid=(CONV_DIM // tc,),
        in_specs=[pl.BlockSpec((SEQ, tc), lambda j: (0, blk0 + j)), pl.BlockSpec((CONV_K, tc), lambda j: (0, j)),
                  pl.BlockSpec((1, tc), lambda j: (0, j)),
                  pl.BlockSpec((SEQ, tc), lambda j: (0, clip(j, n_x - 1))),
                  pl.BlockSpec((SEQ, tc), lambda j: (0, clip(j - n_x, n_b - 1))),
                  pl.BlockSpec((SEQ, tc), lambda j: (0, clip(j - n_x - n_b, n_b - 1))),
                  pl.BlockSpec(memory_space=pl.ANY)],
        out_specs=[pl.BlockSpec((SEQ, tc), lambda j: (0, blk0 + j)), pl.BlockSpec((CONV_K, tc), lambda j: (0, j)),
                   pl.BlockSpec((1, tc), lambda j: (0, j))],
        out_shape=[SDS(dproj.shape, dproj.dtype), SDS((CONV_K, CONV_DIM), F32), SDS((1, CONV_DIM), F32)],
        input_output_aliases={6: 0}, compiler_params=_cparams(("parallel",)), name=name,
    )(proj, w, b, dxs, dbm, dcm, dproj)


def _ssd_tile(xs, zs, bm, cm, hs, dtc, dtr, bias, alog, dsk, gn):
    row = lax.broadcasted_iota(jnp.int32, (CHUNK, CHUNK), 0)
    col = lax.broadcasted_iota(jnp.int32, (CHUNK, CHUNK), 1)
    causal = row >= col
    tri = jnp.where(causal, 1.0, 0.0)
    cb = _dot_nt(cm, bm)
    ygs, hn = [], []
    for r in range(SSM_HPG):
        a = -jnp.exp(alog[r])
        da_c = jax.nn.softplus(dtc[r] + bias[r]) * a
        da_r = jax.nn.softplus(dtr[r] + bias[r]) * a
        dt_c = jax.nn.softplus(dtc[r] + bias[r])
        cs_c = jnp.sum(tri * da_r, axis=1, keepdims=True)
        cs_r = jnp.sum(jnp.where(row <= col, 1.0, 0.0) * da_c, axis=0, keepdims=True)
        cs_last = jnp.sum(da_c, axis=0, keepdims=True)
        lm = jnp.exp(jnp.where(causal, cs_c - cs_r, -1e30))
        xdt = xs[r] * dt_c
        y = _dot(cb * lm, xdt)
        y = y + _dot_nt(cm, hs[r]) * jnp.exp(cs_c)
        y = y + xs[r] * dsk[r]
        states = _dot_tn(xdt * jnp.exp(cs_last - cs_c), bm)
        hn.append(hs[r] * jnp.exp(cs_last) + states)
        ygs.append(y * (zs[r] * jax.nn.sigmoid(zs[r])))
    ms = sum(jnp.sum(t * t, axis=-1, keepdims=True) for t in ygs) * (1.0 / SSM_GROUP_W)
    rs = lax.rsqrt(ms + EPS)
    return [ygs[r] * rs * gn[r] for r in range(SSM_HPG)], hn


def _ssd_in_specs(cidx):
    gw, n = SSM_GROUP_W, SSM_STATE
    bm_blk = D_INNER // n
    return [
        pl.BlockSpec((CHUNK, gw), lambda g, c: (cidx(c), g)),
        pl.BlockSpec((CHUNK, gw), lambda g, c: (cidx(c), g)),
        pl.BlockSpec((CHUNK, n), lambda g, c: (cidx(c), bm_blk + g)),
        pl.BlockSpec((CHUNK, n), lambda g, c: (cidx(c), bm_blk + SSM_GROUPS + g)),
        pl.BlockSpec((None, CHUNK, SSM_HPG), lambda g, c: (g, cidx(c), 0)),
        pl.BlockSpec((None, SSM_HPG, CHUNK), lambda g, c: (g, 0, cidx(c))),
        pl.BlockSpec((None, 1, SSM_HPG), lambda g, c: (g, 0, 0)),
        pl.BlockSpec((None, 1, SSM_HPG), lambda g, c: (g, 0, 0)),
        pl.BlockSpec((None, 1, SSM_HPG), lambda g, c: (g, 0, 0)),
        pl.BlockSpec((1, gw), lambda g, c: (0, g)),
    ]


def _ssd_args(x_ref, z_ref, bm_ref, cm_ref, hs, dtc_ref, dtr_ref, bias_ref, alog_ref, dsk_ref, gn_ref):
    nh, p = SSM_HPG, SSM_HEAD_DIM
    col = lambda ref: [ref[:, r:r + 1] for r in range(nh)]
    return (_split(x_ref, nh, p), _split(z_ref, nh, p), bm_ref[...], cm_ref[...], hs,
            col(dtc_ref), [dtr_ref[r:r + 1, :] for r in range(nh)], col(bias_ref), col(alog_ref), col(dsk_ref),
            _split(gn_ref, nh, p))


def _ssd_fwd(xbc, proj, dt_c, dt_r, bias, alog, dsk, gn, mixcat, *, name):
    nh = SSM_HPG

    def body(x_ref, z_ref, bm_ref, cm_ref, dtc_ref, dtr_ref, bias_ref, alog_ref, dsk_ref, gn_ref, cat_in,
             cat_ref, hprev_ref, h_scr):
        del cat_in

        @pl.when(pl.program_id(1) == 0)
        def _():
            h_scr[...] = jnp.zeros_like(h_scr)

        hs = [h_scr[r] for r in range(nh)]
        for r in range(nh):
            hprev_ref[r] = hs[r]
        yn, hn = _ssd_tile(*_ssd_args(x_ref, z_ref, bm_ref, cm_ref, hs, dtc_ref, dtr_ref, bias_ref, alog_ref,
                                      dsk_ref, gn_ref))
        for r in range(nh):
            cat_ref[:, r * SSM_HEAD_DIM:(r + 1) * SSM_HEAD_DIM] = yn[r].astype(cat_ref.dtype)
            h_scr[r] = hn[r]

    return pl.pallas_call(
        body, grid=(SSM_GROUPS, N_CHUNKS), in_specs=[*_ssd_in_specs(lambda c: c), pl.BlockSpec(memory_space=pl.ANY)],
        out_specs=[pl.BlockSpec((CHUNK, SSM_GROUP_W), lambda g, c: (c, g)),
                   pl.BlockSpec((None, nh, SSM_HEAD_DIM, SSM_STATE), lambda g, c: (c, g, 0, 0))],
        out_shape=[SDS(mixcat.shape, mixcat.dtype), SDS((N_CHUNKS, SSM_HEADS, SSM_HEAD_DIM, SSM_STATE), F32)],
        scratch_shapes=[pltpu.VMEM((nh, SSM_HEAD_DIM, SSM_STATE), F32)],
        input_output_aliases={10: 0}, compiler_params=_cparams(("parallel", "arbitrary")), name=name,
    )(xbc, proj, xbc, xbc, dt_c, dt_r, bias, alog, dsk, gn, mixcat)


def _ssd_bwd(xbc, proj, dt_c, dt_r, bias, alog, dsk, gn, hprev, dcat, dproj, *, name):
    nh, p, gw, n = SSM_HPG, SSM_HEAD_DIM, SSM_GROUP_W, SSM_STATE
    rev = lambda c: N_CHUNKS - 1 - c

    def body(x_ref, z_ref, bm_ref, cm_ref, dtc_ref, dtr_ref, bias_ref, alog_ref, dsk_ref, gn_ref, hprev_ref, dy_ref,
             dproj_in, dz_ref, dxs_ref, dbm_ref, dcm_ref, ddtc_ref, ddtr_ref, dbias_ref, dalog_ref, ddsk_ref, dgn_ref,
             dh_scr):
        del dproj_in
        first = pl.program_id(1) == 0

        @pl.when(first)
        def _():
            dh_scr[...] = jnp.zeros_like(dh_scr)
            for ref in (dbias_ref, dalog_ref, ddsk_ref, dgn_ref):
                ref[...] = jnp.zeros_like(ref)

        hs = [hprev_ref[r] for r in range(nh)]
        args = _ssd_args(x_ref, z_ref, bm_ref, cm_ref, hs, dtc_ref, dtr_ref, bias_ref, alog_ref, dsk_ref, gn_ref)
        _, vjp = jax.vjp(_ssd_tile, *args)
        dxs, dzs, dbm, dcm, dhs, ddtc, ddtr, dbias, dalog, ddsk, dgn = vjp(
            (_split(dy_ref, nh, p), [dh_scr[r] for r in range(nh)]))
        dbm_ref[...] = dbm
        dcm_ref[...] = dcm
        for r in range(nh):
            dxs_ref[:, r * p:(r + 1) * p] = dxs[r]
            dz_ref[:, r * p:(r + 1) * p] = dzs[r].astype(dz_ref.dtype)
            dh_scr[r] = dhs[r]
            ddtc_ref[:, r:r + 1] = ddtc[r]
            ddtr_ref[r:r + 1, :] = ddtr[r]
            dbias_ref[:, r:r + 1] += dbias[r]
            dalog_ref[:, r:r + 1] += dalog[r]
            ddsk_ref[:, r:r + 1] += ddsk[r]
            dgn_ref[:, r * p:(r + 1) * p] += dgn[r]

    par_spec = pl.BlockSpec((None, 1, nh), lambda g, c: (g, 0, 0))
    return pl.pallas_call(
        body, grid=(SSM_GROUPS, N_CHUNKS),
        in_specs=[*_ssd_in_specs(rev),
                  pl.BlockSpec((None, nh, p, n), lambda g, c: (rev(c), g, 0, 0)),
                  pl.BlockSpec((CHUNK, gw), lambda g, c: (rev(c), g)),
                  pl.BlockSpec(memory_space=pl.ANY)],
        out_specs=[pl.BlockSpec((CHUNK, gw), lambda g, c: (rev(c), g)),
                   pl.BlockSpec((CHUNK, gw), lambda g, c: (rev(c), g)),
                   pl.BlockSpec((CHUNK, n), lambda g, c: (rev(c), g)),
                   pl.BlockSpec((CHUNK, n), lambda g, c: (rev(c), g)),
                   pl.BlockSpec((None, CHUNK, nh), lambda g, c: (g, rev(c), 0)),
                   pl.BlockSpec((None, nh, CHUNK), lambda g, c: (g, 0, rev(c))),
                   par_spec, par_spec, par_spec,
                   pl.BlockSpec((1, gw), lambda g, c: (0, g))],
        out_shape=[SDS(dproj.shape, dproj.dtype), SDS((SEQ, D_INNER), F32), SDS((SEQ, SSM_GROUPS * n), F32),
                   SDS((SEQ, SSM_GROUPS * n), F32), SDS((SSM_GROUPS, SEQ, nh), F32), SDS((SSM_GROUPS, nh, SEQ), F32),
                   SDS((SSM_GROUPS, 1, nh), F32), SDS((SSM_GROUPS, 1, nh), F32), SDS((SSM_GROUPS, 1, nh), F32),
                   SDS((1, D_INNER), F32)],
        scratch_shapes=[pltpu.VMEM((nh, p, n), F32)],
        input_output_aliases={12: 0}, compiler_params=_cparams(("parallel", "arbitrary")), name=name,
    )(xbc, proj, xbc, xbc, dt_c, dt_r, bias, alog, dsk, gn, hprev, dcat, dproj)


def _add_pair(a, b, *, name):
    n, r, c = a.shape
    tr = _pick(r, (256, 384, 128, 8))
    spec = pl.BlockSpec((None, tr, c), lambda s, i: (s, i, 0))

    def body(a_ref, b_ref, o_ref):
        o_ref[...] = (a_ref[...] + b_ref[...]).astype(o_ref.dtype)

    return pl.pallas_call(body, grid=(n, r // tr), in_specs=[spec, spec], out_specs=spec, out_shape=SDS(a.shape, BF16),
                          compiler_params=_cparams(("parallel", "parallel")), name=name)(a, b)


def _sum_slots(p, *, name):
    n, r, c = p.shape
    tr = _pick(r, (256, 384, 128, 8))

    def body(p_ref, o_ref):
        acc = p_ref[0].astype(F32)
        for s in range(1, n):
            acc = acc + p_ref[s].astype(F32)
        o_ref[...] = acc

    return pl.pallas_call(body, grid=(r // tr,), in_specs=[pl.BlockSpec((n, tr, c), lambda i: (0, i, 0))],
                          out_specs=pl.BlockSpec((tr, c), lambda i: (i, 0)), out_shape=SDS((r, c), F32),
                          compiler_params=_cparams(("parallel",)), name=name)(p)


def _adamw(w, g, m, v, *, name):
    r, c = w.shape
    tr = r if r <= 256 else _pick(r, (256, 128, 8))
    spec = pl.BlockSpec((tr, c), lambda i: (i, 0))

    def body(w_ref, g_ref, m_ref, v_ref, d_ref, mo_ref, vo_ref):
        g = g_ref[...]
        m_new = ADAM_B1 * m_ref[...] + (1.0 - ADAM_B1) * g
        v_new = ADAM_B2 * v_ref[...] + (1.0 - ADAM_B2) * (g * g)
        m_hat = m_new / (1.0 - ADAM_B1 ** ADAM_STEP)
        v_hat = v_new / (1.0 - ADAM_B2 ** ADAM_STEP)
        d_ref[...] = -ADAM_LR * (m_hat / (jnp.sqrt(v_hat) + ADAM_EPS) + ADAM_WD * w_ref[...])
        mo_ref[...] = m_new
        vo_ref[...] = v_new

    return pl.pallas_call(body, grid=(r // tr,), in_specs=[spec] * 4, out_specs=[spec] * 3,
                          out_shape=[SDS((r, c), F32)] * 3, compiler_params=_cparams(("parallel",)), name=name)(w, g, m, v)


ANY = pl.BlockSpec(memory_space=pl.ANY)


def _place():
    x, y, c = lax.axis_index("x"), lax.axis_index("y"), lax.axis_index("c")
    chips = [(1 - x, y), (x, 1 - y), (1 - x, 1 - y)]
    return x, y, c, chips


def _remote(src, dst, send_sem, recv_sem, to):
    return pltpu.make_async_remote_copy(src_ref=src, dst_ref=dst, send_sem=send_sem, recv_sem=recv_sem,
                                        device_id=to, device_id_type=MESH)


def _all_gather_shards(shards, *, name):
    n = len(shards)

    def body(*refs):
        ins, outs = refs[:n], refs[n:2 * n]
        send_sems, recv_sems, local_sems = refs[2 * n:]
        x, y, c, chips = _place()
        me = 2 * x + y
        sibling = (x, y, 1 - c)
        local = [pltpu.make_async_copy(ins[t], outs[t].at[me], local_sems.at[t]) for t in range(n)]
        for cp in local:
            cp.start()
        first = [[_remote(ins[t].at[c], outs[t].at[me, c], send_sems.at[6 * t + j], recv_sems.at[6 * t + j], (*chip, c))
                  for j, chip in enumerate(chips)] for t in range(n)]
        for t in range(n):
            for cp in first[t]:
                cp.start()
        passed = []
        for j, (cx, cy) in enumerate(chips):
            k = 2 * cx + cy
            for t in range(n):
                landed = outs[t].at[k, c]
                _remote(landed, landed, send_sems.at[6 * t + j], recv_sems.at[6 * t + j], (cx, cy, c)).wait_recv()
                fwd = _remote(landed, landed, send_sems.at[6 * t + 3 + j], recv_sems.at[6 * t + 3 + j], sibling)
                fwd.start()
                passed.append(fwd)
        for j, (cx, cy) in enumerate(chips):
            k = 2 * cx + cy
            for t in range(n):
                got = outs[t].at[k, 1 - c]
                _remote(got, got, send_sems.at[6 * t + 3 + j], recv_sems.at[6 * t + 3 + j], sibling).wait_recv()
        for t in range(n):
            for cp in first[t]:
                cp.wait_send()
        for cp in passed:
            cp.wait_send()
        for cp in local:
            cp.wait()

    return pl.pallas_call(
        body, in_specs=[ANY] * n, out_specs=[ANY] * n,
        out_shape=[SDS((N_CHIPS, *s.shape), s.dtype) for s in shards],
        scratch_shapes=[pltpu.SemaphoreType.DMA((6 * n,)), pltpu.SemaphoreType.DMA((6 * n,)),
                        pltpu.SemaphoreType.DMA((n,))],
        name=name)(*shards)


def _pair_exchange(halves, *, name):
    n = len(halves)
    flat = [a for pair in halves for a in pair]

    def body(*refs):
        ins = refs[:2 * n]
        own, got = refs[2 * n:3 * n], refs[3 * n:4 * n]
        send_sems, recv_sems, local_sems = refs[4 * n:]
        x, y, c, _ = _place()
        sibling = (x, y, 1 - c)
        for mine in (0, 1):
            @pl.when(c == mine)
            def _():
                for t in range(n):
                    pltpu.make_async_copy(ins[2 * t + mine], own[t], local_sems.at[t]).start()
                    _remote(ins[2 * t + 1 - mine], got[t], send_sems.at[t], recv_sems.at[t], sibling).start()
        for t in range(n):
            cp = _remote(ins[2 * t], got[t], send_sems.at[t], recv_sems.at[t], sibling)
            cp.wait_recv()
            cp.wait_send()
            pltpu.make_async_copy(ins[2 * t], own[t], local_sems.at[t]).wait()

    shapes = [SDS(pair[0].shape, pair[0].dtype) for pair in halves]
    outs = pl.pallas_call(
        body, in_specs=[ANY] * (2 * n), out_specs=[ANY] * (2 * n), out_shape=shapes + shapes,
        scratch_shapes=[pltpu.SemaphoreType.DMA((n,)), pltpu.SemaphoreType.DMA((n,)), pltpu.SemaphoreType.DMA((n,))],
        name=name)(*flat)
    return outs[:n], outs[n:]


def _chip_scatter(parts, *, name):
    n = len(parts)

    def body(*refs):
        ins, outs = refs[:n], refs[n:2 * n]
        send_sems, recv_sems, local_sems = refs[2 * n:]
        x, y, c, chips = _place()
        me = 2 * x + y
        local = [pltpu.make_async_copy(ins[t].at[me], outs[t].at[me], local_sems.at[t]) for t in range(n)]
        for cp in local:
            cp.start()
        sends = []
        for t in range(n):
            for j, (cx, cy) in enumerate(chips):
                cp = _remote(ins[t].at[2 * cx + cy], outs[t].at[me], send_sems.at[3 * t + j], recv_sems.at[3 * t + j],
                             (cx, cy, c))
                cp.start()
                sends.append(cp)
        for t in range(n):
            for j, (cx, cy) in enumerate(chips):
                landed = outs[t].at[2 * cx + cy]
                _remote(landed, landed, send_sems.at[3 * t + j], recv_sems.at[3 * t + j], (cx, cy, c)).wait_recv()
        for cp in sends:
            cp.wait_send()
        for cp in local:
            cp.wait()

    return pl.pallas_call(
        body, in_specs=[ANY] * n, out_specs=[ANY] * n, out_shape=[SDS(p.shape, p.dtype) for p in parts],
        scratch_shapes=[pltpu.SemaphoreType.DMA((3 * n,)), pltpu.SemaphoreType.DMA((3 * n,)),
                        pltpu.SemaphoreType.DMA((n,))],
        name=name)(*parts)


def _pair_share(finals, *, name):
    n = len(finals)

    def body(*refs):
        ins, outs = refs[:n], refs[n:2 * n]
        send_sems, recv_sems, local_sems = refs[2 * n:]
        x, y, c, _ = _place()
        sibling = (x, y, 1 - c)
        cps = []
        for t in range(n):
            loc = pltpu.make_async_copy(ins[t], outs[t].at[c], local_sems.at[t])
            loc.start()
            rem = _remote(ins[t], outs[t].at[c], send_sems.at[t], recv_sems.at[t], sibling)
            rem.start()
            cps.append((loc, rem))
        for t, (loc, rem) in enumerate(cps):
            got = outs[t].at[1 - c]
            _remote(got, got, send_sems.at[t], recv_sems.at[t], sibling).wait_recv()
            rem.wait_send()
            loc.wait()

    return pl.pallas_call(
        body, in_specs=[ANY] * n, out_specs=[ANY] * n, out_shape=[SDS((2, *f.shape), f.dtype) for f in finals],
        scratch_shapes=[pltpu.SemaphoreType.DMA((n,)), pltpu.SemaphoreType.DMA((n,)), pltpu.SemaphoreType.DMA((n,))],
        name=name)(*finals)


def _all_reduce_small(v, *, name):
    rows, lanes = v.shape
    n_dev = 8

    def body(v_ref, o_ref, all_ref, send_sems, recv_sems, local_sem):
        x, y, c, chips = _place()
        me, sibling = (x, y, c), (x, y, 1 - c)

        def block(px, py, pc):
            return all_ref.at[4 * px + 2 * py + pc]

        def copy(k, blk, to, src=None):
            return _remote(block(*blk) if src is None else src, block(*blk), send_sems.at[k], recv_sems.at[k], to)

        mine = pltpu.make_async_copy(v_ref, block(*me), local_sem)
        mine.start()
        first = [copy(0, me, sibling, src=v_ref)]
        first += [copy(1 + j, me, (*chip, c), src=v_ref) for j, chip in enumerate(chips)]
        for cp in first:
            cp.start()
        passed = [copy(4 + j, (*chip, c), sibling) for j, chip in enumerate(chips)]
        for j, chip in enumerate(chips):
            copy(1 + j, (*chip, c), me).wait_recv()
            passed[j].start()
        copy(0, sibling, me).wait_recv()
        for j, chip in enumerate(chips):
            copy(4 + j, (*chip, 1 - c), me).wait_recv()
        for cp in first + passed:
            cp.wait_send()
        mine.wait()
        acc = all_ref[0]
        for k in range(1, n_dev):
            acc = acc + all_ref[k]
        o_ref[...] = acc

    vmem = pl.BlockSpec(memory_space=pltpu.VMEM)
    return pl.pallas_call(
        body, in_specs=[vmem], out_specs=vmem, out_shape=SDS((rows, lanes), F32),
        scratch_shapes=[pltpu.VMEM((n_dev, rows, lanes), F32), pltpu.SemaphoreType.DMA((7,)),
                        pltpu.SemaphoreType.DMA((7,)), pltpu.SemaphoreType.DMA],
        compiler_params=pltpu.CompilerParams(vmem_limit_bytes=VMEM_LIMIT), name=name)(v)


def _relu2_epilogue(acc):
    return acc, jnp.square(jnp.maximum(acc, 0.0))


def _res_epilogue(acc, res):
    return (acc + res,)


def _drelu2_epilogue(acc, pre):
    return (acc * (2.0 * jnp.maximum(pre.astype(F32), 0.0)),)


def _ffn_fwd(h, g, w1, w2, tag):
    f = _rms_fwd(h, g, name=f"ffn_norm_{tag}")
    pre, act = _mm_nn(f, w1, name=f"ffn1_{tag}", epilogue=_relu2_epilogue, n_out_dtypes=(BF16, BF16))
    h_out = _mm_nn(act, w2, name=f"ffn2_{tag}", extras=(h,), epilogue=_res_epilogue)
    return h_out, (f, pre, act)


def _ffn_bwd(dh, h, g, w1, w2, saved, tag):
    f, pre, act = saved
    dpre = _mm_nt(dh, w2, name=f"ffn2_dx_{tag}", out_dtype=BF16, extras=(pre,), epilogue=_drelu2_epilogue)
    dw2 = _mm_tn(act, dh, name=f"ffn2_dw_{tag}")
    df = _mm_nt(dpre, w1, name=f"ffn1_dx_{tag}")
    dw1 = _mm_tn(f, dpre, name=f"ffn1_dw_{tag}", slots=N_CHIPS)
    dh, dg = _rms_bwd(h, g, df, dh, name=f"ffn_norm_bwd_{tag}")
    return dh, dg, dw1, dw2


def _kv_fwd(mem, g, w_kv, tag):
    m = _rms_fwd(mem, g, name=f"mem_norm_{tag}")
    return m, _mm_nn(m, w_kv, name=f"kv_{tag}")


def _kv_bwd(mem, g, w_kv, m, dk, dv, tag):
    dkv = jnp.concatenate([dk, dv], axis=1)
    dw = _mm_tn(m, dkv, name=f"kv_dw_{tag}", slots=N_CHIPS)
    dm = _mm_nt(dkv, w_kv, name=f"kv_dx_{tag}")
    _, dg = _rms_bwd(mem, g, dm, dm, name=f"mem_norm_bwd_{tag}")
    return dw, dg


def _local_step(x, mem, target, p):
    row = lambda v: v.reshape(1, -1)
    g = {}

    h0 = x
    a0 = _rms_fwd(h0, row(p["norm_mix"][0]), name="mix_norm_0")
    proj_a = _mm_nn(a0, p["a_in"], name="a_in")
    m0, kv0 = _kv_fwd(mem, row(p["mem_norm"][0]), p["w_kv"][0], "0")
    cat0 = _attn_fwd(proj_a, 2 * D_INNER, kv0, name="attn_0")
    bs_col = p["a_bs"].reshape(A_GROUPS, CHUNK, 1)
    cat0 = _gate_fwd(proj_a, p["a_ln_g"], p["a_ln_b"], p["a_ws"], bs_col, cat0, name="gate")
    h1 = _mm_nn(cat0, p["w_out"][0], name="out_0", extras=(h0,), epilogue=_res_epilogue)
    h2, ffn0 = _ffn_fwd(h1, row(p["norm_ffn"][0]), p["w_ffn1"][0], p["w_ffn2"][0], "0")

    a1 = _rms_fwd(h2, row(p["norm_mix"][1]), name="mix_norm_1")
    proj_b = _mm_nn(a1, p["b_in"], name="b_in")
    m1, kv1 = _kv_fwd(mem, row(p["mem_norm"][1]), p["w_kv"][1], "1")
    cat1 = _attn_fwd(proj_b, B_Q_OFF, kv1, name="attn_1")
    xbc = _conv_fwd(proj_b, p["b_conv_w"], p["b_conv_b"], name="conv")
    dt_raw = proj_b[:, B_DT_OFF:B_DT_OFF + SSM_HEADS].reshape(SEQ, SSM_GROUPS, SSM_HPG)
    dt_c = jnp.transpose(dt_raw, (1, 0, 2))
    dt_r = jnp.transpose(dt_raw, (1, 2, 0))
    per_head = lambda v: v.reshape(SSM_GROUPS, 1, SSM_HPG)
    ssd_par = (per_head(p["b_dt_bias"]), per_head(p["b_a_log"]), per_head(p["b_d"]), p["b_gnorm"])
    cat1, hprev = _ssd_fwd(xbc, proj_b, dt_c, dt_r, *ssd_par, cat1, name="ssd")
    h3 = _mm_nn(cat1, p["w_out"][1], name="out_1", extras=(h2,), epilogue=_res_epilogue)
    h4, ffn1 = _ffn_fwd(h3, row(p["norm_ffn"][1]), p["w_ffn1"][1], p["w_ffn2"][1], "1")

    loss, dh, g["final_norm"] = _loss_head(h4, row(p["final_norm"]), target, name="loss_head")

    dh, dnf1, dw1_1, dw2_1 = _ffn_bwd(dh, h3, row(p["norm_ffn"][1]), p["w_ffn1"][1], p["w_ffn2"][1], ffn1, "1")
    dcat1 = _mm_nt(dh, p["w_out"][1], name="out_dx_1")
    dwo_1 = _mm_tn(cat1, dh, name="out_dw_1")
    dproj_b, dk1, dv1 = _attn_bwd(proj_b, B_Q_OFF, kv1, dcat1, B_IN_PAD, B_Q_OFF, name="attn_bwd_1")
    (dproj_b, dxs, dbm, dcm, ddt_c, ddt_r, g["b_dt_bias"], g["b_a_log"], g["b_d"], g["b_gnorm"]) = _ssd_bwd(
        xbc, proj_b, dt_c, dt_r, *ssd_par, hprev, dcat1, dproj_b, name="ssd_bwd")
    dproj_b, g["b_conv_w"], g["b_conv_b"] = _conv_bwd(proj_b, p["b_conv_w"], p["b_conv_b"], dxs, dbm, dcm, dproj_b,
                                                      name="conv_bwd")
    ddt = jnp.transpose(ddt_c, (1, 0, 2)) + jnp.transpose(ddt_r, (2, 0, 1))
    ddt = jnp.pad(ddt.reshape(SEQ, SSM_HEADS), ((0, 0), (0, B_IN_PAD - B_DT_OFF - SSM_HEADS))).astype(BF16)
    dproj_b = lax.dynamic_update_slice(dproj_b, ddt, (0, B_DT_OFF))
    dwkv_1, dmn1 = _kv_bwd(mem, row(p["mem_norm"][1]), p["w_kv"][1], m1, dk1, dv1, "1")
    half = D_MODEL // 2
    dwb = [_mm_tn(a1, dproj_b, name=f"b_in_dw_{i}", x_cols=(i * half, half)) for i in range(2)]
    da1 = _mm_nt(dproj_b, p["b_in"], name="b_in_dx")
    dh, dnm1 = _rms_bwd(h2, row(p["norm_mix"][1]), da1, dh, name="mix_norm_bwd_1")

    dh, dnf0, dw1_0, dw2_0 = _ffn_bwd(dh, h1, row(p["norm_ffn"][0]), p["w_ffn1"][0], p["w_ffn2"][0], ffn0, "0")
    dcat0 = _mm_nt(dh, p["w_out"][0], name="out_dx_0")
    dwo_0 = _mm_tn(cat0, dh, name="out_dw_0")
    dproj_a, dk0, dv0 = _attn_bwd(proj_a, 2 * D_INNER, kv0, dcat0, A_IN, 2 * D_INNER, name="attn_bwd_0")
    dproj_a, g["a_ln_g"], g["a_ln_b"], g["a_ws"], dbs_col = _gate_bwd(
        proj_a, p["a_ln_g"], p["a_ln_b"], p["a_ws"], bs_col, dcat0, dproj_a, name="gate_bwd")
    g["a_bs"] = dbs_col.reshape(A_GROUPS, CHUNK)
    dwkv_0, dmn0 = _kv_bwd(mem, row(p["mem_norm"][0]), p["w_kv"][0], m0, dk0, dv0, "0")
    dwa = [_mm_tn(a0, dproj_a, name=f"a_in_dw_{i}", slots=N_CHIPS, x_cols=(i * half, half)) for i in range(2)]
    da0 = _mm_nt(dproj_a, p["a_in"], name="a_in_dx")
    dx, dnm0 = _rms_bwd(h0, row(p["norm_mix"][0]), da0, dh, name="mix_norm_bwd_0")

    g["norm_mix"] = jnp.concatenate([dnm0, dnm1], axis=0)
    g["norm_ffn"] = jnp.concatenate([dnf0, dnf1], axis=0)
    g["mem_norm"] = jnp.concatenate([dmn0, dmn1], axis=0)
    slots = lambda a, r: a.reshape(N_CHIPS, r, a.shape[-1])
    g["w_kv"] = (dwkv_0, dwkv_1)
    g["w_out"] = (slots(dwo_0, MIX_OUT // N_CHIPS), slots(dwo_1, MIX_OUT // N_CHIPS))
    g["w_ffn1"] = (dw1_0, dw1_1)
    g["w_ffn2"] = (slots(dw2_0, D_FF // N_CHIPS), slots(dw2_1, D_FF // N_CHIPS))
    g["a_in"] = tuple(dwa)
    g["b_in"] = tuple(_b_in_grad_slots(d) for d in dwb)
    return loss, dx, g


def _b_in_full(gathered):
    full = jnp.transpose(gathered, (1, 0, 2)).reshape(D_MODEL, B_IN)
    dt0 = D_INNER + CONV_DIM
    return jnp.concatenate([full[:, :dt0], full[:, dt0 + SSM_HEADS:], full[:, dt0:dt0 + SSM_HEADS],
                            jnp.zeros((D_MODEL, B_IN_PAD - B_IN), full.dtype)], axis=1)


def _b_in_grad_slots(d):
    dt0 = D_INNER + CONV_DIM
    full = jnp.concatenate([d[:, :dt0], d[:, B_DT_OFF:B_DT_OFF + SSM_HEADS], d[:, dt0:B_DT_OFF]], axis=1)
    return jnp.transpose(full.reshape(d.shape[0], N_CHIPS, B_IN // N_CHIPS), (1, 0, 2))


LARGE = ("w_kv", "w_out", "w_ffn1", "w_ffn2", "a_in", "b_in")
SMALL_REPL = ("norm_mix", "norm_ffn", "mem_norm", "a_ln_g", "a_ln_b", "a_ws", "a_bs", "b_dt_bias", "b_a_log", "b_d",
              "final_norm")
SMALL_SHARD = ("b_conv_w", "b_conv_b", "b_gnorm")
WEIGHTS = ("norm_mix", "norm_ffn", "mem_norm", "w_kv", "w_out", "w_ffn1", "w_ffn2", "a_in", "a_ln_g", "a_ln_b", "a_ws",
           "a_bs", "b_in", "b_conv_w", "b_conv_b", "b_dt_bias", "b_a_log", "b_d", "b_gnorm", "final_norm")
CONV_SHARD = CONV_DIM // N_CHIPS
GN_SHARD = D_INNER // N_CHIPS


def _halves(w):
    if w.shape[0] == 2:
        return w
    return w.reshape(2, w.shape[1] // 2, w.shape[2])


def _gather_weights(w):
    big = [_halves(w[k]).astype(BF16) for k in LARGE]
    small = jnp.zeros((2, CONV_K, CONV_SHARD), F32)
    small = small.at[0].set(w["b_conv_w"][0])
    small = small.at[1, 0].set(w["b_conv_b"][0])
    small = small.at[1, 1, :GN_SHARD].set(w["b_gnorm"][0])
    gathered = _all_gather_shards([*big, small], name="gather_weights")
    p = {}
    kv, wo, w1, w2, a_in, b_in, sm = gathered
    p["w_kv"] = [kv[:, l] for l in range(2)]
    p["w_out"] = [wo[:, l].reshape(MIX_OUT, D_MODEL) for l in range(2)]
    p["w_ffn1"] = [w1[:, l] for l in range(2)]
    p["w_ffn2"] = [w2[:, l].reshape(D_FF, D_MODEL) for l in range(2)]
    p["a_in"] = a_in.reshape(N_CHIPS, D_MODEL, A_IN // N_CHIPS)
    p["b_in"] = _b_in_full(b_in.reshape(N_CHIPS, D_MODEL, B_IN // N_CHIPS))
    p["b_conv_w"] = jnp.transpose(sm[:, 0], (1, 0, 2)).reshape(CONV_K, CONV_DIM)
    p["b_conv_b"] = sm[:, 1, 0].reshape(1, CONV_DIM)
    p["b_gnorm"] = sm[:, 1, 1, :GN_SHARD].reshape(1, D_INNER)
    return p


def _reduce_large(g):
    own, got = _pair_exchange([g[k] for k in LARGE], name="grads_pair_exchange")
    parts = [_add_pair(a, b, name=f"grads_pair_add_{k}") for k, a, b in zip(LARGE, own, got)]
    landed = _chip_scatter(parts, name="grads_chip_scatter")
    finals = [_sum_slots(t, name=f"grads_chip_sum_{k}") for k, t in zip(LARGE, landed)]
    shared = _pair_share(finals, name="grads_pair_share")
    return dict(zip(LARGE, shared))


def _small_layout(shapes):
    offs, o = {}, 0
    for k in (*SMALL_REPL, *SMALL_SHARD):
        size = math.prod(shapes[k])
        offs[k] = (o, size)
        o += size
    rows = -(-o // (8 * 128)) * 8
    return offs, rows


def _reduce_small(g, full_shapes):
    offs, rows = _small_layout(full_shapes)
    flat = jnp.concatenate([g[k].reshape(-1) for k in (*SMALL_REPL, *SMALL_SHARD)])
    flat = jnp.pad(flat, (0, rows * 128 - flat.shape[0])).reshape(rows, 128)
    total = _all_reduce_small(flat, name="grads_small_all_reduce").reshape(-1)
    return {k: total[o:o + n].reshape(full_shapes[k]) for k, (o, n) in offs.items()}


def kernel(x, mem, norm_mix, norm_ffn, mem_norm, w_kv, w_out, w_ffn1, w_ffn2, a_in, a_ln_g, a_ln_b, a_ws, a_bs, b_in, b_conv_w, b_conv_b, b_dt_bias, b_a_log, b_d, b_gnorm, final_norm, loss_target, m_norm_mix, m_norm_ffn, m_mem_norm, m_w_kv, m_w_out, m_w_ffn1, m_w_ffn2, m_a_in, m_a_ln_g, m_a_ln_b, m_a_ws, m_a_bs, m_b_in, m_b_conv_w, m_b_conv_b, m_b_dt_bias, m_b_a_log, m_b_d, m_b_gnorm, m_final_norm, v_norm_mix, v_norm_ffn, v_mem_norm, v_w_kv, v_w_out, v_w_ffn1, v_w_ffn2, v_a_in, v_a_ln_g, v_a_ln_b, v_a_ws, v_a_bs, v_b_in, v_b_conv_w, v_b_conv_b, v_b_dt_bias, v_b_a_log, v_b_d, v_b_gnorm, v_final_norm):
    w = dict(norm_mix=norm_mix, norm_ffn=norm_ffn, mem_norm=mem_norm, w_kv=w_kv, w_out=w_out, w_ffn1=w_ffn1,
             w_ffn2=w_ffn2, a_in=a_in, a_ln_g=a_ln_g, a_ln_b=a_ln_b, a_ws=a_ws, a_bs=a_bs, b_in=b_in, b_conv_w=b_conv_w,
             b_conv_b=b_conv_b, b_dt_bias=b_dt_bias, b_a_log=b_a_log, b_d=b_d, b_gnorm=b_gnorm, final_norm=final_norm)
    mom = dict(norm_mix=m_norm_mix, norm_ffn=m_norm_ffn, mem_norm=m_mem_norm, w_kv=m_w_kv, w_out=m_w_out,
               w_ffn1=m_w_ffn1, w_ffn2=m_w_ffn2, a_in=m_a_in, a_ln_g=m_a_ln_g, a_ln_b=m_a_ln_b, a_ws=m_a_ws,
               a_bs=m_a_bs, b_in=m_b_in, b_conv_w=m_b_conv_w, b_conv_b=m_b_conv_b, b_dt_bias=m_b_dt_bias,
               b_a_log=m_b_a_log, b_d=m_b_d, b_gnorm=m_b_gnorm, final_norm=m_final_norm)
    var = dict(norm_mix=v_norm_mix, norm_ffn=v_norm_ffn, mem_norm=v_mem_norm, w_kv=v_w_kv, w_out=v_w_out,
               w_ffn1=v_w_ffn1, w_ffn2=v_w_ffn2, a_in=v_a_in, a_ln_g=v_a_ln_g, a_ln_b=v_a_ln_b, a_ws=v_a_ws,
               a_bs=v_a_bs, b_in=v_b_in, b_conv_w=v_b_conv_w, b_conv_b=v_b_conv_b, b_dt_bias=v_b_dt_bias,
               b_a_log=v_b_a_log, b_d=v_b_d, b_gnorm=v_b_gnorm, final_norm=v_final_norm)

    p = _gather_weights(w)
    p.update(norm_mix=norm_mix, norm_ffn=norm_ffn, mem_norm=mem_norm, a_ln_g=a_ln_g, a_ln_b=a_ln_b, a_ws=a_ws[0],
             a_bs=a_bs[0], b_dt_bias=b_dt_bias, b_a_log=b_a_log, b_d=b_d, final_norm=final_norm)
    loss_part, dx, g = _local_step(x[0], mem[0], loss_target[0], p)
    loss = lax.psum(loss_part[0, 0], ("x", "y", "c"))

    full_shapes = {k: w[k].shape for k in SMALL_REPL}
    full_shapes.update(b_conv_w=(1, CONV_K, CONV_DIM), b_conv_b=(1, CONV_DIM), b_gnorm=(1, D_INNER))
    gs = _reduce_small(g, full_shapes)
    chip = 2 * lax.axis_index("x") + lax.axis_index("y")
    gs["b_conv_w"] = lax.dynamic_slice_in_dim(gs["b_conv_w"], chip * CONV_SHARD, CONV_SHARD, axis=2)
    gs["b_conv_b"] = lax.dynamic_slice_in_dim(gs["b_conv_b"], chip * CONV_SHARD, CONV_SHARD, axis=1)
    gs["b_gnorm"] = lax.dynamic_slice_in_dim(gs["b_gnorm"], chip * GN_SHARD, GN_SHARD, axis=1)
    gl = _reduce_large(g)
    grads = {k: (gl[k].reshape(w[k].shape) if k in gl else gs[k]) for k in WEIGHTS}

    delta, new_m, new_v = {}, {}, {}
    for k in WEIGHTS:
        shape = w[k].shape
        flat = (lambda a: a.reshape(-1, shape[-1])) if len(shape) > 1 else (lambda a: a.reshape(1, -1))
        d, m_new, v_new = _adamw(flat(w[k]), flat(grads[k]), flat(mom[k]), flat(var[k]), name=f"adamw_{k}")
        delta[k], new_m[k], new_v[k] = d.reshape(shape), m_new.reshape(shape), v_new.reshape(shape)

    return (loss, dx.reshape(x.shape), *[grads[k] for k in WEIGHTS], *[delta[k] for k in WEIGHTS],
            *[new_m[k] for k in WEIGHTS], *[new_v[k] for k in WEIGHTS])
```

```python
import math

import jax
import jax.numpy as jnp
from jax import lax
from jax.experimental import pallas as pl
from jax.experimental.pallas import tpu as pltpu

F32 = jnp.float32
BF16 = jnp.bfloat16
SDS = jax.ShapeDtypeStruct

D_MODEL = 1024
SEQ = 2048
CHUNK = 128
N_MEM = 256
D_INNER = 2048
A_GROUPS = 8
A_GROUP_W = D_INNER // A_GROUPS
SSM_HEADS = 32
SSM_HEAD_DIM = 64
SSM_GROUPS = 4
SSM_HPG = 8
SSM_STATE = 128
SSM_GROUP_W = SSM_HPG * SSM_HEAD_DIM
CONV_K = 4
CONV_DIM = 3072
X_HEADS = 4
X_HEAD_DIM = 256
X_WIDTH = 1024
MIX_OUT = 3072
D_FF = 4096
A_IN = 5120
B_IN = 6176
B_IN_PAD = 6272
B_Q_OFF = 5120
B_DT_OFF = 6144
N_CHUNKS = SEQ // CHUNK
EPS = 1e-6
N_CHIPS = 4

ADAM_LR = 0.001
ADAM_B1 = 0.9
ADAM_B2 = 0.999
ADAM_EPS = 1e-08
ADAM_WD = 0.01
ADAM_STEP = 10

VMEM_LIMIT = 48 * 1024 * 1024
MESH = pl.DeviceIdType.MESH


def _cparams(sem):
    return pltpu.CompilerParams(dimension_semantics=sem, vmem_limit_bytes=VMEM_LIMIT)


def _dot(a, b, dims=(((1,), (0,)), ((), ()))):
    return lax.dot_general(a.astype(BF16), b.astype(BF16), dims, preferred_element_type=F32)


def _dot_nt(a, b):
    return _dot(a, b, (((1,), (1,)), ((), ())))


def _dot_tn(a, b):
    return _dot(a, b, (((0,), (0,)), ((), ())))


def _pick(n, cands):
    for c in cands:
        if n % c == 0:
            return c
    raise ValueError(f"no tile for {n}")


def _mm_call(a, b, *, dims, grid, a_spec, b_spec, acc_shape, out_shapes, out_specs, name,
             extras=(), extra_specs=(), epilogue=None):
    n_k = grid[2]
    n_extra = len(extras)
    n_out = len(out_shapes)

    def body(*refs):
        a_ref, b_ref = refs[0], refs[1]
        extra_refs = refs[2:2 + n_extra]
        out_refs = refs[2 + n_extra:2 + n_extra + n_out]
        acc = refs[-1]
        k = pl.program_id(2)

        @pl.when(k == 0)
        def _():
            acc[...] = jnp.zeros_like(acc)

        acc[...] += _dot(a_ref[...], b_ref[...], dims)

        @pl.when(k == n_k - 1)
        def _():
            vals = (acc[...],) if epilogue is None else epilogue(acc[...], *[e[...] for e in extra_refs])
            for o_ref, v in zip(out_refs, vals):
                o_ref[...] = v.astype(o_ref.dtype)

    return pl.pallas_call(
        body, grid=grid, in_specs=[a_spec, b_spec, *extra_specs], out_specs=list(out_specs),
        out_shape=list(out_shapes), scratch_shapes=[pltpu.VMEM(acc_shape, F32)],
        compiler_params=_cparams(("parallel", "parallel", "arbitrary")), name=name,
    )(a, b, *extras)


def _w_dims(w):
    if w.ndim == 2:
        return w.shape[0], w.shape[1], 1, w.shape[1]
    return w.shape[1], w.shape[0] * w.shape[2], w.shape[0], w.shape[2]


def _mm_nn(a, w, *, name, out_dtype=F32, a_cols=None, extras=(), epilogue=None, n_out_dtypes=None):
    m = a.shape[0]
    k_dim, n_dim, _, n_slot = _w_dims(w)
    a_off, a_w = (0, a.shape[1]) if a_cols is None else a_cols
    assert a_w == k_dim
    tm = _pick(m, (1024, 512, 256))
    tn = _pick(n_slot, (512, 896, 640, 256, 128))
    tk = _pick(k_dim, (512, 384, 256, 128))
    assert a_off % tk == 0
    nb = n_slot // tn
    a_spec = pl.BlockSpec((tm, tk), lambda i, j, k: (i, a_off // tk + k))
    if w.ndim == 2:
        b_spec = pl.BlockSpec((tk, tn), lambda i, j, k: (k, j))
    else:
        b_spec = pl.BlockSpec((None, tk, tn), lambda i, j, k: (j // nb, k, j % nb))
    o_spec = pl.BlockSpec((tm, tn), lambda i, j, k: (i, j))
    dts = n_out_dtypes or (out_dtype,)
    outs = _mm_call(a, w, dims=(((1,), (0,)), ((), ())), grid=(m // tm, n_dim // tn, k_dim // tk),
                    a_spec=a_spec, b_spec=b_spec, acc_shape=(tm, tn),
                    out_shapes=[SDS((m, n_dim), dt) for dt in dts], out_specs=[o_spec] * len(dts), name=name,
                    extras=extras, extra_specs=[o_spec] * len(extras), epilogue=epilogue)
    return outs if n_out_dtypes else outs[0]


def _mm_nt(a, w, *, name, out_dtype=F32, extras=(), epilogue=None):
    m = a.shape[0]
    k_dim, n_dim, _, n_slot = _w_dims(w)
    assert a.shape[1] == n_dim
    tm = _pick(m, (1024, 512, 256))
    to = _pick(k_dim, (512, 384, 256, 128))
    tc = _pick(n_slot, (512, 896, 640, 256, 128))
    nb = n_slot // tc
    a_spec = pl.BlockSpec((tm, tc), lambda i, j, k: (i, k))
    if w.ndim == 2:
        b_spec = pl.BlockSpec((to, tc), lambda i, j, k: (j, k))
    else:
        b_spec = pl.BlockSpec((None, to, tc), lambda i, j, k: (k // nb, j, k % nb))
    o_spec = pl.BlockSpec((tm, to), lambda i, j, k: (i, j))
    return _mm_call(a, w, dims=(((1,), (1,)), ((), ())), grid=(m // tm, k_dim // to, n_dim // tc),
                    a_spec=a_spec, b_spec=b_spec, acc_shape=(tm, to),
                    out_shapes=[SDS((m, k_dim), out_dtype)], out_specs=[o_spec], name=name,
                    extras=extras, extra_specs=[o_spec] * len(extras), epilogue=epilogue)[0]


def _mm_tn(x, dy, *, name, x_cols=None):
    s = x.shape[0]
    x_off, k_dim = (0, x.shape[1]) if x_cols is None else x_cols
    n_dim = dy.shape[1]
    tm = _pick(k_dim, (512, 384, 256, 128))
    tn = _pick(n_dim, (512, 896, 640, 256, 128))
    tk = _pick(s, (1024, 512, 256))
    assert x_off % tm == 0
    a_spec = pl.BlockSpec((tk, tm), lambda i, j, k: (k, x_off // tm + i))
    b_spec = pl.BlockSpec((tk, tn), lambda i, j, k: (k, j))
    o_spec = pl.BlockSpec((tm, tn), lambda i, j, k: (i, j))
    return _mm_call(x, dy, dims=(((0,), (0,)), ((), ())), grid=(k_dim // tm, n_dim // tn, s // tk),
                    a_spec=a_spec, b_spec=b_spec, acc_shape=(tm, tn),
                    out_shapes=[SDS((k_dim, n_dim), F32)], out_specs=[o_spec], name=name)[0]


def _mm_tn_stacked(x, dy, *, name, half, col_slots, stack=None, x_cols=None):
    s = x.shape[0]
    x_off, k_dim = (0, x.shape[1]) if x_cols is None else x_cols
    n_dim = dy.shape[1]
    r, c = (k_dim, n_dim // N_CHIPS) if col_slots else (k_dim // N_CHIPS, n_dim)
    tm = _pick(r, (512, 384, 256, 128))
    tn = _pick(c, (512, 896, 640, 256, 128))
    tk = _pick(s, (1024, 512, 256))
    assert x_off % tm == 0
    a_spec = pl.BlockSpec((tk, tm), lambda i, j, k: (k, x_off // tm + i))
    b_spec = pl.BlockSpec((tk, tn), lambda i, j, k: (k, j))
    if col_slots:
        nb = c // tn
        o_spec = pl.BlockSpec((None, None, tm, tn), lambda i, j, k: (half, j // nb, i, j % nb))
    else:
        nb = r // tm
        o_spec = pl.BlockSpec((None, None, tm, tn), lambda i, j, k: (half, i // nb, i % nb, j))
    n_k = s // tk

    def body(a_ref, b_ref, *rest):
        o_ref, acc = rest[-2], rest[-1]
        k = pl.program_id(2)

        @pl.when(k == 0)
        def _():
            acc[...] = jnp.zeros_like(acc)

        acc[...] += _dot_tn(a_ref[...], b_ref[...])

        @pl.when(k == n_k - 1)
        def _():
            o_ref[...] = acc[...]

    keep = [] if stack is None else [stack]
    return pl.pallas_call(
        body, grid=(k_dim // tm, n_dim // tn, n_k), in_specs=[a_spec, b_spec, *([ANY] * len(keep))],
        out_specs=o_spec, out_shape=SDS((2, N_CHIPS, r, c), F32), scratch_shapes=[pltpu.VMEM((tm, tn), F32)],
        input_output_aliases={2: 0} if keep else {},
        compiler_params=_cparams(("parallel", "parallel", "arbitrary")), name=name,
    )(x, dy, *keep)


def _rms(x, g):
    return x * lax.rsqrt(jnp.mean(x * x, axis=-1, keepdims=True) + EPS) * g


def _rms_fwd(h, g, *, name):
    rows, d = h.shape
    tr = _pick(rows, (512, 256))

    def body(h_ref, g_ref, o_ref):
        o_ref[...] = _rms(h_ref[...], g_ref[...]).astype(o_ref.dtype)

    return pl.pallas_call(
        body, grid=(rows // tr,),
        in_specs=[pl.BlockSpec((tr, d), lambda i: (i, 0)), pl.BlockSpec((1, d), lambda i: (0, 0))],
        out_specs=pl.BlockSpec((tr, d), lambda i: (i, 0)), out_shape=SDS((rows, d), BF16),
        compiler_params=_cparams(("parallel",)), name=name)(h, g)


def _rms_bwd(h, g, da, dres, *, name):
    rows, d = h.shape
    tr = _pick(rows, (512, 256))

    def body(h_ref, g_ref, da_ref, dres_ref, dh_ref, dg_ref):
        _, vjp = jax.vjp(_rms, h_ref[...], g_ref[...])
        dh, dg = vjp(da_ref[...].astype(F32))
        dh_ref[...] = dres_ref[...] + dh

        @pl.when(pl.program_id(0) == 0)
        def _():
            dg_ref[...] = jnp.zeros_like(dg_ref)

        dg_ref[...] += dg

    row_spec = pl.BlockSpec((tr, d), lambda i: (i, 0))
    vec_spec = pl.BlockSpec((1, d), lambda i: (0, 0))
    return pl.pallas_call(
        body, grid=(rows // tr,), in_specs=[row_spec, vec_spec, row_spec, row_spec],
        out_specs=[row_spec, vec_spec], out_shape=[SDS((rows, d), F32), SDS((1, d), F32)],
        compiler_params=_cparams(("arbitrary",)), name=name)(h, g, da, dres)


def _loss_head(h, g, target, *, name):
    rows, d = h.shape
    tr = _pick(rows, (512, 256))

    def body(h_ref, g_ref, t_ref, loss_ref, dh_ref, dg_ref):
        y, vjp = jax.vjp(_rms, h_ref[...], g_ref[...])
        err = y - t_ref[...]
        dh, dg = vjp(err * (1.0 / d))
        dh_ref[...] = dh

        @pl.when(pl.program_id(0) == 0)
        def _():
            dg_ref[...] = jnp.zeros_like(dg_ref)
            loss_ref[...] = jnp.zeros_like(loss_ref)

        dg_ref[...] += dg
        part = jnp.sum(jnp.sum(err * err, axis=-1, keepdims=True), axis=0, keepdims=True) * (0.5 / d)
        loss_ref[...] += jnp.broadcast_to(part, loss_ref.shape)

    row_spec = pl.BlockSpec((tr, d), lambda i: (i, 0))
    vec_spec = pl.BlockSpec((1, d), lambda i: (0, 0))
    loss_spec = pl.BlockSpec((8, 128), lambda i: (0, 0))
    return pl.pallas_call(
        body, grid=(rows // tr,), in_specs=[row_spec, vec_spec, row_spec],
        out_specs=[loss_spec, row_spec, vec_spec],
        out_shape=[SDS((8, 128), F32), SDS((rows, d), F32), SDS((1, d), F32)],
        compiler_params=_cparams(("arbitrary",)), name=name)(h, g, target)


def _gelu(x):
    return 0.5 * x * (1.0 + lax.erf(x * (1.0 / math.sqrt(2.0))))


def _gate_tile(pu, pv, ln_g, ln_b, ws, bs_t):
    u = [_gelu(p) for p in pu]
    v = [_gelu(p) for p in pv]
    mu = sum(jnp.sum(t, axis=-1, keepdims=True) for t in v) * (1.0 / D_INNER)
    vc = [t - mu for t in v]
    var = sum(jnp.sum(t * t, axis=-1, keepdims=True) for t in vc) * (1.0 / D_INNER)
    rstd = lax.rsqrt(var + EPS)
    row = lax.broadcasted_iota(jnp.int32, (CHUNK, CHUNK), 0)
    col = lax.broadcasted_iota(jnp.int32, (CHUNK, CHUNK), 1)
    out = []
    for gi in range(A_GROUPS):
        vn = vc[gi] * rstd * ln_g[gi] + ln_b[gi]
        w = jnp.where(row >= col, ws[gi], 0.0)
        sv = _dot(w, vn) + bs_t[gi]
        out.append(u[gi] * sv)
    return out


def _split(ref, n, width):
    return [ref[:, i * width:(i + 1) * width] for i in range(n)]


def _gate_in_specs():
    return [
        pl.BlockSpec((CHUNK, D_INNER), lambda c: (c, 0)),
        pl.BlockSpec((CHUNK, D_INNER), lambda c: (c, 1)),
        pl.BlockSpec((1, D_INNER), lambda c: (0, 0)),
        pl.BlockSpec((1, D_INNER), lambda c: (0, 0)),
        pl.BlockSpec((A_GROUPS, CHUNK, CHUNK), lambda c: (0, 0, 0)),
        pl.BlockSpec((A_GROUPS, CHUNK, 1), lambda c: (0, 0, 0)),
    ]


def _gate_args(u_ref, v_ref, g_ref, b_ref, ws_ref, bs_ref):
    ng, gw = A_GROUPS, A_GROUP_W
    return (_split(u_ref, ng, gw), _split(v_ref, ng, gw), _split(g_ref, ng, gw), _split(b_ref, ng, gw),
            [ws_ref[i] for i in range(ng)], [bs_ref[i] for i in range(ng)])


def _gate_fwd(proj, ln_g, ln_b, ws, bs_col, mixcat, *, name):
    def body(u_ref, v_ref, g_ref, b_ref, ws_ref, bs_ref, cat_in, cat_ref):
        del cat_in
        out = _gate_tile(*_gate_args(u_ref, v_ref, g_ref, b_ref, ws_ref, bs_ref))
        for gi, o in enumerate(out):
            cat_ref[:, gi * A_GROUP_W:(gi + 1) * A_GROUP_W] = o.astype(cat_ref.dtype)

    return pl.pallas_call(
        body, grid=(N_CHUNKS,), in_specs=[*_gate_in_specs(), pl.BlockSpec(memory_space=pl.ANY)],
        out_specs=pl.BlockSpec((CHUNK, D_INNER), lambda c: (c, 0)), out_shape=SDS(mixcat.shape, mixcat.dtype),
        input_output_aliases={6: 0}, compiler_params=_cparams(("parallel",)), name=name,
    )(proj, proj, ln_g, ln_b, ws, bs_col, mixcat)


def _gate_bwd(proj, ln_g, ln_b, ws, bs_col, dcat, dproj, *, name):
    ng, gw = A_GROUPS, A_GROUP_W

    def body(u_ref, v_ref, g_ref, b_ref, ws_ref, bs_ref, d_ref, dproj_in, dproj_ref, dg_ref, db_ref, dws_ref, dbs_ref):
        del dproj_in
        args = _gate_args(u_ref, v_ref, g_ref, b_ref, ws_ref, bs_ref)
        _, vjp = jax.vjp(_gate_tile, *args)
        dpu, dpv, dg, db, dws, dbs = vjp(_split(d_ref, ng, gw))
        for gi in range(ng):
            dproj_ref[:, gi * gw:(gi + 1) * gw] = dpu[gi].astype(dproj_ref.dtype)
            dproj_ref[:, D_INNER + gi * gw:D_INNER + (gi + 1) * gw] = dpv[gi].astype(dproj_ref.dtype)

        @pl.when(pl.program_id(0) == 0)
        def _():
            for r in (dg_ref, db_ref, dws_ref, dbs_ref):
                r[...] = jnp.zeros_like(r)

        for gi in range(ng):
            dg_ref[:, gi * gw:(gi + 1) * gw] += dg[gi]
            db_ref[:, gi * gw:(gi + 1) * gw] += db[gi]
            dws_ref[gi] += dws[gi]
            dbs_ref[gi] += dbs[gi]

    in_specs = _gate_in_specs()
    return pl.pallas_call(
        body, grid=(N_CHUNKS,),
        in_specs=[*in_specs, pl.BlockSpec((CHUNK, D_INNER), lambda c: (c, 0)), pl.BlockSpec(memory_space=pl.ANY)],
        out_specs=[pl.BlockSpec((CHUNK, 2 * D_INNER), lambda c: (c, 0)), *in_specs[2:]],
        out_shape=[SDS(dproj.shape, dproj.dtype), SDS((1, D_INNER), F32), SDS((1, D_INNER), F32),
                   SDS((ng, CHUNK, CHUNK), F32), SDS((ng, CHUNK, 1), F32)],
        input_output_aliases={7: 0}, compiler_params=_cparams(("arbitrary",)), name=name,
    )(proj, proj, ln_g, ln_b, ws, bs_col, dcat, dproj)


ATT_TQ = 512


def _attn_tile(q, k, v):
    s = _dot_nt(q, k) * (1.0 / math.sqrt(X_HEAD_DIM))
    s = s - jnp.max(s, axis=-1, keepdims=True)
    e = jnp.exp(s)
    p = e / jnp.sum(e, axis=-1, keepdims=True)
    return _dot(p, v)


def _attn_in_specs(q_blk, order):
    hd = X_HEAD_DIM
    return [
        pl.BlockSpec((ATT_TQ, hd), lambda a, b: (order(a, b)[0], q_blk + order(a, b)[1])),
        pl.BlockSpec((N_MEM, hd), lambda a, b: (0, order(a, b)[1])),
        pl.BlockSpec((N_MEM, hd), lambda a, b: (0, X_HEADS + order(a, b)[1])),
    ]


def _attn_fwd(proj, q_off, kv, *, name):
    order = lambda i, h: (i, h)
    cat_blk = D_INNER // X_HEAD_DIM

    def body(q_ref, k_ref, v_ref, o_ref):
        o_ref[...] = _attn_tile(q_ref[...], k_ref[...], v_ref[...]).astype(o_ref.dtype)

    return pl.pallas_call(
        body, grid=(SEQ // ATT_TQ, X_HEADS), in_specs=_attn_in_specs(q_off // X_HEAD_DIM, order),
        out_specs=pl.BlockSpec((ATT_TQ, X_HEAD_DIM), lambda i, h: (i, cat_blk + h)),
        out_shape=SDS((SEQ, MIX_OUT), BF16), compiler_params=_cparams(("parallel", "parallel")), name=name,
    )(proj, kv, kv)


def _attn_bwd(proj, q_off, kv, dcat, dproj_width, dq_off, *, name):
    order = lambda h, i: (i, h)
    cat_blk = D_INNER // X_HEAD_DIM
    dq_blk = dq_off // X_HEAD_DIM

    def body(q_ref, k_ref, v_ref, do_ref, dq_ref, dk_ref, dv_ref):
        _, vjp = jax.vjp(_attn_tile, q_ref[...], k_ref[...], v_ref[...])
        dq, dk, dv = vjp(do_ref[...])
        dq_ref[...] = dq.astype(dq_ref.dtype)

        @pl.when(pl.program_id(1) == 0)
        def _():
            dk_ref[...] = jnp.zeros_like(dk_ref)
            dv_ref[...] = jnp.zeros_like(dv_ref)

        dk_ref[...] += dk
        dv_ref[...] += dv

    kv_spec = pl.BlockSpec((N_MEM, X_HEAD_DIM), lambda h, i: (0, h))
    return pl.pallas_call(
        body, grid=(X_HEADS, SEQ // ATT_TQ),
        in_specs=[*_attn_in_specs(q_off // X_HEAD_DIM, order),
                  pl.BlockSpec((ATT_TQ, X_HEAD_DIM), lambda h, i: (i, cat_blk + h))],
        out_specs=[pl.BlockSpec((ATT_TQ, X_HEAD_DIM), lambda h, i: (i, dq_blk + h)), kv_spec, kv_spec],
        out_shape=[SDS((SEQ, dproj_width), BF16), SDS((N_MEM, X_WIDTH), F32), SDS((N_MEM, X_WIDTH), F32)],
        compiler_params=_cparams(("parallel", "arbitrary")), name=name,
    )(proj, kv, kv, dcat)


CONV_TC = 512


def _shift_down(x, s):
    if s == 0:
        return x
    row = lax.broadcasted_iota(jnp.int32, x.shape, 0)
    return jnp.where(row >= s, pltpu.roll(x, s, 0), 0.0)


def _shift_up(x, s):
    if s == 0:
        return x
    n = x.shape[0]
    row = lax.broadcasted_iota(jnp.int32, x.shape, 0)
    return jnp.where(row < n - s, pltpu.roll(x, n - s, 0), 0.0)


def _conv_pre(x, w_ref, b_ref):
    pre = b_ref[...] + jnp.zeros_like(x)
    for k in range(CONV_K):
        pre = pre + w_ref[k:k + 1, :] * _shift_down(x, CONV_K - 1 - k)
    return pre


def _conv_fwd(proj, w, b, *, name):
    blk0 = D_INNER // CONV_TC

    def body(x_ref, w_ref, b_ref, o_ref):
        pre = _conv_pre(x_ref[...], w_ref, b_ref)
        o_ref[...] = pre * jax.nn.sigmoid(pre)

    return pl.pallas_call(
        body, grid=(CONV_DIM // CONV_TC,),
        in_specs=[pl.BlockSpec((SEQ, CONV_TC), lambda j: (0, blk0 + j)), pl.BlockSpec((CONV_K, CONV_TC), lambda j: (0, j)),
                  pl.BlockSpec((1, CONV_TC), lambda j: (0, j))],
        out_specs=pl.BlockSpec((SEQ, CONV_TC), lambda j: (0, j)), out_shape=SDS((SEQ, CONV_DIM), F32),
        compiler_params=_cparams(("parallel",)), name=name)(proj, w, b)


def _conv_bwd(proj, w, b, dxs, dbm, dcm, dproj, *, name):
    tc = CONV_TC // 2
    blk0 = D_INNER // tc
    n_x = D_INNER // tc
    n_b = SSM_GROUPS * SSM_STATE // tc

    def body(x_ref, w_ref, b_ref, dxs_ref, dbm_ref, dcm_ref, dproj_in, dproj_ref, dw_ref, db_ref):
        del dproj_in
        j = pl.program_id(0)
        x = x_ref[...]
        pre = _conv_pre(x, w_ref, b_ref)
        sg = jax.nn.sigmoid(pre)
        dact = jnp.where(j < n_x, dxs_ref[...], jnp.where(j < n_x + n_b, dbm_ref[...], dcm_ref[...]))
        dpre = dact * (sg * (1.0 + pre * (1.0 - sg)))
        dx = jnp.zeros_like(x)
        for k in range(CONV_K):
            s = CONV_K - 1 - k
            dx = dx + w_ref[k:k + 1, :] * _shift_up(dpre, s)
            dw_ref[k:k + 1, :] = jnp.sum(dpre * _shift_down(x, s), axis=0, keepdims=True)
        dproj_ref[...] = dx.astype(dproj_ref.dtype)
        db_ref[...] = jnp.sum(dpre, axis=0, keepdims=True)

    clip = lambda v, hi: jnp.minimum(jnp.maximum(v, 0), hi)
    return pl.pallas_call(
        body, grid=(CONV_DIM // tc,),
        in_specs=[pl.BlockSpec((SEQ, tc), lambda j: (0, blk0 + j)), pl.BlockSpec((CONV_K, tc), lambda j: (0, j)),
                  pl.BlockSpec((1, tc), lambda j: (0, j)),
                  pl.BlockSpec((SEQ, tc), lambda j: (0, clip(j, n_x - 1))),
                  pl.BlockSpec((SEQ, tc), lambda j: (0, clip(j - n_x, n_b - 1))),
                  pl.BlockSpec((SEQ, tc), lambda j: (0, clip(j - n_x - n_b, n_b - 1))),
                  pl.BlockSpec(memory_space=pl.ANY)],
        out_specs=[pl.BlockSpec((SEQ, tc), lambda j: (0, blk0 + j)), pl.BlockSpec((CONV_K, tc), lambda j: (0, j)),
                   pl.BlockSpec((1, tc), lambda j: (0, j))],
        out_shape=[SDS(dproj.shape, dproj.dtype), SDS((CONV_K, CONV_DIM), F32), SDS((1, CONV_DIM), F32)],
        input_output_aliases={6: 0}, compiler_params=_cparams(("parallel",)), name=name,
    )(proj, w, b, dxs, dbm, dcm, dproj)


def _ssd_tile(xs, zs, bm, cm, hs, dtc, dtr, bias, alog, dsk, gn):
    row = lax.broadcasted_iota(jnp.int32, (CHUNK, CHUNK), 0)
    col = lax.broadcasted_iota(jnp.int32, (CHUNK, CHUNK), 1)
    causal = row >= col
    tri = jnp.where(causal, 1.0, 0.0)
    cb = _dot_nt(cm, bm)
    ygs, hn = [], []
    for r in range(SSM_HPG):
        a = -jnp.exp(alog[r])
        da_c = jax.nn.softplus(dtc[r] + bias[r]) * a
        da_r = jax.nn.softplus(dtr[r] + bias[r]) * a
        dt_c = jax.nn.softplus(dtc[r] + bias[r])
        cs_c = jnp.sum(tri * da_r, axis=1, keepdims=True)
        cs_r = jnp.sum(jnp.where(row <= col, 1.0, 0.0) * da_c, axis=0, keepdims=True)
        cs_last = jnp.sum(da_c, axis=0, keepdims=True)
        lm = jnp.exp(jnp.where(causal, cs_c - cs_r, -1e30))
        xdt = xs[r] * dt_c
        y = _dot(cb * lm, xdt)
        y = y + _dot_nt(cm, hs[r]) * jnp.exp(cs_c)
        y = y + xs[r] * dsk[r]
        states = _dot_tn(xdt * jnp.exp(cs_last - cs_c), bm)
        hn.append(hs[r] * jnp.exp(cs_last) + states)
        ygs.append(y * (zs[r] * jax.nn.sigmoid(zs[r])))
    ms = sum(jnp.sum(t * t, axis=-1, keepdims=True) for t in ygs) * (1.0 / SSM_GROUP_W)
    rs = lax.rsqrt(ms + EPS)
    return [ygs[r] * rs * gn[r] for r in range(SSM_HPG)], hn


def _ssd_in_specs(cidx):
    gw, n = SSM_GROUP_W, SSM_STATE
    bm_blk = D_INNER // n
    return [
        pl.BlockSpec((CHUNK, gw), lambda g, c: (cidx(c), g)),
        pl.BlockSpec((CHUNK, gw), lambda g, c: (cidx(c), g)),
        pl.BlockSpec((CHUNK, n), lambda g, c: (cidx(c), bm_blk + g)),
        pl.BlockSpec((CHUNK, n), lambda g, c: (cidx(c), bm_blk + SSM_GROUPS + g)),
        pl.BlockSpec((None, CHUNK, SSM_HPG), lambda g, c: (g, cidx(c), 0)),
        pl.BlockSpec((None, SSM_HPG, CHUNK), lambda g, c: (g, 0, cidx(c))),
        pl.BlockSpec((None, 1, SSM_HPG), lambda g, c: (g, 0, 0)),
        pl.BlockSpec((None, 1, SSM_HPG), lambda g, c: (g, 0, 0)),
        pl.BlockSpec((None, 1, SSM_HPG), lambda g, c: (g, 0, 0)),
        pl.BlockSpec((1, gw), lambda g, c: (0, g)),
    ]


def _ssd_args(x_ref, z_ref, bm_ref, cm_ref, hs, dtc_ref, dtr_ref, bias_ref, alog_ref, dsk_ref, gn_ref):
    nh, p = SSM_HPG, SSM_HEAD_DIM
    col = lambda ref: [ref[:, r:r + 1] for r in range(nh)]
    return (_split(x_ref, nh, p), _split(z_ref, nh, p), bm_ref[...], cm_ref[...], hs,
            col(dtc_ref), [dtr_ref[r:r + 1, :] for r in range(nh)], col(bias_ref), col(alog_ref), col(dsk_ref),
            _split(gn_ref, nh, p))


def _ssd_fwd(xbc, proj, dt_c, dt_r, bias, alog, dsk, gn, mixcat, *, name):
    nh = SSM_HPG

    def body(x_ref, z_ref, bm_ref, cm_ref, dtc_ref, dtr_ref, bias_ref, alog_ref, dsk_ref, gn_ref, cat_in,
             cat_ref, hprev_ref, h_scr):
        del cat_in

        @pl.when(pl.program_id(1) == 0)
        def _():
            h_scr[...] = jnp.zeros_like(h_scr)

        hs = [h_scr[r] for r in range(nh)]
        for r in range(nh):
            hprev_ref[r] = hs[r]
        yn, hn = _ssd_tile(*_ssd_args(x_ref, z_ref, bm_ref, cm_ref, hs, dtc_ref, dtr_ref, bias_ref, alog_ref,
                                      dsk_ref, gn_ref))
        for r in range(nh):
            cat_ref[:, r * SSM_HEAD_DIM:(r + 1) * SSM_HEAD_DIM] = yn[r].astype(cat_ref.dtype)
            h_scr[r] = hn[r]

    return pl.pallas_call(
        body, grid=(SSM_GROUPS, N_CHUNKS), in_specs=[*_ssd_in_specs(lambda c: c), pl.BlockSpec(memory_space=pl.ANY)],
        out_specs=[pl.BlockSpec((CHUNK, SSM_GROUP_W), lambda g, c: (c, g)),
                   pl.BlockSpec((None, nh, SSM_HEAD_DIM, SSM_STATE), lambda g, c: (c, g, 0, 0))],
        out_shape=[SDS(mixcat.shape, mixcat.dtype), SDS((N_CHUNKS, SSM_HEADS, SSM_HEAD_DIM, SSM_STATE), F32)],
        scratch_shapes=[pltpu.VMEM((nh, SSM_HEAD_DIM, SSM_STATE), F32)],
        input_output_aliases={10: 0}, compiler_params=_cparams(("parallel", "arbitrary")), name=name,
    )(xbc, proj, xbc, xbc, dt_c, dt_r, bias, alog, dsk, gn, mixcat)


def _ssd_bwd(xbc, proj, dt_c, dt_r, bias, alog, dsk, gn, hprev, dcat, dproj, *, name):
    nh, p, gw, n = SSM_HPG, SSM_HEAD_DIM, SSM_GROUP_W, SSM_STATE
    rev = lambda c: N_CHUNKS - 1 - c

    def body(x_ref, z_ref, bm_ref, cm_ref, dtc_ref, dtr_ref, bias_ref, alog_ref, dsk_ref, gn_ref, hprev_ref, dy_ref,
             dproj_in, dz_ref, dxs_ref, dbm_ref, dcm_ref, ddtc_ref, ddtr_ref, dbias_ref, dalog_ref, ddsk_ref, dgn_ref,
             dh_scr):
        del dproj_in
        first = pl.program_id(1) == 0

        @pl.when(first)
        def _():
            dh_scr[...] = jnp.zeros_like(dh_scr)
            for ref in (dbias_ref, dalog_ref, ddsk_ref, dgn_ref):
                ref[...] = jnp.zeros_like(ref)

        hs = [hprev_ref[r] for r in range(nh)]
        args = _ssd_args(x_ref, z_ref, bm_ref, cm_ref, hs, dtc_ref, dtr_ref, bias_ref, alog_ref, dsk_ref, gn_ref)
        _, vjp = jax.vjp(_ssd_tile, *args)
        dxs, dzs, dbm, dcm, dhs, ddtc, ddtr, dbias, dalog, ddsk, dgn = vjp(
            (_split(dy_ref, nh, p), [dh_scr[r] for r in range(nh)]))
        dbm_ref[...] = dbm
        dcm_ref[...] = dcm
        for r in range(nh):
            dxs_ref[:, r * p:(r + 1) * p] = dxs[r]
            dz_ref[:, r * p:(r + 1) * p] = dzs[r].astype(dz_ref.dtype)
            dh_scr[r] = dhs[r]
            ddtc_ref[:, r:r + 1] = ddtc[r]
            ddtr_ref[r:r + 1, :] = ddtr[r]
            dbias_ref[:, r:r + 1] += dbias[r]
            dalog_ref[:, r:r + 1] += dalog[r]
            ddsk_ref[:, r:r + 1] += ddsk[r]
            dgn_ref[:, r * p:(r + 1) * p] += dgn[r]

    par_spec = pl.BlockSpec((None, 1, nh), lambda g, c: (g, 0, 0))
    return pl.pallas_call(
        body, grid=(SSM_GROUPS, N_CHUNKS),
        in_specs=[*_ssd_in_specs(rev),
                  pl.BlockSpec((None, nh, p, n), lambda g, c: (rev(c), g, 0, 0)),
                  pl.BlockSpec((CHUNK, gw), lambda g, c: (rev(c), g)),
                  pl.BlockSpec(memory_space=pl.ANY)],
        out_specs=[pl.BlockSpec((CHUNK, gw), lambda g, c: (rev(c), g)),
                   pl.BlockSpec((CHUNK, gw), lambda g, c: (rev(c), g)),
                   pl.BlockSpec((CHUNK, n), lambda g, c: (rev(c), g)),
                   pl.BlockSpec((CHUNK, n), lambda g, c: (rev(c), g)),
                   pl.BlockSpec((None, CHUNK, nh), lambda g, c: (g, rev(c), 0)),
                   pl.BlockSpec((None, nh, CHUNK), lambda g, c: (g, 0, rev(c))),
                   par_spec, par_spec, par_spec,
                   pl.BlockSpec((1, gw), lambda g, c: (0, g))],
        out_shape=[SDS(dproj.shape, dproj.dtype), SDS((SEQ, D_INNER), F32), SDS((SEQ, SSM_GROUPS * n), F32),
                   SDS((SEQ, SSM_GROUPS * n), F32), SDS((SSM_GROUPS, SEQ, nh), F32), SDS((SSM_GROUPS, nh, SEQ), F32),
                   SDS((SSM_GROUPS, 1, nh), F32), SDS((SSM_GROUPS, 1, nh), F32), SDS((SSM_GROUPS, 1, nh), F32),
                   SDS((1, D_INNER), F32)],
        scratch_shapes=[pltpu.VMEM((nh, p, n), F32)],
        input_output_aliases={12: 0}, compiler_params=_cparams(("parallel", "arbitrary")), name=name,
    )(xbc, proj, xbc, xbc, dt_c, dt_r, bias, alog, dsk, gn, hprev, dcat, dproj)


def _sum_slots(p, *, name):
    n, r, c = p.shape
    tr = _pick(r, (256, 384, 128, 8))

    def body(p_ref, o_ref):
        acc = p_ref[0].astype(F32)
        for s in range(1, n):
            acc = acc + p_ref[s].astype(F32)
        o_ref[...] = acc

    return pl.pallas_call(body, grid=(r // tr,), in_specs=[pl.BlockSpec((n, tr, c), lambda i: (0, i, 0))],
                          out_specs=pl.BlockSpec((tr, c), lambda i: (i, 0)), out_shape=SDS((r, c), F32),
                          compiler_params=_cparams(("parallel",)), name=name)(p)


def _adamw(w, g, m, v, *, name):
    r, c = w.shape
    tr = r if r <= 256 else _pick(r, (256, 128, 8))
    spec = pl.BlockSpec((tr, c), lambda i: (i, 0))

    def body(w_ref, g_ref, m_ref, v_ref, d_ref, mo_ref, vo_ref):
        g = g_ref[...]
        m_new = ADAM_B1 * m_ref[...] + (1.0 - ADAM_B1) * g
        v_new = ADAM_B2 * v_ref[...] + (1.0 - ADAM_B2) * (g * g)
        m_hat = m_new / (1.0 - ADAM_B1 ** ADAM_STEP)
        v_hat = v_new / (1.0 - ADAM_B2 ** ADAM_STEP)
        d_ref[...] = -ADAM_LR * (m_hat / (jnp.sqrt(v_hat) + ADAM_EPS) + ADAM_WD * w_ref[...])
        mo_ref[...] = m_new
        vo_ref[...] = v_new

    return pl.pallas_call(body, grid=(r // tr,), in_specs=[spec] * 4, out_specs=[spec] * 3,
                          out_shape=[SDS((r, c), F32)] * 3, compiler_params=_cparams(("parallel",)), name=name)(w, g, m, v)


ANY = pl.BlockSpec(memory_space=pl.ANY)


def _place():
    x, y, c = lax.axis_index("x"), lax.axis_index("y"), lax.axis_index("c")
    chips = [(1 - x, y), (x, 1 - y), (1 - x, 1 - y)]
    return x, y, c, chips


def _remote(src, dst, send_sem, recv_sem, to):
    return pltpu.make_async_remote_copy(src_ref=src, dst_ref=dst, send_sem=send_sem, recv_sem=recv_sem,
                                        device_id=to, device_id_type=MESH)


def _all_gather_shards(shards, *, name):
    n = len(shards)

    def body(*refs):
        ins, outs = refs[:n], refs[n:2 * n]
        send_sems, recv_sems, local_sems = refs[2 * n:]
        x, y, c, chips = _place()
        me = 2 * x + y
        sibling = (x, y, 1 - c)
        local = [pltpu.make_async_copy(ins[t], outs[t].at[me], local_sems.at[t]) for t in range(n)]
        for cp in local:
            cp.start()
        first = [[_remote(ins[t].at[c], outs[t].at[me, c], send_sems.at[6 * t + j], recv_sems.at[6 * t + j], (*chip, c))
                  for j, chip in enumerate(chips)] for t in range(n)]
        for t in range(n):
            for cp in first[t]:
                cp.start()
        passed = []
        for j, (cx, cy) in enumerate(chips):
            k = 2 * cx + cy
            for t in range(n):
                landed = outs[t].at[k, c]
                _remote(landed, landed, send_sems.at[6 * t + j], recv_sems.at[6 * t + j], (cx, cy, c)).wait_recv()
                fwd = _remote(landed, landed, send_sems.at[6 * t + 3 + j], recv_sems.at[6 * t + 3 + j], sibling)
                fwd.start()
                passed.append(fwd)
        for j, (cx, cy) in enumerate(chips):
            k = 2 * cx + cy
            for t in range(n):
                got = outs[t].at[k, 1 - c]
                _remote(got, got, send_sems.at[6 * t + 3 + j], recv_sems.at[6 * t + 3 + j], sibling).wait_recv()
        for t in range(n):
            for cp in first[t]:
                cp.wait_send()
        for cp in passed:
            cp.wait_send()
        for cp in local:
            cp.wait()

    return pl.pallas_call(
        body, in_specs=[ANY] * n, out_specs=[ANY] * n,
        out_shape=[SDS((N_CHIPS, *s.shape), s.dtype) for s in shards],
        scratch_shapes=[pltpu.SemaphoreType.DMA((6 * n,)), pltpu.SemaphoreType.DMA((6 * n,)),
                        pltpu.SemaphoreType.DMA((n,))],
        name=name)(*shards)


STREAM_ROWS = 128


def _stream_rows(i):
    return pl.ds(pl.multiple_of(i * STREAM_ROWS, STREAM_ROWS), STREAM_ROWS)


def _pair_reduce(stacks, *, name):
    n = len(stacks)
    per = 10

    def body(*refs):
        ins, outs, scr = refs[:n], refs[n:2 * n], refs[2 * n:]
        x, y, c, _ = _place()
        sibling = (x, y, 1 - c)
        for t in range(n):
            sbuf, rbuf, obuf, pbuf, ld_s, ld_o, snd, rcv, st, credit = scr[per * t:per * (t + 1)]
            steps = ins[t].shape[1] // STREAM_ROWS
            src, own, out = ins[t].at[1 - c], ins[t].at[c], outs[t]

            def loads(i, slot, src=src, own=own, sbuf=sbuf, obuf=obuf, ld_s=ld_s, ld_o=ld_o):
                return (pltpu.make_async_copy(src.at[_stream_rows(i)], sbuf.at[slot], ld_s.at[slot]),
                        pltpu.make_async_copy(own.at[_stream_rows(i)], obuf.at[slot], ld_o.at[slot]))

            def push(slot, sbuf=sbuf, rbuf=rbuf, snd=snd, rcv=rcv):
                return _remote(sbuf.at[slot], rbuf.at[slot], snd.at[slot], rcv.at[slot], sibling)

            def store(i, slot, pbuf=pbuf, out=out, st=st):
                return pltpu.make_async_copy(pbuf.at[slot], out.at[_stream_rows(i)], st.at[slot])

            for cp in loads(0, 0):
                cp.start()

            def step(i, carry, loads=loads, push=push, store=store, rbuf=rbuf, obuf=obuf, pbuf=pbuf, credit=credit,
                     steps=steps):
                slot = lax.rem(i, 2)
                nxt = 1 - slot

                @pl.when(i + 1 < steps)
                def _():
                    @pl.when(i >= 1)
                    def _():
                        push(nxt).wait_send()
                    for cp in loads(i + 1, nxt):
                        cp.start()

                load_s, load_o = loads(i, slot)
                load_s.wait()

                @pl.when(i >= 2)
                def _():
                    pl.semaphore_wait(credit.at[slot], 1)

                push(slot).start()
                load_o.wait()
                push(slot).wait_recv()

                @pl.when(i >= 2)
                def _():
                    store(i, slot).wait()

                pbuf[slot] = (obuf[slot] + rbuf[slot]).astype(pbuf.dtype)
                store(i, slot).start()

                @pl.when(i + 2 < steps)
                def _():
                    pl.semaphore_signal(credit.at[slot], 1, device_id=sibling, device_id_type=MESH)
                return carry

            lax.fori_loop(0, steps, step, 0)
            for slot in range(2):
                push(slot).wait_send()
                store(0, slot).wait()

    scratch = []
    for s in stacks:
        buf = (2, STREAM_ROWS, s.shape[2])
        scratch += [pltpu.VMEM(buf, F32), pltpu.VMEM(buf, F32), pltpu.VMEM(buf, F32), pltpu.VMEM(buf, BF16),
                    *([pltpu.SemaphoreType.DMA((2,))] * 5), pltpu.SemaphoreType.REGULAR((2,))]
    return pl.pallas_call(
        body, in_specs=[ANY] * n, out_specs=[ANY] * n, out_shape=[SDS(s.shape[1:], BF16) for s in stacks],
        scratch_shapes=scratch, compiler_params=pltpu.CompilerParams(vmem_limit_bytes=VMEM_LIMIT), name=name)(*stacks)


def _chip_scatter(parts, *, name):
    n = len(parts)

    def body(*refs):
        ins, outs = refs[:n], refs[n:2 * n]
        send_sems, recv_sems, local_sems = refs[2 * n:]
        x, y, c, chips = _place()
        me = 2 * x + y
        local = [pltpu.make_async_copy(ins[t].at[me], outs[t].at[me], local_sems.at[t]) for t in range(n)]
        for cp in local:
            cp.start()
        sends = []
        for t in range(n):
            for j, (cx, cy) in enumerate(chips):
                cp = _remote(ins[t].at[2 * cx + cy], outs[t].at[me], send_sems.at[3 * t + j], recv_sems.at[3 * t + j],
                             (cx, cy, c))
                cp.start()
                sends.append(cp)
        for t in range(n):
            for j, (cx, cy) in enumerate(chips):
                landed = outs[t].at[2 * cx + cy]
                _remote(landed, landed, send_sems.at[3 * t + j], recv_sems.at[3 * t + j], (cx, cy, c)).wait_recv()
        for cp in sends:
            cp.wait_send()
        for cp in local:
            cp.wait()

    return pl.pallas_call(
        body, in_specs=[ANY] * n, out_specs=[ANY] * n, out_shape=[SDS(p.shape, p.dtype) for p in parts],
        scratch_shapes=[pltpu.SemaphoreType.DMA((3 * n,)), pltpu.SemaphoreType.DMA((3 * n,)),
                        pltpu.SemaphoreType.DMA((n,))],
        name=name)(*parts)


def _pair_share(finals, *, name):
    n = len(finals)
    per = 7

    def body(*refs):
        ins, outs, scr = refs[:n], refs[n:2 * n], refs[2 * n:-1]
        local_sems = refs[-1]
        x, y, c, _ = _place()
        sibling = (x, y, 1 - c)
        local = [pltpu.make_async_copy(ins[t], outs[t].at[c], local_sems.at[t]) for t in range(n)]
        for cp in local:
            cp.start()
        for t in range(n):
            sbuf, rbuf, ld, snd, rcv, st, credit = scr[per * t:per * (t + 1)]
            steps = ins[t].shape[0] // STREAM_ROWS
            src, dst = ins[t], outs[t].at[1 - c]

            def load(i, slot, src=src, sbuf=sbuf, ld=ld):
                return pltpu.make_async_copy(src.at[_stream_rows(i)], sbuf.at[slot], ld.at[slot])

            def push(slot, sbuf=sbuf, rbuf=rbuf, snd=snd, rcv=rcv):
                return _remote(sbuf.at[slot], rbuf.at[slot], snd.at[slot], rcv.at[slot], sibling)

            def store(i, slot, rbuf=rbuf, dst=dst, st=st):
                return pltpu.make_async_copy(rbuf.at[slot], dst.at[_stream_rows(i)], st.at[slot])

            load(0, 0).start()

            def step(i, carry, load=load, push=push, store=store, credit=credit, steps=steps):
                slot = lax.rem(i, 2)
                nxt = 1 - slot

                @pl.when(i + 1 < steps)
                def _():
                    @pl.when(i >= 1)
                    def _():
                        push(nxt).wait_send()
                    load(i + 1, nxt).start()

                load(i, slot).wait()

                @pl.when(i >= 2)
                def _():
                    pl.semaphore_wait(credit.at[slot], 1)

                push(slot).start()
                push(slot).wait_recv()
                store(i, slot).start()
                store(i, slot).wait()

                @pl.when(i + 2 < steps)
                def _():
                    pl.semaphore_signal(credit.at[slot], 1, device_id=sibling, device_id_type=MESH)
                return carry

            lax.fori_loop(0, steps, step, 0)
            for slot in range(2):
                push(slot).wait_send()
        for cp in local:
            cp.wait()

    scratch = []
    for f in finals:
        buf = (2, STREAM_ROWS, f.shape[1])
        scratch += [pltpu.VMEM(buf, F32), pltpu.VMEM(buf, F32), *([pltpu.SemaphoreType.DMA((2,))] * 4),
                    pltpu.SemaphoreType.REGULAR((2,))]
    return pl.pallas_call(
        body, in_specs=[ANY] * n, out_specs=[ANY] * n, out_shape=[SDS((2, *f.shape), f.dtype) for f in finals],
        scratch_shapes=[*scratch, pltpu.SemaphoreType.DMA((n,))],
        compiler_params=pltpu.CompilerParams(vmem_limit_bytes=VMEM_LIMIT), name=name)(*finals)


def _all_reduce_small(v, *, name):
    rows, lanes = v.shape
    n_dev = 8

    def body(v_ref, o_ref, all_ref, send_sems, recv_sems, local_sem):
        x, y, c, chips = _place()
        me, sibling = (x, y, c), (x, y, 1 - c)

        def block(px, py, pc):
            return all_ref.at[4 * px + 2 * py + pc]

        def copy(k, blk, to, src=None):
            return _remote(block(*blk) if src is None else src, block(*blk), send_sems.at[k], recv_sems.at[k], to)

        mine = pltpu.make_async_copy(v_ref, block(*me), local_sem)
        mine.start()
        first = [copy(0, me, sibling, src=v_ref)]
        first += [copy(1 + j, me, (*chip, c), src=v_ref) for j, chip in enumerate(chips)]
        for cp in first:
            cp.start()
        passed = [copy(4 + j, (*chip, c), sibling) for j, chip in enumerate(chips)]
        for j, chip in enumerate(chips):
            copy(1 + j, (*chip, c), me).wait_recv()
            passed[j].start()
        copy(0, sibling, me).wait_recv()
        for j, chip in enumerate(chips):
            copy(4 + j, (*chip, 1 - c), me).wait_recv()
        for cp in first + passed:
            cp.wait_send()
        mine.wait()
        acc = all_ref[0]
        for k in range(1, n_dev):
            acc = acc + all_ref[k]
        o_ref[...] = acc

    vmem = pl.BlockSpec(memory_space=pltpu.VMEM)
    return pl.pallas_call(
        body, in_specs=[vmem], out_specs=vmem, out_shape=SDS((rows, lanes), F32),
        scratch_shapes=[pltpu.VMEM((n_dev, rows, lanes), F32), pltpu.SemaphoreType.DMA((7,)),
                        pltpu.SemaphoreType.DMA((7,)), pltpu.SemaphoreType.DMA],
        compiler_params=pltpu.CompilerParams(vmem_limit_bytes=VMEM_LIMIT), name=name)(v)


def _relu2_epilogue(acc):
    return acc, jnp.square(jnp.maximum(acc, 0.0))


def _res_epilogue(acc, res):
    return (acc + res,)


def _drelu2_epilogue(acc, pre):
    return (acc * (2.0 * jnp.maximum(pre.astype(F32), 0.0)),)


def _ffn_fwd(h, g, w1, w2, tag):
    f = _rms_fwd(h, g, name=f"ffn_norm_{tag}")
    pre, act = _mm_nn(f, w1, name=f"ffn1_{tag}", epilogue=_relu2_epilogue, n_out_dtypes=(BF16, BF16))
    h_out = _mm_nn(act, w2, name=f"ffn2_{tag}", extras=(h,), epilogue=_res_epilogue)
    return h_out, (f, pre, act)


def _ffn_bwd(dh, h, g, w1, w2, saved, layer, stacks):
    f, pre, act = saved
    dpre = _mm_nt(dh, w2, name=f"ffn2_dx_{layer}", out_dtype=BF16, extras=(pre,), epilogue=_drelu2_epilogue)
    dw2 = _mm_tn_stacked(act, dh, name=f"ffn2_dw_{layer}", half=layer, col_slots=False, stack=stacks[1])
    df = _mm_nt(dpre, w1, name=f"ffn1_dx_{layer}")
    dw1 = _mm_tn_stacked(f, dpre, name=f"ffn1_dw_{layer}", half=layer, col_slots=True, stack=stacks[0])
    dh, dg = _rms_bwd(h, g, df, dh, name=f"ffn_norm_bwd_{layer}")
    return dh, dg, (dw1, dw2)


def _kv_fwd(mem, g, w_kv, tag):
    m = _rms_fwd(mem, g, name=f"mem_norm_{tag}")
    return m, _mm_nn(m, w_kv, name=f"kv_{tag}")


def _kv_bwd(mem, g, w_kv, m, dk, dv, layer, stack):
    dkv = jnp.concatenate([dk, dv], axis=1)
    dw = _mm_tn_stacked(m, dkv, name=f"kv_dw_{layer}", half=layer, col_slots=True, stack=stack)
    dm = _mm_nt(dkv, w_kv, name=f"kv_dx_{layer}")
    _, dg = _rms_bwd(mem, g, dm, dm, name=f"mem_norm_bwd_{layer}")
    return dw, dg


def _local_step(x, mem, target, p):
    row = lambda v: v.reshape(1, -1)
    g = {}

    h0 = x
    a0 = _rms_fwd(h0, row(p["norm_mix"][0]), name="mix_norm_0")
    proj_a = _mm_nn(a0, p["a_in"], name="a_in")
    m0, kv0 = _kv_fwd(mem, row(p["mem_norm"][0]), p["w_kv"][0], "0")
    cat0 = _attn_fwd(proj_a, 2 * D_INNER, kv0, name="attn_0")
    bs_col = p["a_bs"].reshape(A_GROUPS, CHUNK, 1)
    cat0 = _gate_fwd(proj_a, p["a_ln_g"], p["a_ln_b"], p["a_ws"], bs_col, cat0, name="gate")
    h1 = _mm_nn(cat0, p["w_out"][0], name="out_0", extras=(h0,), epilogue=_res_epilogue)
    h2, ffn0 = _ffn_fwd(h1, row(p["norm_ffn"][0]), p["w_ffn1"][0], p["w_ffn2"][0], "0")

    a1 = _rms_fwd(h2, row(p["norm_mix"][1]), name="mix_norm_1")
    proj_b = _mm_nn(a1, p["b_in"], name="b_in")
    m1, kv1 = _kv_fwd(mem, row(p["mem_norm"][1]), p["w_kv"][1], "1")
    cat1 = _attn_fwd(proj_b, B_Q_OFF, kv1, name="attn_1")
    xbc = _conv_fwd(proj_b, p["b_conv_w"], p["b_conv_b"], name="conv")
    dt_raw = proj_b[:, B_DT_OFF:B_DT_OFF + SSM_HEADS].reshape(SEQ, SSM_GROUPS, SSM_HPG)
    dt_c = jnp.transpose(dt_raw, (1, 0, 2))
    dt_r = jnp.transpose(dt_raw, (1, 2, 0))
    per_head = lambda v: v.reshape(SSM_GROUPS, 1, SSM_HPG)
    ssd_par = (per_head(p["b_dt_bias"]), per_head(p["b_a_log"]), per_head(p["b_d"]), p["b_gnorm"])
    cat1, hprev = _ssd_fwd(xbc, proj_b, dt_c, dt_r, *ssd_par, cat1, name="ssd")
    h3 = _mm_nn(cat1, p["w_out"][1], name="out_1", extras=(h2,), epilogue=_res_epilogue)
    h4, ffn1 = _ffn_fwd(h3, row(p["norm_ffn"][1]), p["w_ffn1"][1], p["w_ffn2"][1], "1")

    loss, dh, g["final_norm"] = _loss_head(h4, row(p["final_norm"]), target, name="loss_head")

    dh, dnf1, dw_ffn = _ffn_bwd(dh, h3, row(p["norm_ffn"][1]), p["w_ffn1"][1], p["w_ffn2"][1], ffn1, 1, (None, None))
    dcat1 = _mm_nt(dh, p["w_out"][1], name="out_dx_1")
    dwo = _mm_tn_stacked(cat1, dh, name="out_dw_1", half=1, col_slots=False)
    dproj_b, dk1, dv1 = _attn_bwd(proj_b, B_Q_OFF, kv1, dcat1, B_IN_PAD, B_Q_OFF, name="attn_bwd_1")
    (dproj_b, dxs, dbm, dcm, ddt_c, ddt_r, g["b_dt_bias"], g["b_a_log"], g["b_d"], g["b_gnorm"]) = _ssd_bwd(
        xbc, proj_b, dt_c, dt_r, *ssd_par, hprev, dcat1, dproj_b, name="ssd_bwd")
    dproj_b, g["b_conv_w"], g["b_conv_b"] = _conv_bwd(proj_b, p["b_conv_w"], p["b_conv_b"], dxs, dbm, dcm, dproj_b,
                                                      name="conv_bwd")
    ddt = jnp.transpose(ddt_c, (1, 0, 2)) + jnp.transpose(ddt_r, (2, 0, 1))
    ddt = jnp.pad(ddt.reshape(SEQ, SSM_HEADS), ((0, 0), (0, B_IN_PAD - B_DT_OFF - SSM_HEADS))).astype(BF16)
    dproj_b = lax.dynamic_update_slice(dproj_b, ddt, (0, B_DT_OFF))
    dwkv, dmn1 = _kv_bwd(mem, row(p["mem_norm"][1]), p["w_kv"][1], m1, dk1, dv1, 1, None)
    half = D_MODEL // 2
    dwb = [_mm_tn(a1, dproj_b, name=f"b_in_dw_{i}", x_cols=(i * half, half)) for i in range(2)]
    da1 = _mm_nt(dproj_b, p["b_in"], name="b_in_dx")
    dh, dnm1 = _rms_bwd(h2, row(p["norm_mix"][1]), da1, dh, name="mix_norm_bwd_1")

    dh, dnf0, dw_ffn = _ffn_bwd(dh, h1, row(p["norm_ffn"][0]), p["w_ffn1"][0], p["w_ffn2"][0], ffn0, 0, dw_ffn)
    dcat0 = _mm_nt(dh, p["w_out"][0], name="out_dx_0")
    dwo = _mm_tn_stacked(cat0, dh, name="out_dw_0", half=0, col_slots=False, stack=dwo)
    dproj_a, dk0, dv0 = _attn_bwd(proj_a, 2 * D_INNER, kv0, dcat0, A_IN, 2 * D_INNER, name="attn_bwd_0")
    dproj_a, g["a_ln_g"], g["a_ln_b"], g["a_ws"], dbs_col = _gate_bwd(
        proj_a, p["a_ln_g"], p["a_ln_b"], p["a_ws"], bs_col, dcat0, dproj_a, name="gate_bwd")
    g["a_bs"] = dbs_col.reshape(A_GROUPS, CHUNK)
    dwkv, dmn0 = _kv_bwd(mem, row(p["mem_norm"][0]), p["w_kv"][0], m0, dk0, dv0, 0, dwkv)
    dwa = None
    for i in range(2):
        dwa = _mm_tn_stacked(a0, dproj_a, name=f"a_in_dw_{i}", half=i, col_slots=True, stack=dwa,
                             x_cols=(i * half, half))
    da0 = _mm_nt(dproj_a, p["a_in"], name="a_in_dx")
    dx, dnm0 = _rms_bwd(h0, row(p["norm_mix"][0]), da0, dh, name="mix_norm_bwd_0")

    g["norm_mix"] = jnp.concatenate([dnm0, dnm1], axis=0)
    g["norm_ffn"] = jnp.concatenate([dnf0, dnf1], axis=0)
    g["mem_norm"] = jnp.concatenate([dmn0, dmn1], axis=0)
    g["w_kv"], g["w_out"], (g["w_ffn1"], g["w_ffn2"]), g["a_in"] = dwkv, dwo, dw_ffn, dwa
    g["b_in"] = jnp.stack([_b_in_grad_slots(d) for d in dwb])
    return loss, dx, g


def _b_in_full(gathered):
    full = jnp.transpose(gathered, (1, 0, 2)).reshape(D_MODEL, B_IN)
    dt0 = D_INNER + CONV_DIM
    return jnp.concatenate([full[:, :dt0], full[:, dt0 + SSM_HEADS:], full[:, dt0:dt0 + SSM_HEADS],
                            jnp.zeros((D_MODEL, B_IN_PAD - B_IN), full.dtype)], axis=1)


def _b_in_grad_slots(d):
    dt0 = D_INNER + CONV_DIM
    full = jnp.concatenate([d[:, :dt0], d[:, B_DT_OFF:B_DT_OFF + SSM_HEADS], d[:, dt0:B_DT_OFF]], axis=1)
    return jnp.transpose(full.reshape(d.shape[0], N_CHIPS, B_IN // N_CHIPS), (1, 0, 2))


LARGE = ("w_kv", "w_out", "w_ffn1", "w_ffn2", "a_in", "b_in")
SMALL_REPL = ("norm_mix", "norm_ffn", "mem_norm", "a_ln_g", "a_ln_b", "a_ws", "a_bs", "b_dt_bias", "b_a_log", "b_d",
              "final_norm")
SMALL_SHARD = ("b_conv_w", "b_conv_b", "b_gnorm")
WEIGHTS = ("norm_mix", "norm_ffn", "mem_norm", "w_kv", "w_out", "w_ffn1", "w_ffn2", "a_in", "a_ln_g", "a_ln_b", "a_ws",
           "a_bs", "b_in", "b_conv_w", "b_conv_b", "b_dt_bias", "b_a_log", "b_d", "b_gnorm", "final_norm")
CONV_SHARD = CONV_DIM // N_CHIPS
GN_SHARD = D_INNER // N_CHIPS


def _halves(w):
    if w.shape[0] == 2:
        return w
    return w.reshape(2, w.shape[1] // 2, w.shape[2])


def _gather_weights(w):
    big = [_halves(w[k]).astype(BF16) for k in LARGE]
    small = jnp.zeros((2, CONV_K, CONV_SHARD), F32)
    small = small.at[0].set(w["b_conv_w"][0])
    small = small.at[1, 0].set(w["b_conv_b"][0])
    small = small.at[1, 1, :GN_SHARD].set(w["b_gnorm"][0])
    gathered = _all_gather_shards([*big, small], name="gather_weights")
    p = {}
    kv, wo, w1, w2, a_in, b_in, sm = gathered
    p["w_kv"] = [kv[:, l] for l in range(2)]
    p["w_out"] = [wo[:, l].reshape(MIX_OUT, D_MODEL) for l in range(2)]
    p["w_ffn1"] = [w1[:, l] for l in range(2)]
    p["w_ffn2"] = [w2[:, l].reshape(D_FF, D_MODEL) for l in range(2)]
    p["a_in"] = a_in.reshape(N_CHIPS, D_MODEL, A_IN // N_CHIPS)
    p["b_in"] = _b_in_full(b_in.reshape(N_CHIPS, D_MODEL, B_IN // N_CHIPS))
    p["b_conv_w"] = jnp.transpose(sm[:, 0], (1, 0, 2)).reshape(CONV_K, CONV_DIM)
    p["b_conv_b"] = sm[:, 1, 0].reshape(1, CONV_DIM)
    p["b_gnorm"] = sm[:, 1, 1, :GN_SHARD].reshape(1, D_INNER)
    return p


def _reduce_large(g):
    stacks = [g[k].reshape(2, -1, g[k].shape[-1]) for k in LARGE]
    parts = _pair_reduce(stacks, name="grads_pair_reduce")
    parts = [t.reshape(N_CHIPS, -1, t.shape[-1]) for t in parts]
    landed = _chip_scatter(parts, name="grads_chip_scatter")
    finals = [_sum_slots(t, name=f"grads_chip_sum_{k}") for k, t in zip(LARGE, landed)]
    shared = _pair_share(finals, name="grads_pair_share")
    return dict(zip(LARGE, shared))


def _small_layout(shapes):
    offs, o = {}, 0
    for k in (*SMALL_REPL, *SMALL_SHARD):
        size = math.prod(shapes[k])
        offs[k] = (o, size)
        o += size
    rows = -(-o // (8 * 128)) * 8
    return offs, rows


def _reduce_small(g, full_shapes):
    offs, rows = _small_layout(full_shapes)
    flat = jnp.concatenate([g[k].reshape(-1) for k in (*SMALL_REPL, *SMALL_SHARD)])
    flat = jnp.pad(flat, (0, rows * 128 - flat.shape[0])).reshape(rows, 128)
    total = _all_reduce_small(flat, name="grads_small_all_reduce").reshape(-1)
    return {k: total[o:o + n].reshape(full_shapes[k]) for k, (o, n) in offs.items()}


def kernel(x, mem, norm_mix, norm_ffn, mem_norm, w_kv, w_out, w_ffn1, w_ffn2, a_in, a_ln_g, a_ln_b, a_ws, a_bs, b_in, b_conv_w, b_conv_b, b_dt_bias, b_a_log, b_d, b_gnorm, final_norm, loss_target, m_norm_mix, m_norm_ffn, m_mem_norm, m_w_kv, m_w_out, m_w_ffn1, m_w_ffn2, m_a_in, m_a_ln_g, m_a_ln_b, m_a_ws, m_a_bs, m_b_in, m_b_conv_w, m_b_conv_b, m_b_dt_bias, m_b_a_log, m_b_d, m_b_gnorm, m_final_norm, v_norm_mix, v_norm_ffn, v_mem_norm, v_w_kv, v_w_out, v_w_ffn1, v_w_ffn2, v_a_in, v_a_ln_g, v_a_ln_b, v_a_ws, v_a_bs, v_b_in, v_b_conv_w, v_b_conv_b, v_b_dt_bias, v_b_a_log, v_b_d, v_b_gnorm, v_final_norm):
    w = dict(norm_mix=norm_mix, norm_ffn=norm_ffn, mem_norm=mem_norm, w_kv=w_kv, w_out=w_out, w_ffn1=w_ffn1,
             w_ffn2=w_ffn2, a_in=a_in, a_ln_g=a_ln_g, a_ln_b=a_ln_b, a_ws=a_ws, a_bs=a_bs, b_in=b_in, b_conv_w=b_conv_w,
             b_conv_b=b_conv_b, b_dt_bias=b_dt_bias, b_a_log=b_a_log, b_d=b_d, b_gnorm=b_gnorm, final_norm=final_norm)
    mom = dict(norm_mix=m_norm_mix, norm_ffn=m_norm_ffn, mem_norm=m_mem_norm, w_kv=m_w_kv, w_out=m_w_out,
               w_ffn1=m_w_ffn1, w_ffn2=m_w_ffn2, a_in=m_a_in, a_ln_g=m_a_ln_g, a_ln_b=m_a_ln_b, a_ws=m_a_ws,
               a_bs=m_a_bs, b_in=m_b_in, b_conv_w=m_b_conv_w, b_conv_b=m_b_conv_b, b_dt_bias=m_b_dt_bias,
               b_a_log=m_b_a_log, b_d=m_b_d, b_gnorm=m_b_gnorm, final_norm=m_final_norm)
    var = dict(norm_mix=v_norm_mix, norm_ffn=v_norm_ffn, mem_norm=v_mem_norm, w_kv=v_w_kv, w_out=v_w_out,
               w_ffn1=v_w_ffn1, w_ffn2=v_w_ffn2, a_in=v_a_in, a_ln_g=v_a_ln_g, a_ln_b=v_a_ln_b, a_ws=v_a_ws,
               a_bs=v_a_bs, b_in=v_b_in, b_conv_w=v_b_conv_w, b_conv_b=v_b_conv_b, b_dt_bias=v_b_dt_bias,
               b_a_log=v_b_a_log, b_d=v_b_d, b_gnorm=v_b_gnorm, final_norm=v_final_norm)

    p = _gather_weights(w)
    p.update(norm_mix=norm_mix, norm_ffn=norm_ffn, mem_norm=mem_norm, a_ln_g=a_ln_g, a_ln_b=a_ln_b, a_ws=a_ws[0],
             a_bs=a_bs[0], b_dt_bias=b_dt_bias, b_a_log=b_a_log, b_d=b_d, final_norm=final_norm)
    loss_part, dx, g = _local_step(x[0], mem[0], loss_target[0], p)
    loss = lax.psum(loss_part[0, 0], ("x", "y", "c"))

    full_shapes = {k: w[k].shape for k in SMALL_REPL}
    full_shapes.update(b_conv_w=(1, CONV_K, CONV_DIM), b_conv_b=(1, CONV_DIM), b_gnorm=(1, D_INNER))
    gs = _reduce_small(g, full_shapes)
    chip = 2 * lax.axis_index("x") + lax.axis_index("y")
    gs["b_conv_w"] = lax.dynamic_slice_in_dim(gs["b_conv_w"], chip * CONV_SHARD, CONV_SHARD, axis=2)
    gs["b_conv_b"] = lax.dynamic_slice_in_dim(gs["b_conv_b"], chip * CONV_SHARD, CONV_SHARD, axis=1)
    gs["b_gnorm"] = lax.dynamic_slice_in_dim(gs["b_gnorm"], chip * GN_SHARD, GN_SHARD, axis=1)
    gl = _reduce_large(g)
    grads = {k: (gl[k].reshape(w[k].shape) if k in gl else gs[k]) for k in WEIGHTS}

    delta, new_m, new_v = {}, {}, {}
    for k in WEIGHTS:
        shape = w[k].shape
        flat = (lambda a: a.reshape(-1, shape[-1])) if len(shape) > 1 else (lambda a: a.reshape(1, -1))
        d, m_new, v_new = _adamw(flat(w[k]), flat(grads[k]), flat(mom[k]), flat(var[k]), name=f"adamw_{k}")
        delta[k], new_m[k], new_v[k] = d.reshape(shape), m_new.reshape(shape), v_new.reshape(shape)

    return (loss, dx.reshape(x.shape), *[grads[k] for k in WEIGHTS], *[delta[k] for k in WEIGHTS],
            *[new_m[k] for k in WEIGHTS], *[new_v[k] for k in WEIGHTS])
```

```python
import math

import jax
import jax.numpy as jnp
from jax import lax
from jax.experimental import pallas as pl
from jax.experimental.pallas import tpu as pltpu

F32 = jnp.float32
BF16 = jnp.bfloat16
SDS = jax.ShapeDtypeStruct

D_MODEL = 1024
SEQ = 2048
CHUNK = 128
N_MEM = 256
D_INNER = 2048
A_GROUPS = 8
A_GROUP_W = D_INNER // A_GROUPS
SSM_HEADS = 32
SSM_HEAD_DIM = 64
SSM_GROUPS = 4
SSM_HPG = 8
SSM_STATE = 128
SSM_GROUP_W = SSM_HPG * SSM_HEAD_DIM
CONV_K = 4
CONV_DIM = 3072
X_HEADS = 4
X_HEAD_DIM = 256
X_WIDTH = 1024
MIX_OUT = 3072
D_FF = 4096
A_IN = 5120
B_IN = 6176
B_IN_PAD = 6272
B_Q_OFF = 5120
B_DT_OFF = 6144
N_CHUNKS = SEQ // CHUNK
EPS = 1e-6
N_CHIPS = 4

ADAM_LR = 0.001
ADAM_B1 = 0.9
ADAM_B2 = 0.999
ADAM_EPS = 1e-08
ADAM_WD = 0.01
ADAM_STEP = 10

VMEM_LIMIT = 48 * 1024 * 1024
MESH = pl.DeviceIdType.MESH


def _cparams(sem):
    return pltpu.CompilerParams(dimension_semantics=sem, vmem_limit_bytes=VMEM_LIMIT)


def _dot(a, b, dims=(((1,), (0,)), ((), ()))):
    return lax.dot_general(a.astype(BF16), b.astype(BF16), dims, preferred_element_type=F32)


def _dot_nt(a, b):
    return _dot(a, b, (((1,), (1,)), ((), ())))


def _dot_tn(a, b):
    return _dot(a, b, (((0,), (0,)), ((), ())))


def _pick(n, cands):
    for c in cands:
        if n % c == 0:
            return c
    raise ValueError(f"no tile for {n}")


def _mm_call(a, b, *, dims, grid, a_spec, b_spec, acc_shape, out_shapes, out_specs, name,
             extras=(), extra_specs=(), epilogue=None):
    n_k = grid[2]
    n_extra = len(extras)
    n_out = len(out_shapes)

    def body(*refs):
        a_ref, b_ref = refs[0], refs[1]
        extra_refs = refs[2:2 + n_extra]
        out_refs = refs[2 + n_extra:2 + n_extra + n_out]
        acc = refs[-1]
        k = pl.program_id(2)

        @pl.when(k == 0)
        def _():
            acc[...] = jnp.zeros_like(acc)

        acc[...] += _dot(a_ref[...], b_ref[...], dims)

        @pl.when(k == n_k - 1)
        def _():
            vals = (acc[...],) if epilogue is None else epilogue(acc[...], *[e[...] for e in extra_refs])
            for o_ref, v in zip(out_refs, vals):
                o_ref[...] = v.astype(o_ref.dtype)

    return pl.pallas_call(
        body, grid=grid, in_specs=[a_spec, b_spec, *extra_specs], out_specs=list(out_specs),
        out_shape=list(out_shapes), scratch_shapes=[pltpu.VMEM(acc_shape, F32)],
        compiler_params=_cparams(("parallel", "parallel", "arbitrary")), name=name,
    )(a, b, *extras)


def _w_dims(w):
    if w.ndim == 2:
        return w.shape[0], w.shape[1], 1, w.shape[1]
    return w.shape[1], w.shape[0] * w.shape[2], w.shape[0], w.shape[2]


def _mm_nn(a, w, *, name, out_dtype=F32, a_cols=None, extras=(), epilogue=None, n_out_dtypes=None):
    m = a.shape[0]
    k_dim, n_dim, _, n_slot = _w_dims(w)
    a_off, a_w = (0, a.shape[1]) if a_cols is None else a_cols
    assert a_w == k_dim
    tm = _pick(m, (1024, 512, 256))
    tn = _pick(n_slot, (512, 896, 640, 256, 128))
    tk = _pick(k_dim, (512, 384, 256, 128))
    assert a_off % tk == 0
    nb = n_slot // tn
    a_spec = pl.BlockSpec((tm, tk), lambda i, j, k: (i, a_off // tk + k))
    if w.ndim == 2:
        b_spec = pl.BlockSpec((tk, tn), lambda i, j, k: (k, j))
    else:
        b_spec = pl.BlockSpec((None, tk, tn), lambda i, j, k: (j // nb, k, j % nb))
    o_spec = pl.BlockSpec((tm, tn), lambda i, j, k: (i, j))
    dts = n_out_dtypes or (out_dtype,)
    outs = _mm_call(a, w, dims=(((1,), (0,)), ((), ())), grid=(m // tm, n_dim // tn, k_dim // tk),
                    a_spec=a_spec, b_spec=b_spec, acc_shape=(tm, tn),
                    out_shapes=[SDS((m, n_dim), dt) for dt in dts], out_specs=[o_spec] * len(dts), name=name,
                    extras=extras, extra_specs=[o_spec] * len(extras), epilogue=epilogue)
    return outs if n_out_dtypes else outs[0]


def _mm_nt(a, w, *, name, out_dtype=F32, extras=(), epilogue=None):
    m = a.shape[0]
    k_dim, n_dim, _, n_slot = _w_dims(w)
    assert a.shape[1] == n_dim
    tm = _pick(m, (1024, 512, 256))
    to = _pick(k_dim, (512, 384, 256, 128))
    tc = _pick(n_slot, (512, 896, 640, 256, 128))
    nb = n_slot // tc
    a_spec = pl.BlockSpec((tm, tc), lambda i, j, k: (i, k))
    if w.ndim == 2:
        b_spec = pl.BlockSpec((to, tc), lambda i, j, k: (j, k))
    else:
        b_spec = pl.BlockSpec((None, to, tc), lambda i, j, k: (k // nb, j, k % nb))
    o_spec = pl.BlockSpec((tm, to), lambda i, j, k: (i, j))
    return _mm_call(a, w, dims=(((1,), (1,)), ((), ())), grid=(m // tm, k_dim // to, n_dim // tc),
                    a_spec=a_spec, b_spec=b_spec, acc_shape=(tm, to),
                    out_shapes=[SDS((m, k_dim), out_dtype)], out_specs=[o_spec], name=name,
                    extras=extras, extra_specs=[o_spec] * len(extras), epilogue=epilogue)[0]


def _mm_tn(x, dy, *, name, x_cols=None):
    s = x.shape[0]
    x_off, k_dim = (0, x.shape[1]) if x_cols is None else x_cols
    n_dim = dy.shape[1]
    tm = _pick(k_dim, (512, 384, 256, 128))
    tn = _pick(n_dim, (512, 896, 640, 256, 128))
    tk = _pick(s, (1024, 512, 256))
    assert x_off % tm == 0
    a_spec = pl.BlockSpec((tk, tm), lambda i, j, k: (k, x_off // tm + i))
    b_spec = pl.BlockSpec((tk, tn), lambda i, j, k: (k, j))
    o_spec = pl.BlockSpec((tm, tn), lambda i, j, k: (i, j))
    return _mm_call(x, dy, dims=(((0,), (0,)), ((), ())), grid=(k_dim // tm, n_dim // tn, s // tk),
                    a_spec=a_spec, b_spec=b_spec, acc_shape=(tm, tn),
                    out_shapes=[SDS((k_dim, n_dim), F32)], out_specs=[o_spec], name=name)[0]


def _mm_tn_stacked(x, dy, *, name, half, col_slots, stack=None, x_cols=None):
    s = x.shape[0]
    x_off, k_dim = (0, x.shape[1]) if x_cols is None else x_cols
    n_dim = dy.shape[1]
    r, c = (k_dim, n_dim // N_CHIPS) if col_slots else (k_dim // N_CHIPS, n_dim)
    tm = _pick(r, (512, 384, 256, 128))
    tn = _pick(c, (512, 896, 640, 256, 128))
    tk = _pick(s, (1024, 512, 256))
    assert x_off % tm == 0
    a_spec = pl.BlockSpec((tk, tm), lambda i, j, k: (k, x_off // tm + i))
    b_spec = pl.BlockSpec((tk, tn), lambda i, j, k: (k, j))
    if col_slots:
        nb = c // tn
        o_spec = pl.BlockSpec((None, None, tm, tn), lambda i, j, k: (half, j // nb, i, j % nb))
    else:
        nb = r // tm
        o_spec = pl.BlockSpec((None, None, tm, tn), lambda i, j, k: (half, i // nb, i % nb, j))
    n_k = s // tk

    def body(a_ref, b_ref, *rest):
        o_ref, acc = rest[-2], rest[-1]
        k = pl.program_id(2)

        @pl.when(k == 0)
        def _():
            acc[...] = jnp.zeros_like(acc)

        acc[...] += _dot_tn(a_ref[...], b_ref[...])

        @pl.when(k == n_k - 1)
        def _():
            o_ref[...] = acc[...]

    keep = [] if stack is None else [stack]
    return pl.pallas_call(
        body, grid=(k_dim // tm, n_dim // tn, n_k), in_specs=[a_spec, b_spec, *([ANY] * len(keep))],
        out_specs=o_spec, out_shape=SDS((2, N_CHIPS, r, c), F32), scratch_shapes=[pltpu.VMEM((tm, tn), F32)],
        input_output_aliases={2: 0} if keep else {},
        compiler_params=_cparams(("parallel", "parallel", "arbitrary")), name=name,
    )(x, dy, *keep)


def _rms(x, g):
    return x * lax.rsqrt(jnp.mean(x * x, axis=-1, keepdims=True) + EPS) * g


def _rms_fwd(h, g, *, name):
    rows, d = h.shape
    tr = _pick(rows, (512, 256))

    def body(h_ref, g_ref, o_ref):
        o_ref[...] = _rms(h_ref[...], g_ref[...]).astype(o_ref.dtype)

    return pl.pallas_call(
        body, grid=(rows // tr,),
        in_specs=[pl.BlockSpec((tr, d), lambda i: (i, 0)), pl.BlockSpec((1, d), lambda i: (0, 0))],
        out_specs=pl.BlockSpec((tr, d), lambda i: (i, 0)), out_shape=SDS((rows, d), BF16),
        compiler_params=_cparams(("parallel",)), name=name)(h, g)


def _rms_bwd(h, g, da, dres, *, name):
    rows, d = h.shape
    tr = _pick(rows, (512, 256))

    def body(h_ref, g_ref, da_ref, dres_ref, dh_ref, dg_ref):
        _, vjp = jax.vjp(_rms, h_ref[...], g_ref[...])
        dh, dg = vjp(da_ref[...].astype(F32))
        dh_ref[...] = dres_ref[...] + dh

        @pl.when(pl.program_id(0) == 0)
        def _():
            dg_ref[...] = jnp.zeros_like(dg_ref)

        dg_ref[...] += dg

    row_spec = pl.BlockSpec((tr, d), lambda i: (i, 0))
    vec_spec = pl.BlockSpec((1, d), lambda i: (0, 0))
    return pl.pallas_call(
        body, grid=(rows // tr,), in_specs=[row_spec, vec_spec, row_spec, row_spec],
        out_specs=[row_spec, vec_spec], out_shape=[SDS((rows, d), F32), SDS((1, d), F32)],
        compiler_params=_cparams(("arbitrary",)), name=name)(h, g, da, dres)


def _loss_head(h, g, target, *, name):
    rows, d = h.shape
    tr = _pick(rows, (512, 256))

    def body(h_ref, g_ref, t_ref, loss_ref, dh_ref, dg_ref):
        y, vjp = jax.vjp(_rms, h_ref[...], g_ref[...])
        err = y - t_ref[...]
        dh, dg = vjp(err * (1.0 / d))
        dh_ref[...] = dh

        @pl.when(pl.program_id(0) == 0)
        def _():
            dg_ref[...] = jnp.zeros_like(dg_ref)
            loss_ref[...] = jnp.zeros_like(loss_ref)

        dg_ref[...] += dg
        part = jnp.sum(jnp.sum(err * err, axis=-1, keepdims=True), axis=0, keepdims=True) * (0.5 / d)
        loss_ref[...] += jnp.broadcast_to(part, loss_ref.shape)

    row_spec = pl.BlockSpec((tr, d), lambda i: (i, 0))
    vec_spec = pl.BlockSpec((1, d), lambda i: (0, 0))
    loss_spec = pl.BlockSpec((8, 128), lambda i: (0, 0))
    return pl.pallas_call(
        body, grid=(rows // tr,), in_specs=[row_spec, vec_spec, row_spec],
        out_specs=[loss_spec, row_spec, vec_spec],
        out_shape=[SDS((8, 128), F32), SDS((rows, d), F32), SDS((1, d), F32)],
        compiler_params=_cparams(("arbitrary",)), name=name)(h, g, target)


def _gelu(x):
    return 0.5 * x * (1.0 + lax.erf(x * (1.0 / math.sqrt(2.0))))


def _gate_tile(pu, pv, ln_g, ln_b, ws, bs_t):
    u = [_gelu(p) for p in pu]
    v = [_gelu(p) for p in pv]
    mu = sum(jnp.sum(t, axis=-1, keepdims=True) for t in v) * (1.0 / D_INNER)
    vc = [t - mu for t in v]
    var = sum(jnp.sum(t * t, axis=-1, keepdims=True) for t in vc) * (1.0 / D_INNER)
    rstd = lax.rsqrt(var + EPS)
    row = lax.broadcasted_iota(jnp.int32, (CHUNK, CHUNK), 0)
    col = lax.broadcasted_iota(jnp.int32, (CHUNK, CHUNK), 1)
    out = []
    for gi in range(A_GROUPS):
        vn = vc[gi] * rstd * ln_g[gi] + ln_b[gi]
        w = jnp.where(row >= col, ws[gi], 0.0)
        sv = _dot(w, vn) + bs_t[gi]
        out.append(u[gi] * sv)
    return out


def _split(ref, n, width):
    return [ref[:, i * width:(i + 1) * width] for i in range(n)]


def _gate_in_specs():
    return [
        pl.BlockSpec((CHUNK, D_INNER), lambda c: (c, 0)),
        pl.BlockSpec((CHUNK, D_INNER), lambda c: (c, 1)),
        pl.BlockSpec((1, D_INNER), lambda c: (0, 0)),
        pl.BlockSpec((1, D_INNER), lambda c: (0, 0)),
        pl.BlockSpec((A_GROUPS, CHUNK, CHUNK), lambda c: (0, 0, 0)),
        pl.BlockSpec((A_GROUPS, CHUNK, 1), lambda c: (0, 0, 0)),
    ]


def _gate_args(u_ref, v_ref, g_ref, b_ref, ws_ref, bs_ref):
    ng, gw = A_GROUPS, A_GROUP_W
    return (_split(u_ref, ng, gw), _split(v_ref, ng, gw), _split(g_ref, ng, gw), _split(b_ref, ng, gw),
            [ws_ref[i] for i in range(ng)], [bs_ref[i] for i in range(ng)])


def _gate_fwd(proj, ln_g, ln_b, ws, bs_col, mixcat, *, name):
    def body(u_ref, v_ref, g_ref, b_ref, ws_ref, bs_ref, cat_in, cat_ref):
        del cat_in
        out = _gate_tile(*_gate_args(u_ref, v_ref, g_ref, b_ref, ws_ref, bs_ref))
        for gi, o in enumerate(out):
            cat_ref[:, gi * A_GROUP_W:(gi + 1) * A_GROUP_W] = o.astype(cat_ref.dtype)

    return pl.pallas_call(
        body, grid=(N_CHUNKS,), in_specs=[*_gate_in_specs(), pl.BlockSpec(memory_space=pl.ANY)],
        out_specs=pl.BlockSpec((CHUNK, D_INNER), lambda c: (c, 0)), out_shape=SDS(mixcat.shape, mixcat.dtype),
        input_output_aliases={6: 0}, compiler_params=_cparams(("parallel",)), name=name,
    )(proj, proj, ln_g, ln_b, ws, bs_col, mixcat)


def _gate_bwd(proj, ln_g, ln_b, ws, bs_col, dcat, dproj, *, name):
    ng, gw = A_GROUPS, A_GROUP_W

    def body(u_ref, v_ref, g_ref, b_ref, ws_ref, bs_ref, d_ref, dproj_in, dproj_ref, dg_ref, db_ref, dws_ref, dbs_ref):
        del dproj_in
        args = _gate_args(u_ref, v_ref, g_ref, b_ref, ws_ref, bs_ref)
        _, vjp = jax.vjp(_gate_tile, *args)
        dpu, dpv, dg, db, dws, dbs = vjp(_split(d_ref, ng, gw))
        for gi in range(ng):
            dproj_ref[:, gi * gw:(gi + 1) * gw] = dpu[gi].astype(dproj_ref.dtype)
            dproj_ref[:, D_INNER + gi * gw:D_INNER + (gi + 1) * gw] = dpv[gi].astype(dproj_ref.dtype)

        @pl.when(pl.program_id(0) == 0)
        def _():
            for r in (dg_ref, db_ref, dws_ref, dbs_ref):
                r[...] = jnp.zeros_like(r)

        for gi in range(ng):
            dg_ref[:, gi * gw:(gi + 1) * gw] += dg[gi]
            db_ref[:, gi * gw:(gi + 1) * gw] += db[gi]
            dws_ref[gi] += dws[gi]
            dbs_ref[gi] += dbs[gi]

    in_specs = _gate_in_specs()
    return pl.pallas_call(
        body, grid=(N_CHUNKS,),
        in_specs=[*in_specs, pl.BlockSpec((CHUNK, D_INNER), lambda c: (c, 0)), pl.BlockSpec(memory_space=pl.ANY)],
        out_specs=[pl.BlockSpec((CHUNK, 2 * D_INNER), lambda c: (c, 0)), *in_specs[2:]],
        out_shape=[SDS(dproj.shape, dproj.dtype), SDS((1, D_INNER), F32), SDS((1, D_INNER), F32),
                   SDS((ng, CHUNK, CHUNK), F32), SDS((ng, CHUNK, 1), F32)],
        input_output_aliases={7: 0}, compiler_params=_cparams(("arbitrary",)), name=name,
    )(proj, proj, ln_g, ln_b, ws, bs_col, dcat, dproj)


ATT_TQ = 512


def _attn_tile(q, k, v):
    s = _dot_nt(q, k) * (1.0 / math.sqrt(X_HEAD_DIM))
    s = s - jnp.max(s, axis=-1, keepdims=True)
    e = jnp.exp(s)
    p = e / jnp.sum(e, axis=-1, keepdims=True)
    return _dot(p, v)


def _attn_in_specs(q_blk, order):
    hd = X_HEAD_DIM
    return [
        pl.BlockSpec((ATT_TQ, hd), lambda a, b: (order(a, b)[0], q_blk + order(a, b)[1])),
        pl.BlockSpec((N_MEM, hd), lambda a, b: (0, order(a, b)[1])),
        pl.BlockSpec((N_MEM, hd), lambda a, b: (0, X_HEADS + order(a, b)[1])),
    ]


def _attn_fwd(proj, q_off, kv, *, name):
    order = lambda i, h: (i, h)
    cat_blk = D_INNER // X_HEAD_DIM

    def body(q_ref, k_ref, v_ref, o_ref):
        o_ref[...] = _attn_tile(q_ref[...], k_ref[...], v_ref[...]).astype(o_ref.dtype)

    return pl.pallas_call(
        body, grid=(SEQ // ATT_TQ, X_HEADS), in_specs=_attn_in_specs(q_off // X_HEAD_DIM, order),
        out_specs=pl.BlockSpec((ATT_TQ, X_HEAD_DIM), lambda i, h: (i, cat_blk + h)),
        out_shape=SDS((SEQ, MIX_OUT), BF16), compiler_params=_cparams(("parallel", "parallel")), name=name,
    )(proj, kv, kv)


def _attn_bwd(proj, q_off, kv, dcat, dproj_width, dq_off, *, name):
    order = lambda h, i: (i, h)
    cat_blk = D_INNER // X_HEAD_DIM
    dq_blk = dq_off // X_HEAD_DIM

    def body(q_ref, k_ref, v_ref, do_ref, dq_ref, dk_ref, dv_ref):
        _, vjp = jax.vjp(_attn_tile, q_ref[...], k_ref[...], v_ref[...])
        dq, dk, dv = vjp(do_ref[...])
        dq_ref[...] = dq.astype(dq_ref.dtype)

        @pl.when(pl.program_id(1) == 0)
        def _():
            dk_ref[...] = jnp.zeros_like(dk_ref)
            dv_ref[...] = jnp.zeros_like(dv_ref)

        dk_ref[...] += dk
        dv_ref[...] += dv

    kv_spec = pl.BlockSpec((N_MEM, X_HEAD_DIM), lambda h, i: (0, h))
    return pl.pallas_call(
        body, grid=(X_HEADS, SEQ // ATT_TQ),
        in_specs=[*_attn_in_specs(q_off // X_HEAD_DIM, order),
                  pl.BlockSpec((ATT_TQ, X_HEAD_DIM), lambda h, i: (i, cat_blk + h))],
        out_specs=[pl.BlockSpec((ATT_TQ, X_HEAD_DIM), lambda h, i: (i, dq_blk + h)), kv_spec, kv_spec],
        out_shape=[SDS((SEQ, dproj_width), BF16), SDS((N_MEM, X_WIDTH), F32), SDS((N_MEM, X_WIDTH), F32)],
        compiler_params=_cparams(("parallel", "arbitrary")), name=name,
    )(proj, kv, kv, dcat)


CONV_TC = 512


def _shift_down(x, s):
    if s == 0:
        return x
    row = lax.broadcasted_iota(jnp.int32, x.shape, 0)
    return jnp.where(row >= s, pltpu.roll(x, s, 0), 0.0)


def _shift_up(x, s):
    if s == 0:
        return x
    n = x.shape[0]
    row = lax.broadcasted_iota(jnp.int32, x.shape, 0)
    return jnp.where(row < n - s, pltpu.roll(x, n - s, 0), 0.0)


def _conv_pre(x, w_ref, b_ref):
    pre = b_ref[...] + jnp.zeros_like(x)
    for k in range(CONV_K):
        pre = pre + w_ref[k:k + 1, :] * _shift_down(x, CONV_K - 1 - k)
    return pre


def _conv_fwd(proj, w, b, *, name):
    blk0 = D_INNER // CONV_TC

    def body(x_ref, w_ref, b_ref, o_ref):
        pre = _conv_pre(x_ref[...], w_ref, b_ref)
        o_ref[...] = pre * jax.nn.sigmoid(pre)

    return pl.pallas_call(
        body, grid=(CONV_DIM // CONV_TC,),
        in_specs=[pl.BlockSpec((SEQ, CONV_TC), lambda j: (0, blk0 + j)), pl.BlockSpec((CONV_K, CONV_TC), lambda j: (0, j)),
                  pl.BlockSpec((1, CONV_TC), lambda j: (0, j))],
        out_specs=pl.BlockSpec((SEQ, CONV_TC), lambda j: (0, j)), out_shape=SDS((SEQ, CONV_DIM), F32),
        compiler_params=_cparams(("parallel",)), name=name)(proj, w, b)


def _conv_bwd(proj, w, b, dxs, dbm, dcm, dproj, *, name):
    tc = CONV_TC // 2
    blk0 = D_INNER // tc
    n_x = D_INNER // tc
    n_b = SSM_GROUPS * SSM_STATE // tc

    def body(x_ref, w_ref, b_ref, dxs_ref, dbm_ref, dcm_ref, dproj_in, dproj_ref, dw_ref, db_ref):
        del dproj_in
        j = pl.program_id(0)
        x = x_ref[...]
        pre = _conv_pre(x, w_ref, b_ref)
        sg = jax.nn.sigmoid(pre)
        dact = jnp.where(j < n_x, dxs_ref[...], jnp.where(j < n_x + n_b, dbm_ref[...], dcm_ref[...]))
        dpre = dact * (sg * (1.0 + pre * (1.0 - sg)))
        dx = jnp.zeros_like(x)
        for k in range(CONV_K):
            s = CONV_K - 1 - k
            dx = dx + w_ref[k:k + 1, :] * _shift_up(dpre, s)
            dw_ref[k:k + 1, :] = jnp.sum(dpre * _shift_down(x, s), axis=0, keepdims=True)
        dproj_ref[...] = dx.astype(dproj_ref.dtype)
        db_ref[...] = jnp.sum(dpre, axis=0, keepdims=True)

    clip = lambda v, hi: jnp.minimum(jnp.maximum(v, 0), hi)
    return pl.pallas_call(
        body, grid=(CONV_DIM // tc,),
        in_specs=[pl.BlockSpec((SEQ, tc), lambda j: (0, blk0 + j)), pl.BlockSpec((CONV_K, tc), lambda j: (0, j)),
                  pl.BlockSpec((1, tc), lambda j: (0, j)),
                  pl.BlockSpec((SEQ, tc), lambda j: (0, clip(j, n_x - 1))),
                  pl.BlockSpec((SEQ, tc), lambda j: (0, clip(j - n_x, n_b - 1))),
                  pl.BlockSpec((SEQ, tc), lambda j: (0, clip(j - n_x - n_b, n_b - 1))),
                  pl.BlockSpec(memory_space=pl.ANY)],
        out_specs=[pl.BlockSpec((SEQ, tc), lambda j: (0, blk0 + j)), pl.BlockSpec((CONV_K, tc), lambda j: (0, j)),
                   pl.BlockSpec((1, tc), lambda j: (0, j))],
        out_shape=[SDS(dproj.shape, dproj.dtype), SDS((CONV_K, CONV_DIM), F32), SDS((1, CONV_DIM), F32)],
        input_output_aliases={6: 0}, compiler_params=_cparams(("parallel",)), name=name,
    )(proj, w, b, dxs, dbm, dcm, dproj)


def _ssd_tile(xs, zs, bm, cm, hs, dtc, dtr, bias, alog, dsk, gn):
    row = lax.broadcasted_iota(jnp.int32, (CHUNK, CHUNK), 0)
    col = lax.broadcasted_iota(jnp.int32, (CHUNK, CHUNK), 1)
    causal = row >= col
    tri = jnp.where(causal, 1.0, 0.0)
    cb = _dot_nt(cm, bm)
    ygs, hn = [], []
    for r in range(SSM_HPG):
        a = -jnp.exp(alog[r])
        da_c = jax.nn.softplus(dtc[r] + bias[r]) * a
        da_r = jax.nn.softplus(dtr[r] + bias[r]) * a
        dt_c = jax.nn.softplus(dtc[r] + bias[r])
        cs_c = jnp.sum(tri * da_r, axis=1, keepdims=True)
        cs_r = jnp.sum(jnp.where(row <= col, 1.0, 0.0) * da_c, axis=0, keepdims=True)
        cs_last = jnp.sum(da_c, axis=0, keepdims=True)
        lm = jnp.exp(jnp.where(causal, cs_c - cs_r, -1e30))
        xdt = xs[r] * dt_c
        y = _dot(cb * lm, xdt)
        y = y + _dot_nt(cm, hs[r]) * jnp.exp(cs_c)
        y = y + xs[r] * dsk[r]
        states = _dot_tn(xdt * jnp.exp(cs_last - cs_c), bm)
        hn.append(hs[r] * jnp.exp(cs_last) + states)
        ygs.append(y * (zs[r] * jax.nn.sigmoid(zs[r])))
    ms = sum(jnp.sum(t * t, axis=-1, keepdims=True) for t in ygs) * (1.0 / SSM_GROUP_W)
    rs = lax.rsqrt(ms + EPS)
    return [ygs[r] * rs * gn[r] for r in range(SSM_HPG)], hn


def _ssd_in_specs(cidx):
    gw, n = SSM_GROUP_W, SSM_STATE
    bm_blk = D_INNER // n
    return [
        pl.BlockSpec((CHUNK, gw), lambda g, c: (cidx(c), g)),
        pl.BlockSpec((CHUNK, gw), lambda g, c: (cidx(c), g)),
        pl.BlockSpec((CHUNK, n), lambda g, c: (cidx(c), bm_blk + g)),
        pl.BlockSpec((CHUNK, n), lambda g, c: (cidx(c), bm_blk + SSM_GROUPS + g)),
        pl.BlockSpec((None, CHUNK, SSM_HPG), lambda g, c: (g, cidx(c), 0)),
        pl.BlockSpec((None, SSM_HPG, CHUNK), lambda g, c: (g, 0, cidx(c))),
        pl.BlockSpec((None, 1, SSM_HPG), lambda g, c: (g, 0, 0)),
        pl.BlockSpec((None, 1, SSM_HPG), lambda g, c: (g, 0, 0)),
        pl.BlockSpec((None, 1, SSM_HPG), lambda g, c: (g, 0, 0)),
        pl.BlockSpec((1, gw), lambda g, c: (0, g)),
    ]


def _ssd_args(x_ref, z_ref, bm_ref, cm_ref, hs, dtc_ref, dtr_ref, bias_ref, alog_ref, dsk_ref, gn_ref):
    nh, p = SSM_HPG, SSM_HEAD_DIM
    col = lambda ref: [ref[:, r:r + 1] for r in range(nh)]
    return (_split(x_ref, nh, p), _split(z_ref, nh, p), bm_ref[...], cm_ref[...], hs,
            col(dtc_ref), [dtr_ref[r:r + 1, :] for r in range(nh)], col(bias_ref), col(alog_ref), col(dsk_ref),
            _split(gn_ref, nh, p))


def _ssd_fwd(xbc, proj, dt_c, dt_r, bias, alog, dsk, gn, mixcat, *, name):
    nh = SSM_HPG

    def body(x_ref, z_ref, bm_ref, cm_ref, dtc_ref, dtr_ref, bias_ref, alog_ref, dsk_ref, gn_ref, cat_in,
             cat_ref, hprev_ref, h_scr):
        del cat_in

        @pl.when(pl.program_id(1) == 0)
        def _():
            h_scr[...] = jnp.zeros_like(h_scr)

        hs = [h_scr[r] for r in range(nh)]
        for r in range(nh):
            hprev_ref[r] = hs[r]
        yn, hn = _ssd_tile(*_ssd_args(x_ref, z_ref, bm_ref, cm_ref, hs, dtc_ref, dtr_ref, bias_ref, alog_ref,
                                      dsk_ref, gn_ref))
        for r in range(nh):
            cat_ref[:, r * SSM_HEAD_DIM:(r + 1) * SSM_HEAD_DIM] = yn[r].astype(cat_ref.dtype)
            h_scr[r] = hn[r]

    return pl.pallas_call(
        body, grid=(SSM_GROUPS, N_CHUNKS), in_specs=[*_ssd_in_specs(lambda c: c), pl.BlockSpec(memory_space=pl.ANY)],
        out_specs=[pl.BlockSpec((CHUNK, SSM_GROUP_W), lambda g, c: (c, g)),
                   pl.BlockSpec((None, nh, SSM_HEAD_DIM, SSM_STATE), lambda g, c: (c, g, 0, 0))],
        out_shape=[SDS(mixcat.shape, mixcat.dtype), SDS((N_CHUNKS, SSM_HEADS, SSM_HEAD_DIM, SSM_STATE), F32)],
        scratch_shapes=[pltpu.VMEM((nh, SSM_HEAD_DIM, SSM_STATE), F32)],
        input_output_aliases={10: 0}, compiler_params=_cparams(("parallel", "arbitrary")), name=name,
    )(xbc, proj, xbc, xbc, dt_c, dt_r, bias, alog, dsk, gn, mixcat)


def _ssd_bwd(xbc, proj, dt_c, dt_r, bias, alog, dsk, gn, hprev, dcat, dproj, *, name):
    nh, p, gw, n = SSM_HPG, SSM_HEAD_DIM, SSM_GROUP_W, SSM_STATE
    rev = lambda c: N_CHUNKS - 1 - c

    def body(x_ref, z_ref, bm_ref, cm_ref, dtc_ref, dtr_ref, bias_ref, alog_ref, dsk_ref, gn_ref, hprev_ref, dy_ref,
             dproj_in, dz_ref, dxs_ref, dbm_ref, dcm_ref, ddtc_ref, ddtr_ref, dbias_ref, dalog_ref, ddsk_ref, dgn_ref,
             dh_scr):
        del dproj_in
        first = pl.program_id(1) == 0

        @pl.when(first)
        def _():
            dh_scr[...] = jnp.zeros_like(dh_scr)
            for ref in (dbias_ref, dalog_ref, ddsk_ref, dgn_ref):
                ref[...] = jnp.zeros_like(ref)

        hs = [hprev_ref[r] for r in range(nh)]
        args = _ssd_args(x_ref, z_ref, bm_ref, cm_ref, hs, dtc_ref, dtr_ref, bias_ref, alog_ref, dsk_ref, gn_ref)
        _, vjp = jax.vjp(_ssd_tile, *args)
        dxs, dzs, dbm, dcm, dhs, ddtc, ddtr, dbias, dalog, ddsk, dgn = vjp(
            (_split(dy_ref, nh, p), [dh_scr[r] for r in range(nh)]))
        dbm_ref[...] = dbm
        dcm_ref[...] = dcm
        for r in range(nh):
            dxs_ref[:, r * p:(r + 1) * p] = dxs[r]
            dz_ref[:, r * p:(r + 1) * p] = dzs[r].astype(dz_ref.dtype)
            dh_scr[r] = dhs[r]
            ddtc_ref[:, r:r + 1] = ddtc[r]
            ddtr_ref[r:r + 1, :] = ddtr[r]
            dbias_ref[:, r:r + 1] += dbias[r]
            dalog_ref[:, r:r + 1] += dalog[r]
            ddsk_ref[:, r:r + 1] += ddsk[r]
            dgn_ref[:, r * p:(r + 1) * p] += dgn[r]

    par_spec = pl.BlockSpec((None, 1, nh), lambda g, c: (g, 0, 0))
    return pl.pallas_call(
        body, grid=(SSM_GROUPS, N_CHUNKS),
        in_specs=[*_ssd_in_specs(rev),
                  pl.BlockSpec((None, nh, p, n), lambda g, c: (rev(c), g, 0, 0)),
                  pl.BlockSpec((CHUNK, gw), lambda g, c: (rev(c), g)),
                  pl.BlockSpec(memory_space=pl.ANY)],
        out_specs=[pl.BlockSpec((CHUNK, gw), lambda g, c: (rev(c), g)),
                   pl.BlockSpec((CHUNK, gw), lambda g, c: (rev(c), g)),
                   pl.BlockSpec((CHUNK, n), lambda g, c: (rev(c), g)),
                   pl.BlockSpec((CHUNK, n), lambda g, c: (rev(c), g)),
                   pl.BlockSpec((None, CHUNK, nh), lambda g, c: (g, rev(c), 0)),
                   pl.BlockSpec((None, nh, CHUNK), lambda g, c: (g, 0, rev(c))),
                   par_spec, par_spec, par_spec,
                   pl.BlockSpec((1, gw), lambda g, c: (0, g))],
        out_shape=[SDS(dproj.shape, dproj.dtype), SDS((SEQ, D_INNER), F32), SDS((SEQ, SSM_GROUPS * n), F32),
                   SDS((SEQ, SSM_GROUPS * n), F32), SDS((SSM_GROUPS, SEQ, nh), F32), SDS((SSM_GROUPS, nh, SEQ), F32),
                   SDS((SSM_GROUPS, 1, nh), F32), SDS((SSM_GROUPS, 1, nh), F32), SDS((SSM_GROUPS, 1, nh), F32),
                   SDS((1, D_INNER), F32)],
        scratch_shapes=[pltpu.VMEM((nh, p, n), F32)],
        input_output_aliases={12: 0}, compiler_params=_cparams(("parallel", "arbitrary")), name=name,
    )(xbc, proj, xbc, xbc, dt_c, dt_r, bias, alog, dsk, gn, hprev, dcat, dproj)


def _sum_slots(p, *, name):
    n, r, c = p.shape
    tr = _pick(r, (256, 384, 128, 8))

    def body(p_ref, o_ref):
        acc = p_ref[0].astype(F32)
        for s in range(1, n):
            acc = acc + p_ref[s].astype(F32)
        o_ref[...] = acc

    return pl.pallas_call(body, grid=(r // tr,), in_specs=[pl.BlockSpec((n, tr, c), lambda i: (0, i, 0))],
                          out_specs=pl.BlockSpec((tr, c), lambda i: (i, 0)), out_shape=SDS((r, c), F32),
                          compiler_params=_cparams(("parallel",)), name=name)(p)


def _adamw(w, g, m, v, *, name):
    r, c = w.shape
    tr = r if r <= 256 else _pick(r, (256, 128, 8))
    spec = pl.BlockSpec((tr, c), lambda i: (i, 0))

    def body(w_ref, g_ref, m_ref, v_ref, d_ref, mo_ref, vo_ref):
        g = g_ref[...]
        m_new = ADAM_B1 * m_ref[...] + (1.0 - ADAM_B1) * g
        v_new = ADAM_B2 * v_ref[...] + (1.0 - ADAM_B2) * (g * g)
        m_hat = m_new / (1.0 - ADAM_B1 ** ADAM_STEP)
        v_hat = v_new / (1.0 - ADAM_B2 ** ADAM_STEP)
        d_ref[...] = -ADAM_LR * (m_hat / (jnp.sqrt(v_hat) + ADAM_EPS) + ADAM_WD * w_ref[...])
        mo_ref[...] = m_new
        vo_ref[...] = v_new

    return pl.pallas_call(body, grid=(r // tr,), in_specs=[spec] * 4, out_specs=[spec] * 3,
                          out_shape=[SDS((r, c), F32)] * 3, compiler_params=_cparams(("parallel",)), name=name)(w, g, m, v)


ANY = pl.BlockSpec(memory_space=pl.ANY)


def _place():
    x, y, c = lax.axis_index("x"), lax.axis_index("y"), lax.axis_index("c")
    chips = [(1 - x, y), (x, 1 - y), (1 - x, 1 - y)]
    return x, y, c, chips


def _remote(src, dst, send_sem, recv_sem, to):
    return pltpu.make_async_remote_copy(src_ref=src, dst_ref=dst, send_sem=send_sem, recv_sem=recv_sem,
                                        device_id=to, device_id_type=MESH)


STREAM_ROWS = 128


def _stream_rows(i):
    return pl.ds(pl.multiple_of(i * STREAM_ROWS, STREAM_ROWS), STREAM_ROWS)


def _channel_scratch(width, dtype):
    buf = (2, STREAM_ROWS, width)
    return [pltpu.VMEM(buf, dtype), pltpu.VMEM(buf, dtype), *([pltpu.SemaphoreType.DMA((2,))] * 5),
            pltpu.SemaphoreType.REGULAR((2,))]


CHANNEL_REFS = 8


def _copy_through_vmem(src, dst, ch):
    sbuf, _, ld, _, _, st, _, _ = ch
    steps = src.shape[0] // STREAM_ROWS
    assert steps >= 2 and steps * STREAM_ROWS == src.shape[0]

    def load(i, slot):
        return pltpu.make_async_copy(src.at[_stream_rows(i)], sbuf.at[slot], ld.at[slot])

    def store(i, slot):
        return pltpu.make_async_copy(sbuf.at[slot], dst.at[_stream_rows(i)], st.at[slot])

    load(0, 0).start()

    def step(i, carry):
        slot = lax.rem(i, 2)
        nxt = 1 - slot

        @pl.when(i + 1 < steps)
        def _():
            @pl.when(i >= 1)
            def _():
                store(0, nxt).wait()
            load(i + 1, nxt).start()

        load(i, slot).wait()
        store(i, slot).start()
        return carry

    lax.fori_loop(0, steps, step, 0)
    for slot in range(2):
        store(0, slot).wait()


def _exchange_stream(src, dst, keep, ch, sibling):
    sbuf, rbuf, ld, snd, rcv, st, kp, credit = ch
    steps = src.shape[0] // STREAM_ROWS
    assert steps >= 2 and steps * STREAM_ROWS == src.shape[0]

    def load(i, slot):
        return pltpu.make_async_copy(src.at[_stream_rows(i)], sbuf.at[slot], ld.at[slot])

    def push(slot):
        return _remote(sbuf.at[slot], rbuf.at[slot], snd.at[slot], rcv.at[slot], sibling)

    def store(i, slot):
        return pltpu.make_async_copy(rbuf.at[slot], dst.at[_stream_rows(i)], st.at[slot])

    def save(i, slot):
        return pltpu.make_async_copy(sbuf.at[slot], keep.at[_stream_rows(i)], kp.at[slot])

    for slot in range(2):
        pl.semaphore_signal(credit.at[slot], 1, device_id=sibling, device_id_type=MESH)
    load(0, 0).start()

    def step(i, carry):
        slot = lax.rem(i, 2)
        nxt = 1 - slot

        @pl.when(i + 1 < steps)
        def _():
            @pl.when(i >= 1)
            def _():
                push(nxt).wait_send()
                if keep is not None:
                    save(0, nxt).wait()
            load(i + 1, nxt).start()

        load(i, slot).wait()
        pl.semaphore_wait(credit.at[slot], 1)
        push(slot).start()
        if keep is not None:
            save(i, slot).start()
        push(slot).wait_recv()
        store(i, slot).start()

        @pl.when(i >= 1)
        def _():
            store(0, nxt).wait()

            @pl.when(i + 1 < steps)
            def _():
                pl.semaphore_signal(credit.at[nxt], 1, device_id=sibling, device_id_type=MESH)
        return carry

    lax.fori_loop(0, steps, step, 0)
    store(0, (steps - 1) % 2).wait()
    for slot in range(2):
        push(slot).wait_send()
        if keep is not None:
            save(0, slot).wait()


def _all_gather_shards(shards, small, *, name):
    n = len(shards)

    def body(*refs):
        ins, small_in = refs[:n], refs[n]
        outs, small_out = refs[n + 1:2 * n + 1], refs[2 * n + 1]
        scr = refs[2 * n + 2:]
        chans = [scr[CHANNEL_REFS * t:CHANNEL_REFS * (t + 1)] for t in range(n)]
        send_sems, recv_sems, small_sems = scr[CHANNEL_REFS * n:]
        x, y, c, chips = _place()
        me = 2 * x + y
        sibling = (x, y, 1 - c)
        first = [_remote(ins[t].at[c], outs[t].at[me, c], send_sems.at[3 * t + j], recv_sems.at[3 * t + j], (*chip, c))
                 for t in range(n) for j, chip in enumerate(chips)]
        first += [_remote(small_in.at[c], small_out.at[me, c], send_sems.at[3 * n + j], recv_sems.at[3 * n + j],
                          (*chip, c)) for j, chip in enumerate(chips)]
        for cp in first:
            cp.start()
        small_local = pltpu.make_async_copy(small_in, small_out.at[me], small_sems.at[6])
        small_local.start()
        for t in range(n):
            for h in range(2):
                _copy_through_vmem(ins[t].at[h], outs[t].at[me, h], chans[t])
        passed = []
        for t in range(n + 1):
            for j, (cx, cy) in enumerate(chips):
                k = 2 * cx + cy
                if t < n:
                    landed = outs[t].at[k, c]
                    _remote(landed, landed, send_sems.at[3 * t + j], recv_sems.at[3 * t + j], (cx, cy, c)).wait_recv()
                    _exchange_stream(landed, outs[t].at[k, 1 - c], None, chans[t], sibling)
                else:
                    landed = small_out.at[k, c]
                    _remote(landed, landed, send_sems.at[3 * n + j], recv_sems.at[3 * n + j], (cx, cy, c)).wait_recv()
                    fwd = _remote(landed, landed, small_sems.at[j], small_sems.at[3 + j], sibling)
                    fwd.start()
                    passed.append(fwd)
        for j, (cx, cy) in enumerate(chips):
            got = small_out.at[2 * cx + cy, 1 - c]
            _remote(got, got, small_sems.at[j], small_sems.at[3 + j], sibling).wait_recv()
        for cp in first + passed:
            cp.wait_send()
        small_local.wait()

    scratch = []
    for s in shards:
        scratch += _channel_scratch(s.shape[2], s.dtype)
    return pl.pallas_call(
        body, in_specs=[ANY] * (n + 1), out_specs=[ANY] * (n + 1),
        out_shape=[SDS((N_CHIPS, *s.shape), s.dtype) for s in (*shards, small)],
        scratch_shapes=[*scratch, pltpu.SemaphoreType.DMA((3 * n + 3,)), pltpu.SemaphoreType.DMA((3 * n + 3,)),
                        pltpu.SemaphoreType.DMA((7,))],
        compiler_params=pltpu.CompilerParams(vmem_limit_bytes=VMEM_LIMIT), name=name)(*shards, small)


def _pair_reduce(stacks, *, name):
    n = len(stacks)
    per = 10

    def body(*refs):
        ins, outs, scr = refs[:n], refs[n:2 * n], refs[2 * n:]
        x, y, c, _ = _place()
        sibling = (x, y, 1 - c)
        for t in range(n):
            sbuf, rbuf, obuf, pbuf, ld_s, ld_o, snd, rcv, st, credit = scr[per * t:per * (t + 1)]
            steps = ins[t].shape[1] // STREAM_ROWS
            src, own, out = ins[t].at[1 - c], ins[t].at[c], outs[t]

            def loads(i, slot, src=src, own=own, sbuf=sbuf, obuf=obuf, ld_s=ld_s, ld_o=ld_o):
                return (pltpu.make_async_copy(src.at[_stream_rows(i)], sbuf.at[slot], ld_s.at[slot]),
                        pltpu.make_async_copy(own.at[_stream_rows(i)], obuf.at[slot], ld_o.at[slot]))

            def push(slot, sbuf=sbuf, rbuf=rbuf, snd=snd, rcv=rcv):
                return _remote(sbuf.at[slot], rbuf.at[slot], snd.at[slot], rcv.at[slot], sibling)

            def store(i, slot, pbuf=pbuf, out=out, st=st):
                return pltpu.make_async_copy(pbuf.at[slot], out.at[_stream_rows(i)], st.at[slot])

            for cp in loads(0, 0):
                cp.start()

            def step(i, carry, loads=loads, push=push, store=store, rbuf=rbuf, obuf=obuf, pbuf=pbuf, credit=credit,
                     steps=steps):
                slot = lax.rem(i, 2)
                nxt = 1 - slot

                @pl.when(i + 1 < steps)
                def _():
                    @pl.when(i >= 1)
                    def _():
                        push(nxt).wait_send()
                    for cp in loads(i + 1, nxt):
                        cp.start()

                load_s, load_o = loads(i, slot)
                load_s.wait()

                @pl.when(i >= 2)
                def _():
                    pl.semaphore_wait(credit.at[slot], 1)

                push(slot).start()
                load_o.wait()
                push(slot).wait_recv()

                @pl.when(i >= 2)
                def _():
                    store(i, slot).wait()

                pbuf[slot] = (obuf[slot] + rbuf[slot]).astype(pbuf.dtype)
                store(i, slot).start()

                @pl.when(i + 2 < steps)
                def _():
                    pl.semaphore_signal(credit.at[slot], 1, device_id=sibling, device_id_type=MESH)
                return carry

            lax.fori_loop(0, steps, step, 0)
            for slot in range(2):
                push(slot).wait_send()
                store(0, slot).wait()

    scratch = []
    for s in stacks:
        buf = (2, STREAM_ROWS, s.shape[2])
        scratch += [pltpu.VMEM(buf, F32), pltpu.VMEM(buf, F32), pltpu.VMEM(buf, F32), pltpu.VMEM(buf, BF16),
                    *([pltpu.SemaphoreType.DMA((2,))] * 5), pltpu.SemaphoreType.REGULAR((2,))]
    return pl.pallas_call(
        body, in_specs=[ANY] * n, out_specs=[ANY] * n, out_shape=[SDS(s.shape[1:], BF16) for s in stacks],
        scratch_shapes=scratch, compiler_params=pltpu.CompilerParams(vmem_limit_bytes=VMEM_LIMIT), name=name)(*stacks)


def _chip_scatter(parts, *, name):
    n = len(parts)

    def body(*refs):
        ins, outs, scr = refs[:n], refs[n:2 * n], refs[2 * n:]
        chans = [scr[CHANNEL_REFS * t:CHANNEL_REFS * (t + 1)] for t in range(n)]
        send_sems, recv_sems = scr[CHANNEL_REFS * n:]
        x, y, c, chips = _place()
        me = 2 * x + y
        sends = []
        for t in range(n):
            for j, (cx, cy) in enumerate(chips):
                cp = _remote(ins[t].at[2 * cx + cy], outs[t].at[me], send_sems.at[3 * t + j], recv_sems.at[3 * t + j],
                             (cx, cy, c))
                cp.start()
                sends.append(cp)
        for t in range(n):
            _copy_through_vmem(ins[t].at[me], outs[t].at[me], chans[t])
        for t in range(n):
            for j, (cx, cy) in enumerate(chips):
                landed = outs[t].at[2 * cx + cy]
                _remote(landed, landed, send_sems.at[3 * t + j], recv_sems.at[3 * t + j], (cx, cy, c)).wait_recv()
        for cp in sends:
            cp.wait_send()

    scratch = []
    for p in parts:
        scratch += _channel_scratch(p.shape[2], p.dtype)
    return pl.pallas_call(
        body, in_specs=[ANY] * n, out_specs=[ANY] * n, out_shape=[SDS(p.shape, p.dtype) for p in parts],
        scratch_shapes=[*scratch, pltpu.SemaphoreType.DMA((3 * n,)), pltpu.SemaphoreType.DMA((3 * n,))],
        compiler_params=pltpu.CompilerParams(vmem_limit_bytes=VMEM_LIMIT), name=name)(*parts)


def _pair_share(finals, *, name):
    n = len(finals)

    def body(*refs):
        ins, outs, scr = refs[:n], refs[n:2 * n], refs[2 * n:]
        x, y, c, _ = _place()
        sibling = (x, y, 1 - c)
        for t in range(n):
            _exchange_stream(ins[t], outs[t].at[1 - c], outs[t].at[c], scr[CHANNEL_REFS * t:CHANNEL_REFS * (t + 1)],
                             sibling)

    scratch = []
    for f in finals:
        scratch += _channel_scratch(f.shape[1], f.dtype)
    return pl.pallas_call(
        body, in_specs=[ANY] * n, out_specs=[ANY] * n, out_shape=[SDS((2, *f.shape), f.dtype) for f in finals],
        scratch_shapes=scratch, compiler_params=pltpu.CompilerParams(vmem_limit_bytes=VMEM_LIMIT), name=name)(*finals)


def _all_reduce_small(v, *, name):
    rows, lanes = v.shape
    n_dev = 8

    def body(v_ref, o_ref, all_ref, send_sems, recv_sems, local_sem):
        x, y, c, chips = _place()
        me, sibling = (x, y, c), (x, y, 1 - c)

        def block(px, py, pc):
            return all_ref.at[4 * px + 2 * py + pc]

        def copy(k, blk, to, src=None):
            return _remote(block(*blk) if src is None else src, block(*blk), send_sems.at[k], recv_sems.at[k], to)

        mine = pltpu.make_async_copy(v_ref, block(*me), local_sem)
        mine.start()
        first = [copy(0, me, sibling, src=v_ref)]
        first += [copy(1 + j, me, (*chip, c), src=v_ref) for j, chip in enumerate(chips)]
        for cp in first:
            cp.start()
        passed = [copy(4 + j, (*chip, c), sibling) for j, chip in enumerate(chips)]
        for j, chip in enumerate(chips):
            copy(1 + j, (*chip, c), me).wait_recv()
            passed[j].start()
        copy(0, sibling, me).wait_recv()
        for j, chip in enumerate(chips):
            copy(4 + j, (*chip, 1 - c), me).wait_recv()
        for cp in first + passed:
            cp.wait_send()
        mine.wait()
        acc = all_ref[0]
        for k in range(1, n_dev):
            acc = acc + all_ref[k]
        o_ref[...] = acc

    vmem = pl.BlockSpec(memory_space=pltpu.VMEM)
    return pl.pallas_call(
        body, in_specs=[vmem], out_specs=vmem, out_shape=SDS((rows, lanes), F32),
        scratch_shapes=[pltpu.VMEM((n_dev, rows, lanes), F32), pltpu.SemaphoreType.DMA((7,)),
                        pltpu.SemaphoreType.DMA((7,)), pltpu.SemaphoreType.DMA],
        compiler_params=pltpu.CompilerParams(vmem_limit_bytes=VMEM_LIMIT), name=name)(v)


def _relu2_epilogue(acc):
    return acc, jnp.square(jnp.maximum(acc, 0.0))


def _res_epilogue(acc, res):
    return (acc + res,)


def _drelu2_epilogue(acc, pre):
    return (acc * (2.0 * jnp.maximum(pre.astype(F32), 0.0)),)


def _ffn_fwd(h, g, w1, w2, tag):
    f = _rms_fwd(h, g, name=f"ffn_norm_{tag}")
    pre, act = _mm_nn(f, w1, name=f"ffn1_{tag}", epilogue=_relu2_epilogue, n_out_dtypes=(BF16, BF16))
    h_out = _mm_nn(act, w2, name=f"ffn2_{tag}", extras=(h,), epilogue=_res_epilogue)
    return h_out, (f, pre, act)


def _ffn_bwd(dh, h, g, w1, w2, saved, layer, stacks):
    f, pre, act = saved
    dpre = _mm_nt(dh, w2, name=f"ffn2_dx_{layer}", out_dtype=BF16, extras=(pre,), epilogue=_drelu2_epilogue)
    dw2 = _mm_tn_stacked(act, dh, name=f"ffn2_dw_{layer}", half=layer, col_slots=False, stack=stacks[1])
    df = _mm_nt(dpre, w1, name=f"ffn1_dx_{layer}")
    dw1 = _mm_tn_stacked(f, dpre, name=f"ffn1_dw_{layer}", half=layer, col_slots=True, stack=stacks[0])
    dh, dg = _rms_bwd(h, g, df, dh, name=f"ffn_norm_bwd_{layer}")
    return dh, dg, (dw1, dw2)


def _kv_fwd(mem, g, w_kv, tag):
    m = _rms_fwd(mem, g, name=f"mem_norm_{tag}")
    return m, _mm_nn(m, w_kv, name=f"kv_{tag}")


def _kv_bwd(mem, g, w_kv, m, dk, dv, layer, stack):
    dkv = jnp.concatenate([dk, dv], axis=1)
    dw = _mm_tn_stacked(m, dkv, name=f"kv_dw_{layer}", half=layer, col_slots=True, stack=stack)
    dm = _mm_nt(dkv, w_kv, name=f"kv_dx_{layer}")
    _, dg = _rms_bwd(mem, g, dm, dm, name=f"mem_norm_bwd_{layer}")
    return dw, dg


def _local_step(x, mem, target, p):
    row = lambda v: v.reshape(1, -1)
    g = {}

    h0 = x
    a0 = _rms_fwd(h0, row(p["norm_mix"][0]), name="mix_norm_0")
    proj_a = _mm_nn(a0, p["a_in"], name="a_in")
    m0, kv0 = _kv_fwd(mem, row(p["mem_norm"][0]), p["w_kv"][0], "0")
    cat0 = _attn_fwd(proj_a, 2 * D_INNER, kv0, name="attn_0")
    bs_col = p["a_bs"].reshape(A_GROUPS, CHUNK, 1)
    cat0 = _gate_fwd(proj_a, p["a_ln_g"], p["a_ln_b"], p["a_ws"], bs_col, cat0, name="gate")
    h1 = _mm_nn(cat0, p["w_out"][0], name="out_0", extras=(h0,), epilogue=_res_epilogue)
    h2, ffn0 = _ffn_fwd(h1, row(p["norm_ffn"][0]), p["w_ffn1"][0], p["w_ffn2"][0], "0")

    a1 = _rms_fwd(h2, row(p["norm_mix"][1]), name="mix_norm_1")
    proj_b = _mm_nn(a1, p["b_in"], name="b_in")
    m1, kv1 = _kv_fwd(mem, row(p["mem_norm"][1]), p["w_kv"][1], "1")
    cat1 = _attn_fwd(proj_b, B_Q_OFF, kv1, name="attn_1")
    xbc = _conv_fwd(proj_b, p["b_conv_w"], p["b_conv_b"], name="conv")
    dt_raw = proj_b[:, B_DT_OFF:B_DT_OFF + SSM_HEADS].reshape(SEQ, SSM_GROUPS, SSM_HPG)
    dt_c = jnp.transpose(dt_raw, (1, 0, 2))
    dt_r = jnp.transpose(dt_raw, (1, 2, 0))
    per_head = lambda v: v.reshape(SSM_GROUPS, 1, SSM_HPG)
    ssd_par = (per_head(p["b_dt_bias"]), per_head(p["b_a_log"]), per_head(p["b_d"]), p["b_gnorm"])
    cat1, hprev = _ssd_fwd(xbc, proj_b, dt_c, dt_r, *ssd_par, cat1, name="ssd")
    h3 = _mm_nn(cat1, p["w_out"][1], name="out_1", extras=(h2,), epilogue=_res_epilogue)
    h4, ffn1 = _ffn_fwd(h3, row(p["norm_ffn"][1]), p["w_ffn1"][1], p["w_ffn2"][1], "1")

    loss, dh, g["final_norm"] = _loss_head(h4, row(p["final_norm"]), target, name="loss_head")

    dh, dnf1, dw_ffn = _ffn_bwd(dh, h3, row(p["norm_ffn"][1]), p["w_ffn1"][1], p["w_ffn2"][1], ffn1, 1, (None, None))
    dcat1 = _mm_nt(dh, p["w_out"][1], name="out_dx_1")
    dwo = _mm_tn_stacked(cat1, dh, name="out_dw_1", half=1, col_slots=False)
    dproj_b, dk1, dv1 = _attn_bwd(proj_b, B_Q_OFF, kv1, dcat1, B_IN_PAD, B_Q_OFF, name="attn_bwd_1")
    (dproj_b, dxs, dbm, dcm, ddt_c, ddt_r, g["b_dt_bias"], g["b_a_log"], g["b_d"], g["b_gnorm"]) = _ssd_bwd(
        xbc, proj_b, dt_c, dt_r, *ssd_par, hprev, dcat1, dproj_b, name="ssd_bwd")
    dproj_b, g["b_conv_w"], g["b_conv_b"] = _conv_bwd(proj_b, p["b_conv_w"], p["b_conv_b"], dxs, dbm, dcm, dproj_b,
                                                      name="conv_bwd")
    ddt = jnp.transpose(ddt_c, (1, 0, 2)) + jnp.transpose(ddt_r, (2, 0, 1))
    ddt = jnp.pad(ddt.reshape(SEQ, SSM_HEADS), ((0, 0), (0, B_IN_PAD - B_DT_OFF - SSM_HEADS))).astype(BF16)
    dproj_b = lax.dynamic_update_slice(dproj_b, ddt, (0, B_DT_OFF))
    dwkv, dmn1 = _kv_bwd(mem, row(p["mem_norm"][1]), p["w_kv"][1], m1, dk1, dv1, 1, None)
    half = D_MODEL // 2
    dwb = [_mm_tn(a1, dproj_b, name=f"b_in_dw_{i}", x_cols=(i * half, half)) for i in range(2)]
    da1 = _mm_nt(dproj_b, p["b_in"], name="b_in_dx")
    dh, dnm1 = _rms_bwd(h2, row(p["norm_mix"][1]), da1, dh, name="mix_norm_bwd_1")

    dh, dnf0, dw_ffn = _ffn_bwd(dh, h1, row(p["norm_ffn"][0]), p["w_ffn1"][0], p["w_ffn2"][0], ffn0, 0, dw_ffn)
    dcat0 = _mm_nt(dh, p["w_out"][0], name="out_dx_0")
    dwo = _mm_tn_stacked(cat0, dh, name="out_dw_0", half=0, col_slots=False, stack=dwo)
    dproj_a, dk0, dv0 = _attn_bwd(proj_a, 2 * D_INNER, kv0, dcat0, A_IN, 2 * D_INNER, name="attn_bwd_0")
    dproj_a, g["a_ln_g"], g["a_ln_b"], g["a_ws"], dbs_col = _gate_bwd(
        proj_a, p["a_ln_g"], p["a_ln_b"], p["a_ws"], bs_col, dcat0, dproj_a, name="gate_bwd")
    g["a_bs"] = dbs_col.reshape(A_GROUPS, CHUNK)
    dwkv, dmn0 = _kv_bwd(mem, row(p["mem_norm"][0]), p["w_kv"][0], m0, dk0, dv0, 0, dwkv)
    dwa = None
    for i in range(2):
        dwa = _mm_tn_stacked(a0, dproj_a, name=f"a_in_dw_{i}", half=i, col_slots=True, stack=dwa,
                             x_cols=(i * half, half))
    da0 = _mm_nt(dproj_a, p["a_in"], name="a_in_dx")
    dx, dnm0 = _rms_bwd(h0, row(p["norm_mix"][0]), da0, dh, name="mix_norm_bwd_0")

    g["norm_mix"] = jnp.concatenate([dnm0, dnm1], axis=0)
    g["norm_ffn"] = jnp.concatenate([dnf0, dnf1], axis=0)
    g["mem_norm"] = jnp.concatenate([dmn0, dmn1], axis=0)
    g["w_kv"], g["w_out"], (g["w_ffn1"], g["w_ffn2"]), g["a_in"] = dwkv, dwo, dw_ffn, dwa
    g["b_in"] = jnp.stack([_b_in_grad_slots(d) for d in dwb])
    return loss, dx, g


def _b_in_full(gathered):
    full = jnp.transpose(gathered, (1, 0, 2)).reshape(D_MODEL, B_IN)
    dt0 = D_INNER + CONV_DIM
    return jnp.concatenate([full[:, :dt0], full[:, dt0 + SSM_HEADS:], full[:, dt0:dt0 + SSM_HEADS],
                            jnp.zeros((D_MODEL, B_IN_PAD - B_IN), full.dtype)], axis=1)


def _b_in_grad_slots(d):
    dt0 = D_INNER + CONV_DIM
    full = jnp.concatenate([d[:, :dt0], d[:, B_DT_OFF:B_DT_OFF + SSM_HEADS], d[:, dt0:B_DT_OFF]], axis=1)
    return jnp.transpose(full.reshape(d.shape[0], N_CHIPS, B_IN // N_CHIPS), (1, 0, 2))


LARGE = ("w_kv", "w_out", "w_ffn1", "w_ffn2", "a_in", "b_in")
SMALL_REPL = ("norm_mix", "norm_ffn", "mem_norm", "a_ln_g", "a_ln_b", "a_ws", "a_bs", "b_dt_bias", "b_a_log", "b_d",
              "final_norm")
SMALL_SHARD = ("b_conv_w", "b_conv_b", "b_gnorm")
WEIGHTS = ("norm_mix", "norm_ffn", "mem_norm", "w_kv", "w_out", "w_ffn1", "w_ffn2", "a_in", "a_ln_g", "a_ln_b", "a_ws",
           "a_bs", "b_in", "b_conv_w", "b_conv_b", "b_dt_bias", "b_a_log", "b_d", "b_gnorm", "final_norm")
CONV_SHARD = CONV_DIM // N_CHIPS
GN_SHARD = D_INNER // N_CHIPS


def _halves(w):
    if w.shape[0] == 2:
        return w
    return w.reshape(2, w.shape[1] // 2, w.shape[2])


def _gather_weights(w):
    big = [_halves(w[k]).astype(BF16) for k in LARGE]
    small = jnp.zeros((2, CONV_K, CONV_SHARD), F32)
    small = small.at[0].set(w["b_conv_w"][0])
    small = small.at[1, 0].set(w["b_conv_b"][0])
    small = small.at[1, 1, :GN_SHARD].set(w["b_gnorm"][0])
    gathered = _all_gather_shards(big, small, name="gather_weights")
    p = {}
    kv, wo, w1, w2, a_in, b_in, sm = gathered
    p["w_kv"] = [kv[:, l] for l in range(2)]
    p["w_out"] = [wo[:, l].reshape(MIX_OUT, D_MODEL) for l in range(2)]
    p["w_ffn1"] = [w1[:, l] for l in range(2)]
    p["w_ffn2"] = [w2[:, l].reshape(D_FF, D_MODEL) for l in range(2)]
    p["a_in"] = a_in.reshape(N_CHIPS, D_MODEL, A_IN // N_CHIPS)
    p["b_in"] = _b_in_full(b_in.reshape(N_CHIPS, D_MODEL, B_IN // N_CHIPS))
    p["b_conv_w"] = jnp.transpose(sm[:, 0], (1, 0, 2)).reshape(CONV_K, CONV_DIM)
    p["b_conv_b"] = sm[:, 1, 0].reshape(1, CONV_DIM)
    p["b_gnorm"] = sm[:, 1, 1, :GN_SHARD].reshape(1, D_INNER)
    return p


def _reduce_large(g):
    stacks = [g[k].reshape(2, -1, g[k].shape[-1]) for k in LARGE]
    parts = _pair_reduce(stacks, name="grads_pair_reduce")
    parts = [t.reshape(N_CHIPS, -1, t.shape[-1]) for t in parts]
    landed = _chip_scatter(parts, name="grads_chip_scatter")
    finals = [_sum_slots(t, name=f"grads_chip_sum_{k}") for k, t in zip(LARGE, landed)]
    shared = _pair_share(finals, name="grads_pair_share")
    return dict(zip(LARGE, shared))


def _small_layout(shapes):
    offs, o = {}, 0
    for k in (*SMALL_REPL, *SMALL_SHARD):
        size = math.prod(shapes[k])
        offs[k] = (o, size)
        o += size
    rows = -(-o // (8 * 128)) * 8
    return offs, rows


def _reduce_small(g, full_shapes):
    offs, rows = _small_layout(full_shapes)
    flat = jnp.concatenate([g[k].reshape(-1) for k in (*SMALL_REPL, *SMALL_SHARD)])
    flat = jnp.pad(flat, (0, rows * 128 - flat.shape[0])).reshape(rows, 128)
    total = _all_reduce_small(flat, name="grads_small_all_reduce").reshape(-1)
    return {k: total[o:o + n].reshape(full_shapes[k]) for k, (o, n) in offs.items()}


def kernel(x, mem, norm_mix, norm_ffn, mem_norm, w_kv, w_out, w_ffn1, w_ffn2, a_in, a_ln_g, a_ln_b, a_ws, a_bs, b_in, b_conv_w, b_conv_b, b_dt_bias, b_a_log, b_d, b_gnorm, final_norm, loss_target, m_norm_mix, m_norm_ffn, m_mem_norm, m_w_kv, m_w_out, m_w_ffn1, m_w_ffn2, m_a_in, m_a_ln_g, m_a_ln_b, m_a_ws, m_a_bs, m_b_in, m_b_conv_w, m_b_conv_b, m_b_dt_bias, m_b_a_log, m_b_d, m_b_gnorm, m_final_norm, v_norm_mix, v_norm_ffn, v_mem_norm, v_w_kv, v_w_out, v_w_ffn1, v_w_ffn2, v_a_in, v_a_ln_g, v_a_ln_b, v_a_ws, v_a_bs, v_b_in, v_b_conv_w, v_b_conv_b, v_b_dt_bias, v_b_a_log, v_b_d, v_b_gnorm, v_final_norm):
    w = dict(norm_mix=norm_mix, norm_ffn=norm_ffn, mem_norm=mem_norm, w_kv=w_kv, w_out=w_out, w_ffn1=w_ffn1,
             w_ffn2=w_ffn2, a_in=a_in, a_ln_g=a_ln_g, a_ln_b=a_ln_b, a_ws=a_ws, a_bs=a_bs, b_in=b_in, b_conv_w=b_conv_w,
             b_conv_b=b_conv_b, b_dt_bias=b_dt_bias, b_a_log=b_a_log, b_d=b_d, b_gnorm=b_gnorm, final_norm=final_norm)
    mom = dict(norm_mix=m_norm_mix, norm_ffn=m_norm_ffn, mem_norm=m_mem_norm, w_kv=m_w_kv, w_out=m_w_out,
               w_ffn1=m_w_ffn1, w_ffn2=m_w_ffn2, a_in=m_a_in, a_ln_g=m_a_ln_g, a_ln_b=m_a_ln_b, a_ws=m_a_ws,
               a_bs=m_a_bs, b_in=m_b_in, b_conv_w=m_b_conv_w, b_conv_b=m_b_conv_b, b_dt_bias=m_b_dt_bias,
               b_a_log=m_b_a_log, b_d=m_b_d, b_gnorm=m_b_gnorm, final_norm=m_final_norm)
    var = dict(norm_mix=v_norm_mix, norm_ffn=v_norm_ffn, mem_norm=v_mem_norm, w_kv=v_w_kv, w_out=v_w_out,
               w_ffn1=v_w_ffn1, w_ffn2=v_w_ffn2, a_in=v_a_in, a_ln_g=v_a_ln_g, a_ln_b=v_a_ln_b, a_ws=v_a_ws,
               a_bs=v_a_bs, b_in=v_b_in, b_conv_w=v_b_conv_w, b_conv_b=v_b_conv_b, b_dt_bias=v_b_dt_bias,
               b_a_log=v_b_a_log, b_d=v_b_d, b_gnorm=v_b_gnorm, final_norm=v_final_norm)

    p = _gather_weights(w)
    p.update(norm_mix=norm_mix, norm_ffn=norm_ffn, mem_norm=mem_norm, a_ln_g=a_ln_g, a_ln_b=a_ln_b, a_ws=a_ws[0],
             a_bs=a_bs[0], b_dt_bias=b_dt_bias, b_a_log=b_a_log, b_d=b_d, final_norm=final_norm)
    loss_part, dx, g = _local_step(x[0], mem[0], loss_target[0], p)
    loss = lax.psum(loss_part[0, 0], ("x", "y", "c"))

    full_shapes = {k: w[k].shape for k in SMALL_REPL}
    full_shapes.update(b_conv_w=(1, CONV_K, CONV_DIM), b_conv_b=(1, CONV_DIM), b_gnorm=(1, D_INNER))
    gs = _reduce_small(g, full_shapes)
    chip = 2 * lax.axis_index("x") + lax.axis_index("y")
    gs["b_conv_w"] = lax.dynamic_slice_in_dim(gs["b_conv_w"], chip * CONV_SHARD, CONV_SHARD, axis=2)
    gs["b_conv_b"] = lax.dynamic_slice_in_dim(gs["b_conv_b"], chip * CONV_SHARD, CONV_SHARD, axis=1)
    gs["b_gnorm"] = lax.dynamic_slice_in_dim(gs["b_gnorm"], chip * GN_SHARD, GN_SHARD, axis=1)
    gl = _reduce_large(g)
    grads = {k: (gl[k].reshape(w[k].shape) if k in gl else gs[k]) for k in WEIGHTS}

    delta, new_m, new_v = {}, {}, {}
    for k in WEIGHTS:
        shape = w[k].shape
        flat = (lambda a: a.reshape(-1, shape[-1])) if len(shape) > 1 else (lambda a: a.reshape(1, -1))
        d, m_new, v_new = _adamw(flat(w[k]), flat(grads[k]), flat(mom[k]), flat(var[k]), name=f"adamw_{k}")
        delta[k], new_m[k], new_v[k] = d.reshape(shape), m_new.reshape(shape), v_new.reshape(shape)

    return (loss, dx.reshape(x.shape), *[grads[k] for k in WEIGHTS], *[delta[k] for k in WEIGHTS],
            *[new_m[k] for k in WEIGHTS], *[new_v[k] for k in WEIGHTS])
```

```python
import math

import jax
import jax.numpy as jnp
from jax import lax
from jax.experimental import pallas as pl
from jax.experimental.pallas import tpu as pltpu

F32 = jnp.float32
BF16 = jnp.bfloat16
SDS = jax.ShapeDtypeStruct

D_MODEL = 1024
SEQ = 2048
CHUNK = 128
N_MEM = 256
D_INNER = 2048
A_GROUPS = 8
A_GROUP_W = D_INNER // A_GROUPS
SSM_HEADS = 32
SSM_HEAD_DIM = 64
SSM_GROUPS = 4
SSM_HPG = 8
SSM_STATE = 128
SSM_GROUP_W = SSM_HPG * SSM_HEAD_DIM
CONV_K = 4
CONV_DIM = 3072
X_HEADS = 4
X_HEAD_DIM = 256
X_WIDTH = 1024
MIX_OUT = 3072
D_FF = 4096
A_IN = 5120
B_IN = 6176
B_IN_PAD = 6272
B_Q_OFF = 5120
B_DT_OFF = 6144
N_CHUNKS = SEQ // CHUNK
EPS = 1e-6
N_CHIPS = 4

ADAM_LR = 0.001
ADAM_B1 = 0.9
ADAM_B2 = 0.999
ADAM_EPS = 1e-08
ADAM_WD = 0.01
ADAM_STEP = 10

VMEM_LIMIT = 48 * 1024 * 1024
MESH = pl.DeviceIdType.MESH


def _cparams(sem):
    return pltpu.CompilerParams(dimension_semantics=sem, vmem_limit_bytes=VMEM_LIMIT)


def _dot(a, b, dims=(((1,), (0,)), ((), ()))):
    return lax.dot_general(a.astype(BF16), b.astype(BF16), dims, preferred_element_type=F32)


def _dot_nt(a, b):
    return _dot(a, b, (((1,), (1,)), ((), ())))


def _dot_tn(a, b):
    return _dot(a, b, (((0,), (0,)), ((), ())))


def _pick(n, cands):
    for c in cands:
        if n % c == 0:
            return c
    raise ValueError(f"no tile for {n}")


def _mm_call(a, b, *, dims, grid, a_spec, b_spec, acc_shape, out_shapes, out_specs, name,
             extras=(), extra_specs=(), epilogue=None):
    n_k = grid[2]
    n_extra = len(extras)
    n_out = len(out_shapes)

    def body(*refs):
        a_ref, b_ref = refs[0], refs[1]
        extra_refs = refs[2:2 + n_extra]
        out_refs = refs[2 + n_extra:2 + n_extra + n_out]
        acc = refs[-1]
        k = pl.program_id(2)

        @pl.when(k == 0)
        def _():
            acc[...] = jnp.zeros_like(acc)

        acc[...] += _dot(a_ref[...], b_ref[...], dims)

        @pl.when(k == n_k - 1)
        def _():
            vals = (acc[...],) if epilogue is None else epilogue(acc[...], *[e[...] for e in extra_refs])
            for o_ref, v in zip(out_refs, vals):
                o_ref[...] = v.astype(o_ref.dtype)

    return pl.pallas_call(
        body, grid=grid, in_specs=[a_spec, b_spec, *extra_specs], out_specs=list(out_specs),
        out_shape=list(out_shapes), scratch_shapes=[pltpu.VMEM(acc_shape, F32)],
        compiler_params=_cparams(("parallel", "parallel", "arbitrary")), name=name,
    )(a, b, *extras)


def _w_dims(w):
    if w.ndim == 2:
        return w.shape[0], w.shape[1], 1, w.shape[1]
    return w.shape[1], w.shape[0] * w.shape[2], w.shape[0], w.shape[2]


def _mm_nn(a, w, *, name, out_dtype=F32, a_cols=None, extras=(), epilogue=None, n_out_dtypes=None):
    m = a.shape[0]
    k_dim, n_dim, _, n_slot = _w_dims(w)
    a_off, a_w = (0, a.shape[1]) if a_cols is None else a_cols
    assert a_w == k_dim
    tm = _pick(m, (2048, 1024, 512, 256))
    tn = _pick(n_slot, (512, 896, 640, 256, 128))
    tk = _pick(k_dim, (1024, 768, 512, 384, 256, 128))
    assert a_off % tk == 0
    nb = n_slot // tn
    a_spec = pl.BlockSpec((tm, tk), lambda i, j, k: (i, a_off // tk + k))
    if w.ndim == 2:
        b_spec = pl.BlockSpec((tk, tn), lambda i, j, k: (k, j))
    else:
        b_spec = pl.BlockSpec((None, tk, tn), lambda i, j, k: (j // nb, k, j % nb))
    o_spec = pl.BlockSpec((tm, tn), lambda i, j, k: (i, j))
    dts = n_out_dtypes or (out_dtype,)
    outs = _mm_call(a, w, dims=(((1,), (0,)), ((), ())), grid=(m // tm, n_dim // tn, k_dim // tk),
                    a_spec=a_spec, b_spec=b_spec, acc_shape=(tm, tn),
                    out_shapes=[SDS((m, n_dim), dt) for dt in dts], out_specs=[o_spec] * len(dts), name=name,
                    extras=extras, extra_specs=[o_spec] * len(extras), epilogue=epilogue)
    return outs if n_out_dtypes else outs[0]


def _mm_nt(a, w, *, name, out_dtype=F32, extras=(), epilogue=None):
    m = a.shape[0]
    k_dim, n_dim, _, n_slot = _w_dims(w)
    assert a.shape[1] == n_dim
    tm = _pick(m, (2048, 1024, 512, 256))
    to = _pick(k_dim, (512, 384, 256, 128))
    tc = _pick(n_slot, (1024, 896, 640, 512, 256, 128))
    nb = n_slot // tc
    a_spec = pl.BlockSpec((tm, tc), lambda i, j, k: (i, k))
    if w.ndim == 2:
        b_spec = pl.BlockSpec((to, tc), lambda i, j, k: (j, k))
    else:
        b_spec = pl.BlockSpec((None, to, tc), lambda i, j, k: (k // nb, j, k % nb))
    o_spec = pl.BlockSpec((tm, to), lambda i, j, k: (i, j))
    return _mm_call(a, w, dims=(((1,), (1,)), ((), ())), grid=(m // tm, k_dim // to, n_dim // tc),
                    a_spec=a_spec, b_spec=b_spec, acc_shape=(tm, to),
                    out_shapes=[SDS((m, k_dim), out_dtype)], out_specs=[o_spec], name=name,
                    extras=extras, extra_specs=[o_spec] * len(extras), epilogue=epilogue)[0]


def _mm_tn(x, dy, *, name, x_cols=None):
    s = x.shape[0]
    x_off, k_dim = (0, x.shape[1]) if x_cols is None else x_cols
    n_dim = dy.shape[1]
    tm = _pick(k_dim, (1024, 768, 512, 384, 256, 128))
    tn = _pick(n_dim, (512, 896, 640, 256, 128))
    tk = _pick(s, (2048, 1024, 512, 256))
    assert x_off % tm == 0
    a_spec = pl.BlockSpec((tk, tm), lambda i, j, k: (k, x_off // tm + i))
    b_spec = pl.BlockSpec((tk, tn), lambda i, j, k: (k, j))
    o_spec = pl.BlockSpec((tm, tn), lambda i, j, k: (i, j))
    return _mm_call(x, dy, dims=(((0,), (0,)), ((), ())), grid=(k_dim // tm, n_dim // tn, s // tk),
                    a_spec=a_spec, b_spec=b_spec, acc_shape=(tm, tn),
                    out_shapes=[SDS((k_dim, n_dim), F32)], out_specs=[o_spec], name=name)[0]


def _mm_tn_stacked(x, dy, *, name, half, col_slots, stack=None, x_cols=None):
    s = x.shape[0]
    x_off, k_dim = (0, x.shape[1]) if x_cols is None else x_cols
    n_dim = dy.shape[1]
    r, c = (k_dim, n_dim // N_CHIPS) if col_slots else (k_dim // N_CHIPS, n_dim)
    tm = _pick(r, (1024, 768, 512, 384, 256, 128))
    tn = _pick(c, (512, 896, 640, 256, 128))
    tk = _pick(s, (2048, 1024, 512, 256))
    assert x_off % tm == 0
    a_spec = pl.BlockSpec((tk, tm), lambda i, j, k: (k, x_off // tm + i))
    b_spec = pl.BlockSpec((tk, tn), lambda i, j, k: (k, j))
    if col_slots:
        nb = c // tn
        o_spec = pl.BlockSpec((None, None, tm, tn), lambda i, j, k: (half, j // nb, i, j % nb))
    else:
        nb = r // tm
        o_spec = pl.BlockSpec((None, None, tm, tn), lambda i, j, k: (half, i // nb, i % nb, j))
    n_k = s // tk

    def body(a_ref, b_ref, *rest):
        o_ref, acc = rest[-2], rest[-1]
        k = pl.program_id(2)

        @pl.when(k == 0)
        def _():
            acc[...] = jnp.zeros_like(acc)

        acc[...] += _dot_tn(a_ref[...], b_ref[...])

        @pl.when(k == n_k - 1)
        def _():
            o_ref[...] = acc[...]

    keep = [] if stack is None else [stack]
    return pl.pallas_call(
        body, grid=(k_dim // tm, n_dim // tn, n_k), in_specs=[a_spec, b_spec, *([ANY] * len(keep))],
        out_specs=o_spec, out_shape=SDS((2, N_CHIPS, r, c), F32), scratch_shapes=[pltpu.VMEM((tm, tn), F32)],
        input_output_aliases={2: 0} if keep else {},
        compiler_params=_cparams(("parallel", "parallel", "arbitrary")), name=name,
    )(x, dy, *keep)


def _rms(x, g):
    return x * lax.rsqrt(jnp.mean(x * x, axis=-1, keepdims=True) + EPS) * g


def _rms_fwd(h, g, *, name):
    rows, d = h.shape
    tr = _pick(rows, (512, 256))

    def body(h_ref, g_ref, o_ref):
        o_ref[...] = _rms(h_ref[...], g_ref[...]).astype(o_ref.dtype)

    return pl.pallas_call(
        body, grid=(rows // tr,),
        in_specs=[pl.BlockSpec((tr, d), lambda i: (i, 0)), pl.BlockSpec((1, d), lambda i: (0, 0))],
        out_specs=pl.BlockSpec((tr, d), lambda i: (i, 0)), out_shape=SDS((rows, d), BF16),
        compiler_params=_cparams(("parallel",)), name=name)(h, g)


def _rms_bwd(h, g, da, dres, *, name):
    rows, d = h.shape
    tr = _pick(rows, (512, 256))

    def body(h_ref, g_ref, da_ref, dres_ref, dh_ref, dg_ref):
        _, vjp = jax.vjp(_rms, h_ref[...], g_ref[...])
        dh, dg = vjp(da_ref[...].astype(F32))
        dh_ref[...] = dres_ref[...] + dh

        @pl.when(pl.program_id(0) == 0)
        def _():
            dg_ref[...] = jnp.zeros_like(dg_ref)

        dg_ref[...] += dg

    row_spec = pl.BlockSpec((tr, d), lambda i: (i, 0))
    vec_spec = pl.BlockSpec((1, d), lambda i: (0, 0))
    return pl.pallas_call(
        body, grid=(rows // tr,), in_specs=[row_spec, vec_spec, row_spec, row_spec],
        out_specs=[row_spec, vec_spec], out_shape=[SDS((rows, d), F32), SDS((1, d), F32)],
        compiler_params=_cparams(("arbitrary",)), name=name)(h, g, da, dres)


def _loss_head(h, g, target, *, name):
    rows, d = h.shape
    tr = _pick(rows, (512, 256))

    def body(h_ref, g_ref, t_ref, loss_ref, dh_ref, dg_ref):
        y, vjp = jax.vjp(_rms, h_ref[...], g_ref[...])
        err = y - t_ref[...]
        dh, dg = vjp(err * (1.0 / d))
        dh_ref[...] = dh

        @pl.when(pl.program_id(0) == 0)
        def _():
            dg_ref[...] = jnp.zeros_like(dg_ref)
            loss_ref[...] = jnp.zeros_like(loss_ref)

        dg_ref[...] += dg
        part = jnp.sum(jnp.sum(err * err, axis=-1, keepdims=True), axis=0, keepdims=True) * (0.5 / d)
        loss_ref[...] += jnp.broadcast_to(part, loss_ref.shape)

    row_spec = pl.BlockSpec((tr, d), lambda i: (i, 0))
    vec_spec = pl.BlockSpec((1, d), lambda i: (0, 0))
    loss_spec = pl.BlockSpec((8, 128), lambda i: (0, 0))
    return pl.pallas_call(
        body, grid=(rows // tr,), in_specs=[row_spec, vec_spec, row_spec],
        out_specs=[loss_spec, row_spec, vec_spec],
        out_shape=[SDS((8, 128), F32), SDS((rows, d), F32), SDS((1, d), F32)],
        compiler_params=_cparams(("arbitrary",)), name=name)(h, g, target)


def _gelu(x):
    return 0.5 * x * (1.0 + lax.erf(x * (1.0 / math.sqrt(2.0))))


def _gate_tile(pu, pv, ln_g, ln_b, ws, bs_t):
    u = [_gelu(p) for p in pu]
    v = [_gelu(p) for p in pv]
    mu = sum(jnp.sum(t, axis=-1, keepdims=True) for t in v) * (1.0 / D_INNER)
    vc = [t - mu for t in v]
    var = sum(jnp.sum(t * t, axis=-1, keepdims=True) for t in vc) * (1.0 / D_INNER)
    rstd = lax.rsqrt(var + EPS)
    row = lax.broadcasted_iota(jnp.int32, (CHUNK, CHUNK), 0)
    col = lax.broadcasted_iota(jnp.int32, (CHUNK, CHUNK), 1)
    out = []
    for gi in range(A_GROUPS):
        vn = vc[gi] * rstd * ln_g[gi] + ln_b[gi]
        w = jnp.where(row >= col, ws[gi], 0.0)
        sv = _dot(w, vn) + bs_t[gi]
        out.append(u[gi] * sv)
    return out


def _split(ref, n, width):
    return [ref[:, i * width:(i + 1) * width] for i in range(n)]


def _gate_in_specs():
    return [
        pl.BlockSpec((CHUNK, D_INNER), lambda c: (c, 0)),
        pl.BlockSpec((CHUNK, D_INNER), lambda c: (c, 1)),
        pl.BlockSpec((1, D_INNER), lambda c: (0, 0)),
        pl.BlockSpec((1, D_INNER), lambda c: (0, 0)),
        pl.BlockSpec((A_GROUPS, CHUNK, CHUNK), lambda c: (0, 0, 0)),
        pl.BlockSpec((A_GROUPS, CHUNK, 1), lambda c: (0, 0, 0)),
    ]


def _gate_args(u_ref, v_ref, g_ref, b_ref, ws_ref, bs_ref):
    ng, gw = A_GROUPS, A_GROUP_W
    return (_split(u_ref, ng, gw), _split(v_ref, ng, gw), _split(g_ref, ng, gw), _split(b_ref, ng, gw),
            [ws_ref[i] for i in range(ng)], [bs_ref[i] for i in range(ng)])


def _gate_fwd(proj, ln_g, ln_b, ws, bs_col, mixcat, *, name):
    def body(u_ref, v_ref, g_ref, b_ref, ws_ref, bs_ref, cat_in, cat_ref):
        del cat_in
        out = _gate_tile(*_gate_args(u_ref, v_ref, g_ref, b_ref, ws_ref, bs_ref))
        for gi, o in enumerate(out):
            cat_ref[:, gi * A_GROUP_W:(gi + 1) * A_GROUP_W] = o.astype(cat_ref.dtype)

    return pl.pallas_call(
        body, grid=(N_CHUNKS,), in_specs=[*_gate_in_specs(), pl.BlockSpec(memory_space=pl.ANY)],
        out_specs=pl.BlockSpec((CHUNK, D_INNER), lambda c: (c, 0)), out_shape=SDS(mixcat.shape, mixcat.dtype),
        input_output_aliases={6: 0}, compiler_params=_cparams(("parallel",)), name=name,
    )(proj, proj, ln_g, ln_b, ws, bs_col, mixcat)


def _gate_bwd(proj, ln_g, ln_b, ws, bs_col, dcat, dproj, *, name):
    ng, gw = A_GROUPS, A_GROUP_W

    def body(u_ref, v_ref, g_ref, b_ref, ws_ref, bs_ref, d_ref, dproj_in, dproj_ref, dg_ref, db_ref, dws_ref, dbs_ref):
        del dproj_in
        args = _gate_args(u_ref, v_ref, g_ref, b_ref, ws_ref, bs_ref)
        _, vjp = jax.vjp(_gate_tile, *args)
        dpu, dpv, dg, db, dws, dbs = vjp(_split(d_ref, ng, gw))
        for gi in range(ng):
            dproj_ref[:, gi * gw:(gi + 1) * gw] = dpu[gi].astype(dproj_ref.dtype)
            dproj_ref[:, D_INNER + gi * gw:D_INNER + (gi + 1) * gw] = dpv[gi].astype(dproj_ref.dtype)

        @pl.when(pl.program_id(0) == 0)
        def _():
            for r in (dg_ref, db_ref, dws_ref, dbs_ref):
                r[...] = jnp.zeros_like(r)

        for gi in range(ng):
            dg_ref[:, gi * gw:(gi + 1) * gw] += dg[gi]
            db_ref[:, gi * gw:(gi + 1) * gw] += db[gi]
            dws_ref[gi] += dws[gi]
            dbs_ref[gi] += dbs[gi]

    in_specs = _gate_in_specs()
    return pl.pallas_call(
        body, grid=(N_CHUNKS,),
        in_specs=[*in_specs, pl.BlockSpec((CHUNK, D_INNER), lambda c: (c, 0)), pl.BlockSpec(memory_space=pl.ANY)],
        out_specs=[pl.BlockSpec((CHUNK, 2 * D_INNER), lambda c: (c, 0)), *in_specs[2:]],
        out_shape=[SDS(dproj.shape, dproj.dtype), SDS((1, D_INNER), F32), SDS((1, D_INNER), F32),
                   SDS((ng, CHUNK, CHUNK), F32), SDS((ng, CHUNK, 1), F32)],
        input_output_aliases={7: 0}, compiler_params=_cparams(("arbitrary",)), name=name,
    )(proj, proj, ln_g, ln_b, ws, bs_col, dcat, dproj)


ATT_TQ = 512


def _attn_tile(q, k, v):
    s = _dot_nt(q, k) * (1.0 / math.sqrt(X_HEAD_DIM))
    s = s - jnp.max(s, axis=-1, keepdims=True)
    e = jnp.exp(s)
    p = e / jnp.sum(e, axis=-1, keepdims=True)
    return _dot(p, v)


def _attn_in_specs(q_blk, order):
    hd = X_HEAD_DIM
    return [
        pl.BlockSpec((ATT_TQ, hd), lambda a, b: (order(a, b)[0], q_blk + order(a, b)[1])),
        pl.BlockSpec((N_MEM, hd), lambda a, b: (0, order(a, b)[1])),
        pl.BlockSpec((N_MEM, hd), lambda a, b: (0, X_HEADS + order(a, b)[1])),
    ]


def _attn_fwd(proj, q_off, kv, *, name):
    order = lambda i, h: (i, h)
    cat_blk = D_INNER // X_HEAD_DIM

    def body(q_ref, k_ref, v_ref, o_ref):
        o_ref[...] = _attn_tile(q_ref[...], k_ref[...], v_ref[...]).astype(o_ref.dtype)

    return pl.pallas_call(
        body, grid=(SEQ // ATT_TQ, X_HEADS), in_specs=_attn_in_specs(q_off // X_HEAD_DIM, order),
        out_specs=pl.BlockSpec((ATT_TQ, X_HEAD_DIM), lambda i, h: (i, cat_blk + h)),
        out_shape=SDS((SEQ, MIX_OUT), BF16), compiler_params=_cparams(("parallel", "parallel")), name=name,
    )(proj, kv, kv)


def _attn_bwd(proj, q_off, kv, dcat, dproj_width, dq_off, *, name):
    order = lambda h, i: (i, h)
    cat_blk = D_INNER // X_HEAD_DIM
    dq_blk = dq_off // X_HEAD_DIM

    def body(q_ref, k_ref, v_ref, do_ref, dq_ref, dk_ref, dv_ref):
        _, vjp = jax.vjp(_attn_tile, q_ref[...], k_ref[...], v_ref[...])
        dq, dk, dv = vjp(do_ref[...])
        dq_ref[...] = dq.astype(dq_ref.dtype)

        @pl.when(pl.program_id(1) == 0)
        def _():
            dk_ref[...] = jnp.zeros_like(dk_ref)
            dv_ref[...] = jnp.zeros_like(dv_ref)

        dk_ref[...] += dk
        dv_ref[...] += dv

    kv_spec = pl.BlockSpec((N_MEM, X_HEAD_DIM), lambda h, i: (0, h))
    return pl.pallas_call(
        body, grid=(X_HEADS, SEQ // ATT_TQ),
        in_specs=[*_attn_in_specs(q_off // X_HEAD_DIM, order),
                  pl.BlockSpec((ATT_TQ, X_HEAD_DIM), lambda h, i: (i, cat_blk + h))],
        out_specs=[pl.BlockSpec((ATT_TQ, X_HEAD_DIM), lambda h, i: (i, dq_blk + h)), kv_spec, kv_spec],
        out_shape=[SDS((SEQ, dproj_width), BF16), SDS((N_MEM, X_WIDTH), F32), SDS((N_MEM, X_WIDTH), F32)],
        compiler_params=_cparams(("parallel", "arbitrary")), name=name,
    )(proj, kv, kv, dcat)


CONV_TC = 512


def _shift_down(x, s):
    if s == 0:
        return x
    row = lax.broadcasted_iota(jnp.int32, x.shape, 0)
    return jnp.where(row >= s, pltpu.roll(x, s, 0), 0.0)


def _shift_up(x, s):
    if s == 0:
        return x
    n = x.shape[0]
    row = lax.broadcasted_iota(jnp.int32, x.shape, 0)
    return jnp.where(row < n - s, pltpu.roll(x, n - s, 0), 0.0)


def _conv_pre(x, w_ref, b_ref):
    pre = b_ref[...] + jnp.zeros_like(x)
    for k in range(CONV_K):
        pre = pre + w_ref[k:k + 1, :] * _shift_down(x, CONV_K - 1 - k)
    return pre


def _conv_fwd(proj, w, b, *, name):
    blk0 = D_INNER // CONV_TC

    def body(x_ref, w_ref, b_ref, o_ref):
        pre = _conv_pre(x_ref[...], w_ref, b_ref)
        o_ref[...] = pre * jax.nn.sigmoid(pre)

    return pl.pallas_call(
        body, grid=(CONV_DIM // CONV_TC,),
        in_specs=[pl.BlockSpec((SEQ, CONV_TC), lambda j: (0, blk0 + j)), pl.BlockSpec((CONV_K, CONV_TC), lambda j: (0, j)),
                  pl.BlockSpec((1, CONV_TC), lambda j: (0, j))],
        out_specs=pl.BlockSpec((SEQ, CONV_TC), lambda j: (0, j)), out_shape=SDS((SEQ, CONV_DIM), F32),
        compiler_params=_cparams(("parallel",)), name=name)(proj, w, b)


def _conv_bwd(proj, w, b, dxs, dbm, dcm, dproj, *, name):
    tc = CONV_TC // 2
    blk0 = D_INNER // tc
    n_x = D_INNER // tc
    n_b = SSM_GROUPS * SSM_STATE // tc

    def body(x_ref, w_ref, b_ref, dxs_ref, dbm_ref, dcm_ref, dproj_in, dproj_ref, dw_ref, db_ref):
        del dproj_in
        j = pl.program_id(0)
        x = x_ref[...]
        pre = _conv_pre(x, w_ref, b_ref)
        sg = jax.nn.sigmoid(pre)
        dact = jnp.where(j < n_x, dxs_ref[...], jnp.where(j < n_x + n_b, dbm_ref[...], dcm_ref[...]))
        dpre = dact * (sg * (1.0 + pre * (1.0 - sg)))
        dx = jnp.zeros_like(x)
        for k in range(CONV_K):
            s = CONV_K - 1 - k
            dx = dx + w_ref[k:k + 1, :] * _shift_up(dpre, s)
            dw_ref[k:k + 1, :] = jnp.sum(dpre * _shift_down(x, s), axis=0, keepdims=True)
        dproj_ref[...] = dx.astype(dproj_ref.dtype)
        db_ref[...] = jnp.sum(dpre, axis=0, keepdims=True)

    clip = lambda v, hi: jnp.minimum(jnp.maximum(v, 0), hi)
    return pl.pallas_call(
        body, grid=(CONV_DIM // tc,),
        in_specs=[pl.BlockSpec((SEQ, tc), lambda j: (0, blk0 + j)), pl.BlockSpec((CONV_K, tc), lambda j: (0, j)),
                  pl.BlockSpec((1, tc), lambda j: (0, j)),
                  pl.BlockSpec((SEQ, tc), lambda j: (0, clip(j, n_x - 1))),
                  pl.BlockSpec((SEQ, tc), lambda j: (0, clip(j - n_x, n_b - 1))),
                  pl.BlockSpec((SEQ, tc), lambda j: (0, clip(j - n_x - n_b, n_b - 1))),
                  pl.BlockSpec(memory_space=pl.ANY)],
        out_specs=[pl.BlockSpec((SEQ, tc), lambda j: (0, blk0 + j)), pl.BlockSpec((CONV_K, tc), lambda j: (0, j)),
                   pl.BlockSpec((1, tc), lambda j: (0, j))],
        out_shape=[SDS(dproj.shape, dproj.dtype), SDS((CONV_K, CONV_DIM), F32), SDS((1, CONV_DIM), F32)],
        input_output_aliases={6: 0}, compiler_params=_cparams(("parallel",)), name=name,
    )(proj, w, b, dxs, dbm, dcm, dproj)


def _ssd_tile(xs, zs, bm, cm, hs, dtc, dtr, bias, alog, dsk, gn):
    row = lax.broadcasted_iota(jnp.int32, (CHUNK, CHUNK), 0)
    col = lax.broadcasted_iota(jnp.int32, (CHUNK, CHUNK), 1)
    causal = row >= col
    tri = jnp.where(causal, 1.0, 0.0)
    cb = _dot_nt(cm, bm)
    ygs, hn = [], []
    for r in range(SSM_HPG):
        a = -jnp.exp(alog[r])
        da_c = jax.nn.softplus(dtc[r] + bias[r]) * a
        da_r = jax.nn.softplus(dtr[r] + bias[r]) * a
        dt_c = jax.nn.softplus(dtc[r] + bias[r])
        cs_c = jnp.sum(tri * da_r, axis=1, keepdims=True)
        cs_r = jnp.sum(jnp.where(row <= col, 1.0, 0.0) * da_c, axis=0, keepdims=True)
        cs_last = jnp.sum(da_c, axis=0, keepdims=True)
        lm = jnp.exp(jnp.where(causal, cs_c - cs_r, -1e30))
        xdt = xs[r] * dt_c
        y = _dot(cb * lm, xdt)
        y = y + _dot_nt(cm, hs[r]) * jnp.exp(cs_c)
        y = y + xs[r] * dsk[r]
        states = _dot_tn(xdt * jnp.exp(cs_last - cs_c), bm)
        hn.append(hs[r] * jnp.exp(cs_last) + states)
        ygs.append(y * (zs[r] * jax.nn.sigmoid(zs[r])))
    ms = sum(jnp.sum(t * t, axis=-1, keepdims=True) for t in ygs) * (1.0 / SSM_GROUP_W)
    rs = lax.rsqrt(ms + EPS)
    return [ygs[r] * rs * gn[r] for r in range(SSM_HPG)], hn


def _ssd_in_specs(cidx):
    gw, n = SSM_GROUP_W, SSM_STATE
    bm_blk = D_INNER // n
    return [
        pl.BlockSpec((CHUNK, gw), lambda g, c: (cidx(c), g)),
        pl.BlockSpec((CHUNK, gw), lambda g, c: (cidx(c), g)),
        pl.BlockSpec((CHUNK, n), lambda g, c: (cidx(c), bm_blk + g)),
        pl.BlockSpec((CHUNK, n), lambda g, c: (cidx(c), bm_blk + SSM_GROUPS + g)),
        pl.BlockSpec((None, CHUNK, SSM_HPG), lambda g, c: (g, cidx(c), 0)),
        pl.BlockSpec((None, SSM_HPG, CHUNK), lambda g, c: (g, 0, cidx(c))),
        pl.BlockSpec((None, 1, SSM_HPG), lambda g, c: (g, 0, 0)),
        pl.BlockSpec((None, 1, SSM_HPG), lambda g, c: (g, 0, 0)),
        pl.BlockSpec((None, 1, SSM_HPG), lambda g, c: (g, 0, 0)),
        pl.BlockSpec((1, gw), lambda g, c: (0, g)),
    ]


def _ssd_args(x_ref, z_ref, bm_ref, cm_ref, hs, dtc_ref, dtr_ref, bias_ref, alog_ref, dsk_ref, gn_ref):
    nh, p = SSM_HPG, SSM_HEAD_DIM
    col = lambda ref: [ref[:, r:r + 1] for r in range(nh)]
    return (_split(x_ref, nh, p), _split(z_ref, nh, p), bm_ref[...], cm_ref[...], hs,
            col(dtc_ref), [dtr_ref[r:r + 1, :] for r in range(nh)], col(bias_ref), col(alog_ref), col(dsk_ref),
            _split(gn_ref, nh, p))


def _ssd_fwd(xbc, proj, dt_c, dt_r, bias, alog, dsk, gn, mixcat, *, name):
    nh = SSM_HPG

    def body(x_ref, z_ref, bm_ref, cm_ref, dtc_ref, dtr_ref, bias_ref, alog_ref, dsk_ref, gn_ref, cat_in,
             cat_ref, hprev_ref, h_scr):
        del cat_in

        @pl.when(pl.program_id(1) == 0)
        def _():
            h_scr[...] = jnp.zeros_like(h_scr)

        hs = [h_scr[r] for r in range(nh)]
        for r in range(nh):
            hprev_ref[r] = hs[r]
        yn, hn = _ssd_tile(*_ssd_args(x_ref, z_ref, bm_ref, cm_ref, hs, dtc_ref, dtr_ref, bias_ref, alog_ref,
                                      dsk_ref, gn_ref))
        for r in range(nh):
            cat_ref[:, r * SSM_HEAD_DIM:(r + 1) * SSM_HEAD_DIM] = yn[r].astype(cat_ref.dtype)
            h_scr[r] = hn[r]

    return pl.pallas_call(
        body, grid=(SSM_GROUPS, N_CHUNKS), in_specs=[*_ssd_in_specs(lambda c: c), pl.BlockSpec(memory_space=pl.ANY)],
        out_specs=[pl.BlockSpec((CHUNK, SSM_GROUP_W), lambda g, c: (c, g)),
                   pl.BlockSpec((None, nh, SSM_HEAD_DIM, SSM_STATE), lambda g, c: (c, g, 0, 0))],
        out_shape=[SDS(mixcat.shape, mixcat.dtype), SDS((N_CHUNKS, SSM_HEADS, SSM_HEAD_DIM, SSM_STATE), F32)],
        scratch_shapes=[pltpu.VMEM((nh, SSM_HEAD_DIM, SSM_STATE), F32)],
        input_output_aliases={10: 0}, compiler_params=_cparams(("parallel", "arbitrary")), name=name,
    )(xbc, proj, xbc, xbc, dt_c, dt_r, bias, alog, dsk, gn, mixcat)


def _ssd_bwd(xbc, proj, dt_c, dt_r, bias, alog, dsk, gn, hprev, dcat, dproj, *, name):
    nh, p, gw, n = SSM_HPG, SSM_HEAD_DIM, SSM_GROUP_W, SSM_STATE
    rev = lambda c: N_CHUNKS - 1 - c

    def body(x_ref, z_ref, bm_ref, cm_ref, dtc_ref, dtr_ref, bias_ref, alog_ref, dsk_ref, gn_ref, hprev_ref, dy_ref,
             dproj_in, dz_ref, dxs_ref, dbm_ref, dcm_ref, ddtc_ref, ddtr_ref, dbias_ref, dalog_ref, ddsk_ref, dgn_ref,
             dh_scr):
        del dproj_in
        first = pl.program_id(1) == 0

        @pl.when(first)
        def _():
            dh_scr[...] = jnp.zeros_like(dh_scr)
            for ref in (dbias_ref, dalog_ref, ddsk_ref, dgn_ref):
                ref[...] = jnp.zeros_like(ref)

        hs = [hprev_ref[r] for r in range(nh)]
        args = _ssd_args(x_ref, z_ref, bm_ref, cm_ref, hs, dtc_ref, dtr_ref, bias_ref, alog_ref, dsk_ref, gn_ref)
        _, vjp = jax.vjp(_ssd_tile, *args)
        dxs, dzs, dbm, dcm, dhs, ddtc, ddtr, dbias, dalog, ddsk, dgn = vjp(
            (_split(dy_ref, nh, p), [dh_scr[r] for r in range(nh)]))
        dbm_ref[...] = dbm
        dcm_ref[...] = dcm
        for r in range(nh):
            dxs_ref[:, r * p:(r + 1) * p] = dxs[r]
            dz_ref[:, r * p:(r + 1) * p] = dzs[r].astype(dz_ref.dtype)
            dh_scr[r] = dhs[r]
            ddtc_ref[:, r:r + 1] = ddtc[r]
            ddtr_ref[r:r + 1, :] = ddtr[r]
            dbias_ref[:, r:r + 1] += dbias[r]
            dalog_ref[:, r:r + 1] += dalog[r]
            ddsk_ref[:, r:r + 1] += ddsk[r]
            dgn_ref[:, r * p:(r + 1) * p] += dgn[r]

    par_spec = pl.BlockSpec((None, 1, nh), lambda g, c: (g, 0, 0))
    return pl.pallas_call(
        body, grid=(SSM_GROUPS, N_CHUNKS),
        in_specs=[*_ssd_in_specs(rev),
                  pl.BlockSpec((None, nh, p, n), lambda g, c: (rev(c), g, 0, 0)),
                  pl.BlockSpec((CHUNK, gw), lambda g, c: (rev(c), g)),
                  pl.BlockSpec(memory_space=pl.ANY)],
        out_specs=[pl.BlockSpec((CHUNK, gw), lambda g, c: (rev(c), g)),
                   pl.BlockSpec((CHUNK, gw), lambda g, c: (rev(c), g)),
                   pl.BlockSpec((CHUNK, n), lambda g, c: (rev(c), g)),
                   pl.BlockSpec((CHUNK, n), lambda g, c: (rev(c), g)),
                   pl.BlockSpec((None, CHUNK, nh), lambda g, c: (g, rev(c), 0)),
                   pl.BlockSpec((None, nh, CHUNK), lambda g, c: (g, 0, rev(c))),
                   par_spec, par_spec, par_spec,
                   pl.BlockSpec((1, gw), lambda g, c: (0, g))],
        out_shape=[SDS(dproj.shape, dproj.dtype), SDS((SEQ, D_INNER), F32), SDS((SEQ, SSM_GROUPS * n), F32),
                   SDS((SEQ, SSM_GROUPS * n), F32), SDS((SSM_GROUPS, SEQ, nh), F32), SDS((SSM_GROUPS, nh, SEQ), F32),
                   SDS((SSM_GROUPS, 1, nh), F32), SDS((SSM_GROUPS, 1, nh), F32), SDS((SSM_GROUPS, 1, nh), F32),
                   SDS((1, D_INNER), F32)],
        scratch_shapes=[pltpu.VMEM((nh, p, n), F32)],
        input_output_aliases={12: 0}, compiler_params=_cparams(("parallel", "arbitrary")), name=name,
    )(xbc, proj, xbc, xbc, dt_c, dt_r, bias, alog, dsk, gn, hprev, dcat, dproj)


def _sum_slots(p, *, name):
    n, r, c = p.shape
    tr = _pick(r, (256, 384, 128, 8))

    def body(p_ref, o_ref):
        acc = p_ref[0].astype(F32)
        for s in range(1, n):
            acc = acc + p_ref[s].astype(F32)
        o_ref[...] = acc

    return pl.pallas_call(body, grid=(r // tr,), in_specs=[pl.BlockSpec((n, tr, c), lambda i: (0, i, 0))],
                          out_specs=pl.BlockSpec((tr, c), lambda i: (i, 0)), out_shape=SDS((r, c), F32),
                          compiler_params=_cparams(("parallel",)), name=name)(p)


def _adamw(w, g, m, v, *, name):
    r, c = w.shape
    tr = r if r <= 256 else _pick(r, (256, 128, 8))
    spec = pl.BlockSpec((tr, c), lambda i: (i, 0))

    def body(w_ref, g_ref, m_ref, v_ref, d_ref, mo_ref, vo_ref):
        g = g_ref[...]
        m_new = ADAM_B1 * m_ref[...] + (1.0 - ADAM_B1) * g
        v_new = ADAM_B2 * v_ref[...] + (1.0 - ADAM_B2) * (g * g)
        m_hat = m_new / (1.0 - ADAM_B1 ** ADAM_STEP)
        v_hat = v_new / (1.0 - ADAM_B2 ** ADAM_STEP)
        d_ref[...] = -ADAM_LR * (m_hat / (jnp.sqrt(v_hat) + ADAM_EPS) + ADAM_WD * w_ref[...])
        mo_ref[...] = m_new
        vo_ref[...] = v_new

    return pl.pallas_call(body, grid=(r // tr,), in_specs=[spec] * 4, out_specs=[spec] * 3,
                          out_shape=[SDS((r, c), F32)] * 3, compiler_params=_cparams(("parallel",)), name=name)(w, g, m, v)


ANY = pl.BlockSpec(memory_space=pl.ANY)


def _place():
    x, y, c = lax.axis_index("x"), lax.axis_index("y"), lax.axis_index("c")
    chips = [(1 - x, y), (x, 1 - y), (1 - x, 1 - y)]
    return x, y, c, chips


def _remote(src, dst, send_sem, recv_sem, to):
    return pltpu.make_async_remote_copy(src_ref=src, dst_ref=dst, send_sem=send_sem, recv_sem=recv_sem,
                                        device_id=to, device_id_type=MESH)


STREAM_ROWS = 128


def _stream_rows(i):
    return pl.ds(pl.multiple_of(i * STREAM_ROWS, STREAM_ROWS), STREAM_ROWS)


def _channel_scratch(width, dtype):
    buf = (2, STREAM_ROWS, width)
    return [pltpu.VMEM(buf, dtype), pltpu.VMEM(buf, dtype), *([pltpu.SemaphoreType.DMA((2,))] * 5),
            pltpu.SemaphoreType.REGULAR((2,))]


CHANNEL_REFS = 8


def _copy_through_vmem(src, dst, ch):
    sbuf, _, ld, _, _, st, _, _ = ch
    steps = src.shape[0] // STREAM_ROWS
    assert steps >= 2 and steps * STREAM_ROWS == src.shape[0]

    def load(i, slot):
        return pltpu.make_async_copy(src.at[_stream_rows(i)], sbuf.at[slot], ld.at[slot])

    def store(i, slot):
        return pltpu.make_async_copy(sbuf.at[slot], dst.at[_stream_rows(i)], st.at[slot])

    load(0, 0).start()

    def step(i, carry):
        slot = lax.rem(i, 2)
        nxt = 1 - slot

        @pl.when(i + 1 < steps)
        def _():
            @pl.when(i >= 1)
            def _():
                store(0, nxt).wait()
            load(i + 1, nxt).start()

        load(i, slot).wait()
        store(i, slot).start()
        return carry

    lax.fori_loop(0, steps, step, 0)
    for slot in range(2):
        store(0, slot).wait()


def _exchange_stream(src, dst, keep, ch, sibling):
    sbuf, rbuf, ld, snd, rcv, st, kp, credit = ch
    steps = src.shape[0] // STREAM_ROWS
    assert steps >= 2 and steps * STREAM_ROWS == src.shape[0]

    def load(i, slot):
        return pltpu.make_async_copy(src.at[_stream_rows(i)], sbuf.at[slot], ld.at[slot])

    def push(slot):
        return _remote(sbuf.at[slot], rbuf.at[slot], snd.at[slot], rcv.at[slot], sibling)

    def store(i, slot):
        return pltpu.make_async_copy(rbuf.at[slot], dst.at[_stream_rows(i)], st.at[slot])

    def save(i, slot):
        return pltpu.make_async_copy(sbuf.at[slot], keep.at[_stream_rows(i)], kp.at[slot])

    for slot in range(2):
        pl.semaphore_signal(credit.at[slot], 1, device_id=sibling, device_id_type=MESH)
    load(0, 0).start()

    def step(i, carry):
        slot = lax.rem(i, 2)
        nxt = 1 - slot

        @pl.when(i + 1 < steps)
        def _():
            @pl.when(i >= 1)
            def _():
                push(nxt).wait_send()
                if keep is not None:
                    save(0, nxt).wait()
            load(i + 1, nxt).start()

        load(i, slot).wait()
        pl.semaphore_wait(credit.at[slot], 1)
        push(slot).start()
        if keep is not None:
            save(i, slot).start()
        push(slot).wait_recv()
        store(i, slot).start()

        @pl.when(i >= 1)
        def _():
            store(0, nxt).wait()

            @pl.when(i + 1 < steps)
            def _():
                pl.semaphore_signal(credit.at[nxt], 1, device_id=sibling, device_id_type=MESH)
        return carry

    lax.fori_loop(0, steps, step, 0)
    store(0, (steps - 1) % 2).wait()
    for slot in range(2):
        push(slot).wait_send()
        if keep is not None:
            save(0, slot).wait()


def _all_gather_shards(shards, small, *, name):
    n = len(shards)

    def body(*refs):
        ins, outs = refs[:n + 1], refs[n + 1:2 * n + 2]
        scr = refs[2 * n + 2:]
        chans = [scr[CHANNEL_REFS * t:CHANNEL_REFS * (t + 1)] for t in range(n)]
        send_sems, recv_sems, small_sems = scr[CHANNEL_REFS * n:]
        x, y, c, _ = _place()
        me = 2 * x + y
        sibling = (x, y, 1 - c)
        near = (lax.rem(x + 1 - c, 2), lax.rem(y + c, 2))
        far = (lax.rem(x + c, 2), lax.rem(y + 1 - c, 2))
        k_near, k_far, k_diag = 2 * near[0] + near[1], 2 * far[0] + far[1], 3 - me
        targets = ((*near, c), (*far, c), (*far, c))
        arrives = (k_near, k_far, k_diag)
        streams_in = (k_far, k_near, k_diag)

        def ici(t, j, src, blk):
            return _remote(src, outs[t].at[blk, c], send_sems.at[3 * t + j], recv_sems.at[3 * t + j], targets[j])

        first = [ici(t, j, ins[t].at[c], me) for t in range(n + 1) for j in range(2)]
        for cp in first:
            cp.start()
        small_local = pltpu.make_async_copy(ins[n], outs[n].at[me], small_sems.at[6])
        small_local.start()
        for t in range(n):
            for h in range(2):
                _copy_through_vmem(ins[t].at[h], outs[t].at[me, h], chans[t])
        passed = []
        for j in range(3):
            for t in range(n + 1):
                landed = outs[t].at[arrives[j], c]
                ici(t, j, landed, arrives[j]).wait_recv()
                if j == 0:
                    fwd = ici(t, 2, landed, k_near)
                    fwd.start()
                    passed.append(fwd)
                if t < n:
                    _exchange_stream(landed, outs[t].at[streams_in[j], 1 - c], None, chans[t], sibling)
                else:
                    fwd = _remote(landed, landed, small_sems.at[j], small_sems.at[3 + j], sibling)
                    fwd.start()
                    passed.append(fwd)
        for j in range(3):
            got = outs[n].at[streams_in[j], 1 - c]
            _remote(got, got, small_sems.at[j], small_sems.at[3 + j], sibling).wait_recv()
        for cp in first + passed:
            cp.wait_send()
        small_local.wait()

    scratch = []
    for s in shards:
        scratch += _channel_scratch(s.shape[2], s.dtype)
    return pl.pallas_call(
        body, in_specs=[ANY] * (n + 1), out_specs=[ANY] * (n + 1),
        out_shape=[SDS((N_CHIPS, *s.shape), s.dtype) for s in (*shards, small)],
        scratch_shapes=[*scratch, pltpu.SemaphoreType.DMA((3 * n + 3,)), pltpu.SemaphoreType.DMA((3 * n + 3,)),
                        pltpu.SemaphoreType.DMA((7,))],
        compiler_params=pltpu.CompilerParams(vmem_limit_bytes=VMEM_LIMIT), name=name)(*shards, small)


def _pair_reduce(stacks, *, name):
    n = len(stacks)
    per = 10

    def body(*refs):
        ins, outs, scr = refs[:n], refs[n:2 * n], refs[2 * n:]
        x, y, c, _ = _place()
        sibling = (x, y, 1 - c)
        for t in range(n):
            sbuf, rbuf, obuf, pbuf, ld_s, ld_o, snd, rcv, st, credit = scr[per * t:per * (t + 1)]
            steps = ins[t].shape[1] // STREAM_ROWS
            src, own, out = ins[t].at[1 - c], ins[t].at[c], outs[t]

            def loads(i, slot, src=src, own=own, sbuf=sbuf, obuf=obuf, ld_s=ld_s, ld_o=ld_o):
                return (pltpu.make_async_copy(src.at[_stream_rows(i)], sbuf.at[slot], ld_s.at[slot]),
                        pltpu.make_async_copy(own.at[_stream_rows(i)], obuf.at[slot], ld_o.at[slot]))

            def push(slot, sbuf=sbuf, rbuf=rbuf, snd=snd, rcv=rcv):
                return _remote(sbuf.at[slot], rbuf.at[slot], snd.at[slot], rcv.at[slot], sibling)

            def store(i, slot, pbuf=pbuf, out=out, st=st):
                return pltpu.make_async_copy(pbuf.at[slot], out.at[_stream_rows(i)], st.at[slot])

            assert steps >= 2
            for slot in range(2):
                pl.semaphore_signal(credit.at[slot], 1, device_id=sibling, device_id_type=MESH)
                for cp in loads(slot, slot):
                    cp.start()
            loads(0, 0)[0].wait()
            pl.semaphore_wait(credit.at[0], 1)
            push(0).start()

            def step(i, carry, loads=loads, push=push, store=store, rbuf=rbuf, obuf=obuf, pbuf=pbuf, credit=credit,
                     steps=steps):
                slot = lax.rem(i, 2)
                nxt = 1 - slot

                @pl.when(i + 1 < steps)
                def _():
                    loads(i + 1, nxt)[0].wait()
                    pl.semaphore_wait(credit.at[nxt], 1)
                    push(nxt).start()

                loads(i, slot)[1].wait()
                push(slot).wait_recv()

                @pl.when(i >= 2)
                def _():
                    store(i, slot).wait()

                pbuf[slot] = (obuf[slot] + rbuf[slot]).astype(pbuf.dtype)
                store(i, slot).start()
                push(slot).wait_send()

                @pl.when(i + 2 < steps)
                def _():
                    for cp in loads(i + 2, slot):
                        cp.start()
                    pl.semaphore_signal(credit.at[slot], 1, device_id=sibling, device_id_type=MESH)
                return carry

            lax.fori_loop(0, steps, step, 0)
            for slot in range(2):
                store(0, slot).wait()

    scratch = []
    for s in stacks:
        buf = (2, STREAM_ROWS, s.shape[2])
        scratch += [pltpu.VMEM(buf, F32), pltpu.VMEM(buf, F32), pltpu.VMEM(buf, F32), pltpu.VMEM(buf, BF16),
                    *([pltpu.SemaphoreType.DMA((2,))] * 5), pltpu.SemaphoreType.REGULAR((2,))]
    return pl.pallas_call(
        body, in_specs=[ANY] * n, out_specs=[ANY] * n, out_shape=[SDS(s.shape[1:], BF16) for s in stacks],
        scratch_shapes=scratch, compiler_params=pltpu.CompilerParams(vmem_limit_bytes=VMEM_LIMIT), name=name)(*stacks)


def _chip_scatter(parts, *, name):
    n = len(parts)

    def body(*refs):
        ins, outs, scr = refs[:n], refs[n:2 * n], refs[2 * n:]
        chans = [scr[CHANNEL_REFS * t:CHANNEL_REFS * (t + 1)] for t in range(n)]
        send_sems, recv_sems = scr[CHANNEL_REFS * n:]
        x, y, c, chips = _place()
        me = 2 * x + y
        sends = []
        for t in range(n):
            for j, (cx, cy) in enumerate(chips):
                cp = _remote(ins[t].at[2 * cx + cy], outs[t].at[me], send_sems.at[3 * t + j], recv_sems.at[3 * t + j],
                             (cx, cy, c))
                cp.start()
                sends.append(cp)
        for t in range(n):
            _copy_through_vmem(ins[t].at[me], outs[t].at[me], chans[t])
        for t in range(n):
            for j, (cx, cy) in enumerate(chips):
                landed = outs[t].at[2 * cx + cy]
                _remote(landed, landed, send_sems.at[3 * t + j], recv_sems.at[3 * t + j], (cx, cy, c)).wait_recv()
        for cp in sends:
            cp.wait_send()

    scratch = []
    for p in parts:
        scratch += _channel_scratch(p.shape[2], p.dtype)
    return pl.pallas_call(
        body, in_specs=[ANY] * n, out_specs=[ANY] * n, out_shape=[SDS(p.shape, p.dtype) for p in parts],
        scratch_shapes=[*scratch, pltpu.SemaphoreType.DMA((3 * n,)), pltpu.SemaphoreType.DMA((3 * n,))],
        compiler_params=pltpu.CompilerParams(vmem_limit_bytes=VMEM_LIMIT), name=name)(*parts)


def _pair_share(finals, *, name):
    n = len(finals)

    def body(*refs):
        ins, outs, scr = refs[:n], refs[n:2 * n], refs[2 * n:]
        x, y, c, _ = _place()
        sibling = (x, y, 1 - c)
        for t in range(n):
            _exchange_stream(ins[t], outs[t].at[1 - c], outs[t].at[c], scr[CHANNEL_REFS * t:CHANNEL_REFS * (t + 1)],
                             sibling)

    scratch = []
    for f in finals:
        scratch += _channel_scratch(f.shape[1], f.dtype)
    return pl.pallas_call(
        body, in_specs=[ANY] * n, out_specs=[ANY] * n, out_shape=[SDS((2, *f.shape), f.dtype) for f in finals],
        scratch_shapes=scratch, compiler_params=pltpu.CompilerParams(vmem_limit_bytes=VMEM_LIMIT), name=name)(*finals)


def _all_reduce_small(v, *, name):
    rows, lanes = v.shape
    n_dev = 8

    def body(v_ref, o_ref, all_ref, send_sems, recv_sems, local_sem):
        x, y, c, chips = _place()
        me, sibling = (x, y, c), (x, y, 1 - c)

        def block(px, py, pc):
            return all_ref.at[4 * px + 2 * py + pc]

        def copy(k, blk, to, src=None):
            return _remote(block(*blk) if src is None else src, block(*blk), send_sems.at[k], recv_sems.at[k], to)

        mine = pltpu.make_async_copy(v_ref, block(*me), local_sem)
        mine.start()
        first = [copy(0, me, sibling, src=v_ref)]
        first += [copy(1 + j, me, (*chip, c), src=v_ref) for j, chip in enumerate(chips)]
        for cp in first:
            cp.start()
        passed = [copy(4 + j, (*chip, c), sibling) for j, chip in enumerate(chips)]
        for j, chip in enumerate(chips):
            copy(1 + j, (*chip, c), me).wait_recv()
            passed[j].start()
        copy(0, sibling, me).wait_recv()
        for j, chip in enumerate(chips):
            copy(4 + j, (*chip, 1 - c), me).wait_recv()
        for cp in first + passed:
            cp.wait_send()
        mine.wait()
        acc = all_ref[0]
        for k in range(1, n_dev):
            acc = acc + all_ref[k]
        o_ref[...] = acc

    vmem = pl.BlockSpec(memory_space=pltpu.VMEM)
    return pl.pallas_call(
        body, in_specs=[vmem], out_specs=vmem, out_shape=SDS((rows, lanes), F32),
        scratch_shapes=[pltpu.VMEM((n_dev, rows, lanes), F32), pltpu.SemaphoreType.DMA((7,)),
                        pltpu.SemaphoreType.DMA((7,)), pltpu.SemaphoreType.DMA],
        compiler_params=pltpu.CompilerParams(vmem_limit_bytes=VMEM_LIMIT), name=name)(v)


def _relu2_epilogue(acc):
    return acc, jnp.square(jnp.maximum(acc, 0.0))


def _res_epilogue(acc, res):
    return (acc + res,)


def _drelu2_epilogue(acc, pre):
    return (acc * (2.0 * jnp.maximum(pre.astype(F32), 0.0)),)


def _ffn_fwd(h, g, w1, w2, tag):
    f = _rms_fwd(h, g, name=f"ffn_norm_{tag}")
    pre, act = _mm_nn(f, w1, name=f"ffn1_{tag}", epilogue=_relu2_epilogue, n_out_dtypes=(BF16, BF16))
    h_out = _mm_nn(act, w2, name=f"ffn2_{tag}", extras=(h,), epilogue=_res_epilogue)
    return h_out, (f, pre, act)


def _ffn_bwd(dh, h, g, w1, w2, saved, layer, stacks):
    f, pre, act = saved
    dpre = _mm_nt(dh, w2, name=f"ffn2_dx_{layer}", out_dtype=BF16, extras=(pre,), epilogue=_drelu2_epilogue)
    dw2 = _mm_tn_stacked(act, dh, name=f"ffn2_dw_{layer}", half=layer, col_slots=False, stack=stacks[1])
    df = _mm_nt(dpre, w1, name=f"ffn1_dx_{layer}")
    dw1 = _mm_tn_stacked(f, dpre, name=f"ffn1_dw_{layer}", half=layer, col_slots=True, stack=stacks[0])
    dh, dg = _rms_bwd(h, g, df, dh, name=f"ffn_norm_bwd_{layer}")
    return dh, dg, (dw1, dw2)


def _kv_fwd(mem, g, w_kv, tag):
    m = _rms_fwd(mem, g, name=f"mem_norm_{tag}")
    return m, _mm_nn(m, w_kv, name=f"kv_{tag}")


def _kv_bwd(mem, g, w_kv, m, dk, dv, layer, stack):
    dkv = jnp.concatenate([dk, dv], axis=1)
    dw = _mm_tn_stacked(m, dkv, name=f"kv_dw_{layer}", half=layer, col_slots=True, stack=stack)
    dm = _mm_nt(dkv, w_kv, name=f"kv_dx_{layer}")
    _, dg = _rms_bwd(mem, g, dm, dm, name=f"mem_norm_bwd_{layer}")
    return dw, dg


def _local_step(x, mem, target, p):
    row = lambda v: v.reshape(1, -1)
    g = {}

    h0 = x
    a0 = _rms_fwd(h0, row(p["norm_mix"][0]), name="mix_norm_0")
    proj_a = _mm_nn(a0, p["a_in"], name="a_in")
    m0, kv0 = _kv_fwd(mem, row(p["mem_norm"][0]), p["w_kv"][0], "0")
    cat0 = _attn_fwd(proj_a, 2 * D_INNER, kv0, name="attn_0")
    bs_col = p["a_bs"].reshape(A_GROUPS, CHUNK, 1)
    cat0 = _gate_fwd(proj_a, p["a_ln_g"], p["a_ln_b"], p["a_ws"], bs_col, cat0, name="gate")
    h1 = _mm_nn(cat0, p["w_out"][0], name="out_0", extras=(h0,), epilogue=_res_epilogue)
    h2, ffn0 = _ffn_fwd(h1, row(p["norm_ffn"][0]), p["w_ffn1"][0], p["w_ffn2"][0], "0")

    a1 = _rms_fwd(h2, row(p["norm_mix"][1]), name="mix_norm_1")
    proj_b = _mm_nn(a1, p["b_in"], name="b_in")
    m1, kv1 = _kv_fwd(mem, row(p["mem_norm"][1]), p["w_kv"][1], "1")
    cat1 = _attn_fwd(proj_b, B_Q_OFF, kv1, name="attn_1")
    xbc = _conv_fwd(proj_b, p["b_conv_w"], p["b_conv_b"], name="conv")
    dt_raw = proj_b[:, B_DT_OFF:B_DT_OFF + SSM_HEADS].reshape(SEQ, SSM_GROUPS, SSM_HPG)
    dt_c = jnp.transpose(dt_raw, (1, 0, 2))
    dt_r = jnp.transpose(dt_raw, (1, 2, 0))
    per_head = lambda v: v.reshape(SSM_GROUPS, 1, SSM_HPG)
    ssd_par = (per_head(p["b_dt_bias"]), per_head(p["b_a_log"]), per_head(p["b_d"]), p["b_gnorm"])
    cat1, hprev = _ssd_fwd(xbc, proj_b, dt_c, dt_r, *ssd_par, cat1, name="ssd")
    h3 = _mm_nn(cat1, p["w_out"][1], name="out_1", extras=(h2,), epilogue=_res_epilogue)
    h4, ffn1 = _ffn_fwd(h3, row(p["norm_ffn"][1]), p["w_ffn1"][1], p["w_ffn2"][1], "1")

    loss, dh, g["final_norm"] = _loss_head(h4, row(p["final_norm"]), target, name="loss_head")

    dh, dnf1, dw_ffn = _ffn_bwd(dh, h3, row(p["norm_ffn"][1]), p["w_ffn1"][1], p["w_ffn2"][1], ffn1, 1, (None, None))
    dcat1 = _mm_nt(dh, p["w_out"][1], name="out_dx_1")
    dwo = _mm_tn_stacked(cat1, dh, name="out_dw_1", half=1, col_slots=False)
    dproj_b, dk1, dv1 = _attn_bwd(proj_b, B_Q_OFF, kv1, dcat1, B_IN_PAD, B_Q_OFF, name="attn_bwd_1")
    (dproj_b, dxs, dbm, dcm, ddt_c, ddt_r, g["b_dt_bias"], g["b_a_log"], g["b_d"], g["b_gnorm"]) = _ssd_bwd(
        xbc, proj_b, dt_c, dt_r, *ssd_par, hprev, dcat1, dproj_b, name="ssd_bwd")
    dproj_b, g["b_conv_w"], g["b_conv_b"] = _conv_bwd(proj_b, p["b_conv_w"], p["b_conv_b"], dxs, dbm, dcm, dproj_b,
                                                      name="conv_bwd")
    ddt = jnp.transpose(ddt_c, (1, 0, 2)) + jnp.transpose(ddt_r, (2, 0, 1))
    ddt = jnp.pad(ddt.reshape(SEQ, SSM_HEADS), ((0, 0), (0, B_IN_PAD - B_DT_OFF - SSM_HEADS))).astype(BF16)
    dproj_b = lax.dynamic_update_slice(dproj_b, ddt, (0, B_DT_OFF))
    dwkv, dmn1 = _kv_bwd(mem, row(p["mem_norm"][1]), p["w_kv"][1], m1, dk1, dv1, 1, None)
    half = D_MODEL // 2
    dwb = [_mm_tn(a1, dproj_b, name=f"b_in_dw_{i}", x_cols=(i * half, half)) for i in range(2)]
    da1 = _mm_nt(dproj_b, p["b_in"], name="b_in_dx")
    dh, dnm1 = _rms_bwd(h2, row(p["norm_mix"][1]), da1, dh, name="mix_norm_bwd_1")

    dh, dnf0, dw_ffn = _ffn_bwd(dh, h1, row(p["norm_ffn"][0]), p["w_ffn1"][0], p["w_ffn2"][0], ffn0, 0, dw_ffn)
    dcat0 = _mm_nt(dh, p["w_out"][0], name="out_dx_0")
    dwo = _mm_tn_stacked(cat0, dh, name="out_dw_0", half=0, col_slots=False, stack=dwo)
    dproj_a, dk0, dv0 = _attn_bwd(proj_a, 2 * D_INNER, kv0, dcat0, A_IN, 2 * D_INNER, name="attn_bwd_0")
    dproj_a, g["a_ln_g"], g["a_ln_b"], g["a_ws"], dbs_col = _gate_bwd(
        proj_a, p["a_ln_g"], p["a_ln_b"], p["a_ws"], bs_col, dcat0, dproj_a, name="gate_bwd")
    g["a_bs"] = dbs_col.reshape(A_GROUPS, CHUNK)
    dwkv, dmn0 = _kv_bwd(mem, row(p["mem_norm"][0]), p["w_kv"][0], m0, dk0, dv0, 0, dwkv)
    dwa = None
    for i in range(2):
        dwa = _mm_tn_stacked(a0, dproj_a, name=f"a_in_dw_{i}", half=i, col_slots=True, stack=dwa,
                             x_cols=(i * half, half))
    da0 = _mm_nt(dproj_a, p["a_in"], name="a_in_dx")
    dx, dnm0 = _rms_bwd(h0, row(p["norm_mix"][0]), da0, dh, name="mix_norm_bwd_0")

    g["norm_mix"] = jnp.concatenate([dnm0, dnm1], axis=0)
    g["norm_ffn"] = jnp.concatenate([dnf0, dnf1], axis=0)
    g["mem_norm"] = jnp.concatenate([dmn0, dmn1], axis=0)
    g["w_kv"], g["w_out"], (g["w_ffn1"], g["w_ffn2"]), g["a_in"] = dwkv, dwo, dw_ffn, dwa
    g["b_in"] = jnp.stack([_b_in_grad_slots(d) for d in dwb])
    return loss, dx, g


def _b_in_full(gathered):
    full = jnp.transpose(gathered, (1, 0, 2)).reshape(D_MODEL, B_IN)
    dt0 = D_INNER + CONV_DIM
    return jnp.concatenate([full[:, :dt0], full[:, dt0 + SSM_HEADS:], full[:, dt0:dt0 + SSM_HEADS],
                            jnp.zeros((D_MODEL, B_IN_PAD - B_IN), full.dtype)], axis=1)


def _b_in_grad_slots(d):
    dt0 = D_INNER + CONV_DIM
    full = jnp.concatenate([d[:, :dt0], d[:, B_DT_OFF:B_DT_OFF + SSM_HEADS], d[:, dt0:B_DT_OFF]], axis=1)
    return jnp.transpose(full.reshape(d.shape[0], N_CHIPS, B_IN // N_CHIPS), (1, 0, 2))


LARGE = ("w_kv", "w_out", "w_ffn1", "w_ffn2", "a_in", "b_in")
SMALL_REPL = ("norm_mix", "norm_ffn", "mem_norm", "a_ln_g", "a_ln_b", "a_ws", "a_bs", "b_dt_bias", "b_a_log", "b_d",
              "final_norm")
SMALL_SHARD = ("b_conv_w", "b_conv_b", "b_gnorm")
WEIGHTS = ("norm_mix", "norm_ffn", "mem_norm", "w_kv", "w_out", "w_ffn1", "w_ffn2", "a_in", "a_ln_g", "a_ln_b", "a_ws",
           "a_bs", "b_in", "b_conv_w", "b_conv_b", "b_dt_bias", "b_a_log", "b_d", "b_gnorm", "final_norm")
CONV_SHARD = CONV_DIM // N_CHIPS
GN_SHARD = D_INNER // N_CHIPS


def _halves(w):
    if w.shape[0] == 2:
        return w
    return w.reshape(2, w.shape[1] // 2, w.shape[2])


def _gather_weights(w):
    big = [_halves(w[k]).astype(BF16) for k in LARGE]
    small = jnp.zeros((2, CONV_K, CONV_SHARD), F32)
    small = small.at[0].set(w["b_conv_w"][0])
    small = small.at[1, 0].set(w["b_conv_b"][0])
    small = small.at[1, 1, :GN_SHARD].set(w["b_gnorm"][0])
    gathered = _all_gather_shards(big, small, name="gather_weights")
    p = {}
    kv, wo, w1, w2, a_in, b_in, sm = gathered
    p["w_kv"] = [kv[:, l] for l in range(2)]
    p["w_out"] = [wo[:, l].reshape(MIX_OUT, D_MODEL) for l in range(2)]
    p["w_ffn1"] = [w1[:, l] for l in range(2)]
    p["w_ffn2"] = [w2[:, l].reshape(D_FF, D_MODEL) for l in range(2)]
    p["a_in"] = a_in.reshape(N_CHIPS, D_MODEL, A_IN // N_CHIPS)
    p["b_in"] = _b_in_full(b_in.reshape(N_CHIPS, D_MODEL, B_IN // N_CHIPS))
    p["b_conv_w"] = jnp.transpose(sm[:, 0], (1, 0, 2)).reshape(CONV_K, CONV_DIM)
    p["b_conv_b"] = sm[:, 1, 0].reshape(1, CONV_DIM)
    p["b_gnorm"] = sm[:, 1, 1, :GN_SHARD].reshape(1, D_INNER)
    return p


def _reduce_large(g):
    stacks = [g[k].reshape(2, -1, g[k].shape[-1]) for k in LARGE]
    parts = _pair_reduce(stacks, name="grads_pair_reduce")
    parts = [t.reshape(N_CHIPS, -1, t.shape[-1]) for t in parts]
    landed = _chip_scatter(parts, name="grads_chip_scatter")
    finals = [_sum_slots(t, name=f"grads_chip_sum_{k}") for k, t in zip(LARGE, landed)]
    shared = _pair_share(finals, name="grads_pair_share")
    return dict(zip(LARGE, shared))


def _small_layout(shapes):
    offs, o = {}, 0
    for k in (*SMALL_REPL, *SMALL_SHARD):
        size = math.prod(shapes[k])
        offs[k] = (o, size)
        o += size
    rows = -(-o // (8 * 128)) * 8
    return offs, rows


def _reduce_small(g, full_shapes):
    offs, rows = _small_layout(full_shapes)
    flat = jnp.concatenate([g[k].reshape(-1) for k in (*SMALL_REPL, *SMALL_SHARD)])
    flat = jnp.pad(flat, (0, rows * 128 - flat.shape[0])).reshape(rows, 128)
    total = _all_reduce_small(flat, name="grads_small_all_reduce").reshape(-1)
    return {k: total[o:o + n].reshape(full_shapes[k]) for k, (o, n) in offs.items()}


def kernel(x, mem, norm_mix, norm_ffn, mem_norm, w_kv, w_out, w_ffn1, w_ffn2, a_in, a_ln_g, a_ln_b, a_ws, a_bs, b_in, b_conv_w, b_conv_b, b_dt_bias, b_a_log, b_d, b_gnorm, final_norm, loss_target, m_norm_mix, m_norm_ffn, m_mem_norm, m_w_kv, m_w_out, m_w_ffn1, m_w_ffn2, m_a_in, m_a_ln_g, m_a_ln_b, m_a_ws, m_a_bs, m_b_in, m_b_conv_w, m_b_conv_b, m_b_dt_bias, m_b_a_log, m_b_d, m_b_gnorm, m_final_norm, v_norm_mix, v_norm_ffn, v_mem_norm, v_w_kv, v_w_out, v_w_ffn1, v_w_ffn2, v_a_in, v_a_ln_g, v_a_ln_b, v_a_ws, v_a_bs, v_b_in, v_b_conv_w, v_b_conv_b, v_b_dt_bias, v_b_a_log, v_b_d, v_b_gnorm, v_final_norm):
    w = dict(norm_mix=norm_mix, norm_ffn=norm_ffn, mem_norm=mem_norm, w_kv=w_kv, w_out=w_out, w_ffn1=w_ffn1,
             w_ffn2=w_ffn2, a_in=a_in, a_ln_g=a_ln_g, a_ln_b=a_ln_b, a_ws=a_ws, a_bs=a_bs, b_in=b_in, b_conv_w=b_conv_w,
             b_conv_b=b_conv_b, b_dt_bias=b_dt_bias, b_a_log=b_a_log, b_d=b_d, b_gnorm=b_gnorm, final_norm=final_norm)
    mom = dict(norm_mix=m_norm_mix, norm_ffn=m_norm_ffn, mem_norm=m_mem_norm, w_kv=m_w_kv, w_out=m_w_out,
               w_ffn1=m_w_ffn1, w_ffn2=m_w_ffn2, a_in=m_a_in, a_ln_g=m_a_ln_g, a_ln_b=m_a_ln_b, a_ws=m_a_ws,
               a_bs=m_a_bs, b_in=m_b_in, b_conv_w=m_b_conv_w, b_conv_b=m_b_conv_b, b_dt_bias=m_b_dt_bias,
               b_a_log=m_b_a_log, b_d=m_b_d, b_gnorm=m_b_gnorm, final_norm=m_final_norm)
    var = dict(norm_mix=v_norm_mix, norm_ffn=v_norm_ffn, mem_norm=v_mem_norm, w_kv=v_w_kv, w_out=v_w_out,
               w_ffn1=v_w_ffn1, w_ffn2=v_w_ffn2, a_in=v_a_in, a_ln_g=v_a_ln_g, a_ln_b=v_a_ln_b, a_ws=v_a_ws,
               a_bs=v_a_bs, b_in=v_b_in, b_conv_w=v_b_conv_w, b_conv_b=v_b_conv_b, b_dt_bias=v_b_dt_bias,
               b_a_log=v_b_a_log, b_d=v_b_d, b_gnorm=v_b_gnorm, final_norm=v_final_norm)

    p = _gather_weights(w)
    p.update(norm_mix=norm_mix, norm_ffn=norm_ffn, mem_norm=mem_norm, a_ln_g=a_ln_g, a_ln_b=a_ln_b, a_ws=a_ws[0],
             a_bs=a_bs[0], b_dt_bias=b_dt_bias, b_a_log=b_a_log, b_d=b_d, final_norm=final_norm)
    loss_part, dx, g = _local_step(x[0], mem[0], loss_target[0], p)
    loss = lax.psum(loss_part[0, 0], ("x", "y", "c"))

    full_shapes = {k: w[k].shape for k in SMALL_REPL}
    full_shapes.update(b_conv_w=(1, CONV_K, CONV_DIM), b_conv_b=(1, CONV_DIM), b_gnorm=(1, D_INNER))
    gs = _reduce_small(g, full_shapes)
    chip = 2 * lax.axis_index("x") + lax.axis_index("y")
    gs["b_conv_w"] = lax.dynamic_slice_in_dim(gs["b_conv_w"], chip * CONV_SHARD, CONV_SHARD, axis=2)
    gs["b_conv_b"] = lax.dynamic_slice_in_dim(gs["b_conv_b"], chip * CONV_SHARD, CONV_SHARD, axis=1)
    gs["b_gnorm"] = lax.dynamic_slice_in_dim(gs["b_gnorm"], chip * GN_SHARD, GN_SHARD, axis=1)
    gl = _reduce_large(g)
    grads = {k: (gl[k].reshape(w[k].shape) if k in gl else gs[k]) for k in WEIGHTS}

    delta, new_m, new_v = {}, {}, {}
    for k in WEIGHTS:
        shape = w[k].shape
        flat = (lambda a: a.reshape(-1, shape[-1])) if len(shape) > 1 else (lambda a: a.reshape(1, -1))
        d, m_new, v_new = _adamw(flat(w[k]), flat(grads[k]), flat(mom[k]), flat(var[k]), name=f"adamw_{k}")
        delta[k], new_m[k], new_v[k] = d.reshape(shape), m_new.reshape(shape), v_new.reshape(shape)

    return (loss, dx.reshape(x.shape), *[grads[k] for k in WEIGHTS], *[delta[k] for k in WEIGHTS],
            *[new_m[k] for k in WEIGHTS], *[new_v[k] for k in WEIGHTS])
```

```python
import math

import jax
import jax.numpy as jnp
from jax import lax
from jax.experimental import pallas as pl
from jax.experimental.pallas import tpu as pltpu

F32 = jnp.float32
BF16 = jnp.bfloat16
SDS = jax.ShapeDtypeStruct

D_MODEL = 1024
SEQ = 2048
CHUNK = 128
N_MEM = 256
D_INNER = 2048
A_GROUPS = 8
A_GROUP_W = D_INNER // A_GROUPS
SSM_HEADS = 32
SSM_HEAD_DIM = 64
SSM_GROUPS = 4
SSM_HPG = 8
SSM_STATE = 128
SSM_GROUP_W = SSM_HPG * SSM_HEAD_DIM
CONV_K = 4
CONV_DIM = 3072
X_HEADS = 4
X_HEAD_DIM = 256
X_WIDTH = 1024
MIX_OUT = 3072
D_FF = 4096
A_IN = 5120
B_IN = 6176
B_IN_PAD = 6272
B_Q_OFF = 5120
B_DT_OFF = 6144
N_CHUNKS = SEQ // CHUNK
EPS = 1e-6
N_CHIPS = 4

ADAM_LR = 0.001
ADAM_B1 = 0.9
ADAM_B2 = 0.999
ADAM_EPS = 1e-08
ADAM_WD = 0.01
ADAM_STEP = 10

VMEM_LIMIT = 48 * 1024 * 1024
MESH = pl.DeviceIdType.MESH


def _cparams(sem):
    return pltpu.CompilerParams(dimension_semantics=sem, vmem_limit_bytes=VMEM_LIMIT)


def _dot(a, b, dims=(((1,), (0,)), ((), ()))):
    return lax.dot_general(a.astype(BF16), b.astype(BF16), dims, preferred_element_type=F32)


def _dot_nt(a, b):
    return _dot(a, b, (((1,), (1,)), ((), ())))


def _dot_tn(a, b):
    return _dot(a, b, (((0,), (0,)), ((), ())))


def _pick(n, cands):
    for c in cands:
        if n % c == 0:
            return c
    raise ValueError(f"no tile for {n}")


def _mm_call(a, b, *, dims, grid, a_spec, b_spec, acc_shape, out_shapes, out_specs, name,
             extras=(), extra_specs=(), epilogue=None):
    n_k = grid[2]
    n_extra = len(extras)
    n_out = len(out_shapes)

    def body(*refs):
        a_ref, b_ref = refs[0], refs[1]
        extra_refs = refs[2:2 + n_extra]
        out_refs = refs[2 + n_extra:2 + n_extra + n_out]
        acc = refs[-1]
        k = pl.program_id(2)

        @pl.when(k == 0)
        def _():
            acc[...] = jnp.zeros_like(acc)

        acc[...] += _dot(a_ref[...], b_ref[...], dims)

        @pl.when(k == n_k - 1)
        def _():
            vals = (acc[...],) if epilogue is None else epilogue(acc[...], *[e[...] for e in extra_refs])
            for o_ref, v in zip(out_refs, vals):
                o_ref[...] = v.astype(o_ref.dtype)

    return pl.pallas_call(
        body, grid=grid, in_specs=[a_spec, b_spec, *extra_specs], out_specs=list(out_specs),
        out_shape=list(out_shapes), scratch_shapes=[pltpu.VMEM(acc_shape, F32)],
        compiler_params=_cparams(("parallel", "parallel", "arbitrary")), name=name,
    )(a, b, *extras)


def _w_dims(w):
    if w.ndim == 2:
        return w.shape[0], w.shape[1], 1, w.shape[1]
    return w.shape[1], w.shape[0] * w.shape[2], w.shape[0], w.shape[2]


def _mm_nn(a, w, *, name, out_dtype=F32, a_cols=None, extras=(), epilogue=None, n_out_dtypes=None):
    m = a.shape[0]
    k_dim, n_dim, _, n_slot = _w_dims(w)
    a_off, a_w = (0, a.shape[1]) if a_cols is None else a_cols
    assert a_w == k_dim
    tm = _pick(m, (2048, 1024, 512, 256))
    tn = _pick(n_slot, (512, 896, 640, 256, 128))
    tk = _pick(k_dim, (1024, 768, 512, 384, 256, 128))
    assert a_off % tk == 0
    nb = n_slot // tn
    a_spec = pl.BlockSpec((tm, tk), lambda i, j, k: (i, a_off // tk + k))
    if w.ndim == 2:
        b_spec = pl.BlockSpec((tk, tn), lambda i, j, k: (k, j))
    else:
        b_spec = pl.BlockSpec((None, tk, tn), lambda i, j, k: (j // nb, k, j % nb))
    o_spec = pl.BlockSpec((tm, tn), lambda i, j, k: (i, j))
    dts = n_out_dtypes or (out_dtype,)
    outs = _mm_call(a, w, dims=(((1,), (0,)), ((), ())), grid=(m // tm, n_dim // tn, k_dim // tk),
                    a_spec=a_spec, b_spec=b_spec, acc_shape=(tm, tn),
                    out_shapes=[SDS((m, n_dim), dt) for dt in dts], out_specs=[o_spec] * len(dts), name=name,
                    extras=extras, extra_specs=[o_spec] * len(extras), epilogue=epilogue)
    return outs if n_out_dtypes else outs[0]


def _mm_nt(a, w, *, name, out_dtype=F32, extras=(), epilogue=None):
    m = a.shape[0]
    k_dim, n_dim, _, n_slot = _w_dims(w)
    assert a.shape[1] == n_dim
    tm = _pick(m, (2048, 1024, 512, 256))
    to = _pick(k_dim, (512, 384, 256, 128))
    tc = _pick(n_slot, (1024, 896, 640, 512, 256, 128))
    nb = n_slot // tc
    a_spec = pl.BlockSpec((tm, tc), lambda i, j, k: (i, k))
    if w.ndim == 2:
        b_spec = pl.BlockSpec((to, tc), lambda i, j, k: (j, k))
    else:
        b_spec = pl.BlockSpec((None, to, tc), lambda i, j, k: (k // nb, j, k % nb))
    o_spec = pl.BlockSpec((tm, to), lambda i, j, k: (i, j))
    return _mm_call(a, w, dims=(((1,), (1,)), ((), ())), grid=(m // tm, k_dim // to, n_dim // tc),
                    a_spec=a_spec, b_spec=b_spec, acc_shape=(tm, to),
                    out_shapes=[SDS((m, k_dim), out_dtype)], out_specs=[o_spec], name=name,
                    extras=extras, extra_specs=[o_spec] * len(extras), epilogue=epilogue)[0]


def _mm_tn(x, dy, *, name, x_cols=None):
    s = x.shape[0]
    x_off, k_dim = (0, x.shape[1]) if x_cols is None else x_cols
    n_dim = dy.shape[1]
    tm = _pick(k_dim, (1024, 768, 512, 384, 256, 128))
    tn = _pick(n_dim, (512, 896, 640, 256, 128))
    tk = _pick(s, (2048, 1024, 512, 256))
    assert x_off % tm == 0
    a_spec = pl.BlockSpec((tk, tm), lambda i, j, k: (k, x_off // tm + i))
    b_spec = pl.BlockSpec((tk, tn), lambda i, j, k: (k, j))
    o_spec = pl.BlockSpec((tm, tn), lambda i, j, k: (i, j))
    return _mm_call(x, dy, dims=(((0,), (0,)), ((), ())), grid=(k_dim // tm, n_dim // tn, s // tk),
                    a_spec=a_spec, b_spec=b_spec, acc_shape=(tm, tn),
                    out_shapes=[SDS((k_dim, n_dim), F32)], out_specs=[o_spec], name=name)[0]


def _mm_tn_stacked(x, dy, *, name, half, col_slots, stack=None, x_cols=None):
    s = x.shape[0]
    x_off, k_dim = (0, x.shape[1]) if x_cols is None else x_cols
    n_dim = dy.shape[1]
    r, c = (k_dim, n_dim // N_CHIPS) if col_slots else (k_dim // N_CHIPS, n_dim)
    tm = _pick(r, (1024, 768, 512, 384, 256, 128))
    tn = _pick(c, (512, 896, 640, 256, 128))
    tk = _pick(s, (2048, 1024, 512, 256))
    assert x_off % tm == 0
    a_spec = pl.BlockSpec((tk, tm), lambda i, j, k: (k, x_off // tm + i))
    b_spec = pl.BlockSpec((tk, tn), lambda i, j, k: (k, j))
    if col_slots:
        nb = c // tn
        o_spec = pl.BlockSpec((None, None, tm, tn), lambda i, j, k: (half, j // nb, i, j % nb))
    else:
        nb = r // tm
        o_spec = pl.BlockSpec((None, None, tm, tn), lambda i, j, k: (half, i // nb, i % nb, j))
    n_k = s // tk

    def body(a_ref, b_ref, *rest):
        o_ref, acc = rest[-2], rest[-1]
        k = pl.program_id(2)

        @pl.when(k == 0)
        def _():
            acc[...] = jnp.zeros_like(acc)

        acc[...] += _dot_tn(a_ref[...], b_ref[...])

        @pl.when(k == n_k - 1)
        def _():
            o_ref[...] = acc[...]

    keep = [] if stack is None else [stack]
    return pl.pallas_call(
        body, grid=(k_dim // tm, n_dim // tn, n_k), in_specs=[a_spec, b_spec, *([ANY] * len(keep))],
        out_specs=o_spec, out_shape=SDS((2, N_CHIPS, r, c), F32), scratch_shapes=[pltpu.VMEM((tm, tn), F32)],
        input_output_aliases={2: 0} if keep else {},
        compiler_params=_cparams(("parallel", "parallel", "arbitrary")), name=name,
    )(x, dy, *keep)


def _rms(x, g):
    return x * lax.rsqrt(jnp.mean(x * x, axis=-1, keepdims=True) + EPS) * g


def _rms_fwd(h, g, *, name):
    rows, d = h.shape
    tr = _pick(rows, (512, 256))

    def body(h_ref, g_ref, o_ref):
        o_ref[...] = _rms(h_ref[...], g_ref[...]).astype(o_ref.dtype)

    return pl.pallas_call(
        body, grid=(rows // tr,),
        in_specs=[pl.BlockSpec((tr, d), lambda i: (i, 0)), pl.BlockSpec((1, d), lambda i: (0, 0))],
        out_specs=pl.BlockSpec((tr, d), lambda i: (i, 0)), out_shape=SDS((rows, d), BF16),
        compiler_params=_cparams(("parallel",)), name=name)(h, g)


def _rms_bwd(h, g, da, dres, *, name):
    rows, d = h.shape
    tr = _pick(rows, (512, 256))

    def body(h_ref, g_ref, da_ref, dres_ref, dh_ref, dg_ref):
        _, vjp = jax.vjp(_rms, h_ref[...], g_ref[...])
        dh, dg = vjp(da_ref[...].astype(F32))
        dh_ref[...] = dres_ref[...] + dh

        @pl.when(pl.program_id(0) == 0)
        def _():
            dg_ref[...] = jnp.zeros_like(dg_ref)

        dg_ref[...] += dg

    row_spec = pl.BlockSpec((tr, d), lambda i: (i, 0))
    vec_spec = pl.BlockSpec((1, d), lambda i: (0, 0))
    return pl.pallas_call(
        body, grid=(rows // tr,), in_specs=[row_spec, vec_spec, row_spec, row_spec],
        out_specs=[row_spec, vec_spec], out_shape=[SDS((rows, d), F32), SDS((1, d), F32)],
        compiler_params=_cparams(("arbitrary",)), name=name)(h, g, da, dres)


def _loss_head(h, g, target, *, name):
    rows, d = h.shape
    tr = _pick(rows, (512, 256))

    def body(h_ref, g_ref, t_ref, loss_ref, dh_ref, dg_ref):
        y, vjp = jax.vjp(_rms, h_ref[...], g_ref[...])
        err = y - t_ref[...]
        dh, dg = vjp(err * (1.0 / d))
        dh_ref[...] = dh

        @pl.when(pl.program_id(0) == 0)
        def _():
            dg_ref[...] = jnp.zeros_like(dg_ref)
            loss_ref[...] = jnp.zeros_like(loss_ref)

        dg_ref[...] += dg
        part = jnp.sum(jnp.sum(err * err, axis=-1, keepdims=True), axis=0, keepdims=True) * (0.5 / d)
        loss_ref[...] += jnp.broadcast_to(part, loss_ref.shape)

    row_spec = pl.BlockSpec((tr, d), lambda i: (i, 0))
    vec_spec = pl.BlockSpec((1, d), lambda i: (0, 0))
    loss_spec = pl.BlockSpec((8, 128), lambda i: (0, 0))
    return pl.pallas_call(
        body, grid=(rows // tr,), in_specs=[row_spec, vec_spec, row_spec],
        out_specs=[loss_spec, row_spec, vec_spec],
        out_shape=[SDS((8, 128), F32), SDS((rows, d), F32), SDS((1, d), F32)],
        compiler_params=_cparams(("arbitrary",)), name=name)(h, g, target)


def _gelu(x):
    return 0.5 * x * (1.0 + lax.erf(x * (1.0 / math.sqrt(2.0))))


def _gate_tile(pu, pv, ln_g, ln_b, ws, bs_t):
    u = [_gelu(p) for p in pu]
    v = [_gelu(p) for p in pv]
    mu = sum(jnp.sum(t, axis=-1, keepdims=True) for t in v) * (1.0 / D_INNER)
    vc = [t - mu for t in v]
    var = sum(jnp.sum(t * t, axis=-1, keepdims=True) for t in vc) * (1.0 / D_INNER)
    rstd = lax.rsqrt(var + EPS)
    row = lax.broadcasted_iota(jnp.int32, (CHUNK, CHUNK), 0)
    col = lax.broadcasted_iota(jnp.int32, (CHUNK, CHUNK), 1)
    out = []
    for gi in range(A_GROUPS):
        vn = vc[gi] * rstd * ln_g[gi] + ln_b[gi]
        w = jnp.where(row >= col, ws[gi], 0.0)
        sv = _dot(w, vn) + bs_t[gi]
        out.append(u[gi] * sv)
    return out


def _split(ref, n, width):
    return [ref[:, i * width:(i + 1) * width] for i in range(n)]


def _gate_in_specs():
    return [
        pl.BlockSpec((CHUNK, D_INNER), lambda c: (c, 0)),
        pl.BlockSpec((CHUNK, D_INNER), lambda c: (c, 1)),
        pl.BlockSpec((1, D_INNER), lambda c: (0, 0)),
        pl.BlockSpec((1, D_INNER), lambda c: (0, 0)),
        pl.BlockSpec((A_GROUPS, CHUNK, CHUNK), lambda c: (0, 0, 0)),
        pl.BlockSpec((A_GROUPS, CHUNK, 1), lambda c: (0, 0, 0)),
    ]


def _gate_args(u_ref, v_ref, g_ref, b_ref, ws_ref, bs_ref):
    ng, gw = A_GROUPS, A_GROUP_W
    return (_split(u_ref, ng, gw), _split(v_ref, ng, gw), _split(g_ref, ng, gw), _split(b_ref, ng, gw),
            [ws_ref[i] for i in range(ng)], [bs_ref[i] for i in range(ng)])


def _gate_fwd(proj, ln_g, ln_b, ws, bs_col, mixcat, *, name):
    def body(u_ref, v_ref, g_ref, b_ref, ws_ref, bs_ref, cat_in, cat_ref):
        del cat_in
        out = _gate_tile(*_gate_args(u_ref, v_ref, g_ref, b_ref, ws_ref, bs_ref))
        for gi, o in enumerate(out):
            cat_ref[:, gi * A_GROUP_W:(gi + 1) * A_GROUP_W] = o.astype(cat_ref.dtype)

    return pl.pallas_call(
        body, grid=(N_CHUNKS,), in_specs=[*_gate_in_specs(), pl.BlockSpec(memory_space=pl.ANY)],
        out_specs=pl.BlockSpec((CHUNK, D_INNER), lambda c: (c, 0)), out_shape=SDS(mixcat.shape, mixcat.dtype),
        input_output_aliases={6: 0}, compiler_params=_cparams(("parallel",)), name=name,
    )(proj, proj, ln_g, ln_b, ws, bs_col, mixcat)


def _gate_bwd(proj, ln_g, ln_b, ws, bs_col, dcat, dproj, *, name):
    ng, gw = A_GROUPS, A_GROUP_W

    def body(u_ref, v_ref, g_ref, b_ref, ws_ref, bs_ref, d_ref, dproj_in, dproj_ref, dg_ref, db_ref, dws_ref, dbs_ref):
        del dproj_in
        args = _gate_args(u_ref, v_ref, g_ref, b_ref, ws_ref, bs_ref)
        _, vjp = jax.vjp(_gate_tile, *args)
        dpu, dpv, dg, db, dws, dbs = vjp(_split(d_ref, ng, gw))
        for gi in range(ng):
            dproj_ref[:, gi * gw:(gi + 1) * gw] = dpu[gi].astype(dproj_ref.dtype)
            dproj_ref[:, D_INNER + gi * gw:D_INNER + (gi + 1) * gw] = dpv[gi].astype(dproj_ref.dtype)

        @pl.when(pl.program_id(0) == 0)
        def _():
            for r in (dg_ref, db_ref, dws_ref, dbs_ref):
                r[...] = jnp.zeros_like(r)

        for gi in range(ng):
            dg_ref[:, gi * gw:(gi + 1) * gw] += dg[gi]
            db_ref[:, gi * gw:(gi + 1) * gw] += db[gi]
            dws_ref[gi] += dws[gi]
            dbs_ref[gi] += dbs[gi]

    in_specs = _gate_in_specs()
    return pl.pallas_call(
        body, grid=(N_CHUNKS,),
        in_specs=[*in_specs, pl.BlockSpec((CHUNK, D_INNER), lambda c: (c, 0)), pl.BlockSpec(memory_space=pl.ANY)],
        out_specs=[pl.BlockSpec((CHUNK, 2 * D_INNER), lambda c: (c, 0)), *in_specs[2:]],
        out_shape=[SDS(dproj.shape, dproj.dtype), SDS((1, D_INNER), F32), SDS((1, D_INNER), F32),
                   SDS((ng, CHUNK, CHUNK), F32), SDS((ng, CHUNK, 1), F32)],
        input_output_aliases={7: 0}, compiler_params=_cparams(("arbitrary",)), name=name,
    )(proj, proj, ln_g, ln_b, ws, bs_col, dcat, dproj)


ATT_TQ = 512


def _attn_tile(q, k, v):
    s = _dot_nt(q, k) * (1.0 / math.sqrt(X_HEAD_DIM))
    s = s - jnp.max(s, axis=-1, keepdims=True)
    e = jnp.exp(s)
    p = e / jnp.sum(e, axis=-1, keepdims=True)
    return _dot(p, v)


def _attn_in_specs(q_blk, order):
    hd = X_HEAD_DIM
    return [
        pl.BlockSpec((ATT_TQ, hd), lambda a, b: (order(a, b)[0], q_blk + order(a, b)[1])),
        pl.BlockSpec((N_MEM, hd), lambda a, b: (0, order(a, b)[1])),
        pl.BlockSpec((N_MEM, hd), lambda a, b: (0, X_HEADS + order(a, b)[1])),
    ]


def _attn_fwd(proj, q_off, kv, *, name):
    order = lambda i, h: (i, h)
    cat_blk = D_INNER // X_HEAD_DIM

    def body(q_ref, k_ref, v_ref, o_ref):
        o_ref[...] = _attn_tile(q_ref[...], k_ref[...], v_ref[...]).astype(o_ref.dtype)

    return pl.pallas_call(
        body, grid=(SEQ // ATT_TQ, X_HEADS), in_specs=_attn_in_specs(q_off // X_HEAD_DIM, order),
        out_specs=pl.BlockSpec((ATT_TQ, X_HEAD_DIM), lambda i, h: (i, cat_blk + h)),
        out_shape=SDS((SEQ, MIX_OUT), BF16), compiler_params=_cparams(("parallel", "parallel")), name=name,
    )(proj, kv, kv)


def _attn_bwd(proj, q_off, kv, dcat, dproj_width, dq_off, *, name):
    order = lambda h, i: (i, h)
    cat_blk = D_INNER // X_HEAD_DIM
    dq_blk = dq_off // X_HEAD_DIM

    def body(q_ref, k_ref, v_ref, do_ref, dq_ref, dk_ref, dv_ref):
        _, vjp = jax.vjp(_attn_tile, q_ref[...], k_ref[...], v_ref[...])
        dq, dk, dv = vjp(do_ref[...])
        dq_ref[...] = dq.astype(dq_ref.dtype)

        @pl.when(pl.program_id(1) == 0)
        def _():
            dk_ref[...] = jnp.zeros_like(dk_ref)
            dv_ref[...] = jnp.zeros_like(dv_ref)

        dk_ref[...] += dk
        dv_ref[...] += dv

    kv_spec = pl.BlockSpec((N_MEM, X_HEAD_DIM), lambda h, i: (0, h))
    return pl.pallas_call(
        body, grid=(X_HEADS, SEQ // ATT_TQ),
        in_specs=[*_attn_in_specs(q_off // X_HEAD_DIM, order),
                  pl.BlockSpec((ATT_TQ, X_HEAD_DIM), lambda h, i: (i, cat_blk + h))],
        out_specs=[pl.BlockSpec((ATT_TQ, X_HEAD_DIM), lambda h, i: (i, dq_blk + h)), kv_spec, kv_spec],
        out_shape=[SDS((SEQ, dproj_width), BF16), SDS((N_MEM, X_WIDTH), F32), SDS((N_MEM, X_WIDTH), F32)],
        compiler_params=_cparams(("parallel", "arbitrary")), name=name,
    )(proj, kv, kv, dcat)


CONV_TC = 512


def _shift_down(x, s):
    if s == 0:
        return x
    row = lax.broadcasted_iota(jnp.int32, x.shape, 0)
    return jnp.where(row >= s, pltpu.roll(x, s, 0), 0.0)


def _shift_up(x, s):
    if s == 0:
        return x
    n = x.shape[0]
    row = lax.broadcasted_iota(jnp.int32, x.shape, 0)
    return jnp.where(row < n - s, pltpu.roll(x, n - s, 0), 0.0)


def _conv_pre(x, w_ref, b_ref):
    pre = b_ref[...] + jnp.zeros_like(x)
    for k in range(CONV_K):
        pre = pre + w_ref[k:k + 1, :] * _shift_down(x, CONV_K - 1 - k)
    return pre


def _conv_fwd(proj, w, b, *, name):
    blk0 = D_INNER // CONV_TC

    def body(x_ref, w_ref, b_ref, o_ref):
        pre = _conv_pre(x_ref[...], w_ref, b_ref)
        o_ref[...] = pre * jax.nn.sigmoid(pre)

    return pl.pallas_call(
        body, grid=(CONV_DIM // CONV_TC,),
        in_specs=[pl.BlockSpec((SEQ, CONV_TC), lambda j: (0, blk0 + j)), pl.BlockSpec((CONV_K, CONV_TC), lambda j: (0, j)),
                  pl.BlockSpec((1, CONV_TC), lambda j: (0, j))],
        out_specs=pl.BlockSpec((SEQ, CONV_TC), lambda j: (0, j)), out_shape=SDS((SEQ, CONV_DIM), F32),
        compiler_params=_cparams(("parallel",)), name=name)(proj, w, b)


def _conv_bwd(proj, w, b, dxs, dbm, dcm, dproj, *, name):
    tc = CONV_TC // 2
    blk0 = D_INNER // tc
    n_x = D_INNER // tc
    n_b = SSM_GROUPS * SSM_STATE // tc

    def body(x_ref, w_ref, b_ref, dxs_ref, dbm_ref, dcm_ref, dproj_in, dproj_ref, dw_ref, db_ref):
        del dproj_in
        j = pl.program_id(0)
        x = x_ref[...]
        pre = _conv_pre(x, w_ref, b_ref)
        sg = jax.nn.sigmoid(pre)
        dact = jnp.where(j < n_x, dxs_ref[...], jnp.where(j < n_x + n_b, dbm_ref[...], dcm_ref[...]))
        dpre = dact * (sg * (1.0 + pre * (1.0 - sg)))
        dx = jnp.zeros_like(x)
        for k in range(CONV_K):
            s = CONV_K - 1 - k
            dx = dx + w_ref[k:k + 1, :] * _shift_up(dpre, s)
            dw_ref[k:k + 1, :] = jnp.sum(dpre * _shift_down(x, s), axis=0, keepdims=True)
        dproj_ref[...] = dx.astype(dproj_ref.dtype)
        db_ref[...] = jnp.sum(dpre, axis=0, keepdims=True)

    clip = lambda v, hi: jnp.minimum(jnp.maximum(v, 0), hi)
    return pl.pallas_call(
        body, grid=(CONV_DIM // tc,),
        in_specs=[pl.BlockSpec((SEQ, tc), lambda j: (0, blk0 + j)), pl.BlockSpec((CONV_K, tc), lambda j: (0, j)),
                  pl.BlockSpec((1, tc), lambda j: (0, j)),
                  pl.BlockSpec((SEQ, tc), lambda j: (0, clip(j, n_x - 1))),
                  pl.BlockSpec((SEQ, tc), lambda j: (0, clip(j - n_x, n_b - 1))),
                  pl.BlockSpec((SEQ, tc), lambda j: (0, clip(j - n_x - n_b, n_b - 1))),
                  pl.BlockSpec(memory_space=pl.ANY)],
        out_specs=[pl.BlockSpec((SEQ, tc), lambda j: (0, blk0 + j)), pl.BlockSpec((CONV_K, tc), lambda j: (0, j)),
                   pl.BlockSpec((1, tc), lambda j: (0, j))],
        out_shape=[SDS(dproj.shape, dproj.dtype), SDS((CONV_K, CONV_DIM), F32), SDS((1, CONV_DIM), F32)],
        input_output_aliases={6: 0}, compiler_params=_cparams(("parallel",)), name=name,
    )(proj, w, b, dxs, dbm, dcm, dproj)


SSM_PAIRS = SSM_HPG // 2


def _ssd_tile(xp, zp, bm, cm, hp, dtc, dtr, bias, alog, dsk, gnp):
    row = lax.broadcasted_iota(jnp.int32, (CHUNK, CHUNK), 0)
    col = lax.broadcasted_iota(jnp.int32, (CHUNK, CHUNK), 1)
    causal = row >= col
    tri = jnp.where(causal, 1.0, 0.0)
    left = col < SSM_HEAD_DIM
    top = row < SSM_HEAD_DIM
    ones = jnp.ones((CHUNK, CHUNK), BF16)
    cb = _dot_nt(cm, bm)
    dt_c, cs_c, cs_last, m = [], [], [], []
    for r in range(SSM_HPG):
        a = -jnp.exp(alog[r])
        dt_c.append(jax.nn.softplus(dtc[r] + bias[r]))
        da_c = dt_c[r] * a
        da_r = jax.nn.softplus(dtr[r] + bias[r]) * a
        cs_c.append(jnp.sum(tri * da_r, axis=1, keepdims=True))
        cs_r = jnp.sum(jnp.where(row <= col, 1.0, 0.0) * da_c, axis=0, keepdims=True)
        cs_last.append(jnp.sum(da_c, axis=0, keepdims=True))
        m.append(cb * jnp.exp(jnp.where(causal, cs_c[r] - cs_r, -1e30)))
    ygs, hn = [], []
    for p in range(SSM_PAIRS):
        a, b = 2 * p, 2 * p + 1
        pair = lambda u, v: jnp.where(left, u, v)
        xdt = xp[p] * pair(dt_c[a], dt_c[b])
        y = pair(_dot(m[a], xdt), _dot(m[b], xdt))
        y = y + _dot_nt(cm, hp[p]) * pair(jnp.exp(cs_c[a]), jnp.exp(cs_c[b]))
        y = y + xp[p] * pair(dsk[a], dsk[b])
        decay = pair(jnp.exp(cs_last[a] - cs_c[a]), jnp.exp(cs_last[b] - cs_c[b]))
        states = _dot_tn(xdt * decay, bm)
        hn.append(hp[p] * jnp.where(top, jnp.exp(cs_last[a]), jnp.exp(cs_last[b])) + states)
        ygs.append(y * (zp[p] * jax.nn.sigmoid(zp[p])))
    ms = sum(_dot(t * t, ones) for t in ygs) * (1.0 / SSM_GROUP_W)
    rs = lax.rsqrt(ms + EPS)
    return [ygs[p] * rs * gnp[p] for p in range(SSM_PAIRS)], hn


def _ssd_in_specs(cidx):
    gw, n = SSM_GROUP_W, SSM_STATE
    bm_blk = D_INNER // n
    return [
        pl.BlockSpec((CHUNK, gw), lambda g, c: (cidx(c), g)),
        pl.BlockSpec((CHUNK, gw), lambda g, c: (cidx(c), g)),
        pl.BlockSpec((CHUNK, n), lambda g, c: (cidx(c), bm_blk + g)),
        pl.BlockSpec((CHUNK, n), lambda g, c: (cidx(c), bm_blk + SSM_GROUPS + g)),
        pl.BlockSpec((None, CHUNK, SSM_HPG), lambda g, c: (g, cidx(c), 0)),
        pl.BlockSpec((None, SSM_HPG, CHUNK), lambda g, c: (g, 0, cidx(c))),
        pl.BlockSpec((None, 1, SSM_HPG), lambda g, c: (g, 0, 0)),
        pl.BlockSpec((None, 1, SSM_HPG), lambda g, c: (g, 0, 0)),
        pl.BlockSpec((None, 1, SSM_HPG), lambda g, c: (g, 0, 0)),
        pl.BlockSpec((1, gw), lambda g, c: (0, g)),
    ]


def _ssd_args(x_ref, z_ref, bm_ref, cm_ref, hp, dtc_ref, dtr_ref, bias_ref, alog_ref, dsk_ref, gn_ref):
    nh, npair, w = SSM_HPG, SSM_PAIRS, 2 * SSM_HEAD_DIM
    col = lambda ref: [ref[:, r:r + 1] for r in range(nh)]
    return (_split(x_ref, npair, w), _split(z_ref, npair, w), bm_ref[...], cm_ref[...], hp,
            col(dtc_ref), [dtr_ref[r:r + 1, :] for r in range(nh)], col(bias_ref), col(alog_ref), col(dsk_ref),
            _split(gn_ref, npair, w))


def _pair_rows(ref):
    w = 2 * SSM_HEAD_DIM
    return [ref[p * w:(p + 1) * w, :] for p in range(SSM_PAIRS)]


def _ssd_fwd(xbc, proj, dt_c, dt_r, bias, alog, dsk, gn, mixcat, *, name):
    w = 2 * SSM_HEAD_DIM

    def body(x_ref, z_ref, bm_ref, cm_ref, dtc_ref, dtr_ref, bias_ref, alog_ref, dsk_ref, gn_ref, cat_in,
             cat_ref, hprev_ref, h_scr):
        del cat_in

        @pl.when(pl.program_id(1) == 0)
        def _():
            h_scr[...] = jnp.zeros_like(h_scr)

        hprev_ref[...] = h_scr[...]
        yn, hn = _ssd_tile(*_ssd_args(x_ref, z_ref, bm_ref, cm_ref, _pair_rows(h_scr), dtc_ref, dtr_ref, bias_ref,
                                      alog_ref, dsk_ref, gn_ref))
        for p in range(SSM_PAIRS):
            cat_ref[:, p * w:(p + 1) * w] = yn[p].astype(cat_ref.dtype)
            h_scr[p * w:(p + 1) * w, :] = hn[p]

    return pl.pallas_call(
        body, grid=(SSM_GROUPS, N_CHUNKS), in_specs=[*_ssd_in_specs(lambda c: c), pl.BlockSpec(memory_space=pl.ANY)],
        out_specs=[pl.BlockSpec((CHUNK, SSM_GROUP_W), lambda g, c: (c, g)),
                   pl.BlockSpec((None, None, SSM_GROUP_W, SSM_STATE), lambda g, c: (c, g, 0, 0))],
        out_shape=[SDS(mixcat.shape, mixcat.dtype), SDS((N_CHUNKS, SSM_GROUPS, SSM_GROUP_W, SSM_STATE), F32)],
        scratch_shapes=[pltpu.VMEM((SSM_GROUP_W, SSM_STATE), F32)],
        input_output_aliases={10: 0}, compiler_params=_cparams(("parallel", "arbitrary")), name=name,
    )(xbc, proj, xbc, xbc, dt_c, dt_r, bias, alog, dsk, gn, mixcat)


def _ssd_bwd(xbc, proj, dt_c, dt_r, bias, alog, dsk, gn, hprev, dcat, dproj, *, name):
    nh, w, gw, n = SSM_HPG, 2 * SSM_HEAD_DIM, SSM_GROUP_W, SSM_STATE
    rev = lambda c: N_CHUNKS - 1 - c

    def body(x_ref, z_ref, bm_ref, cm_ref, dtc_ref, dtr_ref, bias_ref, alog_ref, dsk_ref, gn_ref, hprev_ref, dy_ref,
             dproj_in, dz_ref, dxs_ref, dbm_ref, dcm_ref, ddtc_ref, ddtr_ref, dbias_ref, dalog_ref, ddsk_ref, dgn_ref,
             dh_scr):
        del dproj_in
        first = pl.program_id(1) == 0

        @pl.when(first)
        def _():
            dh_scr[...] = jnp.zeros_like(dh_scr)
            for ref in (dbias_ref, dalog_ref, ddsk_ref, dgn_ref):
                ref[...] = jnp.zeros_like(ref)

        args = _ssd_args(x_ref, z_ref, bm_ref, cm_ref, _pair_rows(hprev_ref), dtc_ref, dtr_ref, bias_ref, alog_ref,
                         dsk_ref, gn_ref)
        _, vjp = jax.vjp(_ssd_tile, *args)
        dxs, dzs, dbm, dcm, dhs, ddtc, ddtr, dbias, dalog, ddsk, dgn = vjp(
            (_split(dy_ref, SSM_PAIRS, w), _pair_rows(dh_scr)))
        dbm_ref[...] = dbm
        dcm_ref[...] = dcm
        for q in range(SSM_PAIRS):
            dxs_ref[:, q * w:(q + 1) * w] = dxs[q]
            dz_ref[:, q * w:(q + 1) * w] = dzs[q].astype(dz_ref.dtype)
            dh_scr[q * w:(q + 1) * w, :] = dhs[q]
            dgn_ref[:, q * w:(q + 1) * w] += dgn[q]
        for r in range(nh):
            ddtc_ref[:, r:r + 1] = ddtc[r]
            ddtr_ref[r:r + 1, :] = ddtr[r]
            dbias_ref[:, r:r + 1] += dbias[r]
            dalog_ref[:, r:r + 1] += dalog[r]
            ddsk_ref[:, r:r + 1] += ddsk[r]

    par_spec = pl.BlockSpec((None, 1, nh), lambda g, c: (g, 0, 0))
    return pl.pallas_call(
        body, grid=(SSM_GROUPS, N_CHUNKS),
        in_specs=[*_ssd_in_specs(rev),
                  pl.BlockSpec((None, None, gw, n), lambda g, c: (rev(c), g, 0, 0)),
                  pl.BlockSpec((CHUNK, gw), lambda g, c: (rev(c), g)),
                  pl.BlockSpec(memory_space=pl.ANY)],
        out_specs=[pl.BlockSpec((CHUNK, gw), lambda g, c: (rev(c), g)),
                   pl.BlockSpec((CHUNK, gw), lambda g, c: (rev(c), g)),
                   pl.BlockSpec((CHUNK, n), lambda g, c: (rev(c), g)),
                   pl.BlockSpec((CHUNK, n), lambda g, c: (rev(c), g)),
                   pl.BlockSpec((None, CHUNK, nh), lambda g, c: (g, rev(c), 0)),
                   pl.BlockSpec((None, nh, CHUNK), lambda g, c: (g, 0, rev(c))),
                   par_spec, par_spec, par_spec,
                   pl.BlockSpec((1, gw), lambda g, c: (0, g))],
        out_shape=[SDS(dproj.shape, dproj.dtype), SDS((SEQ, D_INNER), F32), SDS((SEQ, SSM_GROUPS * n), F32),
                   SDS((SEQ, SSM_GROUPS * n), F32), SDS((SSM_GROUPS, SEQ, nh), F32), SDS((SSM_GROUPS, nh, SEQ), F32),
                   SDS((SSM_GROUPS, 1, nh), F32), SDS((SSM_GROUPS, 1, nh), F32), SDS((SSM_GROUPS, 1, nh), F32),
                   SDS((1, D_INNER), F32)],
        scratch_shapes=[pltpu.VMEM((gw, n), F32)],
        input_output_aliases={12: 0}, compiler_params=_cparams(("parallel", "arbitrary")), name=name,
    )(xbc, proj, xbc, xbc, dt_c, dt_r, bias, alog, dsk, gn, hprev, dcat, dproj)


def _sum_slots(p, *, name):
    n, r, c = p.shape
    tr = _pick(r, (256, 384, 128, 8))

    def body(p_ref, o_ref):
        acc = p_ref[0].astype(F32)
        for s in range(1, n):
            acc = acc + p_ref[s].astype(F32)
        o_ref[...] = acc

    return pl.pallas_call(body, grid=(r // tr,), in_specs=[pl.BlockSpec((n, tr, c), lambda i: (0, i, 0))],
                          out_specs=pl.BlockSpec((tr, c), lambda i: (i, 0)), out_shape=SDS((r, c), F32),
                          compiler_params=_cparams(("parallel",)), name=name)(p)


def _adamw(w, g, m, v, *, name):
    r, c = w.shape
    tr = r if r <= 256 else _pick(r, (256, 128, 8))
    spec = pl.BlockSpec((tr, c), lambda i: (i, 0))

    def body(w_ref, g_ref, m_ref, v_ref, d_ref, mo_ref, vo_ref):
        g = g_ref[...]
        m_new = ADAM_B1 * m_ref[...] + (1.0 - ADAM_B1) * g
        v_new = ADAM_B2 * v_ref[...] + (1.0 - ADAM_B2) * (g * g)
        m_hat = m_new / (1.0 - ADAM_B1 ** ADAM_STEP)
        v_hat = v_new / (1.0 - ADAM_B2 ** ADAM_STEP)
        d_ref[...] = -ADAM_LR * (m_hat / (jnp.sqrt(v_hat) + ADAM_EPS) + ADAM_WD * w_ref[...])
        mo_ref[...] = m_new
        vo_ref[...] = v_new

    return pl.pallas_call(body, grid=(r // tr,), in_specs=[spec] * 4, out_specs=[spec] * 3,
                          out_shape=[SDS((r, c), F32)] * 3, compiler_params=_cparams(("parallel",)), name=name)(w, g, m, v)


ANY = pl.BlockSpec(memory_space=pl.ANY)


def _place():
    x, y, c = lax.axis_index("x"), lax.axis_index("y"), lax.axis_index("c")
    chips = [(1 - x, y), (x, 1 - y), (1 - x, 1 - y)]
    return x, y, c, chips


def _remote(src, dst, send_sem, recv_sem, to):
    return pltpu.make_async_remote_copy(src_ref=src, dst_ref=dst, send_sem=send_sem, recv_sem=recv_sem,
                                        device_id=to, device_id_type=MESH)


STREAM_ROWS = 128


def _stream_rows(i):
    return pl.ds(pl.multiple_of(i * STREAM_ROWS, STREAM_ROWS), STREAM_ROWS)


def _channel_scratch(width, dtype):
    buf = (2, STREAM_ROWS, width)
    return [pltpu.VMEM(buf, dtype), pltpu.VMEM(buf, dtype), *([pltpu.SemaphoreType.DMA((2,))] * 5),
            pltpu.SemaphoreType.REGULAR((2,))]


CHANNEL_REFS = 8


def _copy_through_vmem(src, dst, ch):
    sbuf, _, ld, _, _, st, _, _ = ch
    steps = src.shape[0] // STREAM_ROWS
    assert steps >= 2 and steps * STREAM_ROWS == src.shape[0]

    def load(i, slot):
        return pltpu.make_async_copy(src.at[_stream_rows(i)], sbuf.at[slot], ld.at[slot])

    def store(i, slot):
        return pltpu.make_async_copy(sbuf.at[slot], dst.at[_stream_rows(i)], st.at[slot])

    load(0, 0).start()

    def step(i, carry):
        slot = lax.rem(i, 2)
        nxt = 1 - slot

        @pl.when(i + 1 < steps)
        def _():
            @pl.when(i >= 1)
            def _():
                store(0, nxt).wait()
            load(i + 1, nxt).start()

        load(i, slot).wait()
        store(i, slot).start()
        return carry

    lax.fori_loop(0, steps, step, 0)
    for slot in range(2):
        store(0, slot).wait()


def _exchange_stream(src, dst, keep, ch, sibling):
    sbuf, rbuf, ld, snd, rcv, st, kp, credit = ch
    steps = src.shape[0] // STREAM_ROWS
    assert steps >= 2 and steps * STREAM_ROWS == src.shape[0]

    def load(i, slot):
        return pltpu.make_async_copy(src.at[_stream_rows(i)], sbuf.at[slot], ld.at[slot])

    def push(slot):
        return _remote(sbuf.at[slot], rbuf.at[slot], snd.at[slot], rcv.at[slot], sibling)

    def store(i, slot):
        return pltpu.make_async_copy(rbuf.at[slot], dst.at[_stream_rows(i)], st.at[slot])

    def save(i, slot):
        return pltpu.make_async_copy(sbuf.at[slot], keep.at[_stream_rows(i)], kp.at[slot])

    for slot in range(2):
        pl.semaphore_signal(credit.at[slot], 1, device_id=sibling, device_id_type=MESH)
    load(0, 0).start()

    def step(i, carry):
        slot = lax.rem(i, 2)
        nxt = 1 - slot

        @pl.when(i + 1 < steps)
        def _():
            @pl.when(i >= 1)
            def _():
                push(nxt).wait_send()
                if keep is not None:
                    save(0, nxt).wait()
            load(i + 1, nxt).start()

        load(i, slot).wait()
        pl.semaphore_wait(credit.at[slot], 1)
        push(slot).start()
        if keep is not None:
            save(i, slot).start()
        push(slot).wait_recv()
        store(i, slot).start()

        @pl.when(i >= 1)
        def _():
            store(0, nxt).wait()

            @pl.when(i + 1 < steps)
            def _():
                pl.semaphore_signal(credit.at[nxt], 1, device_id=sibling, device_id_type=MESH)
        return carry

    lax.fori_loop(0, steps, step, 0)
    store(0, (steps - 1) % 2).wait()
    for slot in range(2):
        push(slot).wait_send()
        if keep is not None:
            save(0, slot).wait()


def _all_gather_shards(shards, small, *, name):
    n = len(shards)

    def body(*refs):
        ins, outs = refs[:n + 1], refs[n + 1:2 * n + 2]
        scr = refs[2 * n + 2:]
        chans = [scr[CHANNEL_REFS * t:CHANNEL_REFS * (t + 1)] for t in range(n)]
        send_sems, recv_sems, small_sems = scr[CHANNEL_REFS * n:]
        x, y, c, _ = _place()
        me = 2 * x + y
        sibling = (x, y, 1 - c)
        near = (lax.rem(x + 1 - c, 2), lax.rem(y + c, 2))
        far = (lax.rem(x + c, 2), lax.rem(y + 1 - c, 2))
        k_near, k_far, k_diag = 2 * near[0] + near[1], 2 * far[0] + far[1], 3 - me
        targets = ((*near, c), (*far, c), (*far, c))
        arrives = (k_near, k_far, k_diag)
        streams_in = (k_far, k_near, k_diag)

        def ici(t, j, src, blk):
            return _remote(src, outs[t].at[blk, c], send_sems.at[3 * t + j], recv_sems.at[3 * t + j], targets[j])

        first = [ici(t, j, ins[t].at[c], me) for t in range(n + 1) for j in range(2)]
        for cp in first:
            cp.start()
        small_local = pltpu.make_async_copy(ins[n], outs[n].at[me], small_sems.at[6])
        small_local.start()
        for t in range(n):
            for h in range(2):
                _copy_through_vmem(ins[t].at[h], outs[t].at[me, h], chans[t])
        passed = []
        for j in range(3):
            for t in range(n + 1):
                landed = outs[t].at[arrives[j], c]
                ici(t, j, landed, arrives[j]).wait_recv()
                if j == 0:
                    fwd = ici(t, 2, landed, k_near)
                    fwd.start()
                    passed.append(fwd)
                if t < n:
                    _exchange_stream(landed, outs[t].at[streams_in[j], 1 - c], None, chans[t], sibling)
                else:
                    fwd = _remote(landed, landed, small_sems.at[j], small_sems.at[3 + j], sibling)
                    fwd.start()
                    passed.append(fwd)
        for j in range(3):
            got = outs[n].at[streams_in[j], 1 - c]
            _remote(got, got, small_sems.at[j], small_sems.at[3 + j], sibling).wait_recv()
        for cp in first + passed:
            cp.wait_send()
        small_local.wait()

    scratch = []
    for s in shards:
        scratch += _channel_scratch(s.shape[2], s.dtype)
    return pl.pallas_call(
        body, in_specs=[ANY] * (n + 1), out_specs=[ANY] * (n + 1),
        out_shape=[SDS((N_CHIPS, *s.shape), s.dtype) for s in (*shards, small)],
        scratch_shapes=[*scratch, pltpu.SemaphoreType.DMA((3 * n + 3,)), pltpu.SemaphoreType.DMA((3 * n + 3,)),
                        pltpu.SemaphoreType.DMA((7,))],
        compiler_params=pltpu.CompilerParams(vmem_limit_bytes=VMEM_LIMIT), name=name)(*shards, small)


def _pair_reduce(stacks, *, name):
    n = len(stacks)
    per = 10

    def body(*refs):
        ins, outs, scr = refs[:n], refs[n:2 * n], refs[2 * n:]
        x, y, c, _ = _place()
        sibling = (x, y, 1 - c)
        for t in range(n):
            sbuf, rbuf, obuf, pbuf, ld_s, ld_o, snd, rcv, st, credit = scr[per * t:per * (t + 1)]
            steps = ins[t].shape[1] // STREAM_ROWS
            src, own, out = ins[t].at[1 - c], ins[t].at[c], outs[t]

            def load_s(i, slot, src=src, sbuf=sbuf, ld_s=ld_s):
                return pltpu.make_async_copy(src.at[_stream_rows(i)], sbuf.at[slot], ld_s.at[slot])

            def load_o(i, slot, own=own, obuf=obuf, ld_o=ld_o):
                return pltpu.make_async_copy(own.at[_stream_rows(i)], obuf.at[slot], ld_o.at[slot])

            def push(slot, sbuf=sbuf, rbuf=rbuf, snd=snd, rcv=rcv):
                return _remote(sbuf.at[slot], rbuf.at[slot], snd.at[slot], rcv.at[slot], sibling)

            def store(i, slot, pbuf=pbuf, out=out, st=st):
                return pltpu.make_async_copy(pbuf.at[slot], out.at[_stream_rows(i)], st.at[slot])

            assert steps >= 2
            for slot in range(2):
                pl.semaphore_signal(credit.at[slot], 1, device_id=sibling, device_id_type=MESH)
                load_s(slot, slot).start()
                load_o(slot, slot).start()
            load_s(0, 0).wait()
            pl.semaphore_wait(credit.at[0], 1)
            push(0).start()

            def step(i, carry, load_s=load_s, load_o=load_o, push=push, store=store, rbuf=rbuf, obuf=obuf, pbuf=pbuf,
                     credit=credit, steps=steps):
                slot = lax.rem(i, 2)
                nxt = 1 - slot

                @pl.when(i + 1 < steps)
                def _():
                    load_s(i + 1, nxt).wait()
                    pl.semaphore_wait(credit.at[nxt], 1)
                    push(nxt).start()

                load_o(i, slot).wait()
                push(slot).wait_recv()

                @pl.when(i >= 2)
                def _():
                    store(i, slot).wait()

                pbuf[slot] = (obuf[slot] + rbuf[slot]).astype(pbuf.dtype)
                store(i, slot).start()
                push(slot).wait_send()

                @pl.when(i + 2 < steps)
                def _():
                    load_s(i + 2, slot).start()
                    load_o(i + 2, slot).start()
                    pl.semaphore_signal(credit.at[slot], 1, device_id=sibling, device_id_type=MESH)
                return carry

            lax.fori_loop(0, steps, step, 0)
            for slot in range(2):
                store(0, slot).wait()

    scratch = []
    for s in stacks:
        buf = (2, STREAM_ROWS, s.shape[2])
        scratch += [pltpu.VMEM(buf, F32), pltpu.VMEM(buf, F32), pltpu.VMEM(buf, F32), pltpu.VMEM(buf, BF16),
                    *([pltpu.SemaphoreType.DMA((2,))] * 5), pltpu.SemaphoreType.REGULAR((2,))]
    return pl.pallas_call(
        body, in_specs=[ANY] * n, out_specs=[ANY] * n, out_shape=[SDS(s.shape[1:], BF16) for s in stacks],
        scratch_shapes=scratch, compiler_params=pltpu.CompilerParams(vmem_limit_bytes=VMEM_LIMIT), name=name)(*stacks)


def _chip_scatter(parts, *, name):
    n = len(parts)

    def body(*refs):
        ins, outs, scr = refs[:n], refs[n:2 * n], refs[2 * n:]
        chans = [scr[CHANNEL_REFS * t:CHANNEL_REFS * (t + 1)] for t in range(n)]
        send_sems, recv_sems = scr[CHANNEL_REFS * n:]
        x, y, c, chips = _place()
        me = 2 * x + y
        sends = []
        for t in range(n):
            for j, (cx, cy) in enumerate(chips):
                cp = _remote(ins[t].at[2 * cx + cy], outs[t].at[me], send_sems.at[3 * t + j], recv_sems.at[3 * t + j],
                             (cx, cy, c))
                cp.start()
                sends.append(cp)
        for t in range(n):
            _copy_through_vmem(ins[t].at[me], outs[t].at[me], chans[t])
        for t in range(n):
            for j, (cx, cy) in enumerate(chips):
                landed = outs[t].at[2 * cx + cy]
                _remote(landed, landed, send_sems.at[3 * t + j], recv_sems.at[3 * t + j], (cx, cy, c)).wait_recv()
        for cp in sends:
            cp.wait_send()

    scratch = []
    for p in parts:
        scratch += _channel_scratch(p.shape[2], p.dtype)
    return pl.pallas_call(
        body, in_specs=[ANY] * n, out_specs=[ANY] * n, out_shape=[SDS(p.shape, p.dtype) for p in parts],
        scratch_shapes=[*scratch, pltpu.SemaphoreType.DMA((3 * n,)), pltpu.SemaphoreType.DMA((3 * n,))],
        compiler_params=pltpu.CompilerParams(vmem_limit_bytes=VMEM_LIMIT), name=name)(*parts)


def _pair_share(finals, *, name):
    n = len(finals)

    def body(*refs):
        ins, outs, scr = refs[:n], refs[n:2 * n], refs[2 * n:]
        x, y, c, _ = _place()
        sibling = (x, y, 1 - c)
        for t in range(n):
            _exchange_stream(ins[t], outs[t].at[1 - c], outs[t].at[c], scr[CHANNEL_REFS * t:CHANNEL_REFS * (t + 1)],
                             sibling)

    scratch = []
    for f in finals:
        scratch += _channel_scratch(f.shape[1], f.dtype)
    return pl.pallas_call(
        body, in_specs=[ANY] * n, out_specs=[ANY] * n, out_shape=[SDS((2, *f.shape), f.dtype) for f in finals],
        scratch_shapes=scratch, compiler_params=pltpu.CompilerParams(vmem_limit_bytes=VMEM_LIMIT), name=name)(*finals)


def _all_reduce_small(v, *, name):
    rows, lanes = v.shape
    n_dev = 8

    def body(v_ref, o_ref, all_ref, send_sems, recv_sems, local_sem):
        x, y, c, chips = _place()
        me, sibling = (x, y, c), (x, y, 1 - c)

        def block(px, py, pc):
            return all_ref.at[4 * px + 2 * py + pc]

        def copy(k, blk, to, src=None):
            return _remote(block(*blk) if src is None else src, block(*blk), send_sems.at[k], recv_sems.at[k], to)

        mine = pltpu.make_async_copy(v_ref, block(*me), local_sem)
        mine.start()
        first = [copy(0, me, sibling, src=v_ref)]
        first += [copy(1 + j, me, (*chip, c), src=v_ref) for j, chip in enumerate(chips)]
        for cp in first:
            cp.start()
        passed = [copy(4 + j, (*chip, c), sibling) for j, chip in enumerate(chips)]
        for j, chip in enumerate(chips):
            copy(1 + j, (*chip, c), me).wait_recv()
            passed[j].start()
        copy(0, sibling, me).wait_recv()
        for j, chip in enumerate(chips):
            copy(4 + j, (*chip, 1 - c), me).wait_recv()
        for cp in first + passed:
            cp.wait_send()
        mine.wait()
        acc = all_ref[0]
        for k in range(1, n_dev):
            acc = acc + all_ref[k]
        o_ref[...] = acc

    vmem = pl.BlockSpec(memory_space=pltpu.VMEM)
    return pl.pallas_call(
        body, in_specs=[vmem], out_specs=vmem, out_shape=SDS((rows, lanes), F32),
        scratch_shapes=[pltpu.VMEM((n_dev, rows, lanes), F32), pltpu.SemaphoreType.DMA((7,)),
                        pltpu.SemaphoreType.DMA((7,)), pltpu.SemaphoreType.DMA],
        compiler_params=pltpu.CompilerParams(vmem_limit_bytes=VMEM_LIMIT), name=name)(v)


def _relu2_epilogue(acc):
    return acc, jnp.square(jnp.maximum(acc, 0.0))


def _res_epilogue(acc, res):
    return (acc + res,)


def _drelu2_epilogue(acc, pre):
    return (acc * (2.0 * jnp.maximum(pre.astype(F32), 0.0)),)


def _ffn_fwd(h, g, w1, w2, tag):
    f = _rms_fwd(h, g, name=f"ffn_norm_{tag}")
    pre, act = _mm_nn(f, w1, name=f"ffn1_{tag}", epilogue=_relu2_epilogue, n_out_dtypes=(BF16, BF16))
    h_out = _mm_nn(act, w2, name=f"ffn2_{tag}", extras=(h,), epilogue=_res_epilogue)
    return h_out, (f, pre, act)


def _ffn_bwd(dh, h, g, w1, w2, saved, layer, stacks):
    f, pre, act = saved
    dpre = _mm_nt(dh, w2, name=f"ffn2_dx_{layer}", out_dtype=BF16, extras=(pre,), epilogue=_drelu2_epilogue)
    dw2 = _mm_tn_stacked(act, dh, name=f"ffn2_dw_{layer}", half=layer, col_slots=False, stack=stacks[1])
    df = _mm_nt(dpre, w1, name=f"ffn1_dx_{layer}")
    dw1 = _mm_tn_stacked(f, dpre, name=f"ffn1_dw_{layer}", half=layer, col_slots=True, stack=stacks[0])
    dh, dg = _rms_bwd(h, g, df, dh, name=f"ffn_norm_bwd_{layer}")
    return dh, dg, (dw1, dw2)


def _kv_fwd(mem, g, w_kv, tag):
    m = _rms_fwd(mem, g, name=f"mem_norm_{tag}")
    return m, _mm_nn(m, w_kv, name=f"kv_{tag}")


def _kv_bwd(mem, g, w_kv, m, dk, dv, layer, stack):
    dkv = jnp.concatenate([dk, dv], axis=1)
    dw = _mm_tn_stacked(m, dkv, name=f"kv_dw_{layer}", half=layer, col_slots=True, stack=stack)
    dm = _mm_nt(dkv, w_kv, name=f"kv_dx_{layer}")
    _, dg = _rms_bwd(mem, g, dm, dm, name=f"mem_norm_bwd_{layer}")
    return dw, dg


def _local_step(x, mem, target, p):
    row = lambda v: v.reshape(1, -1)
    g = {}

    h0 = x
    a0 = _rms_fwd(h0, row(p["norm_mix"][0]), name="mix_norm_0")
    proj_a = _mm_nn(a0, p["a_in"], name="a_in")
    m0, kv0 = _kv_fwd(mem, row(p["mem_norm"][0]), p["w_kv"][0], "0")
    cat0 = _attn_fwd(proj_a, 2 * D_INNER, kv0, name="attn_0")
    bs_col = p["a_bs"].reshape(A_GROUPS, CHUNK, 1)
    cat0 = _gate_fwd(proj_a, p["a_ln_g"], p["a_ln_b"], p["a_ws"], bs_col, cat0, name="gate")
    h1 = _mm_nn(cat0, p["w_out"][0], name="out_0", extras=(h0,), epilogue=_res_epilogue)
    h2, ffn0 = _ffn_fwd(h1, row(p["norm_ffn"][0]), p["w_ffn1"][0], p["w_ffn2"][0], "0")

    a1 = _rms_fwd(h2, row(p["norm_mix"][1]), name="mix_norm_1")
    proj_b = _mm_nn(a1, p["b_in"], name="b_in")
    m1, kv1 = _kv_fwd(mem, row(p["mem_norm"][1]), p["w_kv"][1], "1")
    cat1 = _attn_fwd(proj_b, B_Q_OFF, kv1, name="attn_1")
    xbc = _conv_fwd(proj_b, p["b_conv_w"], p["b_conv_b"], name="conv")
    dt_raw = proj_b[:, B_DT_OFF:B_DT_OFF + SSM_HEADS].reshape(SEQ, SSM_GROUPS, SSM_HPG)
    dt_c = jnp.transpose(dt_raw, (1, 0, 2))
    dt_r = jnp.transpose(dt_raw, (1, 2, 0))
    per_head = lambda v: v.reshape(SSM_GROUPS, 1, SSM_HPG)
    ssd_par = (per_head(p["b_dt_bias"]), per_head(p["b_a_log"]), per_head(p["b_d"]), p["b_gnorm"])
    cat1, hprev = _ssd_fwd(xbc, proj_b, dt_c, dt_r, *ssd_par, cat1, name="ssd")
    h3 = _mm_nn(cat1, p["w_out"][1], name="out_1", extras=(h2,), epilogue=_res_epilogue)
    h4, ffn1 = _ffn_fwd(h3, row(p["norm_ffn"][1]), p["w_ffn1"][1], p["w_ffn2"][1], "1")

    loss, dh, g["final_norm"] = _loss_head(h4, row(p["final_norm"]), target, name="loss_head")

    dh, dnf1, dw_ffn = _ffn_bwd(dh, h3, row(p["norm_ffn"][1]), p["w_ffn1"][1], p["w_ffn2"][1], ffn1, 1, (None, None))
    dcat1 = _mm_nt(dh, p["w_out"][1], name="out_dx_1")
    dwo = _mm_tn_stacked(cat1, dh, name="out_dw_1", half=1, col_slots=False)
    dproj_b, dk1, dv1 = _attn_bwd(proj_b, B_Q_OFF, kv1, dcat1, B_IN_PAD, B_Q_OFF, name="attn_bwd_1")
    (dproj_b, dxs, dbm, dcm, ddt_c, ddt_r, g["b_dt_bias"], g["b_a_log"], g["b_d"], g["b_gnorm"]) = _ssd_bwd(
        xbc, proj_b, dt_c, dt_r, *ssd_par, hprev, dcat1, dproj_b, name="ssd_bwd")
    dproj_b, g["b_conv_w"], g["b_conv_b"] = _conv_bwd(proj_b, p["b_conv_w"], p["b_conv_b"], dxs, dbm, dcm, dproj_b,
                                                      name="conv_bwd")
    ddt = jnp.transpose(ddt_c, (1, 0, 2)) + jnp.transpose(ddt_r, (2, 0, 1))
    ddt = jnp.pad(ddt.reshape(SEQ, SSM_HEADS), ((0, 0), (0, B_IN_PAD - B_DT_OFF - SSM_HEADS))).astype(BF16)
    dproj_b = lax.dynamic_update_slice(dproj_b, ddt, (0, B_DT_OFF))
    dwkv, dmn1 = _kv_bwd(mem, row(p["mem_norm"][1]), p["w_kv"][1], m1, dk1, dv1, 1, None)
    half = D_MODEL // 2
    dwb = [_mm_tn(a1, dproj_b, name=f"b_in_dw_{i}", x_cols=(i * half, half)) for i in range(2)]
    da1 = _mm_nt(dproj_b, p["b_in"], name="b_in_dx")
    dh, dnm1 = _rms_bwd(h2, row(p["norm_mix"][1]), da1, dh, name="mix_norm_bwd_1")

    dh, dnf0, dw_ffn = _ffn_bwd(dh, h1, row(p["norm_ffn"][0]), p["w_ffn1"][0], p["w_ffn2"][0], ffn0, 0, dw_ffn)
    dcat0 = _mm_nt(dh, p["w_out"][0], name="out_dx_0")
    dwo = _mm_tn_stacked(cat0, dh, name="out_dw_0", half=0, col_slots=False, stack=dwo)
    dproj_a, dk0, dv0 = _attn_bwd(proj_a, 2 * D_INNER, kv0, dcat0, A_IN, 2 * D_INNER, name="attn_bwd_0")
    dproj_a, g["a_ln_g"], g["a_ln_b"], g["a_ws"], dbs_col = _gate_bwd(
        proj_a, p["a_ln_g"], p["a_ln_b"], p["a_ws"], bs_col, dcat0, dproj_a, name="gate_bwd")
    g["a_bs"] = dbs_col.reshape(A_GROUPS, CHUNK)
    dwkv, dmn0 = _kv_bwd(mem, row(p["mem_norm"][0]), p["w_kv"][0], m0, dk0, dv0, 0, dwkv)
    dwa = None
    for i in range(2):
        dwa = _mm_tn_stacked(a0, dproj_a, name=f"a_in_dw_{i}", half=i, col_slots=True, stack=dwa,
                             x_cols=(i * half, half))
    da0 = _mm_nt(dproj_a, p["a_in"], name="a_in_dx")
    dx, dnm0 = _rms_bwd(h0, row(p["norm_mix"][0]), da0, dh, name="mix_norm_bwd_0")

    g["norm_mix"] = jnp.concatenate([dnm0, dnm1], axis=0)
    g["norm_ffn"] = jnp.concatenate([dnf0, dnf1], axis=0)
    g["mem_norm"] = jnp.concatenate([dmn0, dmn1], axis=0)
    g["w_kv"], g["w_out"], (g["w_ffn1"], g["w_ffn2"]), g["a_in"] = dwkv, dwo, dw_ffn, dwa
    g["b_in"] = jnp.stack([_b_in_grad_slots(d) for d in dwb])
    return loss, dx, g


def _b_in_full(gathered):
    full = jnp.transpose(gathered, (1, 0, 2)).reshape(D_MODEL, B_IN)
    dt0 = D_INNER + CONV_DIM
    return jnp.concatenate([full[:, :dt0], full[:, dt0 + SSM_HEADS:], full[:, dt0:dt0 + SSM_HEADS],
                            jnp.zeros((D_MODEL, B_IN_PAD - B_IN), full.dtype)], axis=1)


def _b_in_grad_slots(d):
    dt0 = D_INNER + CONV_DIM
    full = jnp.concatenate([d[:, :dt0], d[:, B_DT_OFF:B_DT_OFF + SSM_HEADS], d[:, dt0:B_DT_OFF]], axis=1)
    return jnp.transpose(full.reshape(d.shape[0], N_CHIPS, B_IN // N_CHIPS), (1, 0, 2))


LARGE = ("w_kv", "w_out", "w_ffn1", "w_ffn2", "a_in", "b_in")
SMALL_REPL = ("norm_mix", "norm_ffn", "mem_norm", "a_ln_g", "a_ln_b", "a_ws", "a_bs", "b_dt_bias", "b_a_log", "b_d",
              "final_norm")
SMALL_SHARD = ("b_conv_w", "b_conv_b", "b_gnorm")
WEIGHTS = ("norm_mix", "norm_ffn", "mem_norm", "w_kv", "w_out", "w_ffn1", "w_ffn2", "a_in", "a_ln_g", "a_ln_b", "a_ws",
           "a_bs", "b_in", "b_conv_w", "b_conv_b", "b_dt_bias", "b_a_log", "b_d", "b_gnorm", "final_norm")
CONV_SHARD = CONV_DIM // N_CHIPS
GN_SHARD = D_INNER // N_CHIPS


def _halves(w):
    if w.shape[0] == 2:
        return w
    return w.reshape(2, w.shape[1] // 2, w.shape[2])


def _gather_weights(w):
    big = [_halves(w[k]).astype(BF16) for k in LARGE]
    small = jnp.zeros((2, CONV_K, CONV_SHARD), F32)
    small = small.at[0].set(w["b_conv_w"][0])
    small = small.at[1, 0].set(w["b_conv_b"][0])
    small = small.at[1, 1, :GN_SHARD].set(w["b_gnorm"][0])
    gathered = _all_gather_shards(big, small, name="gather_weights")
    p = {}
    kv, wo, w1, w2, a_in, b_in, sm = gathered
    p["w_kv"] = [kv[:, l] for l in range(2)]
    p["w_out"] = [wo[:, l].reshape(MIX_OUT, D_MODEL) for l in range(2)]
    p["w_ffn1"] = [w1[:, l] for l in range(2)]
    p["w_ffn2"] = [w2[:, l].reshape(D_FF, D_MODEL) for l in range(2)]
    p["a_in"] = a_in.reshape(N_CHIPS, D_MODEL, A_IN // N_CHIPS)
    p["b_in"] = _b_in_full(b_in.reshape(N_CHIPS, D_MODEL, B_IN // N_CHIPS))
    p["b_conv_w"] = jnp.transpose(sm[:, 0], (1, 0, 2)).reshape(CONV_K, CONV_DIM)
    p["b_conv_b"] = sm[:, 1, 0].reshape(1, CONV_DIM)
    p["b_gnorm"] = sm[:, 1, 1, :GN_SHARD].reshape(1, D_INNER)
    return p


def _reduce_large(g):
    stacks = [g[k].reshape(2, -1, g[k].shape[-1]) for k in LARGE]
    parts = _pair_reduce(stacks, name="grads_pair_reduce")
    parts = [t.reshape(N_CHIPS, -1, t.shape[-1]) for t in parts]
    landed = _chip_scatter(parts, name="grads_chip_scatter")
    finals = [_sum_slots(t, name=f"grads_chip_sum_{k}") for k, t in zip(LARGE, landed)]
    shared = _pair_share(finals, name="grads_pair_share")
    return dict(zip(LARGE, shared))


def _small_layout(shapes):
    offs, o = {}, 0
    for k in (*SMALL_REPL, *SMALL_SHARD):
        size = math.prod(shapes[k])
        offs[k] = (o, size)
        o += size
    rows = -(-o // (8 * 128)) * 8
    return offs, rows


def _reduce_small(g, full_shapes):
    offs, rows = _small_layout(full_shapes)
    flat = jnp.concatenate([g[k].reshape(-1) for k in (*SMALL_REPL, *SMALL_SHARD)])
    flat = jnp.pad(flat, (0, rows * 128 - flat.shape[0])).reshape(rows, 128)
    total = _all_reduce_small(flat, name="grads_small_all_reduce").reshape(-1)
    return {k: total[o:o + n].reshape(full_shapes[k]) for k, (o, n) in offs.items()}


def kernel(x, mem, norm_mix, norm_ffn, mem_norm, w_kv, w_out, w_ffn1, w_ffn2, a_in, a_ln_g, a_ln_b, a_ws, a_bs, b_in, b_conv_w, b_conv_b, b_dt_bias, b_a_log, b_d, b_gnorm, final_norm, loss_target, m_norm_mix, m_norm_ffn, m_mem_norm, m_w_kv, m_w_out, m_w_ffn1, m_w_ffn2, m_a_in, m_a_ln_g, m_a_ln_b, m_a_ws, m_a_bs, m_b_in, m_b_conv_w, m_b_conv_b, m_b_dt_bias, m_b_a_log, m_b_d, m_b_gnorm, m_final_norm, v_norm_mix, v_norm_ffn, v_mem_norm, v_w_kv, v_w_out, v_w_ffn1, v_w_ffn2, v_a_in, v_a_ln_g, v_a_ln_b, v_a_ws, v_a_bs, v_b_in, v_b_conv_w, v_b_conv_b, v_b_dt_bias, v_b_a_log, v_b_d, v_b_gnorm, v_final_norm):
    w = dict(norm_mix=norm_mix, norm_ffn=norm_ffn, mem_norm=mem_norm, w_kv=w_kv, w_out=w_out, w_ffn1=w_ffn1,
             w_ffn2=w_ffn2, a_in=a_in, a_ln_g=a_ln_g, a_ln_b=a_ln_b, a_ws=a_ws, a_bs=a_bs, b_in=b_in, b_conv_w=b_conv_w,
             b_conv_b=b_conv_b, b_dt_bias=b_dt_bias, b_a_log=b_a_log, b_d=b_d, b_gnorm=b_gnorm, final_norm=final_norm)
    mom = dict(norm_mix=m_norm_mix, norm_ffn=m_norm_ffn, mem_norm=m_mem_norm, w_kv=m_w_kv, w_out=m_w_out,
               w_ffn1=m_w_ffn1, w_ffn2=m_w_ffn2, a_in=m_a_in, a_ln_g=m_a_ln_g, a_ln_b=m_a_ln_b, a_ws=m_a_ws,
               a_bs=m_a_bs, b_in=m_b_in, b_conv_w=m_b_conv_w, b_conv_b=m_b_conv_b, b_dt_bias=m_b_dt_bias,
               b_a_log=m_b_a_log, b_d=m_b_d, b_gnorm=m_b_gnorm, final_norm=m_final_norm)
    var = dict(norm_mix=v_norm_mix, norm_ffn=v_norm_ffn, mem_norm=v_mem_norm, w_kv=v_w_kv, w_out=v_w_out,
               w_ffn1=v_w_ffn1, w_ffn2=v_w_ffn2, a_in=v_a_in, a_ln_g=v_a_ln_g, a_ln_b=v_a_ln_b, a_ws=v_a_ws,
               a_bs=v_a_bs, b_in=v_b_in, b_conv_w=v_b_conv_w, b_conv_b=v_b_conv_b, b_dt_bias=v_b_dt_bias,
               b_a_log=v_b_a_log, b_d=v_b_d, b_gnorm=v_b_gnorm, final_norm=v_final_norm)

    p = _gather_weights(w)
    p.update(norm_mix=norm_mix, norm_ffn=norm_ffn, mem_norm=mem_norm, a_ln_g=a_ln_g, a_ln_b=a_ln_b, a_ws=a_ws[0],
             a_bs=a_bs[0], b_dt_bias=b_dt_bias, b_a_log=b_a_log, b_d=b_d, final_norm=final_norm)
    loss_part, dx, g = _local_step(x[0], mem[0], loss_target[0], p)
    loss = lax.psum(loss_part[0, 0], ("x", "y", "c"))

    full_shapes = {k: w[k].shape for k in SMALL_REPL}
    full_shapes.update(b_conv_w=(1, CONV_K, CONV_DIM), b_conv_b=(1, CONV_DIM), b_gnorm=(1, D_INNER))
    gs = _reduce_small(g, full_shapes)
    chip = 2 * lax.axis_index("x") + lax.axis_index("y")
    gs["b_conv_w"] = lax.dynamic_slice_in_dim(gs["b_conv_w"], chip * CONV_SHARD, CONV_SHARD, axis=2)
    gs["b_conv_b"] = lax.dynamic_slice_in_dim(gs["b_conv_b"], chip * CONV_SHARD, CONV_SHARD, axis=1)
    gs["b_gnorm"] = lax.dynamic_slice_in_dim(gs["b_gnorm"], chip * GN_SHARD, GN_SHARD, axis=1)
    gl = _reduce_large(g)
    grads = {k: (gl[k].reshape(w[k].shape) if k in gl else gs[k]) for k in WEIGHTS}

    delta, new_m, new_v = {}, {}, {}
    for k in WEIGHTS:
        shape = w[k].shape
        flat = (lambda a: a.reshape(-1, shape[-1])) if len(shape) > 1 else (lambda a: a.reshape(1, -1))
        d, m_new, v_new = _adamw(flat(w[k]), flat(grads[k]), flat(mom[k]), flat(var[k]), name=f"adamw_{k}")
        delta[k], new_m[k], new_v[k] = d.reshape(shape), m_new.reshape(shape), v_new.reshape(shape)

    return (loss, dx.reshape(x.shape), *[grads[k] for k in WEIGHTS], *[delta[k] for k in WEIGHTS],
            *[new_m[k] for k in WEIGHTS], *[new_v[k] for k in WEIGHTS])
```

```python
import math

import jax
import jax.numpy as jnp
from jax import lax
from jax.experimental import pallas as pl
from jax.experimental.pallas import tpu as pltpu

F32 = jnp.float32
BF16 = jnp.bfloat16
SDS = jax.ShapeDtypeStruct

D_MODEL = 1024
SEQ = 2048
CHUNK = 128
N_MEM = 256
D_INNER = 2048
A_GROUPS = 8
A_GROUP_W = D_INNER // A_GROUPS
SSM_HEADS = 32
SSM_HEAD_DIM = 64
SSM_GROUPS = 4
SSM_HPG = 8
SSM_STATE = 128
SSM_GROUP_W = SSM_HPG * SSM_HEAD_DIM
CONV_K = 4
CONV_DIM = 3072
X_HEADS = 4
X_HEAD_DIM = 256
X_WIDTH = 1024
MIX_OUT = 3072
D_FF = 4096
A_IN = 5120
B_IN = 6176
B_IN_PAD = 6272
B_Q_OFF = 5120
B_DT_OFF = 6144
N_CHUNKS = SEQ // CHUNK
EPS = 1e-6
N_CHIPS = 4

ADAM_LR = 0.001
ADAM_B1 = 0.9
ADAM_B2 = 0.999
ADAM_EPS = 1e-08
ADAM_WD = 0.01
ADAM_STEP = 10

VMEM_LIMIT = 48 * 1024 * 1024
MESH = pl.DeviceIdType.MESH


def _cparams(sem):
    return pltpu.CompilerParams(dimension_semantics=sem, vmem_limit_bytes=VMEM_LIMIT)


def _dot(a, b, dims=(((1,), (0,)), ((), ()))):
    return lax.dot_general(a.astype(BF16), b.astype(BF16), dims, preferred_element_type=F32)


def _dot_nt(a, b):
    return _dot(a, b, (((1,), (1,)), ((), ())))


def _dot_tn(a, b):
    return _dot(a, b, (((0,), (0,)), ((), ())))


def _pick(n, cands):
    for c in cands:
        if n % c == 0:
            return c
    raise ValueError(f"no tile for {n}")


def _mm_call(a, b, *, dims, grid, a_spec, b_spec, acc_shape, out_shapes, out_specs, name,
             extras=(), extra_specs=(), epilogue=None, after=()):
    n_k = grid[2]
    n_extra = len(extras)
    n_out = len(out_shapes)
    n_in = 2 + n_extra + len(after)

    def body(*refs):
        a_ref, b_ref = refs[0], refs[1]
        extra_refs = refs[2:2 + n_extra]
        out_refs = refs[n_in:n_in + n_out]
        acc = refs[-1]
        k = pl.program_id(2)

        @pl.when(k == 0)
        def _():
            acc[...] = jnp.zeros_like(acc)

        acc[...] += _dot(a_ref[...], b_ref[...], dims)

        @pl.when(k == n_k - 1)
        def _():
            vals = (acc[...],) if epilogue is None else epilogue(acc[...], *[e[...] for e in extra_refs])
            for o_ref, v in zip(out_refs, vals):
                o_ref[...] = v.astype(o_ref.dtype)

    return pl.pallas_call(
        body, grid=grid, in_specs=[a_spec, b_spec, *extra_specs, *([ANY] * len(after))], out_specs=list(out_specs),
        out_shape=list(out_shapes), scratch_shapes=[pltpu.VMEM(acc_shape, F32)],
        compiler_params=_cparams(("parallel", "parallel", "arbitrary")), name=name,
    )(a, b, *extras, *after)


def _w_dims(w):
    if w.ndim == 2:
        return w.shape[0], w.shape[1], 1, w.shape[1]
    return w.shape[1], w.shape[0] * w.shape[2], w.shape[0], w.shape[2]


def _mm_nn(a, w, *, name, out_dtype=F32, a_cols=None, extras=(), epilogue=None, n_out_dtypes=None):
    m = a.shape[0]
    k_dim, n_dim, _, n_slot = _w_dims(w)
    a_off, a_w = (0, a.shape[1]) if a_cols is None else a_cols
    assert a_w == k_dim
    tm = _pick(m, (2048, 1024, 512, 256))
    tn = _pick(n_slot, (512, 896, 640, 256, 128))
    tk = _pick(k_dim, (1024, 768, 512, 384, 256, 128))
    assert a_off % tk == 0
    nb = n_slot // tn
    a_spec = pl.BlockSpec((tm, tk), lambda i, j, k: (i, a_off // tk + k))
    if w.ndim == 2:
        b_spec = pl.BlockSpec((tk, tn), lambda i, j, k: (k, j))
    else:
        b_spec = pl.BlockSpec((None, tk, tn), lambda i, j, k: (j // nb, k, j % nb))
    o_spec = pl.BlockSpec((tm, tn), lambda i, j, k: (i, j))
    dts = n_out_dtypes or (out_dtype,)
    outs = _mm_call(a, w, dims=(((1,), (0,)), ((), ())), grid=(m // tm, n_dim // tn, k_dim // tk),
                    a_spec=a_spec, b_spec=b_spec, acc_shape=(tm, tn),
                    out_shapes=[SDS((m, n_dim), dt) for dt in dts], out_specs=[o_spec] * len(dts), name=name,
                    extras=extras, extra_specs=[o_spec] * len(extras), epilogue=epilogue)
    return outs if n_out_dtypes else outs[0]


def _mm_nt(a, w, *, name, out_dtype=F32, extras=(), epilogue=None, after=()):
    m = a.shape[0]
    k_dim, n_dim, _, n_slot = _w_dims(w)
    assert a.shape[1] == n_dim
    tm = _pick(m, (2048, 1024, 512, 256))
    to = _pick(k_dim, (512, 384, 256, 128))
    tc = _pick(n_slot, (1024, 896, 640, 512, 256, 128))
    nb = n_slot // tc
    a_spec = pl.BlockSpec((tm, tc), lambda i, j, k: (i, k))
    if w.ndim == 2:
        b_spec = pl.BlockSpec((to, tc), lambda i, j, k: (j, k))
    else:
        b_spec = pl.BlockSpec((None, to, tc), lambda i, j, k: (k // nb, j, k % nb))
    o_spec = pl.BlockSpec((tm, to), lambda i, j, k: (i, j))
    return _mm_call(a, w, dims=(((1,), (1,)), ((), ())), grid=(m // tm, k_dim // to, n_dim // tc),
                    a_spec=a_spec, b_spec=b_spec, acc_shape=(tm, to),
                    out_shapes=[SDS((m, k_dim), out_dtype)], out_specs=[o_spec], name=name,
                    extras=extras, extra_specs=[o_spec] * len(extras), epilogue=epilogue, after=after)[0]


def _mm_tn(x, dy, *, name, x_cols=None):
    s = x.shape[0]
    x_off, k_dim = (0, x.shape[1]) if x_cols is None else x_cols
    n_dim = dy.shape[1]
    tm = _pick(k_dim, (1024, 768, 512, 384, 256, 128))
    tn = _pick(n_dim, (512, 896, 640, 256, 128))
    tk = _pick(s, (2048, 1024, 512, 256))
    assert x_off % tm == 0
    a_spec = pl.BlockSpec((tk, tm), lambda i, j, k: (k, x_off // tm + i))
    b_spec = pl.BlockSpec((tk, tn), lambda i, j, k: (k, j))
    o_spec = pl.BlockSpec((tm, tn), lambda i, j, k: (i, j))
    return _mm_call(x, dy, dims=(((0,), (0,)), ((), ())), grid=(k_dim // tm, n_dim // tn, s // tk),
                    a_spec=a_spec, b_spec=b_spec, acc_shape=(tm, tn),
                    out_shapes=[SDS((k_dim, n_dim), F32)], out_specs=[o_spec], name=name)[0]


def _mm_tn_stacked(x, dy, *, name, col_slots):
    s, k_dim = x.shape
    n_dim = dy.shape[1]
    r, c = (k_dim // 2, n_dim // N_CHIPS) if col_slots else (k_dim // N_CHIPS // 2, n_dim)
    tm = _pick(r, (512, 384, 256, 128))
    tn = _pick(c, (512, 896, 640, 256, 128))
    tk = _pick(s, (2048, 1024, 512, 256))
    a_spec = pl.BlockSpec((tk, tm), lambda i, j, k: (k, i))
    b_spec = pl.BlockSpec((tk, tn), lambda i, j, k: (k, j))
    rb = r // tm
    if col_slots:
        nb = c // tn
        o_spec = pl.BlockSpec((None, None, tm, tn), lambda i, j, k: (i // rb, j // nb, i % rb, j % nb))
    else:
        o_spec = pl.BlockSpec((None, None, tm, tn), lambda i, j, k: ((i // rb) % 2, i // (2 * rb), i % rb, j))
    return _mm_call(x, dy, dims=(((0,), (0,)), ((), ())), grid=(k_dim // tm, n_dim // tn, s // tk),
                    a_spec=a_spec, b_spec=b_spec, acc_shape=(tm, tn),
                    out_shapes=[SDS((2, N_CHIPS, r, c), F32)], out_specs=[o_spec], name=name)[0]


def _rms(x, g):
    return x * lax.rsqrt(jnp.mean(x * x, axis=-1, keepdims=True) + EPS) * g


def _rms_fwd(h, g, *, name):
    rows, d = h.shape
    tr = _pick(rows, (512, 256))

    def body(h_ref, g_ref, o_ref):
        o_ref[...] = _rms(h_ref[...], g_ref[...]).astype(o_ref.dtype)

    return pl.pallas_call(
        body, grid=(rows // tr,),
        in_specs=[pl.BlockSpec((tr, d), lambda i: (i, 0)), pl.BlockSpec((1, d), lambda i: (0, 0))],
        out_specs=pl.BlockSpec((tr, d), lambda i: (i, 0)), out_shape=SDS((rows, d), BF16),
        compiler_params=_cparams(("parallel",)), name=name)(h, g)


def _rms_bwd(h, g, da, dres, *, name):
    rows, d = h.shape
    tr = _pick(rows, (512, 256))

    def body(h_ref, g_ref, da_ref, dres_ref, dh_ref, dg_ref):
        _, vjp = jax.vjp(_rms, h_ref[...], g_ref[...])
        dh, dg = vjp(da_ref[...].astype(F32))
        dh_ref[...] = dres_ref[...] + dh

        @pl.when(pl.program_id(0) == 0)
        def _():
            dg_ref[...] = jnp.zeros_like(dg_ref)

        dg_ref[...] += dg

    row_spec = pl.BlockSpec((tr, d), lambda i: (i, 0))
    vec_spec = pl.BlockSpec((1, d), lambda i: (0, 0))
    return pl.pallas_call(
        body, grid=(rows // tr,), in_specs=[row_spec, vec_spec, row_spec, row_spec],
        out_specs=[row_spec, vec_spec], out_shape=[SDS((rows, d), F32), SDS((1, d), F32)],
        compiler_params=_cparams(("arbitrary",)), name=name)(h, g, da, dres)


def _loss_head(h, g, target, *, name):
    rows, d = h.shape
    tr = _pick(rows, (512, 256))

    def body(h_ref, g_ref, t_ref, loss_ref, dh_ref, dg_ref):
        y, vjp = jax.vjp(_rms, h_ref[...], g_ref[...])
        err = y - t_ref[...]
        dh, dg = vjp(err * (1.0 / d))
        dh_ref[...] = dh

        @pl.when(pl.program_id(0) == 0)
        def _():
            dg_ref[...] = jnp.zeros_like(dg_ref)
            loss_ref[...] = jnp.zeros_like(loss_ref)

        dg_ref[...] += dg
        part = jnp.sum(jnp.sum(err * err, axis=-1, keepdims=True), axis=0, keepdims=True) * (0.5 / d)
        loss_ref[...] += jnp.broadcast_to(part, loss_ref.shape)

    row_spec = pl.BlockSpec((tr, d), lambda i: (i, 0))
    vec_spec = pl.BlockSpec((1, d), lambda i: (0, 0))
    loss_spec = pl.BlockSpec((8, 128), lambda i: (0, 0))
    return pl.pallas_call(
        body, grid=(rows // tr,), in_specs=[row_spec, vec_spec, row_spec],
        out_specs=[loss_spec, row_spec, vec_spec],
        out_shape=[SDS((8, 128), F32), SDS((rows, d), F32), SDS((1, d), F32)],
        compiler_params=_cparams(("arbitrary",)), name=name)(h, g, target)


def _gelu(x):
    return 0.5 * x * (1.0 + lax.erf(x * (1.0 / math.sqrt(2.0))))


def _gate_tile(pu, pv, ln_g, ln_b, ws, bs_t):
    u = [_gelu(p) for p in pu]
    v = [_gelu(p) for p in pv]
    mu = sum(jnp.sum(t, axis=-1, keepdims=True) for t in v) * (1.0 / D_INNER)
    vc = [t - mu for t in v]
    var = sum(jnp.sum(t * t, axis=-1, keepdims=True) for t in vc) * (1.0 / D_INNER)
    rstd = lax.rsqrt(var + EPS)
    row = lax.broadcasted_iota(jnp.int32, (CHUNK, CHUNK), 0)
    col = lax.broadcasted_iota(jnp.int32, (CHUNK, CHUNK), 1)
    out = []
    for gi in range(A_GROUPS):
        vn = vc[gi] * rstd * ln_g[gi] + ln_b[gi]
        w = jnp.where(row >= col, ws[gi], 0.0)
        sv = _dot(w, vn) + bs_t[gi]
        out.append(u[gi] * sv)
    return out


def _split(ref, n, width):
    return [ref[:, i * width:(i + 1) * width] for i in range(n)]


def _gate_in_specs():
    return [
        pl.BlockSpec((CHUNK, D_INNER), lambda c: (c, 0)),
        pl.BlockSpec((CHUNK, D_INNER), lambda c: (c, 1)),
        pl.BlockSpec((1, D_INNER), lambda c: (0, 0)),
        pl.BlockSpec((1, D_INNER), lambda c: (0, 0)),
        pl.BlockSpec((A_GROUPS, CHUNK, CHUNK), lambda c: (0, 0, 0)),
        pl.BlockSpec((A_GROUPS, CHUNK, 1), lambda c: (0, 0, 0)),
    ]


def _gate_args(u_ref, v_ref, g_ref, b_ref, ws_ref, bs_ref):
    ng, gw = A_GROUPS, A_GROUP_W
    return (_split(u_ref, ng, gw), _split(v_ref, ng, gw), _split(g_ref, ng, gw), _split(b_ref, ng, gw),
            [ws_ref[i] for i in range(ng)], [bs_ref[i] for i in range(ng)])


def _gate_fwd(proj, ln_g, ln_b, ws, bs_col, mixcat, *, name):
    def body(u_ref, v_ref, g_ref, b_ref, ws_ref, bs_ref, cat_in, cat_ref):
        del cat_in
        out = _gate_tile(*_gate_args(u_ref, v_ref, g_ref, b_ref, ws_ref, bs_ref))
        for gi, o in enumerate(out):
            cat_ref[:, gi * A_GROUP_W:(gi + 1) * A_GROUP_W] = o.astype(cat_ref.dtype)

    return pl.pallas_call(
        body, grid=(N_CHUNKS,), in_specs=[*_gate_in_specs(), pl.BlockSpec(memory_space=pl.ANY)],
        out_specs=pl.BlockSpec((CHUNK, D_INNER), lambda c: (c, 0)), out_shape=SDS(mixcat.shape, mixcat.dtype),
        input_output_aliases={6: 0}, compiler_params=_cparams(("parallel",)), name=name,
    )(proj, proj, ln_g, ln_b, ws, bs_col, mixcat)


def _gate_bwd(proj, ln_g, ln_b, ws, bs_col, dcat, dproj, *, name):
    ng, gw = A_GROUPS, A_GROUP_W

    def body(u_ref, v_ref, g_ref, b_ref, ws_ref, bs_ref, d_ref, dproj_in, dproj_ref, dg_ref, db_ref, dws_ref, dbs_ref):
        del dproj_in
        args = _gate_args(u_ref, v_ref, g_ref, b_ref, ws_ref, bs_ref)
        _, vjp = jax.vjp(_gate_tile, *args)
        dpu, dpv, dg, db, dws, dbs = vjp(_split(d_ref, ng, gw))
        for gi in range(ng):
            dproj_ref[:, gi * gw:(gi + 1) * gw] = dpu[gi].astype(dproj_ref.dtype)
            dproj_ref[:, D_INNER + gi * gw:D_INNER + (gi + 1) * gw] = dpv[gi].astype(dproj_ref.dtype)

        @pl.when(pl.program_id(0) == 0)
        def _():
            for r in (dg_ref, db_ref, dws_ref, dbs_ref):
                r[...] = jnp.zeros_like(r)

        for gi in range(ng):
            dg_ref[:, gi * gw:(gi + 1) * gw] += dg[gi]
            db_ref[:, gi * gw:(gi + 1) * gw] += db[gi]
            dws_ref[gi] += dws[gi]
            dbs_ref[gi] += dbs[gi]

    in_specs = _gate_in_specs()
    return pl.pallas_call(
        body, grid=(N_CHUNKS,),
        in_specs=[*in_specs, pl.BlockSpec((CHUNK, D_INNER), lambda c: (c, 0)), pl.BlockSpec(memory_space=pl.ANY)],
        out_specs=[pl.BlockSpec((CHUNK, 2 * D_INNER), lambda c: (c, 0)), *in_specs[2:]],
        out_shape=[SDS(dproj.shape, dproj.dtype), SDS((1, D_INNER), F32), SDS((1, D_INNER), F32),
                   SDS((ng, CHUNK, CHUNK), F32), SDS((ng, CHUNK, 1), F32)],
        input_output_aliases={7: 0}, compiler_params=_cparams(("arbitrary",)), name=name,
    )(proj, proj, ln_g, ln_b, ws, bs_col, dcat, dproj)


ATT_TQ = 512


def _attn_tile(q, k, v):
    s = _dot_nt(q, k) * (1.0 / math.sqrt(X_HEAD_DIM))
    s = s - jnp.max(s, axis=-1, keepdims=True)
    e = jnp.exp(s)
    p = e / jnp.sum(e, axis=-1, keepdims=True)
    return _dot(p, v)


def _attn_in_specs(q_blk, order):
    hd = X_HEAD_DIM
    return [
        pl.BlockSpec((ATT_TQ, hd), lambda a, b: (order(a, b)[0], q_blk + order(a, b)[1])),
        pl.BlockSpec((N_MEM, hd), lambda a, b: (0, order(a, b)[1])),
        pl.BlockSpec((N_MEM, hd), lambda a, b: (0, X_HEADS + order(a, b)[1])),
    ]


def _attn_fwd(proj, q_off, kv, *, name):
    order = lambda i, h: (i, h)
    cat_blk = D_INNER // X_HEAD_DIM

    def body(q_ref, k_ref, v_ref, o_ref):
        o_ref[...] = _attn_tile(q_ref[...], k_ref[...], v_ref[...]).astype(o_ref.dtype)

    return pl.pallas_call(
        body, grid=(SEQ // ATT_TQ, X_HEADS), in_specs=_attn_in_specs(q_off // X_HEAD_DIM, order),
        out_specs=pl.BlockSpec((ATT_TQ, X_HEAD_DIM), lambda i, h: (i, cat_blk + h)),
        out_shape=SDS((SEQ, MIX_OUT), BF16), compiler_params=_cparams(("parallel", "parallel")), name=name,
    )(proj, kv, kv)


def _attn_bwd(proj, q_off, kv, dcat, dproj_width, dq_off, *, name):
    order = lambda h, i: (i, h)
    cat_blk = D_INNER // X_HEAD_DIM
    dq_blk = dq_off // X_HEAD_DIM

    def body(q_ref, k_ref, v_ref, do_ref, dq_ref, dk_ref, dv_ref):
        _, vjp = jax.vjp(_attn_tile, q_ref[...], k_ref[...], v_ref[...])
        dq, dk, dv = vjp(do_ref[...])
        dq_ref[...] = dq.astype(dq_ref.dtype)

        @pl.when(pl.program_id(1) == 0)
        def _():
            dk_ref[...] = jnp.zeros_like(dk_ref)
            dv_ref[...] = jnp.zeros_like(dv_ref)

        dk_ref[...] += dk
        dv_ref[...] += dv

    kv_spec = pl.BlockSpec((N_MEM, X_HEAD_DIM), lambda h, i: (0, h))
    return pl.pallas_call(
        body, grid=(X_HEADS, SEQ // ATT_TQ),
        in_specs=[*_attn_in_specs(q_off // X_HEAD_DIM, order),
                  pl.BlockSpec((ATT_TQ, X_HEAD_DIM), lambda h, i: (i, cat_blk + h))],
        out_specs=[pl.BlockSpec((ATT_TQ, X_HEAD_DIM), lambda h, i: (i, dq_blk + h)), kv_spec, kv_spec],
        out_shape=[SDS((SEQ, dproj_width), BF16), SDS((N_MEM, X_WIDTH), F32), SDS((N_MEM, X_WIDTH), F32)],
        compiler_params=_cparams(("parallel", "arbitrary")), name=name,
    )(proj, kv, kv, dcat)


CONV_TC = 512


def _shift_down(x, s):
    if s == 0:
        return x
    row = lax.broadcasted_iota(jnp.int32, x.shape, 0)
    return jnp.where(row >= s, pltpu.roll(x, s, 0), 0.0)


def _shift_up(x, s):
    if s == 0:
        return x
    n = x.shape[0]
    row = lax.broadcasted_iota(jnp.int32, x.shape, 0)
    return jnp.where(row < n - s, pltpu.roll(x, n - s, 0), 0.0)


def _conv_pre(x, w_ref, b_ref):
    pre = b_ref[...] + jnp.zeros_like(x)
    for k in range(CONV_K):
        pre = pre + w_ref[k:k + 1, :] * _shift_down(x, CONV_K - 1 - k)
    return pre


def _conv_fwd(proj, w, b, *, name):
    blk0 = D_INNER // CONV_TC

    def body(x_ref, w_ref, b_ref, o_ref):
        pre = _conv_pre(x_ref[...], w_ref, b_ref)
        o_ref[...] = pre * jax.nn.sigmoid(pre)

    return pl.pallas_call(
        body, grid=(CONV_DIM // CONV_TC,),
        in_specs=[pl.BlockSpec((SEQ, CONV_TC), lambda j: (0, blk0 + j)), pl.BlockSpec((CONV_K, CONV_TC), lambda j: (0, j)),
                  pl.BlockSpec((1, CONV_TC), lambda j: (0, j))],
        out_specs=pl.BlockSpec((SEQ, CONV_TC), lambda j: (0, j)), out_shape=SDS((SEQ, CONV_DIM), F32),
        compiler_params=_cparams(("parallel",)), name=name)(proj, w, b)


def _conv_bwd(proj, w, b, dxs, dbm, dcm, dproj, *, name):
    tc = CONV_TC // 2
    blk0 = D_INNER // tc
    n_x = D_INNER // tc
    n_b = SSM_GROUPS * SSM_STATE // tc

    def body(x_ref, w_ref, b_ref, dxs_ref, dbm_ref, dcm_ref, dproj_in, dproj_ref, dw_ref, db_ref):
        del dproj_in
        j = pl.program_id(0)
        x = x_ref[...]
        pre = _conv_pre(x, w_ref, b_ref)
        sg = jax.nn.sigmoid(pre)
        dact = jnp.where(j < n_x, dxs_ref[...], jnp.where(j < n_x + n_b, dbm_ref[...], dcm_ref[...]))
        dpre = dact * (sg * (1.0 + pre * (1.0 - sg)))
        dx = jnp.zeros_like(x)
        for k in range(CONV_K):
            s = CONV_K - 1 - k
            dx = dx + w_ref[k:k + 1, :] * _shift_up(dpre, s)
            dw_ref[k:k + 1, :] = jnp.sum(dpre * _shift_down(x, s), axis=0, keepdims=True)
        dproj_ref[...] = dx.astype(dproj_ref.dtype)
        db_ref[...] = jnp.sum(dpre, axis=0, keepdims=True)

    clip = lambda v, hi: jnp.minimum(jnp.maximum(v, 0), hi)
    return pl.pallas_call(
        body, grid=(CONV_DIM // tc,),
        in_specs=[pl.BlockSpec((SEQ, tc), lambda j: (0, blk0 + j)), pl.BlockSpec((CONV_K, tc), lambda j: (0, j)),
                  pl.BlockSpec((1, tc), lambda j: (0, j)),
                  pl.BlockSpec((SEQ, tc), lambda j: (0, clip(j, n_x - 1))),
                  pl.BlockSpec((SEQ, tc), lambda j: (0, clip(j - n_x, n_b - 1))),
                  pl.BlockSpec((SEQ, tc), lambda j: (0, clip(j - n_x - n_b, n_b - 1))),
                  pl.BlockSpec(memory_space=pl.ANY)],
        out_specs=[pl.BlockSpec((SEQ, tc), lambda j: (0, blk0 + j)), pl.BlockSpec((CONV_K, tc), lambda j: (0, j)),
                   pl.BlockSpec((1, tc), lambda j: (0, j))],
        out_shape=[SDS(dproj.shape, dproj.dtype), SDS((CONV_K, CONV_DIM), F32), SDS((1, CONV_DIM), F32)],
        input_output_aliases={6: 0}, compiler_params=_cparams(("parallel",)), name=name,
    )(proj, w, b, dxs, dbm, dcm, dproj)


SSM_PAIRS = SSM_HPG // 2


def _ssd_tile(xp, zp, bm, cm, hp, dtc, dtr, bias, alog, dsk, gnp):
    row = lax.broadcasted_iota(jnp.int32, (CHUNK, CHUNK), 0)
    col = lax.broadcasted_iota(jnp.int32, (CHUNK, CHUNK), 1)
    causal = row >= col
    tri = jnp.where(causal, 1.0, 0.0)
    left = col < SSM_HEAD_DIM
    top = row < SSM_HEAD_DIM
    ones = jnp.ones((CHUNK, CHUNK), BF16)
    cb = _dot_nt(cm, bm)
    dt_c, cs_c, cs_last, m = [], [], [], []
    for r in range(SSM_HPG):
        a = -jnp.exp(alog[r])
        dt_c.append(jax.nn.softplus(dtc[r] + bias[r]))
        da_c = dt_c[r] * a
        da_r = jax.nn.softplus(dtr[r] + bias[r]) * a
        cs_c.append(jnp.sum(tri * da_r, axis=1, keepdims=True))
        cs_r = jnp.sum(jnp.where(row <= col, 1.0, 0.0) * da_c, axis=0, keepdims=True)
        cs_last.append(jnp.sum(da_c, axis=0, keepdims=True))
        m.append(cb * jnp.exp(jnp.where(causal, cs_c[r] - cs_r, -1e30)))
    ygs, hn = [], []
    for p in range(SSM_PAIRS):
        a, b = 2 * p, 2 * p + 1
        pair = lambda u, v: jnp.where(left, u, v)
        xdt = xp[p] * pair(dt_c[a], dt_c[b])
        y = pair(_dot(m[a], xdt), _dot(m[b], xdt))
        y = y + _dot_nt(cm, hp[p]) * pair(jnp.exp(cs_c[a]), jnp.exp(cs_c[b]))
        y = y + xp[p] * pair(dsk[a], dsk[b])
        decay = pair(jnp.exp(cs_last[a] - cs_c[a]), jnp.exp(cs_last[b] - cs_c[b]))
        states = _dot_tn(xdt * decay, bm)
        hn.append(hp[p] * jnp.where(top, jnp.exp(cs_last[a]), jnp.exp(cs_last[b])) + states)
        ygs.append(y * (zp[p] * jax.nn.sigmoid(zp[p])))
    ms = sum(_dot(t * t, ones) for t in ygs) * (1.0 / SSM_GROUP_W)
    rs = lax.rsqrt(ms + EPS)
    return [ygs[p] * rs * gnp[p] for p in range(SSM_PAIRS)], hn


def _ssd_in_specs(cidx):
    gw, n = SSM_GROUP_W, SSM_STATE
    bm_blk = D_INNER // n
    return [
        pl.BlockSpec((CHUNK, gw), lambda g, c: (cidx(c), g)),
        pl.BlockSpec((CHUNK, gw), lambda g, c: (cidx(c), g)),
        pl.BlockSpec((CHUNK, n), lambda g, c: (cidx(c), bm_blk + g)),
        pl.BlockSpec((CHUNK, n), lambda g, c: (cidx(c), bm_blk + SSM_GROUPS + g)),
        pl.BlockSpec((None, CHUNK, SSM_HPG), lambda g, c: (g, cidx(c), 0)),
        pl.BlockSpec((None, SSM_HPG, CHUNK), lambda g, c: (g, 0, cidx(c))),
        pl.BlockSpec((None, 1, SSM_HPG), lambda g, c: (g, 0, 0)),
        pl.BlockSpec((None, 1, SSM_HPG), lambda g, c: (g, 0, 0)),
        pl.BlockSpec((None, 1, SSM_HPG), lambda g, c: (g, 0, 0)),
        pl.BlockSpec((1, gw), lambda g, c: (0, g)),
    ]


def _ssd_args(x_ref, z_ref, bm_ref, cm_ref, hp, dtc_ref, dtr_ref, bias_ref, alog_ref, dsk_ref, gn_ref):
    nh, npair, w = SSM_HPG, SSM_PAIRS, 2 * SSM_HEAD_DIM
    col = lambda ref: [ref[:, r:r + 1] for r in range(nh)]
    return (_split(x_ref, npair, w), _split(z_ref, npair, w), bm_ref[...], cm_ref[...], hp,
            col(dtc_ref), [dtr_ref[r:r + 1, :] for r in range(nh)], col(bias_ref), col(alog_ref), col(dsk_ref),
            _split(gn_ref, npair, w))


def _pair_rows(ref):
    w = 2 * SSM_HEAD_DIM
    return [ref[p * w:(p + 1) * w, :] for p in range(SSM_PAIRS)]


def _ssd_fwd(xbc, proj, dt_c, dt_r, bias, alog, dsk, gn, mixcat, *, name):
    w = 2 * SSM_HEAD_DIM

    def body(x_ref, z_ref, bm_ref, cm_ref, dtc_ref, dtr_ref, bias_ref, alog_ref, dsk_ref, gn_ref, cat_in,
             cat_ref, hprev_ref, h_scr):
        del cat_in

        @pl.when(pl.program_id(1) == 0)
        def _():
            h_scr[...] = jnp.zeros_like(h_scr)

        hprev_ref[...] = h_scr[...]
        yn, hn = _ssd_tile(*_ssd_args(x_ref, z_ref, bm_ref, cm_ref, _pair_rows(h_scr), dtc_ref, dtr_ref, bias_ref,
                                      alog_ref, dsk_ref, gn_ref))
        for p in range(SSM_PAIRS):
            cat_ref[:, p * w:(p + 1) * w] = yn[p].astype(cat_ref.dtype)
            h_scr[p * w:(p + 1) * w, :] = hn[p]

    return pl.pallas_call(
        body, grid=(SSM_GROUPS, N_CHUNKS), in_specs=[*_ssd_in_specs(lambda c: c), pl.BlockSpec(memory_space=pl.ANY)],
        out_specs=[pl.BlockSpec((CHUNK, SSM_GROUP_W), lambda g, c: (c, g)),
                   pl.BlockSpec((None, None, SSM_GROUP_W, SSM_STATE), lambda g, c: (c, g, 0, 0))],
        out_shape=[SDS(mixcat.shape, mixcat.dtype), SDS((N_CHUNKS, SSM_GROUPS, SSM_GROUP_W, SSM_STATE), F32)],
        scratch_shapes=[pltpu.VMEM((SSM_GROUP_W, SSM_STATE), F32)],
        input_output_aliases={10: 0}, compiler_params=_cparams(("parallel", "arbitrary")), name=name,
    )(xbc, proj, xbc, xbc, dt_c, dt_r, bias, alog, dsk, gn, mixcat)


def _ssd_bwd(xbc, proj, dt_c, dt_r, bias, alog, dsk, gn, hprev, dcat, dproj, *, name):
    nh, w, gw, n = SSM_HPG, 2 * SSM_HEAD_DIM, SSM_GROUP_W, SSM_STATE
    rev = lambda c: N_CHUNKS - 1 - c

    def body(x_ref, z_ref, bm_ref, cm_ref, dtc_ref, dtr_ref, bias_ref, alog_ref, dsk_ref, gn_ref, hprev_ref, dy_ref,
             dproj_in, dz_ref, dxs_ref, dbm_ref, dcm_ref, ddtc_ref, ddtr_ref, dbias_ref, dalog_ref, ddsk_ref, dgn_ref,
             dh_scr):
        del dproj_in
        first = pl.program_id(1) == 0

        @pl.when(first)
        def _():
            dh_scr[...] = jnp.zeros_like(dh_scr)
            for ref in (dbias_ref, dalog_ref, ddsk_ref, dgn_ref):
                ref[...] = jnp.zeros_like(ref)

        args = _ssd_args(x_ref, z_ref, bm_ref, cm_ref, _pair_rows(hprev_ref), dtc_ref, dtr_ref, bias_ref, alog_ref,
                         dsk_ref, gn_ref)
        _, vjp = jax.vjp(_ssd_tile, *args)
        dxs, dzs, dbm, dcm, dhs, ddtc, ddtr, dbias, dalog, ddsk, dgn = vjp(
            (_split(dy_ref, SSM_PAIRS, w), _pair_rows(dh_scr)))
        dbm_ref[...] = dbm
        dcm_ref[...] = dcm
        for q in range(SSM_PAIRS):
            dxs_ref[:, q * w:(q + 1) * w] = dxs[q]
            dz_ref[:, q * w:(q + 1) * w] = dzs[q].astype(dz_ref.dtype)
            dh_scr[q * w:(q + 1) * w, :] = dhs[q]
            dgn_ref[:, q * w:(q + 1) * w] += dgn[q]
        for r in range(nh):
            ddtc_ref[:, r:r + 1] = ddtc[r]
            ddtr_ref[r:r + 1, :] = ddtr[r]
            dbias_ref[:, r:r + 1] += dbias[r]
            dalog_ref[:, r:r + 1] += dalog[r]
            ddsk_ref[:, r:r + 1] += ddsk[r]

    par_spec = pl.BlockSpec((None, 1, nh), lambda g, c: (g, 0, 0))
    return pl.pallas_call(
        body, grid=(SSM_GROUPS, N_CHUNKS),
        in_specs=[*_ssd_in_specs(rev),
                  pl.BlockSpec((None, None, gw, n), lambda g, c: (rev(c), g, 0, 0)),
                  pl.BlockSpec((CHUNK, gw), lambda g, c: (rev(c), g)),
                  pl.BlockSpec(memory_space=pl.ANY)],
        out_specs=[pl.BlockSpec((CHUNK, gw), lambda g, c: (rev(c), g)),
                   pl.BlockSpec((CHUNK, gw), lambda g, c: (rev(c), g)),
                   pl.BlockSpec((CHUNK, n), lambda g, c: (rev(c), g)),
                   pl.BlockSpec((CHUNK, n), lambda g, c: (rev(c), g)),
                   pl.BlockSpec((None, CHUNK, nh), lambda g, c: (g, rev(c), 0)),
                   pl.BlockSpec((None, nh, CHUNK), lambda g, c: (g, 0, rev(c))),
                   par_spec, par_spec, par_spec,
                   pl.BlockSpec((1, gw), lambda g, c: (0, g))],
        out_shape=[SDS(dproj.shape, dproj.dtype), SDS((SEQ, D_INNER), F32), SDS((SEQ, SSM_GROUPS * n), F32),
                   SDS((SEQ, SSM_GROUPS * n), F32), SDS((SSM_GROUPS, SEQ, nh), F32), SDS((SSM_GROUPS, nh, SEQ), F32),
                   SDS((SSM_GROUPS, 1, nh), F32), SDS((SSM_GROUPS, 1, nh), F32), SDS((SSM_GROUPS, 1, nh), F32),
                   SDS((1, D_INNER), F32)],
        scratch_shapes=[pltpu.VMEM((gw, n), F32)],
        input_output_aliases={12: 0}, compiler_params=_cparams(("parallel", "arbitrary")), name=name,
    )(xbc, proj, xbc, xbc, dt_c, dt_r, bias, alog, dsk, gn, hprev, dcat, dproj)


def _sum_contributions(chip, parts, landed, *, name):
    _, r, c = parts.shape
    tr = _pick(r, (256, 384, 128))

    def body(chip_ref, own_ref, landed_ref, o_ref):
        del chip_ref
        acc = own_ref[...].astype(F32)
        for s in range(landed_ref.shape[0]):
            acc = acc + landed_ref[s].astype(F32)
        o_ref[...] = acc

    grid_spec = pltpu.PrefetchScalarGridSpec(
        num_scalar_prefetch=1, grid=(r // tr,),
        in_specs=[pl.BlockSpec((None, tr, c), lambda i, chip_ref: (chip_ref[0], i, 0)),
                  pl.BlockSpec((landed.shape[0], tr, c), lambda i, chip_ref: (0, i, 0))],
        out_specs=pl.BlockSpec((tr, c), lambda i, chip_ref: (i, 0)))
    return pl.pallas_call(body, grid_spec=grid_spec, out_shape=SDS((r, c), F32),
                          compiler_params=_cparams(("parallel",)), name=name)(chip, parts, landed)


def _adamw(w, g, m, v, *, name):
    r, c = w.shape
    tr = r if r <= 256 else _pick(r, (256, 128, 8))
    spec = pl.BlockSpec((tr, c), lambda i: (i, 0))

    def body(w_ref, g_ref, m_ref, v_ref, d_ref, mo_ref, vo_ref):
        g = g_ref[...]
        m_new = ADAM_B1 * m_ref[...] + (1.0 - ADAM_B1) * g
        v_new = ADAM_B2 * v_ref[...] + (1.0 - ADAM_B2) * (g * g)
        m_hat = m_new / (1.0 - ADAM_B1 ** ADAM_STEP)
        v_hat = v_new / (1.0 - ADAM_B2 ** ADAM_STEP)
        d_ref[...] = -ADAM_LR * (m_hat / (jnp.sqrt(v_hat) + ADAM_EPS) + ADAM_WD * w_ref[...])
        mo_ref[...] = m_new
        vo_ref[...] = v_new

    return pl.pallas_call(body, grid=(r // tr,), in_specs=[spec] * 4, out_specs=[spec] * 3,
                          out_shape=[SDS((r, c), F32)] * 3, compiler_params=_cparams(("parallel",)), name=name)(w, g, m, v)


ANY = pl.BlockSpec(memory_space=pl.ANY)


def _place():
    x, y, c = lax.axis_index("x"), lax.axis_index("y"), lax.axis_index("c")
    chips = [(1 - x, y), (x, 1 - y), (1 - x, 1 - y)]
    return x, y, c, chips


def _remote(src, dst, send_sem, recv_sem, to):
    return pltpu.make_async_remote_copy(src_ref=src, dst_ref=dst, send_sem=send_sem, recv_sem=recv_sem,
                                        device_id=to, device_id_type=MESH)


STREAM_ROWS = 128


def _stream_rows(i):
    return pl.ds(pl.multiple_of(i * STREAM_ROWS, STREAM_ROWS), STREAM_ROWS)


def _channel_scratch(width, dtype):
    buf = (2, STREAM_ROWS, width)
    return [pltpu.VMEM(buf, dtype), pltpu.VMEM(buf, dtype), *([pltpu.SemaphoreType.DMA((2,))] * 5),
            pltpu.SemaphoreType.REGULAR((2,))]


CHANNEL_REFS = 8


def _copy_through_vmem(src, dst, ch):
    sbuf, _, ld, _, _, st, _, _ = ch
    steps = src.shape[0] // STREAM_ROWS
    assert steps >= 2 and steps * STREAM_ROWS == src.shape[0]

    def load(i, slot):
        return pltpu.make_async_copy(src.at[_stream_rows(i)], sbuf.at[slot], ld.at[slot])

    def store(i, slot):
        return pltpu.make_async_copy(sbuf.at[slot], dst.at[_stream_rows(i)], st.at[slot])

    load(0, 0).start()

    def step(i, carry):
        slot = lax.rem(i, 2)
        nxt = 1 - slot

        @pl.when(i + 1 < steps)
        def _():
            @pl.when(i >= 1)
            def _():
                store(0, nxt).wait()
            load(i + 1, nxt).start()

        load(i, slot).wait()
        store(i, slot).start()
        return carry

    lax.fori_loop(0, steps, step, 0)
    for slot in range(2):
        store(0, slot).wait()


def _exchange_stream(src, dst, keep, ch, sibling):
    sbuf, rbuf, ld, snd, rcv, st, kp, credit = ch
    steps = src.shape[0] // STREAM_ROWS
    assert steps >= 2 and steps * STREAM_ROWS == src.shape[0]

    def load(i, slot):
        return pltpu.make_async_copy(src.at[_stream_rows(i)], sbuf.at[slot], ld.at[slot])

    def push(slot):
        return _remote(sbuf.at[slot], rbuf.at[slot], snd.at[slot], rcv.at[slot], sibling)

    def store(i, slot):
        return pltpu.make_async_copy(rbuf.at[slot], dst.at[_stream_rows(i)], st.at[slot])

    def save(i, slot):
        return pltpu.make_async_copy(sbuf.at[slot], keep.at[_stream_rows(i)], kp.at[slot])

    for slot in range(2):
        pl.semaphore_signal(credit.at[slot], 1, device_id=sibling, device_id_type=MESH)
    load(0, 0).start()

    def step(i, carry):
        slot = lax.rem(i, 2)
        nxt = 1 - slot

        @pl.when(i + 1 < steps)
        def _():
            @pl.when(i >= 1)
            def _():
                push(nxt).wait_send()
                if keep is not None:
                    save(0, nxt).wait()
            load(i + 1, nxt).start()

        load(i, slot).wait()
        pl.semaphore_wait(credit.at[slot], 1)
        push(slot).start()
        if keep is not None:
            save(i, slot).start()
        push(slot).wait_recv()
        store(i, slot).start()

        @pl.when(i >= 1)
        def _():
            store(0, nxt).wait()

            @pl.when(i + 1 < steps)
            def _():
                pl.semaphore_signal(credit.at[nxt], 1, device_id=sibling, device_id_type=MESH)
        return carry

    lax.fori_loop(0, steps, step, 0)
    store(0, (steps - 1) % 2).wait()
    for slot in range(2):
        push(slot).wait_send()
        if keep is not None:
            save(0, slot).wait()


def _all_gather_shards(shards, small, *, name):
    n = len(shards)

    def body(*refs):
        ins, outs = refs[:n + 1], refs[n + 1:2 * n + 2]
        scr = refs[2 * n + 2:]
        chans = [scr[CHANNEL_REFS * t:CHANNEL_REFS * (t + 1)] for t in range(n)]
        send_sems, recv_sems, small_sems = scr[CHANNEL_REFS * n:]
        x, y, c, _ = _place()
        me = 2 * x + y
        sibling = (x, y, 1 - c)
        near = (lax.rem(x + 1 - c, 2), lax.rem(y + c, 2))
        far = (lax.rem(x + c, 2), lax.rem(y + 1 - c, 2))
        k_near, k_far, k_diag = 2 * near[0] + near[1], 2 * far[0] + far[1], 3 - me
        targets = ((*near, c), (*far, c), (*far, c))
        arrives = (k_near, k_far, k_diag)
        streams_in = (k_far, k_near, k_diag)

        def ici(t, j, src, blk):
            return _remote(src, outs[t].at[blk, c], send_sems.at[3 * t + j], recv_sems.at[3 * t + j], targets[j])

        first = [ici(t, j, ins[t].at[c], me) for t in range(n + 1) for j in range(2)]
        for cp in first:
            cp.start()
        small_local = pltpu.make_async_copy(ins[n], outs[n].at[me], small_sems.at[6])
        small_local.start()
        for t in range(n):
            for h in range(2):
                _copy_through_vmem(ins[t].at[h], outs[t].at[me, h], chans[t])
        passed = []
        for j in range(3):
            for t in range(n + 1):
                landed = outs[t].at[arrives[j], c]
                ici(t, j, landed, arrives[j]).wait_recv()
                if j == 0:
                    fwd = ici(t, 2, landed, k_near)
                    fwd.start()
                    passed.append(fwd)
                if t < n:
                    _exchange_stream(landed, outs[t].at[streams_in[j], 1 - c], None, chans[t], sibling)
                else:
                    fwd = _remote(landed, landed, small_sems.at[j], small_sems.at[3 + j], sibling)
                    fwd.start()
                    passed.append(fwd)
        for j in range(3):
            got = outs[n].at[streams_in[j], 1 - c]
            _remote(got, got, small_sems.at[j], small_sems.at[3 + j], sibling).wait_recv()
        for cp in first + passed:
            cp.wait_send()
        small_local.wait()

    scratch = []
    for s in shards:
        scratch += _channel_scratch(s.shape[2], s.dtype)
    return pl.pallas_call(
        body, in_specs=[ANY] * (n + 1), out_specs=[ANY] * (n + 1),
        out_shape=[SDS((N_CHIPS, *s.shape), s.dtype) for s in (*shards, small)],
        scratch_shapes=[*scratch, pltpu.SemaphoreType.DMA((3 * n + 3,)), pltpu.SemaphoreType.DMA((3 * n + 3,)),
                        pltpu.SemaphoreType.DMA((7,))],
        compiler_params=pltpu.CompilerParams(vmem_limit_bytes=VMEM_LIMIT), name=name)(*shards, small)


def _pair_reduce(stacks, *, name):
    n = len(stacks)
    per = 10

    def body(*refs):
        ins, outs, scr = refs[:n], refs[n:2 * n], refs[2 * n:]
        x, y, c, _ = _place()
        sibling = (x, y, 1 - c)
        for t in range(n):
            sbuf, rbuf, obuf, pbuf, ld_s, ld_o, snd, rcv, st, credit = scr[per * t:per * (t + 1)]
            steps = ins[t].shape[1] // STREAM_ROWS
            src, own, out = ins[t].at[1 - c], ins[t].at[c], outs[t]

            def load_s(i, slot, src=src, sbuf=sbuf, ld_s=ld_s):
                return pltpu.make_async_copy(src.at[_stream_rows(i)], sbuf.at[slot], ld_s.at[slot])

            def load_o(i, slot, own=own, obuf=obuf, ld_o=ld_o):
                return pltpu.make_async_copy(own.at[_stream_rows(i)], obuf.at[slot], ld_o.at[slot])

            def push(slot, sbuf=sbuf, rbuf=rbuf, snd=snd, rcv=rcv):
                return _remote(sbuf.at[slot], rbuf.at[slot], snd.at[slot], rcv.at[slot], sibling)

            def store(i, slot, pbuf=pbuf, out=out, st=st):
                return pltpu.make_async_copy(pbuf.at[slot], out.at[_stream_rows(i)], st.at[slot])

            assert steps >= 2
            for slot in range(2):
                pl.semaphore_signal(credit.at[slot], 1, device_id=sibling, device_id_type=MESH)
                load_s(slot, slot).start()
                load_o(slot, slot).start()
            load_s(0, 0).wait()
            pl.semaphore_wait(credit.at[0], 1)
            push(0).start()

            def step(i, carry, load_s=load_s, load_o=load_o, push=push, store=store, rbuf=rbuf, obuf=obuf, pbuf=pbuf,
                     credit=credit, steps=steps):
                slot = lax.rem(i, 2)
                nxt = 1 - slot

                @pl.when(i + 1 < steps)
                def _():
                    load_s(i + 1, nxt).wait()
                    pl.semaphore_wait(credit.at[nxt], 1)
                    push(nxt).start()

                load_o(i, slot).wait()
                push(slot).wait_recv()

                @pl.when(i >= 2)
                def _():
                    store(i, slot).wait()

                pbuf[slot] = (obuf[slot] + rbuf[slot]).astype(pbuf.dtype)
                store(i, slot).start()
                push(slot).wait_send()

                @pl.when(i + 2 < steps)
                def _():
                    load_s(i + 2, slot).start()
                    load_o(i + 2, slot).start()
                    pl.semaphore_signal(credit.at[slot], 1, device_id=sibling, device_id_type=MESH)
                return carry

            lax.fori_loop(0, steps, step, 0)
            for slot in range(2):
                store(0, slot).wait()

    scratch = []
    for s in stacks:
        buf = (2, STREAM_ROWS, s.shape[2])
        scratch += [pltpu.VMEM(buf, F32), pltpu.VMEM(buf, F32), pltpu.VMEM(buf, F32), pltpu.VMEM(buf, BF16),
                    *([pltpu.SemaphoreType.DMA((2,))] * 5), pltpu.SemaphoreType.REGULAR((2,))]
    return pl.pallas_call(
        body, in_specs=[ANY] * n, out_specs=[ANY] * n, out_shape=[SDS(s.shape[1:], BF16) for s in stacks],
        scratch_shapes=scratch, compiler_params=pltpu.CompilerParams(vmem_limit_bytes=VMEM_LIMIT), name=name)(*stacks)


def _chip_scatter(parts, *, name):
    n = len(parts)

    def body(*refs):
        ins, outs = refs[:n], refs[n:2 * n]
        send_sems, recv_sems = refs[2 * n:]
        _, _, c, chips = _place()
        copies = [_remote(ins[t].at[2 * cx + cy], outs[t].at[j], send_sems.at[3 * t + j], recv_sems.at[3 * t + j],
                          (cx, cy, c)) for t in range(n) for j, (cx, cy) in enumerate(chips)]
        for cp in copies:
            cp.start()
        for cp in copies:
            cp.wait_recv()
        for cp in copies:
            cp.wait_send()

    return pl.pallas_call(
        body, in_specs=[ANY] * n, out_specs=[ANY] * n, out_shape=[SDS((3, *p.shape[1:]), p.dtype) for p in parts],
        scratch_shapes=[pltpu.SemaphoreType.DMA((3 * n,)), pltpu.SemaphoreType.DMA((3 * n,))], name=name)(*parts)


def _pair_share(groups, *, name):
    finals = [f for grp in groups for f in grp]
    n, n_out = len(finals), len(groups)

    def body(*refs):
        ins, outs, scr = refs[:n], refs[n:n + n_out], refs[n + n_out:]
        x, y, c, _ = _place()
        sibling = (x, y, 1 - c)
        t = 0
        for o, grp in enumerate(groups):
            for layer in range(len(grp)):
                _exchange_stream(ins[t], outs[o].at[layer, 1 - c], outs[o].at[layer, c],
                                 scr[CHANNEL_REFS * t:CHANNEL_REFS * (t + 1)], sibling)
                t += 1

    scratch = []
    for f in finals:
        scratch += _channel_scratch(f.shape[1], f.dtype)
    return pl.pallas_call(
        body, in_specs=[ANY] * n, out_specs=[ANY] * n_out,
        out_shape=[SDS((len(grp), 2, *grp[0].shape), grp[0].dtype) for grp in groups],
        scratch_shapes=scratch, compiler_params=pltpu.CompilerParams(vmem_limit_bytes=VMEM_LIMIT), name=name)(*finals)


def _all_reduce_small(v, *, name):
    rows, lanes = v.shape
    n_dev = 8

    def body(v_ref, o_ref, all_ref, send_sems, recv_sems, local_sem):
        x, y, c, chips = _place()
        me, sibling = (x, y, c), (x, y, 1 - c)

        def block(px, py, pc):
            return all_ref.at[4 * px + 2 * py + pc]

        def copy(k, blk, to, src=None):
            return _remote(block(*blk) if src is None else src, block(*blk), send_sems.at[k], recv_sems.at[k], to)

        mine = pltpu.make_async_copy(v_ref, block(*me), local_sem)
        mine.start()
        first = [copy(0, me, sibling, src=v_ref)]
        first += [copy(1 + j, me, (*chip, c), src=v_ref) for j, chip in enumerate(chips)]
        for cp in first:
            cp.start()
        passed = [copy(4 + j, (*chip, c), sibling) for j, chip in enumerate(chips)]
        for j, chip in enumerate(chips):
            copy(1 + j, (*chip, c), me).wait_recv()
            passed[j].start()
        copy(0, sibling, me).wait_recv()
        for j, chip in enumerate(chips):
            copy(4 + j, (*chip, 1 - c), me).wait_recv()
        for cp in first + passed:
            cp.wait_send()
        mine.wait()
        acc = all_ref[0]
        for k in range(1, n_dev):
            acc = acc + all_ref[k]
        o_ref[...] = acc

    vmem = pl.BlockSpec(memory_space=pltpu.VMEM)
    return pl.pallas_call(
        body, in_specs=[vmem], out_specs=vmem, out_shape=SDS((rows, lanes), F32),
        scratch_shapes=[pltpu.VMEM((n_dev, rows, lanes), F32), pltpu.SemaphoreType.DMA((7,)),
                        pltpu.SemaphoreType.DMA((7,)), pltpu.SemaphoreType.DMA],
        compiler_params=pltpu.CompilerParams(vmem_limit_bytes=VMEM_LIMIT), name=name)(v)


def _relu2_epilogue(acc):
    return acc, jnp.square(jnp.maximum(acc, 0.0))


def _res_epilogue(acc, res):
    return (acc + res,)


def _drelu2_epilogue(acc, pre):
    return (acc * (2.0 * jnp.maximum(pre.astype(F32), 0.0)),)


def _ffn_fwd(h, g, w1, w2, tag):
    f = _rms_fwd(h, g, name=f"ffn_norm_{tag}")
    pre, act = _mm_nn(f, w1, name=f"ffn1_{tag}", epilogue=_relu2_epilogue, n_out_dtypes=(BF16, BF16))
    h_out = _mm_nn(act, w2, name=f"ffn2_{tag}", extras=(h,), epilogue=_res_epilogue)
    return h_out, (f, pre, act)


def _ffn_bwd(dh, h, g, w1, w2, saved, layer, after=()):
    f, pre, act = saved
    dpre = _mm_nt(dh, w2, name=f"ffn2_dx_{layer}", out_dtype=BF16, extras=(pre,), epilogue=_drelu2_epilogue,
                  after=after)
    dw2 = _mm_tn_stacked(act, dh, name=f"ffn2_dw_{layer}", col_slots=False)
    df = _mm_nt(dpre, w1, name=f"ffn1_dx_{layer}")
    dw1 = _mm_tn_stacked(f, dpre, name=f"ffn1_dw_{layer}", col_slots=True)
    dh, dg = _rms_bwd(h, g, df, dh, name=f"ffn_norm_bwd_{layer}")
    return dh, dg, dw1, dw2


def _kv_fwd(mem, g, w_kv, tag):
    m = _rms_fwd(mem, g, name=f"mem_norm_{tag}")
    return m, _mm_nn(m, w_kv, name=f"kv_{tag}")


def _kv_bwd(mem, g, w_kv, m, dk, dv, layer):
    dkv = jnp.concatenate([dk, dv], axis=1)
    dw = _mm_tn_stacked(m, dkv, name=f"kv_dw_{layer}", col_slots=True)
    dm = _mm_nt(dkv, w_kv, name=f"kv_dx_{layer}")
    _, dg = _rms_bwd(mem, g, dm, dm, name=f"mem_norm_bwd_{layer}")
    return dw, dg


def _local_step(x, mem, target, p, after_layer1=None):
    row = lambda v: v.reshape(1, -1)
    g = {}

    h0 = x
    a0 = _rms_fwd(h0, row(p["norm_mix"][0]), name="mix_norm_0")
    proj_a = _mm_nn(a0, p["a_in"], name="a_in")
    m0, kv0 = _kv_fwd(mem, row(p["mem_norm"][0]), p["w_kv"][0], "0")
    cat0 = _attn_fwd(proj_a, 2 * D_INNER, kv0, name="attn_0")
    bs_col = p["a_bs"].reshape(A_GROUPS, CHUNK, 1)
    cat0 = _gate_fwd(proj_a, p["a_ln_g"], p["a_ln_b"], p["a_ws"], bs_col, cat0, name="gate")
    h1 = _mm_nn(cat0, p["w_out"][0], name="out_0", extras=(h0,), epilogue=_res_epilogue)
    h2, ffn0 = _ffn_fwd(h1, row(p["norm_ffn"][0]), p["w_ffn1"][0], p["w_ffn2"][0], "0")

    a1 = _rms_fwd(h2, row(p["norm_mix"][1]), name="mix_norm_1")
    proj_b = _mm_nn(a1, p["b_in"], name="b_in")
    m1, kv1 = _kv_fwd(mem, row(p["mem_norm"][1]), p["w_kv"][1], "1")
    cat1 = _attn_fwd(proj_b, B_Q_OFF, kv1, name="attn_1")
    xbc = _conv_fwd(proj_b, p["b_conv_w"], p["b_conv_b"], name="conv")
    dt_raw = proj_b[:, B_DT_OFF:B_DT_OFF + SSM_HEADS].reshape(SEQ, SSM_GROUPS, SSM_HPG)
    dt_c = jnp.transpose(dt_raw, (1, 0, 2))
    dt_r = jnp.transpose(dt_raw, (1, 2, 0))
    per_head = lambda v: v.reshape(SSM_GROUPS, 1, SSM_HPG)
    ssd_par = (per_head(p["b_dt_bias"]), per_head(p["b_a_log"]), per_head(p["b_d"]), p["b_gnorm"])
    cat1, hprev = _ssd_fwd(xbc, proj_b, dt_c, dt_r, *ssd_par, cat1, name="ssd")
    h3 = _mm_nn(cat1, p["w_out"][1], name="out_1", extras=(h2,), epilogue=_res_epilogue)
    h4, ffn1 = _ffn_fwd(h3, row(p["norm_ffn"][1]), p["w_ffn1"][1], p["w_ffn2"][1], "1")

    loss, dh, g["final_norm"] = _loss_head(h4, row(p["final_norm"]), target, name="loss_head")

    dh, dnf1, dw1_1, dw2_1 = _ffn_bwd(dh, h3, row(p["norm_ffn"][1]), p["w_ffn1"][1], p["w_ffn2"][1], ffn1, 1)
    dcat1 = _mm_nt(dh, p["w_out"][1], name="out_dx_1")
    dwo_1 = _mm_tn_stacked(cat1, dh, name="out_dw_1", col_slots=False)
    dproj_b, dk1, dv1 = _attn_bwd(proj_b, B_Q_OFF, kv1, dcat1, B_IN_PAD, B_Q_OFF, name="attn_bwd_1")
    (dproj_b, dxs, dbm, dcm, ddt_c, ddt_r, g["b_dt_bias"], g["b_a_log"], g["b_d"], g["b_gnorm"]) = _ssd_bwd(
        xbc, proj_b, dt_c, dt_r, *ssd_par, hprev, dcat1, dproj_b, name="ssd_bwd")
    dproj_b, g["b_conv_w"], g["b_conv_b"] = _conv_bwd(proj_b, p["b_conv_w"], p["b_conv_b"], dxs, dbm, dcm, dproj_b,
                                                      name="conv_bwd")
    ddt = jnp.transpose(ddt_c, (1, 0, 2)) + jnp.transpose(ddt_r, (2, 0, 1))
    ddt = jnp.pad(ddt.reshape(SEQ, SSM_HEADS), ((0, 0), (0, B_IN_PAD - B_DT_OFF - SSM_HEADS))).astype(BF16)
    dproj_b = lax.dynamic_update_slice(dproj_b, ddt, (0, B_DT_OFF))
    dwkv_1, dmn1 = _kv_bwd(mem, row(p["mem_norm"][1]), p["w_kv"][1], m1, dk1, dv1, 1)
    dwb = _b_in_grad_slots(_mm_tn(a1, dproj_b, name="b_in_dw"))
    da1 = _mm_nt(dproj_b, p["b_in"], name="b_in_dx")
    dh, dnm1 = _rms_bwd(h2, row(p["norm_mix"][1]), da1, dh, name="mix_norm_bwd_1")
    layer1 = dict(w_kv=dwkv_1, w_out=dwo_1, w_ffn1=dw1_1, w_ffn2=dw2_1, b_in=dwb)
    token = () if after_layer1 is None else (after_layer1(layer1),)

    dh, dnf0, dw1_0, dw2_0 = _ffn_bwd(dh, h1, row(p["norm_ffn"][0]), p["w_ffn1"][0], p["w_ffn2"][0], ffn0, 0,
                                      after=token)
    dcat0 = _mm_nt(dh, p["w_out"][0], name="out_dx_0")
    dwo_0 = _mm_tn_stacked(cat0, dh, name="out_dw_0", col_slots=False)
    dproj_a, dk0, dv0 = _attn_bwd(proj_a, 2 * D_INNER, kv0, dcat0, A_IN, 2 * D_INNER, name="attn_bwd_0")
    dproj_a, g["a_ln_g"], g["a_ln_b"], g["a_ws"], dbs_col = _gate_bwd(
        proj_a, p["a_ln_g"], p["a_ln_b"], p["a_ws"], bs_col, dcat0, dproj_a, name="gate_bwd")
    g["a_bs"] = dbs_col.reshape(A_GROUPS, CHUNK)
    dwkv_0, dmn0 = _kv_bwd(mem, row(p["mem_norm"][0]), p["w_kv"][0], m0, dk0, dv0, 0)
    dwa = _mm_tn_stacked(a0, dproj_a, name="a_in_dw", col_slots=True)
    da0 = _mm_nt(dproj_a, p["a_in"], name="a_in_dx")
    dx, dnm0 = _rms_bwd(h0, row(p["norm_mix"][0]), da0, dh, name="mix_norm_bwd_0")

    g["norm_mix"] = jnp.concatenate([dnm0, dnm1], axis=0)
    g["norm_ffn"] = jnp.concatenate([dnf0, dnf1], axis=0)
    g["mem_norm"] = jnp.concatenate([dmn0, dmn1], axis=0)
    layer0 = dict(w_kv=dwkv_0, w_out=dwo_0, w_ffn1=dw1_0, w_ffn2=dw2_0, a_in=dwa)
    return loss, dx, g, layer0, layer1


def _b_in_full(gathered):
    full = jnp.transpose(gathered, (1, 0, 2)).reshape(D_MODEL, B_IN)
    dt0 = D_INNER + CONV_DIM
    return jnp.concatenate([full[:, :dt0], full[:, dt0 + SSM_HEADS:], full[:, dt0:dt0 + SSM_HEADS],
                            jnp.zeros((D_MODEL, B_IN_PAD - B_IN), full.dtype)], axis=1)


def _b_in_grad_slots(d):
    dt0 = D_INNER + CONV_DIM
    full = jnp.concatenate([d[:, :dt0], d[:, B_DT_OFF:B_DT_OFF + SSM_HEADS], d[:, dt0:B_DT_OFF]], axis=1)
    return jnp.transpose(full.reshape(2, D_MODEL // 2, N_CHIPS, B_IN // N_CHIPS), (0, 2, 1, 3))


LARGE = ("w_kv", "w_out", "w_ffn1", "w_ffn2", "a_in", "b_in")
SMALL_REPL = ("norm_mix", "norm_ffn", "mem_norm", "a_ln_g", "a_ln_b", "a_ws", "a_bs", "b_dt_bias", "b_a_log", "b_d",
              "final_norm")
SMALL_SHARD = ("b_conv_w", "b_conv_b", "b_gnorm")
WEIGHTS = ("norm_mix", "norm_ffn", "mem_norm", "w_kv", "w_out", "w_ffn1", "w_ffn2", "a_in", "a_ln_g", "a_ln_b", "a_ws",
           "a_bs", "b_in", "b_conv_w", "b_conv_b", "b_dt_bias", "b_a_log", "b_d", "b_gnorm", "final_norm")
CONV_SHARD = CONV_DIM // N_CHIPS
GN_SHARD = D_INNER // N_CHIPS


LAYERED = ("w_kv", "w_out", "w_ffn1", "w_ffn2")
LAYER_TENSORS = (("w_kv", "w_out", "w_ffn1", "w_ffn2", "a_in"), ("w_kv", "w_out", "w_ffn1", "w_ffn2", "b_in"))


def _gather_weights(w):
    halves = lambda a: a.reshape(2, a.shape[0] // 2, a.shape[1]).astype(BF16)
    big = [halves(w[k][layer] if k in LAYERED else w[k][0]) for layer in range(2) for k in LAYER_TENSORS[layer]]
    small = jnp.zeros((2, CONV_K, CONV_SHARD), F32)
    small = small.at[0].set(w["b_conv_w"][0])
    small = small.at[1, 0].set(w["b_conv_b"][0])
    small = small.at[1, 1, :GN_SHARD].set(w["b_gnorm"][0])
    gathered = _all_gather_shards(big, small, name="gather_weights")
    got = [dict(zip(LAYER_TENSORS[layer], gathered[5 * layer:5 * layer + 5])) for layer in range(2)]
    slots = lambda a: a.reshape(N_CHIPS, -1, a.shape[-1])
    rows = lambda a: a.reshape(-1, a.shape[-1])
    p = {}
    p["w_kv"] = [slots(got[layer]["w_kv"]) for layer in range(2)]
    p["w_out"] = [rows(got[layer]["w_out"]) for layer in range(2)]
    p["w_ffn1"] = [slots(got[layer]["w_ffn1"]) for layer in range(2)]
    p["w_ffn2"] = [rows(got[layer]["w_ffn2"]) for layer in range(2)]
    p["a_in"] = slots(got[0]["a_in"])
    p["b_in"] = _b_in_full(slots(got[1]["b_in"]))
    sm = gathered[-1]
    p["b_conv_w"] = jnp.transpose(sm[:, 0], (1, 0, 2)).reshape(CONV_K, CONV_DIM)
    p["b_conv_b"] = sm[:, 1, 0].reshape(1, CONV_DIM)
    p["b_gnorm"] = sm[:, 1, 1, :GN_SHARD].reshape(1, D_INNER)
    return p


def _reduce_layer(grads, chip, layer):
    names = LAYER_TENSORS[layer]
    stacks = [grads[k].reshape(2, -1, grads[k].shape[-1]) for k in names]
    parts = _pair_reduce(stacks, name=f"grads_pair_reduce_{layer}")
    parts = [t.reshape(N_CHIPS, -1, t.shape[-1]) for t in parts]
    landed = _chip_scatter(parts, name=f"grads_chip_scatter_{layer}")
    return {k: _sum_contributions(chip, t, u, name=f"grads_chip_sum_{k}_{layer}")
            for k, t, u in zip(names, parts, landed)}


def _small_layout(shapes):
    offs, o = {}, 0
    for k in (*SMALL_REPL, *SMALL_SHARD):
        size = math.prod(shapes[k])
        offs[k] = (o, size)
        o += size
    rows = -(-o // (8 * 128)) * 8
    return offs, rows


def _reduce_small(g, full_shapes):
    offs, rows = _small_layout(full_shapes)
    flat = jnp.concatenate([g[k].reshape(-1) for k in (*SMALL_REPL, *SMALL_SHARD)])
    flat = jnp.pad(flat, (0, rows * 128 - flat.shape[0])).reshape(rows, 128)
    total = _all_reduce_small(flat, name="grads_small_all_reduce").reshape(-1)
    return {k: total[o:o + n].reshape(full_shapes[k]) for k, (o, n) in offs.items()}


def kernel(x, mem, norm_mix, norm_ffn, mem_norm, w_kv, w_out, w_ffn1, w_ffn2, a_in, a_ln_g, a_ln_b, a_ws, a_bs, b_in, b_conv_w, b_conv_b, b_dt_bias, b_a_log, b_d, b_gnorm, final_norm, loss_target, m_norm_mix, m_norm_ffn, m_mem_norm, m_w_kv, m_w_out, m_w_ffn1, m_w_ffn2, m_a_in, m_a_ln_g, m_a_ln_b, m_a_ws, m_a_bs, m_b_in, m_b_conv_w, m_b_conv_b, m_b_dt_bias, m_b_a_log, m_b_d, m_b_gnorm, m_final_norm, v_norm_mix, v_norm_ffn, v_mem_norm, v_w_kv, v_w_out, v_w_ffn1, v_w_ffn2, v_a_in, v_a_ln_g, v_a_ln_b, v_a_ws, v_a_bs, v_b_in, v_b_conv_w, v_b_conv_b, v_b_dt_bias, v_b_a_log, v_b_d, v_b_gnorm, v_final_norm):
    w = dict(norm_mix=norm_mix, norm_ffn=norm_ffn, mem_norm=mem_norm, w_kv=w_kv, w_out=w_out, w_ffn1=w_ffn1,
             w_ffn2=w_ffn2, a_in=a_in, a_ln_g=a_ln_g, a_ln_b=a_ln_b, a_ws=a_ws, a_bs=a_bs, b_in=b_in, b_conv_w=b_conv_w,
             b_conv_b=b_conv_b, b_dt_bias=b_dt_bias, b_a_log=b_a_log, b_d=b_d, b_gnorm=b_gnorm, final_norm=final_norm)
    mom = dict(norm_mix=m_norm_mix, norm_ffn=m_norm_ffn, mem_norm=m_mem_norm, w_kv=m_w_kv, w_out=m_w_out,
               w_ffn1=m_w_ffn1, w_ffn2=m_w_ffn2, a_in=m_a_in, a_ln_g=m_a_ln_g, a_ln_b=m_a_ln_b, a_ws=m_a_ws,
               a_bs=m_a_bs, b_in=m_b_in, b_conv_w=m_b_conv_w, b_conv_b=m_b_conv_b, b_dt_bias=m_b_dt_bias,
               b_a_log=m_b_a_log, b_d=m_b_d, b_gnorm=m_b_gnorm, final_norm=m_final_norm)
    var = dict(norm_mix=v_norm_mix, norm_ffn=v_norm_ffn, mem_norm=v_mem_norm, w_kv=v_w_kv, w_out=v_w_out,
               w_ffn1=v_w_ffn1, w_ffn2=v_w_ffn2, a_in=v_a_in, a_ln_g=v_a_ln_g, a_ln_b=v_a_ln_b, a_ws=v_a_ws,
               a_bs=v_a_bs, b_in=v_b_in, b_conv_w=v_b_conv_w, b_conv_b=v_b_conv_b, b_dt_bias=v_b_dt_bias,
               b_a_log=v_b_a_log, b_d=v_b_d, b_gnorm=v_b_gnorm, final_norm=v_final_norm)

    p = _gather_weights(w)
    p.update(norm_mix=norm_mix, norm_ffn=norm_ffn, mem_norm=mem_norm, a_ln_g=a_ln_g, a_ln_b=a_ln_b, a_ws=a_ws[0],
             a_bs=a_bs[0], b_dt_bias=b_dt_bias, b_a_log=b_a_log, b_d=b_d, final_norm=final_norm)
    chip = 2 * lax.axis_index("x") + lax.axis_index("y")
    chip_arr = jnp.reshape(chip, (1,)).astype(jnp.int32)
    loss_part, dx, g, layer0, layer1 = _local_step(x[0], mem[0], loss_target[0], p)
    loss = lax.psum(loss_part[0, 0], ("x", "y", "c"))

    full_shapes = {k: w[k].shape for k in SMALL_REPL}
    full_shapes.update(b_conv_w=(1, CONV_K, CONV_DIM), b_conv_b=(1, CONV_DIM), b_gnorm=(1, D_INNER))
    gs = _reduce_small(g, full_shapes)
    gs["b_conv_w"] = lax.dynamic_slice_in_dim(gs["b_conv_w"], chip * CONV_SHARD, CONV_SHARD, axis=2)
    gs["b_conv_b"] = lax.dynamic_slice_in_dim(gs["b_conv_b"], chip * CONV_SHARD, CONV_SHARD, axis=1)
    gs["b_gnorm"] = lax.dynamic_slice_in_dim(gs["b_gnorm"], chip * GN_SHARD, GN_SHARD, axis=1)
    halves = [_reduce_layer(layer0, chip_arr, 0), _reduce_layer(layer1, chip_arr, 1)]
    groups = [[halves[layer][k] for layer in range(2) if k in halves[layer]] for k in LARGE]
    gl = dict(zip(LARGE, _pair_share(groups, name="grads_pair_share")))
    grads = {k: (gl[k].reshape(w[k].shape) if k in gl else gs[k]) for k in WEIGHTS}

    delta, new_m, new_v = {}, {}, {}
    for k in WEIGHTS:
        shape = w[k].shape
        flat = (lambda a: a.reshape(-1, shape[-1])) if len(shape) > 1 else (lambda a: a.reshape(1, -1))
        d, m_new, v_new = _adamw(flat(w[k]), flat(grads[k]), flat(mom[k]), flat(var[k]), name=f"adamw_{k}")
        delta[k], new_m[k], new_v[k] = d.reshape(shape), m_new.reshape(shape), v_new.reshape(shape)

    return (loss, dx.reshape(x.shape), *[grads[k] for k in WEIGHTS], *[delta[k] for k in WEIGHTS],
            *[new_m[k] for k in WEIGHTS], *[new_v[k] for k in WEIGHTS])
```

```python
import math

import jax
import jax.numpy as jnp
from jax import lax
from jax.experimental import pallas as pl
from jax.experimental.pallas import tpu as pltpu

F32 = jnp.float32
BF16 = jnp.bfloat16
SDS = jax.ShapeDtypeStruct

D_MODEL = 1024
SEQ = 2048
CHUNK = 128
N_MEM = 256
D_INNER = 2048
A_GROUPS = 8
A_GROUP_W = D_INNER // A_GROUPS
SSM_HEADS = 32
SSM_HEAD_DIM = 64
SSM_GROUPS = 4
SSM_HPG = 8
SSM_STATE = 128
SSM_GROUP_W = SSM_HPG * SSM_HEAD_DIM
CONV_K = 4
CONV_DIM = 3072
X_HEADS = 4
X_HEAD_DIM = 256
X_WIDTH = 1024
MIX_OUT = 3072
D_FF = 4096
A_IN = 5120
B_IN = 6176
B_IN_PAD = 6272
B_Q_OFF = 5120
B_DT_OFF = 6144
N_CHUNKS = SEQ // CHUNK
EPS = 1e-6
N_CHIPS = 4

ADAM_LR = 0.001
ADAM_B1 = 0.9
ADAM_B2 = 0.999
ADAM_EPS = 1e-08
ADAM_WD = 0.01
ADAM_STEP = 10

VMEM_LIMIT = 48 * 1024 * 1024
MESH = pl.DeviceIdType.MESH


def _cparams(sem):
    return pltpu.CompilerParams(dimension_semantics=sem, vmem_limit_bytes=VMEM_LIMIT)


def _dot(a, b, dims=(((1,), (0,)), ((), ()))):
    return lax.dot_general(a.astype(BF16), b.astype(BF16), dims, preferred_element_type=F32)


def _dot_nt(a, b):
    return _dot(a, b, (((1,), (1,)), ((), ())))


def _dot_tn(a, b):
    return _dot(a, b, (((0,), (0,)), ((), ())))


def _pick(n, cands):
    for c in cands:
        if n % c == 0:
            return c
    raise ValueError(f"no tile for {n}")


def _mm_call(a, b, *, dims, grid, a_spec, b_spec, acc_shape, out_shapes, out_specs, name,
             extras=(), extra_specs=(), epilogue=None, after=()):
    n_k = grid[2]
    n_extra = len(extras)
    n_out = len(out_shapes)
    n_in = 2 + n_extra + len(after)

    def body(*refs):
        a_ref, b_ref = refs[0], refs[1]
        extra_refs = refs[2:2 + n_extra]
        out_refs = refs[n_in:n_in + n_out]
        acc = refs[-1]
        k = pl.program_id(2)

        @pl.when(k == 0)
        def _():
            acc[...] = jnp.zeros_like(acc)

        acc[...] += _dot(a_ref[...], b_ref[...], dims)

        @pl.when(k == n_k - 1)
        def _():
            vals = (acc[...],) if epilogue is None else epilogue(acc[...], *[e[...] for e in extra_refs])
            for o_ref, v in zip(out_refs, vals):
                o_ref[...] = v.astype(o_ref.dtype)

    return pl.pallas_call(
        body, grid=grid, in_specs=[a_spec, b_spec, *extra_specs, *([ANY] * len(after))], out_specs=list(out_specs),
        out_shape=list(out_shapes), scratch_shapes=[pltpu.VMEM(acc_shape, F32)],
        compiler_params=_cparams(("parallel", "parallel", "arbitrary")), name=name,
    )(a, b, *extras, *after)


def _w_dims(w):
    if w.ndim == 2:
        return w.shape[0], w.shape[1], 1, w.shape[1]
    return w.shape[1], w.shape[0] * w.shape[2], w.shape[0], w.shape[2]


def _mm_nn(a, w, *, name, out_dtype=F32, a_cols=None, extras=(), epilogue=None, n_out_dtypes=None):
    m = a.shape[0]
    k_dim, n_dim, _, n_slot = _w_dims(w)
    a_off, a_w = (0, a.shape[1]) if a_cols is None else a_cols
    assert a_w == k_dim
    tm = _pick(m, (2048, 1024, 512, 256))
    tn = _pick(n_slot, (512, 896, 640, 256, 128))
    tk = _pick(k_dim, (1024, 768, 512, 384, 256, 128))
    assert a_off % tk == 0
    nb = n_slot // tn
    a_spec = pl.BlockSpec((tm, tk), lambda i, j, k: (i, a_off // tk + k))
    if w.ndim == 2:
        b_spec = pl.BlockSpec((tk, tn), lambda i, j, k: (k, j))
    else:
        b_spec = pl.BlockSpec((None, tk, tn), lambda i, j, k: (j // nb, k, j % nb))
    o_spec = pl.BlockSpec((tm, tn), lambda i, j, k: (i, j))
    dts = n_out_dtypes or (out_dtype,)
    outs = _mm_call(a, w, dims=(((1,), (0,)), ((), ())), grid=(m // tm, n_dim // tn, k_dim // tk),
                    a_spec=a_spec, b_spec=b_spec, acc_shape=(tm, tn),
                    out_shapes=[SDS((m, n_dim), dt) for dt in dts], out_specs=[o_spec] * len(dts), name=name,
                    extras=extras, extra_specs=[o_spec] * len(extras), epilogue=epilogue)
    return outs if n_out_dtypes else outs[0]


def _mm_nt(a, w, *, name, out_dtype=F32, extras=(), epilogue=None, after=()):
    m = a.shape[0]
    k_dim, n_dim, _, n_slot = _w_dims(w)
    assert a.shape[1] == n_dim
    tm = _pick(m, (2048, 1024, 512, 256))
    to = _pick(k_dim, (512, 384, 256, 128))
    tc = _pick(n_slot, (1024, 896, 640, 512, 256, 128))
    nb = n_slot // tc
    a_spec = pl.BlockSpec((tm, tc), lambda i, j, k: (i, k))
    if w.ndim == 2:
        b_spec = pl.BlockSpec((to, tc), lambda i, j, k: (j, k))
    else:
        b_spec = pl.BlockSpec((None, to, tc), lambda i, j, k: (k // nb, j, k % nb))
    o_spec = pl.BlockSpec((tm, to), lambda i, j, k: (i, j))
    return _mm_call(a, w, dims=(((1,), (1,)), ((), ())), grid=(m // tm, k_dim // to, n_dim // tc),
                    a_spec=a_spec, b_spec=b_spec, acc_shape=(tm, to),
                    out_shapes=[SDS((m, k_dim), out_dtype)], out_specs=[o_spec], name=name,
                    extras=extras, extra_specs=[o_spec] * len(extras), epilogue=epilogue, after=after)[0]


def _mm_tn(x, dy, *, name, x_cols=None):
    s = x.shape[0]
    x_off, k_dim = (0, x.shape[1]) if x_cols is None else x_cols
    n_dim = dy.shape[1]
    tm = _pick(k_dim, (1024, 768, 512, 384, 256, 128))
    tn = _pick(n_dim, (512, 896, 640, 256, 128))
    tk = _pick(s, (2048, 1024, 512, 256))
    assert x_off % tm == 0
    a_spec = pl.BlockSpec((tk, tm), lambda i, j, k: (k, x_off // tm + i))
    b_spec = pl.BlockSpec((tk, tn), lambda i, j, k: (k, j))
    o_spec = pl.BlockSpec((tm, tn), lambda i, j, k: (i, j))
    return _mm_call(x, dy, dims=(((0,), (0,)), ((), ())), grid=(k_dim // tm, n_dim // tn, s // tk),
                    a_spec=a_spec, b_spec=b_spec, acc_shape=(tm, tn),
                    out_shapes=[SDS((k_dim, n_dim), F32)], out_specs=[o_spec], name=name)[0]


def _mm_tn_stacked(x, dy, *, name, col_slots):
    s, k_dim = x.shape
    n_dim = dy.shape[1]
    r, c = (k_dim // 2, n_dim // N_CHIPS) if col_slots else (k_dim // N_CHIPS // 2, n_dim)
    tm = _pick(r, (512, 384, 256, 128))
    tn = _pick(c, (512, 896, 640, 256, 128))
    tk = _pick(s, (2048, 1024, 512, 256))
    a_spec = pl.BlockSpec((tk, tm), lambda i, j, k: (k, i))
    b_spec = pl.BlockSpec((tk, tn), lambda i, j, k: (k, j))
    rb = r // tm
    if col_slots:
        nb = c // tn
        o_spec = pl.BlockSpec((None, None, tm, tn), lambda i, j, k: (i // rb, j // nb, i % rb, j % nb))
    else:
        o_spec = pl.BlockSpec((None, None, tm, tn), lambda i, j, k: ((i // rb) % 2, i // (2 * rb), i % rb, j))
    return _mm_call(x, dy, dims=(((0,), (0,)), ((), ())), grid=(k_dim // tm, n_dim // tn, s // tk),
                    a_spec=a_spec, b_spec=b_spec, acc_shape=(tm, tn),
                    out_shapes=[SDS((2, N_CHIPS, r, c), F32)], out_specs=[o_spec], name=name)[0]


def _rms(x, g):
    return x * lax.rsqrt(jnp.mean(x * x, axis=-1, keepdims=True) + EPS) * g


def _rms_fwd(h, g, *, name):
    rows, d = h.shape
    tr = _pick(rows, (512, 256))

    def body(h_ref, g_ref, o_ref):
        o_ref[...] = _rms(h_ref[...], g_ref[...]).astype(o_ref.dtype)

    return pl.pallas_call(
        body, grid=(rows // tr,),
        in_specs=[pl.BlockSpec((tr, d), lambda i: (i, 0)), pl.BlockSpec((1, d), lambda i: (0, 0))],
        out_specs=pl.BlockSpec((tr, d), lambda i: (i, 0)), out_shape=SDS((rows, d), BF16),
        compiler_params=_cparams(("parallel",)), name=name)(h, g)


def _rms_bwd(h, g, da, dres, *, name):
    rows, d = h.shape
    tr = _pick(rows, (512, 256))

    def body(h_ref, g_ref, da_ref, dres_ref, dh_ref, dg_ref):
        _, vjp = jax.vjp(_rms, h_ref[...], g_ref[...])
        dh, dg = vjp(da_ref[...].astype(F32))
        dh_ref[...] = dres_ref[...] + dh

        @pl.when(pl.program_id(0) == 0)
        def _():
            dg_ref[...] = jnp.zeros_like(dg_ref)

        dg_ref[...] += dg

    row_spec = pl.BlockSpec((tr, d), lambda i: (i, 0))
    vec_spec = pl.BlockSpec((1, d), lambda i: (0, 0))
    return pl.pallas_call(
        body, grid=(rows // tr,), in_specs=[row_spec, vec_spec, row_spec, row_spec],
        out_specs=[row_spec, vec_spec], out_shape=[SDS((rows, d), F32), SDS((1, d), F32)],
        compiler_params=_cparams(("arbitrary",)), name=name)(h, g, da, dres)


def _loss_head(h, g, target, *, name):
    rows, d = h.shape
    tr = _pick(rows, (512, 256))

    def body(h_ref, g_ref, t_ref, loss_ref, dh_ref, dg_ref):
        y, vjp = jax.vjp(_rms, h_ref[...], g_ref[...])
        err = y - t_ref[...]
        dh, dg = vjp(err * (1.0 / d))
        dh_ref[...] = dh

        @pl.when(pl.program_id(0) == 0)
        def _():
            dg_ref[...] = jnp.zeros_like(dg_ref)
            loss_ref[...] = jnp.zeros_like(loss_ref)

        dg_ref[...] += dg
        part = jnp.sum(jnp.sum(err * err, axis=-1, keepdims=True), axis=0, keepdims=True) * (0.5 / d)
        loss_ref[...] += jnp.broadcast_to(part, loss_ref.shape)

    row_spec = pl.BlockSpec((tr, d), lambda i: (i, 0))
    vec_spec = pl.BlockSpec((1, d), lambda i: (0, 0))
    loss_spec = pl.BlockSpec((8, 128), lambda i: (0, 0))
    return pl.pallas_call(
        body, grid=(rows // tr,), in_specs=[row_spec, vec_spec, row_spec],
        out_specs=[loss_spec, row_spec, vec_spec],
        out_shape=[SDS((8, 128), F32), SDS((rows, d), F32), SDS((1, d), F32)],
        compiler_params=_cparams(("arbitrary",)), name=name)(h, g, target)


def _gelu(x):
    return 0.5 * x * (1.0 + lax.erf(x * (1.0 / math.sqrt(2.0))))


def _gate_tile(pu, pv, ln_g, ln_b, ws, bs_t):
    u = [_gelu(p) for p in pu]
    v = [_gelu(p) for p in pv]
    mu = sum(jnp.sum(t, axis=-1, keepdims=True) for t in v) * (1.0 / D_INNER)
    vc = [t - mu for t in v]
    var = sum(jnp.sum(t * t, axis=-1, keepdims=True) for t in vc) * (1.0 / D_INNER)
    rstd = lax.rsqrt(var + EPS)
    row = lax.broadcasted_iota(jnp.int32, (CHUNK, CHUNK), 0)
    col = lax.broadcasted_iota(jnp.int32, (CHUNK, CHUNK), 1)
    out = []
    for gi in range(A_GROUPS):
        vn = vc[gi] * rstd * ln_g[gi] + ln_b[gi]
        w = jnp.where(row >= col, ws[gi], 0.0)
        sv = _dot(w, vn) + bs_t[gi]
        out.append(u[gi] * sv)
    return out


def _split(ref, n, width):
    return [ref[:, i * width:(i + 1) * width] for i in range(n)]


def _gate_in_specs():
    return [
        pl.BlockSpec((CHUNK, D_INNER), lambda c: (c, 0)),
        pl.BlockSpec((CHUNK, D_INNER), lambda c: (c, 1)),
        pl.BlockSpec((1, D_INNER), lambda c: (0, 0)),
        pl.BlockSpec((1, D_INNER), lambda c: (0, 0)),
        pl.BlockSpec((A_GROUPS, CHUNK, CHUNK), lambda c: (0, 0, 0)),
        pl.BlockSpec((A_GROUPS, CHUNK, 1), lambda c: (0, 0, 0)),
    ]


def _gate_args(u_ref, v_ref, g_ref, b_ref, ws_ref, bs_ref):
    ng, gw = A_GROUPS, A_GROUP_W
    return (_split(u_ref, ng, gw), _split(v_ref, ng, gw), _split(g_ref, ng, gw), _split(b_ref, ng, gw),
            [ws_ref[i] for i in range(ng)], [bs_ref[i] for i in range(ng)])


def _gate_fwd(proj, ln_g, ln_b, ws, bs_col, mixcat, *, name):
    def body(u_ref, v_ref, g_ref, b_ref, ws_ref, bs_ref, cat_in, cat_ref):
        del cat_in
        out = _gate_tile(*_gate_args(u_ref, v_ref, g_ref, b_ref, ws_ref, bs_ref))
        for gi, o in enumerate(out):
            cat_ref[:, gi * A_GROUP_W:(gi + 1) * A_GROUP_W] = o.astype(cat_ref.dtype)

    return pl.pallas_call(
        body, grid=(N_CHUNKS,), in_specs=[*_gate_in_specs(), pl.BlockSpec(memory_space=pl.ANY)],
        out_specs=pl.BlockSpec((CHUNK, D_INNER), lambda c: (c, 0)), out_shape=SDS(mixcat.shape, mixcat.dtype),
        input_output_aliases={6: 0}, compiler_params=_cparams(("parallel",)), name=name,
    )(proj, proj, ln_g, ln_b, ws, bs_col, mixcat)


def _gate_bwd(proj, ln_g, ln_b, ws, bs_col, dcat, dproj, *, name):
    ng, gw = A_GROUPS, A_GROUP_W

    def body(u_ref, v_ref, g_ref, b_ref, ws_ref, bs_ref, d_ref, dproj_in, dproj_ref, dg_ref, db_ref, dws_ref, dbs_ref):
        del dproj_in
        args = _gate_args(u_ref, v_ref, g_ref, b_ref, ws_ref, bs_ref)
        _, vjp = jax.vjp(_gate_tile, *args)
        dpu, dpv, dg, db, dws, dbs = vjp(_split(d_ref, ng, gw))
        for gi in range(ng):
            dproj_ref[:, gi * gw:(gi + 1) * gw] = dpu[gi].astype(dproj_ref.dtype)
            dproj_ref[:, D_INNER + gi * gw:D_INNER + (gi + 1) * gw] = dpv[gi].astype(dproj_ref.dtype)

        @pl.when(pl.program_id(0) == 0)
        def _():
            for r in (dg_ref, db_ref, dws_ref, dbs_ref):
                r[...] = jnp.zeros_like(r)

        for gi in range(ng):
            dg_ref[:, gi * gw:(gi + 1) * gw] += dg[gi]
            db_ref[:, gi * gw:(gi + 1) * gw] += db[gi]
            dws_ref[gi] += dws[gi]
            dbs_ref[gi] += dbs[gi]

    in_specs = _gate_in_specs()
    return pl.pallas_call(
        body, grid=(N_CHUNKS,),
        in_specs=[*in_specs, pl.BlockSpec((CHUNK, D_INNER), lambda c: (c, 0)), pl.BlockSpec(memory_space=pl.ANY)],
        out_specs=[pl.BlockSpec((CHUNK, 2 * D_INNER), lambda c: (c, 0)), *in_specs[2:]],
        out_shape=[SDS(dproj.shape, dproj.dtype), SDS((1, D_INNER), F32), SDS((1, D_INNER), F32),
                   SDS((ng, CHUNK, CHUNK), F32), SDS((ng, CHUNK, 1), F32)],
        input_output_aliases={7: 0}, compiler_params=_cparams(("arbitrary",)), name=name,
    )(proj, proj, ln_g, ln_b, ws, bs_col, dcat, dproj)


ATT_TQ = 512


def _attn_tile(q, k, v):
    s = _dot_nt(q, k) * (1.0 / math.sqrt(X_HEAD_DIM))
    s = s - jnp.max(s, axis=-1, keepdims=True)
    e = jnp.exp(s)
    p = e / jnp.sum(e, axis=-1, keepdims=True)
    return _dot(p, v)


def _attn_in_specs(q_blk, order):
    hd = X_HEAD_DIM
    return [
        pl.BlockSpec((ATT_TQ, hd), lambda a, b: (order(a, b)[0], q_blk + order(a, b)[1])),
        pl.BlockSpec((N_MEM, hd), lambda a, b: (0, order(a, b)[1])),
        pl.BlockSpec((N_MEM, hd), lambda a, b: (0, X_HEADS + order(a, b)[1])),
    ]


def _attn_fwd(proj, q_off, kv, *, name):
    order = lambda i, h: (i, h)
    cat_blk = D_INNER // X_HEAD_DIM

    def body(q_ref, k_ref, v_ref, o_ref):
        o_ref[...] = _attn_tile(q_ref[...], k_ref[...], v_ref[...]).astype(o_ref.dtype)

    return pl.pallas_call(
        body, grid=(SEQ // ATT_TQ, X_HEADS), in_specs=_attn_in_specs(q_off // X_HEAD_DIM, order),
        out_specs=pl.BlockSpec((ATT_TQ, X_HEAD_DIM), lambda i, h: (i, cat_blk + h)),
        out_shape=SDS((SEQ, MIX_OUT), BF16), compiler_params=_cparams(("parallel", "parallel")), name=name,
    )(proj, kv, kv)


def _attn_bwd(proj, q_off, kv, dcat, dproj_width, dq_off, *, name):
    order = lambda h, i: (i, h)
    cat_blk = D_INNER // X_HEAD_DIM
    dq_blk = dq_off // X_HEAD_DIM

    def body(q_ref, k_ref, v_ref, do_ref, dq_ref, dk_ref, dv_ref):
        _, vjp = jax.vjp(_attn_tile, q_ref[...], k_ref[...], v_ref[...])
        dq, dk, dv = vjp(do_ref[...])
        dq_ref[...] = dq.astype(dq_ref.dtype)

        @pl.when(pl.program_id(1) == 0)
        def _():
            dk_ref[...] = jnp.zeros_like(dk_ref)
            dv_ref[...] = jnp.zeros_like(dv_ref)

        dk_ref[...] += dk
        dv_ref[...] += dv

    kv_spec = pl.BlockSpec((N_MEM, X_HEAD_DIM), lambda h, i: (0, h))
    return pl.pallas_call(
        body, grid=(X_HEADS, SEQ // ATT_TQ),
        in_specs=[*_attn_in_specs(q_off // X_HEAD_DIM, order),
                  pl.BlockSpec((ATT_TQ, X_HEAD_DIM), lambda h, i: (i, cat_blk + h))],
        out_specs=[pl.BlockSpec((ATT_TQ, X_HEAD_DIM), lambda h, i: (i, dq_blk + h)), kv_spec, kv_spec],
        out_shape=[SDS((SEQ, dproj_width), BF16), SDS((N_MEM, X_WIDTH), F32), SDS((N_MEM, X_WIDTH), F32)],
        compiler_params=_cparams(("parallel", "arbitrary")), name=name,
    )(proj, kv, kv, dcat)


CONV_TC = 512


def _shift_down(x, s):
    if s == 0:
        return x
    row = lax.broadcasted_iota(jnp.int32, x.shape, 0)
    return jnp.where(row >= s, pltpu.roll(x, s, 0), 0.0)


def _shift_up(x, s):
    if s == 0:
        return x
    n = x.shape[0]
    row = lax.broadcasted_iota(jnp.int32, x.shape, 0)
    return jnp.where(row < n - s, pltpu.roll(x, n - s, 0), 0.0)


def _conv_pre(x, w_ref, b_ref):
    pre = b_ref[...] + jnp.zeros_like(x)
    for k in range(CONV_K):
        pre = pre + w_ref[k:k + 1, :] * _shift_down(x, CONV_K - 1 - k)
    return pre


def _conv_fwd(proj, w, b, *, name):
    blk0 = D_INNER // CONV_TC

    def body(x_ref, w_ref, b_ref, o_ref):
        pre = _conv_pre(x_ref[...], w_ref, b_ref)
        o_ref[...] = pre * jax.nn.sigmoid(pre)

    return pl.pallas_call(
        body, grid=(CONV_DIM // CONV_TC,),
        in_specs=[pl.BlockSpec((SEQ, CONV_TC), lambda j: (0, blk0 + j)), pl.BlockSpec((CONV_K, CONV_TC), lambda j: (0, j)),
                  pl.BlockSpec((1, CONV_TC), lambda j: (0, j))],
        out_specs=pl.BlockSpec((SEQ, CONV_TC), lambda j: (0, j)), out_shape=SDS((SEQ, CONV_DIM), F32),
        compiler_params=_cparams(("parallel",)), name=name)(proj, w, b)


def _conv_bwd(proj, w, b, dxs, dbm, dcm, dproj, *, name):
    tc = CONV_TC // 2
    blk0 = D_INNER // tc
    n_x = D_INNER // tc
    n_b = SSM_GROUPS * SSM_STATE // tc

    def body(x_ref, w_ref, b_ref, dxs_ref, dbm_ref, dcm_ref, dproj_in, dproj_ref, dw_ref, db_ref):
        del dproj_in
        j = pl.program_id(0)
        x = x_ref[...]
        pre = _conv_pre(x, w_ref, b_ref)
        sg = jax.nn.sigmoid(pre)
        dact = jnp.where(j < n_x, dxs_ref[...], jnp.where(j < n_x + n_b, dbm_ref[...], dcm_ref[...]))
        dpre = dact * (sg * (1.0 + pre * (1.0 - sg)))
        dx = jnp.zeros_like(x)
        for k in range(CONV_K):
            s = CONV_K - 1 - k
            dx = dx + w_ref[k:k + 1, :] * _shift_up(dpre, s)
            dw_ref[k:k + 1, :] = jnp.sum(dpre * _shift_down(x, s), axis=0, keepdims=True)
        dproj_ref[...] = dx.astype(dproj_ref.dtype)
        db_ref[...] = jnp.sum(dpre, axis=0, keepdims=True)

    clip = lambda v, hi: jnp.minimum(jnp.maximum(v, 0), hi)
    return pl.pallas_call(
        body, grid=(CONV_DIM // tc,),
        in_specs=[pl.BlockSpec((SEQ, tc), lambda j: (0, blk0 + j)), pl.BlockSpec((CONV_K, tc), lambda j: (0, j)),
                  pl.BlockSpec((1, tc), lambda j: (0, j)),
                  pl.BlockSpec((SEQ, tc), lambda j: (0, clip(j, n_x - 1))),
                  pl.BlockSpec((SEQ, tc), lambda j: (0, clip(j - n_x, n_b - 1))),
                  pl.BlockSpec((SEQ, tc), lambda j: (0, clip(j - n_x - n_b, n_b - 1))),
                  pl.BlockSpec(memory_space=pl.ANY)],
        out_specs=[pl.BlockSpec((SEQ, tc), lambda j: (0, blk0 + j)), pl.BlockSpec((CONV_K, tc), lambda j: (0, j)),
                   pl.BlockSpec((1, tc), lambda j: (0, j))],
        out_shape=[SDS(dproj.shape, dproj.dtype), SDS((CONV_K, CONV_DIM), F32), SDS((1, CONV_DIM), F32)],
        input_output_aliases={6: 0}, compiler_params=_cparams(("parallel",)), name=name,
    )(proj, w, b, dxs, dbm, dcm, dproj)


SSM_PAIRS = SSM_HPG // 2


def _ssd_tile(xp, zp, bm, cm, hp, dtc, dtr, bias, alog, dsk, gnp):
    row = lax.broadcasted_iota(jnp.int32, (CHUNK, CHUNK), 0)
    col = lax.broadcasted_iota(jnp.int32, (CHUNK, CHUNK), 1)
    causal = row >= col
    tri = jnp.where(causal, 1.0, 0.0)
    left = col < SSM_HEAD_DIM
    top = row < SSM_HEAD_DIM
    ones = jnp.ones((CHUNK, CHUNK), BF16)
    cb = _dot_nt(cm, bm)
    dt_c, cs_c, cs_last, m = [], [], [], []
    for r in range(SSM_HPG):
        a = -jnp.exp(alog[r])
        dt_c.append(jax.nn.softplus(dtc[r] + bias[r]))
        da_c = dt_c[r] * a
        da_r = jax.nn.softplus(dtr[r] + bias[r]) * a
        cs_c.append(jnp.sum(tri * da_r, axis=1, keepdims=True))
        cs_r = jnp.sum(jnp.where(row <= col, 1.0, 0.0) * da_c, axis=0, keepdims=True)
        cs_last.append(jnp.sum(da_c, axis=0, keepdims=True))
        m.append(cb * jnp.exp(jnp.where(causal, cs_c[r] - cs_r, -1e30)))
    ygs, hn = [], []
    for p in range(SSM_PAIRS):
        a, b = 2 * p, 2 * p + 1
        pair = lambda u, v: jnp.where(left, u, v)
        xdt = xp[p] * pair(dt_c[a], dt_c[b])
        y = pair(_dot(m[a], xdt), _dot(m[b], xdt))
        y = y + _dot_nt(cm, hp[p]) * pair(jnp.exp(cs_c[a]), jnp.exp(cs_c[b]))
        y = y + xp[p] * pair(dsk[a], dsk[b])
        decay = pair(jnp.exp(cs_last[a] - cs_c[a]), jnp.exp(cs_last[b] - cs_c[b]))
        states = _dot_tn(xdt * decay, bm)
        hn.append(hp[p] * jnp.where(top, jnp.exp(cs_last[a]), jnp.exp(cs_last[b])) + states)
        ygs.append(y * (zp[p] * jax.nn.sigmoid(zp[p])))
    ms = sum(_dot(t * t, ones) for t in ygs) * (1.0 / SSM_GROUP_W)
    rs = lax.rsqrt(ms + EPS)
    return [ygs[p] * rs * gnp[p] for p in range(SSM_PAIRS)], hn


def _ssd_in_specs(cidx):
    gw, n = SSM_GROUP_W, SSM_STATE
    bm_blk = D_INNER // n
    return [
        pl.BlockSpec((CHUNK, gw), lambda g, c: (cidx(c), g)),
        pl.BlockSpec((CHUNK, gw), lambda g, c: (cidx(c), g)),
        pl.BlockSpec((CHUNK, n), lambda g, c: (cidx(c), bm_blk + g)),
        pl.BlockSpec((CHUNK, n), lambda g, c: (cidx(c), bm_blk + SSM_GROUPS + g)),
        pl.BlockSpec((None, CHUNK, SSM_HPG), lambda g, c: (g, cidx(c), 0)),
        pl.BlockSpec((None, SSM_HPG, CHUNK), lambda g, c: (g, 0, cidx(c))),
        pl.BlockSpec((None, 1, SSM_HPG), lambda g, c: (g, 0, 0)),
        pl.BlockSpec((None, 1, SSM_HPG), lambda g, c: (g, 0, 0)),
        pl.BlockSpec((None, 1, SSM_HPG), lambda g, c: (g, 0, 0)),
        pl.BlockSpec((1, gw), lambda g, c: (0, g)),
    ]


def _ssd_args(x_ref, z_ref, bm_ref, cm_ref, hp, dtc_ref, dtr_ref, bias_ref, alog_ref, dsk_ref, gn_ref):
    nh, npair, w = SSM_HPG, SSM_PAIRS, 2 * SSM_HEAD_DIM
    col = lambda ref: [ref[:, r:r + 1] for r in range(nh)]
    return (_split(x_ref, npair, w), _split(z_ref, npair, w), bm_ref[...], cm_ref[...], hp,
            col(dtc_ref), [dtr_ref[r:r + 1, :] for r in range(nh)], col(bias_ref), col(alog_ref), col(dsk_ref),
            _split(gn_ref, npair, w))


def _pair_rows(ref):
    w = 2 * SSM_HEAD_DIM
    return [ref[p * w:(p + 1) * w, :] for p in range(SSM_PAIRS)]


def _ssd_fwd(xbc, proj, dt_c, dt_r, bias, alog, dsk, gn, mixcat, *, name):
    w = 2 * SSM_HEAD_DIM

    def body(x_ref, z_ref, bm_ref, cm_ref, dtc_ref, dtr_ref, bias_ref, alog_ref, dsk_ref, gn_ref, cat_in,
             cat_ref, hprev_ref, h_scr):
        del cat_in

        @pl.when(pl.program_id(1) == 0)
        def _():
            h_scr[...] = jnp.zeros_like(h_scr)

        hprev_ref[...] = h_scr[...]
        yn, hn = _ssd_tile(*_ssd_args(x_ref, z_ref, bm_ref, cm_ref, _pair_rows(h_scr), dtc_ref, dtr_ref, bias_ref,
                                      alog_ref, dsk_ref, gn_ref))
        for p in range(SSM_PAIRS):
            cat_ref[:, p * w:(p + 1) * w] = yn[p].astype(cat_ref.dtype)
            h_scr[p * w:(p + 1) * w, :] = hn[p]

    return pl.pallas_call(
        body, grid=(SSM_GROUPS, N_CHUNKS), in_specs=[*_ssd_in_specs(lambda c: c), pl.BlockSpec(memory_space=pl.ANY)],
        out_specs=[pl.BlockSpec((CHUNK, SSM_GROUP_W), lambda g, c: (c, g)),
                   pl.BlockSpec((None, None, SSM_GROUP_W, SSM_STATE), lambda g, c: (c, g, 0, 0))],
        out_shape=[SDS(mixcat.shape, mixcat.dtype), SDS((N_CHUNKS, SSM_GROUPS, SSM_GROUP_W, SSM_STATE), F32)],
        scratch_shapes=[pltpu.VMEM((SSM_GROUP_W, SSM_STATE), F32)],
        input_output_aliases={10: 0}, compiler_params=_cparams(("parallel", "arbitrary")), name=name,
    )(xbc, proj, xbc, xbc, dt_c, dt_r, bias, alog, dsk, gn, mixcat)


def _ssd_bwd(xbc, proj, dt_c, dt_r, bias, alog, dsk, gn, hprev, dcat, dproj, *, name):
    nh, w, gw, n = SSM_HPG, 2 * SSM_HEAD_DIM, SSM_GROUP_W, SSM_STATE
    rev = lambda c: N_CHUNKS - 1 - c

    def body(x_ref, z_ref, bm_ref, cm_ref, dtc_ref, dtr_ref, bias_ref, alog_ref, dsk_ref, gn_ref, hprev_ref, dy_ref,
             dproj_in, dz_ref, dxs_ref, dbm_ref, dcm_ref, ddtc_ref, ddtr_ref, dbias_ref, dalog_ref, ddsk_ref, dgn_ref,
             dh_scr):
        del dproj_in
        first = pl.program_id(1) == 0

        @pl.when(first)
        def _():
            dh_scr[...] = jnp.zeros_like(dh_scr)
            for ref in (dbias_ref, dalog_ref, ddsk_ref, dgn_ref):
                ref[...] = jnp.zeros_like(ref)

        args = _ssd_args(x_ref, z_ref, bm_ref, cm_ref, _pair_rows(hprev_ref), dtc_ref, dtr_ref, bias_ref, alog_ref,
                         dsk_ref, gn_ref)
        _, vjp = jax.vjp(_ssd_tile, *args)
        dxs, dzs, dbm, dcm, dhs, ddtc, ddtr, dbias, dalog, ddsk, dgn = vjp(
            (_split(dy_ref, SSM_PAIRS, w), _pair_rows(dh_scr)))
        dbm_ref[...] = dbm
        dcm_ref[...] = dcm
        for q in range(SSM_PAIRS):
            dxs_ref[:, q * w:(q + 1) * w] = dxs[q]
            dz_ref[:, q * w:(q + 1) * w] = dzs[q].astype(dz_ref.dtype)
            dh_scr[q * w:(q + 1) * w, :] = dhs[q]
            dgn_ref[:, q * w:(q + 1) * w] += dgn[q]
        for r in range(nh):
            ddtc_ref[:, r:r + 1] = ddtc[r]
            ddtr_ref[r:r + 1, :] = ddtr[r]
            dbias_ref[:, r:r + 1] += dbias[r]
            dalog_ref[:, r:r + 1] += dalog[r]
            ddsk_ref[:, r:r + 1] += ddsk[r]

    par_spec = pl.BlockSpec((None, 1, nh), lambda g, c: (g, 0, 0))
    return pl.pallas_call(
        body, grid=(SSM_GROUPS, N_CHUNKS),
        in_specs=[*_ssd_in_specs(rev),
                  pl.BlockSpec((None, None, gw, n), lambda g, c: (rev(c), g, 0, 0)),
                  pl.BlockSpec((CHUNK, gw), lambda g, c: (rev(c), g)),
                  pl.BlockSpec(memory_space=pl.ANY)],
        out_specs=[pl.BlockSpec((CHUNK, gw), lambda g, c: (rev(c), g)),
                   pl.BlockSpec((CHUNK, gw), lambda g, c: (rev(c), g)),
                   pl.BlockSpec((CHUNK, n), lambda g, c: (rev(c), g)),
                   pl.BlockSpec((CHUNK, n), lambda g, c: (rev(c), g)),
                   pl.BlockSpec((None, CHUNK, nh), lambda g, c: (g, rev(c), 0)),
                   pl.BlockSpec((None, nh, CHUNK), lambda g, c: (g, 0, rev(c))),
                   par_spec, par_spec, par_spec,
                   pl.BlockSpec((1, gw), lambda g, c: (0, g))],
        out_shape=[SDS(dproj.shape, dproj.dtype), SDS((SEQ, D_INNER), F32), SDS((SEQ, SSM_GROUPS * n), F32),
                   SDS((SEQ, SSM_GROUPS * n), F32), SDS((SSM_GROUPS, SEQ, nh), F32), SDS((SSM_GROUPS, nh, SEQ), F32),
                   SDS((SSM_GROUPS, 1, nh), F32), SDS((SSM_GROUPS, 1, nh), F32), SDS((SSM_GROUPS, 1, nh), F32),
                   SDS((1, D_INNER), F32)],
        scratch_shapes=[pltpu.VMEM((gw, n), F32)],
        input_output_aliases={12: 0}, compiler_params=_cparams(("parallel", "arbitrary")), name=name,
    )(xbc, proj, xbc, xbc, dt_c, dt_r, bias, alog, dsk, gn, hprev, dcat, dproj)


def _sum_contributions(chip, parts, landed, *, name):
    _, r, c = parts.shape
    tr = _pick(r, (256, 384, 128))

    def body(chip_ref, own_ref, landed_ref, o_ref):
        del chip_ref
        acc = own_ref[...].astype(F32)
        for s in range(landed_ref.shape[0]):
            acc = acc + landed_ref[s].astype(F32)
        o_ref[...] = acc

    grid_spec = pltpu.PrefetchScalarGridSpec(
        num_scalar_prefetch=1, grid=(r // tr,),
        in_specs=[pl.BlockSpec((None, tr, c), lambda i, chip_ref: (chip_ref[0], i, 0)),
                  pl.BlockSpec((landed.shape[0], tr, c), lambda i, chip_ref: (0, i, 0))],
        out_specs=pl.BlockSpec((tr, c), lambda i, chip_ref: (i, 0)))
    return pl.pallas_call(body, grid_spec=grid_spec, out_shape=SDS((r, c), F32),
                          compiler_params=_cparams(("parallel",)), name=name)(chip, parts, landed)


def _adamw(w, g, m, v, *, name):
    r, c = w.shape
    tr = r if r <= 256 else _pick(r, (256, 128, 8))
    spec = pl.BlockSpec((tr, c), lambda i: (i, 0))

    def body(w_ref, g_ref, m_ref, v_ref, d_ref, mo_ref, vo_ref):
        g = g_ref[...]
        m_new = ADAM_B1 * m_ref[...] + (1.0 - ADAM_B1) * g
        v_new = ADAM_B2 * v_ref[...] + (1.0 - ADAM_B2) * (g * g)
        m_hat = m_new / (1.0 - ADAM_B1 ** ADAM_STEP)
        v_hat = v_new / (1.0 - ADAM_B2 ** ADAM_STEP)
        d_ref[...] = -ADAM_LR * (m_hat / (jnp.sqrt(v_hat) + ADAM_EPS) + ADAM_WD * w_ref[...])
        mo_ref[...] = m_new
        vo_ref[...] = v_new

    return pl.pallas_call(body, grid=(r // tr,), in_specs=[spec] * 4, out_specs=[spec] * 3,
                          out_shape=[SDS((r, c), F32)] * 3, compiler_params=_cparams(("parallel",)), name=name)(w, g, m, v)


ANY = pl.BlockSpec(memory_space=pl.ANY)


def _place():
    x, y, c = lax.axis_index("x"), lax.axis_index("y"), lax.axis_index("c")
    chips = [(1 - x, y), (x, 1 - y), (1 - x, 1 - y)]
    return x, y, c, chips


def _remote(src, dst, send_sem, recv_sem, to):
    return pltpu.make_async_remote_copy(src_ref=src, dst_ref=dst, send_sem=send_sem, recv_sem=recv_sem,
                                        device_id=to, device_id_type=MESH)


STREAM_ROWS = 128


def _stream_rows(i):
    return pl.ds(pl.multiple_of(i * STREAM_ROWS, STREAM_ROWS), STREAM_ROWS)


def _channel_scratch(width, dtype):
    buf = (2, STREAM_ROWS, width)
    return [pltpu.VMEM(buf, dtype), pltpu.VMEM(buf, dtype), *([pltpu.SemaphoreType.DMA((2,))] * 5),
            pltpu.SemaphoreType.REGULAR((2,))]


CHANNEL_REFS = 8


def _copy_through_vmem(src, dst, ch):
    sbuf, _, ld, _, _, st, _, _ = ch
    steps = src.shape[0] // STREAM_ROWS
    assert steps >= 2 and steps * STREAM_ROWS == src.shape[0]

    def load(i, slot):
        return pltpu.make_async_copy(src.at[_stream_rows(i)], sbuf.at[slot], ld.at[slot])

    def store(i, slot):
        return pltpu.make_async_copy(sbuf.at[slot], dst.at[_stream_rows(i)], st.at[slot])

    load(0, 0).start()

    def step(i, carry):
        slot = lax.rem(i, 2)
        nxt = 1 - slot

        @pl.when(i + 1 < steps)
        def _():
            @pl.when(i >= 1)
            def _():
                store(0, nxt).wait()
            load(i + 1, nxt).start()

        load(i, slot).wait()
        store(i, slot).start()
        return carry

    lax.fori_loop(0, steps, step, 0)
    for slot in range(2):
        store(0, slot).wait()


def _exchange_stream(src, dst, keep, ch, sibling):
    sbuf, rbuf, ld, snd, rcv, st, kp, credit = ch
    steps = src.shape[0] // STREAM_ROWS
    assert steps >= 2 and steps * STREAM_ROWS == src.shape[0]

    def load(i, slot):
        return pltpu.make_async_copy(src.at[_stream_rows(i)], sbuf.at[slot], ld.at[slot])

    def push(slot):
        return _remote(sbuf.at[slot], rbuf.at[slot], snd.at[slot], rcv.at[slot], sibling)

    def store(i, slot):
        return pltpu.make_async_copy(rbuf.at[slot], dst.at[_stream_rows(i)], st.at[slot])

    def save(i, slot):
        return pltpu.make_async_copy(sbuf.at[slot], keep.at[_stream_rows(i)], kp.at[slot])

    for slot in range(2):
        pl.semaphore_signal(credit.at[slot], 1, device_id=sibling, device_id_type=MESH)
    load(0, 0).start()

    def step(i, carry):
        slot = lax.rem(i, 2)
        nxt = 1 - slot

        @pl.when(i + 1 < steps)
        def _():
            @pl.when(i >= 1)
            def _():
                push(nxt).wait_send()
                if keep is not None:
                    save(0, nxt).wait()
            load(i + 1, nxt).start()

        load(i, slot).wait()
        pl.semaphore_wait(credit.at[slot], 1)
        push(slot).start()
        if keep is not None:
            save(i, slot).start()
        push(slot).wait_recv()
        store(i, slot).start()

        @pl.when(i >= 1)
        def _():
            store(0, nxt).wait()

            @pl.when(i + 1 < steps)
            def _():
                pl.semaphore_signal(credit.at[nxt], 1, device_id=sibling, device_id_type=MESH)
        return carry

    lax.fori_loop(0, steps, step, 0)
    store(0, (steps - 1) % 2).wait()
    for slot in range(2):
        push(slot).wait_send()
        if keep is not None:
            save(0, slot).wait()


def _all_gather_shards(shards, small, *, name):
    n = len(shards)

    def body(*refs):
        ins, outs = refs[:n + 1], refs[n + 1:2 * n + 2]
        scr = refs[2 * n + 2:]
        chans = [scr[CHANNEL_REFS * t:CHANNEL_REFS * (t + 1)] for t in range(n)]
        send_sems, recv_sems, small_sems = scr[CHANNEL_REFS * n:]
        x, y, c, _ = _place()
        me = 2 * x + y
        sibling = (x, y, 1 - c)
        near = (lax.rem(x + 1 - c, 2), lax.rem(y + c, 2))
        far = (lax.rem(x + c, 2), lax.rem(y + 1 - c, 2))
        k_near, k_far, k_diag = 2 * near[0] + near[1], 2 * far[0] + far[1], 3 - me
        targets = ((*near, c), (*far, c), (*far, c))
        arrives = (k_near, k_far, k_diag)
        streams_in = (k_far, k_near, k_diag)

        def ici(t, j, src, blk):
            return _remote(src, outs[t].at[blk, c], send_sems.at[3 * t + j], recv_sems.at[3 * t + j], targets[j])

        first = [ici(t, j, ins[t].at[c], me) for t in range(n + 1) for j in range(2)]
        for cp in first:
            cp.start()
        small_local = pltpu.make_async_copy(ins[n], outs[n].at[me], small_sems.at[6])
        small_local.start()
        for t in range(n):
            for h in range(2):
                _copy_through_vmem(ins[t].at[h], outs[t].at[me, h], chans[t])
        passed = []
        for j in range(3):
            for t in range(n + 1):
                landed = outs[t].at[arrives[j], c]
                ici(t, j, landed, arrives[j]).wait_recv()
                if j == 0:
                    fwd = ici(t, 2, landed, k_near)
                    fwd.start()
                    passed.append(fwd)
                if t < n:
                    _exchange_stream(landed, outs[t].at[streams_in[j], 1 - c], None, chans[t], sibling)
                else:
                    fwd = _remote(landed, landed, small_sems.at[j], small_sems.at[3 + j], sibling)
                    fwd.start()
                    passed.append(fwd)
        for j in range(3):
            got = outs[n].at[streams_in[j], 1 - c]
            _remote(got, got, small_sems.at[j], small_sems.at[3 + j], sibling).wait_recv()
        for cp in first + passed:
            cp.wait_send()
        small_local.wait()

    scratch = []
    for s in shards:
        scratch += _channel_scratch(s.shape[2], s.dtype)
    return pl.pallas_call(
        body, in_specs=[ANY] * (n + 1), out_specs=[ANY] * (n + 1),
        out_shape=[SDS((N_CHIPS, *s.shape), s.dtype) for s in (*shards, small)],
        scratch_shapes=[*scratch, pltpu.SemaphoreType.DMA((3 * n + 3,)), pltpu.SemaphoreType.DMA((3 * n + 3,)),
                        pltpu.SemaphoreType.DMA((7,))],
        compiler_params=pltpu.CompilerParams(vmem_limit_bytes=VMEM_LIMIT), name=name)(*shards, small)


def _pair_reduce(stacks, *, name):
    n = len(stacks)
    per = 10

    def body(*refs):
        ins, outs, scr = refs[:n], refs[n:2 * n], refs[2 * n:]
        x, y, c, _ = _place()
        sibling = (x, y, 1 - c)
        for t in range(n):
            sbuf, rbuf, obuf, pbuf, ld_s, ld_o, snd, rcv, st, credit = scr[per * t:per * (t + 1)]
            steps = ins[t].shape[1] // STREAM_ROWS
            src, own, out = ins[t].at[1 - c], ins[t].at[c], outs[t]

            def load_s(i, slot, src=src, sbuf=sbuf, ld_s=ld_s):
                return pltpu.make_async_copy(src.at[_stream_rows(i)], sbuf.at[slot], ld_s.at[slot])

            def load_o(i, slot, own=own, obuf=obuf, ld_o=ld_o):
                return pltpu.make_async_copy(own.at[_stream_rows(i)], obuf.at[slot], ld_o.at[slot])

            def push(slot, sbuf=sbuf, rbuf=rbuf, snd=snd, rcv=rcv):
                return _remote(sbuf.at[slot], rbuf.at[slot], snd.at[slot], rcv.at[slot], sibling)

            def store(i, slot, pbuf=pbuf, out=out, st=st):
                return pltpu.make_async_copy(pbuf.at[slot], out.at[_stream_rows(i)], st.at[slot])

            assert steps >= 2
            for slot in range(2):
                pl.semaphore_signal(credit.at[slot], 1, device_id=sibling, device_id_type=MESH)
                load_s(slot, slot).start()
                load_o(slot, slot).start()
            load_s(0, 0).wait()
            pl.semaphore_wait(credit.at[0], 1)
            push(0).start()

            def step(i, carry, load_s=load_s, load_o=load_o, push=push, store=store, rbuf=rbuf, obuf=obuf, pbuf=pbuf,
                     credit=credit, steps=steps):
                slot = lax.rem(i, 2)
                nxt = 1 - slot

                @pl.when(i + 1 < steps)
                def _():
                    load_s(i + 1, nxt).wait()
                    pl.semaphore_wait(credit.at[nxt], 1)
                    push(nxt).start()

                load_o(i, slot).wait()
                push(slot).wait_recv()

                @pl.when(i >= 2)
                def _():
                    store(i, slot).wait()

                pbuf[slot] = (obuf[slot] + rbuf[slot]).astype(pbuf.dtype)
                store(i, slot).start()
                push(slot).wait_send()

                @pl.when(i + 2 < steps)
                def _():
                    load_s(i + 2, slot).start()
                    load_o(i + 2, slot).start()
                    pl.semaphore_signal(credit.at[slot], 1, device_id=sibling, device_id_type=MESH)
                return carry

            lax.fori_loop(0, steps, step, 0)
            for slot in range(2):
                store(0, slot).wait()

    scratch = []
    for s in stacks:
        buf = (2, STREAM_ROWS, s.shape[2])
        scratch += [pltpu.VMEM(buf, F32), pltpu.VMEM(buf, F32), pltpu.VMEM(buf, F32), pltpu.VMEM(buf, BF16),
                    *([pltpu.SemaphoreType.DMA((2,))] * 5), pltpu.SemaphoreType.REGULAR((2,))]
    return pl.pallas_call(
        body, in_specs=[ANY] * n, out_specs=[ANY] * n, out_shape=[SDS(s.shape[1:], BF16) for s in stacks],
        scratch_shapes=scratch, compiler_params=pltpu.CompilerParams(vmem_limit_bytes=VMEM_LIMIT), name=name)(*stacks)


def _chip_scatter(parts, *, name):
    n = len(parts)

    def body(*refs):
        ins, outs = refs[:n], refs[n:2 * n]
        send_sems, recv_sems = refs[2 * n:]
        _, _, c, chips = _place()
        copies = [_remote(ins[t].at[2 * cx + cy], outs[t].at[j], send_sems.at[3 * t + j], recv_sems.at[3 * t + j],
                          (cx, cy, c)) for t in range(n) for j, (cx, cy) in enumerate(chips)]
        for cp in copies:
            cp.start()
        for cp in copies:
            cp.wait_recv()
        for cp in copies:
            cp.wait_send()

    return pl.pallas_call(
        body, in_specs=[ANY] * n, out_specs=[ANY] * n, out_shape=[SDS((3, *p.shape[1:]), p.dtype) for p in parts],
        scratch_shapes=[pltpu.SemaphoreType.DMA((3 * n,)), pltpu.SemaphoreType.DMA((3 * n,))], name=name)(*parts)


HBM_SPEC = pl.BlockSpec(memory_space=pltpu.HBM)
SEM_SPEC = pl.BlockSpec(memory_space=pltpu.SEMAPHORE)
SIDE_EFFECT = pltpu.SideEffectType.DATAFLOW_SIDE_EFFECTING


def _scatter_copies(ins, lands, send_sems, recv_sems):
    _, _, c, chips = _place()
    return [_remote(ins[t].at[2 * cx + cy], lands[t].at[j], send_sems.at[3 * t + j], recv_sems.at[3 * t + j],
                    (cx, cy, c)) for t in range(len(ins)) for j, (cx, cy) in enumerate(chips)]


def _chip_scatter_start(parts, *, name):
    n = len(parts)

    def body(*refs):
        ins, lands = refs[:n], refs[n:2 * n]
        send_sems, recv_sems, token = refs[2 * n], refs[2 * n + 1], refs[-1]
        for cp in _scatter_copies(ins, lands, send_sems, recv_sems):
            cp.start()
        token[...] = jnp.zeros_like(token)

    hbm = lambda a: pltpu.with_memory_space_constraint(a, pltpu.HBM)
    lands = [hbm(lax.empty((3, *p.shape[1:]), p.dtype)) for p in parts]
    thru = [pltpu.HBM(a.shape, a.dtype) for a in (*parts, *lands)]
    outs = pl.pallas_call(
        body, name=name,
        out_shape=(pltpu.SemaphoreType.DMA((3 * n,)), pltpu.SemaphoreType.DMA((3 * n,)), *thru, SDS((8, 128), F32)),
        in_specs=[HBM_SPEC] * (2 * n),
        out_specs=(SEM_SPEC, SEM_SPEC, *([HBM_SPEC] * (2 * n)), pl.BlockSpec(memory_space=pltpu.VMEM)),
        input_output_aliases={i: 2 + i for i in range(2 * n)},
        compiler_params=pltpu.CompilerParams(has_side_effects=SIDE_EFFECT),
    )(*[hbm(p) for p in parts], *lands)
    return outs[0], outs[1], outs[2:2 + n], outs[2 + n:2 + 2 * n], outs[-1]


def _chip_scatter_wait(send_sems, recv_sems, parts, lands, after, *, name):
    n = len(parts)

    def body(*refs):
        ins, lands_in = refs[:n], refs[n:2 * n]
        for cp in _scatter_copies(ins, lands_in, refs[2 * n], refs[2 * n + 1]):
            cp.wait_send()
            cp.wait_recv()

    outs = pl.pallas_call(
        body, name=name, out_shape=[pltpu.HBM(a.shape, a.dtype) for a in (*parts, *lands)],
        in_specs=[*([HBM_SPEC] * (2 * n)), SEM_SPEC, SEM_SPEC, *([ANY] * len(after))],
        out_specs=[HBM_SPEC] * (2 * n), input_output_aliases={i: i for i in range(2 * n)},
        compiler_params=pltpu.CompilerParams(has_side_effects=SIDE_EFFECT),
    )(*parts, *lands, send_sems, recv_sems, *after)
    return outs[:n], outs[n:]


def _pair_share(groups, *, name):
    finals = [f for grp in groups for f in grp]
    n, n_out = len(finals), len(groups)

    def body(*refs):
        ins, outs, scr = refs[:n], refs[n:n + n_out], refs[n + n_out:]
        x, y, c, _ = _place()
        sibling = (x, y, 1 - c)
        t = 0
        for o, grp in enumerate(groups):
            for layer in range(len(grp)):
                _exchange_stream(ins[t], outs[o].at[layer, 1 - c], outs[o].at[layer, c],
                                 scr[CHANNEL_REFS * t:CHANNEL_REFS * (t + 1)], sibling)
                t += 1

    scratch = []
    for f in finals:
        scratch += _channel_scratch(f.shape[1], f.dtype)
    return pl.pallas_call(
        body, in_specs=[ANY] * n, out_specs=[ANY] * n_out,
        out_shape=[SDS((len(grp), 2, *grp[0].shape), grp[0].dtype) for grp in groups],
        scratch_shapes=scratch, compiler_params=pltpu.CompilerParams(vmem_limit_bytes=VMEM_LIMIT), name=name)(*finals)


def _all_reduce_small(v, *, name):
    rows, lanes = v.shape
    n_dev = 8

    def body(v_ref, o_ref, all_ref, send_sems, recv_sems, local_sem):
        x, y, c, chips = _place()
        me, sibling = (x, y, c), (x, y, 1 - c)

        def block(px, py, pc):
            return all_ref.at[4 * px + 2 * py + pc]

        def copy(k, blk, to, src=None):
            return _remote(block(*blk) if src is None else src, block(*blk), send_sems.at[k], recv_sems.at[k], to)

        mine = pltpu.make_async_copy(v_ref, block(*me), local_sem)
        mine.start()
        first = [copy(0, me, sibling, src=v_ref)]
        first += [copy(1 + j, me, (*chip, c), src=v_ref) for j, chip in enumerate(chips)]
        for cp in first:
            cp.start()
        passed = [copy(4 + j, (*chip, c), sibling) for j, chip in enumerate(chips)]
        for j, chip in enumerate(chips):
            copy(1 + j, (*chip, c), me).wait_recv()
            passed[j].start()
        copy(0, sibling, me).wait_recv()
        for j, chip in enumerate(chips):
            copy(4 + j, (*chip, 1 - c), me).wait_recv()
        for cp in first + passed:
            cp.wait_send()
        mine.wait()
        acc = all_ref[0]
        for k in range(1, n_dev):
            acc = acc + all_ref[k]
        o_ref[...] = acc

    vmem = pl.BlockSpec(memory_space=pltpu.VMEM)
    return pl.pallas_call(
        body, in_specs=[vmem], out_specs=vmem, out_shape=SDS((rows, lanes), F32),
        scratch_shapes=[pltpu.VMEM((n_dev, rows, lanes), F32), pltpu.SemaphoreType.DMA((7,)),
                        pltpu.SemaphoreType.DMA((7,)), pltpu.SemaphoreType.DMA],
        compiler_params=pltpu.CompilerParams(vmem_limit_bytes=VMEM_LIMIT), name=name)(v)


def _relu2_epilogue(acc):
    return acc, jnp.square(jnp.maximum(acc, 0.0))


def _res_epilogue(acc, res):
    return (acc + res,)


def _drelu2_epilogue(acc, pre):
    return (acc * (2.0 * jnp.maximum(pre.astype(F32), 0.0)),)


def _ffn_fwd(h, g, w1, w2, tag):
    f = _rms_fwd(h, g, name=f"ffn_norm_{tag}")
    pre, act = _mm_nn(f, w1, name=f"ffn1_{tag}", epilogue=_relu2_epilogue, n_out_dtypes=(BF16, BF16))
    h_out = _mm_nn(act, w2, name=f"ffn2_{tag}", extras=(h,), epilogue=_res_epilogue)
    return h_out, (f, pre, act)


def _ffn_bwd(dh, h, g, w1, w2, saved, layer, after=()):
    f, pre, act = saved
    dpre = _mm_nt(dh, w2, name=f"ffn2_dx_{layer}", out_dtype=BF16, extras=(pre,), epilogue=_drelu2_epilogue,
                  after=after)
    dw2 = _mm_tn_stacked(act, dh, name=f"ffn2_dw_{layer}", col_slots=False)
    df = _mm_nt(dpre, w1, name=f"ffn1_dx_{layer}")
    dw1 = _mm_tn_stacked(f, dpre, name=f"ffn1_dw_{layer}", col_slots=True)
    dh, dg = _rms_bwd(h, g, df, dh, name=f"ffn_norm_bwd_{layer}")
    return dh, dg, dw1, dw2


def _kv_fwd(mem, g, w_kv, tag):
    m = _rms_fwd(mem, g, name=f"mem_norm_{tag}")
    return m, _mm_nn(m, w_kv, name=f"kv_{tag}")


def _kv_bwd(mem, g, w_kv, m, dk, dv, layer):
    dkv = jnp.concatenate([dk, dv], axis=1)
    dw = _mm_tn_stacked(m, dkv, name=f"kv_dw_{layer}", col_slots=True)
    dm = _mm_nt(dkv, w_kv, name=f"kv_dx_{layer}")
    _, dg = _rms_bwd(mem, g, dm, dm, name=f"mem_norm_bwd_{layer}")
    return dw, dg


def _local_step(x, mem, target, p, after_layer1=None):
    row = lambda v: v.reshape(1, -1)
    g = {}

    h0 = x
    a0 = _rms_fwd(h0, row(p["norm_mix"][0]), name="mix_norm_0")
    proj_a = _mm_nn(a0, p["a_in"], name="a_in")
    m0, kv0 = _kv_fwd(mem, row(p["mem_norm"][0]), p["w_kv"][0], "0")
    cat0 = _attn_fwd(proj_a, 2 * D_INNER, kv0, name="attn_0")
    bs_col = p["a_bs"].reshape(A_GROUPS, CHUNK, 1)
    cat0 = _gate_fwd(proj_a, p["a_ln_g"], p["a_ln_b"], p["a_ws"], bs_col, cat0, name="gate")
    h1 = _mm_nn(cat0, p["w_out"][0], name="out_0", extras=(h0,), epilogue=_res_epilogue)
    h2, ffn0 = _ffn_fwd(h1, row(p["norm_ffn"][0]), p["w_ffn1"][0], p["w_ffn2"][0], "0")

    a1 = _rms_fwd(h2, row(p["norm_mix"][1]), name="mix_norm_1")
    proj_b = _mm_nn(a1, p["b_in"], name="b_in")
    m1, kv1 = _kv_fwd(mem, row(p["mem_norm"][1]), p["w_kv"][1], "1")
    cat1 = _attn_fwd(proj_b, B_Q_OFF, kv1, name="attn_1")
    xbc = _conv_fwd(proj_b, p["b_conv_w"], p["b_conv_b"], name="conv")
    dt_raw = proj_b[:, B_DT_OFF:B_DT_OFF + SSM_HEADS].reshape(SEQ, SSM_GROUPS, SSM_HPG)
    dt_c = jnp.transpose(dt_raw, (1, 0, 2))
    dt_r = jnp.transpose(dt_raw, (1, 2, 0))
    per_head = lambda v: v.reshape(SSM_GROUPS, 1, SSM_HPG)
    ssd_par = (per_head(p["b_dt_bias"]), per_head(p["b_a_log"]), per_head(p["b_d"]), p["b_gnorm"])
    cat1, hprev = _ssd_fwd(xbc, proj_b, dt_c, dt_r, *ssd_par, cat1, name="ssd")
    h3 = _mm_nn(cat1, p["w_out"][1], name="out_1", extras=(h2,), epilogue=_res_epilogue)
    h4, ffn1 = _ffn_fwd(h3, row(p["norm_ffn"][1]), p["w_ffn1"][1], p["w_ffn2"][1], "1")

    loss, dh, g["final_norm"] = _loss_head(h4, row(p["final_norm"]), target, name="loss_head")

    dh, dnf1, dw1_1, dw2_1 = _ffn_bwd(dh, h3, row(p["norm_ffn"][1]), p["w_ffn1"][1], p["w_ffn2"][1], ffn1, 1)
    dcat1 = _mm_nt(dh, p["w_out"][1], name="out_dx_1")
    dwo_1 = _mm_tn_stacked(cat1, dh, name="out_dw_1", col_slots=False)
    dproj_b, dk1, dv1 = _attn_bwd(proj_b, B_Q_OFF, kv1, dcat1, B_IN_PAD, B_Q_OFF, name="attn_bwd_1")
    (dproj_b, dxs, dbm, dcm, ddt_c, ddt_r, g["b_dt_bias"], g["b_a_log"], g["b_d"], g["b_gnorm"]) = _ssd_bwd(
        xbc, proj_b, dt_c, dt_r, *ssd_par, hprev, dcat1, dproj_b, name="ssd_bwd")
    dproj_b, g["b_conv_w"], g["b_conv_b"] = _conv_bwd(proj_b, p["b_conv_w"], p["b_conv_b"], dxs, dbm, dcm, dproj_b,
                                                      name="conv_bwd")
    ddt = jnp.transpose(ddt_c, (1, 0, 2)) + jnp.transpose(ddt_r, (2, 0, 1))
    ddt = jnp.pad(ddt.reshape(SEQ, SSM_HEADS), ((0, 0), (0, B_IN_PAD - B_DT_OFF - SSM_HEADS))).astype(BF16)
    dproj_b = lax.dynamic_update_slice(dproj_b, ddt, (0, B_DT_OFF))
    dwkv_1, dmn1 = _kv_bwd(mem, row(p["mem_norm"][1]), p["w_kv"][1], m1, dk1, dv1, 1)
    dwb = _b_in_grad_slots(_mm_tn(a1, dproj_b, name="b_in_dw"))
    da1 = _mm_nt(dproj_b, p["b_in"], name="b_in_dx")
    dh, dnm1 = _rms_bwd(h2, row(p["norm_mix"][1]), da1, dh, name="mix_norm_bwd_1")
    layer1 = dict(w_kv=dwkv_1, w_out=dwo_1, w_ffn1=dw1_1, w_ffn2=dw2_1, b_in=dwb)
    token = () if after_layer1 is None else (after_layer1(layer1),)

    dh, dnf0, dw1_0, dw2_0 = _ffn_bwd(dh, h1, row(p["norm_ffn"][0]), p["w_ffn1"][0], p["w_ffn2"][0], ffn0, 0,
                                      after=token)
    dcat0 = _mm_nt(dh, p["w_out"][0], name="out_dx_0")
    dwo_0 = _mm_tn_stacked(cat0, dh, name="out_dw_0", col_slots=False)
    dproj_a, dk0, dv0 = _attn_bwd(proj_a, 2 * D_INNER, kv0, dcat0, A_IN, 2 * D_INNER, name="attn_bwd_0")
    dproj_a, g["a_ln_g"], g["a_ln_b"], g["a_ws"], dbs_col = _gate_bwd(
        proj_a, p["a_ln_g"], p["a_ln_b"], p["a_ws"], bs_col, dcat0, dproj_a, name="gate_bwd")
    g["a_bs"] = dbs_col.reshape(A_GROUPS, CHUNK)
    dwkv_0, dmn0 = _kv_bwd(mem, row(p["mem_norm"][0]), p["w_kv"][0], m0, dk0, dv0, 0)
    dwa = _mm_tn_stacked(a0, dproj_a, name="a_in_dw", col_slots=True)
    da0 = _mm_nt(dproj_a, p["a_in"], name="a_in_dx")
    dx, dnm0 = _rms_bwd(h0, row(p["norm_mix"][0]), da0, dh, name="mix_norm_bwd_0")

    g["norm_mix"] = jnp.concatenate([dnm0, dnm1], axis=0)
    g["norm_ffn"] = jnp.concatenate([dnf0, dnf1], axis=0)
    g["mem_norm"] = jnp.concatenate([dmn0, dmn1], axis=0)
    layer0 = dict(w_kv=dwkv_0, w_out=dwo_0, w_ffn1=dw1_0, w_ffn2=dw2_0, a_in=dwa)
    return loss, dx, g, layer0, layer1


def _b_in_full(gathered):
    full = jnp.transpose(gathered, (1, 0, 2)).reshape(D_MODEL, B_IN)
    dt0 = D_INNER + CONV_DIM
    return jnp.concatenate([full[:, :dt0], full[:, dt0 + SSM_HEADS:], full[:, dt0:dt0 + SSM_HEADS],
                            jnp.zeros((D_MODEL, B_IN_PAD - B_IN), full.dtype)], axis=1)


def _b_in_grad_slots(d):
    dt0 = D_INNER + CONV_DIM
    full = jnp.concatenate([d[:, :dt0], d[:, B_DT_OFF:B_DT_OFF + SSM_HEADS], d[:, dt0:B_DT_OFF]], axis=1)
    return jnp.transpose(full.reshape(2, D_MODEL // 2, N_CHIPS, B_IN // N_CHIPS), (0, 2, 1, 3))


LARGE = ("w_kv", "w_out", "w_ffn1", "w_ffn2", "a_in", "b_in")
SMALL_REPL = ("norm_mix", "norm_ffn", "mem_norm", "a_ln_g", "a_ln_b", "a_ws", "a_bs", "b_dt_bias", "b_a_log", "b_d",
              "final_norm")
SMALL_SHARD = ("b_conv_w", "b_conv_b", "b_gnorm")
WEIGHTS = ("norm_mix", "norm_ffn", "mem_norm", "w_kv", "w_out", "w_ffn1", "w_ffn2", "a_in", "a_ln_g", "a_ln_b", "a_ws",
           "a_bs", "b_in", "b_conv_w", "b_conv_b", "b_dt_bias", "b_a_log", "b_d", "b_gnorm", "final_norm")
CONV_SHARD = CONV_DIM // N_CHIPS
GN_SHARD = D_INNER // N_CHIPS


LAYERED = ("w_kv", "w_out", "w_ffn1", "w_ffn2")
LAYER_TENSORS = (("w_kv", "w_out", "w_ffn1", "w_ffn2", "a_in"), ("w_kv", "w_out", "w_ffn1", "w_ffn2", "b_in"))


def _gather_weights(w):
    halves = lambda a: a.reshape(2, a.shape[0] // 2, a.shape[1]).astype(BF16)
    big = [halves(w[k][layer] if k in LAYERED else w[k][0]) for layer in range(2) for k in LAYER_TENSORS[layer]]
    small = jnp.zeros((2, CONV_K, CONV_SHARD), F32)
    small = small.at[0].set(w["b_conv_w"][0])
    small = small.at[1, 0].set(w["b_conv_b"][0])
    small = small.at[1, 1, :GN_SHARD].set(w["b_gnorm"][0])
    gathered = _all_gather_shards(big, small, name="gather_weights")
    got = [dict(zip(LAYER_TENSORS[layer], gathered[5 * layer:5 * layer + 5])) for layer in range(2)]
    slots = lambda a: a.reshape(N_CHIPS, -1, a.shape[-1])
    rows = lambda a: a.reshape(-1, a.shape[-1])
    p = {}
    p["w_kv"] = [slots(got[layer]["w_kv"]) for layer in range(2)]
    p["w_out"] = [rows(got[layer]["w_out"]) for layer in range(2)]
    p["w_ffn1"] = [slots(got[layer]["w_ffn1"]) for layer in range(2)]
    p["w_ffn2"] = [rows(got[layer]["w_ffn2"]) for layer in range(2)]
    p["a_in"] = slots(got[0]["a_in"])
    p["b_in"] = _b_in_full(slots(got[1]["b_in"]))
    sm = gathered[-1]
    p["b_conv_w"] = jnp.transpose(sm[:, 0], (1, 0, 2)).reshape(CONV_K, CONV_DIM)
    p["b_conv_b"] = sm[:, 1, 0].reshape(1, CONV_DIM)
    p["b_gnorm"] = sm[:, 1, 1, :GN_SHARD].reshape(1, D_INNER)
    return p


def _pair_parts(grads, layer):
    stacks = [grads[k].reshape(2, -1, grads[k].shape[-1]) for k in LAYER_TENSORS[layer]]
    parts = _pair_reduce(stacks, name=f"grads_pair_reduce_{layer}")
    return [t.reshape(N_CHIPS, -1, t.shape[-1]) for t in parts]


def _chip_sums(chip, parts, landed, layer):
    return {k: _sum_contributions(chip, t, u, name=f"grads_chip_sum_{k}_{layer}")
            for k, t, u in zip(LAYER_TENSORS[layer], parts, landed)}


def _small_layout(shapes):
    offs, o = {}, 0
    for k in (*SMALL_REPL, *SMALL_SHARD):
        size = math.prod(shapes[k])
        offs[k] = (o, size)
        o += size
    rows = -(-o // (8 * 128)) * 8
    return offs, rows


def _reduce_small(g, full_shapes):
    offs, rows = _small_layout(full_shapes)
    flat = jnp.concatenate([g[k].reshape(-1) for k in (*SMALL_REPL, *SMALL_SHARD)])
    flat = jnp.pad(flat, (0, rows * 128 - flat.shape[0])).reshape(rows, 128)
    total = _all_reduce_small(flat, name="grads_small_all_reduce").reshape(-1)
    return {k: total[o:o + n].reshape(full_shapes[k]) for k, (o, n) in offs.items()}


def kernel(x, mem, norm_mix, norm_ffn, mem_norm, w_kv, w_out, w_ffn1, w_ffn2, a_in, a_ln_g, a_ln_b, a_ws, a_bs, b_in, b_conv_w, b_conv_b, b_dt_bias, b_a_log, b_d, b_gnorm, final_norm, loss_target, m_norm_mix, m_norm_ffn, m_mem_norm, m_w_kv, m_w_out, m_w_ffn1, m_w_ffn2, m_a_in, m_a_ln_g, m_a_ln_b, m_a_ws, m_a_bs, m_b_in, m_b_conv_w, m_b_conv_b, m_b_dt_bias, m_b_a_log, m_b_d, m_b_gnorm, m_final_norm, v_norm_mix, v_norm_ffn, v_mem_norm, v_w_kv, v_w_out, v_w_ffn1, v_w_ffn2, v_a_in, v_a_ln_g, v_a_ln_b, v_a_ws, v_a_bs, v_b_in, v_b_conv_w, v_b_conv_b, v_b_dt_bias, v_b_a_log, v_b_d, v_b_gnorm, v_final_norm):
    w = dict(norm_mix=norm_mix, norm_ffn=norm_ffn, mem_norm=mem_norm, w_kv=w_kv, w_out=w_out, w_ffn1=w_ffn1,
             w_ffn2=w_ffn2, a_in=a_in, a_ln_g=a_ln_g, a_ln_b=a_ln_b, a_ws=a_ws, a_bs=a_bs, b_in=b_in, b_conv_w=b_conv_w,
             b_conv_b=b_conv_b, b_dt_bias=b_dt_bias, b_a_log=b_a_log, b_d=b_d, b_gnorm=b_gnorm, final_norm=final_norm)
    mom = dict(norm_mix=m_norm_mix, norm_ffn=m_norm_ffn, mem_norm=m_mem_norm, w_kv=m_w_kv, w_out=m_w_out,
               w_ffn1=m_w_ffn1, w_ffn2=m_w_ffn2, a_in=m_a_in, a_ln_g=m_a_ln_g, a_ln_b=m_a_ln_b, a_ws=m_a_ws,
               a_bs=m_a_bs, b_in=m_b_in, b_conv_w=m_b_conv_w, b_conv_b=m_b_conv_b, b_dt_bias=m_b_dt_bias,
               b_a_log=m_b_a_log, b_d=m_b_d, b_gnorm=m_b_gnorm, final_norm=m_final_norm)
    var = dict(norm_mix=v_norm_mix, norm_ffn=v_norm_ffn, mem_norm=v_mem_norm, w_kv=v_w_kv, w_out=v_w_out,
               w_ffn1=v_w_ffn1, w_ffn2=v_w_ffn2, a_in=v_a_in, a_ln_g=v_a_ln_g, a_ln_b=v_a_ln_b, a_ws=v_a_ws,
               a_bs=v_a_bs, b_in=v_b_in, b_conv_w=v_b_conv_w, b_conv_b=v_b_conv_b, b_dt_bias=v_b_dt_bias,
               b_a_log=v_b_a_log, b_d=v_b_d, b_gnorm=v_b_gnorm, final_norm=v_final_norm)

    p = _gather_weights(w)
    p.update(norm_mix=norm_mix, norm_ffn=norm_ffn, mem_norm=mem_norm, a_ln_g=a_ln_g, a_ln_b=a_ln_b, a_ws=a_ws[0],
             a_bs=a_bs[0], b_dt_bias=b_dt_bias, b_a_log=b_a_log, b_d=b_d, final_norm=final_norm)
    chip = 2 * lax.axis_index("x") + lax.axis_index("y")
    chip_arr = jnp.reshape(chip, (1,)).astype(jnp.int32)
    started = []

    def reduce_layer1(layer1):
        started.extend(_chip_scatter_start(_pair_parts(layer1, 1), name="grads_chip_scatter_start_1"))
        return started[-1]

    loss_part, dx, g, layer0, _ = _local_step(x[0], mem[0], loss_target[0], p, reduce_layer1)
    loss = lax.psum(loss_part[0, 0], ("x", "y", "c"))

    full_shapes = {k: w[k].shape for k in SMALL_REPL}
    full_shapes.update(b_conv_w=(1, CONV_K, CONV_DIM), b_conv_b=(1, CONV_DIM), b_gnorm=(1, D_INNER))
    gs = _reduce_small(g, full_shapes)
    gs["b_conv_w"] = lax.dynamic_slice_in_dim(gs["b_conv_w"], chip * CONV_SHARD, CONV_SHARD, axis=2)
    gs["b_conv_b"] = lax.dynamic_slice_in_dim(gs["b_conv_b"], chip * CONV_SHARD, CONV_SHARD, axis=1)
    gs["b_gnorm"] = lax.dynamic_slice_in_dim(gs["b_gnorm"], chip * GN_SHARD, GN_SHARD, axis=1)
    parts1, landed1 = _chip_scatter_wait(*started[:4], (dx,), name="grads_chip_scatter_wait_1")
    parts0 = _pair_parts(layer0, 0)
    landed0 = _chip_scatter(parts0, name="grads_chip_scatter_0")
    halves = [_chip_sums(chip_arr, parts0, landed0, 0), _chip_sums(chip_arr, parts1, landed1, 1)]
    groups = [[halves[layer][k] for layer in range(2) if k in halves[layer]] for k in LARGE]
    gl = dict(zip(LARGE, _pair_share(groups, name="grads_pair_share")))
    grads = {k: (gl[k].reshape(w[k].shape) if k in gl else gs[k]) for k in WEIGHTS}

    delta, new_m, new_v = {}, {}, {}
    for k in WEIGHTS:
        shape = w[k].shape
        flat = (lambda a: a.reshape(-1, shape[-1])) if len(shape) > 1 else (lambda a: a.reshape(1, -1))
        d, m_new, v_new = _adamw(flat(w[k]), flat(grads[k]), flat(mom[k]), flat(var[k]), name=f"adamw_{k}")
        delta[k], new_m[k], new_v[k] = d.reshape(shape), m_new.reshape(shape), v_new.reshape(shape)

    return (loss, dx.reshape(x.shape), *[grads[k] for k in WEIGHTS], *[delta[k] for k in WEIGHTS],
            *[new_m[k] for k in WEIGHTS], *[new_v[k] for k in WEIGHTS])
```

```python
import math

import jax
import jax.numpy as jnp
from jax import lax
from jax.experimental import pallas as pl
from jax.experimental.pallas import tpu as pltpu

F32 = jnp.float32
BF16 = jnp.bfloat16
SDS = jax.ShapeDtypeStruct

D_MODEL = 1024
SEQ = 2048
CHUNK = 128
N_MEM = 256
D_INNER = 2048
A_GROUPS = 8
A_GROUP_W = D_INNER // A_GROUPS
SSM_HEADS = 32
SSM_HEAD_DIM = 64
SSM_GROUPS = 4
SSM_HPG = 8
SSM_STATE = 128
SSM_GROUP_W = SSM_HPG * SSM_HEAD_DIM
CONV_K = 4
CONV_DIM = 3072
X_HEADS = 4
X_HEAD_DIM = 256
X_WIDTH = 1024
MIX_OUT = 3072
D_FF = 4096
A_IN = 5120
B_IN = 6176
B_IN_PAD = 6272
B_Q_OFF = 5120
B_DT_OFF = 6144
N_CHUNKS = SEQ // CHUNK
EPS = 1e-6
N_CHIPS = 4

ADAM_LR = 0.001
ADAM_B1 = 0.9
ADAM_B2 = 0.999
ADAM_EPS = 1e-08
ADAM_WD = 0.01
ADAM_STEP = 10

VMEM_LIMIT = 48 * 1024 * 1024
MESH = pl.DeviceIdType.MESH


def _cparams(sem):
    return pltpu.CompilerParams(dimension_semantics=sem, vmem_limit_bytes=VMEM_LIMIT)


def _dot(a, b, dims=(((1,), (0,)), ((), ()))):
    return lax.dot_general(a.astype(BF16), b.astype(BF16), dims, preferred_element_type=F32)


def _dot_nt(a, b):
    return _dot(a, b, (((1,), (1,)), ((), ())))


def _dot_tn(a, b):
    return _dot(a, b, (((0,), (0,)), ((), ())))


def _pick(n, cands):
    for c in cands:
        if n % c == 0:
            return c
    raise ValueError(f"no tile for {n}")


def _mm_call(a, b, *, dims, grid, a_spec, b_spec, acc_shape, out_shapes, out_specs, name,
             extras=(), extra_specs=(), epilogue=None, after=()):
    n_k = grid[2]
    n_extra = len(extras)
    n_out = len(out_shapes)
    n_in = 2 + n_extra + len(after)

    def body(*refs):
        a_ref, b_ref = refs[0], refs[1]
        extra_refs = refs[2:2 + n_extra]
        out_refs = refs[n_in:n_in + n_out]
        acc = refs[-1]
        k = pl.program_id(2)

        @pl.when(k == 0)
        def _():
            acc[...] = jnp.zeros_like(acc)

        acc[...] += _dot(a_ref[...], b_ref[...], dims)

        @pl.when(k == n_k - 1)
        def _():
            vals = (acc[...],) if epilogue is None else epilogue(acc[...], *[e[...] for e in extra_refs])
            for o_ref, v in zip(out_refs, vals):
                o_ref[...] = v.astype(o_ref.dtype)

    return pl.pallas_call(
        body, grid=grid, in_specs=[a_spec, b_spec, *extra_specs, *([ANY] * len(after))], out_specs=list(out_specs),
        out_shape=list(out_shapes), scratch_shapes=[pltpu.VMEM(acc_shape, F32)],
        compiler_params=_cparams(("parallel", "parallel", "arbitrary")), name=name,
    )(a, b, *extras, *after)


def _w_dims(w):
    if w.ndim == 2:
        return w.shape[0], w.shape[1], 1, w.shape[1]
    return w.shape[1], w.shape[0] * w.shape[2], w.shape[0], w.shape[2]


def _mm_nn(a, w, *, name, out_dtype=F32, a_cols=None, extras=(), epilogue=None, n_out_dtypes=None):
    m = a.shape[0]
    k_dim, n_dim, _, n_slot = _w_dims(w)
    a_off, a_w = (0, a.shape[1]) if a_cols is None else a_cols
    assert a_w == k_dim
    tm = _pick(m, (2048, 1024, 512, 256))
    tn = _pick(n_slot, (512, 896, 640, 256, 128))
    tk = _pick(k_dim, (1024, 768, 512, 384, 256, 128))
    assert a_off % tk == 0
    nb = n_slot // tn
    a_spec = pl.BlockSpec((tm, tk), lambda i, j, k: (i, a_off // tk + k))
    if w.ndim == 2:
        b_spec = pl.BlockSpec((tk, tn), lambda i, j, k: (k, j))
    else:
        b_spec = pl.BlockSpec((None, tk, tn), lambda i, j, k: (j // nb, k, j % nb))
    o_spec = pl.BlockSpec((tm, tn), lambda i, j, k: (i, j))
    dts = n_out_dtypes or (out_dtype,)
    outs = _mm_call(a, w, dims=(((1,), (0,)), ((), ())), grid=(m // tm, n_dim // tn, k_dim // tk),
                    a_spec=a_spec, b_spec=b_spec, acc_shape=(tm, tn),
                    out_shapes=[SDS((m, n_dim), dt) for dt in dts], out_specs=[o_spec] * len(dts), name=name,
                    extras=extras, extra_specs=[o_spec] * len(extras), epilogue=epilogue)
    return outs if n_out_dtypes else outs[0]


def _mm_nt(a, w, *, name, out_dtype=F32, extras=(), epilogue=None, after=()):
    m = a.shape[0]
    k_dim, n_dim, _, n_slot = _w_dims(w)
    assert a.shape[1] == n_dim
    tm = _pick(m, (2048, 1024, 512, 256))
    to = _pick(k_dim, (512, 384, 256, 128))
    tc = _pick(n_slot, (1024, 896, 640, 512, 256, 128))
    nb = n_slot // tc
    a_spec = pl.BlockSpec((tm, tc), lambda i, j, k: (i, k))
    if w.ndim == 2:
        b_spec = pl.BlockSpec((to, tc), lambda i, j, k: (j, k))
    else:
        b_spec = pl.BlockSpec((None, to, tc), lambda i, j, k: (k // nb, j, k % nb))
    o_spec = pl.BlockSpec((tm, to), lambda i, j, k: (i, j))
    return _mm_call(a, w, dims=(((1,), (1,)), ((), ())), grid=(m // tm, k_dim // to, n_dim // tc),
                    a_spec=a_spec, b_spec=b_spec, acc_shape=(tm, to),
                    out_shapes=[SDS((m, k_dim), out_dtype)], out_specs=[o_spec], name=name,
                    extras=extras, extra_specs=[o_spec] * len(extras), epilogue=epilogue, after=after)[0]


def _mm_tn(x, dy, *, name, x_cols=None):
    s = x.shape[0]
    x_off, k_dim = (0, x.shape[1]) if x_cols is None else x_cols
    n_dim = dy.shape[1]
    tm = _pick(k_dim, (1024, 768, 512, 384, 256, 128))
    tn = _pick(n_dim, (512, 896, 640, 256, 128))
    tk = _pick(s, (2048, 1024, 512, 256))
    assert x_off % tm == 0
    a_spec = pl.BlockSpec((tk, tm), lambda i, j, k: (k, x_off // tm + i))
    b_spec = pl.BlockSpec((tk, tn), lambda i, j, k: (k, j))
    o_spec = pl.BlockSpec((tm, tn), lambda i, j, k: (i, j))
    return _mm_call(x, dy, dims=(((0,), (0,)), ((), ())), grid=(k_dim // tm, n_dim // tn, s // tk),
                    a_spec=a_spec, b_spec=b_spec, acc_shape=(tm, tn),
                    out_shapes=[SDS((k_dim, n_dim), F32)], out_specs=[o_spec], name=name)[0]


def _mm_tn_stacked(x, dy, *, name, col_slots):
    s, k_dim = x.shape
    n_dim = dy.shape[1]
    r, c = (k_dim // 2, n_dim // N_CHIPS) if col_slots else (k_dim // N_CHIPS // 2, n_dim)
    tm = _pick(r, (512, 384, 256, 128))
    tn = _pick(c, (512, 896, 640, 256, 128))
    tk = _pick(s, (2048, 1024, 512, 256))
    a_spec = pl.BlockSpec((tk, tm), lambda i, j, k: (k, i))
    b_spec = pl.BlockSpec((tk, tn), lambda i, j, k: (k, j))
    rb = r // tm
    if col_slots:
        nb = c // tn
        o_spec = pl.BlockSpec((None, None, tm, tn), lambda i, j, k: (i // rb, j // nb, i % rb, j % nb))
    else:
        o_spec = pl.BlockSpec((None, None, tm, tn), lambda i, j, k: ((i // rb) % 2, i // (2 * rb), i % rb, j))
    return _mm_call(x, dy, dims=(((0,), (0,)), ((), ())), grid=(k_dim // tm, n_dim // tn, s // tk),
                    a_spec=a_spec, b_spec=b_spec, acc_shape=(tm, tn),
                    out_shapes=[SDS((2, N_CHIPS, r, c), F32)], out_specs=[o_spec], name=name)[0]


def _rms(x, g):
    return x * lax.rsqrt(jnp.mean(x * x, axis=-1, keepdims=True) + EPS) * g


def _rms_fwd(h, g, *, name):
    rows, d = h.shape
    tr = _pick(rows, (512, 256))

    def body(h_ref, g_ref, o_ref):
        o_ref[...] = _rms(h_ref[...], g_ref[...]).astype(o_ref.dtype)

    return pl.pallas_call(
        body, grid=(rows // tr,),
        in_specs=[pl.BlockSpec((tr, d), lambda i: (i, 0)), pl.BlockSpec((1, d), lambda i: (0, 0))],
        out_specs=pl.BlockSpec((tr, d), lambda i: (i, 0)), out_shape=SDS((rows, d), BF16),
        compiler_params=_cparams(("parallel",)), name=name)(h, g)


def _rms_bwd(h, g, da, dres, *, name):
    rows, d = h.shape
    tr = _pick(rows, (512, 256))

    def body(h_ref, g_ref, da_ref, dres_ref, dh_ref, dg_ref):
        _, vjp = jax.vjp(_rms, h_ref[...], g_ref[...])
        dh, dg = vjp(da_ref[...].astype(F32))
        dh_ref[...] = dres_ref[...] + dh

        @pl.when(pl.program_id(0) == 0)
        def _():
            dg_ref[...] = jnp.zeros_like(dg_ref)

        dg_ref[...] += dg

    row_spec = pl.BlockSpec((tr, d), lambda i: (i, 0))
    vec_spec = pl.BlockSpec((1, d), lambda i: (0, 0))
    return pl.pallas_call(
        body, grid=(rows // tr,), in_specs=[row_spec, vec_spec, row_spec, row_spec],
        out_specs=[row_spec, vec_spec], out_shape=[SDS((rows, d), F32), SDS((1, d), F32)],
        compiler_params=_cparams(("arbitrary",)), name=name)(h, g, da, dres)


def _loss_head(h, g, target, *, name):
    rows, d = h.shape
    tr = _pick(rows, (512, 256))

    def body(h_ref, g_ref, t_ref, loss_ref, dh_ref, dg_ref):
        y, vjp = jax.vjp(_rms, h_ref[...], g_ref[...])
        err = y - t_ref[...]
        dh, dg = vjp(err * (1.0 / d))
        dh_ref[...] = dh

        @pl.when(pl.program_id(0) == 0)
        def _():
            dg_ref[...] = jnp.zeros_like(dg_ref)
            loss_ref[...] = jnp.zeros_like(loss_ref)

        dg_ref[...] += dg
        part = jnp.sum(jnp.sum(err * err, axis=-1, keepdims=True), axis=0, keepdims=True) * (0.5 / d)
        loss_ref[...] += jnp.broadcast_to(part, loss_ref.shape)

    row_spec = pl.BlockSpec((tr, d), lambda i: (i, 0))
    vec_spec = pl.BlockSpec((1, d), lambda i: (0, 0))
    loss_spec = pl.BlockSpec((8, 128), lambda i: (0, 0))
    return pl.pallas_call(
        body, grid=(rows // tr,), in_specs=[row_spec, vec_spec, row_spec],
        out_specs=[loss_spec, row_spec, vec_spec],
        out_shape=[SDS((8, 128), F32), SDS((rows, d), F32), SDS((1, d), F32)],
        compiler_params=_cparams(("arbitrary",)), name=name)(h, g, target)


def _gelu(x):
    return 0.5 * x * (1.0 + lax.erf(x * (1.0 / math.sqrt(2.0))))


def _gate_tile(pu, pv, ln_g, ln_b, ws, bs_t):
    u = [_gelu(p) for p in pu]
    v = [_gelu(p) for p in pv]
    mu = sum(jnp.sum(t, axis=-1, keepdims=True) for t in v) * (1.0 / D_INNER)
    vc = [t - mu for t in v]
    var = sum(jnp.sum(t * t, axis=-1, keepdims=True) for t in vc) * (1.0 / D_INNER)
    rstd = lax.rsqrt(var + EPS)
    row = lax.broadcasted_iota(jnp.int32, (CHUNK, CHUNK), 0)
    col = lax.broadcasted_iota(jnp.int32, (CHUNK, CHUNK), 1)
    out = []
    for gi in range(A_GROUPS):
        vn = vc[gi] * rstd * ln_g[gi] + ln_b[gi]
        w = jnp.where(row >= col, ws[gi], 0.0)
        sv = _dot(w, vn) + bs_t[gi]
        out.append(u[gi] * sv)
    return out


def _split(ref, n, width):
    return [ref[:, i * width:(i + 1) * width] for i in range(n)]


def _gate_in_specs():
    return [
        pl.BlockSpec((CHUNK, D_INNER), lambda c: (c, 0)),
        pl.BlockSpec((CHUNK, D_INNER), lambda c: (c, 1)),
        pl.BlockSpec((1, D_INNER), lambda c: (0, 0)),
        pl.BlockSpec((1, D_INNER), lambda c: (0, 0)),
        pl.BlockSpec((A_GROUPS, CHUNK, CHUNK), lambda c: (0, 0, 0)),
        pl.BlockSpec((A_GROUPS, CHUNK, 1), lambda c: (0, 0, 0)),
    ]


def _gate_args(u_ref, v_ref, g_ref, b_ref, ws_ref, bs_ref):
    ng, gw = A_GROUPS, A_GROUP_W
    return (_split(u_ref, ng, gw), _split(v_ref, ng, gw), _split(g_ref, ng, gw), _split(b_ref, ng, gw),
            [ws_ref[i] for i in range(ng)], [bs_ref[i] for i in range(ng)])


def _gate_fwd(proj, ln_g, ln_b, ws, bs_col, mixcat, *, name):
    def body(u_ref, v_ref, g_ref, b_ref, ws_ref, bs_ref, cat_in, cat_ref):
        del cat_in
        out = _gate_tile(*_gate_args(u_ref, v_ref, g_ref, b_ref, ws_ref, bs_ref))
        for gi, o in enumerate(out):
            cat_ref[:, gi * A_GROUP_W:(gi + 1) * A_GROUP_W] = o.astype(cat_ref.dtype)

    return pl.pallas_call(
        body, grid=(N_CHUNKS,), in_specs=[*_gate_in_specs(), pl.BlockSpec(memory_space=pl.ANY)],
        out_specs=pl.BlockSpec((CHUNK, D_INNER), lambda c: (c, 0)), out_shape=SDS(mixcat.shape, mixcat.dtype),
        input_output_aliases={6: 0}, compiler_params=_cparams(("parallel",)), name=name,
    )(proj, proj, ln_g, ln_b, ws, bs_col, mixcat)


def _gate_bwd(proj, ln_g, ln_b, ws, bs_col, dcat, dproj, *, name):
    ng, gw = A_GROUPS, A_GROUP_W

    def body(u_ref, v_ref, g_ref, b_ref, ws_ref, bs_ref, d_ref, dproj_in, dproj_ref, dg_ref, db_ref, dws_ref, dbs_ref):
        del dproj_in
        args = _gate_args(u_ref, v_ref, g_ref, b_ref, ws_ref, bs_ref)
        _, vjp = jax.vjp(_gate_tile, *args)
        dpu, dpv, dg, db, dws, dbs = vjp(_split(d_ref, ng, gw))
        for gi in range(ng):
            dproj_ref[:, gi * gw:(gi + 1) * gw] = dpu[gi].astype(dproj_ref.dtype)
            dproj_ref[:, D_INNER + gi * gw:D_INNER + (gi + 1) * gw] = dpv[gi].astype(dproj_ref.dtype)

        @pl.when(pl.program_id(0) == 0)
        def _():
            for r in (dg_ref, db_ref, dws_ref, dbs_ref):
                r[...] = jnp.zeros_like(r)

        for gi in range(ng):
            dg_ref[:, gi * gw:(gi + 1) * gw] += dg[gi]
            db_ref[:, gi * gw:(gi + 1) * gw] += db[gi]
            dws_ref[gi] += dws[gi]
            dbs_ref[gi] += dbs[gi]

    in_specs = _gate_in_specs()
    return pl.pallas_call(
        body, grid=(N_CHUNKS,),
        in_specs=[*in_specs, pl.BlockSpec((CHUNK, D_INNER), lambda c: (c, 0)), pl.BlockSpec(memory_space=pl.ANY)],
        out_specs=[pl.BlockSpec((CHUNK, 2 * D_INNER), lambda c: (c, 0)), *in_specs[2:]],
        out_shape=[SDS(dproj.shape, dproj.dtype), SDS((1, D_INNER), F32), SDS((1, D_INNER), F32),
                   SDS((ng, CHUNK, CHUNK), F32), SDS((ng, CHUNK, 1), F32)],
        input_output_aliases={7: 0}, compiler_params=_cparams(("arbitrary",)), name=name,
    )(proj, proj, ln_g, ln_b, ws, bs_col, dcat, dproj)


ATT_TQ = 512


def _attn_tile(q, k, v):
    s = _dot_nt(q, k) * (1.0 / math.sqrt(X_HEAD_DIM))
    s = s - jnp.max(s, axis=-1, keepdims=True)
    e = jnp.exp(s)
    p = e / jnp.sum(e, axis=-1, keepdims=True)
    return _dot(p, v)


def _attn_in_specs(q_blk, order):
    hd = X_HEAD_DIM
    return [
        pl.BlockSpec((ATT_TQ, hd), lambda a, b: (order(a, b)[0], q_blk + order(a, b)[1])),
        pl.BlockSpec((N_MEM, hd), lambda a, b: (0, order(a, b)[1])),
        pl.BlockSpec((N_MEM, hd), lambda a, b: (0, X_HEADS + order(a, b)[1])),
    ]


def _attn_fwd(proj, q_off, kv, *, name):
    order = lambda i, h: (i, h)
    cat_blk = D_INNER // X_HEAD_DIM

    def body(q_ref, k_ref, v_ref, o_ref):
        o_ref[...] = _attn_tile(q_ref[...], k_ref[...], v_ref[...]).astype(o_ref.dtype)

    return pl.pallas_call(
        body, grid=(SEQ // ATT_TQ, X_HEADS), in_specs=_attn_in_specs(q_off // X_HEAD_DIM, order),
        out_specs=pl.BlockSpec((ATT_TQ, X_HEAD_DIM), lambda i, h: (i, cat_blk + h)),
        out_shape=SDS((SEQ, MIX_OUT), BF16), compiler_params=_cparams(("parallel", "parallel")), name=name,
    )(proj, kv, kv)


def _attn_bwd(proj, q_off, kv, dcat, dproj_width, dq_off, *, name):
    order = lambda h, i: (i, h)
    cat_blk = D_INNER // X_HEAD_DIM
    dq_blk = dq_off // X_HEAD_DIM

    def body(q_ref, k_ref, v_ref, do_ref, dq_ref, dk_ref, dv_ref):
        _, vjp = jax.vjp(_attn_tile, q_ref[...], k_ref[...], v_ref[...])
        dq, dk, dv = vjp(do_ref[...])
        dq_ref[...] = dq.astype(dq_ref.dtype)

        @pl.when(pl.program_id(1) == 0)
        def _():
            dk_ref[...] = jnp.zeros_like(dk_ref)
            dv_ref[...] = jnp.zeros_like(dv_ref)

        dk_ref[...] += dk
        dv_ref[...] += dv

    kv_spec = pl.BlockSpec((N_MEM, X_HEAD_DIM), lambda h, i: (0, h))
    return pl.pallas_call(
        body, grid=(X_HEADS, SEQ // ATT_TQ),
        in_specs=[*_attn_in_specs(q_off // X_HEAD_DIM, order),
                  pl.BlockSpec((ATT_TQ, X_HEAD_DIM), lambda h, i: (i, cat_blk + h))],
        out_specs=[pl.BlockSpec((ATT_TQ, X_HEAD_DIM), lambda h, i: (i, dq_blk + h)), kv_spec, kv_spec],
        out_shape=[SDS((SEQ, dproj_width), BF16), SDS((N_MEM, X_WIDTH), F32), SDS((N_MEM, X_WIDTH), F32)],
        compiler_params=_cparams(("parallel", "arbitrary")), name=name,
    )(proj, kv, kv, dcat)


CONV_TC = 512


def _shift_down(x, s):
    if s == 0:
        return x
    row = lax.broadcasted_iota(jnp.int32, x.shape, 0)
    return jnp.where(row >= s, pltpu.roll(x, s, 0), 0.0)


def _shift_up(x, s):
    if s == 0:
        return x
    n = x.shape[0]
    row = lax.broadcasted_iota(jnp.int32, x.shape, 0)
    return jnp.where(row < n - s, pltpu.roll(x, n - s, 0), 0.0)


def _conv_pre(x, w_ref, b_ref):
    pre = b_ref[...] + jnp.zeros_like(x)
    for k in range(CONV_K):
        pre = pre + w_ref[k:k + 1, :] * _shift_down(x, CONV_K - 1 - k)
    return pre


def _conv_fwd(proj, w, b, *, name):
    blk0 = D_INNER // CONV_TC

    def body(x_ref, w_ref, b_ref, o_ref):
        pre = _conv_pre(x_ref[...], w_ref, b_ref)
        o_ref[...] = pre * jax.nn.sigmoid(pre)

    return pl.pallas_call(
        body, grid=(CONV_DIM // CONV_TC,),
        in_specs=[pl.BlockSpec((SEQ, CONV_TC), lambda j: (0, blk0 + j)), pl.BlockSpec((CONV_K, CONV_TC), lambda j: (0, j)),
                  pl.BlockSpec((1, CONV_TC), lambda j: (0, j))],
        out_specs=pl.BlockSpec((SEQ, CONV_TC), lambda j: (0, j)), out_shape=SDS((SEQ, CONV_DIM), F32),
        compiler_params=_cparams(("parallel",)), name=name)(proj, w, b)


def _conv_bwd(proj, w, b, dxs, dbm, dcm, dproj, *, name):
    tc = CONV_TC // 2
    blk0 = D_INNER // tc
    n_x = D_INNER // tc
    n_b = SSM_GROUPS * SSM_STATE // tc

    def body(x_ref, w_ref, b_ref, dxs_ref, dbm_ref, dcm_ref, dproj_in, dproj_ref, dw_ref, db_ref):
        del dproj_in
        j = pl.program_id(0)
        x = x_ref[...]
        pre = _conv_pre(x, w_ref, b_ref)
        sg = jax.nn.sigmoid(pre)
        dact = jnp.where(j < n_x, dxs_ref[...], jnp.where(j < n_x + n_b, dbm_ref[...], dcm_ref[...]))
        dpre = dact * (sg * (1.0 + pre * (1.0 - sg)))
        dx = jnp.zeros_like(x)
        for k in range(CONV_K):
            s = CONV_K - 1 - k
            dx = dx + w_ref[k:k + 1, :] * _shift_up(dpre, s)
            dw_ref[k:k + 1, :] = jnp.sum(dpre * _shift_down(x, s), axis=0, keepdims=True)
        dproj_ref[...] = dx.astype(dproj_ref.dtype)
        db_ref[...] = jnp.sum(dpre, axis=0, keepdims=True)

    clip = lambda v, hi: jnp.minimum(jnp.maximum(v, 0), hi)
    return pl.pallas_call(
        body, grid=(CONV_DIM // tc,),
        in_specs=[pl.BlockSpec((SEQ, tc), lambda j: (0, blk0 + j)), pl.BlockSpec((CONV_K, tc), lambda j: (0, j)),
                  pl.BlockSpec((1, tc), lambda j: (0, j)),
                  pl.BlockSpec((SEQ, tc), lambda j: (0, clip(j, n_x - 1))),
                  pl.BlockSpec((SEQ, tc), lambda j: (0, clip(j - n_x, n_b - 1))),
                  pl.BlockSpec((SEQ, tc), lambda j: (0, clip(j - n_x - n_b, n_b - 1))),
                  pl.BlockSpec(memory_space=pl.ANY)],
        out_specs=[pl.BlockSpec((SEQ, tc), lambda j: (0, blk0 + j)), pl.BlockSpec((CONV_K, tc), lambda j: (0, j)),
                   pl.BlockSpec((1, tc), lambda j: (0, j))],
        out_shape=[SDS(dproj.shape, dproj.dtype), SDS((CONV_K, CONV_DIM), F32), SDS((1, CONV_DIM), F32)],
        input_output_aliases={6: 0}, compiler_params=_cparams(("parallel",)), name=name,
    )(proj, w, b, dxs, dbm, dcm, dproj)


SSM_PAIRS = SSM_HPG // 2


def _ssd_tile(xp, zp, bm, cm, hp, dtc, dtr, bias, alog, dsk, gnp):
    row = lax.broadcasted_iota(jnp.int32, (CHUNK, CHUNK), 0)
    col = lax.broadcasted_iota(jnp.int32, (CHUNK, CHUNK), 1)
    causal = row >= col
    tri = jnp.where(causal, 1.0, 0.0)
    left = col < SSM_HEAD_DIM
    top = row < SSM_HEAD_DIM
    ones = jnp.ones((CHUNK, CHUNK), BF16)
    cb = _dot_nt(cm, bm)
    dt_c, cs_c, cs_last, m = [], [], [], []
    for r in range(SSM_HPG):
        a = -jnp.exp(alog[r])
        dt_c.append(jax.nn.softplus(dtc[r] + bias[r]))
        da_c = dt_c[r] * a
        da_r = jax.nn.softplus(dtr[r] + bias[r]) * a
        cs_c.append(jnp.sum(tri * da_r, axis=1, keepdims=True))
        cs_r = jnp.sum(jnp.where(row <= col, 1.0, 0.0) * da_c, axis=0, keepdims=True)
        cs_last.append(jnp.sum(da_c, axis=0, keepdims=True))
        m.append(cb * jnp.exp(jnp.where(causal, cs_c[r] - cs_r, -1e30)))
    ygs, hn = [], []
    for p in range(SSM_PAIRS):
        a, b = 2 * p, 2 * p + 1
        pair = lambda u, v: jnp.where(left, u, v)
        xdt = xp[p] * pair(dt_c[a], dt_c[b])
        y = pair(_dot(m[a], xdt), _dot(m[b], xdt))
        y = y + _dot_nt(cm, hp[p]) * pair(jnp.exp(cs_c[a]), jnp.exp(cs_c[b]))
        y = y + xp[p] * pair(dsk[a], dsk[b])
        decay = pair(jnp.exp(cs_last[a] - cs_c[a]), jnp.exp(cs_last[b] - cs_c[b]))
        states = _dot_tn(xdt * decay, bm)
        hn.append(hp[p] * jnp.where(top, jnp.exp(cs_last[a]), jnp.exp(cs_last[b])) + states)
        ygs.append(y * (zp[p] * jax.nn.sigmoid(zp[p])))
    ms = sum(_dot(t * t, ones) for t in ygs) * (1.0 / SSM_GROUP_W)
    rs = lax.rsqrt(ms + EPS)
    return [ygs[p] * rs * gnp[p] for p in range(SSM_PAIRS)], hn


def _ssd_in_specs(cidx):
    gw, n = SSM_GROUP_W, SSM_STATE
    bm_blk = D_INNER // n
    return [
        pl.BlockSpec((CHUNK, gw), lambda g, c: (cidx(c), g)),
        pl.BlockSpec((CHUNK, gw), lambda g, c: (cidx(c), g)),
        pl.BlockSpec((CHUNK, n), lambda g, c: (cidx(c), bm_blk + g)),
        pl.BlockSpec((CHUNK, n), lambda g, c: (cidx(c), bm_blk + SSM_GROUPS + g)),
        pl.BlockSpec((None, CHUNK, SSM_HPG), lambda g, c: (g, cidx(c), 0)),
        pl.BlockSpec((None, SSM_HPG, CHUNK), lambda g, c: (g, 0, cidx(c))),
        pl.BlockSpec((None, 1, SSM_HPG), lambda g, c: (g, 0, 0)),
        pl.BlockSpec((None, 1, SSM_HPG), lambda g, c: (g, 0, 0)),
        pl.BlockSpec((None, 1, SSM_HPG), lambda g, c: (g, 0, 0)),
        pl.BlockSpec((1, gw), lambda g, c: (0, g)),
    ]


def _ssd_args(x_ref, z_ref, bm_ref, cm_ref, hp, dtc_ref, dtr_ref, bias_ref, alog_ref, dsk_ref, gn_ref):
    nh, npair, w = SSM_HPG, SSM_PAIRS, 2 * SSM_HEAD_DIM
    col = lambda ref: [ref[:, r:r + 1] for r in range(nh)]
    return (_split(x_ref, npair, w), _split(z_ref, npair, w), bm_ref[...], cm_ref[...], hp,
            col(dtc_ref), [dtr_ref[r:r + 1, :] for r in range(nh)], col(bias_ref), col(alog_ref), col(dsk_ref),
            _split(gn_ref, npair, w))


def _pair_rows(ref):
    w = 2 * SSM_HEAD_DIM
    return [ref[p * w:(p + 1) * w, :] for p in range(SSM_PAIRS)]


def _ssd_fwd(xbc, proj, dt_c, dt_r, bias, alog, dsk, gn, mixcat, *, name):
    w = 2 * SSM_HEAD_DIM

    def body(x_ref, z_ref, bm_ref, cm_ref, dtc_ref, dtr_ref, bias_ref, alog_ref, dsk_ref, gn_ref, cat_in,
             cat_ref, hprev_ref, h_scr):
        del cat_in

        @pl.when(pl.program_id(1) == 0)
        def _():
            h_scr[...] = jnp.zeros_like(h_scr)

        hprev_ref[...] = h_scr[...]
        yn, hn = _ssd_tile(*_ssd_args(x_ref, z_ref, bm_ref, cm_ref, _pair_rows(h_scr), dtc_ref, dtr_ref, bias_ref,
                                      alog_ref, dsk_ref, gn_ref))
        for p in range(SSM_PAIRS):
            cat_ref[:, p * w:(p + 1) * w] = yn[p].astype(cat_ref.dtype)
            h_scr[p * w:(p + 1) * w, :] = hn[p]

    return pl.pallas_call(
        body, grid=(SSM_GROUPS, N_CHUNKS), in_specs=[*_ssd_in_specs(lambda c: c), pl.BlockSpec(memory_space=pl.ANY)],
        out_specs=[pl.BlockSpec((CHUNK, SSM_GROUP_W), lambda g, c: (c, g)),
                   pl.BlockSpec((None, None, SSM_GROUP_W, SSM_STATE), lambda g, c: (c, g, 0, 0))],
        out_shape=[SDS(mixcat.shape, mixcat.dtype), SDS((N_CHUNKS, SSM_GROUPS, SSM_GROUP_W, SSM_STATE), F32)],
        scratch_shapes=[pltpu.VMEM((SSM_GROUP_W, SSM_STATE), F32)],
        input_output_aliases={10: 0}, compiler_params=_cparams(("parallel", "arbitrary")), name=name,
    )(xbc, proj, xbc, xbc, dt_c, dt_r, bias, alog, dsk, gn, mixcat)


def _ssd_bwd(xbc, proj, dt_c, dt_r, bias, alog, dsk, gn, hprev, dcat, dproj, *, name):
    nh, w, gw, n = SSM_HPG, 2 * SSM_HEAD_DIM, SSM_GROUP_W, SSM_STATE
    rev = lambda c: N_CHUNKS - 1 - c

    def body(x_ref, z_ref, bm_ref, cm_ref, dtc_ref, dtr_ref, bias_ref, alog_ref, dsk_ref, gn_ref, hprev_ref, dy_ref,
             dproj_in, dz_ref, dxs_ref, dbm_ref, dcm_ref, ddtc_ref, ddtr_ref, dbias_ref, dalog_ref, ddsk_ref, dgn_ref,
             dh_scr):
        del dproj_in
        first = pl.program_id(1) == 0

        @pl.when(first)
        def _():
            dh_scr[...] = jnp.zeros_like(dh_scr)
            for ref in (dbias_ref, dalog_ref, ddsk_ref, dgn_ref):
                ref[...] = jnp.zeros_like(ref)

        args = _ssd_args(x_ref, z_ref, bm_ref, cm_ref, _pair_rows(hprev_ref), dtc_ref, dtr_ref, bias_ref, alog_ref,
                         dsk_ref, gn_ref)
        _, vjp = jax.vjp(_ssd_tile, *args)
        dxs, dzs, dbm, dcm, dhs, ddtc, ddtr, dbias, dalog, ddsk, dgn = vjp(
            (_split(dy_ref, SSM_PAIRS, w), _pair_rows(dh_scr)))
        dbm_ref[...] = dbm
        dcm_ref[...] = dcm
        for q in range(SSM_PAIRS):
            dxs_ref[:, q * w:(q + 1) * w] = dxs[q]
            dz_ref[:, q * w:(q + 1) * w] = dzs[q].astype(dz_ref.dtype)
            dh_scr[q * w:(q + 1) * w, :] = dhs[q]
            dgn_ref[:, q * w:(q + 1) * w] += dgn[q]
        for r in range(nh):
            ddtc_ref[:, r:r + 1] = ddtc[r]
            ddtr_ref[r:r + 1, :] = ddtr[r]
            dbias_ref[:, r:r + 1] += dbias[r]
            dalog_ref[:, r:r + 1] += dalog[r]
            ddsk_ref[:, r:r + 1] += ddsk[r]

    par_spec = pl.BlockSpec((None, 1, nh), lambda g, c: (g, 0, 0))
    return pl.pallas_call(
        body, grid=(SSM_GROUPS, N_CHUNKS),
        in_specs=[*_ssd_in_specs(rev),
                  pl.BlockSpec((None, None, gw, n), lambda g, c: (rev(c), g, 0, 0)),
                  pl.BlockSpec((CHUNK, gw), lambda g, c: (rev(c), g)),
                  pl.BlockSpec(memory_space=pl.ANY)],
        out_specs=[pl.BlockSpec((CHUNK, gw), lambda g, c: (rev(c), g)),
                   pl.BlockSpec((CHUNK, gw), lambda g, c: (rev(c), g)),
                   pl.BlockSpec((CHUNK, n), lambda g, c: (rev(c), g)),
                   pl.BlockSpec((CHUNK, n), lambda g, c: (rev(c), g)),
                   pl.BlockSpec((None, CHUNK, nh), lambda g, c: (g, rev(c), 0)),
                   pl.BlockSpec((None, nh, CHUNK), lambda g, c: (g, 0, rev(c))),
                   par_spec, par_spec, par_spec,
                   pl.BlockSpec((1, gw), lambda g, c: (0, g))],
        out_shape=[SDS(dproj.shape, dproj.dtype), SDS((SEQ, D_INNER), F32), SDS((SEQ, SSM_GROUPS * n), F32),
                   SDS((SEQ, SSM_GROUPS * n), F32), SDS((SSM_GROUPS, SEQ, nh), F32), SDS((SSM_GROUPS, nh, SEQ), F32),
                   SDS((SSM_GROUPS, 1, nh), F32), SDS((SSM_GROUPS, 1, nh), F32), SDS((SSM_GROUPS, 1, nh), F32),
                   SDS((1, D_INNER), F32)],
        scratch_shapes=[pltpu.VMEM((gw, n), F32)],
        input_output_aliases={12: 0}, compiler_params=_cparams(("parallel", "arbitrary")), name=name,
    )(xbc, proj, xbc, xbc, dt_c, dt_r, bias, alog, dsk, gn, hprev, dcat, dproj)


def _sum_contributions(chip, parts, landed, *, name):
    _, r, c = parts.shape
    tr = _pick(r, (256, 384, 128))

    def body(chip_ref, own_ref, landed_ref, o_ref):
        del chip_ref
        acc = own_ref[...].astype(F32)
        for s in range(landed_ref.shape[0]):
            acc = acc + landed_ref[s].astype(F32)
        o_ref[...] = acc

    grid_spec = pltpu.PrefetchScalarGridSpec(
        num_scalar_prefetch=1, grid=(r // tr,),
        in_specs=[pl.BlockSpec((None, tr, c), lambda i, chip_ref: (chip_ref[0], i, 0)),
                  pl.BlockSpec((landed.shape[0], tr, c), lambda i, chip_ref: (0, i, 0))],
        out_specs=pl.BlockSpec((tr, c), lambda i, chip_ref: (i, 0)))
    return pl.pallas_call(body, grid_spec=grid_spec, out_shape=SDS((r, c), F32),
                          compiler_params=_cparams(("parallel",)), name=name)(chip, parts, landed)


def _adamw(w, g, m, v, *, name):
    r, c = w.shape
    tr = r if r <= 256 else _pick(r, (256, 128, 8))
    spec = pl.BlockSpec((tr, c), lambda i: (i, 0))

    def body(w_ref, g_ref, m_ref, v_ref, d_ref, mo_ref, vo_ref):
        g = g_ref[...]
        m_new = ADAM_B1 * m_ref[...] + (1.0 - ADAM_B1) * g
        v_new = ADAM_B2 * v_ref[...] + (1.0 - ADAM_B2) * (g * g)
        m_hat = m_new / (1.0 - ADAM_B1 ** ADAM_STEP)
        v_hat = v_new / (1.0 - ADAM_B2 ** ADAM_STEP)
        d_ref[...] = -ADAM_LR * (m_hat / (jnp.sqrt(v_hat) + ADAM_EPS) + ADAM_WD * w_ref[...])
        mo_ref[...] = m_new
        vo_ref[...] = v_new

    return pl.pallas_call(body, grid=(r // tr,), in_specs=[spec] * 4, out_specs=[spec] * 3,
                          out_shape=[SDS((r, c), F32)] * 3, compiler_params=_cparams(("parallel",)), name=name)(w, g, m, v)


ANY = pl.BlockSpec(memory_space=pl.ANY)


def _place():
    x, y, c = lax.axis_index("x"), lax.axis_index("y"), lax.axis_index("c")
    chips = [(1 - x, y), (x, 1 - y), (1 - x, 1 - y)]
    return x, y, c, chips


def _remote(src, dst, send_sem, recv_sem, to):
    return pltpu.make_async_remote_copy(src_ref=src, dst_ref=dst, send_sem=send_sem, recv_sem=recv_sem,
                                        device_id=to, device_id_type=MESH)


STREAM_ROWS = 128


def _stream_rows(i):
    return pl.ds(pl.multiple_of(i * STREAM_ROWS, STREAM_ROWS), STREAM_ROWS)


def _channel_scratch(width, dtype):
    buf = (2, STREAM_ROWS, width)
    return [pltpu.VMEM(buf, dtype), pltpu.VMEM(buf, dtype), *([pltpu.SemaphoreType.DMA((2,))] * 5),
            pltpu.SemaphoreType.REGULAR((2,))]


CHANNEL_REFS = 8


def _copy_through_vmem(src, dst, ch):
    sbuf, _, ld, _, _, st, _, _ = ch
    steps = src.shape[0] // STREAM_ROWS
    assert steps >= 2 and steps * STREAM_ROWS == src.shape[0]

    def load(i, slot):
        return pltpu.make_async_copy(src.at[_stream_rows(i)], sbuf.at[slot], ld.at[slot])

    def store(i, slot):
        return pltpu.make_async_copy(sbuf.at[slot], dst.at[_stream_rows(i)], st.at[slot])

    load(0, 0).start()

    def step(i, carry):
        slot = lax.rem(i, 2)
        nxt = 1 - slot

        @pl.when(i + 1 < steps)
        def _():
            @pl.when(i >= 1)
            def _():
                store(0, nxt).wait()
            load(i + 1, nxt).start()

        load(i, slot).wait()
        store(i, slot).start()
        return carry

    lax.fori_loop(0, steps, step, 0)
    for slot in range(2):
        store(0, slot).wait()


def _exchange_stream(src, dst, keep, ch, sibling):
    sbuf, rbuf, ld, snd, rcv, st, kp, credit = ch
    steps = src.shape[0] // STREAM_ROWS
    assert steps >= 2 and steps * STREAM_ROWS == src.shape[0]

    def load(i, slot):
        return pltpu.make_async_copy(src.at[_stream_rows(i)], sbuf.at[slot], ld.at[slot])

    def push(slot):
        return _remote(sbuf.at[slot], rbuf.at[slot], snd.at[slot], rcv.at[slot], sibling)

    def store(i, slot):
        return pltpu.make_async_copy(rbuf.at[slot], dst.at[_stream_rows(i)], st.at[slot])

    def save(i, slot):
        return pltpu.make_async_copy(sbuf.at[slot], keep.at[_stream_rows(i)], kp.at[slot])

    for slot in range(2):
        pl.semaphore_signal(credit.at[slot], 1, device_id=sibling, device_id_type=MESH)
    load(0, 0).start()

    def step(i, carry):
        slot = lax.rem(i, 2)
        nxt = 1 - slot

        @pl.when(i + 1 < steps)
        def _():
            @pl.when(i >= 1)
            def _():
                push(nxt).wait_send()
                if keep is not None:
                    save(0, nxt).wait()
            load(i + 1, nxt).start()

        load(i, slot).wait()
        pl.semaphore_wait(credit.at[slot], 1)
        push(slot).start()
        if keep is not None:
            save(i, slot).start()
        push(slot).wait_recv()
        store(i, slot).start()

        @pl.when(i >= 1)
        def _():
            store(0, nxt).wait()

            @pl.when(i + 1 < steps)
            def _():
                pl.semaphore_signal(credit.at[nxt], 1, device_id=sibling, device_id_type=MESH)
        return carry

    lax.fori_loop(0, steps, step, 0)
    store(0, (steps - 1) % 2).wait()
    for slot in range(2):
        push(slot).wait_send()
        if keep is not None:
            save(0, slot).wait()


def _all_gather_shards(shards, small, *, name):
    n = len(shards)

    def body(*refs):
        ins, outs = refs[:n + 1], refs[n + 1:2 * n + 2]
        scr = refs[2 * n + 2:]
        chans = [scr[CHANNEL_REFS * t:CHANNEL_REFS * (t + 1)] for t in range(n)]
        send_sems, recv_sems, small_sems = scr[CHANNEL_REFS * n:]
        x, y, c, _ = _place()
        me = 2 * x + y
        sibling = (x, y, 1 - c)
        near = (lax.rem(x + 1 - c, 2), lax.rem(y + c, 2))
        far = (lax.rem(x + c, 2), lax.rem(y + 1 - c, 2))
        k_near, k_far, k_diag = 2 * near[0] + near[1], 2 * far[0] + far[1], 3 - me
        targets = ((*near, c), (*far, c), (*far, c))
        arrives = (k_near, k_far, k_diag)
        streams_in = (k_far, k_near, k_diag)

        def ici(t, j, src, blk):
            return _remote(src, outs[t].at[blk, c], send_sems.at[3 * t + j], recv_sems.at[3 * t + j], targets[j])

        first = [ici(t, j, ins[t].at[c], me) for t in range(n + 1) for j in range(2)]
        for cp in first:
            cp.start()
        small_local = pltpu.make_async_copy(ins[n], outs[n].at[me], small_sems.at[6])
        small_local.start()
        for t in range(n):
            for h in range(2):
                _copy_through_vmem(ins[t].at[h], outs[t].at[me, h], chans[t])
        passed = []
        for j in range(3):
            for t in range(n + 1):
                landed = outs[t].at[arrives[j], c]
                ici(t, j, landed, arrives[j]).wait_recv()
                if j == 0:
                    fwd = ici(t, 2, landed, k_near)
                    fwd.start()
                    passed.append(fwd)
                if t < n:
                    _exchange_stream(landed, outs[t].at[streams_in[j], 1 - c], None, chans[t], sibling)
                else:
                    fwd = _remote(landed, landed, small_sems.at[j], small_sems.at[3 + j], sibling)
                    fwd.start()
                    passed.append(fwd)
        for j in range(3):
            got = outs[n].at[streams_in[j], 1 - c]
            _remote(got, got, small_sems.at[j], small_sems.at[3 + j], sibling).wait_recv()
        for cp in first + passed:
            cp.wait_send()
        small_local.wait()

    scratch = []
    for s in shards:
        scratch += _channel_scratch(s.shape[2], s.dtype)
    return pl.pallas_call(
        body, in_specs=[ANY] * (n + 1), out_specs=[ANY] * (n + 1),
        out_shape=[SDS((N_CHIPS, *s.shape), s.dtype) for s in (*shards, small)],
        scratch_shapes=[*scratch, pltpu.SemaphoreType.DMA((3 * n + 3,)), pltpu.SemaphoreType.DMA((3 * n + 3,)),
                        pltpu.SemaphoreType.DMA((7,))],
        compiler_params=pltpu.CompilerParams(vmem_limit_bytes=VMEM_LIMIT), name=name)(*shards, small)


def _pair_reduce(stacks, *, name):
    n = len(stacks)
    per = 11

    def body(*refs):
        ins, outs, scr = refs[:n], refs[n:2 * n], refs[2 * n:]
        x, y, c, _ = _place()
        sibling = (x, y, 1 - c)
        for t in range(n):
            sraw, sbuf, rbuf, obuf, pbuf, ld_s, ld_o, snd, rcv, st, credit = scr[per * t:per * (t + 1)]
            steps = ins[t].shape[1] // STREAM_ROWS
            src, own, out = ins[t].at[1 - c], ins[t].at[c], outs[t]

            def load_s(i, slot, src=src, sraw=sraw, ld_s=ld_s):
                return pltpu.make_async_copy(src.at[_stream_rows(i)], sraw.at[slot], ld_s.at[slot])

            def load_o(i, slot, own=own, obuf=obuf, ld_o=ld_o):
                return pltpu.make_async_copy(own.at[_stream_rows(i)], obuf.at[slot], ld_o.at[slot])

            def push(slot, sbuf=sbuf, rbuf=rbuf, snd=snd, rcv=rcv):
                return _remote(sbuf.at[slot], rbuf.at[slot], snd.at[slot], rcv.at[slot], sibling)

            def store(i, slot, pbuf=pbuf, out=out, st=st):
                return pltpu.make_async_copy(pbuf.at[slot], out.at[_stream_rows(i)], st.at[slot])

            assert steps >= 2
            for slot in range(2):
                pl.semaphore_signal(credit.at[slot], 1, device_id=sibling, device_id_type=MESH)
                load_s(slot, slot).start()
                load_o(slot, slot).start()
            load_s(0, 0).wait()
            sbuf[0] = sraw[0].astype(sbuf.dtype)
            pl.semaphore_wait(credit.at[0], 1)
            push(0).start()

            def step(i, carry, load_s=load_s, load_o=load_o, push=push, store=store, sraw=sraw, sbuf=sbuf, rbuf=rbuf,
                     obuf=obuf, pbuf=pbuf, credit=credit, steps=steps):
                slot = lax.rem(i, 2)
                nxt = 1 - slot

                @pl.when(i + 1 < steps)
                def _():
                    load_s(i + 1, nxt).wait()
                    sbuf[nxt] = sraw[nxt].astype(sbuf.dtype)
                    pl.semaphore_wait(credit.at[nxt], 1)
                    push(nxt).start()

                load_o(i, slot).wait()
                push(slot).wait_recv()

                @pl.when(i >= 2)
                def _():
                    store(i, slot).wait()

                pbuf[slot] = (obuf[slot] + rbuf[slot].astype(F32)).astype(pbuf.dtype)
                store(i, slot).start()
                push(slot).wait_send()

                @pl.when(i + 2 < steps)
                def _():
                    load_s(i + 2, slot).start()
                    load_o(i + 2, slot).start()
                    pl.semaphore_signal(credit.at[slot], 1, device_id=sibling, device_id_type=MESH)
                return carry

            lax.fori_loop(0, steps, step, 0)
            for slot in range(2):
                store(0, slot).wait()

    scratch = []
    for s in stacks:
        buf = (2, STREAM_ROWS, s.shape[2])
        scratch += [pltpu.VMEM(buf, F32), pltpu.VMEM(buf, BF16), pltpu.VMEM(buf, BF16), pltpu.VMEM(buf, F32),
                    pltpu.VMEM(buf, BF16), *([pltpu.SemaphoreType.DMA((2,))] * 5), pltpu.SemaphoreType.REGULAR((2,))]
    return pl.pallas_call(
        body, in_specs=[ANY] * n, out_specs=[ANY] * n, out_shape=[SDS(s.shape[1:], BF16) for s in stacks],
        scratch_shapes=scratch, compiler_params=pltpu.CompilerParams(vmem_limit_bytes=VMEM_LIMIT), name=name)(*stacks)


def _chip_scatter(parts, *, name):
    n = len(parts)

    def body(*refs):
        ins, outs = refs[:n], refs[n:2 * n]
        send_sems, recv_sems = refs[2 * n:]
        _, _, c, chips = _place()
        copies = [_remote(ins[t].at[2 * cx + cy], outs[t].at[j], send_sems.at[3 * t + j], recv_sems.at[3 * t + j],
                          (cx, cy, c)) for t in range(n) for j, (cx, cy) in enumerate(chips)]
        for cp in copies:
            cp.start()
        for cp in copies:
            cp.wait_recv()
        for cp in copies:
            cp.wait_send()

    return pl.pallas_call(
        body, in_specs=[ANY] * n, out_specs=[ANY] * n, out_shape=[SDS((3, *p.shape[1:]), p.dtype) for p in parts],
        scratch_shapes=[pltpu.SemaphoreType.DMA((3 * n,)), pltpu.SemaphoreType.DMA((3 * n,))], name=name)(*parts)


HBM_SPEC = pl.BlockSpec(memory_space=pltpu.HBM)
SEM_SPEC = pl.BlockSpec(memory_space=pltpu.SEMAPHORE)
SIDE_EFFECT = pltpu.SideEffectType.DATAFLOW_SIDE_EFFECTING


def _scatter_copies(ins, lands, send_sems, recv_sems):
    _, _, c, chips = _place()
    return [_remote(ins[t].at[2 * cx + cy], lands[t].at[j], send_sems.at[3 * t + j], recv_sems.at[3 * t + j],
                    (cx, cy, c)) for t in range(len(ins)) for j, (cx, cy) in enumerate(chips)]


def _chip_scatter_start(parts, *, name):
    n = len(parts)

    def body(*refs):
        ins, lands = refs[:n], refs[n:2 * n]
        send_sems, recv_sems, token = refs[2 * n], refs[2 * n + 1], refs[-1]
        for cp in _scatter_copies(ins, lands, send_sems, recv_sems):
            cp.start()
        token[...] = jnp.zeros_like(token)

    hbm = lambda a: pltpu.with_memory_space_constraint(a, pltpu.HBM)
    lands = [hbm(lax.empty((3, *p.shape[1:]), p.dtype)) for p in parts]
    thru = [pltpu.HBM(a.shape, a.dtype) for a in (*parts, *lands)]
    outs = pl.pallas_call(
        body, name=name,
        out_shape=(pltpu.SemaphoreType.DMA((3 * n,)), pltpu.SemaphoreType.DMA((3 * n,)), *thru, SDS((8, 128), F32)),
        in_specs=[HBM_SPEC] * (2 * n),
        out_specs=(SEM_SPEC, SEM_SPEC, *([HBM_SPEC] * (2 * n)), pl.BlockSpec(memory_space=pltpu.VMEM)),
        input_output_aliases={i: 2 + i for i in range(2 * n)},
        compiler_params=pltpu.CompilerParams(has_side_effects=SIDE_EFFECT),
    )(*[hbm(p) for p in parts], *lands)
    return outs[0], outs[1], outs[2:2 + n], outs[2 + n:2 + 2 * n], outs[-1]


def _chip_scatter_wait(send_sems, recv_sems, parts, lands, after, *, name):
    n = len(parts)

    def body(*refs):
        ins, lands_in = refs[:n], refs[n:2 * n]
        for cp in _scatter_copies(ins, lands_in, refs[2 * n], refs[2 * n + 1]):
            cp.wait_send()
            cp.wait_recv()

    outs = pl.pallas_call(
        body, name=name, out_shape=[pltpu.HBM(a.shape, a.dtype) for a in (*parts, *lands)],
        in_specs=[*([HBM_SPEC] * (2 * n)), SEM_SPEC, SEM_SPEC, *([ANY] * len(after))],
        out_specs=[HBM_SPEC] * (2 * n), input_output_aliases={i: i for i in range(2 * n)},
        compiler_params=pltpu.CompilerParams(has_side_effects=SIDE_EFFECT),
    )(*parts, *lands, send_sems, recv_sems, *after)
    return outs[:n], outs[n:]


def _pair_share(groups, *, name):
    finals = [f for grp in groups for f in grp]
    n, n_out = len(finals), len(groups)

    def body(*refs):
        ins, outs, scr = refs[:n], refs[n:n + n_out], refs[n + n_out:]
        x, y, c, _ = _place()
        sibling = (x, y, 1 - c)
        t = 0
        for o, grp in enumerate(groups):
            for layer in range(len(grp)):
                _exchange_stream(ins[t], outs[o].at[layer, 1 - c], outs[o].at[layer, c],
                                 scr[CHANNEL_REFS * t:CHANNEL_REFS * (t + 1)], sibling)
                t += 1

    scratch = []
    for f in finals:
        scratch += _channel_scratch(f.shape[1], f.dtype)
    return pl.pallas_call(
        body, in_specs=[ANY] * n, out_specs=[ANY] * n_out,
        out_shape=[SDS((len(grp), 2, *grp[0].shape), grp[0].dtype) for grp in groups],
        scratch_shapes=scratch, compiler_params=pltpu.CompilerParams(vmem_limit_bytes=VMEM_LIMIT), name=name)(*finals)


def _all_reduce_small(v, *, name):
    rows, lanes = v.shape
    n_dev = 8

    def body(v_ref, o_ref, all_ref, send_sems, recv_sems, local_sem):
        x, y, c, chips = _place()
        me, sibling = (x, y, c), (x, y, 1 - c)

        def block(px, py, pc):
            return all_ref.at[4 * px + 2 * py + pc]

        def copy(k, blk, to, src=None):
            return _remote(block(*blk) if src is None else src, block(*blk), send_sems.at[k], recv_sems.at[k], to)

        mine = pltpu.make_async_copy(v_ref, block(*me), local_sem)
        mine.start()
        first = [copy(0, me, sibling, src=v_ref)]
        first += [copy(1 + j, me, (*chip, c), src=v_ref) for j, chip in enumerate(chips)]
        for cp in first:
            cp.start()
        passed = [copy(4 + j, (*chip, c), sibling) for j, chip in enumerate(chips)]
        for j, chip in enumerate(chips):
            copy(1 + j, (*chip, c), me).wait_recv()
            passed[j].start()
        copy(0, sibling, me).wait_recv()
        for j, chip in enumerate(chips):
            copy(4 + j, (*chip, 1 - c), me).wait_recv()
        for cp in first + passed:
            cp.wait_send()
        mine.wait()
        acc = all_ref[0]
        for k in range(1, n_dev):
            acc = acc + all_ref[k]
        o_ref[...] = acc

    vmem = pl.BlockSpec(memory_space=pltpu.VMEM)
    return pl.pallas_call(
        body, in_specs=[vmem], out_specs=vmem, out_shape=SDS((rows, lanes), F32),
        scratch_shapes=[pltpu.VMEM((n_dev, rows, lanes), F32), pltpu.SemaphoreType.DMA((7,)),
                        pltpu.SemaphoreType.DMA((7,)), pltpu.SemaphoreType.DMA],
        compiler_params=pltpu.CompilerParams(vmem_limit_bytes=VMEM_LIMIT), name=name)(v)


def _relu2_epilogue(acc):
    return acc, jnp.square(jnp.maximum(acc, 0.0))


def _res_epilogue(acc, res):
    return (acc + res,)


def _drelu2_epilogue(acc, pre):
    return (acc * (2.0 * jnp.maximum(pre.astype(F32), 0.0)),)


def _ffn_fwd(h, g, w1, w2, tag):
    f = _rms_fwd(h, g, name=f"ffn_norm_{tag}")
    pre, act = _mm_nn(f, w1, name=f"ffn1_{tag}", epilogue=_relu2_epilogue, n_out_dtypes=(BF16, BF16))
    h_out = _mm_nn(act, w2, name=f"ffn2_{tag}", extras=(h,), epilogue=_res_epilogue)
    return h_out, (f, pre, act)


def _ffn_bwd(dh, h, g, w1, w2, saved, layer, after=()):
    f, pre, act = saved
    dpre = _mm_nt(dh, w2, name=f"ffn2_dx_{layer}", out_dtype=BF16, extras=(pre,), epilogue=_drelu2_epilogue,
                  after=after)
    dw2 = _mm_tn_stacked(act, dh, name=f"ffn2_dw_{layer}", col_slots=False)
    df = _mm_nt(dpre, w1, name=f"ffn1_dx_{layer}")
    dw1 = _mm_tn_stacked(f, dpre, name=f"ffn1_dw_{layer}", col_slots=True)
    dh, dg = _rms_bwd(h, g, df, dh, name=f"ffn_norm_bwd_{layer}")
    return dh, dg, dw1, dw2


def _kv_fwd(mem, g, w_kv, tag):
    m = _rms_fwd(mem, g, name=f"mem_norm_{tag}")
    return m, _mm_nn(m, w_kv, name=f"kv_{tag}")


def _kv_bwd(mem, g, w_kv, m, dk, dv, layer):
    dkv = jnp.concatenate([dk, dv], axis=1)
    dw = _mm_tn_stacked(m, dkv, name=f"kv_dw_{layer}", col_slots=True)
    dm = _mm_nt(dkv, w_kv, name=f"kv_dx_{layer}")
    _, dg = _rms_bwd(mem, g, dm, dm, name=f"mem_norm_bwd_{layer}")
    return dw, dg


def _local_step(x, mem, target, p, after_layer1=None, after_ffn0=None):
    row = lambda v: v.reshape(1, -1)
    g = {}

    h0 = x
    a0 = _rms_fwd(h0, row(p["norm_mix"][0]), name="mix_norm_0")
    proj_a = _mm_nn(a0, p["a_in"], name="a_in")
    m0, kv0 = _kv_fwd(mem, row(p["mem_norm"][0]), p["w_kv"][0], "0")
    cat0 = _attn_fwd(proj_a, 2 * D_INNER, kv0, name="attn_0")
    bs_col = p["a_bs"].reshape(A_GROUPS, CHUNK, 1)
    cat0 = _gate_fwd(proj_a, p["a_ln_g"], p["a_ln_b"], p["a_ws"], bs_col, cat0, name="gate")
    h1 = _mm_nn(cat0, p["w_out"][0], name="out_0", extras=(h0,), epilogue=_res_epilogue)
    h2, ffn0 = _ffn_fwd(h1, row(p["norm_ffn"][0]), p["w_ffn1"][0], p["w_ffn2"][0], "0")

    a1 = _rms_fwd(h2, row(p["norm_mix"][1]), name="mix_norm_1")
    proj_b = _mm_nn(a1, p["b_in"], name="b_in")
    m1, kv1 = _kv_fwd(mem, row(p["mem_norm"][1]), p["w_kv"][1], "1")
    cat1 = _attn_fwd(proj_b, B_Q_OFF, kv1, name="attn_1")
    xbc = _conv_fwd(proj_b, p["b_conv_w"], p["b_conv_b"], name="conv")
    dt_raw = proj_b[:, B_DT_OFF:B_DT_OFF + SSM_HEADS].reshape(SEQ, SSM_GROUPS, SSM_HPG)
    dt_c = jnp.transpose(dt_raw, (1, 0, 2))
    dt_r = jnp.transpose(dt_raw, (1, 2, 0))
    per_head = lambda v: v.reshape(SSM_GROUPS, 1, SSM_HPG)
    ssd_par = (per_head(p["b_dt_bias"]), per_head(p["b_a_log"]), per_head(p["b_d"]), p["b_gnorm"])
    cat1, hprev = _ssd_fwd(xbc, proj_b, dt_c, dt_r, *ssd_par, cat1, name="ssd")
    h3 = _mm_nn(cat1, p["w_out"][1], name="out_1", extras=(h2,), epilogue=_res_epilogue)
    h4, ffn1 = _ffn_fwd(h3, row(p["norm_ffn"][1]), p["w_ffn1"][1], p["w_ffn2"][1], "1")

    loss, dh, g["final_norm"] = _loss_head(h4, row(p["final_norm"]), target, name="loss_head")

    dh, dnf1, dw1_1, dw2_1 = _ffn_bwd(dh, h3, row(p["norm_ffn"][1]), p["w_ffn1"][1], p["w_ffn2"][1], ffn1, 1)
    dcat1 = _mm_nt(dh, p["w_out"][1], name="out_dx_1")
    dwo_1 = _mm_tn_stacked(cat1, dh, name="out_dw_1", col_slots=False)
    dproj_b, dk1, dv1 = _attn_bwd(proj_b, B_Q_OFF, kv1, dcat1, B_IN_PAD, B_Q_OFF, name="attn_bwd_1")
    (dproj_b, dxs, dbm, dcm, ddt_c, ddt_r, g["b_dt_bias"], g["b_a_log"], g["b_d"], g["b_gnorm"]) = _ssd_bwd(
        xbc, proj_b, dt_c, dt_r, *ssd_par, hprev, dcat1, dproj_b, name="ssd_bwd")
    dproj_b, g["b_conv_w"], g["b_conv_b"] = _conv_bwd(proj_b, p["b_conv_w"], p["b_conv_b"], dxs, dbm, dcm, dproj_b,
                                                      name="conv_bwd")
    ddt = jnp.transpose(ddt_c, (1, 0, 2)) + jnp.transpose(ddt_r, (2, 0, 1))
    ddt = jnp.pad(ddt.reshape(SEQ, SSM_HEADS), ((0, 0), (0, B_IN_PAD - B_DT_OFF - SSM_HEADS))).astype(BF16)
    dproj_b = lax.dynamic_update_slice(dproj_b, ddt, (0, B_DT_OFF))
    dwkv_1, dmn1 = _kv_bwd(mem, row(p["mem_norm"][1]), p["w_kv"][1], m1, dk1, dv1, 1)
    dwb = _b_in_grad_slots(_mm_tn(a1, dproj_b, name="b_in_dw"))
    da1 = _mm_nt(dproj_b, p["b_in"], name="b_in_dx")
    dh, dnm1 = _rms_bwd(h2, row(p["norm_mix"][1]), da1, dh, name="mix_norm_bwd_1")
    layer1 = dict(w_kv=dwkv_1, w_out=dwo_1, w_ffn1=dw1_1, w_ffn2=dw2_1, b_in=dwb)
    token = () if after_layer1 is None else (after_layer1(layer1),)

    dh, dnf0, dw1_0, dw2_0 = _ffn_bwd(dh, h1, row(p["norm_ffn"][0]), p["w_ffn1"][0], p["w_ffn2"][0], ffn0, 0,
                                      after=token)
    ffn0_grads = dict(w_ffn1=dw1_0, w_ffn2=dw2_0)
    token = () if after_ffn0 is None else (after_ffn0(ffn0_grads),)
    dcat0 = _mm_nt(dh, p["w_out"][0], name="out_dx_0", after=token)
    dwo_0 = _mm_tn_stacked(cat0, dh, name="out_dw_0", col_slots=False)
    dproj_a, dk0, dv0 = _attn_bwd(proj_a, 2 * D_INNER, kv0, dcat0, A_IN, 2 * D_INNER, name="attn_bwd_0")
    dproj_a, g["a_ln_g"], g["a_ln_b"], g["a_ws"], dbs_col = _gate_bwd(
        proj_a, p["a_ln_g"], p["a_ln_b"], p["a_ws"], bs_col, dcat0, dproj_a, name="gate_bwd")
    g["a_bs"] = dbs_col.reshape(A_GROUPS, CHUNK)
    dwkv_0, dmn0 = _kv_bwd(mem, row(p["mem_norm"][0]), p["w_kv"][0], m0, dk0, dv0, 0)
    dwa = _mm_tn_stacked(a0, dproj_a, name="a_in_dw", col_slots=True)
    da0 = _mm_nt(dproj_a, p["a_in"], name="a_in_dx")
    dx, dnm0 = _rms_bwd(h0, row(p["norm_mix"][0]), da0, dh, name="mix_norm_bwd_0")

    g["norm_mix"] = jnp.concatenate([dnm0, dnm1], axis=0)
    g["norm_ffn"] = jnp.concatenate([dnf0, dnf1], axis=0)
    g["mem_norm"] = jnp.concatenate([dmn0, dmn1], axis=0)
    layer0 = dict(w_kv=dwkv_0, w_out=dwo_0, w_ffn1=dw1_0, w_ffn2=dw2_0, a_in=dwa)
    return loss, dx, g, layer0, layer1


def _b_in_full(gathered):
    full = jnp.transpose(gathered, (1, 0, 2)).reshape(D_MODEL, B_IN)
    dt0 = D_INNER + CONV_DIM
    return jnp.concatenate([full[:, :dt0], full[:, dt0 + SSM_HEADS:], full[:, dt0:dt0 + SSM_HEADS],
                            jnp.zeros((D_MODEL, B_IN_PAD - B_IN), full.dtype)], axis=1)


def _b_in_grad_slots(d):
    dt0 = D_INNER + CONV_DIM
    full = jnp.concatenate([d[:, :dt0], d[:, B_DT_OFF:B_DT_OFF + SSM_HEADS], d[:, dt0:B_DT_OFF]], axis=1)
    return jnp.transpose(full.reshape(2, D_MODEL // 2, N_CHIPS, B_IN // N_CHIPS), (0, 2, 1, 3))


LARGE = ("w_kv", "w_out", "w_ffn1", "w_ffn2", "a_in", "b_in")
SMALL_REPL = ("norm_mix", "norm_ffn", "mem_norm", "a_ln_g", "a_ln_b", "a_ws", "a_bs", "b_dt_bias", "b_a_log", "b_d",
              "final_norm")
SMALL_SHARD = ("b_conv_w", "b_conv_b", "b_gnorm")
WEIGHTS = ("norm_mix", "norm_ffn", "mem_norm", "w_kv", "w_out", "w_ffn1", "w_ffn2", "a_in", "a_ln_g", "a_ln_b", "a_ws",
           "a_bs", "b_in", "b_conv_w", "b_conv_b", "b_dt_bias", "b_a_log", "b_d", "b_gnorm", "final_norm")
CONV_SHARD = CONV_DIM // N_CHIPS
GN_SHARD = D_INNER // N_CHIPS


LAYERED = ("w_kv", "w_out", "w_ffn1", "w_ffn2")
LAYER_TENSORS = (("w_kv", "w_out", "w_ffn1", "w_ffn2", "a_in"), ("w_kv", "w_out", "w_ffn1", "w_ffn2", "b_in"))


def _gather_weights(w):
    halves = lambda a: a.reshape(2, a.shape[0] // 2, a.shape[1]).astype(BF16)
    big = [halves(w[k][layer] if k in LAYERED else w[k][0]) for layer in range(2) for k in LAYER_TENSORS[layer]]
    small = jnp.zeros((2, CONV_K, CONV_SHARD), F32)
    small = small.at[0].set(w["b_conv_w"][0])
    small = small.at[1, 0].set(w["b_conv_b"][0])
    small = small.at[1, 1, :GN_SHARD].set(w["b_gnorm"][0])
    gathered = _all_gather_shards(big, small, name="gather_weights")
    got = [dict(zip(LAYER_TENSORS[layer], gathered[5 * layer:5 * layer + 5])) for layer in range(2)]
    slots = lambda a: a.reshape(N_CHIPS, -1, a.shape[-1])
    rows = lambda a: a.reshape(-1, a.shape[-1])
    p = {}
    p["w_kv"] = [slots(got[layer]["w_kv"]) for layer in range(2)]
    p["w_out"] = [rows(got[layer]["w_out"]) for layer in range(2)]
    p["w_ffn1"] = [slots(got[layer]["w_ffn1"]) for layer in range(2)]
    p["w_ffn2"] = [rows(got[layer]["w_ffn2"]) for layer in range(2)]
    p["a_in"] = slots(got[0]["a_in"])
    p["b_in"] = _b_in_full(slots(got[1]["b_in"]))
    sm = gathered[-1]
    p["b_conv_w"] = jnp.transpose(sm[:, 0], (1, 0, 2)).reshape(CONV_K, CONV_DIM)
    p["b_conv_b"] = sm[:, 1, 0].reshape(1, CONV_DIM)
    p["b_gnorm"] = sm[:, 1, 1, :GN_SHARD].reshape(1, D_INNER)
    return p


def _pair_parts(grads, tag):
    stacks = [g.reshape(2, -1, g.shape[-1]) for g in grads.values()]
    parts = _pair_reduce(stacks, name=f"grads_pair_reduce_{tag}")
    return [t.reshape(N_CHIPS, -1, t.shape[-1]) for t in parts]


def _chip_sums(chip, names, parts, landed, tag):
    return {k: _sum_contributions(chip, t, u, name=f"grads_chip_sum_{k}_{tag}")
            for k, t, u in zip(names, parts, landed)}


def _small_layout(shapes):
    offs, o = {}, 0
    for k in (*SMALL_REPL, *SMALL_SHARD):
        size = math.prod(shapes[k])
        offs[k] = (o, size)
        o += size
    rows = -(-o // (8 * 128)) * 8
    return offs, rows


def _reduce_small(g, full_shapes):
    offs, rows = _small_layout(full_shapes)
    flat = jnp.concatenate([g[k].reshape(-1) for k in (*SMALL_REPL, *SMALL_SHARD)])
    flat = jnp.pad(flat, (0, rows * 128 - flat.shape[0])).reshape(rows, 128)
    total = _all_reduce_small(flat, name="grads_small_all_reduce").reshape(-1)
    return {k: total[o:o + n].reshape(full_shapes[k]) for k, (o, n) in offs.items()}


def kernel(x, mem, norm_mix, norm_ffn, mem_norm, w_kv, w_out, w_ffn1, w_ffn2, a_in, a_ln_g, a_ln_b, a_ws, a_bs, b_in, b_conv_w, b_conv_b, b_dt_bias, b_a_log, b_d, b_gnorm, final_norm, loss_target, m_norm_mix, m_norm_ffn, m_mem_norm, m_w_kv, m_w_out, m_w_ffn1, m_w_ffn2, m_a_in, m_a_ln_g, m_a_ln_b, m_a_ws, m_a_bs, m_b_in, m_b_conv_w, m_b_conv_b, m_b_dt_bias, m_b_a_log, m_b_d, m_b_gnorm, m_final_norm, v_norm_mix, v_norm_ffn, v_mem_norm, v_w_kv, v_w_out, v_w_ffn1, v_w_ffn2, v_a_in, v_a_ln_g, v_a_ln_b, v_a_ws, v_a_bs, v_b_in, v_b_conv_w, v_b_conv_b, v_b_dt_bias, v_b_a_log, v_b_d, v_b_gnorm, v_final_norm):
    w = dict(norm_mix=norm_mix, norm_ffn=norm_ffn, mem_norm=mem_norm, w_kv=w_kv, w_out=w_out, w_ffn1=w_ffn1,
             w_ffn2=w_ffn2, a_in=a_in, a_ln_g=a_ln_g, a_ln_b=a_ln_b, a_ws=a_ws, a_bs=a_bs, b_in=b_in, b_conv_w=b_conv_w,
             b_conv_b=b_conv_b, b_dt_bias=b_dt_bias, b_a_log=b_a_log, b_d=b_d, b_gnorm=b_gnorm, final_norm=final_norm)
    mom = dict(norm_mix=m_norm_mix, norm_ffn=m_norm_ffn, mem_norm=m_mem_norm, w_kv=m_w_kv, w_out=m_w_out,
               w_ffn1=m_w_ffn1, w_ffn2=m_w_ffn2, a_in=m_a_in, a_ln_g=m_a_ln_g, a_ln_b=m_a_ln_b, a_ws=m_a_ws,
               a_bs=m_a_bs, b_in=m_b_in, b_conv_w=m_b_conv_w, b_conv_b=m_b_conv_b, b_dt_bias=m_b_dt_bias,
               b_a_log=m_b_a_log, b_d=m_b_d, b_gnorm=m_b_gnorm, final_norm=m_final_norm)
    var = dict(norm_mix=v_norm_mix, norm_ffn=v_norm_ffn, mem_norm=v_mem_norm, w_kv=v_w_kv, w_out=v_w_out,
               w_ffn1=v_w_ffn1, w_ffn2=v_w_ffn2, a_in=v_a_in, a_ln_g=v_a_ln_g, a_ln_b=v_a_ln_b, a_ws=v_a_ws,
               a_bs=v_a_bs, b_in=v_b_in, b_conv_w=v_b_conv_w, b_conv_b=v_b_conv_b, b_dt_bias=v_b_dt_bias,
               b_a_log=v_b_a_log, b_d=v_b_d, b_gnorm=v_b_gnorm, final_norm=v_final_norm)

    p = _gather_weights(w)
    p.update(norm_mix=norm_mix, norm_ffn=norm_ffn, mem_norm=mem_norm, a_ln_g=a_ln_g, a_ln_b=a_ln_b, a_ws=a_ws[0],
             a_bs=a_bs[0], b_dt_bias=b_dt_bias, b_a_log=b_a_log, b_d=b_d, final_norm=final_norm)
    chip = 2 * lax.axis_index("x") + lax.axis_index("y")
    chip_arr = jnp.reshape(chip, (1,)).astype(jnp.int32)
    started = {}

    def start_scatter(tag):
        def hook(grads):
            started[tag] = (tuple(grads), _chip_scatter_start(_pair_parts(grads, tag), name=f"grads_chip_scatter_start_{tag}"))
            return started[tag][1][-1]
        return hook

    loss_part, dx, g, layer0, _ = _local_step(x[0], mem[0], loss_target[0], p, start_scatter("1"), start_scatter("0f"))
    loss = lax.psum(loss_part[0, 0], ("x", "y", "c"))

    full_shapes = {k: w[k].shape for k in SMALL_REPL}
    full_shapes.update(b_conv_w=(1, CONV_K, CONV_DIM), b_conv_b=(1, CONV_DIM), b_gnorm=(1, D_INNER))
    gs = _reduce_small(g, full_shapes)
    gs["b_conv_w"] = lax.dynamic_slice_in_dim(gs["b_conv_w"], chip * CONV_SHARD, CONV_SHARD, axis=2)
    gs["b_conv_b"] = lax.dynamic_slice_in_dim(gs["b_conv_b"], chip * CONV_SHARD, CONV_SHARD, axis=1)
    gs["b_gnorm"] = lax.dynamic_slice_in_dim(gs["b_gnorm"], chip * GN_SHARD, GN_SHARD, axis=1)
    mixer0 = {k: layer0[k] for k in ("w_kv", "w_out", "a_in")}
    parts_m = _pair_parts(mixer0, "0m")
    landed_m = _chip_scatter(parts_m, name="grads_chip_scatter_0m")
    halves = [_chip_sums(chip_arr, tuple(mixer0), parts_m, landed_m, "0m"), {}]
    for tag, layer in (("0f", 0), ("1", 1)):
        names, (send_sems, recv_sems, parts, lands, _) = started[tag]
        parts, landed = _chip_scatter_wait(send_sems, recv_sems, parts, lands, (dx,),
                                           name=f"grads_chip_scatter_wait_{tag}")
        halves[layer].update(_chip_sums(chip_arr, names, parts, landed, tag))
    groups = [[halves[layer][k] for layer in range(2) if k in halves[layer]] for k in LARGE]
    gl = dict(zip(LARGE, _pair_share(groups, name="grads_pair_share")))
    grads = {k: (gl[k].reshape(w[k].shape) if k in gl else gs[k]) for k in WEIGHTS}

    delta, new_m, new_v = {}, {}, {}
    for k in WEIGHTS:
        shape = w[k].shape
        flat = (lambda a: a.reshape(-1, shape[-1])) if len(shape) > 1 else (lambda a: a.reshape(1, -1))
        d, m_new, v_new = _adamw(flat(w[k]), flat(grads[k]), flat(mom[k]), flat(var[k]), name=f"adamw_{k}")
        delta[k], new_m[k], new_v[k] = d.reshape(shape), m_new.reshape(shape), v_new.reshape(shape)

    return (loss, dx.reshape(x.shape), *[grads[k] for k in WEIGHTS], *[delta[k] for k in WEIGHTS],
            *[new_m[k] for k in WEIGHTS], *[new_v[k] for k in WEIGHTS])
```

```python
import math

import jax
import jax.numpy as jnp
from jax import lax
from jax.experimental import pallas as pl
from jax.experimental.pallas import tpu as pltpu

F32 = jnp.float32
BF16 = jnp.bfloat16
SDS = jax.ShapeDtypeStruct

D_MODEL = 1024
SEQ = 2048
CHUNK = 128
N_MEM = 256
D_INNER = 2048
A_GROUPS = 8
A_GROUP_W = D_INNER // A_GROUPS
SSM_HEADS = 32
SSM_HEAD_DIM = 64
SSM_GROUPS = 4
SSM_HPG = 8
SSM_STATE = 128
SSM_GROUP_W = SSM_HPG * SSM_HEAD_DIM
CONV_K = 4
CONV_DIM = 3072
X_HEADS = 4
X_HEAD_DIM = 256
X_WIDTH = 1024
MIX_OUT = 3072
D_FF = 4096
A_IN = 5120
B_IN = 6176
B_IN_PAD = 6272
B_Q_OFF = 5120
B_DT_OFF = 6144
N_CHUNKS = SEQ // CHUNK
EPS = 1e-6
N_CHIPS = 4

ADAM_LR = 0.001
ADAM_B1 = 0.9
ADAM_B2 = 0.999
ADAM_EPS = 1e-08
ADAM_WD = 0.01
ADAM_STEP = 10

VMEM_LIMIT = 48 * 1024 * 1024
MESH = pl.DeviceIdType.MESH


def _cparams(sem):
    return pltpu.CompilerParams(dimension_semantics=sem, vmem_limit_bytes=VMEM_LIMIT)


def _dot(a, b, dims=(((1,), (0,)), ((), ()))):
    return lax.dot_general(a.astype(BF16), b.astype(BF16), dims, preferred_element_type=F32)


def _dot_nt(a, b):
    return _dot(a, b, (((1,), (1,)), ((), ())))


def _dot_tn(a, b):
    return _dot(a, b, (((0,), (0,)), ((), ())))


def _pick(n, cands):
    for c in cands:
        if n % c == 0:
            return c
    raise ValueError(f"no tile for {n}")


def _mm_call(a, b, *, dims, grid, a_spec, b_spec, acc_shape, out_shapes, out_specs, name,
             extras=(), extra_specs=(), epilogue=None, after=()):
    n_k = grid[2]
    n_extra = len(extras)
    n_out = len(out_shapes)
    n_in = 2 + n_extra + len(after)

    def body(*refs):
        a_ref, b_ref = refs[0], refs[1]
        extra_refs = refs[2:2 + n_extra]
        out_refs = refs[n_in:n_in + n_out]
        acc = refs[-1]
        k = pl.program_id(2)

        @pl.when(k == 0)
        def _():
            acc[...] = jnp.zeros_like(acc)

        acc[...] += _dot(a_ref[...], b_ref[...], dims)

        @pl.when(k == n_k - 1)
        def _():
            vals = (acc[...],) if epilogue is None else epilogue(acc[...], *[e[...] for e in extra_refs])
            for o_ref, v in zip(out_refs, vals):
                o_ref[...] = v.astype(o_ref.dtype)

    return pl.pallas_call(
        body, grid=grid, in_specs=[a_spec, b_spec, *extra_specs, *([ANY] * len(after))], out_specs=list(out_specs),
        out_shape=list(out_shapes), scratch_shapes=[pltpu.VMEM(acc_shape, F32)],
        compiler_params=_cparams(("parallel", "parallel", "arbitrary")), name=name,
    )(a, b, *extras, *after)


def _w_dims(w):
    if w.ndim == 2:
        return w.shape[0], w.shape[1], 1, w.shape[1]
    return w.shape[1], w.shape[0] * w.shape[2], w.shape[0], w.shape[2]


def _mm_nn(a, w, *, name, out_dtype=F32, a_cols=None, extras=(), epilogue=None, n_out_dtypes=None):
    m = a.shape[0]
    k_dim, n_dim, _, n_slot = _w_dims(w)
    a_off, a_w = (0, a.shape[1]) if a_cols is None else a_cols
    assert a_w == k_dim
    tm = _pick(m, (2048, 1024, 512, 256))
    tn = _pick(n_slot, (512, 896, 640, 256, 128))
    tk = _pick(k_dim, (1024, 768, 512, 384, 256, 128))
    assert a_off % tk == 0
    nb = n_slot // tn
    a_spec = pl.BlockSpec((tm, tk), lambda i, j, k: (i, a_off // tk + k))
    if w.ndim == 2:
        b_spec = pl.BlockSpec((tk, tn), lambda i, j, k: (k, j))
    else:
        b_spec = pl.BlockSpec((None, tk, tn), lambda i, j, k: (j // nb, k, j % nb))
    o_spec = pl.BlockSpec((tm, tn), lambda i, j, k: (i, j))
    dts = n_out_dtypes or (out_dtype,)
    outs = _mm_call(a, w, dims=(((1,), (0,)), ((), ())), grid=(m // tm, n_dim // tn, k_dim // tk),
                    a_spec=a_spec, b_spec=b_spec, acc_shape=(tm, tn),
                    out_shapes=[SDS((m, n_dim), dt) for dt in dts], out_specs=[o_spec] * len(dts), name=name,
                    extras=extras, extra_specs=[o_spec] * len(extras), epilogue=epilogue)
    return outs if n_out_dtypes else outs[0]


def _mm_nt(a, w, *, name, out_dtype=F32, extras=(), epilogue=None, after=()):
    m = a.shape[0]
    k_dim, n_dim, _, n_slot = _w_dims(w)
    assert a.shape[1] == n_dim
    tm = _pick(m, (2048, 1024, 512, 256))
    to = _pick(k_dim, (512, 384, 256, 128))
    tc = _pick(n_slot, (1024, 896, 640, 512, 256, 128))
    nb = n_slot // tc
    a_spec = pl.BlockSpec((tm, tc), lambda i, j, k: (i, k))
    if w.ndim == 2:
        b_spec = pl.BlockSpec((to, tc), lambda i, j, k: (j, k))
    else:
        b_spec = pl.BlockSpec((None, to, tc), lambda i, j, k: (k // nb, j, k % nb))
    o_spec = pl.BlockSpec((tm, to), lambda i, j, k: (i, j))
    return _mm_call(a, w, dims=(((1,), (1,)), ((), ())), grid=(m // tm, k_dim // to, n_dim // tc),
                    a_spec=a_spec, b_spec=b_spec, acc_shape=(tm, to),
                    out_shapes=[SDS((m, k_dim), out_dtype)], out_specs=[o_spec], name=name,
                    extras=extras, extra_specs=[o_spec] * len(extras), epilogue=epilogue, after=after)[0]


def _mm_tn(x, dy, *, name, x_cols=None):
    s = x.shape[0]
    x_off, k_dim = (0, x.shape[1]) if x_cols is None else x_cols
    n_dim = dy.shape[1]
    tm = _pick(k_dim, (1024, 768, 512, 384, 256, 128))
    tn = _pick(n_dim, (512, 896, 640, 256, 128))
    tk = _pick(s, (2048, 1024, 512, 256))
    assert x_off % tm == 0
    a_spec = pl.BlockSpec((tk, tm), lambda i, j, k: (k, x_off // tm + i))
    b_spec = pl.BlockSpec((tk, tn), lambda i, j, k: (k, j))
    o_spec = pl.BlockSpec((tm, tn), lambda i, j, k: (i, j))
    return _mm_call(x, dy, dims=(((0,), (0,)), ((), ())), grid=(k_dim // tm, n_dim // tn, s // tk),
                    a_spec=a_spec, b_spec=b_spec, acc_shape=(tm, tn),
                    out_shapes=[SDS((k_dim, n_dim), F32)], out_specs=[o_spec], name=name)[0]


def _mm_tn_stacked(x, dy, *, name, col_slots):
    s, k_dim = x.shape
    n_dim = dy.shape[1]
    r, c = (k_dim // 2, n_dim // N_CHIPS) if col_slots else (k_dim // N_CHIPS // 2, n_dim)
    tm = 2 * r
    tn = _pick(c, (512, 896, 640, 256, 128))
    tk = _pick(s, (2048, 1024, 512, 256))
    a_spec = pl.BlockSpec((tk, tm), lambda i, j, k: (k, i))
    b_spec = pl.BlockSpec((tk, tn), lambda i, j, k: (k, j))
    if col_slots:
        nb = c // tn
        o_spec = pl.BlockSpec((2, None, r, tn), lambda i, j, k: (0, j // nb, 0, j % nb))
    else:
        o_spec = pl.BlockSpec((2, None, r, tn), lambda i, j, k: (0, i, 0, j))
    return _mm_call(x, dy, dims=(((0,), (0,)), ((), ())), grid=(k_dim // tm, n_dim // tn, s // tk),
                    a_spec=a_spec, b_spec=b_spec, acc_shape=(tm, tn), epilogue=lambda acc: (acc.reshape(2, r, tn),),
                    out_shapes=[SDS((2, N_CHIPS, r, c), F32)], out_specs=[o_spec], name=name)[0]


def _rms(x, g):
    return x * lax.rsqrt(jnp.mean(x * x, axis=-1, keepdims=True) + EPS) * g


def _rms_fwd(h, g, *, name):
    rows, d = h.shape
    tr = _pick(rows, (512, 256))

    def body(h_ref, g_ref, o_ref):
        o_ref[...] = _rms(h_ref[...], g_ref[...]).astype(o_ref.dtype)

    return pl.pallas_call(
        body, grid=(rows // tr,),
        in_specs=[pl.BlockSpec((tr, d), lambda i: (i, 0)), pl.BlockSpec((1, d), lambda i: (0, 0))],
        out_specs=pl.BlockSpec((tr, d), lambda i: (i, 0)), out_shape=SDS((rows, d), BF16),
        compiler_params=_cparams(("parallel",)), name=name)(h, g)


def _rms_bwd(h, g, da, dres, *, name):
    rows, d = h.shape
    tr = _pick(rows, (512, 256))

    def body(h_ref, g_ref, da_ref, dres_ref, dh_ref, dg_ref):
        _, vjp = jax.vjp(_rms, h_ref[...], g_ref[...])
        dh, dg = vjp(da_ref[...].astype(F32))
        dh_ref[...] = dres_ref[...] + dh

        @pl.when(pl.program_id(0) == 0)
        def _():
            dg_ref[...] = jnp.zeros_like(dg_ref)

        dg_ref[...] += dg

    row_spec = pl.BlockSpec((tr, d), lambda i: (i, 0))
    vec_spec = pl.BlockSpec((1, d), lambda i: (0, 0))
    return pl.pallas_call(
        body, grid=(rows // tr,), in_specs=[row_spec, vec_spec, row_spec, row_spec],
        out_specs=[row_spec, vec_spec], out_shape=[SDS((rows, d), F32), SDS((1, d), F32)],
        compiler_params=_cparams(("arbitrary",)), name=name)(h, g, da, dres)


def _loss_head(h, g, target, *, name):
    rows, d = h.shape
    tr = _pick(rows, (512, 256))

    def body(h_ref, g_ref, t_ref, loss_ref, dh_ref, dg_ref):
        y, vjp = jax.vjp(_rms, h_ref[...], g_ref[...])
        err = y - t_ref[...]
        dh, dg = vjp(err * (1.0 / d))
        dh_ref[...] = dh

        @pl.when(pl.program_id(0) == 0)
        def _():
            dg_ref[...] = jnp.zeros_like(dg_ref)
            loss_ref[...] = jnp.zeros_like(loss_ref)

        dg_ref[...] += dg
        part = jnp.sum(jnp.sum(err * err, axis=-1, keepdims=True), axis=0, keepdims=True) * (0.5 / d)
        loss_ref[...] += jnp.broadcast_to(part, loss_ref.shape)

    row_spec = pl.BlockSpec((tr, d), lambda i: (i, 0))
    vec_spec = pl.BlockSpec((1, d), lambda i: (0, 0))
    loss_spec = pl.BlockSpec((8, 128), lambda i: (0, 0))
    return pl.pallas_call(
        body, grid=(rows // tr,), in_specs=[row_spec, vec_spec, row_spec],
        out_specs=[loss_spec, row_spec, vec_spec],
        out_shape=[SDS((8, 128), F32), SDS((rows, d), F32), SDS((1, d), F32)],
        compiler_params=_cparams(("arbitrary",)), name=name)(h, g, target)


def _gelu(x):
    return 0.5 * x * (1.0 + lax.erf(x * (1.0 / math.sqrt(2.0))))


def _gate_tile(pu, pv, ln_g, ln_b, ws, bs_t):
    u = [_gelu(p) for p in pu]
    v = [_gelu(p) for p in pv]
    mu = sum(jnp.sum(t, axis=-1, keepdims=True) for t in v) * (1.0 / D_INNER)
    vc = [t - mu for t in v]
    var = sum(jnp.sum(t * t, axis=-1, keepdims=True) for t in vc) * (1.0 / D_INNER)
    rstd = lax.rsqrt(var + EPS)
    row = lax.broadcasted_iota(jnp.int32, (CHUNK, CHUNK), 0)
    col = lax.broadcasted_iota(jnp.int32, (CHUNK, CHUNK), 1)
    out = []
    for gi in range(A_GROUPS):
        vn = vc[gi] * rstd * ln_g[gi] + ln_b[gi]
        w = jnp.where(row >= col, ws[gi], 0.0)
        sv = _dot(w, vn) + bs_t[gi]
        out.append(u[gi] * sv)
    return out


def _split(ref, n, width):
    return [ref[:, i * width:(i + 1) * width] for i in range(n)]


def _gate_in_specs():
    return [
        pl.BlockSpec((CHUNK, D_INNER), lambda c: (c, 0)),
        pl.BlockSpec((CHUNK, D_INNER), lambda c: (c, 1)),
        pl.BlockSpec((1, D_INNER), lambda c: (0, 0)),
        pl.BlockSpec((1, D_INNER), lambda c: (0, 0)),
        pl.BlockSpec((A_GROUPS, CHUNK, CHUNK), lambda c: (0, 0, 0)),
        pl.BlockSpec((A_GROUPS, CHUNK, 1), lambda c: (0, 0, 0)),
    ]


def _gate_args(u_ref, v_ref, g_ref, b_ref, ws_ref, bs_ref):
    ng, gw = A_GROUPS, A_GROUP_W
    return (_split(u_ref, ng, gw), _split(v_ref, ng, gw), _split(g_ref, ng, gw), _split(b_ref, ng, gw),
            [ws_ref[i] for i in range(ng)], [bs_ref[i] for i in range(ng)])


def _gate_fwd(proj, ln_g, ln_b, ws, bs_col, mixcat, *, name):
    def body(u_ref, v_ref, g_ref, b_ref, ws_ref, bs_ref, cat_in, cat_ref):
        del cat_in
        out = _gate_tile(*_gate_args(u_ref, v_ref, g_ref, b_ref, ws_ref, bs_ref))
        for gi, o in enumerate(out):
            cat_ref[:, gi * A_GROUP_W:(gi + 1) * A_GROUP_W] = o.astype(cat_ref.dtype)

    return pl.pallas_call(
        body, grid=(N_CHUNKS,), in_specs=[*_gate_in_specs(), pl.BlockSpec(memory_space=pl.ANY)],
        out_specs=pl.BlockSpec((CHUNK, D_INNER), lambda c: (c, 0)), out_shape=SDS(mixcat.shape, mixcat.dtype),
        input_output_aliases={6: 0}, compiler_params=_cparams(("parallel",)), name=name,
    )(proj, proj, ln_g, ln_b, ws, bs_col, mixcat)


def _gate_bwd(proj, ln_g, ln_b, ws, bs_col, dcat, dproj, *, name):
    ng, gw = A_GROUPS, A_GROUP_W

    def body(u_ref, v_ref, g_ref, b_ref, ws_ref, bs_ref, d_ref, dproj_in, dproj_ref, dg_ref, db_ref, dws_ref, dbs_ref):
        del dproj_in
        args = _gate_args(u_ref, v_ref, g_ref, b_ref, ws_ref, bs_ref)
        _, vjp = jax.vjp(_gate_tile, *args)
        dpu, dpv, dg, db, dws, dbs = vjp(_split(d_ref, ng, gw))
        for gi in range(ng):
            dproj_ref[:, gi * gw:(gi + 1) * gw] = dpu[gi].astype(dproj_ref.dtype)
            dproj_ref[:, D_INNER + gi * gw:D_INNER + (gi + 1) * gw] = dpv[gi].astype(dproj_ref.dtype)

        @pl.when(pl.program_id(0) == 0)
        def _():
            for r in (dg_ref, db_ref, dws_ref, dbs_ref):
                r[...] = jnp.zeros_like(r)

        for gi in range(ng):
            dg_ref[:, gi * gw:(gi + 1) * gw] += dg[gi]
            db_ref[:, gi * gw:(gi + 1) * gw] += db[gi]
            dws_ref[gi] += dws[gi]
            dbs_ref[gi] += dbs[gi]

    in_specs = _gate_in_specs()
    return pl.pallas_call(
        body, grid=(N_CHUNKS,),
        in_specs=[*in_specs, pl.BlockSpec((CHUNK, D_INNER), lambda c: (c, 0)), pl.BlockSpec(memory_space=pl.ANY)],
        out_specs=[pl.BlockSpec((CHUNK, 2 * D_INNER), lambda c: (c, 0)), *in_specs[2:]],
        out_shape=[SDS(dproj.shape, dproj.dtype), SDS((1, D_INNER), F32), SDS((1, D_INNER), F32),
                   SDS((ng, CHUNK, CHUNK), F32), SDS((ng, CHUNK, 1), F32)],
        input_output_aliases={7: 0}, compiler_params=_cparams(("arbitrary",)), name=name,
    )(proj, proj, ln_g, ln_b, ws, bs_col, dcat, dproj)


ATT_TQ = 512


def _attn_tile(q, k, v):
    s = _dot_nt(q, k) * (1.0 / math.sqrt(X_HEAD_DIM))
    s = s - jnp.max(s, axis=-1, keepdims=True)
    e = jnp.exp(s)
    p = e / jnp.sum(e, axis=-1, keepdims=True)
    return _dot(p, v)


def _attn_in_specs(q_blk, order):
    hd = X_HEAD_DIM
    return [
        pl.BlockSpec((ATT_TQ, hd), lambda a, b: (order(a, b)[0], q_blk + order(a, b)[1])),
        pl.BlockSpec((N_MEM, hd), lambda a, b: (0, order(a, b)[1])),
        pl.BlockSpec((N_MEM, hd), lambda a, b: (0, X_HEADS + order(a, b)[1])),
    ]


def _attn_fwd(proj, q_off, kv, *, name):
    order = lambda i, h: (i, h)
    cat_blk = D_INNER // X_HEAD_DIM

    def body(q_ref, k_ref, v_ref, o_ref):
        o_ref[...] = _attn_tile(q_ref[...], k_ref[...], v_ref[...]).astype(o_ref.dtype)

    return pl.pallas_call(
        body, grid=(SEQ // ATT_TQ, X_HEADS), in_specs=_attn_in_specs(q_off // X_HEAD_DIM, order),
        out_specs=pl.BlockSpec((ATT_TQ, X_HEAD_DIM), lambda i, h: (i, cat_blk + h)),
        out_shape=SDS((SEQ, MIX_OUT), BF16), compiler_params=_cparams(("parallel", "parallel")), name=name,
    )(proj, kv, kv)


def _attn_bwd(proj, q_off, kv, dcat, dproj_width, dq_off, *, name):
    order = lambda h, i: (i, h)
    cat_blk = D_INNER // X_HEAD_DIM
    dq_blk = dq_off // X_HEAD_DIM

    def body(q_ref, k_ref, v_ref, do_ref, dq_ref, dk_ref, dv_ref):
        _, vjp = jax.vjp(_attn_tile, q_ref[...], k_ref[...], v_ref[...])
        dq, dk, dv = vjp(do_ref[...])
        dq_ref[...] = dq.astype(dq_ref.dtype)

        @pl.when(pl.program_id(1) == 0)
        def _():
            dk_ref[...] = jnp.zeros_like(dk_ref)
            dv_ref[...] = jnp.zeros_like(dv_ref)

        dk_ref[...] += dk
        dv_ref[...] += dv

    kv_spec = pl.BlockSpec((N_MEM, X_HEAD_DIM), lambda h, i: (0, h))
    return pl.pallas_call(
        body, grid=(X_HEADS, SEQ // ATT_TQ),
        in_specs=[*_attn_in_specs(q_off // X_HEAD_DIM, order),
                  pl.BlockSpec((ATT_TQ, X_HEAD_DIM), lambda h, i: (i, cat_blk + h))],
        out_specs=[pl.BlockSpec((ATT_TQ, X_HEAD_DIM), lambda h, i: (i, dq_blk + h)), kv_spec, kv_spec],
        out_shape=[SDS((SEQ, dproj_width), BF16), SDS((N_MEM, X_WIDTH), F32), SDS((N_MEM, X_WIDTH), F32)],
        compiler_params=_cparams(("parallel", "arbitrary")), name=name,
    )(proj, kv, kv, dcat)


CONV_TC = 512


def _shift_down(x, s):
    if s == 0:
        return x
    row = lax.broadcasted_iota(jnp.int32, x.shape, 0)
    return jnp.where(row >= s, pltpu.roll(x, s, 0), 0.0)


def _shift_up(x, s):
    if s == 0:
        return x
    n = x.shape[0]
    row = lax.broadcasted_iota(jnp.int32, x.shape, 0)
    return jnp.where(row < n - s, pltpu.roll(x, n - s, 0), 0.0)


def _conv_pre(x, w_ref, b_ref):
    pre = b_ref[...] + jnp.zeros_like(x)
    for k in range(CONV_K):
        pre = pre + w_ref[k:k + 1, :] * _shift_down(x, CONV_K - 1 - k)
    return pre


def _conv_fwd(proj, w, b, *, name):
    blk0 = D_INNER // CONV_TC

    def body(x_ref, w_ref, b_ref, o_ref):
        pre = _conv_pre(x_ref[...], w_ref, b_ref)
        o_ref[...] = pre * jax.nn.sigmoid(pre)

    return pl.pallas_call(
        body, grid=(CONV_DIM // CONV_TC,),
        in_specs=[pl.BlockSpec((SEQ, CONV_TC), lambda j: (0, blk0 + j)), pl.BlockSpec((CONV_K, CONV_TC), lambda j: (0, j)),
                  pl.BlockSpec((1, CONV_TC), lambda j: (0, j))],
        out_specs=pl.BlockSpec((SEQ, CONV_TC), lambda j: (0, j)), out_shape=SDS((SEQ, CONV_DIM), F32),
        compiler_params=_cparams(("parallel",)), name=name)(proj, w, b)


def _conv_bwd(proj, w, b, dxs, dbm, dcm, dproj, *, name):
    tc = CONV_TC // 2
    blk0 = D_INNER // tc
    n_x = D_INNER // tc
    n_b = SSM_GROUPS * SSM_STATE // tc

    def body(x_ref, w_ref, b_ref, dxs_ref, dbm_ref, dcm_ref, dproj_in, dproj_ref, dw_ref, db_ref):
        del dproj_in
        j = pl.program_id(0)
        x = x_ref[...]
        pre = _conv_pre(x, w_ref, b_ref)
        sg = jax.nn.sigmoid(pre)
        dact = jnp.where(j < n_x, dxs_ref[...], jnp.where(j < n_x + n_b, dbm_ref[...], dcm_ref[...]))
        dpre = dact * (sg * (1.0 + pre * (1.0 - sg)))
        dx = jnp.zeros_like(x)
        for k in range(CONV_K):
            s = CONV_K - 1 - k
            dx = dx + w_ref[k:k + 1, :] * _shift_up(dpre, s)
            dw_ref[k:k + 1, :] = jnp.sum(dpre * _shift_down(x, s), axis=0, keepdims=True)
        dproj_ref[...] = dx.astype(dproj_ref.dtype)
        db_ref[...] = jnp.sum(dpre, axis=0, keepdims=True)

    clip = lambda v, hi: jnp.minimum(jnp.maximum(v, 0), hi)
    return pl.pallas_call(
        body, grid=(CONV_DIM // tc,),
        in_specs=[pl.BlockSpec((SEQ, tc), lambda j: (0, blk0 + j)), pl.BlockSpec((CONV_K, tc), lambda j: (0, j)),
                  pl.BlockSpec((1, tc), lambda j: (0, j)),
                  pl.BlockSpec((SEQ, tc), lambda j: (0, clip(j, n_x - 1))),
                  pl.BlockSpec((SEQ, tc), lambda j: (0, clip(j - n_x, n_b - 1))),
                  pl.BlockSpec((SEQ, tc), lambda j: (0, clip(j - n_x - n_b, n_b - 1))),
                  pl.BlockSpec(memory_space=pl.ANY)],
        out_specs=[pl.BlockSpec((SEQ, tc), lambda j: (0, blk0 + j)), pl.BlockSpec((CONV_K, tc), lambda j: (0, j)),
                   pl.BlockSpec((1, tc), lambda j: (0, j))],
        out_shape=[SDS(dproj.shape, dproj.dtype), SDS((CONV_K, CONV_DIM), F32), SDS((1, CONV_DIM), F32)],
        input_output_aliases={6: 0}, compiler_params=_cparams(("parallel",)), name=name,
    )(proj, w, b, dxs, dbm, dcm, dproj)


SSM_PAIRS = SSM_HPG // 2


def _ssd_tile(xp, zp, bm, cm, hp, dtc, dtr, bias, alog, dsk, gnp):
    row = lax.broadcasted_iota(jnp.int32, (CHUNK, CHUNK), 0)
    col = lax.broadcasted_iota(jnp.int32, (CHUNK, CHUNK), 1)
    causal = row >= col
    tri = jnp.where(causal, 1.0, 0.0)
    left = col < SSM_HEAD_DIM
    top = row < SSM_HEAD_DIM
    ones = jnp.ones((CHUNK, CHUNK), BF16)
    cb = _dot_nt(cm, bm)
    dt_c, cs_c, cs_last, m = [], [], [], []
    for r in range(SSM_HPG):
        a = -jnp.exp(alog[r])
        dt_c.append(jax.nn.softplus(dtc[r] + bias[r]))
        da_c = dt_c[r] * a
        da_r = jax.nn.softplus(dtr[r] + bias[r]) * a
        cs_c.append(jnp.sum(tri * da_r, axis=1, keepdims=True))
        cs_r = jnp.sum(jnp.where(row <= col, 1.0, 0.0) * da_c, axis=0, keepdims=True)
        cs_last.append(jnp.sum(da_c, axis=0, keepdims=True))
        m.append(cb * jnp.exp(jnp.where(causal, cs_c[r] - cs_r, -1e30)))
    ygs, hn = [], []
    for p in range(SSM_PAIRS):
        a, b = 2 * p, 2 * p + 1
        pair = lambda u, v: jnp.where(left, u, v)
        xdt = xp[p] * pair(dt_c[a], dt_c[b])
        y = pair(_dot(m[a], xdt), _dot(m[b], xdt))
        y = y + _dot_nt(cm, hp[p]) * pair(jnp.exp(cs_c[a]), jnp.exp(cs_c[b]))
        y = y + xp[p] * pair(dsk[a], dsk[b])
        decay = pair(jnp.exp(cs_last[a] - cs_c[a]), jnp.exp(cs_last[b] - cs_c[b]))
        states = _dot_tn(xdt * decay, bm)
        hn.append(hp[p] * jnp.where(top, jnp.exp(cs_last[a]), jnp.exp(cs_last[b])) + states)
        ygs.append(y * (zp[p] * jax.nn.sigmoid(zp[p])))
    ms = sum(_dot(t * t, ones) for t in ygs) * (1.0 / SSM_GROUP_W)
    rs = lax.rsqrt(ms + EPS)
    return [ygs[p] * rs * gnp[p] for p in range(SSM_PAIRS)], hn


def _ssd_in_specs(cidx):
    gw, n = SSM_GROUP_W, SSM_STATE
    bm_blk = D_INNER // n
    return [
        pl.BlockSpec((CHUNK, gw), lambda g, c: (cidx(c), g)),
        pl.BlockSpec((CHUNK, gw), lambda g, c: (cidx(c), g)),
        pl.BlockSpec((CHUNK, n), lambda g, c: (cidx(c), bm_blk + g)),
        pl.BlockSpec((CHUNK, n), lambda g, c: (cidx(c), bm_blk + SSM_GROUPS + g)),
        pl.BlockSpec((None, CHUNK, SSM_HPG), lambda g, c: (g, cidx(c), 0)),
        pl.BlockSpec((None, SSM_HPG, CHUNK), lambda g, c: (g, 0, cidx(c))),
        pl.BlockSpec((None, 1, SSM_HPG), lambda g, c: (g, 0, 0)),
        pl.BlockSpec((None, 1, SSM_HPG), lambda g, c: (g, 0, 0)),
        pl.BlockSpec((None, 1, SSM_HPG), lambda g, c: (g, 0, 0)),
        pl.BlockSpec((1, gw), lambda g, c: (0, g)),
    ]


def _ssd_args(x_ref, z_ref, bm_ref, cm_ref, hp, dtc_ref, dtr_ref, bias_ref, alog_ref, dsk_ref, gn_ref):
    nh, npair, w = SSM_HPG, SSM_PAIRS, 2 * SSM_HEAD_DIM
    col = lambda ref: [ref[:, r:r + 1] for r in range(nh)]
    return (_split(x_ref, npair, w), _split(z_ref, npair, w), bm_ref[...], cm_ref[...], hp,
            col(dtc_ref), [dtr_ref[r:r + 1, :] for r in range(nh)], col(bias_ref), col(alog_ref), col(dsk_ref),
            _split(gn_ref, npair, w))


def _pair_rows(ref):
    w = 2 * SSM_HEAD_DIM
    return [ref[p * w:(p + 1) * w, :] for p in range(SSM_PAIRS)]


def _ssd_fwd(xbc, proj, dt_c, dt_r, bias, alog, dsk, gn, mixcat, *, name):
    w = 2 * SSM_HEAD_DIM

    def body(x_ref, z_ref, bm_ref, cm_ref, dtc_ref, dtr_ref, bias_ref, alog_ref, dsk_ref, gn_ref, cat_in,
             cat_ref, hprev_ref, h_scr):
        del cat_in

        @pl.when(pl.program_id(1) == 0)
        def _():
            h_scr[...] = jnp.zeros_like(h_scr)

        hprev_ref[...] = h_scr[...]
        yn, hn = _ssd_tile(*_ssd_args(x_ref, z_ref, bm_ref, cm_ref, _pair_rows(h_scr), dtc_ref, dtr_ref, bias_ref,
                                      alog_ref, dsk_ref, gn_ref))
        for p in range(SSM_PAIRS):
            cat_ref[:, p * w:(p + 1) * w] = yn[p].astype(cat_ref.dtype)
            h_scr[p * w:(p + 1) * w, :] = hn[p]

    return pl.pallas_call(
        body, grid=(SSM_GROUPS, N_CHUNKS), in_specs=[*_ssd_in_specs(lambda c: c), pl.BlockSpec(memory_space=pl.ANY)],
        out_specs=[pl.BlockSpec((CHUNK, SSM_GROUP_W), lambda g, c: (c, g)),
                   pl.BlockSpec((None, None, SSM_GROUP_W, SSM_STATE), lambda g, c: (c, g, 0, 0))],
        out_shape=[SDS(mixcat.shape, mixcat.dtype), SDS((N_CHUNKS, SSM_GROUPS, SSM_GROUP_W, SSM_STATE), F32)],
        scratch_shapes=[pltpu.VMEM((SSM_GROUP_W, SSM_STATE), F32)],
        input_output_aliases={10: 0}, compiler_params=_cparams(("parallel", "arbitrary")), name=name,
    )(xbc, proj, xbc, xbc, dt_c, dt_r, bias, alog, dsk, gn, mixcat)


def _ssd_bwd(xbc, proj, dt_c, dt_r, bias, alog, dsk, gn, hprev, dcat, dproj, *, name):
    nh, w, gw, n = SSM_HPG, 2 * SSM_HEAD_DIM, SSM_GROUP_W, SSM_STATE
    rev = lambda c: N_CHUNKS - 1 - c

    def body(x_ref, z_ref, bm_ref, cm_ref, dtc_ref, dtr_ref, bias_ref, alog_ref, dsk_ref, gn_ref, hprev_ref, dy_ref,
             dproj_in, dz_ref, dxs_ref, dbm_ref, dcm_ref, ddtc_ref, ddtr_ref, dbias_ref, dalog_ref, ddsk_ref, dgn_ref,
             dh_scr):
        del dproj_in
        first = pl.program_id(1) == 0

        @pl.when(first)
        def _():
            dh_scr[...] = jnp.zeros_like(dh_scr)
            for ref in (dbias_ref, dalog_ref, ddsk_ref, dgn_ref):
                ref[...] = jnp.zeros_like(ref)

        args = _ssd_args(x_ref, z_ref, bm_ref, cm_ref, _pair_rows(hprev_ref), dtc_ref, dtr_ref, bias_ref, alog_ref,
                         dsk_ref, gn_ref)
        _, vjp = jax.vjp(_ssd_tile, *args)
        dxs, dzs, dbm, dcm, dhs, ddtc, ddtr, dbias, dalog, ddsk, dgn = vjp(
            (_split(dy_ref, SSM_PAIRS, w), _pair_rows(dh_scr)))
        dbm_ref[...] = dbm
        dcm_ref[...] = dcm
        for q in range(SSM_PAIRS):
            dxs_ref[:, q * w:(q + 1) * w] = dxs[q]
            dz_ref[:, q * w:(q + 1) * w] = dzs[q].astype(dz_ref.dtype)
            dh_scr[q * w:(q + 1) * w, :] = dhs[q]
            dgn_ref[:, q * w:(q + 1) * w] += dgn[q]
        for r in range(nh):
            ddtc_ref[:, r:r + 1] = ddtc[r]
            ddtr_ref[r:r + 1, :] = ddtr[r]
            dbias_ref[:, r:r + 1] += dbias[r]
            dalog_ref[:, r:r + 1] += dalog[r]
            ddsk_ref[:, r:r + 1] += ddsk[r]

    par_spec = pl.BlockSpec((None, 1, nh), lambda g, c: (g, 0, 0))
    return pl.pallas_call(
        body, grid=(SSM_GROUPS, N_CHUNKS),
        in_specs=[*_ssd_in_specs(rev),
                  pl.BlockSpec((None, None, gw, n), lambda g, c: (rev(c), g, 0, 0)),
                  pl.BlockSpec((CHUNK, gw), lambda g, c: (rev(c), g)),
                  pl.BlockSpec(memory_space=pl.ANY)],
        out_specs=[pl.BlockSpec((CHUNK, gw), lambda g, c: (rev(c), g)),
                   pl.BlockSpec((CHUNK, gw), lambda g, c: (rev(c), g)),
                   pl.BlockSpec((CHUNK, n), lambda g, c: (rev(c), g)),
                   pl.BlockSpec((CHUNK, n), lambda g, c: (rev(c), g)),
                   pl.BlockSpec((None, CHUNK, nh), lambda g, c: (g, rev(c), 0)),
                   pl.BlockSpec((None, nh, CHUNK), lambda g, c: (g, 0, rev(c))),
                   par_spec, par_spec, par_spec,
                   pl.BlockSpec((1, gw), lambda g, c: (0, g))],
        out_shape=[SDS(dproj.shape, dproj.dtype), SDS((SEQ, D_INNER), F32), SDS((SEQ, SSM_GROUPS * n), F32),
                   SDS((SEQ, SSM_GROUPS * n), F32), SDS((SSM_GROUPS, SEQ, nh), F32), SDS((SSM_GROUPS, nh, SEQ), F32),
                   SDS((SSM_GROUPS, 1, nh), F32), SDS((SSM_GROUPS, 1, nh), F32), SDS((SSM_GROUPS, 1, nh), F32),
                   SDS((1, D_INNER), F32)],
        scratch_shapes=[pltpu.VMEM((gw, n), F32)],
        input_output_aliases={12: 0}, compiler_params=_cparams(("parallel", "arbitrary")), name=name,
    )(xbc, proj, xbc, xbc, dt_c, dt_r, bias, alog, dsk, gn, hprev, dcat, dproj)


def _sum_contributions(chip, parts, landed, *, name):
    _, r, c = parts.shape
    tr = _pick(r, (256, 384, 128))

    def body(chip_ref, own_ref, landed_ref, o_ref):
        del chip_ref
        acc = own_ref[...].astype(F32)
        for s in range(landed_ref.shape[0]):
            acc = acc + landed_ref[s].astype(F32)
        o_ref[...] = acc

    grid_spec = pltpu.PrefetchScalarGridSpec(
        num_scalar_prefetch=1, grid=(r // tr,),
        in_specs=[pl.BlockSpec((None, tr, c), lambda i, chip_ref: (chip_ref[0], i, 0)),
                  pl.BlockSpec((landed.shape[0], tr, c), lambda i, chip_ref: (0, i, 0))],
        out_specs=pl.BlockSpec((tr, c), lambda i, chip_ref: (i, 0)))
    return pl.pallas_call(body, grid_spec=grid_spec, out_shape=SDS((r, c), F32),
                          compiler_params=_cparams(("parallel",)), name=name)(chip, parts, landed)


def _adamw(w, g, m, v, *, name):
    r, c = w.shape
    tr = r if r <= 256 else _pick(r, (256, 128, 8))
    spec = pl.BlockSpec((tr, c), lambda i: (i, 0))

    def body(w_ref, g_ref, m_ref, v_ref, d_ref, mo_ref, vo_ref):
        g = g_ref[...]
        m_new = ADAM_B1 * m_ref[...] + (1.0 - ADAM_B1) * g
        v_new = ADAM_B2 * v_ref[...] + (1.0 - ADAM_B2) * (g * g)
        m_hat = m_new / (1.0 - ADAM_B1 ** ADAM_STEP)
        v_hat = v_new / (1.0 - ADAM_B2 ** ADAM_STEP)
        d_ref[...] = -ADAM_LR * (m_hat / (jnp.sqrt(v_hat) + ADAM_EPS) + ADAM_WD * w_ref[...])
        mo_ref[...] = m_new
        vo_ref[...] = v_new

    return pl.pallas_call(body, grid=(r // tr,), in_specs=[spec] * 4, out_specs=[spec] * 3,
                          out_shape=[SDS((r, c), F32)] * 3, compiler_params=_cparams(("parallel",)), name=name)(w, g, m, v)


ANY = pl.BlockSpec(memory_space=pl.ANY)


def _place():
    x, y, c = lax.axis_index("x"), lax.axis_index("y"), lax.axis_index("c")
    chips = [(1 - x, y), (x, 1 - y), (1 - x, 1 - y)]
    return x, y, c, chips


def _remote(src, dst, send_sem, recv_sem, to):
    return pltpu.make_async_remote_copy(src_ref=src, dst_ref=dst, send_sem=send_sem, recv_sem=recv_sem,
                                        device_id=to, device_id_type=MESH)


STREAM_ROWS = 128


def _stream_rows(i):
    return pl.ds(pl.multiple_of(i * STREAM_ROWS, STREAM_ROWS), STREAM_ROWS)


def _channel_scratch(width, dtype):
    buf = (2, STREAM_ROWS, width)
    return [pltpu.VMEM(buf, dtype), pltpu.VMEM(buf, dtype), *([pltpu.SemaphoreType.DMA((2,))] * 5),
            pltpu.SemaphoreType.REGULAR((2,))]


CHANNEL_REFS = 8


def _copy_through_vmem(src, dst, ch):
    sbuf, _, ld, _, _, st, _, _ = ch
    steps = src.shape[0] // STREAM_ROWS
    assert steps >= 2 and steps * STREAM_ROWS == src.shape[0]

    def load(i, slot):
        return pltpu.make_async_copy(src.at[_stream_rows(i)], sbuf.at[slot], ld.at[slot])

    def store(i, slot):
        return pltpu.make_async_copy(sbuf.at[slot], dst.at[_stream_rows(i)], st.at[slot])

    load(0, 0).start()

    def step(i, carry):
        slot = lax.rem(i, 2)
        nxt = 1 - slot

        @pl.when(i + 1 < steps)
        def _():
            @pl.when(i >= 1)
            def _():
                store(0, nxt).wait()
            load(i + 1, nxt).start()

        load(i, slot).wait()
        store(i, slot).start()
        return carry

    lax.fori_loop(0, steps, step, 0)
    for slot in range(2):
        store(0, slot).wait()


def _exchange_stream(src, dst, keep, ch, sibling):
    sbuf, rbuf, ld, snd, rcv, st, kp, credit = ch
    steps = src.shape[0] // STREAM_ROWS
    assert steps >= 2 and steps * STREAM_ROWS == src.shape[0]

    def load(i, slot):
        return pltpu.make_async_copy(src.at[_stream_rows(i)], sbuf.at[slot], ld.at[slot])

    def push(slot):
        return _remote(sbuf.at[slot], rbuf.at[slot], snd.at[slot], rcv.at[slot], sibling)

    def store(i, slot):
        return pltpu.make_async_copy(rbuf.at[slot], dst.at[_stream_rows(i)], st.at[slot])

    def save(i, slot):
        return pltpu.make_async_copy(sbuf.at[slot], keep.at[_stream_rows(i)], kp.at[slot])

    for slot in range(2):
        pl.semaphore_signal(credit.at[slot], 1, device_id=sibling, device_id_type=MESH)
    load(0, 0).start()

    def step(i, carry):
        slot = lax.rem(i, 2)
        nxt = 1 - slot

        @pl.when(i + 1 < steps)
        def _():
            @pl.when(i >= 1)
            def _():
                push(nxt).wait_send()
                if keep is not None:
                    save(0, nxt).wait()
            load(i + 1, nxt).start()

        load(i, slot).wait()
        pl.semaphore_wait(credit.at[slot], 1)
        push(slot).start()
        if keep is not None:
            save(i, slot).start()
        push(slot).wait_recv()
        store(i, slot).start()

        @pl.when(i >= 1)
        def _():
            store(0, nxt).wait()

            @pl.when(i + 1 < steps)
            def _():
                pl.semaphore_signal(credit.at[nxt], 1, device_id=sibling, device_id_type=MESH)
        return carry

    lax.fori_loop(0, steps, step, 0)
    store(0, (steps - 1) % 2).wait()
    for slot in range(2):
        push(slot).wait_send()
        if keep is not None:
            save(0, slot).wait()


def _all_gather_shards(shards, small, *, name):
    n = len(shards)

    def body(*refs):
        ins, outs = refs[:n + 1], refs[n + 1:2 * n + 2]
        scr = refs[2 * n + 2:]
        chans = [scr[CHANNEL_REFS * t:CHANNEL_REFS * (t + 1)] for t in range(n)]
        send_sems, recv_sems, small_sems = scr[CHANNEL_REFS * n:]
        x, y, c, _ = _place()
        me = 2 * x + y
        sibling = (x, y, 1 - c)
        near = (lax.rem(x + 1 - c, 2), lax.rem(y + c, 2))
        far = (lax.rem(x + c, 2), lax.rem(y + 1 - c, 2))
        k_near, k_far, k_diag = 2 * near[0] + near[1], 2 * far[0] + far[1], 3 - me
        targets = ((*near, c), (*far, c), (*far, c))
        arrives = (k_near, k_far, k_diag)
        streams_in = (k_far, k_near, k_diag)

        def ici(t, j, src, blk):
            return _remote(src, outs[t].at[blk, c], send_sems.at[3 * t + j], recv_sems.at[3 * t + j], targets[j])

        first = [ici(t, j, ins[t].at[c], me) for t in range(n + 1) for j in range(2)]
        for cp in first:
            cp.start()
        small_local = pltpu.make_async_copy(ins[n], outs[n].at[me], small_sems.at[6])
        small_local.start()
        for t in range(n):
            for h in range(2):
                _copy_through_vmem(ins[t].at[h], outs[t].at[me, h], chans[t])
        passed = []
        for j in range(3):
            for t in range(n + 1):
                landed = outs[t].at[arrives[j], c]
                ici(t, j, landed, arrives[j]).wait_recv()
                if j == 0:
                    fwd = ici(t, 2, landed, k_near)
                    fwd.start()
                    passed.append(fwd)
                if t < n:
                    _exchange_stream(landed, outs[t].at[streams_in[j], 1 - c], None, chans[t], sibling)
                else:
                    fwd = _remote(landed, landed, small_sems.at[j], small_sems.at[3 + j], sibling)
                    fwd.start()
                    passed.append(fwd)
        for j in range(3):
            got = outs[n].at[streams_in[j], 1 - c]
            _remote(got, got, small_sems.at[j], small_sems.at[3 + j], sibling).wait_recv()
        for cp in first + passed:
            cp.wait_send()
        small_local.wait()

    scratch = []
    for s in shards:
        scratch += _channel_scratch(s.shape[2], s.dtype)
    return pl.pallas_call(
        body, in_specs=[ANY] * (n + 1), out_specs=[ANY] * (n + 1),
        out_shape=[SDS((N_CHIPS, *s.shape), s.dtype) for s in (*shards, small)],
        scratch_shapes=[*scratch, pltpu.SemaphoreType.DMA((3 * n + 3,)), pltpu.SemaphoreType.DMA((3 * n + 3,)),
                        pltpu.SemaphoreType.DMA((7,))],
        compiler_params=pltpu.CompilerParams(vmem_limit_bytes=VMEM_LIMIT), name=name)(*shards, small)


def _pair_reduce(stacks, *, name):
    n = len(stacks)
    per = 11

    def body(*refs):
        ins, outs, scr = refs[:n], refs[n:2 * n], refs[2 * n:]
        x, y, c, _ = _place()
        sibling = (x, y, 1 - c)
        for t in range(n):
            sraw, sbuf, rbuf, obuf, pbuf, ld_s, ld_o, snd, rcv, st, credit = scr[per * t:per * (t + 1)]
            steps = ins[t].shape[1] // STREAM_ROWS
            src, own, out = ins[t].at[1 - c], ins[t].at[c], outs[t]

            def load_s(i, slot, src=src, sraw=sraw, ld_s=ld_s):
                return pltpu.make_async_copy(src.at[_stream_rows(i)], sraw.at[slot], ld_s.at[slot])

            def load_o(i, slot, own=own, obuf=obuf, ld_o=ld_o):
                return pltpu.make_async_copy(own.at[_stream_rows(i)], obuf.at[slot], ld_o.at[slot])

            def push(slot, sbuf=sbuf, rbuf=rbuf, snd=snd, rcv=rcv):
                return _remote(sbuf.at[slot], rbuf.at[slot], snd.at[slot], rcv.at[slot], sibling)

            def store(i, slot, pbuf=pbuf, out=out, st=st):
                return pltpu.make_async_copy(pbuf.at[slot], out.at[_stream_rows(i)], st.at[slot])

            assert steps >= 2
            for slot in range(2):
                pl.semaphore_signal(credit.at[slot], 1, device_id=sibling, device_id_type=MESH)
                load_s(slot, slot).start()
                load_o(slot, slot).start()
            load_s(0, 0).wait()
            sbuf[0] = sraw[0].astype(sbuf.dtype)
            pl.semaphore_wait(credit.at[0], 1)
            push(0).start()

            def step(i, carry, load_s=load_s, load_o=load_o, push=push, store=store, sraw=sraw, sbuf=sbuf, rbuf=rbuf,
                     obuf=obuf, pbuf=pbuf, credit=credit, steps=steps):
                slot = lax.rem(i, 2)
                nxt = 1 - slot

                @pl.when(i + 1 < steps)
                def _():
                    load_s(i + 1, nxt).wait()
                    sbuf[nxt] = sraw[nxt].astype(sbuf.dtype)
                    pl.semaphore_wait(credit.at[nxt], 1)
                    push(nxt).start()

                load_o(i, slot).wait()
                push(slot).wait_recv()

                @pl.when(i >= 2)
                def _():
                    store(i, slot).wait()

                pbuf[slot] = (obuf[slot] + rbuf[slot].astype(F32)).astype(pbuf.dtype)
                store(i, slot).start()
                push(slot).wait_send()

                @pl.when(i + 2 < steps)
                def _():
                    load_s(i + 2, slot).start()
                    load_o(i + 2, slot).start()
                    pl.semaphore_signal(credit.at[slot], 1, device_id=sibling, device_id_type=MESH)
                return carry

            lax.fori_loop(0, steps, step, 0)
            for slot in range(2):
                store(0, slot).wait()

    scratch = []
    for s in stacks:
        buf = (2, STREAM_ROWS, s.shape[2])
        scratch += [pltpu.VMEM(buf, F32), pltpu.VMEM(buf, BF16), pltpu.VMEM(buf, BF16), pltpu.VMEM(buf, F32),
                    pltpu.VMEM(buf, BF16), *([pltpu.SemaphoreType.DMA((2,))] * 5), pltpu.SemaphoreType.REGULAR((2,))]
    return pl.pallas_call(
        body, in_specs=[ANY] * n, out_specs=[ANY] * n, out_shape=[SDS(s.shape[1:], BF16) for s in stacks],
        scratch_shapes=scratch, compiler_params=pltpu.CompilerParams(vmem_limit_bytes=VMEM_LIMIT), name=name)(*stacks)


def _chip_scatter(parts, *, name):
    n = len(parts)

    def body(*refs):
        ins, outs = refs[:n], refs[n:2 * n]
        send_sems, recv_sems = refs[2 * n:]
        _, _, c, chips = _place()
        copies = [_remote(ins[t].at[2 * cx + cy], outs[t].at[j], send_sems.at[3 * t + j], recv_sems.at[3 * t + j],
                          (cx, cy, c)) for t in range(n) for j, (cx, cy) in enumerate(chips)]
        for cp in copies:
            cp.start()
        for cp in copies:
            cp.wait_recv()
        for cp in copies:
            cp.wait_send()

    return pl.pallas_call(
        body, in_specs=[ANY] * n, out_specs=[ANY] * n, out_shape=[SDS((3, *p.shape[1:]), p.dtype) for p in parts],
        scratch_shapes=[pltpu.SemaphoreType.DMA((3 * n,)), pltpu.SemaphoreType.DMA((3 * n,))], name=name)(*parts)


HBM_SPEC = pl.BlockSpec(memory_space=pltpu.HBM)
SEM_SPEC = pl.BlockSpec(memory_space=pltpu.SEMAPHORE)
SIDE_EFFECT = pltpu.SideEffectType.DATAFLOW_SIDE_EFFECTING


def _scatter_copies(ins, lands, send_sems, recv_sems):
    _, _, c, chips = _place()
    return [_remote(ins[t].at[2 * cx + cy], lands[t].at[j], send_sems.at[3 * t + j], recv_sems.at[3 * t + j],
                    (cx, cy, c)) for t in range(len(ins)) for j, (cx, cy) in enumerate(chips)]


def _chip_scatter_start(parts, *, name):
    n = len(parts)

    def body(*refs):
        ins, lands = refs[:n], refs[n:2 * n]
        send_sems, recv_sems, token = refs[2 * n], refs[2 * n + 1], refs[-1]
        for cp in _scatter_copies(ins, lands, send_sems, recv_sems):
            cp.start()
        token[...] = jnp.zeros_like(token)

    hbm = lambda a: pltpu.with_memory_space_constraint(a, pltpu.HBM)
    lands = [hbm(lax.empty((3, *p.shape[1:]), p.dtype)) for p in parts]
    thru = [pltpu.HBM(a.shape, a.dtype) for a in (*parts, *lands)]
    outs = pl.pallas_call(
        body, name=name,
        out_shape=(pltpu.SemaphoreType.DMA((3 * n,)), pltpu.SemaphoreType.DMA((3 * n,)), *thru, SDS((8, 128), F32)),
        in_specs=[HBM_SPEC] * (2 * n),
        out_specs=(SEM_SPEC, SEM_SPEC, *([HBM_SPEC] * (2 * n)), pl.BlockSpec(memory_space=pltpu.VMEM)),
        input_output_aliases={i: 2 + i for i in range(2 * n)},
        compiler_params=pltpu.CompilerParams(has_side_effects=SIDE_EFFECT),
    )(*[hbm(p) for p in parts], *lands)
    return outs[0], outs[1], outs[2:2 + n], outs[2 + n:2 + 2 * n], outs[-1]


def _chip_scatter_wait(send_sems, recv_sems, parts, lands, after, *, name):
    n = len(parts)

    def body(*refs):
        ins, lands_in = refs[:n], refs[n:2 * n]
        for cp in _scatter_copies(ins, lands_in, refs[2 * n], refs[2 * n + 1]):
            cp.wait_send()
            cp.wait_recv()

    outs = pl.pallas_call(
        body, name=name, out_shape=[pltpu.HBM(a.shape, a.dtype) for a in (*parts, *lands)],
        in_specs=[*([HBM_SPEC] * (2 * n)), SEM_SPEC, SEM_SPEC, *([ANY] * len(after))],
        out_specs=[HBM_SPEC] * (2 * n), input_output_aliases={i: i for i in range(2 * n)},
        compiler_params=pltpu.CompilerParams(has_side_effects=SIDE_EFFECT),
    )(*parts, *lands, send_sems, recv_sems, *after)
    return outs[:n], outs[n:]


def _pair_share(groups, *, name):
    finals = [f for grp in groups for f in grp]
    n, n_out = len(finals), len(groups)

    def body(*refs):
        ins, outs, scr = refs[:n], refs[n:n + n_out], refs[n + n_out:]
        x, y, c, _ = _place()
        sibling = (x, y, 1 - c)
        t = 0
        for o, grp in enumerate(groups):
            for layer in range(len(grp)):
                _exchange_stream(ins[t], outs[o].at[layer, 1 - c], outs[o].at[layer, c],
                                 scr[CHANNEL_REFS * t:CHANNEL_REFS * (t + 1)], sibling)
                t += 1

    scratch = []
    for f in finals:
        scratch += _channel_scratch(f.shape[1], f.dtype)
    return pl.pallas_call(
        body, in_specs=[ANY] * n, out_specs=[ANY] * n_out,
        out_shape=[SDS((len(grp), 2, *grp[0].shape), grp[0].dtype) for grp in groups],
        scratch_shapes=scratch, compiler_params=pltpu.CompilerParams(vmem_limit_bytes=VMEM_LIMIT), name=name)(*finals)


def _all_reduce_small(v, *, name):
    rows, lanes = v.shape
    n_dev = 8

    def body(v_ref, o_ref, all_ref, send_sems, recv_sems, local_sem):
        x, y, c, chips = _place()
        me, sibling = (x, y, c), (x, y, 1 - c)

        def block(px, py, pc):
            return all_ref.at[4 * px + 2 * py + pc]

        def copy(k, blk, to, src=None):
            return _remote(block(*blk) if src is None else src, block(*blk), send_sems.at[k], recv_sems.at[k], to)

        mine = pltpu.make_async_copy(v_ref, block(*me), local_sem)
        mine.start()
        first = [copy(0, me, sibling, src=v_ref)]
        first += [copy(1 + j, me, (*chip, c), src=v_ref) for j, chip in enumerate(chips)]
        for cp in first:
            cp.start()
        passed = [copy(4 + j, (*chip, c), sibling) for j, chip in enumerate(chips)]
        for j, chip in enumerate(chips):
            copy(1 + j, (*chip, c), me).wait_recv()
            passed[j].start()
        copy(0, sibling, me).wait_recv()
        for j, chip in enumerate(chips):
            copy(4 + j, (*chip, 1 - c), me).wait_recv()
        for cp in first + passed:
            cp.wait_send()
        mine.wait()
        acc = all_ref[0]
        for k in range(1, n_dev):
            acc = acc + all_ref[k]
        o_ref[...] = acc

    vmem = pl.BlockSpec(memory_space=pltpu.VMEM)
    return pl.pallas_call(
        body, in_specs=[vmem], out_specs=vmem, out_shape=SDS((rows, lanes), F32),
        scratch_shapes=[pltpu.VMEM((n_dev, rows, lanes), F32), pltpu.SemaphoreType.DMA((7,)),
                        pltpu.SemaphoreType.DMA((7,)), pltpu.SemaphoreType.DMA],
        compiler_params=pltpu.CompilerParams(vmem_limit_bytes=VMEM_LIMIT), name=name)(v)


def _relu2_epilogue(acc):
    return acc, jnp.square(jnp.maximum(acc, 0.0))


def _res_epilogue(acc, res):
    return (acc + res,)


def _drelu2_epilogue(acc, pre):
    return (acc * (2.0 * jnp.maximum(pre.astype(F32), 0.0)),)


def _ffn_fwd(h, g, w1, w2, tag):
    f = _rms_fwd(h, g, name=f"ffn_norm_{tag}")
    pre, act = _mm_nn(f, w1, name=f"ffn1_{tag}", epilogue=_relu2_epilogue, n_out_dtypes=(BF16, BF16))
    h_out = _mm_nn(act, w2, name=f"ffn2_{tag}", extras=(h,), epilogue=_res_epilogue)
    return h_out, (f, pre, act)


def _ffn_bwd(dh, h, g, w1, w2, saved, layer, after=()):
    f, pre, act = saved
    dpre = _mm_nt(dh, w2, name=f"ffn2_dx_{layer}", out_dtype=BF16, extras=(pre,), epilogue=_drelu2_epilogue,
                  after=after)
    dw2 = _mm_tn_stacked(act, dh, name=f"ffn2_dw_{layer}", col_slots=False)
    df = _mm_nt(dpre, w1, name=f"ffn1_dx_{layer}")
    dw1 = _mm_tn_stacked(f, dpre, name=f"ffn1_dw_{layer}", col_slots=True)
    dh, dg = _rms_bwd(h, g, df, dh, name=f"ffn_norm_bwd_{layer}")
    return dh, dg, dw1, dw2


def _kv_fwd(mem, g, w_kv, tag):
    m = _rms_fwd(mem, g, name=f"mem_norm_{tag}")
    return m, _mm_nn(m, w_kv, name=f"kv_{tag}")


def _kv_bwd(mem, g, w_kv, m, dk, dv, layer):
    dkv = jnp.concatenate([dk, dv], axis=1)
    dw = _mm_tn_stacked(m, dkv, name=f"kv_dw_{layer}", col_slots=True)
    dm = _mm_nt(dkv, w_kv, name=f"kv_dx_{layer}")
    _, dg = _rms_bwd(mem, g, dm, dm, name=f"mem_norm_bwd_{layer}")
    return dw, dg


def _local_step(x, mem, target, p, after_layer1=None, after_ffn0=None):
    row = lambda v: v.reshape(1, -1)
    g = {}

    h0 = x
    a0 = _rms_fwd(h0, row(p["norm_mix"][0]), name="mix_norm_0")
    proj_a = _mm_nn(a0, p["a_in"], name="a_in")
    m0, kv0 = _kv_fwd(mem, row(p["mem_norm"][0]), p["w_kv"][0], "0")
    cat0 = _attn_fwd(proj_a, 2 * D_INNER, kv0, name="attn_0")
    bs_col = p["a_bs"].reshape(A_GROUPS, CHUNK, 1)
    cat0 = _gate_fwd(proj_a, p["a_ln_g"], p["a_ln_b"], p["a_ws"], bs_col, cat0, name="gate")
    h1 = _mm_nn(cat0, p["w_out"][0], name="out_0", extras=(h0,), epilogue=_res_epilogue)
    h2, ffn0 = _ffn_fwd(h1, row(p["norm_ffn"][0]), p["w_ffn1"][0], p["w_ffn2"][0], "0")

    a1 = _rms_fwd(h2, row(p["norm_mix"][1]), name="mix_norm_1")
    proj_b = _mm_nn(a1, p["b_in"], name="b_in")
    m1, kv1 = _kv_fwd(mem, row(p["mem_norm"][1]), p["w_kv"][1], "1")
    cat1 = _attn_fwd(proj_b, B_Q_OFF, kv1, name="attn_1")
    xbc = _conv_fwd(proj_b, p["b_conv_w"], p["b_conv_b"], name="conv")
    dt_raw = proj_b[:, B_DT_OFF:B_DT_OFF + SSM_HEADS].reshape(SEQ, SSM_GROUPS, SSM_HPG)
    dt_c = jnp.transpose(dt_raw, (1, 0, 2))
    dt_r = jnp.transpose(dt_raw, (1, 2, 0))
    per_head = lambda v: v.reshape(SSM_GROUPS, 1, SSM_HPG)
    ssd_par = (per_head(p["b_dt_bias"]), per_head(p["b_a_log"]), per_head(p["b_d"]), p["b_gnorm"])
    cat1, hprev = _ssd_fwd(xbc, proj_b, dt_c, dt_r, *ssd_par, cat1, name="ssd")
    h3 = _mm_nn(cat1, p["w_out"][1], name="out_1", extras=(h2,), epilogue=_res_epilogue)
    h4, ffn1 = _ffn_fwd(h3, row(p["norm_ffn"][1]), p["w_ffn1"][1], p["w_ffn2"][1], "1")

    loss, dh, g["final_norm"] = _loss_head(h4, row(p["final_norm"]), target, name="loss_head")

    dh, dnf1, dw1_1, dw2_1 = _ffn_bwd(dh, h3, row(p["norm_ffn"][1]), p["w_ffn1"][1], p["w_ffn2"][1], ffn1, 1)
    dcat1 = _mm_nt(dh, p["w_out"][1], name="out_dx_1")
    dwo_1 = _mm_tn_stacked(cat1, dh, name="out_dw_1", col_slots=False)
    dproj_b, dk1, dv1 = _attn_bwd(proj_b, B_Q_OFF, kv1, dcat1, B_IN_PAD, B_Q_OFF, name="attn_bwd_1")
    (dproj_b, dxs, dbm, dcm, ddt_c, ddt_r, g["b_dt_bias"], g["b_a_log"], g["b_d"], g["b_gnorm"]) = _ssd_bwd(
        xbc, proj_b, dt_c, dt_r, *ssd_par, hprev, dcat1, dproj_b, name="ssd_bwd")
    dproj_b, g["b_conv_w"], g["b_conv_b"] = _conv_bwd(proj_b, p["b_conv_w"], p["b_conv_b"], dxs, dbm, dcm, dproj_b,
                                                      name="conv_bwd")
    ddt = jnp.transpose(ddt_c, (1, 0, 2)) + jnp.transpose(ddt_r, (2, 0, 1))
    ddt = jnp.pad(ddt.reshape(SEQ, SSM_HEADS), ((0, 0), (0, B_IN_PAD - B_DT_OFF - SSM_HEADS))).astype(BF16)
    dproj_b = lax.dynamic_update_slice(dproj_b, ddt, (0, B_DT_OFF))
    dwkv_1, dmn1 = _kv_bwd(mem, row(p["mem_norm"][1]), p["w_kv"][1], m1, dk1, dv1, 1)
    dwb = _b_in_grad_slots(_mm_tn(a1, dproj_b, name="b_in_dw"))
    da1 = _mm_nt(dproj_b, p["b_in"], name="b_in_dx")
    dh, dnm1 = _rms_bwd(h2, row(p["norm_mix"][1]), da1, dh, name="mix_norm_bwd_1")
    layer1 = dict(w_kv=dwkv_1, w_out=dwo_1, w_ffn1=dw1_1, w_ffn2=dw2_1, b_in=dwb)
    token = () if after_layer1 is None else (after_layer1(layer1),)

    dh, dnf0, dw1_0, dw2_0 = _ffn_bwd(dh, h1, row(p["norm_ffn"][0]), p["w_ffn1"][0], p["w_ffn2"][0], ffn0, 0,
                                      after=token)
    ffn0_grads = dict(w_ffn1=dw1_0, w_ffn2=dw2_0)
    token = () if after_ffn0 is None else (after_ffn0(ffn0_grads),)
    dcat0 = _mm_nt(dh, p["w_out"][0], name="out_dx_0", after=token)
    dwo_0 = _mm_tn_stacked(cat0, dh, name="out_dw_0", col_slots=False)
    dproj_a, dk0, dv0 = _attn_bwd(proj_a, 2 * D_INNER, kv0, dcat0, A_IN, 2 * D_INNER, name="attn_bwd_0")
    dproj_a, g["a_ln_g"], g["a_ln_b"], g["a_ws"], dbs_col = _gate_bwd(
        proj_a, p["a_ln_g"], p["a_ln_b"], p["a_ws"], bs_col, dcat0, dproj_a, name="gate_bwd")
    g["a_bs"] = dbs_col.reshape(A_GROUPS, CHUNK)
    dwkv_0, dmn0 = _kv_bwd(mem, row(p["mem_norm"][0]), p["w_kv"][0], m0, dk0, dv0, 0)
    dwa = _mm_tn_stacked(a0, dproj_a, name="a_in_dw", col_slots=True)
    da0 = _mm_nt(dproj_a, p["a_in"], name="a_in_dx")
    dx, dnm0 = _rms_bwd(h0, row(p["norm_mix"][0]), da0, dh, name="mix_norm_bwd_0")

    g["norm_mix"] = jnp.concatenate([dnm0, dnm1], axis=0)
    g["norm_ffn"] = jnp.concatenate([dnf0, dnf1], axis=0)
    g["mem_norm"] = jnp.concatenate([dmn0, dmn1], axis=0)
    layer0 = dict(w_kv=dwkv_0, w_out=dwo_0, w_ffn1=dw1_0, w_ffn2=dw2_0, a_in=dwa)
    return loss, dx, g, layer0, layer1


def _b_in_full(gathered):
    n = B_IN // N_CHIPS
    dt0 = D_INNER + CONV_DIM - (N_CHIPS - 1) * n
    last = gathered[N_CHIPS - 1]
    return jnp.concatenate([*[gathered[k] for k in range(N_CHIPS - 1)], last[:, :dt0], last[:, dt0 + SSM_HEADS:],
                            last[:, dt0:dt0 + SSM_HEADS], jnp.zeros((D_MODEL, B_IN_PAD - B_IN), last.dtype)], axis=1)


def _b_in_grad_slots(d):
    n = B_IN // N_CHIPS
    dt0 = D_INNER + CONV_DIM
    last = jnp.concatenate([d[:, (N_CHIPS - 1) * n:dt0], d[:, B_DT_OFF:B_DT_OFF + SSM_HEADS], d[:, dt0:B_DT_OFF]], axis=1)
    slots = [*[d[:, k * n:(k + 1) * n] for k in range(N_CHIPS - 1)], last]
    half = D_MODEL // 2
    return jnp.stack([jnp.stack([s[h * half:(h + 1) * half] for s in slots]) for h in range(2)])


LARGE = ("w_kv", "w_out", "w_ffn1", "w_ffn2", "a_in", "b_in")
SMALL_REPL = ("norm_mix", "norm_ffn", "mem_norm", "a_ln_g", "a_ln_b", "a_ws", "a_bs", "b_dt_bias", "b_a_log", "b_d",
              "final_norm")
SMALL_SHARD = ("b_conv_w", "b_conv_b", "b_gnorm")
WEIGHTS = ("norm_mix", "norm_ffn", "mem_norm", "w_kv", "w_out", "w_ffn1", "w_ffn2", "a_in", "a_ln_g", "a_ln_b", "a_ws",
           "a_bs", "b_in", "b_conv_w", "b_conv_b", "b_dt_bias", "b_a_log", "b_d", "b_gnorm", "final_norm")
CONV_SHARD = CONV_DIM // N_CHIPS
GN_SHARD = D_INNER // N_CHIPS


LAYERED = ("w_kv", "w_out", "w_ffn1", "w_ffn2")
LAYER_TENSORS = (("w_kv", "w_out", "w_ffn1", "w_ffn2", "a_in"), ("w_kv", "w_out", "w_ffn1", "w_ffn2", "b_in"))


def _gather_weights(w):
    halves = lambda a: a.reshape(2, a.shape[0] // 2, a.shape[1]).astype(BF16)
    big = [halves(w[k][layer] if k in LAYERED else w[k][0]) for layer in range(2) for k in LAYER_TENSORS[layer]]
    small = jnp.zeros((2, CONV_K, CONV_SHARD), F32)
    small = small.at[0].set(w["b_conv_w"][0])
    small = small.at[1, 0].set(w["b_conv_b"][0])
    small = small.at[1, 1, :GN_SHARD].set(w["b_gnorm"][0])
    gathered = _all_gather_shards(big, small, name="gather_weights")
    got = [dict(zip(LAYER_TENSORS[layer], gathered[5 * layer:5 * layer + 5])) for layer in range(2)]
    slots = lambda a: a.reshape(N_CHIPS, -1, a.shape[-1])
    rows = lambda a: a.reshape(-1, a.shape[-1])
    p = {}
    p["w_kv"] = [slots(got[layer]["w_kv"]) for layer in range(2)]
    p["w_out"] = [rows(got[layer]["w_out"]) for layer in range(2)]
    p["w_ffn1"] = [slots(got[layer]["w_ffn1"]) for layer in range(2)]
    p["w_ffn2"] = [rows(got[layer]["w_ffn2"]) for layer in range(2)]
    p["a_in"] = slots(got[0]["a_in"])
    p["b_in"] = _b_in_full(slots(got[1]["b_in"]))
    sm = gathered[-1]
    p["b_conv_w"] = jnp.transpose(sm[:, 0], (1, 0, 2)).reshape(CONV_K, CONV_DIM)
    p["b_conv_b"] = sm[:, 1, 0].reshape(1, CONV_DIM)
    p["b_gnorm"] = sm[:, 1, 1, :GN_SHARD].reshape(1, D_INNER)
    return p


def _pair_parts(grads, tag):
    stacks = [g.reshape(2, -1, g.shape[-1]) for g in grads.values()]
    parts = _pair_reduce(stacks, name=f"grads_pair_reduce_{tag}")
    return [t.reshape(N_CHIPS, -1, t.shape[-1]) for t in parts]


def _chip_sums(chip, names, parts, landed, tag):
    return {k: _sum_contributions(chip, t, u, name=f"grads_chip_sum_{k}_{tag}")
            for k, t, u in zip(names, parts, landed)}


def _small_layout(shapes):
    offs, o = {}, 0
    for k in (*SMALL_REPL, *SMALL_SHARD):
        size = math.prod(shapes[k])
        offs[k] = (o, size)
        o += size
    rows = -(-o // (8 * 128)) * 8
    return offs, rows


def _reduce_small(g, full_shapes):
    offs, rows = _small_layout(full_shapes)
    flat = jnp.concatenate([g[k].reshape(-1) for k in (*SMALL_REPL, *SMALL_SHARD)])
    flat = jnp.pad(flat, (0, rows * 128 - flat.shape[0])).reshape(rows, 128)
    total = _all_reduce_small(flat, name="grads_small_all_reduce").reshape(-1)
    return {k: total[o:o + n].reshape(full_shapes[k]) for k, (o, n) in offs.items()}


def kernel(x, mem, norm_mix, norm_ffn, mem_norm, w_kv, w_out, w_ffn1, w_ffn2, a_in, a_ln_g, a_ln_b, a_ws, a_bs, b_in, b_conv_w, b_conv_b, b_dt_bias, b_a_log, b_d, b_gnorm, final_norm, loss_target, m_norm_mix, m_norm_ffn, m_mem_norm, m_w_kv, m_w_out, m_w_ffn1, m_w_ffn2, m_a_in, m_a_ln_g, m_a_ln_b, m_a_ws, m_a_bs, m_b_in, m_b_conv_w, m_b_conv_b, m_b_dt_bias, m_b_a_log, m_b_d, m_b_gnorm, m_final_norm, v_norm_mix, v_norm_ffn, v_mem_norm, v_w_kv, v_w_out, v_w_ffn1, v_w_ffn2, v_a_in, v_a_ln_g, v_a_ln_b, v_a_ws, v_a_bs, v_b_in, v_b_conv_w, v_b_conv_b, v_b_dt_bias, v_b_a_log, v_b_d, v_b_gnorm, v_final_norm):
    w = dict(norm_mix=norm_mix, norm_ffn=norm_ffn, mem_norm=mem_norm, w_kv=w_kv, w_out=w_out, w_ffn1=w_ffn1,
             w_ffn2=w_ffn2, a_in=a_in, a_ln_g=a_ln_g, a_ln_b=a_ln_b, a_ws=a_ws, a_bs=a_bs, b_in=b_in, b_conv_w=b_conv_w,
             b_conv_b=b_conv_b, b_dt_bias=b_dt_bias, b_a_log=b_a_log, b_d=b_d, b_gnorm=b_gnorm, final_norm=final_norm)
    mom = dict(norm_mix=m_norm_mix, norm_ffn=m_norm_ffn, mem_norm=m_mem_norm, w_kv=m_w_kv, w_out=m_w_out,
               w_ffn1=m_w_ffn1, w_ffn2=m_w_ffn2, a_in=m_a_in, a_ln_g=m_a_ln_g, a_ln_b=m_a_ln_b, a_ws=m_a_ws,
               a_bs=m_a_bs, b_in=m_b_in, b_conv_w=m_b_conv_w, b_conv_b=m_b_conv_b, b_dt_bias=m_b_dt_bias,
               b_a_log=m_b_a_log, b_d=m_b_d, b_gnorm=m_b_gnorm, final_norm=m_final_norm)
    var = dict(norm_mix=v_norm_mix, norm_ffn=v_norm_ffn, mem_norm=v_mem_norm, w_kv=v_w_kv, w_out=v_w_out,
               w_ffn1=v_w_ffn1, w_ffn2=v_w_ffn2, a_in=v_a_in, a_ln_g=v_a_ln_g, a_ln_b=v_a_ln_b, a_ws=v_a_ws,
               a_bs=v_a_bs, b_in=v_b_in, b_conv_w=v_b_conv_w, b_conv_b=v_b_conv_b, b_dt_bias=v_b_dt_bias,
               b_a_log=v_b_a_log, b_d=v_b_d, b_gnorm=v_b_gnorm, final_norm=v_final_norm)

    p = _gather_weights(w)
    p.update(norm_mix=norm_mix, norm_ffn=norm_ffn, mem_norm=mem_norm, a_ln_g=a_ln_g, a_ln_b=a_ln_b, a_ws=a_ws[0],
             a_bs=a_bs[0], b_dt_bias=b_dt_bias, b_a_log=b_a_log, b_d=b_d, final_norm=final_norm)
    chip = 2 * lax.axis_index("x") + lax.axis_index("y")
    chip_arr = jnp.reshape(chip, (1,)).astype(jnp.int32)
    started = {}

    def start_scatter(tag):
        def hook(grads):
            started[tag] = (tuple(grads), _chip_scatter_start(_pair_parts(grads, tag), name=f"grads_chip_scatter_start_{tag}"))
            return started[tag][1][-1]
        return hook

    loss_part, dx, g, layer0, _ = _local_step(x[0], mem[0], loss_target[0], p, start_scatter("1"), start_scatter("0f"))
    loss = lax.psum(loss_part[0, 0], ("x", "y", "c"))

    full_shapes = {k: w[k].shape for k in SMALL_REPL}
    full_shapes.update(b_conv_w=(1, CONV_K, CONV_DIM), b_conv_b=(1, CONV_DIM), b_gnorm=(1, D_INNER))
    gs = _reduce_small(g, full_shapes)
    gs["b_conv_w"] = lax.dynamic_slice_in_dim(gs["b_conv_w"], chip * CONV_SHARD, CONV_SHARD, axis=2)
    gs["b_conv_b"] = lax.dynamic_slice_in_dim(gs["b_conv_b"], chip * CONV_SHARD, CONV_SHARD, axis=1)
    gs["b_gnorm"] = lax.dynamic_slice_in_dim(gs["b_gnorm"], chip * GN_SHARD, GN_SHARD, axis=1)
    mixer0 = {k: layer0[k] for k in ("w_kv", "w_out", "a_in")}
    parts_m = _pair_parts(mixer0, "0m")
    landed_m = _chip_scatter(parts_m, name="grads_chip_scatter_0m")
    halves = [_chip_sums(chip_arr, tuple(mixer0), parts_m, landed_m, "0m"), {}]
    for tag, layer in (("0f", 0), ("1", 1)):
        names, (send_sems, recv_sems, parts, lands, _) = started[tag]
        parts, landed = _chip_scatter_wait(send_sems, recv_sems, parts, lands, (dx,),
                                           name=f"grads_chip_scatter_wait_{tag}")
        halves[layer].update(_chip_sums(chip_arr, names, parts, landed, tag))
    groups = [[halves[layer][k] for layer in range(2) if k in halves[layer]] for k in LARGE]
    gl = dict(zip(LARGE, _pair_share(groups, name="grads_pair_share")))
    grads = {k: (gl[k].reshape(w[k].shape) if k in gl else gs[k]) for k in WEIGHTS}

    delta, new_m, new_v = {}, {}, {}
    for k in WEIGHTS:
        shape = w[k].shape
        flat = (lambda a: a.reshape(-1, shape[-1])) if len(shape) > 1 else (lambda a: a.reshape(1, -1))
        d, m_new, v_new = _adamw(flat(w[k]), flat(grads[k]), flat(mom[k]), flat(var[k]), name=f"adamw_{k}")
        delta[k], new_m[k], new_v[k] = d.reshape(shape), m_new.reshape(shape), v_new.reshape(shape)

    return (loss, dx.reshape(x.shape), *[grads[k] for k in WEIGHTS], *[delta[k] for k in WEIGHTS],
            *[new_m[k] for k in WEIGHTS], *[new_v[k] for k in WEIGHTS])
```

```python
import math

import jax
import jax.numpy as jnp
from jax import lax
from jax.experimental import pallas as pl
from jax.experimental.pallas import tpu as pltpu

F32 = jnp.float32
BF16 = jnp.bfloat16
SDS = jax.ShapeDtypeStruct

D_MODEL = 1024
SEQ = 2048
CHUNK = 128
N_MEM = 256
D_INNER = 2048
A_GROUPS = 8
A_GROUP_W = D_INNER // A_GROUPS
SSM_HEADS = 32
SSM_HEAD_DIM = 64
SSM_GROUPS = 4
SSM_HPG = 8
SSM_STATE = 128
SSM_GROUP_W = SSM_HPG * SSM_HEAD_DIM
CONV_K = 4
CONV_DIM = 3072
X_HEADS = 4
X_HEAD_DIM = 256
X_WIDTH = 1024
MIX_OUT = 3072
D_FF = 4096
A_IN = 5120
B_IN = 6176
B_IN_PAD = 6272
B_Q_OFF = 5120
B_DT_OFF = 6144
N_CHUNKS = SEQ // CHUNK
EPS = 1e-6
N_CHIPS = 4

ADAM_LR = 0.001
ADAM_B1 = 0.9
ADAM_B2 = 0.999
ADAM_EPS = 1e-08
ADAM_WD = 0.01
ADAM_STEP = 10

VMEM_LIMIT = 48 * 1024 * 1024
MESH = pl.DeviceIdType.MESH


def _cparams(sem):
    return pltpu.CompilerParams(dimension_semantics=sem, vmem_limit_bytes=VMEM_LIMIT)


def _dot(a, b, dims=(((1,), (0,)), ((), ()))):
    return lax.dot_general(a.astype(BF16), b.astype(BF16), dims, preferred_element_type=F32)


def _dot_nt(a, b):
    return _dot(a, b, (((1,), (1,)), ((), ())))


def _dot_tn(a, b):
    return _dot(a, b, (((0,), (0,)), ((), ())))


def _pick(n, cands):
    for c in cands:
        if n % c == 0:
            return c
    raise ValueError(f"no tile for {n}")


def _mm_call(a, b, *, dims, grid, a_spec, b_spec, acc_shape, out_shapes, out_specs, name,
             extras=(), extra_specs=(), epilogue=None, after=()):
    n_k = grid[2]
    n_extra = len(extras)
    n_out = len(out_shapes)
    n_in = 2 + n_extra + len(after)

    def body(*refs):
        a_ref, b_ref = refs[0], refs[1]
        extra_refs = refs[2:2 + n_extra]
        out_refs = refs[n_in:n_in + n_out]
        acc = refs[-1]
        k = pl.program_id(2)

        @pl.when(k == 0)
        def _():
            acc[...] = jnp.zeros_like(acc)

        acc[...] += _dot(a_ref[...], b_ref[...], dims)

        @pl.when(k == n_k - 1)
        def _():
            vals = (acc[...],) if epilogue is None else epilogue(acc[...], *[e[...] for e in extra_refs])
            for o_ref, v in zip(out_refs, vals):
                o_ref[...] = v.astype(o_ref.dtype)

    return pl.pallas_call(
        body, grid=grid, in_specs=[a_spec, b_spec, *extra_specs, *([ANY] * len(after))], out_specs=list(out_specs),
        out_shape=list(out_shapes), scratch_shapes=[pltpu.VMEM(acc_shape, F32)],
        compiler_params=_cparams(("parallel", "parallel", "arbitrary")), name=name,
    )(a, b, *extras, *after)


def _w_dims(w):
    if w.ndim == 2:
        return w.shape[0], w.shape[1], 1, w.shape[1]
    return w.shape[1], w.shape[0] * w.shape[2], w.shape[0], w.shape[2]


def _mm_nn(a, w, *, name, out_dtype=F32, a_cols=None, extras=(), epilogue=None, n_out_dtypes=None, after=()):
    m = a.shape[0]
    k_dim, n_dim, _, n_slot = _w_dims(w)
    a_off, a_w = (0, a.shape[1]) if a_cols is None else a_cols
    assert a_w == k_dim
    tm = _pick(m, (2048, 1024, 512, 256))
    tn = _pick(n_slot, (512, 896, 640, 256, 128))
    tk = _pick(k_dim, (1024, 768, 512, 384, 256, 128))
    assert a_off % tk == 0
    nb = n_slot // tn
    a_spec = pl.BlockSpec((tm, tk), lambda i, j, k: (i, a_off // tk + k))
    if w.ndim == 2:
        b_spec = pl.BlockSpec((tk, tn), lambda i, j, k: (k, j))
    else:
        b_spec = pl.BlockSpec((None, tk, tn), lambda i, j, k: (j // nb, k, j % nb))
    o_spec = pl.BlockSpec((tm, tn), lambda i, j, k: (i, j))
    dts = n_out_dtypes or (out_dtype,)
    outs = _mm_call(a, w, dims=(((1,), (0,)), ((), ())), grid=(m // tm, n_dim // tn, k_dim // tk),
                    a_spec=a_spec, b_spec=b_spec, acc_shape=(tm, tn),
                    out_shapes=[SDS((m, n_dim), dt) for dt in dts], out_specs=[o_spec] * len(dts), name=name,
                    extras=extras, extra_specs=[o_spec] * len(extras), epilogue=epilogue, after=after)
    return outs if n_out_dtypes else outs[0]


def _mm_nt(a, w, *, name, out_dtype=F32, extras=(), epilogue=None, after=()):
    m = a.shape[0]
    k_dim, n_dim, _, n_slot = _w_dims(w)
    assert a.shape[1] == n_dim
    tm = _pick(m, (2048, 1024, 512, 256))
    to = _pick(k_dim, (512, 384, 256, 128))
    tc = _pick(n_slot, (1024, 896, 640, 512, 256, 128))
    nb = n_slot // tc
    a_spec = pl.BlockSpec((tm, tc), lambda i, j, k: (i, k))
    if w.ndim == 2:
        b_spec = pl.BlockSpec((to, tc), lambda i, j, k: (j, k))
    else:
        b_spec = pl.BlockSpec((None, to, tc), lambda i, j, k: (k // nb, j, k % nb))
    o_spec = pl.BlockSpec((tm, to), lambda i, j, k: (i, j))
    return _mm_call(a, w, dims=(((1,), (1,)), ((), ())), grid=(m // tm, k_dim // to, n_dim // tc),
                    a_spec=a_spec, b_spec=b_spec, acc_shape=(tm, to),
                    out_shapes=[SDS((m, k_dim), out_dtype)], out_specs=[o_spec], name=name,
                    extras=extras, extra_specs=[o_spec] * len(extras), epilogue=epilogue, after=after)[0]


def _mm_tn(x, dy, *, name, x_cols=None):
    s = x.shape[0]
    x_off, k_dim = (0, x.shape[1]) if x_cols is None else x_cols
    n_dim = dy.shape[1]
    tm = _pick(k_dim, (1024, 768, 512, 384, 256, 128))
    tn = _pick(n_dim, (512, 896, 640, 256, 128))
    tk = _pick(s, (2048, 1024, 512, 256))
    assert x_off % tm == 0
    a_spec = pl.BlockSpec((tk, tm), lambda i, j, k: (k, x_off // tm + i))
    b_spec = pl.BlockSpec((tk, tn), lambda i, j, k: (k, j))
    o_spec = pl.BlockSpec((tm, tn), lambda i, j, k: (i, j))
    return _mm_call(x, dy, dims=(((0,), (0,)), ((), ())), grid=(k_dim // tm, n_dim // tn, s // tk),
                    a_spec=a_spec, b_spec=b_spec, acc_shape=(tm, tn),
                    out_shapes=[SDS((k_dim, n_dim), F32)], out_specs=[o_spec], name=name)[0]


def _mm_tn_stacked(x, dy, *, name, col_slots):
    s, k_dim = x.shape
    n_dim = dy.shape[1]
    r, c = (k_dim // 2, n_dim // N_CHIPS) if col_slots else (k_dim // N_CHIPS // 2, n_dim)
    tm = 2 * r
    tn = _pick(c, (512, 896, 640, 256, 128))
    tk = _pick(s, (2048, 1024, 512, 256))
    a_spec = pl.BlockSpec((tk, tm), lambda i, j, k: (k, i))
    b_spec = pl.BlockSpec((tk, tn), lambda i, j, k: (k, j))
    if col_slots:
        nb = c // tn
        o_spec = pl.BlockSpec((2, None, r, tn), lambda i, j, k: (0, j // nb, 0, j % nb))
    else:
        o_spec = pl.BlockSpec((2, None, r, tn), lambda i, j, k: (0, i, 0, j))
    return _mm_call(x, dy, dims=(((0,), (0,)), ((), ())), grid=(k_dim // tm, n_dim // tn, s // tk),
                    a_spec=a_spec, b_spec=b_spec, acc_shape=(tm, tn), epilogue=lambda acc: (acc.reshape(2, r, tn),),
                    out_shapes=[SDS((2, N_CHIPS, r, c), F32)], out_specs=[o_spec], name=name)[0]


def _rms(x, g):
    return x * lax.rsqrt(jnp.mean(x * x, axis=-1, keepdims=True) + EPS) * g


def _rms_fwd(h, g, *, name):
    rows, d = h.shape
    tr = _pick(rows, (512, 256))

    def body(h_ref, g_ref, o_ref):
        o_ref[...] = _rms(h_ref[...], g_ref[...]).astype(o_ref.dtype)

    return pl.pallas_call(
        body, grid=(rows // tr,),
        in_specs=[pl.BlockSpec((tr, d), lambda i: (i, 0)), pl.BlockSpec((1, d), lambda i: (0, 0))],
        out_specs=pl.BlockSpec((tr, d), lambda i: (i, 0)), out_shape=SDS((rows, d), BF16),
        compiler_params=_cparams(("parallel",)), name=name)(h, g)


def _rms_bwd(h, g, da, dres, *, name):
    rows, d = h.shape
    tr = _pick(rows, (512, 256))

    def body(h_ref, g_ref, da_ref, dres_ref, dh_ref, dg_ref):
        _, vjp = jax.vjp(_rms, h_ref[...], g_ref[...])
        dh, dg = vjp(da_ref[...].astype(F32))
        dh_ref[...] = dres_ref[...] + dh

        @pl.when(pl.program_id(0) == 0)
        def _():
            dg_ref[...] = jnp.zeros_like(dg_ref)

        dg_ref[...] += dg

    row_spec = pl.BlockSpec((tr, d), lambda i: (i, 0))
    vec_spec = pl.BlockSpec((1, d), lambda i: (0, 0))
    return pl.pallas_call(
        body, grid=(rows // tr,), in_specs=[row_spec, vec_spec, row_spec, row_spec],
        out_specs=[row_spec, vec_spec], out_shape=[SDS((rows, d), F32), SDS((1, d), F32)],
        compiler_params=_cparams(("arbitrary",)), name=name)(h, g, da, dres)


def _loss_head(h, g, target, *, name):
    rows, d = h.shape
    tr = _pick(rows, (512, 256))

    def body(h_ref, g_ref, t_ref, loss_ref, dh_ref, dg_ref):
        y, vjp = jax.vjp(_rms, h_ref[...], g_ref[...])
        err = y - t_ref[...]
        dh, dg = vjp(err * (1.0 / d))
        dh_ref[...] = dh

        @pl.when(pl.program_id(0) == 0)
        def _():
            dg_ref[...] = jnp.zeros_like(dg_ref)
            loss_ref[...] = jnp.zeros_like(loss_ref)

        dg_ref[...] += dg
        part = jnp.sum(jnp.sum(err * err, axis=-1, keepdims=True), axis=0, keepdims=True) * (0.5 / d)
        loss_ref[...] += jnp.broadcast_to(part, loss_ref.shape)

    row_spec = pl.BlockSpec((tr, d), lambda i: (i, 0))
    vec_spec = pl.BlockSpec((1, d), lambda i: (0, 0))
    loss_spec = pl.BlockSpec((8, 128), lambda i: (0, 0))
    return pl.pallas_call(
        body, grid=(rows // tr,), in_specs=[row_spec, vec_spec, row_spec],
        out_specs=[loss_spec, row_spec, vec_spec],
        out_shape=[SDS((8, 128), F32), SDS((rows, d), F32), SDS((1, d), F32)],
        compiler_params=_cparams(("arbitrary",)), name=name)(h, g, target)


def _gelu(x):
    return 0.5 * x * (1.0 + lax.erf(x * (1.0 / math.sqrt(2.0))))


def _gate_tile(pu, pv, ln_g, ln_b, ws, bs_t):
    u = [_gelu(p) for p in pu]
    v = [_gelu(p) for p in pv]
    mu = sum(jnp.sum(t, axis=-1, keepdims=True) for t in v) * (1.0 / D_INNER)
    vc = [t - mu for t in v]
    var = sum(jnp.sum(t * t, axis=-1, keepdims=True) for t in vc) * (1.0 / D_INNER)
    rstd = lax.rsqrt(var + EPS)
    row = lax.broadcasted_iota(jnp.int32, (CHUNK, CHUNK), 0)
    col = lax.broadcasted_iota(jnp.int32, (CHUNK, CHUNK), 1)
    out = []
    for gi in range(A_GROUPS):
        vn = vc[gi] * rstd * ln_g[gi] + ln_b[gi]
        w = jnp.where(row >= col, ws[gi], 0.0)
        sv = _dot(w, vn) + bs_t[gi]
        out.append(u[gi] * sv)
    return out


def _split(ref, n, width):
    return [ref[:, i * width:(i + 1) * width] for i in range(n)]


def _gate_in_specs():
    return [
        pl.BlockSpec((CHUNK, D_INNER), lambda c: (c, 0)),
        pl.BlockSpec((CHUNK, D_INNER), lambda c: (c, 1)),
        pl.BlockSpec((1, D_INNER), lambda c: (0, 0)),
        pl.BlockSpec((1, D_INNER), lambda c: (0, 0)),
        pl.BlockSpec((A_GROUPS, CHUNK, CHUNK), lambda c: (0, 0, 0)),
        pl.BlockSpec((A_GROUPS, CHUNK, 1), lambda c: (0, 0, 0)),
    ]


def _gate_args(u_ref, v_ref, g_ref, b_ref, ws_ref, bs_ref):
    ng, gw = A_GROUPS, A_GROUP_W
    return (_split(u_ref, ng, gw), _split(v_ref, ng, gw), _split(g_ref, ng, gw), _split(b_ref, ng, gw),
            [ws_ref[i] for i in range(ng)], [bs_ref[i] for i in range(ng)])


def _gate_fwd(proj, ln_g, ln_b, ws, bs_col, mixcat, *, name):
    def body(u_ref, v_ref, g_ref, b_ref, ws_ref, bs_ref, cat_in, cat_ref):
        del cat_in
        out = _gate_tile(*_gate_args(u_ref, v_ref, g_ref, b_ref, ws_ref, bs_ref))
        for gi, o in enumerate(out):
            cat_ref[:, gi * A_GROUP_W:(gi + 1) * A_GROUP_W] = o.astype(cat_ref.dtype)

    return pl.pallas_call(
        body, grid=(N_CHUNKS,), in_specs=[*_gate_in_specs(), pl.BlockSpec(memory_space=pl.ANY)],
        out_specs=pl.BlockSpec((CHUNK, D_INNER), lambda c: (c, 0)), out_shape=SDS(mixcat.shape, mixcat.dtype),
        input_output_aliases={6: 0}, compiler_params=_cparams(("parallel",)), name=name,
    )(proj, proj, ln_g, ln_b, ws, bs_col, mixcat)


def _gate_bwd(proj, ln_g, ln_b, ws, bs_col, dcat, dproj, *, name):
    ng, gw = A_GROUPS, A_GROUP_W

    def body(u_ref, v_ref, g_ref, b_ref, ws_ref, bs_ref, d_ref, dproj_in, dproj_ref, dg_ref, db_ref, dws_ref, dbs_ref):
        del dproj_in
        args = _gate_args(u_ref, v_ref, g_ref, b_ref, ws_ref, bs_ref)
        _, vjp = jax.vjp(_gate_tile, *args)
        dpu, dpv, dg, db, dws, dbs = vjp(_split(d_ref, ng, gw))
        for gi in range(ng):
            dproj_ref[:, gi * gw:(gi + 1) * gw] = dpu[gi].astype(dproj_ref.dtype)
            dproj_ref[:, D_INNER + gi * gw:D_INNER + (gi + 1) * gw] = dpv[gi].astype(dproj_ref.dtype)

        @pl.when(pl.program_id(0) == 0)
        def _():
            for r in (dg_ref, db_ref, dws_ref, dbs_ref):
                r[...] = jnp.zeros_like(r)

        for gi in range(ng):
            dg_ref[:, gi * gw:(gi + 1) * gw] += dg[gi]
            db_ref[:, gi * gw:(gi + 1) * gw] += db[gi]
            dws_ref[gi] += dws[gi]
            dbs_ref[gi] += dbs[gi]

    in_specs = _gate_in_specs()
    return pl.pallas_call(
        body, grid=(N_CHUNKS,),
        in_specs=[*in_specs, pl.BlockSpec((CHUNK, D_INNER), lambda c: (c, 0)), pl.BlockSpec(memory_space=pl.ANY)],
        out_specs=[pl.BlockSpec((CHUNK, 2 * D_INNER), lambda c: (c, 0)), *in_specs[2:]],
        out_shape=[SDS(dproj.shape, dproj.dtype), SDS((1, D_INNER), F32), SDS((1, D_INNER), F32),
                   SDS((ng, CHUNK, CHUNK), F32), SDS((ng, CHUNK, 1), F32)],
        input_output_aliases={7: 0}, compiler_params=_cparams(("arbitrary",)), name=name,
    )(proj, proj, ln_g, ln_b, ws, bs_col, dcat, dproj)


ATT_TQ = 512


def _attn_tile(q, k, v):
    s = _dot_nt(q, k) * (1.0 / math.sqrt(X_HEAD_DIM))
    s = s - jnp.max(s, axis=-1, keepdims=True)
    e = jnp.exp(s)
    p = e / jnp.sum(e, axis=-1, keepdims=True)
    return _dot(p, v)


def _attn_in_specs(q_blk, order):
    hd = X_HEAD_DIM
    return [
        pl.BlockSpec((ATT_TQ, hd), lambda a, b: (order(a, b)[0], q_blk + order(a, b)[1])),
        pl.BlockSpec((N_MEM, hd), lambda a, b: (0, order(a, b)[1])),
        pl.BlockSpec((N_MEM, hd), lambda a, b: (0, X_HEADS + order(a, b)[1])),
    ]


def _attn_fwd(proj, q_off, kv, *, name):
    order = lambda i, h: (i, h)
    cat_blk = D_INNER // X_HEAD_DIM

    def body(q_ref, k_ref, v_ref, o_ref):
        o_ref[...] = _attn_tile(q_ref[...], k_ref[...], v_ref[...]).astype(o_ref.dtype)

    return pl.pallas_call(
        body, grid=(SEQ // ATT_TQ, X_HEADS), in_specs=_attn_in_specs(q_off // X_HEAD_DIM, order),
        out_specs=pl.BlockSpec((ATT_TQ, X_HEAD_DIM), lambda i, h: (i, cat_blk + h)),
        out_shape=SDS((SEQ, MIX_OUT), BF16), compiler_params=_cparams(("parallel", "parallel")), name=name,
    )(proj, kv, kv)


def _attn_bwd(proj, q_off, kv, dcat, dproj_width, dq_off, *, name):
    order = lambda h, i: (i, h)
    cat_blk = D_INNER // X_HEAD_DIM
    dq_blk = dq_off // X_HEAD_DIM

    def body(q_ref, k_ref, v_ref, do_ref, dq_ref, dk_ref, dv_ref):
        _, vjp = jax.vjp(_attn_tile, q_ref[...], k_ref[...], v_ref[...])
        dq, dk, dv = vjp(do_ref[...])
        dq_ref[...] = dq.astype(dq_ref.dtype)

        @pl.when(pl.program_id(1) == 0)
        def _():
            dk_ref[...] = jnp.zeros_like(dk_ref)
            dv_ref[...] = jnp.zeros_like(dv_ref)

        dk_ref[...] += dk
        dv_ref[...] += dv

    kv_spec = pl.BlockSpec((N_MEM, X_HEAD_DIM), lambda h, i: (0, h))
    return pl.pallas_call(
        body, grid=(X_HEADS, SEQ // ATT_TQ),
        in_specs=[*_attn_in_specs(q_off // X_HEAD_DIM, order),
                  pl.BlockSpec((ATT_TQ, X_HEAD_DIM), lambda h, i: (i, cat_blk + h))],
        out_specs=[pl.BlockSpec((ATT_TQ, X_HEAD_DIM), lambda h, i: (i, dq_blk + h)), kv_spec, kv_spec],
        out_shape=[SDS((SEQ, dproj_width), BF16), SDS((N_MEM, X_WIDTH), F32), SDS((N_MEM, X_WIDTH), F32)],
        compiler_params=_cparams(("parallel", "arbitrary")), name=name,
    )(proj, kv, kv, dcat)


CONV_TC = 512


def _shift_down(x, s):
    if s == 0:
        return x
    row = lax.broadcasted_iota(jnp.int32, x.shape, 0)
    return jnp.where(row >= s, pltpu.roll(x, s, 0), 0.0)


def _shift_up(x, s):
    if s == 0:
        return x
    n = x.shape[0]
    row = lax.broadcasted_iota(jnp.int32, x.shape, 0)
    return jnp.where(row < n - s, pltpu.roll(x, n - s, 0), 0.0)


def _conv_pre(x, w_ref, b_ref):
    pre = b_ref[...] + jnp.zeros_like(x)
    for k in range(CONV_K):
        pre = pre + w_ref[k:k + 1, :] * _shift_down(x, CONV_K - 1 - k)
    return pre


def _conv_fwd(proj, w, b, *, name):
    blk0 = D_INNER // CONV_TC

    def body(x_ref, w_ref, b_ref, o_ref):
        pre = _conv_pre(x_ref[...], w_ref, b_ref)
        o_ref[...] = pre * jax.nn.sigmoid(pre)

    return pl.pallas_call(
        body, grid=(CONV_DIM // CONV_TC,),
        in_specs=[pl.BlockSpec((SEQ, CONV_TC), lambda j: (0, blk0 + j)), pl.BlockSpec((CONV_K, CONV_TC), lambda j: (0, j)),
                  pl.BlockSpec((1, CONV_TC), lambda j: (0, j))],
        out_specs=pl.BlockSpec((SEQ, CONV_TC), lambda j: (0, j)), out_shape=SDS((SEQ, CONV_DIM), F32),
        compiler_params=_cparams(("parallel",)), name=name)(proj, w, b)


def _conv_bwd(proj, w, b, dxs, dbm, dcm, dproj, *, name):
    tc = CONV_TC // 2
    blk0 = D_INNER // tc
    n_x = D_INNER // tc
    n_b = SSM_GROUPS * SSM_STATE // tc

    def body(x_ref, w_ref, b_ref, dxs_ref, dbm_ref, dcm_ref, dproj_in, dproj_ref, dw_ref, db_ref):
        del dproj_in
        j = pl.program_id(0)
        x = x_ref[...]
        pre = _conv_pre(x, w_ref, b_ref)
        sg = jax.nn.sigmoid(pre)
        dact = jnp.where(j < n_x, dxs_ref[...], jnp.where(j < n_x + n_b, dbm_ref[...], dcm_ref[...]))
        dpre = dact * (sg * (1.0 + pre * (1.0 - sg)))
        dx = jnp.zeros_like(x)
        for k in range(CONV_K):
            s = CONV_K - 1 - k
            dx = dx + w_ref[k:k + 1, :] * _shift_up(dpre, s)
            dw_ref[k:k + 1, :] = jnp.sum(dpre * _shift_down(x, s), axis=0, keepdims=True)
        dproj_ref[...] = dx.astype(dproj_ref.dtype)
        db_ref[...] = jnp.sum(dpre, axis=0, keepdims=True)

    clip = lambda v, hi: jnp.minimum(jnp.maximum(v, 0), hi)
    return pl.pallas_call(
        body, grid=(CONV_DIM // tc,),
        in_specs=[pl.BlockSpec((SEQ, tc), lambda j: (0, blk0 + j)), pl.BlockSpec((CONV_K, tc), lambda j: (0, j)),
                  pl.BlockSpec((1, tc), lambda j: (0, j)),
                  pl.BlockSpec((SEQ, tc), lambda j: (0, clip(j, n_x - 1))),
                  pl.BlockSpec((SEQ, tc), lambda j: (0, clip(j - n_x, n_b - 1))),
                  pl.BlockSpec((SEQ, tc), lambda j: (0, clip(j - n_x - n_b, n_b - 1))),
                  pl.BlockSpec(memory_space=pl.ANY)],
        out_specs=[pl.BlockSpec((SEQ, tc), lambda j: (0, blk0 + j)), pl.BlockSpec((CONV_K, tc), lambda j: (0, j)),
                   pl.BlockSpec((1, tc), lambda j: (0, j))],
        out_shape=[SDS(dproj.shape, dproj.dtype), SDS((CONV_K, CONV_DIM), F32), SDS((1, CONV_DIM), F32)],
        input_output_aliases={6: 0}, compiler_params=_cparams(("parallel",)), name=name,
    )(proj, w, b, dxs, dbm, dcm, dproj)


SSM_PAIRS = SSM_HPG // 2


def _ssd_tile(xp, zp, bm, cm, hp, dtc, dtr, bias, alog, dsk, gnp):
    row = lax.broadcasted_iota(jnp.int32, (CHUNK, CHUNK), 0)
    col = lax.broadcasted_iota(jnp.int32, (CHUNK, CHUNK), 1)
    causal = row >= col
    tri = jnp.where(causal, 1.0, 0.0)
    left = col < SSM_HEAD_DIM
    top = row < SSM_HEAD_DIM
    ones = jnp.ones((CHUNK, CHUNK), BF16)
    cb = _dot_nt(cm, bm)
    dt_c, cs_c, cs_last, m = [], [], [], []
    for r in range(SSM_HPG):
        a = -jnp.exp(alog[r])
        dt_c.append(jax.nn.softplus(dtc[r] + bias[r]))
        da_c = dt_c[r] * a
        da_r = jax.nn.softplus(dtr[r] + bias[r]) * a
        cs_c.append(jnp.sum(tri * da_r, axis=1, keepdims=True))
        cs_r = jnp.sum(jnp.where(row <= col, 1.0, 0.0) * da_c, axis=0, keepdims=True)
        cs_last.append(jnp.sum(da_c, axis=0, keepdims=True))
        m.append(cb * jnp.exp(jnp.where(causal, cs_c[r] - cs_r, -1e30)))
    ygs, hn = [], []
    for p in range(SSM_PAIRS):
        a, b = 2 * p, 2 * p + 1
        pair = lambda u, v: jnp.where(left, u, v)
        xdt = xp[p] * pair(dt_c[a], dt_c[b])
        y = pair(_dot(m[a], xdt), _dot(m[b], xdt))
        y = y + _dot_nt(cm, hp[p]) * pair(jnp.exp(cs_c[a]), jnp.exp(cs_c[b]))
        y = y + xp[p] * pair(dsk[a], dsk[b])
        decay = pair(jnp.exp(cs_last[a] - cs_c[a]), jnp.exp(cs_last[b] - cs_c[b]))
        states = _dot_tn(xdt * decay, bm)
        hn.append(hp[p] * jnp.where(top, jnp.exp(cs_last[a]), jnp.exp(cs_last[b])) + states)
        ygs.append(y * (zp[p] * jax.nn.sigmoid(zp[p])))
    ms = sum(_dot(t * t, ones) for t in ygs) * (1.0 / SSM_GROUP_W)
    rs = lax.rsqrt(ms + EPS)
    return [ygs[p] * rs * gnp[p] for p in range(SSM_PAIRS)], hn


def _ssd_in_specs(cidx):
    gw, n = SSM_GROUP_W, SSM_STATE
    bm_blk = D_INNER // n
    return [
        pl.BlockSpec((CHUNK, gw), lambda g, c: (cidx(c), g)),
        pl.BlockSpec((CHUNK, gw), lambda g, c: (cidx(c), g)),
        pl.BlockSpec((CHUNK, n), lambda g, c: (cidx(c), bm_blk + g)),
        pl.BlockSpec((CHUNK, n), lambda g, c: (cidx(c), bm_blk + SSM_GROUPS + g)),
        pl.BlockSpec((None, CHUNK, SSM_HPG), lambda g, c: (g, cidx(c), 0)),
        pl.BlockSpec((None, SSM_HPG, CHUNK), lambda g, c: (g, 0, cidx(c))),
        pl.BlockSpec((None, 1, SSM_HPG), lambda g, c: (g, 0, 0)),
        pl.BlockSpec((None, 1, SSM_HPG), lambda g, c: (g, 0, 0)),
        pl.BlockSpec((None, 1, SSM_HPG), lambda g, c: (g, 0, 0)),
        pl.BlockSpec((1, gw), lambda g, c: (0, g)),
    ]


def _ssd_args(x_ref, z_ref, bm_ref, cm_ref, hp, dtc_ref, dtr_ref, bias_ref, alog_ref, dsk_ref, gn_ref):
    nh, npair, w = SSM_HPG, SSM_PAIRS, 2 * SSM_HEAD_DIM
    col = lambda ref: [ref[:, r:r + 1] for r in range(nh)]
    return (_split(x_ref, npair, w), _split(z_ref, npair, w), bm_ref[...], cm_ref[...], hp,
            col(dtc_ref), [dtr_ref[r:r + 1, :] for r in range(nh)], col(bias_ref), col(alog_ref), col(dsk_ref),
            _split(gn_ref, npair, w))


def _pair_rows(ref):
    w = 2 * SSM_HEAD_DIM
    return [ref[p * w:(p + 1) * w, :] for p in range(SSM_PAIRS)]


def _ssd_fwd(xbc, proj, dt_c, dt_r, bias, alog, dsk, gn, mixcat, *, name):
    w = 2 * SSM_HEAD_DIM

    def body(x_ref, z_ref, bm_ref, cm_ref, dtc_ref, dtr_ref, bias_ref, alog_ref, dsk_ref, gn_ref, cat_in,
             cat_ref, hprev_ref, h_scr):
        del cat_in

        @pl.when(pl.program_id(1) == 0)
        def _():
            h_scr[...] = jnp.zeros_like(h_scr)

        hprev_ref[...] = h_scr[...]
        yn, hn = _ssd_tile(*_ssd_args(x_ref, z_ref, bm_ref, cm_ref, _pair_rows(h_scr), dtc_ref, dtr_ref, bias_ref,
                                      alog_ref, dsk_ref, gn_ref))
        for p in range(SSM_PAIRS):
            cat_ref[:, p * w:(p + 1) * w] = yn[p].astype(cat_ref.dtype)
            h_scr[p * w:(p + 1) * w, :] = hn[p]

    return pl.pallas_call(
        body, grid=(SSM_GROUPS, N_CHUNKS), in_specs=[*_ssd_in_specs(lambda c: c), pl.BlockSpec(memory_space=pl.ANY)],
        out_specs=[pl.BlockSpec((CHUNK, SSM_GROUP_W), lambda g, c: (c, g)),
                   pl.BlockSpec((None, None, SSM_GROUP_W, SSM_STATE), lambda g, c: (c, g, 0, 0))],
        out_shape=[SDS(mixcat.shape, mixcat.dtype), SDS((N_CHUNKS, SSM_GROUPS, SSM_GROUP_W, SSM_STATE), F32)],
        scratch_shapes=[pltpu.VMEM((SSM_GROUP_W, SSM_STATE), F32)],
        input_output_aliases={10: 0}, compiler_params=_cparams(("parallel", "arbitrary")), name=name,
    )(xbc, proj, xbc, xbc, dt_c, dt_r, bias, alog, dsk, gn, mixcat)


def _ssd_bwd(xbc, proj, dt_c, dt_r, bias, alog, dsk, gn, hprev, dcat, dproj, *, name):
    nh, w, gw, n = SSM_HPG, 2 * SSM_HEAD_DIM, SSM_GROUP_W, SSM_STATE
    rev = lambda c: N_CHUNKS - 1 - c

    def body(x_ref, z_ref, bm_ref, cm_ref, dtc_ref, dtr_ref, bias_ref, alog_ref, dsk_ref, gn_ref, hprev_ref, dy_ref,
             dproj_in, dz_ref, dxs_ref, dbm_ref, dcm_ref, ddtc_ref, ddtr_ref, dbias_ref, dalog_ref, ddsk_ref, dgn_ref,
             dh_scr):
        del dproj_in
        first = pl.program_id(1) == 0

        @pl.when(first)
        def _():
            dh_scr[...] = jnp.zeros_like(dh_scr)
            for ref in (dbias_ref, dalog_ref, ddsk_ref, dgn_ref):
                ref[...] = jnp.zeros_like(ref)

        args = _ssd_args(x_ref, z_ref, bm_ref, cm_ref, _pair_rows(hprev_ref), dtc_ref, dtr_ref, bias_ref, alog_ref,
                         dsk_ref, gn_ref)
        _, vjp = jax.vjp(_ssd_tile, *args)
        dxs, dzs, dbm, dcm, dhs, ddtc, ddtr, dbias, dalog, ddsk, dgn = vjp(
            (_split(dy_ref, SSM_PAIRS, w), _pair_rows(dh_scr)))
        dbm_ref[...] = dbm
        dcm_ref[...] = dcm
        for q in range(SSM_PAIRS):
            dxs_ref[:, q * w:(q + 1) * w] = dxs[q]
            dz_ref[:, q * w:(q + 1) * w] = dzs[q].astype(dz_ref.dtype)
            dh_scr[q * w:(q + 1) * w, :] = dhs[q]
            dgn_ref[:, q * w:(q + 1) * w] += dgn[q]
        for r in range(nh):
            ddtc_ref[:, r:r + 1] = ddtc[r]
            ddtr_ref[r:r + 1, :] = ddtr[r]
            dbias_ref[:, r:r + 1] += dbias[r]
            dalog_ref[:, r:r + 1] += dalog[r]
            ddsk_ref[:, r:r + 1] += ddsk[r]

    par_spec = pl.BlockSpec((None, 1, nh), lambda g, c: (g, 0, 0))
    return pl.pallas_call(
        body, grid=(SSM_GROUPS, N_CHUNKS),
        in_specs=[*_ssd_in_specs(rev),
                  pl.BlockSpec((None, None, gw, n), lambda g, c: (rev(c), g, 0, 0)),
                  pl.BlockSpec((CHUNK, gw), lambda g, c: (rev(c), g)),
                  pl.BlockSpec(memory_space=pl.ANY)],
        out_specs=[pl.BlockSpec((CHUNK, gw), lambda g, c: (rev(c), g)),
                   pl.BlockSpec((CHUNK, gw), lambda g, c: (rev(c), g)),
                   pl.BlockSpec((CHUNK, n), lambda g, c: (rev(c), g)),
                   pl.BlockSpec((CHUNK, n), lambda g, c: (rev(c), g)),
                   pl.BlockSpec((None, CHUNK, nh), lambda g, c: (g, rev(c), 0)),
                   pl.BlockSpec((None, nh, CHUNK), lambda g, c: (g, 0, rev(c))),
                   par_spec, par_spec, par_spec,
                   pl.BlockSpec((1, gw), lambda g, c: (0, g))],
        out_shape=[SDS(dproj.shape, dproj.dtype), SDS((SEQ, D_INNER), F32), SDS((SEQ, SSM_GROUPS * n), F32),
                   SDS((SEQ, SSM_GROUPS * n), F32), SDS((SSM_GROUPS, SEQ, nh), F32), SDS((SSM_GROUPS, nh, SEQ), F32),
                   SDS((SSM_GROUPS, 1, nh), F32), SDS((SSM_GROUPS, 1, nh), F32), SDS((SSM_GROUPS, 1, nh), F32),
                   SDS((1, D_INNER), F32)],
        scratch_shapes=[pltpu.VMEM((gw, n), F32)],
        input_output_aliases={12: 0}, compiler_params=_cparams(("parallel", "arbitrary")), name=name,
    )(xbc, proj, xbc, xbc, dt_c, dt_r, bias, alog, dsk, gn, hprev, dcat, dproj)


def _sum_contributions(chip, parts, landed, *, name):
    _, r, c = parts.shape
    tr = _pick(r, (256, 384, 128))

    def body(chip_ref, own_ref, landed_ref, o_ref):
        del chip_ref
        acc = own_ref[...].astype(F32)
        for s in range(landed_ref.shape[0]):
            acc = acc + landed_ref[s].astype(F32)
        o_ref[...] = acc

    grid_spec = pltpu.PrefetchScalarGridSpec(
        num_scalar_prefetch=1, grid=(r // tr,),
        in_specs=[pl.BlockSpec((None, tr, c), lambda i, chip_ref: (chip_ref[0], i, 0)),
                  pl.BlockSpec((landed.shape[0], tr, c), lambda i, chip_ref: (0, i, 0))],
        out_specs=pl.BlockSpec((tr, c), lambda i, chip_ref: (i, 0)))
    return pl.pallas_call(body, grid_spec=grid_spec, out_shape=SDS((r, c), F32),
                          compiler_params=_cparams(("parallel",)), name=name)(chip, parts, landed)


def _adamw(w, g, m, v, *, name):
    r, c = w.shape
    tr = r if r <= 256 else _pick(r, (256, 128, 8))
    spec = pl.BlockSpec((tr, c), lambda i: (i, 0))

    def body(w_ref, g_ref, m_ref, v_ref, d_ref, mo_ref, vo_ref):
        g = g_ref[...]
        m_new = ADAM_B1 * m_ref[...] + (1.0 - ADAM_B1) * g
        v_new = ADAM_B2 * v_ref[...] + (1.0 - ADAM_B2) * (g * g)
        m_hat = m_new / (1.0 - ADAM_B1 ** ADAM_STEP)
        v_hat = v_new / (1.0 - ADAM_B2 ** ADAM_STEP)
        d_ref[...] = -ADAM_LR * (m_hat / (jnp.sqrt(v_hat) + ADAM_EPS) + ADAM_WD * w_ref[...])
        mo_ref[...] = m_new
        vo_ref[...] = v_new

    return pl.pallas_call(body, grid=(r // tr,), in_specs=[spec] * 4, out_specs=[spec] * 3,
                          out_shape=[SDS((r, c), F32)] * 3, compiler_params=_cparams(("parallel",)), name=name)(w, g, m, v)


ANY = pl.BlockSpec(memory_space=pl.ANY)


def _place():
    x, y, c = lax.axis_index("x"), lax.axis_index("y"), lax.axis_index("c")
    chips = [(1 - x, y), (x, 1 - y), (1 - x, 1 - y)]
    return x, y, c, chips


def _remote(src, dst, send_sem, recv_sem, to):
    return pltpu.make_async_remote_copy(src_ref=src, dst_ref=dst, send_sem=send_sem, recv_sem=recv_sem,
                                        device_id=to, device_id_type=MESH)


STREAM_ROWS = 128


def _stream_rows(i):
    return pl.ds(pl.multiple_of(i * STREAM_ROWS, STREAM_ROWS), STREAM_ROWS)


def _channel_scratch(width, dtype):
    buf = (2, STREAM_ROWS, width)
    return [pltpu.VMEM(buf, dtype), pltpu.VMEM(buf, dtype), *([pltpu.SemaphoreType.DMA((2,))] * 5),
            pltpu.SemaphoreType.REGULAR((2,))]


CHANNEL_REFS = 8


def _copy_through_vmem(src, dst, ch):
    sbuf, _, ld, _, _, st, _, _ = ch
    steps = src.shape[0] // STREAM_ROWS
    assert steps >= 2 and steps * STREAM_ROWS == src.shape[0]

    def load(i, slot):
        return pltpu.make_async_copy(src.at[_stream_rows(i)], sbuf.at[slot], ld.at[slot])

    def store(i, slot):
        return pltpu.make_async_copy(sbuf.at[slot], dst.at[_stream_rows(i)], st.at[slot])

    load(0, 0).start()

    def step(i, carry):
        slot = lax.rem(i, 2)
        nxt = 1 - slot

        @pl.when(i + 1 < steps)
        def _():
            @pl.when(i >= 1)
            def _():
                store(0, nxt).wait()
            load(i + 1, nxt).start()

        load(i, slot).wait()
        store(i, slot).start()
        return carry

    lax.fori_loop(0, steps, step, 0)
    for slot in range(2):
        store(0, slot).wait()


def _exchange_stream(src, dst, keep, ch, sibling):
    sbuf, rbuf, ld, snd, rcv, st, kp, credit = ch
    steps = src.shape[0] // STREAM_ROWS
    assert steps >= 2 and steps * STREAM_ROWS == src.shape[0]

    def load(i, slot):
        return pltpu.make_async_copy(src.at[_stream_rows(i)], sbuf.at[slot], ld.at[slot])

    def push(slot):
        return _remote(sbuf.at[slot], rbuf.at[slot], snd.at[slot], rcv.at[slot], sibling)

    def store(i, slot):
        return pltpu.make_async_copy(rbuf.at[slot], dst.at[_stream_rows(i)], st.at[slot])

    def save(i, slot):
        return pltpu.make_async_copy(sbuf.at[slot], keep.at[_stream_rows(i)], kp.at[slot])

    for slot in range(2):
        pl.semaphore_signal(credit.at[slot], 1, device_id=sibling, device_id_type=MESH)
    load(0, 0).start()

    def step(i, carry):
        slot = lax.rem(i, 2)
        nxt = 1 - slot

        @pl.when(i + 1 < steps)
        def _():
            @pl.when(i >= 1)
            def _():
                push(nxt).wait_send()
                if keep is not None:
                    save(0, nxt).wait()
            load(i + 1, nxt).start()

        load(i, slot).wait()
        pl.semaphore_wait(credit.at[slot], 1)
        push(slot).start()
        if keep is not None:
            save(i, slot).start()
        push(slot).wait_recv()
        store(i, slot).start()

        @pl.when(i >= 1)
        def _():
            store(0, nxt).wait()

            @pl.when(i + 1 < steps)
            def _():
                pl.semaphore_signal(credit.at[nxt], 1, device_id=sibling, device_id_type=MESH)
        return carry

    lax.fori_loop(0, steps, step, 0)
    store(0, (steps - 1) % 2).wait()
    for slot in range(2):
        push(slot).wait_send()
        if keep is not None:
            save(0, slot).wait()


def _all_gather_shards(shards, small, *, name):
    n = len(shards)

    def body(*refs):
        ins, outs = refs[:n + 1], refs[n + 1:2 * n + 2]
        scr = refs[2 * n + 2:]
        chans = [scr[CHANNEL_REFS * t:CHANNEL_REFS * (t + 1)] for t in range(n)]
        send_sems, recv_sems, small_sems = scr[CHANNEL_REFS * n:]
        x, y, c, _ = _place()
        me = 2 * x + y
        sibling = (x, y, 1 - c)
        near = (lax.rem(x + 1 - c, 2), lax.rem(y + c, 2))
        far = (lax.rem(x + c, 2), lax.rem(y + 1 - c, 2))
        k_near, k_far, k_diag = 2 * near[0] + near[1], 2 * far[0] + far[1], 3 - me
        targets = ((*near, c), (*far, c), (*far, c))
        arrives = (k_near, k_far, k_diag)
        streams_in = (k_far, k_near, k_diag)

        def ici(t, j, src, blk):
            return _remote(src, outs[t].at[blk, c], send_sems.at[3 * t + j], recv_sems.at[3 * t + j], targets[j])

        first = [ici(t, j, ins[t].at[c], me) for t in range(n + 1) for j in range(2)]
        for cp in first:
            cp.start()
        small_local = pltpu.make_async_copy(ins[n], outs[n].at[me], small_sems.at[6])
        small_local.start()
        for t in range(n):
            for h in range(2):
                _copy_through_vmem(ins[t].at[h], outs[t].at[me, h], chans[t])
        passed = []
        for j in range(3):
            for t in range(n + 1):
                landed = outs[t].at[arrives[j], c]
                ici(t, j, landed, arrives[j]).wait_recv()
                if j == 0:
                    fwd = ici(t, 2, landed, k_near)
                    fwd.start()
                    passed.append(fwd)
                if t < n:
                    _exchange_stream(landed, outs[t].at[streams_in[j], 1 - c], None, chans[t], sibling)
                else:
                    fwd = _remote(landed, landed, small_sems.at[j], small_sems.at[3 + j], sibling)
                    fwd.start()
                    passed.append(fwd)
        for j in range(3):
            got = outs[n].at[streams_in[j], 1 - c]
            _remote(got, got, small_sems.at[j], small_sems.at[3 + j], sibling).wait_recv()
        for cp in first + passed:
            cp.wait_send()
        small_local.wait()

    scratch = []
    for s in shards:
        scratch += _channel_scratch(s.shape[2], s.dtype)
    return pl.pallas_call(
        body, in_specs=[ANY] * (n + 1), out_specs=[ANY] * (n + 1),
        out_shape=[SDS((N_CHIPS, *s.shape), s.dtype) for s in (*shards, small)],
        scratch_shapes=[*scratch, pltpu.SemaphoreType.DMA((3 * n + 3,)), pltpu.SemaphoreType.DMA((3 * n + 3,)),
                        pltpu.SemaphoreType.DMA((7,))],
        compiler_params=pltpu.CompilerParams(vmem_limit_bytes=VMEM_LIMIT), name=name)(*shards, small)


def _pair_reduce(stacks, *, name):
    n = len(stacks)
    per = 11

    def body(*refs):
        ins, outs, scr = refs[:n], refs[n:2 * n], refs[2 * n:]
        x, y, c, _ = _place()
        sibling = (x, y, 1 - c)
        for t in range(n):
            sraw, sbuf, rbuf, obuf, pbuf, ld_s, ld_o, snd, rcv, st, credit = scr[per * t:per * (t + 1)]
            steps = ins[t].shape[1] // STREAM_ROWS
            src, own, out = ins[t].at[1 - c], ins[t].at[c], outs[t]

            def load_s(i, slot, src=src, sraw=sraw, ld_s=ld_s):
                return pltpu.make_async_copy(src.at[_stream_rows(i)], sraw.at[slot], ld_s.at[slot])

            def load_o(i, slot, own=own, obuf=obuf, ld_o=ld_o):
                return pltpu.make_async_copy(own.at[_stream_rows(i)], obuf.at[slot], ld_o.at[slot])

            def push(slot, sbuf=sbuf, rbuf=rbuf, snd=snd, rcv=rcv):
                return _remote(sbuf.at[slot], rbuf.at[slot], snd.at[slot], rcv.at[slot], sibling)

            def store(i, slot, pbuf=pbuf, out=out, st=st):
                return pltpu.make_async_copy(pbuf.at[slot], out.at[_stream_rows(i)], st.at[slot])

            assert steps >= 2
            for slot in range(2):
                pl.semaphore_signal(credit.at[slot], 1, device_id=sibling, device_id_type=MESH)
                load_s(slot, slot).start()
                load_o(slot, slot).start()
            load_s(0, 0).wait()
            sbuf[0] = sraw[0].astype(sbuf.dtype)
            pl.semaphore_wait(credit.at[0], 1)
            push(0).start()

            def step(i, carry, load_s=load_s, load_o=load_o, push=push, store=store, sraw=sraw, sbuf=sbuf, rbuf=rbuf,
                     obuf=obuf, pbuf=pbuf, credit=credit, steps=steps):
                slot = lax.rem(i, 2)
                nxt = 1 - slot

                @pl.when(i + 1 < steps)
                def _():
                    load_s(i + 1, nxt).wait()
                    sbuf[nxt] = sraw[nxt].astype(sbuf.dtype)
                    pl.semaphore_wait(credit.at[nxt], 1)
                    push(nxt).start()

                load_o(i, slot).wait()
                push(slot).wait_recv()

                @pl.when(i >= 2)
                def _():
                    store(i, slot).wait()

                pbuf[slot] = (obuf[slot] + rbuf[slot].astype(F32)).astype(pbuf.dtype)
                store(i, slot).start()
                push(slot).wait_send()

                @pl.when(i + 2 < steps)
                def _():
                    load_s(i + 2, slot).start()
                    load_o(i + 2, slot).start()
                    pl.semaphore_signal(credit.at[slot], 1, device_id=sibling, device_id_type=MESH)
                return carry

            lax.fori_loop(0, steps, step, 0)
            for slot in range(2):
                store(0, slot).wait()

    scratch = []
    for s in stacks:
        buf = (2, STREAM_ROWS, s.shape[2])
        scratch += [pltpu.VMEM(buf, F32), pltpu.VMEM(buf, BF16), pltpu.VMEM(buf, BF16), pltpu.VMEM(buf, F32),
                    pltpu.VMEM(buf, BF16), *([pltpu.SemaphoreType.DMA((2,))] * 5), pltpu.SemaphoreType.REGULAR((2,))]
    return pl.pallas_call(
        body, in_specs=[ANY] * n, out_specs=[ANY] * n, out_shape=[SDS(s.shape[1:], BF16) for s in stacks],
        scratch_shapes=scratch, compiler_params=pltpu.CompilerParams(vmem_limit_bytes=VMEM_LIMIT), name=name)(*stacks)


def _chip_scatter(parts, *, name):
    n = len(parts)

    def body(*refs):
        ins, outs = refs[:n], refs[n:2 * n]
        send_sems, recv_sems = refs[2 * n:]
        _, _, c, chips = _place()
        copies = [_remote(ins[t].at[2 * cx + cy], outs[t].at[j], send_sems.at[3 * t + j], recv_sems.at[3 * t + j],
                          (cx, cy, c)) for t in range(n) for j, (cx, cy) in enumerate(chips)]
        for cp in copies:
            cp.start()
        for cp in copies:
            cp.wait_recv()
        for cp in copies:
            cp.wait_send()

    return pl.pallas_call(
        body, in_specs=[ANY] * n, out_specs=[ANY] * n, out_shape=[SDS((3, *p.shape[1:]), p.dtype) for p in parts],
        scratch_shapes=[pltpu.SemaphoreType.DMA((3 * n,)), pltpu.SemaphoreType.DMA((3 * n,))], name=name)(*parts)


HBM_SPEC = pl.BlockSpec(memory_space=pltpu.HBM)
SEM_SPEC = pl.BlockSpec(memory_space=pltpu.SEMAPHORE)
SIDE_EFFECT = pltpu.SideEffectType.DATAFLOW_SIDE_EFFECTING


def _scatter_copies(ins, lands, send_sems, recv_sems):
    _, _, c, chips = _place()
    return [_remote(ins[t].at[2 * cx + cy], lands[t].at[j], send_sems.at[3 * t + j], recv_sems.at[3 * t + j],
                    (cx, cy, c)) for t in range(len(ins)) for j, (cx, cy) in enumerate(chips)]


def _chip_scatter_start(parts, *, name):
    n = len(parts)

    def body(*refs):
        ins, lands = refs[:n], refs[n:2 * n]
        send_sems, recv_sems, token = refs[2 * n], refs[2 * n + 1], refs[-1]
        for cp in _scatter_copies(ins, lands, send_sems, recv_sems):
            cp.start()
        token[...] = jnp.zeros_like(token)

    hbm = lambda a: pltpu.with_memory_space_constraint(a, pltpu.HBM)
    lands = [hbm(lax.empty((3, *p.shape[1:]), p.dtype)) for p in parts]
    thru = [pltpu.HBM(a.shape, a.dtype) for a in (*parts, *lands)]
    outs = pl.pallas_call(
        body, name=name,
        out_shape=(pltpu.SemaphoreType.DMA((3 * n,)), pltpu.SemaphoreType.DMA((3 * n,)), *thru, SDS((8, 128), F32)),
        in_specs=[HBM_SPEC] * (2 * n),
        out_specs=(SEM_SPEC, SEM_SPEC, *([HBM_SPEC] * (2 * n)), pl.BlockSpec(memory_space=pltpu.VMEM)),
        input_output_aliases={i: 2 + i for i in range(2 * n)},
        compiler_params=pltpu.CompilerParams(has_side_effects=SIDE_EFFECT),
    )(*[hbm(p) for p in parts], *lands)
    return outs[0], outs[1], outs[2:2 + n], outs[2 + n:2 + 2 * n], outs[-1]


def _chip_scatter_wait(send_sems, recv_sems, parts, lands, after, *, name):
    n = len(parts)

    def body(*refs):
        ins, lands_in = refs[:n], refs[n:2 * n]
        for cp in _scatter_copies(ins, lands_in, refs[2 * n], refs[2 * n + 1]):
            cp.wait_send()
            cp.wait_recv()

    outs = pl.pallas_call(
        body, name=name, out_shape=[pltpu.HBM(a.shape, a.dtype) for a in (*parts, *lands)],
        in_specs=[*([HBM_SPEC] * (2 * n)), SEM_SPEC, SEM_SPEC, *([ANY] * len(after))],
        out_specs=[HBM_SPEC] * (2 * n), input_output_aliases={i: i for i in range(2 * n)},
        compiler_params=pltpu.CompilerParams(has_side_effects=SIDE_EFFECT),
    )(*parts, *lands, send_sems, recv_sems, *after)
    return outs[:n], outs[n:]


def _gather_copies(shards, zones, send_sems, recv_sems):
    x, y, c, chips = _place()
    return [_remote(shards[t].at[c], zones[t].at[2 * x + y, c], send_sems.at[3 * t + j], recv_sems.at[3 * t + j],
                    (cx, cy, c)) for t in range(len(shards)) for j, (cx, cy) in enumerate(chips)]


def _gather_start(shards, after, *, name):
    n = len(shards)

    def body(*refs):
        ins, zones = refs[:n], refs[n:2 * n]
        send_sems, recv_sems, token = refs[2 * n + len(after)], refs[2 * n + len(after) + 1], refs[-1]
        for cp in _gather_copies(ins, zones, send_sems, recv_sems):
            cp.start()
        token[...] = jnp.zeros_like(token)

    hbm = lambda a: pltpu.with_memory_space_constraint(a, pltpu.HBM)
    zones = [hbm(lax.empty((N_CHIPS, *s.shape), s.dtype)) for s in shards]
    thru = [pltpu.HBM(a.shape, a.dtype) for a in (*shards, *zones)]
    outs = pl.pallas_call(
        body, name=name,
        out_shape=(pltpu.SemaphoreType.DMA((3 * n,)), pltpu.SemaphoreType.DMA((3 * n,)), *thru, SDS((8, 128), F32)),
        in_specs=[*([HBM_SPEC] * (2 * n)), *([ANY] * len(after))],
        out_specs=(SEM_SPEC, SEM_SPEC, *([HBM_SPEC] * (2 * n)), pl.BlockSpec(memory_space=pltpu.VMEM)),
        input_output_aliases={i: 2 + i for i in range(2 * n)},
        compiler_params=pltpu.CompilerParams(has_side_effects=SIDE_EFFECT),
    )(*[hbm(s) for s in shards], *zones, *after)
    return outs[0], outs[1], outs[2:2 + n], outs[2 + n:2 + 2 * n], outs[-1]


def _gather_wait(send_sems, recv_sems, shards, zones, after, *, name):
    n = len(shards)

    def body(*refs):
        for cp in _gather_copies(refs[:n], refs[n:2 * n], refs[2 * n], refs[2 * n + 1]):
            cp.wait_send()
            cp.wait_recv()

    outs = pl.pallas_call(
        body, name=name, out_shape=[pltpu.HBM(a.shape, a.dtype) for a in (*shards, *zones)],
        in_specs=[*([HBM_SPEC] * (2 * n)), SEM_SPEC, SEM_SPEC, *([ANY] * len(after))],
        out_specs=[HBM_SPEC] * (2 * n), input_output_aliases={i: i for i in range(2 * n)},
        compiler_params=pltpu.CompilerParams(has_side_effects=SIDE_EFFECT),
    )(*shards, *zones, send_sems, recv_sems, *after)
    return outs[:n], outs[n:]


def _gather_finish(shards, zones, *, name):
    n = len(shards)

    def body(*refs):
        ins, zones_in, outs, scr = refs[:n], refs[n:2 * n], refs[2 * n:3 * n], refs[3 * n:]
        x, y, c, chips = _place()
        me = 2 * x + y
        sibling = (x, y, 1 - c)
        for t in range(n):
            chan = scr[CHANNEL_REFS * t:CHANNEL_REFS * (t + 1)]
            for h in range(2):
                _copy_through_vmem(ins[t].at[h], outs[t].at[me, h], chan)
            for cx, cy in chips:
                k = 2 * cx + cy
                _exchange_stream(zones_in[t].at[k, c], outs[t].at[k, 1 - c], None, chan, sibling)

    scratch = []
    for s in shards:
        scratch += _channel_scratch(s.shape[2], s.dtype)
    return pl.pallas_call(
        body, in_specs=[ANY] * (2 * n), out_specs=[ANY] * n, out_shape=[SDS(z.shape, z.dtype) for z in zones],
        input_output_aliases={n + t: t for t in range(n)}, scratch_shapes=scratch,
        compiler_params=pltpu.CompilerParams(vmem_limit_bytes=VMEM_LIMIT), name=name)(*shards, *zones)


def _pair_share(groups, *, name):
    finals = [f for grp in groups for f in grp]
    n, n_out = len(finals), len(groups)

    def body(*refs):
        ins, outs, scr = refs[:n], refs[n:n + n_out], refs[n + n_out:]
        x, y, c, _ = _place()
        sibling = (x, y, 1 - c)
        t = 0
        for o, grp in enumerate(groups):
            for layer in range(len(grp)):
                _exchange_stream(ins[t], outs[o].at[layer, 1 - c], outs[o].at[layer, c],
                                 scr[CHANNEL_REFS * t:CHANNEL_REFS * (t + 1)], sibling)
                t += 1

    scratch = []
    for f in finals:
        scratch += _channel_scratch(f.shape[1], f.dtype)
    return pl.pallas_call(
        body, in_specs=[ANY] * n, out_specs=[ANY] * n_out,
        out_shape=[SDS((len(grp), 2, *grp[0].shape), grp[0].dtype) for grp in groups],
        scratch_shapes=scratch, compiler_params=pltpu.CompilerParams(vmem_limit_bytes=VMEM_LIMIT), name=name)(*finals)


def _all_reduce_small(v, *, name):
    rows, lanes = v.shape
    n_dev = 8

    def body(v_ref, o_ref, all_ref, send_sems, recv_sems, local_sem):
        x, y, c, chips = _place()
        me, sibling = (x, y, c), (x, y, 1 - c)

        def block(px, py, pc):
            return all_ref.at[4 * px + 2 * py + pc]

        def copy(k, blk, to, src=None):
            return _remote(block(*blk) if src is None else src, block(*blk), send_sems.at[k], recv_sems.at[k], to)

        mine = pltpu.make_async_copy(v_ref, block(*me), local_sem)
        mine.start()
        first = [copy(0, me, sibling, src=v_ref)]
        first += [copy(1 + j, me, (*chip, c), src=v_ref) for j, chip in enumerate(chips)]
        for cp in first:
            cp.start()
        passed = [copy(4 + j, (*chip, c), sibling) for j, chip in enumerate(chips)]
        for j, chip in enumerate(chips):
            copy(1 + j, (*chip, c), me).wait_recv()
            passed[j].start()
        copy(0, sibling, me).wait_recv()
        for j, chip in enumerate(chips):
            copy(4 + j, (*chip, 1 - c), me).wait_recv()
        for cp in first + passed:
            cp.wait_send()
        mine.wait()
        acc = all_ref[0]
        for k in range(1, n_dev):
            acc = acc + all_ref[k]
        o_ref[...] = acc

    vmem = pl.BlockSpec(memory_space=pltpu.VMEM)
    return pl.pallas_call(
        body, in_specs=[vmem], out_specs=vmem, out_shape=SDS((rows, lanes), F32),
        scratch_shapes=[pltpu.VMEM((n_dev, rows, lanes), F32), pltpu.SemaphoreType.DMA((7,)),
                        pltpu.SemaphoreType.DMA((7,)), pltpu.SemaphoreType.DMA],
        compiler_params=pltpu.CompilerParams(vmem_limit_bytes=VMEM_LIMIT), name=name)(v)


def _relu2_epilogue(acc):
    return acc, jnp.square(jnp.maximum(acc, 0.0))


def _res_epilogue(acc, res):
    return (acc + res,)


def _drelu2_epilogue(acc, pre):
    return (acc * (2.0 * jnp.maximum(pre.astype(F32), 0.0)),)


def _ffn_fwd(h, g, w1, w2, tag):
    f = _rms_fwd(h, g, name=f"ffn_norm_{tag}")
    pre, act = _mm_nn(f, w1, name=f"ffn1_{tag}", epilogue=_relu2_epilogue, n_out_dtypes=(BF16, BF16))
    h_out = _mm_nn(act, w2, name=f"ffn2_{tag}", extras=(h,), epilogue=_res_epilogue)
    return h_out, (f, pre, act)


def _ffn_bwd(dh, h, g, w1, w2, saved, layer, after=()):
    f, pre, act = saved
    dpre = _mm_nt(dh, w2, name=f"ffn2_dx_{layer}", out_dtype=BF16, extras=(pre,), epilogue=_drelu2_epilogue,
                  after=after)
    dw2 = _mm_tn_stacked(act, dh, name=f"ffn2_dw_{layer}", col_slots=False)
    df = _mm_nt(dpre, w1, name=f"ffn1_dx_{layer}")
    dw1 = _mm_tn_stacked(f, dpre, name=f"ffn1_dw_{layer}", col_slots=True)
    dh, dg = _rms_bwd(h, g, df, dh, name=f"ffn_norm_bwd_{layer}")
    return dh, dg, dw1, dw2


def _kv_fwd(mem, g, w_kv, tag):
    m = _rms_fwd(mem, g, name=f"mem_norm_{tag}")
    return m, _mm_nn(m, w_kv, name=f"kv_{tag}")


def _kv_bwd(mem, g, w_kv, m, dk, dv, layer):
    dkv = jnp.concatenate([dk, dv], axis=1)
    dw = _mm_tn_stacked(m, dkv, name=f"kv_dw_{layer}", col_slots=True)
    dm = _mm_nt(dkv, w_kv, name=f"kv_dx_{layer}")
    _, dg = _rms_bwd(mem, g, dm, dm, name=f"mem_norm_bwd_{layer}")
    return dw, dg


def _local_step(x, mem, target, p, after_layer1=None, after_ffn0=None):
    row = lambda v: v.reshape(1, -1)
    g = {}

    h0 = x
    a0 = _rms_fwd(h0, row(p["norm_mix"][0]), name="mix_norm_0")
    proj_a = _mm_nn(a0, p["a_in"], name="a_in", after=p.get("after_start", ()))
    m0, kv0 = _kv_fwd(mem, row(p["mem_norm"][0]), p["w_kv"][0], "0")
    cat0 = _attn_fwd(proj_a, 2 * D_INNER, kv0, name="attn_0")
    bs_col = p["a_bs"].reshape(A_GROUPS, CHUNK, 1)
    cat0 = _gate_fwd(proj_a, p["a_ln_g"], p["a_ln_b"], p["a_ws"], bs_col, cat0, name="gate")
    h1 = _mm_nn(cat0, p["w_out"][0], name="out_0", extras=(h0,), epilogue=_res_epilogue)
    h2, ffn0 = _ffn_fwd(h1, row(p["norm_ffn"][0]), p["w_ffn1"][0], p["w_ffn2"][0], "0")

    if "layer1_mixer" in p:
        w_kv1, w_out1, b_in = p["layer1_mixer"](h2)
    else:
        w_kv1, w_out1, b_in = p["w_kv"][1], p["w_out"][1], p["b_in"]
    a1 = _rms_fwd(h2, row(p["norm_mix"][1]), name="mix_norm_1")
    proj_b = _mm_nn(a1, b_in, name="b_in")
    m1, kv1 = _kv_fwd(mem, row(p["mem_norm"][1]), w_kv1, "1")
    cat1 = _attn_fwd(proj_b, B_Q_OFF, kv1, name="attn_1")
    xbc = _conv_fwd(proj_b, p["b_conv_w"], p["b_conv_b"], name="conv")
    dt_raw = proj_b[:, B_DT_OFF:B_DT_OFF + SSM_HEADS].reshape(SEQ, SSM_GROUPS, SSM_HPG)
    dt_c = jnp.transpose(dt_raw, (1, 0, 2))
    dt_r = jnp.transpose(dt_raw, (1, 2, 0))
    per_head = lambda v: v.reshape(SSM_GROUPS, 1, SSM_HPG)
    ssd_par = (per_head(p["b_dt_bias"]), per_head(p["b_a_log"]), per_head(p["b_d"]), p["b_gnorm"])
    cat1, hprev = _ssd_fwd(xbc, proj_b, dt_c, dt_r, *ssd_par, cat1, name="ssd")
    h3 = _mm_nn(cat1, w_out1, name="out_1", extras=(h2,), epilogue=_res_epilogue)
    w_ffn1_1, w_ffn2_1 = p["layer1_ffn"](h3) if "layer1_ffn" in p else (p["w_ffn1"][1], p["w_ffn2"][1])
    h4, ffn1 = _ffn_fwd(h3, row(p["norm_ffn"][1]), w_ffn1_1, w_ffn2_1, "1")

    loss, dh, g["final_norm"] = _loss_head(h4, row(p["final_norm"]), target, name="loss_head")

    dh, dnf1, dw1_1, dw2_1 = _ffn_bwd(dh, h3, row(p["norm_ffn"][1]), w_ffn1_1, w_ffn2_1, ffn1, 1)
    dcat1 = _mm_nt(dh, w_out1, name="out_dx_1")
    dwo_1 = _mm_tn_stacked(cat1, dh, name="out_dw_1", col_slots=False)
    dproj_b, dk1, dv1 = _attn_bwd(proj_b, B_Q_OFF, kv1, dcat1, B_IN_PAD, B_Q_OFF, name="attn_bwd_1")
    (dproj_b, dxs, dbm, dcm, ddt_c, ddt_r, g["b_dt_bias"], g["b_a_log"], g["b_d"], g["b_gnorm"]) = _ssd_bwd(
        xbc, proj_b, dt_c, dt_r, *ssd_par, hprev, dcat1, dproj_b, name="ssd_bwd")
    dproj_b, g["b_conv_w"], g["b_conv_b"] = _conv_bwd(proj_b, p["b_conv_w"], p["b_conv_b"], dxs, dbm, dcm, dproj_b,
                                                      name="conv_bwd")
    ddt = jnp.transpose(ddt_c, (1, 0, 2)) + jnp.transpose(ddt_r, (2, 0, 1))
    ddt = jnp.pad(ddt.reshape(SEQ, SSM_HEADS), ((0, 0), (0, B_IN_PAD - B_DT_OFF - SSM_HEADS))).astype(BF16)
    dproj_b = lax.dynamic_update_slice(dproj_b, ddt, (0, B_DT_OFF))
    dwkv_1, dmn1 = _kv_bwd(mem, row(p["mem_norm"][1]), w_kv1, m1, dk1, dv1, 1)
    dwb = _b_in_grad_slots(_mm_tn(a1, dproj_b, name="b_in_dw"))
    da1 = _mm_nt(dproj_b, b_in, name="b_in_dx")
    dh, dnm1 = _rms_bwd(h2, row(p["norm_mix"][1]), da1, dh, name="mix_norm_bwd_1")
    layer1 = dict(w_kv=dwkv_1, w_out=dwo_1, w_ffn1=dw1_1, w_ffn2=dw2_1, b_in=dwb)
    token = () if after_layer1 is None else (after_layer1(layer1),)

    dh, dnf0, dw1_0, dw2_0 = _ffn_bwd(dh, h1, row(p["norm_ffn"][0]), p["w_ffn1"][0], p["w_ffn2"][0], ffn0, 0,
                                      after=token)
    ffn0_grads = dict(w_ffn1=dw1_0, w_ffn2=dw2_0)
    token = () if after_ffn0 is None else (after_ffn0(ffn0_grads),)
    dcat0 = _mm_nt(dh, p["w_out"][0], name="out_dx_0", after=token)
    dwo_0 = _mm_tn_stacked(cat0, dh, name="out_dw_0", col_slots=False)
    dproj_a, dk0, dv0 = _attn_bwd(proj_a, 2 * D_INNER, kv0, dcat0, A_IN, 2 * D_INNER, name="attn_bwd_0")
    dproj_a, g["a_ln_g"], g["a_ln_b"], g["a_ws"], dbs_col = _gate_bwd(
        proj_a, p["a_ln_g"], p["a_ln_b"], p["a_ws"], bs_col, dcat0, dproj_a, name="gate_bwd")
    g["a_bs"] = dbs_col.reshape(A_GROUPS, CHUNK)
    dwkv_0, dmn0 = _kv_bwd(mem, row(p["mem_norm"][0]), p["w_kv"][0], m0, dk0, dv0, 0)
    dwa = _mm_tn_stacked(a0, dproj_a, name="a_in_dw", col_slots=True)
    da0 = _mm_nt(dproj_a, p["a_in"], name="a_in_dx")
    dx, dnm0 = _rms_bwd(h0, row(p["norm_mix"][0]), da0, dh, name="mix_norm_bwd_0")

    g["norm_mix"] = jnp.concatenate([dnm0, dnm1], axis=0)
    g["norm_ffn"] = jnp.concatenate([dnf0, dnf1], axis=0)
    g["mem_norm"] = jnp.concatenate([dmn0, dmn1], axis=0)
    layer0 = dict(w_kv=dwkv_0, w_out=dwo_0, w_ffn1=dw1_0, w_ffn2=dw2_0, a_in=dwa)
    return loss, dx, g, layer0, layer1


def _b_in_full(gathered):
    n = B_IN // N_CHIPS
    dt0 = D_INNER + CONV_DIM - (N_CHIPS - 1) * n
    last = gathered[N_CHIPS - 1]
    return jnp.concatenate([*[gathered[k] for k in range(N_CHIPS - 1)], last[:, :dt0], last[:, dt0 + SSM_HEADS:],
                            last[:, dt0:dt0 + SSM_HEADS], jnp.zeros((D_MODEL, B_IN_PAD - B_IN), last.dtype)], axis=1)


def _b_in_grad_slots(d):
    n = B_IN // N_CHIPS
    dt0 = D_INNER + CONV_DIM
    last = jnp.concatenate([d[:, (N_CHIPS - 1) * n:dt0], d[:, B_DT_OFF:B_DT_OFF + SSM_HEADS], d[:, dt0:B_DT_OFF]], axis=1)
    slots = [*[d[:, k * n:(k + 1) * n] for k in range(N_CHIPS - 1)], last]
    half = D_MODEL // 2
    return jnp.stack([jnp.stack([s[h * half:(h + 1) * half] for s in slots]) for h in range(2)])


LARGE = ("w_kv", "w_out", "w_ffn1", "w_ffn2", "a_in", "b_in")
SMALL_REPL = ("norm_mix", "norm_ffn", "mem_norm", "a_ln_g", "a_ln_b", "a_ws", "a_bs", "b_dt_bias", "b_a_log", "b_d",
              "final_norm")
SMALL_SHARD = ("b_conv_w", "b_conv_b", "b_gnorm")
WEIGHTS = ("norm_mix", "norm_ffn", "mem_norm", "w_kv", "w_out", "w_ffn1", "w_ffn2", "a_in", "a_ln_g", "a_ln_b", "a_ws",
           "a_bs", "b_in", "b_conv_w", "b_conv_b", "b_dt_bias", "b_a_log", "b_d", "b_gnorm", "final_norm")
CONV_SHARD = CONV_DIM // N_CHIPS
GN_SHARD = D_INNER // N_CHIPS


LAYERED = ("w_kv", "w_out", "w_ffn1", "w_ffn2")
LAYER_TENSORS = (("w_kv", "w_out", "w_ffn1", "w_ffn2", "a_in"), ("w_kv", "w_out", "w_ffn1", "w_ffn2", "b_in"))


def _gather_weights(w):
    halves = lambda k, layer: (w[k][layer] if k in LAYERED else w[k][0]).reshape(2, -1, w[k].shape[-1]).astype(BF16)
    small = jnp.zeros((2, CONV_K, CONV_SHARD), F32)
    small = small.at[0].set(w["b_conv_w"][0])
    small = small.at[1, 0].set(w["b_conv_b"][0])
    small = small.at[1, 1, :GN_SHARD].set(w["b_gnorm"][0])
    gathered = _all_gather_shards([halves(k, 0) for k in LAYER_TENSORS[0]], small, name="gather_weights_0")
    got = dict(zip(LAYER_TENSORS[0], gathered))
    slots = lambda a: a.reshape(N_CHIPS, -1, a.shape[-1])
    rows = lambda a: a.reshape(-1, a.shape[-1])
    p = dict(w_kv=[slots(got["w_kv"])], w_out=[rows(got["w_out"])], w_ffn1=[slots(got["w_ffn1"])],
             w_ffn2=[rows(got["w_ffn2"])], a_in=slots(got["a_in"]))
    sm = gathered[-1]
    p["b_conv_w"] = jnp.transpose(sm[:, 0], (1, 0, 2)).reshape(CONV_K, CONV_DIM)
    p["b_conv_b"] = sm[:, 1, 0].reshape(1, CONV_DIM)
    p["b_gnorm"] = sm[:, 1, 1, :GN_SHARD].reshape(1, D_INNER)

    after, started = (gathered[0],), {}
    for tag, names in (("mixer", ("w_kv", "w_out", "b_in")), ("ffn", ("w_ffn1", "w_ffn2"))):
        started[tag] = _gather_start([halves(k, 1) for k in names], after, name=f"gather_start_1_{tag}")
        after = (started[tag][-1],)
    p["after_start"] = after

    def finish(tag, first):
        send_sems, recv_sems, shards, zones, _ = started[tag]
        shards, zones = _gather_wait(send_sems, recv_sems, shards, zones, (first,), name=f"gather_wait_1_{tag}")
        return _gather_finish(shards, zones, name=f"gather_finish_1_{tag}")

    def layer1_mixer(first):
        kv, wo, b_in = finish("mixer", first)
        return slots(kv), rows(wo), _b_in_full(slots(b_in))

    def layer1_ffn(first):
        w1, w2 = finish("ffn", first)
        return slots(w1), rows(w2)

    p.update(layer1_mixer=layer1_mixer, layer1_ffn=layer1_ffn)
    return p


def _pair_parts(grads, tag):
    stacks = [g.reshape(2, -1, g.shape[-1]) for g in grads.values()]
    parts = _pair_reduce(stacks, name=f"grads_pair_reduce_{tag}")
    return [t.reshape(N_CHIPS, -1, t.shape[-1]) for t in parts]


def _chip_sums(chip, names, parts, landed, tag):
    return {k: _sum_contributions(chip, t, u, name=f"grads_chip_sum_{k}_{tag}")
            for k, t, u in zip(names, parts, landed)}


def _small_layout(shapes):
    offs, o = {}, 0
    for k in (*SMALL_REPL, *SMALL_SHARD):
        size = math.prod(shapes[k])
        offs[k] = (o, size)
        o += size
    rows = -(-o // (8 * 128)) * 8
    return offs, rows


def _reduce_small(g, full_shapes):
    offs, rows = _small_layout(full_shapes)
    flat = jnp.concatenate([g[k].reshape(-1) for k in (*SMALL_REPL, *SMALL_SHARD)])
    flat = jnp.pad(flat, (0, rows * 128 - flat.shape[0])).reshape(rows, 128)
    total = _all_reduce_small(flat, name="grads_small_all_reduce").reshape(-1)
    return {k: total[o:o + n].reshape(full_shapes[k]) for k, (o, n) in offs.items()}


def kernel(x, mem, norm_mix, norm_ffn, mem_norm, w_kv, w_out, w_ffn1, w_ffn2, a_in, a_ln_g, a_ln_b, a_ws, a_bs, b_in, b_conv_w, b_conv_b, b_dt_bias, b_a_log, b_d, b_gnorm, final_norm, loss_target, m_norm_mix, m_norm_ffn, m_mem_norm, m_w_kv, m_w_out, m_w_ffn1, m_w_ffn2, m_a_in, m_a_ln_g, m_a_ln_b, m_a_ws, m_a_bs, m_b_in, m_b_conv_w, m_b_conv_b, m_b_dt_bias, m_b_a_log, m_b_d, m_b_gnorm, m_final_norm, v_norm_mix, v_norm_ffn, v_mem_norm, v_w_kv, v_w_out, v_w_ffn1, v_w_ffn2, v_a_in, v_a_ln_g, v_a_ln_b, v_a_ws, v_a_bs, v_b_in, v_b_conv_w, v_b_conv_b, v_b_dt_bias, v_b_a_log, v_b_d, v_b_gnorm, v_final_norm):
    w = dict(norm_mix=norm_mix, norm_ffn=norm_ffn, mem_norm=mem_norm, w_kv=w_kv, w_out=w_out, w_ffn1=w_ffn1,
             w_ffn2=w_ffn2, a_in=a_in, a_ln_g=a_ln_g, a_ln_b=a_ln_b, a_ws=a_ws, a_bs=a_bs, b_in=b_in, b_conv_w=b_conv_w,
             b_conv_b=b_conv_b, b_dt_bias=b_dt_bias, b_a_log=b_a_log, b_d=b_d, b_gnorm=b_gnorm, final_norm=final_norm)
    mom = dict(norm_mix=m_norm_mix, norm_ffn=m_norm_ffn, mem_norm=m_mem_norm, w_kv=m_w_kv, w_out=m_w_out,
               w_ffn1=m_w_ffn1, w_ffn2=m_w_ffn2, a_in=m_a_in, a_ln_g=m_a_ln_g, a_ln_b=m_a_ln_b, a_ws=m_a_ws,
               a_bs=m_a_bs, b_in=m_b_in, b_conv_w=m_b_conv_w, b_conv_b=m_b_conv_b, b_dt_bias=m_b_dt_bias,
               b_a_log=m_b_a_log, b_d=m_b_d, b_gnorm=m_b_gnorm, final_norm=m_final_norm)
    var = dict(norm_mix=v_norm_mix, norm_ffn=v_norm_ffn, mem_norm=v_mem_norm, w_kv=v_w_kv, w_out=v_w_out,
               w_ffn1=v_w_ffn1, w_ffn2=v_w_ffn2, a_in=v_a_in, a_ln_g=v_a_ln_g, a_ln_b=v_a_ln_b, a_ws=v_a_ws,
               a_bs=v_a_bs, b_in=v_b_in, b_conv_w=v_b_conv_w, b_conv_b=v_b_conv_b, b_dt_bias=v_b_dt_bias,
               b_a_log=v_b_a_log, b_d=v_b_d, b_gnorm=v_b_gnorm, final_norm=v_final_norm)

    p = _gather_weights(w)
    p.update(norm_mix=norm_mix, norm_ffn=norm_ffn, mem_norm=mem_norm, a_ln_g=a_ln_g, a_ln_b=a_ln_b, a_ws=a_ws[0],
             a_bs=a_bs[0], b_dt_bias=b_dt_bias, b_a_log=b_a_log, b_d=b_d, final_norm=final_norm)
    chip = 2 * lax.axis_index("x") + lax.axis_index("y")
    chip_arr = jnp.reshape(chip, (1,)).astype(jnp.int32)
    started = {}

    def start_scatter(tag):
        def hook(grads):
            started[tag] = (tuple(grads), _chip_scatter_start(_pair_parts(grads, tag), name=f"grads_chip_scatter_start_{tag}"))
            return started[tag][1][-1]
        return hook

    loss_part, dx, g, layer0, _ = _local_step(x[0], mem[0], loss_target[0], p, start_scatter("1"), start_scatter("0f"))
    loss = lax.psum(loss_part[0, 0], ("x", "y", "c"))

    full_shapes = {k: w[k].shape for k in SMALL_REPL}
    full_shapes.update(b_conv_w=(1, CONV_K, CONV_DIM), b_conv_b=(1, CONV_DIM), b_gnorm=(1, D_INNER))
    gs = _reduce_small(g, full_shapes)
    gs["b_conv_w"] = lax.dynamic_slice_in_dim(gs["b_conv_w"], chip * CONV_SHARD, CONV_SHARD, axis=2)
    gs["b_conv_b"] = lax.dynamic_slice_in_dim(gs["b_conv_b"], chip * CONV_SHARD, CONV_SHARD, axis=1)
    gs["b_gnorm"] = lax.dynamic_slice_in_dim(gs["b_gnorm"], chip * GN_SHARD, GN_SHARD, axis=1)
    mixer0 = {k: layer0[k] for k in ("w_kv", "w_out", "a_in")}
    parts_m = _pair_parts(mixer0, "0m")
    landed_m = _chip_scatter(parts_m, name="grads_chip_scatter_0m")
    halves = [_chip_sums(chip_arr, tuple(mixer0), parts_m, landed_m, "0m"), {}]
    for tag, layer in (("0f", 0), ("1", 1)):
        names, (send_sems, recv_sems, parts, lands, _) = started[tag]
        parts, landed = _chip_scatter_wait(send_sems, recv_sems, parts, lands, (dx,),
                                           name=f"grads_chip_scatter_wait_{tag}")
        halves[layer].update(_chip_sums(chip_arr, names, parts, landed, tag))
    groups = [[halves[layer][k] for layer in range(2) if k in halves[layer]] for k in LARGE]
    gl = dict(zip(LARGE, _pair_share(groups, name="grads_pair_share")))
    grads = {k: (gl[k].reshape(w[k].shape) if k in gl else gs[k]) for k in WEIGHTS}

    delta, new_m, new_v = {}, {}, {}
    for k in WEIGHTS:
        shape = w[k].shape
        flat = (lambda a: a.reshape(-1, shape[-1])) if len(shape) > 1 else (lambda a: a.reshape(1, -1))
        d, m_new, v_new = _adamw(flat(w[k]), flat(grads[k]), flat(mom[k]), flat(var[k]), name=f"adamw_{k}")
        delta[k], new_m[k], new_v[k] = d.reshape(shape), m_new.reshape(shape), v_new.reshape(shape)

    return (loss, dx.reshape(x.shape), *[grads[k] for k in WEIGHTS], *[delta[k] for k in WEIGHTS],
            *[new_m[k] for k in WEIGHTS], *[new_v[k] for k in WEIGHTS])
```

```python
import math

import jax
import jax.numpy as jnp
from jax import lax
from jax.experimental import pallas as pl
from jax.experimental.pallas import tpu as pltpu

F32 = jnp.float32
BF16 = jnp.bfloat16
SDS = jax.ShapeDtypeStruct

D_MODEL = 1024
SEQ = 2048
CHUNK = 128
N_MEM = 256
D_INNER = 2048
A_GROUPS = 8
A_GROUP_W = D_INNER // A_GROUPS
SSM_HEADS = 32
SSM_HEAD_DIM = 64
SSM_GROUPS = 4
SSM_HPG = 8
SSM_STATE = 128
SSM_GROUP_W = SSM_HPG * SSM_HEAD_DIM
CONV_K = 4
CONV_DIM = 3072
X_HEADS = 4
X_HEAD_DIM = 256
X_WIDTH = 1024
MIX_OUT = 3072
D_FF = 4096
A_IN = 5120
B_IN = 6176
B_IN_PAD = 6272
B_Q_OFF = 5120
B_DT_OFF = 6144
N_CHUNKS = SEQ // CHUNK
EPS = 1e-6
N_CHIPS = 4

ADAM_LR = 0.001
ADAM_B1 = 0.9
ADAM_B2 = 0.999
ADAM_EPS = 1e-08
ADAM_WD = 0.01
ADAM_STEP = 10

VMEM_LIMIT = 48 * 1024 * 1024
MESH = pl.DeviceIdType.MESH


def _cparams(sem):
    return pltpu.CompilerParams(dimension_semantics=sem, vmem_limit_bytes=VMEM_LIMIT)


def _dot(a, b, dims=(((1,), (0,)), ((), ()))):
    return lax.dot_general(a.astype(BF16), b.astype(BF16), dims, preferred_element_type=F32)


def _dot_nt(a, b):
    return _dot(a, b, (((1,), (1,)), ((), ())))


def _dot_tn(a, b):
    return _dot(a, b, (((0,), (0,)), ((), ())))


def _pick(n, cands):
    for c in cands:
        if n % c == 0:
            return c
    raise ValueError(f"no tile for {n}")


def _mm_call(a, b, *, dims, grid, a_spec, b_spec, acc_shape, out_shapes, out_specs, name,
             extras=(), extra_specs=(), epilogue=None, after=()):
    n_k = grid[2]
    n_extra = len(extras)
    n_out = len(out_shapes)
    n_in = 2 + n_extra + len(after)

    def body(*refs):
        a_ref, b_ref = refs[0], refs[1]
        extra_refs = refs[2:2 + n_extra]
        out_refs = refs[n_in:n_in + n_out]
        acc = refs[-1]
        k = pl.program_id(2)

        @pl.when(k == 0)
        def _():
            acc[...] = jnp.zeros_like(acc)

        acc[...] += _dot(a_ref[...], b_ref[...], dims)

        @pl.when(k == n_k - 1)
        def _():
            vals = (acc[...],) if epilogue is None else epilogue(acc[...], *[e[...] for e in extra_refs])
            for o_ref, v in zip(out_refs, vals):
                o_ref[...] = v.astype(o_ref.dtype)

    return pl.pallas_call(
        body, grid=grid, in_specs=[a_spec, b_spec, *extra_specs, *([ANY] * len(after))], out_specs=list(out_specs),
        out_shape=list(out_shapes), scratch_shapes=[pltpu.VMEM(acc_shape, F32)],
        compiler_params=_cparams(("parallel", "parallel", "arbitrary")), name=name,
    )(a, b, *extras, *after)


def _w_dims(w):
    if w.ndim == 2:
        return w.shape[0], w.shape[1], 1, w.shape[1]
    return w.shape[1], w.shape[0] * w.shape[2], w.shape[0], w.shape[2]


def _mm_nn(a, w, *, name, out_dtype=F32, a_cols=None, extras=(), epilogue=None, n_out_dtypes=None, after=()):
    m = a.shape[0]
    k_dim, n_dim, _, n_slot = _w_dims(w)
    a_off, a_w = (0, a.shape[1]) if a_cols is None else a_cols
    assert a_w == k_dim
    tm = _pick(m, (2048, 1024, 512, 256))
    tn = _pick(n_slot, (512, 896, 640, 256, 128))
    tk = _pick(k_dim, (1024, 768, 512, 384, 256, 128))
    assert a_off % tk == 0
    nb = n_slot // tn
    a_spec = pl.BlockSpec((tm, tk), lambda i, j, k: (i, a_off // tk + k))
    if w.ndim == 2:
        b_spec = pl.BlockSpec((tk, tn), lambda i, j, k: (k, j))
    else:
        b_spec = pl.BlockSpec((None, tk, tn), lambda i, j, k: (j // nb, k, j % nb))
    o_spec = pl.BlockSpec((tm, tn), lambda i, j, k: (i, j))
    dts = n_out_dtypes or (out_dtype,)
    outs = _mm_call(a, w, dims=(((1,), (0,)), ((), ())), grid=(m // tm, n_dim // tn, k_dim // tk),
                    a_spec=a_spec, b_spec=b_spec, acc_shape=(tm, tn),
                    out_shapes=[SDS((m, n_dim), dt) for dt in dts], out_specs=[o_spec] * len(dts), name=name,
                    extras=extras, extra_specs=[o_spec] * len(extras), epilogue=epilogue, after=after)
    return outs if n_out_dtypes else outs[0]


def _mm_nt(a, w, *, name, out_dtype=F32, extras=(), epilogue=None, after=()):
    m = a.shape[0]
    k_dim, n_dim, _, n_slot = _w_dims(w)
    assert a.shape[1] == n_dim
    tm = _pick(m, (2048, 1024, 512, 256))
    to = _pick(k_dim, (512, 384, 256, 128))
    tc = _pick(n_slot, (1024, 896, 640, 512, 256, 128))
    nb = n_slot // tc
    a_spec = pl.BlockSpec((tm, tc), lambda i, j, k: (i, k))
    if w.ndim == 2:
        b_spec = pl.BlockSpec((to, tc), lambda i, j, k: (j, k))
    else:
        b_spec = pl.BlockSpec((None, to, tc), lambda i, j, k: (k // nb, j, k % nb))
    o_spec = pl.BlockSpec((tm, to), lambda i, j, k: (i, j))
    return _mm_call(a, w, dims=(((1,), (1,)), ((), ())), grid=(m // tm, k_dim // to, n_dim // tc),
                    a_spec=a_spec, b_spec=b_spec, acc_shape=(tm, to),
                    out_shapes=[SDS((m, k_dim), out_dtype)], out_specs=[o_spec], name=name,
                    extras=extras, extra_specs=[o_spec] * len(extras), epilogue=epilogue, after=after)[0]


def _mm_tn(x, dy, *, name, x_cols=None):
    s = x.shape[0]
    x_off, k_dim = (0, x.shape[1]) if x_cols is None else x_cols
    n_dim = dy.shape[1]
    tm = _pick(k_dim, (1024, 768, 512, 384, 256, 128))
    tn = _pick(n_dim, (512, 896, 640, 256, 128))
    tk = _pick(s, (2048, 1024, 512, 256))
    assert x_off % tm == 0
    a_spec = pl.BlockSpec((tk, tm), lambda i, j, k: (k, x_off // tm + i))
    b_spec = pl.BlockSpec((tk, tn), lambda i, j, k: (k, j))
    o_spec = pl.BlockSpec((tm, tn), lambda i, j, k: (i, j))
    return _mm_call(x, dy, dims=(((0,), (0,)), ((), ())), grid=(k_dim // tm, n_dim // tn, s // tk),
                    a_spec=a_spec, b_spec=b_spec, acc_shape=(tm, tn),
                    out_shapes=[SDS((k_dim, n_dim), F32)], out_specs=[o_spec], name=name)[0]


def _mm_tn_stacked(x, dy, *, name, col_slots):
    s, k_dim = x.shape
    n_dim = dy.shape[1]
    r, c = (k_dim // 2, n_dim // N_CHIPS) if col_slots else (k_dim // N_CHIPS // 2, n_dim)
    tm = 2 * r
    tn = _pick(c, (512, 896, 640, 256, 128))
    tk = _pick(s, (2048, 1024, 512, 256))
    a_spec = pl.BlockSpec((tk, tm), lambda i, j, k: (k, i))
    b_spec = pl.BlockSpec((tk, tn), lambda i, j, k: (k, j))
    if col_slots:
        nb = c // tn
        o_spec = pl.BlockSpec((2, None, r, tn), lambda i, j, k: (0, j // nb, 0, j % nb))
    else:
        o_spec = pl.BlockSpec((2, None, r, tn), lambda i, j, k: (0, i, 0, j))
    return _mm_call(x, dy, dims=(((0,), (0,)), ((), ())), grid=(k_dim // tm, n_dim // tn, s // tk),
                    a_spec=a_spec, b_spec=b_spec, acc_shape=(tm, tn), epilogue=lambda acc: (acc.reshape(2, r, tn),),
                    out_shapes=[SDS((2, N_CHIPS, r, c), F32)], out_specs=[o_spec], name=name)[0]


def _rms(x, g):
    return x * lax.rsqrt(jnp.mean(x * x, axis=-1, keepdims=True) + EPS) * g


def _rms_fwd(h, g, *, name):
    rows, d = h.shape
    tr = _pick(rows, (512, 256))

    def body(h_ref, g_ref, o_ref):
        o_ref[...] = _rms(h_ref[...], g_ref[...]).astype(o_ref.dtype)

    return pl.pallas_call(
        body, grid=(rows // tr,),
        in_specs=[pl.BlockSpec((tr, d), lambda i: (i, 0)), pl.BlockSpec((1, d), lambda i: (0, 0))],
        out_specs=pl.BlockSpec((tr, d), lambda i: (i, 0)), out_shape=SDS((rows, d), BF16),
        compiler_params=_cparams(("parallel",)), name=name)(h, g)


def _rms_bwd(h, g, da, dres, *, name):
    rows, d = h.shape
    tr = _pick(rows, (512, 256))

    def body(h_ref, g_ref, da_ref, dres_ref, dh_ref, dg_ref):
        _, vjp = jax.vjp(_rms, h_ref[...], g_ref[...])
        dh, dg = vjp(da_ref[...].astype(F32))
        dh_ref[...] = dres_ref[...] + dh

        @pl.when(pl.program_id(0) == 0)
        def _():
            dg_ref[...] = jnp.zeros_like(dg_ref)

        dg_ref[...] += dg

    row_spec = pl.BlockSpec((tr, d), lambda i: (i, 0))
    vec_spec = pl.BlockSpec((1, d), lambda i: (0, 0))
    return pl.pallas_call(
        body, grid=(rows // tr,), in_specs=[row_spec, vec_spec, row_spec, row_spec],
        out_specs=[row_spec, vec_spec], out_shape=[SDS((rows, d), F32), SDS((1, d), F32)],
        compiler_params=_cparams(("arbitrary",)), name=name)(h, g, da, dres)


def _loss_head(h, g, target, *, name):
    rows, d = h.shape
    tr = _pick(rows, (512, 256))

    def body(h_ref, g_ref, t_ref, loss_ref, dh_ref, dg_ref):
        y, vjp = jax.vjp(_rms, h_ref[...], g_ref[...])
        err = y - t_ref[...]
        dh, dg = vjp(err * (1.0 / d))
        dh_ref[...] = dh

        @pl.when(pl.program_id(0) == 0)
        def _():
            dg_ref[...] = jnp.zeros_like(dg_ref)
            loss_ref[...] = jnp.zeros_like(loss_ref)

        dg_ref[...] += dg
        part = jnp.sum(jnp.sum(err * err, axis=-1, keepdims=True), axis=0, keepdims=True) * (0.5 / d)
        loss_ref[...] += jnp.broadcast_to(part, loss_ref.shape)

    row_spec = pl.BlockSpec((tr, d), lambda i: (i, 0))
    vec_spec = pl.BlockSpec((1, d), lambda i: (0, 0))
    loss_spec = pl.BlockSpec((8, 128), lambda i: (0, 0))
    return pl.pallas_call(
        body, grid=(rows // tr,), in_specs=[row_spec, vec_spec, row_spec],
        out_specs=[loss_spec, row_spec, vec_spec],
        out_shape=[SDS((8, 128), F32), SDS((rows, d), F32), SDS((1, d), F32)],
        compiler_params=_cparams(("arbitrary",)), name=name)(h, g, target)


def _gelu(x):
    return 0.5 * x * (1.0 + lax.erf(x * (1.0 / math.sqrt(2.0))))


def _gate_tile(pu, pv, ln_g, ln_b, ws, bs_t):
    u = [_gelu(p) for p in pu]
    v = [_gelu(p) for p in pv]
    mu = sum(jnp.sum(t, axis=-1, keepdims=True) for t in v) * (1.0 / D_INNER)
    vc = [t - mu for t in v]
    var = sum(jnp.sum(t * t, axis=-1, keepdims=True) for t in vc) * (1.0 / D_INNER)
    rstd = lax.rsqrt(var + EPS)
    row = lax.broadcasted_iota(jnp.int32, (CHUNK, CHUNK), 0)
    col = lax.broadcasted_iota(jnp.int32, (CHUNK, CHUNK), 1)
    out = []
    for gi in range(A_GROUPS):
        vn = vc[gi] * rstd * ln_g[gi] + ln_b[gi]
        w = jnp.where(row >= col, ws[gi], 0.0)
        sv = _dot(w, vn) + bs_t[gi]
        out.append(u[gi] * sv)
    return out


def _split(ref, n, width):
    return [ref[:, i * width:(i + 1) * width] for i in range(n)]


def _gate_in_specs():
    return [
        pl.BlockSpec((CHUNK, D_INNER), lambda c: (c, 0)),
        pl.BlockSpec((CHUNK, D_INNER), lambda c: (c, 1)),
        pl.BlockSpec((1, D_INNER), lambda c: (0, 0)),
        pl.BlockSpec((1, D_INNER), lambda c: (0, 0)),
        pl.BlockSpec((A_GROUPS, CHUNK, CHUNK), lambda c: (0, 0, 0)),
        pl.BlockSpec((A_GROUPS, CHUNK, 1), lambda c: (0, 0, 0)),
    ]


def _gate_args(u_ref, v_ref, g_ref, b_ref, ws_ref, bs_ref):
    ng, gw = A_GROUPS, A_GROUP_W
    return (_split(u_ref, ng, gw), _split(v_ref, ng, gw), _split(g_ref, ng, gw), _split(b_ref, ng, gw),
            [ws_ref[i] for i in range(ng)], [bs_ref[i] for i in range(ng)])


def _gate_fwd(proj, ln_g, ln_b, ws, bs_col, mixcat, *, name):
    def body(u_ref, v_ref, g_ref, b_ref, ws_ref, bs_ref, cat_in, cat_ref):
        del cat_in
        out = _gate_tile(*_gate_args(u_ref, v_ref, g_ref, b_ref, ws_ref, bs_ref))
        for gi, o in enumerate(out):
            cat_ref[:, gi * A_GROUP_W:(gi + 1) * A_GROUP_W] = o.astype(cat_ref.dtype)

    return pl.pallas_call(
        body, grid=(N_CHUNKS,), in_specs=[*_gate_in_specs(), pl.BlockSpec(memory_space=pl.ANY)],
        out_specs=pl.BlockSpec((CHUNK, D_INNER), lambda c: (c, 0)), out_shape=SDS(mixcat.shape, mixcat.dtype),
        input_output_aliases={6: 0}, compiler_params=_cparams(("parallel",)), name=name,
    )(proj, proj, ln_g, ln_b, ws, bs_col, mixcat)


def _gate_bwd(proj, ln_g, ln_b, ws, bs_col, dcat, dproj, *, name):
    ng, gw = A_GROUPS, A_GROUP_W

    def body(u_ref, v_ref, g_ref, b_ref, ws_ref, bs_ref, d_ref, dproj_in, dproj_ref, dg_ref, db_ref, dws_ref, dbs_ref):
        del dproj_in
        args = _gate_args(u_ref, v_ref, g_ref, b_ref, ws_ref, bs_ref)
        _, vjp = jax.vjp(_gate_tile, *args)
        dpu, dpv, dg, db, dws, dbs = vjp(_split(d_ref, ng, gw))
        for gi in range(ng):
            dproj_ref[:, gi * gw:(gi + 1) * gw] = dpu[gi].astype(dproj_ref.dtype)
            dproj_ref[:, D_INNER + gi * gw:D_INNER + (gi + 1) * gw] = dpv[gi].astype(dproj_ref.dtype)

        @pl.when(pl.program_id(0) == 0)
        def _():
            for r in (dg_ref, db_ref, dws_ref, dbs_ref):
                r[...] = jnp.zeros_like(r)

        for gi in range(ng):
            dg_ref[:, gi * gw:(gi + 1) * gw] += dg[gi]
            db_ref[:, gi * gw:(gi + 1) * gw] += db[gi]
            dws_ref[gi] += dws[gi]
            dbs_ref[gi] += dbs[gi]

    in_specs = _gate_in_specs()
    return pl.pallas_call(
        body, grid=(N_CHUNKS,),
        in_specs=[*in_specs, pl.BlockSpec((CHUNK, D_INNER), lambda c: (c, 0)), pl.BlockSpec(memory_space=pl.ANY)],
        out_specs=[pl.BlockSpec((CHUNK, 2 * D_INNER), lambda c: (c, 0)), *in_specs[2:]],
        out_shape=[SDS(dproj.shape, dproj.dtype), SDS((1, D_INNER), F32), SDS((1, D_INNER), F32),
                   SDS((ng, CHUNK, CHUNK), F32), SDS((ng, CHUNK, 1), F32)],
        input_output_aliases={7: 0}, compiler_params=_cparams(("arbitrary",)), name=name,
    )(proj, proj, ln_g, ln_b, ws, bs_col, dcat, dproj)


ATT_TQ = 512


def _attn_tile(q, k, v):
    s = _dot_nt(q, k) * (1.0 / math.sqrt(X_HEAD_DIM))
    s = s - jnp.max(s, axis=-1, keepdims=True)
    e = jnp.exp(s)
    p = e / jnp.sum(e, axis=-1, keepdims=True)
    return _dot(p, v)


def _attn_in_specs(q_blk, order):
    hd = X_HEAD_DIM
    return [
        pl.BlockSpec((ATT_TQ, hd), lambda a, b: (order(a, b)[0], q_blk + order(a, b)[1])),
        pl.BlockSpec((N_MEM, hd), lambda a, b: (0, order(a, b)[1])),
        pl.BlockSpec((N_MEM, hd), lambda a, b: (0, X_HEADS + order(a, b)[1])),
    ]


def _attn_fwd(proj, q_off, kv, *, name):
    order = lambda i, h: (i, h)
    cat_blk = D_INNER // X_HEAD_DIM

    def body(q_ref, k_ref, v_ref, o_ref):
        o_ref[...] = _attn_tile(q_ref[...], k_ref[...], v_ref[...]).astype(o_ref.dtype)

    return pl.pallas_call(
        body, grid=(SEQ // ATT_TQ, X_HEADS), in_specs=_attn_in_specs(q_off // X_HEAD_DIM, order),
        out_specs=pl.BlockSpec((ATT_TQ, X_HEAD_DIM), lambda i, h: (i, cat_blk + h)),
        out_shape=SDS((SEQ, MIX_OUT), BF16), compiler_params=_cparams(("parallel", "parallel")), name=name,
    )(proj, kv, kv)


def _attn_bwd(proj, q_off, kv, dcat, dproj_width, dq_off, *, name):
    order = lambda h, i: (i, h)
    cat_blk = D_INNER // X_HEAD_DIM
    dq_blk = dq_off // X_HEAD_DIM

    def body(q_ref, k_ref, v_ref, do_ref, dq_ref, dk_ref, dv_ref):
        _, vjp = jax.vjp(_attn_tile, q_ref[...], k_ref[...], v_ref[...])
        dq, dk, dv = vjp(do_ref[...])
        dq_ref[...] = dq.astype(dq_ref.dtype)

        @pl.when(pl.program_id(1) == 0)
        def _():
            dk_ref[...] = jnp.zeros_like(dk_ref)
            dv_ref[...] = jnp.zeros_like(dv_ref)

        dk_ref[...] += dk
        dv_ref[...] += dv

    kv_spec = pl.BlockSpec((N_MEM, X_HEAD_DIM), lambda h, i: (0, h))
    return pl.pallas_call(
        body, grid=(X_HEADS, SEQ // ATT_TQ),
        in_specs=[*_attn_in_specs(q_off // X_HEAD_DIM, order),
                  pl.BlockSpec((ATT_TQ, X_HEAD_DIM), lambda h, i: (i, cat_blk + h))],
        out_specs=[pl.BlockSpec((ATT_TQ, X_HEAD_DIM), lambda h, i: (i, dq_blk + h)), kv_spec, kv_spec],
        out_shape=[SDS((SEQ, dproj_width), BF16), SDS((N_MEM, X_WIDTH), F32), SDS((N_MEM, X_WIDTH), F32)],
        compiler_params=_cparams(("parallel", "arbitrary")), name=name,
    )(proj, kv, kv, dcat)


CONV_TC = 512


def _shift_down(x, s):
    if s == 0:
        return x
    row = lax.broadcasted_iota(jnp.int32, x.shape, 0)
    return jnp.where(row >= s, pltpu.roll(x, s, 0), 0.0)


def _shift_up(x, s):
    if s == 0:
        return x
    n = x.shape[0]
    row = lax.broadcasted_iota(jnp.int32, x.shape, 0)
    return jnp.where(row < n - s, pltpu.roll(x, n - s, 0), 0.0)


def _conv_pre(x, w_ref, b_ref):
    pre = b_ref[...] + jnp.zeros_like(x)
    for k in range(CONV_K):
        pre = pre + w_ref[k:k + 1, :] * _shift_down(x, CONV_K - 1 - k)
    return pre


def _conv_fwd(proj, w, b, *, name):
    blk0 = D_INNER // CONV_TC

    def body(x_ref, w_ref, b_ref, o_ref):
        pre = _conv_pre(x_ref[...], w_ref, b_ref)
        o_ref[...] = pre * jax.nn.sigmoid(pre)

    return pl.pallas_call(
        body, grid=(CONV_DIM // CONV_TC,),
        in_specs=[pl.BlockSpec((SEQ, CONV_TC), lambda j: (0, blk0 + j)), pl.BlockSpec((CONV_K, CONV_TC), lambda j: (0, j)),
                  pl.BlockSpec((1, CONV_TC), lambda j: (0, j))],
        out_specs=pl.BlockSpec((SEQ, CONV_TC), lambda j: (0, j)), out_shape=SDS((SEQ, CONV_DIM), F32),
        compiler_params=_cparams(("parallel",)), name=name)(proj, w, b)


def _conv_bwd(proj, w, b, dxs, dbm, dcm, dproj, *, name):
    tc = CONV_TC // 2
    blk0 = D_INNER // tc
    n_x = D_INNER // tc
    n_b = SSM_GROUPS * SSM_STATE // tc

    def body(x_ref, w_ref, b_ref, dxs_ref, dbm_ref, dcm_ref, dproj_in, dproj_ref, dw_ref, db_ref):
        del dproj_in
        j = pl.program_id(0)
        x = x_ref[...]
        pre = _conv_pre(x, w_ref, b_ref)
        sg = jax.nn.sigmoid(pre)
        dact = jnp.where(j < n_x, dxs_ref[...], jnp.where(j < n_x + n_b, dbm_ref[...], dcm_ref[...]))
        dpre = dact * (sg * (1.0 + pre * (1.0 - sg)))
        dx = jnp.zeros_like(x)
        for k in range(CONV_K):
            s = CONV_K - 1 - k
            dx = dx + w_ref[k:k + 1, :] * _shift_up(dpre, s)
            dw_ref[k:k + 1, :] = jnp.sum(dpre * _shift_down(x, s), axis=0, keepdims=True)
        dproj_ref[...] = dx.astype(dproj_ref.dtype)
        db_ref[...] = jnp.sum(dpre, axis=0, keepdims=True)

    clip = lambda v, hi: jnp.minimum(jnp.maximum(v, 0), hi)
    return pl.pallas_call(
        body, grid=(CONV_DIM // tc,),
        in_specs=[pl.BlockSpec((SEQ, tc), lambda j: (0, blk0 + j)), pl.BlockSpec((CONV_K, tc), lambda j: (0, j)),
                  pl.BlockSpec((1, tc), lambda j: (0, j)),
                  pl.BlockSpec((SEQ, tc), lambda j: (0, clip(j, n_x - 1))),
                  pl.BlockSpec((SEQ, tc), lambda j: (0, clip(j - n_x, n_b - 1))),
                  pl.BlockSpec((SEQ, tc), lambda j: (0, clip(j - n_x - n_b, n_b - 1))),
                  pl.BlockSpec(memory_space=pl.ANY)],
        out_specs=[pl.BlockSpec((SEQ, tc), lambda j: (0, blk0 + j)), pl.BlockSpec((CONV_K, tc), lambda j: (0, j)),
                   pl.BlockSpec((1, tc), lambda j: (0, j))],
        out_shape=[SDS(dproj.shape, dproj.dtype), SDS((CONV_K, CONV_DIM), F32), SDS((1, CONV_DIM), F32)],
        input_output_aliases={6: 0}, compiler_params=_cparams(("parallel",)), name=name,
    )(proj, w, b, dxs, dbm, dcm, dproj)


SSM_PAIRS = SSM_HPG // 2


def _ssd_tile(xp, zp, bm, cm, hp, dtc, dtr, bias, alog, dsk, gnp):
    row = lax.broadcasted_iota(jnp.int32, (CHUNK, CHUNK), 0)
    col = lax.broadcasted_iota(jnp.int32, (CHUNK, CHUNK), 1)
    causal = row >= col
    tri = jnp.where(causal, 1.0, 0.0)
    left = col < SSM_HEAD_DIM
    top = row < SSM_HEAD_DIM
    ones = jnp.ones((CHUNK, CHUNK), BF16)
    cb = _dot_nt(cm, bm)
    dt_c, cs_c, cs_last, m = [], [], [], []
    for r in range(SSM_HPG):
        a = -jnp.exp(alog[r])
        dt_c.append(jax.nn.softplus(dtc[r] + bias[r]))
        da_c = dt_c[r] * a
        da_r = jax.nn.softplus(dtr[r] + bias[r]) * a
        cs_c.append(jnp.sum(tri * da_r, axis=1, keepdims=True))
        cs_r = jnp.sum(jnp.where(row <= col, 1.0, 0.0) * da_c, axis=0, keepdims=True)
        cs_last.append(jnp.sum(da_c, axis=0, keepdims=True))
        m.append(cb * jnp.exp(jnp.where(causal, cs_c[r] - cs_r, -1e30)))
    ygs, hn = [], []
    for p in range(SSM_PAIRS):
        a, b = 2 * p, 2 * p + 1
        pair = lambda u, v: jnp.where(left, u, v)
        xdt = xp[p] * pair(dt_c[a], dt_c[b])
        y = pair(_dot(m[a], xdt), _dot(m[b], xdt))
        y = y + _dot_nt(cm, hp[p]) * pair(jnp.exp(cs_c[a]), jnp.exp(cs_c[b]))
        y = y + xp[p] * pair(dsk[a], dsk[b])
        decay = pair(jnp.exp(cs_last[a] - cs_c[a]), jnp.exp(cs_last[b] - cs_c[b]))
        states = _dot_tn(xdt * decay, bm)
        hn.append(hp[p] * jnp.where(top, jnp.exp(cs_last[a]), jnp.exp(cs_last[b])) + states)
        ygs.append(y * (zp[p] * jax.nn.sigmoid(zp[p])))
    ms = sum(_dot(t * t, ones) for t in ygs) * (1.0 / SSM_GROUP_W)
    rs = lax.rsqrt(ms + EPS)
    return [ygs[p] * rs * gnp[p] for p in range(SSM_PAIRS)], hn


def _ssd_in_specs(cidx):
    gw, n = SSM_GROUP_W, SSM_STATE
    bm_blk = D_INNER // n
    return [
        pl.BlockSpec((CHUNK, gw), lambda g, c: (cidx(c), g)),
        pl.BlockSpec((CHUNK, gw), lambda g, c: (cidx(c), g)),
        pl.BlockSpec((CHUNK, n), lambda g, c: (cidx(c), bm_blk + g)),
        pl.BlockSpec((CHUNK, n), lambda g, c: (cidx(c), bm_blk + SSM_GROUPS + g)),
        pl.BlockSpec((None, CHUNK, SSM_HPG), lambda g, c: (g, cidx(c), 0)),
        pl.BlockSpec((None, SSM_HPG, CHUNK), lambda g, c: (g, 0, cidx(c))),
        pl.BlockSpec((None, 1, SSM_HPG), lambda g, c: (g, 0, 0)),
        pl.BlockSpec((None, 1, SSM_HPG), lambda g, c: (g, 0, 0)),
        pl.BlockSpec((None, 1, SSM_HPG), lambda g, c: (g, 0, 0)),
        pl.BlockSpec((1, gw), lambda g, c: (0, g)),
    ]


def _ssd_args(x_ref, z_ref, bm_ref, cm_ref, hp, dtc_ref, dtr_ref, bias_ref, alog_ref, dsk_ref, gn_ref):
    nh, npair, w = SSM_HPG, SSM_PAIRS, 2 * SSM_HEAD_DIM
    col = lambda ref: [ref[:, r:r + 1] for r in range(nh)]
    return (_split(x_ref, npair, w), _split(z_ref, npair, w), bm_ref[...], cm_ref[...], hp,
            col(dtc_ref), [dtr_ref[r:r + 1, :] for r in range(nh)], col(bias_ref), col(alog_ref), col(dsk_ref),
            _split(gn_ref, npair, w))


def _pair_rows(ref):
    w = 2 * SSM_HEAD_DIM
    return [ref[p * w:(p + 1) * w, :] for p in range(SSM_PAIRS)]


def _ssd_fwd(xbc, proj, dt_c, dt_r, bias, alog, dsk, gn, mixcat, *, name):
    w = 2 * SSM_HEAD_DIM

    def body(x_ref, z_ref, bm_ref, cm_ref, dtc_ref, dtr_ref, bias_ref, alog_ref, dsk_ref, gn_ref, cat_in,
             cat_ref, hprev_ref, h_scr):
        del cat_in

        @pl.when(pl.program_id(1) == 0)
        def _():
            h_scr[...] = jnp.zeros_like(h_scr)

        hprev_ref[...] = h_scr[...]
        yn, hn = _ssd_tile(*_ssd_args(x_ref, z_ref, bm_ref, cm_ref, _pair_rows(h_scr), dtc_ref, dtr_ref, bias_ref,
                                      alog_ref, dsk_ref, gn_ref))
        for p in range(SSM_PAIRS):
            cat_ref[:, p * w:(p + 1) * w] = yn[p].astype(cat_ref.dtype)
            h_scr[p * w:(p + 1) * w, :] = hn[p]

    return pl.pallas_call(
        body, grid=(SSM_GROUPS, N_CHUNKS), in_specs=[*_ssd_in_specs(lambda c: c), pl.BlockSpec(memory_space=pl.ANY)],
        out_specs=[pl.BlockSpec((CHUNK, SSM_GROUP_W), lambda g, c: (c, g)),
                   pl.BlockSpec((None, None, SSM_GROUP_W, SSM_STATE), lambda g, c: (c, g, 0, 0))],
        out_shape=[SDS(mixcat.shape, mixcat.dtype), SDS((N_CHUNKS, SSM_GROUPS, SSM_GROUP_W, SSM_STATE), F32)],
        scratch_shapes=[pltpu.VMEM((SSM_GROUP_W, SSM_STATE), F32)],
        input_output_aliases={10: 0}, compiler_params=_cparams(("parallel", "arbitrary")), name=name,
    )(xbc, proj, xbc, xbc, dt_c, dt_r, bias, alog, dsk, gn, mixcat)


def _ssd_bwd(xbc, proj, dt_c, dt_r, bias, alog, dsk, gn, hprev, dcat, dproj, *, name):
    nh, w, gw, n = SSM_HPG, 2 * SSM_HEAD_DIM, SSM_GROUP_W, SSM_STATE
    rev = lambda c: N_CHUNKS - 1 - c

    def body(x_ref, z_ref, bm_ref, cm_ref, dtc_ref, dtr_ref, bias_ref, alog_ref, dsk_ref, gn_ref, hprev_ref, dy_ref,
             dproj_in, dz_ref, dxs_ref, dbm_ref, dcm_ref, ddtc_ref, ddtr_ref, dbias_ref, dalog_ref, ddsk_ref, dgn_ref,
             dh_scr):
        del dproj_in
        first = pl.program_id(1) == 0

        @pl.when(first)
        def _():
            dh_scr[...] = jnp.zeros_like(dh_scr)
            for ref in (dbias_ref, dalog_ref, ddsk_ref, dgn_ref):
                ref[...] = jnp.zeros_like(ref)

        args = _ssd_args(x_ref, z_ref, bm_ref, cm_ref, _pair_rows(hprev_ref), dtc_ref, dtr_ref, bias_ref, alog_ref,
                         dsk_ref, gn_ref)
        _, vjp = jax.vjp(_ssd_tile, *args)
        dxs, dzs, dbm, dcm, dhs, ddtc, ddtr, dbias, dalog, ddsk, dgn = vjp(
            (_split(dy_ref, SSM_PAIRS, w), _pair_rows(dh_scr)))
        dbm_ref[...] = dbm
        dcm_ref[...] = dcm
        for q in range(SSM_PAIRS):
            dxs_ref[:, q * w:(q + 1) * w] = dxs[q]
            dz_ref[:, q * w:(q + 1) * w] = dzs[q].astype(dz_ref.dtype)
            dh_scr[q * w:(q + 1) * w, :] = dhs[q]
            dgn_ref[:, q * w:(q + 1) * w] += dgn[q]
        for r in range(nh):
            ddtc_ref[:, r:r + 1] = ddtc[r]
            ddtr_ref[r:r + 1, :] = ddtr[r]
            dbias_ref[:, r:r + 1] += dbias[r]
            dalog_ref[:, r:r + 1] += dalog[r]
            ddsk_ref[:, r:r + 1] += ddsk[r]

    par_spec = pl.BlockSpec((None, 1, nh), lambda g, c: (g, 0, 0))
    return pl.pallas_call(
        body, grid=(SSM_GROUPS, N_CHUNKS),
        in_specs=[*_ssd_in_specs(rev),
                  pl.BlockSpec((None, None, gw, n), lambda g, c: (rev(c), g, 0, 0)),
                  pl.BlockSpec((CHUNK, gw), lambda g, c: (rev(c), g)),
                  pl.BlockSpec(memory_space=pl.ANY)],
        out_specs=[pl.BlockSpec((CHUNK, gw), lambda g, c: (rev(c), g)),
                   pl.BlockSpec((CHUNK, gw), lambda g, c: (rev(c), g)),
                   pl.BlockSpec((CHUNK, n), lambda g, c: (rev(c), g)),
                   pl.BlockSpec((CHUNK, n), lambda g, c: (rev(c), g)),
                   pl.BlockSpec((None, CHUNK, nh), lambda g, c: (g, rev(c), 0)),
                   pl.BlockSpec((None, nh, CHUNK), lambda g, c: (g, 0, rev(c))),
                   par_spec, par_spec, par_spec,
                   pl.BlockSpec((1, gw), lambda g, c: (0, g))],
        out_shape=[SDS(dproj.shape, dproj.dtype), SDS((SEQ, D_INNER), F32), SDS((SEQ, SSM_GROUPS * n), F32),
                   SDS((SEQ, SSM_GROUPS * n), F32), SDS((SSM_GROUPS, SEQ, nh), F32), SDS((SSM_GROUPS, nh, SEQ), F32),
                   SDS((SSM_GROUPS, 1, nh), F32), SDS((SSM_GROUPS, 1, nh), F32), SDS((SSM_GROUPS, 1, nh), F32),
                   SDS((1, D_INNER), F32)],
        scratch_shapes=[pltpu.VMEM((gw, n), F32)],
        input_output_aliases={12: 0}, compiler_params=_cparams(("parallel", "arbitrary")), name=name,
    )(xbc, proj, xbc, xbc, dt_c, dt_r, bias, alog, dsk, gn, hprev, dcat, dproj)


def _sum_contributions(chip, parts, landed, *, name):
    _, r, c = parts.shape
    tr = _pick(r, (256, 384, 128))

    def body(chip_ref, own_ref, landed_ref, o_ref):
        del chip_ref
        acc = own_ref[...].astype(F32)
        for s in range(landed_ref.shape[0]):
            acc = acc + landed_ref[s].astype(F32)
        o_ref[...] = acc

    grid_spec = pltpu.PrefetchScalarGridSpec(
        num_scalar_prefetch=1, grid=(r // tr,),
        in_specs=[pl.BlockSpec((None, tr, c), lambda i, chip_ref: (chip_ref[0], i, 0)),
                  pl.BlockSpec((landed.shape[0], tr, c), lambda i, chip_ref: (0, i, 0))],
        out_specs=pl.BlockSpec((tr, c), lambda i, chip_ref: (i, 0)))
    return pl.pallas_call(body, grid_spec=grid_spec, out_shape=SDS((r, c), F32),
                          compiler_params=_cparams(("parallel",)), name=name)(chip, parts, landed)


def _adamw(w, g, m, v, *, name):
    r, c = w.shape
    tr = r if r <= 256 else _pick(r, (256, 128, 8))
    spec = pl.BlockSpec((tr, c), lambda i: (i, 0))

    def body(w_ref, g_ref, m_ref, v_ref, d_ref, mo_ref, vo_ref):
        g = g_ref[...]
        m_new = ADAM_B1 * m_ref[...] + (1.0 - ADAM_B1) * g
        v_new = ADAM_B2 * v_ref[...] + (1.0 - ADAM_B2) * (g * g)
        m_hat = m_new / (1.0 - ADAM_B1 ** ADAM_STEP)
        v_hat = v_new / (1.0 - ADAM_B2 ** ADAM_STEP)
        d_ref[...] = -ADAM_LR * (m_hat / (jnp.sqrt(v_hat) + ADAM_EPS) + ADAM_WD * w_ref[...])
        mo_ref[...] = m_new
        vo_ref[...] = v_new

    return pl.pallas_call(body, grid=(r // tr,), in_specs=[spec] * 4, out_specs=[spec] * 3,
                          out_shape=[SDS((r, c), F32)] * 3, compiler_params=_cparams(("parallel",)), name=name)(w, g, m, v)


ANY = pl.BlockSpec(memory_space=pl.ANY)


def _place():
    x, y, c = lax.axis_index("x"), lax.axis_index("y"), lax.axis_index("c")
    chips = [(1 - x, y), (x, 1 - y), (1 - x, 1 - y)]
    return x, y, c, chips


def _remote(src, dst, send_sem, recv_sem, to):
    return pltpu.make_async_remote_copy(src_ref=src, dst_ref=dst, send_sem=send_sem, recv_sem=recv_sem,
                                        device_id=to, device_id_type=MESH)


STREAM_ROWS = 128


def _stream_rows(i):
    return pl.ds(pl.multiple_of(i * STREAM_ROWS, STREAM_ROWS), STREAM_ROWS)


def _channel_scratch(width, dtype, rows=STREAM_ROWS):
    buf = (2, rows, width)
    return [pltpu.VMEM(buf, dtype), pltpu.VMEM(buf, dtype), *([pltpu.SemaphoreType.DMA((2,))] * 5),
            pltpu.SemaphoreType.REGULAR((2,))]


CHANNEL_REFS = 8


def _copy_blocks(srcs, dsts, ch):
    sbuf, _, ld, _, _, st, _, _ = ch
    n = len(srcs)
    load = lambda i: pltpu.make_async_copy(srcs[i], sbuf.at[i % 2], ld.at[i % 2])
    store = lambda i: pltpu.make_async_copy(sbuf.at[i % 2], dsts[i], st.at[i % 2])
    load(0).start()
    for i in range(n):
        if i + 1 < n:
            if i >= 1:
                store(i - 1).wait()
            load(i + 1).start()
        load(i).wait()
        store(i).start()
    for i in range(max(0, n - 2), n):
        store(i).wait()


def _exchange_blocks(srcs, dsts, keeps, ch, sibling):
    sbuf, rbuf, ld, snd, rcv, st, kp, credit = ch
    n = len(srcs)
    load = lambda i: pltpu.make_async_copy(srcs[i], sbuf.at[i % 2], ld.at[i % 2])
    push = lambda i: _remote(sbuf.at[i % 2], rbuf.at[i % 2], snd.at[i % 2], rcv.at[i % 2], sibling)
    store = lambda i: pltpu.make_async_copy(rbuf.at[i % 2], dsts[i], st.at[i % 2])
    save = lambda i: pltpu.make_async_copy(sbuf.at[i % 2], keeps[i], kp.at[i % 2])

    def send(i):
        load(i).wait()
        pl.semaphore_wait(credit.at[i % 2], 1)
        push(i).start()
        if keeps[i] is not None:
            save(i).start()

    for i in range(min(2, n)):
        pl.semaphore_signal(credit.at[i], 1, device_id=sibling, device_id_type=MESH)
        load(i).start()
    send(0)
    for i in range(n):
        if i >= 1:
            store(i - 1).wait()
            if i + 1 < n:
                pl.semaphore_signal(credit.at[(i + 1) % 2], 1, device_id=sibling, device_id_type=MESH)
        if i + 1 < n:
            send(i + 1)
        push(i).wait_recv()
        store(i).start()
        push(i).wait_send()
        if keeps[i] is not None:
            save(i).wait()
        if i + 2 < n:
            load(i + 2).start()
    store(n - 1).wait()


def _all_gather_shards(shards, small, *, name):
    n = len(shards)

    def body(*refs):
        ins, outs = refs[:n + 1], refs[n + 1:2 * n + 2]
        scr = refs[2 * n + 2:]
        chans = [scr[CHANNEL_REFS * t:CHANNEL_REFS * (t + 1)] for t in range(n)]
        send_sems, recv_sems, small_sems = scr[CHANNEL_REFS * n:]
        x, y, c, _ = _place()
        me = 2 * x + y
        sibling = (x, y, 1 - c)
        near = (lax.rem(x + 1 - c, 2), lax.rem(y + c, 2))
        far = (lax.rem(x + c, 2), lax.rem(y + 1 - c, 2))
        k_near, k_far, k_diag = 2 * near[0] + near[1], 2 * far[0] + far[1], 3 - me
        targets = ((*near, c), (*far, c), (*far, c))
        arrives = (k_near, k_far, k_diag)
        streams_in = (k_far, k_near, k_diag)

        def ici(t, j, src, blk):
            return _remote(src, outs[t].at[blk, c], send_sems.at[3 * t + j], recv_sems.at[3 * t + j], targets[j])

        first = [ici(t, j, ins[t].at[c], me) for t in range(n + 1) for j in range(2)]
        for cp in first:
            cp.start()
        small_local = pltpu.make_async_copy(ins[n], outs[n].at[me], small_sems.at[6])
        small_local.start()
        for t in range(n):
            _copy_blocks([ins[t].at[h] for h in range(2)], [outs[t].at[me, h] for h in range(2)], chans[t])
        passed = []
        for j in range(3):
            for t in range(n + 1):
                landed = outs[t].at[arrives[j], c]
                ici(t, j, landed, arrives[j]).wait_recv()
                if j == 0:
                    fwd = ici(t, 2, landed, k_near)
                    fwd.start()
                    passed.append(fwd)
                if t < n:
                    _exchange_blocks([landed], [outs[t].at[streams_in[j], 1 - c]], [None], chans[t], sibling)
                else:
                    fwd = _remote(landed, landed, small_sems.at[j], small_sems.at[3 + j], sibling)
                    fwd.start()
                    passed.append(fwd)
        for j in range(3):
            got = outs[n].at[streams_in[j], 1 - c]
            _remote(got, got, small_sems.at[j], small_sems.at[3 + j], sibling).wait_recv()
        for cp in first + passed:
            cp.wait_send()
        small_local.wait()

    scratch = []
    for s in shards:
        scratch += _channel_scratch(s.shape[2], s.dtype, rows=s.shape[1])
    return pl.pallas_call(
        body, in_specs=[ANY] * (n + 1), out_specs=[ANY] * (n + 1),
        out_shape=[SDS((N_CHIPS, *s.shape), s.dtype) for s in (*shards, small)],
        scratch_shapes=[*scratch, pltpu.SemaphoreType.DMA((3 * n + 3,)), pltpu.SemaphoreType.DMA((3 * n + 3,)),
                        pltpu.SemaphoreType.DMA((7,))],
        compiler_params=pltpu.CompilerParams(vmem_limit_bytes=VMEM_LIMIT), name=name)(*shards, small)


def _pair_reduce(stacks, *, name):
    n = len(stacks)
    per = 11

    def body(*refs):
        ins, outs, scr = refs[:n], refs[n:2 * n], refs[2 * n:]
        x, y, c, _ = _place()
        sibling = (x, y, 1 - c)
        for t in range(n):
            sraw, sbuf, rbuf, obuf, pbuf, ld_s, ld_o, snd, rcv, st, credit = scr[per * t:per * (t + 1)]
            steps = ins[t].shape[1] // STREAM_ROWS
            src, own, out = ins[t].at[1 - c], ins[t].at[c], outs[t]

            def load_s(i, slot, src=src, sraw=sraw, ld_s=ld_s):
                return pltpu.make_async_copy(src.at[_stream_rows(i)], sraw.at[slot], ld_s.at[slot])

            def load_o(i, slot, own=own, obuf=obuf, ld_o=ld_o):
                return pltpu.make_async_copy(own.at[_stream_rows(i)], obuf.at[slot], ld_o.at[slot])

            def push(slot, sbuf=sbuf, rbuf=rbuf, snd=snd, rcv=rcv):
                return _remote(sbuf.at[slot], rbuf.at[slot], snd.at[slot], rcv.at[slot], sibling)

            def store(i, slot, pbuf=pbuf, out=out, st=st):
                return pltpu.make_async_copy(pbuf.at[slot], out.at[_stream_rows(i)], st.at[slot])

            assert steps >= 2
            for slot in range(2):
                pl.semaphore_signal(credit.at[slot], 1, device_id=sibling, device_id_type=MESH)
                load_s(slot, slot).start()
                load_o(slot, slot).start()
            load_s(0, 0).wait()
            sbuf[0] = sraw[0].astype(sbuf.dtype)
            pl.semaphore_wait(credit.at[0], 1)
            push(0).start()

            def step(i, carry, load_s=load_s, load_o=load_o, push=push, store=store, sraw=sraw, sbuf=sbuf, rbuf=rbuf,
                     obuf=obuf, pbuf=pbuf, credit=credit, steps=steps):
                slot = lax.rem(i, 2)
                nxt = 1 - slot

                @pl.when(i + 1 < steps)
                def _():
                    load_s(i + 1, nxt).wait()
                    sbuf[nxt] = sraw[nxt].astype(sbuf.dtype)
                    pl.semaphore_wait(credit.at[nxt], 1)
                    push(nxt).start()

                load_o(i, slot).wait()
                push(slot).wait_recv()

                @pl.when(i >= 2)
                def _():
                    store(i, slot).wait()

                pbuf[slot] = (obuf[slot] + rbuf[slot].astype(F32)).astype(pbuf.dtype)
                store(i, slot).start()
                push(slot).wait_send()

                @pl.when(i + 2 < steps)
                def _():
                    load_s(i + 2, slot).start()
                    load_o(i + 2, slot).start()
                    pl.semaphore_signal(credit.at[slot], 1, device_id=sibling, device_id_type=MESH)
                return carry

            lax.fori_loop(0, steps, step, 0)
            for slot in range(2):
                store(0, slot).wait()

    scratch = []
    for s in stacks:
        buf = (2, STREAM_ROWS, s.shape[2])
        scratch += [pltpu.VMEM(buf, F32), pltpu.VMEM(buf, BF16), pltpu.VMEM(buf, BF16), pltpu.VMEM(buf, F32),
                    pltpu.VMEM(buf, BF16), *([pltpu.SemaphoreType.DMA((2,))] * 5), pltpu.SemaphoreType.REGULAR((2,))]
    return pl.pallas_call(
        body, in_specs=[ANY] * n, out_specs=[ANY] * n, out_shape=[SDS(s.shape[1:], BF16) for s in stacks],
        scratch_shapes=scratch, compiler_params=pltpu.CompilerParams(vmem_limit_bytes=VMEM_LIMIT), name=name)(*stacks)


def _chip_scatter(parts, *, name):
    n = len(parts)

    def body(*refs):
        ins, outs = refs[:n], refs[n:2 * n]
        send_sems, recv_sems = refs[2 * n:]
        _, _, c, chips = _place()
        copies = [_remote(ins[t].at[2 * cx + cy], outs[t].at[j], send_sems.at[3 * t + j], recv_sems.at[3 * t + j],
                          (cx, cy, c)) for t in range(n) for j, (cx, cy) in enumerate(chips)]
        for cp in copies:
            cp.start()
        for cp in copies:
            cp.wait_recv()
        for cp in copies:
            cp.wait_send()

    return pl.pallas_call(
        body, in_specs=[ANY] * n, out_specs=[ANY] * n, out_shape=[SDS((3, *p.shape[1:]), p.dtype) for p in parts],
        scratch_shapes=[pltpu.SemaphoreType.DMA((3 * n,)), pltpu.SemaphoreType.DMA((3 * n,))], name=name)(*parts)


HBM_SPEC = pl.BlockSpec(memory_space=pltpu.HBM)
SEM_SPEC = pl.BlockSpec(memory_space=pltpu.SEMAPHORE)
SIDE_EFFECT = pltpu.SideEffectType.DATAFLOW_SIDE_EFFECTING


def _scatter_copies(ins, lands, send_sems, recv_sems):
    _, _, c, chips = _place()
    return [_remote(ins[t].at[2 * cx + cy], lands[t].at[j], send_sems.at[3 * t + j], recv_sems.at[3 * t + j],
                    (cx, cy, c)) for t in range(len(ins)) for j, (cx, cy) in enumerate(chips)]


def _chip_scatter_start(parts, *, name):
    n = len(parts)

    def body(*refs):
        ins, lands = refs[:n], refs[n:2 * n]
        send_sems, recv_sems, token = refs[2 * n], refs[2 * n + 1], refs[-1]
        for cp in _scatter_copies(ins, lands, send_sems, recv_sems):
            cp.start()
        token[...] = jnp.zeros_like(token)

    hbm = lambda a: pltpu.with_memory_space_constraint(a, pltpu.HBM)
    lands = [hbm(lax.empty((3, *p.shape[1:]), p.dtype)) for p in parts]
    thru = [pltpu.HBM(a.shape, a.dtype) for a in (*parts, *lands)]
    outs = pl.pallas_call(
        body, name=name,
        out_shape=(pltpu.SemaphoreType.DMA((3 * n,)), pltpu.SemaphoreType.DMA((3 * n,)), *thru, SDS((8, 128), F32)),
        in_specs=[HBM_SPEC] * (2 * n),
        out_specs=(SEM_SPEC, SEM_SPEC, *([HBM_SPEC] * (2 * n)), pl.BlockSpec(memory_space=pltpu.VMEM)),
        input_output_aliases={i: 2 + i for i in range(2 * n)},
        compiler_params=pltpu.CompilerParams(has_side_effects=SIDE_EFFECT),
    )(*[hbm(p) for p in parts], *lands)
    return outs[0], outs[1], outs[2:2 + n], outs[2 + n:2 + 2 * n], outs[-1]


def _chip_scatter_wait(send_sems, recv_sems, parts, lands, after, *, name):
    n = len(parts)

    def body(*refs):
        ins, lands_in = refs[:n], refs[n:2 * n]
        for cp in _scatter_copies(ins, lands_in, refs[2 * n], refs[2 * n + 1]):
            cp.wait_send()
            cp.wait_recv()

    outs = pl.pallas_call(
        body, name=name, out_shape=[pltpu.HBM(a.shape, a.dtype) for a in (*parts, *lands)],
        in_specs=[*([HBM_SPEC] * (2 * n)), SEM_SPEC, SEM_SPEC, *([ANY] * len(after))],
        out_specs=[HBM_SPEC] * (2 * n), input_output_aliases={i: i for i in range(2 * n)},
        compiler_params=pltpu.CompilerParams(has_side_effects=SIDE_EFFECT),
    )(*parts, *lands, send_sems, recv_sems, *after)
    return outs[:n], outs[n:]


def _gather_copies(shards, zones, send_sems, recv_sems):
    x, y, c, chips = _place()
    return [_remote(shards[t].at[c], zones[t].at[2 * x + y, c], send_sems.at[3 * t + j], recv_sems.at[3 * t + j],
                    (cx, cy, c)) for t in range(len(shards)) for j, (cx, cy) in enumerate(chips)]


def _gather_start(shards, after, *, name):
    n = len(shards)

    def body(*refs):
        ins, zones = refs[:n], refs[n:2 * n]
        send_sems, recv_sems, token = refs[2 * n + len(after)], refs[2 * n + len(after) + 1], refs[-1]
        for cp in _gather_copies(ins, zones, send_sems, recv_sems):
            cp.start()
        token[...] = jnp.zeros_like(token)

    hbm = lambda a: pltpu.with_memory_space_constraint(a, pltpu.HBM)
    zones = [hbm(lax.empty((N_CHIPS, *s.shape), s.dtype)) for s in shards]
    thru = [pltpu.HBM(a.shape, a.dtype) for a in (*shards, *zones)]
    outs = pl.pallas_call(
        body, name=name,
        out_shape=(pltpu.SemaphoreType.DMA((3 * n,)), pltpu.SemaphoreType.DMA((3 * n,)), *thru, SDS((8, 128), F32)),
        in_specs=[*([HBM_SPEC] * (2 * n)), *([ANY] * len(after))],
        out_specs=(SEM_SPEC, SEM_SPEC, *([HBM_SPEC] * (2 * n)), pl.BlockSpec(memory_space=pltpu.VMEM)),
        input_output_aliases={i: 2 + i for i in range(2 * n)},
        compiler_params=pltpu.CompilerParams(has_side_effects=SIDE_EFFECT),
    )(*[hbm(s) for s in shards], *zones, *after)
    return outs[0], outs[1], outs[2:2 + n], outs[2 + n:2 + 2 * n], outs[-1]


def _gather_wait(send_sems, recv_sems, shards, zones, after, *, name):
    n = len(shards)

    def body(*refs):
        for cp in _gather_copies(refs[:n], refs[n:2 * n], refs[2 * n], refs[2 * n + 1]):
            cp.wait_send()
            cp.wait_recv()

    outs = pl.pallas_call(
        body, name=name, out_shape=[pltpu.HBM(a.shape, a.dtype) for a in (*shards, *zones)],
        in_specs=[*([HBM_SPEC] * (2 * n)), SEM_SPEC, SEM_SPEC, *([ANY] * len(after))],
        out_specs=[HBM_SPEC] * (2 * n), input_output_aliases={i: i for i in range(2 * n)},
        compiler_params=pltpu.CompilerParams(has_side_effects=SIDE_EFFECT),
    )(*shards, *zones, send_sems, recv_sems, *after)
    return outs[:n], outs[n:]


def _gather_finish(shards, zones, *, name):
    n = len(shards)

    def body(*refs):
        ins, zones_in, outs, scr = refs[:n], refs[n:2 * n], refs[2 * n:3 * n], refs[3 * n:]
        x, y, c, chips = _place()
        me = 2 * x + y
        sibling = (x, y, 1 - c)
        others = [2 * cx + cy for cx, cy in chips]
        for t in range(n):
            chan = scr[CHANNEL_REFS * t:CHANNEL_REFS * (t + 1)]
            _copy_blocks([ins[t].at[h] for h in range(2)], [outs[t].at[me, h] for h in range(2)], chan)
            _exchange_blocks([zones_in[t].at[k, c] for k in others], [outs[t].at[k, 1 - c] for k in others],
                             [None] * len(others), chan, sibling)

    scratch = []
    for s in shards:
        scratch += _channel_scratch(s.shape[2], s.dtype, rows=s.shape[1])
    return pl.pallas_call(
        body, in_specs=[ANY] * (2 * n), out_specs=[ANY] * n, out_shape=[SDS(z.shape, z.dtype) for z in zones],
        input_output_aliases={n + t: t for t in range(n)}, scratch_shapes=scratch,
        compiler_params=pltpu.CompilerParams(vmem_limit_bytes=VMEM_LIMIT), name=name)(*shards, *zones)


def _pair_share(groups, *, name):
    finals = [f for grp in groups for f in grp]
    n, n_out = len(finals), len(groups)

    def body(*refs):
        ins, outs, scr = refs[:n], refs[n:n + n_out], refs[n + n_out:]
        x, y, c, _ = _place()
        sibling = (x, y, 1 - c)
        t = 0
        for o, grp in enumerate(groups):
            rows = grp[0].shape[0] // 2
            blocks = [(layer, pl.ds(b * rows, rows)) for layer in range(len(grp)) for b in range(2)]
            _exchange_blocks([ins[t + layer].at[rs] for layer, rs in blocks],
                             [outs[o].at[layer, 1 - c, rs] for layer, rs in blocks],
                             [outs[o].at[layer, c, rs] for layer, rs in blocks],
                             scr[CHANNEL_REFS * o:CHANNEL_REFS * (o + 1)], sibling)
            t += len(grp)

    scratch = []
    for grp in groups:
        scratch += _channel_scratch(grp[0].shape[1], grp[0].dtype, rows=grp[0].shape[0] // 2)
    return pl.pallas_call(
        body, in_specs=[ANY] * n, out_specs=[ANY] * n_out,
        out_shape=[SDS((len(grp), 2, *grp[0].shape), grp[0].dtype) for grp in groups],
        scratch_shapes=scratch, compiler_params=pltpu.CompilerParams(vmem_limit_bytes=VMEM_LIMIT), name=name)(*finals)


def _all_reduce_small(v, *, name):
    rows, lanes = v.shape
    n_dev = 8

    def body(v_ref, o_ref, all_ref, send_sems, recv_sems, local_sem):
        x, y, c, chips = _place()
        me, sibling = (x, y, c), (x, y, 1 - c)

        def block(px, py, pc):
            return all_ref.at[4 * px + 2 * py + pc]

        def copy(k, blk, to, src=None):
            return _remote(block(*blk) if src is None else src, block(*blk), send_sems.at[k], recv_sems.at[k], to)

        mine = pltpu.make_async_copy(v_ref, block(*me), local_sem)
        mine.start()
        first = [copy(0, me, sibling, src=v_ref)]
        first += [copy(1 + j, me, (*chip, c), src=v_ref) for j, chip in enumerate(chips)]
        for cp in first:
            cp.start()
        passed = [copy(4 + j, (*chip, c), sibling) for j, chip in enumerate(chips)]
        for j, chip in enumerate(chips):
            copy(1 + j, (*chip, c), me).wait_recv()
            passed[j].start()
        copy(0, sibling, me).wait_recv()
        for j, chip in enumerate(chips):
            copy(4 + j, (*chip, 1 - c), me).wait_recv()
        for cp in first + passed:
            cp.wait_send()
        mine.wait()
        acc = all_ref[0]
        for k in range(1, n_dev):
            acc = acc + all_ref[k]
        o_ref[...] = acc

    vmem = pl.BlockSpec(memory_space=pltpu.VMEM)
    return pl.pallas_call(
        body, in_specs=[vmem], out_specs=vmem, out_shape=SDS((rows, lanes), F32),
        scratch_shapes=[pltpu.VMEM((n_dev, rows, lanes), F32), pltpu.SemaphoreType.DMA((7,)),
                        pltpu.SemaphoreType.DMA((7,)), pltpu.SemaphoreType.DMA],
        compiler_params=pltpu.CompilerParams(vmem_limit_bytes=VMEM_LIMIT), name=name)(v)


def _relu2_epilogue(acc):
    return acc, jnp.square(jnp.maximum(acc, 0.0))


def _res_epilogue(acc, res):
    return (acc + res,)


def _drelu2_epilogue(acc, pre):
    return (acc * (2.0 * jnp.maximum(pre.astype(F32), 0.0)),)


def _ffn_fwd(h, g, w1, w2, tag):
    f = _rms_fwd(h, g, name=f"ffn_norm_{tag}")
    pre, act = _mm_nn(f, w1, name=f"ffn1_{tag}", epilogue=_relu2_epilogue, n_out_dtypes=(BF16, BF16))
    h_out = _mm_nn(act, w2, name=f"ffn2_{tag}", extras=(h,), epilogue=_res_epilogue)
    return h_out, (f, pre, act)


def _ffn_bwd(dh, h, g, w1, w2, saved, layer, after=()):
    f, pre, act = saved
    dpre = _mm_nt(dh, w2, name=f"ffn2_dx_{layer}", out_dtype=BF16, extras=(pre,), epilogue=_drelu2_epilogue,
                  after=after)
    dw2 = _mm_tn_stacked(act, dh, name=f"ffn2_dw_{layer}", col_slots=False)
    df = _mm_nt(dpre, w1, name=f"ffn1_dx_{layer}")
    dw1 = _mm_tn_stacked(f, dpre, name=f"ffn1_dw_{layer}", col_slots=True)
    dh, dg = _rms_bwd(h, g, df, dh, name=f"ffn_norm_bwd_{layer}")
    return dh, dg, dw1, dw2


def _kv_fwd(mem, g, w_kv, tag):
    m = _rms_fwd(mem, g, name=f"mem_norm_{tag}")
    return m, _mm_nn(m, w_kv, name=f"kv_{tag}")


def _kv_bwd(mem, g, w_kv, m, dk, dv, layer):
    dkv = jnp.concatenate([dk, dv], axis=1)
    dw = _mm_tn_stacked(m, dkv, name=f"kv_dw_{layer}", col_slots=True)
    dm = _mm_nt(dkv, w_kv, name=f"kv_dx_{layer}")
    _, dg = _rms_bwd(mem, g, dm, dm, name=f"mem_norm_bwd_{layer}")
    return dw, dg


def _local_step(x, mem, target, p, after_layer1=None, after_ffn0=None):
    row = lambda v: v.reshape(1, -1)
    g = {}

    h0 = x
    a0 = _rms_fwd(h0, row(p["norm_mix"][0]), name="mix_norm_0")
    proj_a = _mm_nn(a0, p["a_in"], name="a_in", after=p.get("after_start", ()))
    m0, kv0 = _kv_fwd(mem, row(p["mem_norm"][0]), p["w_kv"][0], "0")
    cat0 = _attn_fwd(proj_a, 2 * D_INNER, kv0, name="attn_0")
    bs_col = p["a_bs"].reshape(A_GROUPS, CHUNK, 1)
    cat0 = _gate_fwd(proj_a, p["a_ln_g"], p["a_ln_b"], p["a_ws"], bs_col, cat0, name="gate")
    h1 = _mm_nn(cat0, p["w_out"][0], name="out_0", extras=(h0,), epilogue=_res_epilogue)
    h2, ffn0 = _ffn_fwd(h1, row(p["norm_ffn"][0]), p["w_ffn1"][0], p["w_ffn2"][0], "0")

    if "layer1_mixer" in p:
        w_kv1, w_out1, b_in = p["layer1_mixer"](h2)
    else:
        w_kv1, w_out1, b_in = p["w_kv"][1], p["w_out"][1], p["b_in"]
    a1 = _rms_fwd(h2, row(p["norm_mix"][1]), name="mix_norm_1")
    proj_b = _mm_nn(a1, b_in, name="b_in")
    m1, kv1 = _kv_fwd(mem, row(p["mem_norm"][1]), w_kv1, "1")
    cat1 = _attn_fwd(proj_b, B_Q_OFF, kv1, name="attn_1")
    xbc = _conv_fwd(proj_b, p["b_conv_w"], p["b_conv_b"], name="conv")
    dt_raw = proj_b[:, B_DT_OFF:B_DT_OFF + SSM_HEADS].reshape(SEQ, SSM_GROUPS, SSM_HPG)
    dt_c = jnp.transpose(dt_raw, (1, 0, 2))
    dt_r = jnp.transpose(dt_raw, (1, 2, 0))
    per_head = lambda v: v.reshape(SSM_GROUPS, 1, SSM_HPG)
    ssd_par = (per_head(p["b_dt_bias"]), per_head(p["b_a_log"]), per_head(p["b_d"]), p["b_gnorm"])
    cat1, hprev = _ssd_fwd(xbc, proj_b, dt_c, dt_r, *ssd_par, cat1, name="ssd")
    h3 = _mm_nn(cat1, w_out1, name="out_1", extras=(h2,), epilogue=_res_epilogue)
    w_ffn1_1, w_ffn2_1 = p["layer1_ffn"](h3) if "layer1_ffn" in p else (p["w_ffn1"][1], p["w_ffn2"][1])
    h4, ffn1 = _ffn_fwd(h3, row(p["norm_ffn"][1]), w_ffn1_1, w_ffn2_1, "1")

    loss, dh, g["final_norm"] = _loss_head(h4, row(p["final_norm"]), target, name="loss_head")

    dh, dnf1, dw1_1, dw2_1 = _ffn_bwd(dh, h3, row(p["norm_ffn"][1]), w_ffn1_1, w_ffn2_1, ffn1, 1)
    dcat1 = _mm_nt(dh, w_out1, name="out_dx_1")
    dwo_1 = _mm_tn_stacked(cat1, dh, name="out_dw_1", col_slots=False)
    dproj_b, dk1, dv1 = _attn_bwd(proj_b, B_Q_OFF, kv1, dcat1, B_IN_PAD, B_Q_OFF, name="attn_bwd_1")
    (dproj_b, dxs, dbm, dcm, ddt_c, ddt_r, g["b_dt_bias"], g["b_a_log"], g["b_d"], g["b_gnorm"]) = _ssd_bwd(
        xbc, proj_b, dt_c, dt_r, *ssd_par, hprev, dcat1, dproj_b, name="ssd_bwd")
    dproj_b, g["b_conv_w"], g["b_conv_b"] = _conv_bwd(proj_b, p["b_conv_w"], p["b_conv_b"], dxs, dbm, dcm, dproj_b,
                                                      name="conv_bwd")
    ddt = jnp.transpose(ddt_c, (1, 0, 2)) + jnp.transpose(ddt_r, (2, 0, 1))
    ddt = jnp.pad(ddt.reshape(SEQ, SSM_HEADS), ((0, 0), (0, B_IN_PAD - B_DT_OFF - SSM_HEADS))).astype(BF16)
    dproj_b = lax.dynamic_update_slice(dproj_b, ddt, (0, B_DT_OFF))
    dwkv_1, dmn1 = _kv_bwd(mem, row(p["mem_norm"][1]), w_kv1, m1, dk1, dv1, 1)
    dwb = _b_in_grad_slots(_mm_tn(a1, dproj_b, name="b_in_dw"))
    da1 = _mm_nt(dproj_b, b_in, name="b_in_dx")
    dh, dnm1 = _rms_bwd(h2, row(p["norm_mix"][1]), da1, dh, name="mix_norm_bwd_1")
    layer1 = dict(w_kv=dwkv_1, w_out=dwo_1, w_ffn1=dw1_1, w_ffn2=dw2_1, b_in=dwb)
    token = () if after_layer1 is None else (after_layer1(layer1),)

    dh, dnf0, dw1_0, dw2_0 = _ffn_bwd(dh, h1, row(p["norm_ffn"][0]), p["w_ffn1"][0], p["w_ffn2"][0], ffn0, 0,
                                      after=token)
    ffn0_grads = dict(w_ffn1=dw1_0, w_ffn2=dw2_0)
    token = () if after_ffn0 is None else (after_ffn0(ffn0_grads),)
    dcat0 = _mm_nt(dh, p["w_out"][0], name="out_dx_0", after=token)
    dwo_0 = _mm_tn_stacked(cat0, dh, name="out_dw_0", col_slots=False)
    dproj_a, dk0, dv0 = _attn_bwd(proj_a, 2 * D_INNER, kv0, dcat0, A_IN, 2 * D_INNER, name="attn_bwd_0")
    dproj_a, g["a_ln_g"], g["a_ln_b"], g["a_ws"], dbs_col = _gate_bwd(
        proj_a, p["a_ln_g"], p["a_ln_b"], p["a_ws"], bs_col, dcat0, dproj_a, name="gate_bwd")
    g["a_bs"] = dbs_col.reshape(A_GROUPS, CHUNK)
    dwkv_0, dmn0 = _kv_bwd(mem, row(p["mem_norm"][0]), p["w_kv"][0], m0, dk0, dv0, 0)
    dwa = _mm_tn_stacked(a0, dproj_a, name="a_in_dw", col_slots=True)
    da0 = _mm_nt(dproj_a, p["a_in"], name="a_in_dx")
    dx, dnm0 = _rms_bwd(h0, row(p["norm_mix"][0]), da0, dh, name="mix_norm_bwd_0")

    g["norm_mix"] = jnp.concatenate([dnm0, dnm1], axis=0)
    g["norm_ffn"] = jnp.concatenate([dnf0, dnf1], axis=0)
    g["mem_norm"] = jnp.concatenate([dmn0, dmn1], axis=0)
    layer0 = dict(w_kv=dwkv_0, w_out=dwo_0, w_ffn1=dw1_0, w_ffn2=dw2_0, a_in=dwa)
    return loss, dx, g, layer0, layer1


def _b_in_full(gathered):
    n = B_IN // N_CHIPS
    dt0 = D_INNER + CONV_DIM - (N_CHIPS - 1) * n
    last = gathered[N_CHIPS - 1]
    return jnp.concatenate([*[gathered[k] for k in range(N_CHIPS - 1)], last[:, :dt0], last[:, dt0 + SSM_HEADS:],
                            last[:, dt0:dt0 + SSM_HEADS], jnp.zeros((D_MODEL, B_IN_PAD - B_IN), last.dtype)], axis=1)


def _b_in_grad_slots(d):
    n = B_IN // N_CHIPS
    dt0 = D_INNER + CONV_DIM
    last = jnp.concatenate([d[:, (N_CHIPS - 1) * n:dt0], d[:, B_DT_OFF:B_DT_OFF + SSM_HEADS], d[:, dt0:B_DT_OFF]], axis=1)
    slots = [*[d[:, k * n:(k + 1) * n] for k in range(N_CHIPS - 1)], last]
    half = D_MODEL // 2
    return jnp.stack([jnp.stack([s[h * half:(h + 1) * half] for s in slots]) for h in range(2)])


LARGE = ("w_kv", "w_out", "w_ffn1", "w_ffn2", "a_in", "b_in")
SMALL_REPL = ("norm_mix", "norm_ffn", "mem_norm", "a_ln_g", "a_ln_b", "a_ws", "a_bs", "b_dt_bias", "b_a_log", "b_d",
              "final_norm")
SMALL_SHARD = ("b_conv_w", "b_conv_b", "b_gnorm")
WEIGHTS = ("norm_mix", "norm_ffn", "mem_norm", "w_kv", "w_out", "w_ffn1", "w_ffn2", "a_in", "a_ln_g", "a_ln_b", "a_ws",
           "a_bs", "b_in", "b_conv_w", "b_conv_b", "b_dt_bias", "b_a_log", "b_d", "b_gnorm", "final_norm")
CONV_SHARD = CONV_DIM // N_CHIPS
GN_SHARD = D_INNER // N_CHIPS


LAYERED = ("w_kv", "w_out", "w_ffn1", "w_ffn2")
LAYER_TENSORS = (("w_kv", "w_out", "w_ffn1", "w_ffn2", "a_in"), ("w_kv", "w_out", "w_ffn1", "w_ffn2", "b_in"))


def _gather_weights(w):
    halves = lambda k, layer: (w[k][layer] if k in LAYERED else w[k][0]).reshape(2, -1, w[k].shape[-1]).astype(BF16)
    small = jnp.zeros((2, CONV_K, CONV_SHARD), F32)
    small = small.at[0].set(w["b_conv_w"][0])
    small = small.at[1, 0].set(w["b_conv_b"][0])
    small = small.at[1, 1, :GN_SHARD].set(w["b_gnorm"][0])
    gathered = _all_gather_shards([halves(k, 0) for k in LAYER_TENSORS[0]], small, name="gather_weights_0")
    got = dict(zip(LAYER_TENSORS[0], gathered))
    slots = lambda a: a.reshape(N_CHIPS, -1, a.shape[-1])
    rows = lambda a: a.reshape(-1, a.shape[-1])
    p = dict(w_kv=[slots(got["w_kv"])], w_out=[rows(got["w_out"])], w_ffn1=[slots(got["w_ffn1"])],
             w_ffn2=[rows(got["w_ffn2"])], a_in=slots(got["a_in"]))
    sm = gathered[-1]
    p["b_conv_w"] = jnp.transpose(sm[:, 0], (1, 0, 2)).reshape(CONV_K, CONV_DIM)
    p["b_conv_b"] = sm[:, 1, 0].reshape(1, CONV_DIM)
    p["b_gnorm"] = sm[:, 1, 1, :GN_SHARD].reshape(1, D_INNER)

    after, started = (gathered[0],), {}
    for tag, names in (("mixer", ("w_kv", "w_out", "b_in")), ("ffn", ("w_ffn1", "w_ffn2"))):
        started[tag] = _gather_start([halves(k, 1) for k in names], after, name=f"gather_start_1_{tag}")
        after = (started[tag][-1],)
    p["after_start"] = after

    def finish(tag, first):
        send_sems, recv_sems, shards, zones, _ = started[tag]
        shards, zones = _gather_wait(send_sems, recv_sems, shards, zones, (first,), name=f"gather_wait_1_{tag}")
        return _gather_finish(shards, zones, name=f"gather_finish_1_{tag}")

    def layer1_mixer(first):
        kv, wo, b_in = finish("mixer", first)
        return slots(kv), rows(wo), _b_in_full(slots(b_in))

    def layer1_ffn(first):
        w1, w2 = finish("ffn", first)
        return slots(w1), rows(w2)

    p.update(layer1_mixer=layer1_mixer, layer1_ffn=layer1_ffn)
    return p


def _pair_parts(grads, tag):
    stacks = [g.reshape(2, -1, g.shape[-1]) for g in grads.values()]
    parts = _pair_reduce(stacks, name=f"grads_pair_reduce_{tag}")
    return [t.reshape(N_CHIPS, -1, t.shape[-1]) for t in parts]


def _chip_sums(chip, names, parts, landed, tag):
    return {k: _sum_contributions(chip, t, u, name=f"grads_chip_sum_{k}_{tag}")
            for k, t, u in zip(names, parts, landed)}


def _small_layout(shapes):
    offs, o = {}, 0
    for k in (*SMALL_REPL, *SMALL_SHARD):
        size = math.prod(shapes[k])
        offs[k] = (o, size)
        o += size
    rows = -(-o // (8 * 128)) * 8
    return offs, rows


def _reduce_small(g, full_shapes):
    offs, rows = _small_layout(full_shapes)
    flat = jnp.concatenate([g[k].reshape(-1) for k in (*SMALL_REPL, *SMALL_SHARD)])
    flat = jnp.pad(flat, (0, rows * 128 - flat.shape[0])).reshape(rows, 128)
    total = _all_reduce_small(flat, name="grads_small_all_reduce").reshape(-1)
    return {k: total[o:o + n].reshape(full_shapes[k]) for k, (o, n) in offs.items()}


def kernel(x, mem, norm_mix, norm_ffn, mem_norm, w_kv, w_out, w_ffn1, w_ffn2, a_in, a_ln_g, a_ln_b, a_ws, a_bs, b_in, b_conv_w, b_conv_b, b_dt_bias, b_a_log, b_d, b_gnorm, final_norm, loss_target, m_norm_mix, m_norm_ffn, m_mem_norm, m_w_kv, m_w_out, m_w_ffn1, m_w_ffn2, m_a_in, m_a_ln_g, m_a_ln_b, m_a_ws, m_a_bs, m_b_in, m_b_conv_w, m_b_conv_b, m_b_dt_bias, m_b_a_log, m_b_d, m_b_gnorm, m_final_norm, v_norm_mix, v_norm_ffn, v_mem_norm, v_w_kv, v_w_out, v_w_ffn1, v_w_ffn2, v_a_in, v_a_ln_g, v_a_ln_b, v_a_ws, v_a_bs, v_b_in, v_b_conv_w, v_b_conv_b, v_b_dt_bias, v_b_a_log, v_b_d, v_b_gnorm, v_final_norm):
    w = dict(norm_mix=norm_mix, norm_ffn=norm_ffn, mem_norm=mem_norm, w_kv=w_kv, w_out=w_out, w_ffn1=w_ffn1,
             w_ffn2=w_ffn2, a_in=a_in, a_ln_g=a_ln_g, a_ln_b=a_ln_b, a_ws=a_ws, a_bs=a_bs, b_in=b_in, b_conv_w=b_conv_w,
             b_conv_b=b_conv_b, b_dt_bias=b_dt_bias, b_a_log=b_a_log, b_d=b_d, b_gnorm=b_gnorm, final_norm=final_norm)
    mom = dict(norm_mix=m_norm_mix, norm_ffn=m_norm_ffn, mem_norm=m_mem_norm, w_kv=m_w_kv, w_out=m_w_out,
               w_ffn1=m_w_ffn1, w_ffn2=m_w_ffn2, a_in=m_a_in, a_ln_g=m_a_ln_g, a_ln_b=m_a_ln_b, a_ws=m_a_ws,
               a_bs=m_a_bs, b_in=m_b_in, b_conv_w=m_b_conv_w, b_conv_b=m_b_conv_b, b_dt_bias=m_b_dt_bias,
               b_a_log=m_b_a_log, b_d=m_b_d, b_gnorm=m_b_gnorm, final_norm=m_final_norm)
    var = dict(norm_mix=v_norm_mix, norm_ffn=v_norm_ffn, mem_norm=v_mem_norm, w_kv=v_w_kv, w_out=v_w_out,
               w_ffn1=v_w_ffn1, w_ffn2=v_w_ffn2, a_in=v_a_in, a_ln_g=v_a_ln_g, a_ln_b=v_a_ln_b, a_ws=v_a_ws,
               a_bs=v_a_bs, b_in=v_b_in, b_conv_w=v_b_conv_w, b_conv_b=v_b_conv_b, b_dt_bias=v_b_dt_bias,
               b_a_log=v_b_a_log, b_d=v_b_d, b_gnorm=v_b_gnorm, final_norm=v_final_norm)

    p = _gather_weights(w)
    p.update(norm_mix=norm_mix, norm_ffn=norm_ffn, mem_norm=mem_norm, a_ln_g=a_ln_g, a_ln_b=a_ln_b, a_ws=a_ws[0],
             a_bs=a_bs[0], b_dt_bias=b_dt_bias, b_a_log=b_a_log, b_d=b_d, final_norm=final_norm)
    chip = 2 * lax.axis_index("x") + lax.axis_index("y")
    chip_arr = jnp.reshape(chip, (1,)).astype(jnp.int32)
    started = {}

    def start_scatter(tag):
        def hook(grads):
            started[tag] = (tuple(grads), _chip_scatter_start(_pair_parts(grads, tag), name=f"grads_chip_scatter_start_{tag}"))
            return started[tag][1][-1]
        return hook

    loss_part, dx, g, layer0, _ = _local_step(x[0], mem[0], loss_target[0], p, start_scatter("1"), start_scatter("0f"))
    loss = lax.psum(loss_part[0, 0], ("x", "y", "c"))

    full_shapes = {k: w[k].shape for k in SMALL_REPL}
    full_shapes.update(b_conv_w=(1, CONV_K, CONV_DIM), b_conv_b=(1, CONV_DIM), b_gnorm=(1, D_INNER))
    gs = _reduce_small(g, full_shapes)
    gs["b_conv_w"] = lax.dynamic_slice_in_dim(gs["b_conv_w"], chip * CONV_SHARD, CONV_SHARD, axis=2)
    gs["b_conv_b"] = lax.dynamic_slice_in_dim(gs["b_conv_b"], chip * CONV_SHARD, CONV_SHARD, axis=1)
    gs["b_gnorm"] = lax.dynamic_slice_in_dim(gs["b_gnorm"], chip * GN_SHARD, GN_SHARD, axis=1)
    mixer0 = {k: layer0[k] for k in ("w_kv", "w_out", "a_in")}
    parts_m = _pair_parts(mixer0, "0m")
    landed_m = _chip_scatter(parts_m, name="grads_chip_scatter_0m")
    halves = [_chip_sums(chip_arr, tuple(mixer0), parts_m, landed_m, "0m"), {}]
    for tag, layer in (("0f", 0), ("1", 1)):
        names, (send_sems, recv_sems, parts, lands, _) = started[tag]
        parts, landed = _chip_scatter_wait(send_sems, recv_sems, parts, lands, (dx,),
                                           name=f"grads_chip_scatter_wait_{tag}")
        halves[layer].update(_chip_sums(chip_arr, names, parts, landed, tag))
    groups = [[halves[layer][k] for layer in range(2) if k in halves[layer]] for k in LARGE]
    gl = dict(zip(LARGE, _pair_share(groups, name="grads_pair_share")))
    grads = {k: (gl[k].reshape(w[k].shape) if k in gl else gs[k]) for k in WEIGHTS}

    delta, new_m, new_v = {}, {}, {}
    for k in WEIGHTS:
        shape = w[k].shape
        flat = (lambda a: a.reshape(-1, shape[-1])) if len(shape) > 1 else (lambda a: a.reshape(1, -1))
        d, m_new, v_new = _adamw(flat(w[k]), flat(grads[k]), flat(mom[k]), flat(var[k]), name=f"adamw_{k}")
        delta[k], new_m[k], new_v[k] = d.reshape(shape), m_new.reshape(shape), v_new.reshape(shape)

    return (loss, dx.reshape(x.shape), *[grads[k] for k in WEIGHTS], *[delta[k] for k in WEIGHTS],
            *[new_m[k] for k in WEIGHTS], *[new_v[k] for k in WEIGHTS])
```

```python
import math

import jax
import jax.numpy as jnp
from jax import lax
from jax.experimental import pallas as pl
from jax.experimental.pallas import tpu as pltpu

F32 = jnp.float32
BF16 = jnp.bfloat16
SDS = jax.ShapeDtypeStruct

D_MODEL = 1024
SEQ = 2048
CHUNK = 128
N_MEM = 256
D_INNER = 2048
A_GROUPS = 8
A_GROUP_W = D_INNER // A_GROUPS
SSM_HEADS = 32
SSM_HEAD_DIM = 64
SSM_GROUPS = 4
SSM_HPG = 8
SSM_STATE = 128
SSM_GROUP_W = SSM_HPG * SSM_HEAD_DIM
CONV_K = 4
CONV_DIM = 3072
X_HEADS = 4
X_HEAD_DIM = 256
X_WIDTH = 1024
MIX_OUT = 3072
D_FF = 4096
A_IN = 5120
B_IN = 6176
B_IN_PAD = 6272
B_Q_OFF = 5120
B_DT_OFF = 6144
N_CHUNKS = SEQ // CHUNK
EPS = 1e-6
N_CHIPS = 4

ADAM_LR = 0.001
ADAM_B1 = 0.9
ADAM_B2 = 0.999
ADAM_EPS = 1e-08
ADAM_WD = 0.01
ADAM_STEP = 10

VMEM_LIMIT = 48 * 1024 * 1024
MESH = pl.DeviceIdType.MESH


def _cparams(sem):
    return pltpu.CompilerParams(dimension_semantics=sem, vmem_limit_bytes=VMEM_LIMIT)


def _dot(a, b, dims=(((1,), (0,)), ((), ()))):
    return lax.dot_general(a.astype(BF16), b.astype(BF16), dims, preferred_element_type=F32)


def _dot_nt(a, b):
    return _dot(a, b, (((1,), (1,)), ((), ())))


def _dot_tn(a, b):
    return _dot(a, b, (((0,), (0,)), ((), ())))


def _pick(n, cands):
    for c in cands:
        if n % c == 0:
            return c
    raise ValueError(f"no tile for {n}")


def _mm_call(a, b, *, dims, grid, a_spec, b_spec, acc_shape, out_shapes, out_specs, name,
             extras=(), extra_specs=(), epilogue=None, after=()):
    n_k = grid[2]
    n_extra = len(extras)
    n_out = len(out_shapes)
    n_in = 2 + n_extra + len(after)

    def body(*refs):
        a_ref, b_ref = refs[0], refs[1]
        extra_refs = refs[2:2 + n_extra]
        out_refs = refs[n_in:n_in + n_out]
        acc = refs[-1]
        k = pl.program_id(2)

        @pl.when(k == 0)
        def _():
            acc[...] = jnp.zeros_like(acc)

        acc[...] += _dot(a_ref[...], b_ref[...], dims)

        @pl.when(k == n_k - 1)
        def _():
            vals = (acc[...],) if epilogue is None else epilogue(acc[...], *[e[...] for e in extra_refs])
            for o_ref, v in zip(out_refs, vals):
                o_ref[...] = v.astype(o_ref.dtype)

    return pl.pallas_call(
        body, grid=grid, in_specs=[a_spec, b_spec, *extra_specs, *([ANY] * len(after))], out_specs=list(out_specs),
        out_shape=list(out_shapes), scratch_shapes=[pltpu.VMEM(acc_shape, F32)],
        compiler_params=_cparams(("parallel", "parallel", "arbitrary")), name=name,
    )(a, b, *extras, *after)


def _w_dims(w):
    if w.ndim == 2:
        return w.shape[0], w.shape[1], 1, w.shape[1]
    return w.shape[1], w.shape[0] * w.shape[2], w.shape[0], w.shape[2]


def _mm_nn(a, w, *, name, out_dtype=F32, a_cols=None, extras=(), epilogue=None, n_out_dtypes=None, after=()):
    m = a.shape[0]
    k_dim, n_dim, _, n_slot = _w_dims(w)
    a_off, a_w = (0, a.shape[1]) if a_cols is None else a_cols
    assert a_w == k_dim
    tm = _pick(m, (2048, 1024, 512, 256))
    tn = _pick(n_slot, (512, 896, 640, 256, 128))
    tk = _pick(k_dim, (1024, 768, 512, 384, 256, 128))
    assert a_off % tk == 0
    nb = n_slot // tn
    a_spec = pl.BlockSpec((tm, tk), lambda i, j, k: (i, a_off // tk + k))
    if w.ndim == 2:
        b_spec = pl.BlockSpec((tk, tn), lambda i, j, k: (k, j))
    else:
        b_spec = pl.BlockSpec((None, tk, tn), lambda i, j, k: (j // nb, k, j % nb))
    o_spec = pl.BlockSpec((tm, tn), lambda i, j, k: (i, j))
    dts = n_out_dtypes or (out_dtype,)
    outs = _mm_call(a, w, dims=(((1,), (0,)), ((), ())), grid=(m // tm, n_dim // tn, k_dim // tk),
                    a_spec=a_spec, b_spec=b_spec, acc_shape=(tm, tn),
                    out_shapes=[SDS((m, n_dim), dt) for dt in dts], out_specs=[o_spec] * len(dts), name=name,
                    extras=extras, extra_specs=[o_spec] * len(extras), epilogue=epilogue, after=after)
    return outs if n_out_dtypes else outs[0]


def _mm_nt(a, w, *, name, out_dtype=F32, extras=(), epilogue=None, after=()):
    m = a.shape[0]
    k_dim, n_dim, _, n_slot = _w_dims(w)
    assert a.shape[1] == n_dim
    tm = _pick(m, (2048, 1024, 512, 256))
    to = _pick(k_dim, (512, 384, 256, 128))
    tc = _pick(n_slot, (1024, 896, 640, 512, 256, 128))
    nb = n_slot // tc
    a_spec = pl.BlockSpec((tm, tc), lambda i, j, k: (i, k))
    if w.ndim == 2:
        b_spec = pl.BlockSpec((to, tc), lambda i, j, k: (j, k))
    else:
        b_spec = pl.BlockSpec((None, to, tc), lambda i, j, k: (k // nb, j, k % nb))
    o_spec = pl.BlockSpec((tm, to), lambda i, j, k: (i, j))
    return _mm_call(a, w, dims=(((1,), (1,)), ((), ())), grid=(m // tm, k_dim // to, n_dim // tc),
                    a_spec=a_spec, b_spec=b_spec, acc_shape=(tm, to),
                    out_shapes=[SDS((m, k_dim), out_dtype)], out_specs=[o_spec], name=name,
                    extras=extras, extra_specs=[o_spec] * len(extras), epilogue=epilogue, after=after)[0]


def _mm_tn(x, dy, *, name, x_cols=None):
    s = x.shape[0]
    x_off, k_dim = (0, x.shape[1]) if x_cols is None else x_cols
    n_dim = dy.shape[1]
    tm = _pick(k_dim, (1024, 768, 512, 384, 256, 128))
    tn = _pick(n_dim, (512, 896, 640, 256, 128))
    tk = _pick(s, (2048, 1024, 512, 256))
    assert x_off % tm == 0
    a_spec = pl.BlockSpec((tk, tm), lambda i, j, k: (k, x_off // tm + i))
    b_spec = pl.BlockSpec((tk, tn), lambda i, j, k: (k, j))
    o_spec = pl.BlockSpec((tm, tn), lambda i, j, k: (i, j))
    return _mm_call(x, dy, dims=(((0,), (0,)), ((), ())), grid=(k_dim // tm, n_dim // tn, s // tk),
                    a_spec=a_spec, b_spec=b_spec, acc_shape=(tm, tn),
                    out_shapes=[SDS((k_dim, n_dim), F32)], out_specs=[o_spec], name=name)[0]


def _mm_tn_stacked(x, dy, *, name, col_slots):
    s, k_dim = x.shape
    n_dim = dy.shape[1]
    r, c = (k_dim // 2, n_dim // N_CHIPS) if col_slots else (k_dim // N_CHIPS // 2, n_dim)
    tm = 2 * r
    tn = _pick(c, (512, 896, 640, 256, 128))
    tk = _pick(s, (2048, 1024, 512, 256))
    a_spec = pl.BlockSpec((tk, tm), lambda i, j, k: (k, i))
    b_spec = pl.BlockSpec((tk, tn), lambda i, j, k: (k, j))
    if col_slots:
        nb = c // tn
        o_spec = pl.BlockSpec((2, None, r, tn), lambda i, j, k: (0, j // nb, 0, j % nb))
    else:
        o_spec = pl.BlockSpec((2, None, r, tn), lambda i, j, k: (0, i, 0, j))
    return _mm_call(x, dy, dims=(((0,), (0,)), ((), ())), grid=(k_dim // tm, n_dim // tn, s // tk),
                    a_spec=a_spec, b_spec=b_spec, acc_shape=(tm, tn), epilogue=lambda acc: (acc.reshape(2, r, tn),),
                    out_shapes=[SDS((2, N_CHIPS, r, c), F32)], out_specs=[o_spec], name=name)[0]


def _rms(x, g):
    return x * lax.rsqrt(jnp.mean(x * x, axis=-1, keepdims=True) + EPS) * g


def _rms_fwd(h, g, *, name):
    rows, d = h.shape
    tr = _pick(rows, (512, 256))

    def body(h_ref, g_ref, o_ref):
        o_ref[...] = _rms(h_ref[...], g_ref[...]).astype(o_ref.dtype)

    return pl.pallas_call(
        body, grid=(rows // tr,),
        in_specs=[pl.BlockSpec((tr, d), lambda i: (i, 0)), pl.BlockSpec((1, d), lambda i: (0, 0))],
        out_specs=pl.BlockSpec((tr, d), lambda i: (i, 0)), out_shape=SDS((rows, d), BF16),
        compiler_params=_cparams(("parallel",)), name=name)(h, g)


def _rms_bwd(h, g, da, dres, *, name):
    rows, d = h.shape
    tr = _pick(rows, (512, 256))

    def body(h_ref, g_ref, da_ref, dres_ref, dh_ref, dg_ref):
        _, vjp = jax.vjp(_rms, h_ref[...], g_ref[...])
        dh, dg = vjp(da_ref[...].astype(F32))
        dh_ref[...] = dres_ref[...] + dh

        @pl.when(pl.program_id(0) == 0)
        def _():
            dg_ref[...] = jnp.zeros_like(dg_ref)

        dg_ref[...] += dg

    row_spec = pl.BlockSpec((tr, d), lambda i: (i, 0))
    vec_spec = pl.BlockSpec((1, d), lambda i: (0, 0))
    return pl.pallas_call(
        body, grid=(rows // tr,), in_specs=[row_spec, vec_spec, row_spec, row_spec],
        out_specs=[row_spec, vec_spec], out_shape=[SDS((rows, d), F32), SDS((1, d), F32)],
        compiler_params=_cparams(("arbitrary",)), name=name)(h, g, da, dres)


def _loss_head(h, g, target, *, name):
    rows, d = h.shape
    tr = _pick(rows, (512, 256))

    def body(h_ref, g_ref, t_ref, loss_ref, dh_ref, dg_ref):
        y, vjp = jax.vjp(_rms, h_ref[...], g_ref[...])
        err = y - t_ref[...]
        dh, dg = vjp(err * (1.0 / d))
        dh_ref[...] = dh

        @pl.when(pl.program_id(0) == 0)
        def _():
            dg_ref[...] = jnp.zeros_like(dg_ref)
            loss_ref[...] = jnp.zeros_like(loss_ref)

        dg_ref[...] += dg
        part = jnp.sum(jnp.sum(err * err, axis=-1, keepdims=True), axis=0, keepdims=True) * (0.5 / d)
        loss_ref[...] += jnp.broadcast_to(part, loss_ref.shape)

    row_spec = pl.BlockSpec((tr, d), lambda i: (i, 0))
    vec_spec = pl.BlockSpec((1, d), lambda i: (0, 0))
    loss_spec = pl.BlockSpec((8, 128), lambda i: (0, 0))
    return pl.pallas_call(
        body, grid=(rows // tr,), in_specs=[row_spec, vec_spec, row_spec],
        out_specs=[loss_spec, row_spec, vec_spec],
        out_shape=[SDS((8, 128), F32), SDS((rows, d), F32), SDS((1, d), F32)],
        compiler_params=_cparams(("arbitrary",)), name=name)(h, g, target)


def _gelu(x):
    return 0.5 * x * (1.0 + lax.erf(x * (1.0 / math.sqrt(2.0))))


def _gate_tile(pu, pv, ln_g, ln_b, ws, bs_t):
    u = [_gelu(p) for p in pu]
    v = [_gelu(p) for p in pv]
    mu = sum(jnp.sum(t, axis=-1, keepdims=True) for t in v) * (1.0 / D_INNER)
    vc = [t - mu for t in v]
    var = sum(jnp.sum(t * t, axis=-1, keepdims=True) for t in vc) * (1.0 / D_INNER)
    rstd = lax.rsqrt(var + EPS)
    row = lax.broadcasted_iota(jnp.int32, (CHUNK, CHUNK), 0)
    col = lax.broadcasted_iota(jnp.int32, (CHUNK, CHUNK), 1)
    out = []
    for gi in range(A_GROUPS):
        vn = vc[gi] * rstd * ln_g[gi] + ln_b[gi]
        w = jnp.where(row >= col, ws[gi], 0.0)
        sv = _dot(w, vn) + bs_t[gi]
        out.append(u[gi] * sv)
    return out


def _split(ref, n, width):
    return [ref[:, i * width:(i + 1) * width] for i in range(n)]


def _gate_in_specs():
    return [
        pl.BlockSpec((CHUNK, D_INNER), lambda c: (c, 0)),
        pl.BlockSpec((CHUNK, D_INNER), lambda c: (c, 1)),
        pl.BlockSpec((1, D_INNER), lambda c: (0, 0)),
        pl.BlockSpec((1, D_INNER), lambda c: (0, 0)),
        pl.BlockSpec((A_GROUPS, CHUNK, CHUNK), lambda c: (0, 0, 0)),
        pl.BlockSpec((A_GROUPS, CHUNK, 1), lambda c: (0, 0, 0)),
    ]


def _gate_args(u_ref, v_ref, g_ref, b_ref, ws_ref, bs_ref):
    ng, gw = A_GROUPS, A_GROUP_W
    return (_split(u_ref, ng, gw), _split(v_ref, ng, gw), _split(g_ref, ng, gw), _split(b_ref, ng, gw),
            [ws_ref[i] for i in range(ng)], [bs_ref[i] for i in range(ng)])


def _gate_fwd(proj, ln_g, ln_b, ws, bs_col, mixcat, *, name):
    def body(u_ref, v_ref, g_ref, b_ref, ws_ref, bs_ref, cat_in, cat_ref):
        del cat_in
        out = _gate_tile(*_gate_args(u_ref, v_ref, g_ref, b_ref, ws_ref, bs_ref))
        for gi, o in enumerate(out):
            cat_ref[:, gi * A_GROUP_W:(gi + 1) * A_GROUP_W] = o.astype(cat_ref.dtype)

    return pl.pallas_call(
        body, grid=(N_CHUNKS,), in_specs=[*_gate_in_specs(), pl.BlockSpec(memory_space=pl.ANY)],
        out_specs=pl.BlockSpec((CHUNK, D_INNER), lambda c: (c, 0)), out_shape=SDS(mixcat.shape, mixcat.dtype),
        input_output_aliases={6: 0}, compiler_params=_cparams(("parallel",)), name=name,
    )(proj, proj, ln_g, ln_b, ws, bs_col, mixcat)


def _gate_bwd(proj, ln_g, ln_b, ws, bs_col, dcat, dproj, *, name):
    ng, gw = A_GROUPS, A_GROUP_W

    def body(u_ref, v_ref, g_ref, b_ref, ws_ref, bs_ref, d_ref, dproj_in, dproj_ref, dg_ref, db_ref, dws_ref, dbs_ref):
        del dproj_in
        args = _gate_args(u_ref, v_ref, g_ref, b_ref, ws_ref, bs_ref)
        _, vjp = jax.vjp(_gate_tile, *args)
        dpu, dpv, dg, db, dws, dbs = vjp(_split(d_ref, ng, gw))
        for gi in range(ng):
            dproj_ref[:, gi * gw:(gi + 1) * gw] = dpu[gi].astype(dproj_ref.dtype)
            dproj_ref[:, D_INNER + gi * gw:D_INNER + (gi + 1) * gw] = dpv[gi].astype(dproj_ref.dtype)

        @pl.when(pl.program_id(0) == 0)
        def _():
            for r in (dg_ref, db_ref, dws_ref, dbs_ref):
                r[...] = jnp.zeros_like(r)

        for gi in range(ng):
            dg_ref[:, gi * gw:(gi + 1) * gw] += dg[gi]
            db_ref[:, gi * gw:(gi + 1) * gw] += db[gi]
            dws_ref[gi] += dws[gi]
            dbs_ref[gi] += dbs[gi]

    in_specs = _gate_in_specs()
    return pl.pallas_call(
        body, grid=(N_CHUNKS,),
        in_specs=[*in_specs, pl.BlockSpec((CHUNK, D_INNER), lambda c: (c, 0)), pl.BlockSpec(memory_space=pl.ANY)],
        out_specs=[pl.BlockSpec((CHUNK, 2 * D_INNER), lambda c: (c, 0)), *in_specs[2:]],
        out_shape=[SDS(dproj.shape, dproj.dtype), SDS((1, D_INNER), F32), SDS((1, D_INNER), F32),
                   SDS((ng, CHUNK, CHUNK), F32), SDS((ng, CHUNK, 1), F32)],
        input_output_aliases={7: 0}, compiler_params=_cparams(("arbitrary",)), name=name,
    )(proj, proj, ln_g, ln_b, ws, bs_col, dcat, dproj)


ATT_TQ = 512


def _attn_tile(q, k, v):
    s = _dot_nt(q, k) * (1.0 / math.sqrt(X_HEAD_DIM))
    s = s - jnp.max(s, axis=-1, keepdims=True)
    e = jnp.exp(s)
    p = e / jnp.sum(e, axis=-1, keepdims=True)
    return _dot(p, v)


def _attn_in_specs(q_blk, order):
    hd = X_HEAD_DIM
    return [
        pl.BlockSpec((ATT_TQ, hd), lambda a, b: (order(a, b)[0], q_blk + order(a, b)[1])),
        pl.BlockSpec((N_MEM, hd), lambda a, b: (0, order(a, b)[1])),
        pl.BlockSpec((N_MEM, hd), lambda a, b: (0, X_HEADS + order(a, b)[1])),
    ]


def _attn_fwd(proj, q_off, kv, *, name):
    order = lambda i, h: (i, h)
    cat_blk = D_INNER // X_HEAD_DIM

    def body(q_ref, k_ref, v_ref, o_ref):
        o_ref[...] = _attn_tile(q_ref[...], k_ref[...], v_ref[...]).astype(o_ref.dtype)

    return pl.pallas_call(
        body, grid=(SEQ // ATT_TQ, X_HEADS), in_specs=_attn_in_specs(q_off // X_HEAD_DIM, order),
        out_specs=pl.BlockSpec((ATT_TQ, X_HEAD_DIM), lambda i, h: (i, cat_blk + h)),
        out_shape=SDS((SEQ, MIX_OUT), BF16), compiler_params=_cparams(("parallel", "parallel")), name=name,
    )(proj, kv, kv)


def _attn_bwd(proj, q_off, kv, dcat, dproj_width, dq_off, *, name):
    order = lambda h, i: (i, h)
    cat_blk = D_INNER // X_HEAD_DIM
    dq_blk = dq_off // X_HEAD_DIM

    def body(q_ref, k_ref, v_ref, do_ref, dq_ref, dk_ref, dv_ref):
        _, vjp = jax.vjp(_attn_tile, q_ref[...], k_ref[...], v_ref[...])
        dq, dk, dv = vjp(do_ref[...])
        dq_ref[...] = dq.astype(dq_ref.dtype)

        @pl.when(pl.program_id(1) == 0)
        def _():
            dk_ref[...] = jnp.zeros_like(dk_ref)
            dv_ref[...] = jnp.zeros_like(dv_ref)

        dk_ref[...] += dk
        dv_ref[...] += dv

    kv_spec = pl.BlockSpec((N_MEM, X_HEAD_DIM), lambda h, i: (0, h))
    return pl.pallas_call(
        body, grid=(X_HEADS, SEQ // ATT_TQ),
        in_specs=[*_attn_in_specs(q_off // X_HEAD_DIM, order),
                  pl.BlockSpec((ATT_TQ, X_HEAD_DIM), lambda h, i: (i, cat_blk + h))],
        out_specs=[pl.BlockSpec((ATT_TQ, X_HEAD_DIM), lambda h, i: (i, dq_blk + h)), kv_spec, kv_spec],
        out_shape=[SDS((SEQ, dproj_width), BF16), SDS((N_MEM, X_WIDTH), F32), SDS((N_MEM, X_WIDTH), F32)],
        compiler_params=_cparams(("parallel", "arbitrary")), name=name,
    )(proj, kv, kv, dcat)


CONV_TC = 512


def _shift_down(x, s):
    if s == 0:
        return x
    row = lax.broadcasted_iota(jnp.int32, x.shape, 0)
    return jnp.where(row >= s, pltpu.roll(x, s, 0), 0.0)


def _shift_up(x, s):
    if s == 0:
        return x
    n = x.shape[0]
    row = lax.broadcasted_iota(jnp.int32, x.shape, 0)
    return jnp.where(row < n - s, pltpu.roll(x, n - s, 0), 0.0)


def _conv_pre(x, w_ref, b_ref):
    pre = b_ref[...] + jnp.zeros_like(x)
    for k in range(CONV_K):
        pre = pre + w_ref[k:k + 1, :] * _shift_down(x, CONV_K - 1 - k)
    return pre


def _conv_fwd(proj, w, b, *, name):
    blk0 = D_INNER // CONV_TC

    def body(x_ref, w_ref, b_ref, o_ref):
        pre = _conv_pre(x_ref[...], w_ref, b_ref)
        o_ref[...] = pre * jax.nn.sigmoid(pre)

    return pl.pallas_call(
        body, grid=(CONV_DIM // CONV_TC,),
        in_specs=[pl.BlockSpec((SEQ, CONV_TC), lambda j: (0, blk0 + j)), pl.BlockSpec((CONV_K, CONV_TC), lambda j: (0, j)),
                  pl.BlockSpec((1, CONV_TC), lambda j: (0, j))],
        out_specs=pl.BlockSpec((SEQ, CONV_TC), lambda j: (0, j)), out_shape=SDS((SEQ, CONV_DIM), F32),
        compiler_params=_cparams(("parallel",)), name=name)(proj, w, b)


def _conv_bwd(proj, w, b, dxs, dbm, dcm, dproj, *, name):
    tc = CONV_TC // 2
    blk0 = D_INNER // tc
    n_x = D_INNER // tc
    n_b = SSM_GROUPS * SSM_STATE // tc

    def body(x_ref, w_ref, b_ref, dxs_ref, dbm_ref, dcm_ref, dproj_in, dproj_ref, dw_ref, db_ref):
        del dproj_in
        j = pl.program_id(0)
        x = x_ref[...]
        pre = _conv_pre(x, w_ref, b_ref)
        sg = jax.nn.sigmoid(pre)
        dact = jnp.where(j < n_x, dxs_ref[...], jnp.where(j < n_x + n_b, dbm_ref[...], dcm_ref[...]))
        dpre = dact * (sg * (1.0 + pre * (1.0 - sg)))
        dx = jnp.zeros_like(x)
        for k in range(CONV_K):
            s = CONV_K - 1 - k
            dx = dx + w_ref[k:k + 1, :] * _shift_up(dpre, s)
            dw_ref[k:k + 1, :] = jnp.sum(dpre * _shift_down(x, s), axis=0, keepdims=True)
        dproj_ref[...] = dx.astype(dproj_ref.dtype)
        db_ref[...] = jnp.sum(dpre, axis=0, keepdims=True)

    clip = lambda v, hi: jnp.minimum(jnp.maximum(v, 0), hi)
    return pl.pallas_call(
        body, grid=(CONV_DIM // tc,),
        in_specs=[pl.BlockSpec((SEQ, tc), lambda j: (0, blk0 + j)), pl.BlockSpec((CONV_K, tc), lambda j: (0, j)),
                  pl.BlockSpec((1, tc), lambda j: (0, j)),
                  pl.BlockSpec((SEQ, tc), lambda j: (0, clip(j, n_x - 1))),
                  pl.BlockSpec((SEQ, tc), lambda j: (0, clip(j - n_x, n_b - 1))),
                  pl.BlockSpec((SEQ, tc), lambda j: (0, clip(j - n_x - n_b, n_b - 1))),
                  pl.BlockSpec(memory_space=pl.ANY)],
        out_specs=[pl.BlockSpec((SEQ, tc), lambda j: (0, blk0 + j)), pl.BlockSpec((CONV_K, tc), lambda j: (0, j)),
                   pl.BlockSpec((1, tc), lambda j: (0, j))],
        out_shape=[SDS(dproj.shape, dproj.dtype), SDS((CONV_K, CONV_DIM), F32), SDS((1, CONV_DIM), F32)],
        input_output_aliases={6: 0}, compiler_params=_cparams(("parallel",)), name=name,
    )(proj, w, b, dxs, dbm, dcm, dproj)


SSM_PAIRS = SSM_HPG // 2


def _ssd_tile(xp, zp, bm, cm, hp, dtc, dtr, bias, alog, dsk, gnp):
    row = lax.broadcasted_iota(jnp.int32, (CHUNK, CHUNK), 0)
    col = lax.broadcasted_iota(jnp.int32, (CHUNK, CHUNK), 1)
    causal = row >= col
    tri = jnp.where(causal, 1.0, 0.0)
    left = col < SSM_HEAD_DIM
    top = row < SSM_HEAD_DIM
    ones = jnp.ones((CHUNK, CHUNK), BF16)
    cb = _dot_nt(cm, bm)
    dt_c, cs_c, cs_last, m = [], [], [], []
    for r in range(SSM_HPG):
        a = -jnp.exp(alog[r])
        dt_c.append(jax.nn.softplus(dtc[r] + bias[r]))
        da_c = dt_c[r] * a
        da_r = jax.nn.softplus(dtr[r] + bias[r]) * a
        cs_c.append(jnp.sum(tri * da_r, axis=1, keepdims=True))
        cs_r = jnp.sum(jnp.where(row <= col, 1.0, 0.0) * da_c, axis=0, keepdims=True)
        cs_last.append(jnp.sum(da_c, axis=0, keepdims=True))
        m.append(cb * jnp.exp(jnp.where(causal, cs_c[r] - cs_r, -1e30)))
    ygs, hn = [], []
    for p in range(SSM_PAIRS):
        a, b = 2 * p, 2 * p + 1
        pair = lambda u, v: jnp.where(left, u, v)
        xdt = xp[p] * pair(dt_c[a], dt_c[b])
        y = pair(_dot(m[a], xdt), _dot(m[b], xdt))
        y = y + _dot_nt(cm, hp[p]) * pair(jnp.exp(cs_c[a]), jnp.exp(cs_c[b]))
        y = y + xp[p] * pair(dsk[a], dsk[b])
        decay = pair(jnp.exp(cs_last[a] - cs_c[a]), jnp.exp(cs_last[b] - cs_c[b]))
        states = _dot_tn(xdt * decay, bm)
        hn.append(hp[p] * jnp.where(top, jnp.exp(cs_last[a]), jnp.exp(cs_last[b])) + states)
        ygs.append(y * (zp[p] * jax.nn.sigmoid(zp[p])))
    ms = sum(_dot(t * t, ones) for t in ygs) * (1.0 / SSM_GROUP_W)
    rs = lax.rsqrt(ms + EPS)
    return [ygs[p] * rs * gnp[p] for p in range(SSM_PAIRS)], hn


def _ssd_in_specs(cidx):
    gw, n = SSM_GROUP_W, SSM_STATE
    bm_blk = D_INNER // n
    return [
        pl.BlockSpec((CHUNK, gw), lambda g, c: (cidx(c), g)),
        pl.BlockSpec((CHUNK, gw), lambda g, c: (cidx(c), g)),
        pl.BlockSpec((CHUNK, n), lambda g, c: (cidx(c), bm_blk + g)),
        pl.BlockSpec((CHUNK, n), lambda g, c: (cidx(c), bm_blk + SSM_GROUPS + g)),
        pl.BlockSpec((None, CHUNK, SSM_HPG), lambda g, c: (g, cidx(c), 0)),
        pl.BlockSpec((None, SSM_HPG, CHUNK), lambda g, c: (g, 0, cidx(c))),
        pl.BlockSpec((None, 1, SSM_HPG), lambda g, c: (g, 0, 0)),
        pl.BlockSpec((None, 1, SSM_HPG), lambda g, c: (g, 0, 0)),
        pl.BlockSpec((None, 1, SSM_HPG), lambda g, c: (g, 0, 0)),
        pl.BlockSpec((1, gw), lambda g, c: (0, g)),
    ]


def _ssd_args(x_ref, z_ref, bm_ref, cm_ref, hp, dtc_ref, dtr_ref, bias_ref, alog_ref, dsk_ref, gn_ref):
    nh, npair, w = SSM_HPG, SSM_PAIRS, 2 * SSM_HEAD_DIM
    col = lambda ref: [ref[:, r:r + 1] for r in range(nh)]
    return (_split(x_ref, npair, w), _split(z_ref, npair, w), bm_ref[...], cm_ref[...], hp,
            col(dtc_ref), [dtr_ref[r:r + 1, :] for r in range(nh)], col(bias_ref), col(alog_ref), col(dsk_ref),
            _split(gn_ref, npair, w))


def _pair_rows(ref):
    w = 2 * SSM_HEAD_DIM
    return [ref[p * w:(p + 1) * w, :] for p in range(SSM_PAIRS)]


def _ssd_fwd(xbc, proj, dt_c, dt_r, bias, alog, dsk, gn, mixcat, *, name):
    w = 2 * SSM_HEAD_DIM

    def body(x_ref, z_ref, bm_ref, cm_ref, dtc_ref, dtr_ref, bias_ref, alog_ref, dsk_ref, gn_ref, cat_in,
             cat_ref, hprev_ref, h_scr):
        del cat_in

        @pl.when(pl.program_id(1) == 0)
        def _():
            h_scr[...] = jnp.zeros_like(h_scr)

        hprev_ref[...] = h_scr[...]
        yn, hn = _ssd_tile(*_ssd_args(x_ref, z_ref, bm_ref, cm_ref, _pair_rows(h_scr), dtc_ref, dtr_ref, bias_ref,
                                      alog_ref, dsk_ref, gn_ref))
        for p in range(SSM_PAIRS):
            cat_ref[:, p * w:(p + 1) * w] = yn[p].astype(cat_ref.dtype)
            h_scr[p * w:(p + 1) * w, :] = hn[p]

    return pl.pallas_call(
        body, grid=(SSM_GROUPS, N_CHUNKS), in_specs=[*_ssd_in_specs(lambda c: c), pl.BlockSpec(memory_space=pl.ANY)],
        out_specs=[pl.BlockSpec((CHUNK, SSM_GROUP_W), lambda g, c: (c, g)),
                   pl.BlockSpec((None, None, SSM_GROUP_W, SSM_STATE), lambda g, c: (c, g, 0, 0))],
        out_shape=[SDS(mixcat.shape, mixcat.dtype), SDS((N_CHUNKS, SSM_GROUPS, SSM_GROUP_W, SSM_STATE), F32)],
        scratch_shapes=[pltpu.VMEM((SSM_GROUP_W, SSM_STATE), F32)],
        input_output_aliases={10: 0}, compiler_params=_cparams(("parallel", "arbitrary")), name=name,
    )(xbc, proj, xbc, xbc, dt_c, dt_r, bias, alog, dsk, gn, mixcat)


def _ssd_bwd(xbc, proj, dt_c, dt_r, bias, alog, dsk, gn, hprev, dcat, dproj, *, name):
    nh, w, gw, n = SSM_HPG, 2 * SSM_HEAD_DIM, SSM_GROUP_W, SSM_STATE
    rev = lambda c: N_CHUNKS - 1 - c

    def body(x_ref, z_ref, bm_ref, cm_ref, dtc_ref, dtr_ref, bias_ref, alog_ref, dsk_ref, gn_ref, hprev_ref, dy_ref,
             dproj_in, dz_ref, dxs_ref, dbm_ref, dcm_ref, ddtc_ref, ddtr_ref, dbias_ref, dalog_ref, ddsk_ref, dgn_ref,
             dh_scr):
        del dproj_in
        first = pl.program_id(1) == 0

        @pl.when(first)
        def _():
            dh_scr[...] = jnp.zeros_like(dh_scr)
            for ref in (dbias_ref, dalog_ref, ddsk_ref, dgn_ref):
                ref[...] = jnp.zeros_like(ref)

        args = _ssd_args(x_ref, z_ref, bm_ref, cm_ref, _pair_rows(hprev_ref), dtc_ref, dtr_ref, bias_ref, alog_ref,
                         dsk_ref, gn_ref)
        _, vjp = jax.vjp(_ssd_tile, *args)
        dxs, dzs, dbm, dcm, dhs, ddtc, ddtr, dbias, dalog, ddsk, dgn = vjp(
            (_split(dy_ref, SSM_PAIRS, w), _pair_rows(dh_scr)))
        dbm_ref[...] = dbm
        dcm_ref[...] = dcm
        for q in range(SSM_PAIRS):
            dxs_ref[:, q * w:(q + 1) * w] = dxs[q]
            dz_ref[:, q * w:(q + 1) * w] = dzs[q].astype(dz_ref.dtype)
            dh_scr[q * w:(q + 1) * w, :] = dhs[q]
            dgn_ref[:, q * w:(q + 1) * w] += dgn[q]
        for r in range(nh):
            ddtc_ref[:, r:r + 1] = ddtc[r]
            ddtr_ref[r:r + 1, :] = ddtr[r]
            dbias_ref[:, r:r + 1] += dbias[r]
            dalog_ref[:, r:r + 1] += dalog[r]
            ddsk_ref[:, r:r + 1] += ddsk[r]

    par_spec = pl.BlockSpec((None, 1, nh), lambda g, c: (g, 0, 0))
    return pl.pallas_call(
        body, grid=(SSM_GROUPS, N_CHUNKS),
        in_specs=[*_ssd_in_specs(rev),
                  pl.BlockSpec((None, None, gw, n), lambda g, c: (rev(c), g, 0, 0)),
                  pl.BlockSpec((CHUNK, gw), lambda g, c: (rev(c), g)),
                  pl.BlockSpec(memory_space=pl.ANY)],
        out_specs=[pl.BlockSpec((CHUNK, gw), lambda g, c: (rev(c), g)),
                   pl.BlockSpec((CHUNK, gw), lambda g, c: (rev(c), g)),
                   pl.BlockSpec((CHUNK, n), lambda g, c: (rev(c), g)),
                   pl.BlockSpec((CHUNK, n), lambda g, c: (rev(c), g)),
                   pl.BlockSpec((None, CHUNK, nh), lambda g, c: (g, rev(c), 0)),
                   pl.BlockSpec((None, nh, CHUNK), lambda g, c: (g, 0, rev(c))),
                   par_spec, par_spec, par_spec,
                   pl.BlockSpec((1, gw), lambda g, c: (0, g))],
        out_shape=[SDS(dproj.shape, dproj.dtype), SDS((SEQ, D_INNER), F32), SDS((SEQ, SSM_GROUPS * n), F32),
                   SDS((SEQ, SSM_GROUPS * n), F32), SDS((SSM_GROUPS, SEQ, nh), F32), SDS((SSM_GROUPS, nh, SEQ), F32),
                   SDS((SSM_GROUPS, 1, nh), F32), SDS((SSM_GROUPS, 1, nh), F32), SDS((SSM_GROUPS, 1, nh), F32),
                   SDS((1, D_INNER), F32)],
        scratch_shapes=[pltpu.VMEM((gw, n), F32)],
        input_output_aliases={12: 0}, compiler_params=_cparams(("parallel", "arbitrary")), name=name,
    )(xbc, proj, xbc, xbc, dt_c, dt_r, bias, alog, dsk, gn, hprev, dcat, dproj)


def _sum_contributions(chip, parts, landed, *, name):
    _, r, c = parts.shape
    tr = _pick(r, (256, 384, 128))

    def body(chip_ref, own_ref, landed_ref, o_ref):
        del chip_ref
        acc = own_ref[...].astype(F32)
        for s in range(landed_ref.shape[0]):
            acc = acc + landed_ref[s].astype(F32)
        o_ref[...] = acc

    grid_spec = pltpu.PrefetchScalarGridSpec(
        num_scalar_prefetch=1, grid=(r // tr,),
        in_specs=[pl.BlockSpec((None, tr, c), lambda i, chip_ref: (chip_ref[0], i, 0)),
                  pl.BlockSpec((landed.shape[0], tr, c), lambda i, chip_ref: (0, i, 0))],
        out_specs=pl.BlockSpec((tr, c), lambda i, chip_ref: (i, 0)))
    return pl.pallas_call(body, grid_spec=grid_spec, out_shape=SDS((r, c), F32),
                          compiler_params=_cparams(("parallel",)), name=name)(chip, parts, landed)


def _adamw(w, g, m, v, *, name):
    r, c = w.shape
    tr = r if r <= 256 else _pick(r, (256, 128, 8))
    spec = pl.BlockSpec((tr, c), lambda i: (i, 0))

    def body(w_ref, g_ref, m_ref, v_ref, d_ref, mo_ref, vo_ref):
        g = g_ref[...]
        m_new = ADAM_B1 * m_ref[...] + (1.0 - ADAM_B1) * g
        v_new = ADAM_B2 * v_ref[...] + (1.0 - ADAM_B2) * (g * g)
        m_hat = m_new / (1.0 - ADAM_B1 ** ADAM_STEP)
        v_hat = v_new / (1.0 - ADAM_B2 ** ADAM_STEP)
        d_ref[...] = -ADAM_LR * (m_hat / (jnp.sqrt(v_hat) + ADAM_EPS) + ADAM_WD * w_ref[...])
        mo_ref[...] = m_new
        vo_ref[...] = v_new

    return pl.pallas_call(body, grid=(r // tr,), in_specs=[spec] * 4, out_specs=[spec] * 3,
                          out_shape=[SDS((r, c), F32)] * 3, compiler_params=_cparams(("parallel",)), name=name)(w, g, m, v)


ANY = pl.BlockSpec(memory_space=pl.ANY)


def _place():
    x, y, c = lax.axis_index("x"), lax.axis_index("y"), lax.axis_index("c")
    chips = [(1 - x, y), (x, 1 - y), (1 - x, 1 - y)]
    return x, y, c, chips


def _remote(src, dst, send_sem, recv_sem, to):
    return pltpu.make_async_remote_copy(src_ref=src, dst_ref=dst, send_sem=send_sem, recv_sem=recv_sem,
                                        device_id=to, device_id_type=MESH)


STREAM_ROWS = 256


def _stream_rows(i):
    return pl.ds(pl.multiple_of(i * STREAM_ROWS, STREAM_ROWS), STREAM_ROWS)


def _channel_scratch(width, dtype, rows=STREAM_ROWS):
    buf = (2, rows, width)
    return [pltpu.VMEM(buf, dtype), pltpu.VMEM(buf, dtype), *([pltpu.SemaphoreType.DMA((2,))] * 5),
            pltpu.SemaphoreType.REGULAR((2,))]


CHANNEL_REFS = 8


def _copy_blocks(srcs, dsts, ch):
    sbuf, _, ld, _, _, st, _, _ = ch
    n = len(srcs)
    load = lambda i: pltpu.make_async_copy(srcs[i], sbuf.at[i % 2], ld.at[i % 2])
    store = lambda i: pltpu.make_async_copy(sbuf.at[i % 2], dsts[i], st.at[i % 2])
    load(0).start()
    for i in range(n):
        if i + 1 < n:
            if i >= 1:
                store(i - 1).wait()
            load(i + 1).start()
        load(i).wait()
        store(i).start()
    for i in range(max(0, n - 2), n):
        store(i).wait()


def _exchange_blocks(srcs, dsts, keeps, ch, sibling):
    sbuf, rbuf, ld, snd, rcv, st, kp, credit = ch
    n = len(srcs)
    load = lambda i: pltpu.make_async_copy(srcs[i], sbuf.at[i % 2], ld.at[i % 2])
    push = lambda i: _remote(sbuf.at[i % 2], rbuf.at[i % 2], snd.at[i % 2], rcv.at[i % 2], sibling)
    store = lambda i: pltpu.make_async_copy(rbuf.at[i % 2], dsts[i], st.at[i % 2])
    save = lambda i: pltpu.make_async_copy(sbuf.at[i % 2], keeps[i], kp.at[i % 2])

    def send(i):
        load(i).wait()
        pl.semaphore_wait(credit.at[i % 2], 1)
        push(i).start()
        if keeps[i] is not None:
            save(i).start()

    for i in range(min(2, n)):
        pl.semaphore_signal(credit.at[i], 1, device_id=sibling, device_id_type=MESH)
        load(i).start()
    send(0)
    for i in range(n):
        if i >= 1:
            store(i - 1).wait()
            if i + 1 < n:
                pl.semaphore_signal(credit.at[(i + 1) % 2], 1, device_id=sibling, device_id_type=MESH)
        if i + 1 < n:
            send(i + 1)
        push(i).wait_recv()
        store(i).start()
        push(i).wait_send()
        if keeps[i] is not None:
            save(i).wait()
        if i + 2 < n:
            load(i + 2).start()
    store(n - 1).wait()


def _all_gather_shards(shards, small, *, name):
    n = len(shards)

    def body(*refs):
        ins, outs = refs[:n + 1], refs[n + 1:2 * n + 2]
        scr = refs[2 * n + 2:]
        chans = [scr[CHANNEL_REFS * t:CHANNEL_REFS * (t + 1)] for t in range(n)]
        send_sems, recv_sems, small_sems = scr[CHANNEL_REFS * n:]
        x, y, c, _ = _place()
        me = 2 * x + y
        sibling = (x, y, 1 - c)
        near = (lax.rem(x + 1 - c, 2), lax.rem(y + c, 2))
        far = (lax.rem(x + c, 2), lax.rem(y + 1 - c, 2))
        k_near, k_far, k_diag = 2 * near[0] + near[1], 2 * far[0] + far[1], 3 - me
        targets = ((*near, c), (*far, c), (*far, c))
        arrives = (k_near, k_far, k_diag)
        streams_in = (k_far, k_near, k_diag)

        def ici(t, j, src, blk):
            return _remote(src, outs[t].at[blk, c], send_sems.at[3 * t + j], recv_sems.at[3 * t + j], targets[j])

        first = [ici(t, j, ins[t].at[c], me) for t in range(n + 1) for j in range(2)]
        for cp in first:
            cp.start()
        small_local = pltpu.make_async_copy(ins[n], outs[n].at[me], small_sems.at[6])
        small_local.start()
        for t in range(n):
            _copy_blocks([ins[t].at[h] for h in range(2)], [outs[t].at[me, h] for h in range(2)], chans[t])
        passed = []
        for j in range(3):
            for t in range(n + 1):
                landed = outs[t].at[arrives[j], c]
                ici(t, j, landed, arrives[j]).wait_recv()
                if j == 0:
                    fwd = ici(t, 2, landed, k_near)
                    fwd.start()
                    passed.append(fwd)
                if t < n:
                    _exchange_blocks([landed], [outs[t].at[streams_in[j], 1 - c]], [None], chans[t], sibling)
                else:
                    fwd = _remote(landed, landed, small_sems.at[j], small_sems.at[3 + j], sibling)
                    fwd.start()
                    passed.append(fwd)
        for j in range(3):
            got = outs[n].at[streams_in[j], 1 - c]
            _remote(got, got, small_sems.at[j], small_sems.at[3 + j], sibling).wait_recv()
        for cp in first + passed:
            cp.wait_send()
        small_local.wait()

    scratch = []
    for s in shards:
        scratch += _channel_scratch(s.shape[2], s.dtype, rows=s.shape[1])
    return pl.pallas_call(
        body, in_specs=[ANY] * (n + 1), out_specs=[ANY] * (n + 1),
        out_shape=[SDS((N_CHIPS, *s.shape), s.dtype) for s in (*shards, small)],
        scratch_shapes=[*scratch, pltpu.SemaphoreType.DMA((3 * n + 3,)), pltpu.SemaphoreType.DMA((3 * n + 3,)),
                        pltpu.SemaphoreType.DMA((7,))],
        compiler_params=pltpu.CompilerParams(vmem_limit_bytes=VMEM_LIMIT), name=name)(*shards, small)


def _pair_reduce(stacks, *, name):
    n = len(stacks)
    per = 11

    def body(*refs):
        ins, outs, scr = refs[:n], refs[n:2 * n], refs[2 * n:]
        x, y, c, _ = _place()
        sibling = (x, y, 1 - c)
        for t in range(n):
            sraw, sbuf, rbuf, obuf, pbuf, ld_s, ld_o, snd, rcv, st, credit = scr[per * t:per * (t + 1)]
            steps = ins[t].shape[1] // STREAM_ROWS
            src, own, out = ins[t].at[1 - c], ins[t].at[c], outs[t]

            def load_s(i, slot, src=src, sraw=sraw, ld_s=ld_s):
                return pltpu.make_async_copy(src.at[_stream_rows(i)], sraw.at[slot], ld_s.at[slot])

            def load_o(i, slot, own=own, obuf=obuf, ld_o=ld_o):
                return pltpu.make_async_copy(own.at[_stream_rows(i)], obuf.at[slot], ld_o.at[slot])

            def push(slot, sbuf=sbuf, rbuf=rbuf, snd=snd, rcv=rcv):
                return _remote(sbuf.at[slot], rbuf.at[slot], snd.at[slot], rcv.at[slot], sibling)

            def store(i, slot, pbuf=pbuf, out=out, st=st):
                return pltpu.make_async_copy(pbuf.at[slot], out.at[_stream_rows(i)], st.at[slot])

            assert steps >= 2
            for slot in range(2):
                pl.semaphore_signal(credit.at[slot], 1, device_id=sibling, device_id_type=MESH)
                load_s(slot, slot).start()
                load_o(slot, slot).start()
            load_s(0, 0).wait()
            sbuf[0] = sraw[0].astype(sbuf.dtype)
            pl.semaphore_wait(credit.at[0], 1)
            push(0).start()

            def step(i, carry, load_s=load_s, load_o=load_o, push=push, store=store, sraw=sraw, sbuf=sbuf, rbuf=rbuf,
                     obuf=obuf, pbuf=pbuf, credit=credit, steps=steps):
                slot = lax.rem(i, 2)
                nxt = 1 - slot

                @pl.when(i + 1 < steps)
                def _():
                    load_s(i + 1, nxt).wait()
                    sbuf[nxt] = sraw[nxt].astype(sbuf.dtype)
                    pl.semaphore_wait(credit.at[nxt], 1)
                    push(nxt).start()

                load_o(i, slot).wait()
                push(slot).wait_recv()

                @pl.when(i >= 2)
                def _():
                    store(i, slot).wait()

                pbuf[slot] = (obuf[slot] + rbuf[slot].astype(F32)).astype(pbuf.dtype)
                store(i, slot).start()
                push(slot).wait_send()

                @pl.when(i + 2 < steps)
                def _():
                    load_s(i + 2, slot).start()
                    load_o(i + 2, slot).start()
                    pl.semaphore_signal(credit.at[slot], 1, device_id=sibling, device_id_type=MESH)
                return carry

            lax.fori_loop(0, steps, step, 0)
            for slot in range(2):
                store(0, slot).wait()

    scratch = []
    for s in stacks:
        buf = (2, STREAM_ROWS, s.shape[2])
        scratch += [pltpu.VMEM(buf, F32), pltpu.VMEM(buf, BF16), pltpu.VMEM(buf, BF16), pltpu.VMEM(buf, F32),
                    pltpu.VMEM(buf, BF16), *([pltpu.SemaphoreType.DMA((2,))] * 5), pltpu.SemaphoreType.REGULAR((2,))]
    return pl.pallas_call(
        body, in_specs=[ANY] * n, out_specs=[ANY] * n, out_shape=[SDS(s.shape[1:], BF16) for s in stacks],
        scratch_shapes=scratch, compiler_params=pltpu.CompilerParams(vmem_limit_bytes=VMEM_LIMIT), name=name)(*stacks)


HBM_SPEC = pl.BlockSpec(memory_space=pltpu.HBM)
SEM_SPEC = pl.BlockSpec(memory_space=pltpu.SEMAPHORE)
SIDE_EFFECT = pltpu.SideEffectType.DATAFLOW_SIDE_EFFECTING


def _scatter_copies(ins, lands, send_sems, recv_sems):
    _, _, c, chips = _place()
    return [_remote(ins[t].at[2 * cx + cy], lands[t].at[j], send_sems.at[3 * t + j], recv_sems.at[3 * t + j],
                    (cx, cy, c)) for t in range(len(ins)) for j, (cx, cy) in enumerate(chips)]


def _chip_scatter_start(parts, *, name):
    n = len(parts)

    def body(*refs):
        ins, lands = refs[:n], refs[n:2 * n]
        send_sems, recv_sems, token = refs[2 * n], refs[2 * n + 1], refs[-1]
        for cp in _scatter_copies(ins, lands, send_sems, recv_sems):
            cp.start()
        token[...] = jnp.zeros_like(token)

    hbm = lambda a: pltpu.with_memory_space_constraint(a, pltpu.HBM)
    lands = [hbm(lax.empty((3, *p.shape[1:]), p.dtype)) for p in parts]
    thru = [pltpu.HBM(a.shape, a.dtype) for a in (*parts, *lands)]
    outs = pl.pallas_call(
        body, name=name,
        out_shape=(pltpu.SemaphoreType.DMA((3 * n,)), pltpu.SemaphoreType.DMA((3 * n,)), *thru, SDS((8, 128), F32)),
        in_specs=[HBM_SPEC] * (2 * n),
        out_specs=(SEM_SPEC, SEM_SPEC, *([HBM_SPEC] * (2 * n)), pl.BlockSpec(memory_space=pltpu.VMEM)),
        input_output_aliases={i: 2 + i for i in range(2 * n)},
        compiler_params=pltpu.CompilerParams(has_side_effects=SIDE_EFFECT),
    )(*[hbm(p) for p in parts], *lands)
    return outs[0], outs[1], outs[2:2 + n], outs[2 + n:2 + 2 * n], outs[-1]


def _chip_scatter_wait(send_sems, recv_sems, parts, lands, after, *, name):
    n = len(parts)

    def body(*refs):
        ins, lands_in = refs[:n], refs[n:2 * n]
        for cp in _scatter_copies(ins, lands_in, refs[2 * n], refs[2 * n + 1]):
            cp.wait_send()
            cp.wait_recv()

    outs = pl.pallas_call(
        body, name=name, out_shape=[pltpu.HBM(a.shape, a.dtype) for a in (*parts, *lands)],
        in_specs=[*([HBM_SPEC] * (2 * n)), SEM_SPEC, SEM_SPEC, *([ANY] * len(after))],
        out_specs=[HBM_SPEC] * (2 * n), input_output_aliases={i: i for i in range(2 * n)},
        compiler_params=pltpu.CompilerParams(has_side_effects=SIDE_EFFECT),
    )(*parts, *lands, send_sems, recv_sems, *after)
    return outs[:n], outs[n:]


def _gather_copies(shards, zones, send_sems, recv_sems):
    x, y, c, chips = _place()
    return [_remote(shards[t].at[c], zones[t].at[2 * x + y, c], send_sems.at[3 * t + j], recv_sems.at[3 * t + j],
                    (cx, cy, c)) for t in range(len(shards)) for j, (cx, cy) in enumerate(chips)]


def _gather_start(shards, after, *, name):
    n = len(shards)

    def body(*refs):
        ins, zones = refs[:n], refs[n:2 * n]
        send_sems, recv_sems, token = refs[2 * n + len(after)], refs[2 * n + len(after) + 1], refs[-1]
        for cp in _gather_copies(ins, zones, send_sems, recv_sems):
            cp.start()
        token[...] = jnp.zeros_like(token)

    hbm = lambda a: pltpu.with_memory_space_constraint(a, pltpu.HBM)
    zones = [hbm(lax.empty((N_CHIPS, *s.shape), s.dtype)) for s in shards]
    thru = [pltpu.HBM(a.shape, a.dtype) for a in (*shards, *zones)]
    outs = pl.pallas_call(
        body, name=name,
        out_shape=(pltpu.SemaphoreType.DMA((3 * n,)), pltpu.SemaphoreType.DMA((3 * n,)), *thru, SDS((8, 128), F32)),
        in_specs=[*([HBM_SPEC] * (2 * n)), *([ANY] * len(after))],
        out_specs=(SEM_SPEC, SEM_SPEC, *([HBM_SPEC] * (2 * n)), pl.BlockSpec(memory_space=pltpu.VMEM)),
        input_output_aliases={i: 2 + i for i in range(2 * n)},
        compiler_params=pltpu.CompilerParams(has_side_effects=SIDE_EFFECT),
    )(*[hbm(s) for s in shards], *zones, *after)
    return outs[0], outs[1], outs[2:2 + n], outs[2 + n:2 + 2 * n], outs[-1]


def _gather_wait(send_sems, recv_sems, shards, zones, after, *, name):
    n = len(shards)

    def body(*refs):
        for cp in _gather_copies(refs[:n], refs[n:2 * n], refs[2 * n], refs[2 * n + 1]):
            cp.wait_send()
            cp.wait_recv()

    outs = pl.pallas_call(
        body, name=name, out_shape=[pltpu.HBM(a.shape, a.dtype) for a in (*shards, *zones)],
        in_specs=[*([HBM_SPEC] * (2 * n)), SEM_SPEC, SEM_SPEC, *([ANY] * len(after))],
        out_specs=[HBM_SPEC] * (2 * n), input_output_aliases={i: i for i in range(2 * n)},
        compiler_params=pltpu.CompilerParams(has_side_effects=SIDE_EFFECT),
    )(*shards, *zones, send_sems, recv_sems, *after)
    return outs[:n], outs[n:]


def _gather_finish(shards, zones, *, name):
    n = len(shards)

    def body(*refs):
        ins, zones_in, outs, scr = refs[:n], refs[n:2 * n], refs[2 * n:3 * n], refs[3 * n:]
        x, y, c, chips = _place()
        me = 2 * x + y
        sibling = (x, y, 1 - c)
        others = [2 * cx + cy for cx, cy in chips]
        for t in range(n):
            chan = scr[CHANNEL_REFS * t:CHANNEL_REFS * (t + 1)]
            _copy_blocks([ins[t].at[h] for h in range(2)], [outs[t].at[me, h] for h in range(2)], chan)
            _exchange_blocks([zones_in[t].at[k, c] for k in others], [outs[t].at[k, 1 - c] for k in others],
                             [None] * len(others), chan, sibling)

    scratch = []
    for s in shards:
        scratch += _channel_scratch(s.shape[2], s.dtype, rows=s.shape[1])
    return pl.pallas_call(
        body, in_specs=[ANY] * (2 * n), out_specs=[ANY] * n, out_shape=[SDS(z.shape, z.dtype) for z in zones],
        input_output_aliases={n + t: t for t in range(n)}, scratch_shapes=scratch,
        compiler_params=pltpu.CompilerParams(vmem_limit_bytes=VMEM_LIMIT), name=name)(*shards, *zones)


def _pair_share(groups, *, name):
    finals = [f for grp in groups for f in grp]
    n, n_out = len(finals), len(groups)

    def body(*refs):
        ins, outs, scr = refs[:n], refs[n:n + n_out], refs[n + n_out:]
        x, y, c, _ = _place()
        sibling = (x, y, 1 - c)
        t = 0
        for o, grp in enumerate(groups):
            rows = grp[0].shape[0] // 2
            blocks = [(layer, pl.ds(b * rows, rows)) for layer in range(len(grp)) for b in range(2)]
            _exchange_blocks([ins[t + layer].at[rs] for layer, rs in blocks],
                             [outs[o].at[layer, 1 - c, rs] for layer, rs in blocks],
                             [outs[o].at[layer, c, rs] for layer, rs in blocks],
                             scr[CHANNEL_REFS * o:CHANNEL_REFS * (o + 1)], sibling)
            t += len(grp)

    scratch = []
    for grp in groups:
        scratch += _channel_scratch(grp[0].shape[1], grp[0].dtype, rows=grp[0].shape[0] // 2)
    return pl.pallas_call(
        body, in_specs=[ANY] * n, out_specs=[ANY] * n_out,
        out_shape=[SDS((len(grp), 2, *grp[0].shape), grp[0].dtype) for grp in groups],
        scratch_shapes=scratch, compiler_params=pltpu.CompilerParams(vmem_limit_bytes=VMEM_LIMIT), name=name)(*finals)


def _all_reduce_small(v, *, name):
    rows, lanes = v.shape
    n_dev = 8

    def body(v_ref, o_ref, all_ref, send_sems, recv_sems, local_sem):
        x, y, c, chips = _place()
        me, sibling = (x, y, c), (x, y, 1 - c)

        def block(px, py, pc):
            return all_ref.at[4 * px + 2 * py + pc]

        def copy(k, blk, to, src=None):
            return _remote(block(*blk) if src is None else src, block(*blk), send_sems.at[k], recv_sems.at[k], to)

        mine = pltpu.make_async_copy(v_ref, block(*me), local_sem)
        mine.start()
        first = [copy(0, me, sibling, src=v_ref)]
        first += [copy(1 + j, me, (*chip, c), src=v_ref) for j, chip in enumerate(chips)]
        for cp in first:
            cp.start()
        passed = [copy(4 + j, (*chip, c), sibling) for j, chip in enumerate(chips)]
        for j, chip in enumerate(chips):
            copy(1 + j, (*chip, c), me).wait_recv()
            passed[j].start()
        copy(0, sibling, me).wait_recv()
        for j, chip in enumerate(chips):
            copy(4 + j, (*chip, 1 - c), me).wait_recv()
        for cp in first + passed:
            cp.wait_send()
        mine.wait()
        acc = all_ref[0]
        for k in range(1, n_dev):
            acc = acc + all_ref[k]
        o_ref[...] = acc

    vmem = pl.BlockSpec(memory_space=pltpu.VMEM)
    return pl.pallas_call(
        body, in_specs=[vmem], out_specs=vmem, out_shape=SDS((rows, lanes), F32),
        scratch_shapes=[pltpu.VMEM((n_dev, rows, lanes), F32), pltpu.SemaphoreType.DMA((7,)),
                        pltpu.SemaphoreType.DMA((7,)), pltpu.SemaphoreType.DMA],
        compiler_params=pltpu.CompilerParams(vmem_limit_bytes=VMEM_LIMIT), name=name)(v)


def _relu2_epilogue(acc):
    return acc, jnp.square(jnp.maximum(acc, 0.0))


def _res_epilogue(acc, res):
    return (acc + res,)


def _drelu2_epilogue(acc, pre):
    return (acc * (2.0 * jnp.maximum(pre.astype(F32), 0.0)),)


def _ffn_fwd(h, g, w1, w2, tag):
    f = _rms_fwd(h, g, name=f"ffn_norm_{tag}")
    pre, act = _mm_nn(f, w1, name=f"ffn1_{tag}", epilogue=_relu2_epilogue, n_out_dtypes=(BF16, BF16))
    h_out = _mm_nn(act, w2, name=f"ffn2_{tag}", extras=(h,), epilogue=_res_epilogue)
    return h_out, (f, pre, act)


def _ffn_bwd(dh, h, g, w1, w2, saved, layer, after=()):
    f, pre, act = saved
    dpre = _mm_nt(dh, w2, name=f"ffn2_dx_{layer}", out_dtype=BF16, extras=(pre,), epilogue=_drelu2_epilogue,
                  after=after)
    dw2 = _mm_tn_stacked(act, dh, name=f"ffn2_dw_{layer}", col_slots=False)
    df = _mm_nt(dpre, w1, name=f"ffn1_dx_{layer}")
    dw1 = _mm_tn_stacked(f, dpre, name=f"ffn1_dw_{layer}", col_slots=True)
    dh, dg = _rms_bwd(h, g, df, dh, name=f"ffn_norm_bwd_{layer}")
    return dh, dg, dw1, dw2


def _kv_fwd(mem, g, w_kv, tag):
    m = _rms_fwd(mem, g, name=f"mem_norm_{tag}")
    return m, _mm_nn(m, w_kv, name=f"kv_{tag}")


def _kv_bwd(mem, g, w_kv, m, dk, dv, layer):
    dkv = jnp.concatenate([dk, dv], axis=1)
    dw = _mm_tn_stacked(m, dkv, name=f"kv_dw_{layer}", col_slots=True)
    dm = _mm_nt(dkv, w_kv, name=f"kv_dx_{layer}")
    _, dg = _rms_bwd(mem, g, dm, dm, name=f"mem_norm_bwd_{layer}")
    return dw, dg


def _local_step(x, mem, target, p, after_layer1=None, after_ffn0=None, after_mixer0=None):
    row = lambda v: v.reshape(1, -1)
    g = {}

    h0 = x
    a0 = _rms_fwd(h0, row(p["norm_mix"][0]), name="mix_norm_0")
    proj_a = _mm_nn(a0, p["a_in"], name="a_in", after=p.get("after_start", ()))
    m0, kv0 = _kv_fwd(mem, row(p["mem_norm"][0]), p["w_kv"][0], "0")
    cat0 = _attn_fwd(proj_a, 2 * D_INNER, kv0, name="attn_0")
    bs_col = p["a_bs"].reshape(A_GROUPS, CHUNK, 1)
    cat0 = _gate_fwd(proj_a, p["a_ln_g"], p["a_ln_b"], p["a_ws"], bs_col, cat0, name="gate")
    h1 = _mm_nn(cat0, p["w_out"][0], name="out_0", extras=(h0,), epilogue=_res_epilogue)
    h2, ffn0 = _ffn_fwd(h1, row(p["norm_ffn"][0]), p["w_ffn1"][0], p["w_ffn2"][0], "0")

    if "layer1_mixer" in p:
        w_kv1, w_out1, b_in = p["layer1_mixer"](h2)
    else:
        w_kv1, w_out1, b_in = p["w_kv"][1], p["w_out"][1], p["b_in"]
    a1 = _rms_fwd(h2, row(p["norm_mix"][1]), name="mix_norm_1")
    proj_b = _mm_nn(a1, b_in, name="b_in")
    m1, kv1 = _kv_fwd(mem, row(p["mem_norm"][1]), w_kv1, "1")
    cat1 = _attn_fwd(proj_b, B_Q_OFF, kv1, name="attn_1")
    xbc = _conv_fwd(proj_b, p["b_conv_w"], p["b_conv_b"], name="conv")
    dt_raw = proj_b[:, B_DT_OFF:B_DT_OFF + SSM_HEADS].reshape(SEQ, SSM_GROUPS, SSM_HPG)
    dt_c = jnp.transpose(dt_raw, (1, 0, 2))
    dt_r = jnp.transpose(dt_raw, (1, 2, 0))
    per_head = lambda v: v.reshape(SSM_GROUPS, 1, SSM_HPG)
    ssd_par = (per_head(p["b_dt_bias"]), per_head(p["b_a_log"]), per_head(p["b_d"]), p["b_gnorm"])
    cat1, hprev = _ssd_fwd(xbc, proj_b, dt_c, dt_r, *ssd_par, cat1, name="ssd")
    h3 = _mm_nn(cat1, w_out1, name="out_1", extras=(h2,), epilogue=_res_epilogue)
    w_ffn1_1, w_ffn2_1 = p["layer1_ffn"](h3) if "layer1_ffn" in p else (p["w_ffn1"][1], p["w_ffn2"][1])
    h4, ffn1 = _ffn_fwd(h3, row(p["norm_ffn"][1]), w_ffn1_1, w_ffn2_1, "1")

    loss, dh, g["final_norm"] = _loss_head(h4, row(p["final_norm"]), target, name="loss_head")

    dh, dnf1, dw1_1, dw2_1 = _ffn_bwd(dh, h3, row(p["norm_ffn"][1]), w_ffn1_1, w_ffn2_1, ffn1, 1)
    dcat1 = _mm_nt(dh, w_out1, name="out_dx_1")
    dwo_1 = _mm_tn_stacked(cat1, dh, name="out_dw_1", col_slots=False)
    dproj_b, dk1, dv1 = _attn_bwd(proj_b, B_Q_OFF, kv1, dcat1, B_IN_PAD, B_Q_OFF, name="attn_bwd_1")
    (dproj_b, dxs, dbm, dcm, ddt_c, ddt_r, g["b_dt_bias"], g["b_a_log"], g["b_d"], g["b_gnorm"]) = _ssd_bwd(
        xbc, proj_b, dt_c, dt_r, *ssd_par, hprev, dcat1, dproj_b, name="ssd_bwd")
    dproj_b, g["b_conv_w"], g["b_conv_b"] = _conv_bwd(proj_b, p["b_conv_w"], p["b_conv_b"], dxs, dbm, dcm, dproj_b,
                                                      name="conv_bwd")
    ddt = jnp.transpose(ddt_c, (1, 0, 2)) + jnp.transpose(ddt_r, (2, 0, 1))
    ddt = jnp.pad(ddt.reshape(SEQ, SSM_HEADS), ((0, 0), (0, B_IN_PAD - B_DT_OFF - SSM_HEADS))).astype(BF16)
    dproj_b = lax.dynamic_update_slice(dproj_b, ddt, (0, B_DT_OFF))
    dwkv_1, dmn1 = _kv_bwd(mem, row(p["mem_norm"][1]), w_kv1, m1, dk1, dv1, 1)
    dwb = _b_in_grad_slots(_mm_tn(a1, dproj_b, name="b_in_dw"))
    da1 = _mm_nt(dproj_b, b_in, name="b_in_dx")
    dh, dnm1 = _rms_bwd(h2, row(p["norm_mix"][1]), da1, dh, name="mix_norm_bwd_1")
    layer1 = dict(w_kv=dwkv_1, w_out=dwo_1, w_ffn1=dw1_1, w_ffn2=dw2_1, b_in=dwb)
    token = () if after_layer1 is None else (after_layer1(layer1),)

    dh, dnf0, dw1_0, dw2_0 = _ffn_bwd(dh, h1, row(p["norm_ffn"][0]), p["w_ffn1"][0], p["w_ffn2"][0], ffn0, 0,
                                      after=token)
    ffn0_grads = dict(w_ffn1=dw1_0, w_ffn2=dw2_0)
    token = () if after_ffn0 is None else (after_ffn0(ffn0_grads),)
    dcat0 = _mm_nt(dh, p["w_out"][0], name="out_dx_0", after=token)
    dwo_0 = _mm_tn_stacked(cat0, dh, name="out_dw_0", col_slots=False)
    dproj_a, dk0, dv0 = _attn_bwd(proj_a, 2 * D_INNER, kv0, dcat0, A_IN, 2 * D_INNER, name="attn_bwd_0")
    dproj_a, g["a_ln_g"], g["a_ln_b"], g["a_ws"], dbs_col = _gate_bwd(
        proj_a, p["a_ln_g"], p["a_ln_b"], p["a_ws"], bs_col, dcat0, dproj_a, name="gate_bwd")
    g["a_bs"] = dbs_col.reshape(A_GROUPS, CHUNK)
    dwkv_0, dmn0 = _kv_bwd(mem, row(p["mem_norm"][0]), p["w_kv"][0], m0, dk0, dv0, 0)
    dwa = _mm_tn_stacked(a0, dproj_a, name="a_in_dw", col_slots=True)
    mixer0_grads = dict(w_kv=dwkv_0, w_out=dwo_0, a_in=dwa)
    token = () if after_mixer0 is None else (after_mixer0(mixer0_grads),)
    da0 = _mm_nt(dproj_a, p["a_in"], name="a_in_dx", after=token)
    dx, dnm0 = _rms_bwd(h0, row(p["norm_mix"][0]), da0, dh, name="mix_norm_bwd_0")

    g["norm_mix"] = jnp.concatenate([dnm0, dnm1], axis=0)
    g["norm_ffn"] = jnp.concatenate([dnf0, dnf1], axis=0)
    g["mem_norm"] = jnp.concatenate([dmn0, dmn1], axis=0)
    layer0 = dict(w_kv=dwkv_0, w_out=dwo_0, w_ffn1=dw1_0, w_ffn2=dw2_0, a_in=dwa)
    return loss, dx, g, layer0, layer1


def _b_in_full(gathered):
    n = B_IN // N_CHIPS
    dt0 = D_INNER + CONV_DIM - (N_CHIPS - 1) * n
    last = gathered[N_CHIPS - 1]
    return jnp.concatenate([*[gathered[k] for k in range(N_CHIPS - 1)], last[:, :dt0], last[:, dt0 + SSM_HEADS:],
                            last[:, dt0:dt0 + SSM_HEADS], jnp.zeros((D_MODEL, B_IN_PAD - B_IN), last.dtype)], axis=1)


def _b_in_grad_slots(d):
    n = B_IN // N_CHIPS
    dt0 = D_INNER + CONV_DIM
    last = jnp.concatenate([d[:, (N_CHIPS - 1) * n:dt0], d[:, B_DT_OFF:B_DT_OFF + SSM_HEADS], d[:, dt0:B_DT_OFF]], axis=1)
    slots = [*[d[:, k * n:(k + 1) * n] for k in range(N_CHIPS - 1)], last]
    half = D_MODEL // 2
    return jnp.stack([jnp.stack([s[h * half:(h + 1) * half] for s in slots]) for h in range(2)])


LARGE = ("w_kv", "w_out", "w_ffn1", "w_ffn2", "a_in", "b_in")
SMALL_REPL = ("norm_mix", "norm_ffn", "mem_norm", "a_ln_g", "a_ln_b", "a_ws", "a_bs", "b_dt_bias", "b_a_log", "b_d",
              "final_norm")
SMALL_SHARD = ("b_conv_w", "b_conv_b", "b_gnorm")
WEIGHTS = ("norm_mix", "norm_ffn", "mem_norm", "w_kv", "w_out", "w_ffn1", "w_ffn2", "a_in", "a_ln_g", "a_ln_b", "a_ws",
           "a_bs", "b_in", "b_conv_w", "b_conv_b", "b_dt_bias", "b_a_log", "b_d", "b_gnorm", "final_norm")
CONV_SHARD = CONV_DIM // N_CHIPS
GN_SHARD = D_INNER // N_CHIPS


LAYERED = ("w_kv", "w_out", "w_ffn1", "w_ffn2")
LAYER_TENSORS = (("w_kv", "w_out", "w_ffn1", "w_ffn2", "a_in"), ("w_kv", "w_out", "w_ffn1", "w_ffn2", "b_in"))


def _gather_weights(w):
    halves = lambda k, layer: (w[k][layer] if k in LAYERED else w[k][0]).reshape(2, -1, w[k].shape[-1]).astype(BF16)
    small = jnp.zeros((2, CONV_K, CONV_SHARD), F32)
    small = small.at[0].set(w["b_conv_w"][0])
    small = small.at[1, 0].set(w["b_conv_b"][0])
    small = small.at[1, 1, :GN_SHARD].set(w["b_gnorm"][0])
    gathered = _all_gather_shards([halves(k, 0) for k in LAYER_TENSORS[0]], small, name="gather_weights_0")
    got = dict(zip(LAYER_TENSORS[0], gathered))
    slots = lambda a: a.reshape(N_CHIPS, -1, a.shape[-1])
    rows = lambda a: a.reshape(-1, a.shape[-1])
    p = dict(w_kv=[slots(got["w_kv"])], w_out=[rows(got["w_out"])], w_ffn1=[slots(got["w_ffn1"])],
             w_ffn2=[rows(got["w_ffn2"])], a_in=slots(got["a_in"]))
    sm = gathered[-1]
    p["b_conv_w"] = jnp.transpose(sm[:, 0], (1, 0, 2)).reshape(CONV_K, CONV_DIM)
    p["b_conv_b"] = sm[:, 1, 0].reshape(1, CONV_DIM)
    p["b_gnorm"] = sm[:, 1, 1, :GN_SHARD].reshape(1, D_INNER)

    after, started = (gathered[0],), {}
    for tag, names in (("mixer", ("w_kv", "w_out", "b_in")), ("ffn", ("w_ffn1", "w_ffn2"))):
        started[tag] = _gather_start([halves(k, 1) for k in names], after, name=f"gather_start_1_{tag}")
        after = (started[tag][-1],)
    p["after_start"] = after

    def finish(tag, first):
        send_sems, recv_sems, shards, zones, _ = started[tag]
        shards, zones = _gather_wait(send_sems, recv_sems, shards, zones, (first,), name=f"gather_wait_1_{tag}")
        return _gather_finish(shards, zones, name=f"gather_finish_1_{tag}")

    def layer1_mixer(first):
        kv, wo, b_in = finish("mixer", first)
        return slots(kv), rows(wo), _b_in_full(slots(b_in))

    def layer1_ffn(first):
        w1, w2 = finish("ffn", first)
        return slots(w1), rows(w2)

    p.update(layer1_mixer=layer1_mixer, layer1_ffn=layer1_ffn)
    return p


def _pair_parts(grads, tag):
    stacks = [g.reshape(2, -1, g.shape[-1]) for g in grads.values()]
    parts = _pair_reduce(stacks, name=f"grads_pair_reduce_{tag}")
    return [t.reshape(N_CHIPS, -1, t.shape[-1]) for t in parts]


def _chip_sums(chip, names, parts, landed, tag):
    return {k: _sum_contributions(chip, t, u, name=f"grads_chip_sum_{k}_{tag}")
            for k, t, u in zip(names, parts, landed)}


def _small_layout(shapes):
    offs, o = {}, 0
    for k in (*SMALL_REPL, *SMALL_SHARD):
        size = math.prod(shapes[k])
        offs[k] = (o, size)
        o += size
    rows = -(-o // (8 * 128)) * 8
    return offs, rows


def _reduce_small(g, full_shapes):
    offs, rows = _small_layout(full_shapes)
    flat = jnp.concatenate([g[k].reshape(-1) for k in (*SMALL_REPL, *SMALL_SHARD)])
    flat = jnp.pad(flat, (0, rows * 128 - flat.shape[0])).reshape(rows, 128)
    total = _all_reduce_small(flat, name="grads_small_all_reduce").reshape(-1)
    return {k: total[o:o + n].reshape(full_shapes[k]) for k, (o, n) in offs.items()}


def kernel(x, mem, norm_mix, norm_ffn, mem_norm, w_kv, w_out, w_ffn1, w_ffn2, a_in, a_ln_g, a_ln_b, a_ws, a_bs, b_in, b_conv_w, b_conv_b, b_dt_bias, b_a_log, b_d, b_gnorm, final_norm, loss_target, m_norm_mix, m_norm_ffn, m_mem_norm, m_w_kv, m_w_out, m_w_ffn1, m_w_ffn2, m_a_in, m_a_ln_g, m_a_ln_b, m_a_ws, m_a_bs, m_b_in, m_b_conv_w, m_b_conv_b, m_b_dt_bias, m_b_a_log, m_b_d, m_b_gnorm, m_final_norm, v_norm_mix, v_norm_ffn, v_mem_norm, v_w_kv, v_w_out, v_w_ffn1, v_w_ffn2, v_a_in, v_a_ln_g, v_a_ln_b, v_a_ws, v_a_bs, v_b_in, v_b_conv_w, v_b_conv_b, v_b_dt_bias, v_b_a_log, v_b_d, v_b_gnorm, v_final_norm):
    w = dict(norm_mix=norm_mix, norm_ffn=norm_ffn, mem_norm=mem_norm, w_kv=w_kv, w_out=w_out, w_ffn1=w_ffn1,
             w_ffn2=w_ffn2, a_in=a_in, a_ln_g=a_ln_g, a_ln_b=a_ln_b, a_ws=a_ws, a_bs=a_bs, b_in=b_in, b_conv_w=b_conv_w,
             b_conv_b=b_conv_b, b_dt_bias=b_dt_bias, b_a_log=b_a_log, b_d=b_d, b_gnorm=b_gnorm, final_norm=final_norm)
    mom = dict(norm_mix=m_norm_mix, norm_ffn=m_norm_ffn, mem_norm=m_mem_norm, w_kv=m_w_kv, w_out=m_w_out,
               w_ffn1=m_w_ffn1, w_ffn2=m_w_ffn2, a_in=m_a_in, a_ln_g=m_a_ln_g, a_ln_b=m_a_ln_b, a_ws=m_a_ws,
               a_bs=m_a_bs, b_in=m_b_in, b_conv_w=m_b_conv_w, b_conv_b=m_b_conv_b, b_dt_bias=m_b_dt_bias,
               b_a_log=m_b_a_log, b_d=m_b_d, b_gnorm=m_b_gnorm, final_norm=m_final_norm)
    var = dict(norm_mix=v_norm_mix, norm_ffn=v_norm_ffn, mem_norm=v_mem_norm, w_kv=v_w_kv, w_out=v_w_out,
               w_ffn1=v_w_ffn1, w_ffn2=v_w_ffn2, a_in=v_a_in, a_ln_g=v_a_ln_g, a_ln_b=v_a_ln_b, a_ws=v_a_ws,
               a_bs=v_a_bs, b_in=v_b_in, b_conv_w=v_b_conv_w, b_conv_b=v_b_conv_b, b_dt_bias=v_b_dt_bias,
               b_a_log=v_b_a_log, b_d=v_b_d, b_gnorm=v_b_gnorm, final_norm=v_final_norm)

    p = _gather_weights(w)
    p.update(norm_mix=norm_mix, norm_ffn=norm_ffn, mem_norm=mem_norm, a_ln_g=a_ln_g, a_ln_b=a_ln_b, a_ws=a_ws[0],
             a_bs=a_bs[0], b_dt_bias=b_dt_bias, b_a_log=b_a_log, b_d=b_d, final_norm=final_norm)
    chip = 2 * lax.axis_index("x") + lax.axis_index("y")
    chip_arr = jnp.reshape(chip, (1,)).astype(jnp.int32)
    started = {}

    def start_scatter(tag):
        def hook(grads):
            start = _chip_scatter_start(_pair_parts(grads, tag), name=f"grads_chip_scatter_start_{tag}")
            started[tag] = (tuple(grads), start)
            return start[-1]
        return hook

    loss_part, dx, g, _, _ = _local_step(x[0], mem[0], loss_target[0], p, start_scatter("1"), start_scatter("0f"),
                                         start_scatter("0m"))
    loss = lax.psum(loss_part[0, 0], ("x", "y", "c"))

    def finish_scatter(tag, first):
        names, (send_sems, recv_sems, parts, lands, _) = started[tag]
        parts, landed = _chip_scatter_wait(send_sems, recv_sems, parts, lands, (first,),
                                           name=f"grads_chip_scatter_wait_{tag}")
        return _chip_sums(chip_arr, names, parts, landed, tag)

    def adamw(names, grads):
        for k in names:
            shape = w[k].shape
            flat = (lambda a: a.reshape(-1, shape[-1])) if len(shape) > 1 else (lambda a: a.reshape(1, -1))
            d, m_new, v_new = _adamw(flat(w[k]), flat(grads[k]), flat(mom[k]), flat(var[k]), name=f"adamw_{k}")
            delta[k], new_m[k], new_v[k] = d.reshape(shape), m_new.reshape(shape), v_new.reshape(shape)

    full_shapes = {k: w[k].shape for k in SMALL_REPL}
    full_shapes.update(b_conv_w=(1, CONV_K, CONV_DIM), b_conv_b=(1, CONV_DIM), b_gnorm=(1, D_INNER))
    grads = _reduce_small(g, full_shapes)
    grads["b_conv_w"] = lax.dynamic_slice_in_dim(grads["b_conv_w"], chip * CONV_SHARD, CONV_SHARD, axis=2)
    grads["b_conv_b"] = lax.dynamic_slice_in_dim(grads["b_conv_b"], chip * CONV_SHARD, CONV_SHARD, axis=1)
    grads["b_gnorm"] = lax.dynamic_slice_in_dim(grads["b_gnorm"], chip * GN_SHARD, GN_SHARD, axis=1)
    delta, new_m, new_v = {}, {}, {}
    halves = [finish_scatter("0f", dx), finish_scatter("1", dx)]
    early = ("w_ffn1", "w_ffn2", "b_in")
    shared = _pair_share([[halves[layer][k] for layer in range(2) if k in halves[layer]] for k in early],
                         name="grads_pair_share_early")
    grads.update({k: a.reshape(w[k].shape) for k, a in zip(early, shared)})
    adamw([k for k in WEIGHTS if k in grads], grads)
    halves[0].update(finish_scatter("0m", delta["b_in"]))
    late = ("w_kv", "w_out", "a_in")
    shared = _pair_share([[halves[layer][k] for layer in range(2) if k in halves[layer]] for k in late],
                         name="grads_pair_share_late")
    grads.update({k: a.reshape(w[k].shape) for k, a in zip(late, shared)})
    adamw(late, grads)

    return (loss, dx.reshape(x.shape), *[grads[k] for k in WEIGHTS], *[delta[k] for k in WEIGHTS],
            *[new_m[k] for k in WEIGHTS], *[new_v[k] for k in WEIGHTS])
```

```python
import math

import jax
import jax.numpy as jnp
from jax import lax
from jax.experimental import pallas as pl
from jax.experimental.pallas import tpu as pltpu

F32 = jnp.float32
BF16 = jnp.bfloat16
SDS = jax.ShapeDtypeStruct

D_MODEL = 1024
SEQ = 2048
CHUNK = 128
N_MEM = 256
D_INNER = 2048
A_GROUPS = 8
A_GROUP_W = D_INNER // A_GROUPS
SSM_HEADS = 32
SSM_HEAD_DIM = 64
SSM_GROUPS = 4
SSM_HPG = 8
SSM_STATE = 128
SSM_GROUP_W = SSM_HPG * SSM_HEAD_DIM
CONV_K = 4
CONV_DIM = 3072
X_HEADS = 4
X_HEAD_DIM = 256
X_WIDTH = 1024
MIX_OUT = 3072
D_FF = 4096
A_IN = 5120
B_IN = 6176
B_IN_PAD = 6272
B_Q_OFF = 5120
B_DT_OFF = 6144
N_CHUNKS = SEQ // CHUNK
EPS = 1e-6
N_CHIPS = 4

ADAM_LR = 0.001
ADAM_B1 = 0.9
ADAM_B2 = 0.999
ADAM_EPS = 1e-08
ADAM_WD = 0.01
ADAM_STEP = 10

VMEM_LIMIT = 48 * 1024 * 1024
MESH = pl.DeviceIdType.MESH


def _cparams(sem):
    return pltpu.CompilerParams(dimension_semantics=sem, vmem_limit_bytes=VMEM_LIMIT)


def _dot(a, b, dims=(((1,), (0,)), ((), ()))):
    return lax.dot_general(a.astype(BF16), b.astype(BF16), dims, preferred_element_type=F32)


def _dot_nt(a, b):
    return _dot(a, b, (((1,), (1,)), ((), ())))


def _dot_tn(a, b):
    return _dot(a, b, (((0,), (0,)), ((), ())))


def _pick(n, cands):
    for c in cands:
        if n % c == 0:
            return c
    raise ValueError(f"no tile for {n}")


def _mm_call(a, b, *, dims, grid, a_spec, b_spec, acc_shape, out_shapes, out_specs, name,
             extras=(), extra_specs=(), epilogue=None, after=()):
    n_k = grid[2]
    n_extra = len(extras)
    n_out = len(out_shapes)
    n_in = 2 + n_extra + len(after)

    def finish(total, extra_refs, out_refs):
        vals = (total,) if epilogue is None else epilogue(total, *[e[...] for e in extra_refs])
        for o_ref, v in zip(out_refs, vals):
            o_ref[...] = v.astype(o_ref.dtype)

    def body_one_step(*refs):
        finish(_dot(refs[0][...], refs[1][...], dims), refs[2:2 + n_extra], refs[n_in:n_in + n_out])

    def body(*refs):
        acc = refs[-1]
        k = pl.program_id(2)

        @pl.when(k == 0)
        def _():
            acc[...] = jnp.zeros_like(acc)

        acc[...] += _dot(refs[0][...], refs[1][...], dims)

        @pl.when(k == n_k - 1)
        def _():
            finish(acc[...], refs[2:2 + n_extra], refs[n_in:n_in + n_out])

    return pl.pallas_call(
        body_one_step if n_k == 1 else body, grid=grid,
        in_specs=[a_spec, b_spec, *extra_specs, *([ANY] * len(after))], out_specs=list(out_specs),
        out_shape=list(out_shapes), scratch_shapes=[] if n_k == 1 else [pltpu.VMEM(acc_shape, F32)],
        compiler_params=_cparams(("parallel", "parallel", "arbitrary")), name=name,
    )(a, b, *extras, *after)


def _w_dims(w):
    if w.ndim == 2:
        return w.shape[0], w.shape[1], 1, w.shape[1]
    return w.shape[1], w.shape[0] * w.shape[2], w.shape[0], w.shape[2]


def _mm_nn(a, w, *, name, out_dtype=F32, a_cols=None, extras=(), epilogue=None, n_out_dtypes=None, after=()):
    m = a.shape[0]
    k_dim, n_dim, _, n_slot = _w_dims(w)
    a_off, a_w = (0, a.shape[1]) if a_cols is None else a_cols
    assert a_w == k_dim
    tm = _pick(m, (2048, 1024, 512, 256))
    tn = _pick(n_slot, (512, 896, 640, 256, 128))
    tk = _pick(k_dim, (1024, 768, 512, 384, 256, 128))
    assert a_off % tk == 0
    nb = n_slot // tn
    a_spec = pl.BlockSpec((tm, tk), lambda i, j, k: (i, a_off // tk + k))
    if w.ndim == 2:
        b_spec = pl.BlockSpec((tk, tn), lambda i, j, k: (k, j))
    else:
        b_spec = pl.BlockSpec((None, tk, tn), lambda i, j, k: (j // nb, k, j % nb))
    o_spec = pl.BlockSpec((tm, tn), lambda i, j, k: (i, j))
    dts = n_out_dtypes or (out_dtype,)
    outs = _mm_call(a, w, dims=(((1,), (0,)), ((), ())), grid=(m // tm, n_dim // tn, k_dim // tk),
                    a_spec=a_spec, b_spec=b_spec, acc_shape=(tm, tn),
                    out_shapes=[SDS((m, n_dim), dt) for dt in dts], out_specs=[o_spec] * len(dts), name=name,
                    extras=extras, extra_specs=[o_spec] * len(extras), epilogue=epilogue, after=after)
    return outs if n_out_dtypes else outs[0]


def _mm_nt(a, w, *, name, out_dtype=F32, extras=(), epilogue=None, after=()):
    m = a.shape[0]
    k_dim, n_dim, _, n_slot = _w_dims(w)
    assert a.shape[1] == n_dim
    tm = _pick(m, (2048, 1024, 512, 256))
    to = _pick(k_dim, (512, 384, 256, 128))
    tc = _pick(n_slot, (1280, 1024, 896, 640, 512, 256, 128))
    nb = n_slot // tc
    a_spec = pl.BlockSpec((tm, tc), lambda i, j, k: (i, k))
    if w.ndim == 2:
        b_spec = pl.BlockSpec((to, tc), lambda i, j, k: (j, k))
    else:
        b_spec = pl.BlockSpec((None, to, tc), lambda i, j, k: (k // nb, j, k % nb))
    o_spec = pl.BlockSpec((tm, to), lambda i, j, k: (i, j))
    return _mm_call(a, w, dims=(((1,), (1,)), ((), ())), grid=(m // tm, k_dim // to, n_dim // tc),
                    a_spec=a_spec, b_spec=b_spec, acc_shape=(tm, to),
                    out_shapes=[SDS((m, k_dim), out_dtype)], out_specs=[o_spec], name=name,
                    extras=extras, extra_specs=[o_spec] * len(extras), epilogue=epilogue, after=after)[0]


def _mm_tn(x, dy, *, name, x_cols=None):
    s = x.shape[0]
    x_off, k_dim = (0, x.shape[1]) if x_cols is None else x_cols
    n_dim = dy.shape[1]
    tm = _pick(k_dim, (1024, 768, 512, 384, 256, 128))
    tn = _pick(n_dim, (512, 896, 640, 256, 128))
    tk = _pick(s, (2048, 1024, 512, 256))
    assert x_off % tm == 0
    a_spec = pl.BlockSpec((tk, tm), lambda i, j, k: (k, x_off // tm + i))
    b_spec = pl.BlockSpec((tk, tn), lambda i, j, k: (k, j))
    o_spec = pl.BlockSpec((tm, tn), lambda i, j, k: (i, j))
    return _mm_call(x, dy, dims=(((0,), (0,)), ((), ())), grid=(k_dim // tm, n_dim // tn, s // tk),
                    a_spec=a_spec, b_spec=b_spec, acc_shape=(tm, tn),
                    out_shapes=[SDS((k_dim, n_dim), F32)], out_specs=[o_spec], name=name)[0]


def _mm_tn_stacked(x, dy, *, name, col_slots):
    s, k_dim = x.shape
    n_dim = dy.shape[1]
    r, c = (k_dim // 2, n_dim // N_CHIPS) if col_slots else (k_dim // N_CHIPS // 2, n_dim)
    tm = 2 * r
    tn = _pick(c, (512, 896, 640, 256, 128))
    tk = _pick(s, (2048, 1024, 512, 256))
    a_spec = pl.BlockSpec((tk, tm), lambda i, j, k: (k, i))
    b_spec = pl.BlockSpec((tk, tn), lambda i, j, k: (k, j))
    if col_slots:
        nb = c // tn
        o_spec = pl.BlockSpec((2, None, r, tn), lambda i, j, k: (0, j // nb, 0, j % nb))
    else:
        o_spec = pl.BlockSpec((2, None, r, tn), lambda i, j, k: (0, i, 0, j))
    return _mm_call(x, dy, dims=(((0,), (0,)), ((), ())), grid=(k_dim // tm, n_dim // tn, s // tk),
                    a_spec=a_spec, b_spec=b_spec, acc_shape=(tm, tn), epilogue=lambda acc: (acc.reshape(2, r, tn),),
                    out_shapes=[SDS((2, N_CHIPS, r, c), F32)], out_specs=[o_spec], name=name)[0]


def _rms(x, g):
    return x * lax.rsqrt(jnp.mean(x * x, axis=-1, keepdims=True) + EPS) * g


def _rms_fwd(h, g, *, name):
    rows, d = h.shape
    tr = _pick(rows, (512, 256))

    def body(h_ref, g_ref, o_ref):
        o_ref[...] = _rms(h_ref[...], g_ref[...]).astype(o_ref.dtype)

    return pl.pallas_call(
        body, grid=(rows // tr,),
        in_specs=[pl.BlockSpec((tr, d), lambda i: (i, 0)), pl.BlockSpec((1, d), lambda i: (0, 0))],
        out_specs=pl.BlockSpec((tr, d), lambda i: (i, 0)), out_shape=SDS((rows, d), BF16),
        compiler_params=_cparams(("parallel",)), name=name)(h, g)


def _rms_bwd(h, g, da, dres, *, name):
    rows, d = h.shape
    tr = _pick(rows, (512, 256))

    def body(h_ref, g_ref, da_ref, dres_ref, dh_ref, dg_ref):
        _, vjp = jax.vjp(_rms, h_ref[...], g_ref[...])
        dh, dg = vjp(da_ref[...].astype(F32))
        dh_ref[...] = dres_ref[...] + dh

        @pl.when(pl.program_id(0) == 0)
        def _():
            dg_ref[...] = jnp.zeros_like(dg_ref)

        dg_ref[...] += dg

    row_spec = pl.BlockSpec((tr, d), lambda i: (i, 0))
    vec_spec = pl.BlockSpec((1, d), lambda i: (0, 0))
    return pl.pallas_call(
        body, grid=(rows // tr,), in_specs=[row_spec, vec_spec, row_spec, row_spec],
        out_specs=[row_spec, vec_spec], out_shape=[SDS((rows, d), F32), SDS((1, d), F32)],
        compiler_params=_cparams(("arbitrary",)), name=name)(h, g, da, dres)


def _loss_head(h, g, target, *, name):
    rows, d = h.shape
    tr = _pick(rows, (512, 256))

    def body(h_ref, g_ref, t_ref, loss_ref, dh_ref, dg_ref):
        y, vjp = jax.vjp(_rms, h_ref[...], g_ref[...])
        err = y - t_ref[...]
        dh, dg = vjp(err * (1.0 / d))
        dh_ref[...] = dh

        @pl.when(pl.program_id(0) == 0)
        def _():
            dg_ref[...] = jnp.zeros_like(dg_ref)
            loss_ref[...] = jnp.zeros_like(loss_ref)

        dg_ref[...] += dg
        part = jnp.sum(jnp.sum(err * err, axis=-1, keepdims=True), axis=0, keepdims=True) * (0.5 / d)
        loss_ref[...] += jnp.broadcast_to(part, loss_ref.shape)

    row_spec = pl.BlockSpec((tr, d), lambda i: (i, 0))
    vec_spec = pl.BlockSpec((1, d), lambda i: (0, 0))
    loss_spec = pl.BlockSpec((8, 128), lambda i: (0, 0))
    return pl.pallas_call(
        body, grid=(rows // tr,), in_specs=[row_spec, vec_spec, row_spec],
        out_specs=[loss_spec, row_spec, vec_spec],
        out_shape=[SDS((8, 128), F32), SDS((rows, d), F32), SDS((1, d), F32)],
        compiler_params=_cparams(("arbitrary",)), name=name)(h, g, target)


def _gelu(x):
    return 0.5 * x * (1.0 + lax.erf(x * (1.0 / math.sqrt(2.0))))


def _gate_tile(pu, pv, ln_g, ln_b, ws, bs_t):
    u = [_gelu(p) for p in pu]
    v = [_gelu(p) for p in pv]
    mu = sum(jnp.sum(t, axis=-1, keepdims=True) for t in v) * (1.0 / D_INNER)
    vc = [t - mu for t in v]
    var = sum(jnp.sum(t * t, axis=-1, keepdims=True) for t in vc) * (1.0 / D_INNER)
    rstd = lax.rsqrt(var + EPS)
    row = lax.broadcasted_iota(jnp.int32, (CHUNK, CHUNK), 0)
    col = lax.broadcasted_iota(jnp.int32, (CHUNK, CHUNK), 1)
    out = []
    for gi in range(A_GROUPS):
        vn = vc[gi] * rstd * ln_g[gi] + ln_b[gi]
        w = jnp.where(row >= col, ws[gi], 0.0)
        sv = _dot(w, vn) + bs_t[gi]
        out.append(u[gi] * sv)
    return out


def _split(ref, n, width):
    return [ref[:, i * width:(i + 1) * width] for i in range(n)]


def _gate_in_specs():
    return [
        pl.BlockSpec((CHUNK, D_INNER), lambda c: (c, 0)),
        pl.BlockSpec((CHUNK, D_INNER), lambda c: (c, 1)),
        pl.BlockSpec((1, D_INNER), lambda c: (0, 0)),
        pl.BlockSpec((1, D_INNER), lambda c: (0, 0)),
        pl.BlockSpec((A_GROUPS, CHUNK, CHUNK), lambda c: (0, 0, 0)),
        pl.BlockSpec((A_GROUPS, CHUNK, 1), lambda c: (0, 0, 0)),
    ]


def _gate_args(u_ref, v_ref, g_ref, b_ref, ws_ref, bs_ref):
    ng, gw = A_GROUPS, A_GROUP_W
    return (_split(u_ref, ng, gw), _split(v_ref, ng, gw), _split(g_ref, ng, gw), _split(b_ref, ng, gw),
            [ws_ref[i] for i in range(ng)], [bs_ref[i] for i in range(ng)])


def _gate_fwd(proj, ln_g, ln_b, ws, bs_col, mixcat, *, name):
    def body(u_ref, v_ref, g_ref, b_ref, ws_ref, bs_ref, cat_in, cat_ref):
        del cat_in
        out = _gate_tile(*_gate_args(u_ref, v_ref, g_ref, b_ref, ws_ref, bs_ref))
        for gi, o in enumerate(out):
            cat_ref[:, gi * A_GROUP_W:(gi + 1) * A_GROUP_W] = o.astype(cat_ref.dtype)

    return pl.pallas_call(
        body, grid=(N_CHUNKS,), in_specs=[*_gate_in_specs(), pl.BlockSpec(memory_space=pl.ANY)],
        out_specs=pl.BlockSpec((CHUNK, D_INNER), lambda c: (c, 0)), out_shape=SDS(mixcat.shape, mixcat.dtype),
        input_output_aliases={6: 0}, compiler_params=_cparams(("parallel",)), name=name,
    )(proj, proj, ln_g, ln_b, ws, bs_col, mixcat)


def _gate_bwd(proj, ln_g, ln_b, ws, bs_col, dcat, dproj, *, name):
    ng, gw = A_GROUPS, A_GROUP_W

    def body(u_ref, v_ref, g_ref, b_ref, ws_ref, bs_ref, d_ref, dproj_in, dproj_ref, dg_ref, db_ref, dws_ref, dbs_ref):
        del dproj_in
        args = _gate_args(u_ref, v_ref, g_ref, b_ref, ws_ref, bs_ref)
        _, vjp = jax.vjp(_gate_tile, *args)
        dpu, dpv, dg, db, dws, dbs = vjp(_split(d_ref, ng, gw))
        for gi in range(ng):
            dproj_ref[:, gi * gw:(gi + 1) * gw] = dpu[gi].astype(dproj_ref.dtype)
            dproj_ref[:, D_INNER + gi * gw:D_INNER + (gi + 1) * gw] = dpv[gi].astype(dproj_ref.dtype)

        @pl.when(pl.program_id(0) == 0)
        def _():
            for r in (dg_ref, db_ref, dws_ref, dbs_ref):
                r[...] = jnp.zeros_like(r)

        for gi in range(ng):
            dg_ref[:, gi * gw:(gi + 1) * gw] += dg[gi]
            db_ref[:, gi * gw:(gi + 1) * gw] += db[gi]
            dws_ref[gi] += dws[gi]
            dbs_ref[gi] += dbs[gi]

    in_specs = _gate_in_specs()
    return pl.pallas_call(
        body, grid=(N_CHUNKS,),
        in_specs=[*in_specs, pl.BlockSpec((CHUNK, D_INNER), lambda c: (c, 0)), pl.BlockSpec(memory_space=pl.ANY)],
        out_specs=[pl.BlockSpec((CHUNK, 2 * D_INNER), lambda c: (c, 0)), *in_specs[2:]],
        out_shape=[SDS(dproj.shape, dproj.dtype), SDS((1, D_INNER), F32), SDS((1, D_INNER), F32),
                   SDS((ng, CHUNK, CHUNK), F32), SDS((ng, CHUNK, 1), F32)],
        input_output_aliases={7: 0}, compiler_params=_cparams(("arbitrary",)), name=name,
    )(proj, proj, ln_g, ln_b, ws, bs_col, dcat, dproj)


ATT_TQ = 512


def _attn_tile(q, k, v):
    s = _dot_nt(q, k) * (1.0 / math.sqrt(X_HEAD_DIM))
    s = s - jnp.max(s, axis=-1, keepdims=True)
    e = jnp.exp(s)
    p = e / jnp.sum(e, axis=-1, keepdims=True)
    return _dot(p, v)


def _attn_in_specs(q_blk, order):
    hd = X_HEAD_DIM
    return [
        pl.BlockSpec((ATT_TQ, hd), lambda a, b: (order(a, b)[0], q_blk + order(a, b)[1])),
        pl.BlockSpec((N_MEM, hd), lambda a, b: (0, order(a, b)[1])),
        pl.BlockSpec((N_MEM, hd), lambda a, b: (0, X_HEADS + order(a, b)[1])),
    ]


def _attn_fwd(proj, q_off, kv, *, name):
    order = lambda i, h: (i, h)
    cat_blk = D_INNER // X_HEAD_DIM

    def body(q_ref, k_ref, v_ref, o_ref):
        o_ref[...] = _attn_tile(q_ref[...], k_ref[...], v_ref[...]).astype(o_ref.dtype)

    return pl.pallas_call(
        body, grid=(SEQ // ATT_TQ, X_HEADS), in_specs=_attn_in_specs(q_off // X_HEAD_DIM, order),
        out_specs=pl.BlockSpec((ATT_TQ, X_HEAD_DIM), lambda i, h: (i, cat_blk + h)),
        out_shape=SDS((SEQ, MIX_OUT), BF16), compiler_params=_cparams(("parallel", "parallel")), name=name,
    )(proj, kv, kv)


def _attn_bwd(proj, q_off, kv, dcat, dproj_width, dq_off, *, name):
    order = lambda h, i: (i, h)
    cat_blk = D_INNER // X_HEAD_DIM
    dq_blk = dq_off // X_HEAD_DIM

    def body(q_ref, k_ref, v_ref, do_ref, dq_ref, dk_ref, dv_ref):
        _, vjp = jax.vjp(_attn_tile, q_ref[...], k_ref[...], v_ref[...])
        dq, dk, dv = vjp(do_ref[...])
        dq_ref[...] = dq.astype(dq_ref.dtype)

        @pl.when(pl.program_id(1) == 0)
        def _():
            dk_ref[...] = jnp.zeros_like(dk_ref)
            dv_ref[...] = jnp.zeros_like(dv_ref)

        dk_ref[...] += dk
        dv_ref[...] += dv

    kv_spec = pl.BlockSpec((N_MEM, X_HEAD_DIM), lambda h, i: (0, h))
    return pl.pallas_call(
        body, grid=(X_HEADS, SEQ // ATT_TQ),
        in_specs=[*_attn_in_specs(q_off // X_HEAD_DIM, order),
                  pl.BlockSpec((ATT_TQ, X_HEAD_DIM), lambda h, i: (i, cat_blk + h))],
        out_specs=[pl.BlockSpec((ATT_TQ, X_HEAD_DIM), lambda h, i: (i, dq_blk + h)), kv_spec, kv_spec],
        out_shape=[SDS((SEQ, dproj_width), BF16), SDS((N_MEM, X_WIDTH), F32), SDS((N_MEM, X_WIDTH), F32)],
        compiler_params=_cparams(("parallel", "arbitrary")), name=name,
    )(proj, kv, kv, dcat)


CONV_TC = 512


def _shift_down(x, s):
    if s == 0:
        return x
    row = lax.broadcasted_iota(jnp.int32, x.shape, 0)
    return jnp.where(row >= s, pltpu.roll(x, s, 0), 0.0)


def _shift_up(x, s):
    if s == 0:
        return x
    n = x.shape[0]
    row = lax.broadcasted_iota(jnp.int32, x.shape, 0)
    return jnp.where(row < n - s, pltpu.roll(x, n - s, 0), 0.0)


def _conv_pre(x, w_ref, b_ref):
    pre = b_ref[...] + jnp.zeros_like(x)
    for k in range(CONV_K):
        pre = pre + w_ref[k:k + 1, :] * _shift_down(x, CONV_K - 1 - k)
    return pre


def _conv_fwd(proj, w, b, *, name):
    blk0 = D_INNER // CONV_TC

    def body(x_ref, w_ref, b_ref, o_ref):
        pre = _conv_pre(x_ref[...], w_ref, b_ref)
        o_ref[...] = pre * jax.nn.sigmoid(pre)

    return pl.pallas_call(
        body, grid=(CONV_DIM // CONV_TC,),
        in_specs=[pl.BlockSpec((SEQ, CONV_TC), lambda j: (0, blk0 + j)), pl.BlockSpec((CONV_K, CONV_TC), lambda j: (0, j)),
                  pl.BlockSpec((1, CONV_TC), lambda j: (0, j))],
        out_specs=pl.BlockSpec((SEQ, CONV_TC), lambda j: (0, j)), out_shape=SDS((SEQ, CONV_DIM), F32),
        compiler_params=_cparams(("parallel",)), name=name)(proj, w, b)


def _conv_bwd(proj, w, b, dxs, dbm, dcm, dproj, *, name):
    tc = CONV_TC // 2
    blk0 = D_INNER // tc
    n_x = D_INNER // tc
    n_b = SSM_GROUPS * SSM_STATE // tc

    def body(x_ref, w_ref, b_ref, dxs_ref, dbm_ref, dcm_ref, dproj_in, dproj_ref, dw_ref, db_ref):
        del dproj_in
        j = pl.program_id(0)
        x = x_ref[...]
        pre = _conv_pre(x, w_ref, b_ref)
        sg = jax.nn.sigmoid(pre)
        dact = jnp.where(j < n_x, dxs_ref[...], jnp.where(j < n_x + n_b, dbm_ref[...], dcm_ref[...]))
        dpre = dact * (sg * (1.0 + pre * (1.0 - sg)))
        dx = jnp.zeros_like(x)
        for k in range(CONV_K):
            s = CONV_K - 1 - k
            dx = dx + w_ref[k:k + 1, :] * _shift_up(dpre, s)
            dw_ref[k:k + 1, :] = jnp.sum(dpre * _shift_down(x, s), axis=0, keepdims=True)
        dproj_ref[...] = dx.astype(dproj_ref.dtype)
        db_ref[...] = jnp.sum(dpre, axis=0, keepdims=True)

    clip = lambda v, hi: jnp.minimum(jnp.maximum(v, 0), hi)
    return pl.pallas_call(
        body, grid=(CONV_DIM // tc,),
        in_specs=[pl.BlockSpec((SEQ, tc), lambda j: (0, blk0 + j)), pl.BlockSpec((CONV_K, tc), lambda j: (0, j)),
                  pl.BlockSpec((1, tc), lambda j: (0, j)),
                  pl.BlockSpec((SEQ, tc), lambda j: (0, clip(j, n_x - 1))),
                  pl.BlockSpec((SEQ, tc), lambda j: (0, clip(j - n_x, n_b - 1))),
                  pl.BlockSpec((SEQ, tc), lambda j: (0, clip(j - n_x - n_b, n_b - 1))),
                  pl.BlockSpec(memory_space=pl.ANY)],
        out_specs=[pl.BlockSpec((SEQ, tc), lambda j: (0, blk0 + j)), pl.BlockSpec((CONV_K, tc), lambda j: (0, j)),
                   pl.BlockSpec((1, tc), lambda j: (0, j))],
        out_shape=[SDS(dproj.shape, dproj.dtype), SDS((CONV_K, CONV_DIM), F32), SDS((1, CONV_DIM), F32)],
        input_output_aliases={6: 0}, compiler_params=_cparams(("parallel",)), name=name,
    )(proj, w, b, dxs, dbm, dcm, dproj)


SSM_PAIRS = SSM_HPG // 2


def _ssd_tile(xp, zp, bm, cm, hp, dtc, dtr, bias, alog, dsk, gnp):
    row = lax.broadcasted_iota(jnp.int32, (CHUNK, CHUNK), 0)
    col = lax.broadcasted_iota(jnp.int32, (CHUNK, CHUNK), 1)
    causal = row >= col
    tri = jnp.where(causal, 1.0, 0.0)
    left = col < SSM_HEAD_DIM
    top = row < SSM_HEAD_DIM
    ones = jnp.ones((CHUNK, CHUNK), BF16)
    cb = _dot_nt(cm, bm)
    dt_c, cs_c, cs_last, m = [], [], [], []
    for r in range(SSM_HPG):
        a = -jnp.exp(alog[r])
        dt_c.append(jax.nn.softplus(dtc[r] + bias[r]))
        da_c = dt_c[r] * a
        da_r = jax.nn.softplus(dtr[r] + bias[r]) * a
        cs_c.append(jnp.sum(tri * da_r, axis=1, keepdims=True))
        cs_r = jnp.sum(jnp.where(row <= col, 1.0, 0.0) * da_c, axis=0, keepdims=True)
        cs_last.append(jnp.sum(da_c, axis=0, keepdims=True))
        m.append(cb * jnp.exp(jnp.where(causal, cs_c[r] - cs_r, -1e30)))
    ygs, hn = [], []
    for p in range(SSM_PAIRS):
        a, b = 2 * p, 2 * p + 1
        pair = lambda u, v: jnp.where(left, u, v)
        xdt = xp[p] * pair(dt_c[a], dt_c[b])
        y = pair(_dot(m[a], xdt), _dot(m[b], xdt))
        y = y + _dot_nt(cm, hp[p]) * pair(jnp.exp(cs_c[a]), jnp.exp(cs_c[b]))
        y = y + xp[p] * pair(dsk[a], dsk[b])
        decay = pair(jnp.exp(cs_last[a] - cs_c[a]), jnp.exp(cs_last[b] - cs_c[b]))
        states = _dot_tn(xdt * decay, bm)
        hn.append(hp[p] * jnp.where(top, jnp.exp(cs_last[a]), jnp.exp(cs_last[b])) + states)
        ygs.append(y * (zp[p] * jax.nn.sigmoid(zp[p])))
    ms = sum(_dot(t * t, ones) for t in ygs) * (1.0 / SSM_GROUP_W)
    rs = lax.rsqrt(ms + EPS)
    return [ygs[p] * rs * gnp[p] for p in range(SSM_PAIRS)], hn


def _ssd_in_specs(cidx):
    gw, n = SSM_GROUP_W, SSM_STATE
    bm_blk = D_INNER // n
    return [
        pl.BlockSpec((CHUNK, gw), lambda g, c: (cidx(c), g)),
        pl.BlockSpec((CHUNK, gw), lambda g, c: (cidx(c), g)),
        pl.BlockSpec((CHUNK, n), lambda g, c: (cidx(c), bm_blk + g)),
        pl.BlockSpec((CHUNK, n), lambda g, c: (cidx(c), bm_blk + SSM_GROUPS + g)),
        pl.BlockSpec((None, CHUNK, SSM_HPG), lambda g, c: (g, cidx(c), 0)),
        pl.BlockSpec((None, SSM_HPG, CHUNK), lambda g, c: (g, 0, cidx(c))),
        pl.BlockSpec((None, 1, SSM_HPG), lambda g, c: (g, 0, 0)),
        pl.BlockSpec((None, 1, SSM_HPG), lambda g, c: (g, 0, 0)),
        pl.BlockSpec((None, 1, SSM_HPG), lambda g, c: (g, 0, 0)),
        pl.BlockSpec((1, gw), lambda g, c: (0, g)),
    ]


def _ssd_args(x_ref, z_ref, bm_ref, cm_ref, hp, dtc_ref, dtr_ref, bias_ref, alog_ref, dsk_ref, gn_ref):
    nh, npair, w = SSM_HPG, SSM_PAIRS, 2 * SSM_HEAD_DIM
    col = lambda ref: [ref[:, r:r + 1] for r in range(nh)]
    return (_split(x_ref, npair, w), _split(z_ref, npair, w), bm_ref[...], cm_ref[...], hp,
            col(dtc_ref), [dtr_ref[r:r + 1, :] for r in range(nh)], col(bias_ref), col(alog_ref), col(dsk_ref),
            _split(gn_ref, npair, w))


def _pair_rows(ref):
    w = 2 * SSM_HEAD_DIM
    return [ref[p * w:(p + 1) * w, :] for p in range(SSM_PAIRS)]


def _ssd_fwd(xbc, proj, dt_c, dt_r, bias, alog, dsk, gn, mixcat, *, name):
    w = 2 * SSM_HEAD_DIM

    def body(x_ref, z_ref, bm_ref, cm_ref, dtc_ref, dtr_ref, bias_ref, alog_ref, dsk_ref, gn_ref, cat_in,
             cat_ref, hprev_ref, h_scr):
        del cat_in

        @pl.when(pl.program_id(1) == 0)
        def _():
            h_scr[...] = jnp.zeros_like(h_scr)

        hprev_ref[...] = h_scr[...]
        yn, hn = _ssd_tile(*_ssd_args(x_ref, z_ref, bm_ref, cm_ref, _pair_rows(h_scr), dtc_ref, dtr_ref, bias_ref,
                                      alog_ref, dsk_ref, gn_ref))
        for p in range(SSM_PAIRS):
            cat_ref[:, p * w:(p + 1) * w] = yn[p].astype(cat_ref.dtype)
            h_scr[p * w:(p + 1) * w, :] = hn[p]

    return pl.pallas_call(
        body, grid=(SSM_GROUPS, N_CHUNKS), in_specs=[*_ssd_in_specs(lambda c: c), pl.BlockSpec(memory_space=pl.ANY)],
        out_specs=[pl.BlockSpec((CHUNK, SSM_GROUP_W), lambda g, c: (c, g)),
                   pl.BlockSpec((None, None, SSM_GROUP_W, SSM_STATE), lambda g, c: (c, g, 0, 0))],
        out_shape=[SDS(mixcat.shape, mixcat.dtype), SDS((N_CHUNKS, SSM_GROUPS, SSM_GROUP_W, SSM_STATE), F32)],
        scratch_shapes=[pltpu.VMEM((SSM_GROUP_W, SSM_STATE), F32)],
        input_output_aliases={10: 0}, compiler_params=_cparams(("parallel", "arbitrary")), name=name,
    )(xbc, proj, xbc, xbc, dt_c, dt_r, bias, alog, dsk, gn, mixcat)


def _ssd_bwd(xbc, proj, dt_c, dt_r, bias, alog, dsk, gn, hprev, dcat, dproj, *, name):
    nh, w, gw, n = SSM_HPG, 2 * SSM_HEAD_DIM, SSM_GROUP_W, SSM_STATE
    rev = lambda c: N_CHUNKS - 1 - c

    def body(x_ref, z_ref, bm_ref, cm_ref, dtc_ref, dtr_ref, bias_ref, alog_ref, dsk_ref, gn_ref, hprev_ref, dy_ref,
             dproj_in, dz_ref, dxs_ref, dbm_ref, dcm_ref, ddtc_ref, ddtr_ref, dbias_ref, dalog_ref, ddsk_ref, dgn_ref,
             dh_scr):
        del dproj_in
        first = pl.program_id(1) == 0

        @pl.when(first)
        def _():
            dh_scr[...] = jnp.zeros_like(dh_scr)
            for ref in (dbias_ref, dalog_ref, ddsk_ref, dgn_ref):
                ref[...] = jnp.zeros_like(ref)

        args = _ssd_args(x_ref, z_ref, bm_ref, cm_ref, _pair_rows(hprev_ref), dtc_ref, dtr_ref, bias_ref, alog_ref,
                         dsk_ref, gn_ref)
        _, vjp = jax.vjp(_ssd_tile, *args)
        dxs, dzs, dbm, dcm, dhs, ddtc, ddtr, dbias, dalog, ddsk, dgn = vjp(
            (_split(dy_ref, SSM_PAIRS, w), _pair_rows(dh_scr)))
        dbm_ref[...] = dbm
        dcm_ref[...] = dcm
        for q in range(SSM_PAIRS):
            dxs_ref[:, q * w:(q + 1) * w] = dxs[q]
            dz_ref[:, q * w:(q + 1) * w] = dzs[q].astype(dz_ref.dtype)
            dh_scr[q * w:(q + 1) * w, :] = dhs[q]
            dgn_ref[:, q * w:(q + 1) * w] += dgn[q]
        for r in range(nh):
            ddtc_ref[:, r:r + 1] = ddtc[r]
            ddtr_ref[r:r + 1, :] = ddtr[r]
            dbias_ref[:, r:r + 1] += dbias[r]
            dalog_ref[:, r:r + 1] += dalog[r]
            ddsk_ref[:, r:r + 1] += ddsk[r]

    par_spec = pl.BlockSpec((None, 1, nh), lambda g, c: (g, 0, 0))
    return pl.pallas_call(
        body, grid=(SSM_GROUPS, N_CHUNKS),
        in_specs=[*_ssd_in_specs(rev),
                  pl.BlockSpec((None, None, gw, n), lambda g, c: (rev(c), g, 0, 0)),
                  pl.BlockSpec((CHUNK, gw), lambda g, c: (rev(c), g)),
                  pl.BlockSpec(memory_space=pl.ANY)],
        out_specs=[pl.BlockSpec((CHUNK, gw), lambda g, c: (rev(c), g)),
                   pl.BlockSpec((CHUNK, gw), lambda g, c: (rev(c), g)),
                   pl.BlockSpec((CHUNK, n), lambda g, c: (rev(c), g)),
                   pl.BlockSpec((CHUNK, n), lambda g, c: (rev(c), g)),
                   pl.BlockSpec((None, CHUNK, nh), lambda g, c: (g, rev(c), 0)),
                   pl.BlockSpec((None, nh, CHUNK), lambda g, c: (g, 0, rev(c))),
                   par_spec, par_spec, par_spec,
                   pl.BlockSpec((1, gw), lambda g, c: (0, g))],
        out_shape=[SDS(dproj.shape, dproj.dtype), SDS((SEQ, D_INNER), F32), SDS((SEQ, SSM_GROUPS * n), F32),
                   SDS((SEQ, SSM_GROUPS * n), F32), SDS((SSM_GROUPS, SEQ, nh), F32), SDS((SSM_GROUPS, nh, SEQ), F32),
                   SDS((SSM_GROUPS, 1, nh), F32), SDS((SSM_GROUPS, 1, nh), F32), SDS((SSM_GROUPS, 1, nh), F32),
                   SDS((1, D_INNER), F32)],
        scratch_shapes=[pltpu.VMEM((gw, n), F32)],
        input_output_aliases={12: 0}, compiler_params=_cparams(("parallel", "arbitrary")), name=name,
    )(xbc, proj, xbc, xbc, dt_c, dt_r, bias, alog, dsk, gn, hprev, dcat, dproj)


def _sum_contributions(chip, parts, landed, *, name):
    _, r, c = parts.shape
    tr = _pick(r, (256, 384, 128))

    def body(chip_ref, own_ref, landed_ref, o_ref):
        del chip_ref
        acc = own_ref[...].astype(F32)
        for s in range(landed_ref.shape[0]):
            acc = acc + landed_ref[s].astype(F32)
        o_ref[...] = acc

    grid_spec = pltpu.PrefetchScalarGridSpec(
        num_scalar_prefetch=1, grid=(r // tr,),
        in_specs=[pl.BlockSpec((None, tr, c), lambda i, chip_ref: (chip_ref[0], i, 0)),
                  pl.BlockSpec((landed.shape[0], tr, c), lambda i, chip_ref: (0, i, 0))],
        out_specs=pl.BlockSpec((tr, c), lambda i, chip_ref: (i, 0)))
    return pl.pallas_call(body, grid_spec=grid_spec, out_shape=SDS((r, c), F32),
                          compiler_params=_cparams(("parallel",)), name=name)(chip, parts, landed)


def _adamw(w, g, m, v, *, name):
    r, c = w.shape
    tr = r if r <= 256 else _pick(r, (256, 128, 8))
    spec = pl.BlockSpec((tr, c), lambda i: (i, 0))

    def body(w_ref, g_ref, m_ref, v_ref, d_ref, mo_ref, vo_ref):
        g = g_ref[...]
        m_new = ADAM_B1 * m_ref[...] + (1.0 - ADAM_B1) * g
        v_new = ADAM_B2 * v_ref[...] + (1.0 - ADAM_B2) * (g * g)
        m_hat = m_new / (1.0 - ADAM_B1 ** ADAM_STEP)
        v_hat = v_new / (1.0 - ADAM_B2 ** ADAM_STEP)
        d_ref[...] = -ADAM_LR * (m_hat / (jnp.sqrt(v_hat) + ADAM_EPS) + ADAM_WD * w_ref[...])
        mo_ref[...] = m_new
        vo_ref[...] = v_new

    return pl.pallas_call(body, grid=(r // tr,), in_specs=[spec] * 4, out_specs=[spec] * 3,
                          out_shape=[SDS((r, c), F32)] * 3, compiler_params=_cparams(("parallel",)), name=name)(w, g, m, v)


ANY = pl.BlockSpec(memory_space=pl.ANY)


def _place():
    x, y, c = lax.axis_index("x"), lax.axis_index("y"), lax.axis_index("c")
    chips = [(1 - x, y), (x, 1 - y), (1 - x, 1 - y)]
    return x, y, c, chips


def _remote(src, dst, send_sem, recv_sem, to):
    return pltpu.make_async_remote_copy(src_ref=src, dst_ref=dst, send_sem=send_sem, recv_sem=recv_sem,
                                        device_id=to, device_id_type=MESH)


STREAM_ROWS = 256


def _stream_rows(i):
    return pl.ds(pl.multiple_of(i * STREAM_ROWS, STREAM_ROWS), STREAM_ROWS)


def _channel_scratch(width, dtype, rows=STREAM_ROWS):
    buf = (2, rows, width)
    return [pltpu.VMEM(buf, dtype), pltpu.VMEM(buf, dtype), *([pltpu.SemaphoreType.DMA((2,))] * 5),
            pltpu.SemaphoreType.REGULAR((2,))]


CHANNEL_REFS = 8


def _copy_blocks(srcs, dsts, ch):
    sbuf, _, ld, _, _, st, _, _ = ch
    n = len(srcs)
    load = lambda i: pltpu.make_async_copy(srcs[i], sbuf.at[i % 2], ld.at[i % 2])
    store = lambda i: pltpu.make_async_copy(sbuf.at[i % 2], dsts[i], st.at[i % 2])
    load(0).start()
    for i in range(n):
        if i + 1 < n:
            if i >= 1:
                store(i - 1).wait()
            load(i + 1).start()
        load(i).wait()
        store(i).start()
    for i in range(max(0, n - 2), n):
        store(i).wait()


def _exchange_blocks(srcs, dsts, keeps, ch, sibling):
    sbuf, rbuf, ld, snd, rcv, st, kp, credit = ch
    n = len(srcs)
    load = lambda i: pltpu.make_async_copy(srcs[i], sbuf.at[i % 2], ld.at[i % 2])
    push = lambda i: _remote(sbuf.at[i % 2], rbuf.at[i % 2], snd.at[i % 2], rcv.at[i % 2], sibling)
    store = lambda i: pltpu.make_async_copy(rbuf.at[i % 2], dsts[i], st.at[i % 2])
    save = lambda i: pltpu.make_async_copy(sbuf.at[i % 2], keeps[i], kp.at[i % 2])

    def send(i):
        load(i).wait()
        pl.semaphore_wait(credit.at[i % 2], 1)
        push(i).start()
        if keeps[i] is not None:
            save(i).start()

    for i in range(min(2, n)):
        pl.semaphore_signal(credit.at[i], 1, device_id=sibling, device_id_type=MESH)
        load(i).start()
    send(0)
    for i in range(n):
        if i >= 1:
            store(i - 1).wait()
            if i + 1 < n:
                pl.semaphore_signal(credit.at[(i + 1) % 2], 1, device_id=sibling, device_id_type=MESH)
        if i + 1 < n:
            send(i + 1)
        push(i).wait_recv()
        store(i).start()
        push(i).wait_send()
        if keeps[i] is not None:
            save(i).wait()
        if i + 2 < n:
            load(i + 2).start()
    store(n - 1).wait()


def _all_gather_shards(shards, small, *, name):
    n = len(shards)

    def body(*refs):
        ins, outs = refs[:n + 1], refs[n + 1:2 * n + 2]
        scr = refs[2 * n + 2:]
        chans = [scr[CHANNEL_REFS * t:CHANNEL_REFS * (t + 1)] for t in range(n)]
        send_sems, recv_sems, small_sems = scr[CHANNEL_REFS * n:]
        x, y, c, _ = _place()
        me = 2 * x + y
        sibling = (x, y, 1 - c)
        near = (lax.rem(x + 1 - c, 2), lax.rem(y + c, 2))
        far = (lax.rem(x + c, 2), lax.rem(y + 1 - c, 2))
        k_near, k_far, k_diag = 2 * near[0] + near[1], 2 * far[0] + far[1], 3 - me
        targets = ((*near, c), (*far, c), (*far, c))
        arrives = (k_near, k_far, k_diag)
        streams_in = (k_far, k_near, k_diag)

        def ici(t, j, src, blk):
            return _remote(src, outs[t].at[blk, c], send_sems.at[3 * t + j], recv_sems.at[3 * t + j], targets[j])

        first = [ici(t, j, ins[t].at[c], me) for t in range(n + 1) for j in range(2)]
        for cp in first:
            cp.start()
        small_local = pltpu.make_async_copy(ins[n], outs[n].at[me], small_sems.at[6])
        small_local.start()
        for t in range(n):
            _copy_blocks([ins[t].at[h] for h in range(2)], [outs[t].at[me, h] for h in range(2)], chans[t])
        passed = []
        for j in range(3):
            for t in range(n + 1):
                landed = outs[t].at[arrives[j], c]
                ici(t, j, landed, arrives[j]).wait_recv()
                if j == 0:
                    fwd = ici(t, 2, landed, k_near)
                    fwd.start()
                    passed.append(fwd)
                if t < n:
                    _exchange_blocks([landed], [outs[t].at[streams_in[j], 1 - c]], [None], chans[t], sibling)
                else:
                    fwd = _remote(landed, landed, small_sems.at[j], small_sems.at[3 + j], sibling)
                    fwd.start()
                    passed.append(fwd)
        for j in range(3):
            got = outs[n].at[streams_in[j], 1 - c]
            _remote(got, got, small_sems.at[j], small_sems.at[3 + j], sibling).wait_recv()
        for cp in first + passed:
            cp.wait_send()
        small_local.wait()

    scratch = []
    for s in shards:
        scratch += _channel_scratch(s.shape[2], s.dtype, rows=s.shape[1])
    return pl.pallas_call(
        body, in_specs=[ANY] * (n + 1), out_specs=[ANY] * (n + 1),
        out_shape=[SDS((N_CHIPS, *s.shape), s.dtype) for s in (*shards, small)],
        scratch_shapes=[*scratch, pltpu.SemaphoreType.DMA((3 * n + 3,)), pltpu.SemaphoreType.DMA((3 * n + 3,)),
                        pltpu.SemaphoreType.DMA((7,))],
        compiler_params=pltpu.CompilerParams(vmem_limit_bytes=VMEM_LIMIT), name=name)(*shards, small)


def _pair_reduce(stacks, *, name):
    n = len(stacks)
    per = 11

    def body(*refs):
        ins, outs, scr = refs[:n], refs[n:2 * n], refs[2 * n:]
        x, y, c, _ = _place()
        sibling = (x, y, 1 - c)
        for t in range(n):
            sraw, sbuf, rbuf, obuf, pbuf, ld_s, ld_o, snd, rcv, st, credit = scr[per * t:per * (t + 1)]
            steps = ins[t].shape[1] // STREAM_ROWS
            src, own, out = ins[t].at[1 - c], ins[t].at[c], outs[t]

            def load_s(i, slot, src=src, sraw=sraw, ld_s=ld_s):
                return pltpu.make_async_copy(src.at[_stream_rows(i)], sraw.at[slot], ld_s.at[slot])

            def load_o(i, slot, own=own, obuf=obuf, ld_o=ld_o):
                return pltpu.make_async_copy(own.at[_stream_rows(i)], obuf.at[slot], ld_o.at[slot])

            def push(slot, sbuf=sbuf, rbuf=rbuf, snd=snd, rcv=rcv):
                return _remote(sbuf.at[slot], rbuf.at[slot], snd.at[slot], rcv.at[slot], sibling)

            def store(i, slot, pbuf=pbuf, out=out, st=st):
                return pltpu.make_async_copy(pbuf.at[slot], out.at[_stream_rows(i)], st.at[slot])

            assert steps >= 2
            for slot in range(2):
                pl.semaphore_signal(credit.at[slot], 1, device_id=sibling, device_id_type=MESH)
                load_s(slot, slot).start()
                load_o(slot, slot).start()
            load_s(0, 0).wait()
            sbuf[0] = sraw[0].astype(sbuf.dtype)
            pl.semaphore_wait(credit.at[0], 1)
            push(0).start()

            def step(i, carry, load_s=load_s, load_o=load_o, push=push, store=store, sraw=sraw, sbuf=sbuf, rbuf=rbuf,
                     obuf=obuf, pbuf=pbuf, credit=credit, steps=steps):
                slot = lax.rem(i, 2)
                nxt = 1 - slot

                @pl.when(i + 1 < steps)
                def _():
                    load_s(i + 1, nxt).wait()
                    sbuf[nxt] = sraw[nxt].astype(sbuf.dtype)
                    pl.semaphore_wait(credit.at[nxt], 1)
                    push(nxt).start()

                load_o(i, slot).wait()
                push(slot).wait_recv()

                @pl.when(i >= 2)
                def _():
                    store(i, slot).wait()

                pbuf[slot] = (obuf[slot] + rbuf[slot].astype(F32)).astype(pbuf.dtype)
                store(i, slot).start()
                push(slot).wait_send()

                @pl.when(i + 2 < steps)
                def _():
                    load_s(i + 2, slot).start()
                    load_o(i + 2, slot).start()
                    pl.semaphore_signal(credit.at[slot], 1, device_id=sibling, device_id_type=MESH)
                return carry

            lax.fori_loop(0, steps, step, 0)
            for slot in range(2):
                store(0, slot).wait()

    scratch = []
    for s in stacks:
        buf = (2, STREAM_ROWS, s.shape[2])
        scratch += [pltpu.VMEM(buf, F32), pltpu.VMEM(buf, BF16), pltpu.VMEM(buf, BF16), pltpu.VMEM(buf, F32),
                    pltpu.VMEM(buf, BF16), *([pltpu.SemaphoreType.DMA((2,))] * 5), pltpu.SemaphoreType.REGULAR((2,))]
    return pl.pallas_call(
        body, in_specs=[ANY] * n, out_specs=[ANY] * n, out_shape=[SDS(s.shape[1:], BF16) for s in stacks],
        scratch_shapes=scratch, compiler_params=pltpu.CompilerParams(vmem_limit_bytes=VMEM_LIMIT), name=name)(*stacks)


HBM_SPEC = pl.BlockSpec(memory_space=pltpu.HBM)
SEM_SPEC = pl.BlockSpec(memory_space=pltpu.SEMAPHORE)
SIDE_EFFECT = pltpu.SideEffectType.DATAFLOW_SIDE_EFFECTING


def _scatter_copies(ins, lands, send_sems, recv_sems):
    _, _, c, chips = _place()
    return [_remote(ins[t].at[2 * cx + cy], lands[t].at[j], send_sems.at[3 * t + j], recv_sems.at[3 * t + j],
                    (cx, cy, c)) for t in range(len(ins)) for j, (cx, cy) in enumerate(chips)]


def _chip_scatter_start(parts, *, name):
    n = len(parts)

    def body(*refs):
        ins, lands = refs[:n], refs[n:2 * n]
        send_sems, recv_sems, token = refs[2 * n], refs[2 * n + 1], refs[-1]
        for cp in _scatter_copies(ins, lands, send_sems, recv_sems):
            cp.start()
        token[...] = jnp.zeros_like(token)

    hbm = lambda a: pltpu.with_memory_space_constraint(a, pltpu.HBM)
    lands = [hbm(lax.empty((3, *p.shape[1:]), p.dtype)) for p in parts]
    thru = [pltpu.HBM(a.shape, a.dtype) for a in (*parts, *lands)]
    outs = pl.pallas_call(
        body, name=name,
        out_shape=(pltpu.SemaphoreType.DMA((3 * n,)), pltpu.SemaphoreType.DMA((3 * n,)), *thru, SDS((8, 128), F32)),
        in_specs=[HBM_SPEC] * (2 * n),
        out_specs=(SEM_SPEC, SEM_SPEC, *([HBM_SPEC] * (2 * n)), pl.BlockSpec(memory_space=pltpu.VMEM)),
        input_output_aliases={i: 2 + i for i in range(2 * n)},
        compiler_params=pltpu.CompilerParams(has_side_effects=SIDE_EFFECT),
    )(*[hbm(p) for p in parts], *lands)
    return outs[0], outs[1], outs[2:2 + n], outs[2 + n:2 + 2 * n], outs[-1]


def _chip_scatter_wait(send_sems, recv_sems, parts, lands, after, *, name):
    n = len(parts)

    def body(*refs):
        ins, lands_in = refs[:n], refs[n:2 * n]
        for cp in _scatter_copies(ins, lands_in, refs[2 * n], refs[2 * n + 1]):
            cp.wait_send()
            cp.wait_recv()

    outs = pl.pallas_call(
        body, name=name, out_shape=[pltpu.HBM(a.shape, a.dtype) for a in (*parts, *lands)],
        in_specs=[*([HBM_SPEC] * (2 * n)), SEM_SPEC, SEM_SPEC, *([ANY] * len(after))],
        out_specs=[HBM_SPEC] * (2 * n), input_output_aliases={i: i for i in range(2 * n)},
        compiler_params=pltpu.CompilerParams(has_side_effects=SIDE_EFFECT),
    )(*parts, *lands, send_sems, recv_sems, *after)
    return outs[:n], outs[n:]


def _gather_copies(shards, zones, send_sems, recv_sems):
    x, y, c, chips = _place()
    return [_remote(shards[t].at[c], zones[t].at[2 * x + y, c], send_sems.at[3 * t + j], recv_sems.at[3 * t + j],
                    (cx, cy, c)) for t in range(len(shards)) for j, (cx, cy) in enumerate(chips)]


def _gather_start(shards, after, *, name):
    n = len(shards)

    def body(*refs):
        ins, zones = refs[:n], refs[n:2 * n]
        send_sems, recv_sems, token = refs[2 * n + len(after)], refs[2 * n + len(after) + 1], refs[-1]
        for cp in _gather_copies(ins, zones, send_sems, recv_sems):
            cp.start()
        token[...] = jnp.zeros_like(token)

    hbm = lambda a: pltpu.with_memory_space_constraint(a, pltpu.HBM)
    zones = [hbm(lax.empty((N_CHIPS, *s.shape), s.dtype)) for s in shards]
    thru = [pltpu.HBM(a.shape, a.dtype) for a in (*shards, *zones)]
    outs = pl.pallas_call(
        body, name=name,
        out_shape=(pltpu.SemaphoreType.DMA((3 * n,)), pltpu.SemaphoreType.DMA((3 * n,)), *thru, SDS((8, 128), F32)),
        in_specs=[*([HBM_SPEC] * (2 * n)), *([ANY] * len(after))],
        out_specs=(SEM_SPEC, SEM_SPEC, *([HBM_SPEC] * (2 * n)), pl.BlockSpec(memory_space=pltpu.VMEM)),
        input_output_aliases={i: 2 + i for i in range(2 * n)},
        compiler_params=pltpu.CompilerParams(has_side_effects=SIDE_EFFECT),
    )(*[hbm(s) for s in shards], *zones, *after)
    return outs[0], outs[1], outs[2:2 + n], outs[2 + n:2 + 2 * n], outs[-1]


def _gather_wait(send_sems, recv_sems, shards, zones, after, *, name):
    n = len(shards)

    def body(*refs):
        for cp in _gather_copies(refs[:n], refs[n:2 * n], refs[2 * n], refs[2 * n + 1]):
            cp.wait_send()
            cp.wait_recv()

    outs = pl.pallas_call(
        body, name=name, out_shape=[pltpu.HBM(a.shape, a.dtype) for a in (*shards, *zones)],
        in_specs=[*([HBM_SPEC] * (2 * n)), SEM_SPEC, SEM_SPEC, *([ANY] * len(after))],
        out_specs=[HBM_SPEC] * (2 * n), input_output_aliases={i: i for i in range(2 * n)},
        compiler_params=pltpu.CompilerParams(has_side_effects=SIDE_EFFECT),
    )(*shards, *zones, send_sems, recv_sems, *after)
    return outs[:n], outs[n:]


def _gather_finish(shards, zones, *, name):
    n = len(shards)

    def body(*refs):
        ins, zones_in, outs, scr = refs[:n], refs[n:2 * n], refs[2 * n:3 * n], refs[3 * n:]
        x, y, c, chips = _place()
        me = 2 * x + y
        sibling = (x, y, 1 - c)
        others = [2 * cx + cy for cx, cy in chips]
        for t in range(n):
            chan = scr[CHANNEL_REFS * t:CHANNEL_REFS * (t + 1)]
            _copy_blocks([ins[t].at[h] for h in range(2)], [outs[t].at[me, h] for h in range(2)], chan)
            _exchange_blocks([zones_in[t].at[k, c] for k in others], [outs[t].at[k, 1 - c] for k in others],
                             [None] * len(others), chan, sibling)

    scratch = []
    for s in shards:
        scratch += _channel_scratch(s.shape[2], s.dtype, rows=s.shape[1])
    return pl.pallas_call(
        body, in_specs=[ANY] * (2 * n), out_specs=[ANY] * n, out_shape=[SDS(z.shape, z.dtype) for z in zones],
        input_output_aliases={n + t: t for t in range(n)}, scratch_shapes=scratch,
        compiler_params=pltpu.CompilerParams(vmem_limit_bytes=VMEM_LIMIT), name=name)(*shards, *zones)


def _pair_share(groups, *, name):
    finals = [f for grp in groups for f in grp]
    n, n_out = len(finals), len(groups)

    def body(*refs):
        ins, outs, scr = refs[:n], refs[n:n + n_out], refs[n + n_out:]
        x, y, c, _ = _place()
        sibling = (x, y, 1 - c)
        t = 0
        for o, grp in enumerate(groups):
            rows = grp[0].shape[0] // 2
            blocks = [(layer, pl.ds(b * rows, rows)) for layer in range(len(grp)) for b in range(2)]
            _exchange_blocks([ins[t + layer].at[rs] for layer, rs in blocks],
                             [outs[o].at[layer, 1 - c, rs] for layer, rs in blocks],
                             [outs[o].at[layer, c, rs] for layer, rs in blocks],
                             scr[CHANNEL_REFS * o:CHANNEL_REFS * (o + 1)], sibling)
            t += len(grp)

    scratch = []
    for grp in groups:
        scratch += _channel_scratch(grp[0].shape[1], grp[0].dtype, rows=grp[0].shape[0] // 2)
    return pl.pallas_call(
        body, in_specs=[ANY] * n, out_specs=[ANY] * n_out,
        out_shape=[SDS((len(grp), 2, *grp[0].shape), grp[0].dtype) for grp in groups],
        scratch_shapes=scratch, compiler_params=pltpu.CompilerParams(vmem_limit_bytes=VMEM_LIMIT), name=name)(*finals)


def _all_reduce_small(v, *, name):
    rows, lanes = v.shape
    n_dev = 8

    def body(v_ref, o_ref, all_ref, send_sems, recv_sems, local_sem):
        x, y, c, chips = _place()
        me, sibling = (x, y, c), (x, y, 1 - c)

        def block(px, py, pc):
            return all_ref.at[4 * px + 2 * py + pc]

        def copy(k, blk, to, src=None):
            return _remote(block(*blk) if src is None else src, block(*blk), send_sems.at[k], recv_sems.at[k], to)

        mine = pltpu.make_async_copy(v_ref, block(*me), local_sem)
        mine.start()
        first = [copy(0, me, sibling, src=v_ref)]
        first += [copy(1 + j, me, (*chip, c), src=v_ref) for j, chip in enumerate(chips)]
        for cp in first:
            cp.start()
        passed = [copy(4 + j, (*chip, c), sibling) for j, chip in enumerate(chips)]
        for j, chip in enumerate(chips):
            copy(1 + j, (*chip, c), me).wait_recv()
            passed[j].start()
        copy(0, sibling, me).wait_recv()
        for j, chip in enumerate(chips):
            copy(4 + j, (*chip, 1 - c), me).wait_recv()
        for cp in first + passed:
            cp.wait_send()
        mine.wait()
        acc = all_ref[0]
        for k in range(1, n_dev):
            acc = acc + all_ref[k]
        o_ref[...] = acc

    vmem = pl.BlockSpec(memory_space=pltpu.VMEM)
    return pl.pallas_call(
        body, in_specs=[vmem], out_specs=vmem, out_shape=SDS((rows, lanes), F32),
        scratch_shapes=[pltpu.VMEM((n_dev, rows, lanes), F32), pltpu.SemaphoreType.DMA((7,)),
                        pltpu.SemaphoreType.DMA((7,)), pltpu.SemaphoreType.DMA],
        compiler_params=pltpu.CompilerParams(vmem_limit_bytes=VMEM_LIMIT), name=name)(v)


def _relu2_epilogue(acc):
    return acc, jnp.square(jnp.maximum(acc, 0.0))


def _res_epilogue(acc, res):
    return (acc + res,)


def _drelu2_epilogue(acc, pre):
    return (acc * (2.0 * jnp.maximum(pre.astype(F32), 0.0)),)


def _ffn_fwd(h, g, w1, w2, tag):
    f = _rms_fwd(h, g, name=f"ffn_norm_{tag}")
    pre, act = _mm_nn(f, w1, name=f"ffn1_{tag}", epilogue=_relu2_epilogue, n_out_dtypes=(BF16, BF16))
    h_out = _mm_nn(act, w2, name=f"ffn2_{tag}", extras=(h,), epilogue=_res_epilogue)
    return h_out, (f, pre, act)


def _ffn_bwd(dh, h, g, w1, w2, saved, layer, after=()):
    f, pre, act = saved
    dpre = _mm_nt(dh, w2, name=f"ffn2_dx_{layer}", out_dtype=BF16, extras=(pre,), epilogue=_drelu2_epilogue,
                  after=after)
    dw2 = _mm_tn_stacked(act, dh, name=f"ffn2_dw_{layer}", col_slots=False)
    df = _mm_nt(dpre, w1, name=f"ffn1_dx_{layer}")
    dw1 = _mm_tn_stacked(f, dpre, name=f"ffn1_dw_{layer}", col_slots=True)
    dh, dg = _rms_bwd(h, g, df, dh, name=f"ffn_norm_bwd_{layer}")
    return dh, dg, dw1, dw2


def _kv_fwd(mem, g, w_kv, tag):
    m = _rms_fwd(mem, g, name=f"mem_norm_{tag}")
    return m, _mm_nn(m, w_kv, name=f"kv_{tag}")


def _kv_bwd(mem, g, w_kv, m, dk, dv, layer):
    dkv = jnp.concatenate([dk, dv], axis=1)
    dw = _mm_tn_stacked(m, dkv, name=f"kv_dw_{layer}", col_slots=True)
    dm = _mm_nt(dkv, w_kv, name=f"kv_dx_{layer}")
    _, dg = _rms_bwd(mem, g, dm, dm, name=f"mem_norm_bwd_{layer}")
    return dw, dg


def _local_step(x, mem, target, p, after_layer1=None, after_ffn0=None, after_mixer0=None):
    row = lambda v: v.reshape(1, -1)
    g = {}

    h0 = x
    a0 = _rms_fwd(h0, row(p["norm_mix"][0]), name="mix_norm_0")
    proj_a = _mm_nn(a0, p["a_in"], name="a_in", after=p.get("after_start", ()))
    m0, kv0 = _kv_fwd(mem, row(p["mem_norm"][0]), p["w_kv"][0], "0")
    cat0 = _attn_fwd(proj_a, 2 * D_INNER, kv0, name="attn_0")
    bs_col = p["a_bs"].reshape(A_GROUPS, CHUNK, 1)
    cat0 = _gate_fwd(proj_a, p["a_ln_g"], p["a_ln_b"], p["a_ws"], bs_col, cat0, name="gate")
    h1 = _mm_nn(cat0, p["w_out"][0], name="out_0", extras=(h0,), epilogue=_res_epilogue)
    h2, ffn0 = _ffn_fwd(h1, row(p["norm_ffn"][0]), p["w_ffn1"][0], p["w_ffn2"][0], "0")

    if "layer1_mixer" in p:
        w_kv1, w_out1, b_in = p["layer1_mixer"](h2)
    else:
        w_kv1, w_out1, b_in = p["w_kv"][1], p["w_out"][1], p["b_in"]
    a1 = _rms_fwd(h2, row(p["norm_mix"][1]), name="mix_norm_1")
    proj_b = _mm_nn(a1, b_in, name="b_in")
    m1, kv1 = _kv_fwd(mem, row(p["mem_norm"][1]), w_kv1, "1")
    cat1 = _attn_fwd(proj_b, B_Q_OFF, kv1, name="attn_1")
    xbc = _conv_fwd(proj_b, p["b_conv_w"], p["b_conv_b"], name="conv")
    dt_raw = proj_b[:, B_DT_OFF:B_DT_OFF + SSM_HEADS].reshape(SEQ, SSM_GROUPS, SSM_HPG)
    dt_c = jnp.transpose(dt_raw, (1, 0, 2))
    dt_r = jnp.transpose(dt_raw, (1, 2, 0))
    per_head = lambda v: v.reshape(SSM_GROUPS, 1, SSM_HPG)
    ssd_par = (per_head(p["b_dt_bias"]), per_head(p["b_a_log"]), per_head(p["b_d"]), p["b_gnorm"])
    cat1, hprev = _ssd_fwd(xbc, proj_b, dt_c, dt_r, *ssd_par, cat1, name="ssd")
    h3 = _mm_nn(cat1, w_out1, name="out_1", extras=(h2,), epilogue=_res_epilogue)
    w_ffn1_1, w_ffn2_1 = p["layer1_ffn"](h3) if "layer1_ffn" in p else (p["w_ffn1"][1], p["w_ffn2"][1])
    h4, ffn1 = _ffn_fwd(h3, row(p["norm_ffn"][1]), w_ffn1_1, w_ffn2_1, "1")

    loss, dh, g["final_norm"] = _loss_head(h4, row(p["final_norm"]), target, name="loss_head")

    dh, dnf1, dw1_1, dw2_1 = _ffn_bwd(dh, h3, row(p["norm_ffn"][1]), w_ffn1_1, w_ffn2_1, ffn1, 1)
    dcat1 = _mm_nt(dh, w_out1, name="out_dx_1")
    dwo_1 = _mm_tn_stacked(cat1, dh, name="out_dw_1", col_slots=False)
    dproj_b, dk1, dv1 = _attn_bwd(proj_b, B_Q_OFF, kv1, dcat1, B_IN_PAD, B_Q_OFF, name="attn_bwd_1")
    (dproj_b, dxs, dbm, dcm, ddt_c, ddt_r, g["b_dt_bias"], g["b_a_log"], g["b_d"], g["b_gnorm"]) = _ssd_bwd(
        xbc, proj_b, dt_c, dt_r, *ssd_par, hprev, dcat1, dproj_b, name="ssd_bwd")
    dproj_b, g["b_conv_w"], g["b_conv_b"] = _conv_bwd(proj_b, p["b_conv_w"], p["b_conv_b"], dxs, dbm, dcm, dproj_b,
                                                      name="conv_bwd")
    ddt = jnp.transpose(ddt_c, (1, 0, 2)) + jnp.transpose(ddt_r, (2, 0, 1))
    ddt = jnp.pad(ddt.reshape(SEQ, SSM_HEADS), ((0, 0), (0, B_IN_PAD - B_DT_OFF - SSM_HEADS))).astype(BF16)
    dproj_b = lax.dynamic_update_slice(dproj_b, ddt, (0, B_DT_OFF))
    dwkv_1, dmn1 = _kv_bwd(mem, row(p["mem_norm"][1]), w_kv1, m1, dk1, dv1, 1)
    dwb = _b_in_grad_slots(_mm_tn(a1, dproj_b, name="b_in_dw"))
    da1 = _mm_nt(dproj_b, b_in, name="b_in_dx")
    dh, dnm1 = _rms_bwd(h2, row(p["norm_mix"][1]), da1, dh, name="mix_norm_bwd_1")
    layer1 = dict(w_kv=dwkv_1, w_out=dwo_1, w_ffn1=dw1_1, w_ffn2=dw2_1, b_in=dwb)
    token = () if after_layer1 is None else (after_layer1(layer1),)

    dh, dnf0, dw1_0, dw2_0 = _ffn_bwd(dh, h1, row(p["norm_ffn"][0]), p["w_ffn1"][0], p["w_ffn2"][0], ffn0, 0,
                                      after=token)
    ffn0_grads = dict(w_ffn1=dw1_0, w_ffn2=dw2_0)
    token = () if after_ffn0 is None else (after_ffn0(ffn0_grads),)
    dcat0 = _mm_nt(dh, p["w_out"][0], name="out_dx_0", after=token)
    dwo_0 = _mm_tn_stacked(cat0, dh, name="out_dw_0", col_slots=False)
    dproj_a, dk0, dv0 = _attn_bwd(proj_a, 2 * D_INNER, kv0, dcat0, A_IN, 2 * D_INNER, name="attn_bwd_0")
    dproj_a, g["a_ln_g"], g["a_ln_b"], g["a_ws"], dbs_col = _gate_bwd(
        proj_a, p["a_ln_g"], p["a_ln_b"], p["a_ws"], bs_col, dcat0, dproj_a, name="gate_bwd")
    g["a_bs"] = dbs_col.reshape(A_GROUPS, CHUNK)
    dwkv_0, dmn0 = _kv_bwd(mem, row(p["mem_norm"][0]), p["w_kv"][0], m0, dk0, dv0, 0)
    dwa = _mm_tn_stacked(a0, dproj_a, name="a_in_dw", col_slots=True)
    mixer0_grads = dict(w_kv=dwkv_0, w_out=dwo_0, a_in=dwa)
    token = () if after_mixer0 is None else (after_mixer0(mixer0_grads),)
    da0 = _mm_nt(dproj_a, p["a_in"], name="a_in_dx", after=token)
    dx, dnm0 = _rms_bwd(h0, row(p["norm_mix"][0]), da0, dh, name="mix_norm_bwd_0")

    g["norm_mix"] = jnp.concatenate([dnm0, dnm1], axis=0)
    g["norm_ffn"] = jnp.concatenate([dnf0, dnf1], axis=0)
    g["mem_norm"] = jnp.concatenate([dmn0, dmn1], axis=0)
    layer0 = dict(w_kv=dwkv_0, w_out=dwo_0, w_ffn1=dw1_0, w_ffn2=dw2_0, a_in=dwa)
    return loss, dx, g, layer0, layer1


def _b_in_full(gathered):
    n = B_IN // N_CHIPS
    dt0 = D_INNER + CONV_DIM - (N_CHIPS - 1) * n
    last = gathered[N_CHIPS - 1]
    return jnp.concatenate([*[gathered[k] for k in range(N_CHIPS - 1)], last[:, :dt0], last[:, dt0 + SSM_HEADS:],
                            last[:, dt0:dt0 + SSM_HEADS], jnp.zeros((D_MODEL, B_IN_PAD - B_IN), last.dtype)], axis=1)


def _b_in_grad_slots(d):
    n = B_IN // N_CHIPS
    dt0 = D_INNER + CONV_DIM
    last = jnp.concatenate([d[:, (N_CHIPS - 1) * n:dt0], d[:, B_DT_OFF:B_DT_OFF + SSM_HEADS], d[:, dt0:B_DT_OFF]], axis=1)
    slots = [*[d[:, k * n:(k + 1) * n] for k in range(N_CHIPS - 1)], last]
    half = D_MODEL // 2
    return jnp.stack([jnp.stack([s[h * half:(h + 1) * half] for s in slots]) for h in range(2)])


LARGE = ("w_kv", "w_out", "w_ffn1", "w_ffn2", "a_in", "b_in")
SMALL_REPL = ("norm_mix", "norm_ffn", "mem_norm", "a_ln_g", "a_ln_b", "a_ws", "a_bs", "b_dt_bias", "b_a_log", "b_d",
              "final_norm")
SMALL_SHARD = ("b_conv_w", "b_conv_b", "b_gnorm")
WEIGHTS = ("norm_mix", "norm_ffn", "mem_norm", "w_kv", "w_out", "w_ffn1", "w_ffn2", "a_in", "a_ln_g", "a_ln_b", "a_ws",
           "a_bs", "b_in", "b_conv_w", "b_conv_b", "b_dt_bias", "b_a_log", "b_d", "b_gnorm", "final_norm")
CONV_SHARD = CONV_DIM // N_CHIPS
GN_SHARD = D_INNER // N_CHIPS


LAYERED = ("w_kv", "w_out", "w_ffn1", "w_ffn2")
LAYER_TENSORS = (("w_kv", "w_out", "w_ffn1", "w_ffn2", "a_in"), ("w_kv", "w_out", "w_ffn1", "w_ffn2", "b_in"))


def _gather_weights(w):
    halves = lambda k, layer: (w[k][layer] if k in LAYERED else w[k][0]).reshape(2, -1, w[k].shape[-1]).astype(BF16)
    small = jnp.zeros((2, CONV_K, CONV_SHARD), F32)
    small = small.at[0].set(w["b_conv_w"][0])
    small = small.at[1, 0].set(w["b_conv_b"][0])
    small = small.at[1, 1, :GN_SHARD].set(w["b_gnorm"][0])
    gathered = _all_gather_shards([halves(k, 0) for k in LAYER_TENSORS[0]], small, name="gather_weights_0")
    got = dict(zip(LAYER_TENSORS[0], gathered))
    slots = lambda a: a.reshape(N_CHIPS, -1, a.shape[-1])
    rows = lambda a: a.reshape(-1, a.shape[-1])
    p = dict(w_kv=[slots(got["w_kv"])], w_out=[rows(got["w_out"])], w_ffn1=[slots(got["w_ffn1"])],
             w_ffn2=[rows(got["w_ffn2"])], a_in=slots(got["a_in"]))
    sm = gathered[-1]
    p["b_conv_w"] = jnp.transpose(sm[:, 0], (1, 0, 2)).reshape(CONV_K, CONV_DIM)
    p["b_conv_b"] = sm[:, 1, 0].reshape(1, CONV_DIM)
    p["b_gnorm"] = sm[:, 1, 1, :GN_SHARD].reshape(1, D_INNER)

    after, started = (gathered[0],), {}
    for tag, names in (("mixer", ("w_kv", "w_out", "b_in")), ("ffn", ("w_ffn1", "w_ffn2"))):
        started[tag] = _gather_start([halves(k, 1) for k in names], after, name=f"gather_start_1_{tag}")
        after = (started[tag][-1],)
    p["after_start"] = after

    def finish(tag, first):
        send_sems, recv_sems, shards, zones, _ = started[tag]
        shards, zones = _gather_wait(send_sems, recv_sems, shards, zones, (first,), name=f"gather_wait_1_{tag}")
        return _gather_finish(shards, zones, name=f"gather_finish_1_{tag}")

    def layer1_mixer(first):
        kv, wo, b_in = finish("mixer", first)
        return slots(kv), rows(wo), _b_in_full(slots(b_in))

    def layer1_ffn(first):
        w1, w2 = finish("ffn", first)
        return slots(w1), rows(w2)

    p.update(layer1_mixer=layer1_mixer, layer1_ffn=layer1_ffn)
    return p


def _pair_parts(grads, tag):
    stacks = [g.reshape(2, -1, g.shape[-1]) for g in grads.values()]
    parts = _pair_reduce(stacks, name=f"grads_pair_reduce_{tag}")
    return [t.reshape(N_CHIPS, -1, t.shape[-1]) for t in parts]


def _chip_sums(chip, names, parts, landed, tag):
    return {k: _sum_contributions(chip, t, u, name=f"grads_chip_sum_{k}_{tag}")
            for k, t, u in zip(names, parts, landed)}


def _small_layout(shapes):
    offs, o = {}, 0
    for k in (*SMALL_REPL, *SMALL_SHARD):
        size = math.prod(shapes[k])
        offs[k] = (o, size)
        o += size
    rows = -(-o // (8 * 128)) * 8
    return offs, rows


def _reduce_small(g, full_shapes):
    offs, rows = _small_layout(full_shapes)
    flat = jnp.concatenate([g[k].reshape(-1) for k in (*SMALL_REPL, *SMALL_SHARD)])
    flat = jnp.pad(flat, (0, rows * 128 - flat.shape[0])).reshape(rows, 128)
    total = _all_reduce_small(flat, name="grads_small_all_reduce").reshape(-1)
    return {k: total[o:o + n].reshape(full_shapes[k]) for k, (o, n) in offs.items()}


def kernel(x, mem, norm_mix, norm_ffn, mem_norm, w_kv, w_out, w_ffn1, w_ffn2, a_in, a_ln_g, a_ln_b, a_ws, a_bs, b_in, b_conv_w, b_conv_b, b_dt_bias, b_a_log, b_d, b_gnorm, final_norm, loss_target, m_norm_mix, m_norm_ffn, m_mem_norm, m_w_kv, m_w_out, m_w_ffn1, m_w_ffn2, m_a_in, m_a_ln_g, m_a_ln_b, m_a_ws, m_a_bs, m_b_in, m_b_conv_w, m_b_conv_b, m_b_dt_bias, m_b_a_log, m_b_d, m_b_gnorm, m_final_norm, v_norm_mix, v_norm_ffn, v_mem_norm, v_w_kv, v_w_out, v_w_ffn1, v_w_ffn2, v_a_in, v_a_ln_g, v_a_ln_b, v_a_ws, v_a_bs, v_b_in, v_b_conv_w, v_b_conv_b, v_b_dt_bias, v_b_a_log, v_b_d, v_b_gnorm, v_final_norm):
    w = dict(norm_mix=norm_mix, norm_ffn=norm_ffn, mem_norm=mem_norm, w_kv=w_kv, w_out=w_out, w_ffn1=w_ffn1,
             w_ffn2=w_ffn2, a_in=a_in, a_ln_g=a_ln_g, a_ln_b=a_ln_b, a_ws=a_ws, a_bs=a_bs, b_in=b_in, b_conv_w=b_conv_w,
             b_conv_b=b_conv_b, b_dt_bias=b_dt_bias, b_a_log=b_a_log, b_d=b_d, b_gnorm=b_gnorm, final_norm=final_norm)
    mom = dict(norm_mix=m_norm_mix, norm_ffn=m_norm_ffn, mem_norm=m_mem_norm, w_kv=m_w_kv, w_out=m_w_out,
               w_ffn1=m_w_ffn1, w_ffn2=m_w_ffn2, a_in=m_a_in, a_ln_g=m_a_ln_g, a_ln_b=m_a_ln_b, a_ws=m_a_ws,
               a_bs=m_a_bs, b_in=m_b_in, b_conv_w=m_b_conv_w, b_conv_b=m_b_conv_b, b_dt_bias=m_b_dt_bias,
               b_a_log=m_b_a_log, b_d=m_b_d, b_gnorm=m_b_gnorm, final_norm=m_final_norm)
    var = dict(norm_mix=v_norm_mix, norm_ffn=v_norm_ffn, mem_norm=v_mem_norm, w_kv=v_w_kv, w_out=v_w_out,
               w_ffn1=v_w_ffn1, w_ffn2=v_w_ffn2, a_in=v_a_in, a_ln_g=v_a_ln_g, a_ln_b=v_a_ln_b, a_ws=v_a_ws,
               a_bs=v_a_bs, b_in=v_b_in, b_conv_w=v_b_conv_w, b_conv_b=v_b_conv_b, b_dt_bias=v_b_dt_bias,
               b_a_log=v_b_a_log, b_d=v_b_d, b_gnorm=v_b_gnorm, final_norm=v_final_norm)

    p = _gather_weights(w)
    p.update(norm_mix=norm_mix, norm_ffn=norm_ffn, mem_norm=mem_norm, a_ln_g=a_ln_g, a_ln_b=a_ln_b, a_ws=a_ws[0],
             a_bs=a_bs[0], b_dt_bias=b_dt_bias, b_a_log=b_a_log, b_d=b_d, final_norm=final_norm)
    chip = 2 * lax.axis_index("x") + lax.axis_index("y")
    chip_arr = jnp.reshape(chip, (1,)).astype(jnp.int32)
    started = {}

    def start_scatter(tag):
        def hook(grads):
            start = _chip_scatter_start(_pair_parts(grads, tag), name=f"grads_chip_scatter_start_{tag}")
            started[tag] = (tuple(grads), start)
            return start[-1]
        return hook

    loss_part, dx, g, _, _ = _local_step(x[0], mem[0], loss_target[0], p, start_scatter("1"), start_scatter("0f"),
                                         start_scatter("0m"))
    loss = lax.psum(loss_part[0, 0], ("x", "y", "c"))

    def finish_scatter(tag, first):
        names, (send_sems, recv_sems, parts, lands, _) = started[tag]
        parts, landed = _chip_scatter_wait(send_sems, recv_sems, parts, lands, (first,),
                                           name=f"grads_chip_scatter_wait_{tag}")
        return _chip_sums(chip_arr, names, parts, landed, tag)

    def adamw(names, grads):
        for k in names:
            shape = w[k].shape
            flat = (lambda a: a.reshape(-1, shape[-1])) if len(shape) > 1 else (lambda a: a.reshape(1, -1))
            d, m_new, v_new = _adamw(flat(w[k]), flat(grads[k]), flat(mom[k]), flat(var[k]), name=f"adamw_{k}")
            delta[k], new_m[k], new_v[k] = d.reshape(shape), m_new.reshape(shape), v_new.reshape(shape)

    full_shapes = {k: w[k].shape for k in SMALL_REPL}
    full_shapes.update(b_conv_w=(1, CONV_K, CONV_DIM), b_conv_b=(1, CONV_DIM), b_gnorm=(1, D_INNER))
    grads = _reduce_small(g, full_shapes)
    grads["b_conv_w"] = lax.dynamic_slice_in_dim(grads["b_conv_w"], chip * CONV_SHARD, CONV_SHARD, axis=2)
    grads["b_conv_b"] = lax.dynamic_slice_in_dim(grads["b_conv_b"], chip * CONV_SHARD, CONV_SHARD, axis=1)
    grads["b_gnorm"] = lax.dynamic_slice_in_dim(grads["b_gnorm"], chip * GN_SHARD, GN_SHARD, axis=1)
    delta, new_m, new_v = {}, {}, {}
    halves = [finish_scatter("0f", dx), finish_scatter("1", dx)]
    early = ("w_ffn1", "w_ffn2", "b_in")
    shared = _pair_share([[halves[layer][k] for layer in range(2) if k in halves[layer]] for k in early],
                         name="grads_pair_share_early")
    grads.update({k: a.reshape(w[k].shape) for k, a in zip(early, shared)})
    adamw([k for k in WEIGHTS if k in grads], grads)
    halves[0].update(finish_scatter("0m", delta["b_in"]))
    late = ("w_kv", "w_out", "a_in")
    shared = _pair_share([[halves[layer][k] for layer in range(2) if k in halves[layer]] for k in late],
                         name="grads_pair_share_late")
    grads.update({k: a.reshape(w[k].shape) for k, a in zip(late, shared)})
    adamw(late, grads)

    return (loss, dx.reshape(x.shape), *[grads[k] for k in WEIGHTS], *[delta[k] for k in WEIGHTS],
            *[new_m[k] for k in WEIGHTS], *[new_v[k] for k in WEIGHTS])
```

```python
import math

import jax
import jax.numpy as jnp
from jax import lax
from jax.experimental import pallas as pl
from jax.experimental.pallas import tpu as pltpu

F32 = jnp.float32
BF16 = jnp.bfloat16
SDS = jax.ShapeDtypeStruct

D_MODEL = 1024
SEQ = 2048
CHUNK = 128
N_MEM = 256
D_INNER = 2048
A_GROUPS = 8
A_GROUP_W = D_INNER // A_GROUPS
SSM_HEADS = 32
SSM_HEAD_DIM = 64
SSM_GROUPS = 4
SSM_HPG = 8
SSM_STATE = 128
SSM_GROUP_W = SSM_HPG * SSM_HEAD_DIM
CONV_K = 4
CONV_DIM = 3072
X_HEADS = 4
X_HEAD_DIM = 256
X_WIDTH = 1024
MIX_OUT = 3072
D_FF = 4096
A_IN = 5120
B_IN = 6176
B_IN_PAD = 6272
B_Q_OFF = 5120
B_DT_OFF = 6144
N_CHUNKS = SEQ // CHUNK
EPS = 1e-6
N_CHIPS = 4

ADAM_LR = 0.001
ADAM_B1 = 0.9
ADAM_B2 = 0.999
ADAM_EPS = 1e-08
ADAM_WD = 0.01
ADAM_STEP = 10

VMEM_LIMIT = 48 * 1024 * 1024
MESH = pl.DeviceIdType.MESH


def _cparams(sem):
    return pltpu.CompilerParams(dimension_semantics=sem, vmem_limit_bytes=VMEM_LIMIT)


def _dot(a, b, dims=(((1,), (0,)), ((), ()))):
    return lax.dot_general(a.astype(BF16), b.astype(BF16), dims, preferred_element_type=F32)


def _dot_nt(a, b):
    return _dot(a, b, (((1,), (1,)), ((), ())))


def _dot_tn(a, b):
    return _dot(a, b, (((0,), (0,)), ((), ())))


def _pick(n, cands):
    for c in cands:
        if n % c == 0:
            return c
    raise ValueError(f"no tile for {n}")


def _mm_call(a, b, *, dims, grid, a_spec, b_spec, acc_shape, out_shapes, out_specs, name,
             extras=(), extra_specs=(), epilogue=None, after=()):
    n_k = grid[2]
    n_extra = len(extras)
    n_out = len(out_shapes)
    n_in = 2 + n_extra + len(after)

    def finish(total, extra_refs, out_refs):
        vals = (total,) if epilogue is None else epilogue(total, *[e[...] for e in extra_refs])
        for o_ref, v in zip(out_refs, vals):
            o_ref[...] = v.astype(o_ref.dtype)

    def body_one_step(*refs):
        finish(_dot(refs[0][...], refs[1][...], dims), refs[2:2 + n_extra], refs[n_in:n_in + n_out])

    def body(*refs):
        acc = refs[-1]
        k = pl.program_id(2)

        @pl.when(k == 0)
        def _():
            acc[...] = jnp.zeros_like(acc)

        acc[...] += _dot(refs[0][...], refs[1][...], dims)

        @pl.when(k == n_k - 1)
        def _():
            finish(acc[...], refs[2:2 + n_extra], refs[n_in:n_in + n_out])

    return pl.pallas_call(
        body_one_step if n_k == 1 else body, grid=grid,
        in_specs=[a_spec, b_spec, *extra_specs, *([ANY] * len(after))], out_specs=list(out_specs),
        out_shape=list(out_shapes), scratch_shapes=[] if n_k == 1 else [pltpu.VMEM(acc_shape, F32)],
        compiler_params=_cparams(("parallel", "parallel", "arbitrary")), name=name,
    )(a, b, *extras, *after)


def _w_dims(w):
    if w.ndim == 2:
        return w.shape[0], w.shape[1], 1, w.shape[1]
    return w.shape[1], w.shape[0] * w.shape[2], w.shape[0], w.shape[2]


def _mm_nn(a, w, *, name, out_dtype=F32, a_cols=None, extras=(), epilogue=None, n_out_dtypes=None, after=()):
    m = a.shape[0]
    k_dim, n_dim, _, n_slot = _w_dims(w)
    a_off, a_w = (0, a.shape[1]) if a_cols is None else a_cols
    assert a_w == k_dim
    tm = _pick(m, (2048, 1024, 512, 256))
    tn = _pick(n_slot, (512, 896, 640, 256, 128))
    tk = _pick(k_dim, (1024, 768, 512, 384, 256, 128))
    assert a_off % tk == 0
    nb = n_slot // tn
    a_spec = pl.BlockSpec((tm, tk), lambda i, j, k: (i, a_off // tk + k))
    if w.ndim == 2:
        b_spec = pl.BlockSpec((tk, tn), lambda i, j, k: (k, j))
    else:
        b_spec = pl.BlockSpec((None, tk, tn), lambda i, j, k: (j // nb, k, j % nb))
    o_spec = pl.BlockSpec((tm, tn), lambda i, j, k: (i, j))
    dts = n_out_dtypes or (out_dtype,)
    outs = _mm_call(a, w, dims=(((1,), (0,)), ((), ())), grid=(m // tm, n_dim // tn, k_dim // tk),
                    a_spec=a_spec, b_spec=b_spec, acc_shape=(tm, tn),
                    out_shapes=[SDS((m, n_dim), dt) for dt in dts], out_specs=[o_spec] * len(dts), name=name,
                    extras=extras, extra_specs=[o_spec] * len(extras), epilogue=epilogue, after=after)
    return outs if n_out_dtypes else outs[0]


def _mm_nt(a, w, *, name, out_dtype=F32, extras=(), epilogue=None, after=()):
    m = a.shape[0]
    k_dim, n_dim, _, n_slot = _w_dims(w)
    assert a.shape[1] == n_dim
    tm = _pick(m, (2048, 1024, 512, 256))
    to = _pick(k_dim, (512, 384, 256, 128))
    tc = _pick(n_slot, (1280, 1024, 896, 640, 512, 256, 128))
    nb = n_slot // tc
    a_spec = pl.BlockSpec((tm, tc), lambda i, j, k: (i, k))
    if w.ndim == 2:
        b_spec = pl.BlockSpec((to, tc), lambda i, j, k: (j, k))
    else:
        b_spec = pl.BlockSpec((None, to, tc), lambda i, j, k: (k // nb, j, k % nb))
    o_spec = pl.BlockSpec((tm, to), lambda i, j, k: (i, j))
    return _mm_call(a, w, dims=(((1,), (1,)), ((), ())), grid=(m // tm, k_dim // to, n_dim // tc),
                    a_spec=a_spec, b_spec=b_spec, acc_shape=(tm, to),
                    out_shapes=[SDS((m, k_dim), out_dtype)], out_specs=[o_spec], name=name,
                    extras=extras, extra_specs=[o_spec] * len(extras), epilogue=epilogue, after=after)[0]


def _mm_tn(x, dy, *, name, x_cols=None):
    s = x.shape[0]
    x_off, k_dim = (0, x.shape[1]) if x_cols is None else x_cols
    n_dim = dy.shape[1]
    tm = _pick(k_dim, (1024, 768, 512, 384, 256, 128))
    tn = _pick(n_dim, (512, 896, 640, 256, 128))
    tk = _pick(s, (2048, 1024, 512, 256))
    assert x_off % tm == 0
    a_spec = pl.BlockSpec((tk, tm), lambda i, j, k: (k, x_off // tm + i))
    b_spec = pl.BlockSpec((tk, tn), lambda i, j, k: (k, j))
    o_spec = pl.BlockSpec((tm, tn), lambda i, j, k: (i, j))
    return _mm_call(x, dy, dims=(((0,), (0,)), ((), ())), grid=(k_dim // tm, n_dim // tn, s // tk),
                    a_spec=a_spec, b_spec=b_spec, acc_shape=(tm, tn),
                    out_shapes=[SDS((k_dim, n_dim), F32)], out_specs=[o_spec], name=name)[0]


def _mm_tn_stacked(x, dy, *, name, col_slots):
    s, k_dim = x.shape
    n_dim = dy.shape[1]
    r, c = (k_dim // 2, n_dim // N_CHIPS) if col_slots else (k_dim // N_CHIPS // 2, n_dim)
    tm = 2 * r
    tn = _pick(c, (512, 896, 640, 256, 128))
    tk = _pick(s, (2048, 1024, 512, 256))
    a_spec = pl.BlockSpec((tk, tm), lambda i, j, k: (k, i))
    b_spec = pl.BlockSpec((tk, tn), lambda i, j, k: (k, j))
    if col_slots:
        nb = c // tn
        o_spec = pl.BlockSpec((2, None, r, tn), lambda i, j, k: (0, j // nb, 0, j % nb))
    else:
        o_spec = pl.BlockSpec((2, None, r, tn), lambda i, j, k: (0, i, 0, j))
    return _mm_call(x, dy, dims=(((0,), (0,)), ((), ())), grid=(k_dim // tm, n_dim // tn, s // tk),
                    a_spec=a_spec, b_spec=b_spec, acc_shape=(tm, tn), epilogue=lambda acc: (acc.reshape(2, r, tn),),
                    out_shapes=[SDS((2, N_CHIPS, r, c), F32)], out_specs=[o_spec], name=name)[0]


def _rms(x, g):
    return x * lax.rsqrt(jnp.mean(x * x, axis=-1, keepdims=True) + EPS) * g


def _rms_fwd(h, g, *, name):
    rows, d = h.shape
    tr = _pick(rows, (512, 256))

    def body(h_ref, g_ref, o_ref):
        o_ref[...] = _rms(h_ref[...], g_ref[...]).astype(o_ref.dtype)

    return pl.pallas_call(
        body, grid=(rows // tr,),
        in_specs=[pl.BlockSpec((tr, d), lambda i: (i, 0)), pl.BlockSpec((1, d), lambda i: (0, 0))],
        out_specs=pl.BlockSpec((tr, d), lambda i: (i, 0)), out_shape=SDS((rows, d), BF16),
        compiler_params=_cparams(("parallel",)), name=name)(h, g)


def _rms_bwd(h, g, da, dres, *, name):
    rows, d = h.shape
    tr = _pick(rows, (512, 256))

    def body(h_ref, g_ref, da_ref, dres_ref, dh_ref, dg_ref):
        _, vjp = jax.vjp(_rms, h_ref[...], g_ref[...])
        dh, dg = vjp(da_ref[...].astype(F32))
        dh_ref[...] = dres_ref[...] + dh

        @pl.when(pl.program_id(0) == 0)
        def _():
            dg_ref[...] = jnp.zeros_like(dg_ref)

        dg_ref[...] += dg

    row_spec = pl.BlockSpec((tr, d), lambda i: (i, 0))
    vec_spec = pl.BlockSpec((1, d), lambda i: (0, 0))
    return pl.pallas_call(
        body, grid=(rows // tr,), in_specs=[row_spec, vec_spec, row_spec, row_spec],
        out_specs=[row_spec, vec_spec], out_shape=[SDS((rows, d), F32), SDS((1, d), F32)],
        compiler_params=_cparams(("arbitrary",)), name=name)(h, g, da, dres)


def _loss_head(h, g, target, *, name):
    rows, d = h.shape
    tr = _pick(rows, (512, 256))

    def body(h_ref, g_ref, t_ref, loss_ref, dh_ref, dg_ref):
        y, vjp = jax.vjp(_rms, h_ref[...], g_ref[...])
        err = y - t_ref[...]
        dh, dg = vjp(err * (1.0 / d))
        dh_ref[...] = dh

        @pl.when(pl.program_id(0) == 0)
        def _():
            dg_ref[...] = jnp.zeros_like(dg_ref)
            loss_ref[...] = jnp.zeros_like(loss_ref)

        dg_ref[...] += dg
        part = jnp.sum(jnp.sum(err * err, axis=-1, keepdims=True), axis=0, keepdims=True) * (0.5 / d)
        loss_ref[...] += jnp.broadcast_to(part, loss_ref.shape)

    row_spec = pl.BlockSpec((tr, d), lambda i: (i, 0))
    vec_spec = pl.BlockSpec((1, d), lambda i: (0, 0))
    loss_spec = pl.BlockSpec((8, 128), lambda i: (0, 0))
    return pl.pallas_call(
        body, grid=(rows // tr,), in_specs=[row_spec, vec_spec, row_spec],
        out_specs=[loss_spec, row_spec, vec_spec],
        out_shape=[SDS((8, 128), F32), SDS((rows, d), F32), SDS((1, d), F32)],
        compiler_params=_cparams(("arbitrary",)), name=name)(h, g, target)


def _gelu(x):
    return 0.5 * x * (1.0 + lax.erf(x * (1.0 / math.sqrt(2.0))))


def _gate_tile(pu, pv, ln_g, ln_b, ws, bs_t):
    u = [_gelu(p) for p in pu]
    v = [_gelu(p) for p in pv]
    mu = sum(jnp.sum(t, axis=-1, keepdims=True) for t in v) * (1.0 / D_INNER)
    vc = [t - mu for t in v]
    var = sum(jnp.sum(t * t, axis=-1, keepdims=True) for t in vc) * (1.0 / D_INNER)
    rstd = lax.rsqrt(var + EPS)
    row = lax.broadcasted_iota(jnp.int32, (CHUNK, CHUNK), 0)
    col = lax.broadcasted_iota(jnp.int32, (CHUNK, CHUNK), 1)
    out = []
    for gi in range(A_GROUPS):
        vn = vc[gi] * rstd * ln_g[gi] + ln_b[gi]
        w = jnp.where(row >= col, ws[gi], 0.0)
        sv = _dot(w, vn) + bs_t[gi]
        out.append(u[gi] * sv)
    return out


def _split(ref, n, width):
    return [ref[:, i * width:(i + 1) * width] for i in range(n)]


def _gate_in_specs():
    return [
        pl.BlockSpec((CHUNK, D_INNER), lambda c: (c, 0)),
        pl.BlockSpec((CHUNK, D_INNER), lambda c: (c, 1)),
        pl.BlockSpec((1, D_INNER), lambda c: (0, 0)),
        pl.BlockSpec((1, D_INNER), lambda c: (0, 0)),
        pl.BlockSpec((A_GROUPS, CHUNK, CHUNK), lambda c: (0, 0, 0)),
        pl.BlockSpec((A_GROUPS, CHUNK, 1), lambda c: (0, 0, 0)),
    ]


def _gate_args(u_ref, v_ref, g_ref, b_ref, ws_ref, bs_ref):
    ng, gw = A_GROUPS, A_GROUP_W
    return (_split(u_ref, ng, gw), _split(v_ref, ng, gw), _split(g_ref, ng, gw), _split(b_ref, ng, gw),
            [ws_ref[i] for i in range(ng)], [bs_ref[i] for i in range(ng)])


def _gate_fwd(proj, ln_g, ln_b, ws, bs_col, mixcat, *, name):
    def body(u_ref, v_ref, g_ref, b_ref, ws_ref, bs_ref, cat_in, cat_ref):
        del cat_in
        out = _gate_tile(*_gate_args(u_ref, v_ref, g_ref, b_ref, ws_ref, bs_ref))
        for gi, o in enumerate(out):
            cat_ref[:, gi * A_GROUP_W:(gi + 1) * A_GROUP_W] = o.astype(cat_ref.dtype)

    return pl.pallas_call(
        body, grid=(N_CHUNKS,), in_specs=[*_gate_in_specs(), pl.BlockSpec(memory_space=pl.ANY)],
        out_specs=pl.BlockSpec((CHUNK, D_INNER), lambda c: (c, 0)), out_shape=SDS(mixcat.shape, mixcat.dtype),
        input_output_aliases={6: 0}, compiler_params=_cparams(("parallel",)), name=name,
    )(proj, proj, ln_g, ln_b, ws, bs_col, mixcat)


def _gate_bwd(proj, ln_g, ln_b, ws, bs_col, dcat, dproj, *, name):
    ng, gw = A_GROUPS, A_GROUP_W

    def body(u_ref, v_ref, g_ref, b_ref, ws_ref, bs_ref, d_ref, dproj_in, dproj_ref, dg_ref, db_ref, dws_ref, dbs_ref):
        del dproj_in
        args = _gate_args(u_ref, v_ref, g_ref, b_ref, ws_ref, bs_ref)
        _, vjp = jax.vjp(_gate_tile, *args)
        dpu, dpv, dg, db, dws, dbs = vjp(_split(d_ref, ng, gw))
        for gi in range(ng):
            dproj_ref[:, gi * gw:(gi + 1) * gw] = dpu[gi].astype(dproj_ref.dtype)
            dproj_ref[:, D_INNER + gi * gw:D_INNER + (gi + 1) * gw] = dpv[gi].astype(dproj_ref.dtype)

        @pl.when(pl.program_id(0) == 0)
        def _():
            for r in (dg_ref, db_ref, dws_ref, dbs_ref):
                r[...] = jnp.zeros_like(r)

        for gi in range(ng):
            dg_ref[:, gi * gw:(gi + 1) * gw] += dg[gi]
            db_ref[:, gi * gw:(gi + 1) * gw] += db[gi]
            dws_ref[gi] += dws[gi]
            dbs_ref[gi] += dbs[gi]

    in_specs = _gate_in_specs()
    return pl.pallas_call(
        body, grid=(N_CHUNKS,),
        in_specs=[*in_specs, pl.BlockSpec((CHUNK, D_INNER), lambda c: (c, 0)), pl.BlockSpec(memory_space=pl.ANY)],
        out_specs=[pl.BlockSpec((CHUNK, 2 * D_INNER), lambda c: (c, 0)), *in_specs[2:]],
        out_shape=[SDS(dproj.shape, dproj.dtype), SDS((1, D_INNER), F32), SDS((1, D_INNER), F32),
                   SDS((ng, CHUNK, CHUNK), F32), SDS((ng, CHUNK, 1), F32)],
        input_output_aliases={7: 0}, compiler_params=_cparams(("arbitrary",)), name=name,
    )(proj, proj, ln_g, ln_b, ws, bs_col, dcat, dproj)


ATT_TQ = 512


def _attn_tile(q, k, v):
    s = _dot_nt(q, k) * (1.0 / math.sqrt(X_HEAD_DIM))
    s = s - jnp.max(s, axis=-1, keepdims=True)
    e = jnp.exp(s)
    p = e / jnp.sum(e, axis=-1, keepdims=True)
    return _dot(p, v)


def _attn_in_specs(q_blk, order):
    hd = X_HEAD_DIM
    return [
        pl.BlockSpec((ATT_TQ, hd), lambda a, b: (order(a, b)[0], q_blk + order(a, b)[1])),
        pl.BlockSpec((N_MEM, hd), lambda a, b: (0, order(a, b)[1])),
        pl.BlockSpec((N_MEM, hd), lambda a, b: (0, X_HEADS + order(a, b)[1])),
    ]


def _attn_fwd(proj, q_off, kv, *, name):
    order = lambda i, h: (i, h)
    cat_blk = D_INNER // X_HEAD_DIM

    def body(q_ref, k_ref, v_ref, o_ref):
        o_ref[...] = _attn_tile(q_ref[...], k_ref[...], v_ref[...]).astype(o_ref.dtype)

    return pl.pallas_call(
        body, grid=(SEQ // ATT_TQ, X_HEADS), in_specs=_attn_in_specs(q_off // X_HEAD_DIM, order),
        out_specs=pl.BlockSpec((ATT_TQ, X_HEAD_DIM), lambda i, h: (i, cat_blk + h)),
        out_shape=SDS((SEQ, MIX_OUT), BF16), compiler_params=_cparams(("parallel", "parallel")), name=name,
    )(proj, kv, kv)


def _attn_bwd(proj, q_off, kv, dcat, dproj_width, dq_off, *, name):
    order = lambda h, i: (i, h)
    cat_blk = D_INNER // X_HEAD_DIM
    dq_blk = dq_off // X_HEAD_DIM

    def body(q_ref, k_ref, v_ref, do_ref, dq_ref, dk_ref, dv_ref):
        _, vjp = jax.vjp(_attn_tile, q_ref[...], k_ref[...], v_ref[...])
        dq, dk, dv = vjp(do_ref[...])
        dq_ref[...] = dq.astype(dq_ref.dtype)

        @pl.when(pl.program_id(1) == 0)
        def _():
            dk_ref[...] = jnp.zeros_like(dk_ref)
            dv_ref[...] = jnp.zeros_like(dv_ref)

        dk_ref[...] += dk
        dv_ref[...] += dv

    kv_spec = pl.BlockSpec((N_MEM, X_HEAD_DIM), lambda h, i: (0, h))
    return pl.pallas_call(
        body, grid=(X_HEADS, SEQ // ATT_TQ),
        in_specs=[*_attn_in_specs(q_off // X_HEAD_DIM, order),
                  pl.BlockSpec((ATT_TQ, X_HEAD_DIM), lambda h, i: (i, cat_blk + h))],
        out_specs=[pl.BlockSpec((ATT_TQ, X_HEAD_DIM), lambda h, i: (i, dq_blk + h)), kv_spec, kv_spec],
        out_shape=[SDS((SEQ, dproj_width), BF16), SDS((N_MEM, X_WIDTH), F32), SDS((N_MEM, X_WIDTH), F32)],
        compiler_params=_cparams(("parallel", "arbitrary")), name=name,
    )(proj, kv, kv, dcat)


CONV_TC = 512


def _shift_down(x, s):
    if s == 0:
        return x
    row = lax.broadcasted_iota(jnp.int32, x.shape, 0)
    return jnp.where(row >= s, pltpu.roll(x, s, 0), 0.0)


def _shift_up(x, s):
    if s == 0:
        return x
    n = x.shape[0]
    row = lax.broadcasted_iota(jnp.int32, x.shape, 0)
    return jnp.where(row < n - s, pltpu.roll(x, n - s, 0), 0.0)


def _conv_pre(x, w_ref, b_ref):
    pre = b_ref[...] + jnp.zeros_like(x)
    for k in range(CONV_K):
        pre = pre + w_ref[k:k + 1, :] * _shift_down(x, CONV_K - 1 - k)
    return pre


def _conv_fwd(proj, w, b, *, name):
    blk0 = D_INNER // CONV_TC

    def body(x_ref, w_ref, b_ref, o_ref):
        pre = _conv_pre(x_ref[...], w_ref, b_ref)
        o_ref[...] = pre * jax.nn.sigmoid(pre)

    return pl.pallas_call(
        body, grid=(CONV_DIM // CONV_TC,),
        in_specs=[pl.BlockSpec((SEQ, CONV_TC), lambda j: (0, blk0 + j)), pl.BlockSpec((CONV_K, CONV_TC), lambda j: (0, j)),
                  pl.BlockSpec((1, CONV_TC), lambda j: (0, j))],
        out_specs=pl.BlockSpec((SEQ, CONV_TC), lambda j: (0, j)), out_shape=SDS((SEQ, CONV_DIM), F32),
        compiler_params=_cparams(("parallel",)), name=name)(proj, w, b)


def _conv_bwd(proj, w, b, dxs, dbm, dcm, dproj, *, name):
    tc = CONV_TC // 2
    blk0 = D_INNER // tc
    n_x = D_INNER // tc
    n_b = SSM_GROUPS * SSM_STATE // tc

    def body(x_ref, w_ref, b_ref, dxs_ref, dbm_ref, dcm_ref, dproj_in, dproj_ref, dw_ref, db_ref):
        del dproj_in
        j = pl.program_id(0)
        x = x_ref[...]
        pre = _conv_pre(x, w_ref, b_ref)
        sg = jax.nn.sigmoid(pre)
        dact = jnp.where(j < n_x, dxs_ref[...], jnp.where(j < n_x + n_b, dbm_ref[...], dcm_ref[...]))
        dpre = dact * (sg * (1.0 + pre * (1.0 - sg)))
        dx = jnp.zeros_like(x)
        for k in range(CONV_K):
            s = CONV_K - 1 - k
            dx = dx + w_ref[k:k + 1, :] * _shift_up(dpre, s)
            dw_ref[k:k + 1, :] = jnp.sum(dpre * _shift_down(x, s), axis=0, keepdims=True)
        dproj_ref[...] = dx.astype(dproj_ref.dtype)
        db_ref[...] = jnp.sum(dpre, axis=0, keepdims=True)

    clip = lambda v, hi: jnp.minimum(jnp.maximum(v, 0), hi)
    return pl.pallas_call(
        body, grid=(CONV_DIM // tc,),
        in_specs=[pl.BlockSpec((SEQ, tc), lambda j: (0, blk0 + j)), pl.BlockSpec((CONV_K, tc), lambda j: (0, j)),
                  pl.BlockSpec((1, tc), lambda j: (0, j)),
                  pl.BlockSpec((SEQ, tc), lambda j: (0, clip(j, n_x - 1))),
                  pl.BlockSpec((SEQ, tc), lambda j: (0, clip(j - n_x, n_b - 1))),
                  pl.BlockSpec((SEQ, tc), lambda j: (0, clip(j - n_x - n_b, n_b - 1))),
                  pl.BlockSpec(memory_space=pl.ANY)],
        out_specs=[pl.BlockSpec((SEQ, tc), lambda j: (0, blk0 + j)), pl.BlockSpec((CONV_K, tc), lambda j: (0, j)),
                   pl.BlockSpec((1, tc), lambda j: (0, j))],
        out_shape=[SDS(dproj.shape, dproj.dtype), SDS((CONV_K, CONV_DIM), F32), SDS((1, CONV_DIM), F32)],
        input_output_aliases={6: 0}, compiler_params=_cparams(("parallel",)), name=name,
    )(proj, w, b, dxs, dbm, dcm, dproj)


SSM_PAIRS = SSM_HPG // 2


def _dot_exact01(x, m01, m01_t, x_first, differentiable):
    def product(v, m):
        hi = v.astype(BF16)
        rest = v - hi.astype(F32)
        mid = rest.astype(BF16)
        lo = (rest - mid.astype(F32)).astype(BF16)
        dims = (((1,), (0,)), ((), ()))
        dot = lambda part: lax.dot_general(*((part, m) if x_first else (m, part)), dims, preferred_element_type=F32)
        return dot(hi) + dot(mid) + dot(lo)

    if not differentiable:
        return product(x, m01)

    @jax.custom_vjp
    def exact(v):
        return product(v, m01)

    exact.defvjp(lambda v: (product(v, m01), None), lambda _, ct: (product(ct, m01_t),))
    return exact(x)


def _ssd_tile(xp, zp, bm, cm, hp, dt_c, dt_r, bias, bias_col, alog, alog_col, dsk, gnp, differentiable=False):
    row = lax.broadcasted_iota(jnp.int32, (CHUNK, CHUNK), 0)
    col = lax.broadcasted_iota(jnp.int32, (CHUNK, CHUNK), 1)
    causal = row >= col
    left = col < SSM_HEAD_DIM
    top = row < SSM_HEAD_DIM
    ones = jnp.ones((CHUNK, CHUNK), BF16)
    cb = _dot_nt(cm, bm)
    dtp = jax.nn.softplus(dt_c + bias)
    da_c = dtp * -jnp.exp(alog)
    da_r = jax.nn.softplus(dt_r + bias_col) * -jnp.exp(alog_col)
    lower = jnp.where(causal, 1.0, 0.0).astype(BF16)
    upper = jnp.where(row <= col, 1.0, 0.0).astype(BF16)
    cs = _dot_exact01(da_c, lower, upper, False, differentiable)
    cs_rows = _dot_exact01(da_r, upper, lower, True, differentiable)
    cs_last = jnp.sum(da_c, axis=0, keepdims=True)
    ecs, decay, ecl = jnp.exp(cs), jnp.exp(cs_last - cs), jnp.exp(cs_last)
    m = [cb * jnp.exp(jnp.where(causal, cs[:, r:r + 1] - cs_rows[r:r + 1, :], -1e30)) for r in range(SSM_HPG)]
    ygs, hn = [], []
    for p in range(SSM_PAIRS):
        a, b = 2 * p, 2 * p + 1
        pair = lambda v: jnp.where(left, v[:, a:a + 1], v[:, b:b + 1])
        xdt = xp[p] * pair(dtp)
        y = jnp.where(left, _dot(m[a], xdt), _dot(m[b], xdt))
        y = y + _dot_nt(cm, hp[p]) * pair(ecs)
        y = y + xp[p] * pair(dsk)
        states = _dot_tn(xdt * pair(decay), bm)
        hn.append(hp[p] * jnp.where(top, ecl[:, a:a + 1], ecl[:, b:b + 1]) + states)
        ygs.append(y * (zp[p] * jax.nn.sigmoid(zp[p])))
    ms = sum(_dot(t * t, ones) for t in ygs) * (1.0 / SSM_GROUP_W)
    rs = lax.rsqrt(ms + EPS)
    return [ygs[p] * rs * gnp[p] for p in range(SSM_PAIRS)], hn


def _ssd_in_specs(cidx):
    gw, n = SSM_GROUP_W, SSM_STATE
    bm_blk = D_INNER // n
    return [
        pl.BlockSpec((CHUNK, gw), lambda g, c: (cidx(c), g)),
        pl.BlockSpec((CHUNK, gw), lambda g, c: (cidx(c), g)),
        pl.BlockSpec((CHUNK, n), lambda g, c: (cidx(c), bm_blk + g)),
        pl.BlockSpec((CHUNK, n), lambda g, c: (cidx(c), bm_blk + SSM_GROUPS + g)),
        pl.BlockSpec((None, CHUNK, SSM_HPG), lambda g, c: (g, cidx(c), 0)),
        pl.BlockSpec((None, SSM_HPG, CHUNK), lambda g, c: (g, 0, cidx(c))),
        pl.BlockSpec((None, 3, SSM_HPG), lambda g, c: (g, 0, 0)),
        pl.BlockSpec((None, SSM_HPG, 2), lambda g, c: (g, 0, 0)),
        pl.BlockSpec((1, gw), lambda g, c: (0, g)),
    ]


def _ssd_args(x_ref, z_ref, bm_ref, cm_ref, hp, dtc_ref, dtr_ref, prow_ref, pcol_ref, gn_ref):
    npair, w = SSM_PAIRS, 2 * SSM_HEAD_DIM
    return (_split(x_ref, npair, w), _split(z_ref, npair, w), bm_ref[...], cm_ref[...], hp, dtc_ref[...], dtr_ref[...],
            prow_ref[0:1, :], pcol_ref[:, 0:1], prow_ref[1:2, :], pcol_ref[:, 1:2], prow_ref[2:3, :],
            _split(gn_ref, npair, w))


def _pair_rows(ref):
    w = 2 * SSM_HEAD_DIM
    return [ref[p * w:(p + 1) * w, :] for p in range(SSM_PAIRS)]


def _ssd_fwd(xbc, proj, dt_c, dt_r, par_row, par_col, gn, mixcat, *, name):
    w = 2 * SSM_HEAD_DIM

    def body(x_ref, z_ref, bm_ref, cm_ref, dtc_ref, dtr_ref, prow_ref, pcol_ref, gn_ref, cat_in,
             cat_ref, hprev_ref, h_scr):
        del cat_in

        @pl.when(pl.program_id(1) == 0)
        def _():
            h_scr[...] = jnp.zeros_like(h_scr)

        hprev_ref[...] = h_scr[...]
        yn, hn = _ssd_tile(*_ssd_args(x_ref, z_ref, bm_ref, cm_ref, _pair_rows(h_scr), dtc_ref, dtr_ref, prow_ref,
                                      pcol_ref, gn_ref))
        for p in range(SSM_PAIRS):
            cat_ref[:, p * w:(p + 1) * w] = yn[p].astype(cat_ref.dtype)
            h_scr[p * w:(p + 1) * w, :] = hn[p]

    return pl.pallas_call(
        body, grid=(SSM_GROUPS, N_CHUNKS), in_specs=[*_ssd_in_specs(lambda c: c), pl.BlockSpec(memory_space=pl.ANY)],
        out_specs=[pl.BlockSpec((CHUNK, SSM_GROUP_W), lambda g, c: (c, g)),
                   pl.BlockSpec((None, None, SSM_GROUP_W, SSM_STATE), lambda g, c: (c, g, 0, 0))],
        out_shape=[SDS(mixcat.shape, mixcat.dtype), SDS((N_CHUNKS, SSM_GROUPS, SSM_GROUP_W, SSM_STATE), F32)],
        scratch_shapes=[pltpu.VMEM((SSM_GROUP_W, SSM_STATE), F32)],
        input_output_aliases={9: 0}, compiler_params=_cparams(("parallel", "arbitrary")), name=name,
    )(xbc, proj, xbc, xbc, dt_c, dt_r, par_row, par_col, gn, mixcat)


def _ssd_bwd(xbc, proj, dt_c, dt_r, par_row, par_col, gn, hprev, dcat, dproj, *, name):
    nh, w, gw, n = SSM_HPG, 2 * SSM_HEAD_DIM, SSM_GROUP_W, SSM_STATE
    rev = lambda c: N_CHUNKS - 1 - c

    def body(x_ref, z_ref, bm_ref, cm_ref, dtc_ref, dtr_ref, prow_ref, pcol_ref, gn_ref, hprev_ref, dy_ref,
             dproj_in, dz_ref, dxs_ref, dbm_ref, dcm_ref, ddtc_ref, ddtr_ref, dprow_ref, dpcol_ref, dgn_ref, dh_scr):
        del dproj_in
        first = pl.program_id(1) == 0

        @pl.when(first)
        def _():
            dh_scr[...] = jnp.zeros_like(dh_scr)
            for ref in (dprow_ref, dpcol_ref, dgn_ref):
                ref[...] = jnp.zeros_like(ref)

        args = _ssd_args(x_ref, z_ref, bm_ref, cm_ref, _pair_rows(hprev_ref), dtc_ref, dtr_ref, prow_ref, pcol_ref,
                         gn_ref)
        _, vjp = jax.vjp(lambda *a: _ssd_tile(*a, differentiable=True), *args)
        dxs, dzs, dbm, dcm, dhs, ddtc, ddtr, dbias, dbias_col, dalog, dalog_col, ddsk, dgn = vjp(
            (_split(dy_ref, SSM_PAIRS, w), _pair_rows(dh_scr)))
        dbm_ref[...] = dbm
        dcm_ref[...] = dcm
        ddtc_ref[...] = ddtc
        ddtr_ref[...] = ddtr
        for q in range(SSM_PAIRS):
            dxs_ref[:, q * w:(q + 1) * w] = dxs[q]
            dz_ref[:, q * w:(q + 1) * w] = dzs[q].astype(dz_ref.dtype)
            dh_scr[q * w:(q + 1) * w, :] = dhs[q]
            dgn_ref[:, q * w:(q + 1) * w] += dgn[q]
        for i, d in enumerate((dbias, dalog, ddsk)):
            dprow_ref[i:i + 1, :] += d
        for i, d in enumerate((dbias_col, dalog_col)):
            dpcol_ref[:, i:i + 1] += d

    return pl.pallas_call(
        body, grid=(SSM_GROUPS, N_CHUNKS),
        in_specs=[*_ssd_in_specs(rev),
                  pl.BlockSpec((None, None, gw, n), lambda g, c: (rev(c), g, 0, 0)),
                  pl.BlockSpec((CHUNK, gw), lambda g, c: (rev(c), g)),
                  pl.BlockSpec(memory_space=pl.ANY)],
        out_specs=[pl.BlockSpec((CHUNK, gw), lambda g, c: (rev(c), g)),
                   pl.BlockSpec((CHUNK, gw), lambda g, c: (rev(c), g)),
                   pl.BlockSpec((CHUNK, n), lambda g, c: (rev(c), g)),
                   pl.BlockSpec((CHUNK, n), lambda g, c: (rev(c), g)),
                   pl.BlockSpec((None, CHUNK, nh), lambda g, c: (g, rev(c), 0)),
                   pl.BlockSpec((None, nh, CHUNK), lambda g, c: (g, 0, rev(c))),
                   pl.BlockSpec((None, 3, nh), lambda g, c: (g, 0, 0)),
                   pl.BlockSpec((None, nh, 2), lambda g, c: (g, 0, 0)),
                   pl.BlockSpec((1, gw), lambda g, c: (0, g))],
        out_shape=[SDS(dproj.shape, dproj.dtype), SDS((SEQ, D_INNER), F32), SDS((SEQ, SSM_GROUPS * n), F32),
                   SDS((SEQ, SSM_GROUPS * n), F32), SDS((SSM_GROUPS, SEQ, nh), F32), SDS((SSM_GROUPS, nh, SEQ), F32),
                   SDS((SSM_GROUPS, 3, nh), F32), SDS((SSM_GROUPS, nh, 2), F32), SDS((1, D_INNER), F32)],
        scratch_shapes=[pltpu.VMEM((gw, n), F32)],
        input_output_aliases={11: 0}, compiler_params=_cparams(("parallel", "arbitrary")), name=name,
    )(xbc, proj, xbc, xbc, dt_c, dt_r, par_row, par_col, gn, hprev, dcat, dproj)


def _sum_contributions(chip, parts, landed, *, name):
    _, r, c = parts.shape
    tr = _pick(r, (256, 384, 128))

    def body(chip_ref, own_ref, landed_ref, o_ref):
        del chip_ref
        acc = own_ref[...].astype(F32)
        for s in range(landed_ref.shape[0]):
            acc = acc + landed_ref[s].astype(F32)
        o_ref[...] = acc

    grid_spec = pltpu.PrefetchScalarGridSpec(
        num_scalar_prefetch=1, grid=(r // tr,),
        in_specs=[pl.BlockSpec((None, tr, c), lambda i, chip_ref: (chip_ref[0], i, 0)),
                  pl.BlockSpec((landed.shape[0], tr, c), lambda i, chip_ref: (0, i, 0))],
        out_specs=pl.BlockSpec((tr, c), lambda i, chip_ref: (i, 0)))
    return pl.pallas_call(body, grid_spec=grid_spec, out_shape=SDS((r, c), F32),
                          compiler_params=_cparams(("parallel",)), name=name)(chip, parts, landed)


def _adamw(w, g, m, v, *, name):
    r, c = w.shape
    tr = r if r <= 256 else _pick(r, (256, 128, 8))
    spec = pl.BlockSpec((tr, c), lambda i: (i, 0))

    def body(w_ref, g_ref, m_ref, v_ref, d_ref, mo_ref, vo_ref):
        g = g_ref[...]
        m_new = ADAM_B1 * m_ref[...] + (1.0 - ADAM_B1) * g
        v_new = ADAM_B2 * v_ref[...] + (1.0 - ADAM_B2) * (g * g)
        m_hat = m_new / (1.0 - ADAM_B1 ** ADAM_STEP)
        v_hat = v_new / (1.0 - ADAM_B2 ** ADAM_STEP)
        d_ref[...] = -ADAM_LR * (m_hat / (jnp.sqrt(v_hat) + ADAM_EPS) + ADAM_WD * w_ref[...])
        mo_ref[...] = m_new
        vo_ref[...] = v_new

    return pl.pallas_call(body, grid=(r // tr,), in_specs=[spec] * 4, out_specs=[spec] * 3,
                          out_shape=[SDS((r, c), F32)] * 3, compiler_params=_cparams(("parallel",)), name=name)(w, g, m, v)


ANY = pl.BlockSpec(memory_space=pl.ANY)


def _place():
    x, y, c = lax.axis_index("x"), lax.axis_index("y"), lax.axis_index("c")
    chips = [(1 - x, y), (x, 1 - y), (1 - x, 1 - y)]
    return x, y, c, chips


def _remote(src, dst, send_sem, recv_sem, to):
    return pltpu.make_async_remote_copy(src_ref=src, dst_ref=dst, send_sem=send_sem, recv_sem=recv_sem,
                                        device_id=to, device_id_type=MESH)


STREAM_ROWS = 256


def _stream_rows(i):
    return pl.ds(pl.multiple_of(i * STREAM_ROWS, STREAM_ROWS), STREAM_ROWS)


def _channel_scratch(width, dtype, rows=STREAM_ROWS):
    buf = (2, rows, width)
    return [pltpu.VMEM(buf, dtype), pltpu.VMEM(buf, dtype), *([pltpu.SemaphoreType.DMA((2,))] * 5),
            pltpu.SemaphoreType.REGULAR((2,))]


CHANNEL_REFS = 8


def _copy_blocks(srcs, dsts, ch):
    sbuf, _, ld, _, _, st, _, _ = ch
    n = len(srcs)
    load = lambda i: pltpu.make_async_copy(srcs[i], sbuf.at[i % 2], ld.at[i % 2])
    store = lambda i: pltpu.make_async_copy(sbuf.at[i % 2], dsts[i], st.at[i % 2])
    load(0).start()
    for i in range(n):
        if i + 1 < n:
            if i >= 1:
                store(i - 1).wait()
            load(i + 1).start()
        load(i).wait()
        store(i).start()
    for i in range(max(0, n - 2), n):
        store(i).wait()


def _exchange_blocks(srcs, dsts, keeps, ch, sibling):
    sbuf, rbuf, ld, snd, rcv, st, kp, credit = ch
    n = len(srcs)
    load = lambda i: pltpu.make_async_copy(srcs[i], sbuf.at[i % 2], ld.at[i % 2])
    push = lambda i: _remote(sbuf.at[i % 2], rbuf.at[i % 2], snd.at[i % 2], rcv.at[i % 2], sibling)
    store = lambda i: pltpu.make_async_copy(rbuf.at[i % 2], dsts[i], st.at[i % 2])
    save = lambda i: pltpu.make_async_copy(sbuf.at[i % 2], keeps[i], kp.at[i % 2])

    def send(i):
        load(i).wait()
        pl.semaphore_wait(credit.at[i % 2], 1)
        push(i).start()
        if keeps[i] is not None:
            save(i).start()

    for i in range(min(2, n)):
        pl.semaphore_signal(credit.at[i], 1, device_id=sibling, device_id_type=MESH)
        load(i).start()
    send(0)
    for i in range(n):
        if i >= 1:
            store(i - 1).wait()
            if i + 1 < n:
                pl.semaphore_signal(credit.at[(i + 1) % 2], 1, device_id=sibling, device_id_type=MESH)
        if i + 1 < n:
            send(i + 1)
        push(i).wait_recv()
        store(i).start()
        push(i).wait_send()
        if keeps[i] is not None:
            save(i).wait()
        if i + 2 < n:
            load(i + 2).start()
    store(n - 1).wait()


def _all_gather_shards(shards, small, *, name):
    n = len(shards)

    def body(*refs):
        ins, outs = refs[:n + 1], refs[n + 1:2 * n + 2]
        scr = refs[2 * n + 2:]
        chans = [scr[CHANNEL_REFS * t:CHANNEL_REFS * (t + 1)] for t in range(n)]
        send_sems, recv_sems, small_sems = scr[CHANNEL_REFS * n:]
        x, y, c, _ = _place()
        me = 2 * x + y
        sibling = (x, y, 1 - c)
        near = (lax.rem(x + 1 - c, 2), lax.rem(y + c, 2))
        far = (lax.rem(x + c, 2), lax.rem(y + 1 - c, 2))
        k_near, k_far, k_diag = 2 * near[0] + near[1], 2 * far[0] + far[1], 3 - me
        targets = ((*near, c), (*far, c), (*far, c))
        arrives = (k_near, k_far, k_diag)
        streams_in = (k_far, k_near, k_diag)

        def ici(t, j, src, blk):
            return _remote(src, outs[t].at[blk, c], send_sems.at[3 * t + j], recv_sems.at[3 * t + j], targets[j])

        first = [ici(t, j, ins[t].at[c], me) for t in range(n + 1) for j in range(2)]
        for cp in first:
            cp.start()
        small_local = pltpu.make_async_copy(ins[n], outs[n].at[me], small_sems.at[6])
        small_local.start()
        for t in range(n):
            _copy_blocks([ins[t].at[h] for h in range(2)], [outs[t].at[me, h] for h in range(2)], chans[t])
        passed = []
        for j in range(3):
            for t in range(n + 1):
                landed = outs[t].at[arrives[j], c]
                ici(t, j, landed, arrives[j]).wait_recv()
                if j == 0:
                    fwd = ici(t, 2, landed, k_near)
                    fwd.start()
                    passed.append(fwd)
                if t < n:
                    _exchange_blocks([landed], [outs[t].at[streams_in[j], 1 - c]], [None], chans[t], sibling)
                else:
                    fwd = _remote(landed, landed, small_sems.at[j], small_sems.at[3 + j], sibling)
                    fwd.start()
                    passed.append(fwd)
        for j in range(3):
            got = outs[n].at[streams_in[j], 1 - c]
            _remote(got, got, small_sems.at[j], small_sems.at[3 + j], sibling).wait_recv()
        for cp in first + passed:
            cp.wait_send()
        small_local.wait()

    scratch = []
    for s in shards:
        scratch += _channel_scratch(s.shape[2], s.dtype, rows=s.shape[1])
    return pl.pallas_call(
        body, in_specs=[ANY] * (n + 1), out_specs=[ANY] * (n + 1),
        out_shape=[SDS((N_CHIPS, *s.shape), s.dtype) for s in (*shards, small)],
        scratch_shapes=[*scratch, pltpu.SemaphoreType.DMA((3 * n + 3,)), pltpu.SemaphoreType.DMA((3 * n + 3,)),
                        pltpu.SemaphoreType.DMA((7,))],
        compiler_params=pltpu.CompilerParams(vmem_limit_bytes=VMEM_LIMIT), name=name)(*shards, small)


def _pair_reduce(stacks, *, name):
    n = len(stacks)
    per = 11

    def body(*refs):
        ins, outs, scr = refs[:n], refs[n:2 * n], refs[2 * n:]
        x, y, c, _ = _place()
        sibling = (x, y, 1 - c)
        for t in range(n):
            sraw, sbuf, rbuf, obuf, pbuf, ld_s, ld_o, snd, rcv, st, credit = scr[per * t:per * (t + 1)]
            steps = ins[t].shape[1] // STREAM_ROWS
            src, own, out = ins[t].at[1 - c], ins[t].at[c], outs[t]

            def load_s(i, slot, src=src, sraw=sraw, ld_s=ld_s):
                return pltpu.make_async_copy(src.at[_stream_rows(i)], sraw.at[slot], ld_s.at[slot])

            def load_o(i, slot, own=own, obuf=obuf, ld_o=ld_o):
                return pltpu.make_async_copy(own.at[_stream_rows(i)], obuf.at[slot], ld_o.at[slot])

            def push(slot, sbuf=sbuf, rbuf=rbuf, snd=snd, rcv=rcv):
                return _remote(sbuf.at[slot], rbuf.at[slot], snd.at[slot], rcv.at[slot], sibling)

            def store(i, slot, pbuf=pbuf, out=out, st=st):
                return pltpu.make_async_copy(pbuf.at[slot], out.at[_stream_rows(i)], st.at[slot])

            assert steps >= 2
            for slot in range(2):
                pl.semaphore_signal(credit.at[slot], 1, device_id=sibling, device_id_type=MESH)
                load_s(slot, slot).start()
                load_o(slot, slot).start()
            load_s(0, 0).wait()
            sbuf[0] = sraw[0].astype(sbuf.dtype)
            pl.semaphore_wait(credit.at[0], 1)
            push(0).start()

            def step(i, carry, load_s=load_s, load_o=load_o, push=push, store=store, sraw=sraw, sbuf=sbuf, rbuf=rbuf,
                     obuf=obuf, pbuf=pbuf, credit=credit, steps=steps):
                slot = lax.rem(i, 2)
                nxt = 1 - slot

                @pl.when(i + 1 < steps)
                def _():
                    load_s(i + 1, nxt).wait()
                    sbuf[nxt] = sraw[nxt].astype(sbuf.dtype)
                    pl.semaphore_wait(credit.at[nxt], 1)
                    push(nxt).start()

                load_o(i, slot).wait()
                push(slot).wait_recv()

                @pl.when(i >= 2)
                def _():
                    store(i, slot).wait()

                pbuf[slot] = (obuf[slot] + rbuf[slot].astype(F32)).astype(pbuf.dtype)
                store(i, slot).start()
                push(slot).wait_send()

                @pl.when(i + 2 < steps)
                def _():
                    load_s(i + 2, slot).start()
                    load_o(i + 2, slot).start()
                    pl.semaphore_signal(credit.at[slot], 1, device_id=sibling, device_id_type=MESH)
                return carry

            lax.fori_loop(0, steps, step, 0)
            for slot in range(2):
                store(0, slot).wait()

    scratch = []
    for s in stacks:
        buf = (2, STREAM_ROWS, s.shape[2])
        scratch += [pltpu.VMEM(buf, F32), pltpu.VMEM(buf, BF16), pltpu.VMEM(buf, BF16), pltpu.VMEM(buf, F32),
                    pltpu.VMEM(buf, BF16), *([pltpu.SemaphoreType.DMA((2,))] * 5), pltpu.SemaphoreType.REGULAR((2,))]
    return pl.pallas_call(
        body, in_specs=[ANY] * n, out_specs=[ANY] * n, out_shape=[SDS(s.shape[1:], BF16) for s in stacks],
        scratch_shapes=scratch, compiler_params=pltpu.CompilerParams(vmem_limit_bytes=VMEM_LIMIT), name=name)(*stacks)


HBM_SPEC = pl.BlockSpec(memory_space=pltpu.HBM)
SEM_SPEC = pl.BlockSpec(memory_space=pltpu.SEMAPHORE)
SIDE_EFFECT = pltpu.SideEffectType.DATAFLOW_SIDE_EFFECTING


def _scatter_copies(ins, lands, send_sems, recv_sems):
    _, _, c, chips = _place()
    return [_remote(ins[t].at[2 * cx + cy], lands[t].at[j], send_sems.at[3 * t + j], recv_sems.at[3 * t + j],
                    (cx, cy, c)) for t in range(len(ins)) for j, (cx, cy) in enumerate(chips)]


def _chip_scatter_start(parts, *, name):
    n = len(parts)

    def body(*refs):
        ins, lands = refs[:n], refs[n:2 * n]
        send_sems, recv_sems, token = refs[2 * n], refs[2 * n + 1], refs[-1]
        for cp in _scatter_copies(ins, lands, send_sems, recv_sems):
            cp.start()
        token[...] = jnp.zeros_like(token)

    hbm = lambda a: pltpu.with_memory_space_constraint(a, pltpu.HBM)
    lands = [hbm(lax.empty((3, *p.shape[1:]), p.dtype)) for p in parts]
    thru = [pltpu.HBM(a.shape, a.dtype) for a in (*parts, *lands)]
    outs = pl.pallas_call(
        body, name=name,
        out_shape=(pltpu.SemaphoreType.DMA((3 * n,)), pltpu.SemaphoreType.DMA((3 * n,)), *thru, SDS((8, 128), F32)),
        in_specs=[HBM_SPEC] * (2 * n),
        out_specs=(SEM_SPEC, SEM_SPEC, *([HBM_SPEC] * (2 * n)), pl.BlockSpec(memory_space=pltpu.VMEM)),
        input_output_aliases={i: 2 + i for i in range(2 * n)},
        compiler_params=pltpu.CompilerParams(has_side_effects=SIDE_EFFECT),
    )(*[hbm(p) for p in parts], *lands)
    return outs[0], outs[1], outs[2:2 + n], outs[2 + n:2 + 2 * n], outs[-1]


def _chip_scatter_wait(send_sems, recv_sems, parts, lands, after, *, name):
    n = len(parts)

    def body(*refs):
        ins, lands_in = refs[:n], refs[n:2 * n]
        for cp in _scatter_copies(ins, lands_in, refs[2 * n], refs[2 * n + 1]):
            cp.wait_send()
            cp.wait_recv()

    outs = pl.pallas_call(
        body, name=name, out_shape=[pltpu.HBM(a.shape, a.dtype) for a in (*parts, *lands)],
        in_specs=[*([HBM_SPEC] * (2 * n)), SEM_SPEC, SEM_SPEC, *([ANY] * len(after))],
        out_specs=[HBM_SPEC] * (2 * n), input_output_aliases={i: i for i in range(2 * n)},
        compiler_params=pltpu.CompilerParams(has_side_effects=SIDE_EFFECT),
    )(*parts, *lands, send_sems, recv_sems, *after)
    return outs[:n], outs[n:]


def _gather_copies(shards, zones, send_sems, recv_sems):
    x, y, c, chips = _place()
    return [_remote(shards[t].at[c], zones[t].at[2 * x + y, c], send_sems.at[3 * t + j], recv_sems.at[3 * t + j],
                    (cx, cy, c)) for t in range(len(shards)) for j, (cx, cy) in enumerate(chips)]


def _gather_start(shards, after, *, name):
    n = len(shards)

    def body(*refs):
        ins, zones = refs[:n], refs[n:2 * n]
        send_sems, recv_sems, token = refs[2 * n + len(after)], refs[2 * n + len(after) + 1], refs[-1]
        for cp in _gather_copies(ins, zones, send_sems, recv_sems):
            cp.start()
        token[...] = jnp.zeros_like(token)

    hbm = lambda a: pltpu.with_memory_space_constraint(a, pltpu.HBM)
    zones = [hbm(lax.empty((N_CHIPS, *s.shape), s.dtype)) for s in shards]
    thru = [pltpu.HBM(a.shape, a.dtype) for a in (*shards, *zones)]
    outs = pl.pallas_call(
        body, name=name,
        out_shape=(pltpu.SemaphoreType.DMA((3 * n,)), pltpu.SemaphoreType.DMA((3 * n,)), *thru, SDS((8, 128), F32)),
        in_specs=[*([HBM_SPEC] * (2 * n)), *([ANY] * len(after))],
        out_specs=(SEM_SPEC, SEM_SPEC, *([HBM_SPEC] * (2 * n)), pl.BlockSpec(memory_space=pltpu.VMEM)),
        input_output_aliases={i: 2 + i for i in range(2 * n)},
        compiler_params=pltpu.CompilerParams(has_side_effects=SIDE_EFFECT),
    )(*[hbm(s) for s in shards], *zones, *after)
    return outs[0], outs[1], outs[2:2 + n], outs[2 + n:2 + 2 * n], outs[-1]


def _gather_wait(send_sems, recv_sems, shards, zones, after, *, name):
    n = len(shards)

    def body(*refs):
        for cp in _gather_copies(refs[:n], refs[n:2 * n], refs[2 * n], refs[2 * n + 1]):
            cp.wait_send()
            cp.wait_recv()

    outs = pl.pallas_call(
        body, name=name, out_shape=[pltpu.HBM(a.shape, a.dtype) for a in (*shards, *zones)],
        in_specs=[*([HBM_SPEC] * (2 * n)), SEM_SPEC, SEM_SPEC, *([ANY] * len(after))],
        out_specs=[HBM_SPEC] * (2 * n), input_output_aliases={i: i for i in range(2 * n)},
        compiler_params=pltpu.CompilerParams(has_side_effects=SIDE_EFFECT),
    )(*shards, *zones, send_sems, recv_sems, *after)
    return outs[:n], outs[n:]


def _gather_finish(shards, zones, *, name):
    n = len(shards)

    def body(*refs):
        ins, zones_in, outs, scr = refs[:n], refs[n:2 * n], refs[2 * n:3 * n], refs[3 * n:]
        x, y, c, chips = _place()
        me = 2 * x + y
        sibling = (x, y, 1 - c)
        others = [2 * cx + cy for cx, cy in chips]
        for t in range(n):
            chan = scr[CHANNEL_REFS * t:CHANNEL_REFS * (t + 1)]
            _copy_blocks([ins[t].at[h] for h in range(2)], [outs[t].at[me, h] for h in range(2)], chan)
            _exchange_blocks([zones_in[t].at[k, c] for k in others], [outs[t].at[k, 1 - c] for k in others],
                             [None] * len(others), chan, sibling)

    scratch = []
    for s in shards:
        scratch += _channel_scratch(s.shape[2], s.dtype, rows=s.shape[1])
    return pl.pallas_call(
        body, in_specs=[ANY] * (2 * n), out_specs=[ANY] * n, out_shape=[SDS(z.shape, z.dtype) for z in zones],
        input_output_aliases={n + t: t for t in range(n)}, scratch_shapes=scratch,
        compiler_params=pltpu.CompilerParams(vmem_limit_bytes=VMEM_LIMIT), name=name)(*shards, *zones)


def _pair_share(groups, *, name):
    finals = [f for grp in groups for f in grp]
    n, n_out = len(finals), len(groups)

    def body(*refs):
        ins, outs, scr = refs[:n], refs[n:n + n_out], refs[n + n_out:]
        x, y, c, _ = _place()
        sibling = (x, y, 1 - c)
        t = 0
        for o, grp in enumerate(groups):
            rows = grp[0].shape[0] // 2
            blocks = [(layer, pl.ds(b * rows, rows)) for layer in range(len(grp)) for b in range(2)]
            _exchange_blocks([ins[t + layer].at[rs] for layer, rs in blocks],
                             [outs[o].at[layer, 1 - c, rs] for layer, rs in blocks],
                             [outs[o].at[layer, c, rs] for layer, rs in blocks],
                             scr[CHANNEL_REFS * o:CHANNEL_REFS * (o + 1)], sibling)
            t += len(grp)

    scratch = []
    for grp in groups:
        scratch += _channel_scratch(grp[0].shape[1], grp[0].dtype, rows=grp[0].shape[0] // 2)
    return pl.pallas_call(
        body, in_specs=[ANY] * n, out_specs=[ANY] * n_out,
        out_shape=[SDS((len(grp), 2, *grp[0].shape), grp[0].dtype) for grp in groups],
        scratch_shapes=scratch, compiler_params=pltpu.CompilerParams(vmem_limit_bytes=VMEM_LIMIT), name=name)(*finals)


def _all_reduce_small(v, *, name):
    rows, lanes = v.shape
    n_dev = 8

    def body(v_ref, o_ref, all_ref, send_sems, recv_sems, local_sem):
        x, y, c, chips = _place()
        me, sibling = (x, y, c), (x, y, 1 - c)

        def block(px, py, pc):
            return all_ref.at[4 * px + 2 * py + pc]

        def copy(k, blk, to, src=None):
            return _remote(block(*blk) if src is None else src, block(*blk), send_sems.at[k], recv_sems.at[k], to)

        mine = pltpu.make_async_copy(v_ref, block(*me), local_sem)
        mine.start()
        first = [copy(0, me, sibling, src=v_ref)]
        first += [copy(1 + j, me, (*chip, c), src=v_ref) for j, chip in enumerate(chips)]
        for cp in first:
            cp.start()
        passed = [copy(4 + j, (*chip, c), sibling) for j, chip in enumerate(chips)]
        for j, chip in enumerate(chips):
            copy(1 + j, (*chip, c), me).wait_recv()
            passed[j].start()
        copy(0, sibling, me).wait_recv()
        for j, chip in enumerate(chips):
            copy(4 + j, (*chip, 1 - c), me).wait_recv()
        for cp in first + passed:
            cp.wait_send()
        mine.wait()
        acc = all_ref[0]
        for k in range(1, n_dev):
            acc = acc + all_ref[k]
        o_ref[...] = acc

    vmem = pl.BlockSpec(memory_space=pltpu.VMEM)
    return pl.pallas_call(
        body, in_specs=[vmem], out_specs=vmem, out_shape=SDS((rows, lanes), F32),
        scratch_shapes=[pltpu.VMEM((n_dev, rows, lanes), F32), pltpu.SemaphoreType.DMA((7,)),
                        pltpu.SemaphoreType.DMA((7,)), pltpu.SemaphoreType.DMA],
        compiler_params=pltpu.CompilerParams(vmem_limit_bytes=VMEM_LIMIT), name=name)(v)


def _relu2_epilogue(acc):
    return acc, jnp.square(jnp.maximum(acc, 0.0))


def _res_epilogue(acc, res):
    return (acc + res,)


def _drelu2_epilogue(acc, pre):
    return (acc * (2.0 * jnp.maximum(pre.astype(F32), 0.0)),)


def _ffn_fwd(h, g, w1, w2, tag):
    f = _rms_fwd(h, g, name=f"ffn_norm_{tag}")
    pre, act = _mm_nn(f, w1, name=f"ffn1_{tag}", epilogue=_relu2_epilogue, n_out_dtypes=(BF16, BF16))
    h_out = _mm_nn(act, w2, name=f"ffn2_{tag}", extras=(h,), epilogue=_res_epilogue)
    return h_out, (f, pre, act)


def _ffn_bwd(dh, h, g, w1, w2, saved, layer, after=()):
    f, pre, act = saved
    dpre = _mm_nt(dh, w2, name=f"ffn2_dx_{layer}", out_dtype=BF16, extras=(pre,), epilogue=_drelu2_epilogue,
                  after=after)
    dw2 = _mm_tn_stacked(act, dh, name=f"ffn2_dw_{layer}", col_slots=False)
    df = _mm_nt(dpre, w1, name=f"ffn1_dx_{layer}")
    dw1 = _mm_tn_stacked(f, dpre, name=f"ffn1_dw_{layer}", col_slots=True)
    dh, dg = _rms_bwd(h, g, df, dh, name=f"ffn_norm_bwd_{layer}")
    return dh, dg, dw1, dw2


def _kv_fwd(mem, g, w_kv, tag):
    m = _rms_fwd(mem, g, name=f"mem_norm_{tag}")
    return m, _mm_nn(m, w_kv, name=f"kv_{tag}")


def _kv_bwd(mem, g, w_kv, m, dk, dv, layer):
    dkv = jnp.concatenate([dk, dv], axis=1)
    dw = _mm_tn_stacked(m, dkv, name=f"kv_dw_{layer}", col_slots=True)
    dm = _mm_nt(dkv, w_kv, name=f"kv_dx_{layer}")
    _, dg = _rms_bwd(mem, g, dm, dm, name=f"mem_norm_bwd_{layer}")
    return dw, dg


def _local_step(x, mem, target, p, after_layer1=None, after_ffn0=None, after_mixer0=None):
    row = lambda v: v.reshape(1, -1)
    g = {}

    h0 = x
    a0 = _rms_fwd(h0, row(p["norm_mix"][0]), name="mix_norm_0")
    proj_a = _mm_nn(a0, p["a_in"], name="a_in", after=p.get("after_start", ()))
    m0, kv0 = _kv_fwd(mem, row(p["mem_norm"][0]), p["w_kv"][0], "0")
    cat0 = _attn_fwd(proj_a, 2 * D_INNER, kv0, name="attn_0")
    bs_col = p["a_bs"].reshape(A_GROUPS, CHUNK, 1)
    cat0 = _gate_fwd(proj_a, p["a_ln_g"], p["a_ln_b"], p["a_ws"], bs_col, cat0, name="gate")
    h1 = _mm_nn(cat0, p["w_out"][0], name="out_0", extras=(h0,), epilogue=_res_epilogue)
    h2, ffn0 = _ffn_fwd(h1, row(p["norm_ffn"][0]), p["w_ffn1"][0], p["w_ffn2"][0], "0")

    if "layer1_mixer" in p:
        w_kv1, w_out1, b_in = p["layer1_mixer"](h2)
    else:
        w_kv1, w_out1, b_in = p["w_kv"][1], p["w_out"][1], p["b_in"]
    a1 = _rms_fwd(h2, row(p["norm_mix"][1]), name="mix_norm_1")
    proj_b = _mm_nn(a1, b_in, name="b_in")
    m1, kv1 = _kv_fwd(mem, row(p["mem_norm"][1]), w_kv1, "1")
    cat1 = _attn_fwd(proj_b, B_Q_OFF, kv1, name="attn_1")
    xbc = _conv_fwd(proj_b, p["b_conv_w"], p["b_conv_b"], name="conv")
    dt_raw = proj_b[:, B_DT_OFF:B_DT_OFF + SSM_HEADS].reshape(SEQ, SSM_GROUPS, SSM_HPG)
    dt_c = jnp.transpose(dt_raw, (1, 0, 2))
    dt_r = jnp.transpose(dt_raw, (1, 2, 0))
    per_head = lambda v: v.reshape(SSM_GROUPS, 1, SSM_HPG)
    par_row = jnp.concatenate([per_head(p["b_dt_bias"]), per_head(p["b_a_log"]), per_head(p["b_d"])], axis=1)
    ssd_par = (par_row, jnp.transpose(par_row[:, :2], (0, 2, 1)), p["b_gnorm"])
    cat1, hprev = _ssd_fwd(xbc, proj_b, dt_c, dt_r, *ssd_par, cat1, name="ssd")
    h3 = _mm_nn(cat1, w_out1, name="out_1", extras=(h2,), epilogue=_res_epilogue)
    w_ffn1_1, w_ffn2_1 = p["layer1_ffn"](h3) if "layer1_ffn" in p else (p["w_ffn1"][1], p["w_ffn2"][1])
    h4, ffn1 = _ffn_fwd(h3, row(p["norm_ffn"][1]), w_ffn1_1, w_ffn2_1, "1")

    loss, dh, g["final_norm"] = _loss_head(h4, row(p["final_norm"]), target, name="loss_head")

    dh, dnf1, dw1_1, dw2_1 = _ffn_bwd(dh, h3, row(p["norm_ffn"][1]), w_ffn1_1, w_ffn2_1, ffn1, 1)
    dcat1 = _mm_nt(dh, w_out1, name="out_dx_1")
    dwo_1 = _mm_tn_stacked(cat1, dh, name="out_dw_1", col_slots=False)
    dproj_b, dk1, dv1 = _attn_bwd(proj_b, B_Q_OFF, kv1, dcat1, B_IN_PAD, B_Q_OFF, name="attn_bwd_1")
    dproj_b, dxs, dbm, dcm, ddt_c, ddt_r, dpar_row, dpar_col, g["b_gnorm"] = _ssd_bwd(
        xbc, proj_b, dt_c, dt_r, *ssd_par, hprev, dcat1, dproj_b, name="ssd_bwd")
    dpar = dpar_row.at[:, :2].add(jnp.transpose(dpar_col, (0, 2, 1)))
    g["b_dt_bias"], g["b_a_log"], g["b_d"] = dpar[:, 0], dpar[:, 1], dpar[:, 2]
    dproj_b, g["b_conv_w"], g["b_conv_b"] = _conv_bwd(proj_b, p["b_conv_w"], p["b_conv_b"], dxs, dbm, dcm, dproj_b,
                                                      name="conv_bwd")
    ddt = jnp.transpose(ddt_c, (1, 0, 2)) + jnp.transpose(ddt_r, (2, 0, 1))
    ddt = jnp.pad(ddt.reshape(SEQ, SSM_HEADS), ((0, 0), (0, B_IN_PAD - B_DT_OFF - SSM_HEADS))).astype(BF16)
    dproj_b = lax.dynamic_update_slice(dproj_b, ddt, (0, B_DT_OFF))
    dwkv_1, dmn1 = _kv_bwd(mem, row(p["mem_norm"][1]), w_kv1, m1, dk1, dv1, 1)
    dwb = _b_in_grad_slots(_mm_tn(a1, dproj_b, name="b_in_dw"))
    da1 = _mm_nt(dproj_b, b_in, name="b_in_dx")
    dh, dnm1 = _rms_bwd(h2, row(p["norm_mix"][1]), da1, dh, name="mix_norm_bwd_1")
    layer1 = dict(w_kv=dwkv_1, w_out=dwo_1, w_ffn1=dw1_1, w_ffn2=dw2_1, b_in=dwb)
    token = () if after_layer1 is None else (after_layer1(layer1),)

    dh, dnf0, dw1_0, dw2_0 = _ffn_bwd(dh, h1, row(p["norm_ffn"][0]), p["w_ffn1"][0], p["w_ffn2"][0], ffn0, 0,
                                      after=token)
    ffn0_grads = dict(w_ffn1=dw1_0, w_ffn2=dw2_0)
    token = () if after_ffn0 is None else (after_ffn0(ffn0_grads),)
    dcat0 = _mm_nt(dh, p["w_out"][0], name="out_dx_0", after=token)
    dwo_0 = _mm_tn_stacked(cat0, dh, name="out_dw_0", col_slots=False)
    dproj_a, dk0, dv0 = _attn_bwd(proj_a, 2 * D_INNER, kv0, dcat0, A_IN, 2 * D_INNER, name="attn_bwd_0")
    dproj_a, g["a_ln_g"], g["a_ln_b"], g["a_ws"], dbs_col = _gate_bwd(
        proj_a, p["a_ln_g"], p["a_ln_b"], p["a_ws"], bs_col, dcat0, dproj_a, name="gate_bwd")
    g["a_bs"] = dbs_col.reshape(A_GROUPS, CHUNK)
    dwkv_0, dmn0 = _kv_bwd(mem, row(p["mem_norm"][0]), p["w_kv"][0], m0, dk0, dv0, 0)
    dwa = _mm_tn_stacked(a0, dproj_a, name="a_in_dw", col_slots=True)
    mixer0_grads = dict(w_kv=dwkv_0, w_out=dwo_0, a_in=dwa)
    token = () if after_mixer0 is None else (after_mixer0(mixer0_grads),)
    da0 = _mm_nt(dproj_a, p["a_in"], name="a_in_dx", after=token)
    dx, dnm0 = _rms_bwd(h0, row(p["norm_mix"][0]), da0, dh, name="mix_norm_bwd_0")

    g["norm_mix"] = jnp.concatenate([dnm0, dnm1], axis=0)
    g["norm_ffn"] = jnp.concatenate([dnf0, dnf1], axis=0)
    g["mem_norm"] = jnp.concatenate([dmn0, dmn1], axis=0)
    layer0 = dict(w_kv=dwkv_0, w_out=dwo_0, w_ffn1=dw1_0, w_ffn2=dw2_0, a_in=dwa)
    return loss, dx, g, layer0, layer1


def _b_in_full(gathered):
    n = B_IN // N_CHIPS
    dt0 = D_INNER + CONV_DIM - (N_CHIPS - 1) * n
    last = gathered[N_CHIPS - 1]
    return jnp.concatenate([*[gathered[k] for k in range(N_CHIPS - 1)], last[:, :dt0], last[:, dt0 + SSM_HEADS:],
                            last[:, dt0:dt0 + SSM_HEADS], jnp.zeros((D_MODEL, B_IN_PAD - B_IN), last.dtype)], axis=1)


def _b_in_grad_slots(d):
    n = B_IN // N_CHIPS
    dt0 = D_INNER + CONV_DIM
    last = jnp.concatenate([d[:, (N_CHIPS - 1) * n:dt0], d[:, B_DT_OFF:B_DT_OFF + SSM_HEADS], d[:, dt0:B_DT_OFF]], axis=1)
    slots = [*[d[:, k * n:(k + 1) * n] for k in range(N_CHIPS - 1)], last]
    half = D_MODEL // 2
    return jnp.stack([jnp.stack([s[h * half:(h + 1) * half] for s in slots]) for h in range(2)])


LARGE = ("w_kv", "w_out", "w_ffn1", "w_ffn2", "a_in", "b_in")
SMALL_REPL = ("norm_mix", "norm_ffn", "mem_norm", "a_ln_g", "a_ln_b", "a_ws", "a_bs", "b_dt_bias", "b_a_log", "b_d",
              "final_norm")
SMALL_SHARD = ("b_conv_w", "b_conv_b", "b_gnorm")
WEIGHTS = ("norm_mix", "norm_ffn", "mem_norm", "w_kv", "w_out", "w_ffn1", "w_ffn2", "a_in", "a_ln_g", "a_ln_b", "a_ws",
           "a_bs", "b_in", "b_conv_w", "b_conv_b", "b_dt_bias", "b_a_log", "b_d", "b_gnorm", "final_norm")
CONV_SHARD = CONV_DIM // N_CHIPS
GN_SHARD = D_INNER // N_CHIPS


LAYERED = ("w_kv", "w_out", "w_ffn1", "w_ffn2")
LAYER_TENSORS = (("w_kv", "w_out", "w_ffn1", "w_ffn2", "a_in"), ("w_kv", "w_out", "w_ffn1", "w_ffn2", "b_in"))


def _gather_weights(w):
    halves = lambda k, layer: (w[k][layer] if k in LAYERED else w[k][0]).reshape(2, -1, w[k].shape[-1]).astype(BF16)
    small = jnp.zeros((2, CONV_K, CONV_SHARD), F32)
    small = small.at[0].set(w["b_conv_w"][0])
    small = small.at[1, 0].set(w["b_conv_b"][0])
    small = small.at[1, 1, :GN_SHARD].set(w["b_gnorm"][0])
    gathered = _all_gather_shards([halves(k, 0) for k in LAYER_TENSORS[0]], small, name="gather_weights_0")
    got = dict(zip(LAYER_TENSORS[0], gathered))
    slots = lambda a: a.reshape(N_CHIPS, -1, a.shape[-1])
    rows = lambda a: a.reshape(-1, a.shape[-1])
    p = dict(w_kv=[slots(got["w_kv"])], w_out=[rows(got["w_out"])], w_ffn1=[slots(got["w_ffn1"])],
             w_ffn2=[rows(got["w_ffn2"])], a_in=slots(got["a_in"]))
    sm = gathered[-1]
    p["b_conv_w"] = jnp.transpose(sm[:, 0], (1, 0, 2)).reshape(CONV_K, CONV_DIM)
    p["b_conv_b"] = sm[:, 1, 0].reshape(1, CONV_DIM)
    p["b_gnorm"] = sm[:, 1, 1, :GN_SHARD].reshape(1, D_INNER)

    after, started = (gathered[0],), {}
    for tag, names in (("mixer", ("w_kv", "w_out", "b_in")), ("ffn", ("w_ffn1", "w_ffn2"))):
        started[tag] = _gather_start([halves(k, 1) for k in names], after, name=f"gather_start_1_{tag}")
        after = (started[tag][-1],)
    p["after_start"] = after

    def finish(tag, first):
        send_sems, recv_sems, shards, zones, _ = started[tag]
        shards, zones = _gather_wait(send_sems, recv_sems, shards, zones, (first,), name=f"gather_wait_1_{tag}")
        return _gather_finish(shards, zones, name=f"gather_finish_1_{tag}")

    def layer1_mixer(first):
        kv, wo, b_in = finish("mixer", first)
        return slots(kv), rows(wo), _b_in_full(slots(b_in))

    def layer1_ffn(first):
        w1, w2 = finish("ffn", first)
        return slots(w1), rows(w2)

    p.update(layer1_mixer=layer1_mixer, layer1_ffn=layer1_ffn)
    return p


def _pair_parts(grads, tag):
    stacks = [g.reshape(2, -1, g.shape[-1]) for g in grads.values()]
    parts = _pair_reduce(stacks, name=f"grads_pair_reduce_{tag}")
    return [t.reshape(N_CHIPS, -1, t.shape[-1]) for t in parts]


def _chip_sums(chip, names, parts, landed, tag):
    return {k: _sum_contributions(chip, t, u, name=f"grads_chip_sum_{k}_{tag}")
            for k, t, u in zip(names, parts, landed)}


def _small_layout(shapes):
    offs, o = {}, 0
    for k in (*SMALL_REPL, *SMALL_SHARD):
        size = math.prod(shapes[k])
        offs[k] = (o, size)
        o += size
    rows = -(-o // (8 * 128)) * 8
    return offs, rows


def _reduce_small(g, full_shapes):
    offs, rows = _small_layout(full_shapes)
    flat = jnp.concatenate([g[k].reshape(-1) for k in (*SMALL_REPL, *SMALL_SHARD)])
    flat = jnp.pad(flat, (0, rows * 128 - flat.shape[0])).reshape(rows, 128)
    total = _all_reduce_small(flat, name="grads_small_all_reduce").reshape(-1)
    return {k: total[o:o + n].reshape(full_shapes[k]) for k, (o, n) in offs.items()}


def kernel(x, mem, norm_mix, norm_ffn, mem_norm, w_kv, w_out, w_ffn1, w_ffn2, a_in, a_ln_g, a_ln_b, a_ws, a_bs, b_in, b_conv_w, b_conv_b, b_dt_bias, b_a_log, b_d, b_gnorm, final_norm, loss_target, m_norm_mix, m_norm_ffn, m_mem_norm, m_w_kv, m_w_out, m_w_ffn1, m_w_ffn2, m_a_in, m_a_ln_g, m_a_ln_b, m_a_ws, m_a_bs, m_b_in, m_b_conv_w, m_b_conv_b, m_b_dt_bias, m_b_a_log, m_b_d, m_b_gnorm, m_final_norm, v_norm_mix, v_norm_ffn, v_mem_norm, v_w_kv, v_w_out, v_w_ffn1, v_w_ffn2, v_a_in, v_a_ln_g, v_a_ln_b, v_a_ws, v_a_bs, v_b_in, v_b_conv_w, v_b_conv_b, v_b_dt_bias, v_b_a_log, v_b_d, v_b_gnorm, v_final_norm):
    w = dict(norm_mix=norm_mix, norm_ffn=norm_ffn, mem_norm=mem_norm, w_kv=w_kv, w_out=w_out, w_ffn1=w_ffn1,
             w_ffn2=w_ffn2, a_in=a_in, a_ln_g=a_ln_g, a_ln_b=a_ln_b, a_ws=a_ws, a_bs=a_bs, b_in=b_in, b_conv_w=b_conv_w,
             b_conv_b=b_conv_b, b_dt_bias=b_dt_bias, b_a_log=b_a_log, b_d=b_d, b_gnorm=b_gnorm, final_norm=final_norm)
    mom = dict(norm_mix=m_norm_mix, norm_ffn=m_norm_ffn, mem_norm=m_mem_norm, w_kv=m_w_kv, w_out=m_w_out,
               w_ffn1=m_w_ffn1, w_ffn2=m_w_ffn2, a_in=m_a_in, a_ln_g=m_a_ln_g, a_ln_b=m_a_ln_b, a_ws=m_a_ws,
               a_bs=m_a_bs, b_in=m_b_in, b_conv_w=m_b_conv_w, b_conv_b=m_b_conv_b, b_dt_bias=m_b_dt_bias,
               b_a_log=m_b_a_log, b_d=m_b_d, b_gnorm=m_b_gnorm, final_norm=m_final_norm)
    var = dict(norm_mix=v_norm_mix, norm_ffn=v_norm_ffn, mem_norm=v_mem_norm, w_kv=v_w_kv, w_out=v_w_out,
               w_ffn1=v_w_ffn1, w_ffn2=v_w_ffn2, a_in=v_a_in, a_ln_g=v_a_ln_g, a_ln_b=v_a_ln_b, a_ws=v_a_ws,
               a_bs=v_a_bs, b_in=v_b_in, b_conv_w=v_b_conv_w, b_conv_b=v_b_conv_b, b_dt_bias=v_b_dt_bias,
               b_a_log=v_b_a_log, b_d=v_b_d, b_gnorm=v_b_gnorm, final_norm=v_final_norm)

    p = _gather_weights(w)
    p.update(norm_mix=norm_mix, norm_ffn=norm_ffn, mem_norm=mem_norm, a_ln_g=a_ln_g, a_ln_b=a_ln_b, a_ws=a_ws[0],
             a_bs=a_bs[0], b_dt_bias=b_dt_bias, b_a_log=b_a_log, b_d=b_d, final_norm=final_norm)
    chip = 2 * lax.axis_index("x") + lax.axis_index("y")
    chip_arr = jnp.reshape(chip, (1,)).astype(jnp.int32)
    started = {}

    def start_scatter(tag):
        def hook(grads):
            start = _chip_scatter_start(_pair_parts(grads, tag), name=f"grads_chip_scatter_start_{tag}")
            started[tag] = (tuple(grads), start)
            return start[-1]
        return hook

    loss_part, dx, g, _, _ = _local_step(x[0], mem[0], loss_target[0], p, start_scatter("1"), start_scatter("0f"),
                                         start_scatter("0m"))
    loss = lax.psum(loss_part[0, 0], ("x", "y", "c"))

    def finish_scatter(tag, first):
        names, (send_sems, recv_sems, parts, lands, _) = started[tag]
        parts, landed = _chip_scatter_wait(send_sems, recv_sems, parts, lands, (first,),
                                           name=f"grads_chip_scatter_wait_{tag}")
        return _chip_sums(chip_arr, names, parts, landed, tag)

    def adamw(names, grads):
        for k in names:
            shape = w[k].shape
            flat = (lambda a: a.reshape(-1, shape[-1])) if len(shape) > 1 else (lambda a: a.reshape(1, -1))
            d, m_new, v_new = _adamw(flat(w[k]), flat(grads[k]), flat(mom[k]), flat(var[k]), name=f"adamw_{k}")
            delta[k], new_m[k], new_v[k] = d.reshape(shape), m_new.reshape(shape), v_new.reshape(shape)

    full_shapes = {k: w[k].shape for k in SMALL_REPL}
    full_shapes.update(b_conv_w=(1, CONV_K, CONV_DIM), b_conv_b=(1, CONV_DIM), b_gnorm=(1, D_INNER))
    grads = _reduce_small(g, full_shapes)
    grads["b_conv_w"] = lax.dynamic_slice_in_dim(grads["b_conv_w"], chip * CONV_SHARD, CONV_SHARD, axis=2)
    grads["b_conv_b"] = lax.dynamic_slice_in_dim(grads["b_conv_b"], chip * CONV_SHARD, CONV_SHARD, axis=1)
    grads["b_gnorm"] = lax.dynamic_slice_in_dim(grads["b_gnorm"], chip * GN_SHARD, GN_SHARD, axis=1)
    delta, new_m, new_v = {}, {}, {}
    halves = [finish_scatter("0f", dx), finish_scatter("1", dx)]
    early = ("w_ffn1", "w_ffn2", "b_in")
    shared = _pair_share([[halves[layer][k] for layer in range(2) if k in halves[layer]] for k in early],
                         name="grads_pair_share_early")
    grads.update({k: a.reshape(w[k].shape) for k, a in zip(early, shared)})
    adamw([k for k in WEIGHTS if k in grads], grads)
    halves[0].update(finish_scatter("0m", delta["b_in"]))
    late = ("w_kv", "w_out", "a_in")
    shared = _pair_share([[halves[layer][k] for layer in range(2) if k in halves[layer]] for k in late],
                         name="grads_pair_share_late")
    grads.update({k: a.reshape(w[k].shape) for k, a in zip(late, shared)})
    adamw(late, grads)

    return (loss, dx.reshape(x.shape), *[grads[k] for k in WEIGHTS], *[delta[k] for k in WEIGHTS],
            *[new_m[k] for k in WEIGHTS], *[new_v[k] for k in WEIGHTS])
```

```python
import math

import jax
import jax.numpy as jnp
from jax import lax
from jax.experimental import pallas as pl
from jax.experimental.pallas import tpu as pltpu

F32 = jnp.float32
BF16 = jnp.bfloat16
SDS = jax.ShapeDtypeStruct

D_MODEL = 1024
SEQ = 2048
CHUNK = 128
N_MEM = 256
D_INNER = 2048
A_GROUPS = 8
A_GROUP_W = D_INNER // A_GROUPS
SSM_HEADS = 32
SSM_HEAD_DIM = 64
SSM_GROUPS = 4
SSM_HPG = 8
SSM_STATE = 128
SSM_GROUP_W = SSM_HPG * SSM_HEAD_DIM
CONV_K = 4
CONV_DIM = 3072
X_HEADS = 4
X_HEAD_DIM = 256
X_WIDTH = 1024
MIX_OUT = 3072
D_FF = 4096
A_IN = 5120
B_IN = 6176
B_IN_PAD = 6272
B_Q_OFF = 5120
B_DT_OFF = 6144
N_CHUNKS = SEQ // CHUNK
EPS = 1e-6
N_CHIPS = 4

ADAM_LR = 0.001
ADAM_B1 = 0.9
ADAM_B2 = 0.999
ADAM_EPS = 1e-08
ADAM_WD = 0.01
ADAM_STEP = 10

VMEM_LIMIT = 48 * 1024 * 1024
MESH = pl.DeviceIdType.MESH


def _cparams(sem):
    return pltpu.CompilerParams(dimension_semantics=sem, vmem_limit_bytes=VMEM_LIMIT)


def _dot(a, b, dims=(((1,), (0,)), ((), ()))):
    return lax.dot_general(a.astype(BF16), b.astype(BF16), dims, preferred_element_type=F32)


def _dot_nt(a, b):
    return _dot(a, b, (((1,), (1,)), ((), ())))


def _dot_tn(a, b):
    return _dot(a, b, (((0,), (0,)), ((), ())))


def _pick(n, cands):
    for c in cands:
        if n % c == 0:
            return c
    raise ValueError(f"no tile for {n}")


def _mm_call(a, b, *, dims, grid, a_spec, b_spec, acc_shape, out_shapes, out_specs, name,
             extras=(), extra_specs=(), epilogue=None, after=()):
    n_k = grid[2]
    n_extra = len(extras)
    n_out = len(out_shapes)
    n_in = 2 + n_extra + len(after)

    def finish(total, extra_refs, out_refs):
        vals = (total,) if epilogue is None else epilogue(total, *[e[...] for e in extra_refs])
        for o_ref, v in zip(out_refs, vals):
            o_ref[...] = v.astype(o_ref.dtype)

    def body_one_step(*refs):
        finish(_dot(refs[0][...], refs[1][...], dims), refs[2:2 + n_extra], refs[n_in:n_in + n_out])

    def body(*refs):
        acc = refs[-1]
        k = pl.program_id(2)

        @pl.when(k == 0)
        def _():
            acc[...] = jnp.zeros_like(acc)

        acc[...] += _dot(refs[0][...], refs[1][...], dims)

        @pl.when(k == n_k - 1)
        def _():
            finish(acc[...], refs[2:2 + n_extra], refs[n_in:n_in + n_out])

    return pl.pallas_call(
        body_one_step if n_k == 1 else body, grid=grid,
        in_specs=[a_spec, b_spec, *extra_specs, *([ANY] * len(after))], out_specs=list(out_specs),
        out_shape=list(out_shapes), scratch_shapes=[] if n_k == 1 else [pltpu.VMEM(acc_shape, F32)],
        compiler_params=_cparams(("parallel", "parallel", "arbitrary")), name=name,
    )(a, b, *extras, *after)


def _w_dims(w):
    if w.ndim == 2:
        return w.shape[0], w.shape[1], 1, w.shape[1]
    return w.shape[1], w.shape[0] * w.shape[2], w.shape[0], w.shape[2]


def _mm_nn(a, w, *, name, out_dtype=F32, a_cols=None, extras=(), epilogue=None, n_out_dtypes=None, after=()):
    m = a.shape[0]
    k_dim, n_dim, _, n_slot = _w_dims(w)
    a_off, a_w = (0, a.shape[1]) if a_cols is None else a_cols
    assert a_w == k_dim
    tm = _pick(m, (2048, 1024, 512, 256))
    tn = _pick(n_slot, (512, 896, 640, 256, 128))
    tk = _pick(k_dim, (1024, 768, 512, 384, 256, 128))
    assert a_off % tk == 0
    nb = n_slot // tn
    a_spec = pl.BlockSpec((tm, tk), lambda i, j, k: (i, a_off // tk + k))
    if w.ndim == 2:
        b_spec = pl.BlockSpec((tk, tn), lambda i, j, k: (k, j))
    else:
        b_spec = pl.BlockSpec((None, tk, tn), lambda i, j, k: (j // nb, k, j % nb))
    o_spec = pl.BlockSpec((tm, tn), lambda i, j, k: (i, j))
    dts = n_out_dtypes or (out_dtype,)
    outs = _mm_call(a, w, dims=(((1,), (0,)), ((), ())), grid=(m // tm, n_dim // tn, k_dim // tk),
                    a_spec=a_spec, b_spec=b_spec, acc_shape=(tm, tn),
                    out_shapes=[SDS((m, n_dim), dt) for dt in dts], out_specs=[o_spec] * len(dts), name=name,
                    extras=extras, extra_specs=[o_spec] * len(extras), epilogue=epilogue, after=after)
    return outs if n_out_dtypes else outs[0]


def _mm_nt(a, w, *, name, out_dtype=F32, extras=(), epilogue=None, after=()):
    m = a.shape[0]
    k_dim, n_dim, _, n_slot = _w_dims(w)
    assert a.shape[1] == n_dim
    tm = _pick(m, (2048, 1024, 512, 256))
    to = _pick(k_dim, (512, 384, 256, 128))
    tc = _pick(n_slot, (1280, 1024, 896, 640, 512, 256, 128))
    nb = n_slot // tc
    a_spec = pl.BlockSpec((tm, tc), lambda i, j, k: (i, k))
    if w.ndim == 2:
        b_spec = pl.BlockSpec((to, tc), lambda i, j, k: (j, k))
    else:
        b_spec = pl.BlockSpec((None, to, tc), lambda i, j, k: (k // nb, j, k % nb))
    o_spec = pl.BlockSpec((tm, to), lambda i, j, k: (i, j))
    return _mm_call(a, w, dims=(((1,), (1,)), ((), ())), grid=(m // tm, k_dim // to, n_dim // tc),
                    a_spec=a_spec, b_spec=b_spec, acc_shape=(tm, to),
                    out_shapes=[SDS((m, k_dim), out_dtype)], out_specs=[o_spec], name=name,
                    extras=extras, extra_specs=[o_spec] * len(extras), epilogue=epilogue, after=after)[0]


def _mm_tn(x, dy, *, name, x_cols=None):
    s = x.shape[0]
    x_off, k_dim = (0, x.shape[1]) if x_cols is None else x_cols
    n_dim = dy.shape[1]
    tm = _pick(k_dim, (1024, 768, 512, 384, 256, 128))
    tn = _pick(n_dim, (512, 896, 640, 256, 128))
    tk = _pick(s, (2048, 1024, 512, 256))
    assert x_off % tm == 0
    a_spec = pl.BlockSpec((tk, tm), lambda i, j, k: (k, x_off // tm + i))
    b_spec = pl.BlockSpec((tk, tn), lambda i, j, k: (k, j))
    o_spec = pl.BlockSpec((tm, tn), lambda i, j, k: (i, j))
    return _mm_call(x, dy, dims=(((0,), (0,)), ((), ())), grid=(k_dim // tm, n_dim // tn, s // tk),
                    a_spec=a_spec, b_spec=b_spec, acc_shape=(tm, tn),
                    out_shapes=[SDS((k_dim, n_dim), F32)], out_specs=[o_spec], name=name)[0]


def _mm_tn_stacked(x, dy, *, name, col_slots):
    s, k_dim = x.shape
    n_dim = dy.shape[1]
    r, c = (k_dim // 2, n_dim // N_CHIPS) if col_slots else (k_dim // N_CHIPS // 2, n_dim)
    tm = 2 * r
    tn = _pick(c, (512, 896, 640, 256, 128))
    tk = _pick(s, (2048, 1024, 512, 256))
    a_spec = pl.BlockSpec((tk, tm), lambda i, j, k: (k, i))
    b_spec = pl.BlockSpec((tk, tn), lambda i, j, k: (k, j))
    if col_slots:
        nb = c // tn
        o_spec = pl.BlockSpec((2, None, r, tn), lambda i, j, k: (0, j // nb, 0, j % nb))
    else:
        o_spec = pl.BlockSpec((2, None, r, tn), lambda i, j, k: (0, i, 0, j))
    return _mm_call(x, dy, dims=(((0,), (0,)), ((), ())), grid=(k_dim // tm, n_dim // tn, s // tk),
                    a_spec=a_spec, b_spec=b_spec, acc_shape=(tm, tn), epilogue=lambda acc: (acc.reshape(2, r, tn),),
                    out_shapes=[SDS((2, N_CHIPS, r, c), F32)], out_specs=[o_spec], name=name)[0]


def _rms(x, g):
    return x * lax.rsqrt(jnp.mean(x * x, axis=-1, keepdims=True) + EPS) * g


def _rms_fwd(h, g, *, name):
    rows, d = h.shape
    tr = _pick(rows, (512, 256))

    def body(h_ref, g_ref, o_ref):
        o_ref[...] = _rms(h_ref[...], g_ref[...]).astype(o_ref.dtype)

    return pl.pallas_call(
        body, grid=(rows // tr,),
        in_specs=[pl.BlockSpec((tr, d), lambda i: (i, 0)), pl.BlockSpec((1, d), lambda i: (0, 0))],
        out_specs=pl.BlockSpec((tr, d), lambda i: (i, 0)), out_shape=SDS((rows, d), BF16),
        compiler_params=_cparams(("parallel",)), name=name)(h, g)


def _rms_bwd(h, g, da, dres, *, name):
    rows, d = h.shape
    tr = _pick(rows, (512, 256))

    def body(h_ref, g_ref, da_ref, dres_ref, dh_ref, dg_ref):
        _, vjp = jax.vjp(_rms, h_ref[...], g_ref[...])
        dh, dg = vjp(da_ref[...].astype(F32))
        dh_ref[...] = dres_ref[...] + dh

        @pl.when(pl.program_id(0) == 0)
        def _():
            dg_ref[...] = jnp.zeros_like(dg_ref)

        dg_ref[...] += dg

    row_spec = pl.BlockSpec((tr, d), lambda i: (i, 0))
    vec_spec = pl.BlockSpec((1, d), lambda i: (0, 0))
    return pl.pallas_call(
        body, grid=(rows // tr,), in_specs=[row_spec, vec_spec, row_spec, row_spec],
        out_specs=[row_spec, vec_spec], out_shape=[SDS((rows, d), F32), SDS((1, d), F32)],
        compiler_params=_cparams(("arbitrary",)), name=name)(h, g, da, dres)


def _loss_head(h, g, target, *, name):
    rows, d = h.shape
    tr = _pick(rows, (512, 256))

    def body(h_ref, g_ref, t_ref, loss_ref, dh_ref, dg_ref):
        y, vjp = jax.vjp(_rms, h_ref[...], g_ref[...])
        err = y - t_ref[...]
        dh, dg = vjp(err * (1.0 / d))
        dh_ref[...] = dh

        @pl.when(pl.program_id(0) == 0)
        def _():
            dg_ref[...] = jnp.zeros_like(dg_ref)
            loss_ref[...] = jnp.zeros_like(loss_ref)

        dg_ref[...] += dg
        part = jnp.sum(jnp.sum(err * err, axis=-1, keepdims=True), axis=0, keepdims=True) * (0.5 / d)
        loss_ref[...] += jnp.broadcast_to(part, loss_ref.shape)

    row_spec = pl.BlockSpec((tr, d), lambda i: (i, 0))
    vec_spec = pl.BlockSpec((1, d), lambda i: (0, 0))
    loss_spec = pl.BlockSpec((8, 128), lambda i: (0, 0))
    return pl.pallas_call(
        body, grid=(rows // tr,), in_specs=[row_spec, vec_spec, row_spec],
        out_specs=[loss_spec, row_spec, vec_spec],
        out_shape=[SDS((8, 128), F32), SDS((rows, d), F32), SDS((1, d), F32)],
        compiler_params=_cparams(("arbitrary",)), name=name)(h, g, target)


def _gelu(x):
    return 0.5 * x * (1.0 + lax.erf(x * (1.0 / math.sqrt(2.0))))


def _gate_tile(pu, pv, ln_g, ln_b, ws, bs_t):
    u = [_gelu(p) for p in pu]
    v = [_gelu(p) for p in pv]
    mu = sum(jnp.sum(t, axis=-1, keepdims=True) for t in v) * (1.0 / D_INNER)
    vc = [t - mu for t in v]
    var = sum(jnp.sum(t * t, axis=-1, keepdims=True) for t in vc) * (1.0 / D_INNER)
    rstd = lax.rsqrt(var + EPS)
    row = lax.broadcasted_iota(jnp.int32, (CHUNK, CHUNK), 0)
    col = lax.broadcasted_iota(jnp.int32, (CHUNK, CHUNK), 1)
    out = []
    for gi in range(A_GROUPS):
        vn = vc[gi] * rstd * ln_g[gi] + ln_b[gi]
        w = jnp.where(row >= col, ws[gi], 0.0)
        sv = _dot(w, vn) + bs_t[gi]
        out.append(u[gi] * sv)
    return out


def _split(ref, n, width):
    return [ref[:, i * width:(i + 1) * width] for i in range(n)]


def _gate_in_specs():
    return [
        pl.BlockSpec((CHUNK, D_INNER), lambda c: (c, 0)),
        pl.BlockSpec((CHUNK, D_INNER), lambda c: (c, 1)),
        pl.BlockSpec((1, D_INNER), lambda c: (0, 0)),
        pl.BlockSpec((1, D_INNER), lambda c: (0, 0)),
        pl.BlockSpec((A_GROUPS, CHUNK, CHUNK), lambda c: (0, 0, 0)),
        pl.BlockSpec((A_GROUPS, CHUNK, 1), lambda c: (0, 0, 0)),
    ]


def _gate_args(u_ref, v_ref, g_ref, b_ref, ws_ref, bs_ref):
    ng, gw = A_GROUPS, A_GROUP_W
    return (_split(u_ref, ng, gw), _split(v_ref, ng, gw), _split(g_ref, ng, gw), _split(b_ref, ng, gw),
            [ws_ref[i] for i in range(ng)], [bs_ref[i] for i in range(ng)])


def _gate_fwd(proj, ln_g, ln_b, ws, bs_col, mixcat, *, name):
    def body(u_ref, v_ref, g_ref, b_ref, ws_ref, bs_ref, cat_in, cat_ref):
        del cat_in
        out = _gate_tile(*_gate_args(u_ref, v_ref, g_ref, b_ref, ws_ref, bs_ref))
        for gi, o in enumerate(out):
            cat_ref[:, gi * A_GROUP_W:(gi + 1) * A_GROUP_W] = o.astype(cat_ref.dtype)

    return pl.pallas_call(
        body, grid=(N_CHUNKS,), in_specs=[*_gate_in_specs(), pl.BlockSpec(memory_space=pl.ANY)],
        out_specs=pl.BlockSpec((CHUNK, D_INNER), lambda c: (c, 0)), out_shape=SDS(mixcat.shape, mixcat.dtype),
        input_output_aliases={6: 0}, compiler_params=_cparams(("parallel",)), name=name,
    )(proj, proj, ln_g, ln_b, ws, bs_col, mixcat)


def _gate_bwd(proj, ln_g, ln_b, ws, bs_col, dcat, dproj, *, name):
    ng, gw = A_GROUPS, A_GROUP_W

    def body(u_ref, v_ref, g_ref, b_ref, ws_ref, bs_ref, d_ref, dproj_in, dproj_ref, dg_ref, db_ref, dws_ref, dbs_ref):
        del dproj_in
        args = _gate_args(u_ref, v_ref, g_ref, b_ref, ws_ref, bs_ref)
        _, vjp = jax.vjp(_gate_tile, *args)
        dpu, dpv, dg, db, dws, dbs = vjp(_split(d_ref, ng, gw))
        for gi in range(ng):
            dproj_ref[:, gi * gw:(gi + 1) * gw] = dpu[gi].astype(dproj_ref.dtype)
            dproj_ref[:, D_INNER + gi * gw:D_INNER + (gi + 1) * gw] = dpv[gi].astype(dproj_ref.dtype)

        @pl.when(pl.program_id(0) == 0)
        def _():
            for r in (dg_ref, db_ref, dws_ref, dbs_ref):
                r[...] = jnp.zeros_like(r)

        for gi in range(ng):
            dg_ref[:, gi * gw:(gi + 1) * gw] += dg[gi]
            db_ref[:, gi * gw:(gi + 1) * gw] += db[gi]
            dws_ref[gi] += dws[gi]
            dbs_ref[gi] += dbs[gi]

    in_specs = _gate_in_specs()
    return pl.pallas_call(
        body, grid=(N_CHUNKS,),
        in_specs=[*in_specs, pl.BlockSpec((CHUNK, D_INNER), lambda c: (c, 0)), pl.BlockSpec(memory_space=pl.ANY)],
        out_specs=[pl.BlockSpec((CHUNK, 2 * D_INNER), lambda c: (c, 0)), *in_specs[2:]],
        out_shape=[SDS(dproj.shape, dproj.dtype), SDS((1, D_INNER), F32), SDS((1, D_INNER), F32),
                   SDS((ng, CHUNK, CHUNK), F32), SDS((ng, CHUNK, 1), F32)],
        input_output_aliases={7: 0}, compiler_params=_cparams(("arbitrary",)), name=name,
    )(proj, proj, ln_g, ln_b, ws, bs_col, dcat, dproj)


ATT_TQ = 512


def _attn_tile(q, k, v):
    s = _dot_nt(q, k) * (1.0 / math.sqrt(X_HEAD_DIM))
    s = s - jnp.max(s, axis=-1, keepdims=True)
    e = jnp.exp(s)
    p = e / jnp.sum(e, axis=-1, keepdims=True)
    return _dot(p, v)


def _attn_in_specs(q_blk, order):
    hd = X_HEAD_DIM
    return [
        pl.BlockSpec((ATT_TQ, hd), lambda a, b: (order(a, b)[0], q_blk + order(a, b)[1])),
        pl.BlockSpec((N_MEM, hd), lambda a, b: (0, order(a, b)[1])),
        pl.BlockSpec((N_MEM, hd), lambda a, b: (0, X_HEADS + order(a, b)[1])),
    ]


def _attn_fwd(proj, q_off, kv, *, name):
    order = lambda i, h: (i, h)
    cat_blk = D_INNER // X_HEAD_DIM

    def body(q_ref, k_ref, v_ref, o_ref):
        o_ref[...] = _attn_tile(q_ref[...], k_ref[...], v_ref[...]).astype(o_ref.dtype)

    return pl.pallas_call(
        body, grid=(SEQ // ATT_TQ, X_HEADS), in_specs=_attn_in_specs(q_off // X_HEAD_DIM, order),
        out_specs=pl.BlockSpec((ATT_TQ, X_HEAD_DIM), lambda i, h: (i, cat_blk + h)),
        out_shape=SDS((SEQ, MIX_OUT), BF16), compiler_params=_cparams(("parallel", "parallel")), name=name,
    )(proj, kv, kv)


def _attn_bwd(proj, q_off, kv, dcat, dproj_width, dq_off, *, name):
    order = lambda h, i: (i, h)
    cat_blk = D_INNER // X_HEAD_DIM
    dq_blk = dq_off // X_HEAD_DIM

    def body(q_ref, k_ref, v_ref, do_ref, dq_ref, dk_ref, dv_ref):
        _, vjp = jax.vjp(_attn_tile, q_ref[...], k_ref[...], v_ref[...])
        dq, dk, dv = vjp(do_ref[...])
        dq_ref[...] = dq.astype(dq_ref.dtype)

        @pl.when(pl.program_id(1) == 0)
        def _():
            dk_ref[...] = jnp.zeros_like(dk_ref)
            dv_ref[...] = jnp.zeros_like(dv_ref)

        dk_ref[...] += dk
        dv_ref[...] += dv

    kv_spec = pl.BlockSpec((N_MEM, X_HEAD_DIM), lambda h, i: (0, h))
    return pl.pallas_call(
        body, grid=(X_HEADS, SEQ // ATT_TQ),
        in_specs=[*_attn_in_specs(q_off // X_HEAD_DIM, order),
                  pl.BlockSpec((ATT_TQ, X_HEAD_DIM), lambda h, i: (i, cat_blk + h))],
        out_specs=[pl.BlockSpec((ATT_TQ, X_HEAD_DIM), lambda h, i: (i, dq_blk + h)), kv_spec, kv_spec],
        out_shape=[SDS((SEQ, dproj_width), BF16), SDS((N_MEM, X_WIDTH), F32), SDS((N_MEM, X_WIDTH), F32)],
        compiler_params=_cparams(("parallel", "arbitrary")), name=name,
    )(proj, kv, kv, dcat)


CONV_TC = 512


def _shift_down(x, s):
    if s == 0:
        return x
    row = lax.broadcasted_iota(jnp.int32, x.shape, 0)
    return jnp.where(row >= s, pltpu.roll(x, s, 0), 0.0)


def _shift_up(x, s):
    if s == 0:
        return x
    n = x.shape[0]
    row = lax.broadcasted_iota(jnp.int32, x.shape, 0)
    return jnp.where(row < n - s, pltpu.roll(x, n - s, 0), 0.0)


def _conv_pre(x, w_ref, b_ref):
    pre = b_ref[...] + jnp.zeros_like(x)
    for k in range(CONV_K):
        pre = pre + w_ref[k:k + 1, :] * _shift_down(x, CONV_K - 1 - k)
    return pre


def _conv_fwd(proj, w, b, *, name):
    blk0 = D_INNER // CONV_TC

    def body(x_ref, w_ref, b_ref, o_ref):
        pre = _conv_pre(x_ref[...], w_ref, b_ref)
        o_ref[...] = pre * jax.nn.sigmoid(pre)

    return pl.pallas_call(
        body, grid=(CONV_DIM // CONV_TC,),
        in_specs=[pl.BlockSpec((SEQ, CONV_TC), lambda j: (0, blk0 + j)), pl.BlockSpec((CONV_K, CONV_TC), lambda j: (0, j)),
                  pl.BlockSpec((1, CONV_TC), lambda j: (0, j))],
        out_specs=pl.BlockSpec((SEQ, CONV_TC), lambda j: (0, j)), out_shape=SDS((SEQ, CONV_DIM), F32),
        compiler_params=_cparams(("parallel",)), name=name)(proj, w, b)


def _conv_bwd(proj, w, b, dxs, dbm, dcm, dproj, *, name):
    tc = CONV_TC // 2
    blk0 = D_INNER // tc
    n_x = D_INNER // tc
    n_b = SSM_GROUPS * SSM_STATE // tc

    def body(x_ref, w_ref, b_ref, dxs_ref, dbm_ref, dcm_ref, dproj_in, dproj_ref, dw_ref, db_ref):
        del dproj_in
        j = pl.program_id(0)
        x = x_ref[...]
        pre = _conv_pre(x, w_ref, b_ref)
        sg = jax.nn.sigmoid(pre)
        dact = jnp.where(j < n_x, dxs_ref[...], jnp.where(j < n_x + n_b, dbm_ref[...], dcm_ref[...]))
        dpre = dact * (sg * (1.0 + pre * (1.0 - sg)))
        dx = jnp.zeros_like(x)
        for k in range(CONV_K):
            s = CONV_K - 1 - k
            dx = dx + w_ref[k:k + 1, :] * _shift_up(dpre, s)
            dw_ref[k:k + 1, :] = jnp.sum(dpre * _shift_down(x, s), axis=0, keepdims=True)
        dproj_ref[...] = dx.astype(dproj_ref.dtype)
        db_ref[...] = jnp.sum(dpre, axis=0, keepdims=True)

    clip = lambda v, hi: jnp.minimum(jnp.maximum(v, 0), hi)
    return pl.pallas_call(
        body, grid=(CONV_DIM // tc,),
        in_specs=[pl.BlockSpec((SEQ, tc), lambda j: (0, blk0 + j)), pl.BlockSpec((CONV_K, tc), lambda j: (0, j)),
                  pl.BlockSpec((1, tc), lambda j: (0, j)),
                  pl.BlockSpec((SEQ, tc), lambda j: (0, clip(j, n_x - 1))),
                  pl.BlockSpec((SEQ, tc), lambda j: (0, clip(j - n_x, n_b - 1))),
                  pl.BlockSpec((SEQ, tc), lambda j: (0, clip(j - n_x - n_b, n_b - 1))),
                  pl.BlockSpec(memory_space=pl.ANY)],
        out_specs=[pl.BlockSpec((SEQ, tc), lambda j: (0, blk0 + j)), pl.BlockSpec((CONV_K, tc), lambda j: (0, j)),
                   pl.BlockSpec((1, tc), lambda j: (0, j))],
        out_shape=[SDS(dproj.shape, dproj.dtype), SDS((CONV_K, CONV_DIM), F32), SDS((1, CONV_DIM), F32)],
        input_output_aliases={6: 0}, compiler_params=_cparams(("parallel",)), name=name,
    )(proj, w, b, dxs, dbm, dcm, dproj)


SSM_PAIRS = SSM_HPG // 2


def _dot_exact01(x, m01, m01_t, x_first, differentiable):
    def product(v, m):
        hi = v.astype(BF16)
        rest = v - hi.astype(F32)
        mid = rest.astype(BF16)
        lo = (rest - mid.astype(F32)).astype(BF16)
        dims = (((1,), (0,)), ((), ()))
        dot = lambda part: lax.dot_general(*((part, m) if x_first else (m, part)), dims, preferred_element_type=F32)
        return dot(hi) + dot(mid) + dot(lo)

    if not differentiable:
        return product(x, m01)

    @jax.custom_vjp
    def exact(v):
        return product(v, m01)

    exact.defvjp(lambda v: (product(v, m01), None), lambda _, ct: (product(ct, m01_t),))
    return exact(x)


def _ssd_tile(xp, zp, bm, cm, hp, dt_c, dt_r, bias, bias_col, alog, alog_col, dsk, gnp, differentiable=False):
    row = lax.broadcasted_iota(jnp.int32, (CHUNK, CHUNK), 0)
    col = lax.broadcasted_iota(jnp.int32, (CHUNK, CHUNK), 1)
    causal = row >= col
    left = col < SSM_HEAD_DIM
    top = row < SSM_HEAD_DIM
    ones = jnp.ones((CHUNK, CHUNK), BF16)
    cb = _dot_nt(cm, bm)
    dtp = jax.nn.softplus(dt_c + bias)
    da_c = dtp * -jnp.exp(alog)
    da_r = jax.nn.softplus(dt_r + bias_col) * -jnp.exp(alog_col)
    lower = jnp.where(causal, 1.0, 0.0).astype(BF16)
    upper = jnp.where(row <= col, 1.0, 0.0).astype(BF16)
    cs = _dot_exact01(da_c, lower, upper, False, differentiable)
    cs_rows = _dot_exact01(da_r, upper, lower, True, differentiable)
    cs_last = jnp.sum(da_c, axis=0, keepdims=True)
    ecs, decay, ecl = jnp.exp(cs), jnp.exp(cs_last - cs), jnp.exp(cs_last)
    m = [cb * jnp.exp(jnp.where(causal, cs[:, r:r + 1] - cs_rows[r:r + 1, :], -1e30)) for r in range(SSM_HPG)]
    ygs, hn = [], []
    for p in range(SSM_PAIRS):
        a, b = 2 * p, 2 * p + 1
        pair = lambda v: jnp.where(left, v[:, a:a + 1], v[:, b:b + 1])
        xdt = xp[p] * pair(dtp)
        y = jnp.where(left, _dot(m[a], xdt), _dot(m[b], xdt))
        y = y + _dot_nt(cm, hp[p]) * pair(ecs)
        y = y + xp[p] * pair(dsk)
        states = _dot_tn(xdt * pair(decay), bm)
        hn.append(hp[p] * jnp.where(top, ecl[:, a:a + 1], ecl[:, b:b + 1]) + states)
        ygs.append(y * (zp[p] * jax.nn.sigmoid(zp[p])))
    ms = sum(_dot(t * t, ones) for t in ygs) * (1.0 / SSM_GROUP_W)
    rs = lax.rsqrt(ms + EPS)
    return [ygs[p] * rs * gnp[p] for p in range(SSM_PAIRS)], hn


def _ssd_in_specs(cidx):
    gw, n = SSM_GROUP_W, SSM_STATE
    bm_blk = D_INNER // n
    return [
        pl.BlockSpec((CHUNK, gw), lambda g, c: (cidx(c), g)),
        pl.BlockSpec((CHUNK, gw), lambda g, c: (cidx(c), g)),
        pl.BlockSpec((CHUNK, n), lambda g, c: (cidx(c), bm_blk + g)),
        pl.BlockSpec((CHUNK, n), lambda g, c: (cidx(c), bm_blk + SSM_GROUPS + g)),
        pl.BlockSpec((None, CHUNK, SSM_HPG), lambda g, c: (g, cidx(c), 0)),
        pl.BlockSpec((None, SSM_HPG, CHUNK), lambda g, c: (g, 0, cidx(c))),
        pl.BlockSpec((None, 3, SSM_HPG), lambda g, c: (g, 0, 0)),
        pl.BlockSpec((None, SSM_HPG, 2), lambda g, c: (g, 0, 0)),
        pl.BlockSpec((1, gw), lambda g, c: (0, g)),
    ]


def _ssd_args(x_ref, z_ref, bm_ref, cm_ref, hp, dtc_ref, dtr_ref, prow_ref, pcol_ref, gn_ref):
    npair, w = SSM_PAIRS, 2 * SSM_HEAD_DIM
    return (_split(x_ref, npair, w), _split(z_ref, npair, w), bm_ref[...], cm_ref[...], hp, dtc_ref[...], dtr_ref[...],
            prow_ref[0:1, :], pcol_ref[:, 0:1], prow_ref[1:2, :], pcol_ref[:, 1:2], prow_ref[2:3, :],
            _split(gn_ref, npair, w))


def _pair_rows(ref):
    w = 2 * SSM_HEAD_DIM
    return [ref[p * w:(p + 1) * w, :] for p in range(SSM_PAIRS)]


def _ssd_fwd(xbc, proj, dt_c, dt_r, par_row, par_col, gn, mixcat, *, name):
    w = 2 * SSM_HEAD_DIM

    def body(x_ref, z_ref, bm_ref, cm_ref, dtc_ref, dtr_ref, prow_ref, pcol_ref, gn_ref, cat_in,
             cat_ref, hprev_ref, h_scr):
        del cat_in

        @pl.when(pl.program_id(1) == 0)
        def _():
            h_scr[...] = jnp.zeros_like(h_scr)

        hprev_ref[...] = h_scr[...]
        yn, hn = _ssd_tile(*_ssd_args(x_ref, z_ref, bm_ref, cm_ref, _pair_rows(h_scr), dtc_ref, dtr_ref, prow_ref,
                                      pcol_ref, gn_ref))
        for p in range(SSM_PAIRS):
            cat_ref[:, p * w:(p + 1) * w] = yn[p].astype(cat_ref.dtype)
            h_scr[p * w:(p + 1) * w, :] = hn[p]

    return pl.pallas_call(
        body, grid=(SSM_GROUPS, N_CHUNKS), in_specs=[*_ssd_in_specs(lambda c: c), pl.BlockSpec(memory_space=pl.ANY)],
        out_specs=[pl.BlockSpec((CHUNK, SSM_GROUP_W), lambda g, c: (c, g)),
                   pl.BlockSpec((None, None, SSM_GROUP_W, SSM_STATE), lambda g, c: (c, g, 0, 0))],
        out_shape=[SDS(mixcat.shape, mixcat.dtype), SDS((N_CHUNKS, SSM_GROUPS, SSM_GROUP_W, SSM_STATE), F32)],
        scratch_shapes=[pltpu.VMEM((SSM_GROUP_W, SSM_STATE), F32)],
        input_output_aliases={9: 0}, compiler_params=_cparams(("parallel", "arbitrary")), name=name,
    )(xbc, proj, xbc, xbc, dt_c, dt_r, par_row, par_col, gn, mixcat)


def _ssd_bwd(xbc, proj, dt_c, dt_r, par_row, par_col, gn, hprev, dcat, dproj, *, name):
    nh, w, gw, n = SSM_HPG, 2 * SSM_HEAD_DIM, SSM_GROUP_W, SSM_STATE
    rev = lambda c: N_CHUNKS - 1 - c

    def body(x_ref, z_ref, bm_ref, cm_ref, dtc_ref, dtr_ref, prow_ref, pcol_ref, gn_ref, hprev_ref, dy_ref,
             dproj_in, dz_ref, dxs_ref, dbm_ref, dcm_ref, ddtc_ref, ddtr_ref, dprow_ref, dpcol_ref, dgn_ref, dh_scr):
        del dproj_in
        first = pl.program_id(1) == 0

        @pl.when(first)
        def _():
            dh_scr[...] = jnp.zeros_like(dh_scr)
            for ref in (dprow_ref, dpcol_ref, dgn_ref):
                ref[...] = jnp.zeros_like(ref)

        args = _ssd_args(x_ref, z_ref, bm_ref, cm_ref, _pair_rows(hprev_ref), dtc_ref, dtr_ref, prow_ref, pcol_ref,
                         gn_ref)
        _, vjp = jax.vjp(lambda *a: _ssd_tile(*a, differentiable=True), *args)
        dxs, dzs, dbm, dcm, dhs, ddtc, ddtr, dbias, dbias_col, dalog, dalog_col, ddsk, dgn = vjp(
            (_split(dy_ref, SSM_PAIRS, w), _pair_rows(dh_scr)))
        dbm_ref[...] = dbm
        dcm_ref[...] = dcm
        ddtc_ref[...] = ddtc
        ddtr_ref[...] = ddtr
        for q in range(SSM_PAIRS):
            dxs_ref[:, q * w:(q + 1) * w] = dxs[q]
            dz_ref[:, q * w:(q + 1) * w] = dzs[q].astype(dz_ref.dtype)
            dh_scr[q * w:(q + 1) * w, :] = dhs[q]
            dgn_ref[:, q * w:(q + 1) * w] += dgn[q]
        for i, d in enumerate((dbias, dalog, ddsk)):
            dprow_ref[i:i + 1, :] += d
        for i, d in enumerate((dbias_col, dalog_col)):
            dpcol_ref[:, i:i + 1] += d

    return pl.pallas_call(
        body, grid=(SSM_GROUPS, N_CHUNKS),
        in_specs=[*_ssd_in_specs(rev),
                  pl.BlockSpec((None, None, gw, n), lambda g, c: (rev(c), g, 0, 0)),
                  pl.BlockSpec((CHUNK, gw), lambda g, c: (rev(c), g)),
                  pl.BlockSpec(memory_space=pl.ANY)],
        out_specs=[pl.BlockSpec((CHUNK, gw), lambda g, c: (rev(c), g)),
                   pl.BlockSpec((CHUNK, gw), lambda g, c: (rev(c), g)),
                   pl.BlockSpec((CHUNK, n), lambda g, c: (rev(c), g)),
                   pl.BlockSpec((CHUNK, n), lambda g, c: (rev(c), g)),
                   pl.BlockSpec((None, CHUNK, nh), lambda g, c: (g, rev(c), 0)),
                   pl.BlockSpec((None, nh, CHUNK), lambda g, c: (g, 0, rev(c))),
                   pl.BlockSpec((None, 3, nh), lambda g, c: (g, 0, 0)),
                   pl.BlockSpec((None, nh, 2), lambda g, c: (g, 0, 0)),
                   pl.BlockSpec((1, gw), lambda g, c: (0, g))],
        out_shape=[SDS(dproj.shape, dproj.dtype), SDS((SEQ, D_INNER), F32), SDS((SEQ, SSM_GROUPS * n), F32),
                   SDS((SEQ, SSM_GROUPS * n), F32), SDS((SSM_GROUPS, SEQ, nh), F32), SDS((SSM_GROUPS, nh, SEQ), F32),
                   SDS((SSM_GROUPS, 3, nh), F32), SDS((SSM_GROUPS, nh, 2), F32), SDS((1, D_INNER), F32)],
        scratch_shapes=[pltpu.VMEM((gw, n), F32)],
        input_output_aliases={11: 0}, compiler_params=_cparams(("parallel", "arbitrary")), name=name,
    )(xbc, proj, xbc, xbc, dt_c, dt_r, par_row, par_col, gn, hprev, dcat, dproj)


def _sum_contributions(chip, parts, landed, *, name):
    _, r, c = parts.shape
    tr = _pick(r, (256, 384, 128))

    def body(chip_ref, own_ref, landed_ref, o_ref):
        del chip_ref
        acc = own_ref[...].astype(F32)
        for s in range(landed_ref.shape[0]):
            acc = acc + landed_ref[s].astype(F32)
        o_ref[...] = acc

    grid_spec = pltpu.PrefetchScalarGridSpec(
        num_scalar_prefetch=1, grid=(r // tr,),
        in_specs=[pl.BlockSpec((None, tr, c), lambda i, chip_ref: (chip_ref[0], i, 0)),
                  pl.BlockSpec((landed.shape[0], tr, c), lambda i, chip_ref: (0, i, 0))],
        out_specs=pl.BlockSpec((tr, c), lambda i, chip_ref: (i, 0)))
    return pl.pallas_call(body, grid_spec=grid_spec, out_shape=SDS((r, c), F32),
                          compiler_params=_cparams(("parallel",)), name=name)(chip, parts, landed)


def _adamw(w, g, m, v, *, name):
    r, c = w.shape
    tr = r if r <= 256 else _pick(r, (256, 128, 8))
    spec = pl.BlockSpec((tr, c), lambda i: (i, 0))

    def body(w_ref, g_ref, m_ref, v_ref, d_ref, mo_ref, vo_ref):
        g = g_ref[...]
        m_new = ADAM_B1 * m_ref[...] + (1.0 - ADAM_B1) * g
        v_new = ADAM_B2 * v_ref[...] + (1.0 - ADAM_B2) * (g * g)
        m_hat = m_new / (1.0 - ADAM_B1 ** ADAM_STEP)
        v_hat = v_new / (1.0 - ADAM_B2 ** ADAM_STEP)
        d_ref[...] = -ADAM_LR * (m_hat / (jnp.sqrt(v_hat) + ADAM_EPS) + ADAM_WD * w_ref[...])
        mo_ref[...] = m_new
        vo_ref[...] = v_new

    return pl.pallas_call(body, grid=(r // tr,), in_specs=[spec] * 4, out_specs=[spec] * 3,
                          out_shape=[SDS((r, c), F32)] * 3, compiler_params=_cparams(("parallel",)), name=name)(w, g, m, v)


ANY = pl.BlockSpec(memory_space=pl.ANY)


def _place():
    x, y, c = lax.axis_index("x"), lax.axis_index("y"), lax.axis_index("c")
    chips = [(1 - x, y), (x, 1 - y), (1 - x, 1 - y)]
    return x, y, c, chips


def _remote(src, dst, send_sem, recv_sem, to):
    return pltpu.make_async_remote_copy(src_ref=src, dst_ref=dst, send_sem=send_sem, recv_sem=recv_sem,
                                        device_id=to, device_id_type=MESH)


STREAM_ROWS = 256


def _stream_rows(i):
    return pl.ds(pl.multiple_of(i * STREAM_ROWS, STREAM_ROWS), STREAM_ROWS)


def _channel_scratch(width, dtype, rows=STREAM_ROWS):
    buf = (2, rows, width)
    return [pltpu.VMEM(buf, dtype), pltpu.VMEM(buf, dtype), *([pltpu.SemaphoreType.DMA((2,))] * 5),
            pltpu.SemaphoreType.REGULAR((2,))]


CHANNEL_REFS = 8


def _copy_blocks(srcs, dsts, ch):
    sbuf, _, ld, _, _, st, _, _ = ch
    n = len(srcs)
    load = lambda i: pltpu.make_async_copy(srcs[i], sbuf.at[i % 2], ld.at[i % 2])
    store = lambda i: pltpu.make_async_copy(sbuf.at[i % 2], dsts[i], st.at[i % 2])
    load(0).start()
    for i in range(n):
        if i + 1 < n:
            if i >= 1:
                store(i - 1).wait()
            load(i + 1).start()
        load(i).wait()
        store(i).start()
    for i in range(max(0, n - 2), n):
        store(i).wait()


def _exchange_blocks(srcs, dsts, keeps, ch, sibling):
    sbuf, rbuf, ld, snd, rcv, st, kp, credit = ch
    n = len(srcs)
    load = lambda i: pltpu.make_async_copy(srcs[i], sbuf.at[i % 2], ld.at[i % 2])
    push = lambda i: _remote(sbuf.at[i % 2], rbuf.at[i % 2], snd.at[i % 2], rcv.at[i % 2], sibling)
    store = lambda i: pltpu.make_async_copy(rbuf.at[i % 2], dsts[i], st.at[i % 2])
    save = lambda i: pltpu.make_async_copy(sbuf.at[i % 2], keeps[i], kp.at[i % 2])

    def send(i):
        load(i).wait()
        pl.semaphore_wait(credit.at[i % 2], 1)
        push(i).start()
        if keeps[i] is not None:
            save(i).start()

    for i in range(min(2, n)):
        pl.semaphore_signal(credit.at[i], 1, device_id=sibling, device_id_type=MESH)
        load(i).start()
    send(0)
    for i in range(n):
        if i >= 1:
            store(i - 1).wait()
            if i + 1 < n:
                pl.semaphore_signal(credit.at[(i + 1) % 2], 1, device_id=sibling, device_id_type=MESH)
        if i + 1 < n:
            send(i + 1)
        push(i).wait_recv()
        store(i).start()
        push(i).wait_send()
        if keeps[i] is not None:
            save(i).wait()
        if i + 2 < n:
            load(i + 2).start()
    store(n - 1).wait()


def _all_gather_shards(shards, small, *, name):
    n = len(shards)

    def body(*refs):
        ins, outs = refs[:n + 1], refs[n + 1:2 * n + 2]
        scr = refs[2 * n + 2:]
        chans = [scr[CHANNEL_REFS * t:CHANNEL_REFS * (t + 1)] for t in range(n)]
        send_sems, recv_sems, small_sems = scr[CHANNEL_REFS * n:]
        x, y, c, _ = _place()
        me = 2 * x + y
        sibling = (x, y, 1 - c)
        near = (lax.rem(x + 1 - c, 2), lax.rem(y + c, 2))
        far = (lax.rem(x + c, 2), lax.rem(y + 1 - c, 2))
        k_near, k_far, k_diag = 2 * near[0] + near[1], 2 * far[0] + far[1], 3 - me
        targets = ((*near, c), (*far, c), (*far, c))
        arrives = (k_near, k_far, k_diag)
        streams_in = (k_far, k_near, k_diag)

        def ici(t, j, src, blk):
            return _remote(src, outs[t].at[blk, c], send_sems.at[3 * t + j], recv_sems.at[3 * t + j], targets[j])

        first = [ici(t, j, ins[t].at[c], me) for t in range(n + 1) for j in range(2)]
        for cp in first:
            cp.start()
        small_local = pltpu.make_async_copy(ins[n], outs[n].at[me], small_sems.at[6])
        small_local.start()
        for t in range(n):
            _copy_blocks([ins[t].at[h] for h in range(2)], [outs[t].at[me, h] for h in range(2)], chans[t])
        passed = []
        for j in range(3):
            for t in range(n + 1):
                landed = outs[t].at[arrives[j], c]
                ici(t, j, landed, arrives[j]).wait_recv()
                if j == 0:
                    fwd = ici(t, 2, landed, k_near)
                    fwd.start()
                    passed.append(fwd)
                if t < n:
                    _exchange_blocks([landed], [outs[t].at[streams_in[j], 1 - c]], [None], chans[t], sibling)
                else:
                    fwd = _remote(landed, landed, small_sems.at[j], small_sems.at[3 + j], sibling)
                    fwd.start()
                    passed.append(fwd)
        for j in range(3):
            got = outs[n].at[streams_in[j], 1 - c]
            _remote(got, got, small_sems.at[j], small_sems.at[3 + j], sibling).wait_recv()
        for cp in first + passed:
            cp.wait_send()
        small_local.wait()

    scratch = []
    for s in shards:
        scratch += _channel_scratch(s.shape[2], s.dtype, rows=s.shape[1])
    return pl.pallas_call(
        body, in_specs=[ANY] * (n + 1), out_specs=[ANY] * (n + 1),
        out_shape=[SDS((N_CHIPS, *s.shape), s.dtype) for s in (*shards, small)],
        scratch_shapes=[*scratch, pltpu.SemaphoreType.DMA((3 * n + 3,)), pltpu.SemaphoreType.DMA((3 * n + 3,)),
                        pltpu.SemaphoreType.DMA((7,))],
        compiler_params=pltpu.CompilerParams(vmem_limit_bytes=VMEM_LIMIT), name=name)(*shards, small)


def _pair_reduce(stacks, *, name):
    n = len(stacks)
    per = 11

    def body(*refs):
        ins, outs, scr = refs[:n], refs[n:2 * n], refs[2 * n:]
        x, y, c, _ = _place()
        sibling = (x, y, 1 - c)
        streams = []
        for t in range(n):
            sraw, sbuf, rbuf, obuf, pbuf, ld_s, ld_o, snd, rcv, st, credit = scr[per * t:per * (t + 1)]
            steps = ins[t].shape[1] // STREAM_ROWS
            src, own, out = ins[t].at[1 - c], ins[t].at[c], outs[t]
            assert steps >= 2

            def load_s(i, slot, src=src, sraw=sraw, ld_s=ld_s):
                return pltpu.make_async_copy(src.at[_stream_rows(i)], sraw.at[slot], ld_s.at[slot])

            def load_o(i, slot, own=own, obuf=obuf, ld_o=ld_o):
                return pltpu.make_async_copy(own.at[_stream_rows(i)], obuf.at[slot], ld_o.at[slot])

            def push(slot, sbuf=sbuf, rbuf=rbuf, snd=snd, rcv=rcv):
                return _remote(sbuf.at[slot], rbuf.at[slot], snd.at[slot], rcv.at[slot], sibling)

            def store(i, slot, pbuf=pbuf, out=out, st=st):
                return pltpu.make_async_copy(pbuf.at[slot], out.at[_stream_rows(i)], st.at[slot])

            def send(i, slot, load_s=load_s, push=push, sraw=sraw, sbuf=sbuf, credit=credit):
                load_s(i, slot).wait()
                sbuf[slot] = sraw[slot].astype(sbuf.dtype)
                pl.semaphore_wait(credit.at[slot], 1)
                push(slot).start()

            def combine(i, slot, load_s=load_s, load_o=load_o, push=push, store=store, rbuf=rbuf, obuf=obuf, pbuf=pbuf,
                        credit=credit, steps=steps):
                load_o(i, slot).wait()
                push(slot).wait_recv()

                @pl.when(i >= 2)
                def _():
                    store(i, slot).wait()

                pbuf[slot] = (obuf[slot] + rbuf[slot].astype(F32)).astype(pbuf.dtype)
                store(i, slot).start()
                push(slot).wait_send()

                @pl.when(i + 2 < steps)
                def _():
                    load_s(i + 2, slot).start()
                    load_o(i + 2, slot).start()
                    pl.semaphore_signal(credit.at[slot], 1, device_id=sibling, device_id_type=MESH)

            for slot in range(2):
                pl.semaphore_signal(credit.at[slot], 1, device_id=sibling, device_id_type=MESH)
                load_s(slot, slot).start()
                load_o(slot, slot).start()
            streams.append((steps, send, combine, store))
        for _, send, _, _ in streams:
            send(0, 0)

        def step(i, carry):
            slot = lax.rem(i, 2)
            for steps, send, _, _ in streams:
                @pl.when(i + 1 < steps)
                def _(send=send):
                    send(i + 1, 1 - slot)
            for steps, _, combine, _ in streams:
                @pl.when(i < steps)
                def _(combine=combine):
                    combine(i, slot)
            return carry

        lax.fori_loop(0, max(s[0] for s in streams), step, 0)
        for _, _, _, store in streams:
            for slot in range(2):
                store(0, slot).wait()

    scratch = []
    for s in stacks:
        buf = (2, STREAM_ROWS, s.shape[2])
        scratch += [pltpu.VMEM(buf, F32), pltpu.VMEM(buf, BF16), pltpu.VMEM(buf, BF16), pltpu.VMEM(buf, F32),
                    pltpu.VMEM(buf, BF16), *([pltpu.SemaphoreType.DMA((2,))] * 5), pltpu.SemaphoreType.REGULAR((2,))]
    return pl.pallas_call(
        body, in_specs=[ANY] * n, out_specs=[ANY] * n, out_shape=[SDS(s.shape[1:], BF16) for s in stacks],
        scratch_shapes=scratch, compiler_params=pltpu.CompilerParams(vmem_limit_bytes=VMEM_LIMIT), name=name)(*stacks)


HBM_SPEC = pl.BlockSpec(memory_space=pltpu.HBM)
SEM_SPEC = pl.BlockSpec(memory_space=pltpu.SEMAPHORE)
SIDE_EFFECT = pltpu.SideEffectType.DATAFLOW_SIDE_EFFECTING


def _scatter_copies(ins, lands, send_sems, recv_sems):
    _, _, c, chips = _place()
    return [_remote(ins[t].at[2 * cx + cy], lands[t].at[j], send_sems.at[3 * t + j], recv_sems.at[3 * t + j],
                    (cx, cy, c)) for t in range(len(ins)) for j, (cx, cy) in enumerate(chips)]


def _chip_scatter_start(parts, *, name):
    n = len(parts)

    def body(*refs):
        ins, lands = refs[:n], refs[n:2 * n]
        send_sems, recv_sems, token = refs[2 * n], refs[2 * n + 1], refs[-1]
        for cp in _scatter_copies(ins, lands, send_sems, recv_sems):
            cp.start()
        token[...] = jnp.zeros_like(token)

    hbm = lambda a: pltpu.with_memory_space_constraint(a, pltpu.HBM)
    lands = [hbm(lax.empty((3, *p.shape[1:]), p.dtype)) for p in parts]
    thru = [pltpu.HBM(a.shape, a.dtype) for a in (*parts, *lands)]
    outs = pl.pallas_call(
        body, name=name,
        out_shape=(pltpu.SemaphoreType.DMA((3 * n,)), pltpu.SemaphoreType.DMA((3 * n,)), *thru, SDS((8, 128), F32)),
        in_specs=[HBM_SPEC] * (2 * n),
        out_specs=(SEM_SPEC, SEM_SPEC, *([HBM_SPEC] * (2 * n)), pl.BlockSpec(memory_space=pltpu.VMEM)),
        input_output_aliases={i: 2 + i for i in range(2 * n)},
        compiler_params=pltpu.CompilerParams(has_side_effects=SIDE_EFFECT),
    )(*[hbm(p) for p in parts], *lands)
    return outs[0], outs[1], outs[2:2 + n], outs[2 + n:2 + 2 * n], outs[-1]


def _chip_scatter_wait(send_sems, recv_sems, parts, lands, after, *, name):
    n = len(parts)

    def body(*refs):
        ins, lands_in = refs[:n], refs[n:2 * n]
        for cp in _scatter_copies(ins, lands_in, refs[2 * n], refs[2 * n + 1]):
            cp.wait_send()
            cp.wait_recv()

    outs = pl.pallas_call(
        body, name=name, out_shape=[pltpu.HBM(a.shape, a.dtype) for a in (*parts, *lands)],
        in_specs=[*([HBM_SPEC] * (2 * n)), SEM_SPEC, SEM_SPEC, *([ANY] * len(after))],
        out_specs=[HBM_SPEC] * (2 * n), input_output_aliases={i: i for i in range(2 * n)},
        compiler_params=pltpu.CompilerParams(has_side_effects=SIDE_EFFECT),
    )(*parts, *lands, send_sems, recv_sems, *after)
    return outs[:n], outs[n:]


def _gather_copies(shards, zones, send_sems, recv_sems):
    x, y, c, chips = _place()
    return [_remote(shards[t].at[c], zones[t].at[2 * x + y, c], send_sems.at[3 * t + j], recv_sems.at[3 * t + j],
                    (cx, cy, c)) for t in range(len(shards)) for j, (cx, cy) in enumerate(chips)]


def _gather_start(shards, after, *, name):
    n = len(shards)

    def body(*refs):
        ins, zones = refs[:n], refs[n:2 * n]
        send_sems, recv_sems, token = refs[2 * n + len(after)], refs[2 * n + len(after) + 1], refs[-1]
        for cp in _gather_copies(ins, zones, send_sems, recv_sems):
            cp.start()
        token[...] = jnp.zeros_like(token)

    hbm = lambda a: pltpu.with_memory_space_constraint(a, pltpu.HBM)
    zones = [hbm(lax.empty((N_CHIPS, *s.shape), s.dtype)) for s in shards]
    thru = [pltpu.HBM(a.shape, a.dtype) for a in (*shards, *zones)]
    outs = pl.pallas_call(
        body, name=name,
        out_shape=(pltpu.SemaphoreType.DMA((3 * n,)), pltpu.SemaphoreType.DMA((3 * n,)), *thru, SDS((8, 128), F32)),
        in_specs=[*([HBM_SPEC] * (2 * n)), *([ANY] * len(after))],
        out_specs=(SEM_SPEC, SEM_SPEC, *([HBM_SPEC] * (2 * n)), pl.BlockSpec(memory_space=pltpu.VMEM)),
        input_output_aliases={i: 2 + i for i in range(2 * n)},
        compiler_params=pltpu.CompilerParams(has_side_effects=SIDE_EFFECT),
    )(*[hbm(s) for s in shards], *zones, *after)
    return outs[0], outs[1], outs[2:2 + n], outs[2 + n:2 + 2 * n], outs[-1]


def _gather_wait(send_sems, recv_sems, shards, zones, after, *, name):
    n = len(shards)

    def body(*refs):
        for cp in _gather_copies(refs[:n], refs[n:2 * n], refs[2 * n], refs[2 * n + 1]):
            cp.wait_send()
            cp.wait_recv()

    outs = pl.pallas_call(
        body, name=name, out_shape=[pltpu.HBM(a.shape, a.dtype) for a in (*shards, *zones)],
        in_specs=[*([HBM_SPEC] * (2 * n)), SEM_SPEC, SEM_SPEC, *([ANY] * len(after))],
        out_specs=[HBM_SPEC] * (2 * n), input_output_aliases={i: i for i in range(2 * n)},
        compiler_params=pltpu.CompilerParams(has_side_effects=SIDE_EFFECT),
    )(*shards, *zones, send_sems, recv_sems, *after)
    return outs[:n], outs[n:]


def _gather_finish(shards, zones, *, name):
    n = len(shards)

    def body(*refs):
        ins, zones_in, outs, scr = refs[:n], refs[n:2 * n], refs[2 * n:3 * n], refs[3 * n:]
        x, y, c, chips = _place()
        me = 2 * x + y
        sibling = (x, y, 1 - c)
        others = [2 * cx + cy for cx, cy in chips]
        for t in range(n):
            chan = scr[CHANNEL_REFS * t:CHANNEL_REFS * (t + 1)]
            _copy_blocks([ins[t].at[h] for h in range(2)], [outs[t].at[me, h] for h in range(2)], chan)
            _exchange_blocks([zones_in[t].at[k, c] for k in others], [outs[t].at[k, 1 - c] for k in others],
                             [None] * len(others), chan, sibling)

    scratch = []
    for s in shards:
        scratch += _channel_scratch(s.shape[2], s.dtype, rows=s.shape[1])
    return pl.pallas_call(
        body, in_specs=[ANY] * (2 * n), out_specs=[ANY] * n, out_shape=[SDS(z.shape, z.dtype) for z in zones],
        input_output_aliases={n + t: t for t in range(n)}, scratch_shapes=scratch,
        compiler_params=pltpu.CompilerParams(vmem_limit_bytes=VMEM_LIMIT), name=name)(*shards, *zones)


def _pair_share(groups, *, name):
    finals = [f for grp in groups for f in grp]
    n, n_out = len(finals), len(groups)

    def body(*refs):
        ins, outs, scr = refs[:n], refs[n:n + n_out], refs[n + n_out:]
        x, y, c, _ = _place()
        sibling = (x, y, 1 - c)
        t = 0
        for o, grp in enumerate(groups):
            rows = grp[0].shape[0] // 2
            blocks = [(layer, pl.ds(b * rows, rows)) for layer in range(len(grp)) for b in range(2)]
            _exchange_blocks([ins[t + layer].at[rs] for layer, rs in blocks],
                             [outs[o].at[layer, 1 - c, rs] for layer, rs in blocks],
                             [outs[o].at[layer, c, rs] for layer, rs in blocks],
                             scr[CHANNEL_REFS * o:CHANNEL_REFS * (o + 1)], sibling)
            t += len(grp)

    scratch = []
    for grp in groups:
        scratch += _channel_scratch(grp[0].shape[1], grp[0].dtype, rows=grp[0].shape[0] // 2)
    return pl.pallas_call(
        body, in_specs=[ANY] * n, out_specs=[ANY] * n_out,
        out_shape=[SDS((len(grp), 2, *grp[0].shape), grp[0].dtype) for grp in groups],
        scratch_shapes=scratch, compiler_params=pltpu.CompilerParams(vmem_limit_bytes=VMEM_LIMIT), name=name)(*finals)


def _all_reduce_small(v, *, name):
    rows, lanes = v.shape
    n_dev = 8

    def body(v_ref, o_ref, all_ref, send_sems, recv_sems, local_sem):
        x, y, c, chips = _place()
        me, sibling = (x, y, c), (x, y, 1 - c)

        def block(px, py, pc):
            return all_ref.at[4 * px + 2 * py + pc]

        def copy(k, blk, to, src=None):
            return _remote(block(*blk) if src is None else src, block(*blk), send_sems.at[k], recv_sems.at[k], to)

        mine = pltpu.make_async_copy(v_ref, block(*me), local_sem)
        mine.start()
        first = [copy(0, me, sibling, src=v_ref)]
        first += [copy(1 + j, me, (*chip, c), src=v_ref) for j, chip in enumerate(chips)]
        for cp in first:
            cp.start()
        passed = [copy(4 + j, (*chip, c), sibling) for j, chip in enumerate(chips)]
        for j, chip in enumerate(chips):
            copy(1 + j, (*chip, c), me).wait_recv()
            passed[j].start()
        copy(0, sibling, me).wait_recv()
        for j, chip in enumerate(chips):
            copy(4 + j, (*chip, 1 - c), me).wait_recv()
        for cp in first + passed:
            cp.wait_send()
        mine.wait()
        acc = all_ref[0]
        for k in range(1, n_dev):
            acc = acc + all_ref[k]
        o_ref[...] = acc

    vmem = pl.BlockSpec(memory_space=pltpu.VMEM)
    return pl.pallas_call(
        body, in_specs=[vmem], out_specs=vmem, out_shape=SDS((rows, lanes), F32),
        scratch_shapes=[pltpu.VMEM((n_dev, rows, lanes), F32), pltpu.SemaphoreType.DMA((7,)),
                        pltpu.SemaphoreType.DMA((7,)), pltpu.SemaphoreType.DMA],
        compiler_params=pltpu.CompilerParams(vmem_limit_bytes=VMEM_LIMIT), name=name)(v)


def _relu2_epilogue(acc):
    return acc, jnp.square(jnp.maximum(acc, 0.0))


def _res_epilogue(acc, res):
    return (acc + res,)


def _drelu2_epilogue(acc, pre):
    return (acc * (2.0 * jnp.maximum(pre.astype(F32), 0.0)),)


def _ffn_fwd(h, g, w1, w2, tag):
    f = _rms_fwd(h, g, name=f"ffn_norm_{tag}")
    pre, act = _mm_nn(f, w1, name=f"ffn1_{tag}", epilogue=_relu2_epilogue, n_out_dtypes=(BF16, BF16))
    h_out = _mm_nn(act, w2, name=f"ffn2_{tag}", extras=(h,), epilogue=_res_epilogue)
    return h_out, (f, pre, act)


def _ffn_bwd(dh, h, g, w1, w2, saved, layer, after=()):
    f, pre, act = saved
    dpre = _mm_nt(dh, w2, name=f"ffn2_dx_{layer}", out_dtype=BF16, extras=(pre,), epilogue=_drelu2_epilogue,
                  after=after)
    dw2 = _mm_tn_stacked(act, dh, name=f"ffn2_dw_{layer}", col_slots=False)
    df = _mm_nt(dpre, w1, name=f"ffn1_dx_{layer}")
    dw1 = _mm_tn_stacked(f, dpre, name=f"ffn1_dw_{layer}", col_slots=True)
    dh, dg = _rms_bwd(h, g, df, dh, name=f"ffn_norm_bwd_{layer}")
    return dh, dg, dw1, dw2


def _kv_fwd(mem, g, w_kv, tag):
    m = _rms_fwd(mem, g, name=f"mem_norm_{tag}")
    return m, _mm_nn(m, w_kv, name=f"kv_{tag}")


def _kv_bwd(mem, g, w_kv, m, dk, dv, layer):
    dkv = jnp.concatenate([dk, dv], axis=1)
    dw = _mm_tn_stacked(m, dkv, name=f"kv_dw_{layer}", col_slots=True)
    dm = _mm_nt(dkv, w_kv, name=f"kv_dx_{layer}")
    _, dg = _rms_bwd(mem, g, dm, dm, name=f"mem_norm_bwd_{layer}")
    return dw, dg


def _local_step(x, mem, target, p, after_layer1=None, after_ffn0=None, after_mixer0=None):
    row = lambda v: v.reshape(1, -1)
    g = {}

    h0 = x
    a0 = _rms_fwd(h0, row(p["norm_mix"][0]), name="mix_norm_0")
    proj_a = _mm_nn(a0, p["a_in"], name="a_in", after=p.get("after_start", ()))
    m0, kv0 = _kv_fwd(mem, row(p["mem_norm"][0]), p["w_kv"][0], "0")
    cat0 = _attn_fwd(proj_a, 2 * D_INNER, kv0, name="attn_0")
    bs_col = p["a_bs"].reshape(A_GROUPS, CHUNK, 1)
    cat0 = _gate_fwd(proj_a, p["a_ln_g"], p["a_ln_b"], p["a_ws"], bs_col, cat0, name="gate")
    h1 = _mm_nn(cat0, p["w_out"][0], name="out_0", extras=(h0,), epilogue=_res_epilogue)
    h2, ffn0 = _ffn_fwd(h1, row(p["norm_ffn"][0]), p["w_ffn1"][0], p["w_ffn2"][0], "0")

    if "layer1_mixer" in p:
        w_kv1, w_out1, b_in = p["layer1_mixer"](h2)
    else:
        w_kv1, w_out1, b_in = p["w_kv"][1], p["w_out"][1], p["b_in"]
    a1 = _rms_fwd(h2, row(p["norm_mix"][1]), name="mix_norm_1")
    proj_b = _mm_nn(a1, b_in, name="b_in")
    m1, kv1 = _kv_fwd(mem, row(p["mem_norm"][1]), w_kv1, "1")
    cat1 = _attn_fwd(proj_b, B_Q_OFF, kv1, name="attn_1")
    xbc = _conv_fwd(proj_b, p["b_conv_w"], p["b_conv_b"], name="conv")
    dt_raw = proj_b[:, B_DT_OFF:B_DT_OFF + SSM_HEADS].reshape(SEQ, SSM_GROUPS, SSM_HPG)
    dt_c = jnp.transpose(dt_raw, (1, 0, 2))
    dt_r = jnp.transpose(dt_raw, (1, 2, 0))
    per_head = lambda v: v.reshape(SSM_GROUPS, 1, SSM_HPG)
    par_row = jnp.concatenate([per_head(p["b_dt_bias"]), per_head(p["b_a_log"]), per_head(p["b_d"])], axis=1)
    ssd_par = (par_row, jnp.transpose(par_row[:, :2], (0, 2, 1)), p["b_gnorm"])
    cat1, hprev = _ssd_fwd(xbc, proj_b, dt_c, dt_r, *ssd_par, cat1, name="ssd")
    h3 = _mm_nn(cat1, w_out1, name="out_1", extras=(h2,), epilogue=_res_epilogue)
    w_ffn1_1, w_ffn2_1 = p["layer1_ffn"](h3) if "layer1_ffn" in p else (p["w_ffn1"][1], p["w_ffn2"][1])
    h4, ffn1 = _ffn_fwd(h3, row(p["norm_ffn"][1]), w_ffn1_1, w_ffn2_1, "1")

    loss, dh, g["final_norm"] = _loss_head(h4, row(p["final_norm"]), target, name="loss_head")

    dh, dnf1, dw1_1, dw2_1 = _ffn_bwd(dh, h3, row(p["norm_ffn"][1]), w_ffn1_1, w_ffn2_1, ffn1, 1)
    dcat1 = _mm_nt(dh, w_out1, name="out_dx_1")
    dwo_1 = _mm_tn_stacked(cat1, dh, name="out_dw_1", col_slots=False)
    dproj_b, dk1, dv1 = _attn_bwd(proj_b, B_Q_OFF, kv1, dcat1, B_IN_PAD, B_Q_OFF, name="attn_bwd_1")
    dproj_b, dxs, dbm, dcm, ddt_c, ddt_r, dpar_row, dpar_col, g["b_gnorm"] = _ssd_bwd(
        xbc, proj_b, dt_c, dt_r, *ssd_par, hprev, dcat1, dproj_b, name="ssd_bwd")
    dpar = dpar_row.at[:, :2].add(jnp.transpose(dpar_col, (0, 2, 1)))
    g["b_dt_bias"], g["b_a_log"], g["b_d"] = dpar[:, 0], dpar[:, 1], dpar[:, 2]
    dproj_b, g["b_conv_w"], g["b_conv_b"] = _conv_bwd(proj_b, p["b_conv_w"], p["b_conv_b"], dxs, dbm, dcm, dproj_b,
                                                      name="conv_bwd")
    ddt = jnp.transpose(ddt_c, (1, 0, 2)) + jnp.transpose(ddt_r, (2, 0, 1))
    ddt = jnp.pad(ddt.reshape(SEQ, SSM_HEADS), ((0, 0), (0, B_IN_PAD - B_DT_OFF - SSM_HEADS))).astype(BF16)
    dproj_b = lax.dynamic_update_slice(dproj_b, ddt, (0, B_DT_OFF))
    dwkv_1, dmn1 = _kv_bwd(mem, row(p["mem_norm"][1]), w_kv1, m1, dk1, dv1, 1)
    dwb = _b_in_grad_slots(_mm_tn(a1, dproj_b, name="b_in_dw"))
    da1 = _mm_nt(dproj_b, b_in, name="b_in_dx")
    dh, dnm1 = _rms_bwd(h2, row(p["norm_mix"][1]), da1, dh, name="mix_norm_bwd_1")
    layer1 = dict(w_kv=dwkv_1, w_out=dwo_1, w_ffn1=dw1_1, w_ffn2=dw2_1, b_in=dwb)
    token = () if after_layer1 is None else (after_layer1(layer1),)

    dh, dnf0, dw1_0, dw2_0 = _ffn_bwd(dh, h1, row(p["norm_ffn"][0]), p["w_ffn1"][0], p["w_ffn2"][0], ffn0, 0,
                                      after=token)
    ffn0_grads = dict(w_ffn1=dw1_0, w_ffn2=dw2_0)
    token = () if after_ffn0 is None else (after_ffn0(ffn0_grads),)
    dcat0 = _mm_nt(dh, p["w_out"][0], name="out_dx_0", after=token)
    dwo_0 = _mm_tn_stacked(cat0, dh, name="out_dw_0", col_slots=False)
    dproj_a, dk0, dv0 = _attn_bwd(proj_a, 2 * D_INNER, kv0, dcat0, A_IN, 2 * D_INNER, name="attn_bwd_0")
    dproj_a, g["a_ln_g"], g["a_ln_b"], g["a_ws"], dbs_col = _gate_bwd(
        proj_a, p["a_ln_g"], p["a_ln_b"], p["a_ws"], bs_col, dcat0, dproj_a, name="gate_bwd")
    g["a_bs"] = dbs_col.reshape(A_GROUPS, CHUNK)
    dwkv_0, dmn0 = _kv_bwd(mem, row(p["mem_norm"][0]), p["w_kv"][0], m0, dk0, dv0, 0)
    dwa = _mm_tn_stacked(a0, dproj_a, name="a_in_dw", col_slots=True)
    mixer0_grads = dict(w_kv=dwkv_0, w_out=dwo_0, a_in=dwa)
    token = () if after_mixer0 is None else (after_mixer0(mixer0_grads),)
    da0 = _mm_nt(dproj_a, p["a_in"], name="a_in_dx", after=token)
    dx, dnm0 = _rms_bwd(h0, row(p["norm_mix"][0]), da0, dh, name="mix_norm_bwd_0")

    g["norm_mix"] = jnp.concatenate([dnm0, dnm1], axis=0)
    g["norm_ffn"] = jnp.concatenate([dnf0, dnf1], axis=0)
    g["mem_norm"] = jnp.concatenate([dmn0, dmn1], axis=0)
    layer0 = dict(w_kv=dwkv_0, w_out=dwo_0, w_ffn1=dw1_0, w_ffn2=dw2_0, a_in=dwa)
    return loss, dx, g, layer0, layer1


def _b_in_full(gathered):
    n = B_IN // N_CHIPS
    dt0 = D_INNER + CONV_DIM - (N_CHIPS - 1) * n
    last = gathered[N_CHIPS - 1]
    return jnp.concatenate([*[gathered[k] for k in range(N_CHIPS - 1)], last[:, :dt0], last[:, dt0 + SSM_HEADS:],
                            last[:, dt0:dt0 + SSM_HEADS], jnp.zeros((D_MODEL, B_IN_PAD - B_IN), last.dtype)], axis=1)


def _b_in_grad_slots(d):
    n = B_IN // N_CHIPS
    dt0 = D_INNER + CONV_DIM
    last = jnp.concatenate([d[:, (N_CHIPS - 1) * n:dt0], d[:, B_DT_OFF:B_DT_OFF + SSM_HEADS], d[:, dt0:B_DT_OFF]], axis=1)
    slots = [*[d[:, k * n:(k + 1) * n] for k in range(N_CHIPS - 1)], last]
    half = D_MODEL // 2
    return jnp.stack([jnp.stack([s[h * half:(h + 1) * half] for s in slots]) for h in range(2)])


LARGE = ("w_kv", "w_out", "w_ffn1", "w_ffn2", "a_in", "b_in")
SMALL_REPL = ("norm_mix", "norm_ffn", "mem_norm", "a_ln_g", "a_ln_b", "a_ws", "a_bs", "b_dt_bias", "b_a_log", "b_d",
              "final_norm")
SMALL_SHARD = ("b_conv_w", "b_conv_b", "b_gnorm")
WEIGHTS = ("norm_mix", "norm_ffn", "mem_norm", "w_kv", "w_out", "w_ffn1", "w_ffn2", "a_in", "a_ln_g", "a_ln_b", "a_ws",
           "a_bs", "b_in", "b_conv_w", "b_conv_b", "b_dt_bias", "b_a_log", "b_d", "b_gnorm", "final_norm")
CONV_SHARD = CONV_DIM // N_CHIPS
GN_SHARD = D_INNER // N_CHIPS


LAYERED = ("w_kv", "w_out", "w_ffn1", "w_ffn2")
LAYER_TENSORS = (("w_kv", "w_out", "w_ffn1", "w_ffn2", "a_in"), ("w_kv", "w_out", "w_ffn1", "w_ffn2", "b_in"))


def _gather_weights(w):
    halves = lambda k, layer: (w[k][layer] if k in LAYERED else w[k][0]).reshape(2, -1, w[k].shape[-1]).astype(BF16)
    small = jnp.zeros((2, CONV_K, CONV_SHARD), F32)
    small = small.at[0].set(w["b_conv_w"][0])
    small = small.at[1, 0].set(w["b_conv_b"][0])
    small = small.at[1, 1, :GN_SHARD].set(w["b_gnorm"][0])
    gathered = _all_gather_shards([halves(k, 0) for k in LAYER_TENSORS[0]], small, name="gather_weights_0")
    got = dict(zip(LAYER_TENSORS[0], gathered))
    slots = lambda a: a.reshape(N_CHIPS, -1, a.shape[-1])
    rows = lambda a: a.reshape(-1, a.shape[-1])
    p = dict(w_kv=[slots(got["w_kv"])], w_out=[rows(got["w_out"])], w_ffn1=[slots(got["w_ffn1"])],
             w_ffn2=[rows(got["w_ffn2"])], a_in=slots(got["a_in"]))
    sm = gathered[-1]
    p["b_conv_w"] = jnp.transpose(sm[:, 0], (1, 0, 2)).reshape(CONV_K, CONV_DIM)
    p["b_conv_b"] = sm[:, 1, 0].reshape(1, CONV_DIM)
    p["b_gnorm"] = sm[:, 1, 1, :GN_SHARD].reshape(1, D_INNER)

    after, started = (gathered[0],), {}
    for tag, names in (("mixer", ("w_kv", "w_out", "b_in")), ("ffn", ("w_ffn1", "w_ffn2"))):
        started[tag] = _gather_start([halves(k, 1) for k in names], after, name=f"gather_start_1_{tag}")
        after = (started[tag][-1],)
    p["after_start"] = after

    def finish(tag, first):
        send_sems, recv_sems, shards, zones, _ = started[tag]
        shards, zones = _gather_wait(send_sems, recv_sems, shards, zones, (first,), name=f"gather_wait_1_{tag}")
        return _gather_finish(shards, zones, name=f"gather_finish_1_{tag}")

    def layer1_mixer(first):
        kv, wo, b_in = finish("mixer", first)
        return slots(kv), rows(wo), _b_in_full(slots(b_in))

    def layer1_ffn(first):
        w1, w2 = finish("ffn", first)
        return slots(w1), rows(w2)

    p.update(layer1_mixer=layer1_mixer, layer1_ffn=layer1_ffn)
    return p


def _pair_parts(grads, tag):
    stacks = [g.reshape(2, -1, g.shape[-1]) for g in grads.values()]
    parts = _pair_reduce(stacks, name=f"grads_pair_reduce_{tag}")
    return [t.reshape(N_CHIPS, -1, t.shape[-1]) for t in parts]


def _chip_sums(chip, names, parts, landed, tag):
    return {k: _sum_contributions(chip, t, u, name=f"grads_chip_sum_{k}_{tag}")
            for k, t, u in zip(names, parts, landed)}


def _small_layout(shapes):
    offs, o = {}, 0
    for k in (*SMALL_REPL, *SMALL_SHARD):
        size = math.prod(shapes[k])
        offs[k] = (o, size)
        o += size
    rows = -(-o // (8 * 128)) * 8
    return offs, rows


def _reduce_small(g, full_shapes):
    offs, rows = _small_layout(full_shapes)
    flat = jnp.concatenate([g[k].reshape(-1) for k in (*SMALL_REPL, *SMALL_SHARD)])
    flat = jnp.pad(flat, (0, rows * 128 - flat.shape[0])).reshape(rows, 128)
    total = _all_reduce_small(flat, name="grads_small_all_reduce").reshape(-1)
    return {k: total[o:o + n].reshape(full_shapes[k]) for k, (o, n) in offs.items()}


def kernel(x, mem, norm_mix, norm_ffn, mem_norm, w_kv, w_out, w_ffn1, w_ffn2, a_in, a_ln_g, a_ln_b, a_ws, a_bs, b_in, b_conv_w, b_conv_b, b_dt_bias, b_a_log, b_d, b_gnorm, final_norm, loss_target, m_norm_mix, m_norm_ffn, m_mem_norm, m_w_kv, m_w_out, m_w_ffn1, m_w_ffn2, m_a_in, m_a_ln_g, m_a_ln_b, m_a_ws, m_a_bs, m_b_in, m_b_conv_w, m_b_conv_b, m_b_dt_bias, m_b_a_log, m_b_d, m_b_gnorm, m_final_norm, v_norm_mix, v_norm_ffn, v_mem_norm, v_w_kv, v_w_out, v_w_ffn1, v_w_ffn2, v_a_in, v_a_ln_g, v_a_ln_b, v_a_ws, v_a_bs, v_b_in, v_b_conv_w, v_b_conv_b, v_b_dt_bias, v_b_a_log, v_b_d, v_b_gnorm, v_final_norm):
    w = dict(norm_mix=norm_mix, norm_ffn=norm_ffn, mem_norm=mem_norm, w_kv=w_kv, w_out=w_out, w_ffn1=w_ffn1,
             w_ffn2=w_ffn2, a_in=a_in, a_ln_g=a_ln_g, a_ln_b=a_ln_b, a_ws=a_ws, a_bs=a_bs, b_in=b_in, b_conv_w=b_conv_w,
             b_conv_b=b_conv_b, b_dt_bias=b_dt_bias, b_a_log=b_a_log, b_d=b_d, b_gnorm=b_gnorm, final_norm=final_norm)
    mom = dict(norm_mix=m_norm_mix, norm_ffn=m_norm_ffn, mem_norm=m_mem_norm, w_kv=m_w_kv, w_out=m_w_out,
               w_ffn1=m_w_ffn1, w_ffn2=m_w_ffn2, a_in=m_a_in, a_ln_g=m_a_ln_g, a_ln_b=m_a_ln_b, a_ws=m_a_ws,
               a_bs=m_a_bs, b_in=m_b_in, b_conv_w=m_b_conv_w, b_conv_b=m_b_conv_b, b_dt_bias=m_b_dt_bias,
               b_a_log=m_b_a_log, b_d=m_b_d, b_gnorm=m_b_gnorm, final_norm=m_final_norm)
    var = dict(norm_mix=v_norm_mix, norm_ffn=v_norm_ffn, mem_norm=v_mem_norm, w_kv=v_w_kv, w_out=v_w_out,
               w_ffn1=v_w_ffn1, w_ffn2=v_w_ffn2, a_in=v_a_in, a_ln_g=v_a_ln_g, a_ln_b=v_a_ln_b, a_ws=v_a_ws,
               a_bs=v_a_bs, b_in=v_b_in, b_conv_w=v_b_conv_w, b_conv_b=v_b_conv_b, b_dt_bias=v_b_dt_bias,
               b_a_log=v_b_a_log, b_d=v_b_d, b_gnorm=v_b_gnorm, final_norm=v_final_norm)

    p = _gather_weights(w)
    p.update(norm_mix=norm_mix, norm_ffn=norm_ffn, mem_norm=mem_norm, a_ln_g=a_ln_g, a_ln_b=a_ln_b, a_ws=a_ws[0],
             a_bs=a_bs[0], b_dt_bias=b_dt_bias, b_a_log=b_a_log, b_d=b_d, final_norm=final_norm)
    chip = 2 * lax.axis_index("x") + lax.axis_index("y")
    chip_arr = jnp.reshape(chip, (1,)).astype(jnp.int32)
    started = {}

    def start_scatter(tag):
        def hook(grads):
            start = _chip_scatter_start(_pair_parts(grads, tag), name=f"grads_chip_scatter_start_{tag}")
            started[tag] = (tuple(grads), start)
            return start[-1]
        return hook

    loss_part, dx, g, _, _ = _local_step(x[0], mem[0], loss_target[0], p, start_scatter("1"), start_scatter("0f"),
                                         start_scatter("0m"))
    loss = lax.psum(loss_part[0, 0], ("x", "y", "c"))

    def finish_scatter(tag, first):
        names, (send_sems, recv_sems, parts, lands, _) = started[tag]
        parts, landed = _chip_scatter_wait(send_sems, recv_sems, parts, lands, (first,),
                                           name=f"grads_chip_scatter_wait_{tag}")
        return _chip_sums(chip_arr, names, parts, landed, tag)

    def adamw(names, grads):
        for k in names:
            shape = w[k].shape
            flat = (lambda a: a.reshape(-1, shape[-1])) if len(shape) > 1 else (lambda a: a.reshape(1, -1))
            d, m_new, v_new = _adamw(flat(w[k]), flat(grads[k]), flat(mom[k]), flat(var[k]), name=f"adamw_{k}")
            delta[k], new_m[k], new_v[k] = d.reshape(shape), m_new.reshape(shape), v_new.reshape(shape)

    full_shapes = {k: w[k].shape for k in SMALL_REPL}
    full_shapes.update(b_conv_w=(1, CONV_K, CONV_DIM), b_conv_b=(1, CONV_DIM), b_gnorm=(1, D_INNER))
    grads = _reduce_small(g, full_shapes)
    grads["b_conv_w"] = lax.dynamic_slice_in_dim(grads["b_conv_w"], chip * CONV_SHARD, CONV_SHARD, axis=2)
    grads["b_conv_b"] = lax.dynamic_slice_in_dim(grads["b_conv_b"], chip * CONV_SHARD, CONV_SHARD, axis=1)
    grads["b_gnorm"] = lax.dynamic_slice_in_dim(grads["b_gnorm"], chip * GN_SHARD, GN_SHARD, axis=1)
    delta, new_m, new_v = {}, {}, {}
    halves = [finish_scatter("0f", dx), finish_scatter("1", dx)]
    early = ("w_ffn1", "w_ffn2", "b_in")
    shared = _pair_share([[halves[layer][k] for layer in range(2) if k in halves[layer]] for k in early],
                         name="grads_pair_share_early")
    grads.update({k: a.reshape(w[k].shape) for k, a in zip(early, shared)})
    adamw([k for k in WEIGHTS if k in grads], grads)
    halves[0].update(finish_scatter("0m", delta["b_in"]))
    late = ("w_kv", "w_out", "a_in")
    shared = _pair_share([[halves[layer][k] for layer in range(2) if k in halves[layer]] for k in late],
                         name="grads_pair_share_late")
    grads.update({k: a.reshape(w[k].shape) for k, a in zip(late, shared)})
    adamw(late, grads)

    return (loss, dx.reshape(x.shape), *[grads[k] for k in WEIGHTS], *[delta[k] for k in WEIGHTS],
            *[new_m[k] for k in WEIGHTS], *[new_v[k] for k in WEIGHTS])
```

```python
import math

import jax
import jax.numpy as jnp
from jax import lax
from jax.experimental import pallas as pl
from jax.experimental.pallas import tpu as pltpu

F32 = jnp.float32
BF16 = jnp.bfloat16
SDS = jax.ShapeDtypeStruct

D_MODEL = 1024
SEQ = 2048
CHUNK = 128
N_MEM = 256
D_INNER = 2048
A_GROUPS = 8
A_GROUP_W = D_INNER // A_GROUPS
SSM_HEADS = 32
SSM_HEAD_DIM = 64
SSM_GROUPS = 4
SSM_HPG = 8
SSM_STATE = 128
SSM_GROUP_W = SSM_HPG * SSM_HEAD_DIM
CONV_K = 4
CONV_DIM = 3072
X_HEADS = 4
X_HEAD_DIM = 256
X_WIDTH = 1024
MIX_OUT = 3072
D_FF = 4096
A_IN = 5120
B_IN = 6176
B_IN_PAD = 6272
B_Q_OFF = 5120
B_DT_OFF = 6144
N_CHUNKS = SEQ // CHUNK
EPS = 1e-6
N_CHIPS = 4

ADAM_LR = 0.001
ADAM_B1 = 0.9
ADAM_B2 = 0.999
ADAM_EPS = 1e-08
ADAM_WD = 0.01
ADAM_STEP = 10

VMEM_LIMIT = 48 * 1024 * 1024
MESH = pl.DeviceIdType.MESH


def _cparams(sem):
    return pltpu.CompilerParams(dimension_semantics=sem, vmem_limit_bytes=VMEM_LIMIT)


def _dot(a, b, dims=(((1,), (0,)), ((), ()))):
    return lax.dot_general(a.astype(BF16), b.astype(BF16), dims, preferred_element_type=F32)


def _dot_nt(a, b):
    return _dot(a, b, (((1,), (1,)), ((), ())))


def _dot_tn(a, b):
    return _dot(a, b, (((0,), (0,)), ((), ())))


def _pick(n, cands):
    for c in cands:
        if n % c == 0:
            return c
    raise ValueError(f"no tile for {n}")


def _mm_call(a, b, *, dims, grid, a_spec, b_spec, acc_shape, out_shapes, out_specs, name,
             extras=(), extra_specs=(), epilogue=None, after=()):
    n_k = grid[2]
    n_extra = len(extras)
    n_out = len(out_shapes)
    n_in = 2 + n_extra + len(after)

    def finish(total, extra_refs, out_refs):
        vals = (total,) if epilogue is None else epilogue(total, *[e[...] for e in extra_refs])
        for o_ref, v in zip(out_refs, vals):
            o_ref[...] = v.astype(o_ref.dtype)

    def body_one_step(*refs):
        finish(_dot(refs[0][...], refs[1][...], dims), refs[2:2 + n_extra], refs[n_in:n_in + n_out])

    def body(*refs):
        acc = refs[-1]
        k = pl.program_id(2)

        @pl.when(k == 0)
        def _():
            acc[...] = jnp.zeros_like(acc)

        acc[...] += _dot(refs[0][...], refs[1][...], dims)

        @pl.when(k == n_k - 1)
        def _():
            finish(acc[...], refs[2:2 + n_extra], refs[n_in:n_in + n_out])

    return pl.pallas_call(
        body_one_step if n_k == 1 else body, grid=grid,
        in_specs=[a_spec, b_spec, *extra_specs, *([ANY] * len(after))], out_specs=list(out_specs),
        out_shape=list(out_shapes), scratch_shapes=[] if n_k == 1 else [pltpu.VMEM(acc_shape, F32)],
        compiler_params=_cparams(("parallel", "parallel", "arbitrary")), name=name,
    )(a, b, *extras, *after)


def _w_dims(w):
    if w.ndim == 2:
        return w.shape[0], w.shape[1], 1, w.shape[1]
    return w.shape[1], w.shape[0] * w.shape[2], w.shape[0], w.shape[2]


def _mm_nn(a, w, *, name, out_dtype=F32, a_cols=None, extras=(), epilogue=None, n_out_dtypes=None, after=()):
    m = a.shape[0]
    k_dim, n_dim, _, n_slot = _w_dims(w)
    a_off, a_w = (0, a.shape[1]) if a_cols is None else a_cols
    assert a_w == k_dim
    tm = _pick(m, (2048, 1024, 512, 256))
    tn = _pick(n_slot, (512, 896, 640, 256, 128))
    tk = _pick(k_dim, (1024, 768, 512, 384, 256, 128))
    assert a_off % tk == 0
    nb = n_slot // tn
    a_spec = pl.BlockSpec((tm, tk), lambda i, j, k: (i, a_off // tk + k))
    if w.ndim == 2:
        b_spec = pl.BlockSpec((tk, tn), lambda i, j, k: (k, j))
    else:
        b_spec = pl.BlockSpec((None, tk, tn), lambda i, j, k: (j // nb, k, j % nb))
    o_spec = pl.BlockSpec((tm, tn), lambda i, j, k: (i, j))
    dts = n_out_dtypes or (out_dtype,)
    outs = _mm_call(a, w, dims=(((1,), (0,)), ((), ())), grid=(m // tm, n_dim // tn, k_dim // tk),
                    a_spec=a_spec, b_spec=b_spec, acc_shape=(tm, tn),
                    out_shapes=[SDS((m, n_dim), dt) for dt in dts], out_specs=[o_spec] * len(dts), name=name,
                    extras=extras, extra_specs=[o_spec] * len(extras), epilogue=epilogue, after=after)
    return outs if n_out_dtypes else outs[0]


def _mm_nt(a, w, *, name, out_dtype=F32, extras=(), epilogue=None, after=()):
    m = a.shape[0]
    k_dim, n_dim, _, n_slot = _w_dims(w)
    assert a.shape[1] == n_dim
    tm = _pick(m, (2048, 1024, 512, 256))
    to = _pick(k_dim, (512, 384, 256, 128))
    tc = _pick(n_slot, (1280, 1024, 896, 640, 512, 256, 128))
    nb = n_slot // tc
    a_spec = pl.BlockSpec((tm, tc), lambda i, j, k: (i, k))
    if w.ndim == 2:
        b_spec = pl.BlockSpec((to, tc), lambda i, j, k: (j, k))
    else:
        b_spec = pl.BlockSpec((None, to, tc), lambda i, j, k: (k // nb, j, k % nb))
    o_spec = pl.BlockSpec((tm, to), lambda i, j, k: (i, j))
    return _mm_call(a, w, dims=(((1,), (1,)), ((), ())), grid=(m // tm, k_dim // to, n_dim // tc),
                    a_spec=a_spec, b_spec=b_spec, acc_shape=(tm, to),
                    out_shapes=[SDS((m, k_dim), out_dtype)], out_specs=[o_spec], name=name,
                    extras=extras, extra_specs=[o_spec] * len(extras), epilogue=epilogue, after=after)[0]


def _mm_tn(x, dy, *, name, x_cols=None):
    s = x.shape[0]
    x_off, k_dim = (0, x.shape[1]) if x_cols is None else x_cols
    n_dim = dy.shape[1]
    tm = _pick(k_dim, (1024, 768, 512, 384, 256, 128))
    tn = _pick(n_dim, (512, 896, 640, 256, 128))
    tk = _pick(s, (2048, 1024, 512, 256))
    assert x_off % tm == 0
    a_spec = pl.BlockSpec((tk, tm), lambda i, j, k: (k, x_off // tm + i))
    b_spec = pl.BlockSpec((tk, tn), lambda i, j, k: (k, j))
    o_spec = pl.BlockSpec((tm, tn), lambda i, j, k: (i, j))
    return _mm_call(x, dy, dims=(((0,), (0,)), ((), ())), grid=(k_dim // tm, n_dim // tn, s // tk),
                    a_spec=a_spec, b_spec=b_spec, acc_shape=(tm, tn),
                    out_shapes=[SDS((k_dim, n_dim), F32)], out_specs=[o_spec], name=name)[0]


def _mm_tn_stacked(x, dy, *, name, col_slots):
    s, k_dim = x.shape
    n_dim = dy.shape[1]
    r, c = (k_dim // 2, n_dim // N_CHIPS) if col_slots else (k_dim // N_CHIPS // 2, n_dim)
    tm = 2 * r
    tn = _pick(c, (512, 896, 640, 256, 128))
    tk = _pick(s, (2048, 1024, 512, 256))
    a_spec = pl.BlockSpec((tk, tm), lambda i, j, k: (k, i))
    b_spec = pl.BlockSpec((tk, tn), lambda i, j, k: (k, j))
    if col_slots:
        nb = c // tn
        o_spec = pl.BlockSpec((2, None, r, tn), lambda i, j, k: (0, j // nb, 0, j % nb))
    else:
        o_spec = pl.BlockSpec((2, None, r, tn), lambda i, j, k: (0, i, 0, j))
    return _mm_call(x, dy, dims=(((0,), (0,)), ((), ())), grid=(k_dim // tm, n_dim // tn, s // tk),
                    a_spec=a_spec, b_spec=b_spec, acc_shape=(tm, tn), epilogue=lambda acc: (acc.reshape(2, r, tn),),
                    out_shapes=[SDS((2, N_CHIPS, r, c), F32)], out_specs=[o_spec], name=name)[0]


def _rms(x, g):
    return x * lax.rsqrt(jnp.mean(x * x, axis=-1, keepdims=True) + EPS) * g


def _rms_fwd(h, g, *, name):
    rows, d = h.shape
    tr = _pick(rows, (512, 256))

    def body(h_ref, g_ref, o_ref):
        o_ref[...] = _rms(h_ref[...], g_ref[...]).astype(o_ref.dtype)

    return pl.pallas_call(
        body, grid=(rows // tr,),
        in_specs=[pl.BlockSpec((tr, d), lambda i: (i, 0)), pl.BlockSpec((1, d), lambda i: (0, 0))],
        out_specs=pl.BlockSpec((tr, d), lambda i: (i, 0)), out_shape=SDS((rows, d), BF16),
        compiler_params=_cparams(("parallel",)), name=name)(h, g)


def _rms_bwd(h, g, da, dres, *, name):
    rows, d = h.shape
    tr = _pick(rows, (512, 256))

    def body(h_ref, g_ref, da_ref, dres_ref, dh_ref, dg_ref):
        _, vjp = jax.vjp(_rms, h_ref[...], g_ref[...])
        dh, dg = vjp(da_ref[...].astype(F32))
        dh_ref[...] = dres_ref[...] + dh

        @pl.when(pl.program_id(0) == 0)
        def _():
            dg_ref[...] = jnp.zeros_like(dg_ref)

        dg_ref[...] += dg

    row_spec = pl.BlockSpec((tr, d), lambda i: (i, 0))
    vec_spec = pl.BlockSpec((1, d), lambda i: (0, 0))
    return pl.pallas_call(
        body, grid=(rows // tr,), in_specs=[row_spec, vec_spec, row_spec, row_spec],
        out_specs=[row_spec, vec_spec], out_shape=[SDS((rows, d), F32), SDS((1, d), F32)],
        compiler_params=_cparams(("arbitrary",)), name=name)(h, g, da, dres)


def _loss_head(h, g, target, *, name):
    rows, d = h.shape
    tr = _pick(rows, (512, 256))

    def body(h_ref, g_ref, t_ref, loss_ref, dh_ref, dg_ref):
        y, vjp = jax.vjp(_rms, h_ref[...], g_ref[...])
        err = y - t_ref[...]
        dh, dg = vjp(err * (1.0 / d))
        dh_ref[...] = dh

        @pl.when(pl.program_id(0) == 0)
        def _():
            dg_ref[...] = jnp.zeros_like(dg_ref)
            loss_ref[...] = jnp.zeros_like(loss_ref)

        dg_ref[...] += dg
        part = jnp.sum(jnp.sum(err * err, axis=-1, keepdims=True), axis=0, keepdims=True) * (0.5 / d)
        loss_ref[...] += jnp.broadcast_to(part, loss_ref.shape)

    row_spec = pl.BlockSpec((tr, d), lambda i: (i, 0))
    vec_spec = pl.BlockSpec((1, d), lambda i: (0, 0))
    loss_spec = pl.BlockSpec((8, 128), lambda i: (0, 0))
    return pl.pallas_call(
        body, grid=(rows // tr,), in_specs=[row_spec, vec_spec, row_spec],
        out_specs=[loss_spec, row_spec, vec_spec],
        out_shape=[SDS((8, 128), F32), SDS((rows, d), F32), SDS((1, d), F32)],
        compiler_params=_cparams(("arbitrary",)), name=name)(h, g, target)


def _gelu(x):
    return 0.5 * x * (1.0 + lax.erf(x * (1.0 / math.sqrt(2.0))))


def _gate_tile(pu, pv, ln_g, ln_b, ws, bs_t):
    u = [_gelu(p) for p in pu]
    v = [_gelu(p) for p in pv]
    mu = sum(jnp.sum(t, axis=-1, keepdims=True) for t in v) * (1.0 / D_INNER)
    vc = [t - mu for t in v]
    var = sum(jnp.sum(t * t, axis=-1, keepdims=True) for t in vc) * (1.0 / D_INNER)
    rstd = lax.rsqrt(var + EPS)
    row = lax.broadcasted_iota(jnp.int32, (CHUNK, CHUNK), 0)
    col = lax.broadcasted_iota(jnp.int32, (CHUNK, CHUNK), 1)
    out = []
    for gi in range(A_GROUPS):
        vn = vc[gi] * rstd * ln_g[gi] + ln_b[gi]
        w = jnp.where(row >= col, ws[gi], 0.0)
        sv = _dot(w, vn) + bs_t[gi]
        out.append(u[gi] * sv)
    return out


def _split(ref, n, width):
    return [ref[:, i * width:(i + 1) * width] for i in range(n)]


def _gate_in_specs():
    return [
        pl.BlockSpec((CHUNK, D_INNER), lambda c: (c, 0)),
        pl.BlockSpec((CHUNK, D_INNER), lambda c: (c, 1)),
        pl.BlockSpec((1, D_INNER), lambda c: (0, 0)),
        pl.BlockSpec((1, D_INNER), lambda c: (0, 0)),
        pl.BlockSpec((A_GROUPS, CHUNK, CHUNK), lambda c: (0, 0, 0)),
        pl.BlockSpec((A_GROUPS, CHUNK, 1), lambda c: (0, 0, 0)),
    ]


def _gate_args(u_ref, v_ref, g_ref, b_ref, ws_ref, bs_ref):
    ng, gw = A_GROUPS, A_GROUP_W
    return (_split(u_ref, ng, gw), _split(v_ref, ng, gw), _split(g_ref, ng, gw), _split(b_ref, ng, gw),
            [ws_ref[i] for i in range(ng)], [bs_ref[i] for i in range(ng)])


def _gate_fwd(proj, ln_g, ln_b, ws, bs_col, mixcat, *, name):
    def body(u_ref, v_ref, g_ref, b_ref, ws_ref, bs_ref, cat_in, cat_ref):
        del cat_in
        out = _gate_tile(*_gate_args(u_ref, v_ref, g_ref, b_ref, ws_ref, bs_ref))
        for gi, o in enumerate(out):
            cat_ref[:, gi * A_GROUP_W:(gi + 1) * A_GROUP_W] = o.astype(cat_ref.dtype)

    return pl.pallas_call(
        body, grid=(N_CHUNKS,), in_specs=[*_gate_in_specs(), pl.BlockSpec(memory_space=pl.ANY)],
        out_specs=pl.BlockSpec((CHUNK, D_INNER), lambda c: (c, 0)), out_shape=SDS(mixcat.shape, mixcat.dtype),
        input_output_aliases={6: 0}, compiler_params=_cparams(("parallel",)), name=name,
    )(proj, proj, ln_g, ln_b, ws, bs_col, mixcat)


def _gate_bwd(proj, ln_g, ln_b, ws, bs_col, dcat, dproj, *, name):
    ng, gw = A_GROUPS, A_GROUP_W

    def body(u_ref, v_ref, g_ref, b_ref, ws_ref, bs_ref, d_ref, dproj_in, dproj_ref, dg_ref, db_ref, dws_ref, dbs_ref):
        del dproj_in
        args = _gate_args(u_ref, v_ref, g_ref, b_ref, ws_ref, bs_ref)
        _, vjp = jax.vjp(_gate_tile, *args)
        dpu, dpv, dg, db, dws, dbs = vjp(_split(d_ref, ng, gw))
        for gi in range(ng):
            dproj_ref[:, gi * gw:(gi + 1) * gw] = dpu[gi].astype(dproj_ref.dtype)
            dproj_ref[:, D_INNER + gi * gw:D_INNER + (gi + 1) * gw] = dpv[gi].astype(dproj_ref.dtype)

        @pl.when(pl.program_id(0) == 0)
        def _():
            for r in (dg_ref, db_ref, dws_ref, dbs_ref):
                r[...] = jnp.zeros_like(r)

        for gi in range(ng):
            dg_ref[:, gi * gw:(gi + 1) * gw] += dg[gi]
            db_ref[:, gi * gw:(gi + 1) * gw] += db[gi]
            dws_ref[gi] += dws[gi]
            dbs_ref[gi] += dbs[gi]

    in_specs = _gate_in_specs()
    return pl.pallas_call(
        body, grid=(N_CHUNKS,),
        in_specs=[*in_specs, pl.BlockSpec((CHUNK, D_INNER), lambda c: (c, 0)), pl.BlockSpec(memory_space=pl.ANY)],
        out_specs=[pl.BlockSpec((CHUNK, 2 * D_INNER), lambda c: (c, 0)), *in_specs[2:]],
        out_shape=[SDS(dproj.shape, dproj.dtype), SDS((1, D_INNER), F32), SDS((1, D_INNER), F32),
                   SDS((ng, CHUNK, CHUNK), F32), SDS((ng, CHUNK, 1), F32)],
        input_output_aliases={7: 0}, compiler_params=_cparams(("arbitrary",)), name=name,
    )(proj, proj, ln_g, ln_b, ws, bs_col, dcat, dproj)


ATT_TQ = 512


def _attn_tile(q, k, v):
    s = _dot_nt(q, k) * (1.0 / math.sqrt(X_HEAD_DIM))
    s = s - jnp.max(s, axis=-1, keepdims=True)
    e = jnp.exp(s)
    p = e / jnp.sum(e, axis=-1, keepdims=True)
    return _dot(p, v)


def _attn_in_specs(q_blk, order):
    hd = X_HEAD_DIM
    return [
        pl.BlockSpec((ATT_TQ, hd), lambda a, b: (order(a, b)[0], q_blk + order(a, b)[1])),
        pl.BlockSpec((N_MEM, hd), lambda a, b: (0, order(a, b)[1])),
        pl.BlockSpec((N_MEM, hd), lambda a, b: (0, X_HEADS + order(a, b)[1])),
    ]


def _attn_fwd(proj, q_off, kv, *, name):
    order = lambda i, h: (i, h)
    cat_blk = D_INNER // X_HEAD_DIM

    def body(q_ref, k_ref, v_ref, o_ref):
        o_ref[...] = _attn_tile(q_ref[...], k_ref[...], v_ref[...]).astype(o_ref.dtype)

    return pl.pallas_call(
        body, grid=(SEQ // ATT_TQ, X_HEADS), in_specs=_attn_in_specs(q_off // X_HEAD_DIM, order),
        out_specs=pl.BlockSpec((ATT_TQ, X_HEAD_DIM), lambda i, h: (i, cat_blk + h)),
        out_shape=SDS((SEQ, MIX_OUT), BF16), compiler_params=_cparams(("parallel", "parallel")), name=name,
    )(proj, kv, kv)


def _attn_bwd(proj, q_off, kv, dcat, dproj_width, dq_off, *, name):
    order = lambda h, i: (i, h)
    cat_blk = D_INNER // X_HEAD_DIM
    dq_blk = dq_off // X_HEAD_DIM

    def body(q_ref, k_ref, v_ref, do_ref, dq_ref, dk_ref, dv_ref):
        _, vjp = jax.vjp(_attn_tile, q_ref[...], k_ref[...], v_ref[...])
        dq, dk, dv = vjp(do_ref[...])
        dq_ref[...] = dq.astype(dq_ref.dtype)

        @pl.when(pl.program_id(1) == 0)
        def _():
            dk_ref[...] = jnp.zeros_like(dk_ref)
            dv_ref[...] = jnp.zeros_like(dv_ref)

        dk_ref[...] += dk
        dv_ref[...] += dv

    kv_spec = pl.BlockSpec((N_MEM, X_HEAD_DIM), lambda h, i: (0, h))
    return pl.pallas_call(
        body, grid=(X_HEADS, SEQ // ATT_TQ),
        in_specs=[*_attn_in_specs(q_off // X_HEAD_DIM, order),
                  pl.BlockSpec((ATT_TQ, X_HEAD_DIM), lambda h, i: (i, cat_blk + h))],
        out_specs=[pl.BlockSpec((ATT_TQ, X_HEAD_DIM), lambda h, i: (i, dq_blk + h)), kv_spec, kv_spec],
        out_shape=[SDS((SEQ, dproj_width), BF16), SDS((N_MEM, X_WIDTH), F32), SDS((N_MEM, X_WIDTH), F32)],
        compiler_params=_cparams(("parallel", "arbitrary")), name=name,
    )(proj, kv, kv, dcat)


CONV_TC = 512


def _shift_down(x, s):
    if s == 0:
        return x
    row = lax.broadcasted_iota(jnp.int32, x.shape, 0)
    return jnp.where(row >= s, pltpu.roll(x, s, 0), 0.0)


def _shift_up(x, s):
    if s == 0:
        return x
    n = x.shape[0]
    row = lax.broadcasted_iota(jnp.int32, x.shape, 0)
    return jnp.where(row < n - s, pltpu.roll(x, n - s, 0), 0.0)


def _conv_pre(x, w_ref, b_ref):
    pre = b_ref[...] + jnp.zeros_like(x)
    for k in range(CONV_K):
        pre = pre + w_ref[k:k + 1, :] * _shift_down(x, CONV_K - 1 - k)
    return pre


def _conv_fwd(proj, w, b, *, name):
    blk0 = D_INNER // CONV_TC

    def body(x_ref, w_ref, b_ref, o_ref):
        pre = _conv_pre(x_ref[...], w_ref, b_ref)
        o_ref[...] = pre * jax.nn.sigmoid(pre)

    return pl.pallas_call(
        body, grid=(CONV_DIM // CONV_TC,),
        in_specs=[pl.BlockSpec((SEQ, CONV_TC), lambda j: (0, blk0 + j)), pl.BlockSpec((CONV_K, CONV_TC), lambda j: (0, j)),
                  pl.BlockSpec((1, CONV_TC), lambda j: (0, j))],
        out_specs=pl.BlockSpec((SEQ, CONV_TC), lambda j: (0, j)), out_shape=SDS((SEQ, CONV_DIM), F32),
        compiler_params=_cparams(("parallel",)), name=name)(proj, w, b)


def _conv_bwd(proj, w, b, dxs, dbm, dcm, dproj, *, name):
    tc = CONV_TC // 2
    blk0 = D_INNER // tc
    n_x = D_INNER // tc
    n_b = SSM_GROUPS * SSM_STATE // tc

    def body(x_ref, w_ref, b_ref, dxs_ref, dbm_ref, dcm_ref, dproj_in, dproj_ref, dw_ref, db_ref):
        del dproj_in
        j = pl.program_id(0)
        x = x_ref[...]
        pre = _conv_pre(x, w_ref, b_ref)
        sg = jax.nn.sigmoid(pre)
        dact = jnp.where(j < n_x, dxs_ref[...], jnp.where(j < n_x + n_b, dbm_ref[...], dcm_ref[...]))
        dpre = dact * (sg * (1.0 + pre * (1.0 - sg)))
        dx = jnp.zeros_like(x)
        for k in range(CONV_K):
            s = CONV_K - 1 - k
            dx = dx + w_ref[k:k + 1, :] * _shift_up(dpre, s)
            dw_ref[k:k + 1, :] = jnp.sum(dpre * _shift_down(x, s), axis=0, keepdims=True)
        dproj_ref[...] = dx.astype(dproj_ref.dtype)
        db_ref[...] = jnp.sum(dpre, axis=0, keepdims=True)

    clip = lambda v, hi: jnp.minimum(jnp.maximum(v, 0), hi)
    return pl.pallas_call(
        body, grid=(CONV_DIM // tc,),
        in_specs=[pl.BlockSpec((SEQ, tc), lambda j: (0, blk0 + j)), pl.BlockSpec((CONV_K, tc), lambda j: (0, j)),
                  pl.BlockSpec((1, tc), lambda j: (0, j)),
                  pl.BlockSpec((SEQ, tc), lambda j: (0, clip(j, n_x - 1))),
                  pl.BlockSpec((SEQ, tc), lambda j: (0, clip(j - n_x, n_b - 1))),
                  pl.BlockSpec((SEQ, tc), lambda j: (0, clip(j - n_x - n_b, n_b - 1))),
                  pl.BlockSpec(memory_space=pl.ANY)],
        out_specs=[pl.BlockSpec((SEQ, tc), lambda j: (0, blk0 + j)), pl.BlockSpec((CONV_K, tc), lambda j: (0, j)),
                   pl.BlockSpec((1, tc), lambda j: (0, j))],
        out_shape=[SDS(dproj.shape, dproj.dtype), SDS((CONV_K, CONV_DIM), F32), SDS((1, CONV_DIM), F32)],
        input_output_aliases={6: 0}, compiler_params=_cparams(("parallel",)), name=name,
    )(proj, w, b, dxs, dbm, dcm, dproj)


SSM_PAIRS = SSM_HPG // 2


def _dot_exact01(x, m01, m01_t, x_first, differentiable):
    def product(v, m):
        hi = v.astype(BF16)
        rest = v - hi.astype(F32)
        mid = rest.astype(BF16)
        lo = (rest - mid.astype(F32)).astype(BF16)
        dims = (((1,), (0,)), ((), ()))
        dot = lambda part: lax.dot_general(*((part, m) if x_first else (m, part)), dims, preferred_element_type=F32)
        return dot(hi) + dot(mid) + dot(lo)

    if not differentiable:
        return product(x, m01)

    @jax.custom_vjp
    def exact(v):
        return product(v, m01)

    exact.defvjp(lambda v: (product(v, m01), None), lambda _, ct: (product(ct, m01_t),))
    return exact(x)


def _ssd_tile(xp, zp, bm, cm, hp, dt_c, dt_r, bias, bias_col, alog, alog_col, dsk, gnp, differentiable=False):
    row = lax.broadcasted_iota(jnp.int32, (CHUNK, CHUNK), 0)
    col = lax.broadcasted_iota(jnp.int32, (CHUNK, CHUNK), 1)
    causal = row >= col
    left = col < SSM_HEAD_DIM
    top = row < SSM_HEAD_DIM
    ones = jnp.ones((CHUNK, CHUNK), BF16)
    cb = _dot_nt(cm, bm)
    dtp = jax.nn.softplus(dt_c + bias)
    da_c = dtp * -jnp.exp(alog)
    da_r = jax.nn.softplus(dt_r + bias_col) * -jnp.exp(alog_col)
    lower = jnp.where(causal, 1.0, 0.0).astype(BF16)
    upper = jnp.where(row <= col, 1.0, 0.0).astype(BF16)
    cs = _dot_exact01(da_c, lower, upper, False, differentiable)
    cs_rows = _dot_exact01(da_r, upper, lower, True, differentiable)
    cs_last = jnp.sum(da_c, axis=0, keepdims=True)
    ecs, decay, ecl = jnp.exp(cs), jnp.exp(cs_last - cs), jnp.exp(cs_last)
    m = [cb * jnp.exp(jnp.where(causal, cs[:, r:r + 1] - cs_rows[r:r + 1, :], -1e30)) for r in range(SSM_HPG)]
    ygs, hn = [], []
    for p in range(SSM_PAIRS):
        a, b = 2 * p, 2 * p + 1
        pair = lambda v: jnp.where(left, v[:, a:a + 1], v[:, b:b + 1])
        xdt = xp[p] * pair(dtp)
        y = jnp.where(left, _dot(m[a], xdt), _dot(m[b], xdt))
        y = y + _dot_nt(cm, hp[p]) * pair(ecs)
        y = y + xp[p] * pair(dsk)
        states = _dot_tn(xdt * pair(decay), bm)
        hn.append(hp[p] * jnp.where(top, ecl[:, a:a + 1], ecl[:, b:b + 1]) + states)
        ygs.append(y * (zp[p] * jax.nn.sigmoid(zp[p])))
    ms = sum(_dot(t * t, ones) for t in ygs) * (1.0 / SSM_GROUP_W)
    rs = lax.rsqrt(ms + EPS)
    return [ygs[p] * rs * gnp[p] for p in range(SSM_PAIRS)], hn


def _ssd_in_specs(cidx):
    gw, n = SSM_GROUP_W, SSM_STATE
    bm_blk = D_INNER // n
    return [
        pl.BlockSpec((CHUNK, gw), lambda g, c: (cidx(c), g)),
        pl.BlockSpec((CHUNK, gw), lambda g, c: (cidx(c), g)),
        pl.BlockSpec((CHUNK, n), lambda g, c: (cidx(c), bm_blk + g)),
        pl.BlockSpec((CHUNK, n), lambda g, c: (cidx(c), bm_blk + SSM_GROUPS + g)),
        pl.BlockSpec((None, CHUNK, SSM_HPG), lambda g, c: (g, cidx(c), 0)),
        pl.BlockSpec((None, SSM_HPG, CHUNK), lambda g, c: (g, 0, cidx(c))),
        pl.BlockSpec((None, 3, SSM_HPG), lambda g, c: (g, 0, 0)),
        pl.BlockSpec((None, SSM_HPG, 2), lambda g, c: (g, 0, 0)),
        pl.BlockSpec((1, gw), lambda g, c: (0, g)),
    ]


def _ssd_args(x_ref, z_ref, bm_ref, cm_ref, hp, dtc_ref, dtr_ref, prow_ref, pcol_ref, gn_ref):
    npair, w = SSM_PAIRS, 2 * SSM_HEAD_DIM
    return (_split(x_ref, npair, w), _split(z_ref, npair, w), bm_ref[...], cm_ref[...], hp, dtc_ref[...], dtr_ref[...],
            prow_ref[0:1, :], pcol_ref[:, 0:1], prow_ref[1:2, :], pcol_ref[:, 1:2], prow_ref[2:3, :],
            _split(gn_ref, npair, w))


def _pair_rows(ref):
    w = 2 * SSM_HEAD_DIM
    return [ref[p * w:(p + 1) * w, :] for p in range(SSM_PAIRS)]


def _ssd_fwd(xbc, proj, dt_c, dt_r, par_row, par_col, gn, mixcat, *, name):
    w = 2 * SSM_HEAD_DIM

    def body(x_ref, z_ref, bm_ref, cm_ref, dtc_ref, dtr_ref, prow_ref, pcol_ref, gn_ref, cat_in,
             cat_ref, hprev_ref, h_scr):
        del cat_in

        @pl.when(pl.program_id(1) == 0)
        def _():
            h_scr[...] = jnp.zeros_like(h_scr)

        hprev_ref[...] = h_scr[...]
        yn, hn = _ssd_tile(*_ssd_args(x_ref, z_ref, bm_ref, cm_ref, _pair_rows(h_scr), dtc_ref, dtr_ref, prow_ref,
                                      pcol_ref, gn_ref))
        for p in range(SSM_PAIRS):
            cat_ref[:, p * w:(p + 1) * w] = yn[p].astype(cat_ref.dtype)
            h_scr[p * w:(p + 1) * w, :] = hn[p]

    return pl.pallas_call(
        body, grid=(SSM_GROUPS, N_CHUNKS), in_specs=[*_ssd_in_specs(lambda c: c), pl.BlockSpec(memory_space=pl.ANY)],
        out_specs=[pl.BlockSpec((CHUNK, SSM_GROUP_W), lambda g, c: (c, g)),
                   pl.BlockSpec((None, None, SSM_GROUP_W, SSM_STATE), lambda g, c: (c, g, 0, 0))],
        out_shape=[SDS(mixcat.shape, mixcat.dtype), SDS((N_CHUNKS, SSM_GROUPS, SSM_GROUP_W, SSM_STATE), F32)],
        scratch_shapes=[pltpu.VMEM((SSM_GROUP_W, SSM_STATE), F32)],
        input_output_aliases={9: 0}, compiler_params=_cparams(("parallel", "arbitrary")), name=name,
    )(xbc, proj, xbc, xbc, dt_c, dt_r, par_row, par_col, gn, mixcat)


def _ssd_bwd(xbc, proj, dt_c, dt_r, par_row, par_col, gn, hprev, dcat, dproj, *, name):
    nh, w, gw, n = SSM_HPG, 2 * SSM_HEAD_DIM, SSM_GROUP_W, SSM_STATE
    rev = lambda c: N_CHUNKS - 1 - c

    def body(x_ref, z_ref, bm_ref, cm_ref, dtc_ref, dtr_ref, prow_ref, pcol_ref, gn_ref, hprev_ref, dy_ref,
             dproj_in, dz_ref, dxs_ref, dbm_ref, dcm_ref, ddtc_ref, ddtr_ref, dprow_ref, dpcol_ref, dgn_ref, dh_scr):
        del dproj_in
        first = pl.program_id(1) == 0

        @pl.when(first)
        def _():
            dh_scr[...] = jnp.zeros_like(dh_scr)
            for ref in (dprow_ref, dpcol_ref, dgn_ref):
                ref[...] = jnp.zeros_like(ref)

        args = _ssd_args(x_ref, z_ref, bm_ref, cm_ref, _pair_rows(hprev_ref), dtc_ref, dtr_ref, prow_ref, pcol_ref,
                         gn_ref)
        _, vjp = jax.vjp(lambda *a: _ssd_tile(*a, differentiable=True), *args)
        dxs, dzs, dbm, dcm, dhs, ddtc, ddtr, dbias, dbias_col, dalog, dalog_col, ddsk, dgn = vjp(
            (_split(dy_ref, SSM_PAIRS, w), _pair_rows(dh_scr)))
        dbm_ref[...] = dbm
        dcm_ref[...] = dcm
        ddtc_ref[...] = ddtc
        ddtr_ref[...] = ddtr
        for q in range(SSM_PAIRS):
            dxs_ref[:, q * w:(q + 1) * w] = dxs[q]
            dz_ref[:, q * w:(q + 1) * w] = dzs[q].astype(dz_ref.dtype)
            dh_scr[q * w:(q + 1) * w, :] = dhs[q]
            dgn_ref[:, q * w:(q + 1) * w] += dgn[q]
        for i, d in enumerate((dbias, dalog, ddsk)):
            dprow_ref[i:i + 1, :] += d
        for i, d in enumerate((dbias_col, dalog_col)):
            dpcol_ref[:, i:i + 1] += d

    return pl.pallas_call(
        body, grid=(SSM_GROUPS, N_CHUNKS),
        in_specs=[*_ssd_in_specs(rev),
                  pl.BlockSpec((None, None, gw, n), lambda g, c: (rev(c), g, 0, 0)),
                  pl.BlockSpec((CHUNK, gw), lambda g, c: (rev(c), g)),
                  pl.BlockSpec(memory_space=pl.ANY)],
        out_specs=[pl.BlockSpec((CHUNK, gw), lambda g, c: (rev(c), g)),
                   pl.BlockSpec((CHUNK, gw), lambda g, c: (rev(c), g)),
                   pl.BlockSpec((CHUNK, n), lambda g, c: (rev(c), g)),
                   pl.BlockSpec((CHUNK, n), lambda g, c: (rev(c), g)),
                   pl.BlockSpec((None, CHUNK, nh), lambda g, c: (g, rev(c), 0)),
                   pl.BlockSpec((None, nh, CHUNK), lambda g, c: (g, 0, rev(c))),
                   pl.BlockSpec((None, 3, nh), lambda g, c: (g, 0, 0)),
                   pl.BlockSpec((None, nh, 2), lambda g, c: (g, 0, 0)),
                   pl.BlockSpec((1, gw), lambda g, c: (0, g))],
        out_shape=[SDS(dproj.shape, dproj.dtype), SDS((SEQ, D_INNER), F32), SDS((SEQ, SSM_GROUPS * n), F32),
                   SDS((SEQ, SSM_GROUPS * n), F32), SDS((SSM_GROUPS, SEQ, nh), F32), SDS((SSM_GROUPS, nh, SEQ), F32),
                   SDS((SSM_GROUPS, 3, nh), F32), SDS((SSM_GROUPS, nh, 2), F32), SDS((1, D_INNER), F32)],
        scratch_shapes=[pltpu.VMEM((gw, n), F32)],
        input_output_aliases={11: 0}, compiler_params=_cparams(("parallel", "arbitrary")), name=name,
    )(xbc, proj, xbc, xbc, dt_c, dt_r, par_row, par_col, gn, hprev, dcat, dproj)


def _sum_contributions(chip, parts, landed, *, name):
    _, r, c = parts.shape
    tr = _pick(r, (256, 384, 128))

    def body(chip_ref, own_ref, landed_ref, o_ref):
        del chip_ref
        acc = own_ref[...].astype(F32)
        for s in range(landed_ref.shape[0]):
            acc = acc + landed_ref[s].astype(F32)
        o_ref[...] = acc

    grid_spec = pltpu.PrefetchScalarGridSpec(
        num_scalar_prefetch=1, grid=(r // tr,),
        in_specs=[pl.BlockSpec((None, tr, c), lambda i, chip_ref: (chip_ref[0], i, 0)),
                  pl.BlockSpec((landed.shape[0], tr, c), lambda i, chip_ref: (0, i, 0))],
        out_specs=pl.BlockSpec((tr, c), lambda i, chip_ref: (i, 0)))
    return pl.pallas_call(body, grid_spec=grid_spec, out_shape=SDS((r, c), F32),
                          compiler_params=_cparams(("parallel",)), name=name)(chip, parts, landed)


def _adamw(w, g, m, v, *, name):
    r, c = w.shape
    tr = r if r <= 256 else _pick(r, (256, 128, 8))
    spec = pl.BlockSpec((tr, c), lambda i: (i, 0))

    def body(w_ref, g_ref, m_ref, v_ref, d_ref, mo_ref, vo_ref):
        g = g_ref[...]
        m_new = ADAM_B1 * m_ref[...] + (1.0 - ADAM_B1) * g
        v_new = ADAM_B2 * v_ref[...] + (1.0 - ADAM_B2) * (g * g)
        m_hat = m_new / (1.0 - ADAM_B1 ** ADAM_STEP)
        v_hat = v_new / (1.0 - ADAM_B2 ** ADAM_STEP)
        d_ref[...] = -ADAM_LR * (m_hat / (jnp.sqrt(v_hat) + ADAM_EPS) + ADAM_WD * w_ref[...])
        mo_ref[...] = m_new
        vo_ref[...] = v_new

    return pl.pallas_call(body, grid=(r // tr,), in_specs=[spec] * 4, out_specs=[spec] * 3,
                          out_shape=[SDS((r, c), F32)] * 3, compiler_params=_cparams(("parallel",)), name=name)(w, g, m, v)


ANY = pl.BlockSpec(memory_space=pl.ANY)


def _place():
    x, y, c = lax.axis_index("x"), lax.axis_index("y"), lax.axis_index("c")
    chips = [(1 - x, y), (x, 1 - y), (1 - x, 1 - y)]
    return x, y, c, chips


def _remote(src, dst, send_sem, recv_sem, to):
    return pltpu.make_async_remote_copy(src_ref=src, dst_ref=dst, send_sem=send_sem, recv_sem=recv_sem,
                                        device_id=to, device_id_type=MESH)


STREAM_ROWS = 256


def _stream_rows(i):
    return pl.ds(pl.multiple_of(i * STREAM_ROWS, STREAM_ROWS), STREAM_ROWS)


def _channel_scratch(width, dtype, rows=STREAM_ROWS):
    buf = (2, rows, width)
    return [pltpu.VMEM(buf, dtype), pltpu.VMEM(buf, dtype), *([pltpu.SemaphoreType.DMA((2,))] * 5),
            pltpu.SemaphoreType.REGULAR((2,))]


CHANNEL_REFS = 8


def _copy_blocks(srcs, dsts, ch):
    sbuf, _, ld, _, _, st, _, _ = ch
    n = len(srcs)
    load = lambda i: pltpu.make_async_copy(srcs[i], sbuf.at[i % 2], ld.at[i % 2])
    store = lambda i: pltpu.make_async_copy(sbuf.at[i % 2], dsts[i], st.at[i % 2])
    load(0).start()
    for i in range(n):
        if i + 1 < n:
            if i >= 1:
                store(i - 1).wait()
            load(i + 1).start()
        load(i).wait()
        store(i).start()
    for i in range(max(0, n - 2), n):
        store(i).wait()


def _exchange_block_streams(streams, sibling):
    plans = []
    for srcs, dsts, keeps, (sbuf, rbuf, ld, snd, rcv, st, kp, credit) in streams:
        n = len(srcs)

        def load(i, srcs=srcs, sbuf=sbuf, ld=ld):
            return pltpu.make_async_copy(srcs[i], sbuf.at[i % 2], ld.at[i % 2])

        def push(i, sbuf=sbuf, rbuf=rbuf, snd=snd, rcv=rcv):
            return _remote(sbuf.at[i % 2], rbuf.at[i % 2], snd.at[i % 2], rcv.at[i % 2], sibling)

        def store(i, rbuf=rbuf, dsts=dsts, st=st):
            return pltpu.make_async_copy(rbuf.at[i % 2], dsts[i], st.at[i % 2])

        def save(i, sbuf=sbuf, keeps=keeps, kp=kp):
            return pltpu.make_async_copy(sbuf.at[i % 2], keeps[i], kp.at[i % 2])

        def free_slot(i, n=n, store=store, credit=credit):
            if 1 <= i < n:
                store(i - 1).wait()
                if i + 1 < n:
                    pl.semaphore_signal(credit.at[(i + 1) % 2], 1, device_id=sibling, device_id_type=MESH)

        def send(i, n=n, load=load, push=push, save=save, keeps=keeps, credit=credit):
            if i < n:
                load(i).wait()
                pl.semaphore_wait(credit.at[i % 2], 1)
                push(i).start()
                if keeps[i] is not None:
                    save(i).start()

        def receive(i, n=n, load=load, push=push, store=store, save=save, keeps=keeps):
            if i < n:
                push(i).wait_recv()
                store(i).start()
                push(i).wait_send()
                if keeps[i] is not None:
                    save(i).wait()
                if i + 2 < n:
                    load(i + 2).start()

        for i in range(min(2, n)):
            pl.semaphore_signal(credit.at[i], 1, device_id=sibling, device_id_type=MESH)
            load(i).start()
        plans.append((n, free_slot, send, receive, store))
    for _, _, send, _, _ in plans:
        send(0)
    for i in range(max(p[0] for p in plans)):
        for _, free_slot, _, _, _ in plans:
            free_slot(i)
        for _, _, send, _, _ in plans:
            send(i + 1)
        for _, _, _, receive, _ in plans:
            receive(i)
    for n, _, _, _, store in plans:
        store(n - 1).wait()


def _all_gather_shards(shards, small, *, name):
    n = len(shards)

    def body(*refs):
        ins, outs = refs[:n + 1], refs[n + 1:2 * n + 2]
        scr = refs[2 * n + 2:]
        chans = [scr[CHANNEL_REFS * t:CHANNEL_REFS * (t + 1)] for t in range(n)]
        send_sems, recv_sems, small_sems = scr[CHANNEL_REFS * n:]
        x, y, c, _ = _place()
        me = 2 * x + y
        sibling = (x, y, 1 - c)
        near = (lax.rem(x + 1 - c, 2), lax.rem(y + c, 2))
        far = (lax.rem(x + c, 2), lax.rem(y + 1 - c, 2))
        k_near, k_far, k_diag = 2 * near[0] + near[1], 2 * far[0] + far[1], 3 - me
        targets = ((*near, c), (*far, c), (*far, c))
        arrives = (k_near, k_far, k_diag)
        streams_in = (k_far, k_near, k_diag)

        def ici(t, j, src, blk):
            return _remote(src, outs[t].at[blk, c], send_sems.at[3 * t + j], recv_sems.at[3 * t + j], targets[j])

        first = [ici(t, j, ins[t].at[c], me) for t in range(n + 1) for j in range(2)]
        for cp in first:
            cp.start()
        small_local = pltpu.make_async_copy(ins[n], outs[n].at[me], small_sems.at[6])
        small_local.start()
        for t in range(n):
            _copy_blocks([ins[t].at[h] for h in range(2)], [outs[t].at[me, h] for h in range(2)], chans[t])
        passed = []
        for j in range(3):
            for t in range(n + 1):
                landed = outs[t].at[arrives[j], c]
                ici(t, j, landed, arrives[j]).wait_recv()
                if j == 0:
                    fwd = ici(t, 2, landed, k_near)
                    fwd.start()
                    passed.append(fwd)
                if t < n:
                    _exchange_block_streams([([landed], [outs[t].at[streams_in[j], 1 - c]], [None], chans[t])], sibling)
                else:
                    fwd = _remote(landed, landed, small_sems.at[j], small_sems.at[3 + j], sibling)
                    fwd.start()
                    passed.append(fwd)
        for j in range(3):
            got = outs[n].at[streams_in[j], 1 - c]
            _remote(got, got, small_sems.at[j], small_sems.at[3 + j], sibling).wait_recv()
        for cp in first + passed:
            cp.wait_send()
        small_local.wait()

    scratch = []
    for s in shards:
        scratch += _channel_scratch(s.shape[2], s.dtype, rows=s.shape[1])
    return pl.pallas_call(
        body, in_specs=[ANY] * (n + 1), out_specs=[ANY] * (n + 1),
        out_shape=[SDS((N_CHIPS, *s.shape), s.dtype) for s in (*shards, small)],
        scratch_shapes=[*scratch, pltpu.SemaphoreType.DMA((3 * n + 3,)), pltpu.SemaphoreType.DMA((3 * n + 3,)),
                        pltpu.SemaphoreType.DMA((7,))],
        compiler_params=pltpu.CompilerParams(vmem_limit_bytes=VMEM_LIMIT), name=name)(*shards, small)


def _pair_reduce(stacks, *, name):
    n = len(stacks)
    per = 11

    def body(*refs):
        ins, outs, scr = refs[:n], refs[n:2 * n], refs[2 * n:]
        x, y, c, _ = _place()
        sibling = (x, y, 1 - c)
        streams = []
        for t in range(n):
            sraw, sbuf, rbuf, obuf, pbuf, ld_s, ld_o, snd, rcv, st, credit = scr[per * t:per * (t + 1)]
            steps = ins[t].shape[1] // STREAM_ROWS
            src, own, out = ins[t].at[1 - c], ins[t].at[c], outs[t]
            assert steps >= 2

            def load_s(i, slot, src=src, sraw=sraw, ld_s=ld_s):
                return pltpu.make_async_copy(src.at[_stream_rows(i)], sraw.at[slot], ld_s.at[slot])

            def load_o(i, slot, own=own, obuf=obuf, ld_o=ld_o):
                return pltpu.make_async_copy(own.at[_stream_rows(i)], obuf.at[slot], ld_o.at[slot])

            def push(slot, sbuf=sbuf, rbuf=rbuf, snd=snd, rcv=rcv):
                return _remote(sbuf.at[slot], rbuf.at[slot], snd.at[slot], rcv.at[slot], sibling)

            def store(i, slot, pbuf=pbuf, out=out, st=st):
                return pltpu.make_async_copy(pbuf.at[slot], out.at[_stream_rows(i)], st.at[slot])

            def send(i, slot, load_s=load_s, push=push, sraw=sraw, sbuf=sbuf, credit=credit):
                load_s(i, slot).wait()
                sbuf[slot] = sraw[slot].astype(sbuf.dtype)
                pl.semaphore_wait(credit.at[slot], 1)
                push(slot).start()

            def combine(i, slot, load_s=load_s, load_o=load_o, push=push, store=store, rbuf=rbuf, obuf=obuf, pbuf=pbuf,
                        credit=credit, steps=steps):
                load_o(i, slot).wait()
                push(slot).wait_recv()

                @pl.when(i >= 2)
                def _():
                    store(i, slot).wait()

                pbuf[slot] = (obuf[slot] + rbuf[slot].astype(F32)).astype(pbuf.dtype)
                store(i, slot).start()
                push(slot).wait_send()

                @pl.when(i + 2 < steps)
                def _():
                    load_s(i + 2, slot).start()
                    load_o(i + 2, slot).start()
                    pl.semaphore_signal(credit.at[slot], 1, device_id=sibling, device_id_type=MESH)

            for slot in range(2):
                pl.semaphore_signal(credit.at[slot], 1, device_id=sibling, device_id_type=MESH)
                load_s(slot, slot).start()
                load_o(slot, slot).start()
            streams.append((steps, send, combine, store))
        for _, send, _, _ in streams:
            send(0, 0)

        def step(i, carry):
            slot = lax.rem(i, 2)
            for steps, send, _, _ in streams:
                @pl.when(i + 1 < steps)
                def _(send=send):
                    send(i + 1, 1 - slot)
            for steps, _, combine, _ in streams:
                @pl.when(i < steps)
                def _(combine=combine):
                    combine(i, slot)
            return carry

        lax.fori_loop(0, max(s[0] for s in streams), step, 0)
        for _, _, _, store in streams:
            for slot in range(2):
                store(0, slot).wait()

    scratch = []
    for s in stacks:
        buf = (2, STREAM_ROWS, s.shape[2])
        scratch += [pltpu.VMEM(buf, F32), pltpu.VMEM(buf, BF16), pltpu.VMEM(buf, BF16), pltpu.VMEM(buf, F32),
                    pltpu.VMEM(buf, BF16), *([pltpu.SemaphoreType.DMA((2,))] * 5), pltpu.SemaphoreType.REGULAR((2,))]
    return pl.pallas_call(
        body, in_specs=[ANY] * n, out_specs=[ANY] * n, out_shape=[SDS(s.shape[1:], BF16) for s in stacks],
        scratch_shapes=scratch, compiler_params=pltpu.CompilerParams(vmem_limit_bytes=VMEM_LIMIT), name=name)(*stacks)


HBM_SPEC = pl.BlockSpec(memory_space=pltpu.HBM)
SEM_SPEC = pl.BlockSpec(memory_space=pltpu.SEMAPHORE)
SIDE_EFFECT = pltpu.SideEffectType.DATAFLOW_SIDE_EFFECTING


def _scatter_copies(ins, lands, send_sems, recv_sems):
    _, _, c, chips = _place()
    return [_remote(ins[t].at[2 * cx + cy], lands[t].at[j], send_sems.at[3 * t + j], recv_sems.at[3 * t + j],
                    (cx, cy, c)) for t in range(len(ins)) for j, (cx, cy) in enumerate(chips)]


def _chip_scatter_start(parts, *, name):
    n = len(parts)

    def body(*refs):
        ins, lands = refs[:n], refs[n:2 * n]
        send_sems, recv_sems, token = refs[2 * n], refs[2 * n + 1], refs[-1]
        for cp in _scatter_copies(ins, lands, send_sems, recv_sems):
            cp.start()
        token[...] = jnp.zeros_like(token)

    hbm = lambda a: pltpu.with_memory_space_constraint(a, pltpu.HBM)
    lands = [hbm(lax.empty((3, *p.shape[1:]), p.dtype)) for p in parts]
    thru = [pltpu.HBM(a.shape, a.dtype) for a in (*parts, *lands)]
    outs = pl.pallas_call(
        body, name=name,
        out_shape=(pltpu.SemaphoreType.DMA((3 * n,)), pltpu.SemaphoreType.DMA((3 * n,)), *thru, SDS((8, 128), F32)),
        in_specs=[HBM_SPEC] * (2 * n),
        out_specs=(SEM_SPEC, SEM_SPEC, *([HBM_SPEC] * (2 * n)), pl.BlockSpec(memory_space=pltpu.VMEM)),
        input_output_aliases={i: 2 + i for i in range(2 * n)},
        compiler_params=pltpu.CompilerParams(has_side_effects=SIDE_EFFECT),
    )(*[hbm(p) for p in parts], *lands)
    return outs[0], outs[1], outs[2:2 + n], outs[2 + n:2 + 2 * n], outs[-1]


def _chip_scatter_wait(send_sems, recv_sems, parts, lands, after, *, name):
    n = len(parts)

    def body(*refs):
        ins, lands_in = refs[:n], refs[n:2 * n]
        for cp in _scatter_copies(ins, lands_in, refs[2 * n], refs[2 * n + 1]):
            cp.wait_send()
            cp.wait_recv()

    outs = pl.pallas_call(
        body, name=name, out_shape=[pltpu.HBM(a.shape, a.dtype) for a in (*parts, *lands)],
        in_specs=[*([HBM_SPEC] * (2 * n)), SEM_SPEC, SEM_SPEC, *([ANY] * len(after))],
        out_specs=[HBM_SPEC] * (2 * n), input_output_aliases={i: i for i in range(2 * n)},
        compiler_params=pltpu.CompilerParams(has_side_effects=SIDE_EFFECT),
    )(*parts, *lands, send_sems, recv_sems, *after)
    return outs[:n], outs[n:]


def _gather_copies(shards, zones, send_sems, recv_sems):
    x, y, c, chips = _place()
    return [_remote(shards[t].at[c], zones[t].at[2 * x + y, c], send_sems.at[3 * t + j], recv_sems.at[3 * t + j],
                    (cx, cy, c)) for t in range(len(shards)) for j, (cx, cy) in enumerate(chips)]


def _gather_start(shards, after, *, name):
    n = len(shards)

    def body(*refs):
        ins, zones = refs[:n], refs[n:2 * n]
        send_sems, recv_sems, token = refs[2 * n + len(after)], refs[2 * n + len(after) + 1], refs[-1]
        for cp in _gather_copies(ins, zones, send_sems, recv_sems):
            cp.start()
        token[...] = jnp.zeros_like(token)

    hbm = lambda a: pltpu.with_memory_space_constraint(a, pltpu.HBM)
    zones = [hbm(lax.empty((N_CHIPS, *s.shape), s.dtype)) for s in shards]
    thru = [pltpu.HBM(a.shape, a.dtype) for a in (*shards, *zones)]
    outs = pl.pallas_call(
        body, name=name,
        out_shape=(pltpu.SemaphoreType.DMA((3 * n,)), pltpu.SemaphoreType.DMA((3 * n,)), *thru, SDS((8, 128), F32)),
        in_specs=[*([HBM_SPEC] * (2 * n)), *([ANY] * len(after))],
        out_specs=(SEM_SPEC, SEM_SPEC, *([HBM_SPEC] * (2 * n)), pl.BlockSpec(memory_space=pltpu.VMEM)),
        input_output_aliases={i: 2 + i for i in range(2 * n)},
        compiler_params=pltpu.CompilerParams(has_side_effects=SIDE_EFFECT),
    )(*[hbm(s) for s in shards], *zones, *after)
    return outs[0], outs[1], outs[2:2 + n], outs[2 + n:2 + 2 * n], outs[-1]


def _gather_wait(send_sems, recv_sems, shards, zones, after, *, name):
    n = len(shards)

    def body(*refs):
        for cp in _gather_copies(refs[:n], refs[n:2 * n], refs[2 * n], refs[2 * n + 1]):
            cp.wait_send()
            cp.wait_recv()

    outs = pl.pallas_call(
        body, name=name, out_shape=[pltpu.HBM(a.shape, a.dtype) for a in (*shards, *zones)],
        in_specs=[*([HBM_SPEC] * (2 * n)), SEM_SPEC, SEM_SPEC, *([ANY] * len(after))],
        out_specs=[HBM_SPEC] * (2 * n), input_output_aliases={i: i for i in range(2 * n)},
        compiler_params=pltpu.CompilerParams(has_side_effects=SIDE_EFFECT),
    )(*shards, *zones, send_sems, recv_sems, *after)
    return outs[:n], outs[n:]


def _gather_finish(shards, zones, *, name):
    n = len(shards)

    def body(*refs):
        ins, zones_in, outs, scr = refs[:n], refs[n:2 * n], refs[2 * n:3 * n], refs[3 * n:]
        x, y, c, chips = _place()
        me = 2 * x + y
        sibling = (x, y, 1 - c)
        others = [2 * cx + cy for cx, cy in chips]
        chans = [scr[CHANNEL_REFS * t:CHANNEL_REFS * (t + 1)] for t in range(n)]
        for t in range(n):
            _copy_blocks([ins[t].at[h] for h in range(2)], [outs[t].at[me, h] for h in range(2)], chans[t])
        _exchange_block_streams([([zones_in[t].at[k, c] for k in others], [outs[t].at[k, 1 - c] for k in others],
                                  [None] * len(others), chans[t]) for t in range(n)], sibling)

    scratch = []
    for s in shards:
        scratch += _channel_scratch(s.shape[2], s.dtype, rows=s.shape[1])
    return pl.pallas_call(
        body, in_specs=[ANY] * (2 * n), out_specs=[ANY] * n, out_shape=[SDS(z.shape, z.dtype) for z in zones],
        input_output_aliases={n + t: t for t in range(n)}, scratch_shapes=scratch,
        compiler_params=pltpu.CompilerParams(vmem_limit_bytes=VMEM_LIMIT), name=name)(*shards, *zones)


def _pair_share(groups, *, name):
    finals = [f for grp in groups for f in grp]
    n, n_out = len(finals), len(groups)

    def body(*refs):
        ins, outs, scr = refs[:n], refs[n:n + n_out], refs[n + n_out:]
        x, y, c, _ = _place()
        sibling = (x, y, 1 - c)
        t, streams = 0, []
        for o, grp in enumerate(groups):
            rows = grp[0].shape[0] // 2
            blocks = [(layer, pl.ds(b * rows, rows)) for layer in range(len(grp)) for b in range(2)]
            streams.append(([ins[t + layer].at[rs] for layer, rs in blocks],
                            [outs[o].at[layer, 1 - c, rs] for layer, rs in blocks],
                            [outs[o].at[layer, c, rs] for layer, rs in blocks],
                            scr[CHANNEL_REFS * o:CHANNEL_REFS * (o + 1)]))
            t += len(grp)
        _exchange_block_streams(streams, sibling)

    scratch = []
    for grp in groups:
        scratch += _channel_scratch(grp[0].shape[1], grp[0].dtype, rows=grp[0].shape[0] // 2)
    return pl.pallas_call(
        body, in_specs=[ANY] * n, out_specs=[ANY] * n_out,
        out_shape=[SDS((len(grp), 2, *grp[0].shape), grp[0].dtype) for grp in groups],
        scratch_shapes=scratch, compiler_params=pltpu.CompilerParams(vmem_limit_bytes=VMEM_LIMIT), name=name)(*finals)


def _all_reduce_small(v, *, name):
    rows, lanes = v.shape
    n_dev = 8

    def body(v_ref, o_ref, all_ref, send_sems, recv_sems, local_sem):
        x, y, c, chips = _place()
        me, sibling = (x, y, c), (x, y, 1 - c)

        def block(px, py, pc):
            return all_ref.at[4 * px + 2 * py + pc]

        def copy(k, blk, to, src=None):
            return _remote(block(*blk) if src is None else src, block(*blk), send_sems.at[k], recv_sems.at[k], to)

        mine = pltpu.make_async_copy(v_ref, block(*me), local_sem)
        mine.start()
        first = [copy(0, me, sibling, src=v_ref)]
        first += [copy(1 + j, me, (*chip, c), src=v_ref) for j, chip in enumerate(chips)]
        for cp in first:
            cp.start()
        passed = [copy(4 + j, (*chip, c), sibling) for j, chip in enumerate(chips)]
        for j, chip in enumerate(chips):
            copy(1 + j, (*chip, c), me).wait_recv()
            passed[j].start()
        copy(0, sibling, me).wait_recv()
        for j, chip in enumerate(chips):
            copy(4 + j, (*chip, 1 - c), me).wait_recv()
        for cp in first + passed:
            cp.wait_send()
        mine.wait()
        acc = all_ref[0]
        for k in range(1, n_dev):
            acc = acc + all_ref[k]
        o_ref[...] = acc

    vmem = pl.BlockSpec(memory_space=pltpu.VMEM)
    return pl.pallas_call(
        body, in_specs=[vmem], out_specs=vmem, out_shape=SDS((rows, lanes), F32),
        scratch_shapes=[pltpu.VMEM((n_dev, rows, lanes), F32), pltpu.SemaphoreType.DMA((7,)),
                        pltpu.SemaphoreType.DMA((7,)), pltpu.SemaphoreType.DMA],
        compiler_params=pltpu.CompilerParams(vmem_limit_bytes=VMEM_LIMIT), name=name)(v)


def _relu2_epilogue(acc):
    return acc, jnp.square(jnp.maximum(acc, 0.0))


def _res_epilogue(acc, res):
    return (acc + res,)


def _drelu2_epilogue(acc, pre):
    return (acc * (2.0 * jnp.maximum(pre.astype(F32), 0.0)),)


def _ffn_fwd(h, g, w1, w2, tag):
    f = _rms_fwd(h, g, name=f"ffn_norm_{tag}")
    pre, act = _mm_nn(f, w1, name=f"ffn1_{tag}", epilogue=_relu2_epilogue, n_out_dtypes=(BF16, BF16))
    h_out = _mm_nn(act, w2, name=f"ffn2_{tag}", extras=(h,), epilogue=_res_epilogue)
    return h_out, (f, pre, act)


def _ffn_bwd(dh, h, g, w1, w2, saved, layer, after=()):
    f, pre, act = saved
    dpre = _mm_nt(dh, w2, name=f"ffn2_dx_{layer}", out_dtype=BF16, extras=(pre,), epilogue=_drelu2_epilogue,
                  after=after)
    dw2 = _mm_tn_stacked(act, dh, name=f"ffn2_dw_{layer}", col_slots=False)
    df = _mm_nt(dpre, w1, name=f"ffn1_dx_{layer}")
    dw1 = _mm_tn_stacked(f, dpre, name=f"ffn1_dw_{layer}", col_slots=True)
    dh, dg = _rms_bwd(h, g, df, dh, name=f"ffn_norm_bwd_{layer}")
    return dh, dg, dw1, dw2


def _kv_fwd(mem, g, w_kv, tag):
    m = _rms_fwd(mem, g, name=f"mem_norm_{tag}")
    return m, _mm_nn(m, w_kv, name=f"kv_{tag}")


def _kv_bwd(mem, g, w_kv, m, dk, dv, layer):
    dkv = jnp.concatenate([dk, dv], axis=1)
    dw = _mm_tn_stacked(m, dkv, name=f"kv_dw_{layer}", col_slots=True)
    dm = _mm_nt(dkv, w_kv, name=f"kv_dx_{layer}")
    _, dg = _rms_bwd(mem, g, dm, dm, name=f"mem_norm_bwd_{layer}")
    return dw, dg


def _local_step(x, mem, target, p, after_layer1=None, after_ffn0=None, after_mixer0=None):
    row = lambda v: v.reshape(1, -1)
    g = {}

    h0 = x
    a0 = _rms_fwd(h0, row(p["norm_mix"][0]), name="mix_norm_0")
    proj_a = _mm_nn(a0, p["a_in"], name="a_in", after=p.get("after_start", ()))
    m0, kv0 = _kv_fwd(mem, row(p["mem_norm"][0]), p["w_kv"][0], "0")
    cat0 = _attn_fwd(proj_a, 2 * D_INNER, kv0, name="attn_0")
    bs_col = p["a_bs"].reshape(A_GROUPS, CHUNK, 1)
    cat0 = _gate_fwd(proj_a, p["a_ln_g"], p["a_ln_b"], p["a_ws"], bs_col, cat0, name="gate")
    h1 = _mm_nn(cat0, p["w_out"][0], name="out_0", extras=(h0,), epilogue=_res_epilogue)
    h2, ffn0 = _ffn_fwd(h1, row(p["norm_ffn"][0]), p["w_ffn1"][0], p["w_ffn2"][0], "0")

    if "layer1_mixer" in p:
        w_kv1, w_out1, b_in = p["layer1_mixer"](h2)
    else:
        w_kv1, w_out1, b_in = p["w_kv"][1], p["w_out"][1], p["b_in"]
    a1 = _rms_fwd(h2, row(p["norm_mix"][1]), name="mix_norm_1")
    proj_b = _mm_nn(a1, b_in, name="b_in")
    m1, kv1 = _kv_fwd(mem, row(p["mem_norm"][1]), w_kv1, "1")
    cat1 = _attn_fwd(proj_b, B_Q_OFF, kv1, name="attn_1")
    xbc = _conv_fwd(proj_b, p["b_conv_w"], p["b_conv_b"], name="conv")
    dt_raw = proj_b[:, B_DT_OFF:B_DT_OFF + SSM_HEADS].reshape(SEQ, SSM_GROUPS, SSM_HPG)
    dt_c = jnp.transpose(dt_raw, (1, 0, 2))
    dt_r = jnp.transpose(dt_raw, (1, 2, 0))
    per_head = lambda v: v.reshape(SSM_GROUPS, 1, SSM_HPG)
    par_row = jnp.concatenate([per_head(p["b_dt_bias"]), per_head(p["b_a_log"]), per_head(p["b_d"])], axis=1)
    ssd_par = (par_row, jnp.transpose(par_row[:, :2], (0, 2, 1)), p["b_gnorm"])
    cat1, hprev = _ssd_fwd(xbc, proj_b, dt_c, dt_r, *ssd_par, cat1, name="ssd")
    h3 = _mm_nn(cat1, w_out1, name="out_1", extras=(h2,), epilogue=_res_epilogue)
    w_ffn1_1, w_ffn2_1 = p["layer1_ffn"](h3) if "layer1_ffn" in p else (p["w_ffn1"][1], p["w_ffn2"][1])
    h4, ffn1 = _ffn_fwd(h3, row(p["norm_ffn"][1]), w_ffn1_1, w_ffn2_1, "1")

    loss, dh, g["final_norm"] = _loss_head(h4, row(p["final_norm"]), target, name="loss_head")

    dh, dnf1, dw1_1, dw2_1 = _ffn_bwd(dh, h3, row(p["norm_ffn"][1]), w_ffn1_1, w_ffn2_1, ffn1, 1)
    dcat1 = _mm_nt(dh, w_out1, name="out_dx_1")
    dwo_1 = _mm_tn_stacked(cat1, dh, name="out_dw_1", col_slots=False)
    dproj_b, dk1, dv1 = _attn_bwd(proj_b, B_Q_OFF, kv1, dcat1, B_IN_PAD, B_Q_OFF, name="attn_bwd_1")
    dproj_b, dxs, dbm, dcm, ddt_c, ddt_r, dpar_row, dpar_col, g["b_gnorm"] = _ssd_bwd(
        xbc, proj_b, dt_c, dt_r, *ssd_par, hprev, dcat1, dproj_b, name="ssd_bwd")
    dpar = dpar_row.at[:, :2].add(jnp.transpose(dpar_col, (0, 2, 1)))
    g["b_dt_bias"], g["b_a_log"], g["b_d"] = dpar[:, 0], dpar[:, 1], dpar[:, 2]
    dproj_b, g["b_conv_w"], g["b_conv_b"] = _conv_bwd(proj_b, p["b_conv_w"], p["b_conv_b"], dxs, dbm, dcm, dproj_b,
                                                      name="conv_bwd")
    ddt = jnp.transpose(ddt_c, (1, 0, 2)) + jnp.transpose(ddt_r, (2, 0, 1))
    ddt = jnp.pad(ddt.reshape(SEQ, SSM_HEADS), ((0, 0), (0, B_IN_PAD - B_DT_OFF - SSM_HEADS))).astype(BF16)
    dproj_b = lax.dynamic_update_slice(dproj_b, ddt, (0, B_DT_OFF))
    dwkv_1, dmn1 = _kv_bwd(mem, row(p["mem_norm"][1]), w_kv1, m1, dk1, dv1, 1)
    dwb = _b_in_grad_slots(_mm_tn(a1, dproj_b, name="b_in_dw"))
    da1 = _mm_nt(dproj_b, b_in, name="b_in_dx")
    dh, dnm1 = _rms_bwd(h2, row(p["norm_mix"][1]), da1, dh, name="mix_norm_bwd_1")
    layer1 = dict(w_kv=dwkv_1, w_out=dwo_1, w_ffn1=dw1_1, w_ffn2=dw2_1, b_in=dwb)
    token = () if after_layer1 is None else (after_layer1(layer1),)

    dh, dnf0, dw1_0, dw2_0 = _ffn_bwd(dh, h1, row(p["norm_ffn"][0]), p["w_ffn1"][0], p["w_ffn2"][0], ffn0, 0,
                                      after=token)
    ffn0_grads = dict(w_ffn1=dw1_0, w_ffn2=dw2_0)
    token = () if after_ffn0 is None else (after_ffn0(ffn0_grads),)
    dcat0 = _mm_nt(dh, p["w_out"][0], name="out_dx_0", after=token)
    dwo_0 = _mm_tn_stacked(cat0, dh, name="out_dw_0", col_slots=False)
    dproj_a, dk0, dv0 = _attn_bwd(proj_a, 2 * D_INNER, kv0, dcat0, A_IN, 2 * D_INNER, name="attn_bwd_0")
    dproj_a, g["a_ln_g"], g["a_ln_b"], g["a_ws"], dbs_col = _gate_bwd(
        proj_a, p["a_ln_g"], p["a_ln_b"], p["a_ws"], bs_col, dcat0, dproj_a, name="gate_bwd")
    g["a_bs"] = dbs_col.reshape(A_GROUPS, CHUNK)
    dwkv_0, dmn0 = _kv_bwd(mem, row(p["mem_norm"][0]), p["w_kv"][0], m0, dk0, dv0, 0)
    dwa = _mm_tn_stacked(a0, dproj_a, name="a_in_dw", col_slots=True)
    mixer0_grads = dict(w_kv=dwkv_0, w_out=dwo_0, a_in=dwa)
    token = () if after_mixer0 is None else (after_mixer0(mixer0_grads),)
    da0 = _mm_nt(dproj_a, p["a_in"], name="a_in_dx", after=token)
    dx, dnm0 = _rms_bwd(h0, row(p["norm_mix"][0]), da0, dh, name="mix_norm_bwd_0")

    g["norm_mix"] = jnp.concatenate([dnm0, dnm1], axis=0)
    g["norm_ffn"] = jnp.concatenate([dnf0, dnf1], axis=0)
    g["mem_norm"] = jnp.concatenate([dmn0, dmn1], axis=0)
    layer0 = dict(w_kv=dwkv_0, w_out=dwo_0, w_ffn1=dw1_0, w_ffn2=dw2_0, a_in=dwa)
    return loss, dx, g, layer0, layer1


def _b_in_full(gathered):
    n = B_IN // N_CHIPS
    dt0 = D_INNER + CONV_DIM - (N_CHIPS - 1) * n
    last = gathered[N_CHIPS - 1]
    return jnp.concatenate([*[gathered[k] for k in range(N_CHIPS - 1)], last[:, :dt0], last[:, dt0 + SSM_HEADS:],
                            last[:, dt0:dt0 + SSM_HEADS], jnp.zeros((D_MODEL, B_IN_PAD - B_IN), last.dtype)], axis=1)


def _b_in_grad_slots(d):
    n = B_IN // N_CHIPS
    dt0 = D_INNER + CONV_DIM
    last = jnp.concatenate([d[:, (N_CHIPS - 1) * n:dt0], d[:, B_DT_OFF:B_DT_OFF + SSM_HEADS], d[:, dt0:B_DT_OFF]], axis=1)
    slots = [*[d[:, k * n:(k + 1) * n] for k in range(N_CHIPS - 1)], last]
    half = D_MODEL // 2
    return jnp.stack([jnp.stack([s[h * half:(h + 1) * half] for s in slots]) for h in range(2)])


LARGE = ("w_kv", "w_out", "w_ffn1", "w_ffn2", "a_in", "b_in")
SMALL_REPL = ("norm_mix", "norm_ffn", "mem_norm", "a_ln_g", "a_ln_b", "a_ws", "a_bs", "b_dt_bias", "b_a_log", "b_d",
              "final_norm")
SMALL_SHARD = ("b_conv_w", "b_conv_b", "b_gnorm")
WEIGHTS = ("norm_mix", "norm_ffn", "mem_norm", "w_kv", "w_out", "w_ffn1", "w_ffn2", "a_in", "a_ln_g", "a_ln_b", "a_ws",
           "a_bs", "b_in", "b_conv_w", "b_conv_b", "b_dt_bias", "b_a_log", "b_d", "b_gnorm", "final_norm")
CONV_SHARD = CONV_DIM // N_CHIPS
GN_SHARD = D_INNER // N_CHIPS


LAYERED = ("w_kv", "w_out", "w_ffn1", "w_ffn2")
LAYER_TENSORS = (("w_kv", "w_out", "w_ffn1", "w_ffn2", "a_in"), ("w_kv", "w_out", "w_ffn1", "w_ffn2", "b_in"))


def _gather_weights(w):
    halves = lambda k, layer: (w[k][layer] if k in LAYERED else w[k][0]).reshape(2, -1, w[k].shape[-1]).astype(BF16)
    small = jnp.zeros((2, CONV_K, CONV_SHARD), F32)
    small = small.at[0].set(w["b_conv_w"][0])
    small = small.at[1, 0].set(w["b_conv_b"][0])
    small = small.at[1, 1, :GN_SHARD].set(w["b_gnorm"][0])
    gathered = _all_gather_shards([halves(k, 0) for k in LAYER_TENSORS[0]], small, name="gather_weights_0")
    got = dict(zip(LAYER_TENSORS[0], gathered))
    slots = lambda a: a.reshape(N_CHIPS, -1, a.shape[-1])
    rows = lambda a: a.reshape(-1, a.shape[-1])
    p = dict(w_kv=[slots(got["w_kv"])], w_out=[rows(got["w_out"])], w_ffn1=[slots(got["w_ffn1"])],
             w_ffn2=[rows(got["w_ffn2"])], a_in=slots(got["a_in"]))
    sm = gathered[-1]
    p["b_conv_w"] = jnp.transpose(sm[:, 0], (1, 0, 2)).reshape(CONV_K, CONV_DIM)
    p["b_conv_b"] = sm[:, 1, 0].reshape(1, CONV_DIM)
    p["b_gnorm"] = sm[:, 1, 1, :GN_SHARD].reshape(1, D_INNER)

    after, started = (gathered[0],), {}
    for tag, names in (("mixer", ("w_kv", "w_out", "b_in")), ("ffn", ("w_ffn1", "w_ffn2"))):
        started[tag] = _gather_start([halves(k, 1) for k in names], after, name=f"gather_start_1_{tag}")
        after = (started[tag][-1],)
    p["after_start"] = after

    def finish(tag, first):
        send_sems, recv_sems, shards, zones, _ = started[tag]
        shards, zones = _gather_wait(send_sems, recv_sems, shards, zones, (first,), name=f"gather_wait_1_{tag}")
        return _gather_finish(shards, zones, name=f"gather_finish_1_{tag}")

    def layer1_mixer(first):
        kv, wo, b_in = finish("mixer", first)
        return slots(kv), rows(wo), _b_in_full(slots(b_in))

    def layer1_ffn(first):
        w1, w2 = finish("ffn", first)
        return slots(w1), rows(w2)

    p.update(layer1_mixer=layer1_mixer, layer1_ffn=layer1_ffn)
    return p


def _pair_parts(grads, tag):
    stacks = [g.reshape(2, -1, g.shape[-1]) for g in grads.values()]
    parts = _pair_reduce(stacks, name=f"grads_pair_reduce_{tag}")
    return [t.reshape(N_CHIPS, -1, t.shape[-1]) for t in parts]


def _chip_sums(chip, names, parts, landed, tag):
    return {k: _sum_contributions(chip, t, u, name=f"grads_chip_sum_{k}_{tag}")
            for k, t, u in zip(names, parts, landed)}


def _small_layout(shapes):
    offs, o = {}, 0
    for k in (*SMALL_REPL, *SMALL_SHARD):
        size = math.prod(shapes[k])
        offs[k] = (o, size)
        o += size
    rows = -(-o // (8 * 128)) * 8
    return offs, rows


def _reduce_small(g, full_shapes):
    offs, rows = _small_layout(full_shapes)
    flat = jnp.concatenate([g[k].reshape(-1) for k in (*SMALL_REPL, *SMALL_SHARD)])
    flat = jnp.pad(flat, (0, rows * 128 - flat.shape[0])).reshape(rows, 128)
    total = _all_reduce_small(flat, name="grads_small_all_reduce").reshape(-1)
    return {k: total[o:o + n].reshape(full_shapes[k]) for k, (o, n) in offs.items()}


def kernel(x, mem, norm_mix, norm_ffn, mem_norm, w_kv, w_out, w_ffn1, w_ffn2, a_in, a_ln_g, a_ln_b, a_ws, a_bs, b_in, b_conv_w, b_conv_b, b_dt_bias, b_a_log, b_d, b_gnorm, final_norm, loss_target, m_norm_mix, m_norm_ffn, m_mem_norm, m_w_kv, m_w_out, m_w_ffn1, m_w_ffn2, m_a_in, m_a_ln_g, m_a_ln_b, m_a_ws, m_a_bs, m_b_in, m_b_conv_w, m_b_conv_b, m_b_dt_bias, m_b_a_log, m_b_d, m_b_gnorm, m_final_norm, v_norm_mix, v_norm_ffn, v_mem_norm, v_w_kv, v_w_out, v_w_ffn1, v_w_ffn2, v_a_in, v_a_ln_g, v_a_ln_b, v_a_ws, v_a_bs, v_b_in, v_b_conv_w, v_b_conv_b, v_b_dt_bias, v_b_a_log, v_b_d, v_b_gnorm, v_final_norm):
    w = dict(norm_mix=norm_mix, norm_ffn=norm_ffn, mem_norm=mem_norm, w_kv=w_kv, w_out=w_out, w_ffn1=w_ffn1,
             w_ffn2=w_ffn2, a_in=a_in, a_ln_g=a_ln_g, a_ln_b=a_ln_b, a_ws=a_ws, a_bs=a_bs, b_in=b_in, b_conv_w=b_conv_w,
             b_conv_b=b_conv_b, b_dt_bias=b_dt_bias, b_a_log=b_a_log, b_d=b_d, b_gnorm=b_gnorm, final_norm=final_norm)
    mom = dict(norm_mix=m_norm_mix, norm_ffn=m_norm_ffn, mem_norm=m_mem_norm, w_kv=m_w_kv, w_out=m_w_out,
               w_ffn1=m_w_ffn1, w_ffn2=m_w_ffn2, a_in=m_a_in, a_ln_g=m_a_ln_g, a_ln_b=m_a_ln_b, a_ws=m_a_ws,
               a_bs=m_a_bs, b_in=m_b_in, b_conv_w=m_b_conv_w, b_conv_b=m_b_conv_b, b_dt_bias=m_b_dt_bias,
               b_a_log=m_b_a_log, b_d=m_b_d, b_gnorm=m_b_gnorm, final_norm=m_final_norm)
    var = dict(norm_mix=v_norm_mix, norm_ffn=v_norm_ffn, mem_norm=v_mem_norm, w_kv=v_w_kv, w_out=v_w_out,
               w_ffn1=v_w_ffn1, w_ffn2=v_w_ffn2, a_in=v_a_in, a_ln_g=v_a_ln_g, a_ln_b=v_a_ln_b, a_ws=v_a_ws,
               a_bs=v_a_bs, b_in=v_b_in, b_conv_w=v_b_conv_w, b_conv_b=v_b_conv_b, b_dt_bias=v_b_dt_bias,
               b_a_log=v_b_a_log, b_d=v_b_d, b_gnorm=v_b_gnorm, final_norm=v_final_norm)

    p = _gather_weights(w)
    p.update(norm_mix=norm_mix, norm_ffn=norm_ffn, mem_norm=mem_norm, a_ln_g=a_ln_g, a_ln_b=a_ln_b, a_ws=a_ws[0],
             a_bs=a_bs[0], b_dt_bias=b_dt_bias, b_a_log=b_a_log, b_d=b_d, final_norm=final_norm)
    chip = 2 * lax.axis_index("x") + lax.axis_index("y")
    chip_arr = jnp.reshape(chip, (1,)).astype(jnp.int32)
    started = {}

    def start_scatter(tag):
        def hook(grads):
            start = _chip_scatter_start(_pair_parts(grads, tag), name=f"grads_chip_scatter_start_{tag}")
            started[tag] = (tuple(grads), start)
            return start[-1]
        return hook

    loss_part, dx, g, _, _ = _local_step(x[0], mem[0], loss_target[0], p, start_scatter("1"), start_scatter("0f"),
                                         start_scatter("0m"))
    loss = lax.psum(loss_part[0, 0], ("x", "y", "c"))

    def finish_scatter(tag, first):
        names, (send_sems, recv_sems, parts, lands, _) = started[tag]
        parts, landed = _chip_scatter_wait(send_sems, recv_sems, parts, lands, (first,),
                                           name=f"grads_chip_scatter_wait_{tag}")
        return _chip_sums(chip_arr, names, parts, landed, tag)

    def adamw(names, grads):
        for k in names:
            shape = w[k].shape
            flat = (lambda a: a.reshape(-1, shape[-1])) if len(shape) > 1 else (lambda a: a.reshape(1, -1))
            d, m_new, v_new = _adamw(flat(w[k]), flat(grads[k]), flat(mom[k]), flat(var[k]), name=f"adamw_{k}")
            delta[k], new_m[k], new_v[k] = d.reshape(shape), m_new.reshape(shape), v_new.reshape(shape)

    full_shapes = {k: w[k].shape for k in SMALL_REPL}
    full_shapes.update(b_conv_w=(1, CONV_K, CONV_DIM), b_conv_b=(1, CONV_DIM), b_gnorm=(1, D_INNER))
    grads = _reduce_small(g, full_shapes)
    grads["b_conv_w"] = lax.dynamic_slice_in_dim(grads["b_conv_w"], chip * CONV_SHARD, CONV_SHARD, axis=2)
    grads["b_conv_b"] = lax.dynamic_slice_in_dim(grads["b_conv_b"], chip * CONV_SHARD, CONV_SHARD, axis=1)
    grads["b_gnorm"] = lax.dynamic_slice_in_dim(grads["b_gnorm"], chip * GN_SHARD, GN_SHARD, axis=1)
    delta, new_m, new_v = {}, {}, {}
    halves = [finish_scatter("0f", dx), finish_scatter("1", dx)]
    early = ("w_ffn1", "w_ffn2", "b_in")
    shared = _pair_share([[halves[layer][k] for layer in range(2) if k in halves[layer]] for k in early],
                         name="grads_pair_share_early")
    grads.update({k: a.reshape(w[k].shape) for k, a in zip(early, shared)})
    adamw([k for k in WEIGHTS if k in grads], grads)
    halves[0].update(finish_scatter("0m", delta["b_in"]))
    late = ("w_kv", "w_out", "a_in")
    shared = _pair_share([[halves[layer][k] for layer in range(2) if k in halves[layer]] for k in late],
                         name="grads_pair_share_late")
    grads.update({k: a.reshape(w[k].shape) for k, a in zip(late, shared)})
    adamw(late, grads)

    return (loss, dx.reshape(x.shape), *[grads[k] for k in WEIGHTS], *[delta[k] for k in WEIGHTS],
            *[new_m[k] for k in WEIGHTS], *[new_v[k] for k in WEIGHTS])
```

```python
import math

import jax
import jax.numpy as jnp
from jax import lax
from jax.experimental import pallas as pl
from jax.experimental.pallas import tpu as pltpu

F32 = jnp.float32
BF16 = jnp.bfloat16
SDS = jax.ShapeDtypeStruct

D_MODEL = 1024
SEQ = 2048
CHUNK = 128
N_MEM = 256
D_INNER = 2048
A_GROUPS = 8
A_GROUP_W = D_INNER // A_GROUPS
SSM_HEADS = 32
SSM_HEAD_DIM = 64
SSM_GROUPS = 4
SSM_HPG = 8
SSM_STATE = 128
SSM_GROUP_W = SSM_HPG * SSM_HEAD_DIM
CONV_K = 4
CONV_DIM = 3072
X_HEADS = 4
X_HEAD_DIM = 256
X_WIDTH = 1024
MIX_OUT = 3072
D_FF = 4096
A_IN = 5120
B_IN = 6176
B_IN_PAD = 6272
B_Q_OFF = 5120
B_DT_OFF = 6144
N_CHUNKS = SEQ // CHUNK
EPS = 1e-6
N_CHIPS = 4

ADAM_LR = 0.001
ADAM_B1 = 0.9
ADAM_B2 = 0.999
ADAM_EPS = 1e-08
ADAM_WD = 0.01
ADAM_STEP = 10

VMEM_LIMIT = 48 * 1024 * 1024
MESH = pl.DeviceIdType.MESH


def _cparams(sem):
    return pltpu.CompilerParams(dimension_semantics=sem, vmem_limit_bytes=VMEM_LIMIT)


def _dot(a, b, dims=(((1,), (0,)), ((), ()))):
    return lax.dot_general(a.astype(BF16), b.astype(BF16), dims, preferred_element_type=F32)


def _dot_nt(a, b):
    return _dot(a, b, (((1,), (1,)), ((), ())))


def _dot_tn(a, b):
    return _dot(a, b, (((0,), (0,)), ((), ())))


def _pick(n, cands):
    for c in cands:
        if n % c == 0:
            return c
    raise ValueError(f"no tile for {n}")


def _mm_call(a, b, *, dims, grid, a_spec, b_spec, acc_shape, out_shapes, out_specs, name,
             extras=(), extra_specs=(), epilogue=None, after=()):
    n_k = grid[2]
    n_extra = len(extras)
    n_out = len(out_shapes)
    n_in = 2 + n_extra + len(after)

    def finish(total, extra_refs, out_refs):
        vals = (total,) if epilogue is None else epilogue(total, *[e[...] for e in extra_refs])
        for o_ref, v in zip(out_refs, vals):
            o_ref[...] = v.astype(o_ref.dtype)

    def body_one_step(*refs):
        finish(_dot(refs[0][...], refs[1][...], dims), refs[2:2 + n_extra], refs[n_in:n_in + n_out])

    def body(*refs):
        acc = refs[-1]
        k = pl.program_id(2)

        @pl.when(k == 0)
        def _():
            acc[...] = jnp.zeros_like(acc)

        acc[...] += _dot(refs[0][...], refs[1][...], dims)

        @pl.when(k == n_k - 1)
        def _():
            finish(acc[...], refs[2:2 + n_extra], refs[n_in:n_in + n_out])

    return pl.pallas_call(
        body_one_step if n_k == 1 else body, grid=grid,
        in_specs=[a_spec, b_spec, *extra_specs, *([ANY] * len(after))], out_specs=list(out_specs),
        out_shape=list(out_shapes), scratch_shapes=[] if n_k == 1 else [pltpu.VMEM(acc_shape, F32)],
        compiler_params=_cparams(("parallel", "parallel", "arbitrary")), name=name,
    )(a, b, *extras, *after)


def _w_dims(w):
    if w.ndim == 2:
        return w.shape[0], w.shape[1], 1, w.shape[1]
    return w.shape[1], w.shape[0] * w.shape[2], w.shape[0], w.shape[2]


def _mm_nn(a, w, *, name, out_dtype=F32, a_cols=None, extras=(), epilogue=None, n_out_dtypes=None, after=()):
    m = a.shape[0]
    k_dim, n_dim, _, n_slot = _w_dims(w)
    a_off, a_w = (0, a.shape[1]) if a_cols is None else a_cols
    assert a_w == k_dim
    tm = _pick(m, (2048, 1024, 512, 256))
    tn = _pick(n_slot, (512, 896, 640, 256, 128))
    tk = _pick(k_dim, (1024, 768, 512, 384, 256, 128))
    assert a_off % tk == 0
    nb = n_slot // tn
    a_spec = pl.BlockSpec((tm, tk), lambda i, j, k: (i, a_off // tk + k))
    if w.ndim == 2:
        b_spec = pl.BlockSpec((tk, tn), lambda i, j, k: (k, j))
    else:
        b_spec = pl.BlockSpec((None, tk, tn), lambda i, j, k: (j // nb, k, j % nb))
    o_spec = pl.BlockSpec((tm, tn), lambda i, j, k: (i, j))
    dts = n_out_dtypes or (out_dtype,)
    outs = _mm_call(a, w, dims=(((1,), (0,)), ((), ())), grid=(m // tm, n_dim // tn, k_dim // tk),
                    a_spec=a_spec, b_spec=b_spec, acc_shape=(tm, tn),
                    out_shapes=[SDS((m, n_dim), dt) for dt in dts], out_specs=[o_spec] * len(dts), name=name,
                    extras=extras, extra_specs=[o_spec] * len(extras), epilogue=epilogue, after=after)
    return outs if n_out_dtypes else outs[0]


def _mm_nt(a, w, *, name, out_dtype=F32, extras=(), epilogue=None, after=()):
    m = a.shape[0]
    k_dim, n_dim, _, n_slot = _w_dims(w)
    assert a.shape[1] == n_dim
    tm = _pick(m, (2048, 1024, 512, 256))
    to = _pick(k_dim, (512, 384, 256, 128))
    tc = _pick(n_slot, (1280, 1024, 896, 640, 512, 256, 128))
    nb = n_slot // tc
    a_spec = pl.BlockSpec((tm, tc), lambda i, j, k: (i, k))
    if w.ndim == 2:
        b_spec = pl.BlockSpec((to, tc), lambda i, j, k: (j, k))
    else:
        b_spec = pl.BlockSpec((None, to, tc), lambda i, j, k: (k // nb, j, k % nb))
    o_spec = pl.BlockSpec((tm, to), lambda i, j, k: (i, j))
    return _mm_call(a, w, dims=(((1,), (1,)), ((), ())), grid=(m // tm, k_dim // to, n_dim // tc),
                    a_spec=a_spec, b_spec=b_spec, acc_shape=(tm, to),
                    out_shapes=[SDS((m, k_dim), out_dtype)], out_specs=[o_spec], name=name,
                    extras=extras, extra_specs=[o_spec] * len(extras), epilogue=epilogue, after=after)[0]


def _mm_tn(x, dy, *, name, x_cols=None):
    s = x.shape[0]
    x_off, k_dim = (0, x.shape[1]) if x_cols is None else x_cols
    n_dim = dy.shape[1]
    tm = _pick(k_dim, (1024, 768, 512, 384, 256, 128))
    tn = _pick(n_dim, (512, 896, 640, 256, 128))
    tk = _pick(s, (2048, 1024, 512, 256))
    assert x_off % tm == 0
    a_spec = pl.BlockSpec((tk, tm), lambda i, j, k: (k, x_off // tm + i))
    b_spec = pl.BlockSpec((tk, tn), lambda i, j, k: (k, j))
    o_spec = pl.BlockSpec((tm, tn), lambda i, j, k: (i, j))
    return _mm_call(x, dy, dims=(((0,), (0,)), ((), ())), grid=(k_dim // tm, n_dim // tn, s // tk),
                    a_spec=a_spec, b_spec=b_spec, acc_shape=(tm, tn),
                    out_shapes=[SDS((k_dim, n_dim), F32)], out_specs=[o_spec], name=name)[0]


def _mm_tn_stacked(x, dy, *, name, col_slots):
    s, k_dim = x.shape
    n_dim = dy.shape[1]
    r, c = (k_dim // 2, n_dim // N_CHIPS) if col_slots else (k_dim // N_CHIPS // 2, n_dim)
    tm = 2 * r
    tn = _pick(c, (512, 896, 640, 256, 128))
    tk = _pick(s, (2048, 1024, 512, 256))
    a_spec = pl.BlockSpec((tk, tm), lambda i, j, k: (k, i))
    b_spec = pl.BlockSpec((tk, tn), lambda i, j, k: (k, j))
    if col_slots:
        nb = c // tn
        o_spec = pl.BlockSpec((2, None, r, tn), lambda i, j, k: (0, j // nb, 0, j % nb))
    else:
        o_spec = pl.BlockSpec((2, None, r, tn), lambda i, j, k: (0, i, 0, j))
    return _mm_call(x, dy, dims=(((0,), (0,)), ((), ())), grid=(k_dim // tm, n_dim // tn, s // tk),
                    a_spec=a_spec, b_spec=b_spec, acc_shape=(tm, tn), epilogue=lambda acc: (acc.reshape(2, r, tn),),
                    out_shapes=[SDS((2, N_CHIPS, r, c), F32)], out_specs=[o_spec], name=name)[0]


def _rms(x, g):
    return x * lax.rsqrt(jnp.mean(x * x, axis=-1, keepdims=True) + EPS) * g


def _rms_fwd(h, g, *, name):
    rows, d = h.shape
    tr = _pick(rows, (512, 256))

    def body(h_ref, g_ref, o_ref):
        o_ref[...] = _rms(h_ref[...], g_ref[...]).astype(o_ref.dtype)

    return pl.pallas_call(
        body, grid=(rows // tr,),
        in_specs=[pl.BlockSpec((tr, d), lambda i: (i, 0)), pl.BlockSpec((1, d), lambda i: (0, 0))],
        out_specs=pl.BlockSpec((tr, d), lambda i: (i, 0)), out_shape=SDS((rows, d), BF16),
        compiler_params=_cparams(("parallel",)), name=name)(h, g)


def _rms_bwd(h, g, da, dres, *, name):
    rows, d = h.shape
    tr = _pick(rows, (512, 256))

    def body(h_ref, g_ref, da_ref, dres_ref, dh_ref, dg_ref):
        _, vjp = jax.vjp(_rms, h_ref[...], g_ref[...])
        dh, dg = vjp(da_ref[...].astype(F32))
        dh_ref[...] = dres_ref[...] + dh

        @pl.when(pl.program_id(0) == 0)
        def _():
            dg_ref[...] = jnp.zeros_like(dg_ref)

        dg_ref[...] += dg

    row_spec = pl.BlockSpec((tr, d), lambda i: (i, 0))
    vec_spec = pl.BlockSpec((1, d), lambda i: (0, 0))
    return pl.pallas_call(
        body, grid=(rows // tr,), in_specs=[row_spec, vec_spec, row_spec, row_spec],
        out_specs=[row_spec, vec_spec], out_shape=[SDS((rows, d), F32), SDS((1, d), F32)],
        compiler_params=_cparams(("arbitrary",)), name=name)(h, g, da, dres)


def _loss_head(h, g, target, *, name):
    rows, d = h.shape
    tr = _pick(rows, (512, 256))

    def body(h_ref, g_ref, t_ref, loss_ref, dh_ref, dg_ref):
        y, vjp = jax.vjp(_rms, h_ref[...], g_ref[...])
        err = y - t_ref[...]
        dh, dg = vjp(err * (1.0 / d))
        dh_ref[...] = dh

        @pl.when(pl.program_id(0) == 0)
        def _():
            dg_ref[...] = jnp.zeros_like(dg_ref)
            loss_ref[...] = jnp.zeros_like(loss_ref)

        dg_ref[...] += dg
        part = jnp.sum(jnp.sum(err * err, axis=-1, keepdims=True), axis=0, keepdims=True) * (0.5 / d)
        loss_ref[...] += jnp.broadcast_to(part, loss_ref.shape)

    row_spec = pl.BlockSpec((tr, d), lambda i: (i, 0))
    vec_spec = pl.BlockSpec((1, d), lambda i: (0, 0))
    loss_spec = pl.BlockSpec((8, 128), lambda i: (0, 0))
    return pl.pallas_call(
        body, grid=(rows // tr,), in_specs=[row_spec, vec_spec, row_spec],
        out_specs=[loss_spec, row_spec, vec_spec],
        out_shape=[SDS((8, 128), F32), SDS((rows, d), F32), SDS((1, d), F32)],
        compiler_params=_cparams(("arbitrary",)), name=name)(h, g, target)


def _gelu(x):
    return 0.5 * x * (1.0 + lax.erf(x * (1.0 / math.sqrt(2.0))))


def _gate_tile(pu, pv, ln_g, ln_b, ws, bs_t):
    u = [_gelu(p) for p in pu]
    v = [_gelu(p) for p in pv]
    mu = sum(jnp.sum(t, axis=-1, keepdims=True) for t in v) * (1.0 / D_INNER)
    vc = [t - mu for t in v]
    var = sum(jnp.sum(t * t, axis=-1, keepdims=True) for t in vc) * (1.0 / D_INNER)
    rstd = lax.rsqrt(var + EPS)
    row = lax.broadcasted_iota(jnp.int32, (CHUNK, CHUNK), 0)
    col = lax.broadcasted_iota(jnp.int32, (CHUNK, CHUNK), 1)
    out = []
    for gi in range(A_GROUPS):
        vn = vc[gi] * rstd * ln_g[gi] + ln_b[gi]
        w = jnp.where(row >= col, ws[gi], 0.0)
        sv = _dot(w, vn) + bs_t[gi]
        out.append(u[gi] * sv)
    return out


def _split(ref, n, width):
    return [ref[:, i * width:(i + 1) * width] for i in range(n)]


def _gate_in_specs():
    return [
        pl.BlockSpec((CHUNK, D_INNER), lambda c: (c, 0)),
        pl.BlockSpec((CHUNK, D_INNER), lambda c: (c, 1)),
        pl.BlockSpec((1, D_INNER), lambda c: (0, 0)),
        pl.BlockSpec((1, D_INNER), lambda c: (0, 0)),
        pl.BlockSpec((A_GROUPS, CHUNK, CHUNK), lambda c: (0, 0, 0)),
        pl.BlockSpec((A_GROUPS, CHUNK, 1), lambda c: (0, 0, 0)),
    ]


def _gate_args(u_ref, v_ref, g_ref, b_ref, ws_ref, bs_ref):
    ng, gw = A_GROUPS, A_GROUP_W
    return (_split(u_ref, ng, gw), _split(v_ref, ng, gw), _split(g_ref, ng, gw), _split(b_ref, ng, gw),
            [ws_ref[i] for i in range(ng)], [bs_ref[i] for i in range(ng)])


def _gate_fwd(proj, ln_g, ln_b, ws, bs_col, mixcat, *, name):
    def body(u_ref, v_ref, g_ref, b_ref, ws_ref, bs_ref, cat_in, cat_ref):
        del cat_in
        out = _gate_tile(*_gate_args(u_ref, v_ref, g_ref, b_ref, ws_ref, bs_ref))
        for gi, o in enumerate(out):
            cat_ref[:, gi * A_GROUP_W:(gi + 1) * A_GROUP_W] = o.astype(cat_ref.dtype)

    return pl.pallas_call(
        body, grid=(N_CHUNKS,), in_specs=[*_gate_in_specs(), pl.BlockSpec(memory_space=pl.ANY)],
        out_specs=pl.BlockSpec((CHUNK, D_INNER), lambda c: (c, 0)), out_shape=SDS(mixcat.shape, mixcat.dtype),
        input_output_aliases={6: 0}, compiler_params=_cparams(("parallel",)), name=name,
    )(proj, proj, ln_g, ln_b, ws, bs_col, mixcat)


def _gate_bwd(proj, ln_g, ln_b, ws, bs_col, dcat, dproj, *, name):
    ng, gw = A_GROUPS, A_GROUP_W

    def body(u_ref, v_ref, g_ref, b_ref, ws_ref, bs_ref, d_ref, dproj_in, dproj_ref, dg_ref, db_ref, dws_ref, dbs_ref):
        del dproj_in
        args = _gate_args(u_ref, v_ref, g_ref, b_ref, ws_ref, bs_ref)
        _, vjp = jax.vjp(_gate_tile, *args)
        dpu, dpv, dg, db, dws, dbs = vjp(_split(d_ref, ng, gw))
        for gi in range(ng):
            dproj_ref[:, gi * gw:(gi + 1) * gw] = dpu[gi].astype(dproj_ref.dtype)
            dproj_ref[:, D_INNER + gi * gw:D_INNER + (gi + 1) * gw] = dpv[gi].astype(dproj_ref.dtype)

        @pl.when(pl.program_id(0) == 0)
        def _():
            for r in (dg_ref, db_ref, dws_ref, dbs_ref):
                r[...] = jnp.zeros_like(r)

        for gi in range(ng):
            dg_ref[:, gi * gw:(gi + 1) * gw] += dg[gi]
            db_ref[:, gi * gw:(gi + 1) * gw] += db[gi]
            dws_ref[gi] += dws[gi]
            dbs_ref[gi] += dbs[gi]

    in_specs = _gate_in_specs()
    return pl.pallas_call(
        body, grid=(N_CHUNKS,),
        in_specs=[*in_specs, pl.BlockSpec((CHUNK, D_INNER), lambda c: (c, 0)), pl.BlockSpec(memory_space=pl.ANY)],
        out_specs=[pl.BlockSpec((CHUNK, 2 * D_INNER), lambda c: (c, 0)), *in_specs[2:]],
        out_shape=[SDS(dproj.shape, dproj.dtype), SDS((1, D_INNER), F32), SDS((1, D_INNER), F32),
                   SDS((ng, CHUNK, CHUNK), F32), SDS((ng, CHUNK, 1), F32)],
        input_output_aliases={7: 0}, compiler_params=_cparams(("arbitrary",)), name=name,
    )(proj, proj, ln_g, ln_b, ws, bs_col, dcat, dproj)


ATT_TQ = 512


def _attn_tile(q, k, v):
    s = _dot_nt(q, k) * (1.0 / math.sqrt(X_HEAD_DIM))
    s = s - jnp.max(s, axis=-1, keepdims=True)
    e = jnp.exp(s)
    p = e / jnp.sum(e, axis=-1, keepdims=True)
    return _dot(p, v)


def _attn_in_specs(q_blk, order):
    hd = X_HEAD_DIM
    return [
        pl.BlockSpec((ATT_TQ, hd), lambda a, b: (order(a, b)[0], q_blk + order(a, b)[1])),
        pl.BlockSpec((N_MEM, hd), lambda a, b: (0, order(a, b)[1])),
        pl.BlockSpec((N_MEM, hd), lambda a, b: (0, X_HEADS + order(a, b)[1])),
    ]


def _attn_fwd(proj, q_off, kv, *, name):
    order = lambda i, h: (i, h)
    cat_blk = D_INNER // X_HEAD_DIM

    def body(q_ref, k_ref, v_ref, o_ref):
        o_ref[...] = _attn_tile(q_ref[...], k_ref[...], v_ref[...]).astype(o_ref.dtype)

    return pl.pallas_call(
        body, grid=(SEQ // ATT_TQ, X_HEADS), in_specs=_attn_in_specs(q_off // X_HEAD_DIM, order),
        out_specs=pl.BlockSpec((ATT_TQ, X_HEAD_DIM), lambda i, h: (i, cat_blk + h)),
        out_shape=SDS((SEQ, MIX_OUT), BF16), compiler_params=_cparams(("parallel", "parallel")), name=name,
    )(proj, kv, kv)


def _attn_bwd(proj, q_off, kv, dcat, dproj_width, dq_off, *, name):
    order = lambda h, i: (i, h)
    cat_blk = D_INNER // X_HEAD_DIM
    dq_blk = dq_off // X_HEAD_DIM

    def body(q_ref, k_ref, v_ref, do_ref, dq_ref, dk_ref, dv_ref):
        _, vjp = jax.vjp(_attn_tile, q_ref[...], k_ref[...], v_ref[...])
        dq, dk, dv = vjp(do_ref[...])
        dq_ref[...] = dq.astype(dq_ref.dtype)

        @pl.when(pl.program_id(1) == 0)
        def _():
            dk_ref[...] = jnp.zeros_like(dk_ref)
            dv_ref[...] = jnp.zeros_like(dv_ref)

        dk_ref[...] += dk
        dv_ref[...] += dv

    kv_spec = pl.BlockSpec((N_MEM, X_HEAD_DIM), lambda h, i: (0, h))
    return pl.pallas_call(
        body, grid=(X_HEADS, SEQ // ATT_TQ),
        in_specs=[*_attn_in_specs(q_off // X_HEAD_DIM, order),
                  pl.BlockSpec((ATT_TQ, X_HEAD_DIM), lambda h, i: (i, cat_blk + h))],
        out_specs=[pl.BlockSpec((ATT_TQ, X_HEAD_DIM), lambda h, i: (i, dq_blk + h)), kv_spec, kv_spec],
        out_shape=[SDS((SEQ, dproj_width), BF16), SDS((N_MEM, X_WIDTH), F32), SDS((N_MEM, X_WIDTH), F32)],
        compiler_params=_cparams(("parallel", "arbitrary")), name=name,
    )(proj, kv, kv, dcat)


CONV_TC = 512


def _shift_down(x, s):
    if s == 0:
        return x
    row = lax.broadcasted_iota(jnp.int32, x.shape, 0)
    return jnp.where(row >= s, pltpu.roll(x, s, 0), 0.0)


def _shift_up(x, s):
    if s == 0:
        return x
    n = x.shape[0]
    row = lax.broadcasted_iota(jnp.int32, x.shape, 0)
    return jnp.where(row < n - s, pltpu.roll(x, n - s, 0), 0.0)


def _conv_pre(x, w_ref, b_ref):
    pre = b_ref[...] + jnp.zeros_like(x)
    for k in range(CONV_K):
        pre = pre + w_ref[k:k + 1, :] * _shift_down(x, CONV_K - 1 - k)
    return pre


def _conv_fwd(proj, w, b, *, name):
    blk0 = D_INNER // CONV_TC

    def body(x_ref, w_ref, b_ref, o_ref):
        pre = _conv_pre(x_ref[...], w_ref, b_ref)
        o_ref[...] = pre * jax.nn.sigmoid(pre)

    return pl.pallas_call(
        body, grid=(CONV_DIM // CONV_TC,),
        in_specs=[pl.BlockSpec((SEQ, CONV_TC), lambda j: (0, blk0 + j)), pl.BlockSpec((CONV_K, CONV_TC), lambda j: (0, j)),
                  pl.BlockSpec((1, CONV_TC), lambda j: (0, j))],
        out_specs=pl.BlockSpec((SEQ, CONV_TC), lambda j: (0, j)), out_shape=SDS((SEQ, CONV_DIM), F32),
        compiler_params=_cparams(("parallel",)), name=name)(proj, w, b)


def _conv_bwd(proj, w, b, dxs, dbm, dcm, dproj, *, name):
    tc = CONV_TC // 2
    blk0 = D_INNER // tc
    n_x = D_INNER // tc
    n_b = SSM_GROUPS * SSM_STATE // tc

    def body(x_ref, w_ref, b_ref, dxs_ref, dbm_ref, dcm_ref, dproj_in, dproj_ref, dw_ref, db_ref):
        del dproj_in
        j = pl.program_id(0)
        x = x_ref[...]
        pre = _conv_pre(x, w_ref, b_ref)
        sg = jax.nn.sigmoid(pre)
        dact = jnp.where(j < n_x, dxs_ref[...], jnp.where(j < n_x + n_b, dbm_ref[...], dcm_ref[...]))
        dpre = dact * (sg * (1.0 + pre * (1.0 - sg)))
        dx = jnp.zeros_like(x)
        for k in range(CONV_K):
            s = CONV_K - 1 - k
            dx = dx + w_ref[k:k + 1, :] * _shift_up(dpre, s)
            dw_ref[k:k + 1, :] = jnp.sum(dpre * _shift_down(x, s), axis=0, keepdims=True)
        dproj_ref[...] = dx.astype(dproj_ref.dtype)
        db_ref[...] = jnp.sum(dpre, axis=0, keepdims=True)

    clip = lambda v, hi: jnp.minimum(jnp.maximum(v, 0), hi)
    return pl.pallas_call(
        body, grid=(CONV_DIM // tc,),
        in_specs=[pl.BlockSpec((SEQ, tc), lambda j: (0, blk0 + j)), pl.BlockSpec((CONV_K, tc), lambda j: (0, j)),
                  pl.BlockSpec((1, tc), lambda j: (0, j)),
                  pl.BlockSpec((SEQ, tc), lambda j: (0, clip(j, n_x - 1))),
                  pl.BlockSpec((SEQ, tc), lambda j: (0, clip(j - n_x, n_b - 1))),
                  pl.BlockSpec((SEQ, tc), lambda j: (0, clip(j - n_x - n_b, n_b - 1))),
                  pl.BlockSpec(memory_space=pl.ANY)],
        out_specs=[pl.BlockSpec((SEQ, tc), lambda j: (0, blk0 + j)), pl.BlockSpec((CONV_K, tc), lambda j: (0, j)),
                   pl.BlockSpec((1, tc), lambda j: (0, j))],
        out_shape=[SDS(dproj.shape, dproj.dtype), SDS((CONV_K, CONV_DIM), F32), SDS((1, CONV_DIM), F32)],
        input_output_aliases={6: 0}, compiler_params=_cparams(("parallel",)), name=name,
    )(proj, w, b, dxs, dbm, dcm, dproj)


SSM_PAIRS = SSM_HPG // 2


def _dot_exact01(x, m01, m01_t, x_first, differentiable):
    def product(v, m):
        hi = v.astype(BF16)
        rest = v - hi.astype(F32)
        mid = rest.astype(BF16)
        lo = (rest - mid.astype(F32)).astype(BF16)
        dims = (((1,), (0,)), ((), ()))
        dot = lambda part: lax.dot_general(*((part, m) if x_first else (m, part)), dims, preferred_element_type=F32)
        return dot(hi) + dot(mid) + dot(lo)

    if not differentiable:
        return product(x, m01)

    @jax.custom_vjp
    def exact(v):
        return product(v, m01)

    exact.defvjp(lambda v: (product(v, m01), None), lambda _, ct: (product(ct, m01_t),))
    return exact(x)


def _ssd_tile(xp, zp, bm, cm, hp, dt_c, dt_r, bias, bias_col, alog, alog_col, dsk, gnp, differentiable=False):
    row = lax.broadcasted_iota(jnp.int32, (CHUNK, CHUNK), 0)
    col = lax.broadcasted_iota(jnp.int32, (CHUNK, CHUNK), 1)
    causal = row >= col
    left = col < SSM_HEAD_DIM
    top = row < SSM_HEAD_DIM
    ones = jnp.ones((CHUNK, CHUNK), BF16)
    cb = _dot_nt(cm, bm)
    dtp = jax.nn.softplus(dt_c + bias)
    da_c = dtp * -jnp.exp(alog)
    da_r = jax.nn.softplus(dt_r + bias_col) * -jnp.exp(alog_col)
    lower = jnp.where(causal, 1.0, 0.0).astype(BF16)
    upper = jnp.where(row <= col, 1.0, 0.0).astype(BF16)
    cs = _dot_exact01(da_c, lower, upper, False, differentiable)
    cs_rows = _dot_exact01(da_r, upper, lower, True, differentiable)
    cs_last = jnp.sum(da_c, axis=0, keepdims=True)
    ecs, decay, ecl = jnp.exp(cs), jnp.exp(cs_last - cs), jnp.exp(cs_last)
    m = [cb * jnp.exp(jnp.where(causal, cs[:, r:r + 1] - cs_rows[r:r + 1, :], -1e30)) for r in range(SSM_HPG)]
    ygs, hn = [], []
    for p in range(SSM_PAIRS):
        a, b = 2 * p, 2 * p + 1
        pair = lambda v: jnp.where(left, v[:, a:a + 1], v[:, b:b + 1])
        xdt = xp[p] * pair(dtp)
        y = jnp.where(left, _dot(m[a], xdt), _dot(m[b], xdt))
        y = y + _dot_nt(cm, hp[p]) * pair(ecs)
        y = y + xp[p] * pair(dsk)
        states = _dot_tn(xdt * pair(decay), bm)
        hn.append(hp[p] * jnp.where(top, ecl[:, a:a + 1], ecl[:, b:b + 1]) + states)
        ygs.append(y * (zp[p] * jax.nn.sigmoid(zp[p])))
    ms = sum(_dot(t * t, ones) for t in ygs) * (1.0 / SSM_GROUP_W)
    rs = lax.rsqrt(ms + EPS)
    return [ygs[p] * rs * gnp[p] for p in range(SSM_PAIRS)], hn


def _ssd_in_specs(cidx):
    gw, n = SSM_GROUP_W, SSM_STATE
    bm_blk = D_INNER // n
    return [
        pl.BlockSpec((CHUNK, gw), lambda g, c: (cidx(c), g)),
        pl.BlockSpec((CHUNK, gw), lambda g, c: (cidx(c), g)),
        pl.BlockSpec((CHUNK, n), lambda g, c: (cidx(c), bm_blk + g)),
        pl.BlockSpec((CHUNK, n), lambda g, c: (cidx(c), bm_blk + SSM_GROUPS + g)),
        pl.BlockSpec((None, CHUNK, SSM_HPG), lambda g, c: (g, cidx(c), 0)),
        pl.BlockSpec((None, SSM_HPG, CHUNK), lambda g, c: (g, 0, cidx(c))),
        pl.BlockSpec((None, 3, SSM_HPG), lambda g, c: (g, 0, 0)),
        pl.BlockSpec((None, SSM_HPG, 2), lambda g, c: (g, 0, 0)),
        pl.BlockSpec((1, gw), lambda g, c: (0, g)),
    ]


def _ssd_args(x_ref, z_ref, bm_ref, cm_ref, hp, dtc_ref, dtr_ref, prow_ref, pcol_ref, gn_ref):
    npair, w = SSM_PAIRS, 2 * SSM_HEAD_DIM
    return (_split(x_ref, npair, w), _split(z_ref, npair, w), bm_ref[...], cm_ref[...], hp, dtc_ref[...], dtr_ref[...],
            prow_ref[0:1, :], pcol_ref[:, 0:1], prow_ref[1:2, :], pcol_ref[:, 1:2], prow_ref[2:3, :],
            _split(gn_ref, npair, w))


def _pair_rows(ref):
    w = 2 * SSM_HEAD_DIM
    return [ref[p * w:(p + 1) * w, :] for p in range(SSM_PAIRS)]


def _ssd_fwd(xbc, proj, dt_c, dt_r, par_row, par_col, gn, mixcat, *, name):
    w = 2 * SSM_HEAD_DIM

    def body(x_ref, z_ref, bm_ref, cm_ref, dtc_ref, dtr_ref, prow_ref, pcol_ref, gn_ref, cat_in,
             cat_ref, hprev_ref, h_scr):
        del cat_in

        @pl.when(pl.program_id(1) == 0)
        def _():
            h_scr[...] = jnp.zeros_like(h_scr)

        hprev_ref[...] = h_scr[...]
        yn, hn = _ssd_tile(*_ssd_args(x_ref, z_ref, bm_ref, cm_ref, _pair_rows(h_scr), dtc_ref, dtr_ref, prow_ref,
                                      pcol_ref, gn_ref))
        for p in range(SSM_PAIRS):
            cat_ref[:, p * w:(p + 1) * w] = yn[p].astype(cat_ref.dtype)
            h_scr[p * w:(p + 1) * w, :] = hn[p]

    return pl.pallas_call(
        body, grid=(SSM_GROUPS, N_CHUNKS), in_specs=[*_ssd_in_specs(lambda c: c), pl.BlockSpec(memory_space=pl.ANY)],
        out_specs=[pl.BlockSpec((CHUNK, SSM_GROUP_W), lambda g, c: (c, g)),
                   pl.BlockSpec((None, None, SSM_GROUP_W, SSM_STATE), lambda g, c: (c, g, 0, 0))],
        out_shape=[SDS(mixcat.shape, mixcat.dtype), SDS((N_CHUNKS, SSM_GROUPS, SSM_GROUP_W, SSM_STATE), F32)],
        scratch_shapes=[pltpu.VMEM((SSM_GROUP_W, SSM_STATE), F32)],
        input_output_aliases={9: 0}, compiler_params=_cparams(("parallel", "arbitrary")), name=name,
    )(xbc, proj, xbc, xbc, dt_c, dt_r, par_row, par_col, gn, mixcat)


def _ssd_bwd(xbc, proj, dt_c, dt_r, par_row, par_col, gn, hprev, dcat, dproj, *, name):
    nh, w, gw, n = SSM_HPG, 2 * SSM_HEAD_DIM, SSM_GROUP_W, SSM_STATE
    rev = lambda c: N_CHUNKS - 1 - c

    def body(x_ref, z_ref, bm_ref, cm_ref, dtc_ref, dtr_ref, prow_ref, pcol_ref, gn_ref, hprev_ref, dy_ref,
             dproj_in, dz_ref, dxs_ref, dbm_ref, dcm_ref, ddtc_ref, ddtr_ref, dprow_ref, dpcol_ref, dgn_ref, dh_scr):
        del dproj_in
        first = pl.program_id(1) == 0

        @pl.when(first)
        def _():
            dh_scr[...] = jnp.zeros_like(dh_scr)
            for ref in (dprow_ref, dpcol_ref, dgn_ref):
                ref[...] = jnp.zeros_like(ref)

        args = _ssd_args(x_ref, z_ref, bm_ref, cm_ref, _pair_rows(hprev_ref), dtc_ref, dtr_ref, prow_ref, pcol_ref,
                         gn_ref)
        _, vjp = jax.vjp(lambda *a: _ssd_tile(*a, differentiable=True), *args)
        dxs, dzs, dbm, dcm, dhs, ddtc, ddtr, dbias, dbias_col, dalog, dalog_col, ddsk, dgn = vjp(
            (_split(dy_ref, SSM_PAIRS, w), _pair_rows(dh_scr)))
        dbm_ref[...] = dbm
        dcm_ref[...] = dcm
        ddtc_ref[...] = ddtc
        ddtr_ref[...] = ddtr
        for q in range(SSM_PAIRS):
            dxs_ref[:, q * w:(q + 1) * w] = dxs[q]
            dz_ref[:, q * w:(q + 1) * w] = dzs[q].astype(dz_ref.dtype)
            dh_scr[q * w:(q + 1) * w, :] = dhs[q]
            dgn_ref[:, q * w:(q + 1) * w] += dgn[q]
        for i, d in enumerate((dbias, dalog, ddsk)):
            dprow_ref[i:i + 1, :] += d
        for i, d in enumerate((dbias_col, dalog_col)):
            dpcol_ref[:, i:i + 1] += d

    return pl.pallas_call(
        body, grid=(SSM_GROUPS, N_CHUNKS),
        in_specs=[*_ssd_in_specs(rev),
                  pl.BlockSpec((None, None, gw, n), lambda g, c: (rev(c), g, 0, 0)),
                  pl.BlockSpec((CHUNK, gw), lambda g, c: (rev(c), g)),
                  pl.BlockSpec(memory_space=pl.ANY)],
        out_specs=[pl.BlockSpec((CHUNK, gw), lambda g, c: (rev(c), g)),
                   pl.BlockSpec((CHUNK, gw), lambda g, c: (rev(c), g)),
                   pl.BlockSpec((CHUNK, n), lambda g, c: (rev(c), g)),
                   pl.BlockSpec((CHUNK, n), lambda g, c: (rev(c), g)),
                   pl.BlockSpec((None, CHUNK, nh), lambda g, c: (g, rev(c), 0)),
                   pl.BlockSpec((None, nh, CHUNK), lambda g, c: (g, 0, rev(c))),
                   pl.BlockSpec((None, 3, nh), lambda g, c: (g, 0, 0)),
                   pl.BlockSpec((None, nh, 2), lambda g, c: (g, 0, 0)),
                   pl.BlockSpec((1, gw), lambda g, c: (0, g))],
        out_shape=[SDS(dproj.shape, dproj.dtype), SDS((SEQ, D_INNER), F32), SDS((SEQ, SSM_GROUPS * n), F32),
                   SDS((SEQ, SSM_GROUPS * n), F32), SDS((SSM_GROUPS, SEQ, nh), F32), SDS((SSM_GROUPS, nh, SEQ), F32),
                   SDS((SSM_GROUPS, 3, nh), F32), SDS((SSM_GROUPS, nh, 2), F32), SDS((1, D_INNER), F32)],
        scratch_shapes=[pltpu.VMEM((gw, n), F32)],
        input_output_aliases={11: 0}, compiler_params=_cparams(("parallel", "arbitrary")), name=name,
    )(xbc, proj, xbc, xbc, dt_c, dt_r, par_row, par_col, gn, hprev, dcat, dproj)


def _sum_contributions(chip, parts, landed, *, name):
    _, r, c = parts.shape
    tr = _pick(r, (256, 384, 128))

    def body(chip_ref, own_ref, landed_ref, o_ref):
        del chip_ref
        acc = own_ref[...].astype(F32)
        for s in range(landed_ref.shape[0]):
            acc = acc + landed_ref[s].astype(F32)
        o_ref[...] = acc

    grid_spec = pltpu.PrefetchScalarGridSpec(
        num_scalar_prefetch=1, grid=(r // tr,),
        in_specs=[pl.BlockSpec((None, tr, c), lambda i, chip_ref: (chip_ref[0], i, 0)),
                  pl.BlockSpec((landed.shape[0], tr, c), lambda i, chip_ref: (0, i, 0))],
        out_specs=pl.BlockSpec((tr, c), lambda i, chip_ref: (i, 0)))
    return pl.pallas_call(body, grid_spec=grid_spec, out_shape=SDS((r, c), F32),
                          compiler_params=_cparams(("parallel",)), name=name)(chip, parts, landed)


def _adamw(w, g, m, v, *, name):
    layers, r, c = w.shape
    tr = r if r <= 256 else _pick(r, (256, 128, 8))
    spec = pl.BlockSpec((None, tr, c), lambda l, i: (l, i, 0))

    def body(w_ref, g_ref, m_ref, v_ref, d_ref, mo_ref, vo_ref):
        g = g_ref[...]
        m_new = ADAM_B1 * m_ref[...] + (1.0 - ADAM_B1) * g
        v_new = ADAM_B2 * v_ref[...] + (1.0 - ADAM_B2) * (g * g)
        m_hat = m_new / (1.0 - ADAM_B1 ** ADAM_STEP)
        v_hat = v_new / (1.0 - ADAM_B2 ** ADAM_STEP)
        d_ref[...] = -ADAM_LR * (m_hat / (jnp.sqrt(v_hat) + ADAM_EPS) + ADAM_WD * w_ref[...])
        mo_ref[...] = m_new
        vo_ref[...] = v_new

    return pl.pallas_call(body, grid=(layers, r // tr), in_specs=[spec] * 4, out_specs=[spec] * 3,
                          out_shape=[SDS(w.shape, F32)] * 3, compiler_params=_cparams(("parallel", "parallel")),
                          name=name)(w, g, m, v)


ANY = pl.BlockSpec(memory_space=pl.ANY)


def _place():
    x, y, c = lax.axis_index("x"), lax.axis_index("y"), lax.axis_index("c")
    chips = [(1 - x, y), (x, 1 - y), (1 - x, 1 - y)]
    return x, y, c, chips


def _remote(src, dst, send_sem, recv_sem, to):
    return pltpu.make_async_remote_copy(src_ref=src, dst_ref=dst, send_sem=send_sem, recv_sem=recv_sem,
                                        device_id=to, device_id_type=MESH)


STREAM_ROWS = 256


def _stream_rows(i):
    return pl.ds(pl.multiple_of(i * STREAM_ROWS, STREAM_ROWS), STREAM_ROWS)


def _channel_scratch(width, dtype, rows=STREAM_ROWS):
    buf = (2, rows, width)
    return [pltpu.VMEM(buf, dtype), pltpu.VMEM(buf, dtype), *([pltpu.SemaphoreType.DMA((2,))] * 5),
            pltpu.SemaphoreType.REGULAR((2,))]


CHANNEL_REFS = 8


def _copy_blocks(srcs, dsts, ch):
    sbuf, _, ld, _, _, st, _, _ = ch
    n = len(srcs)
    load = lambda i: pltpu.make_async_copy(srcs[i], sbuf.at[i % 2], ld.at[i % 2])
    store = lambda i: pltpu.make_async_copy(sbuf.at[i % 2], dsts[i], st.at[i % 2])
    load(0).start()
    for i in range(n):
        if i + 1 < n:
            if i >= 1:
                store(i - 1).wait()
            load(i + 1).start()
        load(i).wait()
        store(i).start()
    for i in range(max(0, n - 2), n):
        store(i).wait()


def _exchange_block_streams(streams, sibling):
    plans = []
    for srcs, dsts, keeps, (sbuf, rbuf, ld, snd, rcv, st, kp, credit) in streams:
        n = len(srcs)

        def load(i, srcs=srcs, sbuf=sbuf, ld=ld):
            return pltpu.make_async_copy(srcs[i], sbuf.at[i % 2], ld.at[i % 2])

        def push(i, sbuf=sbuf, rbuf=rbuf, snd=snd, rcv=rcv):
            return _remote(sbuf.at[i % 2], rbuf.at[i % 2], snd.at[i % 2], rcv.at[i % 2], sibling)

        def store(i, rbuf=rbuf, dsts=dsts, st=st):
            return pltpu.make_async_copy(rbuf.at[i % 2], dsts[i], st.at[i % 2])

        def save(i, sbuf=sbuf, keeps=keeps, kp=kp):
            return pltpu.make_async_copy(sbuf.at[i % 2], keeps[i], kp.at[i % 2])

        def free_slot(i, n=n, store=store, credit=credit):
            if 1 <= i < n:
                store(i - 1).wait()
                if i + 1 < n:
                    pl.semaphore_signal(credit.at[(i + 1) % 2], 1, device_id=sibling, device_id_type=MESH)

        def send(i, n=n, load=load, push=push, save=save, keeps=keeps, credit=credit):
            if i < n:
                load(i).wait()
                pl.semaphore_wait(credit.at[i % 2], 1)
                push(i).start()
                if keeps[i] is not None:
                    save(i).start()

        def receive(i, n=n, load=load, push=push, store=store, save=save, keeps=keeps):
            if i < n:
                push(i).wait_recv()
                store(i).start()
                push(i).wait_send()
                if keeps[i] is not None:
                    save(i).wait()
                if i + 2 < n:
                    load(i + 2).start()

        for i in range(min(2, n)):
            pl.semaphore_signal(credit.at[i], 1, device_id=sibling, device_id_type=MESH)
            load(i).start()
        plans.append((n, free_slot, send, receive, store))
    for _, _, send, _, _ in plans:
        send(0)
    for i in range(max(p[0] for p in plans)):
        for _, free_slot, _, _, _ in plans:
            free_slot(i)
        for _, _, send, _, _ in plans:
            send(i + 1)
        for _, _, _, receive, _ in plans:
            receive(i)
    for n, _, _, _, store in plans:
        store(n - 1).wait()


def _all_gather_shards(shards, small, *, name):
    n = len(shards)

    def body(*refs):
        ins, outs = refs[:n + 1], refs[n + 1:2 * n + 2]
        scr = refs[2 * n + 2:]
        chans = [scr[CHANNEL_REFS * t:CHANNEL_REFS * (t + 1)] for t in range(n)]
        send_sems, recv_sems, small_sems = scr[CHANNEL_REFS * n:]
        x, y, c, _ = _place()
        me = 2 * x + y
        sibling = (x, y, 1 - c)
        near = (lax.rem(x + 1 - c, 2), lax.rem(y + c, 2))
        far = (lax.rem(x + c, 2), lax.rem(y + 1 - c, 2))
        k_near, k_far, k_diag = 2 * near[0] + near[1], 2 * far[0] + far[1], 3 - me
        targets = ((*near, c), (*far, c), (*far, c))
        arrives = (k_near, k_far, k_diag)
        streams_in = (k_far, k_near, k_diag)

        def ici(t, j, src, blk):
            return _remote(src, outs[t].at[blk, c], send_sems.at[3 * t + j], recv_sems.at[3 * t + j], targets[j])

        first = [ici(t, j, ins[t].at[c], me) for t in range(n + 1) for j in range(2)]
        for cp in first:
            cp.start()
        small_local = pltpu.make_async_copy(ins[n], outs[n].at[me], small_sems.at[6])
        small_local.start()
        for t in range(n):
            _copy_blocks([ins[t].at[h] for h in range(2)], [outs[t].at[me, h] for h in range(2)], chans[t])
        passed = []
        for j in range(3):
            for t in range(n + 1):
                landed = outs[t].at[arrives[j], c]
                ici(t, j, landed, arrives[j]).wait_recv()
                if j == 0:
                    fwd = ici(t, 2, landed, k_near)
                    fwd.start()
                    passed.append(fwd)
                if t < n:
                    _exchange_block_streams([([landed], [outs[t].at[streams_in[j], 1 - c]], [None], chans[t])], sibling)
                else:
                    fwd = _remote(landed, landed, small_sems.at[j], small_sems.at[3 + j], sibling)
                    fwd.start()
                    passed.append(fwd)
        for j in range(3):
            got = outs[n].at[streams_in[j], 1 - c]
            _remote(got, got, small_sems.at[j], small_sems.at[3 + j], sibling).wait_recv()
        for cp in first + passed:
            cp.wait_send()
        small_local.wait()

    scratch = []
    for s in shards:
        scratch += _channel_scratch(s.shape[2], s.dtype, rows=s.shape[1])
    return pl.pallas_call(
        body, in_specs=[ANY] * (n + 1), out_specs=[ANY] * (n + 1),
        out_shape=[SDS((N_CHIPS, *s.shape), s.dtype) for s in (*shards, small)],
        scratch_shapes=[*scratch, pltpu.SemaphoreType.DMA((3 * n + 3,)), pltpu.SemaphoreType.DMA((3 * n + 3,)),
                        pltpu.SemaphoreType.DMA((7,))],
        compiler_params=pltpu.CompilerParams(vmem_limit_bytes=VMEM_LIMIT), name=name)(*shards, small)


def _pair_reduce(stacks, *, name):
    n = len(stacks)
    per = 11

    def body(*refs):
        ins, outs, scr = refs[:n], refs[n:2 * n], refs[2 * n:]
        x, y, c, _ = _place()
        sibling = (x, y, 1 - c)
        streams = []
        for t in range(n):
            sraw, sbuf, rbuf, obuf, pbuf, ld_s, ld_o, snd, rcv, st, credit = scr[per * t:per * (t + 1)]
            steps = ins[t].shape[1] // STREAM_ROWS
            src, own, out = ins[t].at[1 - c], ins[t].at[c], outs[t]
            assert steps >= 2

            def load_s(i, slot, src=src, sraw=sraw, ld_s=ld_s):
                return pltpu.make_async_copy(src.at[_stream_rows(i)], sraw.at[slot], ld_s.at[slot])

            def load_o(i, slot, own=own, obuf=obuf, ld_o=ld_o):
                return pltpu.make_async_copy(own.at[_stream_rows(i)], obuf.at[slot], ld_o.at[slot])

            def push(slot, sbuf=sbuf, rbuf=rbuf, snd=snd, rcv=rcv):
                return _remote(sbuf.at[slot], rbuf.at[slot], snd.at[slot], rcv.at[slot], sibling)

            def store(i, slot, pbuf=pbuf, out=out, st=st):
                return pltpu.make_async_copy(pbuf.at[slot], out.at[_stream_rows(i)], st.at[slot])

            def send(i, slot, load_s=load_s, push=push, sraw=sraw, sbuf=sbuf, credit=credit):
                load_s(i, slot).wait()
                sbuf[slot] = sraw[slot].astype(sbuf.dtype)
                pl.semaphore_wait(credit.at[slot], 1)
                push(slot).start()

            def combine(i, slot, load_s=load_s, load_o=load_o, push=push, store=store, rbuf=rbuf, obuf=obuf, pbuf=pbuf,
                        credit=credit, steps=steps):
                load_o(i, slot).wait()
                push(slot).wait_recv()

                @pl.when(i >= 2)
                def _():
                    store(i, slot).wait()

                pbuf[slot] = (obuf[slot] + rbuf[slot].astype(F32)).astype(pbuf.dtype)
                store(i, slot).start()
                push(slot).wait_send()

                @pl.when(i + 2 < steps)
                def _():
                    load_s(i + 2, slot).start()
                    load_o(i + 2, slot).start()
                    pl.semaphore_signal(credit.at[slot], 1, device_id=sibling, device_id_type=MESH)

            for slot in range(2):
                pl.semaphore_signal(credit.at[slot], 1, device_id=sibling, device_id_type=MESH)
                load_s(slot, slot).start()
                load_o(slot, slot).start()
            streams.append((steps, send, combine, store))
        for _, send, _, _ in streams:
            send(0, 0)

        def step(i, carry):
            slot = lax.rem(i, 2)
            for steps, send, _, _ in streams:
                @pl.when(i + 1 < steps)
                def _(send=send):
                    send(i + 1, 1 - slot)
            for steps, _, combine, _ in streams:
                @pl.when(i < steps)
                def _(combine=combine):
                    combine(i, slot)
            return carry

        lax.fori_loop(0, max(s[0] for s in streams), step, 0)
        for _, _, _, store in streams:
            for slot in range(2):
                store(0, slot).wait()

    scratch = []
    for s in stacks:
        buf = (2, STREAM_ROWS, s.shape[2])
        scratch += [pltpu.VMEM(buf, F32), pltpu.VMEM(buf, BF16), pltpu.VMEM(buf, BF16), pltpu.VMEM(buf, F32),
                    pltpu.VMEM(buf, BF16), *([pltpu.SemaphoreType.DMA((2,))] * 5), pltpu.SemaphoreType.REGULAR((2,))]
    return pl.pallas_call(
        body, in_specs=[ANY] * n, out_specs=[ANY] * n, out_shape=[SDS(s.shape[1:], BF16) for s in stacks],
        scratch_shapes=scratch, compiler_params=pltpu.CompilerParams(vmem_limit_bytes=VMEM_LIMIT), name=name)(*stacks)


HBM_SPEC = pl.BlockSpec(memory_space=pltpu.HBM)
SEM_SPEC = pl.BlockSpec(memory_space=pltpu.SEMAPHORE)
SIDE_EFFECT = pltpu.SideEffectType.DATAFLOW_SIDE_EFFECTING


def _scatter_copies(ins, lands, send_sems, recv_sems):
    _, _, c, chips = _place()
    return [_remote(ins[t].at[2 * cx + cy], lands[t].at[j], send_sems.at[3 * t + j], recv_sems.at[3 * t + j],
                    (cx, cy, c)) for t in range(len(ins)) for j, (cx, cy) in enumerate(chips)]


def _chip_scatter_start(parts, *, name):
    n = len(parts)

    def body(*refs):
        ins, lands = refs[:n], refs[n:2 * n]
        send_sems, recv_sems, token = refs[2 * n], refs[2 * n + 1], refs[-1]
        for cp in _scatter_copies(ins, lands, send_sems, recv_sems):
            cp.start()
        token[...] = jnp.zeros_like(token)

    hbm = lambda a: pltpu.with_memory_space_constraint(a, pltpu.HBM)
    lands = [hbm(lax.empty((3, *p.shape[1:]), p.dtype)) for p in parts]
    thru = [pltpu.HBM(a.shape, a.dtype) for a in (*parts, *lands)]
    outs = pl.pallas_call(
        body, name=name,
        out_shape=(pltpu.SemaphoreType.DMA((3 * n,)), pltpu.SemaphoreType.DMA((3 * n,)), *thru, SDS((8, 128), F32)),
        in_specs=[HBM_SPEC] * (2 * n),
        out_specs=(SEM_SPEC, SEM_SPEC, *([HBM_SPEC] * (2 * n)), pl.BlockSpec(memory_space=pltpu.VMEM)),
        input_output_aliases={i: 2 + i for i in range(2 * n)},
        compiler_params=pltpu.CompilerParams(has_side_effects=SIDE_EFFECT),
    )(*[hbm(p) for p in parts], *lands)
    return outs[0], outs[1], outs[2:2 + n], outs[2 + n:2 + 2 * n], outs[-1]


def _chip_scatter_wait(send_sems, recv_sems, parts, lands, after, *, name):
    n = len(parts)

    def body(*refs):
        ins, lands_in = refs[:n], refs[n:2 * n]
        for cp in _scatter_copies(ins, lands_in, refs[2 * n], refs[2 * n + 1]):
            cp.wait_send()
            cp.wait_recv()

    outs = pl.pallas_call(
        body, name=name, out_shape=[pltpu.HBM(a.shape, a.dtype) for a in (*parts, *lands)],
        in_specs=[*([HBM_SPEC] * (2 * n)), SEM_SPEC, SEM_SPEC, *([ANY] * len(after))],
        out_specs=[HBM_SPEC] * (2 * n), input_output_aliases={i: i for i in range(2 * n)},
        compiler_params=pltpu.CompilerParams(has_side_effects=SIDE_EFFECT),
    )(*parts, *lands, send_sems, recv_sems, *after)
    return outs[:n], outs[n:]


def _gather_copies(shards, zones, send_sems, recv_sems):
    x, y, c, chips = _place()
    return [_remote(shards[t].at[c], zones[t].at[2 * x + y, c], send_sems.at[3 * t + j], recv_sems.at[3 * t + j],
                    (cx, cy, c)) for t in range(len(shards)) for j, (cx, cy) in enumerate(chips)]


def _gather_start(shards, after, *, name):
    n = len(shards)

    def body(*refs):
        ins, zones = refs[:n], refs[n:2 * n]
        send_sems, recv_sems, token = refs[2 * n + len(after)], refs[2 * n + len(after) + 1], refs[-1]
        for cp in _gather_copies(ins, zones, send_sems, recv_sems):
            cp.start()
        token[...] = jnp.zeros_like(token)

    hbm = lambda a: pltpu.with_memory_space_constraint(a, pltpu.HBM)
    zones = [hbm(lax.empty((N_CHIPS, *s.shape), s.dtype)) for s in shards]
    thru = [pltpu.HBM(a.shape, a.dtype) for a in (*shards, *zones)]
    outs = pl.pallas_call(
        body, name=name,
        out_shape=(pltpu.SemaphoreType.DMA((3 * n,)), pltpu.SemaphoreType.DMA((3 * n,)), *thru, SDS((8, 128), F32)),
        in_specs=[*([HBM_SPEC] * (2 * n)), *([ANY] * len(after))],
        out_specs=(SEM_SPEC, SEM_SPEC, *([HBM_SPEC] * (2 * n)), pl.BlockSpec(memory_space=pltpu.VMEM)),
        input_output_aliases={i: 2 + i for i in range(2 * n)},
        compiler_params=pltpu.CompilerParams(has_side_effects=SIDE_EFFECT),
    )(*[hbm(s) for s in shards], *zones, *after)
    return outs[0], outs[1], outs[2:2 + n], outs[2 + n:2 + 2 * n], outs[-1]


def _gather_wait(send_sems, recv_sems, shards, zones, after, *, name):
    n = len(shards)

    def body(*refs):
        for cp in _gather_copies(refs[:n], refs[n:2 * n], refs[2 * n], refs[2 * n + 1]):
            cp.wait_send()
            cp.wait_recv()

    outs = pl.pallas_call(
        body, name=name, out_shape=[pltpu.HBM(a.shape, a.dtype) for a in (*shards, *zones)],
        in_specs=[*([HBM_SPEC] * (2 * n)), SEM_SPEC, SEM_SPEC, *([ANY] * len(after))],
        out_specs=[HBM_SPEC] * (2 * n), input_output_aliases={i: i for i in range(2 * n)},
        compiler_params=pltpu.CompilerParams(has_side_effects=SIDE_EFFECT),
    )(*shards, *zones, send_sems, recv_sems, *after)
    return outs[:n], outs[n:]


def _gather_finish(shards, zones, *, name):
    n = len(shards)

    def body(*refs):
        ins, zones_in, outs, scr = refs[:n], refs[n:2 * n], refs[2 * n:3 * n], refs[3 * n:]
        x, y, c, chips = _place()
        me = 2 * x + y
        sibling = (x, y, 1 - c)
        others = [2 * cx + cy for cx, cy in chips]
        chans = [scr[CHANNEL_REFS * t:CHANNEL_REFS * (t + 1)] for t in range(n)]
        for t in range(n):
            _copy_blocks([ins[t].at[h] for h in range(2)], [outs[t].at[me, h] for h in range(2)], chans[t])
        _exchange_block_streams([([zones_in[t].at[k, c] for k in others], [outs[t].at[k, 1 - c] for k in others],
                                  [None] * len(others), chans[t]) for t in range(n)], sibling)

    scratch = []
    for s in shards:
        scratch += _channel_scratch(s.shape[2], s.dtype, rows=s.shape[1])
    return pl.pallas_call(
        body, in_specs=[ANY] * (2 * n), out_specs=[ANY] * n, out_shape=[SDS(z.shape, z.dtype) for z in zones],
        input_output_aliases={n + t: t for t in range(n)}, scratch_shapes=scratch,
        compiler_params=pltpu.CompilerParams(vmem_limit_bytes=VMEM_LIMIT), name=name)(*shards, *zones)


def _pair_share(groups, *, name):
    finals = [f for grp in groups for f in grp]
    n, n_out = len(finals), len(groups)

    def body(*refs):
        ins, outs, scr = refs[:n], refs[n:n + n_out], refs[n + n_out:]
        x, y, c, _ = _place()
        sibling = (x, y, 1 - c)
        t, streams = 0, []
        for o, grp in enumerate(groups):
            rows = grp[0].shape[0] // 2
            blocks = [(layer, pl.ds(b * rows, rows)) for layer in range(len(grp)) for b in range(2)]
            streams.append(([ins[t + layer].at[rs] for layer, rs in blocks],
                            [outs[o].at[layer, 1 - c, rs] for layer, rs in blocks],
                            [outs[o].at[layer, c, rs] for layer, rs in blocks],
                            scr[CHANNEL_REFS * o:CHANNEL_REFS * (o + 1)]))
            t += len(grp)
        _exchange_block_streams(streams, sibling)

    scratch = []
    for grp in groups:
        scratch += _channel_scratch(grp[0].shape[1], grp[0].dtype, rows=grp[0].shape[0] // 2)
    return pl.pallas_call(
        body, in_specs=[ANY] * n, out_specs=[ANY] * n_out,
        out_shape=[SDS((len(grp), 2, *grp[0].shape), grp[0].dtype) for grp in groups],
        scratch_shapes=scratch, compiler_params=pltpu.CompilerParams(vmem_limit_bytes=VMEM_LIMIT), name=name)(*finals)


def _all_reduce_small(v, *, name):
    rows, lanes = v.shape
    n_dev = 8

    def body(v_ref, o_ref, all_ref, send_sems, recv_sems, local_sem):
        x, y, c, chips = _place()
        me, sibling = (x, y, c), (x, y, 1 - c)

        def block(px, py, pc):
            return all_ref.at[4 * px + 2 * py + pc]

        def copy(k, blk, to, src=None):
            return _remote(block(*blk) if src is None else src, block(*blk), send_sems.at[k], recv_sems.at[k], to)

        mine = pltpu.make_async_copy(v_ref, block(*me), local_sem)
        mine.start()
        first = [copy(0, me, sibling, src=v_ref)]
        first += [copy(1 + j, me, (*chip, c), src=v_ref) for j, chip in enumerate(chips)]
        for cp in first:
            cp.start()
        passed = [copy(4 + j, (*chip, c), sibling) for j, chip in enumerate(chips)]
        for j, chip in enumerate(chips):
            copy(1 + j, (*chip, c), me).wait_recv()
            passed[j].start()
        copy(0, sibling, me).wait_recv()
        for j, chip in enumerate(chips):
            copy(4 + j, (*chip, 1 - c), me).wait_recv()
        for cp in first + passed:
            cp.wait_send()
        mine.wait()
        acc = all_ref[0]
        for k in range(1, n_dev):
            acc = acc + all_ref[k]
        o_ref[...] = acc

    vmem = pl.BlockSpec(memory_space=pltpu.VMEM)
    return pl.pallas_call(
        body, in_specs=[vmem], out_specs=vmem, out_shape=SDS((rows, lanes), F32),
        scratch_shapes=[pltpu.VMEM((n_dev, rows, lanes), F32), pltpu.SemaphoreType.DMA((7,)),
                        pltpu.SemaphoreType.DMA((7,)), pltpu.SemaphoreType.DMA],
        compiler_params=pltpu.CompilerParams(vmem_limit_bytes=VMEM_LIMIT), name=name)(v)


def _relu2_epilogue(acc):
    return acc, jnp.square(jnp.maximum(acc, 0.0))


def _res_epilogue(acc, res):
    return (acc + res,)


def _drelu2_epilogue(acc, pre):
    return (acc * (2.0 * jnp.maximum(pre.astype(F32), 0.0)),)


def _ffn_fwd(h, g, w1, w2, tag):
    f = _rms_fwd(h, g, name=f"ffn_norm_{tag}")
    pre, act = _mm_nn(f, w1, name=f"ffn1_{tag}", epilogue=_relu2_epilogue, n_out_dtypes=(BF16, BF16))
    h_out = _mm_nn(act, w2, name=f"ffn2_{tag}", extras=(h,), epilogue=_res_epilogue)
    return h_out, (f, pre, act)


def _ffn_bwd(dh, h, g, w1, w2, saved, layer, after=()):
    f, pre, act = saved
    dpre = _mm_nt(dh, w2, name=f"ffn2_dx_{layer}", out_dtype=BF16, extras=(pre,), epilogue=_drelu2_epilogue,
                  after=after)
    dw2 = _mm_tn_stacked(act, dh, name=f"ffn2_dw_{layer}", col_slots=False)
    df = _mm_nt(dpre, w1, name=f"ffn1_dx_{layer}")
    dw1 = _mm_tn_stacked(f, dpre, name=f"ffn1_dw_{layer}", col_slots=True)
    dh, dg = _rms_bwd(h, g, df, dh, name=f"ffn_norm_bwd_{layer}")
    return dh, dg, dw1, dw2


def _kv_fwd(mem, g, w_kv, tag):
    m = _rms_fwd(mem, g, name=f"mem_norm_{tag}")
    return m, _mm_nn(m, w_kv, name=f"kv_{tag}")


def _kv_bwd(mem, g, w_kv, m, dk, dv, layer):
    dkv = jnp.concatenate([dk, dv], axis=1)
    dw = _mm_tn_stacked(m, dkv, name=f"kv_dw_{layer}", col_slots=True)
    dm = _mm_nt(dkv, w_kv, name=f"kv_dx_{layer}")
    _, dg = _rms_bwd(mem, g, dm, dm, name=f"mem_norm_bwd_{layer}")
    return dw, dg


def _local_step(x, mem, target, p, after_layer1=None, after_ffn0=None, after_mixer0=None):
    row = lambda v: v.reshape(1, -1)
    g = {}

    h0 = x
    a0 = _rms_fwd(h0, row(p["norm_mix"][0]), name="mix_norm_0")
    proj_a = _mm_nn(a0, p["a_in"], name="a_in", after=p.get("after_start", ()))
    m0, kv0 = _kv_fwd(mem, row(p["mem_norm"][0]), p["w_kv"][0], "0")
    cat0 = _attn_fwd(proj_a, 2 * D_INNER, kv0, name="attn_0")
    bs_col = p["a_bs"].reshape(A_GROUPS, CHUNK, 1)
    cat0 = _gate_fwd(proj_a, p["a_ln_g"], p["a_ln_b"], p["a_ws"], bs_col, cat0, name="gate")
    h1 = _mm_nn(cat0, p["w_out"][0], name="out_0", extras=(h0,), epilogue=_res_epilogue)
    h2, ffn0 = _ffn_fwd(h1, row(p["norm_ffn"][0]), p["w_ffn1"][0], p["w_ffn2"][0], "0")

    if "layer1_mixer" in p:
        w_kv1, w_out1, b_in = p["layer1_mixer"](h2)
    else:
        w_kv1, w_out1, b_in = p["w_kv"][1], p["w_out"][1], p["b_in"]
    a1 = _rms_fwd(h2, row(p["norm_mix"][1]), name="mix_norm_1")
    proj_b = _mm_nn(a1, b_in, name="b_in")
    m1, kv1 = _kv_fwd(mem, row(p["mem_norm"][1]), w_kv1, "1")
    cat1 = _attn_fwd(proj_b, B_Q_OFF, kv1, name="attn_1")
    xbc = _conv_fwd(proj_b, p["b_conv_w"], p["b_conv_b"], name="conv")
    dt_raw = proj_b[:, B_DT_OFF:B_DT_OFF + SSM_HEADS].reshape(SEQ, SSM_GROUPS, SSM_HPG)
    dt_c = jnp.transpose(dt_raw, (1, 0, 2))
    dt_r = jnp.transpose(dt_raw, (1, 2, 0))
    per_head = lambda v: v.reshape(SSM_GROUPS, 1, SSM_HPG)
    par_row = jnp.concatenate([per_head(p["b_dt_bias"]), per_head(p["b_a_log"]), per_head(p["b_d"])], axis=1)
    ssd_par = (par_row, jnp.transpose(par_row[:, :2], (0, 2, 1)), p["b_gnorm"])
    cat1, hprev = _ssd_fwd(xbc, proj_b, dt_c, dt_r, *ssd_par, cat1, name="ssd")
    h3 = _mm_nn(cat1, w_out1, name="out_1", extras=(h2,), epilogue=_res_epilogue)
    w_ffn1_1, w_ffn2_1 = p["layer1_ffn"](h3) if "layer1_ffn" in p else (p["w_ffn1"][1], p["w_ffn2"][1])
    h4, ffn1 = _ffn_fwd(h3, row(p["norm_ffn"][1]), w_ffn1_1, w_ffn2_1, "1")

    loss, dh, g["final_norm"] = _loss_head(h4, row(p["final_norm"]), target, name="loss_head")

    dh, dnf1, dw1_1, dw2_1 = _ffn_bwd(dh, h3, row(p["norm_ffn"][1]), w_ffn1_1, w_ffn2_1, ffn1, 1)
    dcat1 = _mm_nt(dh, w_out1, name="out_dx_1")
    dwo_1 = _mm_tn_stacked(cat1, dh, name="out_dw_1", col_slots=False)
    dproj_b, dk1, dv1 = _attn_bwd(proj_b, B_Q_OFF, kv1, dcat1, B_IN_PAD, B_Q_OFF, name="attn_bwd_1")
    dproj_b, dxs, dbm, dcm, ddt_c, ddt_r, dpar_row, dpar_col, g["b_gnorm"] = _ssd_bwd(
        xbc, proj_b, dt_c, dt_r, *ssd_par, hprev, dcat1, dproj_b, name="ssd_bwd")
    dpar = dpar_row.at[:, :2].add(jnp.transpose(dpar_col, (0, 2, 1)))
    g["b_dt_bias"], g["b_a_log"], g["b_d"] = dpar[:, 0], dpar[:, 1], dpar[:, 2]
    dproj_b, g["b_conv_w"], g["b_conv_b"] = _conv_bwd(proj_b, p["b_conv_w"], p["b_conv_b"], dxs, dbm, dcm, dproj_b,
                                                      name="conv_bwd")
    ddt = jnp.transpose(ddt_c, (1, 0, 2)) + jnp.transpose(ddt_r, (2, 0, 1))
    ddt = jnp.pad(ddt.reshape(SEQ, SSM_HEADS), ((0, 0), (0, B_IN_PAD - B_DT_OFF - SSM_HEADS))).astype(BF16)
    dproj_b = lax.dynamic_update_slice(dproj_b, ddt, (0, B_DT_OFF))
    dwkv_1, dmn1 = _kv_bwd(mem, row(p["mem_norm"][1]), w_kv1, m1, dk1, dv1, 1)
    dwb = _b_in_grad_slots(_mm_tn(a1, dproj_b, name="b_in_dw"))
    da1 = _mm_nt(dproj_b, b_in, name="b_in_dx")
    dh, dnm1 = _rms_bwd(h2, row(p["norm_mix"][1]), da1, dh, name="mix_norm_bwd_1")
    layer1 = dict(w_kv=dwkv_1, w_out=dwo_1, w_ffn1=dw1_1, w_ffn2=dw2_1, b_in=dwb)
    token = () if after_layer1 is None else (after_layer1(layer1),)

    dh, dnf0, dw1_0, dw2_0 = _ffn_bwd(dh, h1, row(p["norm_ffn"][0]), p["w_ffn1"][0], p["w_ffn2"][0], ffn0, 0,
                                      after=token)
    ffn0_grads = dict(w_ffn1=dw1_0, w_ffn2=dw2_0)
    token = () if after_ffn0 is None else (after_ffn0(ffn0_grads),)
    dcat0 = _mm_nt(dh, p["w_out"][0], name="out_dx_0", after=token)
    dwo_0 = _mm_tn_stacked(cat0, dh, name="out_dw_0", col_slots=False)
    dproj_a, dk0, dv0 = _attn_bwd(proj_a, 2 * D_INNER, kv0, dcat0, A_IN, 2 * D_INNER, name="attn_bwd_0")
    dproj_a, g["a_ln_g"], g["a_ln_b"], g["a_ws"], dbs_col = _gate_bwd(
        proj_a, p["a_ln_g"], p["a_ln_b"], p["a_ws"], bs_col, dcat0, dproj_a, name="gate_bwd")
    g["a_bs"] = dbs_col.reshape(A_GROUPS, CHUNK)
    dwkv_0, dmn0 = _kv_bwd(mem, row(p["mem_norm"][0]), p["w_kv"][0], m0, dk0, dv0, 0)
    dwa = _mm_tn_stacked(a0, dproj_a, name="a_in_dw", col_slots=True)
    mixer0_grads = dict(w_kv=dwkv_0, w_out=dwo_0, a_in=dwa)
    token = () if after_mixer0 is None else (after_mixer0(mixer0_grads),)
    da0 = _mm_nt(dproj_a, p["a_in"], name="a_in_dx", after=token)
    dx, dnm0 = _rms_bwd(h0, row(p["norm_mix"][0]), da0, dh, name="mix_norm_bwd_0")

    g["norm_mix"] = jnp.concatenate([dnm0, dnm1], axis=0)
    g["norm_ffn"] = jnp.concatenate([dnf0, dnf1], axis=0)
    g["mem_norm"] = jnp.concatenate([dmn0, dmn1], axis=0)
    layer0 = dict(w_kv=dwkv_0, w_out=dwo_0, w_ffn1=dw1_0, w_ffn2=dw2_0, a_in=dwa)
    return loss, dx, g, layer0, layer1


def _b_in_full(gathered):
    n = B_IN // N_CHIPS
    dt0 = D_INNER + CONV_DIM - (N_CHIPS - 1) * n
    last = gathered[N_CHIPS - 1]
    return jnp.concatenate([*[gathered[k] for k in range(N_CHIPS - 1)], last[:, :dt0], last[:, dt0 + SSM_HEADS:],
                            last[:, dt0:dt0 + SSM_HEADS], jnp.zeros((D_MODEL, B_IN_PAD - B_IN), last.dtype)], axis=1)


def _b_in_grad_slots(d):
    n = B_IN // N_CHIPS
    dt0 = D_INNER + CONV_DIM
    last = jnp.concatenate([d[:, (N_CHIPS - 1) * n:dt0], d[:, B_DT_OFF:B_DT_OFF + SSM_HEADS], d[:, dt0:B_DT_OFF]], axis=1)
    slots = [*[d[:, k * n:(k + 1) * n] for k in range(N_CHIPS - 1)], last]
    half = D_MODEL // 2
    return jnp.stack([jnp.stack([s[h * half:(h + 1) * half] for s in slots]) for h in range(2)])


LARGE = ("w_kv", "w_out", "w_ffn1", "w_ffn2", "a_in", "b_in")
SMALL_REPL = ("norm_mix", "norm_ffn", "mem_norm", "a_ln_g", "a_ln_b", "a_ws", "a_bs", "b_dt_bias", "b_a_log", "b_d",
              "final_norm")
SMALL_SHARD = ("b_conv_w", "b_conv_b", "b_gnorm")
WEIGHTS = ("norm_mix", "norm_ffn", "mem_norm", "w_kv", "w_out", "w_ffn1", "w_ffn2", "a_in", "a_ln_g", "a_ln_b", "a_ws",
           "a_bs", "b_in", "b_conv_w", "b_conv_b", "b_dt_bias", "b_a_log", "b_d", "b_gnorm", "final_norm")
CONV_SHARD = CONV_DIM // N_CHIPS
GN_SHARD = D_INNER // N_CHIPS


LAYERED = ("w_kv", "w_out", "w_ffn1", "w_ffn2")
LAYER_TENSORS = (("w_kv", "w_out", "w_ffn1", "w_ffn2", "a_in"), ("w_kv", "w_out", "w_ffn1", "w_ffn2", "b_in"))


def _gather_weights(w):
    halves = lambda k, layer: (w[k][layer] if k in LAYERED else w[k][0]).reshape(2, -1, w[k].shape[-1]).astype(BF16)
    small = jnp.zeros((2, CONV_K, CONV_SHARD), F32)
    small = small.at[0].set(w["b_conv_w"][0])
    small = small.at[1, 0].set(w["b_conv_b"][0])
    small = small.at[1, 1, :GN_SHARD].set(w["b_gnorm"][0])
    gathered = _all_gather_shards([halves(k, 0) for k in LAYER_TENSORS[0]], small, name="gather_weights_0")
    got = dict(zip(LAYER_TENSORS[0], gathered))
    slots = lambda a: a.reshape(N_CHIPS, -1, a.shape[-1])
    rows = lambda a: a.reshape(-1, a.shape[-1])
    p = dict(w_kv=[slots(got["w_kv"])], w_out=[rows(got["w_out"])], w_ffn1=[slots(got["w_ffn1"])],
             w_ffn2=[rows(got["w_ffn2"])], a_in=slots(got["a_in"]))
    sm = gathered[-1]
    p["b_conv_w"] = jnp.transpose(sm[:, 0], (1, 0, 2)).reshape(CONV_K, CONV_DIM)
    p["b_conv_b"] = sm[:, 1, 0].reshape(1, CONV_DIM)
    p["b_gnorm"] = sm[:, 1, 1, :GN_SHARD].reshape(1, D_INNER)

    after, started = (gathered[0],), {}
    for tag, names in (("mixer", ("w_kv", "w_out", "b_in")), ("ffn", ("w_ffn1", "w_ffn2"))):
        started[tag] = _gather_start([halves(k, 1) for k in names], after, name=f"gather_start_1_{tag}")
        after = (started[tag][-1],)
    p["after_start"] = after

    def finish(tag, first):
        send_sems, recv_sems, shards, zones, _ = started[tag]
        shards, zones = _gather_wait(send_sems, recv_sems, shards, zones, (first,), name=f"gather_wait_1_{tag}")
        return _gather_finish(shards, zones, name=f"gather_finish_1_{tag}")

    def layer1_mixer(first):
        kv, wo, b_in = finish("mixer", first)
        return slots(kv), rows(wo), _b_in_full(slots(b_in))

    def layer1_ffn(first):
        w1, w2 = finish("ffn", first)
        return slots(w1), rows(w2)

    p.update(layer1_mixer=layer1_mixer, layer1_ffn=layer1_ffn)
    return p


def _pair_parts(grads, tag):
    stacks = [g.reshape(2, -1, g.shape[-1]) for g in grads.values()]
    parts = _pair_reduce(stacks, name=f"grads_pair_reduce_{tag}")
    return [t.reshape(N_CHIPS, -1, t.shape[-1]) for t in parts]


def _chip_sums(chip, names, parts, landed, tag):
    return {k: _sum_contributions(chip, t, u, name=f"grads_chip_sum_{k}_{tag}")
            for k, t, u in zip(names, parts, landed)}


def _small_layout(shapes):
    offs, o = {}, 0
    for k in (*SMALL_REPL, *SMALL_SHARD):
        size = math.prod(shapes[k])
        offs[k] = (o, size)
        o += size
    rows = -(-o // (8 * 128)) * 8
    return offs, rows


def _reduce_small(g, full_shapes):
    offs, rows = _small_layout(full_shapes)
    flat = jnp.concatenate([g[k].reshape(-1) for k in (*SMALL_REPL, *SMALL_SHARD)])
    flat = jnp.pad(flat, (0, rows * 128 - flat.shape[0])).reshape(rows, 128)
    total = _all_reduce_small(flat, name="grads_small_all_reduce").reshape(-1)
    return {k: total[o:o + n].reshape(full_shapes[k]) for k, (o, n) in offs.items()}


def kernel(x, mem, norm_mix, norm_ffn, mem_norm, w_kv, w_out, w_ffn1, w_ffn2, a_in, a_ln_g, a_ln_b, a_ws, a_bs, b_in, b_conv_w, b_conv_b, b_dt_bias, b_a_log, b_d, b_gnorm, final_norm, loss_target, m_norm_mix, m_norm_ffn, m_mem_norm, m_w_kv, m_w_out, m_w_ffn1, m_w_ffn2, m_a_in, m_a_ln_g, m_a_ln_b, m_a_ws, m_a_bs, m_b_in, m_b_conv_w, m_b_conv_b, m_b_dt_bias, m_b_a_log, m_b_d, m_b_gnorm, m_final_norm, v_norm_mix, v_norm_ffn, v_mem_norm, v_w_kv, v_w_out, v_w_ffn1, v_w_ffn2, v_a_in, v_a_ln_g, v_a_ln_b, v_a_ws, v_a_bs, v_b_in, v_b_conv_w, v_b_conv_b, v_b_dt_bias, v_b_a_log, v_b_d, v_b_gnorm, v_final_norm):
    w = dict(norm_mix=norm_mix, norm_ffn=norm_ffn, mem_norm=mem_norm, w_kv=w_kv, w_out=w_out, w_ffn1=w_ffn1,
             w_ffn2=w_ffn2, a_in=a_in, a_ln_g=a_ln_g, a_ln_b=a_ln_b, a_ws=a_ws, a_bs=a_bs, b_in=b_in, b_conv_w=b_conv_w,
             b_conv_b=b_conv_b, b_dt_bias=b_dt_bias, b_a_log=b_a_log, b_d=b_d, b_gnorm=b_gnorm, final_norm=final_norm)
    mom = dict(norm_mix=m_norm_mix, norm_ffn=m_norm_ffn, mem_norm=m_mem_norm, w_kv=m_w_kv, w_out=m_w_out,
               w_ffn1=m_w_ffn1, w_ffn2=m_w_ffn2, a_in=m_a_in, a_ln_g=m_a_ln_g, a_ln_b=m_a_ln_b, a_ws=m_a_ws,
               a_bs=m_a_bs, b_in=m_b_in, b_conv_w=m_b_conv_w, b_conv_b=m_b_conv_b, b_dt_bias=m_b_dt_bias,
               b_a_log=m_b_a_log, b_d=m_b_d, b_gnorm=m_b_gnorm, final_norm=m_final_norm)
    var = dict(norm_mix=v_norm_mix, norm_ffn=v_norm_ffn, mem_norm=v_mem_norm, w_kv=v_w_kv, w_out=v_w_out,
               w_ffn1=v_w_ffn1, w_ffn2=v_w_ffn2, a_in=v_a_in, a_ln_g=v_a_ln_g, a_ln_b=v_a_ln_b, a_ws=v_a_ws,
               a_bs=v_a_bs, b_in=v_b_in, b_conv_w=v_b_conv_w, b_conv_b=v_b_conv_b, b_dt_bias=v_b_dt_bias,
               b_a_log=v_b_a_log, b_d=v_b_d, b_gnorm=v_b_gnorm, final_norm=v_final_norm)

    p = _gather_weights(w)
    p.update(norm_mix=norm_mix, norm_ffn=norm_ffn, mem_norm=mem_norm, a_ln_g=a_ln_g, a_ln_b=a_ln_b, a_ws=a_ws[0],
             a_bs=a_bs[0], b_dt_bias=b_dt_bias, b_a_log=b_a_log, b_d=b_d, final_norm=final_norm)
    chip = 2 * lax.axis_index("x") + lax.axis_index("y")
    chip_arr = jnp.reshape(chip, (1,)).astype(jnp.int32)
    started = {}

    def start_scatter(tag):
        def hook(grads):
            start = _chip_scatter_start(_pair_parts(grads, tag), name=f"grads_chip_scatter_start_{tag}")
            started[tag] = (tuple(grads), start)
            return start[-1]
        return hook

    loss_part, dx, g, _, _ = _local_step(x[0], mem[0], loss_target[0], p, start_scatter("1"), start_scatter("0f"),
                                         start_scatter("0m"))
    loss = lax.psum(loss_part[0, 0], ("x", "y", "c"))

    def finish_scatter(tag, first):
        names, (send_sems, recv_sems, parts, lands, _) = started[tag]
        parts, landed = _chip_scatter_wait(send_sems, recv_sems, parts, lands, (first,),
                                           name=f"grads_chip_scatter_wait_{tag}")
        return _chip_sums(chip_arr, names, parts, landed, tag)

    def adamw(names, grads):
        for k in names:
            shape = w[k].shape
            if len(shape) == 3 and shape[2] % 128 and not shape[1] % 128:
                flat = unflat = lambda a: jnp.transpose(a, (0, 2, 1))
            else:
                flat = (lambda a: a) if len(shape) == 3 else (lambda a: a.reshape(1, -1, shape[-1]))
                unflat = lambda a: a.reshape(shape)
            d, m_new, v_new = _adamw(flat(w[k]), flat(grads[k]), flat(mom[k]), flat(var[k]), name=f"adamw_{k}")
            delta[k], new_m[k], new_v[k] = unflat(d), unflat(m_new), unflat(v_new)

    full_shapes = {k: w[k].shape for k in SMALL_REPL}
    full_shapes.update(b_conv_w=(1, CONV_K, CONV_DIM), b_conv_b=(1, CONV_DIM), b_gnorm=(1, D_INNER))
    grads = _reduce_small(g, full_shapes)
    grads["b_conv_w"] = lax.dynamic_slice_in_dim(grads["b_conv_w"], chip * CONV_SHARD, CONV_SHARD, axis=2)
    grads["b_conv_b"] = lax.dynamic_slice_in_dim(grads["b_conv_b"], chip * CONV_SHARD, CONV_SHARD, axis=1)
    grads["b_gnorm"] = lax.dynamic_slice_in_dim(grads["b_gnorm"], chip * GN_SHARD, GN_SHARD, axis=1)
    delta, new_m, new_v = {}, {}, {}
    halves = [finish_scatter("0f", dx), finish_scatter("1", dx)]
    early = ("w_ffn1", "w_ffn2", "b_in")
    shared = _pair_share([[halves[layer][k] for layer in range(2) if k in halves[layer]] for k in early],
                         name="grads_pair_share_early")
    grads.update({k: a.reshape(w[k].shape) for k, a in zip(early, shared)})
    adamw([k for k in WEIGHTS if k in grads], grads)
    halves[0].update(finish_scatter("0m", delta["w_ffn2"]))
    late = ("w_kv", "w_out", "a_in")
    shared = _pair_share([[halves[layer][k] for layer in range(2) if k in halves[layer]] for k in late],
                         name="grads_pair_share_late")
    grads.update({k: a.reshape(w[k].shape) for k, a in zip(late, shared)})
    adamw(late, grads)

    return (loss, dx.reshape(x.shape), *[grads[k] for k in WEIGHTS], *[delta[k] for k in WEIGHTS],
            *[new_m[k] for k in WEIGHTS], *[new_v[k] for k in WEIGHTS])
```

```python
import math

import jax
import jax.numpy as jnp
from jax import lax
from jax.experimental import pallas as pl
from jax.experimental.pallas import tpu as pltpu

F32 = jnp.float32
BF16 = jnp.bfloat16
SDS = jax.ShapeDtypeStruct

D_MODEL = 1024
SEQ = 2048
CHUNK = 128
N_MEM = 256
D_INNER = 2048
A_GROUPS = 8
A_GROUP_W = D_INNER // A_GROUPS
SSM_HEADS = 32
SSM_HEAD_DIM = 64
SSM_GROUPS = 4
SSM_HPG = 8
SSM_STATE = 128
SSM_GROUP_W = SSM_HPG * SSM_HEAD_DIM
CONV_K = 4
CONV_DIM = 3072
X_HEADS = 4
X_HEAD_DIM = 256
X_WIDTH = 1024
MIX_OUT = 3072
D_FF = 4096
A_IN = 5120
B_IN = 6176
B_IN_PAD = 6272
B_Q_OFF = 5120
B_DT_OFF = 6144
N_CHUNKS = SEQ // CHUNK
EPS = 1e-6
N_CHIPS = 4

ADAM_LR = 0.001
ADAM_B1 = 0.9
ADAM_B2 = 0.999
ADAM_EPS = 1e-08
ADAM_WD = 0.01
ADAM_STEP = 10

VMEM_LIMIT = 48 * 1024 * 1024
MESH = pl.DeviceIdType.MESH


def _cparams(sem):
    return pltpu.CompilerParams(dimension_semantics=sem, vmem_limit_bytes=VMEM_LIMIT)


def _dot(a, b, dims=(((1,), (0,)), ((), ()))):
    return lax.dot_general(a.astype(BF16), b.astype(BF16), dims, preferred_element_type=F32)


def _dot_nt(a, b):
    return _dot(a, b, (((1,), (1,)), ((), ())))


def _dot_tn(a, b):
    return _dot(a, b, (((0,), (0,)), ((), ())))


def _pick(n, cands):
    for c in cands:
        if n % c == 0:
            return c
    raise ValueError(f"no tile for {n}")


def _mm_call(a, b, *, dims, grid, a_spec, b_spec, acc_shape, out_shapes, out_specs, name,
             extras=(), extra_specs=(), epilogue=None, after=()):
    n_k = grid[2]
    n_extra = len(extras)
    n_out = len(out_shapes)
    n_in = 2 + n_extra + len(after)

    def finish(total, extra_refs, out_refs):
        vals = (total,) if epilogue is None else epilogue(total, *[e[...] for e in extra_refs])
        for o_ref, v in zip(out_refs, vals):
            o_ref[...] = v.astype(o_ref.dtype)

    def body_one_step(*refs):
        finish(_dot(refs[0][...], refs[1][...], dims), refs[2:2 + n_extra], refs[n_in:n_in + n_out])

    def body(*refs):
        acc = refs[-1]
        k = pl.program_id(2)

        @pl.when(k == 0)
        def _():
            acc[...] = jnp.zeros_like(acc)

        acc[...] += _dot(refs[0][...], refs[1][...], dims)

        @pl.when(k == n_k - 1)
        def _():
            finish(acc[...], refs[2:2 + n_extra], refs[n_in:n_in + n_out])

    return pl.pallas_call(
        body_one_step if n_k == 1 else body, grid=grid,
        in_specs=[a_spec, b_spec, *extra_specs, *([ANY] * len(after))], out_specs=list(out_specs),
        out_shape=list(out_shapes), scratch_shapes=[] if n_k == 1 else [pltpu.VMEM(acc_shape, F32)],
        compiler_params=_cparams(("parallel", "parallel", "arbitrary")), name=name,
    )(a, b, *extras, *after)


def _w_dims(w):
    if w.ndim == 2:
        return w.shape[0], w.shape[1], 1, w.shape[1]
    return w.shape[1], w.shape[0] * w.shape[2], w.shape[0], w.shape[2]


def _mm_nn(a, w, *, name, out_dtype=F32, a_cols=None, extras=(), epilogue=None, n_out_dtypes=None, after=()):
    m = a.shape[0]
    k_dim, n_dim, _, n_slot = _w_dims(w)
    a_off, a_w = (0, a.shape[1]) if a_cols is None else a_cols
    assert a_w == k_dim
    tm = _pick(m, (2048, 1024, 512, 256))
    tn = _pick(n_slot, (512, 896, 640, 256, 128))
    tk = _pick(k_dim, (1024, 768, 512, 384, 256, 128))
    assert a_off % tk == 0
    nb = n_slot // tn
    a_spec = pl.BlockSpec((tm, tk), lambda i, j, k: (i, a_off // tk + k))
    if w.ndim == 2:
        b_spec = pl.BlockSpec((tk, tn), lambda i, j, k: (k, j))
    else:
        b_spec = pl.BlockSpec((None, tk, tn), lambda i, j, k: (j // nb, k, j % nb))
    o_spec = pl.BlockSpec((tm, tn), lambda i, j, k: (i, j))
    dts = n_out_dtypes or (out_dtype,)
    outs = _mm_call(a, w, dims=(((1,), (0,)), ((), ())), grid=(m // tm, n_dim // tn, k_dim // tk),
                    a_spec=a_spec, b_spec=b_spec, acc_shape=(tm, tn),
                    out_shapes=[SDS((m, n_dim), dt) for dt in dts], out_specs=[o_spec] * len(dts), name=name,
                    extras=extras, extra_specs=[o_spec] * len(extras), epilogue=epilogue, after=after)
    return outs if n_out_dtypes else outs[0]


def _mm_nt(a, w, *, name, out_dtype=F32, extras=(), epilogue=None, after=()):
    m = a.shape[0]
    k_dim, n_dim, _, n_slot = _w_dims(w)
    assert a.shape[1] == n_dim
    tm = _pick(m, (2048, 1024, 512, 256))
    to = _pick(k_dim, (512, 384, 256, 128))
    tc = _pick(n_slot, (1280, 1024, 896, 640, 512, 256, 128))
    nb = n_slot // tc
    a_spec = pl.BlockSpec((tm, tc), lambda i, j, k: (i, k))
    if w.ndim == 2:
        b_spec = pl.BlockSpec((to, tc), lambda i, j, k: (j, k))
    else:
        b_spec = pl.BlockSpec((None, to, tc), lambda i, j, k: (k // nb, j, k % nb))
    o_spec = pl.BlockSpec((tm, to), lambda i, j, k: (i, j))
    return _mm_call(a, w, dims=(((1,), (1,)), ((), ())), grid=(m // tm, k_dim // to, n_dim // tc),
                    a_spec=a_spec, b_spec=b_spec, acc_shape=(tm, to),
                    out_shapes=[SDS((m, k_dim), out_dtype)], out_specs=[o_spec], name=name,
                    extras=extras, extra_specs=[o_spec] * len(extras), epilogue=epilogue, after=after)[0]


def _mm_tn(x, dy, *, name, x_cols=None):
    s = x.shape[0]
    x_off, k_dim = (0, x.shape[1]) if x_cols is None else x_cols
    n_dim = dy.shape[1]
    tm = _pick(k_dim, (1024, 768, 512, 384, 256, 128))
    tn = _pick(n_dim, (512, 896, 640, 256, 128))
    tk = _pick(s, (2048, 1024, 512, 256))
    assert x_off % tm == 0
    a_spec = pl.BlockSpec((tk, tm), lambda i, j, k: (k, x_off // tm + i))
    b_spec = pl.BlockSpec((tk, tn), lambda i, j, k: (k, j))
    o_spec = pl.BlockSpec((tm, tn), lambda i, j, k: (i, j))
    return _mm_call(x, dy, dims=(((0,), (0,)), ((), ())), grid=(k_dim // tm, n_dim // tn, s // tk),
                    a_spec=a_spec, b_spec=b_spec, acc_shape=(tm, tn),
                    out_shapes=[SDS((k_dim, n_dim), F32)], out_specs=[o_spec], name=name)[0]


def _mm_tn_stacked(x, dy, *, name, col_slots):
    s, k_dim = x.shape
    n_dim = dy.shape[1]
    r, c = (k_dim // 2, n_dim // N_CHIPS) if col_slots else (k_dim // N_CHIPS // 2, n_dim)
    tm = 2 * r
    tn = _pick(c, (512, 896, 640, 256, 128))
    tk = _pick(s, (2048, 1024, 512, 256))
    a_spec = pl.BlockSpec((tk, tm), lambda i, j, k: (k, i))
    b_spec = pl.BlockSpec((tk, tn), lambda i, j, k: (k, j))
    if col_slots:
        nb = c // tn
        o_spec = pl.BlockSpec((2, None, r, tn), lambda i, j, k: (0, j // nb, 0, j % nb))
    else:
        o_spec = pl.BlockSpec((2, None, r, tn), lambda i, j, k: (0, i, 0, j))
    return _mm_call(x, dy, dims=(((0,), (0,)), ((), ())), grid=(k_dim // tm, n_dim // tn, s // tk),
                    a_spec=a_spec, b_spec=b_spec, acc_shape=(tm, tn), epilogue=lambda acc: (acc.reshape(2, r, tn),),
                    out_shapes=[SDS((2, N_CHIPS, r, c), F32)], out_specs=[o_spec], name=name)[0]


def _rms(x, g):
    return x * lax.rsqrt(jnp.mean(x * x, axis=-1, keepdims=True) + EPS) * g


def _rms_fwd(h, g, *, name):
    rows, d = h.shape
    tr = _pick(rows, (512, 256))

    def body(h_ref, g_ref, o_ref):
        o_ref[...] = _rms(h_ref[...], g_ref[...]).astype(o_ref.dtype)

    return pl.pallas_call(
        body, grid=(rows // tr,),
        in_specs=[pl.BlockSpec((tr, d), lambda i: (i, 0)), pl.BlockSpec((1, d), lambda i: (0, 0))],
        out_specs=pl.BlockSpec((tr, d), lambda i: (i, 0)), out_shape=SDS((rows, d), BF16),
        compiler_params=_cparams(("parallel",)), name=name)(h, g)


def _rms_bwd(h, g, da, dres, *, name):
    rows, d = h.shape
    tr = _pick(rows, (512, 256))

    def body(h_ref, g_ref, da_ref, dres_ref, dh_ref, dg_ref):
        _, vjp = jax.vjp(_rms, h_ref[...], g_ref[...])
        dh, dg = vjp(da_ref[...].astype(F32))
        dh_ref[...] = dres_ref[...] + dh

        @pl.when(pl.program_id(0) == 0)
        def _():
            dg_ref[...] = jnp.zeros_like(dg_ref)

        dg_ref[...] += dg

    row_spec = pl.BlockSpec((tr, d), lambda i: (i, 0))
    vec_spec = pl.BlockSpec((1, d), lambda i: (0, 0))
    return pl.pallas_call(
        body, grid=(rows // tr,), in_specs=[row_spec, vec_spec, row_spec, row_spec],
        out_specs=[row_spec, vec_spec], out_shape=[SDS((rows, d), F32), SDS((1, d), F32)],
        compiler_params=_cparams(("arbitrary",)), name=name)(h, g, da, dres)


def _loss_head(h, g, target, *, name):
    rows, d = h.shape
    tr = _pick(rows, (512, 256))

    def body(h_ref, g_ref, t_ref, loss_ref, dh_ref, dg_ref):
        y, vjp = jax.vjp(_rms, h_ref[...], g_ref[...])
        err = y - t_ref[...]
        dh, dg = vjp(err * (1.0 / d))
        dh_ref[...] = dh

        @pl.when(pl.program_id(0) == 0)
        def _():
            dg_ref[...] = jnp.zeros_like(dg_ref)
            loss_ref[...] = jnp.zeros_like(loss_ref)

        dg_ref[...] += dg
        part = jnp.sum(jnp.sum(err * err, axis=-1, keepdims=True), axis=0, keepdims=True) * (0.5 / d)
        loss_ref[...] += jnp.broadcast_to(part, loss_ref.shape)

    row_spec = pl.BlockSpec((tr, d), lambda i: (i, 0))
    vec_spec = pl.BlockSpec((1, d), lambda i: (0, 0))
    loss_spec = pl.BlockSpec((8, 128), lambda i: (0, 0))
    return pl.pallas_call(
        body, grid=(rows // tr,), in_specs=[row_spec, vec_spec, row_spec],
        out_specs=[loss_spec, row_spec, vec_spec],
        out_shape=[SDS((8, 128), F32), SDS((rows, d), F32), SDS((1, d), F32)],
        compiler_params=_cparams(("arbitrary",)), name=name)(h, g, target)


def _gelu(x):
    return 0.5 * x * (1.0 + lax.erf(x * (1.0 / math.sqrt(2.0))))


def _gate_tile(pu, pv, ln_g, ln_b, ws, bs_t):
    u = [_gelu(p) for p in pu]
    v = [_gelu(p) for p in pv]
    mu = sum(jnp.sum(t, axis=-1, keepdims=True) for t in v) * (1.0 / D_INNER)
    vc = [t - mu for t in v]
    var = sum(jnp.sum(t * t, axis=-1, keepdims=True) for t in vc) * (1.0 / D_INNER)
    rstd = lax.rsqrt(var + EPS)
    row = lax.broadcasted_iota(jnp.int32, (CHUNK, CHUNK), 0)
    col = lax.broadcasted_iota(jnp.int32, (CHUNK, CHUNK), 1)
    out = []
    for gi in range(A_GROUPS):
        vn = vc[gi] * rstd * ln_g[gi] + ln_b[gi]
        w = jnp.where(row >= col, ws[gi], 0.0)
        sv = _dot(w, vn) + bs_t[gi]
        out.append(u[gi] * sv)
    return out


def _split(ref, n, width):
    return [ref[:, i * width:(i + 1) * width] for i in range(n)]


def _gate_in_specs():
    return [
        pl.BlockSpec((CHUNK, D_INNER), lambda c: (c, 0)),
        pl.BlockSpec((CHUNK, D_INNER), lambda c: (c, 1)),
        pl.BlockSpec((1, D_INNER), lambda c: (0, 0)),
        pl.BlockSpec((1, D_INNER), lambda c: (0, 0)),
        pl.BlockSpec((A_GROUPS, CHUNK, CHUNK), lambda c: (0, 0, 0)),
        pl.BlockSpec((A_GROUPS, CHUNK, 1), lambda c: (0, 0, 0)),
    ]


def _gate_args(u_ref, v_ref, g_ref, b_ref, ws_ref, bs_ref):
    ng, gw = A_GROUPS, A_GROUP_W
    return (_split(u_ref, ng, gw), _split(v_ref, ng, gw), _split(g_ref, ng, gw), _split(b_ref, ng, gw),
            [ws_ref[i] for i in range(ng)], [bs_ref[i] for i in range(ng)])


def _gate_fwd(proj, ln_g, ln_b, ws, bs_col, mixcat, *, name):
    def body(u_ref, v_ref, g_ref, b_ref, ws_ref, bs_ref, cat_in, cat_ref):
        del cat_in
        out = _gate_tile(*_gate_args(u_ref, v_ref, g_ref, b_ref, ws_ref, bs_ref))
        for gi, o in enumerate(out):
            cat_ref[:, gi * A_GROUP_W:(gi + 1) * A_GROUP_W] = o.astype(cat_ref.dtype)

    return pl.pallas_call(
        body, grid=(N_CHUNKS,), in_specs=[*_gate_in_specs(), pl.BlockSpec(memory_space=pl.ANY)],
        out_specs=pl.BlockSpec((CHUNK, D_INNER), lambda c: (c, 0)), out_shape=SDS(mixcat.shape, mixcat.dtype),
        input_output_aliases={6: 0}, compiler_params=_cparams(("parallel",)), name=name,
    )(proj, proj, ln_g, ln_b, ws, bs_col, mixcat)


def _gate_bwd(proj, ln_g, ln_b, ws, bs_col, dcat, dproj, *, name):
    ng, gw = A_GROUPS, A_GROUP_W

    def body(u_ref, v_ref, g_ref, b_ref, ws_ref, bs_ref, d_ref, dproj_in, dproj_ref, dg_ref, db_ref, dws_ref, dbs_ref):
        del dproj_in
        args = _gate_args(u_ref, v_ref, g_ref, b_ref, ws_ref, bs_ref)
        _, vjp = jax.vjp(_gate_tile, *args)
        dpu, dpv, dg, db, dws, dbs = vjp(_split(d_ref, ng, gw))
        for gi in range(ng):
            dproj_ref[:, gi * gw:(gi + 1) * gw] = dpu[gi].astype(dproj_ref.dtype)
            dproj_ref[:, D_INNER + gi * gw:D_INNER + (gi + 1) * gw] = dpv[gi].astype(dproj_ref.dtype)

        @pl.when(pl.program_id(0) == 0)
        def _():
            for r in (dg_ref, db_ref, dws_ref, dbs_ref):
                r[...] = jnp.zeros_like(r)

        for gi in range(ng):
            dg_ref[:, gi * gw:(gi + 1) * gw] += dg[gi]
            db_ref[:, gi * gw:(gi + 1) * gw] += db[gi]
            dws_ref[gi] += dws[gi]
            dbs_ref[gi] += dbs[gi]

    in_specs = _gate_in_specs()
    return pl.pallas_call(
        body, grid=(N_CHUNKS,),
        in_specs=[*in_specs, pl.BlockSpec((CHUNK, D_INNER), lambda c: (c, 0)), pl.BlockSpec(memory_space=pl.ANY)],
        out_specs=[pl.BlockSpec((CHUNK, 2 * D_INNER), lambda c: (c, 0)), *in_specs[2:]],
        out_shape=[SDS(dproj.shape, dproj.dtype), SDS((1, D_INNER), F32), SDS((1, D_INNER), F32),
                   SDS((ng, CHUNK, CHUNK), F32), SDS((ng, CHUNK, 1), F32)],
        input_output_aliases={7: 0}, compiler_params=_cparams(("arbitrary",)), name=name,
    )(proj, proj, ln_g, ln_b, ws, bs_col, dcat, dproj)


ATT_TQ = 512


def _attn_tile(q, k, v):
    s = _dot_nt(q, k) * (1.0 / math.sqrt(X_HEAD_DIM))
    s = s - jnp.max(s, axis=-1, keepdims=True)
    e = jnp.exp(s)
    p = e / jnp.sum(e, axis=-1, keepdims=True)
    return _dot(p, v)


def _attn_in_specs(q_blk, order):
    hd = X_HEAD_DIM
    return [
        pl.BlockSpec((ATT_TQ, hd), lambda a, b: (order(a, b)[0], q_blk + order(a, b)[1])),
        pl.BlockSpec((N_MEM, hd), lambda a, b: (0, order(a, b)[1])),
        pl.BlockSpec((N_MEM, hd), lambda a, b: (0, X_HEADS + order(a, b)[1])),
    ]


def _attn_fwd(proj, q_off, kv, *, name):
    order = lambda i, h: (i, h)
    cat_blk = D_INNER // X_HEAD_DIM

    def body(q_ref, k_ref, v_ref, o_ref):
        o_ref[...] = _attn_tile(q_ref[...], k_ref[...], v_ref[...]).astype(o_ref.dtype)

    return pl.pallas_call(
        body, grid=(SEQ // ATT_TQ, X_HEADS), in_specs=_attn_in_specs(q_off // X_HEAD_DIM, order),
        out_specs=pl.BlockSpec((ATT_TQ, X_HEAD_DIM), lambda i, h: (i, cat_blk + h)),
        out_shape=SDS((SEQ, MIX_OUT), BF16), compiler_params=_cparams(("parallel", "parallel")), name=name,
    )(proj, kv, kv)


def _attn_bwd(proj, q_off, kv, dcat, dproj_width, dq_off, *, name):
    order = lambda h, i: (i, h)
    cat_blk = D_INNER // X_HEAD_DIM
    dq_blk = dq_off // X_HEAD_DIM

    def body(q_ref, k_ref, v_ref, do_ref, dq_ref, dk_ref, dv_ref):
        _, vjp = jax.vjp(_attn_tile, q_ref[...], k_ref[...], v_ref[...])
        dq, dk, dv = vjp(do_ref[...])
        dq_ref[...] = dq.astype(dq_ref.dtype)

        @pl.when(pl.program_id(1) == 0)
        def _():
            dk_ref[...] = jnp.zeros_like(dk_ref)
            dv_ref[...] = jnp.zeros_like(dv_ref)

        dk_ref[...] += dk
        dv_ref[...] += dv

    kv_spec = pl.BlockSpec((N_MEM, X_HEAD_DIM), lambda h, i: (0, h))
    return pl.pallas_call(
        body, grid=(X_HEADS, SEQ // ATT_TQ),
        in_specs=[*_attn_in_specs(q_off // X_HEAD_DIM, order),
                  pl.BlockSpec((ATT_TQ, X_HEAD_DIM), lambda h, i: (i, cat_blk + h))],
        out_specs=[pl.BlockSpec((ATT_TQ, X_HEAD_DIM), lambda h, i: (i, dq_blk + h)), kv_spec, kv_spec],
        out_shape=[SDS((SEQ, dproj_width), BF16), SDS((N_MEM, X_WIDTH), F32), SDS((N_MEM, X_WIDTH), F32)],
        compiler_params=_cparams(("parallel", "arbitrary")), name=name,
    )(proj, kv, kv, dcat)


CONV_TC = 512


def _shift_down(x, s):
    if s == 0:
        return x
    row = lax.broadcasted_iota(jnp.int32, x.shape, 0)
    return jnp.where(row >= s, pltpu.roll(x, s, 0), 0.0)


def _shift_up(x, s):
    if s == 0:
        return x
    n = x.shape[0]
    row = lax.broadcasted_iota(jnp.int32, x.shape, 0)
    return jnp.where(row < n - s, pltpu.roll(x, n - s, 0), 0.0)


def _conv_pre(x, w_ref, b_ref):
    pre = b_ref[...] + jnp.zeros_like(x)
    for k in range(CONV_K):
        pre = pre + w_ref[k:k + 1, :] * _shift_down(x, CONV_K - 1 - k)
    return pre


def _conv_fwd(proj, w, b, *, name):
    blk0 = D_INNER // CONV_TC

    def body(x_ref, w_ref, b_ref, o_ref):
        pre = _conv_pre(x_ref[...], w_ref, b_ref)
        o_ref[...] = pre * jax.nn.sigmoid(pre)

    return pl.pallas_call(
        body, grid=(CONV_DIM // CONV_TC,),
        in_specs=[pl.BlockSpec((SEQ, CONV_TC), lambda j: (0, blk0 + j)), pl.BlockSpec((CONV_K, CONV_TC), lambda j: (0, j)),
                  pl.BlockSpec((1, CONV_TC), lambda j: (0, j))],
        out_specs=pl.BlockSpec((SEQ, CONV_TC), lambda j: (0, j)), out_shape=SDS((SEQ, CONV_DIM), F32),
        compiler_params=_cparams(("parallel",)), name=name)(proj, w, b)


def _conv_bwd(proj, w, b, dxs, dbm, dcm, dproj, *, name):
    tc = CONV_TC // 2
    blk0 = D_INNER // tc
    n_x = D_INNER // tc
    n_b = SSM_GROUPS * SSM_STATE // tc

    def body(x_ref, w_ref, b_ref, dxs_ref, dbm_ref, dcm_ref, dproj_in, dproj_ref, dw_ref, db_ref):
        del dproj_in
        j = pl.program_id(0)
        x = x_ref[...]
        pre = _conv_pre(x, w_ref, b_ref)
        sg = jax.nn.sigmoid(pre)
        dact = jnp.where(j < n_x, dxs_ref[...], jnp.where(j < n_x + n_b, dbm_ref[...], dcm_ref[...]))
        dpre = dact * (sg * (1.0 + pre * (1.0 - sg)))
        dx = jnp.zeros_like(x)
        for k in range(CONV_K):
            s = CONV_K - 1 - k
            dx = dx + w_ref[k:k + 1, :] * _shift_up(dpre, s)
            dw_ref[k:k + 1, :] = jnp.sum(dpre * _shift_down(x, s), axis=0, keepdims=True)
        dproj_ref[...] = dx.astype(dproj_ref.dtype)
        db_ref[...] = jnp.sum(dpre, axis=0, keepdims=True)

    clip = lambda v, hi: jnp.minimum(jnp.maximum(v, 0), hi)
    return pl.pallas_call(
        body, grid=(CONV_DIM // tc,),
        in_specs=[pl.BlockSpec((SEQ, tc), lambda j: (0, blk0 + j)), pl.BlockSpec((CONV_K, tc), lambda j: (0, j)),
                  pl.BlockSpec((1, tc), lambda j: (0, j)),
                  pl.BlockSpec((SEQ, tc), lambda j: (0, clip(j, n_x - 1))),
                  pl.BlockSpec((SEQ, tc), lambda j: (0, clip(j - n_x, n_b - 1))),
                  pl.BlockSpec((SEQ, tc), lambda j: (0, clip(j - n_x - n_b, n_b - 1))),
                  pl.BlockSpec(memory_space=pl.ANY)],
        out_specs=[pl.BlockSpec((SEQ, tc), lambda j: (0, blk0 + j)), pl.BlockSpec((CONV_K, tc), lambda j: (0, j)),
                   pl.BlockSpec((1, tc), lambda j: (0, j))],
        out_shape=[SDS(dproj.shape, dproj.dtype), SDS((CONV_K, CONV_DIM), F32), SDS((1, CONV_DIM), F32)],
        input_output_aliases={6: 0}, compiler_params=_cparams(("parallel",)), name=name,
    )(proj, w, b, dxs, dbm, dcm, dproj)


SSM_PAIRS = SSM_HPG // 2


def _dot_exact01(x, m01, m01_t, x_first, differentiable):
    def product(v, m):
        hi = v.astype(BF16)
        rest = v - hi.astype(F32)
        mid = rest.astype(BF16)
        lo = (rest - mid.astype(F32)).astype(BF16)
        dims = (((1,), (0,)), ((), ()))
        dot = lambda part: lax.dot_general(*((part, m) if x_first else (m, part)), dims, preferred_element_type=F32)
        return dot(hi) + dot(mid) + dot(lo)

    if not differentiable:
        return product(x, m01)

    @jax.custom_vjp
    def exact(v):
        return product(v, m01)

    exact.defvjp(lambda v: (product(v, m01), None), lambda _, ct: (product(ct, m01_t),))
    return exact(x)


def _ssd_tile(xp, zp, bm, cm, hp, dt_c, dt_r, bias, bias_col, alog, alog_col, dsk, gnp, differentiable=False):
    row = lax.broadcasted_iota(jnp.int32, (CHUNK, CHUNK), 0)
    col = lax.broadcasted_iota(jnp.int32, (CHUNK, CHUNK), 1)
    causal = row >= col
    left = col < SSM_HEAD_DIM
    top = row < SSM_HEAD_DIM
    ones = jnp.ones((CHUNK, CHUNK), BF16)
    cb = _dot_nt(cm, bm)
    dtp = jax.nn.softplus(dt_c + bias)
    da_c = dtp * -jnp.exp(alog)
    da_r = jax.nn.softplus(dt_r + bias_col) * -jnp.exp(alog_col)
    lower = jnp.where(causal, 1.0, 0.0).astype(BF16)
    upper = jnp.where(row <= col, 1.0, 0.0).astype(BF16)
    cs = _dot_exact01(da_c, lower, upper, False, differentiable)
    cs_rows = _dot_exact01(da_r, upper, lower, True, differentiable)
    cs_last = jnp.sum(da_c, axis=0, keepdims=True)
    ecs, decay, ecl = jnp.exp(cs), jnp.exp(cs_last - cs), jnp.exp(cs_last)
    m = [cb * jnp.exp(jnp.where(causal, cs[:, r:r + 1] - cs_rows[r:r + 1, :], -1e30)) for r in range(SSM_HPG)]
    ygs, hn = [], []
    for p in range(SSM_PAIRS):
        a, b = 2 * p, 2 * p + 1
        pair = lambda v: jnp.where(left, v[:, a:a + 1], v[:, b:b + 1])
        xdt = xp[p] * pair(dtp)
        y = jnp.where(left, _dot(m[a], xdt), _dot(m[b], xdt))
        y = y + _dot_nt(cm, hp[p]) * pair(ecs)
        y = y + xp[p] * pair(dsk)
        states = _dot_tn(xdt * pair(decay), bm)
        hn.append(hp[p] * jnp.where(top, ecl[:, a:a + 1], ecl[:, b:b + 1]) + states)
        ygs.append(y * (zp[p] * jax.nn.sigmoid(zp[p])))
    ms = sum(_dot(t * t, ones) for t in ygs) * (1.0 / SSM_GROUP_W)
    rs = lax.rsqrt(ms + EPS)
    return [ygs[p] * rs * gnp[p] for p in range(SSM_PAIRS)], hn


def _ssd_in_specs(cidx):
    gw, n = SSM_GROUP_W, SSM_STATE
    bm_blk = D_INNER // n
    return [
        pl.BlockSpec((CHUNK, gw), lambda g, c: (cidx(c), g)),
        pl.BlockSpec((CHUNK, gw), lambda g, c: (cidx(c), g)),
        pl.BlockSpec((CHUNK, n), lambda g, c: (cidx(c), bm_blk + g)),
        pl.BlockSpec((CHUNK, n), lambda g, c: (cidx(c), bm_blk + SSM_GROUPS + g)),
        pl.BlockSpec((None, CHUNK, SSM_HPG), lambda g, c: (g, cidx(c), 0)),
        pl.BlockSpec((None, SSM_HPG, CHUNK), lambda g, c: (g, 0, cidx(c))),
        pl.BlockSpec((None, 3, SSM_HPG), lambda g, c: (g, 0, 0)),
        pl.BlockSpec((None, SSM_HPG, 2), lambda g, c: (g, 0, 0)),
        pl.BlockSpec((1, gw), lambda g, c: (0, g)),
    ]


def _ssd_args(x_ref, z_ref, bm_ref, cm_ref, hp, dtc_ref, dtr_ref, prow_ref, pcol_ref, gn_ref):
    npair, w = SSM_PAIRS, 2 * SSM_HEAD_DIM
    return (_split(x_ref, npair, w), _split(z_ref, npair, w), bm_ref[...], cm_ref[...], hp, dtc_ref[...], dtr_ref[...],
            prow_ref[0:1, :], pcol_ref[:, 0:1], prow_ref[1:2, :], pcol_ref[:, 1:2], prow_ref[2:3, :],
            _split(gn_ref, npair, w))


def _pair_rows(ref):
    w = 2 * SSM_HEAD_DIM
    return [ref[p * w:(p + 1) * w, :] for p in range(SSM_PAIRS)]


def _ssd_fwd(xbc, proj, dt_c, dt_r, par_row, par_col, gn, mixcat, *, name):
    w = 2 * SSM_HEAD_DIM

    def body(x_ref, z_ref, bm_ref, cm_ref, dtc_ref, dtr_ref, prow_ref, pcol_ref, gn_ref, cat_in,
             cat_ref, hprev_ref, h_scr):
        del cat_in

        @pl.when(pl.program_id(1) == 0)
        def _():
            h_scr[...] = jnp.zeros_like(h_scr)

        hprev_ref[...] = h_scr[...]
        yn, hn = _ssd_tile(*_ssd_args(x_ref, z_ref, bm_ref, cm_ref, _pair_rows(h_scr), dtc_ref, dtr_ref, prow_ref,
                                      pcol_ref, gn_ref))
        for p in range(SSM_PAIRS):
            cat_ref[:, p * w:(p + 1) * w] = yn[p].astype(cat_ref.dtype)
            h_scr[p * w:(p + 1) * w, :] = hn[p]

    return pl.pallas_call(
        body, grid=(SSM_GROUPS, N_CHUNKS), in_specs=[*_ssd_in_specs(lambda c: c), pl.BlockSpec(memory_space=pl.ANY)],
        out_specs=[pl.BlockSpec((CHUNK, SSM_GROUP_W), lambda g, c: (c, g)),
                   pl.BlockSpec((None, None, SSM_GROUP_W, SSM_STATE), lambda g, c: (c, g, 0, 0))],
        out_shape=[SDS(mixcat.shape, mixcat.dtype), SDS((N_CHUNKS, SSM_GROUPS, SSM_GROUP_W, SSM_STATE), F32)],
        scratch_shapes=[pltpu.VMEM((SSM_GROUP_W, SSM_STATE), F32)],
        input_output_aliases={9: 0}, compiler_params=_cparams(("parallel", "arbitrary")), name=name,
    )(xbc, proj, xbc, xbc, dt_c, dt_r, par_row, par_col, gn, mixcat)


def _ssd_bwd(xbc, proj, dt_c, dt_r, par_row, par_col, gn, hprev, dcat, dproj, *, name):
    nh, w, gw, n = SSM_HPG, 2 * SSM_HEAD_DIM, SSM_GROUP_W, SSM_STATE
    rev = lambda c: N_CHUNKS - 1 - c

    def body(x_ref, z_ref, bm_ref, cm_ref, dtc_ref, dtr_ref, prow_ref, pcol_ref, gn_ref, hprev_ref, dy_ref,
             dproj_in, dz_ref, dxs_ref, dbm_ref, dcm_ref, ddtc_ref, ddtr_ref, dprow_ref, dpcol_ref, dgn_ref, dh_scr):
        del dproj_in
        first = pl.program_id(1) == 0

        @pl.when(first)
        def _():
            dh_scr[...] = jnp.zeros_like(dh_scr)
            for ref in (dprow_ref, dpcol_ref, dgn_ref):
                ref[...] = jnp.zeros_like(ref)

        args = _ssd_args(x_ref, z_ref, bm_ref, cm_ref, _pair_rows(hprev_ref), dtc_ref, dtr_ref, prow_ref, pcol_ref,
                         gn_ref)
        _, vjp = jax.vjp(lambda *a: _ssd_tile(*a, differentiable=True), *args)
        dxs, dzs, dbm, dcm, dhs, ddtc, ddtr, dbias, dbias_col, dalog, dalog_col, ddsk, dgn = vjp(
            (_split(dy_ref, SSM_PAIRS, w), _pair_rows(dh_scr)))
        dbm_ref[...] = dbm
        dcm_ref[...] = dcm
        ddtc_ref[...] = ddtc
        ddtr_ref[...] = ddtr
        for q in range(SSM_PAIRS):
            dxs_ref[:, q * w:(q + 1) * w] = dxs[q]
            dz_ref[:, q * w:(q + 1) * w] = dzs[q].astype(dz_ref.dtype)
            dh_scr[q * w:(q + 1) * w, :] = dhs[q]
            dgn_ref[:, q * w:(q + 1) * w] += dgn[q]
        for i, d in enumerate((dbias, dalog, ddsk)):
            dprow_ref[i:i + 1, :] += d
        for i, d in enumerate((dbias_col, dalog_col)):
            dpcol_ref[:, i:i + 1] += d

    return pl.pallas_call(
        body, grid=(SSM_GROUPS, N_CHUNKS),
        in_specs=[*_ssd_in_specs(rev),
                  pl.BlockSpec((None, None, gw, n), lambda g, c: (rev(c), g, 0, 0)),
                  pl.BlockSpec((CHUNK, gw), lambda g, c: (rev(c), g)),
                  pl.BlockSpec(memory_space=pl.ANY)],
        out_specs=[pl.BlockSpec((CHUNK, gw), lambda g, c: (rev(c), g)),
                   pl.BlockSpec((CHUNK, gw), lambda g, c: (rev(c), g)),
                   pl.BlockSpec((CHUNK, n), lambda g, c: (rev(c), g)),
                   pl.BlockSpec((CHUNK, n), lambda g, c: (rev(c), g)),
                   pl.BlockSpec((None, CHUNK, nh), lambda g, c: (g, rev(c), 0)),
                   pl.BlockSpec((None, nh, CHUNK), lambda g, c: (g, 0, rev(c))),
                   pl.BlockSpec((None, 3, nh), lambda g, c: (g, 0, 0)),
                   pl.BlockSpec((None, nh, 2), lambda g, c: (g, 0, 0)),
                   pl.BlockSpec((1, gw), lambda g, c: (0, g))],
        out_shape=[SDS(dproj.shape, dproj.dtype), SDS((SEQ, D_INNER), F32), SDS((SEQ, SSM_GROUPS * n), F32),
                   SDS((SEQ, SSM_GROUPS * n), F32), SDS((SSM_GROUPS, SEQ, nh), F32), SDS((SSM_GROUPS, nh, SEQ), F32),
                   SDS((SSM_GROUPS, 3, nh), F32), SDS((SSM_GROUPS, nh, 2), F32), SDS((1, D_INNER), F32)],
        scratch_shapes=[pltpu.VMEM((gw, n), F32)],
        input_output_aliases={11: 0}, compiler_params=_cparams(("parallel", "arbitrary")), name=name,
    )(xbc, proj, xbc, xbc, dt_c, dt_r, par_row, par_col, gn, hprev, dcat, dproj)


def _sum_contributions(chip, parts, landed, *, name):
    _, r, c = parts.shape
    tr = _pick(r, (256, 384, 128))

    def body(chip_ref, own_ref, landed_ref, o_ref):
        del chip_ref
        acc = own_ref[...].astype(F32)
        for s in range(landed_ref.shape[0]):
            acc = acc + landed_ref[s].astype(F32)
        o_ref[...] = acc

    grid_spec = pltpu.PrefetchScalarGridSpec(
        num_scalar_prefetch=1, grid=(r // tr,),
        in_specs=[pl.BlockSpec((None, tr, c), lambda i, chip_ref: (chip_ref[0], i, 0)),
                  pl.BlockSpec((landed.shape[0], tr, c), lambda i, chip_ref: (0, i, 0))],
        out_specs=pl.BlockSpec((tr, c), lambda i, chip_ref: (i, 0)))
    return pl.pallas_call(body, grid_spec=grid_spec, out_shape=SDS((r, c), F32),
                          compiler_params=_cparams(("parallel",)), name=name)(chip, parts, landed)


def _adamw(w, g, m, v, *, name):
    layers, r, c = w.shape
    if r <= 256 or r % 128 == 0:
        tr = min(r, 256)
        steps, spec = r // tr, pl.BlockSpec((None, tr, c), lambda l, i: (l, i, 0))
    else:
        tc = _pick(c, (256, 128))
        steps, spec = c // tc, pl.BlockSpec((None, r, tc), lambda l, i: (l, 0, i))

    def body(w_ref, g_ref, m_ref, v_ref, d_ref, mo_ref, vo_ref):
        g = g_ref[...]
        m_new = ADAM_B1 * m_ref[...] + (1.0 - ADAM_B1) * g
        v_new = ADAM_B2 * v_ref[...] + (1.0 - ADAM_B2) * (g * g)
        m_hat = m_new / (1.0 - ADAM_B1 ** ADAM_STEP)
        v_hat = v_new / (1.0 - ADAM_B2 ** ADAM_STEP)
        d_ref[...] = -ADAM_LR * (m_hat / (jnp.sqrt(v_hat) + ADAM_EPS) + ADAM_WD * w_ref[...])
        mo_ref[...] = m_new
        vo_ref[...] = v_new

    return pl.pallas_call(body, grid=(layers, steps), in_specs=[spec] * 4, out_specs=[spec] * 3,
                          out_shape=[SDS(w.shape, F32)] * 3, compiler_params=_cparams(("parallel", "parallel")),
                          name=name)(w, g, m, v)


ANY = pl.BlockSpec(memory_space=pl.ANY)


def _place():
    x, y, c = lax.axis_index("x"), lax.axis_index("y"), lax.axis_index("c")
    chips = [(1 - x, y), (x, 1 - y), (1 - x, 1 - y)]
    return x, y, c, chips


def _remote(src, dst, send_sem, recv_sem, to):
    return pltpu.make_async_remote_copy(src_ref=src, dst_ref=dst, send_sem=send_sem, recv_sem=recv_sem,
                                        device_id=to, device_id_type=MESH)


STREAM_ROWS = 256


def _stream_rows(i):
    return pl.ds(pl.multiple_of(i * STREAM_ROWS, STREAM_ROWS), STREAM_ROWS)


def _channel_scratch(width, dtype, rows=STREAM_ROWS):
    buf = (2, rows, width)
    return [pltpu.VMEM(buf, dtype), pltpu.VMEM(buf, dtype), *([pltpu.SemaphoreType.DMA((2,))] * 5),
            pltpu.SemaphoreType.REGULAR((2,))]


CHANNEL_REFS = 8


def _copy_blocks(srcs, dsts, ch):
    sbuf, _, ld, _, _, st, _, _ = ch
    n = len(srcs)
    load = lambda i: pltpu.make_async_copy(srcs[i], sbuf.at[i % 2], ld.at[i % 2])
    store = lambda i: pltpu.make_async_copy(sbuf.at[i % 2], dsts[i], st.at[i % 2])
    load(0).start()
    for i in range(n):
        if i + 1 < n:
            if i >= 1:
                store(i - 1).wait()
            load(i + 1).start()
        load(i).wait()
        store(i).start()
    for i in range(max(0, n - 2), n):
        store(i).wait()


def _exchange_block_streams(streams, sibling):
    plans = []
    for srcs, dsts, keeps, (sbuf, rbuf, ld, snd, rcv, st, kp, credit) in streams:
        n = len(srcs)

        def load(i, srcs=srcs, sbuf=sbuf, ld=ld):
            return pltpu.make_async_copy(srcs[i], sbuf.at[i % 2], ld.at[i % 2])

        def push(i, sbuf=sbuf, rbuf=rbuf, snd=snd, rcv=rcv):
            return _remote(sbuf.at[i % 2], rbuf.at[i % 2], snd.at[i % 2], rcv.at[i % 2], sibling)

        def store(i, rbuf=rbuf, dsts=dsts, st=st):
            return pltpu.make_async_copy(rbuf.at[i % 2], dsts[i], st.at[i % 2])

        def save(i, sbuf=sbuf, keeps=keeps, kp=kp):
            return pltpu.make_async_copy(sbuf.at[i % 2], keeps[i], kp.at[i % 2])

        def free_slot(i, n=n, store=store, credit=credit):
            if 1 <= i < n:
                store(i - 1).wait()
                if i + 1 < n:
                    pl.semaphore_signal(credit.at[(i + 1) % 2], 1, device_id=sibling, device_id_type=MESH)

        def send(i, n=n, load=load, push=push, save=save, keeps=keeps, credit=credit):
            if i < n:
                load(i).wait()
                pl.semaphore_wait(credit.at[i % 2], 1)
                push(i).start()
                if keeps[i] is not None:
                    save(i).start()

        def receive(i, n=n, load=load, push=push, store=store, save=save, keeps=keeps):
            if i < n:
                push(i).wait_recv()
                store(i).start()
                push(i).wait_send()
                if keeps[i] is not None:
                    save(i).wait()
                if i + 2 < n:
                    load(i + 2).start()

        for i in range(min(2, n)):
            pl.semaphore_signal(credit.at[i], 1, device_id=sibling, device_id_type=MESH)
            load(i).start()
        plans.append((n, free_slot, send, receive, store))
    for _, _, send, _, _ in plans:
        send(0)
    for i in range(max(p[0] for p in plans)):
        for _, free_slot, _, _, _ in plans:
            free_slot(i)
        for _, _, send, _, _ in plans:
            send(i + 1)
        for _, _, _, receive, _ in plans:
            receive(i)
    for n, _, _, _, store in plans:
        store(n - 1).wait()


def _all_gather_shards(shards, small, *, name):
    n = len(shards)

    def body(*refs):
        ins, outs = refs[:n + 1], refs[n + 1:2 * n + 2]
        scr = refs[2 * n + 2:]
        chans = [scr[CHANNEL_REFS * t:CHANNEL_REFS * (t + 1)] for t in range(n)]
        send_sems, recv_sems, small_sems = scr[CHANNEL_REFS * n:]
        x, y, c, _ = _place()
        me = 2 * x + y
        sibling = (x, y, 1 - c)
        near = (lax.rem(x + 1 - c, 2), lax.rem(y + c, 2))
        far = (lax.rem(x + c, 2), lax.rem(y + 1 - c, 2))
        k_near, k_far, k_diag = 2 * near[0] + near[1], 2 * far[0] + far[1], 3 - me
        targets = ((*near, c), (*far, c), (*far, c))
        arrives = (k_near, k_far, k_diag)
        streams_in = (k_far, k_near, k_diag)

        def ici(t, j, src, blk):
            return _remote(src, outs[t].at[blk, c], send_sems.at[3 * t + j], recv_sems.at[3 * t + j], targets[j])

        first = [ici(t, j, ins[t].at[c], me) for t in range(n + 1) for j in range(2)]
        for cp in first:
            cp.start()
        small_local = pltpu.make_async_copy(ins[n], outs[n].at[me], small_sems.at[6])
        small_local.start()
        for t in range(n):
            _copy_blocks([ins[t].at[h] for h in range(2)], [outs[t].at[me, h] for h in range(2)], chans[t])
        passed = []
        for j in range(3):
            for t in range(n + 1):
                landed = outs[t].at[arrives[j], c]
                ici(t, j, landed, arrives[j]).wait_recv()
                if j == 0:
                    fwd = ici(t, 2, landed, k_near)
                    fwd.start()
                    passed.append(fwd)
                if t < n:
                    _exchange_block_streams([([landed], [outs[t].at[streams_in[j], 1 - c]], [None], chans[t])], sibling)
                else:
                    fwd = _remote(landed, landed, small_sems.at[j], small_sems.at[3 + j], sibling)
                    fwd.start()
                    passed.append(fwd)
        for j in range(3):
            got = outs[n].at[streams_in[j], 1 - c]
            _remote(got, got, small_sems.at[j], small_sems.at[3 + j], sibling).wait_recv()
        for cp in first + passed:
            cp.wait_send()
        small_local.wait()

    scratch = []
    for s in shards:
        scratch += _channel_scratch(s.shape[2], s.dtype, rows=s.shape[1])
    return pl.pallas_call(
        body, in_specs=[ANY] * (n + 1), out_specs=[ANY] * (n + 1),
        out_shape=[SDS((N_CHIPS, *s.shape), s.dtype) for s in (*shards, small)],
        scratch_shapes=[*scratch, pltpu.SemaphoreType.DMA((3 * n + 3,)), pltpu.SemaphoreType.DMA((3 * n + 3,)),
                        pltpu.SemaphoreType.DMA((7,))],
        compiler_params=pltpu.CompilerParams(vmem_limit_bytes=VMEM_LIMIT), name=name)(*shards, small)


def _pair_reduce(stacks, *, name):
    n = len(stacks)
    per = 11

    def body(*refs):
        ins, outs, scr = refs[:n], refs[n:2 * n], refs[2 * n:]
        x, y, c, _ = _place()
        sibling = (x, y, 1 - c)
        streams = []
        for t in range(n):
            sraw, sbuf, rbuf, obuf, pbuf, ld_s, ld_o, snd, rcv, st, credit = scr[per * t:per * (t + 1)]
            steps = ins[t].shape[1] // STREAM_ROWS
            src, own, out = ins[t].at[1 - c], ins[t].at[c], outs[t]
            assert steps >= 2

            def load_s(i, slot, src=src, sraw=sraw, ld_s=ld_s):
                return pltpu.make_async_copy(src.at[_stream_rows(i)], sraw.at[slot], ld_s.at[slot])

            def load_o(i, slot, own=own, obuf=obuf, ld_o=ld_o):
                return pltpu.make_async_copy(own.at[_stream_rows(i)], obuf.at[slot], ld_o.at[slot])

            def push(slot, sbuf=sbuf, rbuf=rbuf, snd=snd, rcv=rcv):
                return _remote(sbuf.at[slot], rbuf.at[slot], snd.at[slot], rcv.at[slot], sibling)

            def store(i, slot, pbuf=pbuf, out=out, st=st):
                return pltpu.make_async_copy(pbuf.at[slot], out.at[_stream_rows(i)], st.at[slot])

            def send(i, slot, load_s=load_s, push=push, sraw=sraw, sbuf=sbuf, credit=credit):
                load_s(i, slot).wait()
                sbuf[slot] = sraw[slot].astype(sbuf.dtype)
                pl.semaphore_wait(credit.at[slot], 1)
                push(slot).start()

            def combine(i, slot, load_s=load_s, load_o=load_o, push=push, store=store, rbuf=rbuf, obuf=obuf, pbuf=pbuf,
                        credit=credit, steps=steps):
                load_o(i, slot).wait()
                push(slot).wait_recv()

                @pl.when(i >= 2)
                def _():
                    store(i, slot).wait()

                pbuf[slot] = (obuf[slot] + rbuf[slot].astype(F32)).astype(pbuf.dtype)
                store(i, slot).start()
                push(slot).wait_send()

                @pl.when(i + 2 < steps)
                def _():
                    load_s(i + 2, slot).start()
                    load_o(i + 2, slot).start()
                    pl.semaphore_signal(credit.at[slot], 1, device_id=sibling, device_id_type=MESH)

            for slot in range(2):
                pl.semaphore_signal(credit.at[slot], 1, device_id=sibling, device_id_type=MESH)
                load_s(slot, slot).start()
                load_o(slot, slot).start()
            streams.append((steps, send, combine, store))
        for _, send, _, _ in streams:
            send(0, 0)

        def step(i, carry):
            slot = lax.rem(i, 2)
            for steps, send, _, _ in streams:
                @pl.when(i + 1 < steps)
                def _(send=send):
                    send(i + 1, 1 - slot)
            for steps, _, combine, _ in streams:
                @pl.when(i < steps)
                def _(combine=combine):
                    combine(i, slot)
            return carry

        lax.fori_loop(0, max(s[0] for s in streams), step, 0)
        for _, _, _, store in streams:
            for slot in range(2):
                store(0, slot).wait()

    scratch = []
    for s in stacks:
        buf = (2, STREAM_ROWS, s.shape[2])
        scratch += [pltpu.VMEM(buf, F32), pltpu.VMEM(buf, BF16), pltpu.VMEM(buf, BF16), pltpu.VMEM(buf, F32),
                    pltpu.VMEM(buf, BF16), *([pltpu.SemaphoreType.DMA((2,))] * 5), pltpu.SemaphoreType.REGULAR((2,))]
    return pl.pallas_call(
        body, in_specs=[ANY] * n, out_specs=[ANY] * n, out_shape=[SDS(s.shape[1:], BF16) for s in stacks],
        scratch_shapes=scratch, compiler_params=pltpu.CompilerParams(vmem_limit_bytes=VMEM_LIMIT), name=name)(*stacks)


HBM_SPEC = pl.BlockSpec(memory_space=pltpu.HBM)
SEM_SPEC = pl.BlockSpec(memory_space=pltpu.SEMAPHORE)
SIDE_EFFECT = pltpu.SideEffectType.DATAFLOW_SIDE_EFFECTING


def _scatter_copies(ins, lands, send_sems, recv_sems):
    _, _, c, chips = _place()
    return [_remote(ins[t].at[2 * cx + cy], lands[t].at[j], send_sems.at[3 * t + j], recv_sems.at[3 * t + j],
                    (cx, cy, c)) for t in range(len(ins)) for j, (cx, cy) in enumerate(chips)]


def _chip_scatter_start(parts, *, name):
    n = len(parts)

    def body(*refs):
        ins, lands = refs[:n], refs[n:2 * n]
        send_sems, recv_sems, token = refs[2 * n], refs[2 * n + 1], refs[-1]
        for cp in _scatter_copies(ins, lands, send_sems, recv_sems):
            cp.start()
        token[...] = jnp.zeros_like(token)

    hbm = lambda a: pltpu.with_memory_space_constraint(a, pltpu.HBM)
    lands = [hbm(lax.empty((3, *p.shape[1:]), p.dtype)) for p in parts]
    thru = [pltpu.HBM(a.shape, a.dtype) for a in (*parts, *lands)]
    outs = pl.pallas_call(
        body, name=name,
        out_shape=(pltpu.SemaphoreType.DMA((3 * n,)), pltpu.SemaphoreType.DMA((3 * n,)), *thru, SDS((8, 128), F32)),
        in_specs=[HBM_SPEC] * (2 * n),
        out_specs=(SEM_SPEC, SEM_SPEC, *([HBM_SPEC] * (2 * n)), pl.BlockSpec(memory_space=pltpu.VMEM)),
        input_output_aliases={i: 2 + i for i in range(2 * n)},
        compiler_params=pltpu.CompilerParams(has_side_effects=SIDE_EFFECT),
    )(*[hbm(p) for p in parts], *lands)
    return outs[0], outs[1], outs[2:2 + n], outs[2 + n:2 + 2 * n], outs[-1]


def _chip_scatter_wait(send_sems, recv_sems, parts, lands, after, *, name):
    n = len(parts)

    def body(*refs):
        ins, lands_in = refs[:n], refs[n:2 * n]
        for cp in _scatter_copies(ins, lands_in, refs[2 * n], refs[2 * n + 1]):
            cp.wait_send()
            cp.wait_recv()

    outs = pl.pallas_call(
        body, name=name, out_shape=[pltpu.HBM(a.shape, a.dtype) for a in (*parts, *lands)],
        in_specs=[*([HBM_SPEC] * (2 * n)), SEM_SPEC, SEM_SPEC, *([ANY] * len(after))],
        out_specs=[HBM_SPEC] * (2 * n), input_output_aliases={i: i for i in range(2 * n)},
        compiler_params=pltpu.CompilerParams(has_side_effects=SIDE_EFFECT),
    )(*parts, *lands, send_sems, recv_sems, *after)
    return outs[:n], outs[n:]


def _gather_copies(shards, zones, send_sems, recv_sems):
    x, y, c, chips = _place()
    return [_remote(shards[t].at[c], zones[t].at[2 * x + y, c], send_sems.at[3 * t + j], recv_sems.at[3 * t + j],
                    (cx, cy, c)) for t in range(len(shards)) for j, (cx, cy) in enumerate(chips)]


def _gather_start(shards, after, *, name):
    n = len(shards)

    def body(*refs):
        ins, zones = refs[:n], refs[n:2 * n]
        send_sems, recv_sems, token = refs[2 * n + len(after)], refs[2 * n + len(after) + 1], refs[-1]
        for cp in _gather_copies(ins, zones, send_sems, recv_sems):
            cp.start()
        token[...] = jnp.zeros_like(token)

    hbm = lambda a: pltpu.with_memory_space_constraint(a, pltpu.HBM)
    zones = [hbm(lax.empty((N_CHIPS, *s.shape), s.dtype)) for s in shards]
    thru = [pltpu.HBM(a.shape, a.dtype) for a in (*shards, *zones)]
    outs = pl.pallas_call(
        body, name=name,
        out_shape=(pltpu.SemaphoreType.DMA((3 * n,)), pltpu.SemaphoreType.DMA((3 * n,)), *thru, SDS((8, 128), F32)),
        in_specs=[*([HBM_SPEC] * (2 * n)), *([ANY] * len(after))],
        out_specs=(SEM_SPEC, SEM_SPEC, *([HBM_SPEC] * (2 * n)), pl.BlockSpec(memory_space=pltpu.VMEM)),
        input_output_aliases={i: 2 + i for i in range(2 * n)},
        compiler_params=pltpu.CompilerParams(has_side_effects=SIDE_EFFECT),
    )(*[hbm(s) for s in shards], *zones, *after)
    return outs[0], outs[1], outs[2:2 + n], outs[2 + n:2 + 2 * n], outs[-1]


def _gather_wait(send_sems, recv_sems, shards, zones, after, *, name):
    n = len(shards)

    def body(*refs):
        for cp in _gather_copies(refs[:n], refs[n:2 * n], refs[2 * n], refs[2 * n + 1]):
            cp.wait_send()
            cp.wait_recv()

    outs = pl.pallas_call(
        body, name=name, out_shape=[pltpu.HBM(a.shape, a.dtype) for a in (*shards, *zones)],
        in_specs=[*([HBM_SPEC] * (2 * n)), SEM_SPEC, SEM_SPEC, *([ANY] * len(after))],
        out_specs=[HBM_SPEC] * (2 * n), input_output_aliases={i: i for i in range(2 * n)},
        compiler_params=pltpu.CompilerParams(has_side_effects=SIDE_EFFECT),
    )(*shards, *zones, send_sems, recv_sems, *after)
    return outs[:n], outs[n:]


def _gather_finish(shards, zones, *, name):
    n = len(shards)

    def body(*refs):
        ins, zones_in, outs, scr = refs[:n], refs[n:2 * n], refs[2 * n:3 * n], refs[3 * n:]
        x, y, c, chips = _place()
        me = 2 * x + y
        sibling = (x, y, 1 - c)
        others = [2 * cx + cy for cx, cy in chips]
        chans = [scr[CHANNEL_REFS * t:CHANNEL_REFS * (t + 1)] for t in range(n)]
        for t in range(n):
            _copy_blocks([ins[t].at[h] for h in range(2)], [outs[t].at[me, h] for h in range(2)], chans[t])
        _exchange_block_streams([([zones_in[t].at[k, c] for k in others], [outs[t].at[k, 1 - c] for k in others],
                                  [None] * len(others), chans[t]) for t in range(n)], sibling)

    scratch = []
    for s in shards:
        scratch += _channel_scratch(s.shape[2], s.dtype, rows=s.shape[1])
    return pl.pallas_call(
        body, in_specs=[ANY] * (2 * n), out_specs=[ANY] * n, out_shape=[SDS(z.shape, z.dtype) for z in zones],
        input_output_aliases={n + t: t for t in range(n)}, scratch_shapes=scratch,
        compiler_params=pltpu.CompilerParams(vmem_limit_bytes=VMEM_LIMIT), name=name)(*shards, *zones)


def _pair_share(groups, *, name):
    finals = [f for grp in groups for f in grp]
    n, n_out = len(finals), len(groups)

    def body(*refs):
        ins, outs, scr = refs[:n], refs[n:n + n_out], refs[n + n_out:]
        x, y, c, _ = _place()
        sibling = (x, y, 1 - c)
        t, streams = 0, []
        for o, grp in enumerate(groups):
            rows = grp[0].shape[0] // 2
            blocks = [(layer, pl.ds(b * rows, rows)) for layer in range(len(grp)) for b in range(2)]
            streams.append(([ins[t + layer].at[rs] for layer, rs in blocks],
                            [outs[o].at[layer, 1 - c, rs] for layer, rs in blocks],
                            [outs[o].at[layer, c, rs] for layer, rs in blocks],
                            scr[CHANNEL_REFS * o:CHANNEL_REFS * (o + 1)]))
            t += len(grp)
        _exchange_block_streams(streams, sibling)

    scratch = []
    for grp in groups:
        scratch += _channel_scratch(grp[0].shape[1], grp[0].dtype, rows=grp[0].shape[0] // 2)
    return pl.pallas_call(
        body, in_specs=[ANY] * n, out_specs=[ANY] * n_out,
        out_shape=[SDS((len(grp), 2, *grp[0].shape), grp[0].dtype) for grp in groups],
        scratch_shapes=scratch, compiler_params=pltpu.CompilerParams(vmem_limit_bytes=VMEM_LIMIT), name=name)(*finals)


def _all_reduce_small(v, *, name):
    rows, lanes = v.shape
    n_dev = 8

    def body(v_ref, o_ref, all_ref, send_sems, recv_sems, local_sem):
        x, y, c, chips = _place()
        me, sibling = (x, y, c), (x, y, 1 - c)

        def block(px, py, pc):
            return all_ref.at[4 * px + 2 * py + pc]

        def copy(k, blk, to, src=None):
            return _remote(block(*blk) if src is None else src, block(*blk), send_sems.at[k], recv_sems.at[k], to)

        mine = pltpu.make_async_copy(v_ref, block(*me), local_sem)
        mine.start()
        first = [copy(0, me, sibling, src=v_ref)]
        first += [copy(1 + j, me, (*chip, c), src=v_ref) for j, chip in enumerate(chips)]
        for cp in first:
            cp.start()
        passed = [copy(4 + j, (*chip, c), sibling) for j, chip in enumerate(chips)]
        for j, chip in enumerate(chips):
            copy(1 + j, (*chip, c), me).wait_recv()
            passed[j].start()
        copy(0, sibling, me).wait_recv()
        for j, chip in enumerate(chips):
            copy(4 + j, (*chip, 1 - c), me).wait_recv()
        for cp in first + passed:
            cp.wait_send()
        mine.wait()
        acc = all_ref[0]
        for k in range(1, n_dev):
            acc = acc + all_ref[k]
        o_ref[...] = acc

    vmem = pl.BlockSpec(memory_space=pltpu.VMEM)
    return pl.pallas_call(
        body, in_specs=[vmem], out_specs=vmem, out_shape=SDS((rows, lanes), F32),
        scratch_shapes=[pltpu.VMEM((n_dev, rows, lanes), F32), pltpu.SemaphoreType.DMA((7,)),
                        pltpu.SemaphoreType.DMA((7,)), pltpu.SemaphoreType.DMA],
        compiler_params=pltpu.CompilerParams(vmem_limit_bytes=VMEM_LIMIT), name=name)(v)


def _relu2_epilogue(acc):
    return acc, jnp.square(jnp.maximum(acc, 0.0))


def _res_epilogue(acc, res):
    return (acc + res,)


def _drelu2_epilogue(acc, pre):
    return (acc * (2.0 * jnp.maximum(pre.astype(F32), 0.0)),)


def _ffn_fwd(h, g, w1, w2, tag):
    f = _rms_fwd(h, g, name=f"ffn_norm_{tag}")
    pre, act = _mm_nn(f, w1, name=f"ffn1_{tag}", epilogue=_relu2_epilogue, n_out_dtypes=(BF16, BF16))
    h_out = _mm_nn(act, w2, name=f"ffn2_{tag}", extras=(h,), epilogue=_res_epilogue)
    return h_out, (f, pre, act)


def _ffn_bwd(dh, h, g, w1, w2, saved, layer, after=()):
    f, pre, act = saved
    dpre = _mm_nt(dh, w2, name=f"ffn2_dx_{layer}", out_dtype=BF16, extras=(pre,), epilogue=_drelu2_epilogue,
                  after=after)
    dw2 = _mm_tn_stacked(act, dh, name=f"ffn2_dw_{layer}", col_slots=False)
    df = _mm_nt(dpre, w1, name=f"ffn1_dx_{layer}")
    dw1 = _mm_tn_stacked(f, dpre, name=f"ffn1_dw_{layer}", col_slots=True)
    dh, dg = _rms_bwd(h, g, df, dh, name=f"ffn_norm_bwd_{layer}")
    return dh, dg, dw1, dw2


def _kv_fwd(mem, g, w_kv, tag):
    m = _rms_fwd(mem, g, name=f"mem_norm_{tag}")
    return m, _mm_nn(m, w_kv, name=f"kv_{tag}")


def _kv_bwd(mem, g, w_kv, m, dk, dv, layer):
    dkv = jnp.concatenate([dk, dv], axis=1)
    dw = _mm_tn_stacked(m, dkv, name=f"kv_dw_{layer}", col_slots=True)
    dm = _mm_nt(dkv, w_kv, name=f"kv_dx_{layer}")
    _, dg = _rms_bwd(mem, g, dm, dm, name=f"mem_norm_bwd_{layer}")
    return dw, dg


def _local_step(x, mem, target, p, after_layer1=None, after_ffn0=None, after_mixer0=None):
    row = lambda v: v.reshape(1, -1)
    g = {}

    h0 = x
    a0 = _rms_fwd(h0, row(p["norm_mix"][0]), name="mix_norm_0")
    proj_a = _mm_nn(a0, p["a_in"], name="a_in", after=p.get("after_start", ()))
    m0, kv0 = _kv_fwd(mem, row(p["mem_norm"][0]), p["w_kv"][0], "0")
    cat0 = _attn_fwd(proj_a, 2 * D_INNER, kv0, name="attn_0")
    bs_col = p["a_bs"].reshape(A_GROUPS, CHUNK, 1)
    cat0 = _gate_fwd(proj_a, p["a_ln_g"], p["a_ln_b"], p["a_ws"], bs_col, cat0, name="gate")
    h1 = _mm_nn(cat0, p["w_out"][0], name="out_0", extras=(h0,), epilogue=_res_epilogue)
    h2, ffn0 = _ffn_fwd(h1, row(p["norm_ffn"][0]), p["w_ffn1"][0], p["w_ffn2"][0], "0")

    if "layer1_mixer" in p:
        w_kv1, w_out1, b_in = p["layer1_mixer"](h2)
    else:
        w_kv1, w_out1, b_in = p["w_kv"][1], p["w_out"][1], p["b_in"]
    a1 = _rms_fwd(h2, row(p["norm_mix"][1]), name="mix_norm_1")
    proj_b = _mm_nn(a1, b_in, name="b_in")
    m1, kv1 = _kv_fwd(mem, row(p["mem_norm"][1]), w_kv1, "1")
    cat1 = _attn_fwd(proj_b, B_Q_OFF, kv1, name="attn_1")
    xbc = _conv_fwd(proj_b, p["b_conv_w"], p["b_conv_b"], name="conv")
    dt_raw = proj_b[:, B_DT_OFF:B_DT_OFF + SSM_HEADS].reshape(SEQ, SSM_GROUPS, SSM_HPG)
    dt_c = jnp.transpose(dt_raw, (1, 0, 2))
    dt_r = jnp.transpose(dt_raw, (1, 2, 0))
    per_head = lambda v: v.reshape(SSM_GROUPS, 1, SSM_HPG)
    par_row = jnp.concatenate([per_head(p["b_dt_bias"]), per_head(p["b_a_log"]), per_head(p["b_d"])], axis=1)
    ssd_par = (par_row, jnp.transpose(par_row[:, :2], (0, 2, 1)), p["b_gnorm"])
    cat1, hprev = _ssd_fwd(xbc, proj_b, dt_c, dt_r, *ssd_par, cat1, name="ssd")
    h3 = _mm_nn(cat1, w_out1, name="out_1", extras=(h2,), epilogue=_res_epilogue)
    w_ffn1_1, w_ffn2_1 = p["layer1_ffn"](h3) if "layer1_ffn" in p else (p["w_ffn1"][1], p["w_ffn2"][1])
    h4, ffn1 = _ffn_fwd(h3, row(p["norm_ffn"][1]), w_ffn1_1, w_ffn2_1, "1")

    loss, dh, g["final_norm"] = _loss_head(h4, row(p["final_norm"]), target, name="loss_head")

    dh, dnf1, dw1_1, dw2_1 = _ffn_bwd(dh, h3, row(p["norm_ffn"][1]), w_ffn1_1, w_ffn2_1, ffn1, 1)
    dcat1 = _mm_nt(dh, w_out1, name="out_dx_1")
    dwo_1 = _mm_tn_stacked(cat1, dh, name="out_dw_1", col_slots=False)
    dproj_b, dk1, dv1 = _attn_bwd(proj_b, B_Q_OFF, kv1, dcat1, B_IN_PAD, B_Q_OFF, name="attn_bwd_1")
    dproj_b, dxs, dbm, dcm, ddt_c, ddt_r, dpar_row, dpar_col, g["b_gnorm"] = _ssd_bwd(
        xbc, proj_b, dt_c, dt_r, *ssd_par, hprev, dcat1, dproj_b, name="ssd_bwd")
    dpar = dpar_row.at[:, :2].add(jnp.transpose(dpar_col, (0, 2, 1)))
    g["b_dt_bias"], g["b_a_log"], g["b_d"] = dpar[:, 0], dpar[:, 1], dpar[:, 2]
    dproj_b, g["b_conv_w"], g["b_conv_b"] = _conv_bwd(proj_b, p["b_conv_w"], p["b_conv_b"], dxs, dbm, dcm, dproj_b,
                                                      name="conv_bwd")
    ddt = jnp.transpose(ddt_c, (1, 0, 2)) + jnp.transpose(ddt_r, (2, 0, 1))
    ddt = jnp.pad(ddt.reshape(SEQ, SSM_HEADS), ((0, 0), (0, B_IN_PAD - B_DT_OFF - SSM_HEADS))).astype(BF16)
    dproj_b = lax.dynamic_update_slice(dproj_b, ddt, (0, B_DT_OFF))
    dwkv_1, dmn1 = _kv_bwd(mem, row(p["mem_norm"][1]), w_kv1, m1, dk1, dv1, 1)
    dwb = _b_in_grad_slots(_mm_tn(a1, dproj_b, name="b_in_dw"))
    da1 = _mm_nt(dproj_b, b_in, name="b_in_dx")
    dh, dnm1 = _rms_bwd(h2, row(p["norm_mix"][1]), da1, dh, name="mix_norm_bwd_1")
    layer1 = dict(w_kv=dwkv_1, w_out=dwo_1, w_ffn1=dw1_1, w_ffn2=dw2_1, b_in=dwb)
    token = () if after_layer1 is None else (after_layer1(layer1),)

    dh, dnf0, dw1_0, dw2_0 = _ffn_bwd(dh, h1, row(p["norm_ffn"][0]), p["w_ffn1"][0], p["w_ffn2"][0], ffn0, 0,
                                      after=token)
    ffn0_grads = dict(w_ffn1=dw1_0, w_ffn2=dw2_0)
    token = () if after_ffn0 is None else (after_ffn0(ffn0_grads),)
    dcat0 = _mm_nt(dh, p["w_out"][0], name="out_dx_0", after=token)
    dwo_0 = _mm_tn_stacked(cat0, dh, name="out_dw_0", col_slots=False)
    dproj_a, dk0, dv0 = _attn_bwd(proj_a, 2 * D_INNER, kv0, dcat0, A_IN, 2 * D_INNER, name="attn_bwd_0")
    dproj_a, g["a_ln_g"], g["a_ln_b"], g["a_ws"], dbs_col = _gate_bwd(
        proj_a, p["a_ln_g"], p["a_ln_b"], p["a_ws"], bs_col, dcat0, dproj_a, name="gate_bwd")
    g["a_bs"] = dbs_col.reshape(A_GROUPS, CHUNK)
    dwkv_0, dmn0 = _kv_bwd(mem, row(p["mem_norm"][0]), p["w_kv"][0], m0, dk0, dv0, 0)
    dwa = _mm_tn_stacked(a0, dproj_a, name="a_in_dw", col_slots=True)
    mixer0_grads = dict(w_kv=dwkv_0, w_out=dwo_0, a_in=dwa)
    token = () if after_mixer0 is None else (after_mixer0(mixer0_grads),)
    da0 = _mm_nt(dproj_a, p["a_in"], name="a_in_dx", after=token)
    dx, dnm0 = _rms_bwd(h0, row(p["norm_mix"][0]), da0, dh, name="mix_norm_bwd_0")

    g["norm_mix"] = jnp.concatenate([dnm0, dnm1], axis=0)
    g["norm_ffn"] = jnp.concatenate([dnf0, dnf1], axis=0)
    g["mem_norm"] = jnp.concatenate([dmn0, dmn1], axis=0)
    layer0 = dict(w_kv=dwkv_0, w_out=dwo_0, w_ffn1=dw1_0, w_ffn2=dw2_0, a_in=dwa)
    return loss, dx, g, layer0, layer1


def _b_in_full(gathered):
    n = B_IN // N_CHIPS
    dt0 = D_INNER + CONV_DIM - (N_CHIPS - 1) * n
    last = gathered[N_CHIPS - 1]
    return jnp.concatenate([*[gathered[k] for k in range(N_CHIPS - 1)], last[:, :dt0], last[:, dt0 + SSM_HEADS:],
                            last[:, dt0:dt0 + SSM_HEADS], jnp.zeros((D_MODEL, B_IN_PAD - B_IN), last.dtype)], axis=1)


def _b_in_grad_slots(d):
    n = B_IN // N_CHIPS
    dt0 = D_INNER + CONV_DIM
    last = jnp.concatenate([d[:, (N_CHIPS - 1) * n:dt0], d[:, B_DT_OFF:B_DT_OFF + SSM_HEADS], d[:, dt0:B_DT_OFF]], axis=1)
    slots = [*[d[:, k * n:(k + 1) * n] for k in range(N_CHIPS - 1)], last]
    half = D_MODEL // 2
    return jnp.stack([jnp.stack([s[h * half:(h + 1) * half] for s in slots]) for h in range(2)])


LARGE = ("w_kv", "w_out", "w_ffn1", "w_ffn2", "a_in", "b_in")
SMALL_REPL = ("norm_mix", "norm_ffn", "mem_norm", "a_ln_g", "a_ln_b", "a_ws", "a_bs", "b_dt_bias", "b_a_log", "b_d",
              "final_norm")
SMALL_SHARD = ("b_conv_w", "b_conv_b", "b_gnorm")
WEIGHTS = ("norm_mix", "norm_ffn", "mem_norm", "w_kv", "w_out", "w_ffn1", "w_ffn2", "a_in", "a_ln_g", "a_ln_b", "a_ws",
           "a_bs", "b_in", "b_conv_w", "b_conv_b", "b_dt_bias", "b_a_log", "b_d", "b_gnorm", "final_norm")
CONV_SHARD = CONV_DIM // N_CHIPS
GN_SHARD = D_INNER // N_CHIPS


LAYERED = ("w_kv", "w_out", "w_ffn1", "w_ffn2")
LAYER_TENSORS = (("w_kv", "w_out", "w_ffn1", "w_ffn2", "a_in"), ("w_kv", "w_out", "w_ffn1", "w_ffn2", "b_in"))


def _gather_weights(w):
    halves = lambda k, layer: (w[k][layer] if k in LAYERED else w[k][0]).reshape(2, -1, w[k].shape[-1]).astype(BF16)
    small = jnp.zeros((2, CONV_K, CONV_SHARD), F32)
    small = small.at[0].set(w["b_conv_w"][0])
    small = small.at[1, 0].set(w["b_conv_b"][0])
    small = small.at[1, 1, :GN_SHARD].set(w["b_gnorm"][0])
    gathered = _all_gather_shards([halves(k, 0) for k in LAYER_TENSORS[0]], small, name="gather_weights_0")
    got = dict(zip(LAYER_TENSORS[0], gathered))
    slots = lambda a: a.reshape(N_CHIPS, -1, a.shape[-1])
    rows = lambda a: a.reshape(-1, a.shape[-1])
    p = dict(w_kv=[slots(got["w_kv"])], w_out=[rows(got["w_out"])], w_ffn1=[slots(got["w_ffn1"])],
             w_ffn2=[rows(got["w_ffn2"])], a_in=slots(got["a_in"]))
    sm = gathered[-1]
    p["b_conv_w"] = jnp.transpose(sm[:, 0], (1, 0, 2)).reshape(CONV_K, CONV_DIM)
    p["b_conv_b"] = sm[:, 1, 0].reshape(1, CONV_DIM)
    p["b_gnorm"] = sm[:, 1, 1, :GN_SHARD].reshape(1, D_INNER)

    after, started = (gathered[0],), {}
    for tag, names in (("mixer", ("w_kv", "w_out", "b_in")), ("ffn", ("w_ffn1", "w_ffn2"))):
        started[tag] = _gather_start([halves(k, 1) for k in names], after, name=f"gather_start_1_{tag}")
        after = (started[tag][-1],)
    p["after_start"] = after

    def finish(tag, first):
        send_sems, recv_sems, shards, zones, _ = started[tag]
        shards, zones = _gather_wait(send_sems, recv_sems, shards, zones, (first,), name=f"gather_wait_1_{tag}")
        return _gather_finish(shards, zones, name=f"gather_finish_1_{tag}")

    def layer1_mixer(first):
        kv, wo, b_in = finish("mixer", first)
        return slots(kv), rows(wo), _b_in_full(slots(b_in))

    def layer1_ffn(first):
        w1, w2 = finish("ffn", first)
        return slots(w1), rows(w2)

    p.update(layer1_mixer=layer1_mixer, layer1_ffn=layer1_ffn)
    return p


def _pair_parts(grads, tag):
    stacks = [g.reshape(2, -1, g.shape[-1]) for g in grads.values()]
    parts = _pair_reduce(stacks, name=f"grads_pair_reduce_{tag}")
    return [t.reshape(N_CHIPS, -1, t.shape[-1]) for t in parts]


def _chip_sums(chip, names, parts, landed, tag):
    return {k: _sum_contributions(chip, t, u, name=f"grads_chip_sum_{k}_{tag}")
            for k, t, u in zip(names, parts, landed)}


def _small_layout(shapes):
    offs, o = {}, 0
    for k in (*SMALL_REPL, *SMALL_SHARD):
        size = math.prod(shapes[k])
        offs[k] = (o, size)
        o += size
    rows = -(-o // (8 * 128)) * 8
    return offs, rows


def _reduce_small(g, full_shapes):
    offs, rows = _small_layout(full_shapes)
    flat = jnp.concatenate([g[k].reshape(-1) for k in (*SMALL_REPL, *SMALL_SHARD)])
    flat = jnp.pad(flat, (0, rows * 128 - flat.shape[0])).reshape(rows, 128)
    total = _all_reduce_small(flat, name="grads_small_all_reduce").reshape(-1)
    return {k: total[o:o + n].reshape(full_shapes[k]) for k, (o, n) in offs.items()}


def kernel(x, mem, norm_mix, norm_ffn, mem_norm, w_kv, w_out, w_ffn1, w_ffn2, a_in, a_ln_g, a_ln_b, a_ws, a_bs, b_in, b_conv_w, b_conv_b, b_dt_bias, b_a_log, b_d, b_gnorm, final_norm, loss_target, m_norm_mix, m_norm_ffn, m_mem_norm, m_w_kv, m_w_out, m_w_ffn1, m_w_ffn2, m_a_in, m_a_ln_g, m_a_ln_b, m_a_ws, m_a_bs, m_b_in, m_b_conv_w, m_b_conv_b, m_b_dt_bias, m_b_a_log, m_b_d, m_b_gnorm, m_final_norm, v_norm_mix, v_norm_ffn, v_mem_norm, v_w_kv, v_w_out, v_w_ffn1, v_w_ffn2, v_a_in, v_a_ln_g, v_a_ln_b, v_a_ws, v_a_bs, v_b_in, v_b_conv_w, v_b_conv_b, v_b_dt_bias, v_b_a_log, v_b_d, v_b_gnorm, v_final_norm):
    w = dict(norm_mix=norm_mix, norm_ffn=norm_ffn, mem_norm=mem_norm, w_kv=w_kv, w_out=w_out, w_ffn1=w_ffn1,
             w_ffn2=w_ffn2, a_in=a_in, a_ln_g=a_ln_g, a_ln_b=a_ln_b, a_ws=a_ws, a_bs=a_bs, b_in=b_in, b_conv_w=b_conv_w,
             b_conv_b=b_conv_b, b_dt_bias=b_dt_bias, b_a_log=b_a_log, b_d=b_d, b_gnorm=b_gnorm, final_norm=final_norm)
    mom = dict(norm_mix=m_norm_mix, norm_ffn=m_norm_ffn, mem_norm=m_mem_norm, w_kv=m_w_kv, w_out=m_w_out,
               w_ffn1=m_w_ffn1, w_ffn2=m_w_ffn2, a_in=m_a_in, a_ln_g=m_a_ln_g, a_ln_b=m_a_ln_b, a_ws=m_a_ws,
               a_bs=m_a_bs, b_in=m_b_in, b_conv_w=m_b_conv_w, b_conv_b=m_b_conv_b, b_dt_bias=m_b_dt_bias,
               b_a_log=m_b_a_log, b_d=m_b_d, b_gnorm=m_b_gnorm, final_norm=m_final_norm)
    var = dict(norm_mix=v_norm_mix, norm_ffn=v_norm_ffn, mem_norm=v_mem_norm, w_kv=v_w_kv, w_out=v_w_out,
               w_ffn1=v_w_ffn1, w_ffn2=v_w_ffn2, a_in=v_a_in, a_ln_g=v_a_ln_g, a_ln_b=v_a_ln_b, a_ws=v_a_ws,
               a_bs=v_a_bs, b_in=v_b_in, b_conv_w=v_b_conv_w, b_conv_b=v_b_conv_b, b_dt_bias=v_b_dt_bias,
               b_a_log=v_b_a_log, b_d=v_b_d, b_gnorm=v_b_gnorm, final_norm=v_final_norm)

    p = _gather_weights(w)
    p.update(norm_mix=norm_mix, norm_ffn=norm_ffn, mem_norm=mem_norm, a_ln_g=a_ln_g, a_ln_b=a_ln_b, a_ws=a_ws[0],
             a_bs=a_bs[0], b_dt_bias=b_dt_bias, b_a_log=b_a_log, b_d=b_d, final_norm=final_norm)
    chip = 2 * lax.axis_index("x") + lax.axis_index("y")
    chip_arr = jnp.reshape(chip, (1,)).astype(jnp.int32)
    started = {}

    def start_scatter(tag):
        def hook(grads):
            start = _chip_scatter_start(_pair_parts(grads, tag), name=f"grads_chip_scatter_start_{tag}")
            started[tag] = (tuple(grads), start)
            return start[-1]
        return hook

    loss_part, dx, g, _, _ = _local_step(x[0], mem[0], loss_target[0], p, start_scatter("1"), start_scatter("0f"),
                                         start_scatter("0m"))
    loss = lax.psum(loss_part[0, 0], ("x", "y", "c"))

    def finish_scatter(tag, first):
        names, (send_sems, recv_sems, parts, lands, _) = started[tag]
        parts, landed = _chip_scatter_wait(send_sems, recv_sems, parts, lands, (first,),
                                           name=f"grads_chip_scatter_wait_{tag}")
        return _chip_sums(chip_arr, names, parts, landed, tag)

    def adamw(names, grads):
        for k in names:
            shape = w[k].shape
            if len(shape) == 3 and shape[2] % 128 and not shape[1] % 128:
                flat = unflat = lambda a: jnp.transpose(a, (0, 2, 1))
            else:
                flat = (lambda a: a) if len(shape) == 3 else (lambda a: a.reshape(1, -1, shape[-1]))
                unflat = lambda a: a.reshape(shape)
            d, m_new, v_new = _adamw(flat(w[k]), flat(grads[k]), flat(mom[k]), flat(var[k]), name=f"adamw_{k}")
            delta[k], new_m[k], new_v[k] = unflat(d), unflat(m_new), unflat(v_new)

    full_shapes = {k: w[k].shape for k in SMALL_REPL}
    full_shapes.update(b_conv_w=(1, CONV_K, CONV_DIM), b_conv_b=(1, CONV_DIM), b_gnorm=(1, D_INNER))
    grads = _reduce_small(g, full_shapes)
    grads["b_conv_w"] = lax.dynamic_slice_in_dim(grads["b_conv_w"], chip * CONV_SHARD, CONV_SHARD, axis=2)
    grads["b_conv_b"] = lax.dynamic_slice_in_dim(grads["b_conv_b"], chip * CONV_SHARD, CONV_SHARD, axis=1)
    grads["b_gnorm"] = lax.dynamic_slice_in_dim(grads["b_gnorm"], chip * GN_SHARD, GN_SHARD, axis=1)
    delta, new_m, new_v = {}, {}, {}
    halves = [finish_scatter("0f", dx), finish_scatter("1", dx)]
    early = ("w_ffn1", "w_ffn2", "b_in")
    shared = _pair_share([[halves[layer][k] for layer in range(2) if k in halves[layer]] for k in early],
                         name="grads_pair_share_early")
    grads.update({k: a.reshape(w[k].shape) for k, a in zip(early, shared)})
    adamw([k for k in WEIGHTS if k in grads], grads)
    halves[0].update(finish_scatter("0m", delta["w_ffn2"]))
    late = ("w_kv", "w_out", "a_in")
    shared = _pair_share([[halves[layer][k] for layer in range(2) if k in halves[layer]] for k in late],
                         name="grads_pair_share_late")
    grads.update({k: a.reshape(w[k].shape) for k, a in zip(late, shared)})
    adamw(late, grads)

    return (loss, dx.reshape(x.shape), *[grads[k] for k in WEIGHTS], *[delta[k] for k in WEIGHTS],
            *[new_m[k] for k in WEIGHTS], *[new_v[k] for k in WEIGHTS])
```

```python
import math

import jax
import jax.numpy as jnp
from jax import lax
from jax.experimental import pallas as pl
from jax.experimental.pallas import tpu as pltpu

F32 = jnp.float32
BF16 = jnp.bfloat16
SDS = jax.ShapeDtypeStruct

D_MODEL = 1024
SEQ = 2048
CHUNK = 128
N_MEM = 256
D_INNER = 2048
A_GROUPS = 8
A_GROUP_W = D_INNER // A_GROUPS
SSM_HEADS = 32
SSM_HEAD_DIM = 64
SSM_GROUPS = 4
SSM_HPG = 8
SSM_STATE = 128
SSM_GROUP_W = SSM_HPG * SSM_HEAD_DIM
CONV_K = 4
CONV_DIM = 3072
X_HEADS = 4
X_HEAD_DIM = 256
X_WIDTH = 1024
MIX_OUT = 3072
D_FF = 4096
A_IN = 5120
B_IN = 6176
B_IN_PAD = 6272
B_Q_OFF = 5120
B_DT_OFF = 6144
N_CHUNKS = SEQ // CHUNK
EPS = 1e-6
N_CHIPS = 4

ADAM_LR = 0.001
ADAM_B1 = 0.9
ADAM_B2 = 0.999
ADAM_EPS = 1e-08
ADAM_WD = 0.01
ADAM_STEP = 10

VMEM_LIMIT = 48 * 1024 * 1024
MESH = pl.DeviceIdType.MESH


def _cparams(sem):
    return pltpu.CompilerParams(dimension_semantics=sem, vmem_limit_bytes=VMEM_LIMIT)


def _dot(a, b, dims=(((1,), (0,)), ((), ()))):
    return lax.dot_general(a.astype(BF16), b.astype(BF16), dims, preferred_element_type=F32)


def _dot_nt(a, b):
    return _dot(a, b, (((1,), (1,)), ((), ())))


def _dot_tn(a, b):
    return _dot(a, b, (((0,), (0,)), ((), ())))


def _pick(n, cands):
    for c in cands:
        if n % c == 0:
            return c
    raise ValueError(f"no tile for {n}")


def _mm_call(a, b, *, dims, grid, a_spec, b_spec, acc_shape, out_shapes, out_specs, name,
             extras=(), extra_specs=(), epilogue=None, after=()):
    n_k = grid[2]
    n_extra = len(extras)
    n_out = len(out_shapes)
    n_in = 2 + n_extra + len(after)

    def finish(total, extra_refs, out_refs):
        vals = (total,) if epilogue is None else epilogue(total, *[e[...] for e in extra_refs])
        for o_ref, v in zip(out_refs, vals):
            o_ref[...] = v.astype(o_ref.dtype)

    def body_one_step(*refs):
        finish(_dot(refs[0][...], refs[1][...], dims), refs[2:2 + n_extra], refs[n_in:n_in + n_out])

    def body(*refs):
        acc = refs[-1]
        k = pl.program_id(2)

        @pl.when(k == 0)
        def _():
            acc[...] = jnp.zeros_like(acc)

        acc[...] += _dot(refs[0][...], refs[1][...], dims)

        @pl.when(k == n_k - 1)
        def _():
            finish(acc[...], refs[2:2 + n_extra], refs[n_in:n_in + n_out])

    return pl.pallas_call(
        body_one_step if n_k == 1 else body, grid=grid,
        in_specs=[a_spec, b_spec, *extra_specs, *([ANY] * len(after))], out_specs=list(out_specs),
        out_shape=list(out_shapes), scratch_shapes=[] if n_k == 1 else [pltpu.VMEM(acc_shape, F32)],
        compiler_params=_cparams(("parallel", "parallel", "arbitrary")), name=name,
    )(a, b, *extras, *after)


def _w_dims(w):
    if w.ndim == 2:
        return w.shape[0], w.shape[1], 1, w.shape[1]
    return w.shape[1], w.shape[0] * w.shape[2], w.shape[0], w.shape[2]


def _mm_nn(a, w, *, name, out_dtype=F32, a_cols=None, extras=(), epilogue=None, n_out_dtypes=None, after=()):
    m = a.shape[0]
    k_dim, n_dim, _, n_slot = _w_dims(w)
    a_off, a_w = (0, a.shape[1]) if a_cols is None else a_cols
    assert a_w == k_dim
    tm = _pick(m, (2048, 1024, 512, 256))
    tn = _pick(n_slot, (512, 896, 640, 256, 128))
    tk = _pick(k_dim, (1024, 768, 512, 384, 256, 128))
    assert a_off % tk == 0
    nb = n_slot // tn
    a_spec = pl.BlockSpec((tm, tk), lambda i, j, k: (i, a_off // tk + k))
    if w.ndim == 2:
        b_spec = pl.BlockSpec((tk, tn), lambda i, j, k: (k, j))
    else:
        b_spec = pl.BlockSpec((None, tk, tn), lambda i, j, k: (j // nb, k, j % nb))
    o_spec = pl.BlockSpec((tm, tn), lambda i, j, k: (i, j))
    dts = n_out_dtypes or (out_dtype,)
    outs = _mm_call(a, w, dims=(((1,), (0,)), ((), ())), grid=(m // tm, n_dim // tn, k_dim // tk),
                    a_spec=a_spec, b_spec=b_spec, acc_shape=(tm, tn),
                    out_shapes=[SDS((m, n_dim), dt) for dt in dts], out_specs=[o_spec] * len(dts), name=name,
                    extras=extras, extra_specs=[o_spec] * len(extras), epilogue=epilogue, after=after)
    return outs if n_out_dtypes else outs[0]


def _mm_nt(a, w, *, name, out_dtype=F32, extras=(), epilogue=None, after=()):
    m = a.shape[0]
    k_dim, n_dim, _, n_slot = _w_dims(w)
    assert a.shape[1] == n_dim
    tm = _pick(m, (2048, 1024, 512, 256))
    to = _pick(k_dim, (512, 384, 256, 128))
    tc = _pick(n_slot, (1280, 1024, 896, 640, 512, 256, 128))
    nb = n_slot // tc
    a_spec = pl.BlockSpec((tm, tc), lambda i, j, k: (i, k))
    if w.ndim == 2:
        b_spec = pl.BlockSpec((to, tc), lambda i, j, k: (j, k))
    else:
        b_spec = pl.BlockSpec((None, to, tc), lambda i, j, k: (k // nb, j, k % nb))
    o_spec = pl.BlockSpec((tm, to), lambda i, j, k: (i, j))
    return _mm_call(a, w, dims=(((1,), (1,)), ((), ())), grid=(m // tm, k_dim // to, n_dim // tc),
                    a_spec=a_spec, b_spec=b_spec, acc_shape=(tm, to),
                    out_shapes=[SDS((m, k_dim), out_dtype)], out_specs=[o_spec], name=name,
                    extras=extras, extra_specs=[o_spec] * len(extras), epilogue=epilogue, after=after)[0]


def _mm_tn(x, dy, *, name, x_cols=None):
    s = x.shape[0]
    x_off, k_dim = (0, x.shape[1]) if x_cols is None else x_cols
    n_dim = dy.shape[1]
    tm = _pick(k_dim, (1024, 768, 512, 384, 256, 128))
    tn = _pick(n_dim, (512, 896, 640, 256, 128))
    tk = _pick(s, (2048, 1024, 512, 256))
    assert x_off % tm == 0
    a_spec = pl.BlockSpec((tk, tm), lambda i, j, k: (k, x_off // tm + i))
    b_spec = pl.BlockSpec((tk, tn), lambda i, j, k: (k, j))
    o_spec = pl.BlockSpec((tm, tn), lambda i, j, k: (i, j))
    return _mm_call(x, dy, dims=(((0,), (0,)), ((), ())), grid=(k_dim // tm, n_dim // tn, s // tk),
                    a_spec=a_spec, b_spec=b_spec, acc_shape=(tm, tn),
                    out_shapes=[SDS((k_dim, n_dim), F32)], out_specs=[o_spec], name=name)[0]


def _mm_tn_stacked(x, dy, *, name, col_slots):
    s, k_dim = x.shape
    n_dim = dy.shape[1]
    r, c = (k_dim // 2, n_dim // N_CHIPS) if col_slots else (k_dim // N_CHIPS // 2, n_dim)
    tm = 2 * r
    tn = _pick(c, (512, 896, 640, 256, 128))
    tk = _pick(s, (2048, 1024, 512, 256))
    a_spec = pl.BlockSpec((tk, tm), lambda i, j, k: (k, i))
    b_spec = pl.BlockSpec((tk, tn), lambda i, j, k: (k, j))
    if col_slots:
        nb = c // tn
        o_spec = pl.BlockSpec((2, None, r, tn), lambda i, j, k: (0, j // nb, 0, j % nb))
    else:
        o_spec = pl.BlockSpec((2, None, r, tn), lambda i, j, k: (0, i, 0, j))
    return _mm_call(x, dy, dims=(((0,), (0,)), ((), ())), grid=(k_dim // tm, n_dim // tn, s // tk),
                    a_spec=a_spec, b_spec=b_spec, acc_shape=(tm, tn), epilogue=lambda acc: (acc.reshape(2, r, tn),),
                    out_shapes=[SDS((2, N_CHIPS, r, c), F32)], out_specs=[o_spec], name=name)[0]


def _rms(x, g):
    return x * lax.rsqrt(jnp.mean(x * x, axis=-1, keepdims=True) + EPS) * g


def _rms_fwd(h, g, *, name):
    rows, d = h.shape
    tr = _pick(rows, (512, 256))

    def body(h_ref, g_ref, o_ref):
        o_ref[...] = _rms(h_ref[...], g_ref[...]).astype(o_ref.dtype)

    return pl.pallas_call(
        body, grid=(rows // tr,),
        in_specs=[pl.BlockSpec((tr, d), lambda i: (i, 0)), pl.BlockSpec((1, d), lambda i: (0, 0))],
        out_specs=pl.BlockSpec((tr, d), lambda i: (i, 0)), out_shape=SDS((rows, d), BF16),
        compiler_params=_cparams(("parallel",)), name=name)(h, g)


def _rms_bwd(h, g, da, dres, *, name):
    rows, d = h.shape
    tr = _pick(rows, (512, 256))

    def body(h_ref, g_ref, da_ref, dres_ref, dh_ref, dg_ref):
        _, vjp = jax.vjp(_rms, h_ref[...], g_ref[...])
        dh, dg = vjp(da_ref[...].astype(F32))
        dh_ref[...] = dres_ref[...] + dh

        @pl.when(pl.program_id(0) == 0)
        def _():
            dg_ref[...] = jnp.zeros_like(dg_ref)

        dg_ref[...] += dg

    row_spec = pl.BlockSpec((tr, d), lambda i: (i, 0))
    vec_spec = pl.BlockSpec((1, d), lambda i: (0, 0))
    return pl.pallas_call(
        body, grid=(rows // tr,), in_specs=[row_spec, vec_spec, row_spec, row_spec],
        out_specs=[row_spec, vec_spec], out_shape=[SDS((rows, d), F32), SDS((1, d), F32)],
        compiler_params=_cparams(("arbitrary",)), name=name)(h, g, da, dres)


def _loss_head(h, g, target, *, name):
    rows, d = h.shape
    tr = _pick(rows, (512, 256))

    def body(h_ref, g_ref, t_ref, loss_ref, dh_ref, dg_ref):
        y, vjp = jax.vjp(_rms, h_ref[...], g_ref[...])
        err = y - t_ref[...]
        dh, dg = vjp(err * (1.0 / d))
        dh_ref[...] = dh

        @pl.when(pl.program_id(0) == 0)
        def _():
            dg_ref[...] = jnp.zeros_like(dg_ref)
            loss_ref[...] = jnp.zeros_like(loss_ref)

        dg_ref[...] += dg
        part = jnp.sum(jnp.sum(err * err, axis=-1, keepdims=True), axis=0, keepdims=True) * (0.5 / d)
        loss_ref[...] += jnp.broadcast_to(part, loss_ref.shape)

    row_spec = pl.BlockSpec((tr, d), lambda i: (i, 0))
    vec_spec = pl.BlockSpec((1, d), lambda i: (0, 0))
    loss_spec = pl.BlockSpec((8, 128), lambda i: (0, 0))
    return pl.pallas_call(
        body, grid=(rows // tr,), in_specs=[row_spec, vec_spec, row_spec],
        out_specs=[loss_spec, row_spec, vec_spec],
        out_shape=[SDS((8, 128), F32), SDS((rows, d), F32), SDS((1, d), F32)],
        compiler_params=_cparams(("arbitrary",)), name=name)(h, g, target)


def _gelu(x):
    return 0.5 * x * (1.0 + lax.erf(x * (1.0 / math.sqrt(2.0))))


def _gate_tile(pu, pv, ln_g, ln_b, ws, bs_t):
    u = [_gelu(p) for p in pu]
    v = [_gelu(p) for p in pv]
    mu = sum(jnp.sum(t, axis=-1, keepdims=True) for t in v) * (1.0 / D_INNER)
    vc = [t - mu for t in v]
    var = sum(jnp.sum(t * t, axis=-1, keepdims=True) for t in vc) * (1.0 / D_INNER)
    rstd = lax.rsqrt(var + EPS)
    row = lax.broadcasted_iota(jnp.int32, (CHUNK, CHUNK), 0)
    col = lax.broadcasted_iota(jnp.int32, (CHUNK, CHUNK), 1)
    out = []
    for gi in range(A_GROUPS):
        vn = vc[gi] * rstd * ln_g[gi] + ln_b[gi]
        w = jnp.where(row >= col, ws[gi], 0.0)
        sv = _dot(w, vn) + bs_t[gi]
        out.append(u[gi] * sv)
    return out


def _split(ref, n, width):
    return [ref[:, i * width:(i + 1) * width] for i in range(n)]


def _gate_in_specs():
    return [
        pl.BlockSpec((CHUNK, D_INNER), lambda c: (c, 0)),
        pl.BlockSpec((CHUNK, D_INNER), lambda c: (c, 1)),
        pl.BlockSpec((1, D_INNER), lambda c: (0, 0)),
        pl.BlockSpec((1, D_INNER), lambda c: (0, 0)),
        pl.BlockSpec((A_GROUPS, CHUNK, CHUNK), lambda c: (0, 0, 0)),
        pl.BlockSpec((A_GROUPS, CHUNK, 1), lambda c: (0, 0, 0)),
    ]


def _gate_args(u_ref, v_ref, g_ref, b_ref, ws_ref, bs_ref):
    ng, gw = A_GROUPS, A_GROUP_W
    return (_split(u_ref, ng, gw), _split(v_ref, ng, gw), _split(g_ref, ng, gw), _split(b_ref, ng, gw),
            [ws_ref[i] for i in range(ng)], [bs_ref[i] for i in range(ng)])


def _gate_fwd(proj, ln_g, ln_b, ws, bs_col, mixcat, *, name):
    def body(u_ref, v_ref, g_ref, b_ref, ws_ref, bs_ref, cat_in, cat_ref):
        del cat_in
        out = _gate_tile(*_gate_args(u_ref, v_ref, g_ref, b_ref, ws_ref, bs_ref))
        for gi, o in enumerate(out):
            cat_ref[:, gi * A_GROUP_W:(gi + 1) * A_GROUP_W] = o.astype(cat_ref.dtype)

    return pl.pallas_call(
        body, grid=(N_CHUNKS,), in_specs=[*_gate_in_specs(), pl.BlockSpec(memory_space=pl.ANY)],
        out_specs=pl.BlockSpec((CHUNK, D_INNER), lambda c: (c, 0)), out_shape=SDS(mixcat.shape, mixcat.dtype),
        input_output_aliases={6: 0}, compiler_params=_cparams(("parallel",)), name=name,
    )(proj, proj, ln_g, ln_b, ws, bs_col, mixcat)


def _gate_bwd(proj, ln_g, ln_b, ws, bs_col, dcat, dproj, *, name):
    ng, gw = A_GROUPS, A_GROUP_W

    def body(u_ref, v_ref, g_ref, b_ref, ws_ref, bs_ref, d_ref, dproj_in, dproj_ref, dg_ref, db_ref, dws_ref, dbs_ref):
        del dproj_in
        args = _gate_args(u_ref, v_ref, g_ref, b_ref, ws_ref, bs_ref)
        _, vjp = jax.vjp(_gate_tile, *args)
        dpu, dpv, dg, db, dws, dbs = vjp(_split(d_ref, ng, gw))
        for gi in range(ng):
            dproj_ref[:, gi * gw:(gi + 1) * gw] = dpu[gi].astype(dproj_ref.dtype)
            dproj_ref[:, D_INNER + gi * gw:D_INNER + (gi + 1) * gw] = dpv[gi].astype(dproj_ref.dtype)

        @pl.when(pl.program_id(0) == 0)
        def _():
            for r in (dg_ref, db_ref, dws_ref, dbs_ref):
                r[...] = jnp.zeros_like(r)

        for gi in range(ng):
            dg_ref[:, gi * gw:(gi + 1) * gw] += dg[gi]
            db_ref[:, gi * gw:(gi + 1) * gw] += db[gi]
            dws_ref[gi] += dws[gi]
            dbs_ref[gi] += dbs[gi]

    in_specs = _gate_in_specs()
    return pl.pallas_call(
        body, grid=(N_CHUNKS,),
        in_specs=[*in_specs, pl.BlockSpec((CHUNK, D_INNER), lambda c: (c, 0)), pl.BlockSpec(memory_space=pl.ANY)],
        out_specs=[pl.BlockSpec((CHUNK, 2 * D_INNER), lambda c: (c, 0)), *in_specs[2:]],
        out_shape=[SDS(dproj.shape, dproj.dtype), SDS((1, D_INNER), F32), SDS((1, D_INNER), F32),
                   SDS((ng, CHUNK, CHUNK), F32), SDS((ng, CHUNK, 1), F32)],
        input_output_aliases={7: 0}, compiler_params=_cparams(("arbitrary",)), name=name,
    )(proj, proj, ln_g, ln_b, ws, bs_col, dcat, dproj)


ATT_TQ = 512


def _attn_tile(q, k, v):
    s = _dot_nt(q, k) * (1.0 / math.sqrt(X_HEAD_DIM))
    s = s - jnp.max(s, axis=-1, keepdims=True)
    e = jnp.exp(s)
    p = e / jnp.sum(e, axis=-1, keepdims=True)
    return _dot(p, v)


def _attn_in_specs(q_blk, order):
    hd = X_HEAD_DIM
    return [
        pl.BlockSpec((ATT_TQ, hd), lambda a, b: (order(a, b)[0], q_blk + order(a, b)[1])),
        pl.BlockSpec((N_MEM, hd), lambda a, b: (0, order(a, b)[1])),
        pl.BlockSpec((N_MEM, hd), lambda a, b: (0, X_HEADS + order(a, b)[1])),
    ]


def _attn_fwd(proj, q_off, kv, *, name):
    order = lambda i, h: (i, h)
    cat_blk = D_INNER // X_HEAD_DIM

    def body(q_ref, k_ref, v_ref, o_ref):
        o_ref[...] = _attn_tile(q_ref[...], k_ref[...], v_ref[...]).astype(o_ref.dtype)

    return pl.pallas_call(
        body, grid=(SEQ // ATT_TQ, X_HEADS), in_specs=_attn_in_specs(q_off // X_HEAD_DIM, order),
        out_specs=pl.BlockSpec((ATT_TQ, X_HEAD_DIM), lambda i, h: (i, cat_blk + h)),
        out_shape=SDS((SEQ, MIX_OUT), BF16), compiler_params=_cparams(("parallel", "parallel")), name=name,
    )(proj, kv, kv)


def _attn_bwd(proj, q_off, kv, dcat, dproj_width, dq_off, *, name):
    order = lambda h, i: (i, h)
    cat_blk = D_INNER // X_HEAD_DIM
    dq_blk = dq_off // X_HEAD_DIM

    def body(q_ref, k_ref, v_ref, do_ref, dq_ref, dk_ref, dv_ref):
        _, vjp = jax.vjp(_attn_tile, q_ref[...], k_ref[...], v_ref[...])
        dq, dk, dv = vjp(do_ref[...])
        dq_ref[...] = dq.astype(dq_ref.dtype)

        @pl.when(pl.program_id(1) == 0)
        def _():
            dk_ref[...] = jnp.zeros_like(dk_ref)
            dv_ref[...] = jnp.zeros_like(dv_ref)

        dk_ref[...] += dk
        dv_ref[...] += dv

    kv_spec = pl.BlockSpec((N_MEM, X_HEAD_DIM), lambda h, i: (0, h))
    return pl.pallas_call(
        body, grid=(X_HEADS, SEQ // ATT_TQ),
        in_specs=[*_attn_in_specs(q_off // X_HEAD_DIM, order),
                  pl.BlockSpec((ATT_TQ, X_HEAD_DIM), lambda h, i: (i, cat_blk + h))],
        out_specs=[pl.BlockSpec((ATT_TQ, X_HEAD_DIM), lambda h, i: (i, dq_blk + h)), kv_spec, kv_spec],
        out_shape=[SDS((SEQ, dproj_width), BF16), SDS((N_MEM, X_WIDTH), F32), SDS((N_MEM, X_WIDTH), F32)],
        compiler_params=_cparams(("parallel", "arbitrary")), name=name,
    )(proj, kv, kv, dcat)


CONV_TC = 512


def _shift_down(x, s):
    if s == 0:
        return x
    row = lax.broadcasted_iota(jnp.int32, x.shape, 0)
    return jnp.where(row >= s, pltpu.roll(x, s, 0), 0.0)


def _shift_up(x, s):
    if s == 0:
        return x
    n = x.shape[0]
    row = lax.broadcasted_iota(jnp.int32, x.shape, 0)
    return jnp.where(row < n - s, pltpu.roll(x, n - s, 0), 0.0)


def _conv_pre(x, w_ref, b_ref):
    pre = b_ref[...] + jnp.zeros_like(x)
    for k in range(CONV_K):
        pre = pre + w_ref[k:k + 1, :] * _shift_down(x, CONV_K - 1 - k)
    return pre


def _conv_fwd(proj, w, b, *, name):
    blk0 = D_INNER // CONV_TC

    def body(x_ref, w_ref, b_ref, o_ref):
        pre = _conv_pre(x_ref[...], w_ref, b_ref)
        o_ref[...] = pre * jax.nn.sigmoid(pre)

    return pl.pallas_call(
        body, grid=(CONV_DIM // CONV_TC,),
        in_specs=[pl.BlockSpec((SEQ, CONV_TC), lambda j: (0, blk0 + j)), pl.BlockSpec((CONV_K, CONV_TC), lambda j: (0, j)),
                  pl.BlockSpec((1, CONV_TC), lambda j: (0, j))],
        out_specs=pl.BlockSpec((SEQ, CONV_TC), lambda j: (0, j)), out_shape=SDS((SEQ, CONV_DIM), F32),
        compiler_params=_cparams(("parallel",)), name=name)(proj, w, b)


def _conv_bwd(proj, w, b, dxs, dbm, dcm, dproj, *, name):
    tc = CONV_TC // 2
    blk0 = D_INNER // tc
    n_x = D_INNER // tc
    n_b = SSM_GROUPS * SSM_STATE // tc

    def body(x_ref, w_ref, b_ref, dxs_ref, dbm_ref, dcm_ref, dproj_in, dproj_ref, dw_ref, db_ref):
        del dproj_in
        j = pl.program_id(0)
        x = x_ref[...]
        pre = _conv_pre(x, w_ref, b_ref)
        sg = jax.nn.sigmoid(pre)
        dact = jnp.where(j < n_x, dxs_ref[...], jnp.where(j < n_x + n_b, dbm_ref[...], dcm_ref[...]))
        dpre = dact * (sg * (1.0 + pre * (1.0 - sg)))
        dx = jnp.zeros_like(x)
        for k in range(CONV_K):
            s = CONV_K - 1 - k
            dx = dx + w_ref[k:k + 1, :] * _shift_up(dpre, s)
            dw_ref[k:k + 1, :] = jnp.sum(dpre * _shift_down(x, s), axis=0, keepdims=True)
        dproj_ref[...] = dx.astype(dproj_ref.dtype)
        db_ref[...] = jnp.sum(dpre, axis=0, keepdims=True)

    clip = lambda v, hi: jnp.minimum(jnp.maximum(v, 0), hi)
    return pl.pallas_call(
        body, grid=(CONV_DIM // tc,),
        in_specs=[pl.BlockSpec((SEQ, tc), lambda j: (0, blk0 + j)), pl.BlockSpec((CONV_K, tc), lambda j: (0, j)),
                  pl.BlockSpec((1, tc), lambda j: (0, j)),
                  pl.BlockSpec((SEQ, tc), lambda j: (0, clip(j, n_x - 1))),
                  pl.BlockSpec((SEQ, tc), lambda j: (0, clip(j - n_x, n_b - 1))),
                  pl.BlockSpec((SEQ, tc), lambda j: (0, clip(j - n_x - n_b, n_b - 1))),
                  pl.BlockSpec(memory_space=pl.ANY)],
        out_specs=[pl.BlockSpec((SEQ, tc), lambda j: (0, blk0 + j)), pl.BlockSpec((CONV_K, tc), lambda j: (0, j)),
                   pl.BlockSpec((1, tc), lambda j: (0, j))],
        out_shape=[SDS(dproj.shape, dproj.dtype), SDS((CONV_K, CONV_DIM), F32), SDS((1, CONV_DIM), F32)],
        input_output_aliases={6: 0}, compiler_params=_cparams(("parallel",)), name=name,
    )(proj, w, b, dxs, dbm, dcm, dproj)


SSM_PAIRS = SSM_HPG // 2


def _dot_exact01(x, m01, m01_t, x_first, differentiable):
    def product(v, m):
        hi = v.astype(BF16)
        rest = v - hi.astype(F32)
        mid = rest.astype(BF16)
        lo = (rest - mid.astype(F32)).astype(BF16)
        dims = (((1,), (0,)), ((), ()))
        dot = lambda part: lax.dot_general(*((part, m) if x_first else (m, part)), dims, preferred_element_type=F32)
        return dot(hi) + dot(mid) + dot(lo)

    if not differentiable:
        return product(x, m01)

    @jax.custom_vjp
    def exact(v):
        return product(v, m01)

    exact.defvjp(lambda v: (product(v, m01), None), lambda _, ct: (product(ct, m01_t),))
    return exact(x)


def _ssd_tile(xp, zp, bm, cm, hp, dt_c, dt_r, bias, bias_col, alog, alog_col, dsk, gnp, differentiable=False):
    row = lax.broadcasted_iota(jnp.int32, (CHUNK, CHUNK), 0)
    col = lax.broadcasted_iota(jnp.int32, (CHUNK, CHUNK), 1)
    causal = row >= col
    left = col < SSM_HEAD_DIM
    top = row < SSM_HEAD_DIM
    ones = jnp.ones((CHUNK, CHUNK), BF16)
    cb = _dot_nt(cm, bm)
    dtp = jax.nn.softplus(dt_c + bias)
    da_c = dtp * -jnp.exp(alog)
    da_r = jax.nn.softplus(dt_r + bias_col) * -jnp.exp(alog_col)
    lower = jnp.where(causal, 1.0, 0.0).astype(BF16)
    upper = jnp.where(row <= col, 1.0, 0.0).astype(BF16)
    cs = _dot_exact01(da_c, lower, upper, False, differentiable)
    cs_rows = _dot_exact01(da_r, upper, lower, True, differentiable)
    cs_last = jnp.sum(da_c, axis=0, keepdims=True)
    ecs, decay, ecl = jnp.exp(cs), jnp.exp(cs_last - cs), jnp.exp(cs_last)
    m = [cb * jnp.exp(jnp.where(causal, cs[:, r:r + 1] - cs_rows[r:r + 1, :], -1e30)) for r in range(SSM_HPG)]
    ygs, hn = [], []
    for p in range(SSM_PAIRS):
        a, b = 2 * p, 2 * p + 1
        pair = lambda v: jnp.where(left, v[:, a:a + 1], v[:, b:b + 1])
        xdt = xp[p] * pair(dtp)
        y = jnp.where(left, _dot(m[a], xdt), _dot(m[b], xdt))
        y = y + _dot_nt(cm, hp[p]) * pair(ecs)
        y = y + xp[p] * pair(dsk)
        states = _dot_tn(xdt * pair(decay), bm)
        hn.append(hp[p] * jnp.where(top, ecl[:, a:a + 1], ecl[:, b:b + 1]) + states)
        ygs.append(y * (zp[p] * jax.nn.sigmoid(zp[p])))
    ms = sum(_dot(t * t, ones) for t in ygs) * (1.0 / SSM_GROUP_W)
    rs = lax.rsqrt(ms + EPS)
    return [ygs[p] * rs * gnp[p] for p in range(SSM_PAIRS)], hn


def _ssd_in_specs(cidx):
    gw, n = SSM_GROUP_W, SSM_STATE
    bm_blk = D_INNER // n
    return [
        pl.BlockSpec((CHUNK, gw), lambda g, c: (cidx(c), g)),
        pl.BlockSpec((CHUNK, gw), lambda g, c: (cidx(c), g)),
        pl.BlockSpec((CHUNK, n), lambda g, c: (cidx(c), bm_blk + g)),
        pl.BlockSpec((CHUNK, n), lambda g, c: (cidx(c), bm_blk + SSM_GROUPS + g)),
        pl.BlockSpec((None, CHUNK, SSM_HPG), lambda g, c: (g, cidx(c), 0)),
        pl.BlockSpec((None, SSM_HPG, CHUNK), lambda g, c: (g, 0, cidx(c))),
        pl.BlockSpec((None, 3, SSM_HPG), lambda g, c: (g, 0, 0)),
        pl.BlockSpec((None, SSM_HPG, 2), lambda g, c: (g, 0, 0)),
        pl.BlockSpec((1, gw), lambda g, c: (0, g)),
    ]


def _ssd_args(x_ref, z_ref, bm_ref, cm_ref, hp, dtc_ref, dtr_ref, prow_ref, pcol_ref, gn_ref):
    npair, w = SSM_PAIRS, 2 * SSM_HEAD_DIM
    return (_split(x_ref, npair, w), _split(z_ref, npair, w), bm_ref[...], cm_ref[...], hp, dtc_ref[...], dtr_ref[...],
            prow_ref[0:1, :], pcol_ref[:, 0:1], prow_ref[1:2, :], pcol_ref[:, 1:2], prow_ref[2:3, :],
            _split(gn_ref, npair, w))


def _pair_rows(ref):
    w = 2 * SSM_HEAD_DIM
    return [ref[p * w:(p + 1) * w, :] for p in range(SSM_PAIRS)]


def _ssd_fwd(xbc, proj, dt_c, dt_r, par_row, par_col, gn, mixcat, *, name):
    w = 2 * SSM_HEAD_DIM

    def body(x_ref, z_ref, bm_ref, cm_ref, dtc_ref, dtr_ref, prow_ref, pcol_ref, gn_ref, cat_in,
             cat_ref, hprev_ref, h_scr):
        del cat_in

        @pl.when(pl.program_id(1) == 0)
        def _():
            h_scr[...] = jnp.zeros_like(h_scr)

        hprev_ref[...] = h_scr[...]
        yn, hn = _ssd_tile(*_ssd_args(x_ref, z_ref, bm_ref, cm_ref, _pair_rows(h_scr), dtc_ref, dtr_ref, prow_ref,
                                      pcol_ref, gn_ref))
        for p in range(SSM_PAIRS):
            cat_ref[:, p * w:(p + 1) * w] = yn[p].astype(cat_ref.dtype)
            h_scr[p * w:(p + 1) * w, :] = hn[p]

    return pl.pallas_call(
        body, grid=(SSM_GROUPS, N_CHUNKS), in_specs=[*_ssd_in_specs(lambda c: c), pl.BlockSpec(memory_space=pl.ANY)],
        out_specs=[pl.BlockSpec((CHUNK, SSM_GROUP_W), lambda g, c: (c, g)),
                   pl.BlockSpec((None, None, SSM_GROUP_W, SSM_STATE), lambda g, c: (c, g, 0, 0))],
        out_shape=[SDS(mixcat.shape, mixcat.dtype), SDS((N_CHUNKS, SSM_GROUPS, SSM_GROUP_W, SSM_STATE), F32)],
        scratch_shapes=[pltpu.VMEM((SSM_GROUP_W, SSM_STATE), F32)],
        input_output_aliases={9: 0}, compiler_params=_cparams(("parallel", "arbitrary")), name=name,
    )(xbc, proj, xbc, xbc, dt_c, dt_r, par_row, par_col, gn, mixcat)


def _ssd_bwd(xbc, proj, dt_c, dt_r, par_row, par_col, gn, hprev, dcat, dproj, *, name):
    nh, w, gw, n = SSM_HPG, 2 * SSM_HEAD_DIM, SSM_GROUP_W, SSM_STATE
    rev = lambda c: N_CHUNKS - 1 - c

    def body(x_ref, z_ref, bm_ref, cm_ref, dtc_ref, dtr_ref, prow_ref, pcol_ref, gn_ref, hprev_ref, dy_ref,
             dproj_in, dz_ref, dxs_ref, dbm_ref, dcm_ref, ddtc_ref, ddtr_ref, dprow_ref, dpcol_ref, dgn_ref, dh_scr):
        del dproj_in
        first = pl.program_id(1) == 0

        @pl.when(first)
        def _():
            dh_scr[...] = jnp.zeros_like(dh_scr)
            for ref in (dprow_ref, dpcol_ref, dgn_ref):
                ref[...] = jnp.zeros_like(ref)

        args = _ssd_args(x_ref, z_ref, bm_ref, cm_ref, _pair_rows(hprev_ref), dtc_ref, dtr_ref, prow_ref, pcol_ref,
                         gn_ref)
        _, vjp = jax.vjp(lambda *a: _ssd_tile(*a, differentiable=True), *args)
        dxs, dzs, dbm, dcm, dhs, ddtc, ddtr, dbias, dbias_col, dalog, dalog_col, ddsk, dgn = vjp(
            (_split(dy_ref, SSM_PAIRS, w), _pair_rows(dh_scr)))
        dbm_ref[...] = dbm
        dcm_ref[...] = dcm
        ddtc_ref[...] = ddtc
        ddtr_ref[...] = ddtr
        for q in range(SSM_PAIRS):
            dxs_ref[:, q * w:(q + 1) * w] = dxs[q]
            dz_ref[:, q * w:(q + 1) * w] = dzs[q].astype(dz_ref.dtype)
            dh_scr[q * w:(q + 1) * w, :] = dhs[q]
            dgn_ref[:, q * w:(q + 1) * w] += dgn[q]
        for i, d in enumerate((dbias, dalog, ddsk)):
            dprow_ref[i:i + 1, :] += d
        for i, d in enumerate((dbias_col, dalog_col)):
            dpcol_ref[:, i:i + 1] += d

    return pl.pallas_call(
        body, grid=(SSM_GROUPS, N_CHUNKS),
        in_specs=[*_ssd_in_specs(rev),
                  pl.BlockSpec((None, None, gw, n), lambda g, c: (rev(c), g, 0, 0)),
                  pl.BlockSpec((CHUNK, gw), lambda g, c: (rev(c), g)),
                  pl.BlockSpec(memory_space=pl.ANY)],
        out_specs=[pl.BlockSpec((CHUNK, gw), lambda g, c: (rev(c), g)),
                   pl.BlockSpec((CHUNK, gw), lambda g, c: (rev(c), g)),
                   pl.BlockSpec((CHUNK, n), lambda g, c: (rev(c), g)),
                   pl.BlockSpec((CHUNK, n), lambda g, c: (rev(c), g)),
                   pl.BlockSpec((None, CHUNK, nh), lambda g, c: (g, rev(c), 0)),
                   pl.BlockSpec((None, nh, CHUNK), lambda g, c: (g, 0, rev(c))),
                   pl.BlockSpec((None, 3, nh), lambda g, c: (g, 0, 0)),
                   pl.BlockSpec((None, nh, 2), lambda g, c: (g, 0, 0)),
                   pl.BlockSpec((1, gw), lambda g, c: (0, g))],
        out_shape=[SDS(dproj.shape, dproj.dtype), SDS((SEQ, D_INNER), F32), SDS((SEQ, SSM_GROUPS * n), F32),
                   SDS((SEQ, SSM_GROUPS * n), F32), SDS((SSM_GROUPS, SEQ, nh), F32), SDS((SSM_GROUPS, nh, SEQ), F32),
                   SDS((SSM_GROUPS, 3, nh), F32), SDS((SSM_GROUPS, nh, 2), F32), SDS((1, D_INNER), F32)],
        scratch_shapes=[pltpu.VMEM((gw, n), F32)],
        input_output_aliases={11: 0}, compiler_params=_cparams(("parallel", "arbitrary")), name=name,
    )(xbc, proj, xbc, xbc, dt_c, dt_r, par_row, par_col, gn, hprev, dcat, dproj)


def _sum_contributions(chip, parts, landed, *, name):
    _, r, c = parts.shape
    tr = _pick(r, (256, 384, 128))

    def body(chip_ref, own_ref, landed_ref, o_ref):
        del chip_ref
        acc = own_ref[...].astype(F32)
        for s in range(landed_ref.shape[0]):
            acc = acc + landed_ref[s].astype(F32)
        o_ref[...] = acc

    grid_spec = pltpu.PrefetchScalarGridSpec(
        num_scalar_prefetch=1, grid=(r // tr,),
        in_specs=[pl.BlockSpec((None, tr, c), lambda i, chip_ref: (chip_ref[0], i, 0)),
                  pl.BlockSpec((landed.shape[0], tr, c), lambda i, chip_ref: (0, i, 0))],
        out_specs=pl.BlockSpec((tr, c), lambda i, chip_ref: (i, 0)))
    return pl.pallas_call(body, grid_spec=grid_spec, out_shape=SDS((r, c), F32),
                          compiler_params=_cparams(("parallel",)), name=name)(chip, parts, landed)


def _adamw(w, g, m, v, *, name):
    layers, r, c = w.shape
    if r <= 256 or r % 128 == 0:
        tr = min(r, 256)
        steps, spec = r // tr, pl.BlockSpec((None, tr, c), lambda l, i: (l, i, 0))
    else:
        tc = _pick(c, (256, 128))
        steps, spec = c // tc, pl.BlockSpec((None, r, tc), lambda l, i: (l, 0, i))

    def body(w_ref, g_ref, m_ref, v_ref, d_ref, mo_ref, vo_ref):
        g = g_ref[...]
        m_new = ADAM_B1 * m_ref[...] + (1.0 - ADAM_B1) * g
        v_new = ADAM_B2 * v_ref[...] + (1.0 - ADAM_B2) * (g * g)
        m_hat = m_new / (1.0 - ADAM_B1 ** ADAM_STEP)
        v_hat = v_new / (1.0 - ADAM_B2 ** ADAM_STEP)
        d_ref[...] = -ADAM_LR * (m_hat / (jnp.sqrt(v_hat) + ADAM_EPS) + ADAM_WD * w_ref[...])
        mo_ref[...] = m_new
        vo_ref[...] = v_new

    return pl.pallas_call(body, grid=(layers, steps), in_specs=[spec] * 4, out_specs=[spec] * 3,
                          out_shape=[SDS(w.shape, F32)] * 3, compiler_params=_cparams(("parallel", "parallel")),
                          name=name)(w, g, m, v)


ANY = pl.BlockSpec(memory_space=pl.ANY)


def _place():
    x, y, c = lax.axis_index("x"), lax.axis_index("y"), lax.axis_index("c")
    chips = [(1 - x, y), (x, 1 - y), (1 - x, 1 - y)]
    return x, y, c, chips


def _remote(src, dst, send_sem, recv_sem, to):
    return pltpu.make_async_remote_copy(src_ref=src, dst_ref=dst, send_sem=send_sem, recv_sem=recv_sem,
                                        device_id=to, device_id_type=MESH)


STREAM_ROWS = 256


def _stream_rows(i):
    return pl.ds(pl.multiple_of(i * STREAM_ROWS, STREAM_ROWS), STREAM_ROWS)


def _channel_scratch(width, dtype, rows=STREAM_ROWS):
    buf = (2, rows, width)
    return [pltpu.VMEM(buf, dtype), pltpu.VMEM(buf, dtype), *([pltpu.SemaphoreType.DMA((2,))] * 5),
            pltpu.SemaphoreType.REGULAR((2,))]


CHANNEL_REFS = 8


def _copy_blocks(srcs, dsts, ch):
    sbuf, _, ld, _, _, st, _, _ = ch
    n = len(srcs)
    load = lambda i: pltpu.make_async_copy(srcs[i], sbuf.at[i % 2], ld.at[i % 2])
    store = lambda i: pltpu.make_async_copy(sbuf.at[i % 2], dsts[i], st.at[i % 2])
    load(0).start()
    for i in range(n):
        if i + 1 < n:
            if i >= 1:
                store(i - 1).wait()
            load(i + 1).start()
        load(i).wait()
        store(i).start()
    for i in range(max(0, n - 2), n):
        store(i).wait()


def _exchange_block_streams(streams, sibling):
    plans = []
    for srcs, dsts, keeps, (sbuf, rbuf, ld, snd, rcv, st, kp, credit) in streams:
        n = len(srcs)

        def load(i, srcs=srcs, sbuf=sbuf, ld=ld):
            return pltpu.make_async_copy(srcs[i], sbuf.at[i % 2], ld.at[i % 2])

        def push(i, sbuf=sbuf, rbuf=rbuf, snd=snd, rcv=rcv):
            return _remote(sbuf.at[i % 2], rbuf.at[i % 2], snd.at[i % 2], rcv.at[i % 2], sibling)

        def store(i, rbuf=rbuf, dsts=dsts, st=st):
            return pltpu.make_async_copy(rbuf.at[i % 2], dsts[i], st.at[i % 2])

        def save(i, sbuf=sbuf, keeps=keeps, kp=kp):
            return pltpu.make_async_copy(sbuf.at[i % 2], keeps[i], kp.at[i % 2])

        def free_slot(i, n=n, store=store, credit=credit):
            if 1 <= i < n:
                store(i - 1).wait()
                if i + 1 < n:
                    pl.semaphore_signal(credit.at[(i + 1) % 2], 1, device_id=sibling, device_id_type=MESH)

        def send(i, n=n, load=load, push=push, save=save, keeps=keeps, credit=credit):
            if i < n:
                load(i).wait()
                pl.semaphore_wait(credit.at[i % 2], 1)
                push(i).start()
                if keeps[i] is not None:
                    save(i).start()

        def receive(i, n=n, load=load, push=push, store=store, save=save, keeps=keeps):
            if i < n:
                push(i).wait_recv()
                store(i).start()
                push(i).wait_send()
                if keeps[i] is not None:
                    save(i).wait()
                if i + 2 < n:
                    load(i + 2).start()

        for i in range(min(2, n)):
            pl.semaphore_signal(credit.at[i], 1, device_id=sibling, device_id_type=MESH)
            load(i).start()
        plans.append((n, free_slot, send, receive, store))
    for _, _, send, _, _ in plans:
        send(0)
    for i in range(max(p[0] for p in plans)):
        for _, free_slot, _, _, _ in plans:
            free_slot(i)
        for _, _, send, _, _ in plans:
            send(i + 1)
        for _, _, _, receive, _ in plans:
            receive(i)
    for n, _, _, _, store in plans:
        store(n - 1).wait()


def _all_gather_shards(shards, small, *, name):
    n = len(shards)

    def body(*refs):
        ins, outs = refs[:n + 1], refs[n + 1:2 * n + 2]
        scr = refs[2 * n + 2:]
        chans = [scr[CHANNEL_REFS * t:CHANNEL_REFS * (t + 1)] for t in range(n)]
        send_sems, recv_sems, small_sems = scr[CHANNEL_REFS * n:]
        x, y, c, _ = _place()
        me = 2 * x + y
        sibling = (x, y, 1 - c)
        near = (lax.rem(x + 1 - c, 2), lax.rem(y + c, 2))
        far = (lax.rem(x + c, 2), lax.rem(y + 1 - c, 2))
        k_near, k_far, k_diag = 2 * near[0] + near[1], 2 * far[0] + far[1], 3 - me
        targets = ((*near, c), (*far, c), (*far, c))
        arrives = (k_near, k_far, k_diag)
        streams_in = (k_far, k_near, k_diag)

        def ici(t, j, src, blk):
            return _remote(src, outs[t].at[blk, c], send_sems.at[3 * t + j], recv_sems.at[3 * t + j], targets[j])

        first = [ici(t, j, ins[t].at[c], me) for t in range(n + 1) for j in range(2)]
        for cp in first:
            cp.start()
        small_local = pltpu.make_async_copy(ins[n], outs[n].at[me], small_sems.at[6])
        small_local.start()
        for t in range(n):
            _copy_blocks([ins[t].at[h] for h in range(2)], [outs[t].at[me, h] for h in range(2)], chans[t])
        passed = []
        for j in range(3):
            for t in range(n + 1):
                landed = outs[t].at[arrives[j], c]
                ici(t, j, landed, arrives[j]).wait_recv()
                if j == 0:
                    fwd = ici(t, 2, landed, k_near)
                    fwd.start()
                    passed.append(fwd)
                if t < n:
                    _exchange_block_streams([([landed], [outs[t].at[streams_in[j], 1 - c]], [None], chans[t])], sibling)
                else:
                    fwd = _remote(landed, landed, small_sems.at[j], small_sems.at[3 + j], sibling)
                    fwd.start()
                    passed.append(fwd)
        for j in range(3):
            got = outs[n].at[streams_in[j], 1 - c]
            _remote(got, got, small_sems.at[j], small_sems.at[3 + j], sibling).wait_recv()
        for cp in first + passed:
            cp.wait_send()
        small_local.wait()

    scratch = []
    for s in shards:
        scratch += _channel_scratch(s.shape[2], s.dtype, rows=s.shape[1])
    return pl.pallas_call(
        body, in_specs=[ANY] * (n + 1), out_specs=[ANY] * (n + 1),
        out_shape=[SDS((N_CHIPS, *s.shape), s.dtype) for s in (*shards, small)],
        scratch_shapes=[*scratch, pltpu.SemaphoreType.DMA((3 * n + 3,)), pltpu.SemaphoreType.DMA((3 * n + 3,)),
                        pltpu.SemaphoreType.DMA((7,))],
        compiler_params=pltpu.CompilerParams(vmem_limit_bytes=VMEM_LIMIT), name=name)(*shards, small)


def _pair_reduce(stacks, *, name):
    n = len(stacks)
    per = 11

    def body(*refs):
        ins, outs, scr = refs[:n], refs[n:2 * n], refs[2 * n:]
        x, y, c, _ = _place()
        sibling = (x, y, 1 - c)
        streams = []
        for t in range(n):
            sraw, sbuf, rbuf, obuf, pbuf, ld_s, ld_o, snd, rcv, st, credit = scr[per * t:per * (t + 1)]
            steps = ins[t].shape[1] // STREAM_ROWS
            src, own, out = ins[t].at[1 - c], ins[t].at[c], outs[t]
            assert steps >= 2

            def load_s(i, slot, src=src, sraw=sraw, ld_s=ld_s):
                return pltpu.make_async_copy(src.at[_stream_rows(i)], sraw.at[slot], ld_s.at[slot])

            def load_o(i, slot, own=own, obuf=obuf, ld_o=ld_o):
                return pltpu.make_async_copy(own.at[_stream_rows(i)], obuf.at[slot], ld_o.at[slot])

            def push(slot, sbuf=sbuf, rbuf=rbuf, snd=snd, rcv=rcv):
                return _remote(sbuf.at[slot], rbuf.at[slot], snd.at[slot], rcv.at[slot], sibling)

            def store(i, slot, pbuf=pbuf, out=out, st=st):
                return pltpu.make_async_copy(pbuf.at[slot], out.at[_stream_rows(i)], st.at[slot])

            def send(i, slot, load_s=load_s, push=push, sraw=sraw, sbuf=sbuf, credit=credit):
                load_s(i, slot).wait()
                sbuf[slot] = sraw[slot].astype(sbuf.dtype)
                pl.semaphore_wait(credit.at[slot], 1)
                push(slot).start()

            def combine(i, slot, load_s=load_s, load_o=load_o, push=push, store=store, rbuf=rbuf, obuf=obuf, pbuf=pbuf,
                        credit=credit, steps=steps):
                load_o(i, slot).wait()
                push(slot).wait_recv()

                @pl.when(i >= 2)
                def _():
                    store(i, slot).wait()

                pbuf[slot] = (obuf[slot] + rbuf[slot].astype(F32)).astype(pbuf.dtype)
                store(i, slot).start()
                push(slot).wait_send()

                @pl.when(i + 2 < steps)
                def _():
                    load_s(i + 2, slot).start()
                    load_o(i + 2, slot).start()
                    pl.semaphore_signal(credit.at[slot], 1, device_id=sibling, device_id_type=MESH)

            for slot in range(2):
                pl.semaphore_signal(credit.at[slot], 1, device_id=sibling, device_id_type=MESH)
                load_s(slot, slot).start()
                load_o(slot, slot).start()
            streams.append((steps, send, combine, store))
        for _, send, _, _ in streams:
            send(0, 0)

        def step(i, carry):
            slot = lax.rem(i, 2)
            for steps, send, _, _ in streams:
                @pl.when(i + 1 < steps)
                def _(send=send):
                    send(i + 1, 1 - slot)
            for steps, _, combine, _ in streams:
                @pl.when(i < steps)
                def _(combine=combine):
                    combine(i, slot)
            return carry

        lax.fori_loop(0, max(s[0] for s in streams), step, 0)
        for _, _, _, store in streams:
            for slot in range(2):
                store(0, slot).wait()

    scratch = []
    for s in stacks:
        buf = (2, STREAM_ROWS, s.shape[2])
        scratch += [pltpu.VMEM(buf, F32), pltpu.VMEM(buf, BF16), pltpu.VMEM(buf, BF16), pltpu.VMEM(buf, F32),
                    pltpu.VMEM(buf, BF16), *([pltpu.SemaphoreType.DMA((2,))] * 5), pltpu.SemaphoreType.REGULAR((2,))]
    return pl.pallas_call(
        body, in_specs=[ANY] * n, out_specs=[ANY] * n, out_shape=[SDS(s.shape[1:], BF16) for s in stacks],
        scratch_shapes=scratch, compiler_params=pltpu.CompilerParams(vmem_limit_bytes=VMEM_LIMIT), name=name)(*stacks)


HBM_SPEC = pl.BlockSpec(memory_space=pltpu.HBM)
SEM_SPEC = pl.BlockSpec(memory_space=pltpu.SEMAPHORE)
SIDE_EFFECT = pltpu.SideEffectType.DATAFLOW_SIDE_EFFECTING


def _scatter_copies(ins, lands, send_sems, recv_sems):
    _, _, c, chips = _place()
    return [_remote(ins[t].at[2 * cx + cy], lands[t].at[j], send_sems.at[3 * t + j], recv_sems.at[3 * t + j],
                    (cx, cy, c)) for t in range(len(ins)) for j, (cx, cy) in enumerate(chips)]


def _chip_scatter_start(parts, *, name):
    n = len(parts)

    def body(*refs):
        ins, lands = refs[:n], refs[n:2 * n]
        send_sems, recv_sems, token = refs[2 * n], refs[2 * n + 1], refs[-1]
        for cp in _scatter_copies(ins, lands, send_sems, recv_sems):
            cp.start()
        token[...] = jnp.zeros_like(token)

    hbm = lambda a: pltpu.with_memory_space_constraint(a, pltpu.HBM)
    lands = [hbm(lax.empty((3, *p.shape[1:]), p.dtype)) for p in parts]
    thru = [pltpu.HBM(a.shape, a.dtype) for a in (*parts, *lands)]
    outs = pl.pallas_call(
        body, name=name,
        out_shape=(pltpu.SemaphoreType.DMA((3 * n,)), pltpu.SemaphoreType.DMA((3 * n,)), *thru, SDS((8, 128), F32)),
        in_specs=[HBM_SPEC] * (2 * n),
        out_specs=(SEM_SPEC, SEM_SPEC, *([HBM_SPEC] * (2 * n)), pl.BlockSpec(memory_space=pltpu.VMEM)),
        input_output_aliases={i: 2 + i for i in range(2 * n)},
        compiler_params=pltpu.CompilerParams(has_side_effects=SIDE_EFFECT),
    )(*[hbm(p) for p in parts], *lands)
    return outs[0], outs[1], outs[2:2 + n], outs[2 + n:2 + 2 * n], outs[-1]


def _chip_scatter_wait(send_sems, recv_sems, parts, lands, after, *, name):
    n = len(parts)

    def body(*refs):
        ins, lands_in = refs[:n], refs[n:2 * n]
        for cp in _scatter_copies(ins, lands_in, refs[2 * n], refs[2 * n + 1]):
            cp.wait_send()
            cp.wait_recv()

    outs = pl.pallas_call(
        body, name=name, out_shape=[pltpu.HBM(a.shape, a.dtype) for a in (*parts, *lands)],
        in_specs=[*([HBM_SPEC] * (2 * n)), SEM_SPEC, SEM_SPEC, *([ANY] * len(after))],
        out_specs=[HBM_SPEC] * (2 * n), input_output_aliases={i: i for i in range(2 * n)},
        compiler_params=pltpu.CompilerParams(has_side_effects=SIDE_EFFECT),
    )(*parts, *lands, send_sems, recv_sems, *after)
    return outs[:n], outs[n:]


def _gather_copies(shards, zones, send_sems, recv_sems):
    x, y, c, chips = _place()
    return [_remote(shards[t].at[c], zones[t].at[2 * x + y, c], send_sems.at[3 * t + j], recv_sems.at[3 * t + j],
                    (cx, cy, c)) for t in range(len(shards)) for j, (cx, cy) in enumerate(chips)]


def _gather_start(shards, after, *, name):
    n = len(shards)

    def body(*refs):
        ins, zones = refs[:n], refs[n:2 * n]
        send_sems, recv_sems, token = refs[2 * n + len(after)], refs[2 * n + len(after) + 1], refs[-1]
        for cp in _gather_copies(ins, zones, send_sems, recv_sems):
            cp.start()
        token[...] = jnp.zeros_like(token)

    hbm = lambda a: pltpu.with_memory_space_constraint(a, pltpu.HBM)
    zones = [hbm(lax.empty((N_CHIPS, *s.shape), s.dtype)) for s in shards]
    thru = [pltpu.HBM(a.shape, a.dtype) for a in (*shards, *zones)]
    outs = pl.pallas_call(
        body, name=name,
        out_shape=(pltpu.SemaphoreType.DMA((3 * n,)), pltpu.SemaphoreType.DMA((3 * n,)), *thru, SDS((8, 128), F32)),
        in_specs=[*([HBM_SPEC] * (2 * n)), *([ANY] * len(after))],
        out_specs=(SEM_SPEC, SEM_SPEC, *([HBM_SPEC] * (2 * n)), pl.BlockSpec(memory_space=pltpu.VMEM)),
        input_output_aliases={i: 2 + i for i in range(2 * n)},
        compiler_params=pltpu.CompilerParams(has_side_effects=SIDE_EFFECT),
    )(*[hbm(s) for s in shards], *zones, *after)
    return outs[0], outs[1], outs[2:2 + n], outs[2 + n:2 + 2 * n], outs[-1]


def _gather_wait(send_sems, recv_sems, shards, zones, after, *, name):
    n = len(shards)

    def body(*refs):
        for cp in _gather_copies(refs[:n], refs[n:2 * n], refs[2 * n], refs[2 * n + 1]):
            cp.wait_send()
            cp.wait_recv()

    outs = pl.pallas_call(
        body, name=name, out_shape=[pltpu.HBM(a.shape, a.dtype) for a in (*shards, *zones)],
        in_specs=[*([HBM_SPEC] * (2 * n)), SEM_SPEC, SEM_SPEC, *([ANY] * len(after))],
        out_specs=[HBM_SPEC] * (2 * n), input_output_aliases={i: i for i in range(2 * n)},
        compiler_params=pltpu.CompilerParams(has_side_effects=SIDE_EFFECT),
    )(*shards, *zones, send_sems, recv_sems, *after)
    return outs[:n], outs[n:]


def _gather_finish(shards, zones, *, name):
    n = len(shards)

    def body(*refs):
        ins, zones_in, outs, scr = refs[:n], refs[n:2 * n], refs[2 * n:3 * n], refs[3 * n:]
        x, y, c, chips = _place()
        me = 2 * x + y
        sibling = (x, y, 1 - c)
        others = [2 * cx + cy for cx, cy in chips]
        chans = [scr[CHANNEL_REFS * t:CHANNEL_REFS * (t + 1)] for t in range(n)]
        for t in range(n):
            _copy_blocks([ins[t].at[h] for h in range(2)], [outs[t].at[me, h] for h in range(2)], chans[t])
        _exchange_block_streams([([zones_in[t].at[k, c] for k in others], [outs[t].at[k, 1 - c] for k in others],
                                  [None] * len(others), chans[t]) for t in range(n)], sibling)

    scratch = []
    for s in shards:
        scratch += _channel_scratch(s.shape[2], s.dtype, rows=s.shape[1])
    return pl.pallas_call(
        body, in_specs=[ANY] * (2 * n), out_specs=[ANY] * n, out_shape=[SDS(z.shape, z.dtype) for z in zones],
        input_output_aliases={n + t: t for t in range(n)}, scratch_shapes=scratch,
        compiler_params=pltpu.CompilerParams(vmem_limit_bytes=VMEM_LIMIT), name=name)(*shards, *zones)


def _pair_share(groups, *, name):
    finals = [f for grp in groups for f in grp]
    n, n_out = len(finals), len(groups)

    def body(*refs):
        ins, outs, scr = refs[:n], refs[n:n + n_out], refs[n + n_out:]
        x, y, c, _ = _place()
        sibling = (x, y, 1 - c)
        t, streams = 0, []
        for o, grp in enumerate(groups):
            rows = grp[0].shape[0] // 2
            blocks = [(layer, pl.ds(b * rows, rows)) for layer in range(len(grp)) for b in range(2)]
            streams.append(([ins[t + layer].at[rs] for layer, rs in blocks],
                            [outs[o].at[layer, 1 - c, rs] for layer, rs in blocks],
                            [outs[o].at[layer, c, rs] for layer, rs in blocks],
                            scr[CHANNEL_REFS * o:CHANNEL_REFS * (o + 1)]))
            t += len(grp)
        _exchange_block_streams(streams, sibling)

    scratch = []
    for grp in groups:
        scratch += _channel_scratch(grp[0].shape[1], grp[0].dtype, rows=grp[0].shape[0] // 2)
    return pl.pallas_call(
        body, in_specs=[ANY] * n, out_specs=[ANY] * n_out,
        out_shape=[SDS((len(grp), 2, *grp[0].shape), grp[0].dtype) for grp in groups],
        scratch_shapes=scratch, compiler_params=pltpu.CompilerParams(vmem_limit_bytes=VMEM_LIMIT), name=name)(*finals)


N_DEVICES = 8


def _all_copies(v, zones, send_sems, recv_sems):
    x, y, c, _ = _place()
    me = 4 * x + 2 * y + c
    return [_remote(v, zones.at[me], send_sems.at[k], recv_sems.at[k],
                    (lax.rem(x + ((k + 1) >> 2), 2), lax.rem(y + (((k + 1) >> 1) & 1), 2), lax.rem(c + ((k + 1) & 1), 2)))
            for k in range(N_DEVICES - 1)]


def _all_gather_start(v, *, name):
    def body(v_ref, zones, send_sems, recv_sems, v_thru, zones_thru, token):
        for cp in _all_copies(v_ref, zones, send_sems, recv_sems):
            cp.start()
        token[...] = jnp.zeros_like(token)

    hbm = lambda a: pltpu.with_memory_space_constraint(a, pltpu.HBM)
    zones = hbm(lax.empty((N_DEVICES, *v.shape), v.dtype))
    sems = pltpu.SemaphoreType.DMA((N_DEVICES - 1,))
    return pl.pallas_call(
        body, name=name,
        out_shape=(sems, sems, pltpu.HBM(v.shape, v.dtype), pltpu.HBM(zones.shape, zones.dtype), SDS((8, 128), F32)),
        in_specs=[HBM_SPEC] * 2,
        out_specs=(SEM_SPEC, SEM_SPEC, HBM_SPEC, HBM_SPEC, pl.BlockSpec(memory_space=pltpu.VMEM)),
        input_output_aliases={0: 2, 1: 3}, compiler_params=pltpu.CompilerParams(has_side_effects=SIDE_EFFECT),
    )(hbm(v), zones)


def _all_gather_wait(send_sems, recv_sems, v, zones, after, *, name):
    def body(v_ref, zones_ref, send_sems, recv_sems, *_):
        for cp in _all_copies(v_ref, zones_ref, send_sems, recv_sems):
            cp.wait_send()
            cp.wait_recv()

    return pl.pallas_call(
        body, name=name, out_shape=[pltpu.HBM(v.shape, v.dtype), pltpu.HBM(zones.shape, zones.dtype)],
        in_specs=[HBM_SPEC, HBM_SPEC, SEM_SPEC, SEM_SPEC, *([ANY] * len(after))], out_specs=[HBM_SPEC] * 2,
        input_output_aliases={0: 0, 1: 1}, compiler_params=pltpu.CompilerParams(has_side_effects=SIDE_EFFECT),
    )(v, zones, send_sems, recv_sems, *after)


def _sum_devices(device, v, zones, *, name):
    rows, lanes = v.shape

    def body(device_ref, v_ref, zones_ref, o_ref):
        acc = jnp.zeros((rows, lanes), F32)
        for k in range(N_DEVICES):
            acc = acc + jnp.where(device_ref[0] == k, v_ref[...], zones_ref[k])
        o_ref[...] = acc

    grid_spec = pltpu.PrefetchScalarGridSpec(
        num_scalar_prefetch=1, grid=(1,),
        in_specs=[pl.BlockSpec((rows, lanes), lambda i, d: (0, 0)),
                  pl.BlockSpec((N_DEVICES, rows, lanes), lambda i, d: (0, 0, 0))],
        out_specs=pl.BlockSpec((rows, lanes), lambda i, d: (0, 0)))
    return pl.pallas_call(body, grid_spec=grid_spec, out_shape=SDS((rows, lanes), F32),
                          compiler_params=_cparams(("arbitrary",)), name=name)(device, v, zones)


def _relu2_epilogue(acc):
    return acc, jnp.square(jnp.maximum(acc, 0.0))


def _res_epilogue(acc, res):
    return (acc + res,)


def _drelu2_epilogue(acc, pre):
    return (acc * (2.0 * jnp.maximum(pre.astype(F32), 0.0)),)


def _ffn_fwd(h, g, w1, w2, tag):
    f = _rms_fwd(h, g, name=f"ffn_norm_{tag}")
    pre, act = _mm_nn(f, w1, name=f"ffn1_{tag}", epilogue=_relu2_epilogue, n_out_dtypes=(BF16, BF16))
    h_out = _mm_nn(act, w2, name=f"ffn2_{tag}", extras=(h,), epilogue=_res_epilogue)
    return h_out, (f, pre, act)


def _ffn_bwd(dh, h, g, w1, w2, saved, layer, after=()):
    f, pre, act = saved
    dpre = _mm_nt(dh, w2, name=f"ffn2_dx_{layer}", out_dtype=BF16, extras=(pre,), epilogue=_drelu2_epilogue,
                  after=after)
    dw2 = _mm_tn_stacked(act, dh, name=f"ffn2_dw_{layer}", col_slots=False)
    df = _mm_nt(dpre, w1, name=f"ffn1_dx_{layer}")
    dw1 = _mm_tn_stacked(f, dpre, name=f"ffn1_dw_{layer}", col_slots=True)
    dh, dg = _rms_bwd(h, g, df, dh, name=f"ffn_norm_bwd_{layer}")
    return dh, dg, dw1, dw2


def _kv_fwd(mem, g, w_kv, tag):
    m = _rms_fwd(mem, g, name=f"mem_norm_{tag}")
    return m, _mm_nn(m, w_kv, name=f"kv_{tag}")


def _kv_bwd(mem, g, w_kv, m, dk, dv, layer):
    dkv = jnp.concatenate([dk, dv], axis=1)
    dw = _mm_tn_stacked(m, dkv, name=f"kv_dw_{layer}", col_slots=True)
    dm = _mm_nt(dkv, w_kv, name=f"kv_dx_{layer}")
    _, dg = _rms_bwd(mem, g, dm, dm, name=f"mem_norm_bwd_{layer}")
    return dw, dg


def _local_step(x, mem, target, p, after_layer1=None, after_ffn0=None, after_mixer0=None):
    row = lambda v: v.reshape(1, -1)
    g = {}

    h0 = x
    a0 = _rms_fwd(h0, row(p["norm_mix"][0]), name="mix_norm_0")
    proj_a = _mm_nn(a0, p["a_in"], name="a_in", after=p.get("after_start", ()))
    m0, kv0 = _kv_fwd(mem, row(p["mem_norm"][0]), p["w_kv"][0], "0")
    cat0 = _attn_fwd(proj_a, 2 * D_INNER, kv0, name="attn_0")
    bs_col = p["a_bs"].reshape(A_GROUPS, CHUNK, 1)
    cat0 = _gate_fwd(proj_a, p["a_ln_g"], p["a_ln_b"], p["a_ws"], bs_col, cat0, name="gate")
    h1 = _mm_nn(cat0, p["w_out"][0], name="out_0", extras=(h0,), epilogue=_res_epilogue)
    h2, ffn0 = _ffn_fwd(h1, row(p["norm_ffn"][0]), p["w_ffn1"][0], p["w_ffn2"][0], "0")

    if "layer1_mixer" in p:
        w_kv1, w_out1, b_in = p["layer1_mixer"](h2)
    else:
        w_kv1, w_out1, b_in = p["w_kv"][1], p["w_out"][1], p["b_in"]
    a1 = _rms_fwd(h2, row(p["norm_mix"][1]), name="mix_norm_1")
    proj_b = _mm_nn(a1, b_in, name="b_in")
    m1, kv1 = _kv_fwd(mem, row(p["mem_norm"][1]), w_kv1, "1")
    cat1 = _attn_fwd(proj_b, B_Q_OFF, kv1, name="attn_1")
    xbc = _conv_fwd(proj_b, p["b_conv_w"], p["b_conv_b"], name="conv")
    dt_raw = proj_b[:, B_DT_OFF:B_DT_OFF + SSM_HEADS].reshape(SEQ, SSM_GROUPS, SSM_HPG)
    dt_c = jnp.transpose(dt_raw, (1, 0, 2))
    dt_r = jnp.transpose(dt_raw, (1, 2, 0))
    per_head = lambda v: v.reshape(SSM_GROUPS, 1, SSM_HPG)
    par_row = jnp.concatenate([per_head(p["b_dt_bias"]), per_head(p["b_a_log"]), per_head(p["b_d"])], axis=1)
    ssd_par = (par_row, jnp.transpose(par_row[:, :2], (0, 2, 1)), p["b_gnorm"])
    cat1, hprev = _ssd_fwd(xbc, proj_b, dt_c, dt_r, *ssd_par, cat1, name="ssd")
    h3 = _mm_nn(cat1, w_out1, name="out_1", extras=(h2,), epilogue=_res_epilogue)
    w_ffn1_1, w_ffn2_1 = p["layer1_ffn"](h3) if "layer1_ffn" in p else (p["w_ffn1"][1], p["w_ffn2"][1])
    h4, ffn1 = _ffn_fwd(h3, row(p["norm_ffn"][1]), w_ffn1_1, w_ffn2_1, "1")

    loss, dh, g["final_norm"] = _loss_head(h4, row(p["final_norm"]), target, name="loss_head")

    dh, dnf1, dw1_1, dw2_1 = _ffn_bwd(dh, h3, row(p["norm_ffn"][1]), w_ffn1_1, w_ffn2_1, ffn1, 1)
    dcat1 = _mm_nt(dh, w_out1, name="out_dx_1")
    dwo_1 = _mm_tn_stacked(cat1, dh, name="out_dw_1", col_slots=False)
    dproj_b, dk1, dv1 = _attn_bwd(proj_b, B_Q_OFF, kv1, dcat1, B_IN_PAD, B_Q_OFF, name="attn_bwd_1")
    dproj_b, dxs, dbm, dcm, ddt_c, ddt_r, dpar_row, dpar_col, g["b_gnorm"] = _ssd_bwd(
        xbc, proj_b, dt_c, dt_r, *ssd_par, hprev, dcat1, dproj_b, name="ssd_bwd")
    dpar = dpar_row.at[:, :2].add(jnp.transpose(dpar_col, (0, 2, 1)))
    g["b_dt_bias"], g["b_a_log"], g["b_d"] = dpar[:, 0], dpar[:, 1], dpar[:, 2]
    dproj_b, g["b_conv_w"], g["b_conv_b"] = _conv_bwd(proj_b, p["b_conv_w"], p["b_conv_b"], dxs, dbm, dcm, dproj_b,
                                                      name="conv_bwd")
    ddt = jnp.transpose(ddt_c, (1, 0, 2)) + jnp.transpose(ddt_r, (2, 0, 1))
    ddt = jnp.pad(ddt.reshape(SEQ, SSM_HEADS), ((0, 0), (0, B_IN_PAD - B_DT_OFF - SSM_HEADS))).astype(BF16)
    dproj_b = lax.dynamic_update_slice(dproj_b, ddt, (0, B_DT_OFF))
    dwkv_1, dmn1 = _kv_bwd(mem, row(p["mem_norm"][1]), w_kv1, m1, dk1, dv1, 1)
    dwb = _b_in_grad_slots(_mm_tn(a1, dproj_b, name="b_in_dw"))
    da1 = _mm_nt(dproj_b, b_in, name="b_in_dx")
    dh, dnm1 = _rms_bwd(h2, row(p["norm_mix"][1]), da1, dh, name="mix_norm_bwd_1")
    layer1 = dict(w_kv=dwkv_1, w_out=dwo_1, w_ffn1=dw1_1, w_ffn2=dw2_1, b_in=dwb)
    token = () if after_layer1 is None else (after_layer1(layer1),)

    dh, dnf0, dw1_0, dw2_0 = _ffn_bwd(dh, h1, row(p["norm_ffn"][0]), p["w_ffn1"][0], p["w_ffn2"][0], ffn0, 0,
                                      after=token)
    ffn0_grads = dict(w_ffn1=dw1_0, w_ffn2=dw2_0)
    token = () if after_ffn0 is None else (after_ffn0(ffn0_grads),)
    dcat0 = _mm_nt(dh, p["w_out"][0], name="out_dx_0", after=token)
    dwo_0 = _mm_tn_stacked(cat0, dh, name="out_dw_0", col_slots=False)
    dproj_a, dk0, dv0 = _attn_bwd(proj_a, 2 * D_INNER, kv0, dcat0, A_IN, 2 * D_INNER, name="attn_bwd_0")
    dproj_a, g["a_ln_g"], g["a_ln_b"], g["a_ws"], dbs_col = _gate_bwd(
        proj_a, p["a_ln_g"], p["a_ln_b"], p["a_ws"], bs_col, dcat0, dproj_a, name="gate_bwd")
    g["a_bs"] = dbs_col.reshape(A_GROUPS, CHUNK)
    dwkv_0, dmn0 = _kv_bwd(mem, row(p["mem_norm"][0]), p["w_kv"][0], m0, dk0, dv0, 0)
    dwa = _mm_tn_stacked(a0, dproj_a, name="a_in_dw", col_slots=True)
    mixer0_grads = dict(w_kv=dwkv_0, w_out=dwo_0, a_in=dwa)
    token = () if after_mixer0 is None else (after_mixer0(mixer0_grads),)
    da0 = _mm_nt(dproj_a, p["a_in"], name="a_in_dx", after=token)
    dx, dnm0 = _rms_bwd(h0, row(p["norm_mix"][0]), da0, dh, name="mix_norm_bwd_0")

    g["norm_mix"] = jnp.concatenate([dnm0, dnm1], axis=0)
    g["norm_ffn"] = jnp.concatenate([dnf0, dnf1], axis=0)
    g["mem_norm"] = jnp.concatenate([dmn0, dmn1], axis=0)
    layer0 = dict(w_kv=dwkv_0, w_out=dwo_0, w_ffn1=dw1_0, w_ffn2=dw2_0, a_in=dwa)
    return loss, dx, g, layer0, layer1


def _b_in_full(gathered):
    n = B_IN // N_CHIPS
    dt0 = D_INNER + CONV_DIM - (N_CHIPS - 1) * n
    last = gathered[N_CHIPS - 1]
    return jnp.concatenate([*[gathered[k] for k in range(N_CHIPS - 1)], last[:, :dt0], last[:, dt0 + SSM_HEADS:],
                            last[:, dt0:dt0 + SSM_HEADS], jnp.zeros((D_MODEL, B_IN_PAD - B_IN), last.dtype)], axis=1)


def _b_in_grad_slots(d):
    n = B_IN // N_CHIPS
    dt0 = D_INNER + CONV_DIM
    last = jnp.concatenate([d[:, (N_CHIPS - 1) * n:dt0], d[:, B_DT_OFF:B_DT_OFF + SSM_HEADS], d[:, dt0:B_DT_OFF]], axis=1)
    slots = [*[d[:, k * n:(k + 1) * n] for k in range(N_CHIPS - 1)], last]
    half = D_MODEL // 2
    return jnp.stack([jnp.stack([s[h * half:(h + 1) * half] for s in slots]) for h in range(2)])


LARGE = ("w_kv", "w_out", "w_ffn1", "w_ffn2", "a_in", "b_in")
SMALL_REPL = ("norm_mix", "norm_ffn", "mem_norm", "a_ln_g", "a_ln_b", "a_ws", "a_bs", "b_dt_bias", "b_a_log", "b_d",
              "final_norm")
SMALL_SHARD = ("b_conv_w", "b_conv_b", "b_gnorm")
WEIGHTS = ("norm_mix", "norm_ffn", "mem_norm", "w_kv", "w_out", "w_ffn1", "w_ffn2", "a_in", "a_ln_g", "a_ln_b", "a_ws",
           "a_bs", "b_in", "b_conv_w", "b_conv_b", "b_dt_bias", "b_a_log", "b_d", "b_gnorm", "final_norm")
CONV_SHARD = CONV_DIM // N_CHIPS
GN_SHARD = D_INNER // N_CHIPS


LAYERED = ("w_kv", "w_out", "w_ffn1", "w_ffn2")
LAYER_TENSORS = (("w_kv", "w_out", "w_ffn1", "w_ffn2", "a_in"), ("w_kv", "w_out", "w_ffn1", "w_ffn2", "b_in"))


def _gather_weights(w):
    halves = lambda k, layer: (w[k][layer] if k in LAYERED else w[k][0]).reshape(2, -1, w[k].shape[-1]).astype(BF16)
    small = jnp.zeros((2, CONV_K, CONV_SHARD), F32)
    small = small.at[0].set(w["b_conv_w"][0])
    small = small.at[1, 0].set(w["b_conv_b"][0])
    small = small.at[1, 1, :GN_SHARD].set(w["b_gnorm"][0])
    gathered = _all_gather_shards([halves(k, 0) for k in LAYER_TENSORS[0]], small, name="gather_weights_0")
    got = dict(zip(LAYER_TENSORS[0], gathered))
    slots = lambda a: a.reshape(N_CHIPS, -1, a.shape[-1])
    rows = lambda a: a.reshape(-1, a.shape[-1])
    p = dict(w_kv=[slots(got["w_kv"])], w_out=[rows(got["w_out"])], w_ffn1=[slots(got["w_ffn1"])],
             w_ffn2=[rows(got["w_ffn2"])], a_in=slots(got["a_in"]))
    sm = gathered[-1]
    p["b_conv_w"] = jnp.transpose(sm[:, 0], (1, 0, 2)).reshape(CONV_K, CONV_DIM)
    p["b_conv_b"] = sm[:, 1, 0].reshape(1, CONV_DIM)
    p["b_gnorm"] = sm[:, 1, 1, :GN_SHARD].reshape(1, D_INNER)

    after, started = (gathered[0],), {}
    for tag, names in (("mixer", ("w_kv", "w_out", "b_in")), ("ffn", ("w_ffn1", "w_ffn2"))):
        started[tag] = _gather_start([halves(k, 1) for k in names], after, name=f"gather_start_1_{tag}")
        after = (started[tag][-1],)
    p["after_start"] = after

    def finish(tag, first):
        send_sems, recv_sems, shards, zones, _ = started[tag]
        shards, zones = _gather_wait(send_sems, recv_sems, shards, zones, (first,), name=f"gather_wait_1_{tag}")
        return _gather_finish(shards, zones, name=f"gather_finish_1_{tag}")

    def layer1_mixer(first):
        kv, wo, b_in = finish("mixer", first)
        return slots(kv), rows(wo), _b_in_full(slots(b_in))

    def layer1_ffn(first):
        w1, w2 = finish("ffn", first)
        return slots(w1), rows(w2)

    p.update(layer1_mixer=layer1_mixer, layer1_ffn=layer1_ffn)
    return p


def _pair_parts(grads, tag):
    stacks = [g.reshape(2, -1, g.shape[-1]) for g in grads.values()]
    parts = _pair_reduce(stacks, name=f"grads_pair_reduce_{tag}")
    return [t.reshape(N_CHIPS, -1, t.shape[-1]) for t in parts]


def _chip_sums(chip, names, parts, landed, tag):
    return {k: _sum_contributions(chip, t, u, name=f"grads_chip_sum_{k}_{tag}")
            for k, t, u in zip(names, parts, landed)}


def _small_layout(shapes):
    offs, o = {}, 0
    for k in (*SMALL_REPL, *SMALL_SHARD):
        size = math.prod(shapes[k])
        offs[k] = (o, size)
        o += size
    rows = -(-(o + 1) // (8 * 128)) * 8
    return offs, rows


def _small_start(g, loss_part, full_shapes):
    _, rows = _small_layout(full_shapes)
    flat = jnp.concatenate([*[g[k].reshape(-1) for k in (*SMALL_REPL, *SMALL_SHARD)], loss_part[0, :1]])
    flat = jnp.pad(flat, (0, rows * 128 - flat.shape[0])).reshape(rows, 128)
    return _all_gather_start(flat, name="small_gather_start")


def _small_finish(started, device, after, full_shapes):
    send_sems, recv_sems, flat, zones, _ = started
    flat, zones = _all_gather_wait(send_sems, recv_sems, flat, zones, after, name="small_gather_wait")
    total = _sum_devices(device, flat, zones, name="small_sum").reshape(-1)
    offs, _ = _small_layout(full_shapes)
    end = max(o + n for o, n in offs.values())
    return {k: total[o:o + n].reshape(full_shapes[k]) for k, (o, n) in offs.items()}, total[end]


def kernel(x, mem, norm_mix, norm_ffn, mem_norm, w_kv, w_out, w_ffn1, w_ffn2, a_in, a_ln_g, a_ln_b, a_ws, a_bs, b_in, b_conv_w, b_conv_b, b_dt_bias, b_a_log, b_d, b_gnorm, final_norm, loss_target, m_norm_mix, m_norm_ffn, m_mem_norm, m_w_kv, m_w_out, m_w_ffn1, m_w_ffn2, m_a_in, m_a_ln_g, m_a_ln_b, m_a_ws, m_a_bs, m_b_in, m_b_conv_w, m_b_conv_b, m_b_dt_bias, m_b_a_log, m_b_d, m_b_gnorm, m_final_norm, v_norm_mix, v_norm_ffn, v_mem_norm, v_w_kv, v_w_out, v_w_ffn1, v_w_ffn2, v_a_in, v_a_ln_g, v_a_ln_b, v_a_ws, v_a_bs, v_b_in, v_b_conv_w, v_b_conv_b, v_b_dt_bias, v_b_a_log, v_b_d, v_b_gnorm, v_final_norm):
    w = dict(norm_mix=norm_mix, norm_ffn=norm_ffn, mem_norm=mem_norm, w_kv=w_kv, w_out=w_out, w_ffn1=w_ffn1,
             w_ffn2=w_ffn2, a_in=a_in, a_ln_g=a_ln_g, a_ln_b=a_ln_b, a_ws=a_ws, a_bs=a_bs, b_in=b_in, b_conv_w=b_conv_w,
             b_conv_b=b_conv_b, b_dt_bias=b_dt_bias, b_a_log=b_a_log, b_d=b_d, b_gnorm=b_gnorm, final_norm=final_norm)
    mom = dict(norm_mix=m_norm_mix, norm_ffn=m_norm_ffn, mem_norm=m_mem_norm, w_kv=m_w_kv, w_out=m_w_out,
               w_ffn1=m_w_ffn1, w_ffn2=m_w_ffn2, a_in=m_a_in, a_ln_g=m_a_ln_g, a_ln_b=m_a_ln_b, a_ws=m_a_ws,
               a_bs=m_a_bs, b_in=m_b_in, b_conv_w=m_b_conv_w, b_conv_b=m_b_conv_b, b_dt_bias=m_b_dt_bias,
               b_a_log=m_b_a_log, b_d=m_b_d, b_gnorm=m_b_gnorm, final_norm=m_final_norm)
    var = dict(norm_mix=v_norm_mix, norm_ffn=v_norm_ffn, mem_norm=v_mem_norm, w_kv=v_w_kv, w_out=v_w_out,
               w_ffn1=v_w_ffn1, w_ffn2=v_w_ffn2, a_in=v_a_in, a_ln_g=v_a_ln_g, a_ln_b=v_a_ln_b, a_ws=v_a_ws,
               a_bs=v_a_bs, b_in=v_b_in, b_conv_w=v_b_conv_w, b_conv_b=v_b_conv_b, b_dt_bias=v_b_dt_bias,
               b_a_log=v_b_a_log, b_d=v_b_d, b_gnorm=v_b_gnorm, final_norm=v_final_norm)

    p = _gather_weights(w)
    p.update(norm_mix=norm_mix, norm_ffn=norm_ffn, mem_norm=mem_norm, a_ln_g=a_ln_g, a_ln_b=a_ln_b, a_ws=a_ws[0],
             a_bs=a_bs[0], b_dt_bias=b_dt_bias, b_a_log=b_a_log, b_d=b_d, final_norm=final_norm)
    chip = 2 * lax.axis_index("x") + lax.axis_index("y")
    chip_arr = jnp.reshape(chip, (1,)).astype(jnp.int32)
    started = {}

    def start_scatter(tag):
        def hook(grads):
            start = _chip_scatter_start(_pair_parts(grads, tag), name=f"grads_chip_scatter_start_{tag}")
            started[tag] = (tuple(grads), start)
            return start[-1]
        return hook

    loss_part, dx, g, _, _ = _local_step(x[0], mem[0], loss_target[0], p, start_scatter("1"), start_scatter("0f"),
                                         start_scatter("0m"))
    full_shapes = {k: w[k].shape for k in SMALL_REPL}
    full_shapes.update(b_conv_w=(1, CONV_K, CONV_DIM), b_conv_b=(1, CONV_DIM), b_gnorm=(1, D_INNER))
    small = _small_start(g, loss_part, full_shapes)

    def finish_scatter(tag, *first):
        names, (send_sems, recv_sems, parts, lands, _) = started[tag]
        parts, landed = _chip_scatter_wait(send_sems, recv_sems, parts, lands, first,
                                           name=f"grads_chip_scatter_wait_{tag}")
        return _chip_sums(chip_arr, names, parts, landed, tag)

    def adamw(names, grads):
        for k in names:
            shape = w[k].shape
            if len(shape) == 3 and shape[2] % 128 and not shape[1] % 128:
                flat = unflat = lambda a: jnp.transpose(a, (0, 2, 1))
            else:
                flat = (lambda a: a) if len(shape) == 3 else (lambda a: a.reshape(1, -1, shape[-1]))
                unflat = lambda a: a.reshape(shape)
            d, m_new, v_new = _adamw(flat(w[k]), flat(grads[k]), flat(mom[k]), flat(var[k]), name=f"adamw_{k}")
            delta[k], new_m[k], new_v[k] = unflat(d), unflat(m_new), unflat(v_new)

    delta, new_m, new_v = {}, {}, {}
    halves = [finish_scatter("0f", dx, small[-1]), finish_scatter("1", dx)]
    early = ("w_ffn1", "w_ffn2", "b_in")
    shared = _pair_share([[halves[layer][k] for layer in range(2) if k in halves[layer]] for k in early],
                         name="grads_pair_share_early")
    device = jnp.reshape(2 * chip + lax.axis_index("c"), (1,)).astype(jnp.int32)
    grads, loss = _small_finish(small, device, (shared[0],), full_shapes)
    grads["b_conv_w"] = lax.dynamic_slice_in_dim(grads["b_conv_w"], chip * CONV_SHARD, CONV_SHARD, axis=2)
    grads["b_conv_b"] = lax.dynamic_slice_in_dim(grads["b_conv_b"], chip * CONV_SHARD, CONV_SHARD, axis=1)
    grads["b_gnorm"] = lax.dynamic_slice_in_dim(grads["b_gnorm"], chip * GN_SHARD, GN_SHARD, axis=1)
    grads.update({k: a.reshape(w[k].shape) for k, a in zip(early, shared)})
    adamw([k for k in WEIGHTS if k in grads], grads)
    halves[0].update(finish_scatter("0m", delta["w_ffn2"]))
    late = ("w_kv", "w_out", "a_in")
    shared = _pair_share([[halves[layer][k] for layer in range(2) if k in halves[layer]] for k in late],
                         name="grads_pair_share_late")
    grads.update({k: a.reshape(w[k].shape) for k, a in zip(late, shared)})
    adamw(late, grads)

    return (loss, dx.reshape(x.shape), *[grads[k] for k in WEIGHTS], *[delta[k] for k in WEIGHTS],
            *[new_m[k] for k in WEIGHTS], *[new_v[k] for k in WEIGHTS])
```

```python
import math

import jax
import jax.numpy as jnp
from jax import lax
from jax.experimental import pallas as pl
from jax.experimental.pallas import tpu as pltpu

F32 = jnp.float32
BF16 = jnp.bfloat16
SDS = jax.ShapeDtypeStruct

D_MODEL = 1024
SEQ = 2048
CHUNK = 128
N_MEM = 256
D_INNER = 2048
A_GROUPS = 8
A_GROUP_W = D_INNER // A_GROUPS
SSM_HEADS = 32
SSM_HEAD_DIM = 64
SSM_GROUPS = 4
SSM_HPG = 8
SSM_STATE = 128
SSM_GROUP_W = SSM_HPG * SSM_HEAD_DIM
CONV_K = 4
CONV_DIM = 3072
X_HEADS = 4
X_HEAD_DIM = 256
X_WIDTH = 1024
MIX_OUT = 3072
D_FF = 4096
A_IN = 5120
B_IN = 6176
B_IN_PAD = 6272
B_Q_OFF = 5120
B_DT_OFF = 6144
N_CHUNKS = SEQ // CHUNK
EPS = 1e-6
N_CHIPS = 4

ADAM_LR = 0.001
ADAM_B1 = 0.9
ADAM_B2 = 0.999
ADAM_EPS = 1e-08
ADAM_WD = 0.01
ADAM_STEP = 10

VMEM_LIMIT = 48 * 1024 * 1024
MESH = pl.DeviceIdType.MESH


def _cparams(sem):
    return pltpu.CompilerParams(dimension_semantics=sem, vmem_limit_bytes=VMEM_LIMIT)


def _dot(a, b, dims=(((1,), (0,)), ((), ()))):
    return lax.dot_general(a.astype(BF16), b.astype(BF16), dims, preferred_element_type=F32)


def _dot_nt(a, b):
    return _dot(a, b, (((1,), (1,)), ((), ())))


def _dot_tn(a, b):
    return _dot(a, b, (((0,), (0,)), ((), ())))


def _pick(n, cands):
    for c in cands:
        if n % c == 0:
            return c
    raise ValueError(f"no tile for {n}")


def _mm_call(a, b, *, dims, grid, a_spec, b_spec, acc_shape, out_shapes, out_specs, name,
             extras=(), extra_specs=(), epilogue=None, after=()):
    n_k = grid[2]
    n_extra = len(extras)
    n_out = len(out_shapes)
    n_in = 2 + n_extra + len(after)

    def finish(total, extra_refs, out_refs):
        vals = (total,) if epilogue is None else epilogue(total, *[e[...] for e in extra_refs])
        for o_ref, v in zip(out_refs, vals):
            o_ref[...] = v.astype(o_ref.dtype)

    def body_one_step(*refs):
        finish(_dot(refs[0][...], refs[1][...], dims), refs[2:2 + n_extra], refs[n_in:n_in + n_out])

    def body(*refs):
        acc = refs[-1]
        k = pl.program_id(2)

        @pl.when(k == 0)
        def _():
            acc[...] = jnp.zeros_like(acc)

        acc[...] += _dot(refs[0][...], refs[1][...], dims)

        @pl.when(k == n_k - 1)
        def _():
            finish(acc[...], refs[2:2 + n_extra], refs[n_in:n_in + n_out])

    return pl.pallas_call(
        body_one_step if n_k == 1 else body, grid=grid,
        in_specs=[a_spec, b_spec, *extra_specs, *([ANY] * len(after))], out_specs=list(out_specs),
        out_shape=list(out_shapes), scratch_shapes=[] if n_k == 1 else [pltpu.VMEM(acc_shape, F32)],
        compiler_params=_cparams(("parallel", "parallel", "arbitrary")), name=name,
    )(a, b, *extras, *after)


def _w_dims(w):
    if w.ndim == 2:
        return w.shape[0], w.shape[1], 1, w.shape[1]
    return w.shape[1], w.shape[0] * w.shape[2], w.shape[0], w.shape[2]


def _mm_nn(a, w, *, name, out_dtype=F32, a_cols=None, extras=(), epilogue=None, n_out_dtypes=None, after=()):
    m = a.shape[0]
    k_dim, n_dim, _, n_slot = _w_dims(w)
    a_off, a_w = (0, a.shape[1]) if a_cols is None else a_cols
    assert a_w == k_dim
    tm = _pick(m, (2048, 1024, 512, 256))
    tn = _pick(n_slot, (512, 896, 640, 256, 128))
    tk = _pick(k_dim, (1024, 768, 512, 384, 256, 128))
    assert a_off % tk == 0
    nb = n_slot // tn
    a_spec = pl.BlockSpec((tm, tk), lambda i, j, k: (i, a_off // tk + k))
    if w.ndim == 2:
        b_spec = pl.BlockSpec((tk, tn), lambda i, j, k: (k, j))
    else:
        b_spec = pl.BlockSpec((None, tk, tn), lambda i, j, k: (j // nb, k, j % nb))
    o_spec = pl.BlockSpec((tm, tn), lambda i, j, k: (i, j))
    dts = n_out_dtypes or (out_dtype,)
    outs = _mm_call(a, w, dims=(((1,), (0,)), ((), ())), grid=(m // tm, n_dim // tn, k_dim // tk),
                    a_spec=a_spec, b_spec=b_spec, acc_shape=(tm, tn),
                    out_shapes=[SDS((m, n_dim), dt) for dt in dts], out_specs=[o_spec] * len(dts), name=name,
                    extras=extras, extra_specs=[o_spec] * len(extras), epilogue=epilogue, after=after)
    return outs if n_out_dtypes else outs[0]


def _mm_nt(a, w, *, name, out_dtype=F32, extras=(), epilogue=None, after=()):
    m = a.shape[0]
    k_dim, n_dim, _, n_slot = _w_dims(w)
    assert a.shape[1] == n_dim
    tm = _pick(m, (2048, 1024, 512, 256))
    to = _pick(k_dim, (512, 384, 256, 128))
    tc = _pick(n_slot, (1280, 1024, 896, 640, 512, 256, 128))
    nb = n_slot // tc
    a_spec = pl.BlockSpec((tm, tc), lambda i, j, k: (i, k))
    if w.ndim == 2:
        b_spec = pl.BlockSpec((to, tc), lambda i, j, k: (j, k))
    else:
        b_spec = pl.BlockSpec((None, to, tc), lambda i, j, k: (k // nb, j, k % nb))
    o_spec = pl.BlockSpec((tm, to), lambda i, j, k: (i, j))
    return _mm_call(a, w, dims=(((1,), (1,)), ((), ())), grid=(m // tm, k_dim // to, n_dim // tc),
                    a_spec=a_spec, b_spec=b_spec, acc_shape=(tm, to),
                    out_shapes=[SDS((m, k_dim), out_dtype)], out_specs=[o_spec], name=name,
                    extras=extras, extra_specs=[o_spec] * len(extras), epilogue=epilogue, after=after)[0]


def _mm_tn(x, dy, *, name, x_cols=None):
    s = x.shape[0]
    x_off, k_dim = (0, x.shape[1]) if x_cols is None else x_cols
    n_dim = dy.shape[1]
    tm = _pick(k_dim, (1024, 768, 512, 384, 256, 128))
    tn = _pick(n_dim, (512, 896, 640, 256, 128))
    tk = _pick(s, (2048, 1024, 512, 256))
    assert x_off % tm == 0
    a_spec = pl.BlockSpec((tk, tm), lambda i, j, k: (k, x_off // tm + i))
    b_spec = pl.BlockSpec((tk, tn), lambda i, j, k: (k, j))
    o_spec = pl.BlockSpec((tm, tn), lambda i, j, k: (i, j))
    return _mm_call(x, dy, dims=(((0,), (0,)), ((), ())), grid=(k_dim // tm, n_dim // tn, s // tk),
                    a_spec=a_spec, b_spec=b_spec, acc_shape=(tm, tn),
                    out_shapes=[SDS((k_dim, n_dim), F32)], out_specs=[o_spec], name=name)[0]


def _mm_tn_stacked(x, dy, *, name, col_slots):
    s, k_dim = x.shape
    n_dim = dy.shape[1]
    r, c = (k_dim // 2, n_dim // N_CHIPS) if col_slots else (k_dim // N_CHIPS // 2, n_dim)
    tm = 2 * r
    tn = _pick(c, (512, 896, 640, 256, 128))
    tk = _pick(s, (2048, 1024, 512, 256))
    a_spec = pl.BlockSpec((tk, tm), lambda i, j, k: (k, i))
    b_spec = pl.BlockSpec((tk, tn), lambda i, j, k: (k, j))
    if col_slots:
        nb = c // tn
        o_spec = pl.BlockSpec((2, None, r, tn), lambda i, j, k: (0, j // nb, 0, j % nb))
    else:
        o_spec = pl.BlockSpec((2, None, r, tn), lambda i, j, k: (0, i, 0, j))
    return _mm_call(x, dy, dims=(((0,), (0,)), ((), ())), grid=(k_dim // tm, n_dim // tn, s // tk),
                    a_spec=a_spec, b_spec=b_spec, acc_shape=(tm, tn), epilogue=lambda acc: (acc.reshape(2, r, tn),),
                    out_shapes=[SDS((2, N_CHIPS, r, c), F32)], out_specs=[o_spec], name=name)[0]


def _rms(x, g):
    return x * lax.rsqrt(jnp.mean(x * x, axis=-1, keepdims=True) + EPS) * g


def _rms_fwd(h, g, *, name):
    rows, d = h.shape
    tr = _pick(rows, (512, 256))

    def body(h_ref, g_ref, o_ref):
        o_ref[...] = _rms(h_ref[...], g_ref[...]).astype(o_ref.dtype)

    return pl.pallas_call(
        body, grid=(rows // tr,),
        in_specs=[pl.BlockSpec((tr, d), lambda i: (i, 0)), pl.BlockSpec((1, d), lambda i: (0, 0))],
        out_specs=pl.BlockSpec((tr, d), lambda i: (i, 0)), out_shape=SDS((rows, d), BF16),
        compiler_params=_cparams(("parallel",)), name=name)(h, g)


def _rms_bwd(h, g, da, dres, *, name):
    rows, d = h.shape
    tr = _pick(rows, (512, 256))

    def body(h_ref, g_ref, da_ref, dres_ref, dh_ref, dg_ref):
        _, vjp = jax.vjp(_rms, h_ref[...], g_ref[...])
        dh, dg = vjp(da_ref[...].astype(F32))
        dh_ref[...] = dres_ref[...] + dh

        @pl.when(pl.program_id(0) == 0)
        def _():
            dg_ref[...] = jnp.zeros_like(dg_ref)

        dg_ref[...] += dg

    row_spec = pl.BlockSpec((tr, d), lambda i: (i, 0))
    vec_spec = pl.BlockSpec((1, d), lambda i: (0, 0))
    return pl.pallas_call(
        body, grid=(rows // tr,), in_specs=[row_spec, vec_spec, row_spec, row_spec],
        out_specs=[row_spec, vec_spec], out_shape=[SDS((rows, d), F32), SDS((1, d), F32)],
        compiler_params=_cparams(("arbitrary",)), name=name)(h, g, da, dres)


def _loss_head(h, g, target, *, name):
    rows, d = h.shape
    tr = _pick(rows, (512, 256))

    def body(h_ref, g_ref, t_ref, loss_ref, dh_ref, dg_ref):
        y, vjp = jax.vjp(_rms, h_ref[...], g_ref[...])
        err = y - t_ref[...]
        dh, dg = vjp(err * (1.0 / d))
        dh_ref[...] = dh

        @pl.when(pl.program_id(0) == 0)
        def _():
            dg_ref[...] = jnp.zeros_like(dg_ref)
            loss_ref[...] = jnp.zeros_like(loss_ref)

        dg_ref[...] += dg
        part = jnp.sum(jnp.sum(err * err, axis=-1, keepdims=True), axis=0, keepdims=True) * (0.5 / d)
        loss_ref[...] += jnp.broadcast_to(part, loss_ref.shape)

    row_spec = pl.BlockSpec((tr, d), lambda i: (i, 0))
    vec_spec = pl.BlockSpec((1, d), lambda i: (0, 0))
    loss_spec = pl.BlockSpec((8, 128), lambda i: (0, 0))
    return pl.pallas_call(
        body, grid=(rows // tr,), in_specs=[row_spec, vec_spec, row_spec],
        out_specs=[loss_spec, row_spec, vec_spec],
        out_shape=[SDS((8, 128), F32), SDS((rows, d), F32), SDS((1, d), F32)],
        compiler_params=_cparams(("arbitrary",)), name=name)(h, g, target)


def _gelu(x):
    return 0.5 * x * (1.0 + lax.erf(x * (1.0 / math.sqrt(2.0))))


def _gate_tile(pu, pv, ln_g, ln_b, ws, bs_t):
    u = [_gelu(p) for p in pu]
    v = [_gelu(p) for p in pv]
    mu = sum(jnp.sum(t, axis=-1, keepdims=True) for t in v) * (1.0 / D_INNER)
    vc = [t - mu for t in v]
    var = sum(jnp.sum(t * t, axis=-1, keepdims=True) for t in vc) * (1.0 / D_INNER)
    rstd = lax.rsqrt(var + EPS)
    row = lax.broadcasted_iota(jnp.int32, (CHUNK, CHUNK), 0)
    col = lax.broadcasted_iota(jnp.int32, (CHUNK, CHUNK), 1)
    out = []
    for gi in range(A_GROUPS):
        vn = vc[gi] * rstd * ln_g[gi] + ln_b[gi]
        w = jnp.where(row >= col, ws[gi], 0.0)
        sv = _dot(w, vn) + bs_t[gi]
        out.append(u[gi] * sv)
    return out


def _split(ref, n, width):
    return [ref[:, i * width:(i + 1) * width] for i in range(n)]


def _gate_in_specs():
    return [
        pl.BlockSpec((CHUNK, D_INNER), lambda c: (c, 0)),
        pl.BlockSpec((CHUNK, D_INNER), lambda c: (c, 1)),
        pl.BlockSpec((1, D_INNER), lambda c: (0, 0)),
        pl.BlockSpec((1, D_INNER), lambda c: (0, 0)),
        pl.BlockSpec((A_GROUPS, CHUNK, CHUNK), lambda c: (0, 0, 0)),
        pl.BlockSpec((A_GROUPS, CHUNK, 1), lambda c: (0, 0, 0)),
    ]


def _gate_args(u_ref, v_ref, g_ref, b_ref, ws_ref, bs_ref):
    ng, gw = A_GROUPS, A_GROUP_W
    return (_split(u_ref, ng, gw), _split(v_ref, ng, gw), _split(g_ref, ng, gw), _split(b_ref, ng, gw),
            [ws_ref[i] for i in range(ng)], [bs_ref[i] for i in range(ng)])


def _gate_fwd(proj, ln_g, ln_b, ws, bs_col, mixcat, *, name):
    def body(u_ref, v_ref, g_ref, b_ref, ws_ref, bs_ref, cat_in, cat_ref):
        del cat_in
        out = _gate_tile(*_gate_args(u_ref, v_ref, g_ref, b_ref, ws_ref, bs_ref))
        for gi, o in enumerate(out):
            cat_ref[:, gi * A_GROUP_W:(gi + 1) * A_GROUP_W] = o.astype(cat_ref.dtype)

    return pl.pallas_call(
        body, grid=(N_CHUNKS,), in_specs=[*_gate_in_specs(), pl.BlockSpec(memory_space=pl.ANY)],
        out_specs=pl.BlockSpec((CHUNK, D_INNER), lambda c: (c, 0)), out_shape=SDS(mixcat.shape, mixcat.dtype),
        input_output_aliases={6: 0}, compiler_params=_cparams(("parallel",)), name=name,
    )(proj, proj, ln_g, ln_b, ws, bs_col, mixcat)


def _gate_bwd(proj, ln_g, ln_b, ws, bs_col, dcat, dproj, *, name):
    ng, gw = A_GROUPS, A_GROUP_W

    def body(u_ref, v_ref, g_ref, b_ref, ws_ref, bs_ref, d_ref, dproj_in, dproj_ref, dg_ref, db_ref, dws_ref, dbs_ref):
        del dproj_in
        args = _gate_args(u_ref, v_ref, g_ref, b_ref, ws_ref, bs_ref)
        _, vjp = jax.vjp(_gate_tile, *args)
        dpu, dpv, dg, db, dws, dbs = vjp(_split(d_ref, ng, gw))
        for gi in range(ng):
            dproj_ref[:, gi * gw:(gi + 1) * gw] = dpu[gi].astype(dproj_ref.dtype)
            dproj_ref[:, D_INNER + gi * gw:D_INNER + (gi + 1) * gw] = dpv[gi].astype(dproj_ref.dtype)

        @pl.when(pl.program_id(0) == 0)
        def _():
            for r in (dg_ref, db_ref, dws_ref, dbs_ref):
                r[...] = jnp.zeros_like(r)

        for gi in range(ng):
            dg_ref[:, gi * gw:(gi + 1) * gw] += dg[gi]
            db_ref[:, gi * gw:(gi + 1) * gw] += db[gi]
            dws_ref[gi] += dws[gi]
            dbs_ref[gi] += dbs[gi]

    in_specs = _gate_in_specs()
    return pl.pallas_call(
        body, grid=(N_CHUNKS,),
        in_specs=[*in_specs, pl.BlockSpec((CHUNK, D_INNER), lambda c: (c, 0)), pl.BlockSpec(memory_space=pl.ANY)],
        out_specs=[pl.BlockSpec((CHUNK, 2 * D_INNER), lambda c: (c, 0)), *in_specs[2:]],
        out_shape=[SDS(dproj.shape, dproj.dtype), SDS((1, D_INNER), F32), SDS((1, D_INNER), F32),
                   SDS((ng, CHUNK, CHUNK), F32), SDS((ng, CHUNK, 1), F32)],
        input_output_aliases={7: 0}, compiler_params=_cparams(("arbitrary",)), name=name,
    )(proj, proj, ln_g, ln_b, ws, bs_col, dcat, dproj)


ATT_TQ = 512


def _attn_tile(q, k, v):
    s = _dot_nt(q, k) * (1.0 / math.sqrt(X_HEAD_DIM))
    s = s - jnp.max(s, axis=-1, keepdims=True)
    e = jnp.exp(s)
    p = e / jnp.sum(e, axis=-1, keepdims=True)
    return _dot(p, v)


def _attn_in_specs(q_blk, order):
    hd = X_HEAD_DIM
    return [
        pl.BlockSpec((ATT_TQ, hd), lambda a, b: (order(a, b)[0], q_blk + order(a, b)[1])),
        pl.BlockSpec((N_MEM, hd), lambda a, b: (0, order(a, b)[1])),
        pl.BlockSpec((N_MEM, hd), lambda a, b: (0, X_HEADS + order(a, b)[1])),
    ]


def _attn_fwd(proj, q_off, kv, *, name):
    order = lambda i, h: (i, h)
    cat_blk = D_INNER // X_HEAD_DIM

    def body(q_ref, k_ref, v_ref, o_ref):
        o_ref[...] = _attn_tile(q_ref[...], k_ref[...], v_ref[...]).astype(o_ref.dtype)

    return pl.pallas_call(
        body, grid=(SEQ // ATT_TQ, X_HEADS), in_specs=_attn_in_specs(q_off // X_HEAD_DIM, order),
        out_specs=pl.BlockSpec((ATT_TQ, X_HEAD_DIM), lambda i, h: (i, cat_blk + h)),
        out_shape=SDS((SEQ, MIX_OUT), BF16), compiler_params=_cparams(("parallel", "parallel")), name=name,
    )(proj, kv, kv)


def _attn_bwd(proj, q_off, kv, dcat, dproj_width, dq_off, *, name):
    order = lambda h, i: (i, h)
    cat_blk = D_INNER // X_HEAD_DIM
    dq_blk = dq_off // X_HEAD_DIM

    def body(q_ref, k_ref, v_ref, do_ref, dq_ref, dk_ref, dv_ref):
        _, vjp = jax.vjp(_attn_tile, q_ref[...], k_ref[...], v_ref[...])
        dq, dk, dv = vjp(do_ref[...])
        dq_ref[...] = dq.astype(dq_ref.dtype)

        @pl.when(pl.program_id(1) == 0)
        def _():
            dk_ref[...] = jnp.zeros_like(dk_ref)
            dv_ref[...] = jnp.zeros_like(dv_ref)

        dk_ref[...] += dk
        dv_ref[...] += dv

    kv_spec = pl.BlockSpec((N_MEM, X_HEAD_DIM), lambda h, i: (0, h))
    return pl.pallas_call(
        body, grid=(X_HEADS, SEQ // ATT_TQ),
        in_specs=[*_attn_in_specs(q_off // X_HEAD_DIM, order),
                  pl.BlockSpec((ATT_TQ, X_HEAD_DIM), lambda h, i: (i, cat_blk + h))],
        out_specs=[pl.BlockSpec((ATT_TQ, X_HEAD_DIM), lambda h, i: (i, dq_blk + h)), kv_spec, kv_spec],
        out_shape=[SDS((SEQ, dproj_width), BF16), SDS((N_MEM, X_WIDTH), F32), SDS((N_MEM, X_WIDTH), F32)],
        compiler_params=_cparams(("parallel", "arbitrary")), name=name,
    )(proj, kv, kv, dcat)


CONV_TC = 512


def _shift_down(x, s):
    if s == 0:
        return x
    row = lax.broadcasted_iota(jnp.int32, x.shape, 0)
    return jnp.where(row >= s, pltpu.roll(x, s, 0), 0.0)


def _shift_up(x, s):
    if s == 0:
        return x
    n = x.shape[0]
    row = lax.broadcasted_iota(jnp.int32, x.shape, 0)
    return jnp.where(row < n - s, pltpu.roll(x, n - s, 0), 0.0)


def _conv_pre(x, w_ref, b_ref):
    pre = b_ref[...] + jnp.zeros_like(x)
    for k in range(CONV_K):
        pre = pre + w_ref[k:k + 1, :] * _shift_down(x, CONV_K - 1 - k)
    return pre


def _conv_fwd(proj, w, b, *, name):
    blk0 = D_INNER // CONV_TC

    def body(x_ref, w_ref, b_ref, o_ref):
        pre = _conv_pre(x_ref[...], w_ref, b_ref)
        o_ref[...] = pre * jax.nn.sigmoid(pre)

    return pl.pallas_call(
        body, grid=(CONV_DIM // CONV_TC,),
        in_specs=[pl.BlockSpec((SEQ, CONV_TC), lambda j: (0, blk0 + j)), pl.BlockSpec((CONV_K, CONV_TC), lambda j: (0, j)),
                  pl.BlockSpec((1, CONV_TC), lambda j: (0, j))],
        out_specs=pl.BlockSpec((SEQ, CONV_TC), lambda j: (0, j)), out_shape=SDS((SEQ, CONV_DIM), F32),
        compiler_params=_cparams(("parallel",)), name=name)(proj, w, b)


def _conv_bwd(proj, w, b, dxs, dbm, dcm, dproj, *, name):
    tc = CONV_TC // 2
    blk0 = D_INNER // tc
    n_x = D_INNER // tc
    n_b = SSM_GROUPS * SSM_STATE // tc

    def body(x_ref, w_ref, b_ref, dxs_ref, dbm_ref, dcm_ref, dproj_in, dproj_ref, dw_ref, db_ref):
        del dproj_in
        j = pl.program_id(0)
        x = x_ref[...]
        pre = _conv_pre(x, w_ref, b_ref)
        sg = jax.nn.sigmoid(pre)
        dact = jnp.where(j < n_x, dxs_ref[...], jnp.where(j < n_x + n_b, dbm_ref[...], dcm_ref[...]))
        dpre = dact * (sg * (1.0 + pre * (1.0 - sg)))
        dx = jnp.zeros_like(x)
        for k in range(CONV_K):
            s = CONV_K - 1 - k
            dx = dx + w_ref[k:k + 1, :] * _shift_up(dpre, s)
            dw_ref[k:k + 1, :] = jnp.sum(dpre * _shift_down(x, s), axis=0, keepdims=True)
        dproj_ref[...] = dx.astype(dproj_ref.dtype)
        db_ref[...] = jnp.sum(dpre, axis=0, keepdims=True)

    clip = lambda v, hi: jnp.minimum(jnp.maximum(v, 0), hi)
    return pl.pallas_call(
        body, grid=(CONV_DIM // tc,),
        in_specs=[pl.BlockSpec((SEQ, tc), lambda j: (0, blk0 + j)), pl.BlockSpec((CONV_K, tc), lambda j: (0, j)),
                  pl.BlockSpec((1, tc), lambda j: (0, j)),
                  pl.BlockSpec((SEQ, tc), lambda j: (0, clip(j, n_x - 1))),
                  pl.BlockSpec((SEQ, tc), lambda j: (0, clip(j - n_x, n_b - 1))),
                  pl.BlockSpec((SEQ, tc), lambda j: (0, clip(j - n_x - n_b, n_b - 1))),
                  pl.BlockSpec(memory_space=pl.ANY)],
        out_specs=[pl.BlockSpec((SEQ, tc), lambda j: (0, blk0 + j)), pl.BlockSpec((CONV_K, tc), lambda j: (0, j)),
                   pl.BlockSpec((1, tc), lambda j: (0, j))],
        out_shape=[SDS(dproj.shape, dproj.dtype), SDS((CONV_K, CONV_DIM), F32), SDS((1, CONV_DIM), F32)],
        input_output_aliases={6: 0}, compiler_params=_cparams(("parallel",)), name=name,
    )(proj, w, b, dxs, dbm, dcm, dproj)


SSM_PAIRS = SSM_HPG // 2


def _dot_exact01(x, m01, m01_t, x_first, differentiable):
    def product(v, m):
        hi = v.astype(BF16)
        rest = v - hi.astype(F32)
        mid = rest.astype(BF16)
        lo = (rest - mid.astype(F32)).astype(BF16)
        dims = (((1,), (0,)), ((), ()))
        dot = lambda part: lax.dot_general(*((part, m) if x_first else (m, part)), dims, preferred_element_type=F32)
        return dot(hi) + dot(mid) + dot(lo)

    if not differentiable:
        return product(x, m01)

    @jax.custom_vjp
    def exact(v):
        return product(v, m01)

    exact.defvjp(lambda v: (product(v, m01), None), lambda _, ct: (product(ct, m01_t),))
    return exact(x)


def _ssd_tile(xp, zp, bm, cm, hp, dt_c, dt_r, bias, bias_col, alog, alog_col, dsk, gnp, differentiable=False):
    row = lax.broadcasted_iota(jnp.int32, (CHUNK, CHUNK), 0)
    col = lax.broadcasted_iota(jnp.int32, (CHUNK, CHUNK), 1)
    causal = row >= col
    left = col < SSM_HEAD_DIM
    top = row < SSM_HEAD_DIM
    ones = jnp.ones((CHUNK, CHUNK), BF16)
    cb = _dot_nt(cm, bm)
    dtp = jax.nn.softplus(dt_c + bias)
    da_c = dtp * -jnp.exp(alog)
    da_r = jax.nn.softplus(dt_r + bias_col) * -jnp.exp(alog_col)
    lower = jnp.where(causal, 1.0, 0.0).astype(BF16)
    upper = jnp.where(row <= col, 1.0, 0.0).astype(BF16)
    cs = _dot_exact01(da_c, lower, upper, False, differentiable)
    cs_rows = _dot_exact01(da_r, upper, lower, True, differentiable)
    cs_last = jnp.sum(da_c, axis=0, keepdims=True)
    ecs, decay, ecl = jnp.exp(cs), jnp.exp(cs_last - cs), jnp.exp(cs_last)
    m = [cb * jnp.exp(jnp.where(causal, cs[:, r:r + 1] - cs_rows[r:r + 1, :], -1e30)) for r in range(SSM_HPG)]
    ygs, hn = [], []
    for p in range(SSM_PAIRS):
        a, b = 2 * p, 2 * p + 1
        pair = lambda v: jnp.where(left, v[:, a:a + 1], v[:, b:b + 1])
        xdt = xp[p] * pair(dtp)
        y = jnp.where(left, _dot(m[a], xdt), _dot(m[b], xdt))
        y = y + _dot_nt(cm, hp[p]) * pair(ecs)
        y = y + xp[p] * pair(dsk)
        states = _dot_tn(xdt * pair(decay), bm)
        hn.append(hp[p] * jnp.where(top, ecl[:, a:a + 1], ecl[:, b:b + 1]) + states)
        ygs.append(y * (zp[p] * jax.nn.sigmoid(zp[p])))
    ms = sum(_dot(t * t, ones) for t in ygs) * (1.0 / SSM_GROUP_W)
    rs = lax.rsqrt(ms + EPS)
    return [ygs[p] * rs * gnp[p] for p in range(SSM_PAIRS)], hn


def _ssd_in_specs(cidx):
    gw, n = SSM_GROUP_W, SSM_STATE
    bm_blk = D_INNER // n
    return [
        pl.BlockSpec((CHUNK, gw), lambda g, c: (cidx(c), g)),
        pl.BlockSpec((CHUNK, gw), lambda g, c: (cidx(c), g)),
        pl.BlockSpec((CHUNK, n), lambda g, c: (cidx(c), bm_blk + g)),
        pl.BlockSpec((CHUNK, n), lambda g, c: (cidx(c), bm_blk + SSM_GROUPS + g)),
        pl.BlockSpec((None, CHUNK, SSM_HPG), lambda g, c: (g, cidx(c), 0)),
        pl.BlockSpec((None, SSM_HPG, CHUNK), lambda g, c: (g, 0, cidx(c))),
        pl.BlockSpec((None, 3, SSM_HPG), lambda g, c: (g, 0, 0)),
        pl.BlockSpec((None, SSM_HPG, 2), lambda g, c: (g, 0, 0)),
        pl.BlockSpec((1, gw), lambda g, c: (0, g)),
    ]


def _ssd_args(x_ref, z_ref, bm_ref, cm_ref, hp, dtc_ref, dtr_ref, prow_ref, pcol_ref, gn_ref):
    npair, w = SSM_PAIRS, 2 * SSM_HEAD_DIM
    return (_split(x_ref, npair, w), _split(z_ref, npair, w), bm_ref[...], cm_ref[...], hp, dtc_ref[...], dtr_ref[...],
            prow_ref[0:1, :], pcol_ref[:, 0:1], prow_ref[1:2, :], pcol_ref[:, 1:2], prow_ref[2:3, :],
            _split(gn_ref, npair, w))


def _pair_rows(ref):
    w = 2 * SSM_HEAD_DIM
    return [ref[p * w:(p + 1) * w, :] for p in range(SSM_PAIRS)]


def _ssd_fwd(xbc, proj, dt_c, dt_r, par_row, par_col, gn, mixcat, *, name):
    w = 2 * SSM_HEAD_DIM

    def body(x_ref, z_ref, bm_ref, cm_ref, dtc_ref, dtr_ref, prow_ref, pcol_ref, gn_ref, cat_in,
             cat_ref, hprev_ref, h_scr):
        del cat_in

        @pl.when(pl.program_id(1) == 0)
        def _():
            h_scr[...] = jnp.zeros_like(h_scr)

        hprev_ref[...] = h_scr[...]
        yn, hn = _ssd_tile(*_ssd_args(x_ref, z_ref, bm_ref, cm_ref, _pair_rows(h_scr), dtc_ref, dtr_ref, prow_ref,
                                      pcol_ref, gn_ref))
        for p in range(SSM_PAIRS):
            cat_ref[:, p * w:(p + 1) * w] = yn[p].astype(cat_ref.dtype)
            h_scr[p * w:(p + 1) * w, :] = hn[p]

    return pl.pallas_call(
        body, grid=(SSM_GROUPS, N_CHUNKS), in_specs=[*_ssd_in_specs(lambda c: c), pl.BlockSpec(memory_space=pl.ANY)],
        out_specs=[pl.BlockSpec((CHUNK, SSM_GROUP_W), lambda g, c: (c, g)),
                   pl.BlockSpec((None, None, SSM_GROUP_W, SSM_STATE), lambda g, c: (c, g, 0, 0))],
        out_shape=[SDS(mixcat.shape, mixcat.dtype), SDS((N_CHUNKS, SSM_GROUPS, SSM_GROUP_W, SSM_STATE), F32)],
        scratch_shapes=[pltpu.VMEM((SSM_GROUP_W, SSM_STATE), F32)],
        input_output_aliases={9: 0}, compiler_params=_cparams(("parallel", "arbitrary")), name=name,
    )(xbc, proj, xbc, xbc, dt_c, dt_r, par_row, par_col, gn, mixcat)


def _ssd_bwd(xbc, proj, dt_c, dt_r, par_row, par_col, gn, hprev, dcat, dproj, *, name):
    nh, w, gw, n = SSM_HPG, 2 * SSM_HEAD_DIM, SSM_GROUP_W, SSM_STATE
    rev = lambda c: N_CHUNKS - 1 - c

    def body(x_ref, z_ref, bm_ref, cm_ref, dtc_ref, dtr_ref, prow_ref, pcol_ref, gn_ref, hprev_ref, dy_ref,
             dproj_in, dz_ref, dxs_ref, dbm_ref, dcm_ref, ddtc_ref, ddtr_ref, dprow_ref, dpcol_ref, dgn_ref, dh_scr):
        del dproj_in
        first = pl.program_id(1) == 0

        @pl.when(first)
        def _():
            dh_scr[...] = jnp.zeros_like(dh_scr)
            for ref in (dprow_ref, dpcol_ref, dgn_ref):
                ref[...] = jnp.zeros_like(ref)

        args = _ssd_args(x_ref, z_ref, bm_ref, cm_ref, _pair_rows(hprev_ref), dtc_ref, dtr_ref, prow_ref, pcol_ref,
                         gn_ref)
        _, vjp = jax.vjp(lambda *a: _ssd_tile(*a, differentiable=True), *args)
        dxs, dzs, dbm, dcm, dhs, ddtc, ddtr, dbias, dbias_col, dalog, dalog_col, ddsk, dgn = vjp(
            (_split(dy_ref, SSM_PAIRS, w), _pair_rows(dh_scr)))
        dbm_ref[...] = dbm
        dcm_ref[...] = dcm
        ddtc_ref[...] = ddtc
        ddtr_ref[...] = ddtr
        for q in range(SSM_PAIRS):
            dxs_ref[:, q * w:(q + 1) * w] = dxs[q]
            dz_ref[:, q * w:(q + 1) * w] = dzs[q].astype(dz_ref.dtype)
            dh_scr[q * w:(q + 1) * w, :] = dhs[q]
            dgn_ref[:, q * w:(q + 1) * w] += dgn[q]
        for i, d in enumerate((dbias, dalog, ddsk)):
            dprow_ref[i:i + 1, :] += d
        for i, d in enumerate((dbias_col, dalog_col)):
            dpcol_ref[:, i:i + 1] += d

    return pl.pallas_call(
        body, grid=(SSM_GROUPS, N_CHUNKS),
        in_specs=[*_ssd_in_specs(rev),
                  pl.BlockSpec((None, None, gw, n), lambda g, c: (rev(c), g, 0, 0)),
                  pl.BlockSpec((CHUNK, gw), lambda g, c: (rev(c), g)),
                  pl.BlockSpec(memory_space=pl.ANY)],
        out_specs=[pl.BlockSpec((CHUNK, gw), lambda g, c: (rev(c), g)),
                   pl.BlockSpec((CHUNK, gw), lambda g, c: (rev(c), g)),
                   pl.BlockSpec((CHUNK, n), lambda g, c: (rev(c), g)),
                   pl.BlockSpec((CHUNK, n), lambda g, c: (rev(c), g)),
                   pl.BlockSpec((None, CHUNK, nh), lambda g, c: (g, rev(c), 0)),
                   pl.BlockSpec((None, nh, CHUNK), lambda g, c: (g, 0, rev(c))),
                   pl.BlockSpec((None, 3, nh), lambda g, c: (g, 0, 0)),
                   pl.BlockSpec((None, nh, 2), lambda g, c: (g, 0, 0)),
                   pl.BlockSpec((1, gw), lambda g, c: (0, g))],
        out_shape=[SDS(dproj.shape, dproj.dtype), SDS((SEQ, D_INNER), F32), SDS((SEQ, SSM_GROUPS * n), F32),
                   SDS((SEQ, SSM_GROUPS * n), F32), SDS((SSM_GROUPS, SEQ, nh), F32), SDS((SSM_GROUPS, nh, SEQ), F32),
                   SDS((SSM_GROUPS, 3, nh), F32), SDS((SSM_GROUPS, nh, 2), F32), SDS((1, D_INNER), F32)],
        scratch_shapes=[pltpu.VMEM((gw, n), F32)],
        input_output_aliases={11: 0}, compiler_params=_cparams(("parallel", "arbitrary")), name=name,
    )(xbc, proj, xbc, xbc, dt_c, dt_r, par_row, par_col, gn, hprev, dcat, dproj)


def _sum_contributions(chip, parts, landed, *, name):
    _, r, c = parts.shape
    tr = _pick(r, (256, 384, 128))

    def body(chip_ref, own_ref, landed_ref, o_ref):
        del chip_ref
        acc = own_ref[...].astype(F32)
        for s in range(landed_ref.shape[0]):
            acc = acc + landed_ref[s].astype(F32)
        o_ref[...] = acc

    grid_spec = pltpu.PrefetchScalarGridSpec(
        num_scalar_prefetch=1, grid=(r // tr,),
        in_specs=[pl.BlockSpec((None, tr, c), lambda i, chip_ref: (chip_ref[0], i, 0)),
                  pl.BlockSpec((landed.shape[0], tr, c), lambda i, chip_ref: (0, i, 0))],
        out_specs=pl.BlockSpec((tr, c), lambda i, chip_ref: (i, 0)))
    return pl.pallas_call(body, grid_spec=grid_spec, out_shape=SDS((r, c), F32),
                          compiler_params=_cparams(("parallel",)), name=name)(chip, parts, landed)


def _adamw(w, g, m, v, *, name):
    layers, r, c = w.shape
    if r <= 256 or r % 128 == 0:
        tr = min(r, 256)
        steps, spec = r // tr, pl.BlockSpec((None, tr, c), lambda l, i: (l, i, 0))
    else:
        tc = _pick(c, (256, 128))
        steps, spec = c // tc, pl.BlockSpec((None, r, tc), lambda l, i: (l, 0, i))

    def body(w_ref, g_ref, m_ref, v_ref, d_ref, mo_ref, vo_ref):
        g = g_ref[...]
        m_new = ADAM_B1 * m_ref[...] + (1.0 - ADAM_B1) * g
        v_new = ADAM_B2 * v_ref[...] + (1.0 - ADAM_B2) * (g * g)
        m_hat = m_new / (1.0 - ADAM_B1 ** ADAM_STEP)
        v_hat = v_new / (1.0 - ADAM_B2 ** ADAM_STEP)
        d_ref[...] = -ADAM_LR * (m_hat / (jnp.sqrt(v_hat) + ADAM_EPS) + ADAM_WD * w_ref[...])
        mo_ref[...] = m_new
        vo_ref[...] = v_new

    return pl.pallas_call(body, grid=(layers, steps), in_specs=[spec] * 4, out_specs=[spec] * 3,
                          out_shape=[SDS(w.shape, F32)] * 3, compiler_params=_cparams(("parallel", "parallel")),
                          name=name)(w, g, m, v)


ANY = pl.BlockSpec(memory_space=pl.ANY)


def _place():
    x, y, c = lax.axis_index("x"), lax.axis_index("y"), lax.axis_index("c")
    chips = [(1 - x, y), (x, 1 - y), (1 - x, 1 - y)]
    return x, y, c, chips


def _remote(src, dst, send_sem, recv_sem, to):
    return pltpu.make_async_remote_copy(src_ref=src, dst_ref=dst, send_sem=send_sem, recv_sem=recv_sem,
                                        device_id=to, device_id_type=MESH)


STREAM_ROWS = 256


def _stream_rows(i):
    return pl.ds(pl.multiple_of(i * STREAM_ROWS, STREAM_ROWS), STREAM_ROWS)


def _channel_scratch(width, dtype, rows=STREAM_ROWS):
    buf = (2, rows, width)
    return [pltpu.VMEM(buf, dtype), pltpu.VMEM(buf, dtype), *([pltpu.SemaphoreType.DMA((2,))] * 5),
            pltpu.SemaphoreType.REGULAR((2,))]


CHANNEL_REFS = 8


def _copy_blocks(srcs, dsts, ch):
    sbuf, _, ld, _, _, st, _, _ = ch
    n = len(srcs)
    load = lambda i: pltpu.make_async_copy(srcs[i], sbuf.at[i % 2], ld.at[i % 2])
    store = lambda i: pltpu.make_async_copy(sbuf.at[i % 2], dsts[i], st.at[i % 2])
    load(0).start()
    for i in range(n):
        if i + 1 < n:
            if i >= 1:
                store(i - 1).wait()
            load(i + 1).start()
        load(i).wait()
        store(i).start()
    for i in range(max(0, n - 2), n):
        store(i).wait()


def _exchange_block_streams(streams, sibling):
    plans = []
    for srcs, dsts, keeps, (sbuf, rbuf, ld, snd, rcv, st, kp, credit) in streams:
        n = len(srcs)

        def load(i, srcs=srcs, sbuf=sbuf, ld=ld):
            return pltpu.make_async_copy(srcs[i], sbuf.at[i % 2], ld.at[i % 2])

        def push(i, sbuf=sbuf, rbuf=rbuf, snd=snd, rcv=rcv):
            return _remote(sbuf.at[i % 2], rbuf.at[i % 2], snd.at[i % 2], rcv.at[i % 2], sibling)

        def store(i, rbuf=rbuf, dsts=dsts, st=st):
            return pltpu.make_async_copy(rbuf.at[i % 2], dsts[i], st.at[i % 2])

        def save(i, sbuf=sbuf, keeps=keeps, kp=kp):
            return pltpu.make_async_copy(sbuf.at[i % 2], keeps[i], kp.at[i % 2])

        def free_slot(i, n=n, store=store, credit=credit):
            if 1 <= i < n:
                store(i - 1).wait()
                if i + 1 < n:
                    pl.semaphore_signal(credit.at[(i + 1) % 2], 1, device_id=sibling, device_id_type=MESH)

        def send(i, n=n, load=load, push=push, save=save, keeps=keeps, credit=credit):
            if i < n:
                load(i).wait()
                pl.semaphore_wait(credit.at[i % 2], 1)
                push(i).start()
                if keeps[i] is not None:
                    save(i).start()

        def receive(i, n=n, load=load, push=push, store=store, save=save, keeps=keeps):
            if i < n:
                push(i).wait_recv()
                store(i).start()
                push(i).wait_send()
                if keeps[i] is not None:
                    save(i).wait()
                if i + 2 < n:
                    load(i + 2).start()

        for i in range(min(2, n)):
            pl.semaphore_signal(credit.at[i], 1, device_id=sibling, device_id_type=MESH)
            load(i).start()
        plans.append((n, free_slot, send, receive, store))
    for _, _, send, _, _ in plans:
        send(0)
    for i in range(max(p[0] for p in plans)):
        for _, free_slot, _, _, _ in plans:
            free_slot(i)
        for _, _, send, _, _ in plans:
            send(i + 1)
        for _, _, _, receive, _ in plans:
            receive(i)
    for n, _, _, _, store in plans:
        store(n - 1).wait()


def _all_gather_shards(shards, small, *, name):
    n = len(shards)

    def body(*refs):
        ins, outs = refs[:n + 1], refs[n + 1:2 * n + 2]
        scr = refs[2 * n + 2:]
        chans = [scr[CHANNEL_REFS * t:CHANNEL_REFS * (t + 1)] for t in range(n)]
        send_sems, recv_sems, small_sems = scr[CHANNEL_REFS * n:]
        x, y, c, _ = _place()
        me = 2 * x + y
        sibling = (x, y, 1 - c)
        near = (lax.rem(x + 1 - c, 2), lax.rem(y + c, 2))
        far = (lax.rem(x + c, 2), lax.rem(y + 1 - c, 2))
        k_near, k_far, k_diag = 2 * near[0] + near[1], 2 * far[0] + far[1], 3 - me
        targets = ((*near, c), (*far, c), (*far, c))
        arrives = (k_near, k_far, k_diag)
        streams_in = (k_far, k_near, k_diag)

        def ici(t, j, src, blk):
            return _remote(src, outs[t].at[blk, c], send_sems.at[3 * t + j], recv_sems.at[3 * t + j], targets[j])

        first = [ici(t, j, ins[t].at[c], me) for t in range(n + 1) for j in range(2)]
        for cp in first:
            cp.start()
        small_local = pltpu.make_async_copy(ins[n], outs[n].at[me], small_sems.at[6])
        small_local.start()
        for t in range(n):
            _copy_blocks([ins[t].at[h] for h in range(2)], [outs[t].at[me, h] for h in range(2)], chans[t])
        passed = []
        for j in range(3):
            for t in range(n + 1):
                landed = outs[t].at[arrives[j], c]
                ici(t, j, landed, arrives[j]).wait_recv()
                if j == 0:
                    fwd = ici(t, 2, landed, k_near)
                    fwd.start()
                    passed.append(fwd)
                if t < n:
                    _exchange_block_streams([([landed], [outs[t].at[streams_in[j], 1 - c]], [None], chans[t])], sibling)
                else:
                    fwd = _remote(landed, landed, small_sems.at[j], small_sems.at[3 + j], sibling)
                    fwd.start()
                    passed.append(fwd)
        for j in range(3):
            got = outs[n].at[streams_in[j], 1 - c]
            _remote(got, got, small_sems.at[j], small_sems.at[3 + j], sibling).wait_recv()
        for cp in first + passed:
            cp.wait_send()
        small_local.wait()

    scratch = []
    for s in shards:
        scratch += _channel_scratch(s.shape[2], s.dtype, rows=s.shape[1])
    return pl.pallas_call(
        body, in_specs=[ANY] * (n + 1), out_specs=[ANY] * (n + 1),
        out_shape=[SDS((N_CHIPS, *s.shape), s.dtype) for s in (*shards, small)],
        scratch_shapes=[*scratch, pltpu.SemaphoreType.DMA((3 * n + 3,)), pltpu.SemaphoreType.DMA((3 * n + 3,)),
                        pltpu.SemaphoreType.DMA((7,))],
        compiler_params=pltpu.CompilerParams(vmem_limit_bytes=VMEM_LIMIT), name=name)(*shards, small)


def _pair_reduce(stacks, *, name):
    n = len(stacks)
    per = 11

    def body(*refs):
        ins, outs, scr = refs[:n], refs[n:2 * n], refs[2 * n:]
        x, y, c, _ = _place()
        sibling = (x, y, 1 - c)
        streams = []
        for t in range(n):
            sraw, sbuf, rbuf, obuf, pbuf, ld_s, ld_o, snd, rcv, st, credit = scr[per * t:per * (t + 1)]
            steps = ins[t].shape[1] // STREAM_ROWS
            src, own, out = ins[t].at[1 - c], ins[t].at[c], outs[t]
            assert steps >= 2

            def load_s(i, slot, src=src, sraw=sraw, ld_s=ld_s):
                return pltpu.make_async_copy(src.at[_stream_rows(i)], sraw.at[slot], ld_s.at[slot])

            def load_o(i, slot, own=own, obuf=obuf, ld_o=ld_o):
                return pltpu.make_async_copy(own.at[_stream_rows(i)], obuf.at[slot], ld_o.at[slot])

            def push(slot, sbuf=sbuf, rbuf=rbuf, snd=snd, rcv=rcv):
                return _remote(sbuf.at[slot], rbuf.at[slot], snd.at[slot], rcv.at[slot], sibling)

            def store(i, slot, pbuf=pbuf, out=out, st=st):
                return pltpu.make_async_copy(pbuf.at[slot], out.at[_stream_rows(i)], st.at[slot])

            def send(i, slot, load_s=load_s, push=push, sraw=sraw, sbuf=sbuf, credit=credit):
                load_s(i, slot).wait()
                sbuf[slot] = sraw[slot].astype(sbuf.dtype)
                pl.semaphore_wait(credit.at[slot], 1)
                push(slot).start()

            def combine(i, slot, load_s=load_s, load_o=load_o, push=push, store=store, rbuf=rbuf, obuf=obuf, pbuf=pbuf,
                        credit=credit, steps=steps):
                load_o(i, slot).wait()
                push(slot).wait_recv()

                @pl.when(i >= 2)
                def _():
                    store(i, slot).wait()

                pbuf[slot] = (obuf[slot] + rbuf[slot].astype(F32)).astype(pbuf.dtype)
                store(i, slot).start()
                push(slot).wait_send()

                @pl.when(i + 2 < steps)
                def _():
                    load_s(i + 2, slot).start()
                    load_o(i + 2, slot).start()
                    pl.semaphore_signal(credit.at[slot], 1, device_id=sibling, device_id_type=MESH)

            for slot in range(2):
                pl.semaphore_signal(credit.at[slot], 1, device_id=sibling, device_id_type=MESH)
                load_s(slot, slot).start()
                load_o(slot, slot).start()
            streams.append((steps, send, combine, store))
        for _, send, _, _ in streams:
            send(0, 0)

        def step(i, carry):
            slot = lax.rem(i, 2)
            for steps, send, _, _ in streams:
                @pl.when(i + 1 < steps)
                def _(send=send):
                    send(i + 1, 1 - slot)
            for steps, _, combine, _ in streams:
                @pl.when(i < steps)
                def _(combine=combine):
                    combine(i, slot)
            return carry

        lax.fori_loop(0, max(s[0] for s in streams), step, 0)
        for _, _, _, store in streams:
            for slot in range(2):
                store(0, slot).wait()

    scratch = []
    for s in stacks:
        buf = (2, STREAM_ROWS, s.shape[2])
        scratch += [pltpu.VMEM(buf, F32), pltpu.VMEM(buf, BF16), pltpu.VMEM(buf, BF16), pltpu.VMEM(buf, F32),
                    pltpu.VMEM(buf, BF16), *([pltpu.SemaphoreType.DMA((2,))] * 5), pltpu.SemaphoreType.REGULAR((2,))]
    return pl.pallas_call(
        body, in_specs=[ANY] * n, out_specs=[ANY] * n, out_shape=[SDS(s.shape[1:], BF16) for s in stacks],
        scratch_shapes=scratch, compiler_params=pltpu.CompilerParams(vmem_limit_bytes=VMEM_LIMIT), name=name)(*stacks)


HBM_SPEC = pl.BlockSpec(memory_space=pltpu.HBM)
SEM_SPEC = pl.BlockSpec(memory_space=pltpu.SEMAPHORE)
SIDE_EFFECT = pltpu.SideEffectType.DATAFLOW_SIDE_EFFECTING


def _scatter_copies(ins, lands, send_sems, recv_sems):
    _, _, c, chips = _place()
    return [_remote(ins[t].at[2 * cx + cy], lands[t].at[j], send_sems.at[3 * t + j], recv_sems.at[3 * t + j],
                    (cx, cy, c)) for t in range(len(ins)) for j, (cx, cy) in enumerate(chips)]


def _chip_scatter_start(parts, *, name):
    n = len(parts)

    def body(*refs):
        ins, lands = refs[:n], refs[n:2 * n]
        send_sems, recv_sems, token = refs[2 * n], refs[2 * n + 1], refs[-1]
        for cp in _scatter_copies(ins, lands, send_sems, recv_sems):
            cp.start()
        token[...] = jnp.zeros_like(token)

    hbm = lambda a: pltpu.with_memory_space_constraint(a, pltpu.HBM)
    lands = [hbm(lax.empty((3, *p.shape[1:]), p.dtype)) for p in parts]
    thru = [pltpu.HBM(a.shape, a.dtype) for a in (*parts, *lands)]
    outs = pl.pallas_call(
        body, name=name,
        out_shape=(pltpu.SemaphoreType.DMA((3 * n,)), pltpu.SemaphoreType.DMA((3 * n,)), *thru, SDS((8, 128), F32)),
        in_specs=[HBM_SPEC] * (2 * n),
        out_specs=(SEM_SPEC, SEM_SPEC, *([HBM_SPEC] * (2 * n)), pl.BlockSpec(memory_space=pltpu.VMEM)),
        input_output_aliases={i: 2 + i for i in range(2 * n)},
        compiler_params=pltpu.CompilerParams(has_side_effects=SIDE_EFFECT),
    )(*[hbm(p) for p in parts], *lands)
    return outs[0], outs[1], outs[2:2 + n], outs[2 + n:2 + 2 * n], outs[-1]


def _chip_scatter_wait(send_sems, recv_sems, parts, lands, after, *, name):
    n = len(parts)

    def body(*refs):
        ins, lands_in = refs[:n], refs[n:2 * n]
        for cp in _scatter_copies(ins, lands_in, refs[2 * n], refs[2 * n + 1]):
            cp.wait_send()
            cp.wait_recv()

    outs = pl.pallas_call(
        body, name=name, out_shape=[pltpu.HBM(a.shape, a.dtype) for a in (*parts, *lands)],
        in_specs=[*([HBM_SPEC] * (2 * n)), SEM_SPEC, SEM_SPEC, *([ANY] * len(after))],
        out_specs=[HBM_SPEC] * (2 * n), input_output_aliases={i: i for i in range(2 * n)},
        compiler_params=pltpu.CompilerParams(has_side_effects=SIDE_EFFECT),
    )(*parts, *lands, send_sems, recv_sems, *after)
    return outs[:n], outs[n:]


def _gather_copies(shards, zones, send_sems, recv_sems):
    x, y, c, chips = _place()
    return [_remote(shards[t].at[c], zones[t].at[2 * x + y, c], send_sems.at[3 * t + j], recv_sems.at[3 * t + j],
                    (cx, cy, c)) for t in range(len(shards)) for j, (cx, cy) in enumerate(chips)]


def _gather_start(shards, after, *, name):
    n = len(shards)

    def body(*refs):
        ins, zones = refs[:n], refs[n:2 * n]
        send_sems, recv_sems, token = refs[2 * n + len(after)], refs[2 * n + len(after) + 1], refs[-1]
        for cp in _gather_copies(ins, zones, send_sems, recv_sems):
            cp.start()
        token[...] = jnp.zeros_like(token)

    hbm = lambda a: pltpu.with_memory_space_constraint(a, pltpu.HBM)
    zones = [hbm(lax.empty((N_CHIPS, *s.shape), s.dtype)) for s in shards]
    thru = [pltpu.HBM(a.shape, a.dtype) for a in (*shards, *zones)]
    outs = pl.pallas_call(
        body, name=name,
        out_shape=(pltpu.SemaphoreType.DMA((3 * n,)), pltpu.SemaphoreType.DMA((3 * n,)), *thru, SDS((8, 128), F32)),
        in_specs=[*([HBM_SPEC] * (2 * n)), *([ANY] * len(after))],
        out_specs=(SEM_SPEC, SEM_SPEC, *([HBM_SPEC] * (2 * n)), pl.BlockSpec(memory_space=pltpu.VMEM)),
        input_output_aliases={i: 2 + i for i in range(2 * n)},
        compiler_params=pltpu.CompilerParams(has_side_effects=SIDE_EFFECT),
    )(*[hbm(s) for s in shards], *zones, *after)
    return outs[0], outs[1], outs[2:2 + n], outs[2 + n:2 + 2 * n], outs[-1]


def _gather_wait(send_sems, recv_sems, shards, zones, after, *, name):
    n = len(shards)

    def body(*refs):
        for cp in _gather_copies(refs[:n], refs[n:2 * n], refs[2 * n], refs[2 * n + 1]):
            cp.wait_send()
            cp.wait_recv()

    outs = pl.pallas_call(
        body, name=name, out_shape=[pltpu.HBM(a.shape, a.dtype) for a in (*shards, *zones)],
        in_specs=[*([HBM_SPEC] * (2 * n)), SEM_SPEC, SEM_SPEC, *([ANY] * len(after))],
        out_specs=[HBM_SPEC] * (2 * n), input_output_aliases={i: i for i in range(2 * n)},
        compiler_params=pltpu.CompilerParams(has_side_effects=SIDE_EFFECT),
    )(*shards, *zones, send_sems, recv_sems, *after)
    return outs[:n], outs[n:]


def _gather_finish(shards, zones, *, name):
    n = len(shards)

    def body(*refs):
        ins, zones_in, outs, scr = refs[:n], refs[n:2 * n], refs[2 * n:3 * n], refs[3 * n:]
        x, y, c, chips = _place()
        me = 2 * x + y
        sibling = (x, y, 1 - c)
        others = [2 * cx + cy for cx, cy in chips]
        chans = [scr[CHANNEL_REFS * t:CHANNEL_REFS * (t + 1)] for t in range(n)]
        for t in range(n):
            _copy_blocks([ins[t].at[h] for h in range(2)], [outs[t].at[me, h] for h in range(2)], chans[t])
        _exchange_block_streams([([zones_in[t].at[k, c] for k in others], [outs[t].at[k, 1 - c] for k in others],
                                  [None] * len(others), chans[t]) for t in range(n)], sibling)

    scratch = []
    for s in shards:
        scratch += _channel_scratch(s.shape[2], s.dtype, rows=s.shape[1])
    return pl.pallas_call(
        body, in_specs=[ANY] * (2 * n), out_specs=[ANY] * n, out_shape=[SDS(z.shape, z.dtype) for z in zones],
        input_output_aliases={n + t: t for t in range(n)}, scratch_shapes=scratch,
        compiler_params=pltpu.CompilerParams(vmem_limit_bytes=VMEM_LIMIT), name=name)(*shards, *zones)


def _pair_share(groups, *, name):
    finals = [f for grp in groups for f in grp]
    n, n_out = len(finals), len(groups)

    def body(*refs):
        ins, outs, scr = refs[:n], refs[n:n + n_out], refs[n + n_out:]
        x, y, c, _ = _place()
        sibling = (x, y, 1 - c)
        t, streams = 0, []
        for o, grp in enumerate(groups):
            rows = grp[0].shape[0] // 2
            blocks = [(layer, pl.ds(b * rows, rows)) for layer in range(len(grp)) for b in range(2)]
            streams.append(([ins[t + layer].at[rs] for layer, rs in blocks],
                            [outs[o].at[layer, 1 - c, rs] for layer, rs in blocks],
                            [outs[o].at[layer, c, rs] for layer, rs in blocks],
                            scr[CHANNEL_REFS * o:CHANNEL_REFS * (o + 1)]))
            t += len(grp)
        _exchange_block_streams(streams, sibling)

    scratch = []
    for grp in groups:
        scratch += _channel_scratch(grp[0].shape[1], grp[0].dtype, rows=grp[0].shape[0] // 2)
    return pl.pallas_call(
        body, in_specs=[ANY] * n, out_specs=[ANY] * n_out,
        out_shape=[SDS((len(grp), 2, *grp[0].shape), grp[0].dtype) for grp in groups],
        scratch_shapes=scratch, compiler_params=pltpu.CompilerParams(vmem_limit_bytes=VMEM_LIMIT), name=name)(*finals)


def _all_reduce_small(v, *, name):
    rows, lanes = v.shape
    n_dev = 8

    def body(v_ref, o_ref, all_ref, send_sems, recv_sems, local_sem):
        x, y, c, chips = _place()
        me, sibling = (x, y, c), (x, y, 1 - c)

        def block(px, py, pc):
            return all_ref.at[4 * px + 2 * py + pc]

        def copy(k, blk, to, src=None):
            return _remote(block(*blk) if src is None else src, block(*blk), send_sems.at[k], recv_sems.at[k], to)

        mine = pltpu.make_async_copy(v_ref, block(*me), local_sem)
        mine.start()
        first = [copy(0, me, sibling, src=v_ref)]
        first += [copy(1 + j, me, (*chip, c), src=v_ref) for j, chip in enumerate(chips)]
        for cp in first:
            cp.start()
        passed = [copy(4 + j, (*chip, c), sibling) for j, chip in enumerate(chips)]
        for j, chip in enumerate(chips):
            copy(1 + j, (*chip, c), me).wait_recv()
            passed[j].start()
        copy(0, sibling, me).wait_recv()
        for j, chip in enumerate(chips):
            copy(4 + j, (*chip, 1 - c), me).wait_recv()
        for cp in first + passed:
            cp.wait_send()
        mine.wait()
        acc = all_ref[0]
        for k in range(1, n_dev):
            acc = acc + all_ref[k]
        o_ref[...] = acc

    vmem = pl.BlockSpec(memory_space=pltpu.VMEM)
    return pl.pallas_call(
        body, in_specs=[vmem], out_specs=vmem, out_shape=SDS((rows, lanes), F32),
        scratch_shapes=[pltpu.VMEM((n_dev, rows, lanes), F32), pltpu.SemaphoreType.DMA((7,)),
                        pltpu.SemaphoreType.DMA((7,)), pltpu.SemaphoreType.DMA],
        compiler_params=pltpu.CompilerParams(vmem_limit_bytes=VMEM_LIMIT), name=name)(v)


def _relu2_epilogue(acc):
    return acc, jnp.square(jnp.maximum(acc, 0.0))


def _res_epilogue(acc, res):
    return (acc + res,)


def _drelu2_epilogue(acc, pre):
    return (acc * (2.0 * jnp.maximum(pre.astype(F32), 0.0)),)


def _ffn_fwd(h, g, w1, w2, tag):
    f = _rms_fwd(h, g, name=f"ffn_norm_{tag}")
    pre, act = _mm_nn(f, w1, name=f"ffn1_{tag}", epilogue=_relu2_epilogue, n_out_dtypes=(BF16, BF16))
    h_out = _mm_nn(act, w2, name=f"ffn2_{tag}", extras=(h,), epilogue=_res_epilogue)
    return h_out, (f, pre, act)


def _ffn_bwd(dh, h, g, w1, w2, saved, layer, after=()):
    f, pre, act = saved
    dpre = _mm_nt(dh, w2, name=f"ffn2_dx_{layer}", out_dtype=BF16, extras=(pre,), epilogue=_drelu2_epilogue,
                  after=after)
    dw2 = _mm_tn_stacked(act, dh, name=f"ffn2_dw_{layer}", col_slots=False)
    df = _mm_nt(dpre, w1, name=f"ffn1_dx_{layer}")
    dw1 = _mm_tn_stacked(f, dpre, name=f"ffn1_dw_{layer}", col_slots=True)
    dh, dg = _rms_bwd(h, g, df, dh, name=f"ffn_norm_bwd_{layer}")
    return dh, dg, dw1, dw2


def _kv_fwd(mem, g, w_kv, tag):
    m = _rms_fwd(mem, g, name=f"mem_norm_{tag}")
    return m, _mm_nn(m, w_kv, name=f"kv_{tag}")


def _kv_bwd(mem, g, w_kv, m, dk, dv, layer):
    dkv = jnp.concatenate([dk, dv], axis=1)
    dw = _mm_tn_stacked(m, dkv, name=f"kv_dw_{layer}", col_slots=True)
    dm = _mm_nt(dkv, w_kv, name=f"kv_dx_{layer}")
    _, dg = _rms_bwd(mem, g, dm, dm, name=f"mem_norm_bwd_{layer}")
    return dw, dg


def _local_step(x, mem, target, p, after_layer1=None, after_ffn0=None, after_mixer0=None):
    row = lambda v: v.reshape(1, -1)
    g = {}

    h0 = x
    a0 = _rms_fwd(h0, row(p["norm_mix"][0]), name="mix_norm_0")
    proj_a = _mm_nn(a0, p["a_in"], name="a_in", after=p.get("after_start", ()))
    m0, kv0 = _kv_fwd(mem, row(p["mem_norm"][0]), p["w_kv"][0], "0")
    cat0 = _attn_fwd(proj_a, 2 * D_INNER, kv0, name="attn_0")
    bs_col = p["a_bs"].reshape(A_GROUPS, CHUNK, 1)
    cat0 = _gate_fwd(proj_a, p["a_ln_g"], p["a_ln_b"], p["a_ws"], bs_col, cat0, name="gate")
    h1 = _mm_nn(cat0, p["w_out"][0], name="out_0", extras=(h0,), epilogue=_res_epilogue)
    h2, ffn0 = _ffn_fwd(h1, row(p["norm_ffn"][0]), p["w_ffn1"][0], p["w_ffn2"][0], "0")

    if "layer1_mixer" in p:
        w_kv1, w_out1, b_in = p["layer1_mixer"](h2)
    else:
        w_kv1, w_out1, b_in = p["w_kv"][1], p["w_out"][1], p["b_in"]
    a1 = _rms_fwd(h2, row(p["norm_mix"][1]), name="mix_norm_1")
    proj_b = _mm_nn(a1, b_in, name="b_in")
    m1, kv1 = _kv_fwd(mem, row(p["mem_norm"][1]), w_kv1, "1")
    cat1 = _attn_fwd(proj_b, B_Q_OFF, kv1, name="attn_1")
    xbc = _conv_fwd(proj_b, p["b_conv_w"], p["b_conv_b"], name="conv")
    dt_raw = proj_b[:, B_DT_OFF:B_DT_OFF + SSM_HEADS].reshape(SEQ, SSM_GROUPS, SSM_HPG)
    dt_c = jnp.transpose(dt_raw, (1, 0, 2))
    dt_r = jnp.transpose(dt_raw, (1, 2, 0))
    per_head = lambda v: v.reshape(SSM_GROUPS, 1, SSM_HPG)
    par_row = jnp.concatenate([per_head(p["b_dt_bias"]), per_head(p["b_a_log"]), per_head(p["b_d"])], axis=1)
    ssd_par = (par_row, jnp.transpose(par_row[:, :2], (0, 2, 1)), p["b_gnorm"])
    cat1, hprev = _ssd_fwd(xbc, proj_b, dt_c, dt_r, *ssd_par, cat1, name="ssd")
    h3 = _mm_nn(cat1, w_out1, name="out_1", extras=(h2,), epilogue=_res_epilogue)
    w_ffn1_1, w_ffn2_1 = p["layer1_ffn"](h3) if "layer1_ffn" in p else (p["w_ffn1"][1], p["w_ffn2"][1])
    h4, ffn1 = _ffn_fwd(h3, row(p["norm_ffn"][1]), w_ffn1_1, w_ffn2_1, "1")

    loss, dh, g["final_norm"] = _loss_head(h4, row(p["final_norm"]), target, name="loss_head")

    dh, dnf1, dw1_1, dw2_1 = _ffn_bwd(dh, h3, row(p["norm_ffn"][1]), w_ffn1_1, w_ffn2_1, ffn1, 1)
    dcat1 = _mm_nt(dh, w_out1, name="out_dx_1")
    dwo_1 = _mm_tn_stacked(cat1, dh, name="out_dw_1", col_slots=False)
    dproj_b, dk1, dv1 = _attn_bwd(proj_b, B_Q_OFF, kv1, dcat1, B_IN_PAD, B_Q_OFF, name="attn_bwd_1")
    dproj_b, dxs, dbm, dcm, ddt_c, ddt_r, dpar_row, dpar_col, g["b_gnorm"] = _ssd_bwd(
        xbc, proj_b, dt_c, dt_r, *ssd_par, hprev, dcat1, dproj_b, name="ssd_bwd")
    dpar = dpar_row.at[:, :2].add(jnp.transpose(dpar_col, (0, 2, 1)))
    g["b_dt_bias"], g["b_a_log"], g["b_d"] = dpar[:, 0], dpar[:, 1], dpar[:, 2]
    dproj_b, g["b_conv_w"], g["b_conv_b"] = _conv_bwd(proj_b, p["b_conv_w"], p["b_conv_b"], dxs, dbm, dcm, dproj_b,
                                                      name="conv_bwd")
    ddt = jnp.transpose(ddt_c, (1, 0, 2)) + jnp.transpose(ddt_r, (2, 0, 1))
    ddt = jnp.pad(ddt.reshape(SEQ, SSM_HEADS), ((0, 0), (0, B_IN_PAD - B_DT_OFF - SSM_HEADS))).astype(BF16)
    dproj_b = lax.dynamic_update_slice(dproj_b, ddt, (0, B_DT_OFF))
    dwkv_1, dmn1 = _kv_bwd(mem, row(p["mem_norm"][1]), w_kv1, m1, dk1, dv1, 1)
    dwb = _b_in_grad_slots(_mm_tn(a1, dproj_b, name="b_in_dw"))
    da1 = _mm_nt(dproj_b, b_in, name="b_in_dx")
    dh, dnm1 = _rms_bwd(h2, row(p["norm_mix"][1]), da1, dh, name="mix_norm_bwd_1")
    layer1 = dict(w_kv=dwkv_1, w_out=dwo_1, w_ffn1=dw1_1, w_ffn2=dw2_1, b_in=dwb)
    token = () if after_layer1 is None else (after_layer1(layer1),)

    dh, dnf0, dw1_0, dw2_0 = _ffn_bwd(dh, h1, row(p["norm_ffn"][0]), p["w_ffn1"][0], p["w_ffn2"][0], ffn0, 0,
                                      after=token)
    ffn0_grads = dict(w_ffn1=dw1_0, w_ffn2=dw2_0)
    token = () if after_ffn0 is None else (after_ffn0(ffn0_grads),)
    dcat0 = _mm_nt(dh, p["w_out"][0], name="out_dx_0", after=token)
    dwo_0 = _mm_tn_stacked(cat0, dh, name="out_dw_0", col_slots=False)
    dproj_a, dk0, dv0 = _attn_bwd(proj_a, 2 * D_INNER, kv0, dcat0, A_IN, 2 * D_INNER, name="attn_bwd_0")
    dproj_a, g["a_ln_g"], g["a_ln_b"], g["a_ws"], dbs_col = _gate_bwd(
        proj_a, p["a_ln_g"], p["a_ln_b"], p["a_ws"], bs_col, dcat0, dproj_a, name="gate_bwd")
    g["a_bs"] = dbs_col.reshape(A_GROUPS, CHUNK)
    dwkv_0, dmn0 = _kv_bwd(mem, row(p["mem_norm"][0]), p["w_kv"][0], m0, dk0, dv0, 0)
    dwa = _mm_tn_stacked(a0, dproj_a, name="a_in_dw", col_slots=True)
    mixer0_grads = dict(w_kv=dwkv_0, w_out=dwo_0, a_in=dwa)
    token = () if after_mixer0 is None else (after_mixer0(mixer0_grads),)
    da0 = _mm_nt(dproj_a, p["a_in"], name="a_in_dx", after=token)
    dx, dnm0 = _rms_bwd(h0, row(p["norm_mix"][0]), da0, dh, name="mix_norm_bwd_0")

    g["norm_mix"] = jnp.concatenate([dnm0, dnm1], axis=0)
    g["norm_ffn"] = jnp.concatenate([dnf0, dnf1], axis=0)
    g["mem_norm"] = jnp.concatenate([dmn0, dmn1], axis=0)
    layer0 = dict(w_kv=dwkv_0, w_out=dwo_0, w_ffn1=dw1_0, w_ffn2=dw2_0, a_in=dwa)
    return loss, dx, g, layer0, layer1


def _b_in_full(gathered):
    n = B_IN // N_CHIPS
    dt0 = D_INNER + CONV_DIM - (N_CHIPS - 1) * n
    last = gathered[N_CHIPS - 1]
    return jnp.concatenate([*[gathered[k] for k in range(N_CHIPS - 1)], last[:, :dt0], last[:, dt0 + SSM_HEADS:],
                            last[:, dt0:dt0 + SSM_HEADS], jnp.zeros((D_MODEL, B_IN_PAD - B_IN), last.dtype)], axis=1)


def _b_in_grad_slots(d):
    n = B_IN // N_CHIPS
    dt0 = D_INNER + CONV_DIM
    last = jnp.concatenate([d[:, (N_CHIPS - 1) * n:dt0], d[:, B_DT_OFF:B_DT_OFF + SSM_HEADS], d[:, dt0:B_DT_OFF]], axis=1)
    slots = [*[d[:, k * n:(k + 1) * n] for k in range(N_CHIPS - 1)], last]
    half = D_MODEL // 2
    return jnp.stack([jnp.stack([s[h * half:(h + 1) * half] for s in slots]) for h in range(2)])


LARGE = ("w_kv", "w_out", "w_ffn1", "w_ffn2", "a_in", "b_in")
SMALL_REPL = ("norm_mix", "norm_ffn", "mem_norm", "a_ln_g", "a_ln_b", "a_ws", "a_bs", "b_dt_bias", "b_a_log", "b_d",
              "final_norm")
SMALL_SHARD = ("b_conv_w", "b_conv_b", "b_gnorm")
WEIGHTS = ("norm_mix", "norm_ffn", "mem_norm", "w_kv", "w_out", "w_ffn1", "w_ffn2", "a_in", "a_ln_g", "a_ln_b", "a_ws",
           "a_bs", "b_in", "b_conv_w", "b_conv_b", "b_dt_bias", "b_a_log", "b_d", "b_gnorm", "final_norm")
CONV_SHARD = CONV_DIM // N_CHIPS
GN_SHARD = D_INNER // N_CHIPS


LAYERED = ("w_kv", "w_out", "w_ffn1", "w_ffn2")
LAYER_TENSORS = (("w_kv", "w_out", "w_ffn1", "w_ffn2", "a_in"), ("w_kv", "w_out", "w_ffn1", "w_ffn2", "b_in"))


def _gather_weights(w):
    halves = lambda k, layer: (w[k][layer] if k in LAYERED else w[k][0]).reshape(2, -1, w[k].shape[-1]).astype(BF16)
    small = jnp.zeros((2, CONV_K, CONV_SHARD), F32)
    small = small.at[0].set(w["b_conv_w"][0])
    small = small.at[1, 0].set(w["b_conv_b"][0])
    small = small.at[1, 1, :GN_SHARD].set(w["b_gnorm"][0])
    gathered = _all_gather_shards([halves(k, 0) for k in LAYER_TENSORS[0]], small, name="gather_weights_0")
    got = dict(zip(LAYER_TENSORS[0], gathered))
    slots = lambda a: a.reshape(N_CHIPS, -1, a.shape[-1])
    rows = lambda a: a.reshape(-1, a.shape[-1])
    p = dict(w_kv=[slots(got["w_kv"])], w_out=[rows(got["w_out"])], w_ffn1=[slots(got["w_ffn1"])],
             w_ffn2=[rows(got["w_ffn2"])], a_in=slots(got["a_in"]))
    sm = gathered[-1]
    p["b_conv_w"] = jnp.transpose(sm[:, 0], (1, 0, 2)).reshape(CONV_K, CONV_DIM)
    p["b_conv_b"] = sm[:, 1, 0].reshape(1, CONV_DIM)
    p["b_gnorm"] = sm[:, 1, 1, :GN_SHARD].reshape(1, D_INNER)

    after, started = (gathered[0],), {}
    for tag, names in (("mixer", ("w_kv", "w_out", "b_in")), ("ffn", ("w_ffn1", "w_ffn2"))):
        started[tag] = _gather_start([halves(k, 1) for k in names], after, name=f"gather_start_1_{tag}")
        after = (started[tag][-1],)
    p["after_start"] = after

    def finish(tag, first):
        send_sems, recv_sems, shards, zones, _ = started[tag]
        shards, zones = _gather_wait(send_sems, recv_sems, shards, zones, (first,), name=f"gather_wait_1_{tag}")
        return _gather_finish(shards, zones, name=f"gather_finish_1_{tag}")

    def layer1_mixer(first):
        kv, wo, b_in = finish("mixer", first)
        return slots(kv), rows(wo), _b_in_full(slots(b_in))

    def layer1_ffn(first):
        w1, w2 = finish("ffn", first)
        return slots(w1), rows(w2)

    p.update(layer1_mixer=layer1_mixer, layer1_ffn=layer1_ffn)
    return p


def _pair_parts(grads, tag):
    stacks = [g.reshape(2, -1, g.shape[-1]) for g in grads.values()]
    parts = _pair_reduce(stacks, name=f"grads_pair_reduce_{tag}")
    return [t.reshape(N_CHIPS, -1, t.shape[-1]) for t in parts]


def _chip_sums(chip, names, parts, landed, tag):
    return {k: _sum_contributions(chip, t, u, name=f"grads_chip_sum_{k}_{tag}")
            for k, t, u in zip(names, parts, landed)}


def _small_layout(shapes):
    offs, o = {}, 0
    for k in (*SMALL_REPL, *SMALL_SHARD):
        size = math.prod(shapes[k])
        offs[k] = (o, size)
        o += size
    rows = -(-(o + 1) // (8 * 128)) * 8
    return offs, rows


def _reduce_small(g, loss_part, full_shapes):
    offs, rows = _small_layout(full_shapes)
    flat = jnp.concatenate([*[g[k].reshape(-1) for k in (*SMALL_REPL, *SMALL_SHARD)], loss_part[0, :1]])
    flat = jnp.pad(flat, (0, rows * 128 - flat.shape[0])).reshape(rows, 128)
    total = _all_reduce_small(flat, name="small_all_reduce").reshape(-1)
    end = max(o + n for o, n in offs.values())
    return {k: total[o:o + n].reshape(full_shapes[k]) for k, (o, n) in offs.items()}, total[end]


def kernel(x, mem, norm_mix, norm_ffn, mem_norm, w_kv, w_out, w_ffn1, w_ffn2, a_in, a_ln_g, a_ln_b, a_ws, a_bs, b_in, b_conv_w, b_conv_b, b_dt_bias, b_a_log, b_d, b_gnorm, final_norm, loss_target, m_norm_mix, m_norm_ffn, m_mem_norm, m_w_kv, m_w_out, m_w_ffn1, m_w_ffn2, m_a_in, m_a_ln_g, m_a_ln_b, m_a_ws, m_a_bs, m_b_in, m_b_conv_w, m_b_conv_b, m_b_dt_bias, m_b_a_log, m_b_d, m_b_gnorm, m_final_norm, v_norm_mix, v_norm_ffn, v_mem_norm, v_w_kv, v_w_out, v_w_ffn1, v_w_ffn2, v_a_in, v_a_ln_g, v_a_ln_b, v_a_ws, v_a_bs, v_b_in, v_b_conv_w, v_b_conv_b, v_b_dt_bias, v_b_a_log, v_b_d, v_b_gnorm, v_final_norm):
    w = dict(norm_mix=norm_mix, norm_ffn=norm_ffn, mem_norm=mem_norm, w_kv=w_kv, w_out=w_out, w_ffn1=w_ffn1,
             w_ffn2=w_ffn2, a_in=a_in, a_ln_g=a_ln_g, a_ln_b=a_ln_b, a_ws=a_ws, a_bs=a_bs, b_in=b_in, b_conv_w=b_conv_w,
             b_conv_b=b_conv_b, b_dt_bias=b_dt_bias, b_a_log=b_a_log, b_d=b_d, b_gnorm=b_gnorm, final_norm=final_norm)
    mom = dict(norm_mix=m_norm_mix, norm_ffn=m_norm_ffn, mem_norm=m_mem_norm, w_kv=m_w_kv, w_out=m_w_out,
               w_ffn1=m_w_ffn1, w_ffn2=m_w_ffn2, a_in=m_a_in, a_ln_g=m_a_ln_g, a_ln_b=m_a_ln_b, a_ws=m_a_ws,
               a_bs=m_a_bs, b_in=m_b_in, b_conv_w=m_b_conv_w, b_conv_b=m_b_conv_b, b_dt_bias=m_b_dt_bias,
               b_a_log=m_b_a_log, b_d=m_b_d, b_gnorm=m_b_gnorm, final_norm=m_final_norm)
    var = dict(norm_mix=v_norm_mix, norm_ffn=v_norm_ffn, mem_norm=v_mem_norm, w_kv=v_w_kv, w_out=v_w_out,
               w_ffn1=v_w_ffn1, w_ffn2=v_w_ffn2, a_in=v_a_in, a_ln_g=v_a_ln_g, a_ln_b=v_a_ln_b, a_ws=v_a_ws,
               a_bs=v_a_bs, b_in=v_b_in, b_conv_w=v_b_conv_w, b_conv_b=v_b_conv_b, b_dt_bias=v_b_dt_bias,
               b_a_log=v_b_a_log, b_d=v_b_d, b_gnorm=v_b_gnorm, final_norm=v_final_norm)

    p = _gather_weights(w)
    p.update(norm_mix=norm_mix, norm_ffn=norm_ffn, mem_norm=mem_norm, a_ln_g=a_ln_g, a_ln_b=a_ln_b, a_ws=a_ws[0],
             a_bs=a_bs[0], b_dt_bias=b_dt_bias, b_a_log=b_a_log, b_d=b_d, final_norm=final_norm)
    chip = 2 * lax.axis_index("x") + lax.axis_index("y")
    chip_arr = jnp.reshape(chip, (1,)).astype(jnp.int32)
    started = {}

    def start_scatter(tag):
        def hook(grads):
            start = _chip_scatter_start(_pair_parts(grads, tag), name=f"grads_chip_scatter_start_{tag}")
            started[tag] = (tuple(grads), start)
            return start[-1]
        return hook

    loss_part, dx, g, _, _ = _local_step(x[0], mem[0], loss_target[0], p, start_scatter("1"), start_scatter("0f"),
                                         start_scatter("0m"))
    full_shapes = {k: w[k].shape for k in SMALL_REPL}
    full_shapes.update(b_conv_w=(1, CONV_K, CONV_DIM), b_conv_b=(1, CONV_DIM), b_gnorm=(1, D_INNER))
    grads, loss = _reduce_small(g, loss_part, full_shapes)
    grads["b_conv_w"] = lax.dynamic_slice_in_dim(grads["b_conv_w"], chip * CONV_SHARD, CONV_SHARD, axis=2)
    grads["b_conv_b"] = lax.dynamic_slice_in_dim(grads["b_conv_b"], chip * CONV_SHARD, CONV_SHARD, axis=1)
    grads["b_gnorm"] = lax.dynamic_slice_in_dim(grads["b_gnorm"], chip * GN_SHARD, GN_SHARD, axis=1)

    def finish_scatter(tag, *first):
        names, (send_sems, recv_sems, parts, lands, _) = started[tag]
        parts, landed = _chip_scatter_wait(send_sems, recv_sems, parts, lands, first,
                                           name=f"grads_chip_scatter_wait_{tag}")
        return _chip_sums(chip_arr, names, parts, landed, tag)

    def adamw(names, grads):
        for k in names:
            shape = w[k].shape
            if len(shape) == 3 and shape[2] % 128 and not shape[1] % 128:
                flat = unflat = lambda a: jnp.transpose(a, (0, 2, 1))
            else:
                flat = (lambda a: a) if len(shape) == 3 else (lambda a: a.reshape(1, -1, shape[-1]))
                unflat = lambda a: a.reshape(shape)
            d, m_new, v_new = _adamw(flat(w[k]), flat(grads[k]), flat(mom[k]), flat(var[k]), name=f"adamw_{k}")
            delta[k], new_m[k], new_v[k] = unflat(d), unflat(m_new), unflat(v_new)

    delta, new_m, new_v = {}, {}, {}
    halves = [finish_scatter("0f", dx), finish_scatter("1", dx)]
    early = ("w_ffn1", "w_ffn2", "b_in")
    shared = _pair_share([[halves[layer][k] for layer in range(2) if k in halves[layer]] for k in early],
                         name="grads_pair_share_early")
    grads.update({k: a.reshape(w[k].shape) for k, a in zip(early, shared)})
    adamw([k for k in WEIGHTS if k in grads], grads)
    halves[0].update(finish_scatter("0m", delta["w_ffn2"]))
    late = ("w_kv", "w_out", "a_in")
    shared = _pair_share([[halves[layer][k] for layer in range(2) if k in halves[layer]] for k in late],
                         name="grads_pair_share_late")
    grads.update({k: a.reshape(w[k].shape) for k, a in zip(late, shared)})
    adamw(late, grads)

    return (loss, dx.reshape(x.shape), *[grads[k] for k in WEIGHTS], *[delta[k] for k in WEIGHTS],
            *[new_m[k] for k in WEIGHTS], *[new_v[k] for k in WEIGHTS])
```

```python
import math

import jax
import jax.numpy as jnp
from jax import lax
from jax.experimental import pallas as pl
from jax.experimental.pallas import tpu as pltpu

F32 = jnp.float32
BF16 = jnp.bfloat16
SDS = jax.ShapeDtypeStruct

D_MODEL = 1024
SEQ = 2048
CHUNK = 128
N_MEM = 256
D_INNER = 2048
A_GROUPS = 8
A_GROUP_W = D_INNER // A_GROUPS
SSM_HEADS = 32
SSM_HEAD_DIM = 64
SSM_GROUPS = 4
SSM_HPG = 8
SSM_STATE = 128
SSM_GROUP_W = SSM_HPG * SSM_HEAD_DIM
CONV_K = 4
CONV_DIM = 3072
X_HEADS = 4
X_HEAD_DIM = 256
X_WIDTH = 1024
MIX_OUT = 3072
D_FF = 4096
A_IN = 5120
B_IN = 6176
B_IN_PAD = 6272
B_Q_OFF = 5120
B_DT_OFF = 6144
N_CHUNKS = SEQ // CHUNK
EPS = 1e-6
N_CHIPS = 4

ADAM_LR = 0.001
ADAM_B1 = 0.9
ADAM_B2 = 0.999
ADAM_EPS = 1e-08
ADAM_WD = 0.01
ADAM_STEP = 10

VMEM_LIMIT = 48 * 1024 * 1024
MESH = pl.DeviceIdType.MESH


def _cparams(sem):
    return pltpu.CompilerParams(dimension_semantics=sem, vmem_limit_bytes=VMEM_LIMIT)


def _dot(a, b, dims=(((1,), (0,)), ((), ()))):
    return lax.dot_general(a.astype(BF16), b.astype(BF16), dims, preferred_element_type=F32)


def _dot_nt(a, b):
    return _dot(a, b, (((1,), (1,)), ((), ())))


def _dot_tn(a, b):
    return _dot(a, b, (((0,), (0,)), ((), ())))


def _pick(n, cands):
    for c in cands:
        if n % c == 0:
            return c
    raise ValueError(f"no tile for {n}")


def _mm_call(a, b, *, dims, grid, a_spec, b_spec, acc_shape, out_shapes, out_specs, name,
             extras=(), extra_specs=(), epilogue=None, after=()):
    n_k = grid[2]
    n_extra = len(extras)
    n_out = len(out_shapes)
    n_in = 2 + n_extra + len(after)

    def finish(total, extra_refs, out_refs):
        vals = (total,) if epilogue is None else epilogue(total, *[e[...] for e in extra_refs])
        for o_ref, v in zip(out_refs, vals):
            o_ref[...] = v.astype(o_ref.dtype)

    def body_one_step(*refs):
        finish(_dot(refs[0][...], refs[1][...], dims), refs[2:2 + n_extra], refs[n_in:n_in + n_out])

    def body(*refs):
        acc = refs[-1]
        k = pl.program_id(2)

        @pl.when(k == 0)
        def _():
            acc[...] = jnp.zeros_like(acc)

        acc[...] += _dot(refs[0][...], refs[1][...], dims)

        @pl.when(k == n_k - 1)
        def _():
            finish(acc[...], refs[2:2 + n_extra], refs[n_in:n_in + n_out])

    return pl.pallas_call(
        body_one_step if n_k == 1 else body, grid=grid,
        in_specs=[a_spec, b_spec, *extra_specs, *([ANY] * len(after))], out_specs=list(out_specs),
        out_shape=list(out_shapes), scratch_shapes=[] if n_k == 1 else [pltpu.VMEM(acc_shape, F32)],
        compiler_params=_cparams(("parallel", "parallel", "arbitrary")), name=name,
    )(a, b, *extras, *after)


def _w_dims(w):
    if w.ndim == 2:
        return w.shape[0], w.shape[1], 1, w.shape[1]
    return w.shape[1], w.shape[0] * w.shape[2], w.shape[0], w.shape[2]


def _mm_nn(a, w, *, name, out_dtype=F32, a_cols=None, extras=(), epilogue=None, n_out_dtypes=None, after=()):
    m = a.shape[0]
    k_dim, n_dim, _, n_slot = _w_dims(w)
    a_off, a_w = (0, a.shape[1]) if a_cols is None else a_cols
    assert a_w == k_dim
    tm = _pick(m, (2048, 1024, 512, 256))
    tn = _pick(n_slot, (512, 896, 640, 256, 128))
    tk = _pick(k_dim, (1024, 768, 512, 384, 256, 128))
    assert a_off % tk == 0
    nb = n_slot // tn
    a_spec = pl.BlockSpec((tm, tk), lambda i, j, k: (i, a_off // tk + k))
    if w.ndim == 2:
        b_spec = pl.BlockSpec((tk, tn), lambda i, j, k: (k, j))
    else:
        b_spec = pl.BlockSpec((None, tk, tn), lambda i, j, k: (j // nb, k, j % nb))
    o_spec = pl.BlockSpec((tm, tn), lambda i, j, k: (i, j))
    dts = n_out_dtypes or (out_dtype,)
    outs = _mm_call(a, w, dims=(((1,), (0,)), ((), ())), grid=(m // tm, n_dim // tn, k_dim // tk),
                    a_spec=a_spec, b_spec=b_spec, acc_shape=(tm, tn),
                    out_shapes=[SDS((m, n_dim), dt) for dt in dts], out_specs=[o_spec] * len(dts), name=name,
                    extras=extras, extra_specs=[o_spec] * len(extras), epilogue=epilogue, after=after)
    return outs if n_out_dtypes else outs[0]


def _mm_nt(a, w, *, name, out_dtype=F32, extras=(), epilogue=None, after=()):
    m = a.shape[0]
    k_dim, n_dim, _, n_slot = _w_dims(w)
    assert a.shape[1] == n_dim
    tm = _pick(m, (2048, 1024, 512, 256))
    to = _pick(k_dim, (512, 384, 256, 128))
    tc = _pick(n_slot, (1280, 1024, 896, 640, 512, 256, 128))
    nb = n_slot // tc
    a_spec = pl.BlockSpec((tm, tc), lambda i, j, k: (i, k))
    if w.ndim == 2:
        b_spec = pl.BlockSpec((to, tc), lambda i, j, k: (j, k))
    else:
        b_spec = pl.BlockSpec((None, to, tc), lambda i, j, k: (k // nb, j, k % nb))
    o_spec = pl.BlockSpec((tm, to), lambda i, j, k: (i, j))
    return _mm_call(a, w, dims=(((1,), (1,)), ((), ())), grid=(m // tm, k_dim // to, n_dim // tc),
                    a_spec=a_spec, b_spec=b_spec, acc_shape=(tm, to),
                    out_shapes=[SDS((m, k_dim), out_dtype)], out_specs=[o_spec], name=name,
                    extras=extras, extra_specs=[o_spec] * len(extras), epilogue=epilogue, after=after)[0]


def _mm_tn(x, dy, *, name, x_cols=None):
    s = x.shape[0]
    x_off, k_dim = (0, x.shape[1]) if x_cols is None else x_cols
    n_dim = dy.shape[1]
    tm = _pick(k_dim, (1024, 768, 512, 384, 256, 128))
    tn = _pick(n_dim, (512, 896, 640, 256, 128))
    tk = _pick(s, (2048, 1024, 512, 256))
    assert x_off % tm == 0
    a_spec = pl.BlockSpec((tk, tm), lambda i, j, k: (k, x_off // tm + i))
    b_spec = pl.BlockSpec((tk, tn), lambda i, j, k: (k, j))
    o_spec = pl.BlockSpec((tm, tn), lambda i, j, k: (i, j))
    return _mm_call(x, dy, dims=(((0,), (0,)), ((), ())), grid=(k_dim // tm, n_dim // tn, s // tk),
                    a_spec=a_spec, b_spec=b_spec, acc_shape=(tm, tn),
                    out_shapes=[SDS((k_dim, n_dim), F32)], out_specs=[o_spec], name=name)[0]


def _mm_tn_stacked(x, dy, *, name, col_slots):
    s, k_dim = x.shape
    n_dim = dy.shape[1]
    r, c = (k_dim // 2, n_dim // N_CHIPS) if col_slots else (k_dim // N_CHIPS // 2, n_dim)
    tm = 2 * r
    tn = _pick(c, (512, 896, 640, 256, 128))
    tk = _pick(s, (2048, 1024, 512, 256))
    a_spec = pl.BlockSpec((tk, tm), lambda i, j, k: (k, i))
    b_spec = pl.BlockSpec((tk, tn), lambda i, j, k: (k, j))
    if col_slots:
        nb = c // tn
        o_spec = pl.BlockSpec((2, None, r, tn), lambda i, j, k: (0, j // nb, 0, j % nb))
    else:
        o_spec = pl.BlockSpec((2, None, r, tn), lambda i, j, k: (0, i, 0, j))
    return _mm_call(x, dy, dims=(((0,), (0,)), ((), ())), grid=(k_dim // tm, n_dim // tn, s // tk),
                    a_spec=a_spec, b_spec=b_spec, acc_shape=(tm, tn), epilogue=lambda acc: (acc.reshape(2, r, tn),),
                    out_shapes=[SDS((2, N_CHIPS, r, c), F32)], out_specs=[o_spec], name=name)[0]


def _rms(x, g):
    return x * lax.rsqrt(jnp.mean(x * x, axis=-1, keepdims=True) + EPS) * g


def _rms_fwd(h, g, *, name):
    rows, d = h.shape
    tr = _pick(rows, (512, 256))

    def body(h_ref, g_ref, o_ref):
        o_ref[...] = _rms(h_ref[...], g_ref[...]).astype(o_ref.dtype)

    return pl.pallas_call(
        body, grid=(rows // tr,),
        in_specs=[pl.BlockSpec((tr, d), lambda i: (i, 0)), pl.BlockSpec((1, d), lambda i: (0, 0))],
        out_specs=pl.BlockSpec((tr, d), lambda i: (i, 0)), out_shape=SDS((rows, d), BF16),
        compiler_params=_cparams(("parallel",)), name=name)(h, g)


def _rms_bwd(h, g, da, dres, *, name):
    rows, d = h.shape
    tr = _pick(rows, (512, 256))

    def body(h_ref, g_ref, da_ref, dres_ref, dh_ref, dg_ref):
        _, vjp = jax.vjp(_rms, h_ref[...], g_ref[...])
        dh, dg = vjp(da_ref[...].astype(F32))
        dh_ref[...] = dres_ref[...] + dh

        @pl.when(pl.program_id(0) == 0)
        def _():
            dg_ref[...] = jnp.zeros_like(dg_ref)

        dg_ref[...] += dg

    row_spec = pl.BlockSpec((tr, d), lambda i: (i, 0))
    vec_spec = pl.BlockSpec((1, d), lambda i: (0, 0))
    return pl.pallas_call(
        body, grid=(rows // tr,), in_specs=[row_spec, vec_spec, row_spec, row_spec],
        out_specs=[row_spec, vec_spec], out_shape=[SDS((rows, d), F32), SDS((1, d), F32)],
        compiler_params=_cparams(("arbitrary",)), name=name)(h, g, da, dres)


def _loss_head(h, g, target, *, name):
    rows, d = h.shape
    tr = _pick(rows, (512, 256))

    def body(h_ref, g_ref, t_ref, loss_ref, dh_ref, dg_ref):
        y, vjp = jax.vjp(_rms, h_ref[...], g_ref[...])
        err = y - t_ref[...]
        dh, dg = vjp(err * (1.0 / d))
        dh_ref[...] = dh

        @pl.when(pl.program_id(0) == 0)
        def _():
            dg_ref[...] = jnp.zeros_like(dg_ref)
            loss_ref[...] = jnp.zeros_like(loss_ref)

        dg_ref[...] += dg
        part = jnp.sum(jnp.sum(err * err, axis=-1, keepdims=True), axis=0, keepdims=True) * (0.5 / d)
        loss_ref[...] += jnp.broadcast_to(part, loss_ref.shape)

    row_spec = pl.BlockSpec((tr, d), lambda i: (i, 0))
    vec_spec = pl.BlockSpec((1, d), lambda i: (0, 0))
    loss_spec = pl.BlockSpec((8, 128), lambda i: (0, 0))
    return pl.pallas_call(
        body, grid=(rows // tr,), in_specs=[row_spec, vec_spec, row_spec],
        out_specs=[loss_spec, row_spec, vec_spec],
        out_shape=[SDS((8, 128), F32), SDS((rows, d), F32), SDS((1, d), F32)],
        compiler_params=_cparams(("arbitrary",)), name=name)(h, g, target)


def _gelu(x):
    return 0.5 * x * (1.0 + lax.erf(x * (1.0 / math.sqrt(2.0))))


def _gate_tile(pu, pv, ln_g, ln_b, ws, bs_t):
    u = [_gelu(p) for p in pu]
    v = [_gelu(p) for p in pv]
    mu = sum(jnp.sum(t, axis=-1, keepdims=True) for t in v) * (1.0 / D_INNER)
    vc = [t - mu for t in v]
    var = sum(jnp.sum(t * t, axis=-1, keepdims=True) for t in vc) * (1.0 / D_INNER)
    rstd = lax.rsqrt(var + EPS)
    row = lax.broadcasted_iota(jnp.int32, (CHUNK, CHUNK), 0)
    col = lax.broadcasted_iota(jnp.int32, (CHUNK, CHUNK), 1)
    out = []
    for gi in range(A_GROUPS):
        vn = vc[gi] * rstd * ln_g[gi] + ln_b[gi]
        w = jnp.where(row >= col, ws[gi], 0.0)
        sv = _dot(w, vn) + bs_t[gi]
        out.append(u[gi] * sv)
    return out


def _split(ref, n, width):
    return [ref[:, i * width:(i + 1) * width] for i in range(n)]


def _gate_in_specs():
    return [
        pl.BlockSpec((CHUNK, D_INNER), lambda c: (c, 0)),
        pl.BlockSpec((CHUNK, D_INNER), lambda c: (c, 1)),
        pl.BlockSpec((1, D_INNER), lambda c: (0, 0)),
        pl.BlockSpec((1, D_INNER), lambda c: (0, 0)),
        pl.BlockSpec((A_GROUPS, CHUNK, CHUNK), lambda c: (0, 0, 0)),
        pl.BlockSpec((A_GROUPS, CHUNK, 1), lambda c: (0, 0, 0)),
    ]


def _gate_args(u_ref, v_ref, g_ref, b_ref, ws_ref, bs_ref):
    ng, gw = A_GROUPS, A_GROUP_W
    return (_split(u_ref, ng, gw), _split(v_ref, ng, gw), _split(g_ref, ng, gw), _split(b_ref, ng, gw),
            [ws_ref[i] for i in range(ng)], [bs_ref[i] for i in range(ng)])


def _gate_fwd(proj, ln_g, ln_b, ws, bs_col, mixcat, *, name):
    def body(u_ref, v_ref, g_ref, b_ref, ws_ref, bs_ref, cat_in, cat_ref):
        del cat_in
        out = _gate_tile(*_gate_args(u_ref, v_ref, g_ref, b_ref, ws_ref, bs_ref))
        for gi, o in enumerate(out):
            cat_ref[:, gi * A_GROUP_W:(gi + 1) * A_GROUP_W] = o.astype(cat_ref.dtype)

    return pl.pallas_call(
        body, grid=(N_CHUNKS,), in_specs=[*_gate_in_specs(), pl.BlockSpec(memory_space=pl.ANY)],
        out_specs=pl.BlockSpec((CHUNK, D_INNER), lambda c: (c, 0)), out_shape=SDS(mixcat.shape, mixcat.dtype),
        input_output_aliases={6: 0}, compiler_params=_cparams(("parallel",)), name=name,
    )(proj, proj, ln_g, ln_b, ws, bs_col, mixcat)


def _gate_bwd(proj, ln_g, ln_b, ws, bs_col, dcat, dproj, *, name):
    ng, gw = A_GROUPS, A_GROUP_W

    def body(u_ref, v_ref, g_ref, b_ref, ws_ref, bs_ref, d_ref, dproj_in, dproj_ref, dg_ref, db_ref, dws_ref, dbs_ref):
        del dproj_in
        args = _gate_args(u_ref, v_ref, g_ref, b_ref, ws_ref, bs_ref)
        _, vjp = jax.vjp(_gate_tile, *args)
        dpu, dpv, dg, db, dws, dbs = vjp(_split(d_ref, ng, gw))
        for gi in range(ng):
            dproj_ref[:, gi * gw:(gi + 1) * gw] = dpu[gi].astype(dproj_ref.dtype)
            dproj_ref[:, D_INNER + gi * gw:D_INNER + (gi + 1) * gw] = dpv[gi].astype(dproj_ref.dtype)

        @pl.when(pl.program_id(0) == 0)
        def _():
            for r in (dg_ref, db_ref, dws_ref, dbs_ref):
                r[...] = jnp.zeros_like(r)

        for gi in range(ng):
            dg_ref[:, gi * gw:(gi + 1) * gw] += dg[gi]
            db_ref[:, gi * gw:(gi + 1) * gw] += db[gi]
            dws_ref[gi] += dws[gi]
            dbs_ref[gi] += dbs[gi]

    in_specs = _gate_in_specs()
    return pl.pallas_call(
        body, grid=(N_CHUNKS,),
        in_specs=[*in_specs, pl.BlockSpec((CHUNK, D_INNER), lambda c: (c, 0)), pl.BlockSpec(memory_space=pl.ANY)],
        out_specs=[pl.BlockSpec((CHUNK, 2 * D_INNER), lambda c: (c, 0)), *in_specs[2:]],
        out_shape=[SDS(dproj.shape, dproj.dtype), SDS((1, D_INNER), F32), SDS((1, D_INNER), F32),
                   SDS((ng, CHUNK, CHUNK), F32), SDS((ng, CHUNK, 1), F32)],
        input_output_aliases={7: 0}, compiler_params=_cparams(("arbitrary",)), name=name,
    )(proj, proj, ln_g, ln_b, ws, bs_col, dcat, dproj)


ATT_TQ = 512


def _attn_tile(q, k, v):
    s = _dot_nt(q, k) * (1.0 / math.sqrt(X_HEAD_DIM))
    s = s - jnp.max(s, axis=-1, keepdims=True)
    e = jnp.exp(s)
    p = e / jnp.sum(e, axis=-1, keepdims=True)
    return _dot(p, v)


def _attn_in_specs(q_blk, order):
    hd = X_HEAD_DIM
    return [
        pl.BlockSpec((ATT_TQ, hd), lambda a, b: (order(a, b)[0], q_blk + order(a, b)[1])),
        pl.BlockSpec((N_MEM, hd), lambda a, b: (0, order(a, b)[1])),
        pl.BlockSpec((N_MEM, hd), lambda a, b: (0, X_HEADS + order(a, b)[1])),
    ]


def _attn_fwd(proj, q_off, kv, *, name):
    order = lambda i, h: (i, h)
    cat_blk = D_INNER // X_HEAD_DIM

    def body(q_ref, k_ref, v_ref, o_ref):
        o_ref[...] = _attn_tile(q_ref[...], k_ref[...], v_ref[...]).astype(o_ref.dtype)

    return pl.pallas_call(
        body, grid=(SEQ // ATT_TQ, X_HEADS), in_specs=_attn_in_specs(q_off // X_HEAD_DIM, order),
        out_specs=pl.BlockSpec((ATT_TQ, X_HEAD_DIM), lambda i, h: (i, cat_blk + h)),
        out_shape=SDS((SEQ, MIX_OUT), BF16), compiler_params=_cparams(("parallel", "parallel")), name=name,
    )(proj, kv, kv)


def _attn_bwd(proj, q_off, kv, dcat, dproj_width, dq_off, *, name):
    order = lambda h, i: (i, h)
    cat_blk = D_INNER // X_HEAD_DIM
    dq_blk = dq_off // X_HEAD_DIM

    def body(q_ref, k_ref, v_ref, do_ref, dq_ref, dk_ref, dv_ref):
        _, vjp = jax.vjp(_attn_tile, q_ref[...], k_ref[...], v_ref[...])
        dq, dk, dv = vjp(do_ref[...])
        dq_ref[...] = dq.astype(dq_ref.dtype)

        @pl.when(pl.program_id(1) == 0)
        def _():
            dk_ref[...] = jnp.zeros_like(dk_ref)
            dv_ref[...] = jnp.zeros_like(dv_ref)

        dk_ref[...] += dk
        dv_ref[...] += dv

    kv_spec = pl.BlockSpec((N_MEM, X_HEAD_DIM), lambda h, i: (0, h))
    return pl.pallas_call(
        body, grid=(X_HEADS, SEQ // ATT_TQ),
        in_specs=[*_attn_in_specs(q_off // X_HEAD_DIM, order),
                  pl.BlockSpec((ATT_TQ, X_HEAD_DIM), lambda h, i: (i, cat_blk + h))],
        out_specs=[pl.BlockSpec((ATT_TQ, X_HEAD_DIM), lambda h, i: (i, dq_blk + h)), kv_spec, kv_spec],
        out_shape=[SDS((SEQ, dproj_width), BF16), SDS((N_MEM, X_WIDTH), F32), SDS((N_MEM, X_WIDTH), F32)],
        compiler_params=_cparams(("parallel", "arbitrary")), name=name,
    )(proj, kv, kv, dcat)


CONV_TC = 512


def _shift_down(x, s):
    if s == 0:
        return x
    row = lax.broadcasted_iota(jnp.int32, x.shape, 0)
    return jnp.where(row >= s, pltpu.roll(x, s, 0), 0.0)


def _shift_up(x, s):
    if s == 0:
        return x
    n = x.shape[0]
    row = lax.broadcasted_iota(jnp.int32, x.shape, 0)
    return jnp.where(row < n - s, pltpu.roll(x, n - s, 0), 0.0)


def _conv_pre(x, w_ref, b_ref):
    pre = b_ref[...] + jnp.zeros_like(x)
    for k in range(CONV_K):
        pre = pre + w_ref[k:k + 1, :] * _shift_down(x, CONV_K - 1 - k)
    return pre


def _conv_fwd(proj, w, b, *, name):
    blk0 = D_INNER // CONV_TC

    def body(x_ref, w_ref, b_ref, o_ref):
        pre = _conv_pre(x_ref[...], w_ref, b_ref)
        o_ref[...] = pre * jax.nn.sigmoid(pre)

    return pl.pallas_call(
        body, grid=(CONV_DIM // CONV_TC,),
        in_specs=[pl.BlockSpec((SEQ, CONV_TC), lambda j: (0, blk0 + j)), pl.BlockSpec((CONV_K, CONV_TC), lambda j: (0, j)),
                  pl.BlockSpec((1, CONV_TC), lambda j: (0, j))],
        out_specs=pl.BlockSpec((SEQ, CONV_TC), lambda j: (0, j)), out_shape=SDS((SEQ, CONV_DIM), F32),
        compiler_params=_cparams(("parallel",)), name=name)(proj, w, b)


def _conv_bwd(proj, w, b, dxs, dbm, dcm, dproj, *, name):
    tc = CONV_TC // 2
    blk0 = D_INNER // tc
    n_x = D_INNER // tc
    n_b = SSM_GROUPS * SSM_STATE // tc

    def body(x_ref, w_ref, b_ref, dxs_ref, dbm_ref, dcm_ref, dproj_in, dproj_ref, dw_ref, db_ref):
        del dproj_in
        j = pl.program_id(0)
        x = x_ref[...]
        pre = _conv_pre(x, w_ref, b_ref)
        sg = jax.nn.sigmoid(pre)
        dact = jnp.where(j < n_x, dxs_ref[...], jnp.where(j < n_x + n_b, dbm_ref[...], dcm_ref[...]))
        dpre = dact * (sg * (1.0 + pre * (1.0 - sg)))
        dx = jnp.zeros_like(x)
        for k in range(CONV_K):
            s = CONV_K - 1 - k
            dx = dx + w_ref[k:k + 1, :] * _shift_up(dpre, s)
            dw_ref[k:k + 1, :] = jnp.sum(dpre * _shift_down(x, s), axis=0, keepdims=True)
        dproj_ref[...] = dx.astype(dproj_ref.dtype)
        db_ref[...] = jnp.sum(dpre, axis=0, keepdims=True)

    clip = lambda v, hi: jnp.minimum(jnp.maximum(v, 0), hi)
    return pl.pallas_call(
        body, grid=(CONV_DIM // tc,),
        in_specs=[pl.BlockSpec((SEQ, tc), lambda j: (0, blk0 + j)), pl.BlockSpec((CONV_K, tc), lambda j: (0, j)),
                  pl.BlockSpec((1, tc), lambda j: (0, j)),
                  pl.BlockSpec((SEQ, tc), lambda j: (0, clip(j, n_x - 1))),
                  pl.BlockSpec((SEQ, tc), lambda j: (0, clip(j - n_x, n_b - 1))),
                  pl.BlockSpec((SEQ, tc), lambda j: (0, clip(j - n_x - n_b, n_b - 1))),
                  pl.BlockSpec(memory_space=pl.ANY)],
        out_specs=[pl.BlockSpec((SEQ, tc), lambda j: (0, blk0 + j)), pl.BlockSpec((CONV_K, tc), lambda j: (0, j)),
                   pl.BlockSpec((1, tc), lambda j: (0, j))],
        out_shape=[SDS(dproj.shape, dproj.dtype), SDS((CONV_K, CONV_DIM), F32), SDS((1, CONV_DIM), F32)],
        input_output_aliases={6: 0}, compiler_params=_cparams(("parallel",)), name=name,
    )(proj, w, b, dxs, dbm, dcm, dproj)


SSM_PAIRS = SSM_HPG // 2


def _dot_exact01(x, m01, m01_t, x_first, differentiable):
    def product(v, m):
        hi = v.astype(BF16)
        rest = v - hi.astype(F32)
        mid = rest.astype(BF16)
        lo = (rest - mid.astype(F32)).astype(BF16)
        dims = (((1,), (0,)), ((), ()))
        dot = lambda part: lax.dot_general(*((part, m) if x_first else (m, part)), dims, preferred_element_type=F32)
        return dot(hi) + dot(mid) + dot(lo)

    if not differentiable:
        return product(x, m01)

    @jax.custom_vjp
    def exact(v):
        return product(v, m01)

    exact.defvjp(lambda v: (product(v, m01), None), lambda _, ct: (product(ct, m01_t),))
    return exact(x)


def _ssd_tile(xp, zp, bm, cm, hp, dt_c, dt_r, bias, bias_col, alog, alog_col, dsk, gnp, differentiable=False):
    row = lax.broadcasted_iota(jnp.int32, (CHUNK, CHUNK), 0)
    col = lax.broadcasted_iota(jnp.int32, (CHUNK, CHUNK), 1)
    causal = row >= col
    left = col < SSM_HEAD_DIM
    top = row < SSM_HEAD_DIM
    ones = jnp.ones((CHUNK, CHUNK), BF16)
    cb = _dot_nt(cm, bm)
    dtp = jax.nn.softplus(dt_c + bias)
    da_c = dtp * -jnp.exp(alog)
    da_r = jax.nn.softplus(dt_r + bias_col) * -jnp.exp(alog_col)
    lower = jnp.where(causal, 1.0, 0.0).astype(BF16)
    upper = jnp.where(row <= col, 1.0, 0.0).astype(BF16)
    cs = _dot_exact01(da_c, lower, upper, False, differentiable)
    cs_rows = _dot_exact01(da_r, upper, lower, True, differentiable)
    cs_last = jnp.sum(da_c, axis=0, keepdims=True)
    ecs, decay, ecl = jnp.exp(cs), jnp.exp(cs_last - cs), jnp.exp(cs_last)
    m = [cb * jnp.exp(jnp.where(causal, cs[:, r:r + 1] - cs_rows[r:r + 1, :], -1e30)) for r in range(SSM_HPG)]
    ygs, hn = [], []
    for p in range(SSM_PAIRS):
        a, b = 2 * p, 2 * p + 1
        pair = lambda v: jnp.where(left, v[:, a:a + 1], v[:, b:b + 1])
        xdt = xp[p] * pair(dtp)
        y = jnp.where(left, _dot(m[a], xdt), _dot(m[b], xdt))
        y = y + _dot_nt(cm, hp[p]) * pair(ecs)
        y = y + xp[p] * pair(dsk)
        states = _dot_tn(xdt * pair(decay), bm)
        hn.append(hp[p] * jnp.where(top, ecl[:, a:a + 1], ecl[:, b:b + 1]) + states)
        ygs.append(y * (zp[p] * jax.nn.sigmoid(zp[p])))
    ms = sum(_dot(t * t, ones) for t in ygs) * (1.0 / SSM_GROUP_W)
    rs = lax.rsqrt(ms + EPS)
    return [ygs[p] * rs * gnp[p] for p in range(SSM_PAIRS)], hn


def _ssd_in_specs(cidx):
    gw, n = SSM_GROUP_W, SSM_STATE
    bm_blk = D_INNER // n
    return [
        pl.BlockSpec((CHUNK, gw), lambda g, c: (cidx(c), g)),
        pl.BlockSpec((CHUNK, gw), lambda g, c: (cidx(c), g)),
        pl.BlockSpec((CHUNK, n), lambda g, c: (cidx(c), bm_blk + g)),
        pl.BlockSpec((CHUNK, n), lambda g, c: (cidx(c), bm_blk + SSM_GROUPS + g)),
        pl.BlockSpec((None, CHUNK, SSM_HPG), lambda g, c: (g, cidx(c), 0)),
        pl.BlockSpec((None, SSM_HPG, CHUNK), lambda g, c: (g, 0, cidx(c))),
        pl.BlockSpec((None, 3, SSM_HPG), lambda g, c: (g, 0, 0)),
        pl.BlockSpec((None, SSM_HPG, 2), lambda g, c: (g, 0, 0)),
        pl.BlockSpec((1, gw), lambda g, c: (0, g)),
    ]


def _ssd_args(x_ref, z_ref, bm_ref, cm_ref, hp, dtc_ref, dtr_ref, prow_ref, pcol_ref, gn_ref):
    npair, w = SSM_PAIRS, 2 * SSM_HEAD_DIM
    return (_split(x_ref, npair, w), _split(z_ref, npair, w), bm_ref[...], cm_ref[...], hp, dtc_ref[...], dtr_ref[...],
            prow_ref[0:1, :], pcol_ref[:, 0:1], prow_ref[1:2, :], pcol_ref[:, 1:2], prow_ref[2:3, :],
            _split(gn_ref, npair, w))


def _pair_rows(ref):
    w = 2 * SSM_HEAD_DIM
    return [ref[p * w:(p + 1) * w, :] for p in range(SSM_PAIRS)]


def _ssd_fwd(xbc, proj, dt_c, dt_r, par_row, par_col, gn, mixcat, *, name):
    w = 2 * SSM_HEAD_DIM

    def body(x_ref, z_ref, bm_ref, cm_ref, dtc_ref, dtr_ref, prow_ref, pcol_ref, gn_ref, cat_in,
             cat_ref, hprev_ref, h_scr):
        del cat_in

        @pl.when(pl.program_id(1) == 0)
        def _():
            h_scr[...] = jnp.zeros_like(h_scr)

        hprev_ref[...] = h_scr[...]
        yn, hn = _ssd_tile(*_ssd_args(x_ref, z_ref, bm_ref, cm_ref, _pair_rows(h_scr), dtc_ref, dtr_ref, prow_ref,
                                      pcol_ref, gn_ref))
        for p in range(SSM_PAIRS):
            cat_ref[:, p * w:(p + 1) * w] = yn[p].astype(cat_ref.dtype)
            h_scr[p * w:(p + 1) * w, :] = hn[p]

    return pl.pallas_call(
        body, grid=(SSM_GROUPS, N_CHUNKS), in_specs=[*_ssd_in_specs(lambda c: c), pl.BlockSpec(memory_space=pl.ANY)],
        out_specs=[pl.BlockSpec((CHUNK, SSM_GROUP_W), lambda g, c: (c, g)),
                   pl.BlockSpec((None, None, SSM_GROUP_W, SSM_STATE), lambda g, c: (c, g, 0, 0))],
        out_shape=[SDS(mixcat.shape, mixcat.dtype), SDS((N_CHUNKS, SSM_GROUPS, SSM_GROUP_W, SSM_STATE), F32)],
        scratch_shapes=[pltpu.VMEM((SSM_GROUP_W, SSM_STATE), F32)],
        input_output_aliases={9: 0}, compiler_params=_cparams(("parallel", "arbitrary")), name=name,
    )(xbc, proj, xbc, xbc, dt_c, dt_r, par_row, par_col, gn, mixcat)


def _ssd_bwd(xbc, proj, dt_c, dt_r, par_row, par_col, gn, hprev, dcat, dproj, *, name):
    nh, w, gw, n = SSM_HPG, 2 * SSM_HEAD_DIM, SSM_GROUP_W, SSM_STATE
    rev = lambda c: N_CHUNKS - 1 - c

    def body(x_ref, z_ref, bm_ref, cm_ref, dtc_ref, dtr_ref, prow_ref, pcol_ref, gn_ref, hprev_ref, dy_ref,
             dproj_in, dz_ref, dxs_ref, dbm_ref, dcm_ref, ddtc_ref, ddtr_ref, dprow_ref, dpcol_ref, dgn_ref, dh_scr):
        del dproj_in
        first = pl.program_id(1) == 0

        @pl.when(first)
        def _():
            dh_scr[...] = jnp.zeros_like(dh_scr)
            for ref in (dprow_ref, dpcol_ref, dgn_ref):
                ref[...] = jnp.zeros_like(ref)

        args = _ssd_args(x_ref, z_ref, bm_ref, cm_ref, _pair_rows(hprev_ref), dtc_ref, dtr_ref, prow_ref, pcol_ref,
                         gn_ref)
        _, vjp = jax.vjp(lambda *a: _ssd_tile(*a, differentiable=True), *args)
        dxs, dzs, dbm, dcm, dhs, ddtc, ddtr, dbias, dbias_col, dalog, dalog_col, ddsk, dgn = vjp(
            (_split(dy_ref, SSM_PAIRS, w), _pair_rows(dh_scr)))
        dbm_ref[...] = dbm
        dcm_ref[...] = dcm
        ddtc_ref[...] = ddtc
        ddtr_ref[...] = ddtr
        for q in range(SSM_PAIRS):
            dxs_ref[:, q * w:(q + 1) * w] = dxs[q]
            dz_ref[:, q * w:(q + 1) * w] = dzs[q].astype(dz_ref.dtype)
            dh_scr[q * w:(q + 1) * w, :] = dhs[q]
            dgn_ref[:, q * w:(q + 1) * w] += dgn[q]
        for i, d in enumerate((dbias, dalog, ddsk)):
            dprow_ref[i:i + 1, :] += d
        for i, d in enumerate((dbias_col, dalog_col)):
            dpcol_ref[:, i:i + 1] += d

    return pl.pallas_call(
        body, grid=(SSM_GROUPS, N_CHUNKS),
        in_specs=[*_ssd_in_specs(rev),
                  pl.BlockSpec((None, None, gw, n), lambda g, c: (rev(c), g, 0, 0)),
                  pl.BlockSpec((CHUNK, gw), lambda g, c: (rev(c), g)),
                  pl.BlockSpec(memory_space=pl.ANY)],
        out_specs=[pl.BlockSpec((CHUNK, gw), lambda g, c: (rev(c), g)),
                   pl.BlockSpec((CHUNK, gw), lambda g, c: (rev(c), g)),
                   pl.BlockSpec((CHUNK, n), lambda g, c: (rev(c), g)),
                   pl.BlockSpec((CHUNK, n), lambda g, c: (rev(c), g)),
                   pl.BlockSpec((None, CHUNK, nh), lambda g, c: (g, rev(c), 0)),
                   pl.BlockSpec((None, nh, CHUNK), lambda g, c: (g, 0, rev(c))),
                   pl.BlockSpec((None, 3, nh), lambda g, c: (g, 0, 0)),
                   pl.BlockSpec((None, nh, 2), lambda g, c: (g, 0, 0)),
                   pl.BlockSpec((1, gw), lambda g, c: (0, g))],
        out_shape=[SDS(dproj.shape, dproj.dtype), SDS((SEQ, D_INNER), F32), SDS((SEQ, SSM_GROUPS * n), F32),
                   SDS((SEQ, SSM_GROUPS * n), F32), SDS((SSM_GROUPS, SEQ, nh), F32), SDS((SSM_GROUPS, nh, SEQ), F32),
                   SDS((SSM_GROUPS, 3, nh), F32), SDS((SSM_GROUPS, nh, 2), F32), SDS((1, D_INNER), F32)],
        scratch_shapes=[pltpu.VMEM((gw, n), F32)],
        input_output_aliases={11: 0}, compiler_params=_cparams(("parallel", "arbitrary")), name=name,
    )(xbc, proj, xbc, xbc, dt_c, dt_r, par_row, par_col, gn, hprev, dcat, dproj)


def _sum_contributions(chip, parts, landed, *, name):
    _, r, c = parts.shape
    tr = _pick(r, (256, 384, 128))

    def body(chip_ref, own_ref, landed_ref, o_ref):
        del chip_ref
        acc = own_ref[...].astype(F32)
        for s in range(landed_ref.shape[0]):
            acc = acc + landed_ref[s].astype(F32)
        o_ref[...] = acc

    grid_spec = pltpu.PrefetchScalarGridSpec(
        num_scalar_prefetch=1, grid=(r // tr,),
        in_specs=[pl.BlockSpec((None, tr, c), lambda i, chip_ref: (chip_ref[0], i, 0)),
                  pl.BlockSpec((landed.shape[0], tr, c), lambda i, chip_ref: (0, i, 0))],
        out_specs=pl.BlockSpec((tr, c), lambda i, chip_ref: (i, 0)))
    return pl.pallas_call(body, grid_spec=grid_spec, out_shape=SDS((r, c), F32),
                          compiler_params=_cparams(("parallel",)), name=name)(chip, parts, landed)


def _adamw(w, g, m, v, *, name):
    layers, r, c = w.shape
    if r <= 256 or r % 128 == 0:
        tr = min(r, 256)
        steps, spec = r // tr, pl.BlockSpec((None, tr, c), lambda l, i: (l, i, 0))
    else:
        tc = _pick(c, (256, 128))
        steps, spec = c // tc, pl.BlockSpec((None, r, tc), lambda l, i: (l, 0, i))

    def body(w_ref, g_ref, m_ref, v_ref, d_ref, mo_ref, vo_ref):
        g = g_ref[...]
        m_new = ADAM_B1 * m_ref[...] + (1.0 - ADAM_B1) * g
        v_new = ADAM_B2 * v_ref[...] + (1.0 - ADAM_B2) * (g * g)
        m_hat = m_new / (1.0 - ADAM_B1 ** ADAM_STEP)
        v_hat = v_new / (1.0 - ADAM_B2 ** ADAM_STEP)
        d_ref[...] = -ADAM_LR * (m_hat / (jnp.sqrt(v_hat) + ADAM_EPS) + ADAM_WD * w_ref[...])
        mo_ref[...] = m_new
        vo_ref[...] = v_new

    return pl.pallas_call(body, grid=(layers, steps), in_specs=[spec] * 4, out_specs=[spec] * 3,
                          out_shape=[SDS(w.shape, F32)] * 3, compiler_params=_cparams(("parallel", "parallel")),
                          name=name)(w, g, m, v)


ANY = pl.BlockSpec(memory_space=pl.ANY)


def _place():
    x, y, c = lax.axis_index("x"), lax.axis_index("y"), lax.axis_index("c")
    chips = [(1 - x, y), (x, 1 - y), (1 - x, 1 - y)]
    return x, y, c, chips


def _remote(src, dst, send_sem, recv_sem, to):
    return pltpu.make_async_remote_copy(src_ref=src, dst_ref=dst, send_sem=send_sem, recv_sem=recv_sem,
                                        device_id=to, device_id_type=MESH)


STREAM_ROWS = 256


def _stream_rows(i):
    return pl.ds(pl.multiple_of(i * STREAM_ROWS, STREAM_ROWS), STREAM_ROWS)


def _channel_scratch(width, dtype, rows=STREAM_ROWS):
    buf = (2, rows, width)
    return [pltpu.VMEM(buf, dtype), pltpu.VMEM(buf, dtype), *([pltpu.SemaphoreType.DMA((2,))] * 5),
            pltpu.SemaphoreType.REGULAR((2,))]


CHANNEL_REFS = 8


def _copy_blocks(srcs, dsts, ch):
    sbuf, _, ld, _, _, st, _, _ = ch
    n = len(srcs)
    load = lambda i: pltpu.make_async_copy(srcs[i], sbuf.at[i % 2], ld.at[i % 2])
    store = lambda i: pltpu.make_async_copy(sbuf.at[i % 2], dsts[i], st.at[i % 2])
    load(0).start()
    for i in range(n):
        if i + 1 < n:
            if i >= 1:
                store(i - 1).wait()
            load(i + 1).start()
        load(i).wait()
        store(i).start()
    for i in range(max(0, n - 2), n):
        store(i).wait()


def _exchange_block_streams(streams, sibling):
    plans = []
    for srcs, dsts, keeps, (sbuf, rbuf, ld, snd, rcv, st, kp, credit) in streams:
        n = len(srcs)

        def load(i, srcs=srcs, sbuf=sbuf, ld=ld):
            return pltpu.make_async_copy(srcs[i], sbuf.at[i % 2], ld.at[i % 2])

        def push(i, sbuf=sbuf, rbuf=rbuf, snd=snd, rcv=rcv):
            return _remote(sbuf.at[i % 2], rbuf.at[i % 2], snd.at[i % 2], rcv.at[i % 2], sibling)

        def store(i, rbuf=rbuf, dsts=dsts, st=st):
            return pltpu.make_async_copy(rbuf.at[i % 2], dsts[i], st.at[i % 2])

        def save(i, sbuf=sbuf, keeps=keeps, kp=kp):
            return pltpu.make_async_copy(sbuf.at[i % 2], keeps[i], kp.at[i % 2])

        def free_slot(i, n=n, store=store, credit=credit):
            if 1 <= i < n:
                store(i - 1).wait()
                if i + 1 < n:
                    pl.semaphore_signal(credit.at[(i + 1) % 2], 1, device_id=sibling, device_id_type=MESH)

        def send(i, n=n, load=load, push=push, save=save, keeps=keeps, credit=credit):
            if i < n:
                load(i).wait()
                pl.semaphore_wait(credit.at[i % 2], 1)
                push(i).start()
                if keeps[i] is not None:
                    save(i).start()

        def receive(i, n=n, load=load, push=push, store=store, save=save, keeps=keeps):
            if i < n:
                push(i).wait_recv()
                store(i).start()
                push(i).wait_send()
                if keeps[i] is not None:
                    save(i).wait()
                if i + 2 < n:
                    load(i + 2).start()

        for i in range(min(2, n)):
            pl.semaphore_signal(credit.at[i], 1, device_id=sibling, device_id_type=MESH)
            load(i).start()
        plans.append((n, free_slot, send, receive, store))
    for _, _, send, _, _ in plans:
        send(0)
    for i in range(max(p[0] for p in plans)):
        for _, free_slot, _, _, _ in plans:
            free_slot(i)
        for _, _, send, _, _ in plans:
            send(i + 1)
        for _, _, _, receive, _ in plans:
            receive(i)
    for n, _, _, _, store in plans:
        store(n - 1).wait()


def _all_gather_shards(shards, small, *, name):
    n = len(shards)

    def body(*refs):
        ins, outs = refs[:n + 1], refs[n + 1:2 * n + 2]
        scr = refs[2 * n + 2:]
        chans = [scr[CHANNEL_REFS * t:CHANNEL_REFS * (t + 1)] for t in range(n)]
        send_sems, recv_sems, small_sems = scr[CHANNEL_REFS * n:]
        x, y, c, _ = _place()
        me = 2 * x + y
        sibling = (x, y, 1 - c)
        near = (lax.rem(x + 1 - c, 2), lax.rem(y + c, 2))
        far = (lax.rem(x + c, 2), lax.rem(y + 1 - c, 2))
        k_near, k_far, k_diag = 2 * near[0] + near[1], 2 * far[0] + far[1], 3 - me
        targets = ((*near, c), (*far, c), (*far, c))
        arrives = (k_near, k_far, k_diag)
        streams_in = (k_far, k_near, k_diag)

        def ici(t, j, src, blk):
            return _remote(src, outs[t].at[blk, c], send_sems.at[3 * t + j], recv_sems.at[3 * t + j], targets[j])

        first = [ici(t, j, ins[t].at[c], me) for t in range(n + 1) for j in range(2)]
        for cp in first:
            cp.start()
        small_local = pltpu.make_async_copy(ins[n], outs[n].at[me], small_sems.at[6])
        small_local.start()
        for t in range(n):
            _copy_blocks([ins[t].at[h] for h in range(2)], [outs[t].at[me, h] for h in range(2)], chans[t])
        passed = []
        for j in range(3):
            for t in range(n + 1):
                landed = outs[t].at[arrives[j], c]
                ici(t, j, landed, arrives[j]).wait_recv()
                if j == 0:
                    fwd = ici(t, 2, landed, k_near)
                    fwd.start()
                    passed.append(fwd)
                if t < n:
                    _exchange_block_streams([([landed], [outs[t].at[streams_in[j], 1 - c]], [None], chans[t])], sibling)
                else:
                    fwd = _remote(landed, landed, small_sems.at[j], small_sems.at[3 + j], sibling)
                    fwd.start()
                    passed.append(fwd)
        for j in range(3):
            got = outs[n].at[streams_in[j], 1 - c]
            _remote(got, got, small_sems.at[j], small_sems.at[3 + j], sibling).wait_recv()
        for cp in first + passed:
            cp.wait_send()
        small_local.wait()

    scratch = []
    for s in shards:
        scratch += _channel_scratch(s.shape[2], s.dtype, rows=s.shape[1])
    return pl.pallas_call(
        body, in_specs=[ANY] * (n + 1), out_specs=[ANY] * (n + 1),
        out_shape=[SDS((N_CHIPS, *s.shape), s.dtype) for s in (*shards, small)],
        scratch_shapes=[*scratch, pltpu.SemaphoreType.DMA((3 * n + 3,)), pltpu.SemaphoreType.DMA((3 * n + 3,)),
                        pltpu.SemaphoreType.DMA((7,))],
        compiler_params=pltpu.CompilerParams(vmem_limit_bytes=VMEM_LIMIT), name=name)(*shards, small)


def _pair_reduce(stacks, *, name):
    n = len(stacks)
    per = 11

    def body(*refs):
        ins, outs, scr = refs[:n], refs[n:2 * n], refs[2 * n:]
        x, y, c, _ = _place()
        sibling = (x, y, 1 - c)
        streams = []
        for t in range(n):
            sraw, sbuf, rbuf, obuf, pbuf, ld_s, ld_o, snd, rcv, st, credit = scr[per * t:per * (t + 1)]
            steps = ins[t].shape[1] // STREAM_ROWS
            src, own, out = ins[t].at[1 - c], ins[t].at[c], outs[t]
            assert steps >= 2

            def load_s(i, slot, src=src, sraw=sraw, ld_s=ld_s):
                return pltpu.make_async_copy(src.at[_stream_rows(i)], sraw.at[slot], ld_s.at[slot])

            def load_o(i, slot, own=own, obuf=obuf, ld_o=ld_o):
                return pltpu.make_async_copy(own.at[_stream_rows(i)], obuf.at[slot], ld_o.at[slot])

            def push(slot, sbuf=sbuf, rbuf=rbuf, snd=snd, rcv=rcv):
                return _remote(sbuf.at[slot], rbuf.at[slot], snd.at[slot], rcv.at[slot], sibling)

            def store(i, slot, pbuf=pbuf, out=out, st=st):
                return pltpu.make_async_copy(pbuf.at[slot], out.at[_stream_rows(i)], st.at[slot])

            def send(i, slot, load_s=load_s, push=push, sraw=sraw, sbuf=sbuf, credit=credit):
                load_s(i, slot).wait()
                sbuf[slot] = sraw[slot].astype(sbuf.dtype)
                pl.semaphore_wait(credit.at[slot], 1)
                push(slot).start()

            def combine(i, slot, load_s=load_s, load_o=load_o, push=push, store=store, rbuf=rbuf, obuf=obuf, pbuf=pbuf,
                        credit=credit, steps=steps):
                load_o(i, slot).wait()
                push(slot).wait_recv()

                @pl.when(i >= 2)
                def _():
                    store(i, slot).wait()

                pbuf[slot] = (obuf[slot] + rbuf[slot].astype(F32)).astype(pbuf.dtype)
                store(i, slot).start()
                push(slot).wait_send()

                @pl.when(i + 2 < steps)
                def _():
                    load_s(i + 2, slot).start()
                    load_o(i + 2, slot).start()
                    pl.semaphore_signal(credit.at[slot], 1, device_id=sibling, device_id_type=MESH)

            for slot in range(2):
                pl.semaphore_signal(credit.at[slot], 1, device_id=sibling, device_id_type=MESH)
                load_s(slot, slot).start()
                load_o(slot, slot).start()
            streams.append((steps, send, combine, store))
        for _, send, _, _ in streams:
            send(0, 0)

        def step(i, carry):
            slot = lax.rem(i, 2)
            for steps, send, _, _ in streams:
                @pl.when(i + 1 < steps)
                def _(send=send):
                    send(i + 1, 1 - slot)
            for steps, _, combine, _ in streams:
                @pl.when(i < steps)
                def _(combine=combine):
                    combine(i, slot)
            return carry

        lax.fori_loop(0, max(s[0] for s in streams), step, 0)
        for _, _, _, store in streams:
            for slot in range(2):
                store(0, slot).wait()

    scratch = []
    for s in stacks:
        buf = (2, STREAM_ROWS, s.shape[2])
        scratch += [pltpu.VMEM(buf, F32), pltpu.VMEM(buf, BF16), pltpu.VMEM(buf, BF16), pltpu.VMEM(buf, F32),
                    pltpu.VMEM(buf, BF16), *([pltpu.SemaphoreType.DMA((2,))] * 5), pltpu.SemaphoreType.REGULAR((2,))]
    return pl.pallas_call(
        body, in_specs=[ANY] * n, out_specs=[ANY] * n, out_shape=[SDS(s.shape[1:], BF16) for s in stacks],
        scratch_shapes=scratch, compiler_params=pltpu.CompilerParams(vmem_limit_bytes=VMEM_LIMIT), name=name)(*stacks)


HBM_SPEC = pl.BlockSpec(memory_space=pltpu.HBM)
SEM_SPEC = pl.BlockSpec(memory_space=pltpu.SEMAPHORE)
SIDE_EFFECT = pltpu.SideEffectType.DATAFLOW_SIDE_EFFECTING


def _scatter_copies(ins, lands, send_sems, recv_sems):
    _, _, c, chips = _place()
    return [_remote(ins[t].at[2 * cx + cy], lands[t].at[j], send_sems.at[3 * t + j], recv_sems.at[3 * t + j],
                    (cx, cy, c)) for t in range(len(ins)) for j, (cx, cy) in enumerate(chips)]


def _chip_scatter_start(parts, *, name):
    n = len(parts)

    def body(*refs):
        ins, lands = refs[:n], refs[n:2 * n]
        send_sems, recv_sems, token = refs[2 * n], refs[2 * n + 1], refs[-1]
        for cp in _scatter_copies(ins, lands, send_sems, recv_sems):
            cp.start()
        token[...] = jnp.zeros_like(token)

    hbm = lambda a: pltpu.with_memory_space_constraint(a, pltpu.HBM)
    lands = [hbm(lax.empty((3, *p.shape[1:]), p.dtype)) for p in parts]
    thru = [pltpu.HBM(a.shape, a.dtype) for a in (*parts, *lands)]
    outs = pl.pallas_call(
        body, name=name,
        out_shape=(pltpu.SemaphoreType.DMA((3 * n,)), pltpu.SemaphoreType.DMA((3 * n,)), *thru, SDS((8, 128), F32)),
        in_specs=[HBM_SPEC] * (2 * n),
        out_specs=(SEM_SPEC, SEM_SPEC, *([HBM_SPEC] * (2 * n)), pl.BlockSpec(memory_space=pltpu.VMEM)),
        input_output_aliases={i: 2 + i for i in range(2 * n)},
        compiler_params=pltpu.CompilerParams(has_side_effects=SIDE_EFFECT),
    )(*[hbm(p) for p in parts], *lands)
    return outs[0], outs[1], outs[2:2 + n], outs[2 + n:2 + 2 * n], outs[-1]


def _chip_scatter_wait(send_sems, recv_sems, parts, lands, after, *, name):
    n = len(parts)

    def body(*refs):
        ins, lands_in = refs[:n], refs[n:2 * n]
        for cp in _scatter_copies(ins, lands_in, refs[2 * n], refs[2 * n + 1]):
            cp.wait_send()
            cp.wait_recv()

    outs = pl.pallas_call(
        body, name=name, out_shape=[pltpu.HBM(a.shape, a.dtype) for a in (*parts, *lands)],
        in_specs=[*([HBM_SPEC] * (2 * n)), SEM_SPEC, SEM_SPEC, *([ANY] * len(after))],
        out_specs=[HBM_SPEC] * (2 * n), input_output_aliases={i: i for i in range(2 * n)},
        compiler_params=pltpu.CompilerParams(has_side_effects=SIDE_EFFECT),
    )(*parts, *lands, send_sems, recv_sems, *after)
    return outs[:n], outs[n:]


def _gather_copies(shards, zones, send_sems, recv_sems):
    x, y, c, chips = _place()
    return [_remote(shards[t].at[c], zones[t].at[2 * x + y, c], send_sems.at[3 * t + j], recv_sems.at[3 * t + j],
                    (cx, cy, c)) for t in range(len(shards)) for j, (cx, cy) in enumerate(chips)]


def _gather_start(shards, after, *, name):
    n = len(shards)

    def body(*refs):
        ins, zones = refs[:n], refs[n:2 * n]
        send_sems, recv_sems, token = refs[2 * n + len(after)], refs[2 * n + len(after) + 1], refs[-1]
        for cp in _gather_copies(ins, zones, send_sems, recv_sems):
            cp.start()
        token[...] = jnp.zeros_like(token)

    hbm = lambda a: pltpu.with_memory_space_constraint(a, pltpu.HBM)
    zones = [hbm(lax.empty((N_CHIPS, *s.shape), s.dtype)) for s in shards]
    thru = [pltpu.HBM(a.shape, a.dtype) for a in (*shards, *zones)]
    outs = pl.pallas_call(
        body, name=name,
        out_shape=(pltpu.SemaphoreType.DMA((3 * n,)), pltpu.SemaphoreType.DMA((3 * n,)), *thru, SDS((8, 128), F32)),
        in_specs=[*([HBM_SPEC] * (2 * n)), *([ANY] * len(after))],
        out_specs=(SEM_SPEC, SEM_SPEC, *([HBM_SPEC] * (2 * n)), pl.BlockSpec(memory_space=pltpu.VMEM)),
        input_output_aliases={i: 2 + i for i in range(2 * n)},
        compiler_params=pltpu.CompilerParams(has_side_effects=SIDE_EFFECT),
    )(*[hbm(s) for s in shards], *zones, *after)
    return outs[0], outs[1], outs[2:2 + n], outs[2 + n:2 + 2 * n], outs[-1]


def _gather_wait(send_sems, recv_sems, shards, zones, after, *, name):
    n = len(shards)

    def body(*refs):
        for cp in _gather_copies(refs[:n], refs[n:2 * n], refs[2 * n], refs[2 * n + 1]):
            cp.wait_send()
            cp.wait_recv()

    outs = pl.pallas_call(
        body, name=name, out_shape=[pltpu.HBM(a.shape, a.dtype) for a in (*shards, *zones)],
        in_specs=[*([HBM_SPEC] * (2 * n)), SEM_SPEC, SEM_SPEC, *([ANY] * len(after))],
        out_specs=[HBM_SPEC] * (2 * n), input_output_aliases={i: i for i in range(2 * n)},
        compiler_params=pltpu.CompilerParams(has_side_effects=SIDE_EFFECT),
    )(*shards, *zones, send_sems, recv_sems, *after)
    return outs[:n], outs[n:]


def _gather_finish(shards, zones, *, name):
    n = len(shards)

    def body(*refs):
        ins, zones_in, outs, scr = refs[:n], refs[n:2 * n], refs[2 * n:3 * n], refs[3 * n:]
        x, y, c, chips = _place()
        me = 2 * x + y
        sibling = (x, y, 1 - c)
        others = [2 * cx + cy for cx, cy in chips]
        chans = [scr[CHANNEL_REFS * t:CHANNEL_REFS * (t + 1)] for t in range(n)]
        for t in range(n):
            _copy_blocks([ins[t].at[h] for h in range(2)], [outs[t].at[me, h] for h in range(2)], chans[t])
        _exchange_block_streams([([zones_in[t].at[k, c] for k in others], [outs[t].at[k, 1 - c] for k in others],
                                  [None] * len(others), chans[t]) for t in range(n)], sibling)

    scratch = []
    for s in shards:
        scratch += _channel_scratch(s.shape[2], s.dtype, rows=s.shape[1])
    return pl.pallas_call(
        body, in_specs=[ANY] * (2 * n), out_specs=[ANY] * n, out_shape=[SDS(z.shape, z.dtype) for z in zones],
        input_output_aliases={n + t: t for t in range(n)}, scratch_shapes=scratch,
        compiler_params=pltpu.CompilerParams(vmem_limit_bytes=VMEM_LIMIT), name=name)(*shards, *zones)


def _pair_share(groups, *, name):
    finals = [f for grp in groups for f in grp]
    n, n_out = len(finals), len(groups)

    def body(*refs):
        ins, outs, scr = refs[:n], refs[n:n + n_out], refs[n + n_out:]
        x, y, c, _ = _place()
        sibling = (x, y, 1 - c)
        t, streams = 0, []
        for o, grp in enumerate(groups):
            rows = grp[0].shape[0] // 2
            blocks = [(layer, pl.ds(b * rows, rows)) for layer in range(len(grp)) for b in range(2)]
            streams.append(([ins[t + layer].at[rs] for layer, rs in blocks],
                            [outs[o].at[layer, 1 - c, rs] for layer, rs in blocks],
                            [outs[o].at[layer, c, rs] for layer, rs in blocks],
                            scr[CHANNEL_REFS * o:CHANNEL_REFS * (o + 1)]))
            t += len(grp)
        _exchange_block_streams(streams, sibling)

    scratch = []
    for grp in groups:
        scratch += _channel_scratch(grp[0].shape[1], grp[0].dtype, rows=grp[0].shape[0] // 2)
    return pl.pallas_call(
        body, in_specs=[ANY] * n, out_specs=[ANY] * n_out,
        out_shape=[SDS((len(grp), 2, *grp[0].shape), grp[0].dtype) for grp in groups],
        scratch_shapes=scratch, compiler_params=pltpu.CompilerParams(vmem_limit_bytes=VMEM_LIMIT), name=name)(*finals)


def _all_reduce_small(v, *, name):
    rows, lanes = v.shape
    n_dev = 8

    def body(v_ref, o_ref, all_ref, send_sems, recv_sems, local_sem):
        x, y, c, chips = _place()
        me, sibling = (x, y, c), (x, y, 1 - c)

        def block(px, py, pc):
            return all_ref.at[4 * px + 2 * py + pc]

        def copy(k, blk, to, src=None):
            return _remote(block(*blk) if src is None else src, block(*blk), send_sems.at[k], recv_sems.at[k], to)

        mine = pltpu.make_async_copy(v_ref, block(*me), local_sem)
        mine.start()
        first = [copy(0, me, sibling, src=v_ref)]
        first += [copy(1 + j, me, (*chip, c), src=v_ref) for j, chip in enumerate(chips)]
        for cp in first:
            cp.start()
        passed = [copy(4 + j, (*chip, c), sibling) for j, chip in enumerate(chips)]
        for j, chip in enumerate(chips):
            copy(1 + j, (*chip, c), me).wait_recv()
            passed[j].start()
        copy(0, sibling, me).wait_recv()
        for j, chip in enumerate(chips):
            copy(4 + j, (*chip, 1 - c), me).wait_recv()
        for cp in first + passed:
            cp.wait_send()
        mine.wait()
        acc = all_ref[0]
        for k in range(1, n_dev):
            acc = acc + all_ref[k]
        o_ref[...] = acc

    vmem = pl.BlockSpec(memory_space=pltpu.VMEM)
    return pl.pallas_call(
        body, in_specs=[vmem], out_specs=vmem, out_shape=SDS((rows, lanes), F32),
        scratch_shapes=[pltpu.VMEM((n_dev, rows, lanes), F32), pltpu.SemaphoreType.DMA((7,)),
                        pltpu.SemaphoreType.DMA((7,)), pltpu.SemaphoreType.DMA],
        compiler_params=pltpu.CompilerParams(vmem_limit_bytes=VMEM_LIMIT), name=name)(v)


def _relu2_epilogue(acc):
    return acc, jnp.square(jnp.maximum(acc, 0.0))


def _res_epilogue(acc, res):
    return (acc + res,)


def _drelu2_epilogue(acc, pre):
    return (acc * (2.0 * jnp.maximum(pre.astype(F32), 0.0)),)


def _ffn_fwd(h, g, w1, w2, tag):
    f = _rms_fwd(h, g, name=f"ffn_norm_{tag}")
    pre, act = _mm_nn(f, w1, name=f"ffn1_{tag}", epilogue=_relu2_epilogue, n_out_dtypes=(BF16, BF16))
    h_out = _mm_nn(act, w2, name=f"ffn2_{tag}", extras=(h,), epilogue=_res_epilogue)
    return h_out, (f, pre, act)


def _ffn_bwd(dh, h, g, w1, w2, saved, layer, after=()):
    f, pre, act = saved
    dpre = _mm_nt(dh, w2, name=f"ffn2_dx_{layer}", out_dtype=BF16, extras=(pre,), epilogue=_drelu2_epilogue,
                  after=after)
    dw2 = _mm_tn_stacked(act, dh, name=f"ffn2_dw_{layer}", col_slots=False)
    df = _mm_nt(dpre, w1, name=f"ffn1_dx_{layer}")
    dw1 = _mm_tn_stacked(f, dpre, name=f"ffn1_dw_{layer}", col_slots=True)
    dh, dg = _rms_bwd(h, g, df, dh, name=f"ffn_norm_bwd_{layer}")
    return dh, dg, dw1, dw2


def _kv_fwd(mem, g, w_kv, tag):
    m = _rms_fwd(mem, g, name=f"mem_norm_{tag}")
    return m, _mm_nn(m, w_kv, name=f"kv_{tag}")


def _kv_bwd(mem, g, w_kv, m, dk, dv, layer):
    dkv = jnp.concatenate([dk, dv], axis=1)
    dw = _mm_tn_stacked(m, dkv, name=f"kv_dw_{layer}", col_slots=True)
    dm = _mm_nt(dkv, w_kv, name=f"kv_dx_{layer}")
    _, dg = _rms_bwd(mem, g, dm, dm, name=f"mem_norm_bwd_{layer}")
    return dw, dg


def _local_step(x, mem, target, p, after_layer1=None, after_ffn0=None, after_mixer0=None):
    row = lambda v: v.reshape(1, -1)
    g = {}

    h0 = x
    a0 = _rms_fwd(h0, row(p["norm_mix"][0]), name="mix_norm_0")
    proj_a = _mm_nn(a0, p["a_in"], name="a_in", after=p.get("after_start", ()))
    m0, kv0 = _kv_fwd(mem, row(p["mem_norm"][0]), p["w_kv"][0], "0")
    cat0 = _attn_fwd(proj_a, 2 * D_INNER, kv0, name="attn_0")
    bs_col = p["a_bs"].reshape(A_GROUPS, CHUNK, 1)
    cat0 = _gate_fwd(proj_a, p["a_ln_g"], p["a_ln_b"], p["a_ws"], bs_col, cat0, name="gate")
    h1 = _mm_nn(cat0, p["w_out"][0], name="out_0", extras=(h0,), epilogue=_res_epilogue)
    w_ffn1_0, w_ffn2_0 = p["layer0_ffn"](h1) if "layer0_ffn" in p else (p["w_ffn1"][0], p["w_ffn2"][0])
    h2, ffn0 = _ffn_fwd(h1, row(p["norm_ffn"][0]), w_ffn1_0, w_ffn2_0, "0")

    if "layer1_mixer" in p:
        w_kv1, w_out1, b_in = p["layer1_mixer"](h2)
    else:
        w_kv1, w_out1, b_in = p["w_kv"][1], p["w_out"][1], p["b_in"]
    a1 = _rms_fwd(h2, row(p["norm_mix"][1]), name="mix_norm_1")
    proj_b = _mm_nn(a1, b_in, name="b_in")
    m1, kv1 = _kv_fwd(mem, row(p["mem_norm"][1]), w_kv1, "1")
    cat1 = _attn_fwd(proj_b, B_Q_OFF, kv1, name="attn_1")
    xbc = _conv_fwd(proj_b, p["b_conv_w"], p["b_conv_b"], name="conv")
    dt_raw = proj_b[:, B_DT_OFF:B_DT_OFF + SSM_HEADS].reshape(SEQ, SSM_GROUPS, SSM_HPG)
    dt_c = jnp.transpose(dt_raw, (1, 0, 2))
    dt_r = jnp.transpose(dt_raw, (1, 2, 0))
    per_head = lambda v: v.reshape(SSM_GROUPS, 1, SSM_HPG)
    par_row = jnp.concatenate([per_head(p["b_dt_bias"]), per_head(p["b_a_log"]), per_head(p["b_d"])], axis=1)
    ssd_par = (par_row, jnp.transpose(par_row[:, :2], (0, 2, 1)), p["b_gnorm"])
    cat1, hprev = _ssd_fwd(xbc, proj_b, dt_c, dt_r, *ssd_par, cat1, name="ssd")
    h3 = _mm_nn(cat1, w_out1, name="out_1", extras=(h2,), epilogue=_res_epilogue)
    w_ffn1_1, w_ffn2_1 = p["layer1_ffn"](h3) if "layer1_ffn" in p else (p["w_ffn1"][1], p["w_ffn2"][1])
    h4, ffn1 = _ffn_fwd(h3, row(p["norm_ffn"][1]), w_ffn1_1, w_ffn2_1, "1")

    loss, dh, g["final_norm"] = _loss_head(h4, row(p["final_norm"]), target, name="loss_head")

    dh, dnf1, dw1_1, dw2_1 = _ffn_bwd(dh, h3, row(p["norm_ffn"][1]), w_ffn1_1, w_ffn2_1, ffn1, 1)
    dcat1 = _mm_nt(dh, w_out1, name="out_dx_1")
    dwo_1 = _mm_tn_stacked(cat1, dh, name="out_dw_1", col_slots=False)
    dproj_b, dk1, dv1 = _attn_bwd(proj_b, B_Q_OFF, kv1, dcat1, B_IN_PAD, B_Q_OFF, name="attn_bwd_1")
    dproj_b, dxs, dbm, dcm, ddt_c, ddt_r, dpar_row, dpar_col, g["b_gnorm"] = _ssd_bwd(
        xbc, proj_b, dt_c, dt_r, *ssd_par, hprev, dcat1, dproj_b, name="ssd_bwd")
    dpar = dpar_row.at[:, :2].add(jnp.transpose(dpar_col, (0, 2, 1)))
    g["b_dt_bias"], g["b_a_log"], g["b_d"] = dpar[:, 0], dpar[:, 1], dpar[:, 2]
    dproj_b, g["b_conv_w"], g["b_conv_b"] = _conv_bwd(proj_b, p["b_conv_w"], p["b_conv_b"], dxs, dbm, dcm, dproj_b,
                                                      name="conv_bwd")
    ddt = jnp.transpose(ddt_c, (1, 0, 2)) + jnp.transpose(ddt_r, (2, 0, 1))
    ddt = jnp.pad(ddt.reshape(SEQ, SSM_HEADS), ((0, 0), (0, B_IN_PAD - B_DT_OFF - SSM_HEADS))).astype(BF16)
    dproj_b = lax.dynamic_update_slice(dproj_b, ddt, (0, B_DT_OFF))
    dwkv_1, dmn1 = _kv_bwd(mem, row(p["mem_norm"][1]), w_kv1, m1, dk1, dv1, 1)
    dwb = _b_in_grad_slots(_mm_tn(a1, dproj_b, name="b_in_dw"))
    da1 = _mm_nt(dproj_b, b_in, name="b_in_dx")
    dh, dnm1 = _rms_bwd(h2, row(p["norm_mix"][1]), da1, dh, name="mix_norm_bwd_1")
    layer1 = dict(w_kv=dwkv_1, w_out=dwo_1, w_ffn1=dw1_1, w_ffn2=dw2_1, b_in=dwb)
    token = () if after_layer1 is None else (after_layer1(layer1),)

    dh, dnf0, dw1_0, dw2_0 = _ffn_bwd(dh, h1, row(p["norm_ffn"][0]), w_ffn1_0, w_ffn2_0, ffn0, 0,
                                      after=token)
    ffn0_grads = dict(w_ffn1=dw1_0, w_ffn2=dw2_0)
    token = () if after_ffn0 is None else (after_ffn0(ffn0_grads),)
    dcat0 = _mm_nt(dh, p["w_out"][0], name="out_dx_0", after=token)
    dwo_0 = _mm_tn_stacked(cat0, dh, name="out_dw_0", col_slots=False)
    dproj_a, dk0, dv0 = _attn_bwd(proj_a, 2 * D_INNER, kv0, dcat0, A_IN, 2 * D_INNER, name="attn_bwd_0")
    dproj_a, g["a_ln_g"], g["a_ln_b"], g["a_ws"], dbs_col = _gate_bwd(
        proj_a, p["a_ln_g"], p["a_ln_b"], p["a_ws"], bs_col, dcat0, dproj_a, name="gate_bwd")
    g["a_bs"] = dbs_col.reshape(A_GROUPS, CHUNK)
    dwkv_0, dmn0 = _kv_bwd(mem, row(p["mem_norm"][0]), p["w_kv"][0], m0, dk0, dv0, 0)
    dwa = _mm_tn_stacked(a0, dproj_a, name="a_in_dw", col_slots=True)
    mixer0_grads = dict(w_kv=dwkv_0, w_out=dwo_0, a_in=dwa)
    token = () if after_mixer0 is None else (after_mixer0(mixer0_grads),)
    da0 = _mm_nt(dproj_a, p["a_in"], name="a_in_dx", after=token)
    dx, dnm0 = _rms_bwd(h0, row(p["norm_mix"][0]), da0, dh, name="mix_norm_bwd_0")

    g["norm_mix"] = jnp.concatenate([dnm0, dnm1], axis=0)
    g["norm_ffn"] = jnp.concatenate([dnf0, dnf1], axis=0)
    g["mem_norm"] = jnp.concatenate([dmn0, dmn1], axis=0)
    layer0 = dict(w_kv=dwkv_0, w_out=dwo_0, w_ffn1=dw1_0, w_ffn2=dw2_0, a_in=dwa)
    return loss, dx, g, layer0, layer1


def _b_in_full(gathered):
    n = B_IN // N_CHIPS
    dt0 = D_INNER + CONV_DIM - (N_CHIPS - 1) * n
    last = gathered[N_CHIPS - 1]
    return jnp.concatenate([*[gathered[k] for k in range(N_CHIPS - 1)], last[:, :dt0], last[:, dt0 + SSM_HEADS:],
                            last[:, dt0:dt0 + SSM_HEADS], jnp.zeros((D_MODEL, B_IN_PAD - B_IN), last.dtype)], axis=1)


def _b_in_grad_slots(d):
    n = B_IN // N_CHIPS
    dt0 = D_INNER + CONV_DIM
    last = jnp.concatenate([d[:, (N_CHIPS - 1) * n:dt0], d[:, B_DT_OFF:B_DT_OFF + SSM_HEADS], d[:, dt0:B_DT_OFF]], axis=1)
    slots = [*[d[:, k * n:(k + 1) * n] for k in range(N_CHIPS - 1)], last]
    half = D_MODEL // 2
    return jnp.stack([jnp.stack([s[h * half:(h + 1) * half] for s in slots]) for h in range(2)])


SMALL_REPL = ("norm_mix", "norm_ffn", "mem_norm", "a_ln_g", "a_ln_b", "a_ws", "a_bs", "b_dt_bias", "b_a_log", "b_d",
              "final_norm")
SMALL_SHARD = ("b_conv_w", "b_conv_b", "b_gnorm")
WEIGHTS = ("norm_mix", "norm_ffn", "mem_norm", "w_kv", "w_out", "w_ffn1", "w_ffn2", "a_in", "a_ln_g", "a_ln_b", "a_ws",
           "a_bs", "b_in", "b_conv_w", "b_conv_b", "b_dt_bias", "b_a_log", "b_d", "b_gnorm", "final_norm")
CONV_SHARD = CONV_DIM // N_CHIPS
GN_SHARD = D_INNER // N_CHIPS


LAYERED = ("w_kv", "w_out", "w_ffn1", "w_ffn2")


def _gather_weights(w):
    halves = lambda k, layer: (w[k][layer] if k in LAYERED else w[k][0]).reshape(2, -1, w[k].shape[-1]).astype(BF16)
    small = jnp.zeros((2, CONV_K, CONV_SHARD), F32)
    small = small.at[0].set(w["b_conv_w"][0])
    small = small.at[1, 0].set(w["b_conv_b"][0])
    small = small.at[1, 1, :GN_SHARD].set(w["b_gnorm"][0])
    first_names = ("w_kv", "w_out", "a_in")
    gathered = _all_gather_shards([halves(k, 0) for k in first_names], small, name="gather_weights_0")
    got = dict(zip(first_names, gathered))
    slots = lambda a: a.reshape(N_CHIPS, -1, a.shape[-1])
    rows = lambda a: a.reshape(-1, a.shape[-1])
    p = dict(w_kv=[slots(got["w_kv"])], w_out=[rows(got["w_out"])], a_in=slots(got["a_in"]))
    sm = gathered[-1]
    p["b_conv_w"] = jnp.transpose(sm[:, 0], (1, 0, 2)).reshape(CONV_K, CONV_DIM)
    p["b_conv_b"] = sm[:, 1, 0].reshape(1, CONV_DIM)
    p["b_gnorm"] = sm[:, 1, 1, :GN_SHARD].reshape(1, D_INNER)

    after, started = (gathered[0],), {}
    for tag, layer, names in (("0_ffn", 0, ("w_ffn1", "w_ffn2")), ("1_mixer", 1, ("w_kv", "w_out", "b_in")),
                              ("1_ffn", 1, ("w_ffn1", "w_ffn2"))):
        started[tag] = _gather_start([halves(k, layer) for k in names], after, name=f"gather_start_{tag}")
        after = (started[tag][-1],)
    p["after_start"] = after

    def finish(tag, first):
        send_sems, recv_sems, shards, zones, _ = started[tag]
        shards, zones = _gather_wait(send_sems, recv_sems, shards, zones, (first,), name=f"gather_wait_{tag}")
        return _gather_finish(shards, zones, name=f"gather_finish_{tag}")

    def ffn(tag):
        def weights(first):
            w1, w2 = finish(tag, first)
            return slots(w1), rows(w2)
        return weights

    def layer1_mixer(first):
        kv, wo, b_in = finish("1_mixer", first)
        return slots(kv), rows(wo), _b_in_full(slots(b_in))

    p.update(layer0_ffn=ffn("0_ffn"), layer1_mixer=layer1_mixer, layer1_ffn=ffn("1_ffn"))
    return p


def _pair_parts(grads, tag):
    stacks = [g.reshape(2, -1, g.shape[-1]) for g in grads.values()]
    parts = _pair_reduce(stacks, name=f"grads_pair_reduce_{tag}")
    return [t.reshape(N_CHIPS, -1, t.shape[-1]) for t in parts]


def _chip_sums(chip, names, parts, landed, tag):
    return {k: _sum_contributions(chip, t, u, name=f"grads_chip_sum_{k}_{tag}")
            for k, t, u in zip(names, parts, landed)}


def _small_layout(shapes):
    offs, o = {}, 0
    for k in (*SMALL_REPL, *SMALL_SHARD):
        size = math.prod(shapes[k])
        offs[k] = (o, size)
        o += size
    rows = -(-(o + 1) // (8 * 128)) * 8
    return offs, rows


def _reduce_small(g, loss_part, full_shapes):
    offs, rows = _small_layout(full_shapes)
    flat = jnp.concatenate([*[g[k].reshape(-1) for k in (*SMALL_REPL, *SMALL_SHARD)], loss_part[0, :1]])
    flat = jnp.pad(flat, (0, rows * 128 - flat.shape[0])).reshape(rows, 128)
    total = _all_reduce_small(flat, name="small_all_reduce").reshape(-1)
    end = max(o + n for o, n in offs.values())
    return {k: total[o:o + n].reshape(full_shapes[k]) for k, (o, n) in offs.items()}, total[end]


def kernel(x, mem, norm_mix, norm_ffn, mem_norm, w_kv, w_out, w_ffn1, w_ffn2, a_in, a_ln_g, a_ln_b, a_ws, a_bs, b_in, b_conv_w, b_conv_b, b_dt_bias, b_a_log, b_d, b_gnorm, final_norm, loss_target, m_norm_mix, m_norm_ffn, m_mem_norm, m_w_kv, m_w_out, m_w_ffn1, m_w_ffn2, m_a_in, m_a_ln_g, m_a_ln_b, m_a_ws, m_a_bs, m_b_in, m_b_conv_w, m_b_conv_b, m_b_dt_bias, m_b_a_log, m_b_d, m_b_gnorm, m_final_norm, v_norm_mix, v_norm_ffn, v_mem_norm, v_w_kv, v_w_out, v_w_ffn1, v_w_ffn2, v_a_in, v_a_ln_g, v_a_ln_b, v_a_ws, v_a_bs, v_b_in, v_b_conv_w, v_b_conv_b, v_b_dt_bias, v_b_a_log, v_b_d, v_b_gnorm, v_final_norm):
    w = dict(norm_mix=norm_mix, norm_ffn=norm_ffn, mem_norm=mem_norm, w_kv=w_kv, w_out=w_out, w_ffn1=w_ffn1,
             w_ffn2=w_ffn2, a_in=a_in, a_ln_g=a_ln_g, a_ln_b=a_ln_b, a_ws=a_ws, a_bs=a_bs, b_in=b_in, b_conv_w=b_conv_w,
             b_conv_b=b_conv_b, b_dt_bias=b_dt_bias, b_a_log=b_a_log, b_d=b_d, b_gnorm=b_gnorm, final_norm=final_norm)
    mom = dict(norm_mix=m_norm_mix, norm_ffn=m_norm_ffn, mem_norm=m_mem_norm, w_kv=m_w_kv, w_out=m_w_out,
               w_ffn1=m_w_ffn1, w_ffn2=m_w_ffn2, a_in=m_a_in, a_ln_g=m_a_ln_g, a_ln_b=m_a_ln_b, a_ws=m_a_ws,
               a_bs=m_a_bs, b_in=m_b_in, b_conv_w=m_b_conv_w, b_conv_b=m_b_conv_b, b_dt_bias=m_b_dt_bias,
               b_a_log=m_b_a_log, b_d=m_b_d, b_gnorm=m_b_gnorm, final_norm=m_final_norm)
    var = dict(norm_mix=v_norm_mix, norm_ffn=v_norm_ffn, mem_norm=v_mem_norm, w_kv=v_w_kv, w_out=v_w_out,
               w_ffn1=v_w_ffn1, w_ffn2=v_w_ffn2, a_in=v_a_in, a_ln_g=v_a_ln_g, a_ln_b=v_a_ln_b, a_ws=v_a_ws,
               a_bs=v_a_bs, b_in=v_b_in, b_conv_w=v_b_conv_w, b_conv_b=v_b_conv_b, b_dt_bias=v_b_dt_bias,
               b_a_log=v_b_a_log, b_d=v_b_d, b_gnorm=v_b_gnorm, final_norm=v_final_norm)

    p = _gather_weights(w)
    p.update(norm_mix=norm_mix, norm_ffn=norm_ffn, mem_norm=mem_norm, a_ln_g=a_ln_g, a_ln_b=a_ln_b, a_ws=a_ws[0],
             a_bs=a_bs[0], b_dt_bias=b_dt_bias, b_a_log=b_a_log, b_d=b_d, final_norm=final_norm)
    chip = 2 * lax.axis_index("x") + lax.axis_index("y")
    chip_arr = jnp.reshape(chip, (1,)).astype(jnp.int32)
    started = {}

    def start_scatter(tag):
        def hook(grads):
            start = _chip_scatter_start(_pair_parts(grads, tag), name=f"grads_chip_scatter_start_{tag}")
            started[tag] = (tuple(grads), start)
            return start[-1]
        return hook

    loss_part, dx, g, _, _ = _local_step(x[0], mem[0], loss_target[0], p, start_scatter("1"), start_scatter("0f"),
                                         start_scatter("0m"))
    full_shapes = {k: w[k].shape for k in SMALL_REPL}
    full_shapes.update(b_conv_w=(1, CONV_K, CONV_DIM), b_conv_b=(1, CONV_DIM), b_gnorm=(1, D_INNER))
    grads, loss = _reduce_small(g, loss_part, full_shapes)
    grads["b_conv_w"] = lax.dynamic_slice_in_dim(grads["b_conv_w"], chip * CONV_SHARD, CONV_SHARD, axis=2)
    grads["b_conv_b"] = lax.dynamic_slice_in_dim(grads["b_conv_b"], chip * CONV_SHARD, CONV_SHARD, axis=1)
    grads["b_gnorm"] = lax.dynamic_slice_in_dim(grads["b_gnorm"], chip * GN_SHARD, GN_SHARD, axis=1)

    def finish_scatter(tag, *first):
        names, (send_sems, recv_sems, parts, lands, _) = started[tag]
        parts, landed = _chip_scatter_wait(send_sems, recv_sems, parts, lands, first,
                                           name=f"grads_chip_scatter_wait_{tag}")
        return _chip_sums(chip_arr, names, parts, landed, tag)

    def adamw(names, grads):
        for k in names:
            shape = w[k].shape
            if len(shape) == 3 and shape[2] % 128 and not shape[1] % 128:
                flat = unflat = lambda a: jnp.transpose(a, (0, 2, 1))
            else:
                flat = (lambda a: a) if len(shape) == 3 else (lambda a: a.reshape(1, -1, shape[-1]))
                unflat = lambda a: a.reshape(shape)
            d, m_new, v_new = _adamw(flat(w[k]), flat(grads[k]), flat(mom[k]), flat(var[k]), name=f"adamw_{k}")
            delta[k], new_m[k], new_v[k] = unflat(d), unflat(m_new), unflat(v_new)

    delta, new_m, new_v = {}, {}, {}
    halves = [finish_scatter("0f", dx), finish_scatter("1", dx)]
    early = ("w_ffn1", "w_ffn2", "b_in")
    shared = _pair_share([[halves[layer][k] for layer in range(2) if k in halves[layer]] for k in early],
                         name="grads_pair_share_early")
    grads.update({k: a.reshape(w[k].shape) for k, a in zip(early, shared)})
    adamw([k for k in WEIGHTS if k in grads], grads)
    halves[0].update(finish_scatter("0m", delta["w_ffn2"]))
    late = ("w_kv", "w_out", "a_in")
    shared = _pair_share([[halves[layer][k] for layer in range(2) if k in halves[layer]] for k in late],
                         name="grads_pair_share_late")
    grads.update({k: a.reshape(w[k].shape) for k, a in zip(late, shared)})
    adamw(late, grads)

    return (loss, dx.reshape(x.shape), *[grads[k] for k in WEIGHTS], *[delta[k] for k in WEIGHTS],
            *[new_m[k] for k in WEIGHTS], *[new_v[k] for k in WEIGHTS])
```

```python
import math

import jax
import jax.numpy as jnp
from jax import lax
from jax.experimental import pallas as pl
from jax.experimental.pallas import tpu as pltpu

F32 = jnp.float32
BF16 = jnp.bfloat16
SDS = jax.ShapeDtypeStruct

D_MODEL = 1024
SEQ = 2048
CHUNK = 128
N_MEM = 256
D_INNER = 2048
A_GROUPS = 8
A_GROUP_W = D_INNER // A_GROUPS
SSM_HEADS = 32
SSM_HEAD_DIM = 64
SSM_GROUPS = 4
SSM_HPG = 8
SSM_STATE = 128
SSM_GROUP_W = SSM_HPG * SSM_HEAD_DIM
CONV_K = 4
CONV_DIM = 3072
X_HEADS = 4
X_HEAD_DIM = 256
X_WIDTH = 1024
MIX_OUT = 3072
D_FF = 4096
A_IN = 5120
B_IN = 6176
B_IN_PAD = 6272
B_Q_OFF = 5120
B_DT_OFF = 6144
N_CHUNKS = SEQ // CHUNK
EPS = 1e-6
N_CHIPS = 4

ADAM_LR = 0.001
ADAM_B1 = 0.9
ADAM_B2 = 0.999
ADAM_EPS = 1e-08
ADAM_WD = 0.01
ADAM_STEP = 10

VMEM_LIMIT = 48 * 1024 * 1024
MESH = pl.DeviceIdType.MESH


def _cparams(sem):
    return pltpu.CompilerParams(dimension_semantics=sem, vmem_limit_bytes=VMEM_LIMIT)


def _dot(a, b, dims=(((1,), (0,)), ((), ()))):
    return lax.dot_general(a.astype(BF16), b.astype(BF16), dims, preferred_element_type=F32)


def _dot_nt(a, b):
    return _dot(a, b, (((1,), (1,)), ((), ())))


def _dot_tn(a, b):
    return _dot(a, b, (((0,), (0,)), ((), ())))


def _pick(n, cands):
    for c in cands:
        if n % c == 0:
            return c
    raise ValueError(f"no tile for {n}")


def _mm_call(a, b, *, dims, grid, a_spec, b_spec, acc_shape, out_shapes, out_specs, name,
             extras=(), extra_specs=(), epilogue=None, after=()):
    n_k = grid[2]
    n_extra = len(extras)
    n_out = len(out_shapes)
    n_in = 2 + n_extra + len(after)

    def finish(total, extra_refs, out_refs):
        vals = (total,) if epilogue is None else epilogue(total, *[e[...] for e in extra_refs])
        for o_ref, v in zip(out_refs, vals):
            o_ref[...] = v.astype(o_ref.dtype)

    def body_one_step(*refs):
        finish(_dot(refs[0][...], refs[1][...], dims), refs[2:2 + n_extra], refs[n_in:n_in + n_out])

    def body(*refs):
        acc = refs[-1]
        k = pl.program_id(2)

        @pl.when(k == 0)
        def _():
            acc[...] = jnp.zeros_like(acc)

        acc[...] += _dot(refs[0][...], refs[1][...], dims)

        @pl.when(k == n_k - 1)
        def _():
            finish(acc[...], refs[2:2 + n_extra], refs[n_in:n_in + n_out])

    return pl.pallas_call(
        body_one_step if n_k == 1 else body, grid=grid,
        in_specs=[a_spec, b_spec, *extra_specs, *([ANY] * len(after))], out_specs=list(out_specs),
        out_shape=list(out_shapes), scratch_shapes=[] if n_k == 1 else [pltpu.VMEM(acc_shape, F32)],
        compiler_params=_cparams(("parallel", "parallel", "arbitrary")), name=name,
    )(a, b, *extras, *after)


def _w_dims(w):
    if w.ndim == 2:
        return w.shape[0], w.shape[1], 1, w.shape[1]
    return w.shape[1], w.shape[0] * w.shape[2], w.shape[0], w.shape[2]


def _mm_nn(a, w, *, name, out_dtype=F32, a_cols=None, extras=(), epilogue=None, n_out_dtypes=None, after=()):
    m = a.shape[0]
    k_dim, n_dim, _, n_slot = _w_dims(w)
    a_off, a_w = (0, a.shape[1]) if a_cols is None else a_cols
    assert a_w == k_dim
    tm = _pick(m, (2048, 1024, 512, 256))
    tn = _pick(n_slot, (512, 896, 640, 256, 128))
    tk = _pick(k_dim, (1024, 768, 512, 384, 256, 128))
    assert a_off % tk == 0
    nb = n_slot // tn
    a_spec = pl.BlockSpec((tm, tk), lambda i, j, k: (i, a_off // tk + k))
    if w.ndim == 2:
        b_spec = pl.BlockSpec((tk, tn), lambda i, j, k: (k, j))
    else:
        b_spec = pl.BlockSpec((None, tk, tn), lambda i, j, k: (j // nb, k, j % nb))
    o_spec = pl.BlockSpec((tm, tn), lambda i, j, k: (i, j))
    dts = n_out_dtypes or (out_dtype,)
    outs = _mm_call(a, w, dims=(((1,), (0,)), ((), ())), grid=(m // tm, n_dim // tn, k_dim // tk),
                    a_spec=a_spec, b_spec=b_spec, acc_shape=(tm, tn),
                    out_shapes=[SDS((m, n_dim), dt) for dt in dts], out_specs=[o_spec] * len(dts), name=name,
                    extras=extras, extra_specs=[o_spec] * len(extras), epilogue=epilogue, after=after)
    return outs if n_out_dtypes else outs[0]


def _mm_nt(a, w, *, name, out_dtype=F32, extras=(), epilogue=None, after=()):
    m = a.shape[0]
    k_dim, n_dim, _, n_slot = _w_dims(w)
    assert a.shape[1] == n_dim
    tm = _pick(m, (2048, 1024, 512, 256))
    to = _pick(k_dim, (512, 384, 256, 128))
    tc = _pick(n_slot, (1280, 1024, 896, 640, 512, 256, 128))
    nb = n_slot // tc
    a_spec = pl.BlockSpec((tm, tc), lambda i, j, k: (i, k))
    if w.ndim == 2:
        b_spec = pl.BlockSpec((to, tc), lambda i, j, k: (j, k))
    else:
        b_spec = pl.BlockSpec((None, to, tc), lambda i, j, k: (k // nb, j, k % nb))
    o_spec = pl.BlockSpec((tm, to), lambda i, j, k: (i, j))
    return _mm_call(a, w, dims=(((1,), (1,)), ((), ())), grid=(m // tm, k_dim // to, n_dim // tc),
                    a_spec=a_spec, b_spec=b_spec, acc_shape=(tm, to),
                    out_shapes=[SDS((m, k_dim), out_dtype)], out_specs=[o_spec], name=name,
                    extras=extras, extra_specs=[o_spec] * len(extras), epilogue=epilogue, after=after)[0]


def _mm_tn(x, dy, *, name, x_cols=None):
    s = x.shape[0]
    x_off, k_dim = (0, x.shape[1]) if x_cols is None else x_cols
    n_dim = dy.shape[1]
    tm = _pick(k_dim, (1024, 768, 512, 384, 256, 128))
    tn = _pick(n_dim, (512, 896, 640, 256, 128))
    tk = _pick(s, (2048, 1024, 512, 256))
    assert x_off % tm == 0
    a_spec = pl.BlockSpec((tk, tm), lambda i, j, k: (k, x_off // tm + i))
    b_spec = pl.BlockSpec((tk, tn), lambda i, j, k: (k, j))
    o_spec = pl.BlockSpec((tm, tn), lambda i, j, k: (i, j))
    return _mm_call(x, dy, dims=(((0,), (0,)), ((), ())), grid=(k_dim // tm, n_dim // tn, s // tk),
                    a_spec=a_spec, b_spec=b_spec, acc_shape=(tm, tn),
                    out_shapes=[SDS((k_dim, n_dim), F32)], out_specs=[o_spec], name=name)[0]


def _mm_tn_stacked(x, dy, *, name, col_slots):
    s, k_dim = x.shape
    n_dim = dy.shape[1]
    r, c = (k_dim // 2, n_dim // N_CHIPS) if col_slots else (k_dim // N_CHIPS // 2, n_dim)
    tm = 2 * r
    tn = _pick(c, (512, 896, 640, 256, 128))
    tk = _pick(s, (2048, 1024, 512, 256))
    a_spec = pl.BlockSpec((tk, tm), lambda i, j, k: (k, i))
    b_spec = pl.BlockSpec((tk, tn), lambda i, j, k: (k, j))
    if col_slots:
        nb = c // tn
        o_spec = pl.BlockSpec((2, None, r, tn), lambda i, j, k: (0, j // nb, 0, j % nb))
    else:
        o_spec = pl.BlockSpec((2, None, r, tn), lambda i, j, k: (0, i, 0, j))
    return _mm_call(x, dy, dims=(((0,), (0,)), ((), ())), grid=(k_dim // tm, n_dim // tn, s // tk),
                    a_spec=a_spec, b_spec=b_spec, acc_shape=(tm, tn), epilogue=lambda acc: (acc.reshape(2, r, tn),),
                    out_shapes=[SDS((2, N_CHIPS, r, c), F32)], out_specs=[o_spec], name=name)[0]


def _rms(x, g):
    return x * lax.rsqrt(jnp.mean(x * x, axis=-1, keepdims=True) + EPS) * g


def _rms_fwd(h, g, *, name):
    rows, d = h.shape
    tr = _pick(rows, (512, 256))

    def body(h_ref, g_ref, o_ref):
        o_ref[...] = _rms(h_ref[...], g_ref[...]).astype(o_ref.dtype)

    return pl.pallas_call(
        body, grid=(rows // tr,),
        in_specs=[pl.BlockSpec((tr, d), lambda i: (i, 0)), pl.BlockSpec((1, d), lambda i: (0, 0))],
        out_specs=pl.BlockSpec((tr, d), lambda i: (i, 0)), out_shape=SDS((rows, d), BF16),
        compiler_params=_cparams(("parallel",)), name=name)(h, g)


def _rms_bwd(h, g, da, dres, *, name):
    rows, d = h.shape
    tr = _pick(rows, (512, 256))

    def body(h_ref, g_ref, da_ref, dres_ref, dh_ref, dg_ref):
        _, vjp = jax.vjp(_rms, h_ref[...], g_ref[...])
        dh, dg = vjp(da_ref[...].astype(F32))
        dh_ref[...] = dres_ref[...] + dh

        @pl.when(pl.program_id(0) == 0)
        def _():
            dg_ref[...] = jnp.zeros_like(dg_ref)

        dg_ref[...] += dg

    row_spec = pl.BlockSpec((tr, d), lambda i: (i, 0))
    vec_spec = pl.BlockSpec((1, d), lambda i: (0, 0))
    return pl.pallas_call(
        body, grid=(rows // tr,), in_specs=[row_spec, vec_spec, row_spec, row_spec],
        out_specs=[row_spec, vec_spec], out_shape=[SDS((rows, d), F32), SDS((1, d), F32)],
        compiler_params=_cparams(("arbitrary",)), name=name)(h, g, da, dres)


def _loss_head(h, g, target, *, name):
    rows, d = h.shape
    tr = _pick(rows, (512, 256))

    def body(h_ref, g_ref, t_ref, loss_ref, dh_ref, dg_ref):
        y, vjp = jax.vjp(_rms, h_ref[...], g_ref[...])
        err = y - t_ref[...]
        dh, dg = vjp(err * (1.0 / d))
        dh_ref[...] = dh

        @pl.when(pl.program_id(0) == 0)
        def _():
            dg_ref[...] = jnp.zeros_like(dg_ref)
            loss_ref[...] = jnp.zeros_like(loss_ref)

        dg_ref[...] += dg
        part = jnp.sum(jnp.sum(err * err, axis=-1, keepdims=True), axis=0, keepdims=True) * (0.5 / d)
        loss_ref[...] += jnp.broadcast_to(part, loss_ref.shape)

    row_spec = pl.BlockSpec((tr, d), lambda i: (i, 0))
    vec_spec = pl.BlockSpec((1, d), lambda i: (0, 0))
    loss_spec = pl.BlockSpec((8, 128), lambda i: (0, 0))
    return pl.pallas_call(
        body, grid=(rows // tr,), in_specs=[row_spec, vec_spec, row_spec],
        out_specs=[loss_spec, row_spec, vec_spec],
        out_shape=[SDS((8, 128), F32), SDS((rows, d), F32), SDS((1, d), F32)],
        compiler_params=_cparams(("arbitrary",)), name=name)(h, g, target)


def _gelu(x):
    return 0.5 * x * (1.0 + lax.erf(x * (1.0 / math.sqrt(2.0))))


def _gate_tile(pu, pv, ln_g, ln_b, ws, bs_t):
    u = [_gelu(p) for p in pu]
    v = [_gelu(p) for p in pv]
    mu = sum(jnp.sum(t, axis=-1, keepdims=True) for t in v) * (1.0 / D_INNER)
    vc = [t - mu for t in v]
    var = sum(jnp.sum(t * t, axis=-1, keepdims=True) for t in vc) * (1.0 / D_INNER)
    rstd = lax.rsqrt(var + EPS)
    row = lax.broadcasted_iota(jnp.int32, (CHUNK, CHUNK), 0)
    col = lax.broadcasted_iota(jnp.int32, (CHUNK, CHUNK), 1)
    out = []
    for gi in range(A_GROUPS):
        vn = vc[gi] * rstd * ln_g[gi] + ln_b[gi]
        w = jnp.where(row >= col, ws[gi], 0.0)
        sv = _dot(w, vn) + bs_t[gi]
        out.append(u[gi] * sv)
    return out


def _split(ref, n, width):
    return [ref[:, i * width:(i + 1) * width] for i in range(n)]


def _gate_in_specs():
    return [
        pl.BlockSpec((CHUNK, D_INNER), lambda c: (c, 0)),
        pl.BlockSpec((CHUNK, D_INNER), lambda c: (c, 1)),
        pl.BlockSpec((1, D_INNER), lambda c: (0, 0)),
        pl.BlockSpec((1, D_INNER), lambda c: (0, 0)),
        pl.BlockSpec((A_GROUPS, CHUNK, CHUNK), lambda c: (0, 0, 0)),
        pl.BlockSpec((A_GROUPS, CHUNK, 1), lambda c: (0, 0, 0)),
    ]


def _gate_args(u_ref, v_ref, g_ref, b_ref, ws_ref, bs_ref):
    ng, gw = A_GROUPS, A_GROUP_W
    return (_split(u_ref, ng, gw), _split(v_ref, ng, gw), _split(g_ref, ng, gw), _split(b_ref, ng, gw),
            [ws_ref[i] for i in range(ng)], [bs_ref[i] for i in range(ng)])


def _gate_fwd(proj, ln_g, ln_b, ws, bs_col, mixcat, *, name):
    def body(u_ref, v_ref, g_ref, b_ref, ws_ref, bs_ref, cat_in, cat_ref):
        del cat_in
        out = _gate_tile(*_gate_args(u_ref, v_ref, g_ref, b_ref, ws_ref, bs_ref))
        for gi, o in enumerate(out):
            cat_ref[:, gi * A_GROUP_W:(gi + 1) * A_GROUP_W] = o.astype(cat_ref.dtype)

    return pl.pallas_call(
        body, grid=(N_CHUNKS,), in_specs=[*_gate_in_specs(), pl.BlockSpec(memory_space=pl.ANY)],
        out_specs=pl.BlockSpec((CHUNK, D_INNER), lambda c: (c, 0)), out_shape=SDS(mixcat.shape, mixcat.dtype),
        input_output_aliases={6: 0}, compiler_params=_cparams(("parallel",)), name=name,
    )(proj, proj, ln_g, ln_b, ws, bs_col, mixcat)


def _gate_bwd(proj, ln_g, ln_b, ws, bs_col, dcat, dproj, *, name):
    ng, gw = A_GROUPS, A_GROUP_W

    def body(u_ref, v_ref, g_ref, b_ref, ws_ref, bs_ref, d_ref, dproj_in, dproj_ref, dg_ref, db_ref, dws_ref, dbs_ref):
        del dproj_in
        args = _gate_args(u_ref, v_ref, g_ref, b_ref, ws_ref, bs_ref)
        _, vjp = jax.vjp(_gate_tile, *args)
        dpu, dpv, dg, db, dws, dbs = vjp(_split(d_ref, ng, gw))
        for gi in range(ng):
            dproj_ref[:, gi * gw:(gi + 1) * gw] = dpu[gi].astype(dproj_ref.dtype)
            dproj_ref[:, D_INNER + gi * gw:D_INNER + (gi + 1) * gw] = dpv[gi].astype(dproj_ref.dtype)

        @pl.when(pl.program_id(0) == 0)
        def _():
            for r in (dg_ref, db_ref, dws_ref, dbs_ref):
                r[...] = jnp.zeros_like(r)

        for gi in range(ng):
            dg_ref[:, gi * gw:(gi + 1) * gw] += dg[gi]
            db_ref[:, gi * gw:(gi + 1) * gw] += db[gi]
            dws_ref[gi] += dws[gi]
            dbs_ref[gi] += dbs[gi]

    in_specs = _gate_in_specs()
    return pl.pallas_call(
        body, grid=(N_CHUNKS,),
        in_specs=[*in_specs, pl.BlockSpec((CHUNK, D_INNER), lambda c: (c, 0)), pl.BlockSpec(memory_space=pl.ANY)],
        out_specs=[pl.BlockSpec((CHUNK, 2 * D_INNER), lambda c: (c, 0)), *in_specs[2:]],
        out_shape=[SDS(dproj.shape, dproj.dtype), SDS((1, D_INNER), F32), SDS((1, D_INNER), F32),
                   SDS((ng, CHUNK, CHUNK), F32), SDS((ng, CHUNK, 1), F32)],
        input_output_aliases={7: 0}, compiler_params=_cparams(("arbitrary",)), name=name,
    )(proj, proj, ln_g, ln_b, ws, bs_col, dcat, dproj)


ATT_TQ = 512


def _attn_tile(q, k, v):
    s = _dot_nt(q, k) * (1.0 / math.sqrt(X_HEAD_DIM))
    s = s - jnp.max(s, axis=-1, keepdims=True)
    e = jnp.exp(s)
    p = e / jnp.sum(e, axis=-1, keepdims=True)
    return _dot(p, v)


def _attn_in_specs(q_blk, order):
    hd = X_HEAD_DIM
    return [
        pl.BlockSpec((ATT_TQ, hd), lambda a, b: (order(a, b)[0], q_blk + order(a, b)[1])),
        pl.BlockSpec((N_MEM, hd), lambda a, b: (0, order(a, b)[1])),
        pl.BlockSpec((N_MEM, hd), lambda a, b: (0, X_HEADS + order(a, b)[1])),
    ]


def _attn_fwd(proj, q_off, kv, *, name):
    order = lambda i, h: (i, h)
    cat_blk = D_INNER // X_HEAD_DIM

    def body(q_ref, k_ref, v_ref, o_ref):
        o_ref[...] = _attn_tile(q_ref[...], k_ref[...], v_ref[...]).astype(o_ref.dtype)

    return pl.pallas_call(
        body, grid=(SEQ // ATT_TQ, X_HEADS), in_specs=_attn_in_specs(q_off // X_HEAD_DIM, order),
        out_specs=pl.BlockSpec((ATT_TQ, X_HEAD_DIM), lambda i, h: (i, cat_blk + h)),
        out_shape=SDS((SEQ, MIX_OUT), BF16), compiler_params=_cparams(("parallel", "parallel")), name=name,
    )(proj, kv, kv)


def _attn_bwd(proj, q_off, kv, dcat, dproj_width, dq_off, *, name):
    order = lambda h, i: (i, h)
    cat_blk = D_INNER // X_HEAD_DIM
    dq_blk = dq_off // X_HEAD_DIM

    def body(q_ref, k_ref, v_ref, do_ref, dq_ref, dk_ref, dv_ref):
        _, vjp = jax.vjp(_attn_tile, q_ref[...], k_ref[...], v_ref[...])
        dq, dk, dv = vjp(do_ref[...])
        dq_ref[...] = dq.astype(dq_ref.dtype)

        @pl.when(pl.program_id(1) == 0)
        def _():
            dk_ref[...] = jnp.zeros_like(dk_ref)
            dv_ref[...] = jnp.zeros_like(dv_ref)

        dk_ref[...] += dk
        dv_ref[...] += dv

    kv_spec = pl.BlockSpec((N_MEM, X_HEAD_DIM), lambda h, i: (0, h))
    return pl.pallas_call(
        body, grid=(X_HEADS, SEQ // ATT_TQ),
        in_specs=[*_attn_in_specs(q_off // X_HEAD_DIM, order),
                  pl.BlockSpec((ATT_TQ, X_HEAD_DIM), lambda h, i: (i, cat_blk + h))],
        out_specs=[pl.BlockSpec((ATT_TQ, X_HEAD_DIM), lambda h, i: (i, dq_blk + h)), kv_spec, kv_spec],
        out_shape=[SDS((SEQ, dproj_width), BF16), SDS((N_MEM, X_WIDTH), F32), SDS((N_MEM, X_WIDTH), F32)],
        compiler_params=_cparams(("parallel", "arbitrary")), name=name,
    )(proj, kv, kv, dcat)


CONV_TC = 512


def _shift_down(x, s):
    if s == 0:
        return x
    row = lax.broadcasted_iota(jnp.int32, x.shape, 0)
    return jnp.where(row >= s, pltpu.roll(x, s, 0), 0.0)


def _shift_up(x, s):
    if s == 0:
        return x
    n = x.shape[0]
    row = lax.broadcasted_iota(jnp.int32, x.shape, 0)
    return jnp.where(row < n - s, pltpu.roll(x, n - s, 0), 0.0)


def _conv_pre(x, w_ref, b_ref):
    pre = b_ref[...] + jnp.zeros_like(x)
    for k in range(CONV_K):
        pre = pre + w_ref[k:k + 1, :] * _shift_down(x, CONV_K - 1 - k)
    return pre


def _conv_fwd(proj, w, b, *, name):
    blk0 = D_INNER // CONV_TC

    def body(x_ref, w_ref, b_ref, o_ref):
        pre = _conv_pre(x_ref[...], w_ref, b_ref)
        o_ref[...] = pre * jax.nn.sigmoid(pre)

    return pl.pallas_call(
        body, grid=(CONV_DIM // CONV_TC,),
        in_specs=[pl.BlockSpec((SEQ, CONV_TC), lambda j: (0, blk0 + j)), pl.BlockSpec((CONV_K, CONV_TC), lambda j: (0, j)),
                  pl.BlockSpec((1, CONV_TC), lambda j: (0, j))],
        out_specs=pl.BlockSpec((SEQ, CONV_TC), lambda j: (0, j)), out_shape=SDS((SEQ, CONV_DIM), F32),
        compiler_params=_cparams(("parallel",)), name=name)(proj, w, b)


def _conv_bwd(proj, w, b, dxs, dbm, dcm, dproj, *, name):
    tc = CONV_TC // 2
    blk0 = D_INNER // tc
    n_x = D_INNER // tc
    n_b = SSM_GROUPS * SSM_STATE // tc

    def body(x_ref, w_ref, b_ref, dxs_ref, dbm_ref, dcm_ref, dproj_in, dproj_ref, dw_ref, db_ref):
        del dproj_in
        j = pl.program_id(0)
        x = x_ref[...]
        pre = _conv_pre(x, w_ref, b_ref)
        sg = jax.nn.sigmoid(pre)
        dact = jnp.where(j < n_x, dxs_ref[...], jnp.where(j < n_x + n_b, dbm_ref[...], dcm_ref[...]))
        dpre = dact * (sg * (1.0 + pre * (1.0 - sg)))
        dx = jnp.zeros_like(x)
        for k in range(CONV_K):
            s = CONV_K - 1 - k
            dx = dx + w_ref[k:k + 1, :] * _shift_up(dpre, s)
            dw_ref[k:k + 1, :] = jnp.sum(dpre * _shift_down(x, s), axis=0, keepdims=True)
        dproj_ref[...] = dx.astype(dproj_ref.dtype)
        db_ref[...] = jnp.sum(dpre, axis=0, keepdims=True)

    clip = lambda v, hi: jnp.minimum(jnp.maximum(v, 0), hi)
    return pl.pallas_call(
        body, grid=(CONV_DIM // tc,),
        in_specs=[pl.BlockSpec((SEQ, tc), lambda j: (0, blk0 + j)), pl.BlockSpec((CONV_K, tc), lambda j: (0, j)),
                  pl.BlockSpec((1, tc), lambda j: (0, j)),
                  pl.BlockSpec((SEQ, tc), lambda j: (0, clip(j, n_x - 1))),
                  pl.BlockSpec((SEQ, tc), lambda j: (0, clip(j - n_x, n_b - 1))),
                  pl.BlockSpec((SEQ, tc), lambda j: (0, clip(j - n_x - n_b, n_b - 1))),
                  pl.BlockSpec(memory_space=pl.ANY)],
        out_specs=[pl.BlockSpec((SEQ, tc), lambda j: (0, blk0 + j)), pl.BlockSpec((CONV_K, tc), lambda j: (0, j)),
                   pl.BlockSpec((1, tc), lambda j: (0, j))],
        out_shape=[SDS(dproj.shape, dproj.dtype), SDS((CONV_K, CONV_DIM), F32), SDS((1, CONV_DIM), F32)],
        input_output_aliases={6: 0}, compiler_params=_cparams(("parallel",)), name=name,
    )(proj, w, b, dxs, dbm, dcm, dproj)


SSM_PAIRS = SSM_HPG // 2


def _dot_exact01(x, m01, m01_t, x_first, differentiable):
    def product(v, m):
        hi = v.astype(BF16)
        rest = v - hi.astype(F32)
        mid = rest.astype(BF16)
        lo = (rest - mid.astype(F32)).astype(BF16)
        dims = (((1,), (0,)), ((), ()))
        dot = lambda part: lax.dot_general(*((part, m) if x_first else (m, part)), dims, preferred_element_type=F32)
        return dot(hi) + dot(mid) + dot(lo)

    if not differentiable:
        return product(x, m01)

    @jax.custom_vjp
    def exact(v):
        return product(v, m01)

    exact.defvjp(lambda v: (product(v, m01), None), lambda _, ct: (product(ct, m01_t),))
    return exact(x)


def _ssd_tile(xp, zp, bm, cm, hp, dt_c, dt_r, bias, bias_col, alog, alog_col, dsk, gnp, differentiable=False):
    row = lax.broadcasted_iota(jnp.int32, (CHUNK, CHUNK), 0)
    col = lax.broadcasted_iota(jnp.int32, (CHUNK, CHUNK), 1)
    causal = row >= col
    left = col < SSM_HEAD_DIM
    top = row < SSM_HEAD_DIM
    ones = jnp.ones((CHUNK, CHUNK), BF16)
    cb = _dot_nt(cm, bm)
    dtp = jax.nn.softplus(dt_c + bias)
    da_c = dtp * -jnp.exp(alog)
    da_r = jax.nn.softplus(dt_r + bias_col) * -jnp.exp(alog_col)
    lower = jnp.where(causal, 1.0, 0.0).astype(BF16)
    upper = jnp.where(row <= col, 1.0, 0.0).astype(BF16)
    cs = _dot_exact01(da_c, lower, upper, False, differentiable)
    cs_rows = _dot_exact01(da_r, upper, lower, True, differentiable)
    cs_last = jnp.sum(da_c, axis=0, keepdims=True)
    ecs, decay, ecl = jnp.exp(cs), jnp.exp(cs_last - cs), jnp.exp(cs_last)
    m = [cb * jnp.exp(jnp.where(causal, cs[:, r:r + 1] - cs_rows[r:r + 1, :], -1e30)) for r in range(SSM_HPG)]
    ygs, hn = [], []
    for p in range(SSM_PAIRS):
        a, b = 2 * p, 2 * p + 1
        pair = lambda v: jnp.where(left, v[:, a:a + 1], v[:, b:b + 1])
        xdt = xp[p] * pair(dtp)
        y = jnp.where(left, _dot(m[a], xdt), _dot(m[b], xdt))
        y = y + _dot_nt(cm, hp[p]) * pair(ecs)
        y = y + xp[p] * pair(dsk)
        states = _dot_tn(xdt * pair(decay), bm)
        hn.append(hp[p] * jnp.where(top, ecl[:, a:a + 1], ecl[:, b:b + 1]) + states)
        ygs.append(y * (zp[p] * jax.nn.sigmoid(zp[p])))
    ms = sum(_dot(t * t, ones) for t in ygs) * (1.0 / SSM_GROUP_W)
    rs = lax.rsqrt(ms + EPS)
    return [ygs[p] * rs * gnp[p] for p in range(SSM_PAIRS)], hn


def _ssd_in_specs(cidx):
    gw, n = SSM_GROUP_W, SSM_STATE
    bm_blk = D_INNER // n
    return [
        pl.BlockSpec((CHUNK, gw), lambda g, c: (cidx(c), g)),
        pl.BlockSpec((CHUNK, gw), lambda g, c: (cidx(c), g)),
        pl.BlockSpec((CHUNK, n), lambda g, c: (cidx(c), bm_blk + g)),
        pl.BlockSpec((CHUNK, n), lambda g, c: (cidx(c), bm_blk + SSM_GROUPS + g)),
        pl.BlockSpec((None, CHUNK, SSM_HPG), lambda g, c: (g, cidx(c), 0)),
        pl.BlockSpec((None, SSM_HPG, CHUNK), lambda g, c: (g, 0, cidx(c))),
        pl.BlockSpec((None, 3, SSM_HPG), lambda g, c: (g, 0, 0)),
        pl.BlockSpec((None, SSM_HPG, 2), lambda g, c: (g, 0, 0)),
        pl.BlockSpec((1, gw), lambda g, c: (0, g)),
    ]


def _ssd_args(x_ref, z_ref, bm_ref, cm_ref, hp, dtc_ref, dtr_ref, prow_ref, pcol_ref, gn_ref):
    npair, w = SSM_PAIRS, 2 * SSM_HEAD_DIM
    return (_split(x_ref, npair, w), _split(z_ref, npair, w), bm_ref[...], cm_ref[...], hp, dtc_ref[...], dtr_ref[...],
            prow_ref[0:1, :], pcol_ref[:, 0:1], prow_ref[1:2, :], pcol_ref[:, 1:2], prow_ref[2:3, :],
            _split(gn_ref, npair, w))


def _pair_rows(ref):
    w = 2 * SSM_HEAD_DIM
    return [ref[p * w:(p + 1) * w, :] for p in range(SSM_PAIRS)]


def _ssd_fwd(xbc, proj, dt_c, dt_r, par_row, par_col, gn, mixcat, *, name):
    w = 2 * SSM_HEAD_DIM

    def body(x_ref, z_ref, bm_ref, cm_ref, dtc_ref, dtr_ref, prow_ref, pcol_ref, gn_ref, cat_in,
             cat_ref, hprev_ref, h_scr):
        del cat_in

        @pl.when(pl.program_id(1) == 0)
        def _():
            h_scr[...] = jnp.zeros_like(h_scr)

        hprev_ref[...] = h_scr[...]
        yn, hn = _ssd_tile(*_ssd_args(x_ref, z_ref, bm_ref, cm_ref, _pair_rows(h_scr), dtc_ref, dtr_ref, prow_ref,
                                      pcol_ref, gn_ref))
        for p in range(SSM_PAIRS):
            cat_ref[:, p * w:(p + 1) * w] = yn[p].astype(cat_ref.dtype)
            h_scr[p * w:(p + 1) * w, :] = hn[p]

    return pl.pallas_call(
        body, grid=(SSM_GROUPS, N_CHUNKS), in_specs=[*_ssd_in_specs(lambda c: c), pl.BlockSpec(memory_space=pl.ANY)],
        out_specs=[pl.BlockSpec((CHUNK, SSM_GROUP_W), lambda g, c: (c, g)),
                   pl.BlockSpec((None, None, SSM_GROUP_W, SSM_STATE), lambda g, c: (c, g, 0, 0))],
        out_shape=[SDS(mixcat.shape, mixcat.dtype), SDS((N_CHUNKS, SSM_GROUPS, SSM_GROUP_W, SSM_STATE), F32)],
        scratch_shapes=[pltpu.VMEM((SSM_GROUP_W, SSM_STATE), F32)],
        input_output_aliases={9: 0}, compiler_params=_cparams(("parallel", "arbitrary")), name=name,
    )(xbc, proj, xbc, xbc, dt_c, dt_r, par_row, par_col, gn, mixcat)


def _ssd_bwd(xbc, proj, dt_c, dt_r, par_row, par_col, gn, hprev, dcat, dproj, *, name):
    nh, w, gw, n = SSM_HPG, 2 * SSM_HEAD_DIM, SSM_GROUP_W, SSM_STATE
    rev = lambda c: N_CHUNKS - 1 - c

    def body(x_ref, z_ref, bm_ref, cm_ref, dtc_ref, dtr_ref, prow_ref, pcol_ref, gn_ref, hprev_ref, dy_ref,
             dproj_in, dz_ref, dxs_ref, dbm_ref, dcm_ref, ddtc_ref, ddtr_ref, dprow_ref, dpcol_ref, dgn_ref, dh_scr):
        del dproj_in
        first = pl.program_id(1) == 0

        @pl.when(first)
        def _():
            dh_scr[...] = jnp.zeros_like(dh_scr)
            for ref in (dprow_ref, dpcol_ref, dgn_ref):
                ref[...] = jnp.zeros_like(ref)

        args = _ssd_args(x_ref, z_ref, bm_ref, cm_ref, _pair_rows(hprev_ref), dtc_ref, dtr_ref, prow_ref, pcol_ref,
                         gn_ref)
        _, vjp = jax.vjp(lambda *a: _ssd_tile(*a, differentiable=True), *args)
        dxs, dzs, dbm, dcm, dhs, ddtc, ddtr, dbias, dbias_col, dalog, dalog_col, ddsk, dgn = vjp(
            (_split(dy_ref, SSM_PAIRS, w), _pair_rows(dh_scr)))
        dbm_ref[...] = dbm
        dcm_ref[...] = dcm
        ddtc_ref[...] = ddtc
        ddtr_ref[...] = ddtr
        for q in range(SSM_PAIRS):
            dxs_ref[:, q * w:(q + 1) * w] = dxs[q]
            dz_ref[:, q * w:(q + 1) * w] = dzs[q].astype(dz_ref.dtype)
            dh_scr[q * w:(q + 1) * w, :] = dhs[q]
            dgn_ref[:, q * w:(q + 1) * w] += dgn[q]
        for i, d in enumerate((dbias, dalog, ddsk)):
            dprow_ref[i:i + 1, :] += d
        for i, d in enumerate((dbias_col, dalog_col)):
            dpcol_ref[:, i:i + 1] += d

    return pl.pallas_call(
        body, grid=(SSM_GROUPS, N_CHUNKS),
        in_specs=[*_ssd_in_specs(rev),
                  pl.BlockSpec((None, None, gw, n), lambda g, c: (rev(c), g, 0, 0)),
                  pl.BlockSpec((CHUNK, gw), lambda g, c: (rev(c), g)),
                  pl.BlockSpec(memory_space=pl.ANY)],
        out_specs=[pl.BlockSpec((CHUNK, gw), lambda g, c: (rev(c), g)),
                   pl.BlockSpec((CHUNK, gw), lambda g, c: (rev(c), g)),
                   pl.BlockSpec((CHUNK, n), lambda g, c: (rev(c), g)),
                   pl.BlockSpec((CHUNK, n), lambda g, c: (rev(c), g)),
                   pl.BlockSpec((None, CHUNK, nh), lambda g, c: (g, rev(c), 0)),
                   pl.BlockSpec((None, nh, CHUNK), lambda g, c: (g, 0, rev(c))),
                   pl.BlockSpec((None, 3, nh), lambda g, c: (g, 0, 0)),
                   pl.BlockSpec((None, nh, 2), lambda g, c: (g, 0, 0)),
                   pl.BlockSpec((1, gw), lambda g, c: (0, g))],
        out_shape=[SDS(dproj.shape, dproj.dtype), SDS((SEQ, D_INNER), F32), SDS((SEQ, SSM_GROUPS * n), F32),
                   SDS((SEQ, SSM_GROUPS * n), F32), SDS((SSM_GROUPS, SEQ, nh), F32), SDS((SSM_GROUPS, nh, SEQ), F32),
                   SDS((SSM_GROUPS, 3, nh), F32), SDS((SSM_GROUPS, nh, 2), F32), SDS((1, D_INNER), F32)],
        scratch_shapes=[pltpu.VMEM((gw, n), F32)],
        input_output_aliases={11: 0}, compiler_params=_cparams(("parallel", "arbitrary")), name=name,
    )(xbc, proj, xbc, xbc, dt_c, dt_r, par_row, par_col, gn, hprev, dcat, dproj)


def _sum_contributions(chip, parts, landed, *, name):
    _, r, c = parts.shape
    tr = _pick(r, (256, 384, 128))

    def body(chip_ref, own_ref, landed_ref, o_ref):
        del chip_ref
        acc = own_ref[...].astype(F32)
        for s in range(landed_ref.shape[0]):
            acc = acc + landed_ref[s].astype(F32)
        o_ref[...] = acc

    grid_spec = pltpu.PrefetchScalarGridSpec(
        num_scalar_prefetch=1, grid=(r // tr,),
        in_specs=[pl.BlockSpec((None, tr, c), lambda i, chip_ref: (chip_ref[0], i, 0)),
                  pl.BlockSpec((landed.shape[0], tr, c), lambda i, chip_ref: (0, i, 0))],
        out_specs=pl.BlockSpec((tr, c), lambda i, chip_ref: (i, 0)))
    return pl.pallas_call(body, grid_spec=grid_spec, out_shape=SDS((r, c), F32),
                          compiler_params=_cparams(("parallel",)), name=name)(chip, parts, landed)


def _adamw(w, g, m, v, *, name):
    layers, r, c = w.shape
    if r <= 256 or r % 128 == 0:
        tr = min(r, 256)
        steps, spec = r // tr, pl.BlockSpec((None, tr, c), lambda l, i: (l, i, 0))
    else:
        tc = _pick(c, (256, 128))
        steps, spec = c // tc, pl.BlockSpec((None, r, tc), lambda l, i: (l, 0, i))

    def body(w_ref, g_ref, m_ref, v_ref, d_ref, mo_ref, vo_ref):
        g = g_ref[...]
        m_new = ADAM_B1 * m_ref[...] + (1.0 - ADAM_B1) * g
        v_new = ADAM_B2 * v_ref[...] + (1.0 - ADAM_B2) * (g * g)
        m_hat = m_new / (1.0 - ADAM_B1 ** ADAM_STEP)
        v_hat = v_new / (1.0 - ADAM_B2 ** ADAM_STEP)
        d_ref[...] = -ADAM_LR * (m_hat / (jnp.sqrt(v_hat) + ADAM_EPS) + ADAM_WD * w_ref[...])
        mo_ref[...] = m_new
        vo_ref[...] = v_new

    return pl.pallas_call(body, grid=(layers, steps), in_specs=[spec] * 4, out_specs=[spec] * 3,
                          out_shape=[SDS(w.shape, F32)] * 3, compiler_params=_cparams(("parallel", "parallel")),
                          name=name)(w, g, m, v)


ANY = pl.BlockSpec(memory_space=pl.ANY)


def _place():
    x, y, c = lax.axis_index("x"), lax.axis_index("y"), lax.axis_index("c")
    chips = [(1 - x, y), (x, 1 - y), (1 - x, 1 - y)]
    return x, y, c, chips


def _remote(src, dst, send_sem, recv_sem, to):
    return pltpu.make_async_remote_copy(src_ref=src, dst_ref=dst, send_sem=send_sem, recv_sem=recv_sem,
                                        device_id=to, device_id_type=MESH)


STREAM_ROWS = 256


def _stream_rows(i):
    return pl.ds(pl.multiple_of(i * STREAM_ROWS, STREAM_ROWS), STREAM_ROWS)


def _channel_scratch(width, dtype, rows=STREAM_ROWS):
    buf = (2, rows, width)
    return [pltpu.VMEM(buf, dtype), pltpu.VMEM(buf, dtype), *([pltpu.SemaphoreType.DMA((2,))] * 5),
            pltpu.SemaphoreType.REGULAR((2,))]


CHANNEL_REFS = 8


def _copy_blocks(srcs, dsts, ch):
    sbuf, _, ld, _, _, st, _, _ = ch
    n = len(srcs)
    load = lambda i: pltpu.make_async_copy(srcs[i], sbuf.at[i % 2], ld.at[i % 2])
    store = lambda i: pltpu.make_async_copy(sbuf.at[i % 2], dsts[i], st.at[i % 2])
    load(0).start()
    for i in range(n):
        if i + 1 < n:
            if i >= 1:
                store(i - 1).wait()
            load(i + 1).start()
        load(i).wait()
        store(i).start()
    for i in range(max(0, n - 2), n):
        store(i).wait()


def _exchange_block_streams(streams, sibling):
    plans = []
    for srcs, dsts, keeps, (sbuf, rbuf, ld, snd, rcv, st, kp, credit) in streams:
        n = len(srcs)

        def load(i, srcs=srcs, sbuf=sbuf, ld=ld):
            return pltpu.make_async_copy(srcs[i], sbuf.at[i % 2], ld.at[i % 2])

        def push(i, sbuf=sbuf, rbuf=rbuf, snd=snd, rcv=rcv):
            return _remote(sbuf.at[i % 2], rbuf.at[i % 2], snd.at[i % 2], rcv.at[i % 2], sibling)

        def store(i, rbuf=rbuf, dsts=dsts, st=st):
            return pltpu.make_async_copy(rbuf.at[i % 2], dsts[i], st.at[i % 2])

        def save(i, sbuf=sbuf, keeps=keeps, kp=kp):
            return pltpu.make_async_copy(sbuf.at[i % 2], keeps[i], kp.at[i % 2])

        def free_slot(i, n=n, store=store, credit=credit):
            if 1 <= i < n:
                store(i - 1).wait()
                if i + 1 < n:
                    pl.semaphore_signal(credit.at[(i + 1) % 2], 1, device_id=sibling, device_id_type=MESH)

        def send(i, n=n, load=load, push=push, save=save, keeps=keeps, credit=credit):
            if i < n:
                load(i).wait()
                pl.semaphore_wait(credit.at[i % 2], 1)
                push(i).start()
                if keeps[i] is not None:
                    save(i).start()

        def receive(i, n=n, load=load, push=push, store=store, save=save, keeps=keeps):
            if i < n:
                push(i).wait_recv()
                store(i).start()
                push(i).wait_send()
                if keeps[i] is not None:
                    save(i).wait()
                if i + 2 < n:
                    load(i + 2).start()

        for i in range(min(2, n)):
            pl.semaphore_signal(credit.at[i], 1, device_id=sibling, device_id_type=MESH)
            load(i).start()
        plans.append((n, free_slot, send, receive, store))
    for _, _, send, _, _ in plans:
        send(0)
    for i in range(max(p[0] for p in plans)):
        for _, free_slot, _, _, _ in plans:
            free_slot(i)
        for _, _, send, _, _ in plans:
            send(i + 1)
        for _, _, _, receive, _ in plans:
            receive(i)
    for n, _, _, _, store in plans:
        store(n - 1).wait()


def _all_gather_shards(shards, small, *, name):
    n = len(shards)

    def body(*refs):
        ins, outs = refs[:n + 1], refs[n + 1:2 * n + 2]
        scr = refs[2 * n + 2:]
        chans = [scr[CHANNEL_REFS * t:CHANNEL_REFS * (t + 1)] for t in range(n)]
        send_sems, recv_sems, small_sems = scr[CHANNEL_REFS * n:]
        x, y, c, _ = _place()
        me = 2 * x + y
        sibling = (x, y, 1 - c)
        near = (lax.rem(x + 1 - c, 2), lax.rem(y + c, 2))
        far = (lax.rem(x + c, 2), lax.rem(y + 1 - c, 2))
        k_near, k_far, k_diag = 2 * near[0] + near[1], 2 * far[0] + far[1], 3 - me
        targets = ((*near, c), (*far, c), (*far, c))
        arrives = (k_near, k_far, k_diag)
        streams_in = (k_far, k_near, k_diag)

        def ici(t, j, src, blk):
            return _remote(src, outs[t].at[blk, c], send_sems.at[3 * t + j], recv_sems.at[3 * t + j], targets[j])

        first = [ici(t, j, ins[t].at[c], me) for t in range(n + 1) for j in range(2)]
        for cp in first:
            cp.start()
        small_local = pltpu.make_async_copy(ins[n], outs[n].at[me], small_sems.at[6])
        small_local.start()
        for t in range(n):
            _copy_blocks([ins[t].at[h] for h in range(2)], [outs[t].at[me, h] for h in range(2)], chans[t])
        passed = []
        for j in range(3):
            for t in range(n + 1):
                landed = outs[t].at[arrives[j], c]
                ici(t, j, landed, arrives[j]).wait_recv()
                if j == 0:
                    fwd = ici(t, 2, landed, k_near)
                    fwd.start()
                    passed.append(fwd)
                if t < n:
                    _exchange_block_streams([([landed], [outs[t].at[streams_in[j], 1 - c]], [None], chans[t])], sibling)
                else:
                    fwd = _remote(landed, landed, small_sems.at[j], small_sems.at[3 + j], sibling)
                    fwd.start()
                    passed.append(fwd)
        for j in range(3):
            got = outs[n].at[streams_in[j], 1 - c]
            _remote(got, got, small_sems.at[j], small_sems.at[3 + j], sibling).wait_recv()
        for cp in first + passed:
            cp.wait_send()
        small_local.wait()

    scratch = []
    for s in shards:
        scratch += _channel_scratch(s.shape[2], s.dtype, rows=s.shape[1])
    return pl.pallas_call(
        body, in_specs=[ANY] * (n + 1), out_specs=[ANY] * (n + 1),
        out_shape=[SDS((N_CHIPS, *s.shape), s.dtype) for s in (*shards, small)],
        scratch_shapes=[*scratch, pltpu.SemaphoreType.DMA((3 * n + 3,)), pltpu.SemaphoreType.DMA((3 * n + 3,)),
                        pltpu.SemaphoreType.DMA((7,))],
        compiler_params=pltpu.CompilerParams(vmem_limit_bytes=VMEM_LIMIT), name=name)(*shards, small)


def _pair_reduce(stacks, *, name):
    n = len(stacks)
    per = 11

    def body(*refs):
        ins, outs, scr = refs[:n], refs[n:2 * n], refs[2 * n:]
        x, y, c, _ = _place()
        sibling = (x, y, 1 - c)
        streams = []
        for t in range(n):
            sraw, sbuf, rbuf, obuf, pbuf, ld_s, ld_o, snd, rcv, st, credit = scr[per * t:per * (t + 1)]
            steps = ins[t].shape[1] // STREAM_ROWS
            src, own, out = ins[t].at[1 - c], ins[t].at[c], outs[t]
            assert steps >= 2

            def load_s(i, slot, src=src, sraw=sraw, ld_s=ld_s):
                return pltpu.make_async_copy(src.at[_stream_rows(i)], sraw.at[slot], ld_s.at[slot])

            def load_o(i, slot, own=own, obuf=obuf, ld_o=ld_o):
                return pltpu.make_async_copy(own.at[_stream_rows(i)], obuf.at[slot], ld_o.at[slot])

            def push(slot, sbuf=sbuf, rbuf=rbuf, snd=snd, rcv=rcv):
                return _remote(sbuf.at[slot], rbuf.at[slot], snd.at[slot], rcv.at[slot], sibling)

            def store(i, slot, pbuf=pbuf, out=out, st=st):
                return pltpu.make_async_copy(pbuf.at[slot], out.at[_stream_rows(i)], st.at[slot])

            def send(i, slot, load_s=load_s, push=push, sraw=sraw, sbuf=sbuf, credit=credit):
                load_s(i, slot).wait()
                sbuf[slot] = sraw[slot].astype(sbuf.dtype)
                pl.semaphore_wait(credit.at[slot], 1)
                push(slot).start()

            def combine(i, slot, load_s=load_s, load_o=load_o, push=push, store=store, rbuf=rbuf, obuf=obuf, pbuf=pbuf,
                        credit=credit, steps=steps):
                load_o(i, slot).wait()
                push(slot).wait_recv()

                @pl.when(i >= 2)
                def _():
                    store(i, slot).wait()

                pbuf[slot] = (obuf[slot] + rbuf[slot].astype(F32)).astype(pbuf.dtype)
                store(i, slot).start()
                push(slot).wait_send()

                @pl.when(i + 2 < steps)
                def _():
                    load_s(i + 2, slot).start()
                    load_o(i + 2, slot).start()
                    pl.semaphore_signal(credit.at[slot], 1, device_id=sibling, device_id_type=MESH)

            for slot in range(2):
                pl.semaphore_signal(credit.at[slot], 1, device_id=sibling, device_id_type=MESH)
                load_s(slot, slot).start()
                load_o(slot, slot).start()
            streams.append((steps, send, combine, store))
        for _, send, _, _ in streams:
            send(0, 0)

        def step(i, carry):
            slot = lax.rem(i, 2)
            for steps, send, _, _ in streams:
                @pl.when(i + 1 < steps)
                def _(send=send):
                    send(i + 1, 1 - slot)
            for steps, _, combine, _ in streams:
                @pl.when(i < steps)
                def _(combine=combine):
                    combine(i, slot)
            return carry

        lax.fori_loop(0, max(s[0] for s in streams), step, 0)
        for _, _, _, store in streams:
            for slot in range(2):
                store(0, slot).wait()

    scratch = []
    for s in stacks:
        buf = (2, STREAM_ROWS, s.shape[2])
        scratch += [pltpu.VMEM(buf, F32), pltpu.VMEM(buf, BF16), pltpu.VMEM(buf, BF16), pltpu.VMEM(buf, F32),
                    pltpu.VMEM(buf, BF16), *([pltpu.SemaphoreType.DMA((2,))] * 5), pltpu.SemaphoreType.REGULAR((2,))]
    return pl.pallas_call(
        body, in_specs=[ANY] * n, out_specs=[ANY] * n, out_shape=[SDS(s.shape[1:], BF16) for s in stacks],
        scratch_shapes=scratch, compiler_params=pltpu.CompilerParams(vmem_limit_bytes=VMEM_LIMIT), name=name)(*stacks)


HBM_SPEC = pl.BlockSpec(memory_space=pltpu.HBM)
SEM_SPEC = pl.BlockSpec(memory_space=pltpu.SEMAPHORE)
SIDE_EFFECT = pltpu.SideEffectType.DATAFLOW_SIDE_EFFECTING


def _scatter_copies(ins, lands, send_sems, recv_sems):
    _, _, c, chips = _place()
    return [_remote(ins[t].at[2 * cx + cy], lands[t].at[j], send_sems.at[3 * t + j], recv_sems.at[3 * t + j],
                    (cx, cy, c)) for t in range(len(ins)) for j, (cx, cy) in enumerate(chips)]


def _chip_scatter_start(parts, *, name):
    n = len(parts)

    def body(*refs):
        ins, lands = refs[:n], refs[n:2 * n]
        send_sems, recv_sems, token = refs[2 * n], refs[2 * n + 1], refs[-1]
        for cp in _scatter_copies(ins, lands, send_sems, recv_sems):
            cp.start()
        token[...] = jnp.zeros_like(token)

    hbm = lambda a: pltpu.with_memory_space_constraint(a, pltpu.HBM)
    lands = [hbm(lax.empty((3, *p.shape[1:]), p.dtype)) for p in parts]
    thru = [pltpu.HBM(a.shape, a.dtype) for a in (*parts, *lands)]
    outs = pl.pallas_call(
        body, name=name,
        out_shape=(pltpu.SemaphoreType.DMA((3 * n,)), pltpu.SemaphoreType.DMA((3 * n,)), *thru, SDS((8, 128), F32)),
        in_specs=[HBM_SPEC] * (2 * n),
        out_specs=(SEM_SPEC, SEM_SPEC, *([HBM_SPEC] * (2 * n)), pl.BlockSpec(memory_space=pltpu.VMEM)),
        input_output_aliases={i: 2 + i for i in range(2 * n)},
        compiler_params=pltpu.CompilerParams(has_side_effects=SIDE_EFFECT),
    )(*[hbm(p) for p in parts], *lands)
    return outs[0], outs[1], outs[2:2 + n], outs[2 + n:2 + 2 * n], outs[-1]


def _chip_scatter_wait(send_sems, recv_sems, parts, lands, after, *, name):
    n = len(parts)

    def body(*refs):
        ins, lands_in = refs[:n], refs[n:2 * n]
        for cp in _scatter_copies(ins, lands_in, refs[2 * n], refs[2 * n + 1]):
            cp.wait_send()
            cp.wait_recv()

    outs = pl.pallas_call(
        body, name=name, out_shape=[pltpu.HBM(a.shape, a.dtype) for a in (*parts, *lands)],
        in_specs=[*([HBM_SPEC] * (2 * n)), SEM_SPEC, SEM_SPEC, *([ANY] * len(after))],
        out_specs=[HBM_SPEC] * (2 * n), input_output_aliases={i: i for i in range(2 * n)},
        compiler_params=pltpu.CompilerParams(has_side_effects=SIDE_EFFECT),
    )(*parts, *lands, send_sems, recv_sems, *after)
    return outs[:n], outs[n:]


def _gather_copies(shards, zones, send_sems, recv_sems):
    x, y, c, chips = _place()
    return [_remote(shards[t].at[c], zones[t].at[2 * x + y, c], send_sems.at[3 * t + j], recv_sems.at[3 * t + j],
                    (cx, cy, c)) for t in range(len(shards)) for j, (cx, cy) in enumerate(chips)]


def _gather_start(shards, after, *, name):
    n = len(shards)

    def body(*refs):
        ins, zones = refs[:n], refs[n:2 * n]
        send_sems, recv_sems, token = refs[2 * n + len(after)], refs[2 * n + len(after) + 1], refs[-1]
        for cp in _gather_copies(ins, zones, send_sems, recv_sems):
            cp.start()
        token[...] = jnp.zeros_like(token)

    hbm = lambda a: pltpu.with_memory_space_constraint(a, pltpu.HBM)
    zones = [hbm(lax.empty((N_CHIPS, *s.shape), s.dtype)) for s in shards]
    thru = [pltpu.HBM(a.shape, a.dtype) for a in (*shards, *zones)]
    outs = pl.pallas_call(
        body, name=name,
        out_shape=(pltpu.SemaphoreType.DMA((3 * n,)), pltpu.SemaphoreType.DMA((3 * n,)), *thru, SDS((8, 128), F32)),
        in_specs=[*([HBM_SPEC] * (2 * n)), *([ANY] * len(after))],
        out_specs=(SEM_SPEC, SEM_SPEC, *([HBM_SPEC] * (2 * n)), pl.BlockSpec(memory_space=pltpu.VMEM)),
        input_output_aliases={i: 2 + i for i in range(2 * n)},
        compiler_params=pltpu.CompilerParams(has_side_effects=SIDE_EFFECT),
    )(*[hbm(s) for s in shards], *zones, *after)
    return outs[0], outs[1], outs[2:2 + n], outs[2 + n:2 + 2 * n], outs[-1]


def _gather_wait(send_sems, recv_sems, shards, zones, after, *, name):
    n = len(shards)

    def body(*refs):
        for cp in _gather_copies(refs[:n], refs[n:2 * n], refs[2 * n], refs[2 * n + 1]):
            cp.wait_send()
            cp.wait_recv()

    outs = pl.pallas_call(
        body, name=name, out_shape=[pltpu.HBM(a.shape, a.dtype) for a in (*shards, *zones)],
        in_specs=[*([HBM_SPEC] * (2 * n)), SEM_SPEC, SEM_SPEC, *([ANY] * len(after))],
        out_specs=[HBM_SPEC] * (2 * n), input_output_aliases={i: i for i in range(2 * n)},
        compiler_params=pltpu.CompilerParams(has_side_effects=SIDE_EFFECT),
    )(*shards, *zones, send_sems, recv_sems, *after)
    return outs[:n], outs[n:]


def _gather_finish(shards, zones, *, name):
    n = len(shards)

    def body(*refs):
        ins, zones_in, outs, scr = refs[:n], refs[n:2 * n], refs[2 * n:3 * n], refs[3 * n:]
        x, y, c, chips = _place()
        me = 2 * x + y
        sibling = (x, y, 1 - c)
        others = [2 * cx + cy for cx, cy in chips]
        chans = [scr[CHANNEL_REFS * t:CHANNEL_REFS * (t + 1)] for t in range(n)]
        for t in range(n):
            _copy_blocks([ins[t].at[h] for h in range(2)], [outs[t].at[me, h] for h in range(2)], chans[t])
        _exchange_block_streams([([zones_in[t].at[k, c] for k in others], [outs[t].at[k, 1 - c] for k in others],
                                  [None] * len(others), chans[t]) for t in range(n)], sibling)

    scratch = []
    for s in shards:
        scratch += _channel_scratch(s.shape[2], s.dtype, rows=s.shape[1])
    return pl.pallas_call(
        body, in_specs=[ANY] * (2 * n), out_specs=[ANY] * n, out_shape=[SDS(z.shape, z.dtype) for z in zones],
        input_output_aliases={n + t: t for t in range(n)}, scratch_shapes=scratch,
        compiler_params=pltpu.CompilerParams(vmem_limit_bytes=VMEM_LIMIT), name=name)(*shards, *zones)


def _pair_share(groups, *, name):
    finals = [f for grp in groups for f in grp]
    n, n_out = len(finals), len(groups)

    def body(*refs):
        ins, outs, scr = refs[:n], refs[n:n + n_out], refs[n + n_out:]
        x, y, c, _ = _place()
        sibling = (x, y, 1 - c)
        t, streams = 0, []
        for o, grp in enumerate(groups):
            rows = grp[0].shape[0] // 2
            blocks = [(layer, pl.ds(b * rows, rows)) for layer in range(len(grp)) for b in range(2)]
            streams.append(([ins[t + layer].at[rs] for layer, rs in blocks],
                            [outs[o].at[layer, 1 - c, rs] for layer, rs in blocks],
                            [outs[o].at[layer, c, rs] for layer, rs in blocks],
                            scr[CHANNEL_REFS * o:CHANNEL_REFS * (o + 1)]))
            t += len(grp)
        _exchange_block_streams(streams, sibling)

    scratch = []
    for grp in groups:
        scratch += _channel_scratch(grp[0].shape[1], grp[0].dtype, rows=grp[0].shape[0] // 2)
    return pl.pallas_call(
        body, in_specs=[ANY] * n, out_specs=[ANY] * n_out,
        out_shape=[SDS((len(grp), 2, *grp[0].shape), grp[0].dtype) for grp in groups],
        scratch_shapes=scratch, compiler_params=pltpu.CompilerParams(vmem_limit_bytes=VMEM_LIMIT), name=name)(*finals)


def _all_reduce_small(v, *, name):
    rows, lanes = v.shape
    n_dev = 8

    def body(v_ref, o_ref, all_ref, send_sems, recv_sems, local_sem):
        x, y, c, chips = _place()
        me, sibling = (x, y, c), (x, y, 1 - c)

        def block(px, py, pc):
            return all_ref.at[4 * px + 2 * py + pc]

        def copy(k, blk, to, src=None):
            return _remote(block(*blk) if src is None else src, block(*blk), send_sems.at[k], recv_sems.at[k], to)

        mine = pltpu.make_async_copy(v_ref, block(*me), local_sem)
        mine.start()
        first = [copy(0, me, sibling, src=v_ref)]
        first += [copy(1 + j, me, (*chip, c), src=v_ref) for j, chip in enumerate(chips)]
        for cp in first:
            cp.start()
        passed = [copy(4 + j, (*chip, c), sibling) for j, chip in enumerate(chips)]
        for j, chip in enumerate(chips):
            copy(1 + j, (*chip, c), me).wait_recv()
            passed[j].start()
        copy(0, sibling, me).wait_recv()
        for j, chip in enumerate(chips):
            copy(4 + j, (*chip, 1 - c), me).wait_recv()
        for cp in first + passed:
            cp.wait_send()
        mine.wait()
        acc = all_ref[0]
        for k in range(1, n_dev):
            acc = acc + all_ref[k]
        o_ref[...] = acc

    vmem = pl.BlockSpec(memory_space=pltpu.VMEM)
    return pl.pallas_call(
        body, in_specs=[vmem], out_specs=vmem, out_shape=SDS((rows, lanes), F32),
        scratch_shapes=[pltpu.VMEM((n_dev, rows, lanes), F32), pltpu.SemaphoreType.DMA((7,)),
                        pltpu.SemaphoreType.DMA((7,)), pltpu.SemaphoreType.DMA],
        compiler_params=pltpu.CompilerParams(vmem_limit_bytes=VMEM_LIMIT), name=name)(v)


def _relu2_epilogue(acc):
    return acc, jnp.square(jnp.maximum(acc, 0.0))


def _res_epilogue(acc, res):
    return (acc + res,)


def _drelu2_epilogue(acc, pre):
    return (acc * (2.0 * jnp.maximum(pre.astype(F32), 0.0)),)


def _ffn_fwd(h, g, w1, w2, tag):
    f = _rms_fwd(h, g, name=f"ffn_norm_{tag}")
    pre, act = _mm_nn(f, w1, name=f"ffn1_{tag}", epilogue=_relu2_epilogue, n_out_dtypes=(BF16, BF16))
    h_out = _mm_nn(act, w2, name=f"ffn2_{tag}", extras=(h,), epilogue=_res_epilogue)
    return h_out, (f, pre, act)


def _ffn_bwd(dh, h, g, w1, w2, saved, layer, after=()):
    f, pre, act = saved
    dpre = _mm_nt(dh, w2, name=f"ffn2_dx_{layer}", out_dtype=BF16, extras=(pre,), epilogue=_drelu2_epilogue,
                  after=after)
    dw2 = _mm_tn_stacked(act, dh, name=f"ffn2_dw_{layer}", col_slots=False)
    df = _mm_nt(dpre, w1, name=f"ffn1_dx_{layer}")
    dw1 = _mm_tn_stacked(f, dpre, name=f"ffn1_dw_{layer}", col_slots=True)
    dh, dg = _rms_bwd(h, g, df, dh, name=f"ffn_norm_bwd_{layer}")
    return dh, dg, dw1, dw2


def _kv_fwd(mem, g, w_kv, tag):
    m = _rms_fwd(mem, g, name=f"mem_norm_{tag}")
    return m, _mm_nn(m, w_kv, name=f"kv_{tag}")


def _kv_bwd(mem, g, w_kv, m, dk, dv, layer):
    dkv = jnp.concatenate([dk, dv], axis=1)
    dw = _mm_tn_stacked(m, dkv, name=f"kv_dw_{layer}", col_slots=True)
    dm = _mm_nt(dkv, w_kv, name=f"kv_dx_{layer}")
    _, dg = _rms_bwd(mem, g, dm, dm, name=f"mem_norm_bwd_{layer}")
    return dw, dg


def _local_step(x, mem, target, p, after_layer1=None, after_ffn0=None, after_mixer0=None):
    row = lambda v: v.reshape(1, -1)
    g = {}

    h0 = x
    a0 = _rms_fwd(h0, row(p["norm_mix"][0]), name="mix_norm_0")
    proj_a = _mm_nn(a0, p["a_in"], name="a_in", after=p.get("after_start", ()))
    m0, kv0 = _kv_fwd(mem, row(p["mem_norm"][0]), p["w_kv"][0], "0")
    cat0 = _attn_fwd(proj_a, 2 * D_INNER, kv0, name="attn_0")
    bs_col = p["a_bs"].reshape(A_GROUPS, CHUNK, 1)
    cat0 = _gate_fwd(proj_a, p["a_ln_g"], p["a_ln_b"], p["a_ws"], bs_col, cat0, name="gate")
    h1 = _mm_nn(cat0, p["w_out"][0], name="out_0", extras=(h0,), epilogue=_res_epilogue)
    w_ffn1_0, w_ffn2_0 = p["layer0_ffn"](h1) if "layer0_ffn" in p else (p["w_ffn1"][0], p["w_ffn2"][0])
    h2, ffn0 = _ffn_fwd(h1, row(p["norm_ffn"][0]), w_ffn1_0, w_ffn2_0, "0")

    w_kv1, b_in = p["layer1_mixer"](h2) if "layer1_mixer" in p else (p["w_kv"][1], p["b_in"])
    a1 = _rms_fwd(h2, row(p["norm_mix"][1]), name="mix_norm_1")
    proj_b = _mm_nn(a1, b_in, name="b_in")
    m1, kv1 = _kv_fwd(mem, row(p["mem_norm"][1]), w_kv1, "1")
    cat1 = _attn_fwd(proj_b, B_Q_OFF, kv1, name="attn_1")
    xbc = _conv_fwd(proj_b, p["b_conv_w"], p["b_conv_b"], name="conv")
    dt_raw = proj_b[:, B_DT_OFF:B_DT_OFF + SSM_HEADS].reshape(SEQ, SSM_GROUPS, SSM_HPG)
    dt_c = jnp.transpose(dt_raw, (1, 0, 2))
    dt_r = jnp.transpose(dt_raw, (1, 2, 0))
    per_head = lambda v: v.reshape(SSM_GROUPS, 1, SSM_HPG)
    par_row = jnp.concatenate([per_head(p["b_dt_bias"]), per_head(p["b_a_log"]), per_head(p["b_d"])], axis=1)
    ssd_par = (par_row, jnp.transpose(par_row[:, :2], (0, 2, 1)), p["b_gnorm"])
    cat1, hprev = _ssd_fwd(xbc, proj_b, dt_c, dt_r, *ssd_par, cat1, name="ssd")
    if "layer1_rest" in p:
        w_out1, w_ffn1_1, w_ffn2_1 = p["layer1_rest"](cat1)
    else:
        w_out1, w_ffn1_1, w_ffn2_1 = p["w_out"][1], p["w_ffn1"][1], p["w_ffn2"][1]
    h3 = _mm_nn(cat1, w_out1, name="out_1", extras=(h2,), epilogue=_res_epilogue)
    h4, ffn1 = _ffn_fwd(h3, row(p["norm_ffn"][1]), w_ffn1_1, w_ffn2_1, "1")

    loss, dh, g["final_norm"] = _loss_head(h4, row(p["final_norm"]), target, name="loss_head")

    dh, dnf1, dw1_1, dw2_1 = _ffn_bwd(dh, h3, row(p["norm_ffn"][1]), w_ffn1_1, w_ffn2_1, ffn1, 1)
    dcat1 = _mm_nt(dh, w_out1, name="out_dx_1")
    dwo_1 = _mm_tn_stacked(cat1, dh, name="out_dw_1", col_slots=False)
    dproj_b, dk1, dv1 = _attn_bwd(proj_b, B_Q_OFF, kv1, dcat1, B_IN_PAD, B_Q_OFF, name="attn_bwd_1")
    dproj_b, dxs, dbm, dcm, ddt_c, ddt_r, dpar_row, dpar_col, g["b_gnorm"] = _ssd_bwd(
        xbc, proj_b, dt_c, dt_r, *ssd_par, hprev, dcat1, dproj_b, name="ssd_bwd")
    dpar = dpar_row.at[:, :2].add(jnp.transpose(dpar_col, (0, 2, 1)))
    g["b_dt_bias"], g["b_a_log"], g["b_d"] = dpar[:, 0], dpar[:, 1], dpar[:, 2]
    dproj_b, g["b_conv_w"], g["b_conv_b"] = _conv_bwd(proj_b, p["b_conv_w"], p["b_conv_b"], dxs, dbm, dcm, dproj_b,
                                                      name="conv_bwd")
    ddt = jnp.transpose(ddt_c, (1, 0, 2)) + jnp.transpose(ddt_r, (2, 0, 1))
    ddt = jnp.pad(ddt.reshape(SEQ, SSM_HEADS), ((0, 0), (0, B_IN_PAD - B_DT_OFF - SSM_HEADS))).astype(BF16)
    dproj_b = lax.dynamic_update_slice(dproj_b, ddt, (0, B_DT_OFF))
    dwkv_1, dmn1 = _kv_bwd(mem, row(p["mem_norm"][1]), w_kv1, m1, dk1, dv1, 1)
    dwb = _b_in_grad_slots(_mm_tn(a1, dproj_b, name="b_in_dw"))
    da1 = _mm_nt(dproj_b, b_in, name="b_in_dx")
    dh, dnm1 = _rms_bwd(h2, row(p["norm_mix"][1]), da1, dh, name="mix_norm_bwd_1")
    layer1 = dict(w_kv=dwkv_1, w_out=dwo_1, w_ffn1=dw1_1, w_ffn2=dw2_1, b_in=dwb)
    token = () if after_layer1 is None else (after_layer1(layer1),)

    dh, dnf0, dw1_0, dw2_0 = _ffn_bwd(dh, h1, row(p["norm_ffn"][0]), w_ffn1_0, w_ffn2_0, ffn0, 0,
                                      after=token)
    ffn0_grads = dict(w_ffn1=dw1_0, w_ffn2=dw2_0)
    token = () if after_ffn0 is None else (after_ffn0(ffn0_grads),)
    dcat0 = _mm_nt(dh, p["w_out"][0], name="out_dx_0", after=token)
    dwo_0 = _mm_tn_stacked(cat0, dh, name="out_dw_0", col_slots=False)
    dproj_a, dk0, dv0 = _attn_bwd(proj_a, 2 * D_INNER, kv0, dcat0, A_IN, 2 * D_INNER, name="attn_bwd_0")
    dproj_a, g["a_ln_g"], g["a_ln_b"], g["a_ws"], dbs_col = _gate_bwd(
        proj_a, p["a_ln_g"], p["a_ln_b"], p["a_ws"], bs_col, dcat0, dproj_a, name="gate_bwd")
    g["a_bs"] = dbs_col.reshape(A_GROUPS, CHUNK)
    dwkv_0, dmn0 = _kv_bwd(mem, row(p["mem_norm"][0]), p["w_kv"][0], m0, dk0, dv0, 0)
    dwa = _mm_tn_stacked(a0, dproj_a, name="a_in_dw", col_slots=True)
    mixer0_grads = dict(w_kv=dwkv_0, w_out=dwo_0, a_in=dwa)
    token = () if after_mixer0 is None else (after_mixer0(mixer0_grads),)
    da0 = _mm_nt(dproj_a, p["a_in"], name="a_in_dx", after=token)
    dx, dnm0 = _rms_bwd(h0, row(p["norm_mix"][0]), da0, dh, name="mix_norm_bwd_0")

    g["norm_mix"] = jnp.concatenate([dnm0, dnm1], axis=0)
    g["norm_ffn"] = jnp.concatenate([dnf0, dnf1], axis=0)
    g["mem_norm"] = jnp.concatenate([dmn0, dmn1], axis=0)
    layer0 = dict(w_kv=dwkv_0, w_out=dwo_0, w_ffn1=dw1_0, w_ffn2=dw2_0, a_in=dwa)
    return loss, dx, g, layer0, layer1


def _b_in_full(gathered):
    n = B_IN // N_CHIPS
    dt0 = D_INNER + CONV_DIM - (N_CHIPS - 1) * n
    last = gathered[N_CHIPS - 1]
    return jnp.concatenate([*[gathered[k] for k in range(N_CHIPS - 1)], last[:, :dt0], last[:, dt0 + SSM_HEADS:],
                            last[:, dt0:dt0 + SSM_HEADS], jnp.zeros((D_MODEL, B_IN_PAD - B_IN), last.dtype)], axis=1)


def _b_in_grad_slots(d):
    n = B_IN // N_CHIPS
    dt0 = D_INNER + CONV_DIM
    last = jnp.concatenate([d[:, (N_CHIPS - 1) * n:dt0], d[:, B_DT_OFF:B_DT_OFF + SSM_HEADS], d[:, dt0:B_DT_OFF]], axis=1)
    slots = [*[d[:, k * n:(k + 1) * n] for k in range(N_CHIPS - 1)], last]
    half = D_MODEL // 2
    return jnp.stack([jnp.stack([s[h * half:(h + 1) * half] for s in slots]) for h in range(2)])


SMALL_REPL = ("norm_mix", "norm_ffn", "mem_norm", "a_ln_g", "a_ln_b", "a_ws", "a_bs", "b_dt_bias", "b_a_log", "b_d",
              "final_norm")
SMALL_SHARD = ("b_conv_w", "b_conv_b", "b_gnorm")
WEIGHTS = ("norm_mix", "norm_ffn", "mem_norm", "w_kv", "w_out", "w_ffn1", "w_ffn2", "a_in", "a_ln_g", "a_ln_b", "a_ws",
           "a_bs", "b_in", "b_conv_w", "b_conv_b", "b_dt_bias", "b_a_log", "b_d", "b_gnorm", "final_norm")
CONV_SHARD = CONV_DIM // N_CHIPS
GN_SHARD = D_INNER // N_CHIPS


LAYERED = ("w_kv", "w_out", "w_ffn1", "w_ffn2")


def _gather_weights(w):
    halves = lambda k, layer: (w[k][layer] if k in LAYERED else w[k][0]).reshape(2, -1, w[k].shape[-1]).astype(BF16)
    small = jnp.zeros((2, CONV_K, CONV_SHARD), F32)
    small = small.at[0].set(w["b_conv_w"][0])
    small = small.at[1, 0].set(w["b_conv_b"][0])
    small = small.at[1, 1, :GN_SHARD].set(w["b_gnorm"][0])
    first_names = ("w_kv", "w_out", "a_in")
    gathered = _all_gather_shards([halves(k, 0) for k in first_names], small, name="gather_weights_0")
    got = dict(zip(first_names, gathered))
    slots = lambda a: a.reshape(N_CHIPS, -1, a.shape[-1])
    rows = lambda a: a.reshape(-1, a.shape[-1])
    p = dict(w_kv=[slots(got["w_kv"])], w_out=[rows(got["w_out"])], a_in=slots(got["a_in"]))
    sm = gathered[-1]
    p["b_conv_w"] = jnp.transpose(sm[:, 0], (1, 0, 2)).reshape(CONV_K, CONV_DIM)
    p["b_conv_b"] = sm[:, 1, 0].reshape(1, CONV_DIM)
    p["b_gnorm"] = sm[:, 1, 1, :GN_SHARD].reshape(1, D_INNER)

    after, started = (gathered[0],), {}
    for tag, layer, names in (("0_ffn", 0, ("w_ffn1", "w_ffn2")), ("1_mixer", 1, ("w_kv", "b_in")),
                              ("1_rest", 1, ("w_out", "w_ffn1", "w_ffn2"))):
        started[tag] = _gather_start([halves(k, layer) for k in names], after, name=f"gather_start_{tag}")
        after = (started[tag][-1],)
    p["after_start"] = after

    def finish(tag, first):
        send_sems, recv_sems, shards, zones, _ = started[tag]
        shards, zones = _gather_wait(send_sems, recv_sems, shards, zones, (first,), name=f"gather_wait_{tag}")
        return _gather_finish(shards, zones, name=f"gather_finish_{tag}")

    def layer0_ffn(first):
        w1, w2 = finish("0_ffn", first)
        return slots(w1), rows(w2)

    def layer1_mixer(first):
        kv, b_in = finish("1_mixer", first)
        return slots(kv), _b_in_full(slots(b_in))

    def layer1_rest(first):
        wo, w1, w2 = finish("1_rest", first)
        return rows(wo), slots(w1), rows(w2)

    p.update(layer0_ffn=layer0_ffn, layer1_mixer=layer1_mixer, layer1_rest=layer1_rest)
    return p


def _pair_parts(grads, tag):
    stacks = [g.reshape(2, -1, g.shape[-1]) for g in grads.values()]
    parts = _pair_reduce(stacks, name=f"grads_pair_reduce_{tag}")
    return [t.reshape(N_CHIPS, -1, t.shape[-1]) for t in parts]


def _chip_sums(chip, names, parts, landed, tag):
    return {k: _sum_contributions(chip, t, u, name=f"grads_chip_sum_{k}_{tag}")
            for k, t, u in zip(names, parts, landed)}


def _small_layout(shapes):
    offs, o = {}, 0
    for k in (*SMALL_REPL, *SMALL_SHARD):
        size = math.prod(shapes[k])
        offs[k] = (o, size)
        o += size
    rows = -(-(o + 1) // (8 * 128)) * 8
    return offs, rows


def _reduce_small(g, loss_part, full_shapes):
    offs, rows = _small_layout(full_shapes)
    flat = jnp.concatenate([*[g[k].reshape(-1) for k in (*SMALL_REPL, *SMALL_SHARD)], loss_part[0, :1]])
    flat = jnp.pad(flat, (0, rows * 128 - flat.shape[0])).reshape(rows, 128)
    total = _all_reduce_small(flat, name="small_all_reduce").reshape(-1)
    end = max(o + n for o, n in offs.values())
    return {k: total[o:o + n].reshape(full_shapes[k]) for k, (o, n) in offs.items()}, total[end]


def kernel(x, mem, norm_mix, norm_ffn, mem_norm, w_kv, w_out, w_ffn1, w_ffn2, a_in, a_ln_g, a_ln_b, a_ws, a_bs, b_in, b_conv_w, b_conv_b, b_dt_bias, b_a_log, b_d, b_gnorm, final_norm, loss_target, m_norm_mix, m_norm_ffn, m_mem_norm, m_w_kv, m_w_out, m_w_ffn1, m_w_ffn2, m_a_in, m_a_ln_g, m_a_ln_b, m_a_ws, m_a_bs, m_b_in, m_b_conv_w, m_b_conv_b, m_b_dt_bias, m_b_a_log, m_b_d, m_b_gnorm, m_final_norm, v_norm_mix, v_norm_ffn, v_mem_norm, v_w_kv, v_w_out, v_w_ffn1, v_w_ffn2, v_a_in, v_a_ln_g, v_a_ln_b, v_a_ws, v_a_bs, v_b_in, v_b_conv_w, v_b_conv_b, v_b_dt_bias, v_b_a_log, v_b_d, v_b_gnorm, v_final_norm):
    w = dict(norm_mix=norm_mix, norm_ffn=norm_ffn, mem_norm=mem_norm, w_kv=w_kv, w_out=w_out, w_ffn1=w_ffn1,
             w_ffn2=w_ffn2, a_in=a_in, a_ln_g=a_ln_g, a_ln_b=a_ln_b, a_ws=a_ws, a_bs=a_bs, b_in=b_in, b_conv_w=b_conv_w,
             b_conv_b=b_conv_b, b_dt_bias=b_dt_bias, b_a_log=b_a_log, b_d=b_d, b_gnorm=b_gnorm, final_norm=final_norm)
    mom = dict(norm_mix=m_norm_mix, norm_ffn=m_norm_ffn, mem_norm=m_mem_norm, w_kv=m_w_kv, w_out=m_w_out,
               w_ffn1=m_w_ffn1, w_ffn2=m_w_ffn2, a_in=m_a_in, a_ln_g=m_a_ln_g, a_ln_b=m_a_ln_b, a_ws=m_a_ws,
               a_bs=m_a_bs, b_in=m_b_in, b_conv_w=m_b_conv_w, b_conv_b=m_b_conv_b, b_dt_bias=m_b_dt_bias,
               b_a_log=m_b_a_log, b_d=m_b_d, b_gnorm=m_b_gnorm, final_norm=m_final_norm)
    var = dict(norm_mix=v_norm_mix, norm_ffn=v_norm_ffn, mem_norm=v_mem_norm, w_kv=v_w_kv, w_out=v_w_out,
               w_ffn1=v_w_ffn1, w_ffn2=v_w_ffn2, a_in=v_a_in, a_ln_g=v_a_ln_g, a_ln_b=v_a_ln_b, a_ws=v_a_ws,
               a_bs=v_a_bs, b_in=v_b_in, b_conv_w=v_b_conv_w, b_conv_b=v_b_conv_b, b_dt_bias=v_b_dt_bias,
               b_a_log=v_b_a_log, b_d=v_b_d, b_gnorm=v_b_gnorm, final_norm=v_final_norm)

    p = _gather_weights(w)
    p.update(norm_mix=norm_mix, norm_ffn=norm_ffn, mem_norm=mem_norm, a_ln_g=a_ln_g, a_ln_b=a_ln_b, a_ws=a_ws[0],
             a_bs=a_bs[0], b_dt_bias=b_dt_bias, b_a_log=b_a_log, b_d=b_d, final_norm=final_norm)
    chip = 2 * lax.axis_index("x") + lax.axis_index("y")
    chip_arr = jnp.reshape(chip, (1,)).astype(jnp.int32)
    started = {}

    def start_scatter(tag):
        def hook(grads):
            start = _chip_scatter_start(_pair_parts(grads, tag), name=f"grads_chip_scatter_start_{tag}")
            started[tag] = (tuple(grads), start)
            return start[-1]
        return hook

    loss_part, dx, g, _, _ = _local_step(x[0], mem[0], loss_target[0], p, start_scatter("1"), start_scatter("0f"),
                                         start_scatter("0m"))
    full_shapes = {k: w[k].shape for k in SMALL_REPL}
    full_shapes.update(b_conv_w=(1, CONV_K, CONV_DIM), b_conv_b=(1, CONV_DIM), b_gnorm=(1, D_INNER))
    grads, loss = _reduce_small(g, loss_part, full_shapes)
    grads["b_conv_w"] = lax.dynamic_slice_in_dim(grads["b_conv_w"], chip * CONV_SHARD, CONV_SHARD, axis=2)
    grads["b_conv_b"] = lax.dynamic_slice_in_dim(grads["b_conv_b"], chip * CONV_SHARD, CONV_SHARD, axis=1)
    grads["b_gnorm"] = lax.dynamic_slice_in_dim(grads["b_gnorm"], chip * GN_SHARD, GN_SHARD, axis=1)

    def finish_scatter(tag, *first):
        names, (send_sems, recv_sems, parts, lands, _) = started[tag]
        parts, landed = _chip_scatter_wait(send_sems, recv_sems, parts, lands, first,
                                           name=f"grads_chip_scatter_wait_{tag}")
        return _chip_sums(chip_arr, names, parts, landed, tag)

    def adamw(names, grads):
        for k in names:
            shape = w[k].shape
            if len(shape) == 3 and shape[2] % 128 and not shape[1] % 128:
                flat = unflat = lambda a: jnp.transpose(a, (0, 2, 1))
            else:
                flat = (lambda a: a) if len(shape) == 3 else (lambda a: a.reshape(1, -1, shape[-1]))
                unflat = lambda a: a.reshape(shape)
            d, m_new, v_new = _adamw(flat(w[k]), flat(grads[k]), flat(mom[k]), flat(var[k]), name=f"adamw_{k}")
            delta[k], new_m[k], new_v[k] = unflat(d), unflat(m_new), unflat(v_new)

    delta, new_m, new_v = {}, {}, {}
    halves = [finish_scatter("0f", dx), finish_scatter("1", dx)]
    early = ("w_ffn1", "w_ffn2", "b_in")
    shared = _pair_share([[halves[layer][k] for layer in range(2) if k in halves[layer]] for k in early],
                         name="grads_pair_share_early")
    grads.update({k: a.reshape(w[k].shape) for k, a in zip(early, shared)})
    adamw([k for k in WEIGHTS if k in grads], grads)
    halves[0].update(finish_scatter("0m", delta["w_ffn2"]))
    late = ("w_kv", "w_out", "a_in")
    shared = _pair_share([[halves[layer][k] for layer in range(2) if k in halves[layer]] for k in late],
                         name="grads_pair_share_late")
    grads.update({k: a.reshape(w[k].shape) for k, a in zip(late, shared)})
    adamw(late, grads)

    return (loss, dx.reshape(x.shape), *[grads[k] for k in WEIGHTS], *[delta[k] for k in WEIGHTS],
            *[new_m[k] for k in WEIGHTS], *[new_v[k] for k in WEIGHTS])
```

```python
import math

import jax
import jax.numpy as jnp
from jax import lax
from jax.experimental import pallas as pl
from jax.experimental.pallas import tpu as pltpu

F32 = jnp.float32
BF16 = jnp.bfloat16
SDS = jax.ShapeDtypeStruct

D_MODEL = 1024
SEQ = 2048
CHUNK = 128
N_MEM = 256
D_INNER = 2048
A_GROUPS = 8
A_GROUP_W = D_INNER // A_GROUPS
SSM_HEADS = 32
SSM_HEAD_DIM = 64
SSM_GROUPS = 4
SSM_HPG = 8
SSM_STATE = 128
SSM_GROUP_W = SSM_HPG * SSM_HEAD_DIM
CONV_K = 4
CONV_DIM = 3072
X_HEADS = 4
X_HEAD_DIM = 256
X_WIDTH = 1024
MIX_OUT = 3072
D_FF = 4096
A_IN = 5120
B_IN = 6176
B_IN_PAD = 6272
B_Q_OFF = 5120
B_DT_OFF = 6144
N_CHUNKS = SEQ // CHUNK
EPS = 1e-6
N_CHIPS = 4

ADAM_LR = 0.001
ADAM_B1 = 0.9
ADAM_B2 = 0.999
ADAM_EPS = 1e-08
ADAM_WD = 0.01
ADAM_STEP = 10

VMEM_LIMIT = 48 * 1024 * 1024
MESH = pl.DeviceIdType.MESH


def _cparams(sem):
    return pltpu.CompilerParams(dimension_semantics=sem, vmem_limit_bytes=VMEM_LIMIT)


def _dot(a, b, dims=(((1,), (0,)), ((), ()))):
    return lax.dot_general(a.astype(BF16), b.astype(BF16), dims, preferred_element_type=F32)


def _dot_nt(a, b):
    return _dot(a, b, (((1,), (1,)), ((), ())))


def _dot_tn(a, b):
    return _dot(a, b, (((0,), (0,)), ((), ())))


def _pick(n, cands):
    for c in cands:
        if n % c == 0:
            return c
    raise ValueError(f"no tile for {n}")


def _mm_call(a, b, *, dims, grid, a_spec, b_spec, acc_shape, out_shapes, out_specs, name,
             extras=(), extra_specs=(), epilogue=None, after=()):
    n_k = grid[2]
    n_extra = len(extras)
    n_out = len(out_shapes)
    n_in = 2 + n_extra + len(after)

    def finish(total, extra_refs, out_refs):
        vals = (total,) if epilogue is None else epilogue(total, *[e[...] for e in extra_refs])
        for o_ref, v in zip(out_refs, vals):
            o_ref[...] = v.astype(o_ref.dtype)

    def body_one_step(*refs):
        finish(_dot(refs[0][...], refs[1][...], dims), refs[2:2 + n_extra], refs[n_in:n_in + n_out])

    def body(*refs):
        acc = refs[-1]
        k = pl.program_id(2)

        @pl.when(k == 0)
        def _():
            acc[...] = jnp.zeros_like(acc)

        acc[...] += _dot(refs[0][...], refs[1][...], dims)

        @pl.when(k == n_k - 1)
        def _():
            finish(acc[...], refs[2:2 + n_extra], refs[n_in:n_in + n_out])

    return pl.pallas_call(
        body_one_step if n_k == 1 else body, grid=grid,
        in_specs=[a_spec, b_spec, *extra_specs, *([ANY] * len(after))], out_specs=list(out_specs),
        out_shape=list(out_shapes), scratch_shapes=[] if n_k == 1 else [pltpu.VMEM(acc_shape, F32)],
        compiler_params=_cparams(("parallel", "parallel", "arbitrary")), name=name,
    )(a, b, *extras, *after)


def _w_dims(w):
    if w.ndim == 2:
        return w.shape[0], w.shape[1], 1, w.shape[1]
    return w.shape[1], w.shape[0] * w.shape[2], w.shape[0], w.shape[2]


def _mm_nn(a, w, *, name, out_dtype=F32, a_cols=None, extras=(), epilogue=None, n_out_dtypes=None, after=()):
    m = a.shape[0]
    k_dim, n_dim, _, n_slot = _w_dims(w)
    a_off, a_w = (0, a.shape[1]) if a_cols is None else a_cols
    assert a_w == k_dim
    tm = _pick(m, (2048, 1024, 512, 256))
    tn = _pick(n_slot, (512, 896, 640, 256, 128))
    tk = _pick(k_dim, (1024, 768, 512, 384, 256, 128))
    assert a_off % tk == 0
    nb = n_slot // tn
    a_spec = pl.BlockSpec((tm, tk), lambda i, j, k: (i, a_off // tk + k))
    if w.ndim == 2:
        b_spec = pl.BlockSpec((tk, tn), lambda i, j, k: (k, j))
    else:
        b_spec = pl.BlockSpec((None, tk, tn), lambda i, j, k: (j // nb, k, j % nb))
    o_spec = pl.BlockSpec((tm, tn), lambda i, j, k: (i, j))
    dts = n_out_dtypes or (out_dtype,)
    outs = _mm_call(a, w, dims=(((1,), (0,)), ((), ())), grid=(m // tm, n_dim // tn, k_dim // tk),
                    a_spec=a_spec, b_spec=b_spec, acc_shape=(tm, tn),
                    out_shapes=[SDS((m, n_dim), dt) for dt in dts], out_specs=[o_spec] * len(dts), name=name,
                    extras=extras, extra_specs=[o_spec] * len(extras), epilogue=epilogue, after=after)
    return outs if n_out_dtypes else outs[0]


def _mm_nt(a, w, *, name, out_dtype=F32, extras=(), epilogue=None, after=()):
    m = a.shape[0]
    k_dim, n_dim, _, n_slot = _w_dims(w)
    assert a.shape[1] == n_dim
    tm = _pick(m, (2048, 1024, 512, 256))
    to = _pick(k_dim, (512, 384, 256, 128))
    tc = _pick(n_slot, (1280, 1024, 896, 640, 512, 256, 128))
    nb = n_slot // tc
    a_spec = pl.BlockSpec((tm, tc), lambda i, j, k: (i, k))
    if w.ndim == 2:
        b_spec = pl.BlockSpec((to, tc), lambda i, j, k: (j, k))
    else:
        b_spec = pl.BlockSpec((None, to, tc), lambda i, j, k: (k // nb, j, k % nb))
    o_spec = pl.BlockSpec((tm, to), lambda i, j, k: (i, j))
    return _mm_call(a, w, dims=(((1,), (1,)), ((), ())), grid=(m // tm, k_dim // to, n_dim // tc),
                    a_spec=a_spec, b_spec=b_spec, acc_shape=(tm, to),
                    out_shapes=[SDS((m, k_dim), out_dtype)], out_specs=[o_spec], name=name,
                    extras=extras, extra_specs=[o_spec] * len(extras), epilogue=epilogue, after=after)[0]


def _mm_tn(x, dy, *, name, x_cols=None):
    s = x.shape[0]
    x_off, k_dim = (0, x.shape[1]) if x_cols is None else x_cols
    n_dim = dy.shape[1]
    tm = _pick(k_dim, (1024, 768, 512, 384, 256, 128))
    tn = _pick(n_dim, (512, 896, 640, 256, 128))
    tk = _pick(s, (2048, 1024, 512, 256))
    assert x_off % tm == 0
    a_spec = pl.BlockSpec((tk, tm), lambda i, j, k: (k, x_off // tm + i))
    b_spec = pl.BlockSpec((tk, tn), lambda i, j, k: (k, j))
    o_spec = pl.BlockSpec((tm, tn), lambda i, j, k: (i, j))
    return _mm_call(x, dy, dims=(((0,), (0,)), ((), ())), grid=(k_dim // tm, n_dim // tn, s // tk),
                    a_spec=a_spec, b_spec=b_spec, acc_shape=(tm, tn),
                    out_shapes=[SDS((k_dim, n_dim), F32)], out_specs=[o_spec], name=name)[0]


def _mm_tn_stacked(x, dy, *, name, col_slots):
    s, k_dim = x.shape
    n_dim = dy.shape[1]
    r, c = (k_dim // 2, n_dim // N_CHIPS) if col_slots else (k_dim // N_CHIPS // 2, n_dim)
    tm = 2 * r
    tn = _pick(c, (512, 896, 640, 256, 128))
    tk = _pick(s, (2048, 1024, 512, 256))
    a_spec = pl.BlockSpec((tk, tm), lambda i, j, k: (k, i))
    b_spec = pl.BlockSpec((tk, tn), lambda i, j, k: (k, j))
    if col_slots:
        nb = c // tn
        o_spec = pl.BlockSpec((2, None, r, tn), lambda i, j, k: (0, j // nb, 0, j % nb))
    else:
        o_spec = pl.BlockSpec((2, None, r, tn), lambda i, j, k: (0, i, 0, j))
    return _mm_call(x, dy, dims=(((0,), (0,)), ((), ())), grid=(k_dim // tm, n_dim // tn, s // tk),
                    a_spec=a_spec, b_spec=b_spec, acc_shape=(tm, tn), epilogue=lambda acc: (acc.reshape(2, r, tn),),
                    out_shapes=[SDS((2, N_CHIPS, r, c), F32)], out_specs=[o_spec], name=name)[0]


def _rms(x, g):
    return x * lax.rsqrt(jnp.mean(x * x, axis=-1, keepdims=True) + EPS) * g


def _rms_fwd(h, g, *, name):
    rows, d = h.shape
    tr = _pick(rows, (512, 256))

    def body(h_ref, g_ref, o_ref):
        o_ref[...] = _rms(h_ref[...], g_ref[...]).astype(o_ref.dtype)

    return pl.pallas_call(
        body, grid=(rows // tr,),
        in_specs=[pl.BlockSpec((tr, d), lambda i: (i, 0)), pl.BlockSpec((1, d), lambda i: (0, 0))],
        out_specs=pl.BlockSpec((tr, d), lambda i: (i, 0)), out_shape=SDS((rows, d), BF16),
        compiler_params=_cparams(("parallel",)), name=name)(h, g)


def _rms_bwd(h, g, da, dres, *, name):
    rows, d = h.shape
    tr = _pick(rows, (512, 256))

    def body(h_ref, g_ref, da_ref, dres_ref, dh_ref, dg_ref):
        _, vjp = jax.vjp(_rms, h_ref[...], g_ref[...])
        dh, dg = vjp(da_ref[...].astype(F32))
        dh_ref[...] = dres_ref[...] + dh

        @pl.when(pl.program_id(0) == 0)
        def _():
            dg_ref[...] = jnp.zeros_like(dg_ref)

        dg_ref[...] += dg

    row_spec = pl.BlockSpec((tr, d), lambda i: (i, 0))
    vec_spec = pl.BlockSpec((1, d), lambda i: (0, 0))
    return pl.pallas_call(
        body, grid=(rows // tr,), in_specs=[row_spec, vec_spec, row_spec, row_spec],
        out_specs=[row_spec, vec_spec], out_shape=[SDS((rows, d), F32), SDS((1, d), F32)],
        compiler_params=_cparams(("arbitrary",)), name=name)(h, g, da, dres)


def _loss_head(h, g, target, *, name):
    rows, d = h.shape
    tr = _pick(rows, (512, 256))

    def body(h_ref, g_ref, t_ref, loss_ref, dh_ref, dg_ref):
        y, vjp = jax.vjp(_rms, h_ref[...], g_ref[...])
        err = y - t_ref[...]
        dh, dg = vjp(err * (1.0 / d))
        dh_ref[...] = dh

        @pl.when(pl.program_id(0) == 0)
        def _():
            dg_ref[...] = jnp.zeros_like(dg_ref)
            loss_ref[...] = jnp.zeros_like(loss_ref)

        dg_ref[...] += dg
        part = jnp.sum(jnp.sum(err * err, axis=-1, keepdims=True), axis=0, keepdims=True) * (0.5 / d)
        loss_ref[...] += jnp.broadcast_to(part, loss_ref.shape)

    row_spec = pl.BlockSpec((tr, d), lambda i: (i, 0))
    vec_spec = pl.BlockSpec((1, d), lambda i: (0, 0))
    loss_spec = pl.BlockSpec((8, 128), lambda i: (0, 0))
    return pl.pallas_call(
        body, grid=(rows // tr,), in_specs=[row_spec, vec_spec, row_spec],
        out_specs=[loss_spec, row_spec, vec_spec],
        out_shape=[SDS((8, 128), F32), SDS((rows, d), F32), SDS((1, d), F32)],
        compiler_params=_cparams(("arbitrary",)), name=name)(h, g, target)


def _gelu(x):
    return 0.5 * x * (1.0 + lax.erf(x * (1.0 / math.sqrt(2.0))))


def _gate_tile(pu, pv, ln_g, ln_b, ws, bs_t):
    u = [_gelu(p) for p in pu]
    v = [_gelu(p) for p in pv]
    mu = sum(jnp.sum(t, axis=-1, keepdims=True) for t in v) * (1.0 / D_INNER)
    vc = [t - mu for t in v]
    var = sum(jnp.sum(t * t, axis=-1, keepdims=True) for t in vc) * (1.0 / D_INNER)
    rstd = lax.rsqrt(var + EPS)
    row = lax.broadcasted_iota(jnp.int32, (CHUNK, CHUNK), 0)
    col = lax.broadcasted_iota(jnp.int32, (CHUNK, CHUNK), 1)
    out = []
    for gi in range(A_GROUPS):
        vn = vc[gi] * rstd * ln_g[gi] + ln_b[gi]
        w = jnp.where(row >= col, ws[gi], 0.0)
        sv = _dot(w, vn) + bs_t[gi]
        out.append(u[gi] * sv)
    return out


def _split(ref, n, width):
    return [ref[:, i * width:(i + 1) * width] for i in range(n)]


def _gate_in_specs():
    return [
        pl.BlockSpec((CHUNK, D_INNER), lambda c: (c, 0)),
        pl.BlockSpec((CHUNK, D_INNER), lambda c: (c, 1)),
        pl.BlockSpec((1, D_INNER), lambda c: (0, 0)),
        pl.BlockSpec((1, D_INNER), lambda c: (0, 0)),
        pl.BlockSpec((A_GROUPS, CHUNK, CHUNK), lambda c: (0, 0, 0)),
        pl.BlockSpec((A_GROUPS, CHUNK, 1), lambda c: (0, 0, 0)),
    ]


def _gate_args(u_ref, v_ref, g_ref, b_ref, ws_ref, bs_ref):
    ng, gw = A_GROUPS, A_GROUP_W
    return (_split(u_ref, ng, gw), _split(v_ref, ng, gw), _split(g_ref, ng, gw), _split(b_ref, ng, gw),
            [ws_ref[i] for i in range(ng)], [bs_ref[i] for i in range(ng)])


def _gate_fwd(proj, ln_g, ln_b, ws, bs_col, mixcat, *, name):
    def body(u_ref, v_ref, g_ref, b_ref, ws_ref, bs_ref, cat_in, cat_ref):
        del cat_in
        out = _gate_tile(*_gate_args(u_ref, v_ref, g_ref, b_ref, ws_ref, bs_ref))
        for gi, o in enumerate(out):
            cat_ref[:, gi * A_GROUP_W:(gi + 1) * A_GROUP_W] = o.astype(cat_ref.dtype)

    return pl.pallas_call(
        body, grid=(N_CHUNKS,), in_specs=[*_gate_in_specs(), pl.BlockSpec(memory_space=pl.ANY)],
        out_specs=pl.BlockSpec((CHUNK, D_INNER), lambda c: (c, 0)), out_shape=SDS(mixcat.shape, mixcat.dtype),
        input_output_aliases={6: 0}, compiler_params=_cparams(("parallel",)), name=name,
    )(proj, proj, ln_g, ln_b, ws, bs_col, mixcat)


def _gate_bwd(proj, ln_g, ln_b, ws, bs_col, dcat, dproj, *, name):
    ng, gw = A_GROUPS, A_GROUP_W

    def body(u_ref, v_ref, g_ref, b_ref, ws_ref, bs_ref, d_ref, dproj_in, dproj_ref, dg_ref, db_ref, dws_ref, dbs_ref):
        del dproj_in
        args = _gate_args(u_ref, v_ref, g_ref, b_ref, ws_ref, bs_ref)
        _, vjp = jax.vjp(_gate_tile, *args)
        dpu, dpv, dg, db, dws, dbs = vjp(_split(d_ref, ng, gw))
        for gi in range(ng):
            dproj_ref[:, gi * gw:(gi + 1) * gw] = dpu[gi].astype(dproj_ref.dtype)
            dproj_ref[:, D_INNER + gi * gw:D_INNER + (gi + 1) * gw] = dpv[gi].astype(dproj_ref.dtype)

        @pl.when(pl.program_id(0) == 0)
        def _():
            for r in (dg_ref, db_ref, dws_ref, dbs_ref):
                r[...] = jnp.zeros_like(r)

        for gi in range(ng):
            dg_ref[:, gi * gw:(gi + 1) * gw] += dg[gi]
            db_ref[:, gi * gw:(gi + 1) * gw] += db[gi]
            dws_ref[gi] += dws[gi]
            dbs_ref[gi] += dbs[gi]

    in_specs = _gate_in_specs()
    return pl.pallas_call(
        body, grid=(N_CHUNKS,),
        in_specs=[*in_specs, pl.BlockSpec((CHUNK, D_INNER), lambda c: (c, 0)), pl.BlockSpec(memory_space=pl.ANY)],
        out_specs=[pl.BlockSpec((CHUNK, 2 * D_INNER), lambda c: (c, 0)), *in_specs[2:]],
        out_shape=[SDS(dproj.shape, dproj.dtype), SDS((1, D_INNER), F32), SDS((1, D_INNER), F32),
                   SDS((ng, CHUNK, CHUNK), F32), SDS((ng, CHUNK, 1), F32)],
        input_output_aliases={7: 0}, compiler_params=_cparams(("arbitrary",)), name=name,
    )(proj, proj, ln_g, ln_b, ws, bs_col, dcat, dproj)


ATT_TQ = 512


def _attn_tile(q, k, v):
    s = _dot_nt(q, k) * (1.0 / math.sqrt(X_HEAD_DIM))
    s = s - jnp.max(s, axis=-1, keepdims=True)
    e = jnp.exp(s)
    p = e / jnp.sum(e, axis=-1, keepdims=True)
    return _dot(p, v)


def _attn_in_specs(q_blk, order):
    hd = X_HEAD_DIM
    return [
        pl.BlockSpec((ATT_TQ, hd), lambda a, b: (order(a, b)[0], q_blk + order(a, b)[1])),
        pl.BlockSpec((N_MEM, hd), lambda a, b: (0, order(a, b)[1])),
        pl.BlockSpec((N_MEM, hd), lambda a, b: (0, X_HEADS + order(a, b)[1])),
    ]


def _attn_fwd(proj, q_off, kv, *, name):
    order = lambda i, h: (i, h)
    cat_blk = D_INNER // X_HEAD_DIM

    def body(q_ref, k_ref, v_ref, o_ref):
        o_ref[...] = _attn_tile(q_ref[...], k_ref[...], v_ref[...]).astype(o_ref.dtype)

    return pl.pallas_call(
        body, grid=(SEQ // ATT_TQ, X_HEADS), in_specs=_attn_in_specs(q_off // X_HEAD_DIM, order),
        out_specs=pl.BlockSpec((ATT_TQ, X_HEAD_DIM), lambda i, h: (i, cat_blk + h)),
        out_shape=SDS((SEQ, MIX_OUT), BF16), compiler_params=_cparams(("parallel", "parallel")), name=name,
    )(proj, kv, kv)


def _attn_bwd(proj, q_off, kv, dcat, dproj_width, dq_off, *, name):
    order = lambda h, i: (i, h)
    cat_blk = D_INNER // X_HEAD_DIM
    dq_blk = dq_off // X_HEAD_DIM

    def body(q_ref, k_ref, v_ref, do_ref, dq_ref, dk_ref, dv_ref):
        _, vjp = jax.vjp(_attn_tile, q_ref[...], k_ref[...], v_ref[...])
        dq, dk, dv = vjp(do_ref[...])
        dq_ref[...] = dq.astype(dq_ref.dtype)

        @pl.when(pl.program_id(1) == 0)
        def _():
            dk_ref[...] = jnp.zeros_like(dk_ref)
            dv_ref[...] = jnp.zeros_like(dv_ref)

        dk_ref[...] += dk
        dv_ref[...] += dv

    kv_spec = pl.BlockSpec((N_MEM, X_HEAD_DIM), lambda h, i: (0, h))
    return pl.pallas_call(
        body, grid=(X_HEADS, SEQ // ATT_TQ),
        in_specs=[*_attn_in_specs(q_off // X_HEAD_DIM, order),
                  pl.BlockSpec((ATT_TQ, X_HEAD_DIM), lambda h, i: (i, cat_blk + h))],
        out_specs=[pl.BlockSpec((ATT_TQ, X_HEAD_DIM), lambda h, i: (i, dq_blk + h)), kv_spec, kv_spec],
        out_shape=[SDS((SEQ, dproj_width), BF16), SDS((N_MEM, X_WIDTH), F32), SDS((N_MEM, X_WIDTH), F32)],
        compiler_params=_cparams(("parallel", "arbitrary")), name=name,
    )(proj, kv, kv, dcat)


CONV_TC = 512
CONV_ROWS = 128
CONV_HALO = 8


def _shift_down(x, s):
    if s == 0:
        return x
    row = lax.broadcasted_iota(jnp.int32, x.shape, 0)
    return jnp.where(row >= s, pltpu.roll(x, s, 0), 0.0)


def _shift_up(x, s):
    if s == 0:
        return x
    n = x.shape[0]
    row = lax.broadcasted_iota(jnp.int32, x.shape, 0)
    return jnp.where(row < n - s, pltpu.roll(x, n - s, 0), 0.0)


def _conv_pre(x, w_ref, b_ref):
    pre = b_ref[...] + jnp.zeros_like(x)
    for k in range(CONV_K):
        pre = pre + w_ref[k:k + 1, :] * _shift_down(x, CONV_K - 1 - k)
    return pre


def _conv_fwd(proj, w, b, *, name):
    blk0 = D_INNER // CONV_TC

    def body(x_ref, w_ref, b_ref, o_ref):
        pre = _conv_pre(x_ref[...], w_ref, b_ref)
        o_ref[...] = pre * jax.nn.sigmoid(pre)

    return pl.pallas_call(
        body, grid=(CONV_DIM // CONV_TC,),
        in_specs=[pl.BlockSpec((SEQ, CONV_TC), lambda j: (0, blk0 + j)), pl.BlockSpec((CONV_K, CONV_TC), lambda j: (0, j)),
                  pl.BlockSpec((1, CONV_TC), lambda j: (0, j))],
        out_specs=pl.BlockSpec((SEQ, CONV_TC), lambda j: (0, j)), out_shape=SDS((SEQ, CONV_DIM), F32),
        compiler_params=_cparams(("parallel",)), name=name)(proj, w, b)


def _conv_bwd(proj, w, b, dxs, dbm, dcm, dproj, *, name):
    tc = CONV_TC // 2
    blk0 = D_INNER // tc
    n_x = D_INNER // tc
    n_b = SSM_GROUPS * SSM_STATE // tc

    window = CONV_ROWS + 2 * CONV_HALO
    n_rows = SEQ // CONV_ROWS

    def body(x_ref, w_ref, b_ref, dxs_ref, dbm_ref, dcm_ref, dproj_in, dproj_ref, dw_ref, db_ref):
        del dproj_in
        j = pl.program_id(0)

        def rows_step(lanes, first, keep, sums):
            start = first - keep
            if not isinstance(start, int):
                start = pl.multiple_of(start, CONV_HALO)
            rows = pl.ds(start, window)
            down = _shift_down if keep == 0 else lambda v, s: pltpu.roll(v, s, 0) if s else v
            up = _shift_up if keep == 2 * CONV_HALO else lambda v, s: pltpu.roll(v, window - s, 0) if s else v
            x = x_ref[rows, lanes]
            w = w_ref[:, lanes]
            pre = b_ref[:, lanes] + jnp.zeros_like(x)
            for k in range(CONV_K):
                pre = pre + w[k:k + 1, :] * down(x, CONV_K - 1 - k)
            sg = jax.nn.sigmoid(pre)
            dact = jnp.where(j < n_x, dxs_ref[rows, lanes],
                             jnp.where(j < n_x + n_b, dbm_ref[rows, lanes], dcm_ref[rows, lanes]))
            dpre = dact * (sg * (1.0 + pre * (1.0 - sg)))
            kept = lambda v: jnp.sum(v[keep:keep + CONV_ROWS, :], axis=0, keepdims=True)
            dx = jnp.zeros_like(x)
            new_sums = []
            for k in range(CONV_K):
                s = CONV_K - 1 - k
                dx = dx + w[k:k + 1, :] * up(dpre, s)
                new_sums.append(sums[k] + kept(dpre * down(x, s)))
            new_sums.append(sums[CONV_K] + kept(dpre))
            dproj_ref[pl.ds(first, CONV_ROWS), lanes] = dx[keep:keep + CONV_ROWS, :].astype(dproj_ref.dtype)
            return tuple(new_sums)

        for c in range(tc // 128):
            lanes = pl.ds(c * 128, 128)
            sums = rows_step(lanes, 0, 0, (jnp.zeros((1, 128), F32),) * (CONV_K + 1))
            sums = lax.fori_loop(
                1, n_rows - 1,
                lambda i, sums: rows_step(lanes, pl.multiple_of(i * CONV_ROWS, CONV_ROWS), CONV_HALO, sums), sums)
            sums = rows_step(lanes, SEQ - CONV_ROWS, 2 * CONV_HALO, sums)
            for k in range(CONV_K):
                dw_ref[k:k + 1, lanes] = sums[k]
            db_ref[:, lanes] = sums[CONV_K]

    clip = lambda v, hi: jnp.minimum(jnp.maximum(v, 0), hi)
    return pl.pallas_call(
        body, grid=(CONV_DIM // tc,),
        in_specs=[pl.BlockSpec((SEQ, tc), lambda j: (0, blk0 + j)), pl.BlockSpec((CONV_K, tc), lambda j: (0, j)),
                  pl.BlockSpec((1, tc), lambda j: (0, j)),
                  pl.BlockSpec((SEQ, tc), lambda j: (0, clip(j, n_x - 1))),
                  pl.BlockSpec((SEQ, tc), lambda j: (0, clip(j - n_x, n_b - 1))),
                  pl.BlockSpec((SEQ, tc), lambda j: (0, clip(j - n_x - n_b, n_b - 1))),
                  pl.BlockSpec(memory_space=pl.ANY)],
        out_specs=[pl.BlockSpec((SEQ, tc), lambda j: (0, blk0 + j)), pl.BlockSpec((CONV_K, tc), lambda j: (0, j)),
                   pl.BlockSpec((1, tc), lambda j: (0, j))],
        out_shape=[SDS(dproj.shape, dproj.dtype), SDS((CONV_K, CONV_DIM), F32), SDS((1, CONV_DIM), F32)],
        input_output_aliases={6: 0}, compiler_params=_cparams(("parallel",)), name=name,
    )(proj, w, b, dxs, dbm, dcm, dproj)


SSM_PAIRS = SSM_HPG // 2


def _dot_exact01(x, m01, m01_t, x_first, differentiable):
    def product(v, m):
        hi = v.astype(BF16)
        rest = v - hi.astype(F32)
        mid = rest.astype(BF16)
        lo = (rest - mid.astype(F32)).astype(BF16)
        dims = (((1,), (0,)), ((), ()))
        dot = lambda part: lax.dot_general(*((part, m) if x_first else (m, part)), dims, preferred_element_type=F32)
        return dot(hi) + dot(mid) + dot(lo)

    if not differentiable:
        return product(x, m01)

    @jax.custom_vjp
    def exact(v):
        return product(v, m01)

    exact.defvjp(lambda v: (product(v, m01), None), lambda _, ct: (product(ct, m01_t),))
    return exact(x)


def _ssd_tile(xp, zp, bm, cm, hp, dt_c, dt_r, bias, bias_col, alog, alog_col, dsk, gnp, differentiable=False):
    row = lax.broadcasted_iota(jnp.int32, (CHUNK, CHUNK), 0)
    col = lax.broadcasted_iota(jnp.int32, (CHUNK, CHUNK), 1)
    causal = row >= col
    left = col < SSM_HEAD_DIM
    top = row < SSM_HEAD_DIM
    ones = jnp.ones((CHUNK, CHUNK), BF16)
    cb = _dot_nt(cm, bm)
    dtp = jax.nn.softplus(dt_c + bias)
    da_c = dtp * -jnp.exp(alog)
    da_r = jax.nn.softplus(dt_r + bias_col) * -jnp.exp(alog_col)
    lower = jnp.where(causal, 1.0, 0.0).astype(BF16)
    upper = jnp.where(row <= col, 1.0, 0.0).astype(BF16)
    cs = _dot_exact01(da_c, lower, upper, False, differentiable)
    cs_rows = _dot_exact01(da_r, upper, lower, True, differentiable)
    cs_last = jnp.sum(da_c, axis=0, keepdims=True)
    ecs, decay, ecl = jnp.exp(cs), jnp.exp(cs_last - cs), jnp.exp(cs_last)
    m = [cb * jnp.exp(jnp.where(causal, cs[:, r:r + 1] - cs_rows[r:r + 1, :], -1e30)) for r in range(SSM_HPG)]
    ygs, hn = [], []
    for p in range(SSM_PAIRS):
        a, b = 2 * p, 2 * p + 1
        pair = lambda v: jnp.where(left, v[:, a:a + 1], v[:, b:b + 1])
        xdt = xp[p] * pair(dtp)
        y = jnp.where(left, _dot(m[a], xdt), _dot(m[b], xdt))
        y = y + _dot_nt(cm, hp[p]) * pair(ecs)
        y = y + xp[p] * pair(dsk)
        states = _dot_tn(xdt * pair(decay), bm)
        hn.append(hp[p] * jnp.where(top, ecl[:, a:a + 1], ecl[:, b:b + 1]) + states)
        ygs.append(y * (zp[p] * jax.nn.sigmoid(zp[p])))
    ms = sum(_dot(t * t, ones) for t in ygs) * (1.0 / SSM_GROUP_W)
    rs = lax.rsqrt(ms + EPS)
    return [ygs[p] * rs * gnp[p] for p in range(SSM_PAIRS)], hn


def _ssd_in_specs(cidx):
    gw, n = SSM_GROUP_W, SSM_STATE
    bm_blk = D_INNER // n
    return [
        pl.BlockSpec((CHUNK, gw), lambda g, c: (cidx(c), g)),
        pl.BlockSpec((CHUNK, gw), lambda g, c: (cidx(c), g)),
        pl.BlockSpec((CHUNK, n), lambda g, c: (cidx(c), bm_blk + g)),
        pl.BlockSpec((CHUNK, n), lambda g, c: (cidx(c), bm_blk + SSM_GROUPS + g)),
        pl.BlockSpec((None, CHUNK, SSM_HPG), lambda g, c: (g, cidx(c), 0)),
        pl.BlockSpec((None, SSM_HPG, CHUNK), lambda g, c: (g, 0, cidx(c))),
        pl.BlockSpec((None, 3, SSM_HPG), lambda g, c: (g, 0, 0)),
        pl.BlockSpec((None, SSM_HPG, 2), lambda g, c: (g, 0, 0)),
        pl.BlockSpec((1, gw), lambda g, c: (0, g)),
    ]


def _ssd_args(x_ref, z_ref, bm_ref, cm_ref, hp, dtc_ref, dtr_ref, prow_ref, pcol_ref, gn_ref):
    npair, w = SSM_PAIRS, 2 * SSM_HEAD_DIM
    return (_split(x_ref, npair, w), _split(z_ref, npair, w), bm_ref[...], cm_ref[...], hp, dtc_ref[...], dtr_ref[...],
            prow_ref[0:1, :], pcol_ref[:, 0:1], prow_ref[1:2, :], pcol_ref[:, 1:2], prow_ref[2:3, :],
            _split(gn_ref, npair, w))


def _pair_rows(ref):
    w = 2 * SSM_HEAD_DIM
    return [ref[p * w:(p + 1) * w, :] for p in range(SSM_PAIRS)]


def _ssd_fwd(xbc, proj, dt_c, dt_r, par_row, par_col, gn, mixcat, *, name):
    w = 2 * SSM_HEAD_DIM

    def body(x_ref, z_ref, bm_ref, cm_ref, dtc_ref, dtr_ref, prow_ref, pcol_ref, gn_ref, cat_in,
             cat_ref, hprev_ref, h_scr):
        del cat_in

        @pl.when(pl.program_id(1) == 0)
        def _():
            h_scr[...] = jnp.zeros_like(h_scr)

        hprev_ref[...] = h_scr[...]
        yn, hn = _ssd_tile(*_ssd_args(x_ref, z_ref, bm_ref, cm_ref, _pair_rows(h_scr), dtc_ref, dtr_ref, prow_ref,
                                      pcol_ref, gn_ref))
        for p in range(SSM_PAIRS):
            cat_ref[:, p * w:(p + 1) * w] = yn[p].astype(cat_ref.dtype)
            h_scr[p * w:(p + 1) * w, :] = hn[p]

    return pl.pallas_call(
        body, grid=(SSM_GROUPS, N_CHUNKS), in_specs=[*_ssd_in_specs(lambda c: c), pl.BlockSpec(memory_space=pl.ANY)],
        out_specs=[pl.BlockSpec((CHUNK, SSM_GROUP_W), lambda g, c: (c, g)),
                   pl.BlockSpec((None, None, SSM_GROUP_W, SSM_STATE), lambda g, c: (c, g, 0, 0))],
        out_shape=[SDS(mixcat.shape, mixcat.dtype), SDS((N_CHUNKS, SSM_GROUPS, SSM_GROUP_W, SSM_STATE), F32)],
        scratch_shapes=[pltpu.VMEM((SSM_GROUP_W, SSM_STATE), F32)],
        input_output_aliases={9: 0}, compiler_params=_cparams(("parallel", "arbitrary")), name=name,
    )(xbc, proj, xbc, xbc, dt_c, dt_r, par_row, par_col, gn, mixcat)


def _ssd_bwd(xbc, proj, dt_c, dt_r, par_row, par_col, gn, hprev, dcat, dproj, *, name):
    nh, w, gw, n = SSM_HPG, 2 * SSM_HEAD_DIM, SSM_GROUP_W, SSM_STATE
    rev = lambda c: N_CHUNKS - 1 - c

    def body(x_ref, z_ref, bm_ref, cm_ref, dtc_ref, dtr_ref, prow_ref, pcol_ref, gn_ref, hprev_ref, dy_ref,
             dproj_in, dz_ref, dxs_ref, dbm_ref, dcm_ref, ddtc_ref, ddtr_ref, dprow_ref, dpcol_ref, dgn_ref, dh_scr):
        del dproj_in
        first = pl.program_id(1) == 0

        @pl.when(first)
        def _():
            dh_scr[...] = jnp.zeros_like(dh_scr)
            for ref in (dprow_ref, dpcol_ref, dgn_ref):
                ref[...] = jnp.zeros_like(ref)

        args = _ssd_args(x_ref, z_ref, bm_ref, cm_ref, _pair_rows(hprev_ref), dtc_ref, dtr_ref, prow_ref, pcol_ref,
                         gn_ref)
        _, vjp = jax.vjp(lambda *a: _ssd_tile(*a, differentiable=True), *args)
        dxs, dzs, dbm, dcm, dhs, ddtc, ddtr, dbias, dbias_col, dalog, dalog_col, ddsk, dgn = vjp(
            (_split(dy_ref, SSM_PAIRS, w), _pair_rows(dh_scr)))
        dbm_ref[...] = dbm
        dcm_ref[...] = dcm
        ddtc_ref[...] = ddtc
        ddtr_ref[...] = ddtr
        for q in range(SSM_PAIRS):
            dxs_ref[:, q * w:(q + 1) * w] = dxs[q]
            dz_ref[:, q * w:(q + 1) * w] = dzs[q].astype(dz_ref.dtype)
            dh_scr[q * w:(q + 1) * w, :] = dhs[q]
            dgn_ref[:, q * w:(q + 1) * w] += dgn[q]
        for i, d in enumerate((dbias, dalog, ddsk)):
            dprow_ref[i:i + 1, :] += d
        for i, d in enumerate((dbias_col, dalog_col)):
            dpcol_ref[:, i:i + 1] += d

    return pl.pallas_call(
        body, grid=(SSM_GROUPS, N_CHUNKS),
        in_specs=[*_ssd_in_specs(rev),
                  pl.BlockSpec((None, None, gw, n), lambda g, c: (rev(c), g, 0, 0)),
                  pl.BlockSpec((CHUNK, gw), lambda g, c: (rev(c), g)),
                  pl.BlockSpec(memory_space=pl.ANY)],
        out_specs=[pl.BlockSpec((CHUNK, gw), lambda g, c: (rev(c), g)),
                   pl.BlockSpec((CHUNK, gw), lambda g, c: (rev(c), g)),
                   pl.BlockSpec((CHUNK, n), lambda g, c: (rev(c), g)),
                   pl.BlockSpec((CHUNK, n), lambda g, c: (rev(c), g)),
                   pl.BlockSpec((None, CHUNK, nh), lambda g, c: (g, rev(c), 0)),
                   pl.BlockSpec((None, nh, CHUNK), lambda g, c: (g, 0, rev(c))),
                   pl.BlockSpec((None, 3, nh), lambda g, c: (g, 0, 0)),
                   pl.BlockSpec((None, nh, 2), lambda g, c: (g, 0, 0)),
                   pl.BlockSpec((1, gw), lambda g, c: (0, g))],
        out_shape=[SDS(dproj.shape, dproj.dtype), SDS((SEQ, D_INNER), F32), SDS((SEQ, SSM_GROUPS * n), F32),
                   SDS((SEQ, SSM_GROUPS * n), F32), SDS((SSM_GROUPS, SEQ, nh), F32), SDS((SSM_GROUPS, nh, SEQ), F32),
                   SDS((SSM_GROUPS, 3, nh), F32), SDS((SSM_GROUPS, nh, 2), F32), SDS((1, D_INNER), F32)],
        scratch_shapes=[pltpu.VMEM((gw, n), F32)],
        input_output_aliases={11: 0}, compiler_params=_cparams(("parallel", "arbitrary")), name=name,
    )(xbc, proj, xbc, xbc, dt_c, dt_r, par_row, par_col, gn, hprev, dcat, dproj)


def _sum_contributions(chip, parts, landed, *, name):
    _, r, c = parts.shape
    tr = _pick(r, (256, 384, 128))

    def body(chip_ref, own_ref, landed_ref, o_ref):
        del chip_ref
        acc = own_ref[...].astype(F32)
        for s in range(landed_ref.shape[0]):
            acc = acc + landed_ref[s].astype(F32)
        o_ref[...] = acc

    grid_spec = pltpu.PrefetchScalarGridSpec(
        num_scalar_prefetch=1, grid=(r // tr,),
        in_specs=[pl.BlockSpec((None, tr, c), lambda i, chip_ref: (chip_ref[0], i, 0)),
                  pl.BlockSpec((landed.shape[0], tr, c), lambda i, chip_ref: (0, i, 0))],
        out_specs=pl.BlockSpec((tr, c), lambda i, chip_ref: (i, 0)))
    return pl.pallas_call(body, grid_spec=grid_spec, out_shape=SDS((r, c), F32),
                          compiler_params=_cparams(("parallel",)), name=name)(chip, parts, landed)


def _adamw(w, g, m, v, *, name):
    layers, r, c = w.shape
    if r <= 256 or r % 128 == 0:
        tr = min(r, 256)
        steps, spec = r // tr, pl.BlockSpec((None, tr, c), lambda l, i: (l, i, 0))
    else:
        tc = _pick(c, (256, 128))
        steps, spec = c // tc, pl.BlockSpec((None, r, tc), lambda l, i: (l, 0, i))

    def body(w_ref, g_ref, m_ref, v_ref, d_ref, mo_ref, vo_ref):
        g = g_ref[...]
        m_new = ADAM_B1 * m_ref[...] + (1.0 - ADAM_B1) * g
        v_new = ADAM_B2 * v_ref[...] + (1.0 - ADAM_B2) * (g * g)
        m_hat = m_new / (1.0 - ADAM_B1 ** ADAM_STEP)
        v_hat = v_new / (1.0 - ADAM_B2 ** ADAM_STEP)
        d_ref[...] = -ADAM_LR * (m_hat / (jnp.sqrt(v_hat) + ADAM_EPS) + ADAM_WD * w_ref[...])
        mo_ref[...] = m_new
        vo_ref[...] = v_new

    return pl.pallas_call(body, grid=(layers, steps), in_specs=[spec] * 4, out_specs=[spec] * 3,
                          out_shape=[SDS(w.shape, F32)] * 3, compiler_params=_cparams(("parallel", "parallel")),
                          name=name)(w, g, m, v)


ANY = pl.BlockSpec(memory_space=pl.ANY)


def _place():
    x, y, c = lax.axis_index("x"), lax.axis_index("y"), lax.axis_index("c")
    chips = [(1 - x, y), (x, 1 - y), (1 - x, 1 - y)]
    return x, y, c, chips


def _remote(src, dst, send_sem, recv_sem, to):
    return pltpu.make_async_remote_copy(src_ref=src, dst_ref=dst, send_sem=send_sem, recv_sem=recv_sem,
                                        device_id=to, device_id_type=MESH)


STREAM_ROWS = 256


def _stream_rows(i):
    return pl.ds(pl.multiple_of(i * STREAM_ROWS, STREAM_ROWS), STREAM_ROWS)


def _channel_scratch(width, dtype, rows=STREAM_ROWS):
    buf = (2, rows, width)
    return [pltpu.VMEM(buf, dtype), pltpu.VMEM(buf, dtype), *([pltpu.SemaphoreType.DMA((2,))] * 5),
            pltpu.SemaphoreType.REGULAR((2,))]


CHANNEL_REFS = 8


def _copy_blocks(srcs, dsts, ch):
    sbuf, _, ld, _, _, st, _, _ = ch
    n = len(srcs)
    load = lambda i: pltpu.make_async_copy(srcs[i], sbuf.at[i % 2], ld.at[i % 2])
    store = lambda i: pltpu.make_async_copy(sbuf.at[i % 2], dsts[i], st.at[i % 2])
    load(0).start()
    for i in range(n):
        if i + 1 < n:
            if i >= 1:
                store(i - 1).wait()
            load(i + 1).start()
        load(i).wait()
        store(i).start()
    for i in range(max(0, n - 2), n):
        store(i).wait()


def _exchange_block_streams(streams, sibling):
    plans = []
    for srcs, dsts, keeps, (sbuf, rbuf, ld, snd, rcv, st, kp, credit) in streams:
        n = len(srcs)

        def load(i, srcs=srcs, sbuf=sbuf, ld=ld):
            return pltpu.make_async_copy(srcs[i], sbuf.at[i % 2], ld.at[i % 2])

        def push(i, sbuf=sbuf, rbuf=rbuf, snd=snd, rcv=rcv):
            return _remote(sbuf.at[i % 2], rbuf.at[i % 2], snd.at[i % 2], rcv.at[i % 2], sibling)

        def store(i, rbuf=rbuf, dsts=dsts, st=st):
            return pltpu.make_async_copy(rbuf.at[i % 2], dsts[i], st.at[i % 2])

        def save(i, sbuf=sbuf, keeps=keeps, kp=kp):
            return pltpu.make_async_copy(sbuf.at[i % 2], keeps[i], kp.at[i % 2])

        def free_slot(i, n=n, store=store, credit=credit):
            if 1 <= i < n:
                store(i - 1).wait()
                if i + 1 < n:
                    pl.semaphore_signal(credit.at[(i + 1) % 2], 1, device_id=sibling, device_id_type=MESH)

        def send(i, n=n, load=load, push=push, save=save, keeps=keeps, credit=credit):
            if i < n:
                load(i).wait()
                pl.semaphore_wait(credit.at[i % 2], 1)
                push(i).start()
                if keeps[i] is not None:
                    save(i).start()

        def receive(i, n=n, load=load, push=push, store=store, save=save, keeps=keeps):
            if i < n:
                push(i).wait_recv()
                store(i).start()
                push(i).wait_send()
                if keeps[i] is not None:
                    save(i).wait()
                if i + 2 < n:
                    load(i + 2).start()

        for i in range(min(2, n)):
            pl.semaphore_signal(credit.at[i], 1, device_id=sibling, device_id_type=MESH)
            load(i).start()
        plans.append((n, free_slot, send, receive, store))
    for _, _, send, _, _ in plans:
        send(0)
    for i in range(max(p[0] for p in plans)):
        for _, free_slot, _, _, _ in plans:
            free_slot(i)
        for _, _, send, _, _ in plans:
            send(i + 1)
        for _, _, _, receive, _ in plans:
            receive(i)
    for n, _, _, _, store in plans:
        store(n - 1).wait()


def _all_gather_shards(shards, small, *, name):
    n = len(shards)

    def body(*refs):
        ins, outs = refs[:n + 1], refs[n + 1:2 * n + 2]
        scr = refs[2 * n + 2:]
        chans = [scr[CHANNEL_REFS * t:CHANNEL_REFS * (t + 1)] for t in range(n)]
        send_sems, recv_sems, small_sems = scr[CHANNEL_REFS * n:]
        x, y, c, _ = _place()
        me = 2 * x + y
        sibling = (x, y, 1 - c)
        near = (lax.rem(x + 1 - c, 2), lax.rem(y + c, 2))
        far = (lax.rem(x + c, 2), lax.rem(y + 1 - c, 2))
        k_near, k_far, k_diag = 2 * near[0] + near[1], 2 * far[0] + far[1], 3 - me
        targets = ((*near, c), (*far, c), (*far, c))
        arrives = (k_near, k_far, k_diag)
        streams_in = (k_far, k_near, k_diag)

        def ici(t, j, src, blk):
            return _remote(src, outs[t].at[blk, c], send_sems.at[3 * t + j], recv_sems.at[3 * t + j], targets[j])

        first = [ici(t, j, ins[t].at[c], me) for t in range(n + 1) for j in range(2)]
        for cp in first:
            cp.start()
        small_local = pltpu.make_async_copy(ins[n], outs[n].at[me], small_sems.at[6])
        small_local.start()
        for t in range(n):
            _copy_blocks([ins[t].at[h] for h in range(2)], [outs[t].at[me, h] for h in range(2)], chans[t])
        passed = []
        for j in range(3):
            for t in range(n + 1):
                landed = outs[t].at[arrives[j], c]
                ici(t, j, landed, arrives[j]).wait_recv()
                if j == 0:
                    fwd = ici(t, 2, landed, k_near)
                    fwd.start()
                    passed.append(fwd)
                if t < n:
                    _exchange_block_streams([([landed], [outs[t].at[streams_in[j], 1 - c]], [None], chans[t])], sibling)
                else:
                    fwd = _remote(landed, landed, small_sems.at[j], small_sems.at[3 + j], sibling)
                    fwd.start()
                    passed.append(fwd)
        for j in range(3):
            got = outs[n].at[streams_in[j], 1 - c]
            _remote(got, got, small_sems.at[j], small_sems.at[3 + j], sibling).wait_recv()
        for cp in first + passed:
            cp.wait_send()
        small_local.wait()

    scratch = []
    for s in shards:
        scratch += _channel_scratch(s.shape[2], s.dtype, rows=s.shape[1])
    return pl.pallas_call(
        body, in_specs=[ANY] * (n + 1), out_specs=[ANY] * (n + 1),
        out_shape=[SDS((N_CHIPS, *s.shape), s.dtype) for s in (*shards, small)],
        scratch_shapes=[*scratch, pltpu.SemaphoreType.DMA((3 * n + 3,)), pltpu.SemaphoreType.DMA((3 * n + 3,)),
                        pltpu.SemaphoreType.DMA((7,))],
        compiler_params=pltpu.CompilerParams(vmem_limit_bytes=VMEM_LIMIT), name=name)(*shards, small)


def _pair_reduce(stacks, *, name):
    n = len(stacks)
    per = 11

    def body(*refs):
        ins, outs, scr = refs[:n], refs[n:2 * n], refs[2 * n:]
        x, y, c, _ = _place()
        sibling = (x, y, 1 - c)
        streams = []
        for t in range(n):
            sraw, sbuf, rbuf, obuf, pbuf, ld_s, ld_o, snd, rcv, st, credit = scr[per * t:per * (t + 1)]
            steps = ins[t].shape[1] // STREAM_ROWS
            src, own, out = ins[t].at[1 - c], ins[t].at[c], outs[t]
            assert steps >= 2

            def load_s(i, slot, src=src, sraw=sraw, ld_s=ld_s):
                return pltpu.make_async_copy(src.at[_stream_rows(i)], sraw.at[slot], ld_s.at[slot])

            def load_o(i, slot, own=own, obuf=obuf, ld_o=ld_o):
                return pltpu.make_async_copy(own.at[_stream_rows(i)], obuf.at[slot], ld_o.at[slot])

            def push(slot, sbuf=sbuf, rbuf=rbuf, snd=snd, rcv=rcv):
                return _remote(sbuf.at[slot], rbuf.at[slot], snd.at[slot], rcv.at[slot], sibling)

            def store(i, slot, pbuf=pbuf, out=out, st=st):
                return pltpu.make_async_copy(pbuf.at[slot], out.at[_stream_rows(i)], st.at[slot])

            def send(i, slot, load_s=load_s, push=push, sraw=sraw, sbuf=sbuf, credit=credit):
                load_s(i, slot).wait()
                sbuf[slot] = sraw[slot].astype(sbuf.dtype)
                pl.semaphore_wait(credit.at[slot], 1)
                push(slot).start()

            def combine(i, slot, load_s=load_s, load_o=load_o, push=push, store=store, rbuf=rbuf, obuf=obuf, pbuf=pbuf,
                        credit=credit, steps=steps):
                load_o(i, slot).wait()
                push(slot).wait_recv()

                @pl.when(i >= 2)
                def _():
                    store(i, slot).wait()

                pbuf[slot] = (obuf[slot] + rbuf[slot].astype(F32)).astype(pbuf.dtype)
                store(i, slot).start()
                push(slot).wait_send()

                @pl.when(i + 2 < steps)
                def _():
                    load_s(i + 2, slot).start()
                    load_o(i + 2, slot).start()
                    pl.semaphore_signal(credit.at[slot], 1, device_id=sibling, device_id_type=MESH)

            for slot in range(2):
                pl.semaphore_signal(credit.at[slot], 1, device_id=sibling, device_id_type=MESH)
                load_s(slot, slot).start()
                load_o(slot, slot).start()
            streams.append((steps, send, combine, store))
        for _, send, _, _ in streams:
            send(0, 0)

        def step(i, carry):
            slot = lax.rem(i, 2)
            for steps, send, _, _ in streams:
                @pl.when(i + 1 < steps)
                def _(send=send):
                    send(i + 1, 1 - slot)
            for steps, _, combine, _ in streams:
                @pl.when(i < steps)
                def _(combine=combine):
                    combine(i, slot)
            return carry

        lax.fori_loop(0, max(s[0] for s in streams), step, 0)
        for _, _, _, store in streams:
            for slot in range(2):
                store(0, slot).wait()

    scratch = []
    for s in stacks:
        buf = (2, STREAM_ROWS, s.shape[2])
        scratch += [pltpu.VMEM(buf, F32), pltpu.VMEM(buf, BF16), pltpu.VMEM(buf, BF16), pltpu.VMEM(buf, F32),
                    pltpu.VMEM(buf, BF16), *([pltpu.SemaphoreType.DMA((2,))] * 5), pltpu.SemaphoreType.REGULAR((2,))]
    return pl.pallas_call(
        body, in_specs=[ANY] * n, out_specs=[ANY] * n, out_shape=[SDS(s.shape[1:], BF16) for s in stacks],
        scratch_shapes=scratch, compiler_params=pltpu.CompilerParams(vmem_limit_bytes=VMEM_LIMIT), name=name)(*stacks)


HBM_SPEC = pl.BlockSpec(memory_space=pltpu.HBM)
SEM_SPEC = pl.BlockSpec(memory_space=pltpu.SEMAPHORE)
SIDE_EFFECT = pltpu.SideEffectType.DATAFLOW_SIDE_EFFECTING


def _scatter_copies(ins, lands, send_sems, recv_sems):
    _, _, c, chips = _place()
    return [_remote(ins[t].at[2 * cx + cy], lands[t].at[j], send_sems.at[3 * t + j], recv_sems.at[3 * t + j],
                    (cx, cy, c)) for t in range(len(ins)) for j, (cx, cy) in enumerate(chips)]


def _chip_scatter_start(parts, *, name):
    n = len(parts)

    def body(*refs):
        ins, lands = refs[:n], refs[n:2 * n]
        send_sems, recv_sems, token = refs[2 * n], refs[2 * n + 1], refs[-1]
        for cp in _scatter_copies(ins, lands, send_sems, recv_sems):
            cp.start()
        token[...] = jnp.zeros_like(token)

    hbm = lambda a: pltpu.with_memory_space_constraint(a, pltpu.HBM)
    lands = [hbm(lax.empty((3, *p.shape[1:]), p.dtype)) for p in parts]
    thru = [pltpu.HBM(a.shape, a.dtype) for a in (*parts, *lands)]
    outs = pl.pallas_call(
        body, name=name,
        out_shape=(pltpu.SemaphoreType.DMA((3 * n,)), pltpu.SemaphoreType.DMA((3 * n,)), *thru, SDS((8, 128), F32)),
        in_specs=[HBM_SPEC] * (2 * n),
        out_specs=(SEM_SPEC, SEM_SPEC, *([HBM_SPEC] * (2 * n)), pl.BlockSpec(memory_space=pltpu.VMEM)),
        input_output_aliases={i: 2 + i for i in range(2 * n)},
        compiler_params=pltpu.CompilerParams(has_side_effects=SIDE_EFFECT),
    )(*[hbm(p) for p in parts], *lands)
    return outs[0], outs[1], outs[2:2 + n], outs[2 + n:2 + 2 * n], outs[-1]


def _chip_scatter_wait(send_sems, recv_sems, parts, lands, after, *, name):
    n = len(parts)

    def body(*refs):
        ins, lands_in = refs[:n], refs[n:2 * n]
        for cp in _scatter_copies(ins, lands_in, refs[2 * n], refs[2 * n + 1]):
            cp.wait_send()
            cp.wait_recv()

    outs = pl.pallas_call(
        body, name=name, out_shape=[pltpu.HBM(a.shape, a.dtype) for a in (*parts, *lands)],
        in_specs=[*([HBM_SPEC] * (2 * n)), SEM_SPEC, SEM_SPEC, *([ANY] * len(after))],
        out_specs=[HBM_SPEC] * (2 * n), input_output_aliases={i: i for i in range(2 * n)},
        compiler_params=pltpu.CompilerParams(has_side_effects=SIDE_EFFECT),
    )(*parts, *lands, send_sems, recv_sems, *after)
    return outs[:n], outs[n:]


def _gather_copies(shards, zones, send_sems, recv_sems):
    x, y, c, chips = _place()
    return [_remote(shards[t].at[c], zones[t].at[2 * x + y, c], send_sems.at[3 * t + j], recv_sems.at[3 * t + j],
                    (cx, cy, c)) for t in range(len(shards)) for j, (cx, cy) in enumerate(chips)]


def _gather_start(shards, after, *, name):
    n = len(shards)

    def body(*refs):
        ins, zones = refs[:n], refs[n:2 * n]
        send_sems, recv_sems, token = refs[2 * n + len(after)], refs[2 * n + len(after) + 1], refs[-1]
        for cp in _gather_copies(ins, zones, send_sems, recv_sems):
            cp.start()
        token[...] = jnp.zeros_like(token)

    hbm = lambda a: pltpu.with_memory_space_constraint(a, pltpu.HBM)
    zones = [hbm(lax.empty((N_CHIPS, *s.shape), s.dtype)) for s in shards]
    thru = [pltpu.HBM(a.shape, a.dtype) for a in (*shards, *zones)]
    outs = pl.pallas_call(
        body, name=name,
        out_shape=(pltpu.SemaphoreType.DMA((3 * n,)), pltpu.SemaphoreType.DMA((3 * n,)), *thru, SDS((8, 128), F32)),
        in_specs=[*([HBM_SPEC] * (2 * n)), *([ANY] * len(after))],
        out_specs=(SEM_SPEC, SEM_SPEC, *([HBM_SPEC] * (2 * n)), pl.BlockSpec(memory_space=pltpu.VMEM)),
        input_output_aliases={i: 2 + i for i in range(2 * n)},
        compiler_params=pltpu.CompilerParams(has_side_effects=SIDE_EFFECT),
    )(*[hbm(s) for s in shards], *zones, *after)
    return outs[0], outs[1], outs[2:2 + n], outs[2 + n:2 + 2 * n], outs[-1]


def _gather_wait(send_sems, recv_sems, shards, zones, after, *, name):
    n = len(shards)

    def body(*refs):
        for cp in _gather_copies(refs[:n], refs[n:2 * n], refs[2 * n], refs[2 * n + 1]):
            cp.wait_send()
            cp.wait_recv()

    outs = pl.pallas_call(
        body, name=name, out_shape=[pltpu.HBM(a.shape, a.dtype) for a in (*shards, *zones)],
        in_specs=[*([HBM_SPEC] * (2 * n)), SEM_SPEC, SEM_SPEC, *([ANY] * len(after))],
        out_specs=[HBM_SPEC] * (2 * n), input_output_aliases={i: i for i in range(2 * n)},
        compiler_params=pltpu.CompilerParams(has_side_effects=SIDE_EFFECT),
    )(*shards, *zones, send_sems, recv_sems, *after)
    return outs[:n], outs[n:]


def _gather_finish(shards, zones, *, name):
    n = len(shards)

    def body(*refs):
        ins, zones_in, outs, scr = refs[:n], refs[n:2 * n], refs[2 * n:3 * n], refs[3 * n:]
        x, y, c, chips = _place()
        me = 2 * x + y
        sibling = (x, y, 1 - c)
        others = [2 * cx + cy for cx, cy in chips]
        chans = [scr[CHANNEL_REFS * t:CHANNEL_REFS * (t + 1)] for t in range(n)]
        for t in range(n):
            _copy_blocks([ins[t].at[h] for h in range(2)], [outs[t].at[me, h] for h in range(2)], chans[t])
        _exchange_block_streams([([zones_in[t].at[k, c] for k in others], [outs[t].at[k, 1 - c] for k in others],
                                  [None] * len(others), chans[t]) for t in range(n)], sibling)

    scratch = []
    for s in shards:
        scratch += _channel_scratch(s.shape[2], s.dtype, rows=s.shape[1])
    return pl.pallas_call(
        body, in_specs=[ANY] * (2 * n), out_specs=[ANY] * n, out_shape=[SDS(z.shape, z.dtype) for z in zones],
        input_output_aliases={n + t: t for t in range(n)}, scratch_shapes=scratch,
        compiler_params=pltpu.CompilerParams(vmem_limit_bytes=VMEM_LIMIT), name=name)(*shards, *zones)


def _pair_share(groups, *, name):
    finals = [f for grp in groups for f in grp]
    n, n_out = len(finals), len(groups)

    def body(*refs):
        ins, outs, scr = refs[:n], refs[n:n + n_out], refs[n + n_out:]
        x, y, c, _ = _place()
        sibling = (x, y, 1 - c)
        t, streams = 0, []
        for o, grp in enumerate(groups):
            rows = grp[0].shape[0] // 2
            blocks = [(layer, pl.ds(b * rows, rows)) for layer in range(len(grp)) for b in range(2)]
            streams.append(([ins[t + layer].at[rs] for layer, rs in blocks],
                            [outs[o].at[layer, 1 - c, rs] for layer, rs in blocks],
                            [outs[o].at[layer, c, rs] for layer, rs in blocks],
                            scr[CHANNEL_REFS * o:CHANNEL_REFS * (o + 1)]))
            t += len(grp)
        _exchange_block_streams(streams, sibling)

    scratch = []
    for grp in groups:
        scratch += _channel_scratch(grp[0].shape[1], grp[0].dtype, rows=grp[0].shape[0] // 2)
    return pl.pallas_call(
        body, in_specs=[ANY] * n, out_specs=[ANY] * n_out,
        out_shape=[SDS((len(grp), 2, *grp[0].shape), grp[0].dtype) for grp in groups],
        scratch_shapes=scratch, compiler_params=pltpu.CompilerParams(vmem_limit_bytes=VMEM_LIMIT), name=name)(*finals)


def _all_reduce_small(v, *, name):
    rows, lanes = v.shape
    n_dev = 8

    def body(v_ref, o_ref, all_ref, send_sems, recv_sems, local_sem):
        x, y, c, chips = _place()
        me, sibling = (x, y, c), (x, y, 1 - c)

        def block(px, py, pc):
            return all_ref.at[4 * px + 2 * py + pc]

        def copy(k, blk, to, src=None):
            return _remote(block(*blk) if src is None else src, block(*blk), send_sems.at[k], recv_sems.at[k], to)

        mine = pltpu.make_async_copy(v_ref, block(*me), local_sem)
        mine.start()
        first = [copy(0, me, sibling, src=v_ref)]
        first += [copy(1 + j, me, (*chip, c), src=v_ref) for j, chip in enumerate(chips)]
        for cp in first:
            cp.start()
        passed = [copy(4 + j, (*chip, c), sibling) for j, chip in enumerate(chips)]
        for j, chip in enumerate(chips):
            copy(1 + j, (*chip, c), me).wait_recv()
            passed[j].start()
        copy(0, sibling, me).wait_recv()
        for j, chip in enumerate(chips):
            copy(4 + j, (*chip, 1 - c), me).wait_recv()
        for cp in first + passed:
            cp.wait_send()
        mine.wait()
        acc = all_ref[0]
        for k in range(1, n_dev):
            acc = acc + all_ref[k]
        o_ref[...] = acc

    vmem = pl.BlockSpec(memory_space=pltpu.VMEM)
    return pl.pallas_call(
        body, in_specs=[vmem], out_specs=vmem, out_shape=SDS((rows, lanes), F32),
        scratch_shapes=[pltpu.VMEM((n_dev, rows, lanes), F32), pltpu.SemaphoreType.DMA((7,)),
                        pltpu.SemaphoreType.DMA((7,)), pltpu.SemaphoreType.DMA],
        compiler_params=pltpu.CompilerParams(vmem_limit_bytes=VMEM_LIMIT), name=name)(v)


def _relu2_epilogue(acc):
    return acc, jnp.square(jnp.maximum(acc, 0.0))


def _res_epilogue(acc, res):
    return (acc + res,)


def _drelu2_epilogue(acc, pre):
    return (acc * (2.0 * jnp.maximum(pre.astype(F32), 0.0)),)


def _ffn_fwd(h, g, w1, w2, tag):
    f = _rms_fwd(h, g, name=f"ffn_norm_{tag}")
    pre, act = _mm_nn(f, w1, name=f"ffn1_{tag}", epilogue=_relu2_epilogue, n_out_dtypes=(BF16, BF16))
    h_out = _mm_nn(act, w2, name=f"ffn2_{tag}", extras=(h,), epilogue=_res_epilogue)
    return h_out, (f, pre, act)


def _ffn_bwd(dh, h, g, w1, w2, saved, layer, after=()):
    f, pre, act = saved
    dpre = _mm_nt(dh, w2, name=f"ffn2_dx_{layer}", out_dtype=BF16, extras=(pre,), epilogue=_drelu2_epilogue,
                  after=after)
    dw2 = _mm_tn_stacked(act, dh, name=f"ffn2_dw_{layer}", col_slots=False)
    df = _mm_nt(dpre, w1, name=f"ffn1_dx_{layer}")
    dw1 = _mm_tn_stacked(f, dpre, name=f"ffn1_dw_{layer}", col_slots=True)
    dh, dg = _rms_bwd(h, g, df, dh, name=f"ffn_norm_bwd_{layer}")
    return dh, dg, dw1, dw2


def _kv_fwd(mem, g, w_kv, tag):
    m = _rms_fwd(mem, g, name=f"mem_norm_{tag}")
    return m, _mm_nn(m, w_kv, name=f"kv_{tag}")


def _kv_bwd(mem, g, w_kv, m, dk, dv, layer):
    dkv = jnp.concatenate([dk, dv], axis=1)
    dw = _mm_tn_stacked(m, dkv, name=f"kv_dw_{layer}", col_slots=True)
    dm = _mm_nt(dkv, w_kv, name=f"kv_dx_{layer}")
    _, dg = _rms_bwd(mem, g, dm, dm, name=f"mem_norm_bwd_{layer}")
    return dw, dg


def _local_step(x, mem, target, p, after_layer1=None, after_ffn0=None, after_mixer0=None):
    row = lambda v: v.reshape(1, -1)
    g = {}

    h0 = x
    a0 = _rms_fwd(h0, row(p["norm_mix"][0]), name="mix_norm_0")
    proj_a = _mm_nn(a0, p["a_in"], name="a_in", after=p.get("after_start", ()))
    m0, kv0 = _kv_fwd(mem, row(p["mem_norm"][0]), p["w_kv"][0], "0")
    cat0 = _attn_fwd(proj_a, 2 * D_INNER, kv0, name="attn_0")
    bs_col = p["a_bs"].reshape(A_GROUPS, CHUNK, 1)
    cat0 = _gate_fwd(proj_a, p["a_ln_g"], p["a_ln_b"], p["a_ws"], bs_col, cat0, name="gate")
    h1 = _mm_nn(cat0, p["w_out"][0], name="out_0", extras=(h0,), epilogue=_res_epilogue)
    w_ffn1_0, w_ffn2_0 = p["layer0_ffn"](h1) if "layer0_ffn" in p else (p["w_ffn1"][0], p["w_ffn2"][0])
    h2, ffn0 = _ffn_fwd(h1, row(p["norm_ffn"][0]), w_ffn1_0, w_ffn2_0, "0")

    w_kv1, b_in = p["layer1_mixer"](h2) if "layer1_mixer" in p else (p["w_kv"][1], p["b_in"])
    a1 = _rms_fwd(h2, row(p["norm_mix"][1]), name="mix_norm_1")
    proj_b = _mm_nn(a1, b_in, name="b_in")
    m1, kv1 = _kv_fwd(mem, row(p["mem_norm"][1]), w_kv1, "1")
    cat1 = _attn_fwd(proj_b, B_Q_OFF, kv1, name="attn_1")
    xbc = _conv_fwd(proj_b, p["b_conv_w"], p["b_conv_b"], name="conv")
    dt_raw = proj_b[:, B_DT_OFF:B_DT_OFF + SSM_HEADS].reshape(SEQ, SSM_GROUPS, SSM_HPG)
    dt_c = jnp.transpose(dt_raw, (1, 0, 2))
    dt_r = jnp.transpose(dt_raw, (1, 2, 0))
    per_head = lambda v: v.reshape(SSM_GROUPS, 1, SSM_HPG)
    par_row = jnp.concatenate([per_head(p["b_dt_bias"]), per_head(p["b_a_log"]), per_head(p["b_d"])], axis=1)
    ssd_par = (par_row, jnp.transpose(par_row[:, :2], (0, 2, 1)), p["b_gnorm"])
    cat1, hprev = _ssd_fwd(xbc, proj_b, dt_c, dt_r, *ssd_par, cat1, name="ssd")
    if "layer1_rest" in p:
        w_out1, w_ffn1_1, w_ffn2_1 = p["layer1_rest"](cat1)
    else:
        w_out1, w_ffn1_1, w_ffn2_1 = p["w_out"][1], p["w_ffn1"][1], p["w_ffn2"][1]
    h3 = _mm_nn(cat1, w_out1, name="out_1", extras=(h2,), epilogue=_res_epilogue)
    h4, ffn1 = _ffn_fwd(h3, row(p["norm_ffn"][1]), w_ffn1_1, w_ffn2_1, "1")

    loss, dh, g["final_norm"] = _loss_head(h4, row(p["final_norm"]), target, name="loss_head")

    dh, dnf1, dw1_1, dw2_1 = _ffn_bwd(dh, h3, row(p["norm_ffn"][1]), w_ffn1_1, w_ffn2_1, ffn1, 1)
    dcat1 = _mm_nt(dh, w_out1, name="out_dx_1")
    dwo_1 = _mm_tn_stacked(cat1, dh, name="out_dw_1", col_slots=False)
    dproj_b, dk1, dv1 = _attn_bwd(proj_b, B_Q_OFF, kv1, dcat1, B_IN_PAD, B_Q_OFF, name="attn_bwd_1")
    dproj_b, dxs, dbm, dcm, ddt_c, ddt_r, dpar_row, dpar_col, g["b_gnorm"] = _ssd_bwd(
        xbc, proj_b, dt_c, dt_r, *ssd_par, hprev, dcat1, dproj_b, name="ssd_bwd")
    dpar = dpar_row.at[:, :2].add(jnp.transpose(dpar_col, (0, 2, 1)))
    g["b_dt_bias"], g["b_a_log"], g["b_d"] = dpar[:, 0], dpar[:, 1], dpar[:, 2]
    dproj_b, g["b_conv_w"], g["b_conv_b"] = _conv_bwd(proj_b, p["b_conv_w"], p["b_conv_b"], dxs, dbm, dcm, dproj_b,
                                                      name="conv_bwd")
    ddt = jnp.transpose(ddt_c, (1, 0, 2)) + jnp.transpose(ddt_r, (2, 0, 1))
    ddt = jnp.pad(ddt.reshape(SEQ, SSM_HEADS), ((0, 0), (0, B_IN_PAD - B_DT_OFF - SSM_HEADS))).astype(BF16)
    dproj_b = lax.dynamic_update_slice(dproj_b, ddt, (0, B_DT_OFF))
    dwkv_1, dmn1 = _kv_bwd(mem, row(p["mem_norm"][1]), w_kv1, m1, dk1, dv1, 1)
    dwb = _b_in_grad_slots(_mm_tn(a1, dproj_b, name="b_in_dw"))
    da1 = _mm_nt(dproj_b, b_in, name="b_in_dx")
    dh, dnm1 = _rms_bwd(h2, row(p["norm_mix"][1]), da1, dh, name="mix_norm_bwd_1")
    layer1 = dict(w_kv=dwkv_1, w_out=dwo_1, w_ffn1=dw1_1, w_ffn2=dw2_1, b_in=dwb)
    token = () if after_layer1 is None else (after_layer1(layer1),)

    dh, dnf0, dw1_0, dw2_0 = _ffn_bwd(dh, h1, row(p["norm_ffn"][0]), w_ffn1_0, w_ffn2_0, ffn0, 0,
                                      after=token)
    ffn0_grads = dict(w_ffn1=dw1_0, w_ffn2=dw2_0)
    token = () if after_ffn0 is None else (after_ffn0(ffn0_grads),)
    dcat0 = _mm_nt(dh, p["w_out"][0], name="out_dx_0", after=token)
    dwo_0 = _mm_tn_stacked(cat0, dh, name="out_dw_0", col_slots=False)
    dproj_a, dk0, dv0 = _attn_bwd(proj_a, 2 * D_INNER, kv0, dcat0, A_IN, 2 * D_INNER, name="attn_bwd_0")
    dproj_a, g["a_ln_g"], g["a_ln_b"], g["a_ws"], dbs_col = _gate_bwd(
        proj_a, p["a_ln_g"], p["a_ln_b"], p["a_ws"], bs_col, dcat0, dproj_a, name="gate_bwd")
    g["a_bs"] = dbs_col.reshape(A_GROUPS, CHUNK)
    dwkv_0, dmn0 = _kv_bwd(mem, row(p["mem_norm"][0]), p["w_kv"][0], m0, dk0, dv0, 0)
    dwa = _mm_tn_stacked(a0, dproj_a, name="a_in_dw", col_slots=True)
    mixer0_grads = dict(w_kv=dwkv_0, w_out=dwo_0, a_in=dwa)
    token = () if after_mixer0 is None else (after_mixer0(mixer0_grads),)
    da0 = _mm_nt(dproj_a, p["a_in"], name="a_in_dx", after=token)
    dx, dnm0 = _rms_bwd(h0, row(p["norm_mix"][0]), da0, dh, name="mix_norm_bwd_0")

    g["norm_mix"] = jnp.concatenate([dnm0, dnm1], axis=0)
    g["norm_ffn"] = jnp.concatenate([dnf0, dnf1], axis=0)
    g["mem_norm"] = jnp.concatenate([dmn0, dmn1], axis=0)
    layer0 = dict(w_kv=dwkv_0, w_out=dwo_0, w_ffn1=dw1_0, w_ffn2=dw2_0, a_in=dwa)
    return loss, dx, g, layer0, layer1


def _b_in_full(gathered):
    n = B_IN // N_CHIPS
    dt0 = D_INNER + CONV_DIM - (N_CHIPS - 1) * n
    last = gathered[N_CHIPS - 1]
    return jnp.concatenate([*[gathered[k] for k in range(N_CHIPS - 1)], last[:, :dt0], last[:, dt0 + SSM_HEADS:],
                            last[:, dt0:dt0 + SSM_HEADS], jnp.zeros((D_MODEL, B_IN_PAD - B_IN), last.dtype)], axis=1)


def _b_in_grad_slots(d):
    n = B_IN // N_CHIPS
    dt0 = D_INNER + CONV_DIM
    last = jnp.concatenate([d[:, (N_CHIPS - 1) * n:dt0], d[:, B_DT_OFF:B_DT_OFF + SSM_HEADS], d[:, dt0:B_DT_OFF]], axis=1)
    slots = [*[d[:, k * n:(k + 1) * n] for k in range(N_CHIPS - 1)], last]
    half = D_MODEL // 2
    return jnp.stack([jnp.stack([s[h * half:(h + 1) * half] for s in slots]) for h in range(2)])


SMALL_REPL = ("norm_mix", "norm_ffn", "mem_norm", "a_ln_g", "a_ln_b", "a_ws", "a_bs", "b_dt_bias", "b_a_log", "b_d",
              "final_norm")
SMALL_SHARD = ("b_conv_w", "b_conv_b", "b_gnorm")
WEIGHTS = ("norm_mix", "norm_ffn", "mem_norm", "w_kv", "w_out", "w_ffn1", "w_ffn2", "a_in", "a_ln_g", "a_ln_b", "a_ws",
           "a_bs", "b_in", "b_conv_w", "b_conv_b", "b_dt_bias", "b_a_log", "b_d", "b_gnorm", "final_norm")
CONV_SHARD = CONV_DIM // N_CHIPS
GN_SHARD = D_INNER // N_CHIPS


LAYERED = ("w_kv", "w_out", "w_ffn1", "w_ffn2")


def _gather_weights(w):
    halves = lambda k, layer: (w[k][layer] if k in LAYERED else w[k][0]).reshape(2, -1, w[k].shape[-1]).astype(BF16)
    small = jnp.zeros((2, CONV_K, CONV_SHARD), F32)
    small = small.at[0].set(w["b_conv_w"][0])
    small = small.at[1, 0].set(w["b_conv_b"][0])
    small = small.at[1, 1, :GN_SHARD].set(w["b_gnorm"][0])
    first_names = ("w_kv", "w_out", "a_in")
    gathered = _all_gather_shards([halves(k, 0) for k in first_names], small, name="gather_weights_0")
    got = dict(zip(first_names, gathered))
    slots = lambda a: a.reshape(N_CHIPS, -1, a.shape[-1])
    rows = lambda a: a.reshape(-1, a.shape[-1])
    p = dict(w_kv=[slots(got["w_kv"])], w_out=[rows(got["w_out"])], a_in=slots(got["a_in"]))
    sm = gathered[-1]
    p["b_conv_w"] = jnp.transpose(sm[:, 0], (1, 0, 2)).reshape(CONV_K, CONV_DIM)
    p["b_conv_b"] = sm[:, 1, 0].reshape(1, CONV_DIM)
    p["b_gnorm"] = sm[:, 1, 1, :GN_SHARD].reshape(1, D_INNER)

    after, started = (gathered[0],), {}
    for tag, layer, names in (("0_ffn", 0, ("w_ffn1", "w_ffn2")), ("1_mixer", 1, ("w_kv", "b_in")),
                              ("1_rest", 1, ("w_out", "w_ffn1", "w_ffn2"))):
        started[tag] = _gather_start([halves(k, layer) for k in names], after, name=f"gather_start_{tag}")
        after = (started[tag][-1],)
    p["after_start"] = after

    def finish(tag, first):
        send_sems, recv_sems, shards, zones, _ = started[tag]
        shards, zones = _gather_wait(send_sems, recv_sems, shards, zones, (first,), name=f"gather_wait_{tag}")
        return _gather_finish(shards, zones, name=f"gather_finish_{tag}")

    def layer0_ffn(first):
        w1, w2 = finish("0_ffn", first)
        return slots(w1), rows(w2)

    def layer1_mixer(first):
        kv, b_in = finish("1_mixer", first)
        return slots(kv), _b_in_full(slots(b_in))

    def layer1_rest(first):
        wo, w1, w2 = finish("1_rest", first)
        return rows(wo), slots(w1), rows(w2)

    p.update(layer0_ffn=layer0_ffn, layer1_mixer=layer1_mixer, layer1_rest=layer1_rest)
    return p


def _pair_parts(grads, tag):
    stacks = [g.reshape(2, -1, g.shape[-1]) for g in grads.values()]
    parts = _pair_reduce(stacks, name=f"grads_pair_reduce_{tag}")
    return [t.reshape(N_CHIPS, -1, t.shape[-1]) for t in parts]


def _chip_sums(chip, names, parts, landed, tag):
    return {k: _sum_contributions(chip, t, u, name=f"grads_chip_sum_{k}_{tag}")
            for k, t, u in zip(names, parts, landed)}


def _small_layout(shapes):
    offs, o = {}, 0
    for k in (*SMALL_REPL, *SMALL_SHARD):
        size = math.prod(shapes[k])
        offs[k] = (o, size)
        o += size
    rows = -(-(o + 1) // (8 * 128)) * 8
    return offs, rows


def _reduce_small(g, loss_part, full_shapes):
    offs, rows = _small_layout(full_shapes)
    flat = jnp.concatenate([*[g[k].reshape(-1) for k in (*SMALL_REPL, *SMALL_SHARD)], loss_part[0, :1]])
    flat = jnp.pad(flat, (0, rows * 128 - flat.shape[0])).reshape(rows, 128)
    total = _all_reduce_small(flat, name="small_all_reduce").reshape(-1)
    end = max(o + n for o, n in offs.values())
    return {k: total[o:o + n].reshape(full_shapes[k]) for k, (o, n) in offs.items()}, total[end]


def kernel(x, mem, norm_mix, norm_ffn, mem_norm, w_kv, w_out, w_ffn1, w_ffn2, a_in, a_ln_g, a_ln_b, a_ws, a_bs, b_in, b_conv_w, b_conv_b, b_dt_bias, b_a_log, b_d, b_gnorm, final_norm, loss_target, m_norm_mix, m_norm_ffn, m_mem_norm, m_w_kv, m_w_out, m_w_ffn1, m_w_ffn2, m_a_in, m_a_ln_g, m_a_ln_b, m_a_ws, m_a_bs, m_b_in, m_b_conv_w, m_b_conv_b, m_b_dt_bias, m_b_a_log, m_b_d, m_b_gnorm, m_final_norm, v_norm_mix, v_norm_ffn, v_mem_norm, v_w_kv, v_w_out, v_w_ffn1, v_w_ffn2, v_a_in, v_a_ln_g, v_a_ln_b, v_a_ws, v_a_bs, v_b_in, v_b_conv_w, v_b_conv_b, v_b_dt_bias, v_b_a_log, v_b_d, v_b_gnorm, v_final_norm):
    w = dict(norm_mix=norm_mix, norm_ffn=norm_ffn, mem_norm=mem_norm, w_kv=w_kv, w_out=w_out, w_ffn1=w_ffn1,
             w_ffn2=w_ffn2, a_in=a_in, a_ln_g=a_ln_g, a_ln_b=a_ln_b, a_ws=a_ws, a_bs=a_bs, b_in=b_in, b_conv_w=b_conv_w,
             b_conv_b=b_conv_b, b_dt_bias=b_dt_bias, b_a_log=b_a_log, b_d=b_d, b_gnorm=b_gnorm, final_norm=final_norm)
    mom = dict(norm_mix=m_norm_mix, norm_ffn=m_norm_ffn, mem_norm=m_mem_norm, w_kv=m_w_kv, w_out=m_w_out,
               w_ffn1=m_w_ffn1, w_ffn2=m_w_ffn2, a_in=m_a_in, a_ln_g=m_a_ln_g, a_ln_b=m_a_ln_b, a_ws=m_a_ws,
               a_bs=m_a_bs, b_in=m_b_in, b_conv_w=m_b_conv_w, b_conv_b=m_b_conv_b, b_dt_bias=m_b_dt_bias,
               b_a_log=m_b_a_log, b_d=m_b_d, b_gnorm=m_b_gnorm, final_norm=m_final_norm)
    var = dict(norm_mix=v_norm_mix, norm_ffn=v_norm_ffn, mem_norm=v_mem_norm, w_kv=v_w_kv, w_out=v_w_out,
               w_ffn1=v_w_ffn1, w_ffn2=v_w_ffn2, a_in=v_a_in, a_ln_g=v_a_ln_g, a_ln_b=v_a_ln_b, a_ws=v_a_ws,
               a_bs=v_a_bs, b_in=v_b_in, b_conv_w=v_b_conv_w, b_conv_b=v_b_conv_b, b_dt_bias=v_b_dt_bias,
               b_a_log=v_b_a_log, b_d=v_b_d, b_gnorm=v_b_gnorm, final_norm=v_final_norm)

    p = _gather_weights(w)
    p.update(norm_mix=norm_mix, norm_ffn=norm_ffn, mem_norm=mem_norm, a_ln_g=a_ln_g, a_ln_b=a_ln_b, a_ws=a_ws[0],
             a_bs=a_bs[0], b_dt_bias=b_dt_bias, b_a_log=b_a_log, b_d=b_d, final_norm=final_norm)
    chip = 2 * lax.axis_index("x") + lax.axis_index("y")
    chip_arr = jnp.reshape(chip, (1,)).astype(jnp.int32)
    started = {}

    def start_scatter(tag):
        def hook(grads):
            start = _chip_scatter_start(_pair_parts(grads, tag), name=f"grads_chip_scatter_start_{tag}")
            started[tag] = (tuple(grads), start)
            return start[-1]
        return hook

    loss_part, dx, g, _, _ = _local_step(x[0], mem[0], loss_target[0], p, start_scatter("1"), start_scatter("0f"),
                                         start_scatter("0m"))
    full_shapes = {k: w[k].shape for k in SMALL_REPL}
    full_shapes.update(b_conv_w=(1, CONV_K, CONV_DIM), b_conv_b=(1, CONV_DIM), b_gnorm=(1, D_INNER))
    grads, loss = _reduce_small(g, loss_part, full_shapes)
    grads["b_conv_w"] = lax.dynamic_slice_in_dim(grads["b_conv_w"], chip * CONV_SHARD, CONV_SHARD, axis=2)
    grads["b_conv_b"] = lax.dynamic_slice_in_dim(grads["b_conv_b"], chip * CONV_SHARD, CONV_SHARD, axis=1)
    grads["b_gnorm"] = lax.dynamic_slice_in_dim(grads["b_gnorm"], chip * GN_SHARD, GN_SHARD, axis=1)

    def finish_scatter(tag, *first):
        names, (send_sems, recv_sems, parts, lands, _) = started[tag]
        parts, landed = _chip_scatter_wait(send_sems, recv_sems, parts, lands, first,
                                           name=f"grads_chip_scatter_wait_{tag}")
        return _chip_sums(chip_arr, names, parts, landed, tag)

    def adamw(names, grads):
        for k in names:
            shape = w[k].shape
            if len(shape) == 3 and shape[2] % 128 and not shape[1] % 128:
                flat = unflat = lambda a: jnp.transpose(a, (0, 2, 1))
            else:
                flat = (lambda a: a) if len(shape) == 3 else (lambda a: a.reshape(1, -1, shape[-1]))
                unflat = lambda a: a.reshape(shape)
            d, m_new, v_new = _adamw(flat(w[k]), flat(grads[k]), flat(mom[k]), flat(var[k]), name=f"adamw_{k}")
            delta[k], new_m[k], new_v[k] = unflat(d), unflat(m_new), unflat(v_new)

    delta, new_m, new_v = {}, {}, {}
    halves = [finish_scatter("0f", dx), finish_scatter("1", dx)]
    early = ("w_ffn1", "w_ffn2", "b_in")
    shared = _pair_share([[halves[layer][k] for layer in range(2) if k in halves[layer]] for k in early],
                         name="grads_pair_share_early")
    grads.update({k: a.reshape(w[k].shape) for k, a in zip(early, shared)})
    adamw([k for k in WEIGHTS if k in grads], grads)
    halves[0].update(finish_scatter("0m", delta["w_ffn2"]))
    late = ("w_kv", "w_out", "a_in")
    shared = _pair_share([[halves[layer][k] for layer in range(2) if k in halves[layer]] for k in late],
                         name="grads_pair_share_late")
    grads.update({k: a.reshape(w[k].shape) for k, a in zip(late, shared)})
    adamw(late, grads)

    return (loss, dx.reshape(x.shape), *[grads[k] for k in WEIGHTS], *[delta[k] for k in WEIGHTS],
            *[new_m[k] for k in WEIGHTS], *[new_v[k] for k in WEIGHTS])
```

```python
import math

import jax
import jax.numpy as jnp
from jax import lax
from jax.experimental import pallas as pl
from jax.experimental.pallas import tpu as pltpu

F32 = jnp.float32
BF16 = jnp.bfloat16
SDS = jax.ShapeDtypeStruct

D_MODEL = 1024
SEQ = 2048
CHUNK = 128
N_MEM = 256
D_INNER = 2048
A_GROUPS = 8
A_GROUP_W = D_INNER // A_GROUPS
SSM_HEADS = 32
SSM_HEAD_DIM = 64
SSM_GROUPS = 4
SSM_HPG = 8
SSM_STATE = 128
SSM_GROUP_W = SSM_HPG * SSM_HEAD_DIM
CONV_K = 4
CONV_DIM = 3072
X_HEADS = 4
X_HEAD_DIM = 256
X_WIDTH = 1024
MIX_OUT = 3072
D_FF = 4096
A_IN = 5120
B_IN = 6176
B_IN_PAD = 6272
B_Q_OFF = 5120
B_DT_OFF = 6144
N_CHUNKS = SEQ // CHUNK
EPS = 1e-6
N_CHIPS = 4

ADAM_LR = 0.001
ADAM_B1 = 0.9
ADAM_B2 = 0.999
ADAM_EPS = 1e-08
ADAM_WD = 0.01
ADAM_STEP = 10

VMEM_LIMIT = 48 * 1024 * 1024
MESH = pl.DeviceIdType.MESH


def _cparams(sem):
    return pltpu.CompilerParams(dimension_semantics=sem, vmem_limit_bytes=VMEM_LIMIT)


def _dot(a, b, dims=(((1,), (0,)), ((), ()))):
    return lax.dot_general(a.astype(BF16), b.astype(BF16), dims, preferred_element_type=F32)


def _dot_nt(a, b):
    return _dot(a, b, (((1,), (1,)), ((), ())))


def _dot_tn(a, b):
    return _dot(a, b, (((0,), (0,)), ((), ())))


def _pick(n, cands):
    for c in cands:
        if n % c == 0:
            return c
    raise ValueError(f"no tile for {n}")


def _mm_call(a, b, *, dims, grid, a_spec, b_spec, acc_shape, out_shapes, out_specs, name,
             extras=(), extra_specs=(), epilogue=None, after=()):
    n_k = grid[2]
    n_extra = len(extras)
    n_out = len(out_shapes)
    n_in = 2 + n_extra + len(after)

    def finish(total, extra_refs, out_refs):
        vals = (total,) if epilogue is None else epilogue(total, *[e[...] for e in extra_refs])
        for o_ref, v in zip(out_refs, vals):
            o_ref[...] = v.astype(o_ref.dtype)

    def body_one_step(*refs):
        finish(_dot(refs[0][...], refs[1][...], dims), refs[2:2 + n_extra], refs[n_in:n_in + n_out])

    def body(*refs):
        acc = refs[-1]
        k = pl.program_id(2)

        @pl.when(k == 0)
        def _():
            acc[...] = jnp.zeros_like(acc)

        acc[...] += _dot(refs[0][...], refs[1][...], dims)

        @pl.when(k == n_k - 1)
        def _():
            finish(acc[...], refs[2:2 + n_extra], refs[n_in:n_in + n_out])

    return pl.pallas_call(
        body_one_step if n_k == 1 else body, grid=grid,
        in_specs=[a_spec, b_spec, *extra_specs, *([ANY] * len(after))], out_specs=list(out_specs),
        out_shape=list(out_shapes), scratch_shapes=[] if n_k == 1 else [pltpu.VMEM(acc_shape, F32)],
        compiler_params=_cparams(("parallel", "parallel", "arbitrary")), name=name,
    )(a, b, *extras, *after)


def _w_dims(w):
    if w.ndim == 2:
        return w.shape[0], w.shape[1], 1, w.shape[1]
    return w.shape[1], w.shape[0] * w.shape[2], w.shape[0], w.shape[2]


def _mm_nn(a, w, *, name, out_dtype=F32, a_cols=None, extras=(), epilogue=None, n_out_dtypes=None, after=()):
    m = a.shape[0]
    k_dim, n_dim, _, n_slot = _w_dims(w)
    a_off, a_w = (0, a.shape[1]) if a_cols is None else a_cols
    assert a_w == k_dim
    tm = _pick(m, (2048, 1024, 512, 256))
    tn = _pick(n_slot, (512, 896, 640, 256, 128))
    tk = _pick(k_dim, (1024, 768, 512, 384, 256, 128))
    assert a_off % tk == 0
    nb = n_slot // tn
    a_spec = pl.BlockSpec((tm, tk), lambda i, j, k: (i, a_off // tk + k))
    if w.ndim == 2:
        b_spec = pl.BlockSpec((tk, tn), lambda i, j, k: (k, j))
    else:
        b_spec = pl.BlockSpec((None, tk, tn), lambda i, j, k: (j // nb, k, j % nb))
    o_spec = pl.BlockSpec((tm, tn), lambda i, j, k: (i, j))
    dts = n_out_dtypes or (out_dtype,)
    outs = _mm_call(a, w, dims=(((1,), (0,)), ((), ())), grid=(m // tm, n_dim // tn, k_dim // tk),
                    a_spec=a_spec, b_spec=b_spec, acc_shape=(tm, tn),
                    out_shapes=[SDS((m, n_dim), dt) for dt in dts], out_specs=[o_spec] * len(dts), name=name,
                    extras=extras, extra_specs=[o_spec] * len(extras), epilogue=epilogue, after=after)
    return outs if n_out_dtypes else outs[0]


def _mm_nt(a, w, *, name, out_dtype=F32, extras=(), epilogue=None, after=()):
    m = a.shape[0]
    k_dim, n_dim, _, n_slot = _w_dims(w)
    assert a.shape[1] == n_dim
    tm = _pick(m, (2048, 1024, 512, 256))
    to = _pick(k_dim, (512, 384, 256, 128))
    tc = _pick(n_slot, (1280, 1024, 896, 640, 512, 256, 128))
    nb = n_slot // tc
    a_spec = pl.BlockSpec((tm, tc), lambda i, j, k: (i, k))
    if w.ndim == 2:
        b_spec = pl.BlockSpec((to, tc), lambda i, j, k: (j, k))
    else:
        b_spec = pl.BlockSpec((None, to, tc), lambda i, j, k: (k // nb, j, k % nb))
    o_spec = pl.BlockSpec((tm, to), lambda i, j, k: (i, j))
    return _mm_call(a, w, dims=(((1,), (1,)), ((), ())), grid=(m // tm, k_dim // to, n_dim // tc),
                    a_spec=a_spec, b_spec=b_spec, acc_shape=(tm, to),
                    out_shapes=[SDS((m, k_dim), out_dtype)], out_specs=[o_spec], name=name,
                    extras=extras, extra_specs=[o_spec] * len(extras), epilogue=epilogue, after=after)[0]


def _mm_tn(x, dy, *, name, x_cols=None):
    s = x.shape[0]
    x_off, k_dim = (0, x.shape[1]) if x_cols is None else x_cols
    n_dim = dy.shape[1]
    tm = _pick(k_dim, (1024, 768, 512, 384, 256, 128))
    tn = _pick(n_dim, (512, 896, 640, 256, 128))
    tk = _pick(s, (2048, 1024, 512, 256))
    assert x_off % tm == 0
    a_spec = pl.BlockSpec((tk, tm), lambda i, j, k: (k, x_off // tm + i))
    b_spec = pl.BlockSpec((tk, tn), lambda i, j, k: (k, j))
    o_spec = pl.BlockSpec((tm, tn), lambda i, j, k: (i, j))
    return _mm_call(x, dy, dims=(((0,), (0,)), ((), ())), grid=(k_dim // tm, n_dim // tn, s // tk),
                    a_spec=a_spec, b_spec=b_spec, acc_shape=(tm, tn),
                    out_shapes=[SDS((k_dim, n_dim), F32)], out_specs=[o_spec], name=name)[0]


def _mm_tn_stacked(x, dy, *, name, col_slots):
    s, k_dim = x.shape
    n_dim = dy.shape[1]
    r, c = (k_dim // 2, n_dim // N_CHIPS) if col_slots else (k_dim // N_CHIPS // 2, n_dim)
    tm = 2 * r
    tn = _pick(c, (512, 896, 640, 256, 128))
    tk = _pick(s, (2048, 1024, 512, 256))
    a_spec = pl.BlockSpec((tk, tm), lambda i, j, k: (k, i))
    b_spec = pl.BlockSpec((tk, tn), lambda i, j, k: (k, j))
    if col_slots:
        nb = c // tn
        o_spec = pl.BlockSpec((2, None, r, tn), lambda i, j, k: (0, j // nb, 0, j % nb))
    else:
        o_spec = pl.BlockSpec((2, None, r, tn), lambda i, j, k: (0, i, 0, j))
    return _mm_call(x, dy, dims=(((0,), (0,)), ((), ())), grid=(k_dim // tm, n_dim // tn, s // tk),
                    a_spec=a_spec, b_spec=b_spec, acc_shape=(tm, tn), epilogue=lambda acc: (acc.reshape(2, r, tn),),
                    out_shapes=[SDS((2, N_CHIPS, r, c), F32)], out_specs=[o_spec], name=name)[0]


def _rms(x, g):
    return x * lax.rsqrt(jnp.mean(x * x, axis=-1, keepdims=True) + EPS) * g


def _rms_fwd(h, g, *, name):
    rows, d = h.shape
    tr = _pick(rows, (512, 256))

    def body(h_ref, g_ref, o_ref):
        o_ref[...] = _rms(h_ref[...], g_ref[...]).astype(o_ref.dtype)

    return pl.pallas_call(
        body, grid=(rows // tr,),
        in_specs=[pl.BlockSpec((tr, d), lambda i: (i, 0)), pl.BlockSpec((1, d), lambda i: (0, 0))],
        out_specs=pl.BlockSpec((tr, d), lambda i: (i, 0)), out_shape=SDS((rows, d), BF16),
        compiler_params=_cparams(("parallel",)), name=name)(h, g)


def _rms_bwd(h, g, da, dres, *, name):
    rows, d = h.shape
    tr = _pick(rows, (512, 256))

    def body(h_ref, g_ref, da_ref, dres_ref, dh_ref, dg_ref):
        _, vjp = jax.vjp(_rms, h_ref[...], g_ref[...])
        dh, dg = vjp(da_ref[...].astype(F32))
        dh_ref[...] = dres_ref[...] + dh

        @pl.when(pl.program_id(0) == 0)
        def _():
            dg_ref[...] = jnp.zeros_like(dg_ref)

        dg_ref[...] += dg

    row_spec = pl.BlockSpec((tr, d), lambda i: (i, 0))
    vec_spec = pl.BlockSpec((1, d), lambda i: (0, 0))
    return pl.pallas_call(
        body, grid=(rows // tr,), in_specs=[row_spec, vec_spec, row_spec, row_spec],
        out_specs=[row_spec, vec_spec], out_shape=[SDS((rows, d), F32), SDS((1, d), F32)],
        compiler_params=_cparams(("arbitrary",)), name=name)(h, g, da, dres)


def _loss_head(h, g, target, *, name):
    rows, d = h.shape
    tr = _pick(rows, (512, 256))

    def body(h_ref, g_ref, t_ref, loss_ref, dh_ref, dg_ref):
        y, vjp = jax.vjp(_rms, h_ref[...], g_ref[...])
        err = y - t_ref[...]
        dh, dg = vjp(err * (1.0 / d))
        dh_ref[...] = dh

        @pl.when(pl.program_id(0) == 0)
        def _():
            dg_ref[...] = jnp.zeros_like(dg_ref)
            loss_ref[...] = jnp.zeros_like(loss_ref)

        dg_ref[...] += dg
        part = jnp.sum(jnp.sum(err * err, axis=-1, keepdims=True), axis=0, keepdims=True) * (0.5 / d)
        loss_ref[...] += jnp.broadcast_to(part, loss_ref.shape)

    row_spec = pl.BlockSpec((tr, d), lambda i: (i, 0))
    vec_spec = pl.BlockSpec((1, d), lambda i: (0, 0))
    loss_spec = pl.BlockSpec((8, 128), lambda i: (0, 0))
    return pl.pallas_call(
        body, grid=(rows // tr,), in_specs=[row_spec, vec_spec, row_spec],
        out_specs=[loss_spec, row_spec, vec_spec],
        out_shape=[SDS((8, 128), F32), SDS((rows, d), F32), SDS((1, d), F32)],
        compiler_params=_cparams(("arbitrary",)), name=name)(h, g, target)


def _gelu(x):
    return 0.5 * x * (1.0 + lax.erf(x * (1.0 / math.sqrt(2.0))))


def _gate_tile(pu, pv, ln_g, ln_b, ws, bs_t):
    u = [_gelu(p) for p in pu]
    v = [_gelu(p) for p in pv]
    mu = sum(jnp.sum(t, axis=-1, keepdims=True) for t in v) * (1.0 / D_INNER)
    vc = [t - mu for t in v]
    var = sum(jnp.sum(t * t, axis=-1, keepdims=True) for t in vc) * (1.0 / D_INNER)
    rstd = lax.rsqrt(var + EPS)
    row = lax.broadcasted_iota(jnp.int32, (CHUNK, CHUNK), 0)
    col = lax.broadcasted_iota(jnp.int32, (CHUNK, CHUNK), 1)
    out = []
    for gi in range(A_GROUPS):
        vn = vc[gi] * rstd * ln_g[gi] + ln_b[gi]
        w = jnp.where(row >= col, ws[gi], 0.0)
        sv = _dot(w, vn) + bs_t[gi]
        out.append(u[gi] * sv)
    return out


def _split(ref, n, width):
    return [ref[:, i * width:(i + 1) * width] for i in range(n)]


def _gate_in_specs():
    return [
        pl.BlockSpec((CHUNK, D_INNER), lambda c: (c, 0)),
        pl.BlockSpec((CHUNK, D_INNER), lambda c: (c, 1)),
        pl.BlockSpec((1, D_INNER), lambda c: (0, 0)),
        pl.BlockSpec((1, D_INNER), lambda c: (0, 0)),
        pl.BlockSpec((A_GROUPS, CHUNK, CHUNK), lambda c: (0, 0, 0)),
        pl.BlockSpec((A_GROUPS, CHUNK, 1), lambda c: (0, 0, 0)),
    ]


def _gate_args(u_ref, v_ref, g_ref, b_ref, ws_ref, bs_ref):
    ng, gw = A_GROUPS, A_GROUP_W
    return (_split(u_ref, ng, gw), _split(v_ref, ng, gw), _split(g_ref, ng, gw), _split(b_ref, ng, gw),
            [ws_ref[i] for i in range(ng)], [bs_ref[i] for i in range(ng)])


def _gate_fwd(proj, ln_g, ln_b, ws, bs_col, mixcat, *, name):
    def body(u_ref, v_ref, g_ref, b_ref, ws_ref, bs_ref, cat_in, cat_ref):
        del cat_in
        out = _gate_tile(*_gate_args(u_ref, v_ref, g_ref, b_ref, ws_ref, bs_ref))
        for gi, o in enumerate(out):
            cat_ref[:, gi * A_GROUP_W:(gi + 1) * A_GROUP_W] = o.astype(cat_ref.dtype)

    return pl.pallas_call(
        body, grid=(N_CHUNKS,), in_specs=[*_gate_in_specs(), pl.BlockSpec(memory_space=pl.ANY)],
        out_specs=pl.BlockSpec((CHUNK, D_INNER), lambda c: (c, 0)), out_shape=SDS(mixcat.shape, mixcat.dtype),
        input_output_aliases={6: 0}, compiler_params=_cparams(("parallel",)), name=name,
    )(proj, proj, ln_g, ln_b, ws, bs_col, mixcat)


def _gate_bwd(proj, ln_g, ln_b, ws, bs_col, dcat, dproj, *, name):
    ng, gw = A_GROUPS, A_GROUP_W

    def body(u_ref, v_ref, g_ref, b_ref, ws_ref, bs_ref, d_ref, dproj_in, dproj_ref, dg_ref, db_ref, dws_ref, dbs_ref):
        del dproj_in
        args = _gate_args(u_ref, v_ref, g_ref, b_ref, ws_ref, bs_ref)
        _, vjp = jax.vjp(_gate_tile, *args)
        dpu, dpv, dg, db, dws, dbs = vjp(_split(d_ref, ng, gw))
        for gi in range(ng):
            dproj_ref[:, gi * gw:(gi + 1) * gw] = dpu[gi].astype(dproj_ref.dtype)
            dproj_ref[:, D_INNER + gi * gw:D_INNER + (gi + 1) * gw] = dpv[gi].astype(dproj_ref.dtype)

        @pl.when(pl.program_id(0) == 0)
        def _():
            for r in (dg_ref, db_ref, dws_ref, dbs_ref):
                r[...] = jnp.zeros_like(r)

        for gi in range(ng):
            dg_ref[:, gi * gw:(gi + 1) * gw] += dg[gi]
            db_ref[:, gi * gw:(gi + 1) * gw] += db[gi]
            dws_ref[gi] += dws[gi]
            dbs_ref[gi] += dbs[gi]

    in_specs = _gate_in_specs()
    return pl.pallas_call(
        body, grid=(N_CHUNKS,),
        in_specs=[*in_specs, pl.BlockSpec((CHUNK, D_INNER), lambda c: (c, 0)), pl.BlockSpec(memory_space=pl.ANY)],
        out_specs=[pl.BlockSpec((CHUNK, 2 * D_INNER), lambda c: (c, 0)), *in_specs[2:]],
        out_shape=[SDS(dproj.shape, dproj.dtype), SDS((1, D_INNER), F32), SDS((1, D_INNER), F32),
                   SDS((ng, CHUNK, CHUNK), F32), SDS((ng, CHUNK, 1), F32)],
        input_output_aliases={7: 0}, compiler_params=_cparams(("arbitrary",)), name=name,
    )(proj, proj, ln_g, ln_b, ws, bs_col, dcat, dproj)


ATT_TQ = 2048


def _attn_tile(q, k, v):
    s = _dot_nt(q, k) * (1.0 / math.sqrt(X_HEAD_DIM))
    s = s - jnp.max(s, axis=-1, keepdims=True)
    e = jnp.exp(s)
    p = e / jnp.sum(e, axis=-1, keepdims=True)
    return _dot(p, v)


def _attn_in_specs(q_blk, order):
    hd = X_HEAD_DIM
    return [
        pl.BlockSpec((ATT_TQ, hd), lambda a, b: (order(a, b)[0], q_blk + order(a, b)[1])),
        pl.BlockSpec((N_MEM, hd), lambda a, b: (0, order(a, b)[1])),
        pl.BlockSpec((N_MEM, hd), lambda a, b: (0, X_HEADS + order(a, b)[1])),
    ]


def _attn_fwd(proj, q_off, kv, *, name):
    order = lambda i, h: (i, h)
    cat_blk = D_INNER // X_HEAD_DIM

    def body(q_ref, k_ref, v_ref, o_ref):
        o_ref[...] = _attn_tile(q_ref[...], k_ref[...], v_ref[...]).astype(o_ref.dtype)

    return pl.pallas_call(
        body, grid=(SEQ // ATT_TQ, X_HEADS), in_specs=_attn_in_specs(q_off // X_HEAD_DIM, order),
        out_specs=pl.BlockSpec((ATT_TQ, X_HEAD_DIM), lambda i, h: (i, cat_blk + h)),
        out_shape=SDS((SEQ, MIX_OUT), BF16), compiler_params=_cparams(("parallel", "parallel")), name=name,
    )(proj, kv, kv)


def _attn_bwd(proj, q_off, kv, dcat, dproj_width, dq_off, *, name):
    order = lambda h, i: (i, h)
    cat_blk = D_INNER // X_HEAD_DIM
    dq_blk = dq_off // X_HEAD_DIM

    def body(q_ref, k_ref, v_ref, do_ref, dq_ref, dk_ref, dv_ref):
        _, vjp = jax.vjp(_attn_tile, q_ref[...], k_ref[...], v_ref[...])
        dq, dk, dv = vjp(do_ref[...])
        dq_ref[...] = dq.astype(dq_ref.dtype)

        @pl.when(pl.program_id(1) == 0)
        def _():
            dk_ref[...] = jnp.zeros_like(dk_ref)
            dv_ref[...] = jnp.zeros_like(dv_ref)

        dk_ref[...] += dk
        dv_ref[...] += dv

    kv_spec = pl.BlockSpec((N_MEM, X_HEAD_DIM), lambda h, i: (0, h))
    return pl.pallas_call(
        body, grid=(X_HEADS, SEQ // ATT_TQ),
        in_specs=[*_attn_in_specs(q_off // X_HEAD_DIM, order),
                  pl.BlockSpec((ATT_TQ, X_HEAD_DIM), lambda h, i: (i, cat_blk + h))],
        out_specs=[pl.BlockSpec((ATT_TQ, X_HEAD_DIM), lambda h, i: (i, dq_blk + h)), kv_spec, kv_spec],
        out_shape=[SDS((SEQ, dproj_width), BF16), SDS((N_MEM, X_WIDTH), F32), SDS((N_MEM, X_WIDTH), F32)],
        compiler_params=_cparams(("parallel", "arbitrary")), name=name,
    )(proj, kv, kv, dcat)


CONV_TC = 512
CONV_ROWS = 128
CONV_HALO = 8


def _shift_down(x, s):
    if s == 0:
        return x
    row = lax.broadcasted_iota(jnp.int32, x.shape, 0)
    return jnp.where(row >= s, pltpu.roll(x, s, 0), 0.0)


def _shift_up(x, s):
    if s == 0:
        return x
    n = x.shape[0]
    row = lax.broadcasted_iota(jnp.int32, x.shape, 0)
    return jnp.where(row < n - s, pltpu.roll(x, n - s, 0), 0.0)


def _conv_pre(x, w_ref, b_ref):
    pre = b_ref[...] + jnp.zeros_like(x)
    for k in range(CONV_K):
        pre = pre + w_ref[k:k + 1, :] * _shift_down(x, CONV_K - 1 - k)
    return pre


def _conv_fwd(proj, w, b, *, name):
    blk0 = D_INNER // CONV_TC

    def body(x_ref, w_ref, b_ref, o_ref):
        pre = _conv_pre(x_ref[...], w_ref, b_ref)
        o_ref[...] = pre * jax.nn.sigmoid(pre)

    return pl.pallas_call(
        body, grid=(CONV_DIM // CONV_TC,),
        in_specs=[pl.BlockSpec((SEQ, CONV_TC), lambda j: (0, blk0 + j)), pl.BlockSpec((CONV_K, CONV_TC), lambda j: (0, j)),
                  pl.BlockSpec((1, CONV_TC), lambda j: (0, j))],
        out_specs=pl.BlockSpec((SEQ, CONV_TC), lambda j: (0, j)), out_shape=SDS((SEQ, CONV_DIM), F32),
        compiler_params=_cparams(("parallel",)), name=name)(proj, w, b)


def _conv_bwd(proj, w, b, dxs, dbm, dcm, dproj, *, name):
    tc = CONV_TC // 2
    blk0 = D_INNER // tc
    n_x = D_INNER // tc
    n_b = SSM_GROUPS * SSM_STATE // tc

    window = CONV_ROWS + 2 * CONV_HALO
    n_rows = SEQ // CONV_ROWS

    def body(x_ref, w_ref, b_ref, dxs_ref, dbm_ref, dcm_ref, dproj_in, dproj_ref, dw_ref, db_ref):
        del dproj_in
        j = pl.program_id(0)

        def rows_step(lanes, first, keep, sums):
            start = first - keep
            if not isinstance(start, int):
                start = pl.multiple_of(start, CONV_HALO)
            rows = pl.ds(start, window)
            down = _shift_down if keep == 0 else lambda v, s: pltpu.roll(v, s, 0) if s else v
            up = _shift_up if keep == 2 * CONV_HALO else lambda v, s: pltpu.roll(v, window - s, 0) if s else v
            x = x_ref[rows, lanes]
            w = w_ref[:, lanes]
            pre = b_ref[:, lanes] + jnp.zeros_like(x)
            for k in range(CONV_K):
                pre = pre + w[k:k + 1, :] * down(x, CONV_K - 1 - k)
            sg = jax.nn.sigmoid(pre)
            dact = jnp.where(j < n_x, dxs_ref[rows, lanes],
                             jnp.where(j < n_x + n_b, dbm_ref[rows, lanes], dcm_ref[rows, lanes]))
            dpre = dact * (sg * (1.0 + pre * (1.0 - sg)))
            kept = lambda v: jnp.sum(v[keep:keep + CONV_ROWS, :], axis=0, keepdims=True)
            dx = jnp.zeros_like(x)
            new_sums = []
            for k in range(CONV_K):
                s = CONV_K - 1 - k
                dx = dx + w[k:k + 1, :] * up(dpre, s)
                new_sums.append(sums[k] + kept(dpre * down(x, s)))
            new_sums.append(sums[CONV_K] + kept(dpre))
            dproj_ref[pl.ds(first, CONV_ROWS), lanes] = dx[keep:keep + CONV_ROWS, :].astype(dproj_ref.dtype)
            return tuple(new_sums)

        for c in range(tc // 128):
            lanes = pl.ds(c * 128, 128)
            sums = rows_step(lanes, 0, 0, (jnp.zeros((1, 128), F32),) * (CONV_K + 1))
            sums = lax.fori_loop(
                1, n_rows - 1,
                lambda i, sums: rows_step(lanes, pl.multiple_of(i * CONV_ROWS, CONV_ROWS), CONV_HALO, sums), sums)
            sums = rows_step(lanes, SEQ - CONV_ROWS, 2 * CONV_HALO, sums)
            for k in range(CONV_K):
                dw_ref[k:k + 1, lanes] = sums[k]
            db_ref[:, lanes] = sums[CONV_K]

    clip = lambda v, hi: jnp.minimum(jnp.maximum(v, 0), hi)
    return pl.pallas_call(
        body, grid=(CONV_DIM // tc,),
        in_specs=[pl.BlockSpec((SEQ, tc), lambda j: (0, blk0 + j)), pl.BlockSpec((CONV_K, tc), lambda j: (0, j)),
                  pl.BlockSpec((1, tc), lambda j: (0, j)),
                  pl.BlockSpec((SEQ, tc), lambda j: (0, clip(j, n_x - 1))),
                  pl.BlockSpec((SEQ, tc), lambda j: (0, clip(j - n_x, n_b - 1))),
                  pl.BlockSpec((SEQ, tc), lambda j: (0, clip(j - n_x - n_b, n_b - 1))),
                  pl.BlockSpec(memory_space=pl.ANY)],
        out_specs=[pl.BlockSpec((SEQ, tc), lambda j: (0, blk0 + j)), pl.BlockSpec((CONV_K, tc), lambda j: (0, j)),
                   pl.BlockSpec((1, tc), lambda j: (0, j))],
        out_shape=[SDS(dproj.shape, dproj.dtype), SDS((CONV_K, CONV_DIM), F32), SDS((1, CONV_DIM), F32)],
        input_output_aliases={6: 0}, compiler_params=_cparams(("parallel",)), name=name,
    )(proj, w, b, dxs, dbm, dcm, dproj)


SSM_PAIRS = SSM_HPG // 2


def _dot_exact01(x, m01, m01_t, x_first, differentiable):
    def product(v, m):
        hi = v.astype(BF16)
        rest = v - hi.astype(F32)
        mid = rest.astype(BF16)
        lo = (rest - mid.astype(F32)).astype(BF16)
        dims = (((1,), (0,)), ((), ()))
        dot = lambda part: lax.dot_general(*((part, m) if x_first else (m, part)), dims, preferred_element_type=F32)
        return dot(hi) + dot(mid) + dot(lo)

    if not differentiable:
        return product(x, m01)

    @jax.custom_vjp
    def exact(v):
        return product(v, m01)

    exact.defvjp(lambda v: (product(v, m01), None), lambda _, ct: (product(ct, m01_t),))
    return exact(x)


def _ssd_tile(xp, zp, bm, cm, hp, dt_c, dt_r, bias, bias_col, alog, alog_col, dsk, gnp, differentiable=False):
    row = lax.broadcasted_iota(jnp.int32, (CHUNK, CHUNK), 0)
    col = lax.broadcasted_iota(jnp.int32, (CHUNK, CHUNK), 1)
    causal = row >= col
    left = col < SSM_HEAD_DIM
    top = row < SSM_HEAD_DIM
    ones = jnp.ones((CHUNK, CHUNK), BF16)
    cb = _dot_nt(cm, bm)
    dtp = jax.nn.softplus(dt_c + bias)
    da_c = dtp * -jnp.exp(alog)
    da_r = jax.nn.softplus(dt_r + bias_col) * -jnp.exp(alog_col)
    lower = jnp.where(causal, 1.0, 0.0).astype(BF16)
    upper = jnp.where(row <= col, 1.0, 0.0).astype(BF16)
    cs = _dot_exact01(da_c, lower, upper, False, differentiable)
    cs_rows = _dot_exact01(da_r, upper, lower, True, differentiable)
    cs_last = jnp.sum(da_c, axis=0, keepdims=True)
    ecs, decay, ecl = jnp.exp(cs), jnp.exp(cs_last - cs), jnp.exp(cs_last)
    m = [cb * jnp.exp(jnp.where(causal, cs[:, r:r + 1] - cs_rows[r:r + 1, :], -1e30)) for r in range(SSM_HPG)]
    ygs, hn = [], []
    for p in range(SSM_PAIRS):
        a, b = 2 * p, 2 * p + 1
        pair = lambda v: jnp.where(left, v[:, a:a + 1], v[:, b:b + 1])
        xdt = xp[p] * pair(dtp)
        y = jnp.where(left, _dot(m[a], xdt), _dot(m[b], xdt))
        y = y + _dot_nt(cm, hp[p]) * pair(ecs)
        y = y + xp[p] * pair(dsk)
        states = _dot_tn(xdt * pair(decay), bm)
        hn.append(hp[p] * jnp.where(top, ecl[:, a:a + 1], ecl[:, b:b + 1]) + states)
        ygs.append(y * (zp[p] * jax.nn.sigmoid(zp[p])))
    ms = sum(_dot(t * t, ones) for t in ygs) * (1.0 / SSM_GROUP_W)
    rs = lax.rsqrt(ms + EPS)
    return [ygs[p] * rs * gnp[p] for p in range(SSM_PAIRS)], hn


def _ssd_in_specs(cidx):
    gw, n = SSM_GROUP_W, SSM_STATE
    bm_blk = D_INNER // n
    return [
        pl.BlockSpec((CHUNK, gw), lambda g, c: (cidx(c), g)),
        pl.BlockSpec((CHUNK, gw), lambda g, c: (cidx(c), g)),
        pl.BlockSpec((CHUNK, n), lambda g, c: (cidx(c), bm_blk + g)),
        pl.BlockSpec((CHUNK, n), lambda g, c: (cidx(c), bm_blk + SSM_GROUPS + g)),
        pl.BlockSpec((None, CHUNK, SSM_HPG), lambda g, c: (g, cidx(c), 0)),
        pl.BlockSpec((None, SSM_HPG, CHUNK), lambda g, c: (g, 0, cidx(c))),
        pl.BlockSpec((None, 3, SSM_HPG), lambda g, c: (g, 0, 0)),
        pl.BlockSpec((None, SSM_HPG, 2), lambda g, c: (g, 0, 0)),
        pl.BlockSpec((1, gw), lambda g, c: (0, g)),
    ]


def _ssd_args(x_ref, z_ref, bm_ref, cm_ref, hp, dtc_ref, dtr_ref, prow_ref, pcol_ref, gn_ref):
    npair, w = SSM_PAIRS, 2 * SSM_HEAD_DIM
    return (_split(x_ref, npair, w), _split(z_ref, npair, w), bm_ref[...], cm_ref[...], hp, dtc_ref[...], dtr_ref[...],
            prow_ref[0:1, :], pcol_ref[:, 0:1], prow_ref[1:2, :], pcol_ref[:, 1:2], prow_ref[2:3, :],
            _split(gn_ref, npair, w))


def _pair_rows(ref):
    w = 2 * SSM_HEAD_DIM
    return [ref[p * w:(p + 1) * w, :] for p in range(SSM_PAIRS)]


def _ssd_fwd(xbc, proj, dt_c, dt_r, par_row, par_col, gn, mixcat, *, name):
    w = 2 * SSM_HEAD_DIM

    def body(x_ref, z_ref, bm_ref, cm_ref, dtc_ref, dtr_ref, prow_ref, pcol_ref, gn_ref, cat_in,
             cat_ref, hprev_ref, h_scr):
        del cat_in

        @pl.when(pl.program_id(1) == 0)
        def _():
            h_scr[...] = jnp.zeros_like(h_scr)

        hprev_ref[...] = h_scr[...]
        yn, hn = _ssd_tile(*_ssd_args(x_ref, z_ref, bm_ref, cm_ref, _pair_rows(h_scr), dtc_ref, dtr_ref, prow_ref,
                                      pcol_ref, gn_ref))
        for p in range(SSM_PAIRS):
            cat_ref[:, p * w:(p + 1) * w] = yn[p].astype(cat_ref.dtype)
            h_scr[p * w:(p + 1) * w, :] = hn[p]

    return pl.pallas_call(
        body, grid=(SSM_GROUPS, N_CHUNKS), in_specs=[*_ssd_in_specs(lambda c: c), pl.BlockSpec(memory_space=pl.ANY)],
        out_specs=[pl.BlockSpec((CHUNK, SSM_GROUP_W), lambda g, c: (c, g)),
                   pl.BlockSpec((None, None, SSM_GROUP_W, SSM_STATE), lambda g, c: (c, g, 0, 0))],
        out_shape=[SDS(mixcat.shape, mixcat.dtype), SDS((N_CHUNKS, SSM_GROUPS, SSM_GROUP_W, SSM_STATE), F32)],
        scratch_shapes=[pltpu.VMEM((SSM_GROUP_W, SSM_STATE), F32)],
        input_output_aliases={9: 0}, compiler_params=_cparams(("parallel", "arbitrary")), name=name,
    )(xbc, proj, xbc, xbc, dt_c, dt_r, par_row, par_col, gn, mixcat)


def _ssd_bwd(xbc, proj, dt_c, dt_r, par_row, par_col, gn, hprev, dcat, dproj, *, name):
    nh, w, gw, n = SSM_HPG, 2 * SSM_HEAD_DIM, SSM_GROUP_W, SSM_STATE
    rev = lambda c: N_CHUNKS - 1 - c

    def body(x_ref, z_ref, bm_ref, cm_ref, dtc_ref, dtr_ref, prow_ref, pcol_ref, gn_ref, hprev_ref, dy_ref,
             dproj_in, dz_ref, dxs_ref, dbm_ref, dcm_ref, ddtc_ref, ddtr_ref, dprow_ref, dpcol_ref, dgn_ref, dh_scr):
        del dproj_in
        first = pl.program_id(1) == 0

        @pl.when(first)
        def _():
            dh_scr[...] = jnp.zeros_like(dh_scr)
            for ref in (dprow_ref, dpcol_ref, dgn_ref):
                ref[...] = jnp.zeros_like(ref)

        args = _ssd_args(x_ref, z_ref, bm_ref, cm_ref, _pair_rows(hprev_ref), dtc_ref, dtr_ref, prow_ref, pcol_ref,
                         gn_ref)
        _, vjp = jax.vjp(lambda *a: _ssd_tile(*a, differentiable=True), *args)
        dxs, dzs, dbm, dcm, dhs, ddtc, ddtr, dbias, dbias_col, dalog, dalog_col, ddsk, dgn = vjp(
            (_split(dy_ref, SSM_PAIRS, w), _pair_rows(dh_scr)))
        dbm_ref[...] = dbm
        dcm_ref[...] = dcm
        ddtc_ref[...] = ddtc
        ddtr_ref[...] = ddtr
        for q in range(SSM_PAIRS):
            dxs_ref[:, q * w:(q + 1) * w] = dxs[q]
            dz_ref[:, q * w:(q + 1) * w] = dzs[q].astype(dz_ref.dtype)
            dh_scr[q * w:(q + 1) * w, :] = dhs[q]
            dgn_ref[:, q * w:(q + 1) * w] += dgn[q]
        for i, d in enumerate((dbias, dalog, ddsk)):
            dprow_ref[i:i + 1, :] += d
        for i, d in enumerate((dbias_col, dalog_col)):
            dpcol_ref[:, i:i + 1] += d

    return pl.pallas_call(
        body, grid=(SSM_GROUPS, N_CHUNKS),
        in_specs=[*_ssd_in_specs(rev),
                  pl.BlockSpec((None, None, gw, n), lambda g, c: (rev(c), g, 0, 0)),
                  pl.BlockSpec((CHUNK, gw), lambda g, c: (rev(c), g)),
                  pl.BlockSpec(memory_space=pl.ANY)],
        out_specs=[pl.BlockSpec((CHUNK, gw), lambda g, c: (rev(c), g)),
                   pl.BlockSpec((CHUNK, gw), lambda g, c: (rev(c), g)),
                   pl.BlockSpec((CHUNK, n), lambda g, c: (rev(c), g)),
                   pl.BlockSpec((CHUNK, n), lambda g, c: (rev(c), g)),
                   pl.BlockSpec((None, CHUNK, nh), lambda g, c: (g, rev(c), 0)),
                   pl.BlockSpec((None, nh, CHUNK), lambda g, c: (g, 0, rev(c))),
                   pl.BlockSpec((None, 3, nh), lambda g, c: (g, 0, 0)),
                   pl.BlockSpec((None, nh, 2), lambda g, c: (g, 0, 0)),
                   pl.BlockSpec((1, gw), lambda g, c: (0, g))],
        out_shape=[SDS(dproj.shape, dproj.dtype), SDS((SEQ, D_INNER), F32), SDS((SEQ, SSM_GROUPS * n), F32),
                   SDS((SEQ, SSM_GROUPS * n), F32), SDS((SSM_GROUPS, SEQ, nh), F32), SDS((SSM_GROUPS, nh, SEQ), F32),
                   SDS((SSM_GROUPS, 3, nh), F32), SDS((SSM_GROUPS, nh, 2), F32), SDS((1, D_INNER), F32)],
        scratch_shapes=[pltpu.VMEM((gw, n), F32)],
        input_output_aliases={11: 0}, compiler_params=_cparams(("parallel", "arbitrary")), name=name,
    )(xbc, proj, xbc, xbc, dt_c, dt_r, par_row, par_col, gn, hprev, dcat, dproj)


def _sum_contributions(chip, parts, landed, *, name):
    _, r, c = parts.shape
    tr = _pick(r, (256, 384, 128))

    def body(chip_ref, own_ref, landed_ref, o_ref):
        del chip_ref
        acc = own_ref[...].astype(F32)
        for s in range(landed_ref.shape[0]):
            acc = acc + landed_ref[s].astype(F32)
        o_ref[...] = acc

    grid_spec = pltpu.PrefetchScalarGridSpec(
        num_scalar_prefetch=1, grid=(r // tr,),
        in_specs=[pl.BlockSpec((None, tr, c), lambda i, chip_ref: (chip_ref[0], i, 0)),
                  pl.BlockSpec((landed.shape[0], tr, c), lambda i, chip_ref: (0, i, 0))],
        out_specs=pl.BlockSpec((tr, c), lambda i, chip_ref: (i, 0)))
    return pl.pallas_call(body, grid_spec=grid_spec, out_shape=SDS((r, c), F32),
                          compiler_params=_cparams(("parallel",)), name=name)(chip, parts, landed)


def _adamw(w, g, m, v, *, name):
    layers, r, c = w.shape
    if r <= 256 or r % 128 == 0:
        tr = min(r, 256)
        steps, spec = r // tr, pl.BlockSpec((None, tr, c), lambda l, i: (l, i, 0))
    else:
        tc = _pick(c, (256, 128))
        steps, spec = c // tc, pl.BlockSpec((None, r, tc), lambda l, i: (l, 0, i))

    def body(w_ref, g_ref, m_ref, v_ref, d_ref, mo_ref, vo_ref):
        g = g_ref[...]
        m_new = ADAM_B1 * m_ref[...] + (1.0 - ADAM_B1) * g
        v_new = ADAM_B2 * v_ref[...] + (1.0 - ADAM_B2) * (g * g)
        m_hat = m_new / (1.0 - ADAM_B1 ** ADAM_STEP)
        v_hat = v_new / (1.0 - ADAM_B2 ** ADAM_STEP)
        d_ref[...] = -ADAM_LR * (m_hat / (jnp.sqrt(v_hat) + ADAM_EPS) + ADAM_WD * w_ref[...])
        mo_ref[...] = m_new
        vo_ref[...] = v_new

    return pl.pallas_call(body, grid=(layers, steps), in_specs=[spec] * 4, out_specs=[spec] * 3,
                          out_shape=[SDS(w.shape, F32)] * 3, compiler_params=_cparams(("parallel", "parallel")),
                          name=name)(w, g, m, v)


ANY = pl.BlockSpec(memory_space=pl.ANY)


def _place():
    x, y, c = lax.axis_index("x"), lax.axis_index("y"), lax.axis_index("c")
    chips = [(1 - x, y), (x, 1 - y), (1 - x, 1 - y)]
    return x, y, c, chips


def _remote(src, dst, send_sem, recv_sem, to):
    return pltpu.make_async_remote_copy(src_ref=src, dst_ref=dst, send_sem=send_sem, recv_sem=recv_sem,
                                        device_id=to, device_id_type=MESH)


STREAM_ROWS = 256


def _stream_rows(i):
    return pl.ds(pl.multiple_of(i * STREAM_ROWS, STREAM_ROWS), STREAM_ROWS)


def _channel_scratch(width, dtype, rows=STREAM_ROWS):
    buf = (2, rows, width)
    return [pltpu.VMEM(buf, dtype), pltpu.VMEM(buf, dtype), *([pltpu.SemaphoreType.DMA((2,))] * 5),
            pltpu.SemaphoreType.REGULAR((2,))]


CHANNEL_REFS = 8


def _copy_blocks(srcs, dsts, ch):
    sbuf, _, ld, _, _, st, _, _ = ch
    n = len(srcs)
    load = lambda i: pltpu.make_async_copy(srcs[i], sbuf.at[i % 2], ld.at[i % 2])
    store = lambda i: pltpu.make_async_copy(sbuf.at[i % 2], dsts[i], st.at[i % 2])
    load(0).start()
    for i in range(n):
        if i + 1 < n:
            if i >= 1:
                store(i - 1).wait()
            load(i + 1).start()
        load(i).wait()
        store(i).start()
    for i in range(max(0, n - 2), n):
        store(i).wait()


def _exchange_block_streams(streams, sibling):
    plans = []
    for srcs, dsts, keeps, (sbuf, rbuf, ld, snd, rcv, st, kp, credit) in streams:
        n = len(srcs)

        def load(i, srcs=srcs, sbuf=sbuf, ld=ld):
            return pltpu.make_async_copy(srcs[i], sbuf.at[i % 2], ld.at[i % 2])

        def push(i, sbuf=sbuf, rbuf=rbuf, snd=snd, rcv=rcv):
            return _remote(sbuf.at[i % 2], rbuf.at[i % 2], snd.at[i % 2], rcv.at[i % 2], sibling)

        def store(i, rbuf=rbuf, dsts=dsts, st=st):
            return pltpu.make_async_copy(rbuf.at[i % 2], dsts[i], st.at[i % 2])

        def save(i, sbuf=sbuf, keeps=keeps, kp=kp):
            return pltpu.make_async_copy(sbuf.at[i % 2], keeps[i], kp.at[i % 2])

        def free_slot(i, n=n, store=store, credit=credit):
            if 1 <= i < n:
                store(i - 1).wait()
                if i + 1 < n:
                    pl.semaphore_signal(credit.at[(i + 1) % 2], 1, device_id=sibling, device_id_type=MESH)

        def send(i, n=n, load=load, push=push, save=save, keeps=keeps, credit=credit):
            if i < n:
                load(i).wait()
                pl.semaphore_wait(credit.at[i % 2], 1)
                push(i).start()
                if keeps[i] is not None:
                    save(i).start()

        def receive(i, n=n, load=load, push=push, store=store, save=save, keeps=keeps):
            if i < n:
                push(i).wait_recv()
                store(i).start()
                push(i).wait_send()
                if keeps[i] is not None:
                    save(i).wait()
                if i + 2 < n:
                    load(i + 2).start()

        for i in range(min(2, n)):
            pl.semaphore_signal(credit.at[i], 1, device_id=sibling, device_id_type=MESH)
            load(i).start()
        plans.append((n, free_slot, send, receive, store))
    for _, _, send, _, _ in plans:
        send(0)
    for i in range(max(p[0] for p in plans)):
        for _, free_slot, _, _, _ in plans:
            free_slot(i)
        for _, _, send, _, _ in plans:
            send(i + 1)
        for _, _, _, receive, _ in plans:
            receive(i)
    for n, _, _, _, store in plans:
        store(n - 1).wait()


def _all_gather_shards(shards, small, *, name):
    n = len(shards)

    def body(*refs):
        ins, outs = refs[:n + 1], refs[n + 1:2 * n + 2]
        scr = refs[2 * n + 2:]
        chans = [scr[CHANNEL_REFS * t:CHANNEL_REFS * (t + 1)] for t in range(n)]
        send_sems, recv_sems, small_sems = scr[CHANNEL_REFS * n:]
        x, y, c, _ = _place()
        me = 2 * x + y
        sibling = (x, y, 1 - c)
        near = (lax.rem(x + 1 - c, 2), lax.rem(y + c, 2))
        far = (lax.rem(x + c, 2), lax.rem(y + 1 - c, 2))
        k_near, k_far, k_diag = 2 * near[0] + near[1], 2 * far[0] + far[1], 3 - me
        targets = ((*near, c), (*far, c), (*far, c))
        arrives = (k_near, k_far, k_diag)
        streams_in = (k_far, k_near, k_diag)

        def ici(t, j, src, blk):
            return _remote(src, outs[t].at[blk, c], send_sems.at[3 * t + j], recv_sems.at[3 * t + j], targets[j])

        first = [ici(t, j, ins[t].at[c], me) for t in range(n + 1) for j in range(2)]
        for cp in first:
            cp.start()
        small_local = pltpu.make_async_copy(ins[n], outs[n].at[me], small_sems.at[6])
        small_local.start()
        for t in range(n):
            _copy_blocks([ins[t].at[h] for h in range(2)], [outs[t].at[me, h] for h in range(2)], chans[t])
        passed = []
        for j in range(3):
            for t in range(n + 1):
                landed = outs[t].at[arrives[j], c]
                ici(t, j, landed, arrives[j]).wait_recv()
                if j == 0:
                    fwd = ici(t, 2, landed, k_near)
                    fwd.start()
                    passed.append(fwd)
                if t < n:
                    _exchange_block_streams([([landed], [outs[t].at[streams_in[j], 1 - c]], [None], chans[t])], sibling)
                else:
                    fwd = _remote(landed, landed, small_sems.at[j], small_sems.at[3 + j], sibling)
                    fwd.start()
                    passed.append(fwd)
        for j in range(3):
            got = outs[n].at[streams_in[j], 1 - c]
            _remote(got, got, small_sems.at[j], small_sems.at[3 + j], sibling).wait_recv()
        for cp in first + passed:
            cp.wait_send()
        small_local.wait()

    scratch = []
    for s in shards:
        scratch += _channel_scratch(s.shape[2], s.dtype, rows=s.shape[1])
    return pl.pallas_call(
        body, in_specs=[ANY] * (n + 1), out_specs=[ANY] * (n + 1),
        out_shape=[SDS((N_CHIPS, *s.shape), s.dtype) for s in (*shards, small)],
        scratch_shapes=[*scratch, pltpu.SemaphoreType.DMA((3 * n + 3,)), pltpu.SemaphoreType.DMA((3 * n + 3,)),
                        pltpu.SemaphoreType.DMA((7,))],
        compiler_params=pltpu.CompilerParams(vmem_limit_bytes=VMEM_LIMIT), name=name)(*shards, small)


def _pair_reduce(stacks, *, name):
    n = len(stacks)
    per = 11

    def body(*refs):
        ins, outs, scr = refs[:n], refs[n:2 * n], refs[2 * n:]
        x, y, c, _ = _place()
        sibling = (x, y, 1 - c)
        streams = []
        for t in range(n):
            sraw, sbuf, rbuf, obuf, pbuf, ld_s, ld_o, snd, rcv, st, credit = scr[per * t:per * (t + 1)]
            steps = ins[t].shape[1] // STREAM_ROWS
            src, own, out = ins[t].at[1 - c], ins[t].at[c], outs[t]
            assert steps >= 2

            def load_s(i, slot, src=src, sraw=sraw, ld_s=ld_s):
                return pltpu.make_async_copy(src.at[_stream_rows(i)], sraw.at[slot], ld_s.at[slot])

            def load_o(i, slot, own=own, obuf=obuf, ld_o=ld_o):
                return pltpu.make_async_copy(own.at[_stream_rows(i)], obuf.at[slot], ld_o.at[slot])

            def push(slot, sbuf=sbuf, rbuf=rbuf, snd=snd, rcv=rcv):
                return _remote(sbuf.at[slot], rbuf.at[slot], snd.at[slot], rcv.at[slot], sibling)

            def store(i, slot, pbuf=pbuf, out=out, st=st):
                return pltpu.make_async_copy(pbuf.at[slot], out.at[_stream_rows(i)], st.at[slot])

            def send(i, slot, load_s=load_s, push=push, sraw=sraw, sbuf=sbuf, credit=credit):
                load_s(i, slot).wait()
                sbuf[slot] = sraw[slot].astype(sbuf.dtype)
                pl.semaphore_wait(credit.at[slot], 1)
                push(slot).start()

            def combine(i, slot, load_s=load_s, load_o=load_o, push=push, store=store, rbuf=rbuf, obuf=obuf, pbuf=pbuf,
                        credit=credit, steps=steps):
                load_o(i, slot).wait()
                push(slot).wait_recv()

                @pl.when(i >= 2)
                def _():
                    store(i, slot).wait()

                pbuf[slot] = (obuf[slot] + rbuf[slot].astype(F32)).astype(pbuf.dtype)
                store(i, slot).start()
                push(slot).wait_send()

                @pl.when(i + 2 < steps)
                def _():
                    load_s(i + 2, slot).start()
                    load_o(i + 2, slot).start()
                    pl.semaphore_signal(credit.at[slot], 1, device_id=sibling, device_id_type=MESH)

            for slot in range(2):
                pl.semaphore_signal(credit.at[slot], 1, device_id=sibling, device_id_type=MESH)
                load_s(slot, slot).start()
                load_o(slot, slot).start()
            streams.append((steps, send, combine, store))
        for _, send, _, _ in streams:
            send(0, 0)

        def step(i, carry):
            slot = lax.rem(i, 2)
            for steps, send, _, _ in streams:
                @pl.when(i + 1 < steps)
                def _(send=send):
                    send(i + 1, 1 - slot)
            for steps, _, combine, _ in streams:
                @pl.when(i < steps)
                def _(combine=combine):
                    combine(i, slot)
            return carry

        lax.fori_loop(0, max(s[0] for s in streams), step, 0)
        for _, _, _, store in streams:
            for slot in range(2):
                store(0, slot).wait()

    scratch = []
    for s in stacks:
        buf = (2, STREAM_ROWS, s.shape[2])
        scratch += [pltpu.VMEM(buf, F32), pltpu.VMEM(buf, BF16), pltpu.VMEM(buf, BF16), pltpu.VMEM(buf, F32),
                    pltpu.VMEM(buf, BF16), *([pltpu.SemaphoreType.DMA((2,))] * 5), pltpu.SemaphoreType.REGULAR((2,))]
    return pl.pallas_call(
        body, in_specs=[ANY] * n, out_specs=[ANY] * n, out_shape=[SDS(s.shape[1:], BF16) for s in stacks],
        scratch_shapes=scratch, compiler_params=pltpu.CompilerParams(vmem_limit_bytes=VMEM_LIMIT), name=name)(*stacks)


HBM_SPEC = pl.BlockSpec(memory_space=pltpu.HBM)
SEM_SPEC = pl.BlockSpec(memory_space=pltpu.SEMAPHORE)
SIDE_EFFECT = pltpu.SideEffectType.DATAFLOW_SIDE_EFFECTING


def _scatter_copies(ins, lands, send_sems, recv_sems):
    _, _, c, chips = _place()
    return [_remote(ins[t].at[2 * cx + cy], lands[t].at[j], send_sems.at[3 * t + j], recv_sems.at[3 * t + j],
                    (cx, cy, c)) for t in range(len(ins)) for j, (cx, cy) in enumerate(chips)]


def _chip_scatter_start(parts, *, name):
    n = len(parts)

    def body(*refs):
        ins, lands = refs[:n], refs[n:2 * n]
        send_sems, recv_sems, token = refs[2 * n], refs[2 * n + 1], refs[-1]
        for cp in _scatter_copies(ins, lands, send_sems, recv_sems):
            cp.start()
        token[...] = jnp.zeros_like(token)

    hbm = lambda a: pltpu.with_memory_space_constraint(a, pltpu.HBM)
    lands = [hbm(lax.empty((3, *p.shape[1:]), p.dtype)) for p in parts]
    thru = [pltpu.HBM(a.shape, a.dtype) for a in (*parts, *lands)]
    outs = pl.pallas_call(
        body, name=name,
        out_shape=(pltpu.SemaphoreType.DMA((3 * n,)), pltpu.SemaphoreType.DMA((3 * n,)), *thru, SDS((8, 128), F32)),
        in_specs=[HBM_SPEC] * (2 * n),
        out_specs=(SEM_SPEC, SEM_SPEC, *([HBM_SPEC] * (2 * n)), pl.BlockSpec(memory_space=pltpu.VMEM)),
        input_output_aliases={i: 2 + i for i in range(2 * n)},
        compiler_params=pltpu.CompilerParams(has_side_effects=SIDE_EFFECT),
    )(*[hbm(p) for p in parts], *lands)
    return outs[0], outs[1], outs[2:2 + n], outs[2 + n:2 + 2 * n], outs[-1]


def _chip_scatter_wait(send_sems, recv_sems, parts, lands, after, *, name):
    n = len(parts)

    def body(*refs):
        ins, lands_in = refs[:n], refs[n:2 * n]
        for cp in _scatter_copies(ins, lands_in, refs[2 * n], refs[2 * n + 1]):
            cp.wait_send()
            cp.wait_recv()

    outs = pl.pallas_call(
        body, name=name, out_shape=[pltpu.HBM(a.shape, a.dtype) for a in (*parts, *lands)],
        in_specs=[*([HBM_SPEC] * (2 * n)), SEM_SPEC, SEM_SPEC, *([ANY] * len(after))],
        out_specs=[HBM_SPEC] * (2 * n), input_output_aliases={i: i for i in range(2 * n)},
        compiler_params=pltpu.CompilerParams(has_side_effects=SIDE_EFFECT),
    )(*parts, *lands, send_sems, recv_sems, *after)
    return outs[:n], outs[n:]


def _gather_copies(shards, zones, send_sems, recv_sems):
    x, y, c, chips = _place()
    return [_remote(shards[t].at[c], zones[t].at[2 * x + y, c], send_sems.at[3 * t + j], recv_sems.at[3 * t + j],
                    (cx, cy, c)) for t in range(len(shards)) for j, (cx, cy) in enumerate(chips)]


def _gather_start(shards, after, *, name):
    n = len(shards)

    def body(*refs):
        ins, zones = refs[:n], refs[n:2 * n]
        send_sems, recv_sems, token = refs[2 * n + len(after)], refs[2 * n + len(after) + 1], refs[-1]
        for cp in _gather_copies(ins, zones, send_sems, recv_sems):
            cp.start()
        token[...] = jnp.zeros_like(token)

    hbm = lambda a: pltpu.with_memory_space_constraint(a, pltpu.HBM)
    zones = [hbm(lax.empty((N_CHIPS, *s.shape), s.dtype)) for s in shards]
    thru = [pltpu.HBM(a.shape, a.dtype) for a in (*shards, *zones)]
    outs = pl.pallas_call(
        body, name=name,
        out_shape=(pltpu.SemaphoreType.DMA((3 * n,)), pltpu.SemaphoreType.DMA((3 * n,)), *thru, SDS((8, 128), F32)),
        in_specs=[*([HBM_SPEC] * (2 * n)), *([ANY] * len(after))],
        out_specs=(SEM_SPEC, SEM_SPEC, *([HBM_SPEC] * (2 * n)), pl.BlockSpec(memory_space=pltpu.VMEM)),
        input_output_aliases={i: 2 + i for i in range(2 * n)},
        compiler_params=pltpu.CompilerParams(has_side_effects=SIDE_EFFECT),
    )(*[hbm(s) for s in shards], *zones, *after)
    return outs[0], outs[1], outs[2:2 + n], outs[2 + n:2 + 2 * n], outs[-1]


def _gather_wait(send_sems, recv_sems, shards, zones, after, *, name):
    n = len(shards)

    def body(*refs):
        for cp in _gather_copies(refs[:n], refs[n:2 * n], refs[2 * n], refs[2 * n + 1]):
            cp.wait_send()
            cp.wait_recv()

    outs = pl.pallas_call(
        body, name=name, out_shape=[pltpu.HBM(a.shape, a.dtype) for a in (*shards, *zones)],
        in_specs=[*([HBM_SPEC] * (2 * n)), SEM_SPEC, SEM_SPEC, *([ANY] * len(after))],
        out_specs=[HBM_SPEC] * (2 * n), input_output_aliases={i: i for i in range(2 * n)},
        compiler_params=pltpu.CompilerParams(has_side_effects=SIDE_EFFECT),
    )(*shards, *zones, send_sems, recv_sems, *after)
    return outs[:n], outs[n:]


def _gather_finish(shards, zones, *, name):
    n = len(shards)

    def body(*refs):
        ins, zones_in, outs, scr = refs[:n], refs[n:2 * n], refs[2 * n:3 * n], refs[3 * n:]
        x, y, c, chips = _place()
        me = 2 * x + y
        sibling = (x, y, 1 - c)
        others = [2 * cx + cy for cx, cy in chips]
        chans = [scr[CHANNEL_REFS * t:CHANNEL_REFS * (t + 1)] for t in range(n)]
        for t in range(n):
            _copy_blocks([ins[t].at[h] for h in range(2)], [outs[t].at[me, h] for h in range(2)], chans[t])
        _exchange_block_streams([([zones_in[t].at[k, c] for k in others], [outs[t].at[k, 1 - c] for k in others],
                                  [None] * len(others), chans[t]) for t in range(n)], sibling)

    scratch = []
    for s in shards:
        scratch += _channel_scratch(s.shape[2], s.dtype, rows=s.shape[1])
    return pl.pallas_call(
        body, in_specs=[ANY] * (2 * n), out_specs=[ANY] * n, out_shape=[SDS(z.shape, z.dtype) for z in zones],
        input_output_aliases={n + t: t for t in range(n)}, scratch_shapes=scratch,
        compiler_params=pltpu.CompilerParams(vmem_limit_bytes=VMEM_LIMIT), name=name)(*shards, *zones)


def _pair_share(groups, *, name):
    finals = [f for grp in groups for f in grp]
    n, n_out = len(finals), len(groups)

    def body(*refs):
        ins, outs, scr = refs[:n], refs[n:n + n_out], refs[n + n_out:]
        x, y, c, _ = _place()
        sibling = (x, y, 1 - c)
        t, streams = 0, []
        for o, grp in enumerate(groups):
            rows = grp[0].shape[0] // 2
            blocks = [(layer, pl.ds(b * rows, rows)) for layer in range(len(grp)) for b in range(2)]
            streams.append(([ins[t + layer].at[rs] for layer, rs in blocks],
                            [outs[o].at[layer, 1 - c, rs] for layer, rs in blocks],
                            [outs[o].at[layer, c, rs] for layer, rs in blocks],
                            scr[CHANNEL_REFS * o:CHANNEL_REFS * (o + 1)]))
            t += len(grp)
        _exchange_block_streams(streams, sibling)

    scratch = []
    for grp in groups:
        scratch += _channel_scratch(grp[0].shape[1], grp[0].dtype, rows=grp[0].shape[0] // 2)
    return pl.pallas_call(
        body, in_specs=[ANY] * n, out_specs=[ANY] * n_out,
        out_shape=[SDS((len(grp), 2, *grp[0].shape), grp[0].dtype) for grp in groups],
        scratch_shapes=scratch, compiler_params=pltpu.CompilerParams(vmem_limit_bytes=VMEM_LIMIT), name=name)(*finals)


def _all_reduce_small(v, *, name):
    rows, lanes = v.shape
    n_dev = 8

    def body(v_ref, o_ref, all_ref, send_sems, recv_sems, local_sem):
        x, y, c, chips = _place()
        me, sibling = (x, y, c), (x, y, 1 - c)

        def block(px, py, pc):
            return all_ref.at[4 * px + 2 * py + pc]

        def copy(k, blk, to, src=None):
            return _remote(block(*blk) if src is None else src, block(*blk), send_sems.at[k], recv_sems.at[k], to)

        mine = pltpu.make_async_copy(v_ref, block(*me), local_sem)
        mine.start()
        first = [copy(0, me, sibling, src=v_ref)]
        first += [copy(1 + j, me, (*chip, c), src=v_ref) for j, chip in enumerate(chips)]
        for cp in first:
            cp.start()
        passed = [copy(4 + j, (*chip, c), sibling) for j, chip in enumerate(chips)]
        for j, chip in enumerate(chips):
            copy(1 + j, (*chip, c), me).wait_recv()
            passed[j].start()
        copy(0, sibling, me).wait_recv()
        for j, chip in enumerate(chips):
            copy(4 + j, (*chip, 1 - c), me).wait_recv()
        for cp in first + passed:
            cp.wait_send()
        mine.wait()
        acc = all_ref[0]
        for k in range(1, n_dev):
            acc = acc + all_ref[k]
        o_ref[...] = acc

    vmem = pl.BlockSpec(memory_space=pltpu.VMEM)
    return pl.pallas_call(
        body, in_specs=[vmem], out_specs=vmem, out_shape=SDS((rows, lanes), F32),
        scratch_shapes=[pltpu.VMEM((n_dev, rows, lanes), F32), pltpu.SemaphoreType.DMA((7,)),
                        pltpu.SemaphoreType.DMA((7,)), pltpu.SemaphoreType.DMA],
        compiler_params=pltpu.CompilerParams(vmem_limit_bytes=VMEM_LIMIT), name=name)(v)


def _relu2_epilogue(acc):
    return acc, jnp.square(jnp.maximum(acc, 0.0))


def _res_epilogue(acc, res):
    return (acc + res,)


def _drelu2_epilogue(acc, pre):
    return (acc * (2.0 * jnp.maximum(pre.astype(F32), 0.0)),)


def _ffn_fwd(h, g, w1, w2, tag):
    f = _rms_fwd(h, g, name=f"ffn_norm_{tag}")
    pre, act = _mm_nn(f, w1, name=f"ffn1_{tag}", epilogue=_relu2_epilogue, n_out_dtypes=(BF16, BF16))
    h_out = _mm_nn(act, w2, name=f"ffn2_{tag}", extras=(h,), epilogue=_res_epilogue)
    return h_out, (f, pre, act)


def _ffn_bwd(dh, h, g, w1, w2, saved, layer, after=()):
    f, pre, act = saved
    dpre = _mm_nt(dh, w2, name=f"ffn2_dx_{layer}", out_dtype=BF16, extras=(pre,), epilogue=_drelu2_epilogue,
                  after=after)
    dw2 = _mm_tn_stacked(act, dh, name=f"ffn2_dw_{layer}", col_slots=False)
    df = _mm_nt(dpre, w1, name=f"ffn1_dx_{layer}")
    dw1 = _mm_tn_stacked(f, dpre, name=f"ffn1_dw_{layer}", col_slots=True)
    dh, dg = _rms_bwd(h, g, df, dh, name=f"ffn_norm_bwd_{layer}")
    return dh, dg, dw1, dw2


def _kv_fwd(mem, g, w_kv, tag):
    m = _rms_fwd(mem, g, name=f"mem_norm_{tag}")
    return m, _mm_nn(m, w_kv, name=f"kv_{tag}")


def _kv_bwd(mem, g, w_kv, m, dk, dv, layer):
    dkv = jnp.concatenate([dk, dv], axis=1)
    dw = _mm_tn_stacked(m, dkv, name=f"kv_dw_{layer}", col_slots=True)
    dm = _mm_nt(dkv, w_kv, name=f"kv_dx_{layer}")
    _, dg = _rms_bwd(mem, g, dm, dm, name=f"mem_norm_bwd_{layer}")
    return dw, dg


def _local_step(x, mem, target, p, after_layer1=None, after_ffn0=None, after_mixer0=None):
    row = lambda v: v.reshape(1, -1)
    g = {}

    h0 = x
    a0 = _rms_fwd(h0, row(p["norm_mix"][0]), name="mix_norm_0")
    proj_a = _mm_nn(a0, p["a_in"], name="a_in", after=p.get("after_start", ()))
    m0, kv0 = _kv_fwd(mem, row(p["mem_norm"][0]), p["w_kv"][0], "0")
    cat0 = _attn_fwd(proj_a, 2 * D_INNER, kv0, name="attn_0")
    bs_col = p["a_bs"].reshape(A_GROUPS, CHUNK, 1)
    cat0 = _gate_fwd(proj_a, p["a_ln_g"], p["a_ln_b"], p["a_ws"], bs_col, cat0, name="gate")
    h1 = _mm_nn(cat0, p["w_out"][0], name="out_0", extras=(h0,), epilogue=_res_epilogue)
    w_ffn1_0, w_ffn2_0 = p["layer0_ffn"](h1) if "layer0_ffn" in p else (p["w_ffn1"][0], p["w_ffn2"][0])
    h2, ffn0 = _ffn_fwd(h1, row(p["norm_ffn"][0]), w_ffn1_0, w_ffn2_0, "0")

    w_kv1, b_in = p["layer1_mixer"](h2) if "layer1_mixer" in p else (p["w_kv"][1], p["b_in"])
    a1 = _rms_fwd(h2, row(p["norm_mix"][1]), name="mix_norm_1")
    proj_b = _mm_nn(a1, b_in, name="b_in")
    m1, kv1 = _kv_fwd(mem, row(p["mem_norm"][1]), w_kv1, "1")
    cat1 = _attn_fwd(proj_b, B_Q_OFF, kv1, name="attn_1")
    xbc = _conv_fwd(proj_b, p["b_conv_w"], p["b_conv_b"], name="conv")
    dt_raw = proj_b[:, B_DT_OFF:B_DT_OFF + SSM_HEADS].reshape(SEQ, SSM_GROUPS, SSM_HPG)
    dt_c = jnp.transpose(dt_raw, (1, 0, 2))
    dt_r = jnp.transpose(dt_raw, (1, 2, 0))
    per_head = lambda v: v.reshape(SSM_GROUPS, 1, SSM_HPG)
    par_row = jnp.concatenate([per_head(p["b_dt_bias"]), per_head(p["b_a_log"]), per_head(p["b_d"])], axis=1)
    ssd_par = (par_row, jnp.transpose(par_row[:, :2], (0, 2, 1)), p["b_gnorm"])
    cat1, hprev = _ssd_fwd(xbc, proj_b, dt_c, dt_r, *ssd_par, cat1, name="ssd")
    if "layer1_rest" in p:
        w_out1, w_ffn1_1, w_ffn2_1 = p["layer1_rest"](cat1)
    else:
        w_out1, w_ffn1_1, w_ffn2_1 = p["w_out"][1], p["w_ffn1"][1], p["w_ffn2"][1]
    h3 = _mm_nn(cat1, w_out1, name="out_1", extras=(h2,), epilogue=_res_epilogue)
    h4, ffn1 = _ffn_fwd(h3, row(p["norm_ffn"][1]), w_ffn1_1, w_ffn2_1, "1")

    loss, dh, g["final_norm"] = _loss_head(h4, row(p["final_norm"]), target, name="loss_head")

    dh, dnf1, dw1_1, dw2_1 = _ffn_bwd(dh, h3, row(p["norm_ffn"][1]), w_ffn1_1, w_ffn2_1, ffn1, 1)
    dcat1 = _mm_nt(dh, w_out1, name="out_dx_1")
    dwo_1 = _mm_tn_stacked(cat1, dh, name="out_dw_1", col_slots=False)
    dproj_b, dk1, dv1 = _attn_bwd(proj_b, B_Q_OFF, kv1, dcat1, B_IN_PAD, B_Q_OFF, name="attn_bwd_1")
    dproj_b, dxs, dbm, dcm, ddt_c, ddt_r, dpar_row, dpar_col, g["b_gnorm"] = _ssd_bwd(
        xbc, proj_b, dt_c, dt_r, *ssd_par, hprev, dcat1, dproj_b, name="ssd_bwd")
    dpar = dpar_row.at[:, :2].add(jnp.transpose(dpar_col, (0, 2, 1)))
    g["b_dt_bias"], g["b_a_log"], g["b_d"] = dpar[:, 0], dpar[:, 1], dpar[:, 2]
    dproj_b, g["b_conv_w"], g["b_conv_b"] = _conv_bwd(proj_b, p["b_conv_w"], p["b_conv_b"], dxs, dbm, dcm, dproj_b,
                                                      name="conv_bwd")
    ddt = jnp.transpose(ddt_c, (1, 0, 2)) + jnp.transpose(ddt_r, (2, 0, 1))
    ddt = jnp.pad(ddt.reshape(SEQ, SSM_HEADS), ((0, 0), (0, B_IN_PAD - B_DT_OFF - SSM_HEADS))).astype(BF16)
    dproj_b = lax.dynamic_update_slice(dproj_b, ddt, (0, B_DT_OFF))
    dwkv_1, dmn1 = _kv_bwd(mem, row(p["mem_norm"][1]), w_kv1, m1, dk1, dv1, 1)
    dwb = _b_in_grad_slots(_mm_tn(a1, dproj_b, name="b_in_dw"))
    da1 = _mm_nt(dproj_b, b_in, name="b_in_dx")
    dh, dnm1 = _rms_bwd(h2, row(p["norm_mix"][1]), da1, dh, name="mix_norm_bwd_1")
    layer1 = dict(w_kv=dwkv_1, w_out=dwo_1, w_ffn1=dw1_1, w_ffn2=dw2_1, b_in=dwb)
    token = () if after_layer1 is None else (after_layer1(layer1),)

    dh, dnf0, dw1_0, dw2_0 = _ffn_bwd(dh, h1, row(p["norm_ffn"][0]), w_ffn1_0, w_ffn2_0, ffn0, 0,
                                      after=token)
    ffn0_grads = dict(w_ffn1=dw1_0, w_ffn2=dw2_0)
    token = () if after_ffn0 is None else (after_ffn0(ffn0_grads),)
    dcat0 = _mm_nt(dh, p["w_out"][0], name="out_dx_0", after=token)
    dwo_0 = _mm_tn_stacked(cat0, dh, name="out_dw_0", col_slots=False)
    dproj_a, dk0, dv0 = _attn_bwd(proj_a, 2 * D_INNER, kv0, dcat0, A_IN, 2 * D_INNER, name="attn_bwd_0")
    dproj_a, g["a_ln_g"], g["a_ln_b"], g["a_ws"], dbs_col = _gate_bwd(
        proj_a, p["a_ln_g"], p["a_ln_b"], p["a_ws"], bs_col, dcat0, dproj_a, name="gate_bwd")
    g["a_bs"] = dbs_col.reshape(A_GROUPS, CHUNK)
    dwkv_0, dmn0 = _kv_bwd(mem, row(p["mem_norm"][0]), p["w_kv"][0], m0, dk0, dv0, 0)
    dwa = _mm_tn_stacked(a0, dproj_a, name="a_in_dw", col_slots=True)
    mixer0_grads = dict(w_kv=dwkv_0, w_out=dwo_0, a_in=dwa)
    token = () if after_mixer0 is None else (after_mixer0(mixer0_grads),)
    da0 = _mm_nt(dproj_a, p["a_in"], name="a_in_dx", after=token)
    dx, dnm0 = _rms_bwd(h0, row(p["norm_mix"][0]), da0, dh, name="mix_norm_bwd_0")

    g["norm_mix"] = jnp.concatenate([dnm0, dnm1], axis=0)
    g["norm_ffn"] = jnp.concatenate([dnf0, dnf1], axis=0)
    g["mem_norm"] = jnp.concatenate([dmn0, dmn1], axis=0)
    layer0 = dict(w_kv=dwkv_0, w_out=dwo_0, w_ffn1=dw1_0, w_ffn2=dw2_0, a_in=dwa)
    return loss, dx, g, layer0, layer1


def _b_in_full(gathered):
    n = B_IN // N_CHIPS
    dt0 = D_INNER + CONV_DIM - (N_CHIPS - 1) * n
    last = gathered[N_CHIPS - 1]
    return jnp.concatenate([*[gathered[k] for k in range(N_CHIPS - 1)], last[:, :dt0], last[:, dt0 + SSM_HEADS:],
                            last[:, dt0:dt0 + SSM_HEADS], jnp.zeros((D_MODEL, B_IN_PAD - B_IN), last.dtype)], axis=1)


def _b_in_grad_slots(d):
    n = B_IN // N_CHIPS
    dt0 = D_INNER + CONV_DIM
    last = jnp.concatenate([d[:, (N_CHIPS - 1) * n:dt0], d[:, B_DT_OFF:B_DT_OFF + SSM_HEADS], d[:, dt0:B_DT_OFF]], axis=1)
    slots = [*[d[:, k * n:(k + 1) * n] for k in range(N_CHIPS - 1)], last]
    half = D_MODEL // 2
    return jnp.stack([jnp.stack([s[h * half:(h + 1) * half] for s in slots]) for h in range(2)])


SMALL_REPL = ("norm_mix", "norm_ffn", "mem_norm", "a_ln_g", "a_ln_b", "a_ws", "a_bs", "b_dt_bias", "b_a_log", "b_d",
              "final_norm")
SMALL_SHARD = ("b_conv_w", "b_conv_b", "b_gnorm")
WEIGHTS = ("norm_mix", "norm_ffn", "mem_norm", "w_kv", "w_out", "w_ffn1", "w_ffn2", "a_in", "a_ln_g", "a_ln_b", "a_ws",
           "a_bs", "b_in", "b_conv_w", "b_conv_b", "b_dt_bias", "b_a_log", "b_d", "b_gnorm", "final_norm")
CONV_SHARD = CONV_DIM // N_CHIPS
GN_SHARD = D_INNER // N_CHIPS


LAYERED = ("w_kv", "w_out", "w_ffn1", "w_ffn2")


def _gather_weights(w):
    halves = lambda k, layer: (w[k][layer] if k in LAYERED else w[k][0]).reshape(2, -1, w[k].shape[-1]).astype(BF16)
    small = jnp.zeros((2, CONV_K, CONV_SHARD), F32)
    small = small.at[0].set(w["b_conv_w"][0])
    small = small.at[1, 0].set(w["b_conv_b"][0])
    small = small.at[1, 1, :GN_SHARD].set(w["b_gnorm"][0])
    first_names = ("w_kv", "w_out", "a_in")
    gathered = _all_gather_shards([halves(k, 0) for k in first_names], small, name="gather_weights_0")
    got = dict(zip(first_names, gathered))
    slots = lambda a: a.reshape(N_CHIPS, -1, a.shape[-1])
    rows = lambda a: a.reshape(-1, a.shape[-1])
    p = dict(w_kv=[slots(got["w_kv"])], w_out=[rows(got["w_out"])], a_in=slots(got["a_in"]))
    sm = gathered[-1]
    p["b_conv_w"] = jnp.transpose(sm[:, 0], (1, 0, 2)).reshape(CONV_K, CONV_DIM)
    p["b_conv_b"] = sm[:, 1, 0].reshape(1, CONV_DIM)
    p["b_gnorm"] = sm[:, 1, 1, :GN_SHARD].reshape(1, D_INNER)

    after, started = (gathered[0],), {}
    for tag, layer, names in (("0_ffn", 0, ("w_ffn1", "w_ffn2")), ("1_mixer", 1, ("w_kv", "b_in")),
                              ("1_rest", 1, ("w_out", "w_ffn1", "w_ffn2"))):
        started[tag] = _gather_start([halves(k, layer) for k in names], after, name=f"gather_start_{tag}")
        after = (started[tag][-1],)
    p["after_start"] = after

    def finish(tag, first):
        send_sems, recv_sems, shards, zones, _ = started[tag]
        shards, zones = _gather_wait(send_sems, recv_sems, shards, zones, (first,), name=f"gather_wait_{tag}")
        return _gather_finish(shards, zones, name=f"gather_finish_{tag}")

    def layer0_ffn(first):
        w1, w2 = finish("0_ffn", first)
        return slots(w1), rows(w2)

    def layer1_mixer(first):
        kv, b_in = finish("1_mixer", first)
        return slots(kv), _b_in_full(slots(b_in))

    def layer1_rest(first):
        wo, w1, w2 = finish("1_rest", first)
        return rows(wo), slots(w1), rows(w2)

    p.update(layer0_ffn=layer0_ffn, layer1_mixer=layer1_mixer, layer1_rest=layer1_rest)
    return p


def _pair_parts(grads, tag):
    stacks = [g.reshape(2, -1, g.shape[-1]) for g in grads.values()]
    parts = _pair_reduce(stacks, name=f"grads_pair_reduce_{tag}")
    return [t.reshape(N_CHIPS, -1, t.shape[-1]) for t in parts]


def _chip_sums(chip, names, parts, landed, tag):
    return {k: _sum_contributions(chip, t, u, name=f"grads_chip_sum_{k}_{tag}")
            for k, t, u in zip(names, parts, landed)}


def _small_layout(shapes):
    offs, o = {}, 0
    for k in (*SMALL_REPL, *SMALL_SHARD):
        size = math.prod(shapes[k])
        offs[k] = (o, size)
        o += size
    rows = -(-(o + 1) // (8 * 128)) * 8
    return offs, rows


def _reduce_small(g, loss_part, full_shapes):
    offs, rows = _small_layout(full_shapes)
    flat = jnp.concatenate([*[g[k].reshape(-1) for k in (*SMALL_REPL, *SMALL_SHARD)], loss_part[0, :1]])
    flat = jnp.pad(flat, (0, rows * 128 - flat.shape[0])).reshape(rows, 128)
    total = _all_reduce_small(flat, name="small_all_reduce").reshape(-1)
    end = max(o + n for o, n in offs.values())
    return {k: total[o:o + n].reshape(full_shapes[k]) for k, (o, n) in offs.items()}, total[end]


def kernel(x, mem, norm_mix, norm_ffn, mem_norm, w_kv, w_out, w_ffn1, w_ffn2, a_in, a_ln_g, a_ln_b, a_ws, a_bs, b_in, b_conv_w, b_conv_b, b_dt_bias, b_a_log, b_d, b_gnorm, final_norm, loss_target, m_norm_mix, m_norm_ffn, m_mem_norm, m_w_kv, m_w_out, m_w_ffn1, m_w_ffn2, m_a_in, m_a_ln_g, m_a_ln_b, m_a_ws, m_a_bs, m_b_in, m_b_conv_w, m_b_conv_b, m_b_dt_bias, m_b_a_log, m_b_d, m_b_gnorm, m_final_norm, v_norm_mix, v_norm_ffn, v_mem_norm, v_w_kv, v_w_out, v_w_ffn1, v_w_ffn2, v_a_in, v_a_ln_g, v_a_ln_b, v_a_ws, v_a_bs, v_b_in, v_b_conv_w, v_b_conv_b, v_b_dt_bias, v_b_a_log, v_b_d, v_b_gnorm, v_final_norm):
    w = dict(norm_mix=norm_mix, norm_ffn=norm_ffn, mem_norm=mem_norm, w_kv=w_kv, w_out=w_out, w_ffn1=w_ffn1,
             w_ffn2=w_ffn2, a_in=a_in, a_ln_g=a_ln_g, a_ln_b=a_ln_b, a_ws=a_ws, a_bs=a_bs, b_in=b_in, b_conv_w=b_conv_w,
             b_conv_b=b_conv_b, b_dt_bias=b_dt_bias, b_a_log=b_a_log, b_d=b_d, b_gnorm=b_gnorm, final_norm=final_norm)
    mom = dict(norm_mix=m_norm_mix, norm_ffn=m_norm_ffn, mem_norm=m_mem_norm, w_kv=m_w_kv, w_out=m_w_out,
               w_ffn1=m_w_ffn1, w_ffn2=m_w_ffn2, a_in=m_a_in, a_ln_g=m_a_ln_g, a_ln_b=m_a_ln_b, a_ws=m_a_ws,
               a_bs=m_a_bs, b_in=m_b_in, b_conv_w=m_b_conv_w, b_conv_b=m_b_conv_b, b_dt_bias=m_b_dt_bias,
               b_a_log=m_b_a_log, b_d=m_b_d, b_gnorm=m_b_gnorm, final_norm=m_final_norm)
    var = dict(norm_mix=v_norm_mix, norm_ffn=v_norm_ffn, mem_norm=v_mem_norm, w_kv=v_w_kv, w_out=v_w_out,
               w_ffn1=v_w_ffn1, w_ffn2=v_w_ffn2, a_in=v_a_in, a_ln_g=v_a_ln_g, a_ln_b=v_a_ln_b, a_ws=v_a_ws,
               a_bs=v_a_bs, b_in=v_b_in, b_conv_w=v_b_conv_w, b_conv_b=v_b_conv_b, b_dt_bias=v_b_dt_bias,
               b_a_log=v_b_a_log, b_d=v_b_d, b_gnorm=v_b_gnorm, final_norm=v_final_norm)

    p = _gather_weights(w)
    p.update(norm_mix=norm_mix, norm_ffn=norm_ffn, mem_norm=mem_norm, a_ln_g=a_ln_g, a_ln_b=a_ln_b, a_ws=a_ws[0],
             a_bs=a_bs[0], b_dt_bias=b_dt_bias, b_a_log=b_a_log, b_d=b_d, final_norm=final_norm)
    chip = 2 * lax.axis_index("x") + lax.axis_index("y")
    chip_arr = jnp.reshape(chip, (1,)).astype(jnp.int32)
    started = {}

    def start_scatter(tag):
        def hook(grads):
            start = _chip_scatter_start(_pair_parts(grads, tag), name=f"grads_chip_scatter_start_{tag}")
            started[tag] = (tuple(grads), start)
            return start[-1]
        return hook

    loss_part, dx, g, _, _ = _local_step(x[0], mem[0], loss_target[0], p, start_scatter("1"), start_scatter("0f"),
                                         start_scatter("0m"))
    full_shapes = {k: w[k].shape for k in SMALL_REPL}
    full_shapes.update(b_conv_w=(1, CONV_K, CONV_DIM), b_conv_b=(1, CONV_DIM), b_gnorm=(1, D_INNER))
    grads, loss = _reduce_small(g, loss_part, full_shapes)
    grads["b_conv_w"] = lax.dynamic_slice_in_dim(grads["b_conv_w"], chip * CONV_SHARD, CONV_SHARD, axis=2)
    grads["b_conv_b"] = lax.dynamic_slice_in_dim(grads["b_conv_b"], chip * CONV_SHARD, CONV_SHARD, axis=1)
    grads["b_gnorm"] = lax.dynamic_slice_in_dim(grads["b_gnorm"], chip * GN_SHARD, GN_SHARD, axis=1)

    def finish_scatter(tag, *first):
        names, (send_sems, recv_sems, parts, lands, _) = started[tag]
        parts, landed = _chip_scatter_wait(send_sems, recv_sems, parts, lands, first,
                                           name=f"grads_chip_scatter_wait_{tag}")
        return _chip_sums(chip_arr, names, parts, landed, tag)

    def adamw(names, grads):
        for k in names:
            shape = w[k].shape
            if len(shape) == 3 and shape[2] % 128 and not shape[1] % 128:
                flat = unflat = lambda a: jnp.transpose(a, (0, 2, 1))
            else:
                flat = (lambda a: a) if len(shape) == 3 else (lambda a: a.reshape(1, -1, shape[-1]))
                unflat = lambda a: a.reshape(shape)
            d, m_new, v_new = _adamw(flat(w[k]), flat(grads[k]), flat(mom[k]), flat(var[k]), name=f"adamw_{k}")
            delta[k], new_m[k], new_v[k] = unflat(d), unflat(m_new), unflat(v_new)

    delta, new_m, new_v = {}, {}, {}
    halves = [finish_scatter("0f", dx), finish_scatter("1", dx)]
    early = ("w_ffn1", "w_ffn2", "b_in")
    shared = _pair_share([[halves[layer][k] for layer in range(2) if k in halves[layer]] for k in early],
                         name="grads_pair_share_early")
    grads.update({k: a.reshape(w[k].shape) for k, a in zip(early, shared)})
    adamw([k for k in WEIGHTS if k in grads], grads)
    halves[0].update(finish_scatter("0m", delta["w_ffn2"]))
    late = ("w_kv", "w_out", "a_in")
    shared = _pair_share([[halves[layer][k] for layer in range(2) if k in halves[layer]] for k in late],
                         name="grads_pair_share_late")
    grads.update({k: a.reshape(w[k].shape) for k, a in zip(late, shared)})
    adamw(late, grads)

    return (loss, dx.reshape(x.shape), *[grads[k] for k in WEIGHTS], *[delta[k] for k in WEIGHTS],
            *[new_m[k] for k in WEIGHTS], *[new_v[k] for k in WEIGHTS])
```

```python
import math

import jax
import jax.numpy as jnp
from jax import lax
from jax.experimental import pallas as pl
from jax.experimental.pallas import tpu as pltpu

F32 = jnp.float32
BF16 = jnp.bfloat16
SDS = jax.ShapeDtypeStruct

D_MODEL = 1024
SEQ = 2048
CHUNK = 128
N_MEM = 256
D_INNER = 2048
A_GROUPS = 8
A_GROUP_W = D_INNER // A_GROUPS
SSM_HEADS = 32
SSM_HEAD_DIM = 64
SSM_GROUPS = 4
SSM_HPG = 8
SSM_STATE = 128
SSM_GROUP_W = SSM_HPG * SSM_HEAD_DIM
CONV_K = 4
CONV_DIM = 3072
X_HEADS = 4
X_HEAD_DIM = 256
X_WIDTH = 1024
MIX_OUT = 3072
D_FF = 4096
A_IN = 5120
B_IN = 6176
B_IN_PAD = 6272
B_Q_OFF = 5120
B_DT_OFF = 6144
N_CHUNKS = SEQ // CHUNK
EPS = 1e-6
N_CHIPS = 4

ADAM_LR = 0.001
ADAM_B1 = 0.9
ADAM_B2 = 0.999
ADAM_EPS = 1e-08
ADAM_WD = 0.01
ADAM_STEP = 10

VMEM_LIMIT = 48 * 1024 * 1024
MESH = pl.DeviceIdType.MESH


def _cparams(sem):
    return pltpu.CompilerParams(dimension_semantics=sem, vmem_limit_bytes=VMEM_LIMIT)


def _dot(a, b, dims=(((1,), (0,)), ((), ()))):
    return lax.dot_general(a.astype(BF16), b.astype(BF16), dims, preferred_element_type=F32)


def _dot_nt(a, b):
    return _dot(a, b, (((1,), (1,)), ((), ())))


def _dot_tn(a, b):
    return _dot(a, b, (((0,), (0,)), ((), ())))


def _pick(n, cands):
    for c in cands:
        if n % c == 0:
            return c
    raise ValueError(f"no tile for {n}")


def _mm_call(a, b, *, dims, grid, a_spec, b_spec, acc_shape, out_shapes, out_specs, name,
             extras=(), extra_specs=(), epilogue=None, after=()):
    n_k = grid[2]
    n_extra = len(extras)
    n_out = len(out_shapes)
    n_in = 2 + n_extra + len(after)

    def finish(total, extra_refs, out_refs):
        vals = (total,) if epilogue is None else epilogue(total, *[e[...] for e in extra_refs])
        for o_ref, v in zip(out_refs, vals):
            o_ref[...] = v.astype(o_ref.dtype)

    def body_one_step(*refs):
        finish(_dot(refs[0][...], refs[1][...], dims), refs[2:2 + n_extra], refs[n_in:n_in + n_out])

    def body(*refs):
        acc = refs[-1]
        k = pl.program_id(2)

        @pl.when(k == 0)
        def _():
            acc[...] = jnp.zeros_like(acc)

        acc[...] += _dot(refs[0][...], refs[1][...], dims)

        @pl.when(k == n_k - 1)
        def _():
            finish(acc[...], refs[2:2 + n_extra], refs[n_in:n_in + n_out])

    return pl.pallas_call(
        body_one_step if n_k == 1 else body, grid=grid,
        in_specs=[a_spec, b_spec, *extra_specs, *([ANY] * len(after))], out_specs=list(out_specs),
        out_shape=list(out_shapes), scratch_shapes=[] if n_k == 1 else [pltpu.VMEM(acc_shape, F32)],
        compiler_params=_cparams(("parallel", "parallel", "arbitrary")), name=name,
    )(a, b, *extras, *after)


def _w_dims(w):
    if w.ndim == 2:
        return w.shape[0], w.shape[1], 1, w.shape[1]
    return w.shape[1], w.shape[0] * w.shape[2], w.shape[0], w.shape[2]


def _mm_nn(a, w, *, name, out_dtype=F32, a_cols=None, extras=(), epilogue=None, n_out_dtypes=None, after=()):
    m = a.shape[0]
    k_dim, n_dim, _, n_slot = _w_dims(w)
    a_off, a_w = (0, a.shape[1]) if a_cols is None else a_cols
    assert a_w == k_dim
    tm = _pick(m, (2048, 1024, 512, 256))
    tn = _pick(n_slot, (512, 896, 640, 256, 128))
    tk = _pick(k_dim, (1024, 768, 512, 384, 256, 128))
    assert a_off % tk == 0
    nb = n_slot // tn
    a_spec = pl.BlockSpec((tm, tk), lambda i, j, k: (i, a_off // tk + k))
    if w.ndim == 2:
        b_spec = pl.BlockSpec((tk, tn), lambda i, j, k: (k, j))
    else:
        b_spec = pl.BlockSpec((None, tk, tn), lambda i, j, k: (j // nb, k, j % nb))
    o_spec = pl.BlockSpec((tm, tn), lambda i, j, k: (i, j))
    dts = n_out_dtypes or (out_dtype,)
    outs = _mm_call(a, w, dims=(((1,), (0,)), ((), ())), grid=(m // tm, n_dim // tn, k_dim // tk),
                    a_spec=a_spec, b_spec=b_spec, acc_shape=(tm, tn),
                    out_shapes=[SDS((m, n_dim), dt) for dt in dts], out_specs=[o_spec] * len(dts), name=name,
                    extras=extras, extra_specs=[o_spec] * len(extras), epilogue=epilogue, after=after)
    return outs if n_out_dtypes else outs[0]


def _mm_nt(a, w, *, name, out_dtype=F32, extras=(), epilogue=None, after=()):
    m = a.shape[0]
    k_dim, n_dim, _, n_slot = _w_dims(w)
    assert a.shape[1] == n_dim
    tm = _pick(m, (2048, 1024, 512, 256))
    to = _pick(k_dim, (512, 384, 256, 128))
    tc = _pick(n_slot, (1280, 1024, 896, 640, 512, 256, 128))
    nb = n_slot // tc
    a_spec = pl.BlockSpec((tm, tc), lambda i, j, k: (i, k))
    if w.ndim == 2:
        b_spec = pl.BlockSpec((to, tc), lambda i, j, k: (j, k))
    else:
        b_spec = pl.BlockSpec((None, to, tc), lambda i, j, k: (k // nb, j, k % nb))
    o_spec = pl.BlockSpec((tm, to), lambda i, j, k: (i, j))
    return _mm_call(a, w, dims=(((1,), (1,)), ((), ())), grid=(m // tm, k_dim // to, n_dim // tc),
                    a_spec=a_spec, b_spec=b_spec, acc_shape=(tm, to),
                    out_shapes=[SDS((m, k_dim), out_dtype)], out_specs=[o_spec], name=name,
                    extras=extras, extra_specs=[o_spec] * len(extras), epilogue=epilogue, after=after)[0]


def _mm_tn(x, dy, *, name, x_cols=None):
    s = x.shape[0]
    x_off, k_dim = (0, x.shape[1]) if x_cols is None else x_cols
    n_dim = dy.shape[1]
    tm = _pick(k_dim, (1024, 768, 512, 384, 256, 128))
    tn = _pick(n_dim, (512, 896, 640, 256, 128))
    tk = _pick(s, (2048, 1024, 512, 256))
    assert x_off % tm == 0
    a_spec = pl.BlockSpec((tk, tm), lambda i, j, k: (k, x_off // tm + i))
    b_spec = pl.BlockSpec((tk, tn), lambda i, j, k: (k, j))
    o_spec = pl.BlockSpec((tm, tn), lambda i, j, k: (i, j))
    return _mm_call(x, dy, dims=(((0,), (0,)), ((), ())), grid=(k_dim // tm, n_dim // tn, s // tk),
                    a_spec=a_spec, b_spec=b_spec, acc_shape=(tm, tn),
                    out_shapes=[SDS((k_dim, n_dim), F32)], out_specs=[o_spec], name=name)[0]


def _mm_tn_stacked(x, dy, *, name, col_slots):
    s, k_dim = x.shape
    n_dim = dy.shape[1]
    r, c = (k_dim // 2, n_dim // N_CHIPS) if col_slots else (k_dim // N_CHIPS // 2, n_dim)
    tm = 2 * r
    tn = _pick(c, (512, 896, 640, 256, 128))
    tk = _pick(s, (2048, 1024, 512, 256))
    a_spec = pl.BlockSpec((tk, tm), lambda i, j, k: (k, i))
    b_spec = pl.BlockSpec((tk, tn), lambda i, j, k: (k, j))
    if col_slots:
        nb = c // tn
        o_spec = pl.BlockSpec((2, None, r, tn), lambda i, j, k: (0, j // nb, 0, j % nb))
    else:
        o_spec = pl.BlockSpec((2, None, r, tn), lambda i, j, k: (0, i, 0, j))
    return _mm_call(x, dy, dims=(((0,), (0,)), ((), ())), grid=(k_dim // tm, n_dim // tn, s // tk),
                    a_spec=a_spec, b_spec=b_spec, acc_shape=(tm, tn), epilogue=lambda acc: (acc.reshape(2, r, tn),),
                    out_shapes=[SDS((2, N_CHIPS, r, c), F32)], out_specs=[o_spec], name=name)[0]


def _rms(x, g):
    return x * lax.rsqrt(jnp.mean(x * x, axis=-1, keepdims=True) + EPS) * g


def _rms_fwd(h, g, *, name):
    rows, d = h.shape
    tr = _pick(rows, (512, 256))

    def body(h_ref, g_ref, o_ref):
        o_ref[...] = _rms(h_ref[...], g_ref[...]).astype(o_ref.dtype)

    return pl.pallas_call(
        body, grid=(rows // tr,),
        in_specs=[pl.BlockSpec((tr, d), lambda i: (i, 0)), pl.BlockSpec((1, d), lambda i: (0, 0))],
        out_specs=pl.BlockSpec((tr, d), lambda i: (i, 0)), out_shape=SDS((rows, d), BF16),
        compiler_params=_cparams(("parallel",)), name=name)(h, g)


def _rms_bwd(h, g, da, dres, *, name):
    rows, d = h.shape
    tr = _pick(rows, (512, 256))

    def body(h_ref, g_ref, da_ref, dres_ref, dh_ref, dg_ref):
        _, vjp = jax.vjp(_rms, h_ref[...], g_ref[...])
        dh, dg = vjp(da_ref[...].astype(F32))
        dh_ref[...] = dres_ref[...] + dh

        @pl.when(pl.program_id(0) == 0)
        def _():
            dg_ref[...] = jnp.zeros_like(dg_ref)

        dg_ref[...] += dg

    row_spec = pl.BlockSpec((tr, d), lambda i: (i, 0))
    vec_spec = pl.BlockSpec((1, d), lambda i: (0, 0))
    return pl.pallas_call(
        body, grid=(rows // tr,), in_specs=[row_spec, vec_spec, row_spec, row_spec],
        out_specs=[row_spec, vec_spec], out_shape=[SDS((rows, d), F32), SDS((1, d), F32)],
        compiler_params=_cparams(("arbitrary",)), name=name)(h, g, da, dres)


def _loss_head(h, g, target, *, name):
    rows, d = h.shape
    tr = _pick(rows, (512, 256))

    def body(h_ref, g_ref, t_ref, loss_ref, dh_ref, dg_ref):
        y, vjp = jax.vjp(_rms, h_ref[...], g_ref[...])
        err = y - t_ref[...]
        dh, dg = vjp(err * (1.0 / d))
        dh_ref[...] = dh

        @pl.when(pl.program_id(0) == 0)
        def _():
            dg_ref[...] = jnp.zeros_like(dg_ref)
            loss_ref[...] = jnp.zeros_like(loss_ref)

        dg_ref[...] += dg
        part = jnp.sum(jnp.sum(err * err, axis=-1, keepdims=True), axis=0, keepdims=True) * (0.5 / d)
        loss_ref[...] += jnp.broadcast_to(part, loss_ref.shape)

    row_spec = pl.BlockSpec((tr, d), lambda i: (i, 0))
    vec_spec = pl.BlockSpec((1, d), lambda i: (0, 0))
    loss_spec = pl.BlockSpec((8, 128), lambda i: (0, 0))
    return pl.pallas_call(
        body, grid=(rows // tr,), in_specs=[row_spec, vec_spec, row_spec],
        out_specs=[loss_spec, row_spec, vec_spec],
        out_shape=[SDS((8, 128), F32), SDS((rows, d), F32), SDS((1, d), F32)],
        compiler_params=_cparams(("arbitrary",)), name=name)(h, g, target)


def _gelu(x):
    return 0.5 * x * (1.0 + lax.erf(x * (1.0 / math.sqrt(2.0))))


def _gate_tile(pu, pv, ln_g, ln_b, ws, bs_t):
    u = [_gelu(p) for p in pu]
    v = [_gelu(p) for p in pv]
    mu = sum(jnp.sum(t, axis=-1, keepdims=True) for t in v) * (1.0 / D_INNER)
    vc = [t - mu for t in v]
    var = sum(jnp.sum(t * t, axis=-1, keepdims=True) for t in vc) * (1.0 / D_INNER)
    rstd = lax.rsqrt(var + EPS)
    row = lax.broadcasted_iota(jnp.int32, (CHUNK, CHUNK), 0)
    col = lax.broadcasted_iota(jnp.int32, (CHUNK, CHUNK), 1)
    out = []
    for gi in range(A_GROUPS):
        vn = vc[gi] * rstd * ln_g[gi] + ln_b[gi]
        w = jnp.where(row >= col, ws[gi], 0.0)
        sv = _dot(w, vn) + bs_t[gi]
        out.append(u[gi] * sv)
    return out


def _split(ref, n, width):
    return [ref[:, i * width:(i + 1) * width] for i in range(n)]


def _gate_in_specs():
    return [
        pl.BlockSpec((CHUNK, D_INNER), lambda c: (c, 0)),
        pl.BlockSpec((CHUNK, D_INNER), lambda c: (c, 1)),
        pl.BlockSpec((1, D_INNER), lambda c: (0, 0)),
        pl.BlockSpec((1, D_INNER), lambda c: (0, 0)),
        pl.BlockSpec((A_GROUPS, CHUNK, CHUNK), lambda c: (0, 0, 0)),
        pl.BlockSpec((A_GROUPS, CHUNK, 1), lambda c: (0, 0, 0)),
    ]


def _gate_args(u_ref, v_ref, g_ref, b_ref, ws_ref, bs_ref):
    ng, gw = A_GROUPS, A_GROUP_W
    return (_split(u_ref, ng, gw), _split(v_ref, ng, gw), _split(g_ref, ng, gw), _split(b_ref, ng, gw),
            [ws_ref[i] for i in range(ng)], [bs_ref[i] for i in range(ng)])


def _gate_fwd(proj, ln_g, ln_b, ws, bs_col, mixcat, *, name):
    def body(u_ref, v_ref, g_ref, b_ref, ws_ref, bs_ref, cat_in, cat_ref):
        del cat_in
        out = _gate_tile(*_gate_args(u_ref, v_ref, g_ref, b_ref, ws_ref, bs_ref))
        for gi, o in enumerate(out):
            cat_ref[:, gi * A_GROUP_W:(gi + 1) * A_GROUP_W] = o.astype(cat_ref.dtype)

    return pl.pallas_call(
        body, grid=(N_CHUNKS,), in_specs=[*_gate_in_specs(), pl.BlockSpec(memory_space=pl.ANY)],
        out_specs=pl.BlockSpec((CHUNK, D_INNER), lambda c: (c, 0)), out_shape=SDS(mixcat.shape, mixcat.dtype),
        input_output_aliases={6: 0}, compiler_params=_cparams(("parallel",)), name=name,
    )(proj, proj, ln_g, ln_b, ws, bs_col, mixcat)


def _gate_bwd(proj, ln_g, ln_b, ws, bs_col, dcat, dproj, *, name):
    ng, gw = A_GROUPS, A_GROUP_W

    def body(u_ref, v_ref, g_ref, b_ref, ws_ref, bs_ref, d_ref, dproj_in, dproj_ref, dg_ref, db_ref, dws_ref, dbs_ref):
        del dproj_in
        args = _gate_args(u_ref, v_ref, g_ref, b_ref, ws_ref, bs_ref)
        _, vjp = jax.vjp(_gate_tile, *args)
        dpu, dpv, dg, db, dws, dbs = vjp(_split(d_ref, ng, gw))
        for gi in range(ng):
            dproj_ref[:, gi * gw:(gi + 1) * gw] = dpu[gi].astype(dproj_ref.dtype)
            dproj_ref[:, D_INNER + gi * gw:D_INNER + (gi + 1) * gw] = dpv[gi].astype(dproj_ref.dtype)

        @pl.when(pl.program_id(0) == 0)
        def _():
            for r in (dg_ref, db_ref, dws_ref, dbs_ref):
                r[...] = jnp.zeros_like(r)

        for gi in range(ng):
            dg_ref[:, gi * gw:(gi + 1) * gw] += dg[gi]
            db_ref[:, gi * gw:(gi + 1) * gw] += db[gi]
            dws_ref[gi] += dws[gi]
            dbs_ref[gi] += dbs[gi]

    in_specs = _gate_in_specs()
    return pl.pallas_call(
        body, grid=(N_CHUNKS,),
        in_specs=[*in_specs, pl.BlockSpec((CHUNK, D_INNER), lambda c: (c, 0)), pl.BlockSpec(memory_space=pl.ANY)],
        out_specs=[pl.BlockSpec((CHUNK, 2 * D_INNER), lambda c: (c, 0)), *in_specs[2:]],
        out_shape=[SDS(dproj.shape, dproj.dtype), SDS((1, D_INNER), F32), SDS((1, D_INNER), F32),
                   SDS((ng, CHUNK, CHUNK), F32), SDS((ng, CHUNK, 1), F32)],
        input_output_aliases={7: 0}, compiler_params=_cparams(("arbitrary",)), name=name,
    )(proj, proj, ln_g, ln_b, ws, bs_col, dcat, dproj)


ATT_TQ = 2048


def _attn_tile(q, k, v):
    s = _dot_nt(q, k) * (1.0 / math.sqrt(X_HEAD_DIM))
    s = s - jnp.max(s, axis=-1, keepdims=True)
    e = jnp.exp(s)
    p = e / jnp.sum(e, axis=-1, keepdims=True)
    return _dot(p, v)


def _attn_in_specs(q_blk, order):
    hd = X_HEAD_DIM
    return [
        pl.BlockSpec((ATT_TQ, hd), lambda a, b: (order(a, b)[0], q_blk + order(a, b)[1])),
        pl.BlockSpec((N_MEM, hd), lambda a, b: (0, order(a, b)[1])),
        pl.BlockSpec((N_MEM, hd), lambda a, b: (0, X_HEADS + order(a, b)[1])),
    ]


def _attn_fwd(proj, q_off, kv, *, name):
    order = lambda i, h: (i, h)
    cat_blk = D_INNER // X_HEAD_DIM

    def body(q_ref, k_ref, v_ref, o_ref):
        o_ref[...] = _attn_tile(q_ref[...], k_ref[...], v_ref[...]).astype(o_ref.dtype)

    return pl.pallas_call(
        body, grid=(SEQ // ATT_TQ, X_HEADS), in_specs=_attn_in_specs(q_off // X_HEAD_DIM, order),
        out_specs=pl.BlockSpec((ATT_TQ, X_HEAD_DIM), lambda i, h: (i, cat_blk + h)),
        out_shape=SDS((SEQ, MIX_OUT), BF16), compiler_params=_cparams(("parallel", "parallel")), name=name,
    )(proj, kv, kv)


def _attn_bwd(proj, q_off, kv, dcat, dproj_width, dq_off, *, name):
    order = lambda h, i: (i, h)
    cat_blk = D_INNER // X_HEAD_DIM
    dq_blk = dq_off // X_HEAD_DIM

    def body(q_ref, k_ref, v_ref, do_ref, dq_ref, dk_ref, dv_ref):
        _, vjp = jax.vjp(_attn_tile, q_ref[...], k_ref[...], v_ref[...])
        dq, dk, dv = vjp(do_ref[...])
        dq_ref[...] = dq.astype(dq_ref.dtype)

        @pl.when(pl.program_id(1) == 0)
        def _():
            dk_ref[...] = jnp.zeros_like(dk_ref)
            dv_ref[...] = jnp.zeros_like(dv_ref)

        dk_ref[...] += dk
        dv_ref[...] += dv

    kv_spec = pl.BlockSpec((N_MEM, X_HEAD_DIM), lambda h, i: (0, h))
    return pl.pallas_call(
        body, grid=(X_HEADS, SEQ // ATT_TQ),
        in_specs=[*_attn_in_specs(q_off // X_HEAD_DIM, order),
                  pl.BlockSpec((ATT_TQ, X_HEAD_DIM), lambda h, i: (i, cat_blk + h))],
        out_specs=[pl.BlockSpec((ATT_TQ, X_HEAD_DIM), lambda h, i: (i, dq_blk + h)), kv_spec, kv_spec],
        out_shape=[SDS((SEQ, dproj_width), BF16), SDS((N_MEM, X_WIDTH), F32), SDS((N_MEM, X_WIDTH), F32)],
        compiler_params=_cparams(("parallel", "arbitrary")), name=name,
    )(proj, kv, kv, dcat)


CONV_TC = 512
CONV_ROWS = 128
CONV_HALO = 8


def _shift_down(x, s):
    if s == 0:
        return x
    row = lax.broadcasted_iota(jnp.int32, x.shape, 0)
    return jnp.where(row >= s, pltpu.roll(x, s, 0), 0.0)


def _shift_up(x, s):
    if s == 0:
        return x
    n = x.shape[0]
    row = lax.broadcasted_iota(jnp.int32, x.shape, 0)
    return jnp.where(row < n - s, pltpu.roll(x, n - s, 0), 0.0)


def _conv_fwd(proj, w, b, *, name):
    blk0 = D_INNER // CONV_TC

    window = CONV_HALO + CONV_ROWS

    def body(x_ref, w_ref, b_ref, o_ref):
        def rows_step(lanes, first, keep):
            start = first - keep
            if not isinstance(start, int):
                start = pl.multiple_of(start, CONV_HALO)
            down = _shift_down if keep == 0 else lambda v, s: pltpu.roll(v, s, 0) if s else v
            x = x_ref[pl.ds(start, window), lanes]
            w = w_ref[:, lanes]
            pre = b_ref[:, lanes] + jnp.zeros_like(x)
            for k in range(CONV_K):
                pre = pre + w[k:k + 1, :] * down(x, CONV_K - 1 - k)
            o_ref[pl.ds(first, CONV_ROWS), lanes] = (pre * jax.nn.sigmoid(pre))[keep:keep + CONV_ROWS, :]

        def rest(lanes):
            def step(i, carry):
                rows_step(lanes, pl.multiple_of(i * CONV_ROWS, CONV_ROWS), CONV_HALO)
                return carry
            return step

        for c in range(CONV_TC // 128):
            lanes = pl.ds(c * 128, 128)
            rows_step(lanes, 0, 0)
            lax.fori_loop(1, SEQ // CONV_ROWS, rest(lanes), 0)

    return pl.pallas_call(
        body, grid=(CONV_DIM // CONV_TC,),
        in_specs=[pl.BlockSpec((SEQ, CONV_TC), lambda j: (0, blk0 + j)), pl.BlockSpec((CONV_K, CONV_TC), lambda j: (0, j)),
                  pl.BlockSpec((1, CONV_TC), lambda j: (0, j))],
        out_specs=pl.BlockSpec((SEQ, CONV_TC), lambda j: (0, j)), out_shape=SDS((SEQ, CONV_DIM), F32),
        compiler_params=_cparams(("parallel",)), name=name)(proj, w, b)


def _conv_bwd(proj, w, b, dxs, dbm, dcm, dproj, *, name):
    tc = CONV_TC // 2
    blk0 = D_INNER // tc
    n_x = D_INNER // tc
    n_b = SSM_GROUPS * SSM_STATE // tc

    window = CONV_ROWS + 2 * CONV_HALO
    n_rows = SEQ // CONV_ROWS

    def body(x_ref, w_ref, b_ref, dxs_ref, dbm_ref, dcm_ref, dproj_in, dproj_ref, dw_ref, db_ref):
        del dproj_in
        j = pl.program_id(0)

        def rows_step(lanes, first, keep, sums):
            start = first - keep
            if not isinstance(start, int):
                start = pl.multiple_of(start, CONV_HALO)
            rows = pl.ds(start, window)
            down = _shift_down if keep == 0 else lambda v, s: pltpu.roll(v, s, 0) if s else v
            up = _shift_up if keep == 2 * CONV_HALO else lambda v, s: pltpu.roll(v, window - s, 0) if s else v
            x = x_ref[rows, lanes]
            w = w_ref[:, lanes]
            pre = b_ref[:, lanes] + jnp.zeros_like(x)
            for k in range(CONV_K):
                pre = pre + w[k:k + 1, :] * down(x, CONV_K - 1 - k)
            sg = jax.nn.sigmoid(pre)
            dact = jnp.where(j < n_x, dxs_ref[rows, lanes],
                             jnp.where(j < n_x + n_b, dbm_ref[rows, lanes], dcm_ref[rows, lanes]))
            dpre = dact * (sg * (1.0 + pre * (1.0 - sg)))
            kept = lambda v: jnp.sum(v[keep:keep + CONV_ROWS, :], axis=0, keepdims=True)
            dx = jnp.zeros_like(x)
            new_sums = []
            for k in range(CONV_K):
                s = CONV_K - 1 - k
                dx = dx + w[k:k + 1, :] * up(dpre, s)
                new_sums.append(sums[k] + kept(dpre * down(x, s)))
            new_sums.append(sums[CONV_K] + kept(dpre))
            dproj_ref[pl.ds(first, CONV_ROWS), lanes] = dx[keep:keep + CONV_ROWS, :].astype(dproj_ref.dtype)
            return tuple(new_sums)

        for c in range(tc // 128):
            lanes = pl.ds(c * 128, 128)
            sums = rows_step(lanes, 0, 0, (jnp.zeros((1, 128), F32),) * (CONV_K + 1))
            sums = lax.fori_loop(
                1, n_rows - 1,
                lambda i, sums: rows_step(lanes, pl.multiple_of(i * CONV_ROWS, CONV_ROWS), CONV_HALO, sums), sums)
            sums = rows_step(lanes, SEQ - CONV_ROWS, 2 * CONV_HALO, sums)
            for k in range(CONV_K):
                dw_ref[k:k + 1, lanes] = sums[k]
            db_ref[:, lanes] = sums[CONV_K]

    clip = lambda v, hi: jnp.minimum(jnp.maximum(v, 0), hi)
    return pl.pallas_call(
        body, grid=(CONV_DIM // tc,),
        in_specs=[pl.BlockSpec((SEQ, tc), lambda j: (0, blk0 + j)), pl.BlockSpec((CONV_K, tc), lambda j: (0, j)),
                  pl.BlockSpec((1, tc), lambda j: (0, j)),
                  pl.BlockSpec((SEQ, tc), lambda j: (0, clip(j, n_x - 1))),
                  pl.BlockSpec((SEQ, tc), lambda j: (0, clip(j - n_x, n_b - 1))),
                  pl.BlockSpec((SEQ, tc), lambda j: (0, clip(j - n_x - n_b, n_b - 1))),
                  pl.BlockSpec(memory_space=pl.ANY)],
        out_specs=[pl.BlockSpec((SEQ, tc), lambda j: (0, blk0 + j)), pl.BlockSpec((CONV_K, tc), lambda j: (0, j)),
                   pl.BlockSpec((1, tc), lambda j: (0, j))],
        out_shape=[SDS(dproj.shape, dproj.dtype), SDS((CONV_K, CONV_DIM), F32), SDS((1, CONV_DIM), F32)],
        input_output_aliases={6: 0}, compiler_params=_cparams(("parallel",)), name=name,
    )(proj, w, b, dxs, dbm, dcm, dproj)


SSM_PAIRS = SSM_HPG // 2


def _dot_exact01(x, m01, m01_t, x_first, differentiable):
    def product(v, m):
        hi = v.astype(BF16)
        rest = v - hi.astype(F32)
        mid = rest.astype(BF16)
        lo = (rest - mid.astype(F32)).astype(BF16)
        dims = (((1,), (0,)), ((), ()))
        dot = lambda part: lax.dot_general(*((part, m) if x_first else (m, part)), dims, preferred_element_type=F32)
        return dot(hi) + dot(mid) + dot(lo)

    if not differentiable:
        return product(x, m01)

    @jax.custom_vjp
    def exact(v):
        return product(v, m01)

    exact.defvjp(lambda v: (product(v, m01), None), lambda _, ct: (product(ct, m01_t),))
    return exact(x)


def _ssd_tile(xp, zp, bm, cm, hp, dt_c, dt_r, bias, bias_col, alog, alog_col, dsk, gnp, differentiable=False):
    row = lax.broadcasted_iota(jnp.int32, (CHUNK, CHUNK), 0)
    col = lax.broadcasted_iota(jnp.int32, (CHUNK, CHUNK), 1)
    causal = row >= col
    left = col < SSM_HEAD_DIM
    top = row < SSM_HEAD_DIM
    ones = jnp.ones((CHUNK, CHUNK), BF16)
    cb = _dot_nt(cm, bm)
    dtp = jax.nn.softplus(dt_c + bias)
    da_c = dtp * -jnp.exp(alog)
    da_r = jax.nn.softplus(dt_r + bias_col) * -jnp.exp(alog_col)
    lower = jnp.where(causal, 1.0, 0.0).astype(BF16)
    upper = jnp.where(row <= col, 1.0, 0.0).astype(BF16)
    cs = _dot_exact01(da_c, lower, upper, False, differentiable)
    cs_rows = _dot_exact01(da_r, upper, lower, True, differentiable)
    cs_last = jnp.sum(da_c, axis=0, keepdims=True)
    ecs, decay, ecl = jnp.exp(cs), jnp.exp(cs_last - cs), jnp.exp(cs_last)
    m = [cb * jnp.exp(jnp.where(causal, cs[:, r:r + 1] - cs_rows[r:r + 1, :], -1e30)) for r in range(SSM_HPG)]
    ygs, hn = [], []
    for p in range(SSM_PAIRS):
        a, b = 2 * p, 2 * p + 1
        pair = lambda v: jnp.where(left, v[:, a:a + 1], v[:, b:b + 1])
        xdt = xp[p] * pair(dtp)
        y = jnp.where(left, _dot(m[a], xdt), _dot(m[b], xdt))
        y = y + _dot_nt(cm, hp[p]) * pair(ecs)
        y = y + xp[p] * pair(dsk)
        states = _dot_tn(xdt * pair(decay), bm)
        hn.append(hp[p] * jnp.where(top, ecl[:, a:a + 1], ecl[:, b:b + 1]) + states)
        ygs.append(y * (zp[p] * jax.nn.sigmoid(zp[p])))
    ms = sum(_dot(t * t, ones) for t in ygs) * (1.0 / SSM_GROUP_W)
    rs = lax.rsqrt(ms + EPS)
    return [ygs[p] * rs * gnp[p] for p in range(SSM_PAIRS)], hn


def _ssd_in_specs(cidx):
    gw, n = SSM_GROUP_W, SSM_STATE
    bm_blk = D_INNER // n
    return [
        pl.BlockSpec((CHUNK, gw), lambda g, c: (cidx(c), g)),
        pl.BlockSpec((CHUNK, gw), lambda g, c: (cidx(c), g)),
        pl.BlockSpec((CHUNK, n), lambda g, c: (cidx(c), bm_blk + g)),
        pl.BlockSpec((CHUNK, n), lambda g, c: (cidx(c), bm_blk + SSM_GROUPS + g)),
        pl.BlockSpec((None, CHUNK, SSM_HPG), lambda g, c: (g, cidx(c), 0)),
        pl.BlockSpec((None, SSM_HPG, CHUNK), lambda g, c: (g, 0, cidx(c))),
        pl.BlockSpec((None, 3, SSM_HPG), lambda g, c: (g, 0, 0)),
        pl.BlockSpec((None, SSM_HPG, 2), lambda g, c: (g, 0, 0)),
        pl.BlockSpec((1, gw), lambda g, c: (0, g)),
    ]


def _ssd_args(x_ref, z_ref, bm_ref, cm_ref, hp, dtc_ref, dtr_ref, prow_ref, pcol_ref, gn_ref):
    npair, w = SSM_PAIRS, 2 * SSM_HEAD_DIM
    return (_split(x_ref, npair, w), _split(z_ref, npair, w), bm_ref[...], cm_ref[...], hp, dtc_ref[...], dtr_ref[...],
            prow_ref[0:1, :], pcol_ref[:, 0:1], prow_ref[1:2, :], pcol_ref[:, 1:2], prow_ref[2:3, :],
            _split(gn_ref, npair, w))


def _pair_rows(ref):
    w = 2 * SSM_HEAD_DIM
    return [ref[p * w:(p + 1) * w, :] for p in range(SSM_PAIRS)]


def _ssd_fwd(xbc, proj, dt_c, dt_r, par_row, par_col, gn, mixcat, *, name):
    w = 2 * SSM_HEAD_DIM

    def body(x_ref, z_ref, bm_ref, cm_ref, dtc_ref, dtr_ref, prow_ref, pcol_ref, gn_ref, cat_in,
             cat_ref, hprev_ref, h_scr):
        del cat_in

        @pl.when(pl.program_id(1) == 0)
        def _():
            h_scr[...] = jnp.zeros_like(h_scr)

        hprev_ref[...] = h_scr[...]
        yn, hn = _ssd_tile(*_ssd_args(x_ref, z_ref, bm_ref, cm_ref, _pair_rows(h_scr), dtc_ref, dtr_ref, prow_ref,
                                      pcol_ref, gn_ref))
        for p in range(SSM_PAIRS):
            cat_ref[:, p * w:(p + 1) * w] = yn[p].astype(cat_ref.dtype)
            h_scr[p * w:(p + 1) * w, :] = hn[p]

    return pl.pallas_call(
        body, grid=(SSM_GROUPS, N_CHUNKS), in_specs=[*_ssd_in_specs(lambda c: c), pl.BlockSpec(memory_space=pl.ANY)],
        out_specs=[pl.BlockSpec((CHUNK, SSM_GROUP_W), lambda g, c: (c, g)),
                   pl.BlockSpec((None, None, SSM_GROUP_W, SSM_STATE), lambda g, c: (c, g, 0, 0))],
        out_shape=[SDS(mixcat.shape, mixcat.dtype), SDS((N_CHUNKS, SSM_GROUPS, SSM_GROUP_W, SSM_STATE), F32)],
        scratch_shapes=[pltpu.VMEM((SSM_GROUP_W, SSM_STATE), F32)],
        input_output_aliases={9: 0}, compiler_params=_cparams(("parallel", "arbitrary")), name=name,
    )(xbc, proj, xbc, xbc, dt_c, dt_r, par_row, par_col, gn, mixcat)


def _ssd_bwd(xbc, proj, dt_c, dt_r, par_row, par_col, gn, hprev, dcat, dproj, *, name):
    nh, w, gw, n = SSM_HPG, 2 * SSM_HEAD_DIM, SSM_GROUP_W, SSM_STATE
    rev = lambda c: N_CHUNKS - 1 - c

    def body(x_ref, z_ref, bm_ref, cm_ref, dtc_ref, dtr_ref, prow_ref, pcol_ref, gn_ref, hprev_ref, dy_ref,
             dproj_in, dz_ref, dxs_ref, dbm_ref, dcm_ref, ddtc_ref, ddtr_ref, dprow_ref, dpcol_ref, dgn_ref, dh_scr):
        del dproj_in
        first = pl.program_id(1) == 0

        @pl.when(first)
        def _():
            dh_scr[...] = jnp.zeros_like(dh_scr)
            for ref in (dprow_ref, dpcol_ref, dgn_ref):
                ref[...] = jnp.zeros_like(ref)

        args = _ssd_args(x_ref, z_ref, bm_ref, cm_ref, _pair_rows(hprev_ref), dtc_ref, dtr_ref, prow_ref, pcol_ref,
                         gn_ref)
        _, vjp = jax.vjp(lambda *a: _ssd_tile(*a, differentiable=True), *args)
        dxs, dzs, dbm, dcm, dhs, ddtc, ddtr, dbias, dbias_col, dalog, dalog_col, ddsk, dgn = vjp(
            (_split(dy_ref, SSM_PAIRS, w), _pair_rows(dh_scr)))
        dbm_ref[...] = dbm
        dcm_ref[...] = dcm
        ddtc_ref[...] = ddtc
        ddtr_ref[...] = ddtr
        for q in range(SSM_PAIRS):
            dxs_ref[:, q * w:(q + 1) * w] = dxs[q]
            dz_ref[:, q * w:(q + 1) * w] = dzs[q].astype(dz_ref.dtype)
            dh_scr[q * w:(q + 1) * w, :] = dhs[q]
            dgn_ref[:, q * w:(q + 1) * w] += dgn[q]
        for i, d in enumerate((dbias, dalog, ddsk)):
            dprow_ref[i:i + 1, :] += d
        for i, d in enumerate((dbias_col, dalog_col)):
            dpcol_ref[:, i:i + 1] += d

    return pl.pallas_call(
        body, grid=(SSM_GROUPS, N_CHUNKS),
        in_specs=[*_ssd_in_specs(rev),
                  pl.BlockSpec((None, None, gw, n), lambda g, c: (rev(c), g, 0, 0)),
                  pl.BlockSpec((CHUNK, gw), lambda g, c: (rev(c), g)),
                  pl.BlockSpec(memory_space=pl.ANY)],
        out_specs=[pl.BlockSpec((CHUNK, gw), lambda g, c: (rev(c), g)),
                   pl.BlockSpec((CHUNK, gw), lambda g, c: (rev(c), g)),
                   pl.BlockSpec((CHUNK, n), lambda g, c: (rev(c), g)),
                   pl.BlockSpec((CHUNK, n), lambda g, c: (rev(c), g)),
                   pl.BlockSpec((None, CHUNK, nh), lambda g, c: (g, rev(c), 0)),
                   pl.BlockSpec((None, nh, CHUNK), lambda g, c: (g, 0, rev(c))),
                   pl.BlockSpec((None, 3, nh), lambda g, c: (g, 0, 0)),
                   pl.BlockSpec((None, nh, 2), lambda g, c: (g, 0, 0)),
                   pl.BlockSpec((1, gw), lambda g, c: (0, g))],
        out_shape=[SDS(dproj.shape, dproj.dtype), SDS((SEQ, D_INNER), F32), SDS((SEQ, SSM_GROUPS * n), F32),
                   SDS((SEQ, SSM_GROUPS * n), F32), SDS((SSM_GROUPS, SEQ, nh), F32), SDS((SSM_GROUPS, nh, SEQ), F32),
                   SDS((SSM_GROUPS, 3, nh), F32), SDS((SSM_GROUPS, nh, 2), F32), SDS((1, D_INNER), F32)],
        scratch_shapes=[pltpu.VMEM((gw, n), F32)],
        input_output_aliases={11: 0}, compiler_params=_cparams(("parallel", "arbitrary")), name=name,
    )(xbc, proj, xbc, xbc, dt_c, dt_r, par_row, par_col, gn, hprev, dcat, dproj)


def _sum_contributions(chip, parts, landed, *, name):
    _, r, c = parts.shape
    tr = _pick(r, (256, 384, 128))

    def body(chip_ref, own_ref, landed_ref, o_ref):
        del chip_ref
        acc = own_ref[...].astype(F32)
        for s in range(landed_ref.shape[0]):
            acc = acc + landed_ref[s].astype(F32)
        o_ref[...] = acc

    grid_spec = pltpu.PrefetchScalarGridSpec(
        num_scalar_prefetch=1, grid=(r // tr,),
        in_specs=[pl.BlockSpec((None, tr, c), lambda i, chip_ref: (chip_ref[0], i, 0)),
                  pl.BlockSpec((landed.shape[0], tr, c), lambda i, chip_ref: (0, i, 0))],
        out_specs=pl.BlockSpec((tr, c), lambda i, chip_ref: (i, 0)))
    return pl.pallas_call(body, grid_spec=grid_spec, out_shape=SDS((r, c), F32),
                          compiler_params=_cparams(("parallel",)), name=name)(chip, parts, landed)


def _adamw(w, g, m, v, *, name):
    layers, r, c = w.shape
    if r <= 256 or r % 128 == 0:
        tr = min(r, 256)
        steps, spec = r // tr, pl.BlockSpec((None, tr, c), lambda l, i: (l, i, 0))
    else:
        tc = _pick(c, (256, 128))
        steps, spec = c // tc, pl.BlockSpec((None, r, tc), lambda l, i: (l, 0, i))

    def body(w_ref, g_ref, m_ref, v_ref, d_ref, mo_ref, vo_ref):
        g = g_ref[...]
        m_new = ADAM_B1 * m_ref[...] + (1.0 - ADAM_B1) * g
        v_new = ADAM_B2 * v_ref[...] + (1.0 - ADAM_B2) * (g * g)
        m_hat = m_new / (1.0 - ADAM_B1 ** ADAM_STEP)
        v_hat = v_new / (1.0 - ADAM_B2 ** ADAM_STEP)
        d_ref[...] = -ADAM_LR * (m_hat / (jnp.sqrt(v_hat) + ADAM_EPS) + ADAM_WD * w_ref[...])
        mo_ref[...] = m_new
        vo_ref[...] = v_new

    return pl.pallas_call(body, grid=(layers, steps), in_specs=[spec] * 4, out_specs=[spec] * 3,
                          out_shape=[SDS(w.shape, F32)] * 3, compiler_params=_cparams(("parallel", "parallel")),
                          name=name)(w, g, m, v)


ANY = pl.BlockSpec(memory_space=pl.ANY)


def _place():
    x, y, c = lax.axis_index("x"), lax.axis_index("y"), lax.axis_index("c")
    chips = [(1 - x, y), (x, 1 - y), (1 - x, 1 - y)]
    return x, y, c, chips


def _remote(src, dst, send_sem, recv_sem, to):
    return pltpu.make_async_remote_copy(src_ref=src, dst_ref=dst, send_sem=send_sem, recv_sem=recv_sem,
                                        device_id=to, device_id_type=MESH)


STREAM_ROWS = 256


def _stream_rows(i):
    return pl.ds(pl.multiple_of(i * STREAM_ROWS, STREAM_ROWS), STREAM_ROWS)


def _channel_scratch(width, dtype, rows=STREAM_ROWS):
    buf = (2, rows, width)
    return [pltpu.VMEM(buf, dtype), pltpu.VMEM(buf, dtype), *([pltpu.SemaphoreType.DMA((2,))] * 5),
            pltpu.SemaphoreType.REGULAR((2,))]


CHANNEL_REFS = 8


def _copy_blocks(srcs, dsts, ch):
    sbuf, _, ld, _, _, st, _, _ = ch
    n = len(srcs)
    load = lambda i: pltpu.make_async_copy(srcs[i], sbuf.at[i % 2], ld.at[i % 2])
    store = lambda i: pltpu.make_async_copy(sbuf.at[i % 2], dsts[i], st.at[i % 2])
    load(0).start()
    for i in range(n):
        if i + 1 < n:
            if i >= 1:
                store(i - 1).wait()
            load(i + 1).start()
        load(i).wait()
        store(i).start()
    for i in range(max(0, n - 2), n):
        store(i).wait()


def _exchange_block_streams(streams, sibling):
    plans = []
    for srcs, dsts, keeps, (sbuf, rbuf, ld, snd, rcv, st, kp, credit) in streams:
        n = len(srcs)

        def load(i, srcs=srcs, sbuf=sbuf, ld=ld):
            return pltpu.make_async_copy(srcs[i], sbuf.at[i % 2], ld.at[i % 2])

        def push(i, sbuf=sbuf, rbuf=rbuf, snd=snd, rcv=rcv):
            return _remote(sbuf.at[i % 2], rbuf.at[i % 2], snd.at[i % 2], rcv.at[i % 2], sibling)

        def store(i, rbuf=rbuf, dsts=dsts, st=st):
            return pltpu.make_async_copy(rbuf.at[i % 2], dsts[i], st.at[i % 2])

        def save(i, sbuf=sbuf, keeps=keeps, kp=kp):
            return pltpu.make_async_copy(sbuf.at[i % 2], keeps[i], kp.at[i % 2])

        def free_slot(i, n=n, store=store, credit=credit):
            if 1 <= i < n:
                store(i - 1).wait()
                if i + 1 < n:
                    pl.semaphore_signal(credit.at[(i + 1) % 2], 1, device_id=sibling, device_id_type=MESH)

        def send(i, n=n, load=load, push=push, save=save, keeps=keeps, credit=credit):
            if i < n:
                load(i).wait()
                pl.semaphore_wait(credit.at[i % 2], 1)
                push(i).start()
                if keeps[i] is not None:
                    save(i).start()

        def receive(i, n=n, load=load, push=push, store=store, save=save, keeps=keeps):
            if i < n:
                push(i).wait_recv()
                store(i).start()
                push(i).wait_send()
                if keeps[i] is not None:
                    save(i).wait()
                if i + 2 < n:
                    load(i + 2).start()

        for i in range(min(2, n)):
            pl.semaphore_signal(credit.at[i], 1, device_id=sibling, device_id_type=MESH)
            load(i).start()
        plans.append((n, free_slot, send, receive, store))
    for _, _, send, _, _ in plans:
        send(0)
    for i in range(max(p[0] for p in plans)):
        for _, free_slot, _, _, _ in plans:
            free_slot(i)
        for _, _, send, _, _ in plans:
            send(i + 1)
        for _, _, _, receive, _ in plans:
            receive(i)
    for n, _, _, _, store in plans:
        store(n - 1).wait()


def _all_gather_shards(shards, small, *, name):
    n = len(shards)

    def body(*refs):
        ins, outs = refs[:n + 1], refs[n + 1:2 * n + 2]
        scr = refs[2 * n + 2:]
        chans = [scr[CHANNEL_REFS * t:CHANNEL_REFS * (t + 1)] for t in range(n)]
        send_sems, recv_sems, small_sems = scr[CHANNEL_REFS * n:]
        x, y, c, _ = _place()
        me = 2 * x + y
        sibling = (x, y, 1 - c)
        near = (lax.rem(x + 1 - c, 2), lax.rem(y + c, 2))
        far = (lax.rem(x + c, 2), lax.rem(y + 1 - c, 2))
        k_near, k_far, k_diag = 2 * near[0] + near[1], 2 * far[0] + far[1], 3 - me
        targets = ((*near, c), (*far, c), (*far, c))
        arrives = (k_near, k_far, k_diag)
        streams_in = (k_far, k_near, k_diag)

        def ici(t, j, src, blk):
            return _remote(src, outs[t].at[blk, c], send_sems.at[3 * t + j], recv_sems.at[3 * t + j], targets[j])

        first = [ici(t, j, ins[t].at[c], me) for t in range(n + 1) for j in range(2)]
        for cp in first:
            cp.start()
        small_local = pltpu.make_async_copy(ins[n], outs[n].at[me], small_sems.at[6])
        small_local.start()
        for t in range(n):
            _copy_blocks([ins[t].at[h] for h in range(2)], [outs[t].at[me, h] for h in range(2)], chans[t])
        passed = []
        for j in range(3):
            for t in range(n + 1):
                landed = outs[t].at[arrives[j], c]
                ici(t, j, landed, arrives[j]).wait_recv()
                if j == 0:
                    fwd = ici(t, 2, landed, k_near)
                    fwd.start()
                    passed.append(fwd)
                if t < n:
                    _exchange_block_streams([([landed], [outs[t].at[streams_in[j], 1 - c]], [None], chans[t])], sibling)
                else:
                    fwd = _remote(landed, landed, small_sems.at[j], small_sems.at[3 + j], sibling)
                    fwd.start()
                    passed.append(fwd)
        for j in range(3):
            got = outs[n].at[streams_in[j], 1 - c]
            _remote(got, got, small_sems.at[j], small_sems.at[3 + j], sibling).wait_recv()
        for cp in first + passed:
            cp.wait_send()
        small_local.wait()

    scratch = []
    for s in shards:
        scratch += _channel_scratch(s.shape[2], s.dtype, rows=s.shape[1])
    return pl.pallas_call(
        body, in_specs=[ANY] * (n + 1), out_specs=[ANY] * (n + 1),
        out_shape=[SDS((N_CHIPS, *s.shape), s.dtype) for s in (*shards, small)],
        scratch_shapes=[*scratch, pltpu.SemaphoreType.DMA((3 * n + 3,)), pltpu.SemaphoreType.DMA((3 * n + 3,)),
                        pltpu.SemaphoreType.DMA((7,))],
        compiler_params=pltpu.CompilerParams(vmem_limit_bytes=VMEM_LIMIT), name=name)(*shards, small)


def _pair_reduce(stacks, *, name):
    n = len(stacks)
    per = 11

    def body(*refs):
        ins, outs, scr = refs[:n], refs[n:2 * n], refs[2 * n:]
        x, y, c, _ = _place()
        sibling = (x, y, 1 - c)
        streams = []
        for t in range(n):
            sraw, sbuf, rbuf, obuf, pbuf, ld_s, ld_o, snd, rcv, st, credit = scr[per * t:per * (t + 1)]
            steps = ins[t].shape[1] // STREAM_ROWS
            src, own, out = ins[t].at[1 - c], ins[t].at[c], outs[t]
            assert steps >= 2

            def load_s(i, slot, src=src, sraw=sraw, ld_s=ld_s):
                return pltpu.make_async_copy(src.at[_stream_rows(i)], sraw.at[slot], ld_s.at[slot])

            def load_o(i, slot, own=own, obuf=obuf, ld_o=ld_o):
                return pltpu.make_async_copy(own.at[_stream_rows(i)], obuf.at[slot], ld_o.at[slot])

            def push(slot, sbuf=sbuf, rbuf=rbuf, snd=snd, rcv=rcv):
                return _remote(sbuf.at[slot], rbuf.at[slot], snd.at[slot], rcv.at[slot], sibling)

            def store(i, slot, pbuf=pbuf, out=out, st=st):
                return pltpu.make_async_copy(pbuf.at[slot], out.at[_stream_rows(i)], st.at[slot])

            def send(i, slot, load_s=load_s, push=push, sraw=sraw, sbuf=sbuf, credit=credit):
                load_s(i, slot).wait()
                sbuf[slot] = sraw[slot].astype(sbuf.dtype)
                pl.semaphore_wait(credit.at[slot], 1)
                push(slot).start()

            def combine(i, slot, load_s=load_s, load_o=load_o, push=push, store=store, rbuf=rbuf, obuf=obuf, pbuf=pbuf,
                        credit=credit, steps=steps):
                load_o(i, slot).wait()
                push(slot).wait_recv()

                @pl.when(i >= 2)
                def _():
                    store(i, slot).wait()

                pbuf[slot] = (obuf[slot] + rbuf[slot].astype(F32)).astype(pbuf.dtype)
                store(i, slot).start()
                push(slot).wait_send()

                @pl.when(i + 2 < steps)
                def _():
                    load_s(i + 2, slot).start()
                    load_o(i + 2, slot).start()
                    pl.semaphore_signal(credit.at[slot], 1, device_id=sibling, device_id_type=MESH)

            for slot in range(2):
                pl.semaphore_signal(credit.at[slot], 1, device_id=sibling, device_id_type=MESH)
                load_s(slot, slot).start()
                load_o(slot, slot).start()
            streams.append((steps, send, combine, store))
        for _, send, _, _ in streams:
            send(0, 0)

        def step(i, carry):
            slot = lax.rem(i, 2)
            for steps, send, _, _ in streams:
                @pl.when(i + 1 < steps)
                def _(send=send):
                    send(i + 1, 1 - slot)
            for steps, _, combine, _ in streams:
                @pl.when(i < steps)
                def _(combine=combine):
                    combine(i, slot)
            return carry

        lax.fori_loop(0, max(s[0] for s in streams), step, 0)
        for _, _, _, store in streams:
            for slot in range(2):
                store(0, slot).wait()

    scratch = []
    for s in stacks:
        buf = (2, STREAM_ROWS, s.shape[2])
        scratch += [pltpu.VMEM(buf, F32), pltpu.VMEM(buf, BF16), pltpu.VMEM(buf, BF16), pltpu.VMEM(buf, F32),
                    pltpu.VMEM(buf, BF16), *([pltpu.SemaphoreType.DMA((2,))] * 5), pltpu.SemaphoreType.REGULAR((2,))]
    return pl.pallas_call(
        body, in_specs=[ANY] * n, out_specs=[ANY] * n, out_shape=[SDS(s.shape[1:], BF16) for s in stacks],
        scratch_shapes=scratch, compiler_params=pltpu.CompilerParams(vmem_limit_bytes=VMEM_LIMIT), name=name)(*stacks)


HBM_SPEC = pl.BlockSpec(memory_space=pltpu.HBM)
SEM_SPEC = pl.BlockSpec(memory_space=pltpu.SEMAPHORE)
SIDE_EFFECT = pltpu.SideEffectType.DATAFLOW_SIDE_EFFECTING


def _scatter_copies(ins, lands, send_sems, recv_sems):
    _, _, c, chips = _place()
    return [_remote(ins[t].at[2 * cx + cy], lands[t].at[j], send_sems.at[3 * t + j], recv_sems.at[3 * t + j],
                    (cx, cy, c)) for t in range(len(ins)) for j, (cx, cy) in enumerate(chips)]


def _chip_scatter_start(parts, *, name):
    n = len(parts)

    def body(*refs):
        ins, lands = refs[:n], refs[n:2 * n]
        send_sems, recv_sems, token = refs[2 * n], refs[2 * n + 1], refs[-1]
        for cp in _scatter_copies(ins, lands, send_sems, recv_sems):
            cp.start()
        token[...] = jnp.zeros_like(token)

    hbm = lambda a: pltpu.with_memory_space_constraint(a, pltpu.HBM)
    lands = [hbm(lax.empty((3, *p.shape[1:]), p.dtype)) for p in parts]
    thru = [pltpu.HBM(a.shape, a.dtype) for a in (*parts, *lands)]
    outs = pl.pallas_call(
        body, name=name,
        out_shape=(pltpu.SemaphoreType.DMA((3 * n,)), pltpu.SemaphoreType.DMA((3 * n,)), *thru, SDS((8, 128), F32)),
        in_specs=[HBM_SPEC] * (2 * n),
        out_specs=(SEM_SPEC, SEM_SPEC, *([HBM_SPEC] * (2 * n)), pl.BlockSpec(memory_space=pltpu.VMEM)),
        input_output_aliases={i: 2 + i for i in range(2 * n)},
        compiler_params=pltpu.CompilerParams(has_side_effects=SIDE_EFFECT),
    )(*[hbm(p) for p in parts], *lands)
    return outs[0], outs[1], outs[2:2 + n], outs[2 + n:2 + 2 * n], outs[-1]


def _chip_scatter_wait(send_sems, recv_sems, parts, lands, after, *, name):
    n = len(parts)

    def body(*refs):
        ins, lands_in = refs[:n], refs[n:2 * n]
        for cp in _scatter_copies(ins, lands_in, refs[2 * n], refs[2 * n + 1]):
            cp.wait_send()
            cp.wait_recv()

    outs = pl.pallas_call(
        body, name=name, out_shape=[pltpu.HBM(a.shape, a.dtype) for a in (*parts, *lands)],
        in_specs=[*([HBM_SPEC] * (2 * n)), SEM_SPEC, SEM_SPEC, *([ANY] * len(after))],
        out_specs=[HBM_SPEC] * (2 * n), input_output_aliases={i: i for i in range(2 * n)},
        compiler_params=pltpu.CompilerParams(has_side_effects=SIDE_EFFECT),
    )(*parts, *lands, send_sems, recv_sems, *after)
    return outs[:n], outs[n:]


def _gather_copies(shards, zones, send_sems, recv_sems):
    x, y, c, chips = _place()
    return [_remote(shards[t].at[c], zones[t].at[2 * x + y, c], send_sems.at[3 * t + j], recv_sems.at[3 * t + j],
                    (cx, cy, c)) for t in range(len(shards)) for j, (cx, cy) in enumerate(chips)]


def _gather_start(shards, after, *, name):
    n = len(shards)

    def body(*refs):
        ins, zones = refs[:n], refs[n:2 * n]
        send_sems, recv_sems, token = refs[2 * n + len(after)], refs[2 * n + len(after) + 1], refs[-1]
        for cp in _gather_copies(ins, zones, send_sems, recv_sems):
            cp.start()
        token[...] = jnp.zeros_like(token)

    hbm = lambda a: pltpu.with_memory_space_constraint(a, pltpu.HBM)
    zones = [hbm(lax.empty((N_CHIPS, *s.shape), s.dtype)) for s in shards]
    thru = [pltpu.HBM(a.shape, a.dtype) for a in (*shards, *zones)]
    outs = pl.pallas_call(
        body, name=name,
        out_shape=(pltpu.SemaphoreType.DMA((3 * n,)), pltpu.SemaphoreType.DMA((3 * n,)), *thru, SDS((8, 128), F32)),
        in_specs=[*([HBM_SPEC] * (2 * n)), *([ANY] * len(after))],
        out_specs=(SEM_SPEC, SEM_SPEC, *([HBM_SPEC] * (2 * n)), pl.BlockSpec(memory_space=pltpu.VMEM)),
        input_output_aliases={i: 2 + i for i in range(2 * n)},
        compiler_params=pltpu.CompilerParams(has_side_effects=SIDE_EFFECT),
    )(*[hbm(s) for s in shards], *zones, *after)
    return outs[0], outs[1], outs[2:2 + n], outs[2 + n:2 + 2 * n], outs[-1]


def _gather_wait(send_sems, recv_sems, shards, zones, after, *, name):
    n = len(shards)

    def body(*refs):
        for cp in _gather_copies(refs[:n], refs[n:2 * n], refs[2 * n], refs[2 * n + 1]):
            cp.wait_send()
            cp.wait_recv()

    outs = pl.pallas_call(
        body, name=name, out_shape=[pltpu.HBM(a.shape, a.dtype) for a in (*shards, *zones)],
        in_specs=[*([HBM_SPEC] * (2 * n)), SEM_SPEC, SEM_SPEC, *([ANY] * len(after))],
        out_specs=[HBM_SPEC] * (2 * n), input_output_aliases={i: i for i in range(2 * n)},
        compiler_params=pltpu.CompilerParams(has_side_effects=SIDE_EFFECT),
    )(*shards, *zones, send_sems, recv_sems, *after)
    return outs[:n], outs[n:]


def _gather_finish(shards, zones, *, name):
    n = len(shards)

    def body(*refs):
        ins, zones_in, outs, scr = refs[:n], refs[n:2 * n], refs[2 * n:3 * n], refs[3 * n:]
        x, y, c, chips = _place()
        me = 2 * x + y
        sibling = (x, y, 1 - c)
        others = [2 * cx + cy for cx, cy in chips]
        chans = [scr[CHANNEL_REFS * t:CHANNEL_REFS * (t + 1)] for t in range(n)]
        for t in range(n):
            _copy_blocks([ins[t].at[h] for h in range(2)], [outs[t].at[me, h] for h in range(2)], chans[t])
        _exchange_block_streams([([zones_in[t].at[k, c] for k in others], [outs[t].at[k, 1 - c] for k in others],
                                  [None] * len(others), chans[t]) for t in range(n)], sibling)

    scratch = []
    for s in shards:
        scratch += _channel_scratch(s.shape[2], s.dtype, rows=s.shape[1])
    return pl.pallas_call(
        body, in_specs=[ANY] * (2 * n), out_specs=[ANY] * n, out_shape=[SDS(z.shape, z.dtype) for z in zones],
        input_output_aliases={n + t: t for t in range(n)}, scratch_shapes=scratch,
        compiler_params=pltpu.CompilerParams(vmem_limit_bytes=VMEM_LIMIT), name=name)(*shards, *zones)


def _pair_share(groups, *, name):
    finals = [f for grp in groups for f in grp]
    n, n_out = len(finals), len(groups)

    def body(*refs):
        ins, outs, scr = refs[:n], refs[n:n + n_out], refs[n + n_out:]
        x, y, c, _ = _place()
        sibling = (x, y, 1 - c)
        t, streams = 0, []
        for o, grp in enumerate(groups):
            rows = grp[0].shape[0] // 2
            blocks = [(layer, pl.ds(b * rows, rows)) for layer in range(len(grp)) for b in range(2)]
            streams.append(([ins[t + layer].at[rs] for layer, rs in blocks],
                            [outs[o].at[layer, 1 - c, rs] for layer, rs in blocks],
                            [outs[o].at[layer, c, rs] for layer, rs in blocks],
                            scr[CHANNEL_REFS * o:CHANNEL_REFS * (o + 1)]))
            t += len(grp)
        _exchange_block_streams(streams, sibling)

    scratch = []
    for grp in groups:
        scratch += _channel_scratch(grp[0].shape[1], grp[0].dtype, rows=grp[0].shape[0] // 2)
    return pl.pallas_call(
        body, in_specs=[ANY] * n, out_specs=[ANY] * n_out,
        out_shape=[SDS((len(grp), 2, *grp[0].shape), grp[0].dtype) for grp in groups],
        scratch_shapes=scratch, compiler_params=pltpu.CompilerParams(vmem_limit_bytes=VMEM_LIMIT), name=name)(*finals)


def _all_reduce_small(v, *, name):
    rows, lanes = v.shape
    n_dev = 8

    def body(v_ref, o_ref, all_ref, send_sems, recv_sems, local_sem):
        x, y, c, chips = _place()
        me, sibling = (x, y, c), (x, y, 1 - c)

        def block(px, py, pc):
            return all_ref.at[4 * px + 2 * py + pc]

        def copy(k, blk, to, src=None):
            return _remote(block(*blk) if src is None else src, block(*blk), send_sems.at[k], recv_sems.at[k], to)

        mine = pltpu.make_async_copy(v_ref, block(*me), local_sem)
        mine.start()
        first = [copy(0, me, sibling, src=v_ref)]
        first += [copy(1 + j, me, (*chip, c), src=v_ref) for j, chip in enumerate(chips)]
        for cp in first:
            cp.start()
        passed = [copy(4 + j, (*chip, c), sibling) for j, chip in enumerate(chips)]
        for j, chip in enumerate(chips):
            copy(1 + j, (*chip, c), me).wait_recv()
            passed[j].start()
        copy(0, sibling, me).wait_recv()
        for j, chip in enumerate(chips):
            copy(4 + j, (*chip, 1 - c), me).wait_recv()
        for cp in first + passed:
            cp.wait_send()
        mine.wait()
        acc = all_ref[0]
        for k in range(1, n_dev):
            acc = acc + all_ref[k]
        o_ref[...] = acc

    vmem = pl.BlockSpec(memory_space=pltpu.VMEM)
    return pl.pallas_call(
        body, in_specs=[vmem], out_specs=vmem, out_shape=SDS((rows, lanes), F32),
        scratch_shapes=[pltpu.VMEM((n_dev, rows, lanes), F32), pltpu.SemaphoreType.DMA((7,)),
                        pltpu.SemaphoreType.DMA((7,)), pltpu.SemaphoreType.DMA],
        compiler_params=pltpu.CompilerParams(vmem_limit_bytes=VMEM_LIMIT), name=name)(v)


def _relu2_epilogue(acc):
    return acc, jnp.square(jnp.maximum(acc, 0.0))


def _res_epilogue(acc, res):
    return (acc + res,)


def _drelu2_epilogue(acc, pre):
    return (acc * (2.0 * jnp.maximum(pre.astype(F32), 0.0)),)


def _ffn_fwd(h, g, w1, w2, tag):
    f = _rms_fwd(h, g, name=f"ffn_norm_{tag}")
    pre, act = _mm_nn(f, w1, name=f"ffn1_{tag}", epilogue=_relu2_epilogue, n_out_dtypes=(BF16, BF16))
    h_out = _mm_nn(act, w2, name=f"ffn2_{tag}", extras=(h,), epilogue=_res_epilogue)
    return h_out, (f, pre, act)


def _ffn_bwd(dh, h, g, w1, w2, saved, layer, after=()):
    f, pre, act = saved
    dpre = _mm_nt(dh, w2, name=f"ffn2_dx_{layer}", out_dtype=BF16, extras=(pre,), epilogue=_drelu2_epilogue,
                  after=after)
    dw2 = _mm_tn_stacked(act, dh, name=f"ffn2_dw_{layer}", col_slots=False)
    df = _mm_nt(dpre, w1, name=f"ffn1_dx_{layer}")
    dw1 = _mm_tn_stacked(f, dpre, name=f"ffn1_dw_{layer}", col_slots=True)
    dh, dg = _rms_bwd(h, g, df, dh, name=f"ffn_norm_bwd_{layer}")
    return dh, dg, dw1, dw2


def _kv_fwd(mem, g, w_kv, tag):
    m = _rms_fwd(mem, g, name=f"mem_norm_{tag}")
    return m, _mm_nn(m, w_kv, name=f"kv_{tag}")


def _kv_bwd(mem, g, w_kv, m, dk, dv, layer):
    dkv = jnp.concatenate([dk, dv], axis=1)
    dw = _mm_tn_stacked(m, dkv, name=f"kv_dw_{layer}", col_slots=True)
    dm = _mm_nt(dkv, w_kv, name=f"kv_dx_{layer}")
    _, dg = _rms_bwd(mem, g, dm, dm, name=f"mem_norm_bwd_{layer}")
    return dw, dg


def _local_step(x, mem, target, p, after_layer1=None, after_ffn0=None, after_mixer0=None):
    row = lambda v: v.reshape(1, -1)
    g = {}

    h0 = x
    a0 = _rms_fwd(h0, row(p["norm_mix"][0]), name="mix_norm_0")
    proj_a = _mm_nn(a0, p["a_in"], name="a_in", after=p.get("after_start", ()))
    m0, kv0 = _kv_fwd(mem, row(p["mem_norm"][0]), p["w_kv"][0], "0")
    cat0 = _attn_fwd(proj_a, 2 * D_INNER, kv0, name="attn_0")
    bs_col = p["a_bs"].reshape(A_GROUPS, CHUNK, 1)
    cat0 = _gate_fwd(proj_a, p["a_ln_g"], p["a_ln_b"], p["a_ws"], bs_col, cat0, name="gate")
    h1 = _mm_nn(cat0, p["w_out"][0], name="out_0", extras=(h0,), epilogue=_res_epilogue)
    w_ffn1_0, w_ffn2_0 = p["layer0_ffn"](h1) if "layer0_ffn" in p else (p["w_ffn1"][0], p["w_ffn2"][0])
    h2, ffn0 = _ffn_fwd(h1, row(p["norm_ffn"][0]), w_ffn1_0, w_ffn2_0, "0")

    w_kv1, b_in = p["layer1_mixer"](h2) if "layer1_mixer" in p else (p["w_kv"][1], p["b_in"])
    a1 = _rms_fwd(h2, row(p["norm_mix"][1]), name="mix_norm_1")
    proj_b = _mm_nn(a1, b_in, name="b_in")
    m1, kv1 = _kv_fwd(mem, row(p["mem_norm"][1]), w_kv1, "1")
    cat1 = _attn_fwd(proj_b, B_Q_OFF, kv1, name="attn_1")
    xbc = _conv_fwd(proj_b, p["b_conv_w"], p["b_conv_b"], name="conv")
    dt_raw = proj_b[:, B_DT_OFF:B_DT_OFF + SSM_HEADS].reshape(SEQ, SSM_GROUPS, SSM_HPG)
    dt_c = jnp.transpose(dt_raw, (1, 0, 2))
    dt_r = jnp.transpose(dt_raw, (1, 2, 0))
    per_head = lambda v: v.reshape(SSM_GROUPS, 1, SSM_HPG)
    par_row = jnp.concatenate([per_head(p["b_dt_bias"]), per_head(p["b_a_log"]), per_head(p["b_d"])], axis=1)
    ssd_par = (par_row, jnp.transpose(par_row[:, :2], (0, 2, 1)), p["b_gnorm"])
    cat1, hprev = _ssd_fwd(xbc, proj_b, dt_c, dt_r, *ssd_par, cat1, name="ssd")
    if "layer1_rest" in p:
        w_out1, w_ffn1_1, w_ffn2_1 = p["layer1_rest"](cat1)
    else:
        w_out1, w_ffn1_1, w_ffn2_1 = p["w_out"][1], p["w_ffn1"][1], p["w_ffn2"][1]
    h3 = _mm_nn(cat1, w_out1, name="out_1", extras=(h2,), epilogue=_res_epilogue)
    h4, ffn1 = _ffn_fwd(h3, row(p["norm_ffn"][1]), w_ffn1_1, w_ffn2_1, "1")

    loss, dh, g["final_norm"] = _loss_head(h4, row(p["final_norm"]), target, name="loss_head")

    dh, dnf1, dw1_1, dw2_1 = _ffn_bwd(dh, h3, row(p["norm_ffn"][1]), w_ffn1_1, w_ffn2_1, ffn1, 1)
    dcat1 = _mm_nt(dh, w_out1, name="out_dx_1")
    dwo_1 = _mm_tn_stacked(cat1, dh, name="out_dw_1", col_slots=False)
    dproj_b, dk1, dv1 = _attn_bwd(proj_b, B_Q_OFF, kv1, dcat1, B_IN_PAD, B_Q_OFF, name="attn_bwd_1")
    dproj_b, dxs, dbm, dcm, ddt_c, ddt_r, dpar_row, dpar_col, g["b_gnorm"] = _ssd_bwd(
        xbc, proj_b, dt_c, dt_r, *ssd_par, hprev, dcat1, dproj_b, name="ssd_bwd")
    dpar = dpar_row.at[:, :2].add(jnp.transpose(dpar_col, (0, 2, 1)))
    g["b_dt_bias"], g["b_a_log"], g["b_d"] = dpar[:, 0], dpar[:, 1], dpar[:, 2]
    dproj_b, g["b_conv_w"], g["b_conv_b"] = _conv_bwd(proj_b, p["b_conv_w"], p["b_conv_b"], dxs, dbm, dcm, dproj_b,
                                                      name="conv_bwd")
    ddt = jnp.transpose(ddt_c, (1, 0, 2)) + jnp.transpose(ddt_r, (2, 0, 1))
    ddt = jnp.pad(ddt.reshape(SEQ, SSM_HEADS), ((0, 0), (0, B_IN_PAD - B_DT_OFF - SSM_HEADS))).astype(BF16)
    dproj_b = lax.dynamic_update_slice(dproj_b, ddt, (0, B_DT_OFF))
    dwkv_1, dmn1 = _kv_bwd(mem, row(p["mem_norm"][1]), w_kv1, m1, dk1, dv1, 1)
    dwb = _b_in_grad_slots(_mm_tn(a1, dproj_b, name="b_in_dw"))
    da1 = _mm_nt(dproj_b, b_in, name="b_in_dx")
    dh, dnm1 = _rms_bwd(h2, row(p["norm_mix"][1]), da1, dh, name="mix_norm_bwd_1")
    layer1 = dict(w_kv=dwkv_1, w_out=dwo_1, w_ffn1=dw1_1, w_ffn2=dw2_1, b_in=dwb)
    token = () if after_layer1 is None else (after_layer1(layer1),)

    dh, dnf0, dw1_0, dw2_0 = _ffn_bwd(dh, h1, row(p["norm_ffn"][0]), w_ffn1_0, w_ffn2_0, ffn0, 0,
                                      after=token)
    ffn0_grads = dict(w_ffn1=dw1_0, w_ffn2=dw2_0)
    token = () if after_ffn0 is None else (after_ffn0(ffn0_grads),)
    dcat0 = _mm_nt(dh, p["w_out"][0], name="out_dx_0", after=token)
    dwo_0 = _mm_tn_stacked(cat0, dh, name="out_dw_0", col_slots=False)
    dproj_a, dk0, dv0 = _attn_bwd(proj_a, 2 * D_INNER, kv0, dcat0, A_IN, 2 * D_INNER, name="attn_bwd_0")
    dproj_a, g["a_ln_g"], g["a_ln_b"], g["a_ws"], dbs_col = _gate_bwd(
        proj_a, p["a_ln_g"], p["a_ln_b"], p["a_ws"], bs_col, dcat0, dproj_a, name="gate_bwd")
    g["a_bs"] = dbs_col.reshape(A_GROUPS, CHUNK)
    dwkv_0, dmn0 = _kv_bwd(mem, row(p["mem_norm"][0]), p["w_kv"][0], m0, dk0, dv0, 0)
    dwa = _mm_tn_stacked(a0, dproj_a, name="a_in_dw", col_slots=True)
    mixer0_grads = dict(w_kv=dwkv_0, w_out=dwo_0, a_in=dwa)
    token = () if after_mixer0 is None else (after_mixer0(mixer0_grads),)
    da0 = _mm_nt(dproj_a, p["a_in"], name="a_in_dx", after=token)
    dx, dnm0 = _rms_bwd(h0, row(p["norm_mix"][0]), da0, dh, name="mix_norm_bwd_0")

    g["norm_mix"] = jnp.concatenate([dnm0, dnm1], axis=0)
    g["norm_ffn"] = jnp.concatenate([dnf0, dnf1], axis=0)
    g["mem_norm"] = jnp.concatenate([dmn0, dmn1], axis=0)
    layer0 = dict(w_kv=dwkv_0, w_out=dwo_0, w_ffn1=dw1_0, w_ffn2=dw2_0, a_in=dwa)
    return loss, dx, g, layer0, layer1


def _b_in_full(gathered):
    n = B_IN // N_CHIPS
    dt0 = D_INNER + CONV_DIM - (N_CHIPS - 1) * n
    last = gathered[N_CHIPS - 1]
    return jnp.concatenate([*[gathered[k] for k in range(N_CHIPS - 1)], last[:, :dt0], last[:, dt0 + SSM_HEADS:],
                            last[:, dt0:dt0 + SSM_HEADS], jnp.zeros((D_MODEL, B_IN_PAD - B_IN), last.dtype)], axis=1)


def _b_in_grad_slots(d):
    n = B_IN // N_CHIPS
    dt0 = D_INNER + CONV_DIM
    last = jnp.concatenate([d[:, (N_CHIPS - 1) * n:dt0], d[:, B_DT_OFF:B_DT_OFF + SSM_HEADS], d[:, dt0:B_DT_OFF]], axis=1)
    slots = [*[d[:, k * n:(k + 1) * n] for k in range(N_CHIPS - 1)], last]
    half = D_MODEL // 2
    return jnp.stack([jnp.stack([s[h * half:(h + 1) * half] for s in slots]) for h in range(2)])


SMALL_REPL = ("norm_mix", "norm_ffn", "mem_norm", "a_ln_g", "a_ln_b", "a_ws", "a_bs", "b_dt_bias", "b_a_log", "b_d",
              "final_norm")
SMALL_SHARD = ("b_conv_w", "b_conv_b", "b_gnorm")
WEIGHTS = ("norm_mix", "norm_ffn", "mem_norm", "w_kv", "w_out", "w_ffn1", "w_ffn2", "a_in", "a_ln_g", "a_ln_b", "a_ws",
           "a_bs", "b_in", "b_conv_w", "b_conv_b", "b_dt_bias", "b_a_log", "b_d", "b_gnorm", "final_norm")
CONV_SHARD = CONV_DIM // N_CHIPS
GN_SHARD = D_INNER // N_CHIPS


LAYERED = ("w_kv", "w_out", "w_ffn1", "w_ffn2")


def _gather_weights(w):
    halves = lambda k, layer: (w[k][layer] if k in LAYERED else w[k][0]).reshape(2, -1, w[k].shape[-1]).astype(BF16)
    small = jnp.zeros((2, CONV_K, CONV_SHARD), F32)
    small = small.at[0].set(w["b_conv_w"][0])
    small = small.at[1, 0].set(w["b_conv_b"][0])
    small = small.at[1, 1, :GN_SHARD].set(w["b_gnorm"][0])
    first_names = ("w_kv", "w_out", "a_in")
    gathered = _all_gather_shards([halves(k, 0) for k in first_names], small, name="gather_weights_0")
    got = dict(zip(first_names, gathered))
    slots = lambda a: a.reshape(N_CHIPS, -1, a.shape[-1])
    rows = lambda a: a.reshape(-1, a.shape[-1])
    p = dict(w_kv=[slots(got["w_kv"])], w_out=[rows(got["w_out"])], a_in=slots(got["a_in"]))
    sm = gathered[-1]
    p["b_conv_w"] = jnp.transpose(sm[:, 0], (1, 0, 2)).reshape(CONV_K, CONV_DIM)
    p["b_conv_b"] = sm[:, 1, 0].reshape(1, CONV_DIM)
    p["b_gnorm"] = sm[:, 1, 1, :GN_SHARD].reshape(1, D_INNER)

    after, started = (gathered[0],), {}
    for tag, layer, names in (("0_ffn", 0, ("w_ffn1", "w_ffn2")), ("1_mixer", 1, ("w_kv", "b_in")),
                              ("1_rest", 1, ("w_out", "w_ffn1", "w_ffn2"))):
        started[tag] = _gather_start([halves(k, layer) for k in names], after, name=f"gather_start_{tag}")
        after = (started[tag][-1],)
    p["after_start"] = after

    def finish(tag, first):
        send_sems, recv_sems, shards, zones, _ = started[tag]
        shards, zones = _gather_wait(send_sems, recv_sems, shards, zones, (first,), name=f"gather_wait_{tag}")
        return _gather_finish(shards, zones, name=f"gather_finish_{tag}")

    def layer0_ffn(first):
        w1, w2 = finish("0_ffn", first)
        return slots(w1), rows(w2)

    def layer1_mixer(first):
        kv, b_in = finish("1_mixer", first)
        return slots(kv), _b_in_full(slots(b_in))

    def layer1_rest(first):
        wo, w1, w2 = finish("1_rest", first)
        return rows(wo), slots(w1), rows(w2)

    p.update(layer0_ffn=layer0_ffn, layer1_mixer=layer1_mixer, layer1_rest=layer1_rest)
    return p


def _pair_parts(grads, tag):
    stacks = [g.reshape(2, -1, g.shape[-1]) for g in grads.values()]
    parts = _pair_reduce(stacks, name=f"grads_pair_reduce_{tag}")
    return [t.reshape(N_CHIPS, -1, t.shape[-1]) for t in parts]


def _chip_sums(chip, names, parts, landed, tag):
    return {k: _sum_contributions(chip, t, u, name=f"grads_chip_sum_{k}_{tag}")
            for k, t, u in zip(names, parts, landed)}


def _small_layout(shapes):
    offs, o = {}, 0
    for k in (*SMALL_REPL, *SMALL_SHARD):
        size = math.prod(shapes[k])
        offs[k] = (o, size)
        o += size
    rows = -(-(o + 1) // (8 * 128)) * 8
    return offs, rows


def _reduce_small(g, loss_part, full_shapes):
    offs, rows = _small_layout(full_shapes)
    flat = jnp.concatenate([*[g[k].reshape(-1) for k in (*SMALL_REPL, *SMALL_SHARD)], loss_part[0, :1]])
    flat = jnp.pad(flat, (0, rows * 128 - flat.shape[0])).reshape(rows, 128)
    total = _all_reduce_small(flat, name="small_all_reduce").reshape(-1)
    end = max(o + n for o, n in offs.values())
    return {k: total[o:o + n].reshape(full_shapes[k]) for k, (o, n) in offs.items()}, total[end]


def kernel(x, mem, norm_mix, norm_ffn, mem_norm, w_kv, w_out, w_ffn1, w_ffn2, a_in, a_ln_g, a_ln_b, a_ws, a_bs, b_in, b_conv_w, b_conv_b, b_dt_bias, b_a_log, b_d, b_gnorm, final_norm, loss_target, m_norm_mix, m_norm_ffn, m_mem_norm, m_w_kv, m_w_out, m_w_ffn1, m_w_ffn2, m_a_in, m_a_ln_g, m_a_ln_b, m_a_ws, m_a_bs, m_b_in, m_b_conv_w, m_b_conv_b, m_b_dt_bias, m_b_a_log, m_b_d, m_b_gnorm, m_final_norm, v_norm_mix, v_norm_ffn, v_mem_norm, v_w_kv, v_w_out, v_w_ffn1, v_w_ffn2, v_a_in, v_a_ln_g, v_a_ln_b, v_a_ws, v_a_bs, v_b_in, v_b_conv_w, v_b_conv_b, v_b_dt_bias, v_b_a_log, v_b_d, v_b_gnorm, v_final_norm):
    w = dict(norm_mix=norm_mix, norm_ffn=norm_ffn, mem_norm=mem_norm, w_kv=w_kv, w_out=w_out, w_ffn1=w_ffn1,
             w_ffn2=w_ffn2, a_in=a_in, a_ln_g=a_ln_g, a_ln_b=a_ln_b, a_ws=a_ws, a_bs=a_bs, b_in=b_in, b_conv_w=b_conv_w,
             b_conv_b=b_conv_b, b_dt_bias=b_dt_bias, b_a_log=b_a_log, b_d=b_d, b_gnorm=b_gnorm, final_norm=final_norm)
    mom = dict(norm_mix=m_norm_mix, norm_ffn=m_norm_ffn, mem_norm=m_mem_norm, w_kv=m_w_kv, w_out=m_w_out,
               w_ffn1=m_w_ffn1, w_ffn2=m_w_ffn2, a_in=m_a_in, a_ln_g=m_a_ln_g, a_ln_b=m_a_ln_b, a_ws=m_a_ws,
               a_bs=m_a_bs, b_in=m_b_in, b_conv_w=m_b_conv_w, b_conv_b=m_b_conv_b, b_dt_bias=m_b_dt_bias,
               b_a_log=m_b_a_log, b_d=m_b_d, b_gnorm=m_b_gnorm, final_norm=m_final_norm)
    var = dict(norm_mix=v_norm_mix, norm_ffn=v_norm_ffn, mem_norm=v_mem_norm, w_kv=v_w_kv, w_out=v_w_out,
               w_ffn1=v_w_ffn1, w_ffn2=v_w_ffn2, a_in=v_a_in, a_ln_g=v_a_ln_g, a_ln_b=v_a_ln_b, a_ws=v_a_ws,
               a_bs=v_a_bs, b_in=v_b_in, b_conv_w=v_b_conv_w, b_conv_b=v_b_conv_b, b_dt_bias=v_b_dt_bias,
               b_a_log=v_b_a_log, b_d=v_b_d, b_gnorm=v_b_gnorm, final_norm=v_final_norm)

    p = _gather_weights(w)
    p.update(norm_mix=norm_mix, norm_ffn=norm_ffn, mem_norm=mem_norm, a_ln_g=a_ln_g, a_ln_b=a_ln_b, a_ws=a_ws[0],
             a_bs=a_bs[0], b_dt_bias=b_dt_bias, b_a_log=b_a_log, b_d=b_d, final_norm=final_norm)
    chip = 2 * lax.axis_index("x") + lax.axis_index("y")
    chip_arr = jnp.reshape(chip, (1,)).astype(jnp.int32)
    started = {}

    def start_scatter(tag):
        def hook(grads):
            start = _chip_scatter_start(_pair_parts(grads, tag), name=f"grads_chip_scatter_start_{tag}")
            started[tag] = (tuple(grads), start)
            return start[-1]
        return hook

    loss_part, dx, g, _, _ = _local_step(x[0], mem[0], loss_target[0], p, start_scatter("1"), start_scatter("0f"),
                                         start_scatter("0m"))
    full_shapes = {k: w[k].shape for k in SMALL_REPL}
    full_shapes.update(b_conv_w=(1, CONV_K, CONV_DIM), b_conv_b=(1, CONV_DIM), b_gnorm=(1, D_INNER))
    grads, loss = _reduce_small(g, loss_part, full_shapes)
    grads["b_conv_w"] = lax.dynamic_slice_in_dim(grads["b_conv_w"], chip * CONV_SHARD, CONV_SHARD, axis=2)
    grads["b_conv_b"] = lax.dynamic_slice_in_dim(grads["b_conv_b"], chip * CONV_SHARD, CONV_SHARD, axis=1)
    grads["b_gnorm"] = lax.dynamic_slice_in_dim(grads["b_gnorm"], chip * GN_SHARD, GN_SHARD, axis=1)

    def finish_scatter(tag, *first):
        names, (send_sems, recv_sems, parts, lands, _) = started[tag]
        parts, landed = _chip_scatter_wait(send_sems, recv_sems, parts, lands, first,
                                           name=f"grads_chip_scatter_wait_{tag}")
        return _chip_sums(chip_arr, names, parts, landed, tag)

    def adamw(names, grads):
        for k in names:
            shape = w[k].shape
            if len(shape) == 3 and shape[2] % 128 and not shape[1] % 128:
                flat = unflat = lambda a: jnp.transpose(a, (0, 2, 1))
            else:
                flat = (lambda a: a) if len(shape) == 3 else (lambda a: a.reshape(1, -1, shape[-1]))
                unflat = lambda a: a.reshape(shape)
            d, m_new, v_new = _adamw(flat(w[k]), flat(grads[k]), flat(mom[k]), flat(var[k]), name=f"adamw_{k}")
            delta[k], new_m[k], new_v[k] = unflat(d), unflat(m_new), unflat(v_new)

    delta, new_m, new_v = {}, {}, {}
    halves = [finish_scatter("0f", dx), finish_scatter("1", dx)]
    early = ("w_ffn1", "w_ffn2", "b_in")
    shared = _pair_share([[halves[layer][k] for layer in range(2) if k in halves[layer]] for k in early],
                         name="grads_pair_share_early")
    grads.update({k: a.reshape(w[k].shape) for k, a in zip(early, shared)})
    adamw([k for k in WEIGHTS if k in grads], grads)
    halves[0].update(finish_scatter("0m", delta["w_ffn2"]))
    late = ("w_kv", "w_out", "a_in")
    shared = _pair_share([[halves[layer][k] for layer in range(2) if k in halves[layer]] for k in late],
                         name="grads_pair_share_late")
    grads.update({k: a.reshape(w[k].shape) for k, a in zip(late, shared)})
    adamw(late, grads)

    return (loss, dx.reshape(x.shape), *[grads[k] for k in WEIGHTS], *[delta[k] for k in WEIGHTS],
            *[new_m[k] for k in WEIGHTS], *[new_v[k] for k in WEIGHTS])
```

```python
import math

import jax
import jax.numpy as jnp
from jax import lax
from jax.experimental import pallas as pl
from jax.experimental.pallas import tpu as pltpu

F32 = jnp.float32
BF16 = jnp.bfloat16
SDS = jax.ShapeDtypeStruct

D_MODEL = 1024
SEQ = 2048
CHUNK = 128
N_MEM = 256
D_INNER = 2048
A_GROUPS = 8
A_GROUP_W = D_INNER // A_GROUPS
SSM_HEADS = 32
SSM_HEAD_DIM = 64
SSM_GROUPS = 4
SSM_HPG = 8
SSM_STATE = 128
SSM_GROUP_W = SSM_HPG * SSM_HEAD_DIM
CONV_K = 4
CONV_DIM = 3072
X_HEADS = 4
X_HEAD_DIM = 256
X_WIDTH = 1024
MIX_OUT = 3072
D_FF = 4096
A_IN = 5120
B_IN = 6176
B_IN_PAD = 6272
B_Q_OFF = 5120
B_DT_OFF = 6144
N_CHUNKS = SEQ // CHUNK
EPS = 1e-6
N_CHIPS = 4

ADAM_LR = 0.001
ADAM_B1 = 0.9
ADAM_B2 = 0.999
ADAM_EPS = 1e-08
ADAM_WD = 0.01
ADAM_STEP = 10

VMEM_LIMIT = 48 * 1024 * 1024
MESH = pl.DeviceIdType.MESH


def _cparams(sem):
    return pltpu.CompilerParams(dimension_semantics=sem, vmem_limit_bytes=VMEM_LIMIT)


def _dot(a, b, dims=(((1,), (0,)), ((), ()))):
    return lax.dot_general(a.astype(BF16), b.astype(BF16), dims, preferred_element_type=F32)


def _dot_nt(a, b):
    return _dot(a, b, (((1,), (1,)), ((), ())))


def _dot_tn(a, b):
    return _dot(a, b, (((0,), (0,)), ((), ())))


def _pick(n, cands):
    for c in cands:
        if n % c == 0:
            return c
    raise ValueError(f"no tile for {n}")


MM_RING = 3


def _mm_call(a, b, *, dims, grid, a_spec, b_spec, acc_shape, out_shapes, out_specs, name,
             extras=(), extra_specs=(), epilogue=None, after=()):
    _, n_j, n_k = grid
    steps = grid[0] * n_j * n_k
    n_extra = len(extras)
    n_out = len(out_shapes)
    n_in = 2 + n_extra + len(after)
    b_block = tuple(size for size in b_spec.block_shape if size is not None)

    def b_copy(b_ref, ring, sems, step):
        idx = b_spec.index_map(step // (n_j * n_k), (step // n_k) % n_j, step % n_k)
        window = tuple(ix if size is None else pl.ds(ix * size, size) for ix, size in zip(idx, b_spec.block_shape))
        slot = step % MM_RING
        return pltpu.make_async_copy(b_ref.at[window], ring.at[slot], sems.at[slot])

    def b_block_now(refs):
        b_ref, ring, sems = refs[1], refs[n_in + n_out], refs[n_in + n_out + 1]
        step = (pl.program_id(0) * n_j + pl.program_id(1)) * n_k + pl.program_id(2)

        @pl.when(step == 0)
        def _():
            for first in range(min(MM_RING - 1, steps)):
                b_copy(b_ref, ring, sems, first).start()

        @pl.when(step + MM_RING - 1 < steps)
        def _():
            b_copy(b_ref, ring, sems, step + MM_RING - 1).start()

        b_copy(b_ref, ring, sems, step).wait()
        return ring[step % MM_RING]

    def finish(total, extra_refs, out_refs):
        vals = (total,) if epilogue is None else epilogue(total, *[e[...] for e in extra_refs])
        for o_ref, v in zip(out_refs, vals):
            o_ref[...] = v.astype(o_ref.dtype)

    def body_one_step(*refs):
        finish(_dot(refs[0][...], b_block_now(refs), dims), refs[2:2 + n_extra], refs[n_in:n_in + n_out])

    def body(*refs):
        acc = refs[-1]
        k = pl.program_id(2)

        @pl.when(k == 0)
        def _():
            acc[...] = jnp.zeros_like(acc)

        acc[...] += _dot(refs[0][...], b_block_now(refs), dims)

        @pl.when(k == n_k - 1)
        def _():
            finish(acc[...], refs[2:2 + n_extra], refs[n_in:n_in + n_out])

    ring = [pltpu.VMEM((MM_RING, *b_block), b.dtype), pltpu.SemaphoreType.DMA((MM_RING,))]
    return pl.pallas_call(
        body_one_step if n_k == 1 else body, grid=grid,
        in_specs=[a_spec, ANY, *extra_specs, *([ANY] * len(after))], out_specs=list(out_specs),
        out_shape=list(out_shapes), scratch_shapes=ring + ([] if n_k == 1 else [pltpu.VMEM(acc_shape, F32)]),
        compiler_params=_cparams(("arbitrary", "arbitrary", "arbitrary")), name=name,
    )(a, b, *extras, *after)


def _w_dims(w):
    if w.ndim == 2:
        return w.shape[0], w.shape[1], 1, w.shape[1]
    return w.shape[1], w.shape[0] * w.shape[2], w.shape[0], w.shape[2]


def _mm_nn(a, w, *, name, out_dtype=F32, a_cols=None, extras=(), epilogue=None, n_out_dtypes=None, after=()):
    m = a.shape[0]
    k_dim, n_dim, _, n_slot = _w_dims(w)
    a_off, a_w = (0, a.shape[1]) if a_cols is None else a_cols
    assert a_w == k_dim
    tm = _pick(m, (2048, 1024, 512, 256))
    tn = _pick(n_slot, (512, 896, 640, 256, 128))
    tk = _pick(k_dim, (1024, 768, 512, 384, 256, 128))
    assert a_off % tk == 0
    nb = n_slot // tn
    a_spec = pl.BlockSpec((tm, tk), lambda i, j, k: (i, a_off // tk + k))
    if w.ndim == 2:
        b_spec = pl.BlockSpec((tk, tn), lambda i, j, k: (k, j))
    else:
        b_spec = pl.BlockSpec((None, tk, tn), lambda i, j, k: (j // nb, k, j % nb))
    o_spec = pl.BlockSpec((tm, tn), lambda i, j, k: (i, j))
    dts = n_out_dtypes or (out_dtype,)
    outs = _mm_call(a, w, dims=(((1,), (0,)), ((), ())), grid=(m // tm, n_dim // tn, k_dim // tk),
                    a_spec=a_spec, b_spec=b_spec, acc_shape=(tm, tn),
                    out_shapes=[SDS((m, n_dim), dt) for dt in dts], out_specs=[o_spec] * len(dts), name=name,
                    extras=extras, extra_specs=[o_spec] * len(extras), epilogue=epilogue, after=after)
    return outs if n_out_dtypes else outs[0]


def _mm_nt(a, w, *, name, out_dtype=F32, extras=(), epilogue=None, after=()):
    m = a.shape[0]
    k_dim, n_dim, _, n_slot = _w_dims(w)
    assert a.shape[1] == n_dim
    tm = _pick(m, (2048, 1024, 512, 256))
    to = _pick(k_dim, (512, 384, 256, 128))
    tc = _pick(n_slot, (1280, 1024, 896, 640, 512, 256, 128))
    nb = n_slot // tc
    a_spec = pl.BlockSpec((tm, tc), lambda i, j, k: (i, k))
    if w.ndim == 2:
        b_spec = pl.BlockSpec((to, tc), lambda i, j, k: (j, k))
    else:
        b_spec = pl.BlockSpec((None, to, tc), lambda i, j, k: (k // nb, j, k % nb))
    o_spec = pl.BlockSpec((tm, to), lambda i, j, k: (i, j))
    return _mm_call(a, w, dims=(((1,), (1,)), ((), ())), grid=(m // tm, k_dim // to, n_dim // tc),
                    a_spec=a_spec, b_spec=b_spec, acc_shape=(tm, to),
                    out_shapes=[SDS((m, k_dim), out_dtype)], out_specs=[o_spec], name=name,
                    extras=extras, extra_specs=[o_spec] * len(extras), epilogue=epilogue, after=after)[0]


def _mm_tn(x, dy, *, name, x_cols=None):
    s = x.shape[0]
    x_off, k_dim = (0, x.shape[1]) if x_cols is None else x_cols
    n_dim = dy.shape[1]
    tm = _pick(k_dim, (1024, 768, 512, 384, 256, 128))
    tn = _pick(n_dim, (512, 896, 640, 256, 128))
    tk = _pick(s, (2048, 1024, 512, 256))
    assert x_off % tm == 0
    a_spec = pl.BlockSpec((tk, tm), lambda i, j, k: (k, x_off // tm + i))
    b_spec = pl.BlockSpec((tk, tn), lambda i, j, k: (k, j))
    o_spec = pl.BlockSpec((tm, tn), lambda i, j, k: (i, j))
    return _mm_call(x, dy, dims=(((0,), (0,)), ((), ())), grid=(k_dim // tm, n_dim // tn, s // tk),
                    a_spec=a_spec, b_spec=b_spec, acc_shape=(tm, tn),
                    out_shapes=[SDS((k_dim, n_dim), F32)], out_specs=[o_spec], name=name)[0]


def _mm_tn_stacked(x, dy, *, name, col_slots):
    s, k_dim = x.shape
    n_dim = dy.shape[1]
    r, c = (k_dim // 2, n_dim // N_CHIPS) if col_slots else (k_dim // N_CHIPS // 2, n_dim)
    tm = 2 * r
    tn = _pick(c, (512, 896, 640, 256, 128))
    tk = _pick(s, (2048, 1024, 512, 256))
    a_spec = pl.BlockSpec((tk, tm), lambda i, j, k: (k, i))
    b_spec = pl.BlockSpec((tk, tn), lambda i, j, k: (k, j))
    if col_slots:
        nb = c // tn
        o_spec = pl.BlockSpec((2, None, r, tn), lambda i, j, k: (0, j // nb, 0, j % nb))
    else:
        o_spec = pl.BlockSpec((2, None, r, tn), lambda i, j, k: (0, i, 0, j))
    return _mm_call(x, dy, dims=(((0,), (0,)), ((), ())), grid=(k_dim // tm, n_dim // tn, s // tk),
                    a_spec=a_spec, b_spec=b_spec, acc_shape=(tm, tn), epilogue=lambda acc: (acc.reshape(2, r, tn),),
                    out_shapes=[SDS((2, N_CHIPS, r, c), F32)], out_specs=[o_spec], name=name)[0]


def _rms(x, g):
    return x * lax.rsqrt(jnp.mean(x * x, axis=-1, keepdims=True) + EPS) * g


def _rms_fwd(h, g, *, name):
    rows, d = h.shape
    tr = _pick(rows, (512, 256))

    def body(h_ref, g_ref, o_ref):
        o_ref[...] = _rms(h_ref[...], g_ref[...]).astype(o_ref.dtype)

    return pl.pallas_call(
        body, grid=(rows // tr,),
        in_specs=[pl.BlockSpec((tr, d), lambda i: (i, 0)), pl.BlockSpec((1, d), lambda i: (0, 0))],
        out_specs=pl.BlockSpec((tr, d), lambda i: (i, 0)), out_shape=SDS((rows, d), BF16),
        compiler_params=_cparams(("parallel",)), name=name)(h, g)


def _rms_bwd(h, g, da, dres, *, name):
    rows, d = h.shape
    tr = _pick(rows, (512, 256))

    def body(h_ref, g_ref, da_ref, dres_ref, dh_ref, dg_ref):
        _, vjp = jax.vjp(_rms, h_ref[...], g_ref[...])
        dh, dg = vjp(da_ref[...].astype(F32))
        dh_ref[...] = dres_ref[...] + dh

        @pl.when(pl.program_id(0) == 0)
        def _():
            dg_ref[...] = jnp.zeros_like(dg_ref)

        dg_ref[...] += dg

    row_spec = pl.BlockSpec((tr, d), lambda i: (i, 0))
    vec_spec = pl.BlockSpec((1, d), lambda i: (0, 0))
    return pl.pallas_call(
        body, grid=(rows // tr,), in_specs=[row_spec, vec_spec, row_spec, row_spec],
        out_specs=[row_spec, vec_spec], out_shape=[SDS((rows, d), F32), SDS((1, d), F32)],
        compiler_params=_cparams(("arbitrary",)), name=name)(h, g, da, dres)


def _loss_head(h, g, target, *, name):
    rows, d = h.shape
    tr = _pick(rows, (512, 256))

    def body(h_ref, g_ref, t_ref, loss_ref, dh_ref, dg_ref):
        y, vjp = jax.vjp(_rms, h_ref[...], g_ref[...])
        err = y - t_ref[...]
        dh, dg = vjp(err * (1.0 / d))
        dh_ref[...] = dh

        @pl.when(pl.program_id(0) == 0)
        def _():
            dg_ref[...] = jnp.zeros_like(dg_ref)
            loss_ref[...] = jnp.zeros_like(loss_ref)

        dg_ref[...] += dg
        part = jnp.sum(jnp.sum(err * err, axis=-1, keepdims=True), axis=0, keepdims=True) * (0.5 / d)
        loss_ref[...] += jnp.broadcast_to(part, loss_ref.shape)

    row_spec = pl.BlockSpec((tr, d), lambda i: (i, 0))
    vec_spec = pl.BlockSpec((1, d), lambda i: (0, 0))
    loss_spec = pl.BlockSpec((8, 128), lambda i: (0, 0))
    return pl.pallas_call(
        body, grid=(rows // tr,), in_specs=[row_spec, vec_spec, row_spec],
        out_specs=[loss_spec, row_spec, vec_spec],
        out_shape=[SDS((8, 128), F32), SDS((rows, d), F32), SDS((1, d), F32)],
        compiler_params=_cparams(("arbitrary",)), name=name)(h, g, target)


def _gelu(x):
    return 0.5 * x * (1.0 + lax.erf(x * (1.0 / math.sqrt(2.0))))


def _gate_tile(pu, pv, ln_g, ln_b, ws, bs_t):
    u = [_gelu(p) for p in pu]
    v = [_gelu(p) for p in pv]
    mu = sum(jnp.sum(t, axis=-1, keepdims=True) for t in v) * (1.0 / D_INNER)
    vc = [t - mu for t in v]
    var = sum(jnp.sum(t * t, axis=-1, keepdims=True) for t in vc) * (1.0 / D_INNER)
    rstd = lax.rsqrt(var + EPS)
    row = lax.broadcasted_iota(jnp.int32, (CHUNK, CHUNK), 0)
    col = lax.broadcasted_iota(jnp.int32, (CHUNK, CHUNK), 1)
    out = []
    for gi in range(A_GROUPS):
        vn = vc[gi] * rstd * ln_g[gi] + ln_b[gi]
        w = jnp.where(row >= col, ws[gi], 0.0)
        sv = _dot(w, vn) + bs_t[gi]
        out.append(u[gi] * sv)
    return out


def _split(ref, n, width):
    return [ref[:, i * width:(i + 1) * width] for i in range(n)]


def _gate_in_specs():
    return [
        pl.BlockSpec((CHUNK, D_INNER), lambda c: (c, 0)),
        pl.BlockSpec((CHUNK, D_INNER), lambda c: (c, 1)),
        pl.BlockSpec((1, D_INNER), lambda c: (0, 0)),
        pl.BlockSpec((1, D_INNER), lambda c: (0, 0)),
        pl.BlockSpec((A_GROUPS, CHUNK, CHUNK), lambda c: (0, 0, 0)),
        pl.BlockSpec((A_GROUPS, CHUNK, 1), lambda c: (0, 0, 0)),
    ]


def _gate_args(u_ref, v_ref, g_ref, b_ref, ws_ref, bs_ref):
    ng, gw = A_GROUPS, A_GROUP_W
    return (_split(u_ref, ng, gw), _split(v_ref, ng, gw), _split(g_ref, ng, gw), _split(b_ref, ng, gw),
            [ws_ref[i] for i in range(ng)], [bs_ref[i] for i in range(ng)])


def _gate_fwd(proj, ln_g, ln_b, ws, bs_col, mixcat, *, name):
    def body(u_ref, v_ref, g_ref, b_ref, ws_ref, bs_ref, cat_in, cat_ref):
        del cat_in
        out = _gate_tile(*_gate_args(u_ref, v_ref, g_ref, b_ref, ws_ref, bs_ref))
        for gi, o in enumerate(out):
            cat_ref[:, gi * A_GROUP_W:(gi + 1) * A_GROUP_W] = o.astype(cat_ref.dtype)

    return pl.pallas_call(
        body, grid=(N_CHUNKS,), in_specs=[*_gate_in_specs(), pl.BlockSpec(memory_space=pl.ANY)],
        out_specs=pl.BlockSpec((CHUNK, D_INNER), lambda c: (c, 0)), out_shape=SDS(mixcat.shape, mixcat.dtype),
        input_output_aliases={6: 0}, compiler_params=_cparams(("parallel",)), name=name,
    )(proj, proj, ln_g, ln_b, ws, bs_col, mixcat)


def _gate_bwd(proj, ln_g, ln_b, ws, bs_col, dcat, dproj, *, name):
    ng, gw = A_GROUPS, A_GROUP_W

    def body(u_ref, v_ref, g_ref, b_ref, ws_ref, bs_ref, d_ref, dproj_in, dproj_ref, dg_ref, db_ref, dws_ref, dbs_ref):
        del dproj_in
        args = _gate_args(u_ref, v_ref, g_ref, b_ref, ws_ref, bs_ref)
        _, vjp = jax.vjp(_gate_tile, *args)
        dpu, dpv, dg, db, dws, dbs = vjp(_split(d_ref, ng, gw))
        for gi in range(ng):
            dproj_ref[:, gi * gw:(gi + 1) * gw] = dpu[gi].astype(dproj_ref.dtype)
            dproj_ref[:, D_INNER + gi * gw:D_INNER + (gi + 1) * gw] = dpv[gi].astype(dproj_ref.dtype)

        @pl.when(pl.program_id(0) == 0)
        def _():
            for r in (dg_ref, db_ref, dws_ref, dbs_ref):
                r[...] = jnp.zeros_like(r)

        for gi in range(ng):
            dg_ref[:, gi * gw:(gi + 1) * gw] += dg[gi]
            db_ref[:, gi * gw:(gi + 1) * gw] += db[gi]
            dws_ref[gi] += dws[gi]
            dbs_ref[gi] += dbs[gi]

    in_specs = _gate_in_specs()
    return pl.pallas_call(
        body, grid=(N_CHUNKS,),
        in_specs=[*in_specs, pl.BlockSpec((CHUNK, D_INNER), lambda c: (c, 0)), pl.BlockSpec(memory_space=pl.ANY)],
        out_specs=[pl.BlockSpec((CHUNK, 2 * D_INNER), lambda c: (c, 0)), *in_specs[2:]],
        out_shape=[SDS(dproj.shape, dproj.dtype), SDS((1, D_INNER), F32), SDS((1, D_INNER), F32),
                   SDS((ng, CHUNK, CHUNK), F32), SDS((ng, CHUNK, 1), F32)],
        input_output_aliases={7: 0}, compiler_params=_cparams(("arbitrary",)), name=name,
    )(proj, proj, ln_g, ln_b, ws, bs_col, dcat, dproj)


ATT_TQ = 2048


def _attn_tile(q, k, v):
    s = _dot_nt(q, k) * (1.0 / math.sqrt(X_HEAD_DIM))
    s = s - jnp.max(s, axis=-1, keepdims=True)
    e = jnp.exp(s)
    p = e / jnp.sum(e, axis=-1, keepdims=True)
    return _dot(p, v)


def _attn_in_specs(q_blk, order):
    hd = X_HEAD_DIM
    return [
        pl.BlockSpec((ATT_TQ, hd), lambda a, b: (order(a, b)[0], q_blk + order(a, b)[1])),
        pl.BlockSpec((N_MEM, hd), lambda a, b: (0, order(a, b)[1])),
        pl.BlockSpec((N_MEM, hd), lambda a, b: (0, X_HEADS + order(a, b)[1])),
    ]


def _attn_fwd(proj, q_off, kv, *, name):
    order = lambda i, h: (i, h)
    cat_blk = D_INNER // X_HEAD_DIM

    def body(q_ref, k_ref, v_ref, o_ref):
        o_ref[...] = _attn_tile(q_ref[...], k_ref[...], v_ref[...]).astype(o_ref.dtype)

    return pl.pallas_call(
        body, grid=(SEQ // ATT_TQ, X_HEADS), in_specs=_attn_in_specs(q_off // X_HEAD_DIM, order),
        out_specs=pl.BlockSpec((ATT_TQ, X_HEAD_DIM), lambda i, h: (i, cat_blk + h)),
        out_shape=SDS((SEQ, MIX_OUT), BF16), compiler_params=_cparams(("parallel", "parallel")), name=name,
    )(proj, kv, kv)


def _attn_bwd(proj, q_off, kv, dcat, dproj_width, dq_off, *, name):
    order = lambda h, i: (i, h)
    cat_blk = D_INNER // X_HEAD_DIM
    dq_blk = dq_off // X_HEAD_DIM

    def body(q_ref, k_ref, v_ref, do_ref, dq_ref, dk_ref, dv_ref):
        _, vjp = jax.vjp(_attn_tile, q_ref[...], k_ref[...], v_ref[...])
        dq, dk, dv = vjp(do_ref[...])
        dq_ref[...] = dq.astype(dq_ref.dtype)

        @pl.when(pl.program_id(1) == 0)
        def _():
            dk_ref[...] = jnp.zeros_like(dk_ref)
            dv_ref[...] = jnp.zeros_like(dv_ref)

        dk_ref[...] += dk
        dv_ref[...] += dv

    kv_spec = pl.BlockSpec((N_MEM, X_HEAD_DIM), lambda h, i: (0, h))
    return pl.pallas_call(
        body, grid=(X_HEADS, SEQ // ATT_TQ),
        in_specs=[*_attn_in_specs(q_off // X_HEAD_DIM, order),
                  pl.BlockSpec((ATT_TQ, X_HEAD_DIM), lambda h, i: (i, cat_blk + h))],
        out_specs=[pl.BlockSpec((ATT_TQ, X_HEAD_DIM), lambda h, i: (i, dq_blk + h)), kv_spec, kv_spec],
        out_shape=[SDS((SEQ, dproj_width), BF16), SDS((N_MEM, X_WIDTH), F32), SDS((N_MEM, X_WIDTH), F32)],
        compiler_params=_cparams(("parallel", "arbitrary")), name=name,
    )(proj, kv, kv, dcat)


CONV_TC = 512
CONV_ROWS = 128
CONV_HALO = 8


def _shift_down(x, s):
    if s == 0:
        return x
    row = lax.broadcasted_iota(jnp.int32, x.shape, 0)
    return jnp.where(row >= s, pltpu.roll(x, s, 0), 0.0)


def _shift_up(x, s):
    if s == 0:
        return x
    n = x.shape[0]
    row = lax.broadcasted_iota(jnp.int32, x.shape, 0)
    return jnp.where(row < n - s, pltpu.roll(x, n - s, 0), 0.0)


def _conv_pre(x, w_ref, b_ref):
    pre = b_ref[...] + jnp.zeros_like(x)
    for k in range(CONV_K):
        pre = pre + w_ref[k:k + 1, :] * _shift_down(x, CONV_K - 1 - k)
    return pre


def _conv_fwd(proj, w, b, *, name):
    blk0 = D_INNER // CONV_TC

    def body(x_ref, w_ref, b_ref, o_ref):
        pre = _conv_pre(x_ref[...], w_ref, b_ref)
        o_ref[...] = pre * jax.nn.sigmoid(pre)

    return pl.pallas_call(
        body, grid=(CONV_DIM // CONV_TC,),
        in_specs=[pl.BlockSpec((SEQ, CONV_TC), lambda j: (0, blk0 + j)), pl.BlockSpec((CONV_K, CONV_TC), lambda j: (0, j)),
                  pl.BlockSpec((1, CONV_TC), lambda j: (0, j))],
        out_specs=pl.BlockSpec((SEQ, CONV_TC), lambda j: (0, j)), out_shape=SDS((SEQ, CONV_DIM), F32),
        compiler_params=_cparams(("parallel",)), name=name)(proj, w, b)


def _conv_bwd(proj, w, b, dxs, dbm, dcm, dproj, *, name):
    tc = CONV_TC // 2
    blk0 = D_INNER // tc
    n_x = D_INNER // tc
    n_b = SSM_GROUPS * SSM_STATE // tc

    window = CONV_ROWS + 2 * CONV_HALO
    n_rows = SEQ // CONV_ROWS

    def body(x_ref, w_ref, b_ref, dxs_ref, dbm_ref, dcm_ref, dproj_in, dproj_ref, dw_ref, db_ref):
        del dproj_in
        j = pl.program_id(0)

        def rows_step(lanes, first, keep, sums):
            start = first - keep
            if not isinstance(start, int):
                start = pl.multiple_of(start, CONV_HALO)
            rows = pl.ds(start, window)
            down = _shift_down if keep == 0 else lambda v, s: pltpu.roll(v, s, 0) if s else v
            up = _shift_up if keep == 2 * CONV_HALO else lambda v, s: pltpu.roll(v, window - s, 0) if s else v
            x = x_ref[rows, lanes]
            w = w_ref[:, lanes]
            pre = b_ref[:, lanes] + jnp.zeros_like(x)
            for k in range(CONV_K):
                pre = pre + w[k:k + 1, :] * down(x, CONV_K - 1 - k)
            sg = jax.nn.sigmoid(pre)
            dact = jnp.where(j < n_x, dxs_ref[rows, lanes],
                             jnp.where(j < n_x + n_b, dbm_ref[rows, lanes], dcm_ref[rows, lanes]))
            dpre = dact * (sg * (1.0 + pre * (1.0 - sg)))
            kept = lambda v: jnp.sum(v[keep:keep + CONV_ROWS, :], axis=0, keepdims=True)
            dx = jnp.zeros_like(x)
            new_sums = []
            for k in range(CONV_K):
                s = CONV_K - 1 - k
                dx = dx + w[k:k + 1, :] * up(dpre, s)
                new_sums.append(sums[k] + kept(dpre * down(x, s)))
            new_sums.append(sums[CONV_K] + kept(dpre))
            dproj_ref[pl.ds(first, CONV_ROWS), lanes] = dx[keep:keep + CONV_ROWS, :].astype(dproj_ref.dtype)
            return tuple(new_sums)

        for c in range(tc // 128):
            lanes = pl.ds(c * 128, 128)
            sums = rows_step(lanes, 0, 0, (jnp.zeros((1, 128), F32),) * (CONV_K + 1))
            sums = lax.fori_loop(
                1, n_rows - 1,
                lambda i, sums: rows_step(lanes, pl.multiple_of(i * CONV_ROWS, CONV_ROWS), CONV_HALO, sums), sums)
            sums = rows_step(lanes, SEQ - CONV_ROWS, 2 * CONV_HALO, sums)
            for k in range(CONV_K):
                dw_ref[k:k + 1, lanes] = sums[k]
            db_ref[:, lanes] = sums[CONV_K]

    clip = lambda v, hi: jnp.minimum(jnp.maximum(v, 0), hi)
    return pl.pallas_call(
        body, grid=(CONV_DIM // tc,),
        in_specs=[pl.BlockSpec((SEQ, tc), lambda j: (0, blk0 + j)), pl.BlockSpec((CONV_K, tc), lambda j: (0, j)),
                  pl.BlockSpec((1, tc), lambda j: (0, j)),
                  pl.BlockSpec((SEQ, tc), lambda j: (0, clip(j, n_x - 1))),
                  pl.BlockSpec((SEQ, tc), lambda j: (0, clip(j - n_x, n_b - 1))),
                  pl.BlockSpec((SEQ, tc), lambda j: (0, clip(j - n_x - n_b, n_b - 1))),
                  pl.BlockSpec(memory_space=pl.ANY)],
        out_specs=[pl.BlockSpec((SEQ, tc), lambda j: (0, blk0 + j)), pl.BlockSpec((CONV_K, tc), lambda j: (0, j)),
                   pl.BlockSpec((1, tc), lambda j: (0, j))],
        out_shape=[SDS(dproj.shape, dproj.dtype), SDS((CONV_K, CONV_DIM), F32), SDS((1, CONV_DIM), F32)],
        input_output_aliases={6: 0}, compiler_params=_cparams(("parallel",)), name=name,
    )(proj, w, b, dxs, dbm, dcm, dproj)


SSM_PAIRS = SSM_HPG // 2


def _dot_exact01(x, m01, m01_t, x_first, differentiable):
    def product(v, m):
        hi = v.astype(BF16)
        rest = v - hi.astype(F32)
        mid = rest.astype(BF16)
        lo = (rest - mid.astype(F32)).astype(BF16)
        dims = (((1,), (0,)), ((), ()))
        dot = lambda part: lax.dot_general(*((part, m) if x_first else (m, part)), dims, preferred_element_type=F32)
        return dot(hi) + dot(mid) + dot(lo)

    if not differentiable:
        return product(x, m01)

    @jax.custom_vjp
    def exact(v):
        return product(v, m01)

    exact.defvjp(lambda v: (product(v, m01), None), lambda _, ct: (product(ct, m01_t),))
    return exact(x)


def _ssd_tile(xp, zp, bm, cm, hp, dt_c, dt_r, bias, bias_col, alog, alog_col, dsk, gnp, differentiable=False):
    row = lax.broadcasted_iota(jnp.int32, (CHUNK, CHUNK), 0)
    col = lax.broadcasted_iota(jnp.int32, (CHUNK, CHUNK), 1)
    causal = row >= col
    left = col < SSM_HEAD_DIM
    top = row < SSM_HEAD_DIM
    ones = jnp.ones((CHUNK, CHUNK), BF16)
    cb = _dot_nt(cm, bm)
    dtp = jax.nn.softplus(dt_c + bias)
    da_c = dtp * -jnp.exp(alog)
    da_r = jax.nn.softplus(dt_r + bias_col) * -jnp.exp(alog_col)
    lower = jnp.where(causal, 1.0, 0.0).astype(BF16)
    upper = jnp.where(row <= col, 1.0, 0.0).astype(BF16)
    cs = _dot_exact01(da_c, lower, upper, False, differentiable)
    cs_rows = _dot_exact01(da_r, upper, lower, True, differentiable)
    cs_last = jnp.sum(da_c, axis=0, keepdims=True)
    ecs, decay, ecl = jnp.exp(cs), jnp.exp(cs_last - cs), jnp.exp(cs_last)
    m = [cb * jnp.exp(jnp.where(causal, cs[:, r:r + 1] - cs_rows[r:r + 1, :], -1e30)) for r in range(SSM_HPG)]
    ygs, hn = [], []
    for p in range(SSM_PAIRS):
        a, b = 2 * p, 2 * p + 1
        pair = lambda v: jnp.where(left, v[:, a:a + 1], v[:, b:b + 1])
        xdt = xp[p] * pair(dtp)
        y = jnp.where(left, _dot(m[a], xdt), _dot(m[b], xdt))
        y = y + _dot_nt(cm, hp[p]) * pair(ecs)
        y = y + xp[p] * pair(dsk)
        states = _dot_tn(xdt * pair(decay), bm)
        hn.append(hp[p] * jnp.where(top, ecl[:, a:a + 1], ecl[:, b:b + 1]) + states)
        ygs.append(y * (zp[p] * jax.nn.sigmoid(zp[p])))
    ms = sum(_dot(t * t, ones) for t in ygs) * (1.0 / SSM_GROUP_W)
    rs = lax.rsqrt(ms + EPS)
    return [ygs[p] * rs * gnp[p] for p in range(SSM_PAIRS)], hn


def _ssd_in_specs(cidx):
    gw, n = SSM_GROUP_W, SSM_STATE
    bm_blk = D_INNER // n
    return [
        pl.BlockSpec((CHUNK, gw), lambda g, c: (cidx(c), g)),
        pl.BlockSpec((CHUNK, gw), lambda g, c: (cidx(c), g)),
        pl.BlockSpec((CHUNK, n), lambda g, c: (cidx(c), bm_blk + g)),
        pl.BlockSpec((CHUNK, n), lambda g, c: (cidx(c), bm_blk + SSM_GROUPS + g)),
        pl.BlockSpec((None, CHUNK, SSM_HPG), lambda g, c: (g, cidx(c), 0)),
        pl.BlockSpec((None, SSM_HPG, CHUNK), lambda g, c: (g, 0, cidx(c))),
        pl.BlockSpec((None, 3, SSM_HPG), lambda g, c: (g, 0, 0)),
        pl.BlockSpec((None, SSM_HPG, 2), lambda g, c: (g, 0, 0)),
        pl.BlockSpec((1, gw), lambda g, c: (0, g)),
    ]


def _ssd_args(x_ref, z_ref, bm_ref, cm_ref, hp, dtc_ref, dtr_ref, prow_ref, pcol_ref, gn_ref):
    npair, w = SSM_PAIRS, 2 * SSM_HEAD_DIM
    return (_split(x_ref, npair, w), _split(z_ref, npair, w), bm_ref[...], cm_ref[...], hp, dtc_ref[...], dtr_ref[...],
            prow_ref[0:1, :], pcol_ref[:, 0:1], prow_ref[1:2, :], pcol_ref[:, 1:2], prow_ref[2:3, :],
            _split(gn_ref, npair, w))


def _pair_rows(ref):
    w = 2 * SSM_HEAD_DIM
    return [ref[p * w:(p + 1) * w, :] for p in range(SSM_PAIRS)]


def _ssd_fwd(xbc, proj, dt_c, dt_r, par_row, par_col, gn, mixcat, *, name):
    w = 2 * SSM_HEAD_DIM

    def body(x_ref, z_ref, bm_ref, cm_ref, dtc_ref, dtr_ref, prow_ref, pcol_ref, gn_ref, cat_in,
             cat_ref, hprev_ref, h_scr):
        del cat_in

        @pl.when(pl.program_id(1) == 0)
        def _():
            h_scr[...] = jnp.zeros_like(h_scr)

        hprev_ref[...] = h_scr[...]
        yn, hn = _ssd_tile(*_ssd_args(x_ref, z_ref, bm_ref, cm_ref, _pair_rows(h_scr), dtc_ref, dtr_ref, prow_ref,
                                      pcol_ref, gn_ref))
        for p in range(SSM_PAIRS):
            cat_ref[:, p * w:(p + 1) * w] = yn[p].astype(cat_ref.dtype)
            h_scr[p * w:(p + 1) * w, :] = hn[p]

    return pl.pallas_call(
        body, grid=(SSM_GROUPS, N_CHUNKS), in_specs=[*_ssd_in_specs(lambda c: c), pl.BlockSpec(memory_space=pl.ANY)],
        out_specs=[pl.BlockSpec((CHUNK, SSM_GROUP_W), lambda g, c: (c, g)),
                   pl.BlockSpec((None, None, SSM_GROUP_W, SSM_STATE), lambda g, c: (c, g, 0, 0))],
        out_shape=[SDS(mixcat.shape, mixcat.dtype), SDS((N_CHUNKS, SSM_GROUPS, SSM_GROUP_W, SSM_STATE), F32)],
        scratch_shapes=[pltpu.VMEM((SSM_GROUP_W, SSM_STATE), F32)],
        input_output_aliases={9: 0}, compiler_params=_cparams(("parallel", "arbitrary")), name=name,
    )(xbc, proj, xbc, xbc, dt_c, dt_r, par_row, par_col, gn, mixcat)


def _ssd_bwd(xbc, proj, dt_c, dt_r, par_row, par_col, gn, hprev, dcat, dproj, *, name):
    nh, w, gw, n = SSM_HPG, 2 * SSM_HEAD_DIM, SSM_GROUP_W, SSM_STATE
    rev = lambda c: N_CHUNKS - 1 - c

    def body(x_ref, z_ref, bm_ref, cm_ref, dtc_ref, dtr_ref, prow_ref, pcol_ref, gn_ref, hprev_ref, dy_ref,
             dproj_in, dz_ref, dxs_ref, dbm_ref, dcm_ref, ddtc_ref, ddtr_ref, dprow_ref, dpcol_ref, dgn_ref, dh_scr):
        del dproj_in
        first = pl.program_id(1) == 0

        @pl.when(first)
        def _():
            dh_scr[...] = jnp.zeros_like(dh_scr)
            for ref in (dprow_ref, dpcol_ref, dgn_ref):
                ref[...] = jnp.zeros_like(ref)

        args = _ssd_args(x_ref, z_ref, bm_ref, cm_ref, _pair_rows(hprev_ref), dtc_ref, dtr_ref, prow_ref, pcol_ref,
                         gn_ref)
        _, vjp = jax.vjp(lambda *a: _ssd_tile(*a, differentiable=True), *args)
        dxs, dzs, dbm, dcm, dhs, ddtc, ddtr, dbias, dbias_col, dalog, dalog_col, ddsk, dgn = vjp(
            (_split(dy_ref, SSM_PAIRS, w), _pair_rows(dh_scr)))
        dbm_ref[...] = dbm
        dcm_ref[...] = dcm
        ddtc_ref[...] = ddtc
        ddtr_ref[...] = ddtr
        for q in range(SSM_PAIRS):
            dxs_ref[:, q * w:(q + 1) * w] = dxs[q]
            dz_ref[:, q * w:(q + 1) * w] = dzs[q].astype(dz_ref.dtype)
            dh_scr[q * w:(q + 1) * w, :] = dhs[q]
            dgn_ref[:, q * w:(q + 1) * w] += dgn[q]
        for i, d in enumerate((dbias, dalog, ddsk)):
            dprow_ref[i:i + 1, :] += d
        for i, d in enumerate((dbias_col, dalog_col)):
            dpcol_ref[:, i:i + 1] += d

    return pl.pallas_call(
        body, grid=(SSM_GROUPS, N_CHUNKS),
        in_specs=[*_ssd_in_specs(rev),
                  pl.BlockSpec((None, None, gw, n), lambda g, c: (rev(c), g, 0, 0)),
                  pl.BlockSpec((CHUNK, gw), lambda g, c: (rev(c), g)),
                  pl.BlockSpec(memory_space=pl.ANY)],
        out_specs=[pl.BlockSpec((CHUNK, gw), lambda g, c: (rev(c), g)),
                   pl.BlockSpec((CHUNK, gw), lambda g, c: (rev(c), g)),
                   pl.BlockSpec((CHUNK, n), lambda g, c: (rev(c), g)),
                   pl.BlockSpec((CHUNK, n), lambda g, c: (rev(c), g)),
                   pl.BlockSpec((None, CHUNK, nh), lambda g, c: (g, rev(c), 0)),
                   pl.BlockSpec((None, nh, CHUNK), lambda g, c: (g, 0, rev(c))),
                   pl.BlockSpec((None, 3, nh), lambda g, c: (g, 0, 0)),
                   pl.BlockSpec((None, nh, 2), lambda g, c: (g, 0, 0)),
                   pl.BlockSpec((1, gw), lambda g, c: (0, g))],
        out_shape=[SDS(dproj.shape, dproj.dtype), SDS((SEQ, D_INNER), F32), SDS((SEQ, SSM_GROUPS * n), F32),
                   SDS((SEQ, SSM_GROUPS * n), F32), SDS((SSM_GROUPS, SEQ, nh), F32), SDS((SSM_GROUPS, nh, SEQ), F32),
                   SDS((SSM_GROUPS, 3, nh), F32), SDS((SSM_GROUPS, nh, 2), F32), SDS((1, D_INNER), F32)],
        scratch_shapes=[pltpu.VMEM((gw, n), F32)],
        input_output_aliases={11: 0}, compiler_params=_cparams(("parallel", "arbitrary")), name=name,
    )(xbc, proj, xbc, xbc, dt_c, dt_r, par_row, par_col, gn, hprev, dcat, dproj)


def _sum_contributions(chip, parts, landed, *, name):
    _, r, c = parts.shape
    tr = _pick(r, (256, 384, 128))

    def body(chip_ref, own_ref, landed_ref, o_ref):
        del chip_ref
        acc = own_ref[...].astype(F32)
        for s in range(landed_ref.shape[0]):
            acc = acc + landed_ref[s].astype(F32)
        o_ref[...] = acc

    grid_spec = pltpu.PrefetchScalarGridSpec(
        num_scalar_prefetch=1, grid=(r // tr,),
        in_specs=[pl.BlockSpec((None, tr, c), lambda i, chip_ref: (chip_ref[0], i, 0)),
                  pl.BlockSpec((landed.shape[0], tr, c), lambda i, chip_ref: (0, i, 0))],
        out_specs=pl.BlockSpec((tr, c), lambda i, chip_ref: (i, 0)))
    return pl.pallas_call(body, grid_spec=grid_spec, out_shape=SDS((r, c), F32),
                          compiler_params=_cparams(("parallel",)), name=name)(chip, parts, landed)


def _adamw(w, g, m, v, *, name):
    layers, r, c = w.shape
    if r <= 256 or r % 128 == 0:
        tr = min(r, 256)
        steps, spec = r // tr, pl.BlockSpec((None, tr, c), lambda l, i: (l, i, 0))
    else:
        tc = _pick(c, (256, 128))
        steps, spec = c // tc, pl.BlockSpec((None, r, tc), lambda l, i: (l, 0, i))

    def body(w_ref, g_ref, m_ref, v_ref, d_ref, mo_ref, vo_ref):
        g = g_ref[...]
        m_new = ADAM_B1 * m_ref[...] + (1.0 - ADAM_B1) * g
        v_new = ADAM_B2 * v_ref[...] + (1.0 - ADAM_B2) * (g * g)
        m_hat = m_new / (1.0 - ADAM_B1 ** ADAM_STEP)
        v_hat = v_new / (1.0 - ADAM_B2 ** ADAM_STEP)
        d_ref[...] = -ADAM_LR * (m_hat / (jnp.sqrt(v_hat) + ADAM_EPS) + ADAM_WD * w_ref[...])
        mo_ref[...] = m_new
        vo_ref[...] = v_new

    return pl.pallas_call(body, grid=(layers, steps), in_specs=[spec] * 4, out_specs=[spec] * 3,
                          out_shape=[SDS(w.shape, F32)] * 3, compiler_params=_cparams(("parallel", "parallel")),
                          name=name)(w, g, m, v)


ANY = pl.BlockSpec(memory_space=pl.ANY)


def _place():
    x, y, c = lax.axis_index("x"), lax.axis_index("y"), lax.axis_index("c")
    chips = [(1 - x, y), (x, 1 - y), (1 - x, 1 - y)]
    return x, y, c, chips


def _remote(src, dst, send_sem, recv_sem, to):
    return pltpu.make_async_remote_copy(src_ref=src, dst_ref=dst, send_sem=send_sem, recv_sem=recv_sem,
                                        device_id=to, device_id_type=MESH)


STREAM_ROWS = 256


def _stream_rows(i):
    return pl.ds(pl.multiple_of(i * STREAM_ROWS, STREAM_ROWS), STREAM_ROWS)


def _channel_scratch(width, dtype, rows=STREAM_ROWS):
    buf = (2, rows, width)
    return [pltpu.VMEM(buf, dtype), pltpu.VMEM(buf, dtype), *([pltpu.SemaphoreType.DMA((2,))] * 5),
            pltpu.SemaphoreType.REGULAR((2,))]


CHANNEL_REFS = 8


def _copy_blocks(srcs, dsts, ch):
    sbuf, _, ld, _, _, st, _, _ = ch
    n = len(srcs)
    load = lambda i: pltpu.make_async_copy(srcs[i], sbuf.at[i % 2], ld.at[i % 2])
    store = lambda i: pltpu.make_async_copy(sbuf.at[i % 2], dsts[i], st.at[i % 2])
    load(0).start()
    for i in range(n):
        if i + 1 < n:
            if i >= 1:
                store(i - 1).wait()
            load(i + 1).start()
        load(i).wait()
        store(i).start()
    for i in range(max(0, n - 2), n):
        store(i).wait()


def _exchange_block_streams(streams, sibling):
    plans = []
    for srcs, dsts, keeps, (sbuf, rbuf, ld, snd, rcv, st, kp, credit) in streams:
        n = len(srcs)

        def load(i, srcs=srcs, sbuf=sbuf, ld=ld):
            return pltpu.make_async_copy(srcs[i], sbuf.at[i % 2], ld.at[i % 2])

        def push(i, sbuf=sbuf, rbuf=rbuf, snd=snd, rcv=rcv):
            return _remote(sbuf.at[i % 2], rbuf.at[i % 2], snd.at[i % 2], rcv.at[i % 2], sibling)

        def store(i, rbuf=rbuf, dsts=dsts, st=st):
            return pltpu.make_async_copy(rbuf.at[i % 2], dsts[i], st.at[i % 2])

        def save(i, sbuf=sbuf, keeps=keeps, kp=kp):
            return pltpu.make_async_copy(sbuf.at[i % 2], keeps[i], kp.at[i % 2])

        def free_slot(i, n=n, store=store, credit=credit):
            if 1 <= i < n:
                store(i - 1).wait()
                if i + 1 < n:
                    pl.semaphore_signal(credit.at[(i + 1) % 2], 1, device_id=sibling, device_id_type=MESH)

        def send(i, n=n, load=load, push=push, save=save, keeps=keeps, credit=credit):
            if i < n:
                load(i).wait()
                pl.semaphore_wait(credit.at[i % 2], 1)
                push(i).start()
                if keeps[i] is not None:
                    save(i).start()

        def receive(i, n=n, load=load, push=push, store=store, save=save, keeps=keeps):
            if i < n:
                push(i).wait_recv()
                store(i).start()
                push(i).wait_send()
                if keeps[i] is not None:
                    save(i).wait()
                if i + 2 < n:
                    load(i + 2).start()

        for i in range(min(2, n)):
            pl.semaphore_signal(credit.at[i], 1, device_id=sibling, device_id_type=MESH)
            load(i).start()
        plans.append((n, free_slot, send, receive, store))
    for _, _, send, _, _ in plans:
        send(0)
    for i in range(max(p[0] for p in plans)):
        for _, free_slot, _, _, _ in plans:
            free_slot(i)
        for _, _, send, _, _ in plans:
            send(i + 1)
        for _, _, _, receive, _ in plans:
            receive(i)
    for n, _, _, _, store in plans:
        store(n - 1).wait()


def _all_gather_shards(shards, small, *, name):
    n = len(shards)

    def body(*refs):
        ins, outs = refs[:n + 1], refs[n + 1:2 * n + 2]
        scr = refs[2 * n + 2:]
        chans = [scr[CHANNEL_REFS * t:CHANNEL_REFS * (t + 1)] for t in range(n)]
        send_sems, recv_sems, small_sems = scr[CHANNEL_REFS * n:]
        x, y, c, _ = _place()
        me = 2 * x + y
        sibling = (x, y, 1 - c)
        near = (lax.rem(x + 1 - c, 2), lax.rem(y + c, 2))
        far = (lax.rem(x + c, 2), lax.rem(y + 1 - c, 2))
        k_near, k_far, k_diag = 2 * near[0] + near[1], 2 * far[0] + far[1], 3 - me
        targets = ((*near, c), (*far, c), (*far, c))
        arrives = (k_near, k_far, k_diag)
        streams_in = (k_far, k_near, k_diag)

        def ici(t, j, src, blk):
            return _remote(src, outs[t].at[blk, c], send_sems.at[3 * t + j], recv_sems.at[3 * t + j], targets[j])

        first = [ici(t, j, ins[t].at[c], me) for t in range(n + 1) for j in range(2)]
        for cp in first:
            cp.start()
        small_local = pltpu.make_async_copy(ins[n], outs[n].at[me], small_sems.at[6])
        small_local.start()
        for t in range(n):
            _copy_blocks([ins[t].at[h] for h in range(2)], [outs[t].at[me, h] for h in range(2)], chans[t])
        passed = []
        for j in range(3):
            for t in range(n + 1):
                landed = outs[t].at[arrives[j], c]
                ici(t, j, landed, arrives[j]).wait_recv()
                if j == 0:
                    fwd = ici(t, 2, landed, k_near)
                    fwd.start()
                    passed.append(fwd)
                if t < n:
                    _exchange_block_streams([([landed], [outs[t].at[streams_in[j], 1 - c]], [None], chans[t])], sibling)
                else:
                    fwd = _remote(landed, landed, small_sems.at[j], small_sems.at[3 + j], sibling)
                    fwd.start()
                    passed.append(fwd)
        for j in range(3):
            got = outs[n].at[streams_in[j], 1 - c]
            _remote(got, got, small_sems.at[j], small_sems.at[3 + j], sibling).wait_recv()
        for cp in first + passed:
            cp.wait_send()
        small_local.wait()

    scratch = []
    for s in shards:
        scratch += _channel_scratch(s.shape[2], s.dtype, rows=s.shape[1])
    return pl.pallas_call(
        body, in_specs=[ANY] * (n + 1), out_specs=[ANY] * (n + 1),
        out_shape=[SDS((N_CHIPS, *s.shape), s.dtype) for s in (*shards, small)],
        scratch_shapes=[*scratch, pltpu.SemaphoreType.DMA((3 * n + 3,)), pltpu.SemaphoreType.DMA((3 * n + 3,)),
                        pltpu.SemaphoreType.DMA((7,))],
        compiler_params=pltpu.CompilerParams(vmem_limit_bytes=VMEM_LIMIT), name=name)(*shards, small)


def _pair_reduce(stacks, *, name):
    n = len(stacks)
    per = 11

    def body(*refs):
        ins, outs, scr = refs[:n], refs[n:2 * n], refs[2 * n:]
        x, y, c, _ = _place()
        sibling = (x, y, 1 - c)
        streams = []
        for t in range(n):
            sraw, sbuf, rbuf, obuf, pbuf, ld_s, ld_o, snd, rcv, st, credit = scr[per * t:per * (t + 1)]
            steps = ins[t].shape[1] // STREAM_ROWS
            src, own, out = ins[t].at[1 - c], ins[t].at[c], outs[t]
            assert steps >= 2

            def load_s(i, slot, src=src, sraw=sraw, ld_s=ld_s):
                return pltpu.make_async_copy(src.at[_stream_rows(i)], sraw.at[slot], ld_s.at[slot])

            def load_o(i, slot, own=own, obuf=obuf, ld_o=ld_o):
                return pltpu.make_async_copy(own.at[_stream_rows(i)], obuf.at[slot], ld_o.at[slot])

            def push(slot, sbuf=sbuf, rbuf=rbuf, snd=snd, rcv=rcv):
                return _remote(sbuf.at[slot], rbuf.at[slot], snd.at[slot], rcv.at[slot], sibling)

            def store(i, slot, pbuf=pbuf, out=out, st=st):
                return pltpu.make_async_copy(pbuf.at[slot], out.at[_stream_rows(i)], st.at[slot])

            def send(i, slot, load_s=load_s, push=push, sraw=sraw, sbuf=sbuf, credit=credit):
                load_s(i, slot).wait()
                sbuf[slot] = sraw[slot].astype(sbuf.dtype)
                pl.semaphore_wait(credit.at[slot], 1)
                push(slot).start()

            def combine(i, slot, load_s=load_s, load_o=load_o, push=push, store=store, rbuf=rbuf, obuf=obuf, pbuf=pbuf,
                        credit=credit, steps=steps):
                load_o(i, slot).wait()
                push(slot).wait_recv()

                @pl.when(i >= 2)
                def _():
                    store(i, slot).wait()

                pbuf[slot] = (obuf[slot] + rbuf[slot].astype(F32)).astype(pbuf.dtype)
                store(i, slot).start()
                push(slot).wait_send()

                @pl.when(i + 2 < steps)
                def _():
                    load_s(i + 2, slot).start()
                    load_o(i + 2, slot).start()
                    pl.semaphore_signal(credit.at[slot], 1, device_id=sibling, device_id_type=MESH)

            for slot in range(2):
                pl.semaphore_signal(credit.at[slot], 1, device_id=sibling, device_id_type=MESH)
                load_s(slot, slot).start()
                load_o(slot, slot).start()
            streams.append((steps, send, combine, store))
        for _, send, _, _ in streams:
            send(0, 0)

        def step(i, carry):
            slot = lax.rem(i, 2)
            for steps, send, _, _ in streams:
                @pl.when(i + 1 < steps)
                def _(send=send):
                    send(i + 1, 1 - slot)
            for steps, _, combine, _ in streams:
                @pl.when(i < steps)
                def _(combine=combine):
                    combine(i, slot)
            return carry

        lax.fori_loop(0, max(s[0] for s in streams), step, 0)
        for _, _, _, store in streams:
            for slot in range(2):
                store(0, slot).wait()

    scratch = []
    for s in stacks:
        buf = (2, STREAM_ROWS, s.shape[2])
        scratch += [pltpu.VMEM(buf, F32), pltpu.VMEM(buf, BF16), pltpu.VMEM(buf, BF16), pltpu.VMEM(buf, F32),
                    pltpu.VMEM(buf, BF16), *([pltpu.SemaphoreType.DMA((2,))] * 5), pltpu.SemaphoreType.REGULAR((2,))]
    return pl.pallas_call(
        body, in_specs=[ANY] * n, out_specs=[ANY] * n, out_shape=[SDS(s.shape[1:], BF16) for s in stacks],
        scratch_shapes=scratch, compiler_params=pltpu.CompilerParams(vmem_limit_bytes=VMEM_LIMIT), name=name)(*stacks)


HBM_SPEC = pl.BlockSpec(memory_space=pltpu.HBM)
SEM_SPEC = pl.BlockSpec(memory_space=pltpu.SEMAPHORE)
SIDE_EFFECT = pltpu.SideEffectType.DATAFLOW_SIDE_EFFECTING


def _scatter_copies(ins, lands, send_sems, recv_sems):
    _, _, c, chips = _place()
    return [_remote(ins[t].at[2 * cx + cy], lands[t].at[j], send_sems.at[3 * t + j], recv_sems.at[3 * t + j],
                    (cx, cy, c)) for t in range(len(ins)) for j, (cx, cy) in enumerate(chips)]


def _chip_scatter_start(parts, *, name):
    n = len(parts)

    def body(*refs):
        ins, lands = refs[:n], refs[n:2 * n]
        send_sems, recv_sems, token = refs[2 * n], refs[2 * n + 1], refs[-1]
        for cp in _scatter_copies(ins, lands, send_sems, recv_sems):
            cp.start()
        token[...] = jnp.zeros_like(token)

    hbm = lambda a: pltpu.with_memory_space_constraint(a, pltpu.HBM)
    lands = [hbm(lax.empty((3, *p.shape[1:]), p.dtype)) for p in parts]
    thru = [pltpu.HBM(a.shape, a.dtype) for a in (*parts, *lands)]
    outs = pl.pallas_call(
        body, name=name,
        out_shape=(pltpu.SemaphoreType.DMA((3 * n,)), pltpu.SemaphoreType.DMA((3 * n,)), *thru, SDS((8, 128), F32)),
        in_specs=[HBM_SPEC] * (2 * n),
        out_specs=(SEM_SPEC, SEM_SPEC, *([HBM_SPEC] * (2 * n)), pl.BlockSpec(memory_space=pltpu.VMEM)),
        input_output_aliases={i: 2 + i for i in range(2 * n)},
        compiler_params=pltpu.CompilerParams(has_side_effects=SIDE_EFFECT),
    )(*[hbm(p) for p in parts], *lands)
    return outs[0], outs[1], outs[2:2 + n], outs[2 + n:2 + 2 * n], outs[-1]


def _chip_scatter_wait(send_sems, recv_sems, parts, lands, after, *, name):
    n = len(parts)

    def body(*refs):
        ins, lands_in = refs[:n], refs[n:2 * n]
        for cp in _scatter_copies(ins, lands_in, refs[2 * n], refs[2 * n + 1]):
            cp.wait_send()
            cp.wait_recv()

    outs = pl.pallas_call(
        body, name=name, out_shape=[pltpu.HBM(a.shape, a.dtype) for a in (*parts, *lands)],
        in_specs=[*([HBM_SPEC] * (2 * n)), SEM_SPEC, SEM_SPEC, *([ANY] * len(after))],
        out_specs=[HBM_SPEC] * (2 * n), input_output_aliases={i: i for i in range(2 * n)},
        compiler_params=pltpu.CompilerParams(has_side_effects=SIDE_EFFECT),
    )(*parts, *lands, send_sems, recv_sems, *after)
    return outs[:n], outs[n:]


def _gather_copies(shards, zones, send_sems, recv_sems):
    x, y, c, chips = _place()
    return [_remote(shards[t].at[c], zones[t].at[2 * x + y, c], send_sems.at[3 * t + j], recv_sems.at[3 * t + j],
                    (cx, cy, c)) for t in range(len(shards)) for j, (cx, cy) in enumerate(chips)]


def _gather_start(shards, after, *, name):
    n = len(shards)

    def body(*refs):
        ins, zones = refs[:n], refs[n:2 * n]
        send_sems, recv_sems, token = refs[2 * n + len(after)], refs[2 * n + len(after) + 1], refs[-1]
        for cp in _gather_copies(ins, zones, send_sems, recv_sems):
            cp.start()
        token[...] = jnp.zeros_like(token)

    hbm = lambda a: pltpu.with_memory_space_constraint(a, pltpu.HBM)
    zones = [hbm(lax.empty((N_CHIPS, *s.shape), s.dtype)) for s in shards]
    thru = [pltpu.HBM(a.shape, a.dtype) for a in (*shards, *zones)]
    outs = pl.pallas_call(
        body, name=name,
        out_shape=(pltpu.SemaphoreType.DMA((3 * n,)), pltpu.SemaphoreType.DMA((3 * n,)), *thru, SDS((8, 128), F32)),
        in_specs=[*([HBM_SPEC] * (2 * n)), *([ANY] * len(after))],
        out_specs=(SEM_SPEC, SEM_SPEC, *([HBM_SPEC] * (2 * n)), pl.BlockSpec(memory_space=pltpu.VMEM)),
        input_output_aliases={i: 2 + i for i in range(2 * n)},
        compiler_params=pltpu.CompilerParams(has_side_effects=SIDE_EFFECT),
    )(*[hbm(s) for s in shards], *zones, *after)
    return outs[0], outs[1], outs[2:2 + n], outs[2 + n:2 + 2 * n], outs[-1]


def _gather_wait(send_sems, recv_sems, shards, zones, after, *, name):
    n = len(shards)

    def body(*refs):
        for cp in _gather_copies(refs[:n], refs[n:2 * n], refs[2 * n], refs[2 * n + 1]):
            cp.wait_send()
            cp.wait_recv()

    outs = pl.pallas_call(
        body, name=name, out_shape=[pltpu.HBM(a.shape, a.dtype) for a in (*shards, *zones)],
        in_specs=[*([HBM_SPEC] * (2 * n)), SEM_SPEC, SEM_SPEC, *([ANY] * len(after))],
        out_specs=[HBM_SPEC] * (2 * n), input_output_aliases={i: i for i in range(2 * n)},
        compiler_params=pltpu.CompilerParams(has_side_effects=SIDE_EFFECT),
    )(*shards, *zones, send_sems, recv_sems, *after)
    return outs[:n], outs[n:]


def _gather_finish(shards, zones, *, name):
    n = len(shards)

    def body(*refs):
        ins, zones_in, outs, scr = refs[:n], refs[n:2 * n], refs[2 * n:3 * n], refs[3 * n:]
        x, y, c, chips = _place()
        me = 2 * x + y
        sibling = (x, y, 1 - c)
        others = [2 * cx + cy for cx, cy in chips]
        chans = [scr[CHANNEL_REFS * t:CHANNEL_REFS * (t + 1)] for t in range(n)]
        for t in range(n):
            _copy_blocks([ins[t].at[h] for h in range(2)], [outs[t].at[me, h] for h in range(2)], chans[t])
        _exchange_block_streams([([zones_in[t].at[k, c] for k in others], [outs[t].at[k, 1 - c] for k in others],
                                  [None] * len(others), chans[t]) for t in range(n)], sibling)

    scratch = []
    for s in shards:
        scratch += _channel_scratch(s.shape[2], s.dtype, rows=s.shape[1])
    return pl.pallas_call(
        body, in_specs=[ANY] * (2 * n), out_specs=[ANY] * n, out_shape=[SDS(z.shape, z.dtype) for z in zones],
        input_output_aliases={n + t: t for t in range(n)}, scratch_shapes=scratch,
        compiler_params=pltpu.CompilerParams(vmem_limit_bytes=VMEM_LIMIT), name=name)(*shards, *zones)


def _pair_share(groups, *, name):
    finals = [f for grp in groups for f in grp]
    n, n_out = len(finals), len(groups)

    def body(*refs):
        ins, outs, scr = refs[:n], refs[n:n + n_out], refs[n + n_out:]
        x, y, c, _ = _place()
        sibling = (x, y, 1 - c)
        t, streams = 0, []
        for o, grp in enumerate(groups):
            rows = grp[0].shape[0] // 2
            blocks = [(layer, pl.ds(b * rows, rows)) for layer in range(len(grp)) for b in range(2)]
            streams.append(([ins[t + layer].at[rs] for layer, rs in blocks],
                            [outs[o].at[layer, 1 - c, rs] for layer, rs in blocks],
                            [outs[o].at[layer, c, rs] for layer, rs in blocks],
                            scr[CHANNEL_REFS * o:CHANNEL_REFS * (o + 1)]))
            t += len(grp)
        _exchange_block_streams(streams, sibling)

    scratch = []
    for grp in groups:
        scratch += _channel_scratch(grp[0].shape[1], grp[0].dtype, rows=grp[0].shape[0] // 2)
    return pl.pallas_call(
        body, in_specs=[ANY] * n, out_specs=[ANY] * n_out,
        out_shape=[SDS((len(grp), 2, *grp[0].shape), grp[0].dtype) for grp in groups],
        scratch_shapes=scratch, compiler_params=pltpu.CompilerParams(vmem_limit_bytes=VMEM_LIMIT), name=name)(*finals)


def _all_reduce_small(v, *, name):
    rows, lanes = v.shape
    n_dev = 8

    def body(v_ref, o_ref, all_ref, send_sems, recv_sems, local_sem):
        x, y, c, chips = _place()
        me, sibling = (x, y, c), (x, y, 1 - c)

        def block(px, py, pc):
            return all_ref.at[4 * px + 2 * py + pc]

        def copy(k, blk, to, src=None):
            return _remote(block(*blk) if src is None else src, block(*blk), send_sems.at[k], recv_sems.at[k], to)

        mine = pltpu.make_async_copy(v_ref, block(*me), local_sem)
        mine.start()
        first = [copy(0, me, sibling, src=v_ref)]
        first += [copy(1 + j, me, (*chip, c), src=v_ref) for j, chip in enumerate(chips)]
        for cp in first:
            cp.start()
        passed = [copy(4 + j, (*chip, c), sibling) for j, chip in enumerate(chips)]
        for j, chip in enumerate(chips):
            copy(1 + j, (*chip, c), me).wait_recv()
            passed[j].start()
        copy(0, sibling, me).wait_recv()
        for j, chip in enumerate(chips):
            copy(4 + j, (*chip, 1 - c), me).wait_recv()
        for cp in first + passed:
            cp.wait_send()
        mine.wait()
        acc = all_ref[0]
        for k in range(1, n_dev):
            acc = acc + all_ref[k]
        o_ref[...] = acc

    vmem = pl.BlockSpec(memory_space=pltpu.VMEM)
    return pl.pallas_call(
        body, in_specs=[vmem], out_specs=vmem, out_shape=SDS((rows, lanes), F32),
        scratch_shapes=[pltpu.VMEM((n_dev, rows, lanes), F32), pltpu.SemaphoreType.DMA((7,)),
                        pltpu.SemaphoreType.DMA((7,)), pltpu.SemaphoreType.DMA],
        compiler_params=pltpu.CompilerParams(vmem_limit_bytes=VMEM_LIMIT), name=name)(v)


def _relu2_epilogue(acc):
    return acc, jnp.square(jnp.maximum(acc, 0.0))


def _res_epilogue(acc, res):
    return (acc + res,)


def _drelu2_epilogue(acc, pre):
    return (acc * (2.0 * jnp.maximum(pre.astype(F32), 0.0)),)


def _ffn_fwd(h, g, w1, w2, tag):
    f = _rms_fwd(h, g, name=f"ffn_norm_{tag}")
    pre, act = _mm_nn(f, w1, name=f"ffn1_{tag}", epilogue=_relu2_epilogue, n_out_dtypes=(BF16, BF16))
    h_out = _mm_nn(act, w2, name=f"ffn2_{tag}", extras=(h,), epilogue=_res_epilogue)
    return h_out, (f, pre, act)


def _ffn_bwd(dh, h, g, w1, w2, saved, layer, after=()):
    f, pre, act = saved
    dpre = _mm_nt(dh, w2, name=f"ffn2_dx_{layer}", out_dtype=BF16, extras=(pre,), epilogue=_drelu2_epilogue,
                  after=after)
    dw2 = _mm_tn_stacked(act, dh, name=f"ffn2_dw_{layer}", col_slots=False)
    df = _mm_nt(dpre, w1, name=f"ffn1_dx_{layer}")
    dw1 = _mm_tn_stacked(f, dpre, name=f"ffn1_dw_{layer}", col_slots=True)
    dh, dg = _rms_bwd(h, g, df, dh, name=f"ffn_norm_bwd_{layer}")
    return dh, dg, dw1, dw2


def _kv_fwd(mem, g, w_kv, tag):
    m = _rms_fwd(mem, g, name=f"mem_norm_{tag}")
    return m, _mm_nn(m, w_kv, name=f"kv_{tag}")


def _kv_bwd(mem, g, w_kv, m, dk, dv, layer):
    dkv = jnp.concatenate([dk, dv], axis=1)
    dw = _mm_tn_stacked(m, dkv, name=f"kv_dw_{layer}", col_slots=True)
    dm = _mm_nt(dkv, w_kv, name=f"kv_dx_{layer}")
    _, dg = _rms_bwd(mem, g, dm, dm, name=f"mem_norm_bwd_{layer}")
    return dw, dg


def _local_step(x, mem, target, p, after_layer1=None, after_ffn0=None, after_mixer0=None):
    row = lambda v: v.reshape(1, -1)
    g = {}

    h0 = x
    a0 = _rms_fwd(h0, row(p["norm_mix"][0]), name="mix_norm_0")
    proj_a = _mm_nn(a0, p["a_in"], name="a_in", after=p.get("after_start", ()))
    m0, kv0 = _kv_fwd(mem, row(p["mem_norm"][0]), p["w_kv"][0], "0")
    cat0 = _attn_fwd(proj_a, 2 * D_INNER, kv0, name="attn_0")
    bs_col = p["a_bs"].reshape(A_GROUPS, CHUNK, 1)
    cat0 = _gate_fwd(proj_a, p["a_ln_g"], p["a_ln_b"], p["a_ws"], bs_col, cat0, name="gate")
    h1 = _mm_nn(cat0, p["w_out"][0], name="out_0", extras=(h0,), epilogue=_res_epilogue)
    w_ffn1_0, w_ffn2_0 = p["layer0_ffn"](h1) if "layer0_ffn" in p else (p["w_ffn1"][0], p["w_ffn2"][0])
    h2, ffn0 = _ffn_fwd(h1, row(p["norm_ffn"][0]), w_ffn1_0, w_ffn2_0, "0")

    w_kv1, b_in = p["layer1_mixer"](h2) if "layer1_mixer" in p else (p["w_kv"][1], p["b_in"])
    a1 = _rms_fwd(h2, row(p["norm_mix"][1]), name="mix_norm_1")
    proj_b = _mm_nn(a1, b_in, name="b_in")
    m1, kv1 = _kv_fwd(mem, row(p["mem_norm"][1]), w_kv1, "1")
    cat1 = _attn_fwd(proj_b, B_Q_OFF, kv1, name="attn_1")
    xbc = _conv_fwd(proj_b, p["b_conv_w"], p["b_conv_b"], name="conv")
    dt_raw = proj_b[:, B_DT_OFF:B_DT_OFF + SSM_HEADS].reshape(SEQ, SSM_GROUPS, SSM_HPG)
    dt_c = jnp.transpose(dt_raw, (1, 0, 2))
    dt_r = jnp.transpose(dt_raw, (1, 2, 0))
    per_head = lambda v: v.reshape(SSM_GROUPS, 1, SSM_HPG)
    par_row = jnp.concatenate([per_head(p["b_dt_bias"]), per_head(p["b_a_log"]), per_head(p["b_d"])], axis=1)
    ssd_par = (par_row, jnp.transpose(par_row[:, :2], (0, 2, 1)), p["b_gnorm"])
    cat1, hprev = _ssd_fwd(xbc, proj_b, dt_c, dt_r, *ssd_par, cat1, name="ssd")
    if "layer1_rest" in p:
        w_out1, w_ffn1_1, w_ffn2_1 = p["layer1_rest"](cat1)
    else:
        w_out1, w_ffn1_1, w_ffn2_1 = p["w_out"][1], p["w_ffn1"][1], p["w_ffn2"][1]
    h3 = _mm_nn(cat1, w_out1, name="out_1", extras=(h2,), epilogue=_res_epilogue)
    h4, ffn1 = _ffn_fwd(h3, row(p["norm_ffn"][1]), w_ffn1_1, w_ffn2_1, "1")

    loss, dh, g["final_norm"] = _loss_head(h4, row(p["final_norm"]), target, name="loss_head")

    dh, dnf1, dw1_1, dw2_1 = _ffn_bwd(dh, h3, row(p["norm_ffn"][1]), w_ffn1_1, w_ffn2_1, ffn1, 1)
    dcat1 = _mm_nt(dh, w_out1, name="out_dx_1")
    dwo_1 = _mm_tn_stacked(cat1, dh, name="out_dw_1", col_slots=False)
    dproj_b, dk1, dv1 = _attn_bwd(proj_b, B_Q_OFF, kv1, dcat1, B_IN_PAD, B_Q_OFF, name="attn_bwd_1")
    dproj_b, dxs, dbm, dcm, ddt_c, ddt_r, dpar_row, dpar_col, g["b_gnorm"] = _ssd_bwd(
        xbc, proj_b, dt_c, dt_r, *ssd_par, hprev, dcat1, dproj_b, name="ssd_bwd")
    dpar = dpar_row.at[:, :2].add(jnp.transpose(dpar_col, (0, 2, 1)))
    g["b_dt_bias"], g["b_a_log"], g["b_d"] = dpar[:, 0], dpar[:, 1], dpar[:, 2]
    dproj_b, g["b_conv_w"], g["b_conv_b"] = _conv_bwd(proj_b, p["b_conv_w"], p["b_conv_b"], dxs, dbm, dcm, dproj_b,
                                                      name="conv_bwd")
    ddt = jnp.transpose(ddt_c, (1, 0, 2)) + jnp.transpose(ddt_r, (2, 0, 1))
    ddt = jnp.pad(ddt.reshape(SEQ, SSM_HEADS), ((0, 0), (0, B_IN_PAD - B_DT_OFF - SSM_HEADS))).astype(BF16)
    dproj_b = lax.dynamic_update_slice(dproj_b, ddt, (0, B_DT_OFF))
    dwkv_1, dmn1 = _kv_bwd(mem, row(p["mem_norm"][1]), w_kv1, m1, dk1, dv1, 1)
    dwb = _b_in_grad_slots(_mm_tn(a1, dproj_b, name="b_in_dw"))
    da1 = _mm_nt(dproj_b, b_in, name="b_in_dx")
    dh, dnm1 = _rms_bwd(h2, row(p["norm_mix"][1]), da1, dh, name="mix_norm_bwd_1")
    layer1 = dict(w_kv=dwkv_1, w_out=dwo_1, w_ffn1=dw1_1, w_ffn2=dw2_1, b_in=dwb)
    token = () if after_layer1 is None else (after_layer1(layer1),)

    dh, dnf0, dw1_0, dw2_0 = _ffn_bwd(dh, h1, row(p["norm_ffn"][0]), w_ffn1_0, w_ffn2_0, ffn0, 0,
                                      after=token)
    ffn0_grads = dict(w_ffn1=dw1_0, w_ffn2=dw2_0)
    token = () if after_ffn0 is None else (after_ffn0(ffn0_grads),)
    dcat0 = _mm_nt(dh, p["w_out"][0], name="out_dx_0", after=token)
    dwo_0 = _mm_tn_stacked(cat0, dh, name="out_dw_0", col_slots=False)
    dproj_a, dk0, dv0 = _attn_bwd(proj_a, 2 * D_INNER, kv0, dcat0, A_IN, 2 * D_INNER, name="attn_bwd_0")
    dproj_a, g["a_ln_g"], g["a_ln_b"], g["a_ws"], dbs_col = _gate_bwd(
        proj_a, p["a_ln_g"], p["a_ln_b"], p["a_ws"], bs_col, dcat0, dproj_a, name="gate_bwd")
    g["a_bs"] = dbs_col.reshape(A_GROUPS, CHUNK)
    dwkv_0, dmn0 = _kv_bwd(mem, row(p["mem_norm"][0]), p["w_kv"][0], m0, dk0, dv0, 0)
    dwa = _mm_tn_stacked(a0, dproj_a, name="a_in_dw", col_slots=True)
    mixer0_grads = dict(w_kv=dwkv_0, w_out=dwo_0, a_in=dwa)
    token = () if after_mixer0 is None else (after_mixer0(mixer0_grads),)
    da0 = _mm_nt(dproj_a, p["a_in"], name="a_in_dx", after=token)
    dx, dnm0 = _rms_bwd(h0, row(p["norm_mix"][0]), da0, dh, name="mix_norm_bwd_0")

    g["norm_mix"] = jnp.concatenate([dnm0, dnm1], axis=0)
    g["norm_ffn"] = jnp.concatenate([dnf0, dnf1], axis=0)
    g["mem_norm"] = jnp.concatenate([dmn0, dmn1], axis=0)
    layer0 = dict(w_kv=dwkv_0, w_out=dwo_0, w_ffn1=dw1_0, w_ffn2=dw2_0, a_in=dwa)
    return loss, dx, g, layer0, layer1


def _b_in_full(gathered):
    n = B_IN // N_CHIPS
    dt0 = D_INNER + CONV_DIM - (N_CHIPS - 1) * n
    last = gathered[N_CHIPS - 1]
    return jnp.concatenate([*[gathered[k] for k in range(N_CHIPS - 1)], last[:, :dt0], last[:, dt0 + SSM_HEADS:],
                            last[:, dt0:dt0 + SSM_HEADS], jnp.zeros((D_MODEL, B_IN_PAD - B_IN), last.dtype)], axis=1)


def _b_in_grad_slots(d):
    n = B_IN // N_CHIPS
    dt0 = D_INNER + CONV_DIM
    last = jnp.concatenate([d[:, (N_CHIPS - 1) * n:dt0], d[:, B_DT_OFF:B_DT_OFF + SSM_HEADS], d[:, dt0:B_DT_OFF]], axis=1)
    slots = [*[d[:, k * n:(k + 1) * n] for k in range(N_CHIPS - 1)], last]
    half = D_MODEL // 2
    return jnp.stack([jnp.stack([s[h * half:(h + 1) * half] for s in slots]) for h in range(2)])


SMALL_REPL = ("norm_mix", "norm_ffn", "mem_norm", "a_ln_g", "a_ln_b", "a_ws", "a_bs", "b_dt_bias", "b_a_log", "b_d",
              "final_norm")
SMALL_SHARD = ("b_conv_w", "b_conv_b", "b_gnorm")
WEIGHTS = ("norm_mix", "norm_ffn", "mem_norm", "w_kv", "w_out", "w_ffn1", "w_ffn2", "a_in", "a_ln_g", "a_ln_b", "a_ws",
           "a_bs", "b_in", "b_conv_w", "b_conv_b", "b_dt_bias", "b_a_log", "b_d", "b_gnorm", "final_norm")
CONV_SHARD = CONV_DIM // N_CHIPS
GN_SHARD = D_INNER // N_CHIPS


LAYERED = ("w_kv", "w_out", "w_ffn1", "w_ffn2")


def _gather_weights(w):
    halves = lambda k, layer: (w[k][layer] if k in LAYERED else w[k][0]).reshape(2, -1, w[k].shape[-1]).astype(BF16)
    small = jnp.zeros((2, CONV_K, CONV_SHARD), F32)
    small = small.at[0].set(w["b_conv_w"][0])
    small = small.at[1, 0].set(w["b_conv_b"][0])
    small = small.at[1, 1, :GN_SHARD].set(w["b_gnorm"][0])
    first_names = ("w_kv", "w_out", "a_in")
    gathered = _all_gather_shards([halves(k, 0) for k in first_names], small, name="gather_weights_0")
    got = dict(zip(first_names, gathered))
    slots = lambda a: a.reshape(N_CHIPS, -1, a.shape[-1])
    rows = lambda a: a.reshape(-1, a.shape[-1])
    p = dict(w_kv=[slots(got["w_kv"])], w_out=[rows(got["w_out"])], a_in=slots(got["a_in"]))
    sm = gathered[-1]
    p["b_conv_w"] = jnp.transpose(sm[:, 0], (1, 0, 2)).reshape(CONV_K, CONV_DIM)
    p["b_conv_b"] = sm[:, 1, 0].reshape(1, CONV_DIM)
    p["b_gnorm"] = sm[:, 1, 1, :GN_SHARD].reshape(1, D_INNER)

    after, started = (gathered[0],), {}
    for tag, layer, names in (("0_ffn", 0, ("w_ffn1", "w_ffn2")), ("1_mixer", 1, ("w_kv", "b_in")),
                              ("1_rest", 1, ("w_out", "w_ffn1", "w_ffn2"))):
        started[tag] = _gather_start([halves(k, layer) for k in names], after, name=f"gather_start_{tag}")
        after = (started[tag][-1],)
    p["after_start"] = after

    def finish(tag, first):
        send_sems, recv_sems, shards, zones, _ = started[tag]
        shards, zones = _gather_wait(send_sems, recv_sems, shards, zones, (first,), name=f"gather_wait_{tag}")
        return _gather_finish(shards, zones, name=f"gather_finish_{tag}")

    def layer0_ffn(first):
        w1, w2 = finish("0_ffn", first)
        return slots(w1), rows(w2)

    def layer1_mixer(first):
        kv, b_in = finish("1_mixer", first)
        return slots(kv), _b_in_full(slots(b_in))

    def layer1_rest(first):
        wo, w1, w2 = finish("1_rest", first)
        return rows(wo), slots(w1), rows(w2)

    p.update(layer0_ffn=layer0_ffn, layer1_mixer=layer1_mixer, layer1_rest=layer1_rest)
    return p


def _pair_parts(grads, tag):
    stacks = [g.reshape(2, -1, g.shape[-1]) for g in grads.values()]
    parts = _pair_reduce(stacks, name=f"grads_pair_reduce_{tag}")
    return [t.reshape(N_CHIPS, -1, t.shape[-1]) for t in parts]


def _chip_sums(chip, names, parts, landed, tag):
    return {k: _sum_contributions(chip, t, u, name=f"grads_chip_sum_{k}_{tag}")
            for k, t, u in zip(names, parts, landed)}


def _small_layout(shapes):
    offs, o = {}, 0
    for k in (*SMALL_REPL, *SMALL_SHARD):
        size = math.prod(shapes[k])
        offs[k] = (o, size)
        o += size
    rows = -(-(o + 1) // (8 * 128)) * 8
    return offs, rows


def _reduce_small(g, loss_part, full_shapes):
    offs, rows = _small_layout(full_shapes)
    flat = jnp.concatenate([*[g[k].reshape(-1) for k in (*SMALL_REPL, *SMALL_SHARD)], loss_part[0, :1]])
    flat = jnp.pad(flat, (0, rows * 128 - flat.shape[0])).reshape(rows, 128)
    total = _all_reduce_small(flat, name="small_all_reduce").reshape(-1)
    end = max(o + n for o, n in offs.values())
    return {k: total[o:o + n].reshape(full_shapes[k]) for k, (o, n) in offs.items()}, total[end]


def kernel(x, mem, norm_mix, norm_ffn, mem_norm, w_kv, w_out, w_ffn1, w_ffn2, a_in, a_ln_g, a_ln_b, a_ws, a_bs, b_in, b_conv_w, b_conv_b, b_dt_bias, b_a_log, b_d, b_gnorm, final_norm, loss_target, m_norm_mix, m_norm_ffn, m_mem_norm, m_w_kv, m_w_out, m_w_ffn1, m_w_ffn2, m_a_in, m_a_ln_g, m_a_ln_b, m_a_ws, m_a_bs, m_b_in, m_b_conv_w, m_b_conv_b, m_b_dt_bias, m_b_a_log, m_b_d, m_b_gnorm, m_final_norm, v_norm_mix, v_norm_ffn, v_mem_norm, v_w_kv, v_w_out, v_w_ffn1, v_w_ffn2, v_a_in, v_a_ln_g, v_a_ln_b, v_a_ws, v_a_bs, v_b_in, v_b_conv_w, v_b_conv_b, v_b_dt_bias, v_b_a_log, v_b_d, v_b_gnorm, v_final_norm):
    w = dict(norm_mix=norm_mix, norm_ffn=norm_ffn, mem_norm=mem_norm, w_kv=w_kv, w_out=w_out, w_ffn1=w_ffn1,
             w_ffn2=w_ffn2, a_in=a_in, a_ln_g=a_ln_g, a_ln_b=a_ln_b, a_ws=a_ws, a_bs=a_bs, b_in=b_in, b_conv_w=b_conv_w,
             b_conv_b=b_conv_b, b_dt_bias=b_dt_bias, b_a_log=b_a_log, b_d=b_d, b_gnorm=b_gnorm, final_norm=final_norm)
    mom = dict(norm_mix=m_norm_mix, norm_ffn=m_norm_ffn, mem_norm=m_mem_norm, w_kv=m_w_kv, w_out=m_w_out,
               w_ffn1=m_w_ffn1, w_ffn2=m_w_ffn2, a_in=m_a_in, a_ln_g=m_a_ln_g, a_ln_b=m_a_ln_b, a_ws=m_a_ws,
               a_bs=m_a_bs, b_in=m_b_in, b_conv_w=m_b_conv_w, b_conv_b=m_b_conv_b, b_dt_bias=m_b_dt_bias,
               b_a_log=m_b_a_log, b_d=m_b_d, b_gnorm=m_b_gnorm, final_norm=m_final_norm)
    var = dict(norm_mix=v_norm_mix, norm_ffn=v_norm_ffn, mem_norm=v_mem_norm, w_kv=v_w_kv, w_out=v_w_out,
               w_ffn1=v_w_ffn1, w_ffn2=v_w_ffn2, a_in=v_a_in, a_ln_g=v_a_ln_g, a_ln_b=v_a_ln_b, a_ws=v_a_ws,
               a_bs=v_a_bs, b_in=v_b_in, b_conv_w=v_b_conv_w, b_conv_b=v_b_conv_b, b_dt_bias=v_b_dt_bias,
               b_a_log=v_b_a_log, b_d=v_b_d, b_gnorm=v_b_gnorm, final_norm=v_final_norm)

    p = _gather_weights(w)
    p.update(norm_mix=norm_mix, norm_ffn=norm_ffn, mem_norm=mem_norm, a_ln_g=a_ln_g, a_ln_b=a_ln_b, a_ws=a_ws[0],
             a_bs=a_bs[0], b_dt_bias=b_dt_bias, b_a_log=b_a_log, b_d=b_d, final_norm=final_norm)
    chip = 2 * lax.axis_index("x") + lax.axis_index("y")
    chip_arr = jnp.reshape(chip, (1,)).astype(jnp.int32)
    started = {}

    def start_scatter(tag):
        def hook(grads):
            start = _chip_scatter_start(_pair_parts(grads, tag), name=f"grads_chip_scatter_start_{tag}")
            started[tag] = (tuple(grads), start)
            return start[-1]
        return hook

    loss_part, dx, g, _, _ = _local_step(x[0], mem[0], loss_target[0], p, start_scatter("1"), start_scatter("0f"),
                                         start_scatter("0m"))
    full_shapes = {k: w[k].shape for k in SMALL_REPL}
    full_shapes.update(b_conv_w=(1, CONV_K, CONV_DIM), b_conv_b=(1, CONV_DIM), b_gnorm=(1, D_INNER))
    grads, loss = _reduce_small(g, loss_part, full_shapes)
    grads["b_conv_w"] = lax.dynamic_slice_in_dim(grads["b_conv_w"], chip * CONV_SHARD, CONV_SHARD, axis=2)
    grads["b_conv_b"] = lax.dynamic_slice_in_dim(grads["b_conv_b"], chip * CONV_SHARD, CONV_SHARD, axis=1)
    grads["b_gnorm"] = lax.dynamic_slice_in_dim(grads["b_gnorm"], chip * GN_SHARD, GN_SHARD, axis=1)

    def finish_scatter(tag, *first):
        names, (send_sems, recv_sems, parts, lands, _) = started[tag]
        parts, landed = _chip_scatter_wait(send_sems, recv_sems, parts, lands, first,
                                           name=f"grads_chip_scatter_wait_{tag}")
        return _chip_sums(chip_arr, names, parts, landed, tag)

    def adamw(names, grads):
        for k in names:
            shape = w[k].shape
            if len(shape) == 3 and shape[2] % 128 and not shape[1] % 128:
                flat = unflat = lambda a: jnp.transpose(a, (0, 2, 1))
            else:
                flat = (lambda a: a) if len(shape) == 3 else (lambda a: a.reshape(1, -1, shape[-1]))
                unflat = lambda a: a.reshape(shape)
            d, m_new, v_new = _adamw(flat(w[k]), flat(grads[k]), flat(mom[k]), flat(var[k]), name=f"adamw_{k}")
            delta[k], new_m[k], new_v[k] = unflat(d), unflat(m_new), unflat(v_new)

    delta, new_m, new_v = {}, {}, {}
    halves = [finish_scatter("0f", dx), finish_scatter("1", dx)]
    early = ("w_ffn1", "w_ffn2", "b_in")
    shared = _pair_share([[halves[layer][k] for layer in range(2) if k in halves[layer]] for k in early],
                         name="grads_pair_share_early")
    grads.update({k: a.reshape(w[k].shape) for k, a in zip(early, shared)})
    adamw([k for k in WEIGHTS if k in grads], grads)
    halves[0].update(finish_scatter("0m", delta["w_ffn2"]))
    late = ("w_kv", "w_out", "a_in")
    shared = _pair_share([[halves[layer][k] for layer in range(2) if k in halves[layer]] for k in late],
                         name="grads_pair_share_late")
    grads.update({k: a.reshape(w[k].shape) for k, a in zip(late, shared)})
    adamw(late, grads)

    return (loss, dx.reshape(x.shape), *[grads[k] for k in WEIGHTS], *[delta[k] for k in WEIGHTS],
            *[new_m[k] for k in WEIGHTS], *[new_v[k] for k in WEIGHTS])
```

```python
import math

import jax
import jax.numpy as jnp
from jax import lax
from jax.experimental import pallas as pl
from jax.experimental.pallas import tpu as pltpu

F32 = jnp.float32
BF16 = jnp.bfloat16
SDS = jax.ShapeDtypeStruct

D_MODEL = 1024
SEQ = 2048
CHUNK = 128
N_MEM = 256
D_INNER = 2048
A_GROUPS = 8
A_GROUP_W = D_INNER // A_GROUPS
SSM_HEADS = 32
SSM_HEAD_DIM = 64
SSM_GROUPS = 4
SSM_HPG = 8
SSM_STATE = 128
SSM_GROUP_W = SSM_HPG * SSM_HEAD_DIM
CONV_K = 4
CONV_DIM = 3072
X_HEADS = 4
X_HEAD_DIM = 256
X_WIDTH = 1024
MIX_OUT = 3072
D_FF = 4096
A_IN = 5120
B_IN = 6176
B_IN_PAD = 6272
B_Q_OFF = 5120
B_DT_OFF = 6144
N_CHUNKS = SEQ // CHUNK
EPS = 1e-6
N_CHIPS = 4

ADAM_LR = 0.001
ADAM_B1 = 0.9
ADAM_B2 = 0.999
ADAM_EPS = 1e-08
ADAM_WD = 0.01
ADAM_STEP = 10

VMEM_LIMIT = 48 * 1024 * 1024
MESH = pl.DeviceIdType.MESH


def _cparams(sem):
    return pltpu.CompilerParams(dimension_semantics=sem, vmem_limit_bytes=VMEM_LIMIT)


def _dot(a, b, dims=(((1,), (0,)), ((), ()))):
    return lax.dot_general(a.astype(BF16), b.astype(BF16), dims, preferred_element_type=F32)


def _dot_nt(a, b):
    return _dot(a, b, (((1,), (1,)), ((), ())))


def _dot_tn(a, b):
    return _dot(a, b, (((0,), (0,)), ((), ())))


def _pick(n, cands):
    for c in cands:
        if n % c == 0:
            return c
    raise ValueError(f"no tile for {n}")


MM_RING = 3


def _mm_call(a, b, *, dims, grid, a_spec, b_spec, acc_shape, out_shapes, out_specs, name,
             extras=(), extra_specs=(), epilogue=None, after=(), ring_b=False):
    _, n_j, n_k = grid
    steps = grid[0] * n_j * n_k
    n_extra = len(extras)
    n_out = len(out_shapes)
    n_in = 2 + n_extra + len(after)
    b_block = tuple(size for size in b_spec.block_shape if size is not None)

    def b_copy(b_ref, ring, sems, step):
        idx = b_spec.index_map(step // (n_j * n_k), (step // n_k) % n_j, step % n_k)
        window = tuple(ix if size is None else pl.ds(ix * size, size) for ix, size in zip(idx, b_spec.block_shape))
        slot = step % MM_RING
        return pltpu.make_async_copy(b_ref.at[window], ring.at[slot], sems.at[slot])

    def b_block_now(refs):
        if not ring_b:
            return refs[1][...]
        b_ref, ring, sems = refs[1], refs[n_in + n_out], refs[n_in + n_out + 1]
        step = (pl.program_id(0) * n_j + pl.program_id(1)) * n_k + pl.program_id(2)

        @pl.when(step == 0)
        def _():
            for first in range(min(MM_RING - 1, steps)):
                b_copy(b_ref, ring, sems, first).start()

        @pl.when(step + MM_RING - 1 < steps)
        def _():
            b_copy(b_ref, ring, sems, step + MM_RING - 1).start()

        b_copy(b_ref, ring, sems, step).wait()
        return ring[step % MM_RING]

    def finish(total, extra_refs, out_refs):
        vals = (total,) if epilogue is None else epilogue(total, *[e[...] for e in extra_refs])
        for o_ref, v in zip(out_refs, vals):
            o_ref[...] = v.astype(o_ref.dtype)

    def body_one_step(*refs):
        finish(_dot(refs[0][...], b_block_now(refs), dims), refs[2:2 + n_extra], refs[n_in:n_in + n_out])

    def body(*refs):
        acc = refs[-1]
        k = pl.program_id(2)

        @pl.when(k == 0)
        def _():
            acc[...] = jnp.zeros_like(acc)

        acc[...] += _dot(refs[0][...], b_block_now(refs), dims)

        @pl.when(k == n_k - 1)
        def _():
            finish(acc[...], refs[2:2 + n_extra], refs[n_in:n_in + n_out])

    ring = [pltpu.VMEM((MM_RING, *b_block), b.dtype), pltpu.SemaphoreType.DMA((MM_RING,))] if ring_b else []
    return pl.pallas_call(
        body_one_step if n_k == 1 else body, grid=grid,
        in_specs=[a_spec, ANY if ring_b else b_spec, *extra_specs, *([ANY] * len(after))], out_specs=list(out_specs),
        out_shape=list(out_shapes), scratch_shapes=ring + ([] if n_k == 1 else [pltpu.VMEM(acc_shape, F32)]),
        compiler_params=_cparams(("arbitrary",) * 3 if ring_b else ("parallel", "parallel", "arbitrary")), name=name,
    )(a, b, *extras, *after)


def _w_dims(w):
    if w.ndim == 2:
        return w.shape[0], w.shape[1], 1, w.shape[1]
    return w.shape[1], w.shape[0] * w.shape[2], w.shape[0], w.shape[2]


def _mm_nn(a, w, *, name, out_dtype=F32, a_cols=None, extras=(), epilogue=None, n_out_dtypes=None, after=()):
    m = a.shape[0]
    k_dim, n_dim, _, n_slot = _w_dims(w)
    a_off, a_w = (0, a.shape[1]) if a_cols is None else a_cols
    assert a_w == k_dim
    tm = _pick(m, (2048, 1024, 512, 256))
    tn = _pick(n_slot, (512, 896, 640, 256, 128))
    tk = _pick(k_dim, (1024, 768, 512, 384, 256, 128))
    assert a_off % tk == 0
    nb = n_slot // tn
    a_spec = pl.BlockSpec((tm, tk), lambda i, j, k: (i, a_off // tk + k))
    if w.ndim == 2:
        b_spec = pl.BlockSpec((tk, tn), lambda i, j, k: (k, j))
    else:
        b_spec = pl.BlockSpec((None, tk, tn), lambda i, j, k: (j // nb, k, j % nb))
    o_spec = pl.BlockSpec((tm, tn), lambda i, j, k: (i, j))
    dts = n_out_dtypes or (out_dtype,)
    outs = _mm_call(a, w, dims=(((1,), (0,)), ((), ())), grid=(m // tm, n_dim // tn, k_dim // tk),
                    a_spec=a_spec, b_spec=b_spec, acc_shape=(tm, tn),
                    out_shapes=[SDS((m, n_dim), dt) for dt in dts], out_specs=[o_spec] * len(dts), name=name,
                    extras=extras, extra_specs=[o_spec] * len(extras), epilogue=epilogue, after=after)
    return outs if n_out_dtypes else outs[0]


def _mm_nt(a, w, *, name, out_dtype=F32, extras=(), epilogue=None, after=()):
    m = a.shape[0]
    k_dim, n_dim, _, n_slot = _w_dims(w)
    assert a.shape[1] == n_dim
    tm = _pick(m, (2048, 1024, 512, 256))
    to = _pick(k_dim, (512, 384, 256, 128))
    tc = _pick(n_slot, (1280, 1024, 896, 640, 512, 256, 128))
    nb = n_slot // tc
    a_spec = pl.BlockSpec((tm, tc), lambda i, j, k: (i, k))
    if w.ndim == 2:
        b_spec = pl.BlockSpec((to, tc), lambda i, j, k: (j, k))
    else:
        b_spec = pl.BlockSpec((None, to, tc), lambda i, j, k: (k // nb, j, k % nb))
    o_spec = pl.BlockSpec((tm, to), lambda i, j, k: (i, j))
    return _mm_call(a, w, dims=(((1,), (1,)), ((), ())), grid=(m // tm, k_dim // to, n_dim // tc),
                    a_spec=a_spec, b_spec=b_spec, acc_shape=(tm, to),
                    out_shapes=[SDS((m, k_dim), out_dtype)], out_specs=[o_spec], name=name,
                    extras=extras, extra_specs=[o_spec] * len(extras), epilogue=epilogue, after=after)[0]


def _mm_tn(x, dy, *, name, x_cols=None):
    s = x.shape[0]
    x_off, k_dim = (0, x.shape[1]) if x_cols is None else x_cols
    n_dim = dy.shape[1]
    tm = _pick(k_dim, (1024, 768, 512, 384, 256, 128))
    tn = _pick(n_dim, (512, 896, 640, 256, 128))
    tk = _pick(s, (2048, 1024, 512, 256))
    assert x_off % tm == 0
    a_spec = pl.BlockSpec((tk, tm), lambda i, j, k: (k, x_off // tm + i))
    b_spec = pl.BlockSpec((tk, tn), lambda i, j, k: (k, j))
    o_spec = pl.BlockSpec((tm, tn), lambda i, j, k: (i, j))
    return _mm_call(x, dy, dims=(((0,), (0,)), ((), ())), grid=(k_dim // tm, n_dim // tn, s // tk),
                    a_spec=a_spec, b_spec=b_spec, acc_shape=(tm, tn),
                    out_shapes=[SDS((k_dim, n_dim), F32)], out_specs=[o_spec], name=name, ring_b=True)[0]


def _mm_tn_stacked(x, dy, *, name, col_slots):
    s, k_dim = x.shape
    n_dim = dy.shape[1]
    r, c = (k_dim // 2, n_dim // N_CHIPS) if col_slots else (k_dim // N_CHIPS // 2, n_dim)
    tm = 2 * r
    tn = _pick(c, (512, 896, 640, 256, 128))
    tk = _pick(s, (2048, 1024, 512, 256))
    a_spec = pl.BlockSpec((tk, tm), lambda i, j, k: (k, i))
    b_spec = pl.BlockSpec((tk, tn), lambda i, j, k: (k, j))
    if col_slots:
        nb = c // tn
        o_spec = pl.BlockSpec((2, None, r, tn), lambda i, j, k: (0, j // nb, 0, j % nb))
    else:
        o_spec = pl.BlockSpec((2, None, r, tn), lambda i, j, k: (0, i, 0, j))
    return _mm_call(x, dy, dims=(((0,), (0,)), ((), ())), grid=(k_dim // tm, n_dim // tn, s // tk),
                    a_spec=a_spec, b_spec=b_spec, acc_shape=(tm, tn), epilogue=lambda acc: (acc.reshape(2, r, tn),),
                    out_shapes=[SDS((2, N_CHIPS, r, c), F32)], out_specs=[o_spec], name=name)[0]


def _rms(x, g):
    return x * lax.rsqrt(jnp.mean(x * x, axis=-1, keepdims=True) + EPS) * g


def _rms_fwd(h, g, *, name):
    rows, d = h.shape
    tr = _pick(rows, (512, 256))

    def body(h_ref, g_ref, o_ref):
        o_ref[...] = _rms(h_ref[...], g_ref[...]).astype(o_ref.dtype)

    return pl.pallas_call(
        body, grid=(rows // tr,),
        in_specs=[pl.BlockSpec((tr, d), lambda i: (i, 0)), pl.BlockSpec((1, d), lambda i: (0, 0))],
        out_specs=pl.BlockSpec((tr, d), lambda i: (i, 0)), out_shape=SDS((rows, d), BF16),
        compiler_params=_cparams(("parallel",)), name=name)(h, g)


def _rms_bwd(h, g, da, dres, *, name):
    rows, d = h.shape
    tr = _pick(rows, (512, 256))

    def body(h_ref, g_ref, da_ref, dres_ref, dh_ref, dg_ref):
        _, vjp = jax.vjp(_rms, h_ref[...], g_ref[...])
        dh, dg = vjp(da_ref[...].astype(F32))
        dh_ref[...] = dres_ref[...] + dh

        @pl.when(pl.program_id(0) == 0)
        def _():
            dg_ref[...] = jnp.zeros_like(dg_ref)

        dg_ref[...] += dg

    row_spec = pl.BlockSpec((tr, d), lambda i: (i, 0))
    vec_spec = pl.BlockSpec((1, d), lambda i: (0, 0))
    return pl.pallas_call(
        body, grid=(rows // tr,), in_specs=[row_spec, vec_spec, row_spec, row_spec],
        out_specs=[row_spec, vec_spec], out_shape=[SDS((rows, d), F32), SDS((1, d), F32)],
        compiler_params=_cparams(("arbitrary",)), name=name)(h, g, da, dres)


def _loss_head(h, g, target, *, name):
    rows, d = h.shape
    tr = _pick(rows, (512, 256))

    def body(h_ref, g_ref, t_ref, loss_ref, dh_ref, dg_ref):
        y, vjp = jax.vjp(_rms, h_ref[...], g_ref[...])
        err = y - t_ref[...]
        dh, dg = vjp(err * (1.0 / d))
        dh_ref[...] = dh

        @pl.when(pl.program_id(0) == 0)
        def _():
            dg_ref[...] = jnp.zeros_like(dg_ref)
            loss_ref[...] = jnp.zeros_like(loss_ref)

        dg_ref[...] += dg
        part = jnp.sum(jnp.sum(err * err, axis=-1, keepdims=True), axis=0, keepdims=True) * (0.5 / d)
        loss_ref[...] += jnp.broadcast_to(part, loss_ref.shape)

    row_spec = pl.BlockSpec((tr, d), lambda i: (i, 0))
    vec_spec = pl.BlockSpec((1, d), lambda i: (0, 0))
    loss_spec = pl.BlockSpec((8, 128), lambda i: (0, 0))
    return pl.pallas_call(
        body, grid=(rows // tr,), in_specs=[row_spec, vec_spec, row_spec],
        out_specs=[loss_spec, row_spec, vec_spec],
        out_shape=[SDS((8, 128), F32), SDS((rows, d), F32), SDS((1, d), F32)],
        compiler_params=_cparams(("arbitrary",)), name=name)(h, g, target)


def _gelu(x):
    return 0.5 * x * (1.0 + lax.erf(x * (1.0 / math.sqrt(2.0))))


def _gate_tile(pu, pv, ln_g, ln_b, ws, bs_t):
    u = [_gelu(p) for p in pu]
    v = [_gelu(p) for p in pv]
    mu = sum(jnp.sum(t, axis=-1, keepdims=True) for t in v) * (1.0 / D_INNER)
    vc = [t - mu for t in v]
    var = sum(jnp.sum(t * t, axis=-1, keepdims=True) for t in vc) * (1.0 / D_INNER)
    rstd = lax.rsqrt(var + EPS)
    row = lax.broadcasted_iota(jnp.int32, (CHUNK, CHUNK), 0)
    col = lax.broadcasted_iota(jnp.int32, (CHUNK, CHUNK), 1)
    out = []
    for gi in range(A_GROUPS):
        vn = vc[gi] * rstd * ln_g[gi] + ln_b[gi]
        w = jnp.where(row >= col, ws[gi], 0.0)
        sv = _dot(w, vn) + bs_t[gi]
        out.append(u[gi] * sv)
    return out


def _split(ref, n, width):
    return [ref[:, i * width:(i + 1) * width] for i in range(n)]


def _gate_in_specs():
    return [
        pl.BlockSpec((CHUNK, D_INNER), lambda c: (c, 0)),
        pl.BlockSpec((CHUNK, D_INNER), lambda c: (c, 1)),
        pl.BlockSpec((1, D_INNER), lambda c: (0, 0)),
        pl.BlockSpec((1, D_INNER), lambda c: (0, 0)),
        pl.BlockSpec((A_GROUPS, CHUNK, CHUNK), lambda c: (0, 0, 0)),
        pl.BlockSpec((A_GROUPS, CHUNK, 1), lambda c: (0, 0, 0)),
    ]


def _gate_args(u_ref, v_ref, g_ref, b_ref, ws_ref, bs_ref):
    ng, gw = A_GROUPS, A_GROUP_W
    return (_split(u_ref, ng, gw), _split(v_ref, ng, gw), _split(g_ref, ng, gw), _split(b_ref, ng, gw),
            [ws_ref[i] for i in range(ng)], [bs_ref[i] for i in range(ng)])


def _gate_fwd(proj, ln_g, ln_b, ws, bs_col, mixcat, *, name):
    def body(u_ref, v_ref, g_ref, b_ref, ws_ref, bs_ref, cat_in, cat_ref):
        del cat_in
        out = _gate_tile(*_gate_args(u_ref, v_ref, g_ref, b_ref, ws_ref, bs_ref))
        for gi, o in enumerate(out):
            cat_ref[:, gi * A_GROUP_W:(gi + 1) * A_GROUP_W] = o.astype(cat_ref.dtype)

    return pl.pallas_call(
        body, grid=(N_CHUNKS,), in_specs=[*_gate_in_specs(), pl.BlockSpec(memory_space=pl.ANY)],
        out_specs=pl.BlockSpec((CHUNK, D_INNER), lambda c: (c, 0)), out_shape=SDS(mixcat.shape, mixcat.dtype),
        input_output_aliases={6: 0}, compiler_params=_cparams(("parallel",)), name=name,
    )(proj, proj, ln_g, ln_b, ws, bs_col, mixcat)


def _gate_bwd(proj, ln_g, ln_b, ws, bs_col, dcat, dproj, *, name):
    ng, gw = A_GROUPS, A_GROUP_W

    def body(u_ref, v_ref, g_ref, b_ref, ws_ref, bs_ref, d_ref, dproj_in, dproj_ref, dg_ref, db_ref, dws_ref, dbs_ref):
        del dproj_in
        args = _gate_args(u_ref, v_ref, g_ref, b_ref, ws_ref, bs_ref)
        _, vjp = jax.vjp(_gate_tile, *args)
        dpu, dpv, dg, db, dws, dbs = vjp(_split(d_ref, ng, gw))
        for gi in range(ng):
            dproj_ref[:, gi * gw:(gi + 1) * gw] = dpu[gi].astype(dproj_ref.dtype)
            dproj_ref[:, D_INNER + gi * gw:D_INNER + (gi + 1) * gw] = dpv[gi].astype(dproj_ref.dtype)

        @pl.when(pl.program_id(0) == 0)
        def _():
            for r in (dg_ref, db_ref, dws_ref, dbs_ref):
                r[...] = jnp.zeros_like(r)

        for gi in range(ng):
            dg_ref[:, gi * gw:(gi + 1) * gw] += dg[gi]
            db_ref[:, gi * gw:(gi + 1) * gw] += db[gi]
            dws_ref[gi] += dws[gi]
            dbs_ref[gi] += dbs[gi]

    in_specs = _gate_in_specs()
    return pl.pallas_call(
        body, grid=(N_CHUNKS,),
        in_specs=[*in_specs, pl.BlockSpec((CHUNK, D_INNER), lambda c: (c, 0)), pl.BlockSpec(memory_space=pl.ANY)],
        out_specs=[pl.BlockSpec((CHUNK, 2 * D_INNER), lambda c: (c, 0)), *in_specs[2:]],
        out_shape=[SDS(dproj.shape, dproj.dtype), SDS((1, D_INNER), F32), SDS((1, D_INNER), F32),
                   SDS((ng, CHUNK, CHUNK), F32), SDS((ng, CHUNK, 1), F32)],
        input_output_aliases={7: 0}, compiler_params=_cparams(("arbitrary",)), name=name,
    )(proj, proj, ln_g, ln_b, ws, bs_col, dcat, dproj)


ATT_TQ = 2048


def _attn_tile(q, k, v):
    s = _dot_nt(q, k) * (1.0 / math.sqrt(X_HEAD_DIM))
    s = s - jnp.max(s, axis=-1, keepdims=True)
    e = jnp.exp(s)
    p = e / jnp.sum(e, axis=-1, keepdims=True)
    return _dot(p, v)


def _attn_in_specs(q_blk, order):
    hd = X_HEAD_DIM
    return [
        pl.BlockSpec((ATT_TQ, hd), lambda a, b: (order(a, b)[0], q_blk + order(a, b)[1])),
        pl.BlockSpec((N_MEM, hd), lambda a, b: (0, order(a, b)[1])),
        pl.BlockSpec((N_MEM, hd), lambda a, b: (0, X_HEADS + order(a, b)[1])),
    ]


def _attn_fwd(proj, q_off, kv, *, name):
    order = lambda i, h: (i, h)
    cat_blk = D_INNER // X_HEAD_DIM

    def body(q_ref, k_ref, v_ref, o_ref):
        o_ref[...] = _attn_tile(q_ref[...], k_ref[...], v_ref[...]).astype(o_ref.dtype)

    return pl.pallas_call(
        body, grid=(SEQ // ATT_TQ, X_HEADS), in_specs=_attn_in_specs(q_off // X_HEAD_DIM, order),
        out_specs=pl.BlockSpec((ATT_TQ, X_HEAD_DIM), lambda i, h: (i, cat_blk + h)),
        out_shape=SDS((SEQ, MIX_OUT), BF16), compiler_params=_cparams(("parallel", "parallel")), name=name,
    )(proj, kv, kv)


def _attn_bwd(proj, q_off, kv, dcat, dproj_width, dq_off, *, name):
    order = lambda h, i: (i, h)
    cat_blk = D_INNER // X_HEAD_DIM
    dq_blk = dq_off // X_HEAD_DIM

    def body(q_ref, k_ref, v_ref, do_ref, dq_ref, dk_ref, dv_ref):
        _, vjp = jax.vjp(_attn_tile, q_ref[...], k_ref[...], v_ref[...])
        dq, dk, dv = vjp(do_ref[...])
        dq_ref[...] = dq.astype(dq_ref.dtype)

        @pl.when(pl.program_id(1) == 0)
        def _():
            dk_ref[...] = jnp.zeros_like(dk_ref)
            dv_ref[...] = jnp.zeros_like(dv_ref)

        dk_ref[...] += dk
        dv_ref[...] += dv

    kv_spec = pl.BlockSpec((N_MEM, X_HEAD_DIM), lambda h, i: (0, h))
    return pl.pallas_call(
        body, grid=(X_HEADS, SEQ // ATT_TQ),
        in_specs=[*_attn_in_specs(q_off // X_HEAD_DIM, order),
                  pl.BlockSpec((ATT_TQ, X_HEAD_DIM), lambda h, i: (i, cat_blk + h))],
        out_specs=[pl.BlockSpec((ATT_TQ, X_HEAD_DIM), lambda h, i: (i, dq_blk + h)), kv_spec, kv_spec],
        out_shape=[SDS((SEQ, dproj_width), BF16), SDS((N_MEM, X_WIDTH), F32), SDS((N_MEM, X_WIDTH), F32)],
        compiler_params=_cparams(("parallel", "arbitrary")), name=name,
    )(proj, kv, kv, dcat)


CONV_TC = 512
CONV_ROWS = 128
CONV_HALO = 8


def _shift_down(x, s):
    if s == 0:
        return x
    row = lax.broadcasted_iota(jnp.int32, x.shape, 0)
    return jnp.where(row >= s, pltpu.roll(x, s, 0), 0.0)


def _shift_up(x, s):
    if s == 0:
        return x
    n = x.shape[0]
    row = lax.broadcasted_iota(jnp.int32, x.shape, 0)
    return jnp.where(row < n - s, pltpu.roll(x, n - s, 0), 0.0)


def _conv_pre(x, w_ref, b_ref):
    pre = b_ref[...] + jnp.zeros_like(x)
    for k in range(CONV_K):
        pre = pre + w_ref[k:k + 1, :] * _shift_down(x, CONV_K - 1 - k)
    return pre


def _conv_fwd(proj, w, b, *, name):
    blk0 = D_INNER // CONV_TC

    def body(x_ref, w_ref, b_ref, o_ref):
        pre = _conv_pre(x_ref[...], w_ref, b_ref)
        o_ref[...] = pre * jax.nn.sigmoid(pre)

    return pl.pallas_call(
        body, grid=(CONV_DIM // CONV_TC,),
        in_specs=[pl.BlockSpec((SEQ, CONV_TC), lambda j: (0, blk0 + j)), pl.BlockSpec((CONV_K, CONV_TC), lambda j: (0, j)),
                  pl.BlockSpec((1, CONV_TC), lambda j: (0, j))],
        out_specs=pl.BlockSpec((SEQ, CONV_TC), lambda j: (0, j)), out_shape=SDS((SEQ, CONV_DIM), F32),
        compiler_params=_cparams(("parallel",)), name=name)(proj, w, b)


def _conv_bwd(proj, w, b, dxs, dbm, dcm, dproj, *, name):
    tc = CONV_TC // 2
    blk0 = D_INNER // tc
    n_x = D_INNER // tc
    n_b = SSM_GROUPS * SSM_STATE // tc

    window = CONV_ROWS + 2 * CONV_HALO
    n_rows = SEQ // CONV_ROWS

    def body(x_ref, w_ref, b_ref, dxs_ref, dbm_ref, dcm_ref, dproj_in, dproj_ref, dw_ref, db_ref):
        del dproj_in
        j = pl.program_id(0)

        def rows_step(lanes, first, keep, sums):
            start = first - keep
            if not isinstance(start, int):
                start = pl.multiple_of(start, CONV_HALO)
            rows = pl.ds(start, window)
            down = _shift_down if keep == 0 else lambda v, s: pltpu.roll(v, s, 0) if s else v
            up = _shift_up if keep == 2 * CONV_HALO else lambda v, s: pltpu.roll(v, window - s, 0) if s else v
            x = x_ref[rows, lanes]
            w = w_ref[:, lanes]
            pre = b_ref[:, lanes] + jnp.zeros_like(x)
            for k in range(CONV_K):
                pre = pre + w[k:k + 1, :] * down(x, CONV_K - 1 - k)
            sg = jax.nn.sigmoid(pre)
            dact = jnp.where(j < n_x, dxs_ref[rows, lanes],
                             jnp.where(j < n_x + n_b, dbm_ref[rows, lanes], dcm_ref[rows, lanes]))
            dpre = dact * (sg * (1.0 + pre * (1.0 - sg)))
            kept = lambda v: jnp.sum(v[keep:keep + CONV_ROWS, :], axis=0, keepdims=True)
            dx = jnp.zeros_like(x)
            new_sums = []
            for k in range(CONV_K):
                s = CONV_K - 1 - k
                dx = dx + w[k:k + 1, :] * up(dpre, s)
                new_sums.append(sums[k] + kept(dpre * down(x, s)))
            new_sums.append(sums[CONV_K] + kept(dpre))
            dproj_ref[pl.ds(first, CONV_ROWS), lanes] = dx[keep:keep + CONV_ROWS, :].astype(dproj_ref.dtype)
            return tuple(new_sums)

        for c in range(tc // 128):
            lanes = pl.ds(c * 128, 128)
            sums = rows_step(lanes, 0, 0, (jnp.zeros((1, 128), F32),) * (CONV_K + 1))
            sums = lax.fori_loop(
                1, n_rows - 1,
                lambda i, sums: rows_step(lanes, pl.multiple_of(i * CONV_ROWS, CONV_ROWS), CONV_HALO, sums), sums)
            sums = rows_step(lanes, SEQ - CONV_ROWS, 2 * CONV_HALO, sums)
            for k in range(CONV_K):
                dw_ref[k:k + 1, lanes] = sums[k]
            db_ref[:, lanes] = sums[CONV_K]

    clip = lambda v, hi: jnp.minimum(jnp.maximum(v, 0), hi)
    return pl.pallas_call(
        body, grid=(CONV_DIM // tc,),
        in_specs=[pl.BlockSpec((SEQ, tc), lambda j: (0, blk0 + j)), pl.BlockSpec((CONV_K, tc), lambda j: (0, j)),
                  pl.BlockSpec((1, tc), lambda j: (0, j)),
                  pl.BlockSpec((SEQ, tc), lambda j: (0, clip(j, n_x - 1))),
                  pl.BlockSpec((SEQ, tc), lambda j: (0, clip(j - n_x, n_b - 1))),
                  pl.BlockSpec((SEQ, tc), lambda j: (0, clip(j - n_x - n_b, n_b - 1))),
                  pl.BlockSpec(memory_space=pl.ANY)],
        out_specs=[pl.BlockSpec((SEQ, tc), lambda j: (0, blk0 + j)), pl.BlockSpec((CONV_K, tc), lambda j: (0, j)),
                   pl.BlockSpec((1, tc), lambda j: (0, j))],
        out_shape=[SDS(dproj.shape, dproj.dtype), SDS((CONV_K, CONV_DIM), F32), SDS((1, CONV_DIM), F32)],
        input_output_aliases={6: 0}, compiler_params=_cparams(("parallel",)), name=name,
    )(proj, w, b, dxs, dbm, dcm, dproj)


SSM_PAIRS = SSM_HPG // 2


def _dot_exact01(x, m01, m01_t, x_first, differentiable):
    def product(v, m):
        hi = v.astype(BF16)
        rest = v - hi.astype(F32)
        mid = rest.astype(BF16)
        lo = (rest - mid.astype(F32)).astype(BF16)
        dims = (((1,), (0,)), ((), ()))
        dot = lambda part: lax.dot_general(*((part, m) if x_first else (m, part)), dims, preferred_element_type=F32)
        return dot(hi) + dot(mid) + dot(lo)

    if not differentiable:
        return product(x, m01)

    @jax.custom_vjp
    def exact(v):
        return product(v, m01)

    exact.defvjp(lambda v: (product(v, m01), None), lambda _, ct: (product(ct, m01_t),))
    return exact(x)


def _ssd_tile(xp, zp, bm, cm, hp, dt_c, dt_r, bias, bias_col, alog, alog_col, dsk, gnp, differentiable=False):
    row = lax.broadcasted_iota(jnp.int32, (CHUNK, CHUNK), 0)
    col = lax.broadcasted_iota(jnp.int32, (CHUNK, CHUNK), 1)
    causal = row >= col
    left = col < SSM_HEAD_DIM
    top = row < SSM_HEAD_DIM
    ones = jnp.ones((CHUNK, CHUNK), BF16)
    cb = _dot_nt(cm, bm)
    dtp = jax.nn.softplus(dt_c + bias)
    da_c = dtp * -jnp.exp(alog)
    da_r = jax.nn.softplus(dt_r + bias_col) * -jnp.exp(alog_col)
    lower = jnp.where(causal, 1.0, 0.0).astype(BF16)
    upper = jnp.where(row <= col, 1.0, 0.0).astype(BF16)
    cs = _dot_exact01(da_c, lower, upper, False, differentiable)
    cs_rows = _dot_exact01(da_r, upper, lower, True, differentiable)
    cs_last = jnp.sum(da_c, axis=0, keepdims=True)
    ecs, decay, ecl = jnp.exp(cs), jnp.exp(cs_last - cs), jnp.exp(cs_last)
    m = [cb * jnp.exp(jnp.where(causal, cs[:, r:r + 1] - cs_rows[r:r + 1, :], -1e30)) for r in range(SSM_HPG)]
    ygs, hn = [], []
    for p in range(SSM_PAIRS):
        a, b = 2 * p, 2 * p + 1
        pair = lambda v: jnp.where(left, v[:, a:a + 1], v[:, b:b + 1])
        xdt = xp[p] * pair(dtp)
        y = jnp.where(left, _dot(m[a], xdt), _dot(m[b], xdt))
        y = y + _dot_nt(cm, hp[p]) * pair(ecs)
        y = y + xp[p] * pair(dsk)
        states = _dot_tn(xdt * pair(decay), bm)
        hn.append(hp[p] * jnp.where(top, ecl[:, a:a + 1], ecl[:, b:b + 1]) + states)
        ygs.append(y * (zp[p] * jax.nn.sigmoid(zp[p])))
    ms = sum(_dot(t * t, ones) for t in ygs) * (1.0 / SSM_GROUP_W)
    rs = lax.rsqrt(ms + EPS)
    return [ygs[p] * rs * gnp[p] for p in range(SSM_PAIRS)], hn


def _ssd_in_specs(cidx):
    gw, n = SSM_GROUP_W, SSM_STATE
    bm_blk = D_INNER // n
    return [
        pl.BlockSpec((CHUNK, gw), lambda g, c: (cidx(c), g)),
        pl.BlockSpec((CHUNK, gw), lambda g, c: (cidx(c), g)),
        pl.BlockSpec((CHUNK, n), lambda g, c: (cidx(c), bm_blk + g)),
        pl.BlockSpec((CHUNK, n), lambda g, c: (cidx(c), bm_blk + SSM_GROUPS + g)),
        pl.BlockSpec((None, CHUNK, SSM_HPG), lambda g, c: (g, cidx(c), 0)),
        pl.BlockSpec((None, SSM_HPG, CHUNK), lambda g, c: (g, 0, cidx(c))),
        pl.BlockSpec((None, 3, SSM_HPG), lambda g, c: (g, 0, 0)),
        pl.BlockSpec((None, SSM_HPG, 2), lambda g, c: (g, 0, 0)),
        pl.BlockSpec((1, gw), lambda g, c: (0, g)),
    ]


def _ssd_args(x_ref, z_ref, bm_ref, cm_ref, hp, dtc_ref, dtr_ref, prow_ref, pcol_ref, gn_ref):
    npair, w = SSM_PAIRS, 2 * SSM_HEAD_DIM
    return (_split(x_ref, npair, w), _split(z_ref, npair, w), bm_ref[...], cm_ref[...], hp, dtc_ref[...], dtr_ref[...],
            prow_ref[0:1, :], pcol_ref[:, 0:1], prow_ref[1:2, :], pcol_ref[:, 1:2], prow_ref[2:3, :],
            _split(gn_ref, npair, w))


def _pair_rows(ref):
    w = 2 * SSM_HEAD_DIM
    return [ref[p * w:(p + 1) * w, :] for p in range(SSM_PAIRS)]


def _ssd_fwd(xbc, proj, dt_c, dt_r, par_row, par_col, gn, mixcat, *, name):
    w = 2 * SSM_HEAD_DIM

    def body(x_ref, z_ref, bm_ref, cm_ref, dtc_ref, dtr_ref, prow_ref, pcol_ref, gn_ref, cat_in,
             cat_ref, hprev_ref, h_scr):
        del cat_in

        @pl.when(pl.program_id(1) == 0)
        def _():
            h_scr[...] = jnp.zeros_like(h_scr)

        hprev_ref[...] = h_scr[...]
        yn, hn = _ssd_tile(*_ssd_args(x_ref, z_ref, bm_ref, cm_ref, _pair_rows(h_scr), dtc_ref, dtr_ref, prow_ref,
                                      pcol_ref, gn_ref))
        for p in range(SSM_PAIRS):
            cat_ref[:, p * w:(p + 1) * w] = yn[p].astype(cat_ref.dtype)
            h_scr[p * w:(p + 1) * w, :] = hn[p]

    return pl.pallas_call(
        body, grid=(SSM_GROUPS, N_CHUNKS), in_specs=[*_ssd_in_specs(lambda c: c), pl.BlockSpec(memory_space=pl.ANY)],
        out_specs=[pl.BlockSpec((CHUNK, SSM_GROUP_W), lambda g, c: (c, g)),
                   pl.BlockSpec((None, None, SSM_GROUP_W, SSM_STATE), lambda g, c: (c, g, 0, 0))],
        out_shape=[SDS(mixcat.shape, mixcat.dtype), SDS((N_CHUNKS, SSM_GROUPS, SSM_GROUP_W, SSM_STATE), F32)],
        scratch_shapes=[pltpu.VMEM((SSM_GROUP_W, SSM_STATE), F32)],
        input_output_aliases={9: 0}, compiler_params=_cparams(("parallel", "arbitrary")), name=name,
    )(xbc, proj, xbc, xbc, dt_c, dt_r, par_row, par_col, gn, mixcat)


def _ssd_bwd(xbc, proj, dt_c, dt_r, par_row, par_col, gn, hprev, dcat, dproj, *, name):
    nh, w, gw, n = SSM_HPG, 2 * SSM_HEAD_DIM, SSM_GROUP_W, SSM_STATE
    rev = lambda c: N_CHUNKS - 1 - c

    def body(x_ref, z_ref, bm_ref, cm_ref, dtc_ref, dtr_ref, prow_ref, pcol_ref, gn_ref, hprev_ref, dy_ref,
             dproj_in, dz_ref, dxs_ref, dbm_ref, dcm_ref, ddtc_ref, ddtr_ref, dprow_ref, dpcol_ref, dgn_ref, dh_scr):
        del dproj_in
        first = pl.program_id(1) == 0

        @pl.when(first)
        def _():
            dh_scr[...] = jnp.zeros_like(dh_scr)
            for ref in (dprow_ref, dpcol_ref, dgn_ref):
                ref[...] = jnp.zeros_like(ref)

        args = _ssd_args(x_ref, z_ref, bm_ref, cm_ref, _pair_rows(hprev_ref), dtc_ref, dtr_ref, prow_ref, pcol_ref,
                         gn_ref)
        _, vjp = jax.vjp(lambda *a: _ssd_tile(*a, differentiable=True), *args)
        dxs, dzs, dbm, dcm, dhs, ddtc, ddtr, dbias, dbias_col, dalog, dalog_col, ddsk, dgn = vjp(
            (_split(dy_ref, SSM_PAIRS, w), _pair_rows(dh_scr)))
        dbm_ref[...] = dbm
        dcm_ref[...] = dcm
        ddtc_ref[...] = ddtc
        ddtr_ref[...] = ddtr
        for q in range(SSM_PAIRS):
            dxs_ref[:, q * w:(q + 1) * w] = dxs[q]
            dz_ref[:, q * w:(q + 1) * w] = dzs[q].astype(dz_ref.dtype)
            dh_scr[q * w:(q + 1) * w, :] = dhs[q]
            dgn_ref[:, q * w:(q + 1) * w] += dgn[q]
        for i, d in enumerate((dbias, dalog, ddsk)):
            dprow_ref[i:i + 1, :] += d
        for i, d in enumerate((dbias_col, dalog_col)):
            dpcol_ref[:, i:i + 1] += d

    return pl.pallas_call(
        body, grid=(SSM_GROUPS, N_CHUNKS),
        in_specs=[*_ssd_in_specs(rev),
                  pl.BlockSpec((None, None, gw, n), lambda g, c: (rev(c), g, 0, 0)),
                  pl.BlockSpec((CHUNK, gw), lambda g, c: (rev(c), g)),
                  pl.BlockSpec(memory_space=pl.ANY)],
        out_specs=[pl.BlockSpec((CHUNK, gw), lambda g, c: (rev(c), g)),
                   pl.BlockSpec((CHUNK, gw), lambda g, c: (rev(c), g)),
                   pl.BlockSpec((CHUNK, n), lambda g, c: (rev(c), g)),
                   pl.BlockSpec((CHUNK, n), lambda g, c: (rev(c), g)),
                   pl.BlockSpec((None, CHUNK, nh), lambda g, c: (g, rev(c), 0)),
                   pl.BlockSpec((None, nh, CHUNK), lambda g, c: (g, 0, rev(c))),
                   pl.BlockSpec((None, 3, nh), lambda g, c: (g, 0, 0)),
                   pl.BlockSpec((None, nh, 2), lambda g, c: (g, 0, 0)),
                   pl.BlockSpec((1, gw), lambda g, c: (0, g))],
        out_shape=[SDS(dproj.shape, dproj.dtype), SDS((SEQ, D_INNER), F32), SDS((SEQ, SSM_GROUPS * n), F32),
                   SDS((SEQ, SSM_GROUPS * n), F32), SDS((SSM_GROUPS, SEQ, nh), F32), SDS((SSM_GROUPS, nh, SEQ), F32),
                   SDS((SSM_GROUPS, 3, nh), F32), SDS((SSM_GROUPS, nh, 2), F32), SDS((1, D_INNER), F32)],
        scratch_shapes=[pltpu.VMEM((gw, n), F32)],
        input_output_aliases={11: 0}, compiler_params=_cparams(("parallel", "arbitrary")), name=name,
    )(xbc, proj, xbc, xbc, dt_c, dt_r, par_row, par_col, gn, hprev, dcat, dproj)


def _sum_contributions(chip, parts, landed, *, name):
    _, r, c = parts.shape
    tr = _pick(r, (256, 384, 128))

    def body(chip_ref, own_ref, landed_ref, o_ref):
        del chip_ref
        acc = own_ref[...].astype(F32)
        for s in range(landed_ref.shape[0]):
            acc = acc + landed_ref[s].astype(F32)
        o_ref[...] = acc

    grid_spec = pltpu.PrefetchScalarGridSpec(
        num_scalar_prefetch=1, grid=(r // tr,),
        in_specs=[pl.BlockSpec((None, tr, c), lambda i, chip_ref: (chip_ref[0], i, 0)),
                  pl.BlockSpec((landed.shape[0], tr, c), lambda i, chip_ref: (0, i, 0))],
        out_specs=pl.BlockSpec((tr, c), lambda i, chip_ref: (i, 0)))
    return pl.pallas_call(body, grid_spec=grid_spec, out_shape=SDS((r, c), F32),
                          compiler_params=_cparams(("parallel",)), name=name)(chip, parts, landed)


def _adamw(w, g, m, v, *, name):
    layers, r, c = w.shape
    if r <= 256 or r % 128 == 0:
        tr = min(r, 256)
        steps, spec = r // tr, pl.BlockSpec((None, tr, c), lambda l, i: (l, i, 0))
    else:
        tc = _pick(c, (256, 128))
        steps, spec = c // tc, pl.BlockSpec((None, r, tc), lambda l, i: (l, 0, i))

    def body(w_ref, g_ref, m_ref, v_ref, d_ref, mo_ref, vo_ref):
        g = g_ref[...]
        m_new = ADAM_B1 * m_ref[...] + (1.0 - ADAM_B1) * g
        v_new = ADAM_B2 * v_ref[...] + (1.0 - ADAM_B2) * (g * g)
        m_hat = m_new / (1.0 - ADAM_B1 ** ADAM_STEP)
        v_hat = v_new / (1.0 - ADAM_B2 ** ADAM_STEP)
        d_ref[...] = -ADAM_LR * (m_hat / (jnp.sqrt(v_hat) + ADAM_EPS) + ADAM_WD * w_ref[...])
        mo_ref[...] = m_new
        vo_ref[...] = v_new

    return pl.pallas_call(body, grid=(layers, steps), in_specs=[spec] * 4, out_specs=[spec] * 3,
                          out_shape=[SDS(w.shape, F32)] * 3, compiler_params=_cparams(("parallel", "parallel")),
                          name=name)(w, g, m, v)


ANY = pl.BlockSpec(memory_space=pl.ANY)


def _place():
    x, y, c = lax.axis_index("x"), lax.axis_index("y"), lax.axis_index("c")
    chips = [(1 - x, y), (x, 1 - y), (1 - x, 1 - y)]
    return x, y, c, chips


def _remote(src, dst, send_sem, recv_sem, to):
    return pltpu.make_async_remote_copy(src_ref=src, dst_ref=dst, send_sem=send_sem, recv_sem=recv_sem,
                                        device_id=to, device_id_type=MESH)


STREAM_ROWS = 256


def _stream_rows(i):
    return pl.ds(pl.multiple_of(i * STREAM_ROWS, STREAM_ROWS), STREAM_ROWS)


def _channel_scratch(width, dtype, rows=STREAM_ROWS):
    buf = (2, rows, width)
    return [pltpu.VMEM(buf, dtype), pltpu.VMEM(buf, dtype), *([pltpu.SemaphoreType.DMA((2,))] * 5),
            pltpu.SemaphoreType.REGULAR((2,))]


CHANNEL_REFS = 8


def _copy_blocks(srcs, dsts, ch):
    sbuf, _, ld, _, _, st, _, _ = ch
    n = len(srcs)
    load = lambda i: pltpu.make_async_copy(srcs[i], sbuf.at[i % 2], ld.at[i % 2])
    store = lambda i: pltpu.make_async_copy(sbuf.at[i % 2], dsts[i], st.at[i % 2])
    load(0).start()
    for i in range(n):
        if i + 1 < n:
            if i >= 1:
                store(i - 1).wait()
            load(i + 1).start()
        load(i).wait()
        store(i).start()
    for i in range(max(0, n - 2), n):
        store(i).wait()


def _exchange_block_streams(streams, sibling):
    plans = []
    for srcs, dsts, keeps, (sbuf, rbuf, ld, snd, rcv, st, kp, credit) in streams:
        n = len(srcs)

        def load(i, srcs=srcs, sbuf=sbuf, ld=ld):
            return pltpu.make_async_copy(srcs[i], sbuf.at[i % 2], ld.at[i % 2])

        def push(i, sbuf=sbuf, rbuf=rbuf, snd=snd, rcv=rcv):
            return _remote(sbuf.at[i % 2], rbuf.at[i % 2], snd.at[i % 2], rcv.at[i % 2], sibling)

        def store(i, rbuf=rbuf, dsts=dsts, st=st):
            return pltpu.make_async_copy(rbuf.at[i % 2], dsts[i], st.at[i % 2])

        def save(i, sbuf=sbuf, keeps=keeps, kp=kp):
            return pltpu.make_async_copy(sbuf.at[i % 2], keeps[i], kp.at[i % 2])

        def free_slot(i, n=n, store=store, credit=credit):
            if 1 <= i < n:
                store(i - 1).wait()
                if i + 1 < n:
                    pl.semaphore_signal(credit.at[(i + 1) % 2], 1, device_id=sibling, device_id_type=MESH)

        def send(i, n=n, load=load, push=push, save=save, keeps=keeps, credit=credit):
            if i < n:
                load(i).wait()
                pl.semaphore_wait(credit.at[i % 2], 1)
                push(i).start()
                if keeps[i] is not None:
                    save(i).start()

        def receive(i, n=n, load=load, push=push, store=store, save=save, keeps=keeps):
            if i < n:
                push(i).wait_recv()
                store(i).start()
                push(i).wait_send()
                if keeps[i] is not None:
                    save(i).wait()
                if i + 2 < n:
                    load(i + 2).start()

        for i in range(min(2, n)):
            pl.semaphore_signal(credit.at[i], 1, device_id=sibling, device_id_type=MESH)
            load(i).start()
        plans.append((n, free_slot, send, receive, store))
    for _, _, send, _, _ in plans:
        send(0)
    for i in range(max(p[0] for p in plans)):
        for _, free_slot, _, _, _ in plans:
            free_slot(i)
        for _, _, send, _, _ in plans:
            send(i + 1)
        for _, _, _, receive, _ in plans:
            receive(i)
    for n, _, _, _, store in plans:
        store(n - 1).wait()


def _all_gather_shards(shards, small, *, name):
    n = len(shards)

    def body(*refs):
        ins, outs = refs[:n + 1], refs[n + 1:2 * n + 2]
        scr = refs[2 * n + 2:]
        chans = [scr[CHANNEL_REFS * t:CHANNEL_REFS * (t + 1)] for t in range(n)]
        send_sems, recv_sems, small_sems = scr[CHANNEL_REFS * n:]
        x, y, c, _ = _place()
        me = 2 * x + y
        sibling = (x, y, 1 - c)
        near = (lax.rem(x + 1 - c, 2), lax.rem(y + c, 2))
        far = (lax.rem(x + c, 2), lax.rem(y + 1 - c, 2))
        k_near, k_far, k_diag = 2 * near[0] + near[1], 2 * far[0] + far[1], 3 - me
        targets = ((*near, c), (*far, c), (*far, c))
        arrives = (k_near, k_far, k_diag)
        streams_in = (k_far, k_near, k_diag)

        def ici(t, j, src, blk):
            return _remote(src, outs[t].at[blk, c], send_sems.at[3 * t + j], recv_sems.at[3 * t + j], targets[j])

        first = [ici(t, j, ins[t].at[c], me) for t in range(n + 1) for j in range(2)]
        for cp in first:
            cp.start()
        small_local = pltpu.make_async_copy(ins[n], outs[n].at[me], small_sems.at[6])
        small_local.start()
        for t in range(n):
            _copy_blocks([ins[t].at[h] for h in range(2)], [outs[t].at[me, h] for h in range(2)], chans[t])
        passed = []
        for j in range(3):
            for t in range(n + 1):
                landed = outs[t].at[arrives[j], c]
                ici(t, j, landed, arrives[j]).wait_recv()
                if j == 0:
                    fwd = ici(t, 2, landed, k_near)
                    fwd.start()
                    passed.append(fwd)
                if t < n:
                    _exchange_block_streams([([landed], [outs[t].at[streams_in[j], 1 - c]], [None], chans[t])], sibling)
                else:
                    fwd = _remote(landed, landed, small_sems.at[j], small_sems.at[3 + j], sibling)
                    fwd.start()
                    passed.append(fwd)
        for j in range(3):
            got = outs[n].at[streams_in[j], 1 - c]
            _remote(got, got, small_sems.at[j], small_sems.at[3 + j], sibling).wait_recv()
        for cp in first + passed:
            cp.wait_send()
        small_local.wait()

    scratch = []
    for s in shards:
        scratch += _channel_scratch(s.shape[2], s.dtype, rows=s.shape[1])
    return pl.pallas_call(
        body, in_specs=[ANY] * (n + 1), out_specs=[ANY] * (n + 1),
        out_shape=[SDS((N_CHIPS, *s.shape), s.dtype) for s in (*shards, small)],
        scratch_shapes=[*scratch, pltpu.SemaphoreType.DMA((3 * n + 3,)), pltpu.SemaphoreType.DMA((3 * n + 3,)),
                        pltpu.SemaphoreType.DMA((7,))],
        compiler_params=pltpu.CompilerParams(vmem_limit_bytes=VMEM_LIMIT), name=name)(*shards, small)


def _pair_reduce(stacks, *, name):
    n = len(stacks)
    per = 11

    def body(*refs):
        ins, outs, scr = refs[:n], refs[n:2 * n], refs[2 * n:]
        x, y, c, _ = _place()
        sibling = (x, y, 1 - c)
        streams = []
        for t in range(n):
            sraw, sbuf, rbuf, obuf, pbuf, ld_s, ld_o, snd, rcv, st, credit = scr[per * t:per * (t + 1)]
            steps = ins[t].shape[1] // STREAM_ROWS
            src, own, out = ins[t].at[1 - c], ins[t].at[c], outs[t]
            assert steps >= 2

            def load_s(i, slot, src=src, sraw=sraw, ld_s=ld_s):
                return pltpu.make_async_copy(src.at[_stream_rows(i)], sraw.at[slot], ld_s.at[slot])

            def load_o(i, slot, own=own, obuf=obuf, ld_o=ld_o):
                return pltpu.make_async_copy(own.at[_stream_rows(i)], obuf.at[slot], ld_o.at[slot])

            def push(slot, sbuf=sbuf, rbuf=rbuf, snd=snd, rcv=rcv):
                return _remote(sbuf.at[slot], rbuf.at[slot], snd.at[slot], rcv.at[slot], sibling)

            def store(i, slot, pbuf=pbuf, out=out, st=st):
                return pltpu.make_async_copy(pbuf.at[slot], out.at[_stream_rows(i)], st.at[slot])

            def send(i, slot, load_s=load_s, push=push, sraw=sraw, sbuf=sbuf, credit=credit):
                load_s(i, slot).wait()
                sbuf[slot] = sraw[slot].astype(sbuf.dtype)
                pl.semaphore_wait(credit.at[slot], 1)
                push(slot).start()

            def combine(i, slot, load_s=load_s, load_o=load_o, push=push, store=store, rbuf=rbuf, obuf=obuf, pbuf=pbuf,
                        credit=credit, steps=steps):
                load_o(i, slot).wait()
                push(slot).wait_recv()

                @pl.when(i >= 2)
                def _():
                    store(i, slot).wait()

                pbuf[slot] = (obuf[slot] + rbuf[slot].astype(F32)).astype(pbuf.dtype)
                store(i, slot).start()
                push(slot).wait_send()

                @pl.when(i + 2 < steps)
                def _():
                    load_s(i + 2, slot).start()
                    load_o(i + 2, slot).start()
                    pl.semaphore_signal(credit.at[slot], 1, device_id=sibling, device_id_type=MESH)

            for slot in range(2):
                pl.semaphore_signal(credit.at[slot], 1, device_id=sibling, device_id_type=MESH)
                load_s(slot, slot).start()
                load_o(slot, slot).start()
            streams.append((steps, send, combine, store))
        for _, send, _, _ in streams:
            send(0, 0)

        def step(i, carry):
            slot = lax.rem(i, 2)
            for steps, send, _, _ in streams:
                @pl.when(i + 1 < steps)
                def _(send=send):
                    send(i + 1, 1 - slot)
            for steps, _, combine, _ in streams:
                @pl.when(i < steps)
                def _(combine=combine):
                    combine(i, slot)
            return carry

        lax.fori_loop(0, max(s[0] for s in streams), step, 0)
        for _, _, _, store in streams:
            for slot in range(2):
                store(0, slot).wait()

    scratch = []
    for s in stacks:
        buf = (2, STREAM_ROWS, s.shape[2])
        scratch += [pltpu.VMEM(buf, F32), pltpu.VMEM(buf, BF16), pltpu.VMEM(buf, BF16), pltpu.VMEM(buf, F32),
                    pltpu.VMEM(buf, BF16), *([pltpu.SemaphoreType.DMA((2,))] * 5), pltpu.SemaphoreType.REGULAR((2,))]
    return pl.pallas_call(
        body, in_specs=[ANY] * n, out_specs=[ANY] * n, out_shape=[SDS(s.shape[1:], BF16) for s in stacks],
        scratch_shapes=scratch, compiler_params=pltpu.CompilerParams(vmem_limit_bytes=VMEM_LIMIT), name=name)(*stacks)


HBM_SPEC = pl.BlockSpec(memory_space=pltpu.HBM)
SEM_SPEC = pl.BlockSpec(memory_space=pltpu.SEMAPHORE)
SIDE_EFFECT = pltpu.SideEffectType.DATAFLOW_SIDE_EFFECTING


def _scatter_copies(ins, lands, send_sems, recv_sems):
    _, _, c, chips = _place()
    return [_remote(ins[t].at[2 * cx + cy], lands[t].at[j], send_sems.at[3 * t + j], recv_sems.at[3 * t + j],
                    (cx, cy, c)) for t in range(len(ins)) for j, (cx, cy) in enumerate(chips)]


def _chip_scatter_start(parts, *, name):
    n = len(parts)

    def body(*refs):
        ins, lands = refs[:n], refs[n:2 * n]
        send_sems, recv_sems, token = refs[2 * n], refs[2 * n + 1], refs[-1]
        for cp in _scatter_copies(ins, lands, send_sems, recv_sems):
            cp.start()
        token[...] = jnp.zeros_like(token)

    hbm = lambda a: pltpu.with_memory_space_constraint(a, pltpu.HBM)
    lands = [hbm(lax.empty((3, *p.shape[1:]), p.dtype)) for p in parts]
    thru = [pltpu.HBM(a.shape, a.dtype) for a in (*parts, *lands)]
    outs = pl.pallas_call(
        body, name=name,
        out_shape=(pltpu.SemaphoreType.DMA((3 * n,)), pltpu.SemaphoreType.DMA((3 * n,)), *thru, SDS((8, 128), F32)),
        in_specs=[HBM_SPEC] * (2 * n),
        out_specs=(SEM_SPEC, SEM_SPEC, *([HBM_SPEC] * (2 * n)), pl.BlockSpec(memory_space=pltpu.VMEM)),
        input_output_aliases={i: 2 + i for i in range(2 * n)},
        compiler_params=pltpu.CompilerParams(has_side_effects=SIDE_EFFECT),
    )(*[hbm(p) for p in parts], *lands)
    return outs[0], outs[1], outs[2:2 + n], outs[2 + n:2 + 2 * n], outs[-1]


def _chip_scatter_wait(send_sems, recv_sems, parts, lands, after, *, name):
    n = len(parts)

    def body(*refs):
        ins, lands_in = refs[:n], refs[n:2 * n]
        for cp in _scatter_copies(ins, lands_in, refs[2 * n], refs[2 * n + 1]):
            cp.wait_send()
            cp.wait_recv()

    outs = pl.pallas_call(
        body, name=name, out_shape=[pltpu.HBM(a.shape, a.dtype) for a in (*parts, *lands)],
        in_specs=[*([HBM_SPEC] * (2 * n)), SEM_SPEC, SEM_SPEC, *([ANY] * len(after))],
        out_specs=[HBM_SPEC] * (2 * n), input_output_aliases={i: i for i in range(2 * n)},
        compiler_params=pltpu.CompilerParams(has_side_effects=SIDE_EFFECT),
    )(*parts, *lands, send_sems, recv_sems, *after)
    return outs[:n], outs[n:]


def _gather_copies(shards, zones, send_sems, recv_sems):
    x, y, c, chips = _place()
    return [_remote(shards[t].at[c], zones[t].at[2 * x + y, c], send_sems.at[3 * t + j], recv_sems.at[3 * t + j],
                    (cx, cy, c)) for t in range(len(shards)) for j, (cx, cy) in enumerate(chips)]


def _gather_start(shards, after, *, name):
    n = len(shards)

    def body(*refs):
        ins, zones = refs[:n], refs[n:2 * n]
        send_sems, recv_sems, token = refs[2 * n + len(after)], refs[2 * n + len(after) + 1], refs[-1]
        for cp in _gather_copies(ins, zones, send_sems, recv_sems):
            cp.start()
        token[...] = jnp.zeros_like(token)

    hbm = lambda a: pltpu.with_memory_space_constraint(a, pltpu.HBM)
    zones = [hbm(lax.empty((N_CHIPS, *s.shape), s.dtype)) for s in shards]
    thru = [pltpu.HBM(a.shape, a.dtype) for a in (*shards, *zones)]
    outs = pl.pallas_call(
        body, name=name,
        out_shape=(pltpu.SemaphoreType.DMA((3 * n,)), pltpu.SemaphoreType.DMA((3 * n,)), *thru, SDS((8, 128), F32)),
        in_specs=[*([HBM_SPEC] * (2 * n)), *([ANY] * len(after))],
        out_specs=(SEM_SPEC, SEM_SPEC, *([HBM_SPEC] * (2 * n)), pl.BlockSpec(memory_space=pltpu.VMEM)),
        input_output_aliases={i: 2 + i for i in range(2 * n)},
        compiler_params=pltpu.CompilerParams(has_side_effects=SIDE_EFFECT),
    )(*[hbm(s) for s in shards], *zones, *after)
    return outs[0], outs[1], outs[2:2 + n], outs[2 + n:2 + 2 * n], outs[-1]


def _gather_wait(send_sems, recv_sems, shards, zones, after, *, name):
    n = len(shards)

    def body(*refs):
        for cp in _gather_copies(refs[:n], refs[n:2 * n], refs[2 * n], refs[2 * n + 1]):
            cp.wait_send()
            cp.wait_recv()

    outs = pl.pallas_call(
        body, name=name, out_shape=[pltpu.HBM(a.shape, a.dtype) for a in (*shards, *zones)],
        in_specs=[*([HBM_SPEC] * (2 * n)), SEM_SPEC, SEM_SPEC, *([ANY] * len(after))],
        out_specs=[HBM_SPEC] * (2 * n), input_output_aliases={i: i for i in range(2 * n)},
        compiler_params=pltpu.CompilerParams(has_side_effects=SIDE_EFFECT),
    )(*shards, *zones, send_sems, recv_sems, *after)
    return outs[:n], outs[n:]


def _gather_finish(shards, zones, *, name):
    n = len(shards)

    def body(*refs):
        ins, zones_in, outs, scr = refs[:n], refs[n:2 * n], refs[2 * n:3 * n], refs[3 * n:]
        x, y, c, chips = _place()
        me = 2 * x + y
        sibling = (x, y, 1 - c)
        others = [2 * cx + cy for cx, cy in chips]
        chans = [scr[CHANNEL_REFS * t:CHANNEL_REFS * (t + 1)] for t in range(n)]
        for t in range(n):
            _copy_blocks([ins[t].at[h] for h in range(2)], [outs[t].at[me, h] for h in range(2)], chans[t])
        _exchange_block_streams([([zones_in[t].at[k, c] for k in others], [outs[t].at[k, 1 - c] for k in others],
                                  [None] * len(others), chans[t]) for t in range(n)], sibling)

    scratch = []
    for s in shards:
        scratch += _channel_scratch(s.shape[2], s.dtype, rows=s.shape[1])
    return pl.pallas_call(
        body, in_specs=[ANY] * (2 * n), out_specs=[ANY] * n, out_shape=[SDS(z.shape, z.dtype) for z in zones],
        input_output_aliases={n + t: t for t in range(n)}, scratch_shapes=scratch,
        compiler_params=pltpu.CompilerParams(vmem_limit_bytes=VMEM_LIMIT), name=name)(*shards, *zones)


def _pair_share(groups, *, name):
    finals = [f for grp in groups for f in grp]
    n, n_out = len(finals), len(groups)

    def body(*refs):
        ins, outs, scr = refs[:n], refs[n:n + n_out], refs[n + n_out:]
        x, y, c, _ = _place()
        sibling = (x, y, 1 - c)
        t, streams = 0, []
        for o, grp in enumerate(groups):
            rows = grp[0].shape[0] // 2
            blocks = [(layer, pl.ds(b * rows, rows)) for layer in range(len(grp)) for b in range(2)]
            streams.append(([ins[t + layer].at[rs] for layer, rs in blocks],
                            [outs[o].at[layer, 1 - c, rs] for layer, rs in blocks],
                            [outs[o].at[layer, c, rs] for layer, rs in blocks],
                            scr[CHANNEL_REFS * o:CHANNEL_REFS * (o + 1)]))
            t += len(grp)
        _exchange_block_streams(streams, sibling)

    scratch = []
    for grp in groups:
        scratch += _channel_scratch(grp[0].shape[1], grp[0].dtype, rows=grp[0].shape[0] // 2)
    return pl.pallas_call(
        body, in_specs=[ANY] * n, out_specs=[ANY] * n_out,
        out_shape=[SDS((len(grp), 2, *grp[0].shape), grp[0].dtype) for grp in groups],
        scratch_shapes=scratch, compiler_params=pltpu.CompilerParams(vmem_limit_bytes=VMEM_LIMIT), name=name)(*finals)


def _all_reduce_small(v, *, name):
    rows, lanes = v.shape
    n_dev = 8

    def body(v_ref, o_ref, all_ref, send_sems, recv_sems, local_sem):
        x, y, c, chips = _place()
        me, sibling = (x, y, c), (x, y, 1 - c)

        def block(px, py, pc):
            return all_ref.at[4 * px + 2 * py + pc]

        def copy(k, blk, to, src=None):
            return _remote(block(*blk) if src is None else src, block(*blk), send_sems.at[k], recv_sems.at[k], to)

        mine = pltpu.make_async_copy(v_ref, block(*me), local_sem)
        mine.start()
        first = [copy(0, me, sibling, src=v_ref)]
        first += [copy(1 + j, me, (*chip, c), src=v_ref) for j, chip in enumerate(chips)]
        for cp in first:
            cp.start()
        passed = [copy(4 + j, (*chip, c), sibling) for j, chip in enumerate(chips)]
        for j, chip in enumerate(chips):
            copy(1 + j, (*chip, c), me).wait_recv()
            passed[j].start()
        copy(0, sibling, me).wait_recv()
        for j, chip in enumerate(chips):
            copy(4 + j, (*chip, 1 - c), me).wait_recv()
        for cp in first + passed:
            cp.wait_send()
        mine.wait()
        acc = all_ref[0]
        for k in range(1, n_dev):
            acc = acc + all_ref[k]
        o_ref[...] = acc

    vmem = pl.BlockSpec(memory_space=pltpu.VMEM)
    return pl.pallas_call(
        body, in_specs=[vmem], out_specs=vmem, out_shape=SDS((rows, lanes), F32),
        scratch_shapes=[pltpu.VMEM((n_dev, rows, lanes), F32), pltpu.SemaphoreType.DMA((7,)),
                        pltpu.SemaphoreType.DMA((7,)), pltpu.SemaphoreType.DMA],
        compiler_params=pltpu.CompilerParams(vmem_limit_bytes=VMEM_LIMIT), name=name)(v)


def _relu2_epilogue(acc):
    return acc, jnp.square(jnp.maximum(acc, 0.0))


def _res_epilogue(acc, res):
    return (acc + res,)


def _drelu2_epilogue(acc, pre):
    return (acc * (2.0 * jnp.maximum(pre.astype(F32), 0.0)),)


def _ffn_fwd(h, g, w1, w2, tag):
    f = _rms_fwd(h, g, name=f"ffn_norm_{tag}")
    pre, act = _mm_nn(f, w1, name=f"ffn1_{tag}", epilogue=_relu2_epilogue, n_out_dtypes=(BF16, BF16))
    h_out = _mm_nn(act, w2, name=f"ffn2_{tag}", extras=(h,), epilogue=_res_epilogue)
    return h_out, (f, pre, act)


def _ffn_bwd(dh, h, g, w1, w2, saved, layer, after=()):
    f, pre, act = saved
    dpre = _mm_nt(dh, w2, name=f"ffn2_dx_{layer}", out_dtype=BF16, extras=(pre,), epilogue=_drelu2_epilogue,
                  after=after)
    dw2 = _mm_tn_stacked(act, dh, name=f"ffn2_dw_{layer}", col_slots=False)
    df = _mm_nt(dpre, w1, name=f"ffn1_dx_{layer}")
    dw1 = _mm_tn_stacked(f, dpre, name=f"ffn1_dw_{layer}", col_slots=True)
    dh, dg = _rms_bwd(h, g, df, dh, name=f"ffn_norm_bwd_{layer}")
    return dh, dg, dw1, dw2


def _kv_fwd(mem, g, w_kv, tag):
    m = _rms_fwd(mem, g, name=f"mem_norm_{tag}")
    return m, _mm_nn(m, w_kv, name=f"kv_{tag}")


def _kv_bwd(mem, g, w_kv, m, dk, dv, layer):
    dkv = jnp.concatenate([dk, dv], axis=1)
    dw = _mm_tn_stacked(m, dkv, name=f"kv_dw_{layer}", col_slots=True)
    dm = _mm_nt(dkv, w_kv, name=f"kv_dx_{layer}")
    _, dg = _rms_bwd(mem, g, dm, dm, name=f"mem_norm_bwd_{layer}")
    return dw, dg


def _local_step(x, mem, target, p, after_layer1=None, after_ffn0=None, after_mixer0=None):
    row = lambda v: v.reshape(1, -1)
    g = {}

    h0 = x
    a0 = _rms_fwd(h0, row(p["norm_mix"][0]), name="mix_norm_0")
    proj_a = _mm_nn(a0, p["a_in"], name="a_in", after=p.get("after_start", ()))
    m0, kv0 = _kv_fwd(mem, row(p["mem_norm"][0]), p["w_kv"][0], "0")
    cat0 = _attn_fwd(proj_a, 2 * D_INNER, kv0, name="attn_0")
    bs_col = p["a_bs"].reshape(A_GROUPS, CHUNK, 1)
    cat0 = _gate_fwd(proj_a, p["a_ln_g"], p["a_ln_b"], p["a_ws"], bs_col, cat0, name="gate")
    h1 = _mm_nn(cat0, p["w_out"][0], name="out_0", extras=(h0,), epilogue=_res_epilogue)
    w_ffn1_0, w_ffn2_0 = p["layer0_ffn"](h1) if "layer0_ffn" in p else (p["w_ffn1"][0], p["w_ffn2"][0])
    h2, ffn0 = _ffn_fwd(h1, row(p["norm_ffn"][0]), w_ffn1_0, w_ffn2_0, "0")

    w_kv1, b_in = p["layer1_mixer"](h2) if "layer1_mixer" in p else (p["w_kv"][1], p["b_in"])
    a1 = _rms_fwd(h2, row(p["norm_mix"][1]), name="mix_norm_1")
    proj_b = _mm_nn(a1, b_in, name="b_in")
    m1, kv1 = _kv_fwd(mem, row(p["mem_norm"][1]), w_kv1, "1")
    cat1 = _attn_fwd(proj_b, B_Q_OFF, kv1, name="attn_1")
    xbc = _conv_fwd(proj_b, p["b_conv_w"], p["b_conv_b"], name="conv")
    dt_raw = proj_b[:, B_DT_OFF:B_DT_OFF + SSM_HEADS].reshape(SEQ, SSM_GROUPS, SSM_HPG)
    dt_c = jnp.transpose(dt_raw, (1, 0, 2))
    dt_r = jnp.transpose(dt_raw, (1, 2, 0))
    per_head = lambda v: v.reshape(SSM_GROUPS, 1, SSM_HPG)
    par_row = jnp.concatenate([per_head(p["b_dt_bias"]), per_head(p["b_a_log"]), per_head(p["b_d"])], axis=1)
    ssd_par = (par_row, jnp.transpose(par_row[:, :2], (0, 2, 1)), p["b_gnorm"])
    cat1, hprev = _ssd_fwd(xbc, proj_b, dt_c, dt_r, *ssd_par, cat1, name="ssd")
    if "layer1_rest" in p:
        w_out1, w_ffn1_1, w_ffn2_1 = p["layer1_rest"](cat1)
    else:
        w_out1, w_ffn1_1, w_ffn2_1 = p["w_out"][1], p["w_ffn1"][1], p["w_ffn2"][1]
    h3 = _mm_nn(cat1, w_out1, name="out_1", extras=(h2,), epilogue=_res_epilogue)
    h4, ffn1 = _ffn_fwd(h3, row(p["norm_ffn"][1]), w_ffn1_1, w_ffn2_1, "1")

    loss, dh, g["final_norm"] = _loss_head(h4, row(p["final_norm"]), target, name="loss_head")

    dh, dnf1, dw1_1, dw2_1 = _ffn_bwd(dh, h3, row(p["norm_ffn"][1]), w_ffn1_1, w_ffn2_1, ffn1, 1)
    dcat1 = _mm_nt(dh, w_out1, name="out_dx_1")
    dwo_1 = _mm_tn_stacked(cat1, dh, name="out_dw_1", col_slots=False)
    dproj_b, dk1, dv1 = _attn_bwd(proj_b, B_Q_OFF, kv1, dcat1, B_IN_PAD, B_Q_OFF, name="attn_bwd_1")
    dproj_b, dxs, dbm, dcm, ddt_c, ddt_r, dpar_row, dpar_col, g["b_gnorm"] = _ssd_bwd(
        xbc, proj_b, dt_c, dt_r, *ssd_par, hprev, dcat1, dproj_b, name="ssd_bwd")
    dpar = dpar_row.at[:, :2].add(jnp.transpose(dpar_col, (0, 2, 1)))
    g["b_dt_bias"], g["b_a_log"], g["b_d"] = dpar[:, 0], dpar[:, 1], dpar[:, 2]
    dproj_b, g["b_conv_w"], g["b_conv_b"] = _conv_bwd(proj_b, p["b_conv_w"], p["b_conv_b"], dxs, dbm, dcm, dproj_b,
                                                      name="conv_bwd")
    ddt = jnp.transpose(ddt_c, (1, 0, 2)) + jnp.transpose(ddt_r, (2, 0, 1))
    ddt = jnp.pad(ddt.reshape(SEQ, SSM_HEADS), ((0, 0), (0, B_IN_PAD - B_DT_OFF - SSM_HEADS))).astype(BF16)
    dproj_b = lax.dynamic_update_slice(dproj_b, ddt, (0, B_DT_OFF))
    dwkv_1, dmn1 = _kv_bwd(mem, row(p["mem_norm"][1]), w_kv1, m1, dk1, dv1, 1)
    dwb = _b_in_grad_slots(_mm_tn(a1, dproj_b, name="b_in_dw"))
    da1 = _mm_nt(dproj_b, b_in, name="b_in_dx")
    dh, dnm1 = _rms_bwd(h2, row(p["norm_mix"][1]), da1, dh, name="mix_norm_bwd_1")
    layer1 = dict(w_kv=dwkv_1, w_out=dwo_1, w_ffn1=dw1_1, w_ffn2=dw2_1, b_in=dwb)
    token = () if after_layer1 is None else (after_layer1(layer1),)

    dh, dnf0, dw1_0, dw2_0 = _ffn_bwd(dh, h1, row(p["norm_ffn"][0]), w_ffn1_0, w_ffn2_0, ffn0, 0,
                                      after=token)
    ffn0_grads = dict(w_ffn1=dw1_0, w_ffn2=dw2_0)
    token = () if after_ffn0 is None else (after_ffn0(ffn0_grads),)
    dcat0 = _mm_nt(dh, p["w_out"][0], name="out_dx_0", after=token)
    dwo_0 = _mm_tn_stacked(cat0, dh, name="out_dw_0", col_slots=False)
    dproj_a, dk0, dv0 = _attn_bwd(proj_a, 2 * D_INNER, kv0, dcat0, A_IN, 2 * D_INNER, name="attn_bwd_0")
    dproj_a, g["a_ln_g"], g["a_ln_b"], g["a_ws"], dbs_col = _gate_bwd(
        proj_a, p["a_ln_g"], p["a_ln_b"], p["a_ws"], bs_col, dcat0, dproj_a, name="gate_bwd")
    g["a_bs"] = dbs_col.reshape(A_GROUPS, CHUNK)
    dwkv_0, dmn0 = _kv_bwd(mem, row(p["mem_norm"][0]), p["w_kv"][0], m0, dk0, dv0, 0)
    dwa = _mm_tn_stacked(a0, dproj_a, name="a_in_dw", col_slots=True)
    mixer0_grads = dict(w_kv=dwkv_0, w_out=dwo_0, a_in=dwa)
    token = () if after_mixer0 is None else (after_mixer0(mixer0_grads),)
    da0 = _mm_nt(dproj_a, p["a_in"], name="a_in_dx", after=token)
    dx, dnm0 = _rms_bwd(h0, row(p["norm_mix"][0]), da0, dh, name="mix_norm_bwd_0")

    g["norm_mix"] = jnp.concatenate([dnm0, dnm1], axis=0)
    g["norm_ffn"] = jnp.concatenate([dnf0, dnf1], axis=0)
    g["mem_norm"] = jnp.concatenate([dmn0, dmn1], axis=0)
    layer0 = dict(w_kv=dwkv_0, w_out=dwo_0, w_ffn1=dw1_0, w_ffn2=dw2_0, a_in=dwa)
    return loss, dx, g, layer0, layer1


def _b_in_full(gathered):
    n = B_IN // N_CHIPS
    dt0 = D_INNER + CONV_DIM - (N_CHIPS - 1) * n
    last = gathered[N_CHIPS - 1]
    return jnp.concatenate([*[gathered[k] for k in range(N_CHIPS - 1)], last[:, :dt0], last[:, dt0 + SSM_HEADS:],
                            last[:, dt0:dt0 + SSM_HEADS], jnp.zeros((D_MODEL, B_IN_PAD - B_IN), last.dtype)], axis=1)


def _b_in_grad_slots(d):
    n = B_IN // N_CHIPS
    dt0 = D_INNER + CONV_DIM
    last = jnp.concatenate([d[:, (N_CHIPS - 1) * n:dt0], d[:, B_DT_OFF:B_DT_OFF + SSM_HEADS], d[:, dt0:B_DT_OFF]], axis=1)
    slots = [*[d[:, k * n:(k + 1) * n] for k in range(N_CHIPS - 1)], last]
    half = D_MODEL // 2
    return jnp.stack([jnp.stack([s[h * half:(h + 1) * half] for s in slots]) for h in range(2)])


SMALL_REPL = ("norm_mix", "norm_ffn", "mem_norm", "a_ln_g", "a_ln_b", "a_ws", "a_bs", "b_dt_bias", "b_a_log", "b_d",
              "final_norm")
SMALL_SHARD = ("b_conv_w", "b_conv_b", "b_gnorm")
WEIGHTS = ("norm_mix", "norm_ffn", "mem_norm", "w_kv", "w_out", "w_ffn1", "w_ffn2", "a_in", "a_ln_g", "a_ln_b", "a_ws",
           "a_bs", "b_in", "b_conv_w", "b_conv_b", "b_dt_bias", "b_a_log", "b_d", "b_gnorm", "final_norm")
CONV_SHARD = CONV_DIM // N_CHIPS
GN_SHARD = D_INNER // N_CHIPS


LAYERED = ("w_kv", "w_out", "w_ffn1", "w_ffn2")


def _gather_weights(w):
    halves = lambda k, layer: (w[k][layer] if k in LAYERED else w[k][0]).reshape(2, -1, w[k].shape[-1]).astype(BF16)
    small = jnp.zeros((2, CONV_K, CONV_SHARD), F32)
    small = small.at[0].set(w["b_conv_w"][0])
    small = small.at[1, 0].set(w["b_conv_b"][0])
    small = small.at[1, 1, :GN_SHARD].set(w["b_gnorm"][0])
    first_names = ("w_kv", "w_out", "a_in")
    gathered = _all_gather_shards([halves(k, 0) for k in first_names], small, name="gather_weights_0")
    got = dict(zip(first_names, gathered))
    slots = lambda a: a.reshape(N_CHIPS, -1, a.shape[-1])
    rows = lambda a: a.reshape(-1, a.shape[-1])
    p = dict(w_kv=[slots(got["w_kv"])], w_out=[rows(got["w_out"])], a_in=slots(got["a_in"]))
    sm = gathered[-1]
    p["b_conv_w"] = jnp.transpose(sm[:, 0], (1, 0, 2)).reshape(CONV_K, CONV_DIM)
    p["b_conv_b"] = sm[:, 1, 0].reshape(1, CONV_DIM)
    p["b_gnorm"] = sm[:, 1, 1, :GN_SHARD].reshape(1, D_INNER)

    after, started = (gathered[0],), {}
    for tag, layer, names in (("0_ffn", 0, ("w_ffn1", "w_ffn2")), ("1_mixer", 1, ("w_kv", "b_in")),
                              ("1_rest", 1, ("w_out", "w_ffn1", "w_ffn2"))):
        started[tag] = _gather_start([halves(k, layer) for k in names], after, name=f"gather_start_{tag}")
        after = (started[tag][-1],)
    p["after_start"] = after

    def finish(tag, first):
        send_sems, recv_sems, shards, zones, _ = started[tag]
        shards, zones = _gather_wait(send_sems, recv_sems, shards, zones, (first,), name=f"gather_wait_{tag}")
        return _gather_finish(shards, zones, name=f"gather_finish_{tag}")

    def layer0_ffn(first):
        w1, w2 = finish("0_ffn", first)
        return slots(w1), rows(w2)

    def layer1_mixer(first):
        kv, b_in = finish("1_mixer", first)
        return slots(kv), _b_in_full(slots(b_in))

    def layer1_rest(first):
        wo, w1, w2 = finish("1_rest", first)
        return rows(wo), slots(w1), rows(w2)

    p.update(layer0_ffn=layer0_ffn, layer1_mixer=layer1_mixer, layer1_rest=layer1_rest)
    return p


def _pair_parts(grads, tag):
    stacks = [g.reshape(2, -1, g.shape[-1]) for g in grads.values()]
    parts = _pair_reduce(stacks, name=f"grads_pair_reduce_{tag}")
    return [t.reshape(N_CHIPS, -1, t.shape[-1]) for t in parts]


def _chip_sums(chip, names, parts, landed, tag):
    return {k: _sum_contributions(chip, t, u, name=f"grads_chip_sum_{k}_{tag}")
            for k, t, u in zip(names, parts, landed)}


def _small_layout(shapes):
    offs, o = {}, 0
    for k in (*SMALL_REPL, *SMALL_SHARD):
        size = math.prod(shapes[k])
        offs[k] = (o, size)
        o += size
    rows = -(-(o + 1) // (8 * 128)) * 8
    return offs, rows


def _reduce_small(g, loss_part, full_shapes):
    offs, rows = _small_layout(full_shapes)
    flat = jnp.concatenate([*[g[k].reshape(-1) for k in (*SMALL_REPL, *SMALL_SHARD)], loss_part[0, :1]])
    flat = jnp.pad(flat, (0, rows * 128 - flat.shape[0])).reshape(rows, 128)
    total = _all_reduce_small(flat, name="small_all_reduce").reshape(-1)
    end = max(o + n for o, n in offs.values())
    return {k: total[o:o + n].reshape(full_shapes[k]) for k, (o, n) in offs.items()}, total[end]


def kernel(x, mem, norm_mix, norm_ffn, mem_norm, w_kv, w_out, w_ffn1, w_ffn2, a_in, a_ln_g, a_ln_b, a_ws, a_bs, b_in, b_conv_w, b_conv_b, b_dt_bias, b_a_log, b_d, b_gnorm, final_norm, loss_target, m_norm_mix, m_norm_ffn, m_mem_norm, m_w_kv, m_w_out, m_w_ffn1, m_w_ffn2, m_a_in, m_a_ln_g, m_a_ln_b, m_a_ws, m_a_bs, m_b_in, m_b_conv_w, m_b_conv_b, m_b_dt_bias, m_b_a_log, m_b_d, m_b_gnorm, m_final_norm, v_norm_mix, v_norm_ffn, v_mem_norm, v_w_kv, v_w_out, v_w_ffn1, v_w_ffn2, v_a_in, v_a_ln_g, v_a_ln_b, v_a_ws, v_a_bs, v_b_in, v_b_conv_w, v_b_conv_b, v_b_dt_bias, v_b_a_log, v_b_d, v_b_gnorm, v_final_norm):
    w = dict(norm_mix=norm_mix, norm_ffn=norm_ffn, mem_norm=mem_norm, w_kv=w_kv, w_out=w_out, w_ffn1=w_ffn1,
             w_ffn2=w_ffn2, a_in=a_in, a_ln_g=a_ln_g, a_ln_b=a_ln_b, a_ws=a_ws, a_bs=a_bs, b_in=b_in, b_conv_w=b_conv_w,
             b_conv_b=b_conv_b, b_dt_bias=b_dt_bias, b_a_log=b_a_log, b_d=b_d, b_gnorm=b_gnorm, final_norm=final_norm)
    mom = dict(norm_mix=m_norm_mix, norm_ffn=m_norm_ffn, mem_norm=m_mem_norm, w_kv=m_w_kv, w_out=m_w_out,
               w_ffn1=m_w_ffn1, w_ffn2=m_w_ffn2, a_in=m_a_in, a_ln_g=m_a_ln_g, a_ln_b=m_a_ln_b, a_ws=m_a_ws,
               a_bs=m_a_bs, b_in=m_b_in, b_conv_w=m_b_conv_w, b_conv_b=m_b_conv_b, b_dt_bias=m_b_dt_bias,
               b_a_log=m_b_a_log, b_d=m_b_d, b_gnorm=m_b_gnorm, final_norm=m_final_norm)
    var = dict(norm_mix=v_norm_mix, norm_ffn=v_norm_ffn, mem_norm=v_mem_norm, w_kv=v_w_kv, w_out=v_w_out,
               w_ffn1=v_w_ffn1, w_ffn2=v_w_ffn2, a_in=v_a_in, a_ln_g=v_a_ln_g, a_ln_b=v_a_ln_b, a_ws=v_a_ws,
               a_bs=v_a_bs, b_in=v_b_in, b_conv_w=v_b_conv_w, b_conv_b=v_b_conv_b, b_dt_bias=v_b_dt_bias,
               b_a_log=v_b_a_log, b_d=v_b_d, b_gnorm=v_b_gnorm, final_norm=v_final_norm)

    p = _gather_weights(w)
    p.update(norm_mix=norm_mix, norm_ffn=norm_ffn, mem_norm=mem_norm, a_ln_g=a_ln_g, a_ln_b=a_ln_b, a_ws=a_ws[0],
             a_bs=a_bs[0], b_dt_bias=b_dt_bias, b_a_log=b_a_log, b_d=b_d, final_norm=final_norm)
    chip = 2 * lax.axis_index("x") + lax.axis_index("y")
    chip_arr = jnp.reshape(chip, (1,)).astype(jnp.int32)
    started = {}

    def start_scatter(tag):
        def hook(grads):
            start = _chip_scatter_start(_pair_parts(grads, tag), name=f"grads_chip_scatter_start_{tag}")
            started[tag] = (tuple(grads), start)
            return start[-1]
        return hook

    loss_part, dx, g, _, _ = _local_step(x[0], mem[0], loss_target[0], p, start_scatter("1"), start_scatter("0f"),
                                         start_scatter("0m"))
    full_shapes = {k: w[k].shape for k in SMALL_REPL}
    full_shapes.update(b_conv_w=(1, CONV_K, CONV_DIM), b_conv_b=(1, CONV_DIM), b_gnorm=(1, D_INNER))
    grads, loss = _reduce_small(g, loss_part, full_shapes)
    grads["b_conv_w"] = lax.dynamic_slice_in_dim(grads["b_conv_w"], chip * CONV_SHARD, CONV_SHARD, axis=2)
    grads["b_conv_b"] = lax.dynamic_slice_in_dim(grads["b_conv_b"], chip * CONV_SHARD, CONV_SHARD, axis=1)
    grads["b_gnorm"] = lax.dynamic_slice_in_dim(grads["b_gnorm"], chip * GN_SHARD, GN_SHARD, axis=1)

    def finish_scatter(tag, *first):
        names, (send_sems, recv_sems, parts, lands, _) = started[tag]
        parts, landed = _chip_scatter_wait(send_sems, recv_sems, parts, lands, first,
                                           name=f"grads_chip_scatter_wait_{tag}")
        return _chip_sums(chip_arr, names, parts, landed, tag)

    def adamw(names, grads):
        for k in names:
            shape = w[k].shape
            if len(shape) == 3 and shape[2] % 128 and not shape[1] % 128:
                flat = unflat = lambda a: jnp.transpose(a, (0, 2, 1))
            else:
                flat = (lambda a: a) if len(shape) == 3 else (lambda a: a.reshape(1, -1, shape[-1]))
                unflat = lambda a: a.reshape(shape)
            d, m_new, v_new = _adamw(flat(w[k]), flat(grads[k]), flat(mom[k]), flat(var[k]), name=f"adamw_{k}")
            delta[k], new_m[k], new_v[k] = unflat(d), unflat(m_new), unflat(v_new)

    delta, new_m, new_v = {}, {}, {}
    halves = [finish_scatter("0f", dx), finish_scatter("1", dx)]
    early = ("w_ffn1", "w_ffn2", "b_in")
    shared = _pair_share([[halves[layer][k] for layer in range(2) if k in halves[layer]] for k in early],
                         name="grads_pair_share_early")
    grads.update({k: a.reshape(w[k].shape) for k, a in zip(early, shared)})
    adamw([k for k in WEIGHTS if k in grads], grads)
    halves[0].update(finish_scatter("0m", delta["w_ffn2"]))
    late = ("w_kv", "w_out", "a_in")
    shared = _pair_share([[halves[layer][k] for layer in range(2) if k in halves[layer]] for k in late],
                         name="grads_pair_share_late")
    grads.update({k: a.reshape(w[k].shape) for k, a in zip(late, shared)})
    adamw(late, grads)

    return (loss, dx.reshape(x.shape), *[grads[k] for k in WEIGHTS], *[delta[k] for k in WEIGHTS],
            *[new_m[k] for k in WEIGHTS], *[new_v[k] for k in WEIGHTS])
```
